```python
import jax
import jax.numpy as jnp
from jax import lax
import numpy as np

D_MODEL = 1024
BATCH = 8
SEQ = 2048
DEPTH = 2

CTX_LEN = 256
GRID_W = 64
N_MIXERS = 2
N_REC_LAYERS = (DEPTH + 1) // 2
N_CONV_LAYERS = DEPTH // 2
D_RNN = (4 * D_MODEL // 3) // 128 * 128
N_RNN_BLOCKS = 16
RNN_BLOCK = D_RNN // N_RNN_BLOCKS
REC_CONV_W = 4
REC_CONV_PAD = (1, 2)
RG_C = 8.0
CONF_KW = 31
CONF_PAD = (CONF_KW // 2, CONF_KW // 2)
D_FF = 4 * D_MODEL
N_MOD = 6
EPS = 1e-6
POS_BASE = 10000.0

kernel_name = 'hybrid_rglru_conformer_dit_block'


def rmsnorm(x, g):
    xf = x.astype(jnp.float32)
    y = xf * lax.rsqrt(jnp.mean(xf * xf, axis=-1, keepdims=True) + EPS)
    return (y * g.astype(jnp.float32)).astype(x.dtype)


def layernorm(x, g, b):
    xf = x.astype(jnp.float32)
    mu = jnp.mean(xf, axis=-1, keepdims=True)
    var = jnp.mean(jnp.square(xf - mu), axis=-1, keepdims=True)
    y = (xf - mu) * lax.rsqrt(var + EPS)
    return (y * g.astype(jnp.float32) + b.astype(jnp.float32)).astype(x.dtype)


def modulate(h, shift, scale):
    return h * (1 + scale) + shift


def grid_pos_embed(rows, d, dtype):
    t = jnp.arange(rows * GRID_W, dtype=jnp.int32)
    row = (t // GRID_W).astype(jnp.float32)
    col = (t % GRID_W).astype(jnp.float32)
    q = d // 4
    omega = 1.0 / (POS_BASE ** (jnp.arange(q, dtype=jnp.float32) / q))
    er = row[:, None] * omega[None, :]
    ec = col[:, None] * omega[None, :]
    return jnp.concatenate([jnp.sin(er), jnp.cos(er), jnp.sin(ec), jnp.cos(ec)], axis=-1).astype(dtype)


def dwconv(x, w, b, pad):
    y = lax.conv_general_dilated(x, w[:, None, :].astype(x.dtype), window_strides=(1,), padding=[pad],
                                 dimension_numbers=('NWC', 'WIO', 'NWC'), feature_group_count=x.shape[-1])
    return y + b.astype(x.dtype)


def sq_relu_mlp(h, w_in, w_out):
    return jnp.square(jax.nn.relu(h @ w_in)) @ w_out


def block_diag(u, w, b):
    ub = u.reshape(u.shape[:-1] + (N_RNN_BLOCKS, RNN_BLOCK))
    y = jnp.einsum('bthi,hij->bthj', ub, w.astype(jnp.float32)) + b.astype(jnp.float32)
    return y.reshape(u.shape)


def rglru_coeffs(u, lam, w_a, b_a, w_x, b_x):
    uf = u.astype(jnp.float32)
    r = jax.nn.sigmoid(block_diag(uf, w_a, b_a))
    i = jax.nn.sigmoid(block_diag(uf, w_x, b_x))
    log_a = -RG_C * r * jax.nn.softplus(-lam.astype(jnp.float32))
    a = jnp.exp(log_a)
    return a, jnp.sqrt(-jnp.expm1(2.0 * log_a)) * (i * uf)


def linear_scan(a, b, h0, reverse):
    def combine(l, r):
        al, bl = l
        ar, br = r
        return al * ar, ar * bl + br
    a_cum, b_cum = lax.associative_scan(combine, (a, b), axis=1, reverse=reverse)
    return a_cum * h0[:, None, :] + b_cum


def recurrent_block(h_lat, h_ctx, w_in, conv_w, conv_b, lam, w_a, b_a, w_x, b_x, w_out, ctx_out):
    w_gate, w_rec = w_in[:, :D_RNN], w_in[:, D_RNN:]
    u_lat = dwconv(h_lat @ w_rec, conv_w, conv_b, REC_CONV_PAD)
    u_ctx = dwconv(h_ctx @ w_rec, conv_w, conv_b, REC_CONV_PAD)
    zeros = jnp.zeros((h_lat.shape[0], D_RNN), jnp.float32)
    ys_lat, ys_ctx = [], []
    for d, rev in enumerate((False, True)):
        a_c, b_c = rglru_coeffs(u_ctx, lam[d], w_a[d], b_a[d], w_x[d], b_x[d])
        s_ctx = linear_scan(a_c, b_c, zeros, rev)
        h0 = s_ctx[:, 0] if rev else s_ctx[:, -1]
        a_l, b_l = rglru_coeffs(u_lat, lam[d], w_a[d], b_a[d], w_x[d], b_x[d])
        ys_lat.append(linear_scan(a_l, b_l, h0, rev))
        ys_ctx.append(s_ctx)
    y_lat = (ys_lat[0] + ys_lat[1]).astype(h_lat.dtype)
    out_lat = (jax.nn.gelu(h_lat @ w_gate) * y_lat) @ w_out
    if not ctx_out:
        return out_lat, None
    y_ctx = (ys_ctx[0] + ys_ctx[1]).astype(h_ctx.dtype)
    out_ctx = (jax.nn.gelu(h_ctx @ w_gate) * y_ctx) @ w_out
    return out_lat, out_ctx


def conformer_conv(h, w_pw1, b_pw1, conv_w, conv_b, ln_g, ln_b, w_pw2, b_pw2):
    z = jax.nn.glu(h @ w_pw1 + b_pw1, axis=-1)
    z = dwconv(z, conv_w, conv_b, CONF_PAD)
    z = jax.nn.silu(layernorm(z, ln_g, ln_b))
    return z @ w_pw2 + b_pw2


def _fwd_setup_inputs(seed: int = 0) -> dict:
    key = jax.random.key(seed)
    ks = jax.random.split(key, 32)
    f32 = jnp.float32

    def nrm(k, shape, scale):
        return jax.random.normal(k, shape, f32) * scale

    x = nrm(ks[0], (BATCH, SEQ, D_MODEL), 1.0)
    c = nrm(ks[1], (BATCH, D_MODEL), 1.0)
    ctx = nrm(ks[2], (BATCH, CTX_LEN, D_MODEL), 1.0)
    c_ctx = nrm(ks[3], (D_MODEL,), 1.0)
    w_ada = nrm(ks[4], (DEPTH, D_MODEL, N_MOD * D_MODEL), 0.5 * D_MODEL ** -0.5)
    b_ada = nrm(ks[5], (DEPTH, N_MOD * D_MODEL), 0.02)
    norm_g = 1.0 + nrm(ks[6], (DEPTH, 2, D_MODEL), 0.05)
    rec_w_in = nrm(ks[7], (N_REC_LAYERS, D_MODEL, 2 * D_RNN), D_MODEL ** -0.5)
    rec_conv_w = nrm(ks[8], (N_REC_LAYERS, REC_CONV_W, D_RNN), REC_CONV_W ** -0.5)
    rec_conv_b = nrm(ks[9], (N_REC_LAYERS, D_RNN), 0.02)
    u = jax.random.uniform(ks[10], (N_REC_LAYERS, 2, D_RNN), f32, 0.9, 0.999)
    a_base = u ** (1.0 / RG_C)
    rec_lambda = jnp.log(a_base) - jnp.log1p(-a_base)
    rec_w_a = nrm(ks[11], (N_REC_LAYERS, 2, N_RNN_BLOCKS, RNN_BLOCK, RNN_BLOCK), RNN_BLOCK ** -0.5)
    rec_b_a = nrm(ks[12], (N_REC_LAYERS, 2, N_RNN_BLOCKS, RNN_BLOCK), 0.02)
    rec_w_x = nrm(ks[13], (N_REC_LAYERS, 2, N_RNN_BLOCKS, RNN_BLOCK, RNN_BLOCK), RNN_BLOCK ** -0.5)
    rec_b_x = nrm(ks[14], (N_REC_LAYERS, 2, N_RNN_BLOCKS, RNN_BLOCK), 0.02)
    rec_w_out = nrm(ks[15], (N_REC_LAYERS, D_RNN, D_MODEL), D_RNN ** -0.5)
    conf_w_pw1 = nrm(ks[16], (N_CONV_LAYERS, D_MODEL, 2 * D_MODEL), D_MODEL ** -0.5)
    conf_b_pw1 = nrm(ks[17], (N_CONV_LAYERS, 2 * D_MODEL), 0.02)
    conf_conv_w = nrm(ks[18], (N_CONV_LAYERS, CONF_KW, D_MODEL), CONF_KW ** -0.5)
    conf_conv_b = nrm(ks[19], (N_CONV_LAYERS, D_MODEL), 0.02)
    conf_ln_g = 1.0 + nrm(ks[20], (N_CONV_LAYERS, D_MODEL), 0.05)
    conf_ln_b = nrm(ks[21], (N_CONV_LAYERS, D_MODEL), 0.02)
    conf_w_pw2 = nrm(ks[22], (N_CONV_LAYERS, D_MODEL, D_MODEL), D_MODEL ** -0.5)
    conf_b_pw2 = nrm(ks[23], (N_CONV_LAYERS, D_MODEL), 0.02)
    mlp_w_in = nrm(ks[24], (DEPTH, D_MODEL, D_FF), D_MODEL ** -0.5)
    mlp_w_out = nrm(ks[25], (DEPTH, D_FF, D_MODEL), D_FF ** -0.5)
    final_g = 1.0 + nrm(ks[26], (D_MODEL,), 0.05)
    return {'x': x, 'c': c, 'ctx': ctx, 'c_ctx': c_ctx, 'w_ada': w_ada, 'b_ada': b_ada, 'norm_g': norm_g,
            'rec_w_in': rec_w_in, 'rec_conv_w': rec_conv_w, 'rec_conv_b': rec_conv_b, 'rec_lambda': rec_lambda,
            'rec_w_a': rec_w_a, 'rec_b_a': rec_b_a, 'rec_w_x': rec_w_x, 'rec_b_x': rec_b_x, 'rec_w_out': rec_w_out,
            'conf_w_pw1': conf_w_pw1, 'conf_b_pw1': conf_b_pw1, 'conf_conv_w': conf_conv_w, 'conf_conv_b': conf_conv_b,
            'conf_ln_g': conf_ln_g, 'conf_ln_b': conf_ln_b, 'conf_w_pw2': conf_w_pw2, 'conf_b_pw2': conf_b_pw2,
            'mlp_w_in': mlp_w_in, 'mlp_w_out': mlp_w_out, 'final_g': final_g}


def _fwd_reference(x, c, ctx, c_ctx, w_ada, b_ada, norm_g, rec_w_in, rec_conv_w, rec_conv_b, rec_lambda,
              rec_w_a, rec_b_a, rec_w_x, rec_b_x, rec_w_out, conf_w_pw1, conf_b_pw1, conf_conv_w, conf_conv_b,
              conf_ln_g, conf_ln_b, conf_w_pw2, conf_b_pw2, mlp_w_in, mlp_w_out, final_g):
    rows = x.shape[1] // GRID_W
    x = x + grid_pos_embed(rows, x.shape[-1], x.dtype)[None]
    xc = ctx
    last_ctx_layer = ((DEPTH - 1) // N_MIXERS) * N_MIXERS
    s_c = jax.nn.silu(c)
    s_cc = jax.nn.silu(c_ctx)
    for i in range(DEPTH):
        sh1, sc1, g1, sh2, sc2, g2 = jnp.split((s_c @ w_ada[i] + b_ada[i])[:, None, :], N_MOD, axis=-1)
        use_ctx = i <= last_ctx_layer
        ctx_out = i < last_ctx_layer
        h = modulate(rmsnorm(x, norm_g[i, 0]), sh1, sc1)
        hc = None
        if use_ctx:
            csh1, csc1, cg1, csh2, csc2, cg2 = jnp.split((s_cc @ w_ada[i] + b_ada[i])[None, None, :], N_MOD, axis=-1)
            hc = modulate(rmsnorm(xc, norm_g[i, 0]), csh1, csc1)
        j = i // N_MIXERS
        if i % N_MIXERS == 0:
            y, yc = recurrent_block(h, hc, rec_w_in[j], rec_conv_w[j], rec_conv_b[j], rec_lambda[j],
                                    rec_w_a[j], rec_b_a[j], rec_w_x[j], rec_b_x[j], rec_w_out[j], ctx_out)
        else:
            conf_p = (conf_w_pw1[j], conf_b_pw1[j], conf_conv_w[j], conf_conv_b[j],
                      conf_ln_g[j], conf_ln_b[j], conf_w_pw2[j], conf_b_pw2[j])
            y = conformer_conv(h, *conf_p)
            yc = conformer_conv(hc, *conf_p) if ctx_out else None
        x = x + g1 * y
        x = x + g2 * sq_relu_mlp(modulate(rmsnorm(x, norm_g[i, 1]), sh2, sc2), mlp_w_in[i], mlp_w_out[i])
        if ctx_out:
            xc = xc + cg1 * yc
            xc = xc + cg2 * sq_relu_mlp(modulate(rmsnorm(xc, norm_g[i, 1]), csh2, csc2), mlp_w_in[i], mlp_w_out[i])
    return rmsnorm(x, final_g)


import jax as _jax
import jax.numpy as _jnp

TWIN_FORMAT = 'train_step'
FWD_PARAMS = ['x', 'c', 'ctx', 'c_ctx', 'w_ada', 'b_ada', 'norm_g', 'rec_w_in', 'rec_conv_w', 'rec_conv_b', 'rec_lambda', 'rec_w_a', 'rec_b_a', 'rec_w_x', 'rec_b_x', 'rec_w_out', 'conf_w_pw1', 'conf_b_pw1', 'conf_conv_w', 'conf_conv_b', 'conf_ln_g', 'conf_ln_b', 'conf_w_pw2', 'conf_b_pw2', 'mlp_w_in', 'mlp_w_out', 'final_g']
TWIN_WEIGHTS = ['c_ctx', 'w_ada', 'b_ada', 'norm_g', 'rec_w_in', 'rec_conv_w', 'rec_conv_b', 'rec_lambda', 'rec_w_a', 'rec_b_a', 'rec_w_x', 'rec_b_x', 'rec_w_out', 'conf_w_pw1', 'conf_b_pw1', 'conf_conv_w', 'conf_conv_b', 'conf_ln_g', 'conf_ln_b', 'conf_w_pw2', 'conf_b_pw2', 'mlp_w_in', 'mlp_w_out', 'final_g']
TWIN_DIFF_INPUT = 'x'
TWIN_INPUTS = ['x', 'c', 'ctx', 'c_ctx', 'w_ada', 'b_ada', 'norm_g', 'rec_w_in', 'rec_conv_w', 'rec_conv_b', 'rec_lambda', 'rec_w_a', 'rec_b_a', 'rec_w_x', 'rec_b_x', 'rec_w_out', 'conf_w_pw1', 'conf_b_pw1', 'conf_conv_w', 'conf_conv_b', 'conf_ln_g', 'conf_ln_b', 'conf_w_pw2', 'conf_b_pw2', 'mlp_w_in', 'mlp_w_out', 'final_g', 'loss_target', 'm_c_ctx', 'm_w_ada', 'm_b_ada', 'm_norm_g', 'm_rec_w_in', 'm_rec_conv_w', 'm_rec_conv_b', 'm_rec_lambda', 'm_rec_w_a', 'm_rec_b_a', 'm_rec_w_x', 'm_rec_b_x', 'm_rec_w_out', 'm_conf_w_pw1', 'm_conf_b_pw1', 'm_conf_conv_w', 'm_conf_conv_b', 'm_conf_ln_g', 'm_conf_ln_b', 'm_conf_w_pw2', 'm_conf_b_pw2', 'm_mlp_w_in', 'm_mlp_w_out', 'm_final_g', 'v_c_ctx', 'v_w_ada', 'v_b_ada', 'v_norm_g', 'v_rec_w_in', 'v_rec_conv_w', 'v_rec_conv_b', 'v_rec_lambda', 'v_rec_w_a', 'v_rec_b_a', 'v_rec_w_x', 'v_rec_b_x', 'v_rec_w_out', 'v_conf_w_pw1', 'v_conf_b_pw1', 'v_conf_conv_w', 'v_conf_conv_b', 'v_conf_ln_g', 'v_conf_ln_b', 'v_conf_w_pw2', 'v_conf_b_pw2', 'v_mlp_w_in', 'v_mlp_w_out', 'v_final_g']
TWIN_OUTPUTS = ['loss', 'grad_x', 'grad_c_ctx', 'grad_w_ada', 'grad_b_ada', 'grad_norm_g', 'grad_rec_w_in', 'grad_rec_conv_w', 'grad_rec_conv_b', 'grad_rec_lambda', 'grad_rec_w_a', 'grad_rec_b_a', 'grad_rec_w_x', 'grad_rec_b_x', 'grad_rec_w_out', 'grad_conf_w_pw1', 'grad_conf_b_pw1', 'grad_conf_conv_w', 'grad_conf_conv_b', 'grad_conf_ln_g', 'grad_conf_ln_b', 'grad_conf_w_pw2', 'grad_conf_b_pw2', 'grad_mlp_w_in', 'grad_mlp_w_out', 'grad_final_g', 'delta_c_ctx', 'delta_w_ada', 'delta_b_ada', 'delta_norm_g', 'delta_rec_w_in', 'delta_rec_conv_w', 'delta_rec_conv_b', 'delta_rec_lambda', 'delta_rec_w_a', 'delta_rec_b_a', 'delta_rec_w_x', 'delta_rec_b_x', 'delta_rec_w_out', 'delta_conf_w_pw1', 'delta_conf_b_pw1', 'delta_conf_conv_w', 'delta_conf_conv_b', 'delta_conf_ln_g', 'delta_conf_ln_b', 'delta_conf_w_pw2', 'delta_conf_b_pw2', 'delta_mlp_w_in', 'delta_mlp_w_out', 'delta_final_g', 'new_m_c_ctx', 'new_m_w_ada', 'new_m_b_ada', 'new_m_norm_g', 'new_m_rec_w_in', 'new_m_rec_conv_w', 'new_m_rec_conv_b', 'new_m_rec_lambda', 'new_m_rec_w_a', 'new_m_rec_b_a', 'new_m_rec_w_x', 'new_m_rec_b_x', 'new_m_rec_w_out', 'new_m_conf_w_pw1', 'new_m_conf_b_pw1', 'new_m_conf_conv_w', 'new_m_conf_conv_b', 'new_m_conf_ln_g', 'new_m_conf_ln_b', 'new_m_conf_w_pw2', 'new_m_conf_b_pw2', 'new_m_mlp_w_in', 'new_m_mlp_w_out', 'new_m_final_g', 'new_v_c_ctx', 'new_v_w_ada', 'new_v_b_ada', 'new_v_norm_g', 'new_v_rec_w_in', 'new_v_rec_conv_w', 'new_v_rec_conv_b', 'new_v_rec_lambda', 'new_v_rec_w_a', 'new_v_rec_b_a', 'new_v_rec_w_x', 'new_v_rec_b_x', 'new_v_rec_w_out', 'new_v_conf_w_pw1', 'new_v_conf_b_pw1', 'new_v_conf_conv_w', 'new_v_conf_conv_b', 'new_v_conf_ln_g', 'new_v_conf_ln_b', 'new_v_conf_w_pw2', 'new_v_conf_b_pw2', 'new_v_mlp_w_in', 'new_v_mlp_w_out', 'new_v_final_g']
TWIN_LEAF_KINDS = {'loss': 'loss', 'grad_x': 'grad_x', 'grad_c_ctx': 'grad_w', 'grad_w_ada': 'grad_w', 'grad_b_ada': 'grad_w', 'grad_norm_g': 'grad_w', 'grad_rec_w_in': 'grad_w', 'grad_rec_conv_w': 'grad_w', 'grad_rec_conv_b': 'grad_w', 'grad_rec_lambda': 'grad_w', 'grad_rec_w_a': 'grad_w', 'grad_rec_b_a': 'grad_w', 'grad_rec_w_x': 'grad_w', 'grad_rec_b_x': 'grad_w', 'grad_rec_w_out': 'grad_w', 'grad_conf_w_pw1': 'grad_w', 'grad_conf_b_pw1': 'grad_w', 'grad_conf_conv_w': 'grad_w', 'grad_conf_conv_b': 'grad_w', 'grad_conf_ln_g': 'grad_w', 'grad_conf_ln_b': 'grad_w', 'grad_conf_w_pw2': 'grad_w', 'grad_conf_b_pw2': 'grad_w', 'grad_mlp_w_in': 'grad_w', 'grad_mlp_w_out': 'grad_w', 'grad_final_g': 'grad_w', 'delta_c_ctx': 'delta_w', 'delta_w_ada': 'delta_w', 'delta_b_ada': 'delta_w', 'delta_norm_g': 'delta_w', 'delta_rec_w_in': 'delta_w', 'delta_rec_conv_w': 'delta_w', 'delta_rec_conv_b': 'delta_w', 'delta_rec_lambda': 'delta_w', 'delta_rec_w_a': 'delta_w', 'delta_rec_b_a': 'delta_w', 'delta_rec_w_x': 'delta_w', 'delta_rec_b_x': 'delta_w', 'delta_rec_w_out': 'delta_w', 'delta_conf_w_pw1': 'delta_w', 'delta_conf_b_pw1': 'delta_w', 'delta_conf_conv_w': 'delta_w', 'delta_conf_conv_b': 'delta_w', 'delta_conf_ln_g': 'delta_w', 'delta_conf_ln_b': 'delta_w', 'delta_conf_w_pw2': 'delta_w', 'delta_conf_b_pw2': 'delta_w', 'delta_mlp_w_in': 'delta_w', 'delta_mlp_w_out': 'delta_w', 'delta_final_g': 'delta_w', 'new_m_c_ctx': 'new_m', 'new_m_w_ada': 'new_m', 'new_m_b_ada': 'new_m', 'new_m_norm_g': 'new_m', 'new_m_rec_w_in': 'new_m', 'new_m_rec_conv_w': 'new_m', 'new_m_rec_conv_b': 'new_m', 'new_m_rec_lambda': 'new_m', 'new_m_rec_w_a': 'new_m', 'new_m_rec_b_a': 'new_m', 'new_m_rec_w_x': 'new_m', 'new_m_rec_b_x': 'new_m', 'new_m_rec_w_out': 'new_m', 'new_m_conf_w_pw1': 'new_m', 'new_m_conf_b_pw1': 'new_m', 'new_m_conf_conv_w': 'new_m', 'new_m_conf_conv_b': 'new_m', 'new_m_conf_ln_g': 'new_m', 'new_m_conf_ln_b': 'new_m', 'new_m_conf_w_pw2': 'new_m', 'new_m_conf_b_pw2': 'new_m', 'new_m_mlp_w_in': 'new_m', 'new_m_mlp_w_out': 'new_m', 'new_m_final_g': 'new_m', 'new_v_c_ctx': 'new_v', 'new_v_w_ada': 'new_v', 'new_v_b_ada': 'new_v', 'new_v_norm_g': 'new_v', 'new_v_rec_w_in': 'new_v', 'new_v_rec_conv_w': 'new_v', 'new_v_rec_conv_b': 'new_v', 'new_v_rec_lambda': 'new_v', 'new_v_rec_w_a': 'new_v', 'new_v_rec_b_a': 'new_v', 'new_v_rec_w_x': 'new_v', 'new_v_rec_b_x': 'new_v', 'new_v_rec_w_out': 'new_v', 'new_v_conf_w_pw1': 'new_v', 'new_v_conf_b_pw1': 'new_v', 'new_v_conf_conv_w': 'new_v', 'new_v_conf_conv_b': 'new_v', 'new_v_conf_ln_g': 'new_v', 'new_v_conf_ln_b': 'new_v', 'new_v_conf_w_pw2': 'new_v', 'new_v_conf_b_pw2': 'new_v', 'new_v_mlp_w_in': 'new_v', 'new_v_mlp_w_out': 'new_v', 'new_v_final_g': 'new_v'}


def _forward(args):
    return _fwd_reference(*[args[k] for k in FWD_PARAMS])


def _output_shape():
    out = _jax.eval_shape(lambda: _forward(_fwd_setup_inputs(0)))
    return out.shape, out.dtype

N_MICROBATCH = 1
ADAM_LR = 0.001
ADAM_B1 = 0.9
ADAM_B2 = 0.999
ADAM_EPS = 1e-08
ADAM_WD = 0.01
ADAM_STEP = 10
PER_EXAMPLE_BATCH_AXIS = {'x': 0, 'c': 0, 'ctx': 0, 'loss_target': 0}
SHARED_INPUTS = []
_WEIGHT_DTYPES = {'c_ctx': _jnp.float32, 'w_ada': _jnp.float32, 'b_ada': _jnp.float32, 'norm_g': _jnp.float32, 'rec_w_in': _jnp.float32, 'rec_conv_w': _jnp.float32, 'rec_conv_b': _jnp.float32, 'rec_lambda': _jnp.float32, 'rec_w_a': _jnp.float32, 'rec_b_a': _jnp.float32, 'rec_w_x': _jnp.float32, 'rec_b_x': _jnp.float32, 'rec_w_out': _jnp.float32, 'conf_w_pw1': _jnp.float32, 'conf_b_pw1': _jnp.float32, 'conf_conv_w': _jnp.float32, 'conf_conv_b': _jnp.float32, 'conf_ln_g': _jnp.float32, 'conf_ln_b': _jnp.float32, 'conf_w_pw2': _jnp.float32, 'conf_b_pw2': _jnp.float32, 'mlp_w_in': _jnp.float32, 'mlp_w_out': _jnp.float32, 'final_g': _jnp.float32}
MOMENT_SCALE = {'c_ctx': 2.638651e-02, 'w_ada': 2.902293e-01, 'b_ada': 5.234988e-01, 'norm_g': 1.568003e-01, 'rec_w_in': 2.120652e-01, 'rec_conv_w': 2.791063e-01, 'rec_conv_b': 4.559965e-01, 'rec_lambda': 5.955811e-02, 'rec_w_a': 1.230285e-02, 'rec_b_a': 2.139212e-02, 'rec_w_x': 3.108979e-02, 'rec_b_x': 5.134852e-02, 'rec_w_out': 2.754968e-01, 'conf_w_pw1': 1.294847e-02, 'conf_b_pw1': 1.712547e-02, 'conf_conv_w': 1.733071e-02, 'conf_conv_b': 4.501149e-02, 'conf_ln_g': 2.229966e-02, 'conf_ln_b': 2.670478e-02, 'conf_w_pw2': 1.953712e-02, 'conf_b_pw2': 5.342014e-02, 'mlp_w_in': 2.369708e-02, 'mlp_w_out': 5.351174e-02, 'final_g': 1.723146e+01}


def _to_microbatches(a, axis):
    t = _jnp.moveaxis(a, axis, 0)
    t = t.reshape((N_MICROBATCH, t.shape[0] // N_MICROBATCH) + t.shape[1:])
    return _jnp.moveaxis(t, 1, axis + 1)


def setup_inputs(seed: int = 0) -> dict:
    inp = _fwd_setup_inputs(seed)
    key = _jax.random.fold_in(_jax.random.key(seed), 7919)
    shape, _ = _output_shape()
    out = dict(inp)
    out["loss_target"] = _jax.random.normal(_jax.random.fold_in(key, 0), shape, _jnp.float32)
    for i, name in enumerate(TWIN_WEIGHTS):
        w = inp[name].astype(_jnp.float32)
        if MOMENT_SCALE is None:
            s = _jnp.sqrt(_jnp.mean(_jnp.square(w)) + 1e-30)
        else:
            s = MOMENT_SCALE[name]
        km, kv = _jax.random.split(_jax.random.fold_in(key, i + 1))
        out[name] = w
        out["m_" + name] = s * _jax.random.normal(km, w.shape, _jnp.float32)
        out["v_" + name] = (s * s) * _jax.random.uniform(kv, w.shape, _jnp.float32, 0.5, 1.5)
    if N_MICROBATCH > 1:
        for name, axis in PER_EXAMPLE_BATCH_AXIS.items():
            out[name] = _to_microbatches(out[name], axis)
    return {'x': out['x'], 'c': out['c'], 'ctx': out['ctx'], 'c_ctx': out['c_ctx'], 'w_ada': out['w_ada'], 'b_ada': out['b_ada'], 'norm_g': out['norm_g'], 'rec_w_in': out['rec_w_in'], 'rec_conv_w': out['rec_conv_w'], 'rec_conv_b': out['rec_conv_b'], 'rec_lambda': out['rec_lambda'], 'rec_w_a': out['rec_w_a'], 'rec_b_a': out['rec_b_a'], 'rec_w_x': out['rec_w_x'], 'rec_b_x': out['rec_b_x'], 'rec_w_out': out['rec_w_out'], 'conf_w_pw1': out['conf_w_pw1'], 'conf_b_pw1': out['conf_b_pw1'], 'conf_conv_w': out['conf_conv_w'], 'conf_conv_b': out['conf_conv_b'], 'conf_ln_g': out['conf_ln_g'], 'conf_ln_b': out['conf_ln_b'], 'conf_w_pw2': out['conf_w_pw2'], 'conf_b_pw2': out['conf_b_pw2'], 'mlp_w_in': out['mlp_w_in'], 'mlp_w_out': out['mlp_w_out'], 'final_g': out['final_g'], 'loss_target': out['loss_target'], 'm_c_ctx': out['m_c_ctx'], 'm_w_ada': out['m_w_ada'], 'm_b_ada': out['m_b_ada'], 'm_norm_g': out['m_norm_g'], 'm_rec_w_in': out['m_rec_w_in'], 'm_rec_conv_w': out['m_rec_conv_w'], 'm_rec_conv_b': out['m_rec_conv_b'], 'm_rec_lambda': out['m_rec_lambda'], 'm_rec_w_a': out['m_rec_w_a'], 'm_rec_b_a': out['m_rec_b_a'], 'm_rec_w_x': out['m_rec_w_x'], 'm_rec_b_x': out['m_rec_b_x'], 'm_rec_w_out': out['m_rec_w_out'], 'm_conf_w_pw1': out['m_conf_w_pw1'], 'm_conf_b_pw1': out['m_conf_b_pw1'], 'm_conf_conv_w': out['m_conf_conv_w'], 'm_conf_conv_b': out['m_conf_conv_b'], 'm_conf_ln_g': out['m_conf_ln_g'], 'm_conf_ln_b': out['m_conf_ln_b'], 'm_conf_w_pw2': out['m_conf_w_pw2'], 'm_conf_b_pw2': out['m_conf_b_pw2'], 'm_mlp_w_in': out['m_mlp_w_in'], 'm_mlp_w_out': out['m_mlp_w_out'], 'm_final_g': out['m_final_g'], 'v_c_ctx': out['v_c_ctx'], 'v_w_ada': out['v_w_ada'], 'v_b_ada': out['v_b_ada'], 'v_norm_g': out['v_norm_g'], 'v_rec_w_in': out['v_rec_w_in'], 'v_rec_conv_w': out['v_rec_conv_w'], 'v_rec_conv_b': out['v_rec_conv_b'], 'v_rec_lambda': out['v_rec_lambda'], 'v_rec_w_a': out['v_rec_w_a'], 'v_rec_b_a': out['v_rec_b_a'], 'v_rec_w_x': out['v_rec_w_x'], 'v_rec_b_x': out['v_rec_b_x'], 'v_rec_w_out': out['v_rec_w_out'], 'v_conf_w_pw1': out['v_conf_w_pw1'], 'v_conf_b_pw1': out['v_conf_b_pw1'], 'v_conf_conv_w': out['v_conf_conv_w'], 'v_conf_conv_b': out['v_conf_conv_b'], 'v_conf_ln_g': out['v_conf_ln_g'], 'v_conf_ln_b': out['v_conf_ln_b'], 'v_conf_w_pw2': out['v_conf_w_pw2'], 'v_conf_b_pw2': out['v_conf_b_pw2'], 'v_mlp_w_in': out['v_mlp_w_in'], 'v_mlp_w_out': out['v_mlp_w_out'], 'v_final_g': out['v_final_g']}


def _loss(weights, diff, rest, loss_target):
    with _jax.named_scope("forward"):
        args = {**rest, TWIN_DIFF_INPUT: diff, **{k: w.astype(_WEIGHT_DTYPES[k]) for k, w in weights.items()}}
        y = _forward(args)
    with _jax.named_scope("loss_head"):
        err = _jnp.square(y.astype(_jnp.float32) - loss_target)
        return 0.5 * _jnp.sum(_jnp.mean(err, axis=-1)) if err.ndim else 0.5 * err


def _adamw(w, g, m, v):
    m = ADAM_B1 * m + (1.0 - ADAM_B1) * g
    v = ADAM_B2 * v + (1.0 - ADAM_B2) * _jnp.square(g)
    m_hat = m / (1.0 - ADAM_B1 ** ADAM_STEP)
    v_hat = v / (1.0 - ADAM_B2 ** ADAM_STEP)
    delta = -ADAM_LR * (m_hat / (_jnp.sqrt(v_hat) + ADAM_EPS) + ADAM_WD * w)
    return delta, m, v


def reference(x, c, ctx, c_ctx, w_ada, b_ada, norm_g, rec_w_in, rec_conv_w, rec_conv_b, rec_lambda, rec_w_a, rec_b_a, rec_w_x, rec_b_x, rec_w_out, conf_w_pw1, conf_b_pw1, conf_conv_w, conf_conv_b, conf_ln_g, conf_ln_b, conf_w_pw2, conf_b_pw2, mlp_w_in, mlp_w_out, final_g, loss_target, m_c_ctx, m_w_ada, m_b_ada, m_norm_g, m_rec_w_in, m_rec_conv_w, m_rec_conv_b, m_rec_lambda, m_rec_w_a, m_rec_b_a, m_rec_w_x, m_rec_b_x, m_rec_w_out, m_conf_w_pw1, m_conf_b_pw1, m_conf_conv_w, m_conf_conv_b, m_conf_ln_g, m_conf_ln_b, m_conf_w_pw2, m_conf_b_pw2, m_mlp_w_in, m_mlp_w_out, m_final_g, v_c_ctx, v_w_ada, v_b_ada, v_norm_g, v_rec_w_in, v_rec_conv_w, v_rec_conv_b, v_rec_lambda, v_rec_w_a, v_rec_b_a, v_rec_w_x, v_rec_b_x, v_rec_w_out, v_conf_w_pw1, v_conf_b_pw1, v_conf_conv_w, v_conf_conv_b, v_conf_ln_g, v_conf_ln_b, v_conf_w_pw2, v_conf_b_pw2, v_mlp_w_in, v_mlp_w_out, v_final_g):
    given = dict(x=x, c=c, ctx=ctx, c_ctx=c_ctx, w_ada=w_ada, b_ada=b_ada, norm_g=norm_g, rec_w_in=rec_w_in, rec_conv_w=rec_conv_w, rec_conv_b=rec_conv_b, rec_lambda=rec_lambda, rec_w_a=rec_w_a, rec_b_a=rec_b_a, rec_w_x=rec_w_x, rec_b_x=rec_b_x, rec_w_out=rec_w_out, conf_w_pw1=conf_w_pw1, conf_b_pw1=conf_b_pw1, conf_conv_w=conf_conv_w, conf_conv_b=conf_conv_b, conf_ln_g=conf_ln_g, conf_ln_b=conf_ln_b, conf_w_pw2=conf_w_pw2, conf_b_pw2=conf_b_pw2, mlp_w_in=mlp_w_in, mlp_w_out=mlp_w_out, final_g=final_g, loss_target=loss_target, m_c_ctx=m_c_ctx, m_w_ada=m_w_ada, m_b_ada=m_b_ada, m_norm_g=m_norm_g, m_rec_w_in=m_rec_w_in, m_rec_conv_w=m_rec_conv_w, m_rec_conv_b=m_rec_conv_b, m_rec_lambda=m_rec_lambda, m_rec_w_a=m_rec_w_a, m_rec_b_a=m_rec_b_a, m_rec_w_x=m_rec_w_x, m_rec_b_x=m_rec_b_x, m_rec_w_out=m_rec_w_out, m_conf_w_pw1=m_conf_w_pw1, m_conf_b_pw1=m_conf_b_pw1, m_conf_conv_w=m_conf_conv_w, m_conf_conv_b=m_conf_conv_b, m_conf_ln_g=m_conf_ln_g, m_conf_ln_b=m_conf_ln_b, m_conf_w_pw2=m_conf_w_pw2, m_conf_b_pw2=m_conf_b_pw2, m_mlp_w_in=m_mlp_w_in, m_mlp_w_out=m_mlp_w_out, m_final_g=m_final_g, v_c_ctx=v_c_ctx, v_w_ada=v_w_ada, v_b_ada=v_b_ada, v_norm_g=v_norm_g, v_rec_w_in=v_rec_w_in, v_rec_conv_w=v_rec_conv_w, v_rec_conv_b=v_rec_conv_b, v_rec_lambda=v_rec_lambda, v_rec_w_a=v_rec_w_a, v_rec_b_a=v_rec_b_a, v_rec_w_x=v_rec_w_x, v_rec_b_x=v_rec_b_x, v_rec_w_out=v_rec_w_out, v_conf_w_pw1=v_conf_w_pw1, v_conf_b_pw1=v_conf_b_pw1, v_conf_conv_w=v_conf_conv_w, v_conf_conv_b=v_conf_conv_b, v_conf_ln_g=v_conf_ln_g, v_conf_ln_b=v_conf_ln_b, v_conf_w_pw2=v_conf_w_pw2, v_conf_b_pw2=v_conf_b_pw2, v_mlp_w_in=v_mlp_w_in, v_mlp_w_out=v_mlp_w_out, v_final_g=v_final_g)
    weights = {n: given[n] for n in TWIN_WEIGHTS}
    shared = {n: given[n] for n in SHARED_INPUTS}
    per_example = {n: given[n] for n in ['x', 'c', 'ctx']}
    grad_fn = _jax.value_and_grad(_loss, argnums=(0, 1))

    def one_microbatch(ex, loss_target):
        ex = dict(ex)
        diff = ex.pop(TWIN_DIFF_INPUT)
        return grad_fn(weights, diff, {**shared, **ex}, loss_target)

    if N_MICROBATCH == 1:
        loss, (grad_w, grad_x) = one_microbatch(per_example, given["loss_target"])
    else:
        def body(carry, xs):
            loss_sum, grad_sum = carry
            l_k, (gw_k, gx_k) = one_microbatch(xs[0], xs[1])
            with _jax.named_scope("update"):
                return (loss_sum + l_k, _jax.tree.map(_jnp.add, grad_sum, gw_k)), gx_k

        init = (_jnp.zeros((), _jnp.float32), _jax.tree.map(_jnp.zeros_like, weights))
        (loss, grad_w), grad_x = _jax.lax.scan(body, init, (per_example, given["loss_target"]))
    with _jax.named_scope("update"):
        delta_w, new_m, new_v = {}, {}, {}
        for n in TWIN_WEIGHTS:
            delta_w[n], new_m[n], new_v[n] = _adamw(weights[n], grad_w[n], given["m_" + n], given["v_" + n])
    return (loss, grad_x, *[grad_w[n] for n in TWIN_WEIGHTS], *[delta_w[n] for n in TWIN_WEIGHTS],
            *[new_m[n] for n in TWIN_WEIGHTS], *[new_v[n] for n in TWIN_WEIGHTS])
```

```python
import functools
import math

import jax
import jax.numpy as jnp
from jax import lax
from jax.experimental import pallas as pl
from jax.experimental.pallas import tpu as pltpu

F32 = jnp.float32
BF16 = jnp.bfloat16

D = 1024
T = 2048
TC = 256
TA = T + TC
R = 1280
RH = R // 2
NQ = 4 * RH
FF = 4096
N_BLK = 16
BLK = R // N_BLK
GRID_W = 64
EPS = 1e-6
RG_C = 8.0
CONF_KW = 31
REC_KW = 4
LANE = 128
ROW_TILE = 256
HALO = 16
RG_TILE = 128
V7X_VMEM_BYTES = 64 * 1024 * 1024
VMEM_LIMIT = V7X_VMEM_BYTES - 8 * 1024 * 1024

ADAM_LR = 0.001
ADAM_B1 = 0.9
ADAM_B2 = 0.999
ADAM_EPS = 1e-08
ADAM_WD = 0.01
ADAM_STEP = 10

MESH = pl.DeviceIdType.MESH
ANY = pl.BlockSpec(memory_space=pl.ANY)


def _sds(shape, dtype):
    return jax.ShapeDtypeStruct(tuple(shape), dtype)


def _pcall(body, **kw):
    return pl.pallas_call(body, **kw)


def _cparams():
    return pltpu.CompilerParams(vmem_limit_bytes=VMEM_LIMIT)


def _full_spec(arr):
    nd = arr.ndim
    return pl.BlockSpec(arr.shape, lambda *ids, _n=nd: (0,) * _n)


def _sum0(v):
    return jnp.sum(v, axis=0, keepdims=True)


def _tiled(name, fn, grid, ins, vecs, outs, vec_outs=(), vec_refs=False):
    n_in, n_vec, n_out = len(ins), len(vecs), len(outs)
    n_grid = len(grid)

    def kern(*refs):
        ids = [pl.program_id(a) for a in range(n_grid)]
        tin = [r[...] for r in refs[:n_in]]
        vin = list(refs[n_in:n_in + n_vec]) if vec_refs else [r[...] for r in refs[n_in:n_in + n_vec]]
        o_refs = refs[n_in + n_vec:n_in + n_vec + n_out]
        a_refs = refs[n_in + n_vec + n_out:]
        tout, incs = fn(ids, tin, vin)
        for r, v in zip(o_refs, tout):
            r[...] = v.astype(r.dtype)
        if a_refs:
            first = functools.reduce(jnp.logical_and, [i == 0 for i in ids])

            @pl.when(first)
            def _():
                for r in a_refs:
                    r[...] = jnp.zeros_like(r)

            for r, v in zip(a_refs, incs):
                r[...] += v

    out_shape = [o for o, _ in outs] + [_sds(s, F32) for s in vec_outs]
    out_specs = [s for _, s in outs] + [
        pl.BlockSpec(tuple(s), lambda *ids, _n=len(s): (0,) * _n) for s in vec_outs]
    res = _pcall(
        kern, name=name, grid=tuple(grid),
        in_specs=[s for _, s in ins] + [_full_spec(v) for v in vecs],
        out_specs=out_specs, out_shape=out_shape, compiler_params=_cparams(),
    )(*[a for a, _ in ins], *vecs)
    return list(res)


def _rows(arr, ncols=None, tm=ROW_TILE, off=0, col=0, clamp_lo=False):
    ncols = arr.shape[1] if ncols is None else ncols
    if clamp_lo:
        return arr, pl.BlockSpec((tm, ncols), lambda i: (jnp.maximum(i + off, 0), col))
    return arr, pl.BlockSpec((tm, ncols), lambda i: (i + off, col))


def _orow(nrows, ncols, dtype, tm=ROW_TILE, off=0, clamp_lo=False):
    if clamp_lo:
        return _sds((nrows, ncols), dtype), pl.BlockSpec((tm, ncols), lambda i: (jnp.maximum(i + off, 0), 0))
    return _sds((nrows, ncols), dtype), pl.BlockSpec((tm, ncols), lambda i: (i + off, 0))


_NN = (((1,), (0,)), ((), ()))
_TN = (((0,), (0,)), ((), ()))
_NT = (((1,), (1,)), ((), ()))


def _mm(name, a, b, dims, grid, a_spec, b_spec, out, acc_shape, extra=(), a_pre=None, epi=None):
    n_k = grid[2]
    n_ex = len(extra)

    def kern(a_ref, b_ref, *rest):
        ex = rest[:n_ex]
        o_refs = rest[n_ex:-1]
        acc = rest[-1]
        k = pl.program_id(2)

        @pl.when(k == 0)
        def _():
            acc[...] = jnp.zeros_like(acc)

        av = a_ref[...]
        if a_pre is not None:
            av = a_pre(av)
        acc[...] += lax.dot_general(av.astype(BF16), b_ref[...].astype(BF16), dims,
                                    preferred_element_type=F32)

        @pl.when(k == n_k - 1)
        def _():
            vals = [acc[...]] if epi is None else epi(acc[...], [e[...] for e in ex])
            for r, v in zip(o_refs, vals):
                r[...] = v.astype(r.dtype)

    res = _pcall(
        kern, name=name, grid=tuple(grid),
        in_specs=[a_spec, b_spec] + [s for _, s in extra],
        out_specs=[s for _, s in out], out_shape=[o for o, _ in out],
        scratch_shapes=[pltpu.VMEM(tuple(acc_shape), F32)], compiler_params=_cparams(),
    )(a, b, *[e for e, _ in extra])
    return list(res)


def _rms(x):
    r = lax.rsqrt(jnp.mean(x * x, axis=-1, keepdims=True) + EPS)
    return x * r, r


def _norm_mod(x, g, sc, sh):
    n, _ = _rms(x)
    return (n * g) * (1.0 + sc) + sh


def _norm_mod_bwd(dh, x, g, sc):
    n, r = _rms(x)
    d_sh = _sum0(dh)
    d_sc = _sum0(dh * (n * g))
    d_g = _sum0(dh * (1.0 + sc) * n)
    dn = dh * (g * (1.0 + sc))
    dx = r * (dn - n * jnp.mean(dn * n, axis=-1, keepdims=True))
    return dx, d_sh, d_sc, d_g


_GELU_K = math.sqrt(2.0 / math.pi)


def _gelu(x):
    t = jnp.tanh(_GELU_K * (x + 0.044715 * x * x * x))
    return 0.5 * x * (1.0 + t), t


def _gelu_grad(x, t):
    return 0.5 * (1.0 + t) + 0.5 * x * (1.0 - t * t) * (_GELU_K * (1.0 + 3.0 * 0.044715 * x * x))


def _sigmoid(x):
    return 1.0 / (1.0 + jnp.exp(-x))


def _expm1(x):
    p = 1.0 + x * (1.0 / 9.0)
    for n in (8.0, 7.0, 6.0, 5.0, 4.0, 3.0, 2.0):
        p = 1.0 + (x * (1.0 / n)) * p
    return jnp.where(jnp.abs(x) < 0.5, x * p, jnp.exp(x) - 1.0)


def _softplus_neg(lam):
    return jnp.log1p(jnp.exp(-jnp.abs(lam))) + jnp.maximum(-lam, 0.0)


def _layernorm_parts(x):
    mu = jnp.mean(x, axis=-1, keepdims=True)
    xc = x - mu
    rstd = lax.rsqrt(jnp.mean(xc * xc, axis=-1, keepdims=True) + EPS)
    return xc * rstd, rstd


def _rg_gates(u, wbd, gbias, lam):
    sp = _softplus_neg(lam)
    parts = {}
    for h in range(2):
        uh = u[:, h * RH:(h + 1) * RH]
        g = jnp.dot(uh.astype(BF16), wbd[h], preferred_element_type=F32) + gbias[:, h * NQ:(h + 1) * NQ]
        for d in range(2):
            r = _sigmoid(g[:, (2 * d) * RH:(2 * d + 1) * RH])
            i = _sigmoid(g[:, (2 * d + 1) * RH:(2 * d + 2) * RH])
            sph = sp[d:d + 1, h * RH:(h + 1) * RH]
            la = (-RG_C) * r * sph
            e2 = _expm1(2.0 * la)
            parts[(d, h)] = dict(r=r, i=i, la=la, a=jnp.exp(la), e2=e2, mult=jnp.sqrt(-e2), uh=uh, sp=sph)
    return parts


def _rg_fwd_fn(ids, tin, vin):
    (u,) = tin
    wbd = vin[0]
    parts = _rg_gates(u, wbd, vin[1][...], vin[2][...])
    outs = []
    for d in range(2):
        a = jnp.concatenate([parts[(d, h)]["a"] for h in range(2)], axis=1)
        b = jnp.concatenate([parts[(d, h)]["mult"] * parts[(d, h)]["i"] * parts[(d, h)]["uh"]
                             for h in range(2)], axis=1)
        outs += [a, b]
    return outs, []


def _rg_bwd_fn(ids, tin, vin):
    u, da_f, db_f, da_r, db_r = tin
    wbd, lam = vin[0], vin[2][...]
    parts = _rg_gates(u, wbd, vin[1][...], lam)
    dab = ((da_f, db_f), (da_r, db_r))
    dsig_lam = -_sigmoid(-lam)
    du_halves, dpre_halves, dlam = [], [], [[None, None], [None, None]]
    for h in range(2):
        du = jnp.zeros_like(parts[(0, h)]["uh"])
        dpre = []
        for d in range(2):
            p = parts[(d, h)]
            da = dab[d][0][:, h * RH:(h + 1) * RH]
            db = dab[d][1][:, h * RH:(h + 1) * RH]
            d_mult = db * p["i"] * p["uh"]
            d_i = db * p["mult"] * p["uh"]
            du = du + db * p["mult"] * p["i"]
            d_la = da * p["a"] - d_mult * (p["e2"] + 1.0) / p["mult"]
            d_r = d_la * ((-RG_C) * p["sp"])
            dlam[d][h] = _sum0(d_la * ((-RG_C) * p["r"])) * dsig_lam[d:d + 1, h * RH:(h + 1) * RH]
            dpre += [d_r * p["r"] * (1.0 - p["r"]), d_i * p["i"] * (1.0 - p["i"])]
        dpre = jnp.concatenate(dpre, axis=1)
        du = du + lax.dot_general(dpre.astype(BF16), wbd[h], _NT, preferred_element_type=F32)
        du_halves.append(du)
        dpre_halves.append(dpre)
    dpre_all = jnp.concatenate(dpre_halves, axis=1)
    dlam_row = jnp.concatenate([dlam[0][0], dlam[0][1], dlam[1][0], dlam[1][1]], axis=1)
    return [dpre_all, jnp.concatenate(du_halves, axis=1)], [_sum0(dpre_all), dlam_row]


def _tile_flags(i, n_tiles, seq_starts):
    starts_here = functools.reduce(jnp.logical_or, [i == s for s in seq_starts])
    ends_here = functools.reduce(jnp.logical_or, [i + 1 == s for s in seq_starts] + [i + 1 == n_tiles])
    return jnp.logical_not(starts_here), jnp.logical_not(ends_here)


def _halo_specs(col0, cw):
    hb = ROW_TILE // HALO
    prev = pl.BlockSpec((HALO, cw), lambda i, c: (jnp.maximum(i * hb - 1, 0), col0 + c))
    cur = pl.BlockSpec((ROW_TILE, cw), lambda i, c: (i, col0 + c))
    return prev, cur, hb


def _window(prev_ref, cur_ref, next_ref, has_prev, has_next):
    prev = jnp.where(has_prev, prev_ref[...], 0.0)
    nxt = jnp.where(has_next, next_ref[...], 0.0)
    return jnp.concatenate([prev, cur_ref[...], nxt], axis=0)


def _dwconv(name, x, col0, w, bias, pad_left, seq_starts, n_ch, cw=256):
    n_rows = x.shape[0]
    n_tiles = n_rows // ROW_TILE
    n_taps = w.shape[0]
    prev_spec, cur_spec, hb = _halo_specs(col0, cw)
    last_hb = n_rows // HALO - 1
    next_spec = pl.BlockSpec((HALO, cw), lambda i, c: (jnp.minimum((i + 1) * hb, last_hb), col0 + c))

    def kern(prev_ref, cur_ref, next_ref, w_ref, b_ref, o_ref):
        has_prev, has_next = _tile_flags(pl.program_id(0), n_tiles, seq_starts)
        win = _window(prev_ref, cur_ref, next_ref, has_prev, has_next)
        wv = w_ref[...]
        acc = jnp.zeros((ROW_TILE, cw), F32) + b_ref[...]
        for k in range(n_taps):
            off = HALO + k - pad_left
            acc = acc + wv[k:k + 1, :] * win[off:off + ROW_TILE, :]
        o_ref[...] = acc

    return _pcall(
        kern, name=name, grid=(n_tiles, n_ch // cw),
        in_specs=[prev_spec, cur_spec, next_spec,
                  pl.BlockSpec((n_taps, cw), lambda i, c: (0, c)), pl.BlockSpec((1, cw), lambda i, c: (0, c))],
        out_specs=pl.BlockSpec((ROW_TILE, cw), lambda i, c: (i, c)),
        out_shape=_sds((n_rows, n_ch), F32), compiler_params=_cparams(),
    )(x, x, x, w, bias)


def _dwconv_wgrad(name, dy, x, col0, n_taps, pad_left, seq_starts, n_ch, cw=256):
    n_rows = dy.shape[0]
    n_tiles = n_rows // ROW_TILE
    n_out = -(-(n_taps + 1) // 8) * 8
    prev_spec, cur_spec, hb = _halo_specs(col0, cw)
    last_hb = n_rows // HALO - 1
    next_spec = pl.BlockSpec((HALO, cw), lambda c, i: (jnp.minimum((i + 1) * hb, last_hb), col0 + c))
    prev_spec = pl.BlockSpec((HALO, cw), lambda c, i: (jnp.maximum(i * hb - 1, 0), col0 + c))
    cur_spec = pl.BlockSpec((ROW_TILE, cw), lambda c, i: (i, col0 + c))

    def kern(dy_ref, prev_ref, cur_ref, next_ref, o_ref):
        i = pl.program_id(1)
        has_prev, has_next = _tile_flags(i, n_tiles, seq_starts)
        win = _window(prev_ref, cur_ref, next_ref, has_prev, has_next)
        dyv = dy_ref[...]
        rid = lax.broadcasted_iota(jnp.int32, (n_out, cw), 0)
        inc = jnp.where(rid == n_taps, _sum0(dyv), 0.0)
        for k in range(n_taps):
            off = HALO + k - pad_left
            inc = inc + jnp.where(rid == k, _sum0(dyv * win[off:off + ROW_TILE, :]), 0.0)

        @pl.when(i == 0)
        def _():
            o_ref[...] = jnp.zeros_like(o_ref)

        o_ref[...] += inc

    return _pcall(
        kern, name=name, grid=(n_ch // cw, n_tiles),
        in_specs=[pl.BlockSpec((ROW_TILE, cw), lambda c, i: (i, c)), prev_spec, cur_spec, next_spec],
        out_specs=pl.BlockSpec((n_out, cw), lambda c, i: (0, c)),
        out_shape=_sds((n_out, n_ch), F32), compiler_params=_cparams(),
    )(dy, x, x, x)


N_SCAN = TA // ROW_TILE


def _rev_block(j):
    return jnp.where(j == 0, 0, N_SCAN - j)


def _scan_fwd(a_f, b_f, a_r, b_r):
    fwd_spec = pl.BlockSpec((ROW_TILE, R), lambda i: (i, 0))
    rev_spec = pl.BlockSpec((ROW_TILE, R), lambda i: (_rev_block(i), 0))
    hin_spec = pl.BlockSpec((None, 1, R), lambda i: (i, 0, 0))

    def kern(af, bf, ar, br, yf, yr, hin_f, hin_r, hf_s, hr_s):
        @pl.when(pl.program_id(0) == 0)
        def _():
            hf_s[...] = jnp.zeros_like(hf_s)
            hr_s[...] = jnp.zeros_like(hr_s)

        hin_f[...] = hf_s[...]
        hin_r[...] = hr_s[...]

        def step(s8, carry):
            hf, hr = carry
            t0 = pl.multiple_of(s8 * 8, 8)
            for q in range(8):
                tf = t0 + q
                hf = af[pl.ds(tf, 1), :] * hf + bf[pl.ds(tf, 1), :]
                yf[pl.ds(tf, 1), :] = hf
                tr = ROW_TILE - 1 - tf
                hr = ar[pl.ds(tr, 1), :] * hr + br[pl.ds(tr, 1), :]
                yr[pl.ds(tr, 1), :] = hr
            return hf, hr

        hf, hr = lax.fori_loop(0, ROW_TILE // 8, step, (hf_s[...], hr_s[...]))
        hf_s[...] = hf
        hr_s[...] = hr

    return _pcall(
        kern, name="scan_fwd", grid=(N_SCAN,),
        in_specs=[fwd_spec, fwd_spec, rev_spec, rev_spec],
        out_specs=[fwd_spec, rev_spec, hin_spec, hin_spec],
        out_shape=[_sds((TA, R), F32), _sds((TA, R), F32), _sds((N_SCAN, 1, R), F32), _sds((N_SCAN, 1, R), F32)],
        scratch_shapes=[pltpu.VMEM((1, R), F32), pltpu.VMEM((1, R), F32)], compiler_params=_cparams(),
    )(a_f, b_f, a_r, b_r)


def _scan_bwd(dy, a_f, y_f, hin_f, a_r, y_r, hin_r):
    fwd_spec = pl.BlockSpec((ROW_TILE, R), lambda i: (N_SCAN - 1 - i, 0))
    rev_spec = pl.BlockSpec((ROW_TILE, R), lambda i: (_rev_block(N_SCAN - 1 - i), 0))
    hin_spec = pl.BlockSpec((None, 1, R), lambda i: (N_SCAN - 1 - i, 0, 0))
    last = ROW_TILE - 1

    def kern(dyf, af, yf, hf0, dyr, ar, yr, hr0, daf, dbf, dar, dbr, gf_s, anf_s, gr_s, anr_s):
        @pl.when(pl.program_id(0) == 0)
        def _():
            for r in (gf_s, anf_s, gr_s, anr_s):
                r[...] = jnp.zeros_like(r)

        def one(dy_ref, a_ref, y_ref, da_ref, db_ref, g, an, p, pprev):
            gnew = dy_ref[pl.ds(p, 1), :] + an * g
            db_ref[pl.ds(p, 1), :] = gnew
            da_ref[pl.ds(p, 1), :] = gnew * y_ref[pl.ds(pprev, 1), :]
            return gnew, a_ref[pl.ds(p, 1), :]

        def step(s8, carry):
            gf, anf, gr, anr = carry
            base = s8 * 8
            for q in range(8):
                s = last - (base + q)
                gf, anf = one(dyf, af, yf, daf, dbf, gf, anf, s, s - 1)
                gr, anr = one(dyr, ar, yr, dar, dbr, gr, anr, last - s, last - s + 1)
            return gf, anf, gr, anr

        carry = (gf_s[...], anf_s[...], gr_s[...], anr_s[...])
        carry = lax.fori_loop(0, ROW_TILE // 8 - 1, step, carry)
        gf, anf, gr, anr = carry
        for s in range(7, 0, -1):
            gf, anf = one(dyf, af, yf, daf, dbf, gf, anf, s, s - 1)
            gr, anr = one(dyr, ar, yr, dar, dbr, gr, anr, last - s, last - s + 1)
        gf0 = dyf[0:1, :] + anf * gf
        dbf[0:1, :] = gf0
        daf[0:1, :] = gf0 * hf0[...]
        gr0 = dyr[last:last + 1, :] + anr * gr
        dbr[last:last + 1, :] = gr0
        dar[last:last + 1, :] = gr0 * hr0[...]
        gf_s[...] = gf0
        anf_s[...] = af[0:1, :]
        gr_s[...] = gr0
        anr_s[...] = ar[last:last + 1, :]

    return _pcall(
        kern, name="scan_bwd", grid=(N_SCAN,),
        in_specs=[fwd_spec, fwd_spec, fwd_spec, hin_spec, rev_spec, rev_spec, rev_spec, hin_spec],
        out_specs=[fwd_spec, fwd_spec, rev_spec, rev_spec],
        out_shape=[_sds((TA, R), F32)] * 4,
        scratch_shapes=[pltpu.VMEM((1, R), F32)] * 4, compiler_params=_cparams(),
    )(dy, a_f, y_f, hin_f, dy, a_r, y_r, hin_r)


def _me():
    return lax.axis_index("x"), lax.axis_index("y"), lax.axis_index("c")


def _other_chips(mx, my):
    return [(1 - mx, my), (mx, 1 - my), (1 - mx, 1 - my)]


def _rcopy(src, dst, ssem, rsem, dev):
    return pltpu.make_async_remote_copy(src_ref=src, dst_ref=dst, send_sem=ssem, recv_sem=rsem,
                                        device_id=dev, device_id_type=MESH)


def _allgather8(name, x):
    rows, cols = x.shape

    def kern(x_ref, o_ref, ssem, rsem, lsem):
        mx, my, mc = _me()
        me = 4 * mx + 2 * my + mc
        peers = []
        for k in range(1, 8):
            px = 1 - mx if (k >> 2) & 1 else mx
            py = 1 - my if (k >> 1) & 1 else my
            pc = 1 - mc if k & 1 else mc
            peers.append((px, py, pc))
        mine = pltpu.make_async_copy(x_ref, o_ref.at[me], lsem)
        mine.start()
        sends = [_rcopy(x_ref, o_ref.at[me], ssem.at[k], rsem.at[k], p) for k, p in enumerate(peers)]
        for cp in sends:
            cp.start()
        for k, (px, py, pc) in enumerate(peers):
            _rcopy(x_ref, o_ref.at[4 * px + 2 * py + pc], ssem.at[k], rsem.at[k], (px, py, pc)).wait_recv()
        for cp in sends:
            cp.wait_send()
        mine.wait()

    return _pcall(
        kern, name=name, in_specs=[ANY], out_specs=ANY, out_shape=_sds((8, rows, cols), F32),
        scratch_shapes=[pltpu.SemaphoreType.DMA((7,)), pltpu.SemaphoreType.DMA((7,)), pltpu.SemaphoreType.DMA(())],
    )(x)


def _gather_weights(ws):
    n = len(ws)

    def kern(*refs):
        w, o = refs[:n], refs[n:2 * n]
        s1, r1, s2, r2, ls = refs[2 * n:]
        mx, my, mc = _me()
        j0 = 2 * mx + my
        chips = _other_chips(mx, my)
        sib = (mx, my, 1 - mc)
        started = []
        for t in range(n):
            lc = pltpu.make_async_copy(w[t], o[t].at[j0], ls.at[t])
            lc.start()
            started.append(lc)
        firsts = []
        for t in range(n):
            for q, (qx, qy) in enumerate(chips):
                cp = _rcopy(w[t].at[mc], o[t].at[j0, mc], s1.at[3 * t + q], r1.at[3 * t + q], (qx, qy, mc))
                cp.start()
                firsts.append(cp)
        passed = []
        for t in range(n):
            for q, (qx, qy) in enumerate(chips):
                jq = 2 * qx + qy
                _rcopy(w[t].at[mc], o[t].at[jq, mc], s1.at[3 * t + q], r1.at[3 * t + q], (qx, qy, mc)).wait_recv()
                fw = _rcopy(o[t].at[jq, mc], o[t].at[jq, mc], s2.at[3 * t + q], r2.at[3 * t + q], sib)
                fw.start()
                passed.append(fw)
        for t in range(n):
            for q, (qx, qy) in enumerate(chips):
                jq = 2 * qx + qy
                _rcopy(o[t].at[jq, 1 - mc], o[t].at[jq, 1 - mc], s2.at[3 * t + q], r2.at[3 * t + q], sib).wait_recv()
        for cp in firsts + passed:
            cp.wait_send()
        for lc in started:
            lc.wait()

    dma = pltpu.SemaphoreType.DMA
    return _pcall(
        kern, name="gather_weights", in_specs=[ANY] * n, out_specs=[ANY] * n,
        out_shape=[_sds((4,) + w.shape, w.dtype) for w in ws],
        scratch_shapes=[dma((3 * n,)), dma((3 * n,)), dma((3 * n,)), dma((3 * n,)), dma((n,))],
    )(*ws)


def _reduce_pair(gs):
    n = len(gs)

    def kern(*refs):
        g, o = refs[:n], refs[n:2 * n]
        ss, rs, ls = refs[2 * n:]
        mx, my, mc = _me()
        sib = (mx, my, 1 - mc)
        locs, sends = [], []
        for t in range(n):
            for j in range(4):
                lc = pltpu.make_async_copy(g[t].at[j, mc], o[t].at[j, 0], ls.at[4 * t + j])
                lc.start()
                locs.append(lc)
                cp = _rcopy(g[t].at[j, 1 - mc], o[t].at[j, 1], ss.at[4 * t + j], rs.at[4 * t + j], sib)
                cp.start()
                sends.append(cp)
        for cp in sends:
            cp.wait_recv()
        for cp in sends:
            cp.wait_send()
        for lc in locs:
            lc.wait()

    dma = pltpu.SemaphoreType.DMA
    return _pcall(
        kern, name="reduce_pair", in_specs=[ANY] * n, out_specs=[ANY] * n,
        out_shape=[_sds(g.shape, g.dtype) for g in gs],
        scratch_shapes=[dma((4 * n,)), dma((4 * n,)), dma((4 * n,))],
    )(*gs)


def _reduce_chips(ss_):
    n = len(ss_)

    def kern(*refs):
        s, o = refs[:n], refs[n:2 * n]
        ssem, rsem, ls = refs[2 * n:]
        mx, my, mc = _me()
        j0 = 2 * mx + my
        chips = _other_chips(mx, my)
        locs, sends = [], []
        for t in range(n):
            lc = pltpu.make_async_copy(s[t].at[j0], o[t].at[j0], ls.at[t])
            lc.start()
            locs.append(lc)
            for q, (qx, qy) in enumerate(chips):
                cp = _rcopy(s[t].at[2 * qx + qy], o[t].at[j0], ssem.at[3 * t + q], rsem.at[3 * t + q], (qx, qy, mc))
                cp.start()
                sends.append(cp)
        for t in range(n):
            for q, (qx, qy) in enumerate(chips):
                jq = 2 * qx + qy
                _rcopy(s[t].at[jq], o[t].at[jq], ssem.at[3 * t + q], rsem.at[3 * t + q], (qx, qy, mc)).wait_recv()
        for cp in sends:
            cp.wait_send()
        for lc in locs:
            lc.wait()

    dma = pltpu.SemaphoreType.DMA
    return _pcall(
        kern, name="reduce_chips", in_specs=[ANY] * n, out_specs=[ANY] * n,
        out_shape=[_sds(s.shape, s.dtype) for s in ss_],
        scratch_shapes=[dma((3 * n,)), dma((3 * n,)), dma((n,))],
    )(*ss_)


def _share_halves(reds):
    n = len(reds)

    def kern(*refs):
        r, o = refs[:n], refs[n:2 * n]
        ss, rs, ls = refs[2 * n:]
        mx, my, mc = _me()
        sib = (mx, my, 1 - mc)
        locs, sends = [], []
        for t in range(n):
            lc = pltpu.make_async_copy(r[t], o[t].at[mc], ls.at[t])
            lc.start()
            locs.append(lc)
            cp = _rcopy(r[t], o[t].at[mc], ss.at[t], rs.at[t], sib)
            cp.start()
            sends.append(cp)
        for t in range(n):
            _rcopy(r[t], o[t].at[1 - mc], ss.at[t], rs.at[t], sib).wait_recv()
        for cp in sends:
            cp.wait_send()
        for lc in locs:
            lc.wait()

    dma = pltpu.SemaphoreType.DMA
    return _pcall(
        kern, name="share_halves", in_specs=[ANY] * n, out_specs=[ANY] * n,
        out_shape=[_sds((2,) + r.shape, r.dtype) for r in reds],
        scratch_shapes=[dma((n,)), dma((n,)), dma((n,))],
    )(*reds)


def _pack(parts):
    flat, offs, pos = [], [], 0
    for p in parts:
        v = p.reshape(-1).astype(F32)
        n = -(-v.shape[0] // LANE) * LANE
        flat.append(jnp.pad(v, (0, n - v.shape[0])))
        offs.append((pos, v.shape[0], p.shape))
        pos += n
    total = -(-pos // (8 * LANE)) * 8 * LANE
    flat.append(jnp.zeros((total - pos,), F32))
    return jnp.concatenate(flat).reshape(-1, LANE), offs


def _unpack(vec, offs):
    v = vec.reshape(-1)
    return [v[p:p + n].reshape(shape) for p, n, shape in offs]


def _sum_devices(name, g8):
    rows = g8.shape[1]
    tr = 8
    for cand in (1024, 512, 256, 128, 64, 32, 16, 8):
        if rows % cand == 0:
            tr = cand
            break

    def fn(ids, tin, vin):
        acc = tin[0]
        for v in tin[1:]:
            acc = acc + v
        return [acc], []

    ins = [(g8, pl.BlockSpec((None, tr, LANE), lambda i, _d=d: (_d, i, 0))) for d in range(8)]
    out = (_sds((rows, LANE), F32), pl.BlockSpec((tr, LANE), lambda i: (i, 0)))
    return _tiled(name, fn, (rows // tr,), ins, [], [out])[0]


def _adamw(name, w, g, m, v):
    rows, cols = w.shape
    tr = rows
    for cand in (512, 256, 128, 64, 32, 16, 8):
        if rows % cand == 0 and cand * cols * 4 <= 2 * 1024 * 1024:
            tr = cand
            break
    bc1 = 1.0 - ADAM_B1 ** ADAM_STEP
    bc2 = 1.0 - ADAM_B2 ** ADAM_STEP

    def fn(ids, tin, vin):
        wv, gv, mv, vv = tin
        mn = ADAM_B1 * mv + (1.0 - ADAM_B1) * gv
        vn = ADAM_B2 * vv + (1.0 - ADAM_B2) * (gv * gv)
        delta = -ADAM_LR * ((mn / bc1) / (jnp.sqrt(vn / bc2) + ADAM_EPS) + ADAM_WD * wv)
        return [delta, mn, vn], []

    spec = pl.BlockSpec((tr, cols), lambda i: (i, 0))
    outs = [(_sds((rows, cols), F32), spec)] * 3
    return _tiled(name, fn, (rows // tr,), [(a, spec) for a in (w, g, m, v)], [], outs)


def _pos_embed():
    t = jnp.arange(T, dtype=jnp.int32)
    row = (t // GRID_W).astype(F32)
    col = (t % GRID_W).astype(F32)
    q = D // 4
    omega = 1.0 / (10000.0 ** (jnp.arange(q, dtype=F32) / q))
    er = row[:, None] * omega[None, :]
    ec = col[:, None] * omega[None, :]
    return jnp.concatenate([jnp.sin(er), jnp.cos(er), jnp.sin(ec), jnp.cos(ec)], axis=-1).astype(F32)


def _dense_gates(w_a, w_x):
    per = N_BLK // 2
    out = jnp.zeros((2, per, BLK, 4, per, BLK), F32)
    for q, src in enumerate((w_a[0], w_x[0], w_a[1], w_x[1])):
        blocks = src.reshape(2, per, BLK, BLK)
        for nb in range(per):
            out = out.at[:, nb, :, q, nb, :].set(blocks[:, nb])
    return out.reshape(2, RH, NQ).astype(BF16)


def _gate_block_grads(dwbd):
    per = N_BLK // 2
    d6 = dwbd.reshape(2, per, BLK, 4, per, BLK)
    kinds = []
    for q in range(4):
        kinds.append(jnp.stack([d6[:, nb, :, q, nb, :] for nb in range(per)], axis=1).reshape(N_BLK, BLK, BLK))
    return jnp.stack([kinds[0], kinds[2]]), jnp.stack([kinds[1], kinds[3]])


def _gate_bias_dense(b_a, b_x):
    cols = []
    for h in range(2):
        for src in (b_a[0], b_x[0], b_a[1], b_x[1]):
            cols.append(src.reshape(R)[h * RH:(h + 1) * RH])
    return jnp.concatenate(cols).reshape(1, 2 * NQ)


def _gate_bias_grads(dgb):
    v = dgb.reshape(2, 4, RH)
    kinds = [jnp.concatenate([v[0, q], v[1, q]]).reshape(N_BLK, BLK) for q in range(4)]
    return jnp.stack([kinds[0], kinds[2]]), jnp.stack([kinds[1], kinds[3]])


def _mlp_fwd(tag, x_in, g_norm, sh, sc, gate, w_in, w_out):
    n_t = T // ROW_TILE
    (h,) = _tiled(f"{tag}_norm", lambda ids, t, v: ([_norm_mod(t[0], v[0], v[1], v[2])], []), (n_t,),
                  [_rows(x_in)], [g_norm, sc, sh], [_orow(T, D, BF16)])
    tm = 512
    (r,) = _mm(f"{tag}_in", h, w_in, _NN, (T // tm, 4, 1),
               pl.BlockSpec((tm, D), lambda i, j, k: (i, 0)), pl.BlockSpec((None, D, D), lambda i, j, k: (j, 0, 0)),
               [(_sds((T, FF), BF16), pl.BlockSpec((tm, D), lambda i, j, k: (i, j)))], (tm, D),
               epi=lambda acc, ex: [jnp.maximum(acc, 0.0)])
    o, x_out = _mm(f"{tag}_out", r, w_out, _NN, (T // tm, 1, FF // D),
                   pl.BlockSpec((tm, D), lambda i, j, k: (i, k)), pl.BlockSpec((D, D), lambda i, j, k: (k, 0)),
                   [(_sds((T, D), F32), pl.BlockSpec((tm, D), lambda i, j, k: (i, 0)))] * 2, (tm, D),
                   extra=[(x_in, pl.BlockSpec((tm, D), lambda i, j, k: (i, 0))), (gate, _full_spec(gate))],
                   a_pre=lambda a: a * a, epi=lambda acc, ex: [acc, ex[0] + ex[1] * acc])
    return dict(h=h, r=r, o=o, x_in=x_in), x_out


def _gate_bwd(tag, dx, o, gate):
    def fn(ids, t, v):
        d_o = t[0] * v[0]
        return [d_o], [_sum0(t[0] * t[1]), _sum0(d_o)]
    return _tiled(f"{tag}_gate_bwd", fn, (T // ROW_TILE,), [_rows(dx), _rows(o)], [gate],
                  [_orow(T, D, BF16)], [(1, D), (1, D)])


def _norm_bwd(tag, dx_res, dh, dh_off, x, g_norm, sc, with_dx=True):
    n_t = x.shape[0] // ROW_TILE

    def fn(ids, t, v):
        if with_dx:
            dres, dhv, xv = t
        else:
            dhv, xv = t
        dxv, d_sh, d_sc, d_g = _norm_mod_bwd(dhv, xv, v[0], v[1])
        return ([dres + dxv] if with_dx else []), [d_sh, d_sc, d_g]

    ins = ([_rows(dx_res)] if with_dx else []) + [_rows(dh, off=dh_off), _rows(x)]
    outs = [_orow(x.shape[0], D, F32)] if with_dx else []
    return _tiled(f"{tag}_norm_bwd", fn, (n_t,), ins, [g_norm, sc], outs, [(1, D)] * 3)


def _mlp_bwd(tag, dx, saved, g_norm, sc, gate, w_in, w_out):
    d_o, d_gate, _ = _gate_bwd(tag, dx, saved["o"], gate)
    tm = 512
    r = saved["r"]
    (da,) = _mm(f"{tag}_dz", d_o, w_out, _NT, (T // tm, FF // D, 1),
                pl.BlockSpec((tm, D), lambda i, j, k: (i, 0)), pl.BlockSpec((D, D), lambda i, j, k: (j, 0)),
                [(_sds((T, FF), BF16), pl.BlockSpec((tm, D), lambda i, j, k: (i, j)))], (tm, D),
                extra=[(r, pl.BlockSpec((tm, D), lambda i, j, k: (i, j)))],
                epi=lambda acc, ex: [acc * (2.0 * ex[0].astype(F32))])
    tk = 512
    (dw_out,) = _mm(f"{tag}_dwout", r, d_o, _TN, (FF // tm, 1, T // tk),
                    pl.BlockSpec((tk, tm), lambda i, j, k: (k, i)), pl.BlockSpec((tk, D), lambda i, j, k: (k, 0)),
                    [(_sds((FF, D), F32), pl.BlockSpec((tm, D), lambda i, j, k: (i, 0)))], (tm, D),
                    a_pre=lambda a: a * a)
    (dh,) = _mm(f"{tag}_dh", da, w_in, _NT, (T // tm, 1, 4),
                pl.BlockSpec((tm, D), lambda i, j, k: (i, k)), pl.BlockSpec((None, D, D), lambda i, j, k: (k, 0, 0)),
                [(_sds((T, D), F32), pl.BlockSpec((tm, D), lambda i, j, k: (i, 0)))], (tm, D))
    (dw_in,) = _mm(f"{tag}_dwin", saved["h"], da, _TN, (D // tm, 4, T // tk),
                   pl.BlockSpec((tk, tm), lambda i, j, k: (k, i)), pl.BlockSpec((tk, D), lambda i, j, k: (k, j)),
                   [(_sds((4, D, D), F32), pl.BlockSpec((None, tm, D), lambda i, j, k: (j, i, 0)))], (tm, D))
    dx_in, d_sh, d_sc, d_g = _norm_bwd(tag, dx, dh, 0, saved["x_in"], g_norm, sc)
    return dx_in, dw_in, dw_out, dict(sh=d_sh, sc=d_sc, gate=d_gate, g_norm=d_g)


def _local_step(x, ctx, tgt, mods, cmods, norm_g, final_g, rec, conf, wg):
    n_t = T // ROW_TILE
    row = lambda v: v.reshape(1, -1)
    m0 = [row(mods[0, q]) for q in range(6)]
    m1 = [row(mods[1, q]) for q in range(6)]
    g00, g01, g10, g11 = (row(norm_g[0, 0]), row(norm_g[0, 1]), row(norm_g[1, 0]), row(norm_g[1, 1]))
    csh, csc = row(cmods[0]), row(cmods[1])
    pos = _pos_embed()

    def prep0(ids, t, v):
        cx, xv, pv = t
        is_ctx = ids[0] == 0
        xin = jnp.where(is_ctx, cx, xv + pv)
        sh = jnp.where(is_ctx, v[3], v[1])
        sc = jnp.where(is_ctx, v[4], v[2])
        return [_norm_mod(xin, v[0], sc, sh), xv + pv], []

    hcat, x0 = _tiled(
        "prep0", prep0, (N_SCAN,),
        [(ctx, pl.BlockSpec((ROW_TILE, D), lambda i: (0, 0))), _rows(x, off=-1, clamp_lo=True),
         _rows(pos, off=-1, clamp_lo=True)],
        [g00, m0[0], m0[1], csh, csc],
        [_orow(TA, D, BF16), _orow(T, D, F32, off=-1, clamp_lo=True)])

    tm_a = 768
    (a_in,) = _mm("rec_in", hcat, wg["rec_w_in"], _NN, (TA // tm_a, 4, 1),
                  pl.BlockSpec((tm_a, D), lambda i, j, k: (i, 0)),
                  pl.BlockSpec((None, D, RH), lambda i, j, k: (j, 0, 0)),
                  [(_sds((TA, 2 * R), F32), pl.BlockSpec((tm_a, RH), lambda i, j, k: (i, j)))], (tm_a, RH))
    cw = 256
    rec_starts = (0, 1)
    u = _dwconv("rec_conv", a_in, R // cw, rec["conv_w"], row(rec["conv_b"]), 1, rec_starts, R, cw)
    wbd = _dense_gates(rec["w_a"], rec["w_x"])
    gbias = _gate_bias_dense(rec["b_a"], rec["b_x"])
    lam = rec["lam"]
    a_f, b_f, a_r, b_r = _tiled("rg_fwd", _rg_fwd_fn, (TA // RG_TILE,), [_rows(u, tm=RG_TILE)], [wbd, gbias, lam],
                                [_orow(TA, R, F32, tm=RG_TILE)] * 4, vec_refs=True)
    y_f, y_r, hin_f, hin_r = _scan_fwd(a_f, b_f, a_r, b_r)

    def rec_mid(ids, t, v):
        gp, yf, yr = t
        g, _ = _gelu(gp)
        return [g * (yf + yr)], []

    (m_rec,) = _tiled("rec_mid", rec_mid, (n_t,),
                      [_rows(a_in, R, off=1), _rows(y_f, off=1), _rows(y_r, off=1)], [], [_orow(T, R, BF16)])
    tm = 512
    o_rec, x1 = _mm("rec_out", m_rec, wg["rec_w_out"], _NN, (T // tm, 1, 1),
                    pl.BlockSpec((tm, R), lambda i, j, k: (i, 0)), pl.BlockSpec((R, D), lambda i, j, k: (0, 0)),
                    [(_sds((T, D), F32), pl.BlockSpec((tm, D), lambda i, j, k: (i, 0)))] * 2, (tm, D),
                    extra=[(x0, pl.BlockSpec((tm, D), lambda i, j, k: (i, 0))), (m0[2], _full_spec(m0[2]))],
                    epi=lambda acc, ex: [acc, ex[0] + ex[1] * acc])
    mlp0, x2 = _mlp_fwd("mlp0", x1, g01, m0[3], m0[4], m0[5], wg["mlp_w_in"][0], wg["mlp_w_out"][0])

    (h1,) = _tiled("conf_norm", lambda ids, t, v: ([_norm_mod(t[0], v[0], v[1], v[2])], []), (n_t,),
                   [_rows(x2)], [g10, m1[1], m1[0]], [_orow(T, D, BF16)])
    b_pw1 = row(conf["b_pw1"])
    (pre,) = _mm("conf_pw1", h1, wg["conf_w_pw1"], _NN, (T // tm, 4, 1),
                 pl.BlockSpec((tm, D), lambda i, j, k: (i, 0)),
                 pl.BlockSpec((None, D, D // 2), lambda i, j, k: (j, 0, 0)),
                 [(_sds((T, 2 * D), F32), pl.BlockSpec((tm, D // 2), lambda i, j, k: (i, j)))], (tm, D // 2),
                 extra=[(b_pw1, pl.BlockSpec((1, D // 2), lambda i, j, k: (0, j)))],
                 epi=lambda acc, ex: [acc + ex[0]])
    (zg,) = _tiled("conf_glu", lambda ids, t, v: ([t[0] * _sigmoid(t[1])], []), (n_t,),
                   [_rows(pre, D, col=0), _rows(pre, D, col=1)], [], [_orow(T, D, F32)])
    conf_starts = (0,)
    zc = _dwconv("conf_conv", zg, 0, conf["conv_w"], row(conf["conv_b"]), CONF_KW // 2, conf_starts, D, cw)
    ln_g, ln_b = row(conf["ln_g"]), row(conf["ln_b"])

    def ln_silu(ids, t, v):
        nh, _ = _layernorm_parts(t[0])
        ln = nh * v[0] + v[1]
        return [ln * _sigmoid(ln)], []

    (s_conf,) = _tiled("conf_ln", ln_silu, (n_t,), [_rows(zc)], [ln_g, ln_b], [_orow(T, D, BF16)])
    b_pw2 = row(conf["b_pw2"])
    y_conf, x3 = _mm("conf_pw2", s_conf, wg["conf_w_pw2"], _NN, (T // tm, 1, 1),
                     pl.BlockSpec((tm, D), lambda i, j, k: (i, 0)), pl.BlockSpec((D, D), lambda i, j, k: (0, 0)),
                     [(_sds((T, D), F32), pl.BlockSpec((tm, D), lambda i, j, k: (i, 0)))] * 2, (tm, D),
                     extra=[(x2, pl.BlockSpec((tm, D), lambda i, j, k: (i, 0))), (m1[2], _full_spec(m1[2])),
                            (b_pw2, _full_spec(b_pw2))],
                     epi=lambda acc, ex: [acc + ex[2], ex[0] + ex[1] * (acc + ex[2])])
    mlp1, x4 = _mlp_fwd("mlp1", x3, g11, m1[3], m1[4], m1[5], wg["mlp_w_in"][1], wg["mlp_w_out"][1])

    fg = row(final_g)

    def head(ids, t, v):
        n, r = _rms(t[0])
        err = n * v[0] - t[1]
        d_out = err * (1.0 / D)
        dn = d_out * v[0]
        dxv = r * (dn - n * jnp.mean(dn * n, axis=-1, keepdims=True))
        part = jnp.sum(_sum0(err * err), axis=1, keepdims=True) * (0.5 / D)
        return [dxv], [part, _sum0(d_out * n)]

    dx4, loss, d_fg = _tiled("head", head, (n_t,), [_rows(x4), _rows(tgt)], [fg], [_orow(T, D, F32)],
                             [(1, 1), (1, D)])

    dx3, dw_in1, dw_out1, dm_mlp1 = _mlp_bwd("mlp1", dx4, mlp1, g11, m1[4], m1[5],
                                             wg["mlp_w_in"][1], wg["mlp_w_out"][1])
    d_y, d_g1c, d_bpw2 = _gate_bwd("conf", dx3, y_conf, m1[2])
    tk = 512
    (dw_pw2,) = _mm("conf_dwpw2", s_conf, d_y, _TN, (D // tm, 1, T // tk),
                    pl.BlockSpec((tk, tm), lambda i, j, k: (k, i)), pl.BlockSpec((tk, D), lambda i, j, k: (k, 0)),
                    [(_sds((D, D), F32), pl.BlockSpec((tm, D), lambda i, j, k: (i, 0)))], (tm, D))
    (ds,) = _mm("conf_ds", d_y, wg["conf_w_pw2"], _NT, (T // tm, 1, 1),
                pl.BlockSpec((tm, D), lambda i, j, k: (i, 0)), pl.BlockSpec((D, D), lambda i, j, k: (0, 0)),
                [(_sds((T, D), F32), pl.BlockSpec((tm, D), lambda i, j, k: (i, 0)))], (tm, D))

    def ln_silu_bwd(ids, t, v):
        dsv, zcv = t
        nh, rstd = _layernorm_parts(zcv)
        ln = nh * v[0] + v[1]
        sg = _sigmoid(ln)
        d_ln = dsv * (sg * (1.0 + ln * (1.0 - sg)))
        d_nh = d_ln * v[0]
        d_zc = rstd * (d_nh - jnp.mean(d_nh, axis=-1, keepdims=True)
                       - nh * jnp.mean(d_nh * nh, axis=-1, keepdims=True))
        return [d_zc], [_sum0(d_ln * nh), _sum0(d_ln)]

    d_zc, d_lng, d_lnb = _tiled("conf_ln_bwd", ln_silu_bwd, (n_t,), [_rows(ds), _rows(zc)], [ln_g, ln_b],
                                [_orow(T, D, F32)], [(1, D), (1, D)])
    d_zg = _dwconv("conf_conv_dx", d_zc, 0, conf["conv_w"][::-1], jnp.zeros((1, D), F32),
                   CONF_KW - 1 - CONF_KW // 2, conf_starts, D, cw)
    d_cw_conf = _dwconv_wgrad("conf_conv_dw", d_zc, zg, 0, CONF_KW, CONF_KW // 2, conf_starts, D, cw)

    def glu_bwd(ids, t, v):
        dz, pa, pb = t
        sg = _sigmoid(pb)
        d_a = dz * sg
        d_b = dz * pa * sg * (1.0 - sg)
        return [d_a, d_b], [_sum0(d_a), _sum0(d_b)]

    d_pre_a, d_pre_b, d_b1a, d_b1b = _tiled(
        "conf_glu_bwd", glu_bwd, (n_t,), [_rows(d_zg), _rows(pre, D, col=0), _rows(pre, D, col=1)], [],
        [_orow(T, D, BF16), _orow(T, D, BF16)], [(1, D), (1, D)])
    d_pre = jnp.concatenate([d_pre_a, d_pre_b], axis=1)
    (dw_pw1,) = _mm("conf_dwpw1", h1, d_pre, _TN, (D // tm, 4, T // tk),
                    pl.BlockSpec((tk, tm), lambda i, j, k: (k, i)),
                    pl.BlockSpec((tk, D // 2), lambda i, j, k: (k, j)),
                    [(_sds((4, D, D // 2), F32), pl.BlockSpec((None, tm, D // 2), lambda i, j, k: (j, i, 0)))],
                    (tm, D // 2))
    (dh1,) = _mm("conf_dh", d_pre, wg["conf_w_pw1"], _NT, (T // tm, 1, 4),
                 pl.BlockSpec((tm, D // 2), lambda i, j, k: (i, k)),
                 pl.BlockSpec((None, D, D // 2), lambda i, j, k: (k, 0, 0)),
                 [(_sds((T, D), F32), pl.BlockSpec((tm, D), lambda i, j, k: (i, 0)))], (tm, D))
    dx2, d_sh1c, d_sc1c, d_g10 = _norm_bwd("conf", dx3, dh1, 0, x2, g10, m1[1])

    dx1, dw_in0, dw_out0, dm_mlp0 = _mlp_bwd("mlp0", dx2, mlp0, g01, m0[4], m0[5],
                                             wg["mlp_w_in"][0], wg["mlp_w_out"][0])
    d_orec, d_g1r, _ = _gate_bwd("rec", dx1, o_rec, m0[2])
    (dw_rout,) = _mm("rec_dwout", m_rec, d_orec, _TN, (R // RH, 1, T // tk),
                     pl.BlockSpec((tk, RH), lambda i, j, k: (k, i)), pl.BlockSpec((tk, D), lambda i, j, k: (k, 0)),
                     [(_sds((R, D), F32), pl.BlockSpec((RH, D), lambda i, j, k: (i, 0)))], (RH, D))
    (dm_rec,) = _mm("rec_dm", d_orec, wg["rec_w_out"], _NT, (T // tm, 1, 1),
                    pl.BlockSpec((tm, D), lambda i, j, k: (i, 0)), pl.BlockSpec((R, D), lambda i, j, k: (0, 0)),
                    [(_sds((T, R), F32), pl.BlockSpec((tm, R), lambda i, j, k: (i, 0)))], (tm, R))

    def rec_mid_bwd(ids, t, v):
        dmv, gp, yf, yr = t
        g, th = _gelu(gp)
        lat = ids[0] > 0
        d_gp = jnp.where(lat, dmv * (yf + yr) * _gelu_grad(gp, th), 0.0)
        dy = jnp.where(lat, dmv * g, 0.0)
        return [d_gp, dy], []

    d_gp, dy = _tiled("rec_mid_bwd", rec_mid_bwd, (N_SCAN,),
                      [_rows(dm_rec, off=-1, clamp_lo=True), _rows(a_in, R), _rows(y_f), _rows(y_r)], [],
                      [_orow(TA, R, BF16), _orow(TA, R, F32)])
    da_f, db_f, da_r, db_r = _scan_bwd(dy, a_f, y_f, hin_f, a_r, y_r, hin_r)
    d_gpre, d_u, d_gbias, d_lam = _tiled(
        "rg_bwd", _rg_bwd_fn, (TA // RG_TILE,), [_rows(a, tm=RG_TILE) for a in (u, da_f, db_f, da_r, db_r)],
        [wbd, gbias, lam], [_orow(TA, 2 * NQ, BF16, tm=RG_TILE), _orow(TA, R, F32, tm=RG_TILE)],
        [(1, 2 * NQ), (1, 2 * R)], vec_refs=True)
    tk_a = 768
    (d_wbd,) = _mm("rg_dw", u, d_gpre, _TN, (2, 2, TA // tk_a),
                   pl.BlockSpec((tk_a, RH), lambda i, j, k: (k, i)),
                   pl.BlockSpec((tk_a, NQ // 2), lambda i, j, k: (k, 2 * i + j)),
                   [(_sds((2, RH, NQ), F32), pl.BlockSpec((None, RH, NQ // 2), lambda i, j, k: (i, 0, j)))],
                   (RH, NQ // 2))
    d_p = _dwconv("rec_conv_dx", d_u, 0, rec["conv_w"][::-1], jnp.zeros((1, R), F32), REC_KW - 1 - 1,
                  rec_starts, R, cw)
    d_cw_rec = _dwconv_wgrad("rec_conv_dw", d_u, a_in, R // cw, REC_KW, 1, rec_starts, R, cw)
    d_a = jnp.concatenate([d_gp, d_p.astype(BF16)], axis=1)
    (dw_rin,) = _mm("rec_dwin", hcat, d_a, _TN, (D // tm, 4, TA // tk_a),
                    pl.BlockSpec((tk_a, tm), lambda i, j, k: (k, i)), pl.BlockSpec((tk_a, RH), lambda i, j, k: (k, j)),
                    [(_sds((4, D, RH), F32), pl.BlockSpec((None, tm, RH), lambda i, j, k: (j, i, 0)))], (tm, RH))
    (dhcat,) = _mm("rec_dh", d_a, wg["rec_w_in"], _NT, (TA // tm_a, 1, 4),
                   pl.BlockSpec((tm_a, RH), lambda i, j, k: (i, k)),
                   pl.BlockSpec((None, D, RH), lambda i, j, k: (k, 0, 0)),
                   [(_sds((TA, D), F32), pl.BlockSpec((tm_a, D), lambda i, j, k: (i, 0)))], (tm_a, D))
    dx0, d_sh1r, d_sc1r, d_g00 = _norm_bwd("rec", dx1, dhcat, 1, x0, g00, m0[1])
    d_csh, d_csc, d_g00c = _norm_bwd("ctx", None, dhcat, 0, ctx, g00, csc, with_dx=False)

    big = dict(rec_w_in=dw_rin, rec_w_out=dw_rout, conf_w_pw1=dw_pw1, conf_w_pw2=dw_pw2,
               mlp_w_in=(dw_in0, dw_in1), mlp_w_out=(dw_out0, dw_out1))
    d_wa, d_wx = _gate_block_grads(d_wbd)
    d_ba, d_bx = _gate_bias_grads(d_gbias)
    d_mod = jnp.concatenate([
        d_sh1r, d_sc1r, d_g1r, dm_mlp0["sh"], dm_mlp0["sc"], dm_mlp0["gate"],
        d_sh1c, d_sc1c, d_g1c, dm_mlp1["sh"], dm_mlp1["sc"], dm_mlp1["gate"]], axis=1).reshape(2, 6 * D)
    small = dict(
        d_mod=d_mod, d_cmod=jnp.concatenate([d_csh, d_csc], axis=1),
        norm_g=jnp.concatenate([d_g00 + d_g00c, dm_mlp0["g_norm"], d_g10, dm_mlp1["g_norm"]], axis=1),
        rec_conv_w=d_cw_rec[:REC_KW], rec_conv_b=d_cw_rec[REC_KW], rec_lambda=d_lam.reshape(2, R),
        rec_w_a=d_wa, rec_b_a=d_ba, rec_w_x=d_wx, rec_b_x=d_bx,
        conf_b_pw1=jnp.concatenate([d_b1a, d_b1b], axis=1), conf_conv_w=d_cw_conf[:CONF_KW],
        conf_conv_b=d_cw_conf[CONF_KW], conf_ln_g=d_lng, conf_ln_b=d_lnb, conf_b_pw2=d_bpw2, final_g=d_fg)
    return loss.reshape(()), dx0, big, small


_BIG = ("rec_w_in", "rec_w_out", "conf_w_pw1", "conf_w_pw2", "mlp_w_in", "mlp_w_out")


def _halves(w):
    return w.reshape(2, w.shape[0] // 2, w.shape[1])


def _ada_fwd(c16, w_ada, b_shard):
    ns = w_ada.shape[2]
    tn = 512

    def kern(c_ref, w_ref, b_ref, o_ref):
        cv = c_ref[...]
        s = (cv * _sigmoid(cv)).astype(BF16)
        o_ref[...] = jnp.dot(s, w_ref[...].astype(BF16), preferred_element_type=F32) + b_ref[...]

    return _pcall(
        kern, name="ada_fwd", grid=(2, ns // tn),
        in_specs=[pl.BlockSpec((16, D), lambda l, j: (0, 0)), pl.BlockSpec((None, D, tn), lambda l, j: (l, 0, j)),
                  pl.BlockSpec((None, 1, tn), lambda l, j: (l, 0, j))],
        out_specs=pl.BlockSpec((None, 16, tn), lambda l, j: (l, 0, j)),
        out_shape=_sds((2, 16, ns), F32), compiler_params=_cparams(),
    )(c16, w_ada, b_shard)


def _ada_bwd(c16, dm16, w_ada):
    ns = w_ada.shape[2]
    tn = 512

    def kern(c_ref, dm_ref, w_ref, gw_ref, ds_ref):
        cv = c_ref[...]
        s = (cv * _sigmoid(cv)).astype(BF16)
        dm = dm_ref[...].astype(BF16)
        gw_ref[...] = lax.dot_general(s, dm, _TN, preferred_element_type=F32)

        @pl.when(jnp.logical_and(pl.program_id(0) == 0, pl.program_id(1) == 0))
        def _():
            ds_ref[...] = jnp.zeros_like(ds_ref)

        ds_ref[...] += lax.dot_general(dm, w_ref[...].astype(BF16), _NT, preferred_element_type=F32)

    return _pcall(
        kern, name="ada_bwd", grid=(2, ns // tn),
        in_specs=[pl.BlockSpec((16, D), lambda l, j: (0, 0)), pl.BlockSpec((None, 16, tn), lambda l, j: (l, 0, j)),
                  pl.BlockSpec((None, D, tn), lambda l, j: (l, 0, j))],
        out_specs=[pl.BlockSpec((None, D, tn), lambda l, j: (l, 0, j)), pl.BlockSpec((16, D), lambda l, j: (0, 0))],
        out_shape=[_sds((2, D, ns), F32), _sds((16, D), F32)], compiler_params=_cparams(),
    )(c16, dm16, w_ada)


def _cctx_grad(ds8, c_ctx):
    def kern(d_ref, c_ref, o_ref):
        tot = d_ref[0, 8:9, :] + d_ref[2, 8:9, :] + d_ref[4, 8:9, :] + d_ref[6, 8:9, :]
        cv = c_ref[...]
        sg = _sigmoid(cv)
        o_ref[...] = tot * (sg * (1.0 + cv * (1.0 - sg)))

    return _pcall(kern, name="cctx_grad", out_shape=_sds((1, D), F32))(ds8, c_ctx.reshape(1, D))


def kernel(x, c, ctx, c_ctx, w_ada, b_ada, norm_g, rec_w_in, rec_conv_w, rec_conv_b, rec_lambda, rec_w_a, rec_b_a, rec_w_x, rec_b_x, rec_w_out, conf_w_pw1, conf_b_pw1, conf_conv_w, conf_conv_b, conf_ln_g, conf_ln_b, conf_w_pw2, conf_b_pw2, mlp_w_in, mlp_w_out, final_g, loss_target, m_c_ctx, m_w_ada, m_b_ada, m_norm_g, m_rec_w_in, m_rec_conv_w, m_rec_conv_b, m_rec_lambda, m_rec_w_a, m_rec_b_a, m_rec_w_x, m_rec_b_x, m_rec_w_out, m_conf_w_pw1, m_conf_b_pw1, m_conf_conv_w, m_conf_conv_b, m_conf_ln_g, m_conf_ln_b, m_conf_w_pw2, m_conf_b_pw2, m_mlp_w_in, m_mlp_w_out, m_final_g, v_c_ctx, v_w_ada, v_b_ada, v_norm_g, v_rec_w_in, v_rec_conv_w, v_rec_conv_b, v_rec_lambda, v_rec_w_a, v_rec_b_a, v_rec_w_x, v_rec_b_x, v_rec_w_out, v_conf_w_pw1, v_conf_b_pw1, v_conf_conv_w, v_conf_conv_b, v_conf_ln_g, v_conf_ln_b, v_conf_w_pw2, v_conf_b_pw2, v_mlp_w_in, v_mlp_w_out, v_final_g):
    names = ["c_ctx", "w_ada", "b_ada", "norm_g", "rec_w_in", "rec_conv_w", "rec_conv_b", "rec_lambda", "rec_w_a",
             "rec_b_a", "rec_w_x", "rec_b_x", "rec_w_out", "conf_w_pw1", "conf_b_pw1", "conf_conv_w", "conf_conv_b",
             "conf_ln_g", "conf_ln_b", "conf_w_pw2", "conf_b_pw2", "mlp_w_in", "mlp_w_out", "final_g"]
    w = dict(zip(names, [c_ctx, w_ada, b_ada, norm_g, rec_w_in, rec_conv_w, rec_conv_b, rec_lambda, rec_w_a,
                         rec_b_a, rec_w_x, rec_b_x, rec_w_out, conf_w_pw1, conf_b_pw1, conf_conv_w, conf_conv_b,
                         conf_ln_g, conf_ln_b, conf_w_pw2, conf_b_pw2, mlp_w_in, mlp_w_out, final_g]))
    m = dict(zip(names, [m_c_ctx, m_w_ada, m_b_ada, m_norm_g, m_rec_w_in, m_rec_conv_w, m_rec_conv_b, m_rec_lambda,
                         m_rec_w_a, m_rec_b_a, m_rec_w_x, m_rec_b_x, m_rec_w_out, m_conf_w_pw1, m_conf_b_pw1,
                         m_conf_conv_w, m_conf_conv_b, m_conf_ln_g, m_conf_ln_b, m_conf_w_pw2, m_conf_b_pw2,
                         m_mlp_w_in, m_mlp_w_out, m_final_g]))
    v = dict(zip(names, [v_c_ctx, v_w_ada, v_b_ada, v_norm_g, v_rec_w_in, v_rec_conv_w, v_rec_conv_b, v_rec_lambda,
                         v_rec_w_a, v_rec_b_a, v_rec_w_x, v_rec_b_x, v_rec_w_out, v_conf_w_pw1, v_conf_b_pw1,
                         v_conf_conv_w, v_conf_conv_b, v_conf_ln_g, v_conf_ln_b, v_conf_w_pw2, v_conf_b_pw2,
                         v_mlp_w_in, v_mlp_w_out, v_final_g]))
    mx, my, mc = _me()
    chip = 2 * mx + my
    me = 4 * mx + 2 * my + mc

    sharded_small = ["norm_g", "rec_conv_w", "rec_lambda", "conf_b_pw1", "conf_conv_w", "conf_conv_b", "conf_ln_g",
                     "conf_ln_b", "conf_b_pw2"]
    packed, offs = _pack([c] + [w[k] for k in sharded_small])
    got = _allgather8("gather_small", packed)
    per_dev = [_unpack(got[d], offs) for d in range(8)]
    c_rows = jnp.concatenate([per_dev[d][0].reshape(1, D) for d in range(8)], axis=0)
    full = {k: jnp.concatenate([per_dev[2 * j][1 + i] for j in range(4)], axis=-1)
            for i, k in enumerate(sharded_small)}
    c16 = jnp.concatenate([c_rows, c_ctx.reshape(1, D), jnp.zeros((7, D), F32)], axis=0)

    ns = w_ada.shape[2]
    b_shard = lax.dynamic_slice_in_dim(b_ada, chip * ns, ns, axis=1).reshape(2, 1, ns)
    prod = _ada_fwd(c16, w_ada, b_shard)
    prod8 = _allgather8("gather_mod", prod.reshape(32, ns)).reshape(8, 2, 16, ns)
    mod_all = jnp.concatenate([prod8[2 * j] for j in range(4)], axis=-1)
    mods = lax.dynamic_index_in_dim(mod_all, me, axis=1, keepdims=False).reshape(2, 6, D)
    cmods = mod_all[0, 8].reshape(6, D)[:2]

    shards = [_halves(rec_w_in[0].astype(BF16)), _halves(rec_w_out[0].astype(BF16)),
              _halves(conf_w_pw1[0].astype(BF16)), _halves(conf_w_pw2[0].astype(BF16)),
              _halves(mlp_w_in[0].astype(BF16)), _halves(mlp_w_in[1].astype(BF16)),
              _halves(mlp_w_out[0].astype(BF16)), _halves(mlp_w_out[1].astype(BF16))]
    gw = _gather_weights(shards)
    wg = dict(rec_w_in=gw[0].reshape(4, D, RH), rec_w_out=gw[1].reshape(R, D),
              conf_w_pw1=gw[2].reshape(4, D, D // 2), conf_w_pw2=gw[3].reshape(D, D),
              mlp_w_in=(gw[4].reshape(4, D, D), gw[5].reshape(4, D, D)),
              mlp_w_out=(gw[6].reshape(FF, D), gw[7].reshape(FF, D)))

    rec = dict(conv_w=full["rec_conv_w"][0], conv_b=rec_conv_b[0], lam=full["rec_lambda"][0],
               w_a=rec_w_a[0], b_a=rec_b_a[0], w_x=rec_w_x[0], b_x=rec_b_x[0])
    conf = dict(b_pw1=full["conf_b_pw1"][0], conv_w=full["conf_conv_w"][0], conv_b=full["conf_conv_b"][0],
                ln_g=full["conf_ln_g"][0], ln_b=full["conf_ln_b"][0], b_pw2=full["conf_b_pw2"][0])
    loss_local, grad_x, big, small = _local_step(x[0], ctx[0], loss_target[0], mods, cmods, full["norm_g"], final_g,
                                                 rec, conf, wg)
    loss = lax.psum(loss_local, ("x", "y", "c"))

    parts = [big["rec_w_in"], big["rec_w_out"], big["conf_w_pw1"], big["conf_w_pw2"],
             big["mlp_w_in"][0], big["mlp_w_in"][1], big["mlp_w_out"][0], big["mlp_w_out"][1]]
    parts = [p.reshape(4, 2, s.shape[1], s.shape[2]) for p, s in zip(parts, shards)]
    paired = _reduce_pair(parts)
    sums = []
    for t, p in enumerate(paired):
        rr, cc = p.shape[2], p.shape[3]
        tr = rr if rr * cc * 4 <= 2 * 1024 * 1024 else rr // 2
        spec = lambda h: pl.BlockSpec((None, None, tr, cc), lambda j, i, _h=h: (j, _h, i, 0))
        (s_t,) = _tiled(f"pair_add{t}", lambda ids, tin, vin: ([tin[0] + tin[1]], []), (4, rr // tr),
                        [(p, spec(0)), (p, spec(1))], [],
                        [(_sds((4, rr, cc), BF16), pl.BlockSpec((None, tr, cc), lambda j, i: (j, i, 0)))])
        sums.append(s_t)
    landed = _reduce_chips(sums)
    reds = []
    for t, q in enumerate(landed):
        rr, cc = q.shape[1], q.shape[2]
        tr = rr if rr * cc * 4 <= 2 * 1024 * 1024 else rr // 2
        ins = [(q, pl.BlockSpec((None, tr, cc), lambda i, _j=j: (_j, i, 0))) for j in range(4)]

        def add4(ids, tin, vin):
            return [((tin[0].astype(F32) + tin[1].astype(F32)) + tin[2].astype(F32)) + tin[3].astype(F32)], []

        (r_t,) = _tiled(f"chip_add{t}", add4, (rr // tr,), ins, [],
                        [(_sds((rr, cc), F32), pl.BlockSpec((tr, cc), lambda i: (i, 0)))])
        reds.append(r_t)
    whole = _share_halves(reds)
    g_big = dict(rec_w_in=whole[0].reshape(rec_w_in.shape), rec_w_out=whole[1].reshape(rec_w_out.shape),
                 conf_w_pw1=whole[2].reshape(conf_w_pw1.shape), conf_w_pw2=whole[3].reshape(conf_w_pw2.shape),
                 mlp_w_in=jnp.stack([whole[4].reshape(D, D), whole[5].reshape(D, D)]),
                 mlp_w_out=jnp.stack([whole[6].reshape(D, D), whole[7].reshape(D, D)]))

    small_names = ["d_mod", "d_cmod", "norm_g", "rec_conv_w", "rec_conv_b", "rec_lambda", "rec_w_a", "rec_b_a",
                   "rec_w_x", "rec_b_x", "conf_b_pw1", "conf_conv_w", "conf_conv_b", "conf_ln_g", "conf_ln_b",
                   "conf_b_pw2", "final_g"]
    spacked, soffs = _pack([small[k] for k in small_names])
    s8 = _allgather8("gather_grads", spacked)
    ssum = dict(zip(small_names, _unpack(_sum_devices("sum_grads", s8), soffs)))
    dmod_rows = jnp.stack([_unpack(s8[d], soffs)[0] for d in range(8)], axis=1)
    d_cmod_full = jnp.concatenate([ssum["d_cmod"].reshape(1, 2 * D), jnp.zeros((1, 4 * D), F32)], axis=1)
    dm16 = jnp.concatenate([dmod_rows, jnp.stack([d_cmod_full, jnp.zeros((1, 6 * D), F32)]),
                            jnp.zeros((2, 7, 6 * D), F32)], axis=1)
    dm16_shard = lax.dynamic_slice_in_dim(dm16, chip * ns, ns, axis=2)
    g_w_ada, ds_part = _ada_bwd(c16, dm16_shard, w_ada)
    ds8 = _allgather8("gather_dsilu", ds_part)
    g_c_ctx = _cctx_grad(ds8, c_ctx).reshape(D)
    g_b_ada = ssum["d_mod"] + jnp.stack([d_cmod_full[0], jnp.zeros((6 * D,), F32)])

    def shard_of(a, axis):
        n = a.shape[axis] // 4
        return lax.dynamic_slice_in_dim(a, chip * n, n, axis=axis)

    grads = dict(
        c_ctx=g_c_ctx, w_ada=g_w_ada, b_ada=g_b_ada,
        norm_g=shard_of(ssum["norm_g"].reshape(2, 2, D), 2),
        rec_w_in=g_big["rec_w_in"], rec_conv_w=shard_of(ssum["rec_conv_w"].reshape(1, REC_KW, R), 2),
        rec_conv_b=ssum["rec_conv_b"].reshape(1, R), rec_lambda=shard_of(ssum["rec_lambda"].reshape(1, 2, R), 2),
        rec_w_a=ssum["rec_w_a"].reshape(rec_w_a.shape), rec_b_a=ssum["rec_b_a"].reshape(rec_b_a.shape),
        rec_w_x=ssum["rec_w_x"].reshape(rec_w_x.shape), rec_b_x=ssum["rec_b_x"].reshape(rec_b_x.shape),
        rec_w_out=g_big["rec_w_out"], conf_w_pw1=g_big["conf_w_pw1"],
        conf_b_pw1=shard_of(ssum["conf_b_pw1"].reshape(1, 2 * D), 1),
        conf_conv_w=shard_of(ssum["conf_conv_w"].reshape(1, CONF_KW, D), 2),
        conf_conv_b=shard_of(ssum["conf_conv_b"].reshape(1, D), 1),
        conf_ln_g=shard_of(ssum["conf_ln_g"].reshape(1, D), 1), conf_ln_b=shard_of(ssum["conf_ln_b"].reshape(1, D), 1),
        conf_w_pw2=g_big["conf_w_pw2"], conf_b_pw2=shard_of(ssum["conf_b_pw2"].reshape(1, D), 1),
        mlp_w_in=g_big["mlp_w_in"], mlp_w_out=g_big["mlp_w_out"], final_g=ssum["final_g"].reshape(D))

    delta, new_m, new_v = {}, {}, {}
    big_names = ("w_ada",) + _BIG
    for k in big_names:
        cols = w[k].shape[-1]
        d_, m_, v_ = _adamw(f"adamw_{k}", w[k].reshape(-1, cols), grads[k].reshape(-1, cols),
                            m[k].reshape(-1, cols), v[k].reshape(-1, cols))
        delta[k], new_m[k], new_v[k] = (a.reshape(w[k].shape) for a in (d_, m_, v_))
    rest = [k for k in names if k not in big_names]
    pw, poffs = _pack([w[k] for k in rest])
    pg, _ = _pack([grads[k] for k in rest])
    pm, _ = _pack([m[k] for k in rest])
    pv, _ = _pack([v[k] for k in rest])
    d_, m_, v_ = _adamw("adamw_small", pw, pg, pm, pv)
    for k, dd, mm, vv in zip(rest, _unpack(d_, poffs), _unpack(m_, poffs), _unpack(v_, poffs)):
        delta[k], new_m[k], new_v[k] = dd, mm, vv

    return (loss, grad_x[None], *[grads[k] for k in names], *[delta[k] for k in names],
            *[new_m[k] for k in names], *[new_v[k] for k in names])
```

```python
import functools
import math

import jax
import jax.numpy as jnp
from jax import lax
from jax.experimental import pallas as pl
from jax.experimental.pallas import tpu as pltpu

F32 = jnp.float32
BF16 = jnp.bfloat16

D = 1024
T = 2048
TC = 256
TA = T + TC
R = 1280
RH = R // 2
NQ = 4 * RH
FF = 4096
N_BLK = 16
BLK = R // N_BLK
GRID_W = 64
EPS = 1e-6
RG_C = 8.0
CONF_KW = 31
REC_KW = 4
LANE = 128
ROW_TILE = 256
HALO = 16
RG_TILE = 128
PACK_ROWS = 512
V7X_VMEM_BYTES = 64 * 1024 * 1024
VMEM_LIMIT = V7X_VMEM_BYTES - 8 * 1024 * 1024

ADAM_LR = 0.001
ADAM_B1 = 0.9
ADAM_B2 = 0.999
ADAM_EPS = 1e-08
ADAM_WD = 0.01
ADAM_STEP = 10

MESH = pl.DeviceIdType.MESH
ANY = pl.BlockSpec(memory_space=pl.ANY)


def _sds(shape, dtype):
    return jax.ShapeDtypeStruct(tuple(shape), dtype)


def _pcall(body, **kw):
    return pl.pallas_call(body, **kw)


def _cparams():
    return pltpu.CompilerParams(vmem_limit_bytes=VMEM_LIMIT)


def _full_spec(arr):
    nd = arr.ndim
    return pl.BlockSpec(arr.shape, lambda *ids, _n=nd: (0,) * _n)


def _sum0(v):
    return jnp.sum(v, axis=0, keepdims=True)


def _tiled(name, fn, grid, ins, vecs, outs, vec_outs=(), vec_refs=False):
    n_in, n_vec, n_out = len(ins), len(vecs), len(outs)
    n_grid = len(grid)

    def kern(*refs):
        ids = [pl.program_id(a) for a in range(n_grid)]
        tin = [r[...] for r in refs[:n_in]]
        vin = list(refs[n_in:n_in + n_vec]) if vec_refs else [r[...] for r in refs[n_in:n_in + n_vec]]
        o_refs = refs[n_in + n_vec:n_in + n_vec + n_out]
        a_refs = refs[n_in + n_vec + n_out:]
        tout, incs = fn(ids, tin, vin)
        for r, v in zip(o_refs, tout):
            r[...] = v.astype(r.dtype)
        if a_refs:
            first = functools.reduce(jnp.logical_and, [i == 0 for i in ids])

            @pl.when(first)
            def _():
                for r in a_refs:
                    r[...] = jnp.zeros_like(r)

            for r, v in zip(a_refs, incs):
                r[...] += v

    out_shape = [o for o, _ in outs] + [_sds(s, F32) for s in vec_outs]
    out_specs = [s for _, s in outs] + [
        pl.BlockSpec(tuple(s), lambda *ids, _n=len(s): (0,) * _n) for s in vec_outs]
    res = _pcall(
        kern, name=name, grid=tuple(grid),
        in_specs=[s for _, s in ins] + [_full_spec(v) for v in vecs],
        out_specs=out_specs, out_shape=out_shape, compiler_params=_cparams(),
    )(*[a for a, _ in ins], *vecs)
    return list(res)


def _rows(arr, ncols=None, tm=ROW_TILE, off=0, col=0, clamp_lo=False):
    ncols = arr.shape[1] if ncols is None else ncols
    if clamp_lo:
        return arr, pl.BlockSpec((tm, ncols), lambda i: (jnp.maximum(i + off, 0), col))
    return arr, pl.BlockSpec((tm, ncols), lambda i: (i + off, col))


def _orow(nrows, ncols, dtype, tm=ROW_TILE, off=0, clamp_lo=False):
    if clamp_lo:
        return _sds((nrows, ncols), dtype), pl.BlockSpec((tm, ncols), lambda i: (jnp.maximum(i + off, 0), 0))
    return _sds((nrows, ncols), dtype), pl.BlockSpec((tm, ncols), lambda i: (i + off, 0))


_NN = (((1,), (0,)), ((), ()))
_TN = (((0,), (0,)), ((), ()))
_NT = (((1,), (1,)), ((), ()))


def _mm(name, a, b, dims, grid, a_spec, b_spec, out, acc_shape, extra=(), a_pre=None, epi=None):
    n_k = grid[2]
    n_ex = len(extra)

    def kern(a_ref, b_ref, *rest):
        ex = rest[:n_ex]
        o_refs = rest[n_ex:-1]
        acc = rest[-1]
        k = pl.program_id(2)

        @pl.when(k == 0)
        def _():
            acc[...] = jnp.zeros_like(acc)

        av = a_ref[...]
        if a_pre is not None:
            av = a_pre(av)
        acc[...] += lax.dot_general(av.astype(BF16), b_ref[...].astype(BF16), dims,
                                    preferred_element_type=F32)

        @pl.when(k == n_k - 1)
        def _():
            vals = [acc[...]] if epi is None else epi(acc[...], [e[...] for e in ex])
            for r, v in zip(o_refs, vals):
                r[...] = v.astype(r.dtype)

    res = _pcall(
        kern, name=name, grid=tuple(grid),
        in_specs=[a_spec, b_spec] + [s for _, s in extra],
        out_specs=[s for _, s in out], out_shape=[o for o, _ in out],
        scratch_shapes=[pltpu.VMEM(tuple(acc_shape), F32)], compiler_params=_cparams(),
    )(a, b, *[e for e, _ in extra])
    return list(res)


def _rms(x):
    r = lax.rsqrt(jnp.mean(x * x, axis=-1, keepdims=True) + EPS)
    return x * r, r


def _norm_mod(x, g, sc, sh):
    n, _ = _rms(x)
    return (n * g) * (1.0 + sc) + sh


def _norm_mod_bwd(dh, x, g, sc):
    n, r = _rms(x)
    d_sh = _sum0(dh)
    d_sc = _sum0(dh * (n * g))
    d_g = _sum0(dh * (1.0 + sc) * n)
    dn = dh * (g * (1.0 + sc))
    dx = r * (dn - n * jnp.mean(dn * n, axis=-1, keepdims=True))
    return dx, d_sh, d_sc, d_g


_GELU_K = math.sqrt(2.0 / math.pi)


def _gelu(x):
    t = jnp.tanh(_GELU_K * (x + 0.044715 * x * x * x))
    return 0.5 * x * (1.0 + t), t


def _gelu_grad(x, t):
    return 0.5 * (1.0 + t) + 0.5 * x * (1.0 - t * t) * (_GELU_K * (1.0 + 3.0 * 0.044715 * x * x))


def _sigmoid(x):
    return 1.0 / (1.0 + jnp.exp(-x))


def _expm1(x):
    p = 1.0 + x * (1.0 / 9.0)
    for n in (8.0, 7.0, 6.0, 5.0, 4.0, 3.0, 2.0):
        p = 1.0 + (x * (1.0 / n)) * p
    return jnp.where(jnp.abs(x) < 0.5, x * p, jnp.exp(x) - 1.0)


def _softplus_neg(lam):
    return jnp.log1p(jnp.exp(-jnp.abs(lam))) + jnp.maximum(-lam, 0.0)


def _layernorm_parts(x):
    mu = jnp.mean(x, axis=-1, keepdims=True)
    xc = x - mu
    rstd = lax.rsqrt(jnp.mean(xc * xc, axis=-1, keepdims=True) + EPS)
    return xc * rstd, rstd


def _rg_gates(u, wbd, gbias, lam):
    sp = _softplus_neg(lam)
    parts = {}
    for h in range(2):
        uh = u[:, h * RH:(h + 1) * RH]
        g = jnp.dot(uh.astype(BF16), wbd[h], preferred_element_type=F32) + gbias[:, h * NQ:(h + 1) * NQ]
        for d in range(2):
            r = _sigmoid(g[:, (2 * d) * RH:(2 * d + 1) * RH])
            i = _sigmoid(g[:, (2 * d + 1) * RH:(2 * d + 2) * RH])
            sph = sp[d:d + 1, h * RH:(h + 1) * RH]
            la = (-RG_C) * r * sph
            e2 = _expm1(2.0 * la)
            parts[(d, h)] = dict(r=r, i=i, la=la, a=jnp.exp(la), e2=e2, mult=jnp.sqrt(-e2), uh=uh, sp=sph)
    return parts


def _rg_fwd_fn(ids, tin, vin):
    (u,) = tin
    wbd = vin[0]
    parts = _rg_gates(u, wbd, vin[1][...], vin[2][...])
    outs = []
    for d in range(2):
        a = jnp.concatenate([parts[(d, h)]["a"] for h in range(2)], axis=1)
        b = jnp.concatenate([parts[(d, h)]["mult"] * parts[(d, h)]["i"] * parts[(d, h)]["uh"]
                             for h in range(2)], axis=1)
        outs += [a, b]
    return outs, []


def _rg_bwd_fn(ids, tin, vin):
    u, da_f, db_f, da_r, db_r = tin
    wbd, lam = vin[0], vin[2][...]
    parts = _rg_gates(u, wbd, vin[1][...], lam)
    dab = ((da_f, db_f), (da_r, db_r))
    dsig_lam = -_sigmoid(-lam)
    du_halves, dpre_halves, dlam = [], [], [[None, None], [None, None]]
    for h in range(2):
        du = jnp.zeros_like(parts[(0, h)]["uh"])
        dpre = []
        for d in range(2):
            p = parts[(d, h)]
            da = dab[d][0][:, h * RH:(h + 1) * RH]
            db = dab[d][1][:, h * RH:(h + 1) * RH]
            d_mult = db * p["i"] * p["uh"]
            d_i = db * p["mult"] * p["uh"]
            du = du + db * p["mult"] * p["i"]
            d_la = da * p["a"] - d_mult * (p["e2"] + 1.0) / p["mult"]
            d_r = d_la * ((-RG_C) * p["sp"])
            dlam[d][h] = _sum0(d_la * ((-RG_C) * p["r"])) * dsig_lam[d:d + 1, h * RH:(h + 1) * RH]
            dpre += [d_r * p["r"] * (1.0 - p["r"]), d_i * p["i"] * (1.0 - p["i"])]
        dpre = jnp.concatenate(dpre, axis=1)
        du = du + lax.dot_general(dpre.astype(BF16), wbd[h], _NT, preferred_element_type=F32)
        du_halves.append(du)
        dpre_halves.append(dpre)
    dpre_all = jnp.concatenate(dpre_halves, axis=1)
    dlam_row = jnp.concatenate([dlam[0][0], dlam[0][1], dlam[1][0], dlam[1][1]], axis=1)
    return [dpre_all, jnp.concatenate(du_halves, axis=1)], [_sum0(dpre_all), dlam_row]


def _tile_flags(i, n_tiles, seq_starts):
    starts_here = functools.reduce(jnp.logical_or, [i == s for s in seq_starts])
    ends_here = functools.reduce(jnp.logical_or, [i + 1 == s for s in seq_starts] + [i + 1 == n_tiles])
    return jnp.logical_not(starts_here), jnp.logical_not(ends_here)


def _halo_specs(col0, cw):
    hb = ROW_TILE // HALO
    prev = pl.BlockSpec((HALO, cw), lambda i, c: (jnp.maximum(i * hb - 1, 0), col0 + c))
    cur = pl.BlockSpec((ROW_TILE, cw), lambda i, c: (i, col0 + c))
    return prev, cur, hb


def _window(prev_ref, cur_ref, next_ref, has_prev, has_next):
    prev = jnp.where(has_prev, prev_ref[...], 0.0)
    nxt = jnp.where(has_next, next_ref[...], 0.0)
    return jnp.concatenate([prev, cur_ref[...], nxt], axis=0)


def _dwconv(name, x, col0, w, bias, pad_left, seq_starts, n_ch, cw=256):
    n_rows = x.shape[0]
    n_tiles = n_rows // ROW_TILE
    n_taps = w.shape[0]
    prev_spec, cur_spec, hb = _halo_specs(col0, cw)
    last_hb = n_rows // HALO - 1
    next_spec = pl.BlockSpec((HALO, cw), lambda i, c: (jnp.minimum((i + 1) * hb, last_hb), col0 + c))

    def kern(prev_ref, cur_ref, next_ref, w_ref, b_ref, o_ref):
        has_prev, has_next = _tile_flags(pl.program_id(0), n_tiles, seq_starts)
        win = _window(prev_ref, cur_ref, next_ref, has_prev, has_next)
        wv = w_ref[...]
        acc = jnp.zeros((ROW_TILE, cw), F32) + b_ref[...]
        for k in range(n_taps):
            off = HALO + k - pad_left
            acc = acc + wv[k:k + 1, :] * win[off:off + ROW_TILE, :]
        o_ref[...] = acc

    return _pcall(
        kern, name=name, grid=(n_tiles, n_ch // cw),
        in_specs=[prev_spec, cur_spec, next_spec,
                  pl.BlockSpec((n_taps, cw), lambda i, c: (0, c)), pl.BlockSpec((1, cw), lambda i, c: (0, c))],
        out_specs=pl.BlockSpec((ROW_TILE, cw), lambda i, c: (i, c)),
        out_shape=_sds((n_rows, n_ch), F32), compiler_params=_cparams(),
    )(x, x, x, w, bias)


def _dwconv_wgrad(name, dy, x, col0, n_taps, pad_left, seq_starts, n_ch, cw=256):
    n_rows = dy.shape[0]
    n_tiles = n_rows // ROW_TILE
    n_out = -(-(n_taps + 1) // 8) * 8
    prev_spec, cur_spec, hb = _halo_specs(col0, cw)
    last_hb = n_rows // HALO - 1
    next_spec = pl.BlockSpec((HALO, cw), lambda c, i: (jnp.minimum((i + 1) * hb, last_hb), col0 + c))
    prev_spec = pl.BlockSpec((HALO, cw), lambda c, i: (jnp.maximum(i * hb - 1, 0), col0 + c))
    cur_spec = pl.BlockSpec((ROW_TILE, cw), lambda c, i: (i, col0 + c))

    def kern(dy_ref, prev_ref, cur_ref, next_ref, o_ref):
        i = pl.program_id(1)
        has_prev, has_next = _tile_flags(i, n_tiles, seq_starts)
        win = _window(prev_ref, cur_ref, next_ref, has_prev, has_next)
        dyv = dy_ref[...]
        rid = lax.broadcasted_iota(jnp.int32, (n_out, cw), 0)
        inc = jnp.where(rid == n_taps, _sum0(dyv), 0.0)
        for k in range(n_taps):
            off = HALO + k - pad_left
            inc = inc + jnp.where(rid == k, _sum0(dyv * win[off:off + ROW_TILE, :]), 0.0)

        @pl.when(i == 0)
        def _():
            o_ref[...] = jnp.zeros_like(o_ref)

        o_ref[...] += inc

    return _pcall(
        kern, name=name, grid=(n_ch // cw, n_tiles),
        in_specs=[pl.BlockSpec((ROW_TILE, cw), lambda c, i: (i, c)), prev_spec, cur_spec, next_spec],
        out_specs=pl.BlockSpec((n_out, cw), lambda c, i: (0, c)),
        out_shape=_sds((n_out, n_ch), F32), compiler_params=_cparams(),
    )(dy, x, x, x)


N_SCAN = TA // ROW_TILE


def _rev_block(j):
    return jnp.where(j == 0, 0, N_SCAN - j)


def _scan_fwd(a_f, b_f, a_r, b_r):
    fwd_spec = pl.BlockSpec((ROW_TILE, R), lambda i: (i, 0))
    rev_spec = pl.BlockSpec((ROW_TILE, R), lambda i: (_rev_block(i), 0))
    hin_spec = pl.BlockSpec((None, 1, R), lambda i: (i, 0, 0))

    def kern(af, bf, ar, br, yf, yr, hin_f, hin_r, hf_s, hr_s):
        @pl.when(pl.program_id(0) == 0)
        def _():
            hf_s[...] = jnp.zeros_like(hf_s)
            hr_s[...] = jnp.zeros_like(hr_s)

        hin_f[...] = hf_s[...]
        hin_r[...] = hr_s[...]

        def step(s8, carry):
            hf, hr = carry
            t0 = pl.multiple_of(s8 * 8, 8)
            for q in range(8):
                tf = t0 + q
                hf = af[pl.ds(tf, 1), :] * hf + bf[pl.ds(tf, 1), :]
                yf[pl.ds(tf, 1), :] = hf
                tr = ROW_TILE - 1 - tf
                hr = ar[pl.ds(tr, 1), :] * hr + br[pl.ds(tr, 1), :]
                yr[pl.ds(tr, 1), :] = hr
            return hf, hr

        hf, hr = lax.fori_loop(0, ROW_TILE // 8, step, (hf_s[...], hr_s[...]))
        hf_s[...] = hf
        hr_s[...] = hr

    return _pcall(
        kern, name="scan_fwd", grid=(N_SCAN,),
        in_specs=[fwd_spec, fwd_spec, rev_spec, rev_spec],
        out_specs=[fwd_spec, rev_spec, hin_spec, hin_spec],
        out_shape=[_sds((TA, R), F32), _sds((TA, R), F32), _sds((N_SCAN, 1, R), F32), _sds((N_SCAN, 1, R), F32)],
        scratch_shapes=[pltpu.VMEM((1, R), F32), pltpu.VMEM((1, R), F32)], compiler_params=_cparams(),
    )(a_f, b_f, a_r, b_r)


def _scan_bwd(dy, a_f, y_f, hin_f, a_r, y_r, hin_r):
    fwd_spec = pl.BlockSpec((ROW_TILE, R), lambda i: (N_SCAN - 1 - i, 0))
    rev_spec = pl.BlockSpec((ROW_TILE, R), lambda i: (_rev_block(N_SCAN - 1 - i), 0))
    hin_spec = pl.BlockSpec((None, 1, R), lambda i: (N_SCAN - 1 - i, 0, 0))
    last = ROW_TILE - 1

    def kern(dyf, af, yf, hf0, dyr, ar, yr, hr0, daf, dbf, dar, dbr, gf_s, anf_s, gr_s, anr_s):
        @pl.when(pl.program_id(0) == 0)
        def _():
            for r in (gf_s, anf_s, gr_s, anr_s):
                r[...] = jnp.zeros_like(r)

        def one(dy_ref, a_ref, y_ref, da_ref, db_ref, g, an, p, pprev):
            gnew = dy_ref[pl.ds(p, 1), :] + an * g
            db_ref[pl.ds(p, 1), :] = gnew
            da_ref[pl.ds(p, 1), :] = gnew * y_ref[pl.ds(pprev, 1), :]
            return gnew, a_ref[pl.ds(p, 1), :]

        def step(s8, carry):
            gf, anf, gr, anr = carry
            base = s8 * 8
            for q in range(8):
                s = last - (base + q)
                gf, anf = one(dyf, af, yf, daf, dbf, gf, anf, s, s - 1)
                gr, anr = one(dyr, ar, yr, dar, dbr, gr, anr, last - s, last - s + 1)
            return gf, anf, gr, anr

        carry = (gf_s[...], anf_s[...], gr_s[...], anr_s[...])
        carry = lax.fori_loop(0, ROW_TILE // 8 - 1, step, carry)
        gf, anf, gr, anr = carry
        for s in range(7, 0, -1):
            gf, anf = one(dyf, af, yf, daf, dbf, gf, anf, s, s - 1)
            gr, anr = one(dyr, ar, yr, dar, dbr, gr, anr, last - s, last - s + 1)
        gf0 = dyf[0:1, :] + anf * gf
        dbf[0:1, :] = gf0
        daf[0:1, :] = gf0 * hf0[...]
        gr0 = dyr[last:last + 1, :] + anr * gr
        dbr[last:last + 1, :] = gr0
        dar[last:last + 1, :] = gr0 * hr0[...]
        gf_s[...] = gf0
        anf_s[...] = af[0:1, :]
        gr_s[...] = gr0
        anr_s[...] = ar[last:last + 1, :]

    return _pcall(
        kern, name="scan_bwd", grid=(N_SCAN,),
        in_specs=[fwd_spec, fwd_spec, fwd_spec, hin_spec, rev_spec, rev_spec, rev_spec, hin_spec],
        out_specs=[fwd_spec, fwd_spec, rev_spec, rev_spec],
        out_shape=[_sds((TA, R), F32)] * 4,
        scratch_shapes=[pltpu.VMEM((1, R), F32)] * 4, compiler_params=_cparams(),
    )(dy, a_f, y_f, hin_f, dy, a_r, y_r, hin_r)


def _me():
    return lax.axis_index("x"), lax.axis_index("y"), lax.axis_index("c")


def _other_chips(mx, my):
    return [(1 - mx, my), (mx, 1 - my), (1 - mx, 1 - my)]


def _rcopy(src, dst, ssem, rsem, dev):
    return pltpu.make_async_remote_copy(src_ref=src, dst_ref=dst, send_sem=ssem, recv_sem=rsem,
                                        device_id=dev, device_id_type=MESH)


def _allgather8(name, x):
    rows, cols = x.shape

    def kern(x_ref, o_ref, ssem, rsem, lsem):
        mx, my, mc = _me()
        me = 4 * mx + 2 * my + mc
        peers = []
        for k in range(1, 8):
            px = 1 - mx if (k >> 2) & 1 else mx
            py = 1 - my if (k >> 1) & 1 else my
            pc = 1 - mc if k & 1 else mc
            peers.append((px, py, pc))
        mine = pltpu.make_async_copy(x_ref, o_ref.at[me], lsem)
        mine.start()
        sends = [_rcopy(x_ref, o_ref.at[me], ssem.at[k], rsem.at[k], p) for k, p in enumerate(peers)]
        for cp in sends:
            cp.start()
        for k, (px, py, pc) in enumerate(peers):
            _rcopy(x_ref, o_ref.at[4 * px + 2 * py + pc], ssem.at[k], rsem.at[k], (px, py, pc)).wait_recv()
        for cp in sends:
            cp.wait_send()
        mine.wait()

    return _pcall(
        kern, name=name, in_specs=[ANY], out_specs=ANY, out_shape=_sds((8, rows, cols), F32),
        scratch_shapes=[pltpu.SemaphoreType.DMA((7,)), pltpu.SemaphoreType.DMA((7,)), pltpu.SemaphoreType.DMA(())],
    )(x)


def _gather_weights(ws):
    n = len(ws)

    def kern(*refs):
        o = refs[n:2 * n]
        s1, r1, s2, r2 = refs[2 * n:]
        mx, my, mc = _me()
        j0 = 2 * mx + my
        chips = _other_chips(mx, my)
        sib = (mx, my, 1 - mc)
        firsts = []
        for t in range(n):
            for q, (qx, qy) in enumerate(chips):
                cp = _rcopy(o[t].at[j0, mc], o[t].at[j0, mc], s1.at[3 * t + q], r1.at[3 * t + q], (qx, qy, mc))
                cp.start()
                firsts.append(cp)
        passed = []
        for t in range(n):
            for q, (qx, qy) in enumerate(chips):
                jq = 2 * qx + qy
                _rcopy(o[t].at[jq, mc], o[t].at[jq, mc], s1.at[3 * t + q], r1.at[3 * t + q], (qx, qy, mc)).wait_recv()
                fw = _rcopy(o[t].at[jq, mc], o[t].at[jq, mc], s2.at[3 * t + q], r2.at[3 * t + q], sib)
                fw.start()
                passed.append(fw)
        for t in range(n):
            for q, (qx, qy) in enumerate(chips):
                jq = 2 * qx + qy
                _rcopy(o[t].at[jq, 1 - mc], o[t].at[jq, 1 - mc], s2.at[3 * t + q], r2.at[3 * t + q], sib).wait_recv()
        for cp in firsts + passed:
            cp.wait_send()

    dma = pltpu.SemaphoreType.DMA
    return _pcall(
        kern, name="gather_weights", in_specs=[ANY] * n, out_specs=[ANY] * n,
        out_shape=[_sds(w.shape, w.dtype) for w in ws], input_output_aliases={t: t for t in range(n)},
        scratch_shapes=[dma((3 * n,)), dma((3 * n,)), dma((3 * n,)), dma((3 * n,))],
    )(*ws)


def _reduce_pair(gs):
    n = len(gs)

    def kern(*refs):
        g, o = refs[:n], refs[n:2 * n]
        ss, rs = refs[2 * n:]
        mx, my, mc = _me()
        sib = (mx, my, 1 - mc)
        sends = []
        for t in range(n):
            for j in range(4):
                cp = _rcopy(g[t].at[j, 1 - mc], o[t].at[j], ss.at[4 * t + j], rs.at[4 * t + j], sib)
                cp.start()
                sends.append(cp)
        for cp in sends:
            cp.wait_recv()
        for cp in sends:
            cp.wait_send()

    dma = pltpu.SemaphoreType.DMA
    return _pcall(
        kern, name="reduce_pair", in_specs=[ANY] * n, out_specs=[ANY] * n,
        out_shape=[_sds((4,) + g.shape[2:], g.dtype) for g in gs],
        scratch_shapes=[dma((4 * n,)), dma((4 * n,))],
    )(*gs)


def _reduce_chips(ss_):
    n = len(ss_)

    def kern(*refs):
        s, o = refs[:n], refs[n:2 * n]
        ssem, rsem = refs[2 * n:]
        mx, my, mc = _me()
        chips = _other_chips(mx, my)
        sends = []
        for t in range(n):
            for q, (qx, qy) in enumerate(chips):
                cp = _rcopy(s[t].at[2 * qx + qy], o[t].at[q], ssem.at[3 * t + q], rsem.at[3 * t + q], (qx, qy, mc))
                cp.start()
                sends.append(cp)
        for cp in sends:
            cp.wait_recv()
        for cp in sends:
            cp.wait_send()

    dma = pltpu.SemaphoreType.DMA
    return _pcall(
        kern, name="reduce_chips", in_specs=[ANY] * n, out_specs=[ANY] * n,
        out_shape=[_sds((3,) + s.shape[1:], s.dtype) for s in ss_],
        scratch_shapes=[dma((3 * n,)), dma((3 * n,))],
    )(*ss_)


def _share_halves(fulls):
    n = len(fulls)

    def kern(*refs):
        o = refs[n:2 * n]
        ss, rs = refs[2 * n:]
        mx, my, mc = _me()
        sib = (mx, my, 1 - mc)
        sends = []
        for t in range(n):
            cp = _rcopy(o[t].at[mc], o[t].at[mc], ss.at[t], rs.at[t], sib)
            cp.start()
            sends.append(cp)
        for t in range(n):
            _rcopy(o[t].at[1 - mc], o[t].at[1 - mc], ss.at[t], rs.at[t], sib).wait_recv()
        for cp in sends:
            cp.wait_send()

    dma = pltpu.SemaphoreType.DMA
    return _pcall(
        kern, name="share_halves", in_specs=[ANY] * n, out_specs=[ANY] * n,
        out_shape=[_sds(f.shape, f.dtype) for f in fulls], input_output_aliases={t: t for t in range(n)},
        scratch_shapes=[dma((n,)), dma((n,))],
    )(*fulls)


def _tiled_sp(name, fn, grid, sp, ins, outs):
    n_in = len(ins)

    def kern(sp_ref, *refs):
        tout = fn([r[...] for r in refs[:n_in]])
        for r, v in zip(refs[n_in:], tout):
            r[...] = v.astype(r.dtype)

    gs = pltpu.PrefetchScalarGridSpec(num_scalar_prefetch=1, grid=tuple(grid),
                                      in_specs=[s for _, s in ins], out_specs=[s for _, s in outs])
    res = _pcall(kern, name=name, grid_spec=gs, out_shape=[o for o, _ in outs], compiler_params=_cparams(),
                 )(sp, *[a for a, _ in ins])
    return list(res)


def _row_tile(rows, cols, itemsize=4, budget=2 * 1024 * 1024):
    tr = rows
    while tr * cols * itemsize > budget and tr % 32 == 0:
        tr //= 2
    return tr


def _gather_big(shards, place):
    slots = []
    for t, s in enumerate(shards):
        rr, cc = s.shape[1], s.shape[2]
        tr = _row_tile(rr, cc)
        (slot,) = _tiled_sp(
            f"place{t}", lambda tin: [tin[0]], (2, rr // tr), place,
            [(s, pl.BlockSpec((None, tr, cc), lambda h, i, sp: (h, i, 0)))],
            [(_sds((4, 2, rr, cc), BF16), pl.BlockSpec((None, None, tr, cc), lambda h, i, sp: (sp[0], h, i, 0)))])
        slots.append(slot)
    return _gather_weights(slots)


def _reduce_big(parts, place):
    theirs = _reduce_pair(parts)
    sums = []
    for t, (p, o) in enumerate(zip(parts, theirs)):
        rr, cc = p.shape[2], p.shape[3]
        tr = _row_tile(rr, cc)
        (s_t,) = _tiled_sp(
            f"pair_add{t}", lambda tin: [tin[0].astype(F32) + tin[1].astype(F32)], (4, rr // tr), place,
            [(p, pl.BlockSpec((None, None, tr, cc), lambda j, i, sp: (j, sp[1], i, 0))),
             (o, pl.BlockSpec((None, tr, cc), lambda j, i, sp: (j, i, 0)))],
            [(_sds((4, rr, cc), BF16), pl.BlockSpec((None, tr, cc), lambda j, i, sp: (j, i, 0)))])
        sums.append(s_t)
    landed = _reduce_chips(sums)
    fulls = []
    for t, (s, q) in enumerate(zip(sums, landed)):
        rr, cc = q.shape[1], q.shape[2]
        tr = _row_tile(rr, cc)

        def add4(tin):
            return [((tin[0].astype(F32) + tin[1].astype(F32)) + tin[2].astype(F32)) + tin[3].astype(F32)]

        ins = [(s, pl.BlockSpec((None, tr, cc), lambda i, sp: (sp[0], i, 0)))]
        ins += [(q, pl.BlockSpec((None, tr, cc), lambda i, sp, _k=k: (_k, i, 0))) for k in range(3)]
        (f_t,) = _tiled_sp(f"chip_add{t}", add4, (rr // tr,), place, ins,
                           [(_sds((2, rr, cc), F32), pl.BlockSpec((None, tr, cc), lambda i, sp: (sp[1], i, 0)))])
        fulls.append(f_t)
    return _share_halves(fulls)


def _pack(parts, PACK_ROWS=PACK_ROWS):
    flat, offs, pos = [], [], 0
    for p in parts:
        v = p.reshape(-1).astype(F32)
        n = -(-v.shape[0] // LANE) * LANE
        flat.append(jnp.pad(v, (0, n - v.shape[0])))
        offs.append((pos, v.shape[0], p.shape))
        pos += n
    total = -(-pos // (PACK_ROWS * LANE)) * PACK_ROWS * LANE
    flat.append(jnp.zeros((total - pos,), F32))
    return jnp.concatenate(flat).reshape(-1, LANE), offs


def _unpack(vec, offs):
    v = vec.reshape(-1)
    return [v[p:p + n].reshape(shape) for p, n, shape in offs]


def _sum_devices(name, g8):
    rows = g8.shape[1]
    tr = 8
    for cand in (1024, 512, 256, 128, 64, 32, 16, 8):
        if rows % cand == 0:
            tr = cand
            break

    def fn(ids, tin, vin):
        acc = tin[0]
        for v in tin[1:]:
            acc = acc + v
        return [acc], []

    ins = [(g8, pl.BlockSpec((None, tr, LANE), lambda i, _d=d: (_d, i, 0))) for d in range(8)]
    out = (_sds((rows, LANE), F32), pl.BlockSpec((tr, LANE), lambda i: (i, 0)))
    return _tiled(name, fn, (rows // tr,), ins, [], [out])[0]


def _adamw(name, w, g, m, v):
    rows, cols = w.shape
    tr = rows
    for cand in (512, 256, 128, 64, 32, 16, 8):
        if rows % cand == 0 and cand * cols * 4 <= 2 * 1024 * 1024:
            tr = cand
            break
    bc1 = 1.0 - ADAM_B1 ** ADAM_STEP
    bc2 = 1.0 - ADAM_B2 ** ADAM_STEP

    def fn(ids, tin, vin):
        wv, gv, mv, vv = tin
        mn = ADAM_B1 * mv + (1.0 - ADAM_B1) * gv
        vn = ADAM_B2 * vv + (1.0 - ADAM_B2) * (gv * gv)
        delta = -ADAM_LR * ((mn / bc1) / (jnp.sqrt(vn / bc2) + ADAM_EPS) + ADAM_WD * wv)
        return [delta, mn, vn], []

    spec = pl.BlockSpec((tr, cols), lambda i: (i, 0))
    outs = [(_sds((rows, cols), F32), spec)] * 3
    return _tiled(name, fn, (rows // tr,), [(a, spec) for a in (w, g, m, v)], [], outs)


def _pos_embed():
    n_rows = T // GRID_W
    q = D // 4
    omega = 1.0 / (10000.0 ** (jnp.arange(q, dtype=F32) / q))
    er = jnp.arange(n_rows, dtype=jnp.int32).astype(F32)[:, None] * omega[None, :]
    ec = jnp.arange(GRID_W, dtype=jnp.int32).astype(F32)[:, None] * omega[None, :]
    by_row = jnp.concatenate([jnp.sin(er), jnp.cos(er)], axis=-1)
    by_col = jnp.concatenate([jnp.sin(ec), jnp.cos(ec)], axis=-1)
    return jnp.concatenate([jnp.repeat(by_row, GRID_W, axis=0), jnp.tile(by_col, (n_rows, 1))], axis=-1)


def _dense_gates(w_a, w_x):
    per = N_BLK // 2
    kinds = jnp.stack([src.reshape(2, per, BLK, BLK) for src in (w_a[0], w_x[0], w_a[1], w_x[1])], axis=3)
    on_diag = jnp.eye(per, dtype=bool)[None, :, None, None, :, None]
    out = jnp.where(on_diag, kinds[:, :, :, :, None, :], 0.0)
    return out.reshape(2, RH, NQ).astype(BF16)


def _gate_block_grads(dwbd):
    per = N_BLK // 2
    d6 = dwbd.reshape(2, per, BLK, 4, per, BLK)
    kinds = []
    for q in range(4):
        kinds.append(jnp.stack([d6[:, nb, :, q, nb, :] for nb in range(per)], axis=1).reshape(N_BLK, BLK, BLK))
    return jnp.stack([kinds[0], kinds[2]]), jnp.stack([kinds[1], kinds[3]])


def _gate_bias_dense(b_a, b_x):
    cols = []
    for h in range(2):
        for src in (b_a[0], b_x[0], b_a[1], b_x[1]):
            cols.append(src.reshape(R)[h * RH:(h + 1) * RH])
    return jnp.concatenate(cols).reshape(1, 2 * NQ)


def _gate_bias_grads(dgb):
    v = dgb.reshape(2, 4, RH)
    kinds = [jnp.concatenate([v[0, q], v[1, q]]).reshape(N_BLK, BLK) for q in range(4)]
    return jnp.stack([kinds[0], kinds[2]]), jnp.stack([kinds[1], kinds[3]])


def _mlp_fwd(tag, x_in, g_norm, sh, sc, gate, w_in, w_out):
    n_t = T // ROW_TILE
    (h,) = _tiled(f"{tag}_norm", lambda ids, t, v: ([_norm_mod(t[0], v[0], v[1], v[2])], []), (n_t,),
                  [_rows(x_in)], [g_norm, sc, sh], [_orow(T, D, BF16)])
    tm = 512
    (r,) = _mm(f"{tag}_in", h, w_in, _NN, (T // tm, 4, 1),
               pl.BlockSpec((tm, D), lambda i, j, k: (i, 0)), pl.BlockSpec((None, D, D), lambda i, j, k: (j, 0, 0)),
               [(_sds((T, FF), BF16), pl.BlockSpec((tm, D), lambda i, j, k: (i, j)))], (tm, D),
               epi=lambda acc, ex: [jnp.maximum(acc, 0.0)])
    o, x_out = _mm(f"{tag}_out", r, w_out, _NN, (T // tm, 1, FF // D),
                   pl.BlockSpec((tm, D), lambda i, j, k: (i, k)), pl.BlockSpec((D, D), lambda i, j, k: (k, 0)),
                   [(_sds((T, D), F32), pl.BlockSpec((tm, D), lambda i, j, k: (i, 0)))] * 2, (tm, D),
                   extra=[(x_in, pl.BlockSpec((tm, D), lambda i, j, k: (i, 0))), (gate, _full_spec(gate))],
                   a_pre=lambda a: a * a, epi=lambda acc, ex: [acc, ex[0] + ex[1] * acc])
    return dict(h=h, r=r, o=o, x_in=x_in), x_out


def _gate_bwd(tag, dx, o, gate):
    def fn(ids, t, v):
        d_o = t[0] * v[0]
        return [d_o], [_sum0(t[0] * t[1]), _sum0(d_o)]
    return _tiled(f"{tag}_gate_bwd", fn, (T // ROW_TILE,), [_rows(dx), _rows(o)], [gate],
                  [_orow(T, D, BF16)], [(1, D), (1, D)])


def _norm_bwd(tag, dx_res, dh, dh_off, x, g_norm, sc, with_dx=True):
    n_t = x.shape[0] // ROW_TILE

    def fn(ids, t, v):
        if with_dx:
            dres, dhv, xv = t
        else:
            dhv, xv = t
        dxv, d_sh, d_sc, d_g = _norm_mod_bwd(dhv, xv, v[0], v[1])
        return ([dres + dxv] if with_dx else []), [d_sh, d_sc, d_g]

    ins = ([_rows(dx_res)] if with_dx else []) + [_rows(dh, off=dh_off), _rows(x)]
    outs = [_orow(x.shape[0], D, F32)] if with_dx else []
    return _tiled(f"{tag}_norm_bwd", fn, (n_t,), ins, [g_norm, sc], outs, [(1, D)] * 3)


def _mlp_bwd(tag, dx, saved, g_norm, sc, gate, w_in, w_out):
    d_o, d_gate, _ = _gate_bwd(tag, dx, saved["o"], gate)
    tm = 512
    r = saved["r"]
    (da,) = _mm(f"{tag}_dz", d_o, w_out, _NT, (T // tm, FF // D, 1),
                pl.BlockSpec((tm, D), lambda i, j, k: (i, 0)), pl.BlockSpec((D, D), lambda i, j, k: (j, 0)),
                [(_sds((T, FF), BF16), pl.BlockSpec((tm, D), lambda i, j, k: (i, j)))], (tm, D),
                extra=[(r, pl.BlockSpec((tm, D), lambda i, j, k: (i, j)))],
                epi=lambda acc, ex: [acc * (2.0 * ex[0].astype(F32))])
    tk = 512
    (dw_out,) = _mm(f"{tag}_dwout", r, d_o, _TN, (FF // tm, 1, T // tk),
                    pl.BlockSpec((tk, tm), lambda i, j, k: (k, i)), pl.BlockSpec((tk, D), lambda i, j, k: (k, 0)),
                    [(_sds((FF, D), BF16), pl.BlockSpec((tm, D), lambda i, j, k: (i, 0)))], (tm, D),
                    a_pre=lambda a: a * a)
    (dh,) = _mm(f"{tag}_dh", da, w_in, _NT, (T // tm, 1, 4),
                pl.BlockSpec((tm, D), lambda i, j, k: (i, k)), pl.BlockSpec((None, D, D), lambda i, j, k: (k, 0, 0)),
                [(_sds((T, D), F32), pl.BlockSpec((tm, D), lambda i, j, k: (i, 0)))], (tm, D))
    (dw_in,) = _mm(f"{tag}_dwin", saved["h"], da, _TN, (D // tm, 4, T // tk),
                   pl.BlockSpec((tk, tm), lambda i, j, k: (k, i)), pl.BlockSpec((tk, D), lambda i, j, k: (k, j)),
                   [(_sds((4, D, D), BF16), pl.BlockSpec((None, tm, D), lambda i, j, k: (j, i, 0)))], (tm, D))
    dx_in, d_sh, d_sc, d_g = _norm_bwd(tag, dx, dh, 0, saved["x_in"], g_norm, sc)
    return dx_in, dw_in, dw_out, dict(sh=d_sh, sc=d_sc, gate=d_gate, g_norm=d_g)


def _local_step(x, ctx, tgt, mods, cmods, norm_g, final_g, rec, conf, wg):
    n_t = T // ROW_TILE
    row = lambda v: v.reshape(1, -1)
    m0 = [row(mods[0, q]) for q in range(6)]
    m1 = [row(mods[1, q]) for q in range(6)]
    g00, g01, g10, g11 = (row(norm_g[0, 0]), row(norm_g[0, 1]), row(norm_g[1, 0]), row(norm_g[1, 1]))
    csh, csc = row(cmods[0]), row(cmods[1])
    pos = _pos_embed()

    def prep0(ids, t, v):
        cx, xv, pv = t
        is_ctx = ids[0] == 0
        xin = jnp.where(is_ctx, cx, xv + pv)
        sh = jnp.where(is_ctx, v[3], v[1])
        sc = jnp.where(is_ctx, v[4], v[2])
        return [_norm_mod(xin, v[0], sc, sh), xv + pv], []

    hcat, x0 = _tiled(
        "prep0", prep0, (N_SCAN,),
        [(ctx, pl.BlockSpec((ROW_TILE, D), lambda i: (0, 0))), _rows(x, off=-1, clamp_lo=True),
         _rows(pos, off=-1, clamp_lo=True)],
        [g00, m0[0], m0[1], csh, csc],
        [_orow(TA, D, BF16), _orow(T, D, F32, off=-1, clamp_lo=True)])

    tm_a = 768
    (a_in,) = _mm("rec_in", hcat, wg["rec_w_in"], _NN, (TA // tm_a, 4, 1),
                  pl.BlockSpec((tm_a, D), lambda i, j, k: (i, 0)),
                  pl.BlockSpec((None, D, RH), lambda i, j, k: (j, 0, 0)),
                  [(_sds((TA, 2 * R), F32), pl.BlockSpec((tm_a, RH), lambda i, j, k: (i, j)))], (tm_a, RH))
    cw = 256
    rec_starts = (0, 1)
    u = _dwconv("rec_conv", a_in, R // cw, rec["conv_w"], row(rec["conv_b"]), 1, rec_starts, R, cw)
    wbd = _dense_gates(rec["w_a"], rec["w_x"])
    gbias = _gate_bias_dense(rec["b_a"], rec["b_x"])
    lam = rec["lam"]
    a_f, b_f, a_r, b_r = _tiled("rg_fwd", _rg_fwd_fn, (TA // RG_TILE,), [_rows(u, tm=RG_TILE)], [wbd, gbias, lam],
                                [_orow(TA, R, F32, tm=RG_TILE)] * 4, vec_refs=True)
    y_f, y_r, hin_f, hin_r = _scan_fwd(a_f, b_f, a_r, b_r)

    def rec_mid(ids, t, v):
        gp, yf, yr = t
        g, _ = _gelu(gp)
        return [g * (yf + yr)], []

    (m_rec,) = _tiled("rec_mid", rec_mid, (n_t,),
                      [_rows(a_in, R, off=1), _rows(y_f, off=1), _rows(y_r, off=1)], [], [_orow(T, R, BF16)])
    tm = 512
    o_rec, x1 = _mm("rec_out", m_rec, wg["rec_w_out"], _NN, (T // tm, 1, 1),
                    pl.BlockSpec((tm, R), lambda i, j, k: (i, 0)), pl.BlockSpec((R, D), lambda i, j, k: (0, 0)),
                    [(_sds((T, D), F32), pl.BlockSpec((tm, D), lambda i, j, k: (i, 0)))] * 2, (tm, D),
                    extra=[(x0, pl.BlockSpec((tm, D), lambda i, j, k: (i, 0))), (m0[2], _full_spec(m0[2]))],
                    epi=lambda acc, ex: [acc, ex[0] + ex[1] * acc])
    mlp0, x2 = _mlp_fwd("mlp0", x1, g01, m0[3], m0[4], m0[5], wg["mlp_w_in"][0], wg["mlp_w_out"][0])

    (h1,) = _tiled("conf_norm", lambda ids, t, v: ([_norm_mod(t[0], v[0], v[1], v[2])], []), (n_t,),
                   [_rows(x2)], [g10, m1[1], m1[0]], [_orow(T, D, BF16)])
    b_pw1 = row(conf["b_pw1"])
    (pre,) = _mm("conf_pw1", h1, wg["conf_w_pw1"], _NN, (T // tm, 4, 1),
                 pl.BlockSpec((tm, D), lambda i, j, k: (i, 0)),
                 pl.BlockSpec((None, D, D // 2), lambda i, j, k: (j, 0, 0)),
                 [(_sds((T, 2 * D), F32), pl.BlockSpec((tm, D // 2), lambda i, j, k: (i, j)))], (tm, D // 2),
                 extra=[(b_pw1, pl.BlockSpec((1, D // 2), lambda i, j, k: (0, j)))],
                 epi=lambda acc, ex: [acc + ex[0]])
    (zg,) = _tiled("conf_glu", lambda ids, t, v: ([t[0] * _sigmoid(t[1])], []), (n_t,),
                   [_rows(pre, D, col=0), _rows(pre, D, col=1)], [], [_orow(T, D, F32)])
    conf_starts = (0,)
    zc = _dwconv("conf_conv", zg, 0, conf["conv_w"], row(conf["conv_b"]), CONF_KW // 2, conf_starts, D, cw)
    ln_g, ln_b = row(conf["ln_g"]), row(conf["ln_b"])

    def ln_silu(ids, t, v):
        nh, _ = _layernorm_parts(t[0])
        ln = nh * v[0] + v[1]
        return [ln * _sigmoid(ln)], []

    (s_conf,) = _tiled("conf_ln", ln_silu, (n_t,), [_rows(zc)], [ln_g, ln_b], [_orow(T, D, BF16)])
    b_pw2 = row(conf["b_pw2"])
    y_conf, x3 = _mm("conf_pw2", s_conf, wg["conf_w_pw2"], _NN, (T // tm, 1, 1),
                     pl.BlockSpec((tm, D), lambda i, j, k: (i, 0)), pl.BlockSpec((D, D), lambda i, j, k: (0, 0)),
                     [(_sds((T, D), F32), pl.BlockSpec((tm, D), lambda i, j, k: (i, 0)))] * 2, (tm, D),
                     extra=[(x2, pl.BlockSpec((tm, D), lambda i, j, k: (i, 0))), (m1[2], _full_spec(m1[2])),
                            (b_pw2, _full_spec(b_pw2))],
                     epi=lambda acc, ex: [acc + ex[2], ex[0] + ex[1] * (acc + ex[2])])
    mlp1, x4 = _mlp_fwd("mlp1", x3, g11, m1[3], m1[4], m1[5], wg["mlp_w_in"][1], wg["mlp_w_out"][1])

    fg = row(final_g)

    def head(ids, t, v):
        n, r = _rms(t[0])
        err = n * v[0] - t[1]
        d_out = err * (1.0 / D)
        dn = d_out * v[0]
        dxv = r * (dn - n * jnp.mean(dn * n, axis=-1, keepdims=True))
        part = jnp.sum(_sum0(err * err), axis=1, keepdims=True) * (0.5 / D)
        return [dxv], [part, _sum0(d_out * n)]

    dx4, loss, d_fg = _tiled("head", head, (n_t,), [_rows(x4), _rows(tgt)], [fg], [_orow(T, D, F32)],
                             [(1, 1), (1, D)])

    dx3, dw_in1, dw_out1, dm_mlp1 = _mlp_bwd("mlp1", dx4, mlp1, g11, m1[4], m1[5],
                                             wg["mlp_w_in"][1], wg["mlp_w_out"][1])
    d_y, d_g1c, d_bpw2 = _gate_bwd("conf", dx3, y_conf, m1[2])
    tk = 512
    (dw_pw2,) = _mm("conf_dwpw2", s_conf, d_y, _TN, (D // tm, 1, T // tk),
                    pl.BlockSpec((tk, tm), lambda i, j, k: (k, i)), pl.BlockSpec((tk, D), lambda i, j, k: (k, 0)),
                    [(_sds((D, D), BF16), pl.BlockSpec((tm, D), lambda i, j, k: (i, 0)))], (tm, D))
    (ds,) = _mm("conf_ds", d_y, wg["conf_w_pw2"], _NT, (T // tm, 1, 1),
                pl.BlockSpec((tm, D), lambda i, j, k: (i, 0)), pl.BlockSpec((D, D), lambda i, j, k: (0, 0)),
                [(_sds((T, D), F32), pl.BlockSpec((tm, D), lambda i, j, k: (i, 0)))], (tm, D))

    def ln_silu_bwd(ids, t, v):
        dsv, zcv = t
        nh, rstd = _layernorm_parts(zcv)
        ln = nh * v[0] + v[1]
        sg = _sigmoid(ln)
        d_ln = dsv * (sg * (1.0 + ln * (1.0 - sg)))
        d_nh = d_ln * v[0]
        d_zc = rstd * (d_nh - jnp.mean(d_nh, axis=-1, keepdims=True)
                       - nh * jnp.mean(d_nh * nh, axis=-1, keepdims=True))
        return [d_zc], [_sum0(d_ln * nh), _sum0(d_ln)]

    d_zc, d_lng, d_lnb = _tiled("conf_ln_bwd", ln_silu_bwd, (n_t,), [_rows(ds), _rows(zc)], [ln_g, ln_b],
                                [_orow(T, D, F32)], [(1, D), (1, D)])
    d_zg = _dwconv("conf_conv_dx", d_zc, 0, conf["conv_w"][::-1], jnp.zeros((1, D), F32),
                   CONF_KW - 1 - CONF_KW // 2, conf_starts, D, cw)
    d_cw_conf = _dwconv_wgrad("conf_conv_dw", d_zc, zg, 0, CONF_KW, CONF_KW // 2, conf_starts, D, cw)

    def glu_bwd(ids, t, v):
        dz, pa, pb = t
        sg = _sigmoid(pb)
        d_a = dz * sg
        d_b = dz * pa * sg * (1.0 - sg)
        return [d_a, d_b], [_sum0(d_a), _sum0(d_b)]

    d_pre_a, d_pre_b, d_b1a, d_b1b = _tiled(
        "conf_glu_bwd", glu_bwd, (n_t,), [_rows(d_zg), _rows(pre, D, col=0), _rows(pre, D, col=1)], [],
        [_orow(T, D, BF16), _orow(T, D, BF16)], [(1, D), (1, D)])
    d_pre = jnp.concatenate([d_pre_a, d_pre_b], axis=1)
    (dw_pw1,) = _mm("conf_dwpw1", h1, d_pre, _TN, (D // tm, 4, T // tk),
                    pl.BlockSpec((tk, tm), lambda i, j, k: (k, i)),
                    pl.BlockSpec((tk, D // 2), lambda i, j, k: (k, j)),
                    [(_sds((4, D, D // 2), BF16), pl.BlockSpec((None, tm, D // 2), lambda i, j, k: (j, i, 0)))],
                    (tm, D // 2))
    (dh1,) = _mm("conf_dh", d_pre, wg["conf_w_pw1"], _NT, (T // tm, 1, 4),
                 pl.BlockSpec((tm, D // 2), lambda i, j, k: (i, k)),
                 pl.BlockSpec((None, D, D // 2), lambda i, j, k: (k, 0, 0)),
                 [(_sds((T, D), F32), pl.BlockSpec((tm, D), lambda i, j, k: (i, 0)))], (tm, D))
    dx2, d_sh1c, d_sc1c, d_g10 = _norm_bwd("conf", dx3, dh1, 0, x2, g10, m1[1])

    dx1, dw_in0, dw_out0, dm_mlp0 = _mlp_bwd("mlp0", dx2, mlp0, g01, m0[4], m0[5],
                                             wg["mlp_w_in"][0], wg["mlp_w_out"][0])
    d_orec, d_g1r, _ = _gate_bwd("rec", dx1, o_rec, m0[2])
    (dw_rout,) = _mm("rec_dwout", m_rec, d_orec, _TN, (R // RH, 1, T // tk),
                     pl.BlockSpec((tk, RH), lambda i, j, k: (k, i)), pl.BlockSpec((tk, D), lambda i, j, k: (k, 0)),
                     [(_sds((R, D), BF16), pl.BlockSpec((RH, D), lambda i, j, k: (i, 0)))], (RH, D))
    (dm_rec,) = _mm("rec_dm", d_orec, wg["rec_w_out"], _NT, (T // tm, 1, 1),
                    pl.BlockSpec((tm, D), lambda i, j, k: (i, 0)), pl.BlockSpec((R, D), lambda i, j, k: (0, 0)),
                    [(_sds((T, R), F32), pl.BlockSpec((tm, R), lambda i, j, k: (i, 0)))], (tm, R))

    def rec_mid_bwd(ids, t, v):
        dmv, gp, yf, yr = t
        g, th = _gelu(gp)
        lat = ids[0] > 0
        d_gp = jnp.where(lat, dmv * (yf + yr) * _gelu_grad(gp, th), 0.0)
        dy = jnp.where(lat, dmv * g, 0.0)
        return [d_gp, dy], []

    d_gp, dy = _tiled("rec_mid_bwd", rec_mid_bwd, (N_SCAN,),
                      [_rows(dm_rec, off=-1, clamp_lo=True), _rows(a_in, R), _rows(y_f), _rows(y_r)], [],
                      [_orow(TA, R, BF16), _orow(TA, R, F32)])
    da_f, db_f, da_r, db_r = _scan_bwd(dy, a_f, y_f, hin_f, a_r, y_r, hin_r)
    d_gpre, d_u, d_gbias, d_lam = _tiled(
        "rg_bwd", _rg_bwd_fn, (TA // RG_TILE,), [_rows(a, tm=RG_TILE) for a in (u, da_f, db_f, da_r, db_r)],
        [wbd, gbias, lam], [_orow(TA, 2 * NQ, BF16, tm=RG_TILE), _orow(TA, R, F32, tm=RG_TILE)],
        [(1, 2 * NQ), (1, 2 * R)], vec_refs=True)
    tk_a = 768
    (d_wbd,) = _mm("rg_dw", u, d_gpre, _TN, (2, 2, TA // tk_a),
                   pl.BlockSpec((tk_a, RH), lambda i, j, k: (k, i)),
                   pl.BlockSpec((tk_a, NQ // 2), lambda i, j, k: (k, 2 * i + j)),
                   [(_sds((2, RH, NQ), F32), pl.BlockSpec((None, RH, NQ // 2), lambda i, j, k: (i, 0, j)))],
                   (RH, NQ // 2))
    d_p = _dwconv("rec_conv_dx", d_u, 0, rec["conv_w"][::-1], jnp.zeros((1, R), F32), REC_KW - 1 - 1,
                  rec_starts, R, cw)
    d_cw_rec = _dwconv_wgrad("rec_conv_dw", d_u, a_in, R // cw, REC_KW, 1, rec_starts, R, cw)
    d_a = jnp.concatenate([d_gp, d_p.astype(BF16)], axis=1)
    (dw_rin,) = _mm("rec_dwin", hcat, d_a, _TN, (D // tm, 4, TA // tk_a),
                    pl.BlockSpec((tk_a, tm), lambda i, j, k: (k, i)), pl.BlockSpec((tk_a, RH), lambda i, j, k: (k, j)),
                    [(_sds((4, D, RH), BF16), pl.BlockSpec((None, tm, RH), lambda i, j, k: (j, i, 0)))], (tm, RH))
    (dhcat,) = _mm("rec_dh", d_a, wg["rec_w_in"], _NT, (TA // tm_a, 1, 4),
                   pl.BlockSpec((tm_a, RH), lambda i, j, k: (i, k)),
                   pl.BlockSpec((None, D, RH), lambda i, j, k: (k, 0, 0)),
                   [(_sds((TA, D), F32), pl.BlockSpec((tm_a, D), lambda i, j, k: (i, 0)))], (tm_a, D))
    dx0, d_sh1r, d_sc1r, d_g00 = _norm_bwd("rec", dx1, dhcat, 1, x0, g00, m0[1])
    d_csh, d_csc, d_g00c = _norm_bwd("ctx", None, dhcat, 0, ctx, g00, csc, with_dx=False)

    big = dict(rec_w_in=dw_rin, rec_w_out=dw_rout, conf_w_pw1=dw_pw1, conf_w_pw2=dw_pw2,
               mlp_w_in=(dw_in0, dw_in1), mlp_w_out=(dw_out0, dw_out1))
    d_wa, d_wx = _gate_block_grads(d_wbd)
    d_ba, d_bx = _gate_bias_grads(d_gbias)
    d_mod = jnp.concatenate([
        d_sh1r, d_sc1r, d_g1r, dm_mlp0["sh"], dm_mlp0["sc"], dm_mlp0["gate"],
        d_sh1c, d_sc1c, d_g1c, dm_mlp1["sh"], dm_mlp1["sc"], dm_mlp1["gate"]], axis=1).reshape(2, 6 * D)
    small = dict(
        d_mod=d_mod, d_cmod=jnp.concatenate([d_csh, d_csc], axis=1),
        norm_g=jnp.concatenate([d_g00 + d_g00c, dm_mlp0["g_norm"], d_g10, dm_mlp1["g_norm"]], axis=1),
        rec_conv_w=d_cw_rec[:REC_KW], rec_conv_b=d_cw_rec[REC_KW], rec_lambda=d_lam.reshape(2, R),
        rec_w_a=d_wa, rec_b_a=d_ba, rec_w_x=d_wx, rec_b_x=d_bx,
        conf_b_pw1=jnp.concatenate([d_b1a, d_b1b], axis=1), conf_conv_w=d_cw_conf[:CONF_KW],
        conf_conv_b=d_cw_conf[CONF_KW], conf_ln_g=d_lng, conf_ln_b=d_lnb, conf_b_pw2=d_bpw2, final_g=d_fg)
    return loss.reshape(()), dx0, big, small


_BIG = ("rec_w_in", "rec_w_out", "conf_w_pw1", "conf_w_pw2", "mlp_w_in", "mlp_w_out")


def _halves(w):
    return w.reshape(2, w.shape[0] // 2, w.shape[1])


def _ada_fwd(c16, w_ada, b_shard):
    ns = w_ada.shape[2]
    tn = 512

    def kern(c_ref, w_ref, b_ref, o_ref):
        cv = c_ref[...]
        s = (cv * _sigmoid(cv)).astype(BF16)
        o_ref[...] = jnp.dot(s, w_ref[...].astype(BF16), preferred_element_type=F32) + b_ref[...]

    return _pcall(
        kern, name="ada_fwd", grid=(2, ns // tn),
        in_specs=[pl.BlockSpec((16, D), lambda l, j: (0, 0)), pl.BlockSpec((None, D, tn), lambda l, j: (l, 0, j)),
                  pl.BlockSpec((None, 1, tn), lambda l, j: (l, 0, j))],
        out_specs=pl.BlockSpec((None, 16, tn), lambda l, j: (l, 0, j)),
        out_shape=_sds((2, 16, ns), F32), compiler_params=_cparams(),
    )(c16, w_ada, b_shard)


def _ada_bwd(c16, dm16, w_ada):
    ns = w_ada.shape[2]
    tn = 512

    def kern(c_ref, dm_ref, w_ref, gw_ref, ds_ref):
        cv = c_ref[...]
        s = (cv * _sigmoid(cv)).astype(BF16)
        dm = dm_ref[...].astype(BF16)
        gw_ref[...] = lax.dot_general(s, dm, _TN, preferred_element_type=F32)

        @pl.when(jnp.logical_and(pl.program_id(0) == 0, pl.program_id(1) == 0))
        def _():
            ds_ref[...] = jnp.zeros_like(ds_ref)

        ds_ref[...] += lax.dot_general(dm, w_ref[...].astype(BF16), _NT, preferred_element_type=F32)

    return _pcall(
        kern, name="ada_bwd", grid=(2, ns // tn),
        in_specs=[pl.BlockSpec((16, D), lambda l, j: (0, 0)), pl.BlockSpec((None, 16, tn), lambda l, j: (l, 0, j)),
                  pl.BlockSpec((None, D, tn), lambda l, j: (l, 0, j))],
        out_specs=[pl.BlockSpec((None, D, tn), lambda l, j: (l, 0, j)), pl.BlockSpec((16, D), lambda l, j: (0, 0))],
        out_shape=[_sds((2, D, ns), F32), _sds((16, D), F32)], compiler_params=_cparams(),
    )(c16, dm16, w_ada)


def _cctx_grad(ds8, c_ctx):
    def kern(d_ref, c_ref, o_ref):
        tot = d_ref[0, 8:9, :] + d_ref[2, 8:9, :] + d_ref[4, 8:9, :] + d_ref[6, 8:9, :]
        cv = c_ref[...]
        sg = _sigmoid(cv)
        o_ref[...] = tot * (sg * (1.0 + cv * (1.0 - sg)))

    return _pcall(kern, name="cctx_grad", out_shape=_sds((1, D), F32))(ds8, c_ctx.reshape(1, D))


def kernel(x, c, ctx, c_ctx, w_ada, b_ada, norm_g, rec_w_in, rec_conv_w, rec_conv_b, rec_lambda, rec_w_a, rec_b_a, rec_w_x, rec_b_x, rec_w_out, conf_w_pw1, conf_b_pw1, conf_conv_w, conf_conv_b, conf_ln_g, conf_ln_b, conf_w_pw2, conf_b_pw2, mlp_w_in, mlp_w_out, final_g, loss_target, m_c_ctx, m_w_ada, m_b_ada, m_norm_g, m_rec_w_in, m_rec_conv_w, m_rec_conv_b, m_rec_lambda, m_rec_w_a, m_rec_b_a, m_rec_w_x, m_rec_b_x, m_rec_w_out, m_conf_w_pw1, m_conf_b_pw1, m_conf_conv_w, m_conf_conv_b, m_conf_ln_g, m_conf_ln_b, m_conf_w_pw2, m_conf_b_pw2, m_mlp_w_in, m_mlp_w_out, m_final_g, v_c_ctx, v_w_ada, v_b_ada, v_norm_g, v_rec_w_in, v_rec_conv_w, v_rec_conv_b, v_rec_lambda, v_rec_w_a, v_rec_b_a, v_rec_w_x, v_rec_b_x, v_rec_w_out, v_conf_w_pw1, v_conf_b_pw1, v_conf_conv_w, v_conf_conv_b, v_conf_ln_g, v_conf_ln_b, v_conf_w_pw2, v_conf_b_pw2, v_mlp_w_in, v_mlp_w_out, v_final_g):
    names = ["c_ctx", "w_ada", "b_ada", "norm_g", "rec_w_in", "rec_conv_w", "rec_conv_b", "rec_lambda", "rec_w_a",
             "rec_b_a", "rec_w_x", "rec_b_x", "rec_w_out", "conf_w_pw1", "conf_b_pw1", "conf_conv_w", "conf_conv_b",
             "conf_ln_g", "conf_ln_b", "conf_w_pw2", "conf_b_pw2", "mlp_w_in", "mlp_w_out", "final_g"]
    w = dict(zip(names, [c_ctx, w_ada, b_ada, norm_g, rec_w_in, rec_conv_w, rec_conv_b, rec_lambda, rec_w_a,
                         rec_b_a, rec_w_x, rec_b_x, rec_w_out, conf_w_pw1, conf_b_pw1, conf_conv_w, conf_conv_b,
                         conf_ln_g, conf_ln_b, conf_w_pw2, conf_b_pw2, mlp_w_in, mlp_w_out, final_g]))
    m = dict(zip(names, [m_c_ctx, m_w_ada, m_b_ada, m_norm_g, m_rec_w_in, m_rec_conv_w, m_rec_conv_b, m_rec_lambda,
                         m_rec_w_a, m_rec_b_a, m_rec_w_x, m_rec_b_x, m_rec_w_out, m_conf_w_pw1, m_conf_b_pw1,
                         m_conf_conv_w, m_conf_conv_b, m_conf_ln_g, m_conf_ln_b, m_conf_w_pw2, m_conf_b_pw2,
                         m_mlp_w_in, m_mlp_w_out, m_final_g]))
    v = dict(zip(names, [v_c_ctx, v_w_ada, v_b_ada, v_norm_g, v_rec_w_in, v_rec_conv_w, v_rec_conv_b, v_rec_lambda,
                         v_rec_w_a, v_rec_b_a, v_rec_w_x, v_rec_b_x, v_rec_w_out, v_conf_w_pw1, v_conf_b_pw1,
                         v_conf_conv_w, v_conf_conv_b, v_conf_ln_g, v_conf_ln_b, v_conf_w_pw2, v_conf_b_pw2,
                         v_mlp_w_in, v_mlp_w_out, v_final_g]))
    mx, my, mc = _me()
    chip = 2 * mx + my
    me = 4 * mx + 2 * my + mc

    sharded_small = ["norm_g", "rec_conv_w", "rec_lambda", "conf_b_pw1", "conf_conv_w", "conf_conv_b", "conf_ln_g",
                     "conf_ln_b", "conf_b_pw2"]
    packed, offs = _pack([c] + [w[k] for k in sharded_small], 8)
    got = _allgather8("gather_small", packed)
    per_dev = [_unpack(got[d], offs) for d in range(8)]
    c_rows = jnp.concatenate([per_dev[d][0].reshape(1, D) for d in range(8)], axis=0)
    full = {k: jnp.concatenate([per_dev[2 * j][1 + i] for j in range(4)], axis=-1)
            for i, k in enumerate(sharded_small)}
    c16 = jnp.concatenate([c_rows, c_ctx.reshape(1, D), jnp.zeros((7, D), F32)], axis=0)

    ns = w_ada.shape[2]
    b_shard = lax.dynamic_slice_in_dim(b_ada, chip * ns, ns, axis=1).reshape(2, 1, ns)
    prod = _ada_fwd(c16, w_ada, b_shard)
    prod8 = _allgather8("gather_mod", prod.reshape(32, ns)).reshape(8, 2, 16, ns)
    mod_all = jnp.concatenate([prod8[2 * j] for j in range(4)], axis=-1)
    mods = lax.dynamic_index_in_dim(mod_all, me, axis=1, keepdims=False).reshape(2, 6, D)
    cmods = mod_all[0, 8].reshape(6, D)[:2]

    place = jnp.stack([chip, mc]).astype(jnp.int32)
    shards = [_halves(rec_w_in[0]), _halves(rec_w_out[0]), _halves(conf_w_pw1[0]), _halves(conf_w_pw2[0]),
              _halves(mlp_w_in[0]), _halves(mlp_w_in[1]), _halves(mlp_w_out[0]), _halves(mlp_w_out[1])]
    gw = _gather_big(shards, place)
    wg = dict(rec_w_in=gw[0].reshape(4, D, RH), rec_w_out=gw[1].reshape(R, D),
              conf_w_pw1=gw[2].reshape(4, D, D // 2), conf_w_pw2=gw[3].reshape(D, D),
              mlp_w_in=(gw[4].reshape(4, D, D), gw[5].reshape(4, D, D)),
              mlp_w_out=(gw[6].reshape(FF, D), gw[7].reshape(FF, D)))

    rec = dict(conv_w=full["rec_conv_w"][0], conv_b=rec_conv_b[0], lam=full["rec_lambda"][0],
               w_a=rec_w_a[0], b_a=rec_b_a[0], w_x=rec_w_x[0], b_x=rec_b_x[0])
    conf = dict(b_pw1=full["conf_b_pw1"][0], conv_w=full["conf_conv_w"][0], conv_b=full["conf_conv_b"][0],
                ln_g=full["conf_ln_g"][0], ln_b=full["conf_ln_b"][0], b_pw2=full["conf_b_pw2"][0])
    loss_local, grad_x, big, small = _local_step(x[0], ctx[0], loss_target[0], mods, cmods, full["norm_g"], final_g,
                                                 rec, conf, wg)
    loss = lax.psum(loss_local, ("x", "y", "c"))

    parts = [big["rec_w_in"], big["rec_w_out"], big["conf_w_pw1"], big["conf_w_pw2"],
             big["mlp_w_in"][0], big["mlp_w_in"][1], big["mlp_w_out"][0], big["mlp_w_out"][1]]
    parts = [p.reshape(4, 2, s.shape[1], s.shape[2]) for p, s in zip(parts, shards)]
    whole = _reduce_big(parts, place)
    g_big = dict(rec_w_in=whole[0].reshape(rec_w_in.shape), rec_w_out=whole[1].reshape(rec_w_out.shape),
                 conf_w_pw1=whole[2].reshape(conf_w_pw1.shape), conf_w_pw2=whole[3].reshape(conf_w_pw2.shape),
                 mlp_w_in=jnp.stack([whole[4].reshape(D, D), whole[5].reshape(D, D)]),
                 mlp_w_out=jnp.stack([whole[6].reshape(D, D), whole[7].reshape(D, D)]))

    small_names = ["d_mod", "d_cmod", "norm_g", "rec_conv_w", "rec_conv_b", "rec_lambda", "rec_w_a", "rec_b_a",
                   "rec_w_x", "rec_b_x", "conf_b_pw1", "conf_conv_w", "conf_conv_b", "conf_ln_g", "conf_ln_b",
                   "conf_b_pw2", "final_g"]
    spacked, soffs = _pack([small[k] for k in small_names])
    s8 = _allgather8("gather_grads", spacked)
    ssum = dict(zip(small_names, _unpack(_sum_devices("sum_grads", s8), soffs)))
    dmod_rows = jnp.stack([_unpack(s8[d], soffs)[0] for d in range(8)], axis=1)
    d_cmod_full = jnp.concatenate([ssum["d_cmod"].reshape(1, 2 * D), jnp.zeros((1, 4 * D), F32)], axis=1)
    dm16 = jnp.concatenate([dmod_rows, jnp.stack([d_cmod_full, jnp.zeros((1, 6 * D), F32)]),
                            jnp.zeros((2, 7, 6 * D), F32)], axis=1)
    dm16_shard = lax.dynamic_slice_in_dim(dm16, chip * ns, ns, axis=2)
    g_w_ada, ds_part = _ada_bwd(c16, dm16_shard, w_ada)
    ds8 = _allgather8("gather_dsilu", ds_part)
    g_c_ctx = _cctx_grad(ds8, c_ctx).reshape(D)
    g_b_ada = ssum["d_mod"] + jnp.stack([d_cmod_full[0], jnp.zeros((6 * D,), F32)])

    def shard_of(a, axis):
        n = a.shape[axis] // 4
        return lax.dynamic_slice_in_dim(a, chip * n, n, axis=axis)

    grads = dict(
        c_ctx=g_c_ctx, w_ada=g_w_ada, b_ada=g_b_ada,
        norm_g=shard_of(ssum["norm_g"].reshape(2, 2, D), 2),
        rec_w_in=g_big["rec_w_in"], rec_conv_w=shard_of(ssum["rec_conv_w"].reshape(1, REC_KW, R), 2),
        rec_conv_b=ssum["rec_conv_b"].reshape(1, R), rec_lambda=shard_of(ssum["rec_lambda"].reshape(1, 2, R), 2),
        rec_w_a=ssum["rec_w_a"].reshape(rec_w_a.shape), rec_b_a=ssum["rec_b_a"].reshape(rec_b_a.shape),
        rec_w_x=ssum["rec_w_x"].reshape(rec_w_x.shape), rec_b_x=ssum["rec_b_x"].reshape(rec_b_x.shape),
        rec_w_out=g_big["rec_w_out"], conf_w_pw1=g_big["conf_w_pw1"],
        conf_b_pw1=shard_of(ssum["conf_b_pw1"].reshape(1, 2 * D), 1),
        conf_conv_w=shard_of(ssum["conf_conv_w"].reshape(1, CONF_KW, D), 2),
        conf_conv_b=shard_of(ssum["conf_conv_b"].reshape(1, D), 1),
        conf_ln_g=shard_of(ssum["conf_ln_g"].reshape(1, D), 1), conf_ln_b=shard_of(ssum["conf_ln_b"].reshape(1, D), 1),
        conf_w_pw2=g_big["conf_w_pw2"], conf_b_pw2=shard_of(ssum["conf_b_pw2"].reshape(1, D), 1),
        mlp_w_in=g_big["mlp_w_in"], mlp_w_out=g_big["mlp_w_out"], final_g=ssum["final_g"].reshape(D))

    delta, new_m, new_v = {}, {}, {}
    big_names = ("w_ada",) + _BIG
    for k in big_names:
        cols = w[k].shape[-1]
        d_, m_, v_ = _adamw(f"adamw_{k}", w[k].reshape(-1, cols), grads[k].reshape(-1, cols),
                            m[k].reshape(-1, cols), v[k].reshape(-1, cols))
        delta[k], new_m[k], new_v[k] = (a.reshape(w[k].shape) for a in (d_, m_, v_))
    rest = [k for k in names if k not in big_names]
    pw, poffs = _pack([w[k] for k in rest])
    pg, _ = _pack([grads[k] for k in rest])
    pm, _ = _pack([m[k] for k in rest])
    pv, _ = _pack([v[k] for k in rest])
    d_, m_, v_ = _adamw("adamw_small", pw, pg, pm, pv)
    for k, dd, mm, vv in zip(rest, _unpack(d_, poffs), _unpack(m_, poffs), _unpack(v_, poffs)):
        delta[k], new_m[k], new_v[k] = dd, mm, vv

    return (loss, grad_x[None], *[grads[k] for k in names], *[delta[k] for k in names],
            *[new_m[k] for k in names], *[new_v[k] for k in names])
```

```python
import functools
import math

import jax
import jax.numpy as jnp
from jax import lax
from jax.experimental import pallas as pl
from jax.experimental.pallas import tpu as pltpu

F32 = jnp.float32
BF16 = jnp.bfloat16

D = 1024
T = 2048
TC = 256
TA = T + TC
R = 1280
RH = R // 2
NQ = 4 * RH
FF = 4096
N_BLK = 16
BLK = R // N_BLK
GRID_W = 64
EPS = 1e-6
RG_C = 8.0
CONF_KW = 31
REC_KW = 4
LANE = 128
ROW_TILE = 256
HALO = 16
RG_TILE = 128
PACK_ROWS = 512
V7X_VMEM_BYTES = 64 * 1024 * 1024
VMEM_LIMIT = V7X_VMEM_BYTES - 8 * 1024 * 1024

ADAM_LR = 0.001
ADAM_B1 = 0.9
ADAM_B2 = 0.999
ADAM_EPS = 1e-08
ADAM_WD = 0.01
ADAM_STEP = 10

MESH = pl.DeviceIdType.MESH
ANY = pl.BlockSpec(memory_space=pl.ANY)


def _sds(shape, dtype):
    return jax.ShapeDtypeStruct(tuple(shape), dtype)


def _pcall(body, **kw):
    return pl.pallas_call(body, **kw)


def _cparams():
    return pltpu.CompilerParams(vmem_limit_bytes=VMEM_LIMIT)


def _full_spec(arr):
    nd = arr.ndim
    return pl.BlockSpec(arr.shape, lambda *ids, _n=nd: (0,) * _n)


def _sum0(v):
    return jnp.sum(v, axis=0, keepdims=True)


def _tiled(name, fn, grid, ins, vecs, outs, vec_outs=(), vec_refs=False):
    n_in, n_vec, n_out = len(ins), len(vecs), len(outs)
    n_grid = len(grid)

    def kern(*refs):
        ids = [pl.program_id(a) for a in range(n_grid)]
        tin = [r[...] for r in refs[:n_in]]
        vin = list(refs[n_in:n_in + n_vec]) if vec_refs else [r[...] for r in refs[n_in:n_in + n_vec]]
        o_refs = refs[n_in + n_vec:n_in + n_vec + n_out]
        a_refs = refs[n_in + n_vec + n_out:]
        tout, incs = fn(ids, tin, vin)
        for r, v in zip(o_refs, tout):
            r[...] = v.astype(r.dtype)
        if a_refs:
            first = functools.reduce(jnp.logical_and, [i == 0 for i in ids])

            @pl.when(first)
            def _():
                for r in a_refs:
                    r[...] = jnp.zeros_like(r)

            for r, v in zip(a_refs, incs):
                r[...] += v

    out_shape = [o for o, _ in outs] + [_sds(s, F32) for s in vec_outs]
    out_specs = [s for _, s in outs] + [
        pl.BlockSpec(tuple(s), lambda *ids, _n=len(s): (0,) * _n) for s in vec_outs]
    res = _pcall(
        kern, name=name, grid=tuple(grid),
        in_specs=[s for _, s in ins] + [_full_spec(v) for v in vecs],
        out_specs=out_specs, out_shape=out_shape, compiler_params=_cparams(),
    )(*[a for a, _ in ins], *vecs)
    return list(res)


def _rows(arr, ncols=None, tm=ROW_TILE, off=0, col=0, clamp_lo=False):
    ncols = arr.shape[1] if ncols is None else ncols
    if clamp_lo:
        return arr, pl.BlockSpec((tm, ncols), lambda i: (jnp.maximum(i + off, 0), col))
    return arr, pl.BlockSpec((tm, ncols), lambda i: (i + off, col))


def _orow(nrows, ncols, dtype, tm=ROW_TILE, off=0, clamp_lo=False):
    if clamp_lo:
        return _sds((nrows, ncols), dtype), pl.BlockSpec((tm, ncols), lambda i: (jnp.maximum(i + off, 0), 0))
    return _sds((nrows, ncols), dtype), pl.BlockSpec((tm, ncols), lambda i: (i + off, 0))


_NN = (((1,), (0,)), ((), ()))
_TN = (((0,), (0,)), ((), ()))
_NT = (((1,), (1,)), ((), ()))


def _mm(name, a, b, dims, grid, a_spec, b_spec, out, acc_shape, extra=(), a_pre=None, epi=None):
    n_k = grid[2]
    n_ex = len(extra)

    def kern(a_ref, b_ref, *rest):
        ex = rest[:n_ex]
        o_refs = rest[n_ex:-1]
        acc = rest[-1]
        k = pl.program_id(2)

        @pl.when(k == 0)
        def _():
            acc[...] = jnp.zeros_like(acc)

        av = a_ref[...]
        if a_pre is not None:
            av = a_pre(av)
        acc[...] += lax.dot_general(av.astype(BF16), b_ref[...].astype(BF16), dims,
                                    preferred_element_type=F32)

        @pl.when(k == n_k - 1)
        def _():
            vals = [acc[...]] if epi is None else epi(acc[...], [e[...] for e in ex])
            for r, v in zip(o_refs, vals):
                r[...] = v.astype(r.dtype)

    res = _pcall(
        kern, name=name, grid=tuple(grid),
        in_specs=[a_spec, b_spec] + [s for _, s in extra],
        out_specs=[s for _, s in out], out_shape=[o for o, _ in out],
        scratch_shapes=[pltpu.VMEM(tuple(acc_shape), F32)], compiler_params=_cparams(),
    )(a, b, *[e for e, _ in extra])
    return list(res)


def _rms(x):
    r = lax.rsqrt(jnp.mean(x * x, axis=-1, keepdims=True) + EPS)
    return x * r, r


def _norm_mod(x, g, sc, sh):
    n, _ = _rms(x)
    return (n * g) * (1.0 + sc) + sh


def _norm_mod_bwd(dh, x, g, sc):
    n, r = _rms(x)
    d_sh = _sum0(dh)
    d_sc = _sum0(dh * (n * g))
    d_g = _sum0(dh * (1.0 + sc) * n)
    dn = dh * (g * (1.0 + sc))
    dx = r * (dn - n * jnp.mean(dn * n, axis=-1, keepdims=True))
    return dx, d_sh, d_sc, d_g


_GELU_K = math.sqrt(2.0 / math.pi)


def _gelu(x):
    t = jnp.tanh(_GELU_K * (x + 0.044715 * x * x * x))
    return 0.5 * x * (1.0 + t), t


def _gelu_grad(x, t):
    return 0.5 * (1.0 + t) + 0.5 * x * (1.0 - t * t) * (_GELU_K * (1.0 + 3.0 * 0.044715 * x * x))


def _sigmoid(x):
    return 1.0 / (1.0 + jnp.exp(-x))


def _expm1(x):
    p = 1.0 + x * (1.0 / 9.0)
    for n in (8.0, 7.0, 6.0, 5.0, 4.0, 3.0, 2.0):
        p = 1.0 + (x * (1.0 / n)) * p
    return jnp.where(jnp.abs(x) < 0.5, x * p, jnp.exp(x) - 1.0)


def _softplus_neg(lam):
    return jnp.log1p(jnp.exp(-jnp.abs(lam))) + jnp.maximum(-lam, 0.0)


def _layernorm_parts(x):
    mu = jnp.mean(x, axis=-1, keepdims=True)
    xc = x - mu
    rstd = lax.rsqrt(jnp.mean(xc * xc, axis=-1, keepdims=True) + EPS)
    return xc * rstd, rstd


def _rg_gates(u, wbd, gbias, lam):
    sp = _softplus_neg(lam)
    parts = {}
    for h in range(2):
        uh = u[:, h * RH:(h + 1) * RH]
        g = jnp.dot(uh.astype(BF16), wbd[h], preferred_element_type=F32) + gbias[:, h * NQ:(h + 1) * NQ]
        for d in range(2):
            r = _sigmoid(g[:, (2 * d) * RH:(2 * d + 1) * RH])
            i = _sigmoid(g[:, (2 * d + 1) * RH:(2 * d + 2) * RH])
            sph = sp[d:d + 1, h * RH:(h + 1) * RH]
            la = (-RG_C) * r * sph
            e2 = _expm1(2.0 * la)
            parts[(d, h)] = dict(r=r, i=i, la=la, a=jnp.exp(la), e2=e2, mult=jnp.sqrt(-e2), uh=uh, sp=sph)
    return parts


def _rg_fwd_fn(ids, tin, vin):
    (u,) = tin
    wbd = vin[0]
    parts = _rg_gates(u, wbd, vin[1][...], vin[2][...])
    outs = []
    for d in range(2):
        a = jnp.concatenate([parts[(d, h)]["a"] for h in range(2)], axis=1)
        b = jnp.concatenate([parts[(d, h)]["mult"] * parts[(d, h)]["i"] * parts[(d, h)]["uh"]
                             for h in range(2)], axis=1)
        outs += [a, b]
    return outs, []


def _rg_bwd_fn(ids, tin, vin):
    u, da_f, db_f, da_r, db_r = tin
    wbd, lam = vin[0], vin[2][...]
    parts = _rg_gates(u, wbd, vin[1][...], lam)
    dab = ((da_f, db_f), (da_r, db_r))
    dsig_lam = -_sigmoid(-lam)
    du_halves, dpre_halves, dlam = [], [], [[None, None], [None, None]]
    for h in range(2):
        du = jnp.zeros_like(parts[(0, h)]["uh"])
        dpre = []
        for d in range(2):
            p = parts[(d, h)]
            da = dab[d][0][:, h * RH:(h + 1) * RH]
            db = dab[d][1][:, h * RH:(h + 1) * RH]
            d_mult = db * p["i"] * p["uh"]
            d_i = db * p["mult"] * p["uh"]
            du = du + db * p["mult"] * p["i"]
            d_la = da * p["a"] - d_mult * (p["e2"] + 1.0) / p["mult"]
            d_r = d_la * ((-RG_C) * p["sp"])
            dlam[d][h] = _sum0(d_la * ((-RG_C) * p["r"])) * dsig_lam[d:d + 1, h * RH:(h + 1) * RH]
            dpre += [d_r * p["r"] * (1.0 - p["r"]), d_i * p["i"] * (1.0 - p["i"])]
        dpre = jnp.concatenate(dpre, axis=1)
        du = du + lax.dot_general(dpre.astype(BF16), wbd[h], _NT, preferred_element_type=F32)
        du_halves.append(du)
        dpre_halves.append(dpre)
    dpre_all = jnp.concatenate(dpre_halves, axis=1)
    dlam_row = jnp.concatenate([dlam[0][0], dlam[0][1], dlam[1][0], dlam[1][1]], axis=1)
    return [dpre_all, jnp.concatenate(du_halves, axis=1)], [_sum0(dpre_all), dlam_row]


def _tile_flags(i, n_tiles, seq_starts):
    starts_here = functools.reduce(jnp.logical_or, [i == s for s in seq_starts])
    ends_here = functools.reduce(jnp.logical_or, [i + 1 == s for s in seq_starts] + [i + 1 == n_tiles])
    return jnp.logical_not(starts_here), jnp.logical_not(ends_here)


def _halo_specs(col0, cw):
    hb = ROW_TILE // HALO
    prev = pl.BlockSpec((HALO, cw), lambda i, c: (jnp.maximum(i * hb - 1, 0), col0 + c))
    cur = pl.BlockSpec((ROW_TILE, cw), lambda i, c: (i, col0 + c))
    return prev, cur, hb


def _window(prev_ref, cur_ref, next_ref, has_prev, has_next):
    prev = jnp.where(has_prev, prev_ref[...], 0.0)
    nxt = jnp.where(has_next, next_ref[...], 0.0)
    return jnp.concatenate([prev, cur_ref[...], nxt], axis=0)


def _dwconv(name, x, col0, w, bias, pad_left, seq_starts, n_ch, cw=256):
    n_rows = x.shape[0]
    n_tiles = n_rows // ROW_TILE
    n_taps = w.shape[0]
    prev_spec, cur_spec, hb = _halo_specs(col0, cw)
    last_hb = n_rows // HALO - 1
    next_spec = pl.BlockSpec((HALO, cw), lambda i, c: (jnp.minimum((i + 1) * hb, last_hb), col0 + c))

    def kern(prev_ref, cur_ref, next_ref, w_ref, b_ref, o_ref):
        has_prev, has_next = _tile_flags(pl.program_id(0), n_tiles, seq_starts)
        win = _window(prev_ref, cur_ref, next_ref, has_prev, has_next)
        wv = w_ref[...]
        acc = jnp.zeros((ROW_TILE, cw), F32) + b_ref[...]
        for k in range(n_taps):
            off = HALO + k - pad_left
            acc = acc + wv[k:k + 1, :] * win[off:off + ROW_TILE, :]
        o_ref[...] = acc

    return _pcall(
        kern, name=name, grid=(n_tiles, n_ch // cw),
        in_specs=[prev_spec, cur_spec, next_spec,
                  pl.BlockSpec((n_taps, cw), lambda i, c: (0, c)), pl.BlockSpec((1, cw), lambda i, c: (0, c))],
        out_specs=pl.BlockSpec((ROW_TILE, cw), lambda i, c: (i, c)),
        out_shape=_sds((n_rows, n_ch), F32), compiler_params=_cparams(),
    )(x, x, x, w, bias)


def _dwconv_wgrad(name, dy, x, col0, n_taps, pad_left, seq_starts, n_ch, cw=256):
    n_rows = dy.shape[0]
    n_tiles = n_rows // ROW_TILE
    n_out = -(-(n_taps + 1) // 8) * 8
    prev_spec, cur_spec, hb = _halo_specs(col0, cw)
    last_hb = n_rows // HALO - 1
    next_spec = pl.BlockSpec((HALO, cw), lambda c, i: (jnp.minimum((i + 1) * hb, last_hb), col0 + c))
    prev_spec = pl.BlockSpec((HALO, cw), lambda c, i: (jnp.maximum(i * hb - 1, 0), col0 + c))
    cur_spec = pl.BlockSpec((ROW_TILE, cw), lambda c, i: (i, col0 + c))

    def kern(dy_ref, prev_ref, cur_ref, next_ref, o_ref):
        i = pl.program_id(1)
        has_prev, has_next = _tile_flags(i, n_tiles, seq_starts)
        win = _window(prev_ref, cur_ref, next_ref, has_prev, has_next)
        dyv = dy_ref[...]
        rid = lax.broadcasted_iota(jnp.int32, (n_out, cw), 0)
        inc = jnp.where(rid == n_taps, _sum0(dyv), 0.0)
        for k in range(n_taps):
            off = HALO + k - pad_left
            inc = inc + jnp.where(rid == k, _sum0(dyv * win[off:off + ROW_TILE, :]), 0.0)

        @pl.when(i == 0)
        def _():
            o_ref[...] = jnp.zeros_like(o_ref)

        o_ref[...] += inc

    return _pcall(
        kern, name=name, grid=(n_ch // cw, n_tiles),
        in_specs=[pl.BlockSpec((ROW_TILE, cw), lambda c, i: (i, c)), prev_spec, cur_spec, next_spec],
        out_specs=pl.BlockSpec((n_out, cw), lambda c, i: (0, c)),
        out_shape=_sds((n_out, n_ch), F32), compiler_params=_cparams(),
    )(dy, x, x, x)


N_SCAN = TA // ROW_TILE


def _rev_block(j):
    return jnp.where(j == 0, 0, N_SCAN - j)


def _scan_fwd(a_f, b_f, a_r, b_r):
    fwd_spec = pl.BlockSpec((ROW_TILE, R), lambda i: (i, 0))
    rev_spec = pl.BlockSpec((ROW_TILE, R), lambda i: (_rev_block(i), 0))
    hin_spec = pl.BlockSpec((None, 1, R), lambda i: (i, 0, 0))

    def kern(af, bf, ar, br, yf, yr, hin_f, hin_r, hf_s, hr_s):
        @pl.when(pl.program_id(0) == 0)
        def _():
            hf_s[...] = jnp.zeros_like(hf_s)
            hr_s[...] = jnp.zeros_like(hr_s)

        hin_f[...] = hf_s[...]
        hin_r[...] = hr_s[...]

        def step(s8, carry):
            hf, hr = carry
            t0 = pl.multiple_of(s8 * 8, 8)
            for q in range(8):
                tf = t0 + q
                hf = af[pl.ds(tf, 1), :] * hf + bf[pl.ds(tf, 1), :]
                yf[pl.ds(tf, 1), :] = hf
                tr = ROW_TILE - 1 - tf
                hr = ar[pl.ds(tr, 1), :] * hr + br[pl.ds(tr, 1), :]
                yr[pl.ds(tr, 1), :] = hr
            return hf, hr

        hf, hr = lax.fori_loop(0, ROW_TILE // 8, step, (hf_s[...], hr_s[...]))
        hf_s[...] = hf
        hr_s[...] = hr

    return _pcall(
        kern, name="scan_fwd", grid=(N_SCAN,),
        in_specs=[fwd_spec, fwd_spec, rev_spec, rev_spec],
        out_specs=[fwd_spec, rev_spec, hin_spec, hin_spec],
        out_shape=[_sds((TA, R), F32), _sds((TA, R), F32), _sds((N_SCAN, 1, R), F32), _sds((N_SCAN, 1, R), F32)],
        scratch_shapes=[pltpu.VMEM((1, R), F32), pltpu.VMEM((1, R), F32)], compiler_params=_cparams(),
    )(a_f, b_f, a_r, b_r)


def _scan_bwd(dy, a_f, y_f, hin_f, a_r, y_r, hin_r):
    fwd_spec = pl.BlockSpec((ROW_TILE, R), lambda i: (N_SCAN - 1 - i, 0))
    rev_spec = pl.BlockSpec((ROW_TILE, R), lambda i: (_rev_block(N_SCAN - 1 - i), 0))
    hin_spec = pl.BlockSpec((None, 1, R), lambda i: (N_SCAN - 1 - i, 0, 0))
    last = ROW_TILE - 1

    def kern(dyf, af, yf, hf0, dyr, ar, yr, hr0, daf, dbf, dar, dbr, gf_s, anf_s, gr_s, anr_s):
        @pl.when(pl.program_id(0) == 0)
        def _():
            for r in (gf_s, anf_s, gr_s, anr_s):
                r[...] = jnp.zeros_like(r)

        def one(dy_ref, a_ref, y_ref, da_ref, db_ref, g, an, p, pprev):
            gnew = dy_ref[pl.ds(p, 1), :] + an * g
            db_ref[pl.ds(p, 1), :] = gnew
            da_ref[pl.ds(p, 1), :] = gnew * y_ref[pl.ds(pprev, 1), :]
            return gnew, a_ref[pl.ds(p, 1), :]

        def step(s8, carry):
            gf, anf, gr, anr = carry
            base = s8 * 8
            for q in range(8):
                s = last - (base + q)
                gf, anf = one(dyf, af, yf, daf, dbf, gf, anf, s, s - 1)
                gr, anr = one(dyr, ar, yr, dar, dbr, gr, anr, last - s, last - s + 1)
            return gf, anf, gr, anr

        carry = (gf_s[...], anf_s[...], gr_s[...], anr_s[...])
        carry = lax.fori_loop(0, ROW_TILE // 8 - 1, step, carry)
        gf, anf, gr, anr = carry
        for s in range(7, 0, -1):
            gf, anf = one(dyf, af, yf, daf, dbf, gf, anf, s, s - 1)
            gr, anr = one(dyr, ar, yr, dar, dbr, gr, anr, last - s, last - s + 1)
        gf0 = dyf[0:1, :] + anf * gf
        dbf[0:1, :] = gf0
        daf[0:1, :] = gf0 * hf0[...]
        gr0 = dyr[last:last + 1, :] + anr * gr
        dbr[last:last + 1, :] = gr0
        dar[last:last + 1, :] = gr0 * hr0[...]
        gf_s[...] = gf0
        anf_s[...] = af[0:1, :]
        gr_s[...] = gr0
        anr_s[...] = ar[last:last + 1, :]

    return _pcall(
        kern, name="scan_bwd", grid=(N_SCAN,),
        in_specs=[fwd_spec, fwd_spec, fwd_spec, hin_spec, rev_spec, rev_spec, rev_spec, hin_spec],
        out_specs=[fwd_spec, fwd_spec, rev_spec, rev_spec],
        out_shape=[_sds((TA, R), F32)] * 4,
        scratch_shapes=[pltpu.VMEM((1, R), F32)] * 4, compiler_params=_cparams(),
    )(dy, a_f, y_f, hin_f, dy, a_r, y_r, hin_r)


def _me():
    return lax.axis_index("x"), lax.axis_index("y"), lax.axis_index("c")


def _other_chips(mx, my):
    return [(1 - mx, my), (mx, 1 - my), (1 - mx, 1 - my)]


def _rcopy(src, dst, ssem, rsem, dev):
    return pltpu.make_async_remote_copy(src_ref=src, dst_ref=dst, send_sem=ssem, recv_sem=rsem,
                                        device_id=dev, device_id_type=MESH)


def _allgather8(name, x):
    rows, cols = x.shape

    def kern(x_ref, o_ref, ssem, rsem, lsem):
        mx, my, mc = _me()
        me = 4 * mx + 2 * my + mc
        peers = []
        for k in range(1, 8):
            px = 1 - mx if (k >> 2) & 1 else mx
            py = 1 - my if (k >> 1) & 1 else my
            pc = 1 - mc if k & 1 else mc
            peers.append((px, py, pc))
        mine = pltpu.make_async_copy(x_ref, o_ref.at[me], lsem)
        mine.start()
        sends = [_rcopy(x_ref, o_ref.at[me], ssem.at[k], rsem.at[k], p) for k, p in enumerate(peers)]
        for cp in sends:
            cp.start()
        for k, (px, py, pc) in enumerate(peers):
            _rcopy(x_ref, o_ref.at[4 * px + 2 * py + pc], ssem.at[k], rsem.at[k], (px, py, pc)).wait_recv()
        for cp in sends:
            cp.wait_send()
        mine.wait()

    return _pcall(
        kern, name=name, in_specs=[ANY], out_specs=ANY, out_shape=_sds((8, rows, cols), F32),
        scratch_shapes=[pltpu.SemaphoreType.DMA((7,)), pltpu.SemaphoreType.DMA((7,)), pltpu.SemaphoreType.DMA(())],
    )(x)


def _gather_weights(ws):
    n = len(ws)

    def kern(*refs):
        o = refs[n:2 * n]
        s1, r1, s2, r2 = refs[2 * n:]
        mx, my, mc = _me()
        j0 = 2 * mx + my
        chips = _other_chips(mx, my)
        sib = (mx, my, 1 - mc)
        firsts = []
        for t in range(n):
            for q, (qx, qy) in enumerate(chips):
                cp = _rcopy(o[t].at[j0, mc], o[t].at[j0, mc], s1.at[3 * t + q], r1.at[3 * t + q], (qx, qy, mc))
                cp.start()
                firsts.append(cp)
        passed = []
        for t in range(n):
            for q, (qx, qy) in enumerate(chips):
                jq = 2 * qx + qy
                _rcopy(o[t].at[jq, mc], o[t].at[jq, mc], s1.at[3 * t + q], r1.at[3 * t + q], (qx, qy, mc)).wait_recv()
                fw = _rcopy(o[t].at[jq, mc], o[t].at[jq, mc], s2.at[3 * t + q], r2.at[3 * t + q], sib)
                fw.start()
                passed.append(fw)
        for t in range(n):
            for q, (qx, qy) in enumerate(chips):
                jq = 2 * qx + qy
                _rcopy(o[t].at[jq, 1 - mc], o[t].at[jq, 1 - mc], s2.at[3 * t + q], r2.at[3 * t + q], sib).wait_recv()
        for cp in firsts + passed:
            cp.wait_send()

    dma = pltpu.SemaphoreType.DMA
    return _pcall(
        kern, name="gather_weights", in_specs=[ANY] * n, out_specs=[ANY] * n,
        out_shape=[_sds(w.shape, w.dtype) for w in ws], input_output_aliases={t: t for t in range(n)},
        scratch_shapes=[dma((3 * n,)), dma((3 * n,)), dma((3 * n,)), dma((3 * n,))],
    )(*ws)


def _reduce_pair(gs):
    n = len(gs)

    def kern(*refs):
        g, o = refs[:n], refs[n:2 * n]
        ss, rs = refs[2 * n:]
        mx, my, mc = _me()
        sib = (mx, my, 1 - mc)
        sends = []
        for t in range(n):
            for j in range(4):
                cp = _rcopy(g[t].at[j, 1 - mc], o[t].at[j], ss.at[4 * t + j], rs.at[4 * t + j], sib)
                cp.start()
                sends.append(cp)
        for cp in sends:
            cp.wait_recv()
        for cp in sends:
            cp.wait_send()

    dma = pltpu.SemaphoreType.DMA
    return _pcall(
        kern, name="reduce_pair", in_specs=[ANY] * n, out_specs=[ANY] * n,
        out_shape=[_sds((4,) + g.shape[2:], g.dtype) for g in gs],
        scratch_shapes=[dma((4 * n,)), dma((4 * n,))],
    )(*gs)


def _reduce_chips(ss_):
    n = len(ss_)

    def kern(*refs):
        s, o = refs[:n], refs[n:2 * n]
        ssem, rsem = refs[2 * n:]
        mx, my, mc = _me()
        chips = _other_chips(mx, my)
        sends = []
        for t in range(n):
            for q, (qx, qy) in enumerate(chips):
                cp = _rcopy(s[t].at[2 * qx + qy], o[t].at[q], ssem.at[3 * t + q], rsem.at[3 * t + q], (qx, qy, mc))
                cp.start()
                sends.append(cp)
        for cp in sends:
            cp.wait_recv()
        for cp in sends:
            cp.wait_send()

    dma = pltpu.SemaphoreType.DMA
    return _pcall(
        kern, name="reduce_chips", in_specs=[ANY] * n, out_specs=[ANY] * n,
        out_shape=[_sds((3,) + s.shape[1:], s.dtype) for s in ss_],
        scratch_shapes=[dma((3 * n,)), dma((3 * n,))],
    )(*ss_)


def _share_halves(fulls):
    n = len(fulls)

    def kern(*refs):
        o = refs[n:2 * n]
        ss, rs = refs[2 * n:]
        mx, my, mc = _me()
        sib = (mx, my, 1 - mc)
        sends = []
        for t in range(n):
            cp = _rcopy(o[t].at[mc], o[t].at[mc], ss.at[t], rs.at[t], sib)
            cp.start()
            sends.append(cp)
        for t in range(n):
            _rcopy(o[t].at[1 - mc], o[t].at[1 - mc], ss.at[t], rs.at[t], sib).wait_recv()
        for cp in sends:
            cp.wait_send()

    dma = pltpu.SemaphoreType.DMA
    return _pcall(
        kern, name="share_halves", in_specs=[ANY] * n, out_specs=[ANY] * n,
        out_shape=[_sds(f.shape, f.dtype) for f in fulls], input_output_aliases={t: t for t in range(n)},
        scratch_shapes=[dma((n,)), dma((n,))],
    )(*fulls)


def _tiled_sp(name, fn, grid, sp, ins, outs):
    n_in = len(ins)

    def kern(sp_ref, *refs):
        tout = fn([r[...] for r in refs[:n_in]])
        for r, v in zip(refs[n_in:], tout):
            r[...] = v.astype(r.dtype)

    gs = pltpu.PrefetchScalarGridSpec(num_scalar_prefetch=1, grid=tuple(grid),
                                      in_specs=[s for _, s in ins], out_specs=[s for _, s in outs])
    res = _pcall(kern, name=name, grid_spec=gs, out_shape=[o for o, _ in outs], compiler_params=_cparams(),
                 )(sp, *[a for a, _ in ins])
    return list(res)


def _row_tile(rows, cols, itemsize=4, budget=2 * 1024 * 1024):
    tr = rows
    while tr * cols * itemsize > budget and tr % 32 == 0:
        tr //= 2
    return tr


def _place_big(shards, place):
    slots = []
    for t, s in enumerate(shards):
        rr, cc = s.shape[1], s.shape[2]
        tr = _row_tile(rr, cc)
        (slot,) = _tiled_sp(
            f"place{t}", lambda tin: [tin[0]], (2, rr // tr), place,
            [(s, pl.BlockSpec((None, tr, cc), lambda h, i, sp: (h, i, 0)))],
            [(_sds((4, 2, rr, cc), BF16), pl.BlockSpec((None, None, tr, cc), lambda h, i, sp: (sp[0], h, i, 0)))])
        slots.append(slot)
    return slots


def _gather_big(shards, place):
    return _gather_weights(_place_big(shards, place))


SEM = pl.BlockSpec(memory_space=pltpu.SEMAPHORE)
_DATAFLOW = pltpu.SideEffectType.DATAFLOW_SIDE_EFFECTING


def _gather_start(slots, groups):
    n = len(slots)

    def kern(*refs):
        o = refs[n:2 * n]
        sems = refs[2 * n:]
        mx, my, mc = _me()
        j0 = 2 * mx + my
        for gi, grp in enumerate(groups):
            for k, t in enumerate(grp):
                for q, (qx, qy) in enumerate(_other_chips(mx, my)):
                    _rcopy(o[t].at[j0, mc], o[t].at[j0, mc], sems[2 * gi].at[3 * k + q],
                           sems[2 * gi + 1].at[3 * k + q], (qx, qy, mc)).start()

    sem_shapes = []
    for grp in groups:
        sem_shapes += [pltpu.SemaphoreType.DMA((3 * len(grp),))] * 2
    res = _pcall(
        kern, name="gather_start", in_specs=[ANY] * n, out_specs=[ANY] * n + [SEM] * len(sem_shapes),
        out_shape=[_sds(w.shape, w.dtype) for w in slots] + sem_shapes,
        input_output_aliases={t: t for t in range(n)},
        compiler_params=pltpu.CompilerParams(has_side_effects=_DATAFLOW),
    )(*slots)
    return list(res[:n]), list(res[n:])


def _gather_wait(name, bufs, ssem, rsem, after):
    n = len(bufs)

    def kern(*refs):
        b = refs[:n]
        ssem_ref, rsem_ref = refs[n], refs[n + 1]
        mx, my, mc = _me()
        j0 = 2 * mx + my
        for k in range(n):
            for q, (qx, qy) in enumerate(_other_chips(mx, my)):
                jq = 2 * qx + qy
                _rcopy(b[k].at[jq, mc], b[k].at[jq, mc], ssem_ref.at[3 * k + q], rsem_ref.at[3 * k + q],
                       (qx, qy, mc)).wait_recv()
                _rcopy(b[k].at[j0, mc], b[k].at[j0, mc], ssem_ref.at[3 * k + q], rsem_ref.at[3 * k + q],
                       (qx, qy, mc)).wait_send()

    return list(_pcall(
        kern, name=name, in_specs=[ANY] * n + [SEM, SEM, ANY], out_specs=[ANY] * n,
        out_shape=[_sds(w.shape, w.dtype) for w in bufs], input_output_aliases={k: k for k in range(n)},
        compiler_params=pltpu.CompilerParams(has_side_effects=_DATAFLOW),
    )(*bufs, ssem, rsem, after))


def _swap_halves(name, bufs):
    n = len(bufs)

    def kern(*refs):
        o = refs[n:2 * n]
        ss, rs = refs[2 * n:]
        mx, my, mc = _me()
        sib = (mx, my, 1 - mc)
        sends = []
        for k in range(n):
            for q, (qx, qy) in enumerate(_other_chips(mx, my)):
                jq = 2 * qx + qy
                cp = _rcopy(o[k].at[jq, mc], o[k].at[jq, mc], ss.at[3 * k + q], rs.at[3 * k + q], sib)
                cp.start()
                sends.append(cp)
        for k in range(n):
            for q, (qx, qy) in enumerate(_other_chips(mx, my)):
                jq = 2 * qx + qy
                _rcopy(o[k].at[jq, 1 - mc], o[k].at[jq, 1 - mc], ss.at[3 * k + q], rs.at[3 * k + q], sib).wait_recv()
        for cp in sends:
            cp.wait_send()

    dma = pltpu.SemaphoreType.DMA
    return list(_pcall(
        kern, name=name, in_specs=[ANY] * n, out_specs=[ANY] * n,
        out_shape=[_sds(w.shape, w.dtype) for w in bufs], input_output_aliases={k: k for k in range(n)},
        scratch_shapes=[dma((3 * n,)), dma((3 * n,))],
    )(*bufs))


def _reduce_big(parts, place):
    theirs = _reduce_pair(parts)
    sums = []
    for t, (p, o) in enumerate(zip(parts, theirs)):
        rr, cc = p.shape[2], p.shape[3]
        tr = _row_tile(rr, cc)
        (s_t,) = _tiled_sp(
            f"pair_add{t}", lambda tin: [tin[0].astype(F32) + tin[1].astype(F32)], (4, rr // tr), place,
            [(p, pl.BlockSpec((None, None, tr, cc), lambda j, i, sp: (j, sp[1], i, 0))),
             (o, pl.BlockSpec((None, tr, cc), lambda j, i, sp: (j, i, 0)))],
            [(_sds((4, rr, cc), BF16), pl.BlockSpec((None, tr, cc), lambda j, i, sp: (j, i, 0)))])
        sums.append(s_t)
    landed = _reduce_chips(sums)
    fulls = []
    for t, (s, q) in enumerate(zip(sums, landed)):
        rr, cc = q.shape[1], q.shape[2]
        tr = _row_tile(rr, cc)

        def add4(tin):
            return [((tin[0].astype(F32) + tin[1].astype(F32)) + tin[2].astype(F32)) + tin[3].astype(F32)]

        ins = [(s, pl.BlockSpec((None, tr, cc), lambda i, sp: (sp[0], i, 0)))]
        ins += [(q, pl.BlockSpec((None, tr, cc), lambda i, sp, _k=k: (_k, i, 0))) for k in range(3)]
        (f_t,) = _tiled_sp(f"chip_add{t}", add4, (rr // tr,), place, ins,
                           [(_sds((2, rr, cc), F32), pl.BlockSpec((None, tr, cc), lambda i, sp: (sp[1], i, 0)))])
        fulls.append(f_t)
    return _share_halves(fulls)


def _pack(parts, PACK_ROWS=PACK_ROWS):
    flat, offs, pos = [], [], 0
    for p in parts:
        v = p.reshape(-1).astype(F32)
        n = -(-v.shape[0] // LANE) * LANE
        flat.append(jnp.pad(v, (0, n - v.shape[0])))
        offs.append((pos, v.shape[0], p.shape))
        pos += n
    total = -(-pos // (PACK_ROWS * LANE)) * PACK_ROWS * LANE
    flat.append(jnp.zeros((total - pos,), F32))
    return jnp.concatenate(flat).reshape(-1, LANE), offs


def _unpack(vec, offs):
    v = vec.reshape(-1)
    return [v[p:p + n].reshape(shape) for p, n, shape in offs]


def _sum_devices(name, g8):
    rows = g8.shape[1]
    tr = 8
    for cand in (1024, 512, 256, 128, 64, 32, 16, 8):
        if rows % cand == 0:
            tr = cand
            break

    def fn(ids, tin, vin):
        acc = tin[0]
        for v in tin[1:]:
            acc = acc + v
        return [acc], []

    ins = [(g8, pl.BlockSpec((None, tr, LANE), lambda i, _d=d: (_d, i, 0))) for d in range(8)]
    out = (_sds((rows, LANE), F32), pl.BlockSpec((tr, LANE), lambda i: (i, 0)))
    return _tiled(name, fn, (rows // tr,), ins, [], [out])[0]


def _adamw(name, w, g, m, v):
    rows, cols = w.shape
    tr = rows
    for cand in (512, 256, 128, 64, 32, 16, 8):
        if rows % cand == 0 and cand * cols * 4 <= 2 * 1024 * 1024:
            tr = cand
            break
    bc1 = 1.0 - ADAM_B1 ** ADAM_STEP
    bc2 = 1.0 - ADAM_B2 ** ADAM_STEP

    def fn(ids, tin, vin):
        wv, gv, mv, vv = tin
        mn = ADAM_B1 * mv + (1.0 - ADAM_B1) * gv
        vn = ADAM_B2 * vv + (1.0 - ADAM_B2) * (gv * gv)
        delta = -ADAM_LR * ((mn / bc1) / (jnp.sqrt(vn / bc2) + ADAM_EPS) + ADAM_WD * wv)
        return [delta, mn, vn], []

    spec = pl.BlockSpec((tr, cols), lambda i: (i, 0))
    outs = [(_sds((rows, cols), F32), spec)] * 3
    return _tiled(name, fn, (rows // tr,), [(a, spec) for a in (w, g, m, v)], [], outs)


def _pos_embed():
    n_rows = T // GRID_W
    q = D // 4
    omega = 1.0 / (10000.0 ** (jnp.arange(q, dtype=F32) / q))
    er = jnp.arange(n_rows, dtype=jnp.int32).astype(F32)[:, None] * omega[None, :]
    ec = jnp.arange(GRID_W, dtype=jnp.int32).astype(F32)[:, None] * omega[None, :]
    by_row = jnp.concatenate([jnp.sin(er), jnp.cos(er)], axis=-1)
    by_col = jnp.concatenate([jnp.sin(ec), jnp.cos(ec)], axis=-1)
    return jnp.concatenate([jnp.repeat(by_row, GRID_W, axis=0), jnp.tile(by_col, (n_rows, 1))], axis=-1)


def _dense_gates(w_a, w_x):
    per = N_BLK // 2
    kinds = jnp.stack([src.reshape(2, per, BLK, BLK) for src in (w_a[0], w_x[0], w_a[1], w_x[1])], axis=3)
    on_diag = jnp.eye(per, dtype=bool)[None, :, None, None, :, None]
    out = jnp.where(on_diag, kinds[:, :, :, :, None, :], 0.0)
    return out.reshape(2, RH, NQ).astype(BF16)


def _gate_block_grads(dwbd):
    per = N_BLK // 2
    d6 = dwbd.reshape(2, per, BLK, 4, per, BLK)
    kinds = []
    for q in range(4):
        kinds.append(jnp.stack([d6[:, nb, :, q, nb, :] for nb in range(per)], axis=1).reshape(N_BLK, BLK, BLK))
    return jnp.stack([kinds[0], kinds[2]]), jnp.stack([kinds[1], kinds[3]])


def _gate_bias_dense(b_a, b_x):
    cols = []
    for h in range(2):
        for src in (b_a[0], b_x[0], b_a[1], b_x[1]):
            cols.append(src.reshape(R)[h * RH:(h + 1) * RH])
    return jnp.concatenate(cols).reshape(1, 2 * NQ)


def _gate_bias_grads(dgb):
    v = dgb.reshape(2, 4, RH)
    kinds = [jnp.concatenate([v[0, q], v[1, q]]).reshape(N_BLK, BLK) for q in range(4)]
    return jnp.stack([kinds[0], kinds[2]]), jnp.stack([kinds[1], kinds[3]])


def _mlp_fwd(tag, x_in, g_norm, sh, sc, gate, w_in, w_out):
    n_t = T // ROW_TILE
    (h,) = _tiled(f"{tag}_norm", lambda ids, t, v: ([_norm_mod(t[0], v[0], v[1], v[2])], []), (n_t,),
                  [_rows(x_in)], [g_norm, sc, sh], [_orow(T, D, BF16)])
    tm = 512
    (r,) = _mm(f"{tag}_in", h, w_in, _NN, (T // tm, 4, 1),
               pl.BlockSpec((tm, D), lambda i, j, k: (i, 0)), pl.BlockSpec((None, D, D), lambda i, j, k: (j, 0, 0)),
               [(_sds((T, FF), BF16), pl.BlockSpec((tm, D), lambda i, j, k: (i, j)))], (tm, D),
               epi=lambda acc, ex: [jnp.maximum(acc, 0.0)])
    o, x_out = _mm(f"{tag}_out", r, w_out, _NN, (T // tm, 1, FF // D),
                   pl.BlockSpec((tm, D), lambda i, j, k: (i, k)), pl.BlockSpec((D, D), lambda i, j, k: (k, 0)),
                   [(_sds((T, D), F32), pl.BlockSpec((tm, D), lambda i, j, k: (i, 0)))] * 2, (tm, D),
                   extra=[(x_in, pl.BlockSpec((tm, D), lambda i, j, k: (i, 0))), (gate, _full_spec(gate))],
                   a_pre=lambda a: a * a, epi=lambda acc, ex: [acc, ex[0] + ex[1] * acc])
    return dict(h=h, r=r, o=o, x_in=x_in), x_out


def _gate_bwd(tag, dx, o, gate):
    def fn(ids, t, v):
        d_o = t[0] * v[0]
        return [d_o], [_sum0(t[0] * t[1]), _sum0(d_o)]
    return _tiled(f"{tag}_gate_bwd", fn, (T // ROW_TILE,), [_rows(dx), _rows(o)], [gate],
                  [_orow(T, D, BF16)], [(1, D), (1, D)])


def _norm_bwd(tag, dx_res, dh, dh_off, x, g_norm, sc, with_dx=True):
    n_t = x.shape[0] // ROW_TILE

    def fn(ids, t, v):
        if with_dx:
            dres, dhv, xv = t
        else:
            dhv, xv = t
        dxv, d_sh, d_sc, d_g = _norm_mod_bwd(dhv, xv, v[0], v[1])
        return ([dres + dxv] if with_dx else []), [d_sh, d_sc, d_g]

    ins = ([_rows(dx_res)] if with_dx else []) + [_rows(dh, off=dh_off), _rows(x)]
    outs = [_orow(x.shape[0], D, F32)] if with_dx else []
    return _tiled(f"{tag}_norm_bwd", fn, (n_t,), ins, [g_norm, sc], outs, [(1, D)] * 3)


def _mlp_bwd(tag, dx, saved, g_norm, sc, gate, w_in, w_out):
    d_o, d_gate, _ = _gate_bwd(tag, dx, saved["o"], gate)
    tm = 512
    r = saved["r"]
    (da,) = _mm(f"{tag}_dz", d_o, w_out, _NT, (T // tm, FF // D, 1),
                pl.BlockSpec((tm, D), lambda i, j, k: (i, 0)), pl.BlockSpec((D, D), lambda i, j, k: (j, 0)),
                [(_sds((T, FF), BF16), pl.BlockSpec((tm, D), lambda i, j, k: (i, j)))], (tm, D),
                extra=[(r, pl.BlockSpec((tm, D), lambda i, j, k: (i, j)))],
                epi=lambda acc, ex: [acc * (2.0 * ex[0].astype(F32))])
    tk = 512
    (dw_out,) = _mm(f"{tag}_dwout", r, d_o, _TN, (FF // tm, 1, T // tk),
                    pl.BlockSpec((tk, tm), lambda i, j, k: (k, i)), pl.BlockSpec((tk, D), lambda i, j, k: (k, 0)),
                    [(_sds((FF, D), BF16), pl.BlockSpec((tm, D), lambda i, j, k: (i, 0)))], (tm, D),
                    a_pre=lambda a: a * a)
    (dh,) = _mm(f"{tag}_dh", da, w_in, _NT, (T // tm, 1, 4),
                pl.BlockSpec((tm, D), lambda i, j, k: (i, k)), pl.BlockSpec((None, D, D), lambda i, j, k: (k, 0, 0)),
                [(_sds((T, D), F32), pl.BlockSpec((tm, D), lambda i, j, k: (i, 0)))], (tm, D))
    (dw_in,) = _mm(f"{tag}_dwin", saved["h"], da, _TN, (D // tm, 4, T // tk),
                   pl.BlockSpec((tk, tm), lambda i, j, k: (k, i)), pl.BlockSpec((tk, D), lambda i, j, k: (k, j)),
                   [(_sds((4, D, D), BF16), pl.BlockSpec((None, tm, D), lambda i, j, k: (j, i, 0)))], (tm, D))
    dx_in, d_sh, d_sc, d_g = _norm_bwd(tag, dx, dh, 0, saved["x_in"], g_norm, sc)
    return dx_in, dw_in, dw_out, dict(sh=d_sh, sc=d_sc, gate=d_gate, g_norm=d_g)


def _local_step(x, ctx, tgt, mods, cmods, norm_g, final_g, rec, conf, wg):
    n_t = T // ROW_TILE
    row = lambda v: v.reshape(1, -1)
    m0 = [row(mods[0, q]) for q in range(6)]
    m1 = [row(mods[1, q]) for q in range(6)]
    g00, g01, g10, g11 = (row(norm_g[0, 0]), row(norm_g[0, 1]), row(norm_g[1, 0]), row(norm_g[1, 1]))
    csh, csc = row(cmods[0]), row(cmods[1])
    pos = _pos_embed()

    def prep0(ids, t, v):
        cx, xv, pv = t
        is_ctx = ids[0] == 0
        xin = jnp.where(is_ctx, cx, xv + pv)
        sh = jnp.where(is_ctx, v[3], v[1])
        sc = jnp.where(is_ctx, v[4], v[2])
        return [_norm_mod(xin, v[0], sc, sh), xv + pv], []

    hcat, x0 = _tiled(
        "prep0", prep0, (N_SCAN,),
        [(ctx, pl.BlockSpec((ROW_TILE, D), lambda i: (0, 0))), _rows(x, off=-1, clamp_lo=True),
         _rows(pos, off=-1, clamp_lo=True)],
        [g00, m0[0], m0[1], csh, csc],
        [_orow(TA, D, BF16), _orow(T, D, F32, off=-1, clamp_lo=True)])

    tm_a = 768
    w_rec = wg("rec", hcat)
    (a_in,) = _mm("rec_in", hcat, w_rec["rec_w_in"], _NN, (TA // tm_a, 4, 1),
                  pl.BlockSpec((tm_a, D), lambda i, j, k: (i, 0)),
                  pl.BlockSpec((None, D, RH), lambda i, j, k: (j, 0, 0)),
                  [(_sds((TA, 2 * R), F32), pl.BlockSpec((tm_a, RH), lambda i, j, k: (i, j)))], (tm_a, RH))
    cw = 256
    rec_starts = (0, 1)
    u = _dwconv("rec_conv", a_in, R // cw, rec["conv_w"], row(rec["conv_b"]), 1, rec_starts, R, cw)
    wbd = _dense_gates(rec["w_a"], rec["w_x"])
    gbias = _gate_bias_dense(rec["b_a"], rec["b_x"])
    lam = rec["lam"]
    a_f, b_f, a_r, b_r = _tiled("rg_fwd", _rg_fwd_fn, (TA // RG_TILE,), [_rows(u, tm=RG_TILE)], [wbd, gbias, lam],
                                [_orow(TA, R, F32, tm=RG_TILE)] * 4, vec_refs=True)
    y_f, y_r, hin_f, hin_r = _scan_fwd(a_f, b_f, a_r, b_r)

    def rec_mid(ids, t, v):
        gp, yf, yr = t
        g, _ = _gelu(gp)
        return [g * (yf + yr)], []

    (m_rec,) = _tiled("rec_mid", rec_mid, (n_t,),
                      [_rows(a_in, R, off=1), _rows(y_f, off=1), _rows(y_r, off=1)], [], [_orow(T, R, BF16)])
    tm = 512
    o_rec, x1 = _mm("rec_out", m_rec, w_rec["rec_w_out"], _NN, (T // tm, 1, 1),
                    pl.BlockSpec((tm, R), lambda i, j, k: (i, 0)), pl.BlockSpec((R, D), lambda i, j, k: (0, 0)),
                    [(_sds((T, D), F32), pl.BlockSpec((tm, D), lambda i, j, k: (i, 0)))] * 2, (tm, D),
                    extra=[(x0, pl.BlockSpec((tm, D), lambda i, j, k: (i, 0))), (m0[2], _full_spec(m0[2]))],
                    epi=lambda acc, ex: [acc, ex[0] + ex[1] * acc])
    w_m0 = wg("mlp0", x1)
    mlp0, x2 = _mlp_fwd("mlp0", x1, g01, m0[3], m0[4], m0[5], w_m0["w_in"], w_m0["w_out"])

    (h1,) = _tiled("conf_norm", lambda ids, t, v: ([_norm_mod(t[0], v[0], v[1], v[2])], []), (n_t,),
                   [_rows(x2)], [g10, m1[1], m1[0]], [_orow(T, D, BF16)])
    b_pw1 = row(conf["b_pw1"])
    w_cf = wg("conf", x2)
    (pre,) = _mm("conf_pw1", h1, w_cf["conf_w_pw1"], _NN, (T // tm, 4, 1),
                 pl.BlockSpec((tm, D), lambda i, j, k: (i, 0)),
                 pl.BlockSpec((None, D, D // 2), lambda i, j, k: (j, 0, 0)),
                 [(_sds((T, 2 * D), F32), pl.BlockSpec((tm, D // 2), lambda i, j, k: (i, j)))], (tm, D // 2),
                 extra=[(b_pw1, pl.BlockSpec((1, D // 2), lambda i, j, k: (0, j)))],
                 epi=lambda acc, ex: [acc + ex[0]])
    (zg,) = _tiled("conf_glu", lambda ids, t, v: ([t[0] * _sigmoid(t[1])], []), (n_t,),
                   [_rows(pre, D, col=0), _rows(pre, D, col=1)], [], [_orow(T, D, F32)])
    conf_starts = (0,)
    zc = _dwconv("conf_conv", zg, 0, conf["conv_w"], row(conf["conv_b"]), CONF_KW // 2, conf_starts, D, cw)
    ln_g, ln_b = row(conf["ln_g"]), row(conf["ln_b"])

    def ln_silu(ids, t, v):
        nh, _ = _layernorm_parts(t[0])
        ln = nh * v[0] + v[1]
        return [ln * _sigmoid(ln)], []

    (s_conf,) = _tiled("conf_ln", ln_silu, (n_t,), [_rows(zc)], [ln_g, ln_b], [_orow(T, D, BF16)])
    b_pw2 = row(conf["b_pw2"])
    y_conf, x3 = _mm("conf_pw2", s_conf, w_cf["conf_w_pw2"], _NN, (T // tm, 1, 1),
                     pl.BlockSpec((tm, D), lambda i, j, k: (i, 0)), pl.BlockSpec((D, D), lambda i, j, k: (0, 0)),
                     [(_sds((T, D), F32), pl.BlockSpec((tm, D), lambda i, j, k: (i, 0)))] * 2, (tm, D),
                     extra=[(x2, pl.BlockSpec((tm, D), lambda i, j, k: (i, 0))), (m1[2], _full_spec(m1[2])),
                            (b_pw2, _full_spec(b_pw2))],
                     epi=lambda acc, ex: [acc + ex[2], ex[0] + ex[1] * (acc + ex[2])])
    w_m1 = wg("mlp1", x3)
    mlp1, x4 = _mlp_fwd("mlp1", x3, g11, m1[3], m1[4], m1[5], w_m1["w_in"], w_m1["w_out"])

    fg = row(final_g)

    def head(ids, t, v):
        n, r = _rms(t[0])
        err = n * v[0] - t[1]
        d_out = err * (1.0 / D)
        dn = d_out * v[0]
        dxv = r * (dn - n * jnp.mean(dn * n, axis=-1, keepdims=True))
        part = jnp.sum(_sum0(err * err), axis=1, keepdims=True) * (0.5 / D)
        return [dxv], [part, _sum0(d_out * n)]

    dx4, loss, d_fg = _tiled("head", head, (n_t,), [_rows(x4), _rows(tgt)], [fg], [_orow(T, D, F32)],
                             [(1, 1), (1, D)])

    dx3, dw_in1, dw_out1, dm_mlp1 = _mlp_bwd("mlp1", dx4, mlp1, g11, m1[4], m1[5],
                                             w_m1["w_in"], w_m1["w_out"])
    d_y, d_g1c, d_bpw2 = _gate_bwd("conf", dx3, y_conf, m1[2])
    tk = 512
    (dw_pw2,) = _mm("conf_dwpw2", s_conf, d_y, _TN, (D // tm, 1, T // tk),
                    pl.BlockSpec((tk, tm), lambda i, j, k: (k, i)), pl.BlockSpec((tk, D), lambda i, j, k: (k, 0)),
                    [(_sds((D, D), BF16), pl.BlockSpec((tm, D), lambda i, j, k: (i, 0)))], (tm, D))
    (ds,) = _mm("conf_ds", d_y, w_cf["conf_w_pw2"], _NT, (T // tm, 1, 1),
                pl.BlockSpec((tm, D), lambda i, j, k: (i, 0)), pl.BlockSpec((D, D), lambda i, j, k: (0, 0)),
                [(_sds((T, D), F32), pl.BlockSpec((tm, D), lambda i, j, k: (i, 0)))], (tm, D))

    def ln_silu_bwd(ids, t, v):
        dsv, zcv = t
        nh, rstd = _layernorm_parts(zcv)
        ln = nh * v[0] + v[1]
        sg = _sigmoid(ln)
        d_ln = dsv * (sg * (1.0 + ln * (1.0 - sg)))
        d_nh = d_ln * v[0]
        d_zc = rstd * (d_nh - jnp.mean(d_nh, axis=-1, keepdims=True)
                       - nh * jnp.mean(d_nh * nh, axis=-1, keepdims=True))
        return [d_zc], [_sum0(d_ln * nh), _sum0(d_ln)]

    d_zc, d_lng, d_lnb = _tiled("conf_ln_bwd", ln_silu_bwd, (n_t,), [_rows(ds), _rows(zc)], [ln_g, ln_b],
                                [_orow(T, D, F32)], [(1, D), (1, D)])
    d_zg = _dwconv("conf_conv_dx", d_zc, 0, conf["conv_w"][::-1], jnp.zeros((1, D), F32),
                   CONF_KW - 1 - CONF_KW // 2, conf_starts, D, cw)
    d_cw_conf = _dwconv_wgrad("conf_conv_dw", d_zc, zg, 0, CONF_KW, CONF_KW // 2, conf_starts, D, cw)

    def glu_bwd(ids, t, v):
        dz, pa, pb = t
        sg = _sigmoid(pb)
        d_a = dz * sg
        d_b = dz * pa * sg * (1.0 - sg)
        return [d_a, d_b], [_sum0(d_a), _sum0(d_b)]

    d_pre_a, d_pre_b, d_b1a, d_b1b = _tiled(
        "conf_glu_bwd", glu_bwd, (n_t,), [_rows(d_zg), _rows(pre, D, col=0), _rows(pre, D, col=1)], [],
        [_orow(T, D, BF16), _orow(T, D, BF16)], [(1, D), (1, D)])
    d_pre = jnp.concatenate([d_pre_a, d_pre_b], axis=1)
    (dw_pw1,) = _mm("conf_dwpw1", h1, d_pre, _TN, (D // tm, 4, T // tk),
                    pl.BlockSpec((tk, tm), lambda i, j, k: (k, i)),
                    pl.BlockSpec((tk, D // 2), lambda i, j, k: (k, j)),
                    [(_sds((4, D, D // 2), BF16), pl.BlockSpec((None, tm, D // 2), lambda i, j, k: (j, i, 0)))],
                    (tm, D // 2))
    (dh1,) = _mm("conf_dh", d_pre, w_cf["conf_w_pw1"], _NT, (T // tm, 1, 4),
                 pl.BlockSpec((tm, D // 2), lambda i, j, k: (i, k)),
                 pl.BlockSpec((None, D, D // 2), lambda i, j, k: (k, 0, 0)),
                 [(_sds((T, D), F32), pl.BlockSpec((tm, D), lambda i, j, k: (i, 0)))], (tm, D))
    dx2, d_sh1c, d_sc1c, d_g10 = _norm_bwd("conf", dx3, dh1, 0, x2, g10, m1[1])

    dx1, dw_in0, dw_out0, dm_mlp0 = _mlp_bwd("mlp0", dx2, mlp0, g01, m0[4], m0[5],
                                             w_m0["w_in"], w_m0["w_out"])
    d_orec, d_g1r, _ = _gate_bwd("rec", dx1, o_rec, m0[2])
    (dw_rout,) = _mm("rec_dwout", m_rec, d_orec, _TN, (R // RH, 1, T // tk),
                     pl.BlockSpec((tk, RH), lambda i, j, k: (k, i)), pl.BlockSpec((tk, D), lambda i, j, k: (k, 0)),
                     [(_sds((R, D), BF16), pl.BlockSpec((RH, D), lambda i, j, k: (i, 0)))], (RH, D))
    (dm_rec,) = _mm("rec_dm", d_orec, w_rec["rec_w_out"], _NT, (T // tm, 1, 1),
                    pl.BlockSpec((tm, D), lambda i, j, k: (i, 0)), pl.BlockSpec((R, D), lambda i, j, k: (0, 0)),
                    [(_sds((T, R), F32), pl.BlockSpec((tm, R), lambda i, j, k: (i, 0)))], (tm, R))

    def rec_mid_bwd(ids, t, v):
        dmv, gp, yf, yr = t
        g, th = _gelu(gp)
        lat = ids[0] > 0
        d_gp = jnp.where(lat, dmv * (yf + yr) * _gelu_grad(gp, th), 0.0)
        dy = jnp.where(lat, dmv * g, 0.0)
        return [d_gp, dy], []

    d_gp, dy = _tiled("rec_mid_bwd", rec_mid_bwd, (N_SCAN,),
                      [_rows(dm_rec, off=-1, clamp_lo=True), _rows(a_in, R), _rows(y_f), _rows(y_r)], [],
                      [_orow(TA, R, BF16), _orow(TA, R, F32)])
    da_f, db_f, da_r, db_r = _scan_bwd(dy, a_f, y_f, hin_f, a_r, y_r, hin_r)
    d_gpre, d_u, d_gbias, d_lam = _tiled(
        "rg_bwd", _rg_bwd_fn, (TA // RG_TILE,), [_rows(a, tm=RG_TILE) for a in (u, da_f, db_f, da_r, db_r)],
        [wbd, gbias, lam], [_orow(TA, 2 * NQ, BF16, tm=RG_TILE), _orow(TA, R, F32, tm=RG_TILE)],
        [(1, 2 * NQ), (1, 2 * R)], vec_refs=True)
    tk_a = 768
    (d_wbd,) = _mm("rg_dw", u, d_gpre, _TN, (2, 2, TA // tk_a),
                   pl.BlockSpec((tk_a, RH), lambda i, j, k: (k, i)),
                   pl.BlockSpec((tk_a, NQ // 2), lambda i, j, k: (k, 2 * i + j)),
                   [(_sds((2, RH, NQ), F32), pl.BlockSpec((None, RH, NQ // 2), lambda i, j, k: (i, 0, j)))],
                   (RH, NQ // 2))
    d_p = _dwconv("rec_conv_dx", d_u, 0, rec["conv_w"][::-1], jnp.zeros((1, R), F32), REC_KW - 1 - 1,
                  rec_starts, R, cw)
    d_cw_rec = _dwconv_wgrad("rec_conv_dw", d_u, a_in, R // cw, REC_KW, 1, rec_starts, R, cw)
    d_a = jnp.concatenate([d_gp, d_p.astype(BF16)], axis=1)
    (dw_rin,) = _mm("rec_dwin", hcat, d_a, _TN, (D // tm, 4, TA // tk_a),
                    pl.BlockSpec((tk_a, tm), lambda i, j, k: (k, i)), pl.BlockSpec((tk_a, RH), lambda i, j, k: (k, j)),
                    [(_sds((4, D, RH), BF16), pl.BlockSpec((None, tm, RH), lambda i, j, k: (j, i, 0)))], (tm, RH))
    (dhcat,) = _mm("rec_dh", d_a, w_rec["rec_w_in"], _NT, (TA // tm_a, 1, 4),
                   pl.BlockSpec((tm_a, RH), lambda i, j, k: (i, k)),
                   pl.BlockSpec((None, D, RH), lambda i, j, k: (k, 0, 0)),
                   [(_sds((TA, D), F32), pl.BlockSpec((tm_a, D), lambda i, j, k: (i, 0)))], (tm_a, D))
    dx0, d_sh1r, d_sc1r, d_g00 = _norm_bwd("rec", dx1, dhcat, 1, x0, g00, m0[1])
    d_csh, d_csc, d_g00c = _norm_bwd("ctx", None, dhcat, 0, ctx, g00, csc, with_dx=False)

    big = dict(rec_w_in=dw_rin, rec_w_out=dw_rout, conf_w_pw1=dw_pw1, conf_w_pw2=dw_pw2,
               mlp_w_in=(dw_in0, dw_in1), mlp_w_out=(dw_out0, dw_out1))
    d_wa, d_wx = _gate_block_grads(d_wbd)
    d_ba, d_bx = _gate_bias_grads(d_gbias)
    d_mod = jnp.concatenate([
        d_sh1r, d_sc1r, d_g1r, dm_mlp0["sh"], dm_mlp0["sc"], dm_mlp0["gate"],
        d_sh1c, d_sc1c, d_g1c, dm_mlp1["sh"], dm_mlp1["sc"], dm_mlp1["gate"]], axis=1).reshape(2, 6 * D)
    small = dict(
        d_mod=d_mod, d_cmod=jnp.concatenate([d_csh, d_csc], axis=1),
        norm_g=jnp.concatenate([d_g00 + d_g00c, dm_mlp0["g_norm"], d_g10, dm_mlp1["g_norm"]], axis=1),
        rec_conv_w=d_cw_rec[:REC_KW], rec_conv_b=d_cw_rec[REC_KW], rec_lambda=d_lam.reshape(2, R),
        rec_w_a=d_wa, rec_b_a=d_ba, rec_w_x=d_wx, rec_b_x=d_bx,
        conf_b_pw1=jnp.concatenate([d_b1a, d_b1b], axis=1), conf_conv_w=d_cw_conf[:CONF_KW],
        conf_conv_b=d_cw_conf[CONF_KW], conf_ln_g=d_lng, conf_ln_b=d_lnb, conf_b_pw2=d_bpw2, final_g=d_fg)
    return loss.reshape(()), dx0, big, small


_BIG = ("rec_w_in", "rec_w_out", "conf_w_pw1", "conf_w_pw2", "mlp_w_in", "mlp_w_out")


def _halves(w):
    return w.reshape(2, w.shape[0] // 2, w.shape[1])


def _ada_fwd(c16, w_ada, b_shard):
    ns = w_ada.shape[2]
    tn = 512

    def kern(c_ref, w_ref, b_ref, o_ref):
        cv = c_ref[...]
        s = (cv * _sigmoid(cv)).astype(BF16)
        o_ref[...] = jnp.dot(s, w_ref[...].astype(BF16), preferred_element_type=F32) + b_ref[...]

    return _pcall(
        kern, name="ada_fwd", grid=(2, ns // tn),
        in_specs=[pl.BlockSpec((16, D), lambda l, j: (0, 0)), pl.BlockSpec((None, D, tn), lambda l, j: (l, 0, j)),
                  pl.BlockSpec((None, 1, tn), lambda l, j: (l, 0, j))],
        out_specs=pl.BlockSpec((None, 16, tn), lambda l, j: (l, 0, j)),
        out_shape=_sds((2, 16, ns), F32), compiler_params=_cparams(),
    )(c16, w_ada, b_shard)


def _ada_bwd(c16, dm16, w_ada):
    ns = w_ada.shape[2]
    tn = 512

    def kern(c_ref, dm_ref, w_ref, gw_ref, ds_ref):
        cv = c_ref[...]
        s = (cv * _sigmoid(cv)).astype(BF16)
        dm = dm_ref[...].astype(BF16)
        gw_ref[...] = lax.dot_general(s, dm, _TN, preferred_element_type=F32)

        @pl.when(jnp.logical_and(pl.program_id(0) == 0, pl.program_id(1) == 0))
        def _():
            ds_ref[...] = jnp.zeros_like(ds_ref)

        ds_ref[...] += lax.dot_general(dm, w_ref[...].astype(BF16), _NT, preferred_element_type=F32)

    return _pcall(
        kern, name="ada_bwd", grid=(2, ns // tn),
        in_specs=[pl.BlockSpec((16, D), lambda l, j: (0, 0)), pl.BlockSpec((None, 16, tn), lambda l, j: (l, 0, j)),
                  pl.BlockSpec((None, D, tn), lambda l, j: (l, 0, j))],
        out_specs=[pl.BlockSpec((None, D, tn), lambda l, j: (l, 0, j)), pl.BlockSpec((16, D), lambda l, j: (0, 0))],
        out_shape=[_sds((2, D, ns), F32), _sds((16, D), F32)], compiler_params=_cparams(),
    )(c16, dm16, w_ada)


def _cctx_grad(ds8, c_ctx):
    def kern(d_ref, c_ref, o_ref):
        tot = d_ref[0, 8:9, :] + d_ref[2, 8:9, :] + d_ref[4, 8:9, :] + d_ref[6, 8:9, :]
        cv = c_ref[...]
        sg = _sigmoid(cv)
        o_ref[...] = tot * (sg * (1.0 + cv * (1.0 - sg)))

    return _pcall(kern, name="cctx_grad", out_shape=_sds((1, D), F32))(ds8, c_ctx.reshape(1, D))


def kernel(x, c, ctx, c_ctx, w_ada, b_ada, norm_g, rec_w_in, rec_conv_w, rec_conv_b, rec_lambda, rec_w_a, rec_b_a, rec_w_x, rec_b_x, rec_w_out, conf_w_pw1, conf_b_pw1, conf_conv_w, conf_conv_b, conf_ln_g, conf_ln_b, conf_w_pw2, conf_b_pw2, mlp_w_in, mlp_w_out, final_g, loss_target, m_c_ctx, m_w_ada, m_b_ada, m_norm_g, m_rec_w_in, m_rec_conv_w, m_rec_conv_b, m_rec_lambda, m_rec_w_a, m_rec_b_a, m_rec_w_x, m_rec_b_x, m_rec_w_out, m_conf_w_pw1, m_conf_b_pw1, m_conf_conv_w, m_conf_conv_b, m_conf_ln_g, m_conf_ln_b, m_conf_w_pw2, m_conf_b_pw2, m_mlp_w_in, m_mlp_w_out, m_final_g, v_c_ctx, v_w_ada, v_b_ada, v_norm_g, v_rec_w_in, v_rec_conv_w, v_rec_conv_b, v_rec_lambda, v_rec_w_a, v_rec_b_a, v_rec_w_x, v_rec_b_x, v_rec_w_out, v_conf_w_pw1, v_conf_b_pw1, v_conf_conv_w, v_conf_conv_b, v_conf_ln_g, v_conf_ln_b, v_conf_w_pw2, v_conf_b_pw2, v_mlp_w_in, v_mlp_w_out, v_final_g):
    names = ["c_ctx", "w_ada", "b_ada", "norm_g", "rec_w_in", "rec_conv_w", "rec_conv_b", "rec_lambda", "rec_w_a",
             "rec_b_a", "rec_w_x", "rec_b_x", "rec_w_out", "conf_w_pw1", "conf_b_pw1", "conf_conv_w", "conf_conv_b",
             "conf_ln_g", "conf_ln_b", "conf_w_pw2", "conf_b_pw2", "mlp_w_in", "mlp_w_out", "final_g"]
    w = dict(zip(names, [c_ctx, w_ada, b_ada, norm_g, rec_w_in, rec_conv_w, rec_conv_b, rec_lambda, rec_w_a,
                         rec_b_a, rec_w_x, rec_b_x, rec_w_out, conf_w_pw1, conf_b_pw1, conf_conv_w, conf_conv_b,
                         conf_ln_g, conf_ln_b, conf_w_pw2, conf_b_pw2, mlp_w_in, mlp_w_out, final_g]))
    m = dict(zip(names, [m_c_ctx, m_w_ada, m_b_ada, m_norm_g, m_rec_w_in, m_rec_conv_w, m_rec_conv_b, m_rec_lambda,
                         m_rec_w_a, m_rec_b_a, m_rec_w_x, m_rec_b_x, m_rec_w_out, m_conf_w_pw1, m_conf_b_pw1,
                         m_conf_conv_w, m_conf_conv_b, m_conf_ln_g, m_conf_ln_b, m_conf_w_pw2, m_conf_b_pw2,
                         m_mlp_w_in, m_mlp_w_out, m_final_g]))
    v = dict(zip(names, [v_c_ctx, v_w_ada, v_b_ada, v_norm_g, v_rec_w_in, v_rec_conv_w, v_rec_conv_b, v_rec_lambda,
                         v_rec_w_a, v_rec_b_a, v_rec_w_x, v_rec_b_x, v_rec_w_out, v_conf_w_pw1, v_conf_b_pw1,
                         v_conf_conv_w, v_conf_conv_b, v_conf_ln_g, v_conf_ln_b, v_conf_w_pw2, v_conf_b_pw2,
                         v_mlp_w_in, v_mlp_w_out, v_final_g]))
    mx, my, mc = _me()
    chip = 2 * mx + my
    me = 4 * mx + 2 * my + mc

    place = jnp.stack([chip, mc]).astype(jnp.int32)
    shards = [_halves(rec_w_in[0]), _halves(rec_w_out[0]), _halves(conf_w_pw1[0]), _halves(conf_w_pw2[0]),
              _halves(mlp_w_in[0]), _halves(mlp_w_in[1]), _halves(mlp_w_out[0]), _halves(mlp_w_out[1])]
    use_order = dict(rec=(0, 1), mlp0=(4, 6), conf=(2, 3), mlp1=(5, 7))
    flying, gsems = _gather_start(_place_big(shards, place), tuple(use_order.values()))

    def wg(group, after):
        gi = list(use_order).index(group)
        bufs = _gather_wait(f"gather_wait_{group}", [flying[t] for t in use_order[group]], gsems[2 * gi],
                            gsems[2 * gi + 1], after)
        a, b = _swap_halves(f"swap_{group}", bufs)
        if group == "rec":
            return dict(rec_w_in=a.reshape(4, D, RH), rec_w_out=b.reshape(R, D))
        if group == "conf":
            return dict(conf_w_pw1=a.reshape(4, D, D // 2), conf_w_pw2=b.reshape(D, D))
        return dict(w_in=a.reshape(4, D, D), w_out=b.reshape(FF, D))

    sharded_small = ["norm_g", "rec_conv_w", "rec_lambda", "conf_b_pw1", "conf_conv_w", "conf_conv_b", "conf_ln_g",
                     "conf_ln_b", "conf_b_pw2"]
    packed, offs = _pack([c] + [w[k] for k in sharded_small], 8)
    got = _allgather8("gather_small", packed)
    per_dev = [_unpack(got[d], offs) for d in range(8)]
    c_rows = jnp.concatenate([per_dev[d][0].reshape(1, D) for d in range(8)], axis=0)
    full = {k: jnp.concatenate([per_dev[2 * j][1 + i] for j in range(4)], axis=-1)
            for i, k in enumerate(sharded_small)}
    c16 = jnp.concatenate([c_rows, c_ctx.reshape(1, D), jnp.zeros((7, D), F32)], axis=0)

    ns = w_ada.shape[2]
    b_shard = lax.dynamic_slice_in_dim(b_ada, chip * ns, ns, axis=1).reshape(2, 1, ns)
    prod = _ada_fwd(c16, w_ada, b_shard)
    prod8 = _allgather8("gather_mod", prod.reshape(32, ns)).reshape(8, 2, 16, ns)
    mod_all = jnp.concatenate([prod8[2 * j] for j in range(4)], axis=-1)
    mods = lax.dynamic_index_in_dim(mod_all, me, axis=1, keepdims=False).reshape(2, 6, D)
    cmods = mod_all[0, 8].reshape(6, D)[:2]

    rec = dict(conv_w=full["rec_conv_w"][0], conv_b=rec_conv_b[0], lam=full["rec_lambda"][0],
               w_a=rec_w_a[0], b_a=rec_b_a[0], w_x=rec_w_x[0], b_x=rec_b_x[0])
    conf = dict(b_pw1=full["conf_b_pw1"][0], conv_w=full["conf_conv_w"][0], conv_b=full["conf_conv_b"][0],
                ln_g=full["conf_ln_g"][0], ln_b=full["conf_ln_b"][0], b_pw2=full["conf_b_pw2"][0])
    loss_local, grad_x, big, small = _local_step(x[0], ctx[0], loss_target[0], mods, cmods, full["norm_g"], final_g,
                                                 rec, conf, wg)
    loss = lax.psum(loss_local, ("x", "y", "c"))

    parts = [big["rec_w_in"], big["rec_w_out"], big["conf_w_pw1"], big["conf_w_pw2"],
             big["mlp_w_in"][0], big["mlp_w_in"][1], big["mlp_w_out"][0], big["mlp_w_out"][1]]
    parts = [p.reshape(4, 2, s.shape[1], s.shape[2]) for p, s in zip(parts, shards)]
    whole = _reduce_big(parts, place)
    g_big = dict(rec_w_in=whole[0].reshape(rec_w_in.shape), rec_w_out=whole[1].reshape(rec_w_out.shape),
                 conf_w_pw1=whole[2].reshape(conf_w_pw1.shape), conf_w_pw2=whole[3].reshape(conf_w_pw2.shape),
                 mlp_w_in=jnp.stack([whole[4].reshape(D, D), whole[5].reshape(D, D)]),
                 mlp_w_out=jnp.stack([whole[6].reshape(D, D), whole[7].reshape(D, D)]))

    small_names = ["d_mod", "d_cmod", "norm_g", "rec_conv_w", "rec_conv_b", "rec_lambda", "rec_w_a", "rec_b_a",
                   "rec_w_x", "rec_b_x", "conf_b_pw1", "conf_conv_w", "conf_conv_b", "conf_ln_g", "conf_ln_b",
                   "conf_b_pw2", "final_g"]
    spacked, soffs = _pack([small[k] for k in small_names])
    s8 = _allgather8("gather_grads", spacked)
    ssum = dict(zip(small_names, _unpack(_sum_devices("sum_grads", s8), soffs)))
    dmod_rows = jnp.stack([_unpack(s8[d], soffs)[0] for d in range(8)], axis=1)
    d_cmod_full = jnp.concatenate([ssum["d_cmod"].reshape(1, 2 * D), jnp.zeros((1, 4 * D), F32)], axis=1)
    dm16 = jnp.concatenate([dmod_rows, jnp.stack([d_cmod_full, jnp.zeros((1, 6 * D), F32)]),
                            jnp.zeros((2, 7, 6 * D), F32)], axis=1)
    dm16_shard = lax.dynamic_slice_in_dim(dm16, chip * ns, ns, axis=2)
    g_w_ada, ds_part = _ada_bwd(c16, dm16_shard, w_ada)
    ds8 = _allgather8("gather_dsilu", ds_part)
    g_c_ctx = _cctx_grad(ds8, c_ctx).reshape(D)
    g_b_ada = ssum["d_mod"] + jnp.stack([d_cmod_full[0], jnp.zeros((6 * D,), F32)])

    def shard_of(a, axis):
        n = a.shape[axis] // 4
        return lax.dynamic_slice_in_dim(a, chip * n, n, axis=axis)

    grads = dict(
        c_ctx=g_c_ctx, w_ada=g_w_ada, b_ada=g_b_ada,
        norm_g=shard_of(ssum["norm_g"].reshape(2, 2, D), 2),
        rec_w_in=g_big["rec_w_in"], rec_conv_w=shard_of(ssum["rec_conv_w"].reshape(1, REC_KW, R), 2),
        rec_conv_b=ssum["rec_conv_b"].reshape(1, R), rec_lambda=shard_of(ssum["rec_lambda"].reshape(1, 2, R), 2),
        rec_w_a=ssum["rec_w_a"].reshape(rec_w_a.shape), rec_b_a=ssum["rec_b_a"].reshape(rec_b_a.shape),
        rec_w_x=ssum["rec_w_x"].reshape(rec_w_x.shape), rec_b_x=ssum["rec_b_x"].reshape(rec_b_x.shape),
        rec_w_out=g_big["rec_w_out"], conf_w_pw1=g_big["conf_w_pw1"],
        conf_b_pw1=shard_of(ssum["conf_b_pw1"].reshape(1, 2 * D), 1),
        conf_conv_w=shard_of(ssum["conf_conv_w"].reshape(1, CONF_KW, D), 2),
        conf_conv_b=shard_of(ssum["conf_conv_b"].reshape(1, D), 1),
        conf_ln_g=shard_of(ssum["conf_ln_g"].reshape(1, D), 1), conf_ln_b=shard_of(ssum["conf_ln_b"].reshape(1, D), 1),
        conf_w_pw2=g_big["conf_w_pw2"], conf_b_pw2=shard_of(ssum["conf_b_pw2"].reshape(1, D), 1),
        mlp_w_in=g_big["mlp_w_in"], mlp_w_out=g_big["mlp_w_out"], final_g=ssum["final_g"].reshape(D))

    delta, new_m, new_v = {}, {}, {}
    big_names = ("w_ada",) + _BIG
    for k in big_names:
        cols = w[k].shape[-1]
        d_, m_, v_ = _adamw(f"adamw_{k}", w[k].reshape(-1, cols), grads[k].reshape(-1, cols),
                            m[k].reshape(-1, cols), v[k].reshape(-1, cols))
        delta[k], new_m[k], new_v[k] = (a.reshape(w[k].shape) for a in (d_, m_, v_))
    rest = [k for k in names if k not in big_names]
    pw, poffs = _pack([w[k] for k in rest])
    pg, _ = _pack([grads[k] for k in rest])
    pm, _ = _pack([m[k] for k in rest])
    pv, _ = _pack([v[k] for k in rest])
    d_, m_, v_ = _adamw("adamw_small", pw, pg, pm, pv)
    for k, dd, mm, vv in zip(rest, _unpack(d_, poffs), _unpack(m_, poffs), _unpack(v_, poffs)):
        delta[k], new_m[k], new_v[k] = dd, mm, vv

    return (loss, grad_x[None], *[grads[k] for k in names], *[delta[k] for k in names],
            *[new_m[k] for k in names], *[new_v[k] for k in names])
```

```python
import functools
import math

import jax
import jax.numpy as jnp
from jax import lax
from jax.experimental import pallas as pl
from jax.experimental.pallas import tpu as pltpu

F32 = jnp.float32
BF16 = jnp.bfloat16

D = 1024
T = 2048
TC = 256
TA = T + TC
R = 1280
RH = R // 2
NQ = 4 * RH
FF = 4096
N_BLK = 16
BLK = R // N_BLK
GRID_W = 64
EPS = 1e-6
RG_C = 8.0
CONF_KW = 31
REC_KW = 4
LANE = 128
ROW_TILE = 256
HALO = 16
RG_TILE = 128
PACK_ROWS = 512
V7X_VMEM_BYTES = 64 * 1024 * 1024
VMEM_LIMIT = V7X_VMEM_BYTES - 8 * 1024 * 1024

ADAM_LR = 0.001
ADAM_B1 = 0.9
ADAM_B2 = 0.999
ADAM_EPS = 1e-08
ADAM_WD = 0.01
ADAM_STEP = 10

MESH = pl.DeviceIdType.MESH
ANY = pl.BlockSpec(memory_space=pl.ANY)


def _sds(shape, dtype):
    return jax.ShapeDtypeStruct(tuple(shape), dtype)


def _pcall(body, **kw):
    return pl.pallas_call(body, **kw)


def _cparams():
    return pltpu.CompilerParams(vmem_limit_bytes=VMEM_LIMIT)


def _full_spec(arr):
    nd = arr.ndim
    return pl.BlockSpec(arr.shape, lambda *ids, _n=nd: (0,) * _n)


def _sum0(v):
    return jnp.sum(v, axis=0, keepdims=True)


def _tiled(name, fn, grid, ins, vecs, outs, vec_outs=(), vec_refs=False):
    n_in, n_vec, n_out = len(ins), len(vecs), len(outs)
    n_grid = len(grid)

    def kern(*refs):
        ids = [pl.program_id(a) for a in range(n_grid)]
        tin = [r[...] for r in refs[:n_in]]
        vin = list(refs[n_in:n_in + n_vec]) if vec_refs else [r[...] for r in refs[n_in:n_in + n_vec]]
        o_refs = refs[n_in + n_vec:n_in + n_vec + n_out]
        a_refs = refs[n_in + n_vec + n_out:]
        tout, incs = fn(ids, tin, vin)
        for r, v in zip(o_refs, tout):
            r[...] = v.astype(r.dtype)
        if a_refs:
            first = functools.reduce(jnp.logical_and, [i == 0 for i in ids])

            @pl.when(first)
            def _():
                for r in a_refs:
                    r[...] = jnp.zeros_like(r)

            for r, v in zip(a_refs, incs):
                r[...] += v

    out_shape = [o for o, _ in outs] + [_sds(s, F32) for s in vec_outs]
    out_specs = [s for _, s in outs] + [
        pl.BlockSpec(tuple(s), lambda *ids, _n=len(s): (0,) * _n) for s in vec_outs]
    res = _pcall(
        kern, name=name, grid=tuple(grid),
        in_specs=[s for _, s in ins] + [_full_spec(v) for v in vecs],
        out_specs=out_specs, out_shape=out_shape, compiler_params=_cparams(),
    )(*[a for a, _ in ins], *vecs)
    return list(res)


def _rows(arr, ncols=None, tm=ROW_TILE, off=0, col=0, clamp_lo=False):
    ncols = arr.shape[1] if ncols is None else ncols
    if clamp_lo:
        return arr, pl.BlockSpec((tm, ncols), lambda i: (jnp.maximum(i + off, 0), col))
    return arr, pl.BlockSpec((tm, ncols), lambda i: (i + off, col))


def _orow(nrows, ncols, dtype, tm=ROW_TILE, off=0, clamp_lo=False):
    if clamp_lo:
        return _sds((nrows, ncols), dtype), pl.BlockSpec((tm, ncols), lambda i: (jnp.maximum(i + off, 0), 0))
    return _sds((nrows, ncols), dtype), pl.BlockSpec((tm, ncols), lambda i: (i + off, 0))


_NN = (((1,), (0,)), ((), ()))
_TN = (((0,), (0,)), ((), ()))
_NT = (((1,), (1,)), ((), ()))


def _mm(name, a, b, dims, grid, a_spec, b_spec, out, acc_shape, extra=(), a_pre=None, epi=None):
    n_k = grid[2]
    n_ex = len(extra)

    def kern(a_ref, b_ref, *rest):
        ex = rest[:n_ex]
        o_refs = rest[n_ex:-1]
        acc = rest[-1]
        k = pl.program_id(2)

        @pl.when(k == 0)
        def _():
            acc[...] = jnp.zeros_like(acc)

        av = a_ref[...]
        if a_pre is not None:
            av = a_pre(av)
        acc[...] += lax.dot_general(av.astype(BF16), b_ref[...].astype(BF16), dims,
                                    preferred_element_type=F32)

        @pl.when(k == n_k - 1)
        def _():
            vals = [acc[...]] if epi is None else epi(acc[...], [e[...] for e in ex])
            for r, v in zip(o_refs, vals):
                r[...] = v.astype(r.dtype)

    res = _pcall(
        kern, name=name, grid=tuple(grid),
        in_specs=[a_spec, b_spec] + [s for _, s in extra],
        out_specs=[s for _, s in out], out_shape=[o for o, _ in out],
        scratch_shapes=[pltpu.VMEM(tuple(acc_shape), F32)], compiler_params=_cparams(),
    )(a, b, *[e for e, _ in extra])
    return list(res)


def _rms(x):
    r = lax.rsqrt(jnp.mean(x * x, axis=-1, keepdims=True) + EPS)
    return x * r, r


def _norm_mod(x, g, sc, sh):
    n, _ = _rms(x)
    return (n * g) * (1.0 + sc) + sh


def _norm_mod_bwd(dh, x, g, sc):
    n, r = _rms(x)
    d_sh = _sum0(dh)
    d_sc = _sum0(dh * (n * g))
    d_g = _sum0(dh * (1.0 + sc) * n)
    dn = dh * (g * (1.0 + sc))
    dx = r * (dn - n * jnp.mean(dn * n, axis=-1, keepdims=True))
    return dx, d_sh, d_sc, d_g


_GELU_K = math.sqrt(2.0 / math.pi)


def _gelu(x):
    t = jnp.tanh(_GELU_K * (x + 0.044715 * x * x * x))
    return 0.5 * x * (1.0 + t), t


def _gelu_grad(x, t):
    return 0.5 * (1.0 + t) + 0.5 * x * (1.0 - t * t) * (_GELU_K * (1.0 + 3.0 * 0.044715 * x * x))


def _sigmoid(x):
    return 1.0 / (1.0 + jnp.exp(-x))


def _expm1(x):
    p = 1.0 + x * (1.0 / 9.0)
    for n in (8.0, 7.0, 6.0, 5.0, 4.0, 3.0, 2.0):
        p = 1.0 + (x * (1.0 / n)) * p
    return jnp.where(jnp.abs(x) < 0.5, x * p, jnp.exp(x) - 1.0)


def _softplus_neg(lam):
    return jnp.log1p(jnp.exp(-jnp.abs(lam))) + jnp.maximum(-lam, 0.0)


def _layernorm_parts(x):
    mu = jnp.mean(x, axis=-1, keepdims=True)
    xc = x - mu
    rstd = lax.rsqrt(jnp.mean(xc * xc, axis=-1, keepdims=True) + EPS)
    return xc * rstd, rstd


def _rg_gates(u, wbd, gbias, lam):
    sp = _softplus_neg(lam)
    parts = {}
    for h in range(2):
        uh = u[:, h * RH:(h + 1) * RH]
        g = jnp.dot(uh.astype(BF16), wbd[h], preferred_element_type=F32) + gbias[:, h * NQ:(h + 1) * NQ]
        for d in range(2):
            r = _sigmoid(g[:, (2 * d) * RH:(2 * d + 1) * RH])
            i = _sigmoid(g[:, (2 * d + 1) * RH:(2 * d + 2) * RH])
            sph = sp[d:d + 1, h * RH:(h + 1) * RH]
            la = (-RG_C) * r * sph
            e2 = _expm1(2.0 * la)
            parts[(d, h)] = dict(r=r, i=i, la=la, a=jnp.exp(la), e2=e2, mult=jnp.sqrt(-e2), uh=uh, sp=sph)
    return parts


def _rg_fwd_fn(ids, tin, vin):
    (u,) = tin
    wbd = vin[0]
    parts = _rg_gates(u, wbd, vin[1][...], vin[2][...])
    outs = []
    for d in range(2):
        a = jnp.concatenate([parts[(d, h)]["a"] for h in range(2)], axis=1)
        b = jnp.concatenate([parts[(d, h)]["mult"] * parts[(d, h)]["i"] * parts[(d, h)]["uh"]
                             for h in range(2)], axis=1)
        outs += [a, b]
    return outs, []


def _rg_bwd_fn(ids, tin, vin):
    u, da_f, db_f, da_r, db_r = tin
    wbd, lam = vin[0], vin[2][...]
    parts = _rg_gates(u, wbd, vin[1][...], lam)
    dab = ((da_f, db_f), (da_r, db_r))
    dsig_lam = -_sigmoid(-lam)
    du_halves, dpre_halves, dlam = [], [], [[None, None], [None, None]]
    for h in range(2):
        du = jnp.zeros_like(parts[(0, h)]["uh"])
        dpre = []
        for d in range(2):
            p = parts[(d, h)]
            da = dab[d][0][:, h * RH:(h + 1) * RH]
            db = dab[d][1][:, h * RH:(h + 1) * RH]
            d_mult = db * p["i"] * p["uh"]
            d_i = db * p["mult"] * p["uh"]
            du = du + db * p["mult"] * p["i"]
            d_la = da * p["a"] - d_mult * (p["e2"] + 1.0) / p["mult"]
            d_r = d_la * ((-RG_C) * p["sp"])
            dlam[d][h] = _sum0(d_la * ((-RG_C) * p["r"])) * dsig_lam[d:d + 1, h * RH:(h + 1) * RH]
            dpre += [d_r * p["r"] * (1.0 - p["r"]), d_i * p["i"] * (1.0 - p["i"])]
        dpre = jnp.concatenate(dpre, axis=1)
        du = du + lax.dot_general(dpre.astype(BF16), wbd[h], _NT, preferred_element_type=F32)
        du_halves.append(du)
        dpre_halves.append(dpre)
    dpre_all = jnp.concatenate(dpre_halves, axis=1)
    dlam_row = jnp.concatenate([dlam[0][0], dlam[0][1], dlam[1][0], dlam[1][1]], axis=1)
    return [dpre_all, jnp.concatenate(du_halves, axis=1)], [_sum0(dpre_all), dlam_row]


def _tile_flags(i, n_tiles, seq_starts):
    starts_here = functools.reduce(jnp.logical_or, [i == s for s in seq_starts])
    ends_here = functools.reduce(jnp.logical_or, [i + 1 == s for s in seq_starts] + [i + 1 == n_tiles])
    return jnp.logical_not(starts_here), jnp.logical_not(ends_here)


def _halo_specs(col0, cw):
    hb = ROW_TILE // HALO
    prev = pl.BlockSpec((HALO, cw), lambda i, c: (jnp.maximum(i * hb - 1, 0), col0 + c))
    cur = pl.BlockSpec((ROW_TILE, cw), lambda i, c: (i, col0 + c))
    return prev, cur, hb


def _window(prev_ref, cur_ref, next_ref, has_prev, has_next):
    prev = jnp.where(has_prev, prev_ref[...], 0.0)
    nxt = jnp.where(has_next, next_ref[...], 0.0)
    return jnp.concatenate([prev, cur_ref[...], nxt], axis=0)


def _dwconv(name, x, col0, w, bias, pad_left, seq_starts, n_ch, cw=256):
    n_rows = x.shape[0]
    n_tiles = n_rows // ROW_TILE
    n_taps = w.shape[0]
    prev_spec, cur_spec, hb = _halo_specs(col0, cw)
    last_hb = n_rows // HALO - 1
    next_spec = pl.BlockSpec((HALO, cw), lambda i, c: (jnp.minimum((i + 1) * hb, last_hb), col0 + c))

    def kern(prev_ref, cur_ref, next_ref, w_ref, b_ref, o_ref):
        has_prev, has_next = _tile_flags(pl.program_id(0), n_tiles, seq_starts)
        win = _window(prev_ref, cur_ref, next_ref, has_prev, has_next)
        wv = w_ref[...]
        acc = jnp.zeros((ROW_TILE, cw), F32) + b_ref[...]
        for k in range(n_taps):
            off = HALO + k - pad_left
            acc = acc + wv[k:k + 1, :] * win[off:off + ROW_TILE, :]
        o_ref[...] = acc

    return _pcall(
        kern, name=name, grid=(n_tiles, n_ch // cw),
        in_specs=[prev_spec, cur_spec, next_spec,
                  pl.BlockSpec((n_taps, cw), lambda i, c: (0, c)), pl.BlockSpec((1, cw), lambda i, c: (0, c))],
        out_specs=pl.BlockSpec((ROW_TILE, cw), lambda i, c: (i, c)),
        out_shape=_sds((n_rows, n_ch), F32), compiler_params=_cparams(),
    )(x, x, x, w, bias)


def _dwconv_wgrad(name, dy, x, col0, n_taps, pad_left, seq_starts, n_ch, cw=256):
    n_rows = dy.shape[0]
    n_tiles = n_rows // ROW_TILE
    n_out = -(-(n_taps + 1) // 8) * 8
    prev_spec, cur_spec, hb = _halo_specs(col0, cw)
    last_hb = n_rows // HALO - 1
    next_spec = pl.BlockSpec((HALO, cw), lambda c, i: (jnp.minimum((i + 1) * hb, last_hb), col0 + c))
    prev_spec = pl.BlockSpec((HALO, cw), lambda c, i: (jnp.maximum(i * hb - 1, 0), col0 + c))
    cur_spec = pl.BlockSpec((ROW_TILE, cw), lambda c, i: (i, col0 + c))

    def kern(dy_ref, prev_ref, cur_ref, next_ref, o_ref):
        i = pl.program_id(1)
        has_prev, has_next = _tile_flags(i, n_tiles, seq_starts)
        win = _window(prev_ref, cur_ref, next_ref, has_prev, has_next)
        dyv = dy_ref[...]
        rid = lax.broadcasted_iota(jnp.int32, (n_out, cw), 0)
        inc = jnp.where(rid == n_taps, _sum0(dyv), 0.0)
        for k in range(n_taps):
            off = HALO + k - pad_left
            inc = inc + jnp.where(rid == k, _sum0(dyv * win[off:off + ROW_TILE, :]), 0.0)

        @pl.when(i == 0)
        def _():
            o_ref[...] = jnp.zeros_like(o_ref)

        o_ref[...] += inc

    return _pcall(
        kern, name=name, grid=(n_ch // cw, n_tiles),
        in_specs=[pl.BlockSpec((ROW_TILE, cw), lambda c, i: (i, c)), prev_spec, cur_spec, next_spec],
        out_specs=pl.BlockSpec((n_out, cw), lambda c, i: (0, c)),
        out_shape=_sds((n_out, n_ch), F32), compiler_params=_cparams(),
    )(dy, x, x, x)


N_SCAN = TA // ROW_TILE


def _rev_block(j):
    return jnp.where(j == 0, 0, N_SCAN - j)


def _scan_fwd(a_f, b_f, a_r, b_r):
    fwd_spec = pl.BlockSpec((ROW_TILE, R), lambda i: (i, 0))
    rev_spec = pl.BlockSpec((ROW_TILE, R), lambda i: (_rev_block(i), 0))
    hin_spec = pl.BlockSpec((None, 1, R), lambda i: (i, 0, 0))

    def kern(af, bf, ar, br, yf, yr, hin_f, hin_r, hf_s, hr_s):
        @pl.when(pl.program_id(0) == 0)
        def _():
            hf_s[...] = jnp.zeros_like(hf_s)
            hr_s[...] = jnp.zeros_like(hr_s)

        hin_f[...] = hf_s[...]
        hin_r[...] = hr_s[...]

        def step(s8, carry):
            hf, hr = carry
            t0 = pl.multiple_of(s8 * 8, 8)
            for q in range(8):
                tf = t0 + q
                hf = af[pl.ds(tf, 1), :] * hf + bf[pl.ds(tf, 1), :]
                yf[pl.ds(tf, 1), :] = hf
                tr = ROW_TILE - 1 - tf
                hr = ar[pl.ds(tr, 1), :] * hr + br[pl.ds(tr, 1), :]
                yr[pl.ds(tr, 1), :] = hr
            return hf, hr

        hf, hr = lax.fori_loop(0, ROW_TILE // 8, step, (hf_s[...], hr_s[...]))
        hf_s[...] = hf
        hr_s[...] = hr

    return _pcall(
        kern, name="scan_fwd", grid=(N_SCAN,),
        in_specs=[fwd_spec, fwd_spec, rev_spec, rev_spec],
        out_specs=[fwd_spec, rev_spec, hin_spec, hin_spec],
        out_shape=[_sds((TA, R), F32), _sds((TA, R), F32), _sds((N_SCAN, 1, R), F32), _sds((N_SCAN, 1, R), F32)],
        scratch_shapes=[pltpu.VMEM((1, R), F32), pltpu.VMEM((1, R), F32)], compiler_params=_cparams(),
    )(a_f, b_f, a_r, b_r)


def _scan_bwd(dy, a_f, y_f, hin_f, a_r, y_r, hin_r):
    fwd_spec = pl.BlockSpec((ROW_TILE, R), lambda i: (N_SCAN - 1 - i, 0))
    rev_spec = pl.BlockSpec((ROW_TILE, R), lambda i: (_rev_block(N_SCAN - 1 - i), 0))
    hin_spec = pl.BlockSpec((None, 1, R), lambda i: (N_SCAN - 1 - i, 0, 0))
    last = ROW_TILE - 1

    def kern(dyf, af, yf, hf0, dyr, ar, yr, hr0, daf, dbf, dar, dbr, gf_s, anf_s, gr_s, anr_s):
        @pl.when(pl.program_id(0) == 0)
        def _():
            for r in (gf_s, anf_s, gr_s, anr_s):
                r[...] = jnp.zeros_like(r)

        def one(dy_ref, a_ref, y_ref, da_ref, db_ref, g, an, p, pprev):
            gnew = dy_ref[pl.ds(p, 1), :] + an * g
            db_ref[pl.ds(p, 1), :] = gnew
            da_ref[pl.ds(p, 1), :] = gnew * y_ref[pl.ds(pprev, 1), :]
            return gnew, a_ref[pl.ds(p, 1), :]

        def step(s8, carry):
            gf, anf, gr, anr = carry
            base = s8 * 8
            for q in range(8):
                s = last - (base + q)
                gf, anf = one(dyf, af, yf, daf, dbf, gf, anf, s, s - 1)
                gr, anr = one(dyr, ar, yr, dar, dbr, gr, anr, last - s, last - s + 1)
            return gf, anf, gr, anr

        carry = (gf_s[...], anf_s[...], gr_s[...], anr_s[...])
        carry = lax.fori_loop(0, ROW_TILE // 8 - 1, step, carry)
        gf, anf, gr, anr = carry
        for s in range(7, 0, -1):
            gf, anf = one(dyf, af, yf, daf, dbf, gf, anf, s, s - 1)
            gr, anr = one(dyr, ar, yr, dar, dbr, gr, anr, last - s, last - s + 1)
        gf0 = dyf[0:1, :] + anf * gf
        dbf[0:1, :] = gf0
        daf[0:1, :] = gf0 * hf0[...]
        gr0 = dyr[last:last + 1, :] + anr * gr
        dbr[last:last + 1, :] = gr0
        dar[last:last + 1, :] = gr0 * hr0[...]
        gf_s[...] = gf0
        anf_s[...] = af[0:1, :]
        gr_s[...] = gr0
        anr_s[...] = ar[last:last + 1, :]

    return _pcall(
        kern, name="scan_bwd", grid=(N_SCAN,),
        in_specs=[fwd_spec, fwd_spec, fwd_spec, hin_spec, rev_spec, rev_spec, rev_spec, hin_spec],
        out_specs=[fwd_spec, fwd_spec, rev_spec, rev_spec],
        out_shape=[_sds((TA, R), F32)] * 4,
        scratch_shapes=[pltpu.VMEM((1, R), F32)] * 4, compiler_params=_cparams(),
    )(dy, a_f, y_f, hin_f, dy, a_r, y_r, hin_r)


def _me():
    return lax.axis_index("x"), lax.axis_index("y"), lax.axis_index("c")


def _other_chips(mx, my):
    return [(1 - mx, my), (mx, 1 - my), (1 - mx, 1 - my)]


def _rcopy(src, dst, ssem, rsem, dev):
    return pltpu.make_async_remote_copy(src_ref=src, dst_ref=dst, send_sem=ssem, recv_sem=rsem,
                                        device_id=dev, device_id_type=MESH)


def _allgather8(name, x):
    rows, cols = x.shape

    def kern(x_ref, o_ref, ssem, rsem, lsem):
        mx, my, mc = _me()
        me = 4 * mx + 2 * my + mc
        peers = []
        for k in range(1, 8):
            px = 1 - mx if (k >> 2) & 1 else mx
            py = 1 - my if (k >> 1) & 1 else my
            pc = 1 - mc if k & 1 else mc
            peers.append((px, py, pc))
        mine = pltpu.make_async_copy(x_ref, o_ref.at[me], lsem)
        mine.start()
        sends = [_rcopy(x_ref, o_ref.at[me], ssem.at[k], rsem.at[k], p) for k, p in enumerate(peers)]
        for cp in sends:
            cp.start()
        for k, (px, py, pc) in enumerate(peers):
            _rcopy(x_ref, o_ref.at[4 * px + 2 * py + pc], ssem.at[k], rsem.at[k], (px, py, pc)).wait_recv()
        for cp in sends:
            cp.wait_send()
        mine.wait()

    return _pcall(
        kern, name=name, in_specs=[ANY], out_specs=ANY, out_shape=_sds((8, rows, cols), F32),
        scratch_shapes=[pltpu.SemaphoreType.DMA((7,)), pltpu.SemaphoreType.DMA((7,)), pltpu.SemaphoreType.DMA(())],
    )(x)


def _gather_chips(name, ws):
    n = len(ws)

    def kern(*refs):
        o = refs[n:2 * n]
        s1, r1, s2, r2 = refs[2 * n:]
        mx, my, mc = _me()
        j0 = 2 * mx + my
        chips = _other_chips(mx, my)
        sib = (mx, my, 1 - mc)
        firsts = []
        for t in range(n):
            for q, (qx, qy) in enumerate(chips):
                cp = _rcopy(o[t].at[j0, mc], o[t].at[j0, mc], s1.at[3 * t + q], r1.at[3 * t + q], (qx, qy, mc))
                cp.start()
                firsts.append(cp)
        passed = []
        for t in range(n):
            for q, (qx, qy) in enumerate(chips):
                jq = 2 * qx + qy
                _rcopy(o[t].at[jq, mc], o[t].at[jq, mc], s1.at[3 * t + q], r1.at[3 * t + q], (qx, qy, mc)).wait_recv()
                fw = _rcopy(o[t].at[jq, mc], o[t].at[jq, mc], s2.at[3 * t + q], r2.at[3 * t + q], sib)
                fw.start()
                passed.append(fw)
        for t in range(n):
            for q, (qx, qy) in enumerate(chips):
                jq = 2 * qx + qy
                _rcopy(o[t].at[jq, 1 - mc], o[t].at[jq, 1 - mc], s2.at[3 * t + q], r2.at[3 * t + q], sib).wait_recv()
        for cp in firsts + passed:
            cp.wait_send()

    dma = pltpu.SemaphoreType.DMA
    return _pcall(
        kern, name=name, in_specs=[ANY] * n, out_specs=[ANY] * n,
        out_shape=[_sds(w.shape, w.dtype) for w in ws], input_output_aliases={t: t for t in range(n)},
        scratch_shapes=[dma((3 * n,)), dma((3 * n,)), dma((3 * n,)), dma((3 * n,))],
    )(*ws)


def _reduce_pair(name, gs):
    n = len(gs)

    def kern(*refs):
        g, o = refs[:n], refs[n:2 * n]
        ss, rs = refs[2 * n:]
        mx, my, mc = _me()
        sib = (mx, my, 1 - mc)
        sends = []
        for t in range(n):
            for j in range(4):
                cp = _rcopy(g[t].at[j, 1 - mc], o[t].at[j], ss.at[4 * t + j], rs.at[4 * t + j], sib)
                cp.start()
                sends.append(cp)
        for cp in sends:
            cp.wait_recv()
        for cp in sends:
            cp.wait_send()

    dma = pltpu.SemaphoreType.DMA
    return _pcall(
        kern, name=name, in_specs=[ANY] * n, out_specs=[ANY] * n,
        out_shape=[_sds((4,) + g.shape[2:], g.dtype) for g in gs],
        scratch_shapes=[dma((4 * n,)), dma((4 * n,))],
    )(*gs)


def _share_halves(name, fulls):
    n = len(fulls)

    def kern(*refs):
        o = refs[n:2 * n]
        ss, rs = refs[2 * n:]
        mx, my, mc = _me()
        sib = (mx, my, 1 - mc)
        sends = []
        for t in range(n):
            cp = _rcopy(o[t].at[mc], o[t].at[mc], ss.at[t], rs.at[t], sib)
            cp.start()
            sends.append(cp)
        for t in range(n):
            _rcopy(o[t].at[1 - mc], o[t].at[1 - mc], ss.at[t], rs.at[t], sib).wait_recv()
        for cp in sends:
            cp.wait_send()

    dma = pltpu.SemaphoreType.DMA
    return _pcall(
        kern, name=name, in_specs=[ANY] * n, out_specs=[ANY] * n,
        out_shape=[_sds(f.shape, f.dtype) for f in fulls], input_output_aliases={t: t for t in range(n)},
        scratch_shapes=[dma((n,)), dma((n,))],
    )(*fulls)


def _tiled_sp(name, fn, grid, sp, ins, outs):
    n_in = len(ins)

    def kern(sp_ref, *refs):
        tout = fn([r[...] for r in refs[:n_in]])
        for r, v in zip(refs[n_in:], tout):
            r[...] = v.astype(r.dtype)

    gs = pltpu.PrefetchScalarGridSpec(num_scalar_prefetch=1, grid=tuple(grid),
                                      in_specs=[s for _, s in ins], out_specs=[s for _, s in outs])
    res = _pcall(kern, name=name, grid_spec=gs, out_shape=[o for o, _ in outs], compiler_params=_cparams(),
                 )(sp, *[a for a, _ in ins])
    return list(res)


def _row_tile(rows, cols, itemsize=4, budget=2 * 1024 * 1024):
    tr = rows
    while tr * cols * itemsize > budget and tr % 32 == 0:
        tr //= 2
    return tr


def _place_big(shards, place):
    slots = []
    for t, s in enumerate(shards):
        rr, cc = s.shape[1], s.shape[2]
        tr = _row_tile(rr, cc)
        (slot,) = _tiled_sp(
            f"place{t}", lambda tin: [tin[0]], (2, rr // tr), place,
            [(s, pl.BlockSpec((None, tr, cc), lambda h, i, sp: (h, i, 0)))],
            [(_sds((4, 2, rr, cc), BF16), pl.BlockSpec((None, None, tr, cc), lambda h, i, sp: (sp[0], h, i, 0)))])
        slots.append(slot)
    return slots


def _allreduce_small(vec, place):
    hr = vec.shape[0] // 2
    tr = _row_tile(hr, LANE)
    blk = (None, None, tr, LANE)
    (pair,) = _tiled_sp(
        "small_place", lambda tin: [tin[0]], (2, hr // tr), place,
        [(vec.reshape(2, hr, LANE), pl.BlockSpec((None, tr, LANE), lambda h, i, sp: (h, i, 0)))],
        [(_sds((2, 2, hr, LANE), F32), pl.BlockSpec(blk, lambda h, i, sp: (sp[1], h, i, 0)))])
    (pair,) = _share_halves("small_share", [pair])
    (slot,) = _tiled_sp(
        "small_pair_add", lambda tin: [tin[0] + tin[1]], (2, hr // tr), place,
        [(pair, pl.BlockSpec(blk, lambda h, i, sp: (0, h, i, 0))),
         (pair, pl.BlockSpec(blk, lambda h, i, sp: (1, h, i, 0)))],
        [(_sds((4, 2, hr, LANE), F32), pl.BlockSpec(blk, lambda h, i, sp: (sp[0], h, i, 0)))])
    (chips,) = _gather_chips("small_gather", [slot])
    (total,) = _tiled(
        "small_chip_sum", lambda ids, tin, vin: ([((tin[0] + tin[1]) + tin[2]) + tin[3]], []), (2, hr // tr),
        [(chips, pl.BlockSpec(blk, lambda h, i, _j=j: (_j, h, i, 0))) for j in range(4)], [],
        [(_sds((2, hr, LANE), F32), pl.BlockSpec((None, tr, LANE), lambda h, i: (h, i, 0)))])
    return total.reshape(2 * hr, LANE)


SEM =pl.BlockSpec(memory_space=pltpu.SEMAPHORE)
_DATAFLOW = pltpu.SideEffectType.DATAFLOW_SIDE_EFFECTING


def _gather_start(slots, groups):
    n = len(slots)

    def kern(*refs):
        o = refs[n:2 * n]
        sems = refs[2 * n:]
        mx, my, mc = _me()
        j0 = 2 * mx + my
        for gi, grp in enumerate(groups):
            for k, t in enumerate(grp):
                for q, (qx, qy) in enumerate(_other_chips(mx, my)):
                    _rcopy(o[t].at[j0, mc], o[t].at[j0, mc], sems[2 * gi].at[3 * k + q],
                           sems[2 * gi + 1].at[3 * k + q], (qx, qy, mc)).start()

    sem_shapes = []
    for grp in groups:
        sem_shapes += [pltpu.SemaphoreType.DMA((3 * len(grp),))] * 2
    res = _pcall(
        kern, name="gather_start", in_specs=[ANY] * n, out_specs=[ANY] * n + [SEM] * len(sem_shapes),
        out_shape=[_sds(w.shape, w.dtype) for w in slots] + sem_shapes,
        input_output_aliases={t: t for t in range(n)},
        compiler_params=pltpu.CompilerParams(has_side_effects=_DATAFLOW),
    )(*slots)
    return list(res[:n]), list(res[n:])


def _gather_wait(name, bufs, ssem, rsem, after):
    n = len(bufs)

    def kern(*refs):
        b = refs[:n]
        ssem_ref, rsem_ref = refs[n], refs[n + 1]
        mx, my, mc = _me()
        j0 = 2 * mx + my
        for k in range(n):
            for q, (qx, qy) in enumerate(_other_chips(mx, my)):
                jq = 2 * qx + qy
                _rcopy(b[k].at[jq, mc], b[k].at[jq, mc], ssem_ref.at[3 * k + q], rsem_ref.at[3 * k + q],
                       (qx, qy, mc)).wait_recv()
                _rcopy(b[k].at[j0, mc], b[k].at[j0, mc], ssem_ref.at[3 * k + q], rsem_ref.at[3 * k + q],
                       (qx, qy, mc)).wait_send()

    return list(_pcall(
        kern, name=name, in_specs=[ANY] * n + [SEM, SEM, ANY], out_specs=[ANY] * n,
        out_shape=[_sds(w.shape, w.dtype) for w in bufs], input_output_aliases={k: k for k in range(n)},
        compiler_params=pltpu.CompilerParams(has_side_effects=_DATAFLOW),
    )(*bufs, ssem, rsem, after))


def _swap_halves(name, bufs):
    n = len(bufs)

    def kern(*refs):
        o = refs[n:2 * n]
        ss, rs = refs[2 * n:]
        mx, my, mc = _me()
        sib = (mx, my, 1 - mc)
        sends = []
        for k in range(n):
            for q, (qx, qy) in enumerate(_other_chips(mx, my)):
                jq = 2 * qx + qy
                cp = _rcopy(o[k].at[jq, mc], o[k].at[jq, mc], ss.at[3 * k + q], rs.at[3 * k + q], sib)
                cp.start()
                sends.append(cp)
        for k in range(n):
            for q, (qx, qy) in enumerate(_other_chips(mx, my)):
                jq = 2 * qx + qy
                _rcopy(o[k].at[jq, 1 - mc], o[k].at[jq, 1 - mc], ss.at[3 * k + q], rs.at[3 * k + q], sib).wait_recv()
        for cp in sends:
            cp.wait_send()

    dma = pltpu.SemaphoreType.DMA
    return list(_pcall(
        kern, name=name, in_specs=[ANY] * n, out_specs=[ANY] * n,
        out_shape=[_sds(w.shape, w.dtype) for w in bufs], input_output_aliases={k: k for k in range(n)},
        scratch_shapes=[dma((3 * n,)), dma((3 * n,))],
    )(*bufs))


def _chips_start(name, sums):
    n = len(sums)

    def kern(*refs):
        s, land = refs[n:2 * n], refs[2 * n:3 * n]
        ssem, rsem = refs[3 * n:]
        mx, my, mc = _me()
        for k in range(n):
            for q, (qx, qy) in enumerate(_other_chips(mx, my)):
                _rcopy(s[k].at[2 * qx + qy], land[k].at[q], ssem.at[3 * k + q], rsem.at[3 * k + q], (qx, qy, mc)).start()

    dma = pltpu.SemaphoreType.DMA
    res = _pcall(
        kern, name=name, in_specs=[ANY] * n, out_specs=[ANY] * (2 * n) + [SEM, SEM],
        out_shape=[_sds(s.shape, s.dtype) for s in sums] + [_sds((3,) + s.shape[1:], s.dtype) for s in sums]
        + [dma((3 * n,)), dma((3 * n,))],
        input_output_aliases={k: k for k in range(n)},
        compiler_params=pltpu.CompilerParams(has_side_effects=_DATAFLOW),
    )(*sums)
    return list(res[:n]), list(res[n:2 * n]), res[2 * n], res[2 * n + 1]


def _chips_wait(name, sums, lands, ssem, rsem, after):
    n = len(sums)

    def kern(*refs):
        s, land = refs[:n], refs[n:2 * n]
        ssem_ref, rsem_ref = refs[2 * n], refs[2 * n + 1]
        mx, my, mc = _me()
        for k in range(n):
            for q, (qx, qy) in enumerate(_other_chips(mx, my)):
                cp = _rcopy(s[k].at[2 * qx + qy], land[k].at[q], ssem_ref.at[3 * k + q], rsem_ref.at[3 * k + q],
                            (qx, qy, mc))
                cp.wait_recv()
                cp.wait_send()

    res = _pcall(
        kern, name=name, in_specs=[ANY] * (2 * n) + [SEM, SEM, ANY], out_specs=[ANY] * (2 * n),
        out_shape=[_sds(a.shape, a.dtype) for a in list(sums) + list(lands)],
        input_output_aliases={k: k for k in range(2 * n)},
        compiler_params=pltpu.CompilerParams(has_side_effects=_DATAFLOW),
    )(*sums, *lands, ssem, rsem, after)
    return list(res[:n]), list(res[n:])


def _reduce_begin(tag, parts, place):
    theirs = _reduce_pair(f"reduce_pair_{tag}", parts)
    sums = []
    for k, (p, o) in enumerate(zip(parts, theirs)):
        rr, cc = p.shape[2], p.shape[3]
        tr = _row_tile(rr, cc)
        (s_k,) = _tiled_sp(
            f"pair_add_{tag}{k}", lambda tin: [tin[0].astype(F32) + tin[1].astype(F32)], (4, rr // tr), place,
            [(p, pl.BlockSpec((None, None, tr, cc), lambda j, i, sp: (j, sp[1], i, 0))),
             (o, pl.BlockSpec((None, tr, cc), lambda j, i, sp: (j, i, 0)))],
            [(_sds((4, rr, cc), BF16), pl.BlockSpec((None, tr, cc), lambda j, i, sp: (j, i, 0)))])
        sums.append(s_k)
    return _chips_start(f"chips_start_{tag}", sums)


def _reduce_end(tag, flying, place, after):
    sums, lands = _chips_wait(f"chips_wait_{tag}", *flying, after)
    fulls = []
    for k, (s, q) in enumerate(zip(sums, lands)):
        rr, cc = q.shape[1], q.shape[2]
        tr = _row_tile(rr, cc)

        def add4(tin):
            return [((tin[0].astype(F32) + tin[1].astype(F32)) + tin[2].astype(F32)) + tin[3].astype(F32)]

        ins = [(s, pl.BlockSpec((None, tr, cc), lambda i, sp: (sp[0], i, 0)))]
        ins += [(q, pl.BlockSpec((None, tr, cc), lambda i, sp, _k=kk: (_k, i, 0))) for kk in range(3)]
        (f_k,) = _tiled_sp(f"chip_add_{tag}{k}", add4, (rr // tr,), place, ins,
                           [(_sds((2, rr, cc), F32), pl.BlockSpec((None, tr, cc), lambda i, sp: (sp[1], i, 0)))])
        fulls.append(f_k)
    return fulls


def _pack(parts, PACK_ROWS=PACK_ROWS):
    flat, offs, pos = [], [], 0
    for p in parts:
        v = p.reshape(-1).astype(F32)
        n = -(-v.shape[0] // LANE) * LANE
        flat.append(jnp.pad(v, (0, n - v.shape[0])))
        offs.append((pos, v.shape[0], p.shape))
        pos += n
    total = -(-pos // (PACK_ROWS * LANE)) * PACK_ROWS * LANE
    flat.append(jnp.zeros((total - pos,), F32))
    return jnp.concatenate(flat).reshape(-1, LANE), offs


def _unpack(vec, offs):
    v = vec.reshape(-1)
    return [v[p:p + n].reshape(shape) for p, n, shape in offs]


def _adamw(name, w, g, m, v):
    rows, cols = w.shape
    tr = rows
    for cand in (512, 256, 128, 64, 32, 16, 8):
        if rows % cand == 0 and cand * cols * 4 <= 2 * 1024 * 1024:
            tr = cand
            break
    bc1 = 1.0 - ADAM_B1 ** ADAM_STEP
    bc2 = 1.0 - ADAM_B2 ** ADAM_STEP

    def fn(ids, tin, vin):
        wv, gv, mv, vv = tin
        mn = ADAM_B1 * mv + (1.0 - ADAM_B1) * gv
        vn = ADAM_B2 * vv + (1.0 - ADAM_B2) * (gv * gv)
        delta = -ADAM_LR * ((mn / bc1) / (jnp.sqrt(vn / bc2) + ADAM_EPS) + ADAM_WD * wv)
        return [delta, mn, vn], []

    spec = pl.BlockSpec((tr, cols), lambda i: (i, 0))
    outs = [(_sds((rows, cols), F32), spec)] * 3
    return _tiled(name, fn, (rows // tr,), [(a, spec) for a in (w, g, m, v)], [], outs)


def _pos_embed():
    n_rows = T // GRID_W
    q = D // 4
    omega = 1.0 / (10000.0 ** (jnp.arange(q, dtype=F32) / q))
    er = jnp.arange(n_rows, dtype=jnp.int32).astype(F32)[:, None] * omega[None, :]
    ec = jnp.arange(GRID_W, dtype=jnp.int32).astype(F32)[:, None] * omega[None, :]
    by_row = jnp.concatenate([jnp.sin(er), jnp.cos(er)], axis=-1)
    by_col = jnp.concatenate([jnp.sin(ec), jnp.cos(ec)], axis=-1)
    return jnp.concatenate([jnp.repeat(by_row, GRID_W, axis=0), jnp.tile(by_col, (n_rows, 1))], axis=-1)


def _dense_gates(w_a, w_x):
    per = N_BLK // 2
    kinds = jnp.stack([src.reshape(2, per, BLK, BLK) for src in (w_a[0], w_x[0], w_a[1], w_x[1])], axis=3)
    on_diag = jnp.eye(per, dtype=bool)[None, :, None, None, :, None]
    out = jnp.where(on_diag, kinds[:, :, :, :, None, :], 0.0)
    return out.reshape(2, RH, NQ).astype(BF16)


def _gate_block_grads(dwbd):
    per = N_BLK // 2
    d6 = dwbd.reshape(2, per, BLK, 4, per, BLK)
    kinds = []
    for q in range(4):
        kinds.append(jnp.stack([d6[:, nb, :, q, nb, :] for nb in range(per)], axis=1).reshape(N_BLK, BLK, BLK))
    return jnp.stack([kinds[0], kinds[2]]), jnp.stack([kinds[1], kinds[3]])


def _gate_bias_dense(b_a, b_x):
    cols = []
    for h in range(2):
        for src in (b_a[0], b_x[0], b_a[1], b_x[1]):
            cols.append(src.reshape(R)[h * RH:(h + 1) * RH])
    return jnp.concatenate(cols).reshape(1, 2 * NQ)


def _gate_bias_grads(dgb):
    v = dgb.reshape(2, 4, RH)
    kinds = [jnp.concatenate([v[0, q], v[1, q]]).reshape(N_BLK, BLK) for q in range(4)]
    return jnp.stack([kinds[0], kinds[2]]), jnp.stack([kinds[1], kinds[3]])


def _mlp_fwd(tag, x_in, g_norm, sh, sc, gate, w_in, w_out):
    n_t = T // ROW_TILE
    (h,) = _tiled(f"{tag}_norm", lambda ids, t, v: ([_norm_mod(t[0], v[0], v[1], v[2])], []), (n_t,),
                  [_rows(x_in)], [g_norm, sc, sh], [_orow(T, D, BF16)])
    tm = 512
    (r,) = _mm(f"{tag}_in", h, w_in, _NN, (T // tm, 4, 1),
               pl.BlockSpec((tm, D), lambda i, j, k: (i, 0)), pl.BlockSpec((None, D, D), lambda i, j, k: (j, 0, 0)),
               [(_sds((T, FF), BF16), pl.BlockSpec((tm, D), lambda i, j, k: (i, j)))], (tm, D),
               epi=lambda acc, ex: [jnp.maximum(acc, 0.0)])
    o, x_out = _mm(f"{tag}_out", r, w_out, _NN, (T // tm, 1, FF // D),
                   pl.BlockSpec((tm, D), lambda i, j, k: (i, k)), pl.BlockSpec((D, D), lambda i, j, k: (k, 0)),
                   [(_sds((T, D), F32), pl.BlockSpec((tm, D), lambda i, j, k: (i, 0)))] * 2, (tm, D),
                   extra=[(x_in, pl.BlockSpec((tm, D), lambda i, j, k: (i, 0))), (gate, _full_spec(gate))],
                   a_pre=lambda a: a * a, epi=lambda acc, ex: [acc, ex[0] + ex[1] * acc])
    return dict(h=h, r=r, o=o, x_in=x_in), x_out


def _gate_bwd(tag, dx, o, gate):
    def fn(ids, t, v):
        d_o = t[0] * v[0]
        return [d_o], [_sum0(t[0] * t[1]), _sum0(d_o)]
    return _tiled(f"{tag}_gate_bwd", fn, (T // ROW_TILE,), [_rows(dx), _rows(o)], [gate],
                  [_orow(T, D, BF16)], [(1, D), (1, D)])


def _norm_bwd(tag, dx_res, dh, dh_off, x, g_norm, sc, with_dx=True):
    n_t = x.shape[0] // ROW_TILE

    def fn(ids, t, v):
        if with_dx:
            dres, dhv, xv = t
        else:
            dhv, xv = t
        dxv, d_sh, d_sc, d_g = _norm_mod_bwd(dhv, xv, v[0], v[1])
        return ([dres + dxv] if with_dx else []), [d_sh, d_sc, d_g]

    ins = ([_rows(dx_res)] if with_dx else []) + [_rows(dh, off=dh_off), _rows(x)]
    outs = [_orow(x.shape[0], D, F32)] if with_dx else []
    return _tiled(f"{tag}_norm_bwd", fn, (n_t,), ins, [g_norm, sc], outs, [(1, D)] * 3)


def _mlp_bwd(tag, dx, saved, g_norm, sc, gate, w_in, w_out):
    d_o, d_gate, _ = _gate_bwd(tag, dx, saved["o"], gate)
    tm = 512
    r = saved["r"]
    (da,) = _mm(f"{tag}_dz", d_o, w_out, _NT, (T // tm, FF // D, 1),
                pl.BlockSpec((tm, D), lambda i, j, k: (i, 0)), pl.BlockSpec((D, D), lambda i, j, k: (j, 0)),
                [(_sds((T, FF), BF16), pl.BlockSpec((tm, D), lambda i, j, k: (i, j)))], (tm, D),
                extra=[(r, pl.BlockSpec((tm, D), lambda i, j, k: (i, j)))],
                epi=lambda acc, ex: [acc * (2.0 * ex[0].astype(F32))])
    tk = 512
    (dw_out,) = _mm(f"{tag}_dwout", r, d_o, _TN, (FF // tm, 1, T // tk),
                    pl.BlockSpec((tk, tm), lambda i, j, k: (k, i)), pl.BlockSpec((tk, D), lambda i, j, k: (k, 0)),
                    [(_sds((FF, D), BF16), pl.BlockSpec((tm, D), lambda i, j, k: (i, 0)))], (tm, D),
                    a_pre=lambda a: a * a)
    (dh,) = _mm(f"{tag}_dh", da, w_in, _NT, (T // tm, 1, 4),
                pl.BlockSpec((tm, D), lambda i, j, k: (i, k)), pl.BlockSpec((None, D, D), lambda i, j, k: (k, 0, 0)),
                [(_sds((T, D), F32), pl.BlockSpec((tm, D), lambda i, j, k: (i, 0)))], (tm, D))
    (dw_in,) = _mm(f"{tag}_dwin", saved["h"], da, _TN, (D // tm, 4, T // tk),
                   pl.BlockSpec((tk, tm), lambda i, j, k: (k, i)), pl.BlockSpec((tk, D), lambda i, j, k: (k, j)),
                   [(_sds((4, D, D), BF16), pl.BlockSpec((None, tm, D), lambda i, j, k: (j, i, 0)))], (tm, D))
    dx_in, d_sh, d_sc, d_g = _norm_bwd(tag, dx, dh, 0, saved["x_in"], g_norm, sc)
    return dx_in, dw_in, dw_out, dict(sh=d_sh, sc=d_sc, gate=d_gate, g_norm=d_g)


def _local_step(x, ctx, tgt, mods, cmods, norm_g, final_g, rec, conf, wg, on_grads=None):
    on_grads = on_grads or (lambda group, dws: None)
    n_t = T // ROW_TILE
    row = lambda v: v.reshape(1, -1)
    m0 = [row(mods[0, q]) for q in range(6)]
    m1 = [row(mods[1, q]) for q in range(6)]
    g00, g01, g10, g11 = (row(norm_g[0, 0]), row(norm_g[0, 1]), row(norm_g[1, 0]), row(norm_g[1, 1]))
    csh, csc = row(cmods[0]), row(cmods[1])
    pos = _pos_embed()

    def prep0(ids, t, v):
        cx, xv, pv = t
        is_ctx = ids[0] == 0
        xin = jnp.where(is_ctx, cx, xv + pv)
        sh = jnp.where(is_ctx, v[3], v[1])
        sc = jnp.where(is_ctx, v[4], v[2])
        return [_norm_mod(xin, v[0], sc, sh), xv + pv], []

    hcat, x0 = _tiled(
        "prep0", prep0, (N_SCAN,),
        [(ctx, pl.BlockSpec((ROW_TILE, D), lambda i: (0, 0))), _rows(x, off=-1, clamp_lo=True),
         _rows(pos, off=-1, clamp_lo=True)],
        [g00, m0[0], m0[1], csh, csc],
        [_orow(TA, D, BF16), _orow(T, D, F32, off=-1, clamp_lo=True)])

    tm_a = 768
    w_rec = wg("rec", hcat)
    (a_in,) = _mm("rec_in", hcat, w_rec["rec_w_in"], _NN, (TA // tm_a, 4, 1),
                  pl.BlockSpec((tm_a, D), lambda i, j, k: (i, 0)),
                  pl.BlockSpec((None, D, RH), lambda i, j, k: (j, 0, 0)),
                  [(_sds((TA, 2 * R), F32), pl.BlockSpec((tm_a, RH), lambda i, j, k: (i, j)))], (tm_a, RH))
    cw = 256
    rec_starts = (0, 1)
    u = _dwconv("rec_conv", a_in, R // cw, rec["conv_w"], row(rec["conv_b"]), 1, rec_starts, R, cw)
    wbd = _dense_gates(rec["w_a"], rec["w_x"])
    gbias = _gate_bias_dense(rec["b_a"], rec["b_x"])
    lam = rec["lam"]
    a_f, b_f, a_r, b_r = _tiled("rg_fwd", _rg_fwd_fn, (TA // RG_TILE,), [_rows(u, tm=RG_TILE)], [wbd, gbias, lam],
                                [_orow(TA, R, F32, tm=RG_TILE)] * 4, vec_refs=True)
    y_f, y_r, hin_f, hin_r = _scan_fwd(a_f, b_f, a_r, b_r)

    def rec_mid(ids, t, v):
        gp, yf, yr = t
        g, _ = _gelu(gp)
        return [g * (yf + yr)], []

    (m_rec,) = _tiled("rec_mid", rec_mid, (n_t,),
                      [_rows(a_in, R, off=1), _rows(y_f, off=1), _rows(y_r, off=1)], [], [_orow(T, R, BF16)])
    tm = 512
    o_rec, x1 = _mm("rec_out", m_rec, w_rec["rec_w_out"], _NN, (T // tm, 1, 1),
                    pl.BlockSpec((tm, R), lambda i, j, k: (i, 0)), pl.BlockSpec((R, D), lambda i, j, k: (0, 0)),
                    [(_sds((T, D), F32), pl.BlockSpec((tm, D), lambda i, j, k: (i, 0)))] * 2, (tm, D),
                    extra=[(x0, pl.BlockSpec((tm, D), lambda i, j, k: (i, 0))), (m0[2], _full_spec(m0[2]))],
                    epi=lambda acc, ex: [acc, ex[0] + ex[1] * acc])
    w_m0 = wg("mlp0", x1)
    mlp0, x2 = _mlp_fwd("mlp0", x1, g01, m0[3], m0[4], m0[5], w_m0["w_in"], w_m0["w_out"])

    (h1,) = _tiled("conf_norm", lambda ids, t, v: ([_norm_mod(t[0], v[0], v[1], v[2])], []), (n_t,),
                   [_rows(x2)], [g10, m1[1], m1[0]], [_orow(T, D, BF16)])
    b_pw1 = row(conf["b_pw1"])
    w_cf = wg("conf", x2)
    (pre,) = _mm("conf_pw1", h1, w_cf["conf_w_pw1"], _NN, (T // tm, 4, 1),
                 pl.BlockSpec((tm, D), lambda i, j, k: (i, 0)),
                 pl.BlockSpec((None, D, D // 2), lambda i, j, k: (j, 0, 0)),
                 [(_sds((T, 2 * D), F32), pl.BlockSpec((tm, D // 2), lambda i, j, k: (i, j)))], (tm, D // 2),
                 extra=[(b_pw1, pl.BlockSpec((1, D // 2), lambda i, j, k: (0, j)))],
                 epi=lambda acc, ex: [acc + ex[0]])
    (zg,) = _tiled("conf_glu", lambda ids, t, v: ([t[0] * _sigmoid(t[1])], []), (n_t,),
                   [_rows(pre, D, col=0), _rows(pre, D, col=1)], [], [_orow(T, D, F32)])
    conf_starts = (0,)
    zc = _dwconv("conf_conv", zg, 0, conf["conv_w"], row(conf["conv_b"]), CONF_KW // 2, conf_starts, D, cw)
    ln_g, ln_b = row(conf["ln_g"]), row(conf["ln_b"])

    def ln_silu(ids, t, v):
        nh, _ = _layernorm_parts(t[0])
        ln = nh * v[0] + v[1]
        return [ln * _sigmoid(ln)], []

    (s_conf,) = _tiled("conf_ln", ln_silu, (n_t,), [_rows(zc)], [ln_g, ln_b], [_orow(T, D, BF16)])
    b_pw2 = row(conf["b_pw2"])
    y_conf, x3 = _mm("conf_pw2", s_conf, w_cf["conf_w_pw2"], _NN, (T // tm, 1, 1),
                     pl.BlockSpec((tm, D), lambda i, j, k: (i, 0)), pl.BlockSpec((D, D), lambda i, j, k: (0, 0)),
                     [(_sds((T, D), F32), pl.BlockSpec((tm, D), lambda i, j, k: (i, 0)))] * 2, (tm, D),
                     extra=[(x2, pl.BlockSpec((tm, D), lambda i, j, k: (i, 0))), (m1[2], _full_spec(m1[2])),
                            (b_pw2, _full_spec(b_pw2))],
                     epi=lambda acc, ex: [acc + ex[2], ex[0] + ex[1] * (acc + ex[2])])
    w_m1 = wg("mlp1", x3)
    mlp1, x4 = _mlp_fwd("mlp1", x3, g11, m1[3], m1[4], m1[5], w_m1["w_in"], w_m1["w_out"])

    fg = row(final_g)

    def head(ids, t, v):
        n, r = _rms(t[0])
        err = n * v[0] - t[1]
        d_out = err * (1.0 / D)
        dn = d_out * v[0]
        dxv = r * (dn - n * jnp.mean(dn * n, axis=-1, keepdims=True))
        part = jnp.sum(_sum0(err * err), axis=1, keepdims=True) * (0.5 / D)
        return [dxv], [part, _sum0(d_out * n)]

    dx4, loss, d_fg = _tiled("head", head, (n_t,), [_rows(x4), _rows(tgt)], [fg], [_orow(T, D, F32)],
                             [(1, 1), (1, D)])

    dx3, dw_in1, dw_out1, dm_mlp1 = _mlp_bwd("mlp1", dx4, mlp1, g11, m1[4], m1[5],
                                             w_m1["w_in"], w_m1["w_out"])
    on_grads("mlp1", (dw_in1, dw_out1))
    d_y, d_g1c, d_bpw2 = _gate_bwd("conf", dx3, y_conf, m1[2])
    tk = 512
    (dw_pw2,) = _mm("conf_dwpw2", s_conf, d_y, _TN, (D // tm, 1, T // tk),
                    pl.BlockSpec((tk, tm), lambda i, j, k: (k, i)), pl.BlockSpec((tk, D), lambda i, j, k: (k, 0)),
                    [(_sds((D, D), BF16), pl.BlockSpec((tm, D), lambda i, j, k: (i, 0)))], (tm, D))
    (ds,) = _mm("conf_ds", d_y, w_cf["conf_w_pw2"], _NT, (T // tm, 1, 1),
                pl.BlockSpec((tm, D), lambda i, j, k: (i, 0)), pl.BlockSpec((D, D), lambda i, j, k: (0, 0)),
                [(_sds((T, D), F32), pl.BlockSpec((tm, D), lambda i, j, k: (i, 0)))], (tm, D))

    def ln_silu_bwd(ids, t, v):
        dsv, zcv = t
        nh, rstd = _layernorm_parts(zcv)
        ln = nh * v[0] + v[1]
        sg = _sigmoid(ln)
        d_ln = dsv * (sg * (1.0 + ln * (1.0 - sg)))
        d_nh = d_ln * v[0]
        d_zc = rstd * (d_nh - jnp.mean(d_nh, axis=-1, keepdims=True)
                       - nh * jnp.mean(d_nh * nh, axis=-1, keepdims=True))
        return [d_zc], [_sum0(d_ln * nh), _sum0(d_ln)]

    d_zc, d_lng, d_lnb = _tiled("conf_ln_bwd", ln_silu_bwd, (n_t,), [_rows(ds), _rows(zc)], [ln_g, ln_b],
                                [_orow(T, D, F32)], [(1, D), (1, D)])
    d_zg = _dwconv("conf_conv_dx", d_zc, 0, conf["conv_w"][::-1], jnp.zeros((1, D), F32),
                   CONF_KW - 1 - CONF_KW // 2, conf_starts, D, cw)
    d_cw_conf = _dwconv_wgrad("conf_conv_dw", d_zc, zg, 0, CONF_KW, CONF_KW // 2, conf_starts, D, cw)

    def glu_bwd(ids, t, v):
        dz, pa, pb = t
        sg = _sigmoid(pb)
        d_a = dz * sg
        d_b = dz * pa * sg * (1.0 - sg)
        return [d_a, d_b], [_sum0(d_a), _sum0(d_b)]

    d_pre_a, d_pre_b, d_b1a, d_b1b = _tiled(
        "conf_glu_bwd", glu_bwd, (n_t,), [_rows(d_zg), _rows(pre, D, col=0), _rows(pre, D, col=1)], [],
        [_orow(T, D, BF16), _orow(T, D, BF16)], [(1, D), (1, D)])
    d_pre = jnp.concatenate([d_pre_a, d_pre_b], axis=1)
    (dw_pw1,) = _mm("conf_dwpw1", h1, d_pre, _TN, (D // tm, 4, T // tk),
                    pl.BlockSpec((tk, tm), lambda i, j, k: (k, i)),
                    pl.BlockSpec((tk, D // 2), lambda i, j, k: (k, j)),
                    [(_sds((4, D, D // 2), BF16), pl.BlockSpec((None, tm, D // 2), lambda i, j, k: (j, i, 0)))],
                    (tm, D // 2))
    on_grads("conf", (dw_pw1, dw_pw2))
    (dh1,) = _mm("conf_dh", d_pre, w_cf["conf_w_pw1"], _NT, (T // tm, 1, 4),
                 pl.BlockSpec((tm, D // 2), lambda i, j, k: (i, k)),
                 pl.BlockSpec((None, D, D // 2), lambda i, j, k: (k, 0, 0)),
                 [(_sds((T, D), F32), pl.BlockSpec((tm, D), lambda i, j, k: (i, 0)))], (tm, D))
    dx2, d_sh1c, d_sc1c, d_g10 = _norm_bwd("conf", dx3, dh1, 0, x2, g10, m1[1])

    dx1, dw_in0, dw_out0, dm_mlp0 = _mlp_bwd("mlp0", dx2, mlp0, g01, m0[4], m0[5],
                                             w_m0["w_in"], w_m0["w_out"])
    on_grads("mlp0", (dw_in0, dw_out0))
    d_orec, d_g1r, _ = _gate_bwd("rec", dx1, o_rec, m0[2])
    (dw_rout,) = _mm("rec_dwout", m_rec, d_orec, _TN, (R // RH, 1, T // tk),
                     pl.BlockSpec((tk, RH), lambda i, j, k: (k, i)), pl.BlockSpec((tk, D), lambda i, j, k: (k, 0)),
                     [(_sds((R, D), BF16), pl.BlockSpec((RH, D), lambda i, j, k: (i, 0)))], (RH, D))
    (dm_rec,) = _mm("rec_dm", d_orec, w_rec["rec_w_out"], _NT, (T // tm, 1, 1),
                    pl.BlockSpec((tm, D), lambda i, j, k: (i, 0)), pl.BlockSpec((R, D), lambda i, j, k: (0, 0)),
                    [(_sds((T, R), F32), pl.BlockSpec((tm, R), lambda i, j, k: (i, 0)))], (tm, R))

    def rec_mid_bwd(ids, t, v):
        dmv, gp, yf, yr = t
        g, th = _gelu(gp)
        lat = ids[0] > 0
        d_gp = jnp.where(lat, dmv * (yf + yr) * _gelu_grad(gp, th), 0.0)
        dy = jnp.where(lat, dmv * g, 0.0)
        return [d_gp, dy], []

    d_gp, dy = _tiled("rec_mid_bwd", rec_mid_bwd, (N_SCAN,),
                      [_rows(dm_rec, off=-1, clamp_lo=True), _rows(a_in, R), _rows(y_f), _rows(y_r)], [],
                      [_orow(TA, R, BF16), _orow(TA, R, F32)])
    da_f, db_f, da_r, db_r = _scan_bwd(dy, a_f, y_f, hin_f, a_r, y_r, hin_r)
    d_gpre, d_u, d_gbias, d_lam = _tiled(
        "rg_bwd", _rg_bwd_fn, (TA // RG_TILE,), [_rows(a, tm=RG_TILE) for a in (u, da_f, db_f, da_r, db_r)],
        [wbd, gbias, lam], [_orow(TA, 2 * NQ, BF16, tm=RG_TILE), _orow(TA, R, F32, tm=RG_TILE)],
        [(1, 2 * NQ), (1, 2 * R)], vec_refs=True)
    tk_a = 768
    (d_wbd,) = _mm("rg_dw", u, d_gpre, _TN, (2, 2, TA // tk_a),
                   pl.BlockSpec((tk_a, RH), lambda i, j, k: (k, i)),
                   pl.BlockSpec((tk_a, NQ // 2), lambda i, j, k: (k, 2 * i + j)),
                   [(_sds((2, RH, NQ), F32), pl.BlockSpec((None, RH, NQ // 2), lambda i, j, k: (i, 0, j)))],
                   (RH, NQ // 2))
    d_p = _dwconv("rec_conv_dx", d_u, 0, rec["conv_w"][::-1], jnp.zeros((1, R), F32), REC_KW - 1 - 1,
                  rec_starts, R, cw)
    d_cw_rec = _dwconv_wgrad("rec_conv_dw", d_u, a_in, R // cw, REC_KW, 1, rec_starts, R, cw)
    d_a = jnp.concatenate([d_gp, d_p.astype(BF16)], axis=1)
    (dw_rin,) = _mm("rec_dwin", hcat, d_a, _TN, (D // tm, 4, TA // tk_a),
                    pl.BlockSpec((tk_a, tm), lambda i, j, k: (k, i)), pl.BlockSpec((tk_a, RH), lambda i, j, k: (k, j)),
                    [(_sds((4, D, RH), BF16), pl.BlockSpec((None, tm, RH), lambda i, j, k: (j, i, 0)))], (tm, RH))
    on_grads("rec", (dw_rin, dw_rout))
    (dhcat,) = _mm("rec_dh", d_a, w_rec["rec_w_in"], _NT, (TA // tm_a, 1, 4),
                   pl.BlockSpec((tm_a, RH), lambda i, j, k: (i, k)),
                   pl.BlockSpec((None, D, RH), lambda i, j, k: (k, 0, 0)),
                   [(_sds((TA, D), F32), pl.BlockSpec((tm_a, D), lambda i, j, k: (i, 0)))], (tm_a, D))
    dx0, d_sh1r, d_sc1r, d_g00 = _norm_bwd("rec", dx1, dhcat, 1, x0, g00, m0[1])
    d_csh, d_csc, d_g00c = _norm_bwd("ctx", None, dhcat, 0, ctx, g00, csc, with_dx=False)

    big = dict(rec_w_in=dw_rin, rec_w_out=dw_rout, conf_w_pw1=dw_pw1, conf_w_pw2=dw_pw2,
               mlp_w_in=(dw_in0, dw_in1), mlp_w_out=(dw_out0, dw_out1))
    d_wa, d_wx = _gate_block_grads(d_wbd)
    d_ba, d_bx = _gate_bias_grads(d_gbias)
    d_mod = jnp.concatenate([
        d_sh1r, d_sc1r, d_g1r, dm_mlp0["sh"], dm_mlp0["sc"], dm_mlp0["gate"],
        d_sh1c, d_sc1c, d_g1c, dm_mlp1["sh"], dm_mlp1["sc"], dm_mlp1["gate"]], axis=1).reshape(2, 6 * D)
    small = dict(
        d_mod=d_mod, d_cmod=jnp.concatenate([d_csh, d_csc], axis=1),
        norm_g=jnp.concatenate([d_g00 + d_g00c, dm_mlp0["g_norm"], d_g10, dm_mlp1["g_norm"]], axis=1),
        rec_conv_w=d_cw_rec[:REC_KW], rec_conv_b=d_cw_rec[REC_KW], rec_lambda=d_lam.reshape(2, R),
        rec_w_a=d_wa, rec_b_a=d_ba, rec_w_x=d_wx, rec_b_x=d_bx,
        conf_b_pw1=jnp.concatenate([d_b1a, d_b1b], axis=1), conf_conv_w=d_cw_conf[:CONF_KW],
        conf_conv_b=d_cw_conf[CONF_KW], conf_ln_g=d_lng, conf_ln_b=d_lnb, conf_b_pw2=d_bpw2, final_g=d_fg)
    return loss.reshape(()), dx0, big, small


_BIG = ("rec_w_in", "rec_w_out", "conf_w_pw1", "conf_w_pw2", "mlp_w_in", "mlp_w_out")


def _halves(w):
    return w.reshape(2, w.shape[0] // 2, w.shape[1])


def _ada_fwd(c16, w_ada, b_shard):
    ns = w_ada.shape[2]
    tn = 512

    def kern(c_ref, w_ref, b_ref, o_ref):
        cv = c_ref[...]
        s = (cv * _sigmoid(cv)).astype(BF16)
        o_ref[...] = jnp.dot(s, w_ref[...].astype(BF16), preferred_element_type=F32) + b_ref[...]

    return _pcall(
        kern, name="ada_fwd", grid=(2, ns // tn),
        in_specs=[pl.BlockSpec((16, D), lambda l, j: (0, 0)), pl.BlockSpec((None, D, tn), lambda l, j: (l, 0, j)),
                  pl.BlockSpec((None, 1, tn), lambda l, j: (l, 0, j))],
        out_specs=pl.BlockSpec((None, 16, tn), lambda l, j: (l, 0, j)),
        out_shape=_sds((2, 16, ns), F32), compiler_params=_cparams(),
    )(c16, w_ada, b_shard)


def _ada_bwd(c16, dm16, w_ada):
    ns = w_ada.shape[2]
    tn = 512

    def kern(c_ref, dm_ref, w_ref, gw_ref, ds_ref):
        cv = c_ref[...]
        s = (cv * _sigmoid(cv)).astype(BF16)
        dm = dm_ref[...].astype(BF16)
        gw_ref[...] = lax.dot_general(s, dm, _TN, preferred_element_type=F32)

        @pl.when(jnp.logical_and(pl.program_id(0) == 0, pl.program_id(1) == 0))
        def _():
            ds_ref[...] = jnp.zeros_like(ds_ref)

        ds_ref[...] += lax.dot_general(dm, w_ref[...].astype(BF16), _NT, preferred_element_type=F32)

    return _pcall(
        kern, name="ada_bwd", grid=(2, ns // tn),
        in_specs=[pl.BlockSpec((16, D), lambda l, j: (0, 0)), pl.BlockSpec((None, 16, tn), lambda l, j: (l, 0, j)),
                  pl.BlockSpec((None, D, tn), lambda l, j: (l, 0, j))],
        out_specs=[pl.BlockSpec((None, D, tn), lambda l, j: (l, 0, j)), pl.BlockSpec((16, D), lambda l, j: (0, 0))],
        out_shape=[_sds((2, D, ns), F32), _sds((16, D), F32)], compiler_params=_cparams(),
    )(c16, dm16, w_ada)


def _cctx_grad(ds8, c_ctx):
    def kern(d_ref, c_ref, o_ref):
        tot = d_ref[0, 8:9, :] + d_ref[2, 8:9, :] + d_ref[4, 8:9, :] + d_ref[6, 8:9, :]
        cv = c_ref[...]
        sg = _sigmoid(cv)
        o_ref[...] = tot * (sg * (1.0 + cv * (1.0 - sg)))

    return _pcall(kern, name="cctx_grad", out_shape=_sds((1, D), F32))(ds8, c_ctx.reshape(1, D))


def kernel(x, c, ctx, c_ctx, w_ada, b_ada, norm_g, rec_w_in, rec_conv_w, rec_conv_b, rec_lambda, rec_w_a, rec_b_a, rec_w_x, rec_b_x, rec_w_out, conf_w_pw1, conf_b_pw1, conf_conv_w, conf_conv_b, conf_ln_g, conf_ln_b, conf_w_pw2, conf_b_pw2, mlp_w_in, mlp_w_out, final_g, loss_target, m_c_ctx, m_w_ada, m_b_ada, m_norm_g, m_rec_w_in, m_rec_conv_w, m_rec_conv_b, m_rec_lambda, m_rec_w_a, m_rec_b_a, m_rec_w_x, m_rec_b_x, m_rec_w_out, m_conf_w_pw1, m_conf_b_pw1, m_conf_conv_w, m_conf_conv_b, m_conf_ln_g, m_conf_ln_b, m_conf_w_pw2, m_conf_b_pw2, m_mlp_w_in, m_mlp_w_out, m_final_g, v_c_ctx, v_w_ada, v_b_ada, v_norm_g, v_rec_w_in, v_rec_conv_w, v_rec_conv_b, v_rec_lambda, v_rec_w_a, v_rec_b_a, v_rec_w_x, v_rec_b_x, v_rec_w_out, v_conf_w_pw1, v_conf_b_pw1, v_conf_conv_w, v_conf_conv_b, v_conf_ln_g, v_conf_ln_b, v_conf_w_pw2, v_conf_b_pw2, v_mlp_w_in, v_mlp_w_out, v_final_g):
    names = ["c_ctx", "w_ada", "b_ada", "norm_g", "rec_w_in", "rec_conv_w", "rec_conv_b", "rec_lambda", "rec_w_a",
             "rec_b_a", "rec_w_x", "rec_b_x", "rec_w_out", "conf_w_pw1", "conf_b_pw1", "conf_conv_w", "conf_conv_b",
             "conf_ln_g", "conf_ln_b", "conf_w_pw2", "conf_b_pw2", "mlp_w_in", "mlp_w_out", "final_g"]
    w = dict(zip(names, [c_ctx, w_ada, b_ada, norm_g, rec_w_in, rec_conv_w, rec_conv_b, rec_lambda, rec_w_a,
                         rec_b_a, rec_w_x, rec_b_x, rec_w_out, conf_w_pw1, conf_b_pw1, conf_conv_w, conf_conv_b,
                         conf_ln_g, conf_ln_b, conf_w_pw2, conf_b_pw2, mlp_w_in, mlp_w_out, final_g]))
    m = dict(zip(names, [m_c_ctx, m_w_ada, m_b_ada, m_norm_g, m_rec_w_in, m_rec_conv_w, m_rec_conv_b, m_rec_lambda,
                         m_rec_w_a, m_rec_b_a, m_rec_w_x, m_rec_b_x, m_rec_w_out, m_conf_w_pw1, m_conf_b_pw1,
                         m_conf_conv_w, m_conf_conv_b, m_conf_ln_g, m_conf_ln_b, m_conf_w_pw2, m_conf_b_pw2,
                         m_mlp_w_in, m_mlp_w_out, m_final_g]))
    v = dict(zip(names, [v_c_ctx, v_w_ada, v_b_ada, v_norm_g, v_rec_w_in, v_rec_conv_w, v_rec_conv_b, v_rec_lambda,
                         v_rec_w_a, v_rec_b_a, v_rec_w_x, v_rec_b_x, v_rec_w_out, v_conf_w_pw1, v_conf_b_pw1,
                         v_conf_conv_w, v_conf_conv_b, v_conf_ln_g, v_conf_ln_b, v_conf_w_pw2, v_conf_b_pw2,
                         v_mlp_w_in, v_mlp_w_out, v_final_g]))
    mx, my, mc = _me()
    chip = 2 * mx + my
    me = 4 * mx + 2 * my + mc

    place = jnp.stack([chip, mc]).astype(jnp.int32)
    shards = [_halves(rec_w_in[0]), _halves(rec_w_out[0]), _halves(conf_w_pw1[0]), _halves(conf_w_pw2[0]),
              _halves(mlp_w_in[0]), _halves(mlp_w_in[1]), _halves(mlp_w_out[0]), _halves(mlp_w_out[1])]
    use_order = dict(rec=(0, 1), mlp0=(4, 6), conf=(2, 3), mlp1=(5, 7))
    flying, gsems = _gather_start(_place_big(shards, place), tuple(use_order.values()))

    def wg(group, after):
        gi = list(use_order).index(group)
        bufs = _gather_wait(f"gather_wait_{group}", [flying[t] for t in use_order[group]], gsems[2 * gi],
                            gsems[2 * gi + 1], after)
        a, b = _swap_halves(f"swap_{group}", bufs)
        if group == "rec":
            return dict(rec_w_in=a.reshape(4, D, RH), rec_w_out=b.reshape(R, D))
        if group == "conf":
            return dict(conf_w_pw1=a.reshape(4, D, D // 2), conf_w_pw2=b.reshape(D, D))
        return dict(w_in=a.reshape(4, D, D), w_out=b.reshape(FF, D))

    sharded_small = ["norm_g", "rec_conv_w", "rec_lambda", "conf_b_pw1", "conf_conv_w", "conf_conv_b", "conf_ln_g",
                     "conf_ln_b", "conf_b_pw2"]
    packed, offs = _pack([c] + [w[k] for k in sharded_small], 8)
    got = _allgather8("gather_small", packed)
    per_dev = [_unpack(got[d], offs) for d in range(8)]
    c_rows = jnp.concatenate([per_dev[d][0].reshape(1, D) for d in range(8)], axis=0)
    full = {k: jnp.concatenate([per_dev[2 * j][1 + i] for j in range(4)], axis=-1)
            for i, k in enumerate(sharded_small)}
    c16 = jnp.concatenate([c_rows, c_ctx.reshape(1, D), jnp.zeros((7, D), F32)], axis=0)

    ns = w_ada.shape[2]
    b_shard = lax.dynamic_slice_in_dim(b_ada, chip * ns, ns, axis=1).reshape(2, 1, ns)
    prod = _ada_fwd(c16, w_ada, b_shard)
    prod8 = _allgather8("gather_mod", prod.reshape(32, ns)).reshape(8, 2, 16, ns)
    mod_all = jnp.concatenate([prod8[2 * j] for j in range(4)], axis=-1)
    mods = lax.dynamic_index_in_dim(mod_all, me, axis=1, keepdims=False).reshape(2, 6, D)
    cmods = mod_all[0, 8].reshape(6, D)[:2]

    rec = dict(conv_w=full["rec_conv_w"][0], conv_b=rec_conv_b[0], lam=full["rec_lambda"][0],
               w_a=rec_w_a[0], b_a=rec_b_a[0], w_x=rec_w_x[0], b_x=rec_b_x[0])
    conf = dict(b_pw1=full["conf_b_pw1"][0], conv_w=full["conf_conv_w"][0], conv_b=full["conf_conv_b"][0],
                ln_g=full["conf_ln_g"][0], ln_b=full["conf_ln_b"][0], b_pw2=full["conf_b_pw2"][0])
    sent = {}

    def on_grads(group, dws):
        parts = [dw.reshape(4, 2, shards[t].shape[1], shards[t].shape[2]) for dw, t in zip(dws, use_order[group])]
        sent[group] = _reduce_begin(group, parts, place)

    loss_local, grad_x, _, small = _local_step(x[0], ctx[0], loss_target[0], mods, cmods, full["norm_g"], final_g,
                                               rec, conf, wg, on_grads)
    loss = lax.psum(loss_local, ("x", "y", "c"))

    small_names = ["d_mod", "d_cmod", "norm_g", "rec_conv_w", "rec_conv_b", "rec_lambda", "rec_w_a", "rec_b_a",
                   "rec_w_x", "rec_b_x", "conf_b_pw1", "conf_conv_w", "conf_conv_b", "conf_ln_g", "conf_ln_b",
                   "conf_b_pw2", "final_g"]
    mod_slots = lax.dynamic_update_slice(jnp.zeros((8, 2 * 6 * D), F32), small["d_mod"].reshape(1, -1), (me, 0))
    spacked, soffs = _pack([small[k] for k in small_names] + [mod_slots])
    stotal = _allreduce_small(spacked, place)
    unpacked = _unpack(stotal, soffs)
    ssum = dict(zip(small_names, unpacked[:-1]))
    dmod_rows = unpacked[-1].reshape(8, 2, 6 * D).transpose(1, 0, 2)

    fulls = {}
    for group in ("mlp1", "conf", "mlp0", "rec"):
        for t, f in zip(use_order[group], _reduce_end(group, sent[group], place, stotal)):
            fulls[t] = f
    whole = _share_halves("share_grads", [fulls[t] for t in range(8)])
    g_big = dict(rec_w_in=whole[0].reshape(rec_w_in.shape), rec_w_out=whole[1].reshape(rec_w_out.shape),
                 conf_w_pw1=whole[2].reshape(conf_w_pw1.shape), conf_w_pw2=whole[3].reshape(conf_w_pw2.shape),
                 mlp_w_in=jnp.stack([whole[4].reshape(D, D), whole[5].reshape(D, D)]),
                 mlp_w_out=jnp.stack([whole[6].reshape(D, D), whole[7].reshape(D, D)]))

    d_cmod_full =jnp.concatenate([ssum["d_cmod"].reshape(1, 2 * D), jnp.zeros((1, 4 * D), F32)], axis=1)
    dm16 = jnp.concatenate([dmod_rows, jnp.stack([d_cmod_full, jnp.zeros((1, 6 * D), F32)]),
                            jnp.zeros((2, 7, 6 * D), F32)], axis=1)
    dm16_shard = lax.dynamic_slice_in_dim(dm16, chip * ns, ns, axis=2)
    g_w_ada, ds_part = _ada_bwd(c16, dm16_shard, w_ada)
    ds8 = _allgather8("gather_dsilu", ds_part)
    g_c_ctx = _cctx_grad(ds8, c_ctx).reshape(D)
    g_b_ada = ssum["d_mod"] + jnp.stack([d_cmod_full[0], jnp.zeros((6 * D,), F32)])

    def shard_of(a, axis):
        n = a.shape[axis] // 4
        return lax.dynamic_slice_in_dim(a, chip * n, n, axis=axis)

    grads = dict(
        c_ctx=g_c_ctx, w_ada=g_w_ada, b_ada=g_b_ada,
        norm_g=shard_of(ssum["norm_g"].reshape(2, 2, D), 2),
        rec_w_in=g_big["rec_w_in"], rec_conv_w=shard_of(ssum["rec_conv_w"].reshape(1, REC_KW, R), 2),
        rec_conv_b=ssum["rec_conv_b"].reshape(1, R), rec_lambda=shard_of(ssum["rec_lambda"].reshape(1, 2, R), 2),
        rec_w_a=ssum["rec_w_a"].reshape(rec_w_a.shape), rec_b_a=ssum["rec_b_a"].reshape(rec_b_a.shape),
        rec_w_x=ssum["rec_w_x"].reshape(rec_w_x.shape), rec_b_x=ssum["rec_b_x"].reshape(rec_b_x.shape),
        rec_w_out=g_big["rec_w_out"], conf_w_pw1=g_big["conf_w_pw1"],
        conf_b_pw1=shard_of(ssum["conf_b_pw1"].reshape(1, 2 * D), 1),
        conf_conv_w=shard_of(ssum["conf_conv_w"].reshape(1, CONF_KW, D), 2),
        conf_conv_b=shard_of(ssum["conf_conv_b"].reshape(1, D), 1),
        conf_ln_g=shard_of(ssum["conf_ln_g"].reshape(1, D), 1), conf_ln_b=shard_of(ssum["conf_ln_b"].reshape(1, D), 1),
        conf_w_pw2=g_big["conf_w_pw2"], conf_b_pw2=shard_of(ssum["conf_b_pw2"].reshape(1, D), 1),
        mlp_w_in=g_big["mlp_w_in"], mlp_w_out=g_big["mlp_w_out"], final_g=ssum["final_g"].reshape(D))

    delta, new_m, new_v = {}, {}, {}
    big_names = ("w_ada",) + _BIG
    for k in big_names:
        cols = w[k].shape[-1]
        d_, m_, v_ = _adamw(f"adamw_{k}", w[k].reshape(-1, cols), grads[k].reshape(-1, cols),
                            m[k].reshape(-1, cols), v[k].reshape(-1, cols))
        delta[k], new_m[k], new_v[k] = (a.reshape(w[k].shape) for a in (d_, m_, v_))
    rest = [k for k in names if k not in big_names]
    pw, poffs = _pack([w[k] for k in rest])
    pg, _ = _pack([grads[k] for k in rest])
    pm, _ = _pack([m[k] for k in rest])
    pv, _ = _pack([v[k] for k in rest])
    d_, m_, v_ = _adamw("adamw_small", pw, pg, pm, pv)
    for k, dd, mm, vv in zip(rest, _unpack(d_, poffs), _unpack(m_, poffs), _unpack(v_, poffs)):
        delta[k], new_m[k], new_v[k] = dd, mm, vv

    return (loss, grad_x[None], *[grads[k] for k in names], *[delta[k] for k in names],
            *[new_m[k] for k in names], *[new_v[k] for k in names])
```

```python
import functools
import math

import jax
import jax.numpy as jnp
from jax import lax
from jax.experimental import pallas as pl
from jax.experimental.pallas import tpu as pltpu

F32 = jnp.float32
BF16 = jnp.bfloat16

D = 1024
T = 2048
TC = 256
TA = T + TC
R = 1280
RH = R // 2
NQ = 4 * RH
FF = 4096
N_BLK = 16
BLK = R // N_BLK
GRID_W = 64
EPS = 1e-6
RG_C = 8.0
CONF_KW = 31
REC_KW = 4
LANE = 128
ROW_TILE = 256
HALO = 16
RG_TILE = 128
PACK_ROWS = 512
V7X_VMEM_BYTES = 64 * 1024 * 1024
VMEM_LIMIT = V7X_VMEM_BYTES - 8 * 1024 * 1024

ADAM_LR = 0.001
ADAM_B1 = 0.9
ADAM_B2 = 0.999
ADAM_EPS = 1e-08
ADAM_WD = 0.01
ADAM_STEP = 10

MESH = pl.DeviceIdType.MESH
ANY = pl.BlockSpec(memory_space=pl.ANY)


def _sds(shape, dtype):
    return jax.ShapeDtypeStruct(tuple(shape), dtype)


def _pcall(body, **kw):
    return pl.pallas_call(body, **kw)


def _cparams():
    return pltpu.CompilerParams(vmem_limit_bytes=VMEM_LIMIT)


def _full_spec(arr):
    nd = arr.ndim
    return pl.BlockSpec(arr.shape, lambda *ids, _n=nd: (0,) * _n)


def _sum0(v):
    return jnp.sum(v, axis=0, keepdims=True)


def _tiled(name, fn, grid, ins, vecs, outs, vec_outs=(), vec_refs=False):
    n_in, n_vec, n_out = len(ins), len(vecs), len(outs)
    n_grid = len(grid)

    def kern(*refs):
        ids = [pl.program_id(a) for a in range(n_grid)]
        tin = [r[...] for r in refs[:n_in]]
        vin = list(refs[n_in:n_in + n_vec]) if vec_refs else [r[...] for r in refs[n_in:n_in + n_vec]]
        o_refs = refs[n_in + n_vec:n_in + n_vec + n_out]
        a_refs = refs[n_in + n_vec + n_out:]
        tout, incs = fn(ids, tin, vin)
        for r, v in zip(o_refs, tout):
            r[...] = v.astype(r.dtype)
        if a_refs:
            first = functools.reduce(jnp.logical_and, [i == 0 for i in ids])

            @pl.when(first)
            def _():
                for r in a_refs:
                    r[...] = jnp.zeros_like(r)

            for r, v in zip(a_refs, incs):
                r[...] += v

    out_shape = [o for o, _ in outs] + [_sds(s, F32) for s in vec_outs]
    out_specs = [s for _, s in outs] + [
        pl.BlockSpec(tuple(s), lambda *ids, _n=len(s): (0,) * _n) for s in vec_outs]
    res = _pcall(
        kern, name=name, grid=tuple(grid),
        in_specs=[s for _, s in ins] + [_full_spec(v) for v in vecs],
        out_specs=out_specs, out_shape=out_shape, compiler_params=_cparams(),
    )(*[a for a, _ in ins], *vecs)
    return list(res)


def _rows(arr, ncols=None, tm=ROW_TILE, off=0, col=0, clamp_lo=False):
    ncols = arr.shape[1] if ncols is None else ncols
    if clamp_lo:
        return arr, pl.BlockSpec((tm, ncols), lambda i: (jnp.maximum(i + off, 0), col))
    return arr, pl.BlockSpec((tm, ncols), lambda i: (i + off, col))


def _orow(nrows, ncols, dtype, tm=ROW_TILE, off=0, clamp_lo=False):
    if clamp_lo:
        return _sds((nrows, ncols), dtype), pl.BlockSpec((tm, ncols), lambda i: (jnp.maximum(i + off, 0), 0))
    return _sds((nrows, ncols), dtype), pl.BlockSpec((tm, ncols), lambda i: (i + off, 0))


_NN = (((1,), (0,)), ((), ()))
_TN = (((0,), (0,)), ((), ()))
_NT = (((1,), (1,)), ((), ()))


def _mm(name, a, b, dims, grid, a_spec, b_spec, out, acc_shape, extra=(), a_pre=None, epi=None):
    n_k = grid[2]
    n_ex = len(extra)

    def kern(a_ref, b_ref, *rest):
        ex = rest[:n_ex]
        o_refs = rest[n_ex:-1]
        acc = rest[-1]
        k = pl.program_id(2)

        @pl.when(k == 0)
        def _():
            acc[...] = jnp.zeros_like(acc)

        av = a_ref[...]
        if a_pre is not None:
            av = a_pre(av)
        acc[...] += lax.dot_general(av.astype(BF16), b_ref[...].astype(BF16), dims,
                                    preferred_element_type=F32)

        @pl.when(k == n_k - 1)
        def _():
            vals = [acc[...]] if epi is None else epi(acc[...], [e[...] for e in ex])
            for r, v in zip(o_refs, vals):
                r[...] = v.astype(r.dtype)

    res = _pcall(
        kern, name=name, grid=tuple(grid),
        in_specs=[a_spec, b_spec] + [s for _, s in extra],
        out_specs=[s for _, s in out], out_shape=[o for o, _ in out],
        scratch_shapes=[pltpu.VMEM(tuple(acc_shape), F32)], compiler_params=_cparams(),
    )(a, b, *[e for e, _ in extra])
    return list(res)


def _rms(x):
    r = lax.rsqrt(jnp.mean(x * x, axis=-1, keepdims=True) + EPS)
    return x * r, r


def _norm_mod(x, g, sc, sh):
    n, _ = _rms(x)
    return (n * g) * (1.0 + sc) + sh


def _norm_mod_bwd(dh, x, g, sc):
    n, r = _rms(x)
    d_sh = _sum0(dh)
    d_sc = _sum0(dh * (n * g))
    d_g = _sum0(dh * (1.0 + sc) * n)
    dn = dh * (g * (1.0 + sc))
    dx = r * (dn - n * jnp.mean(dn * n, axis=-1, keepdims=True))
    return dx, d_sh, d_sc, d_g


_GELU_K = math.sqrt(2.0 / math.pi)


def _gelu(x):
    t = jnp.tanh(_GELU_K * (x + 0.044715 * x * x * x))
    return 0.5 * x * (1.0 + t), t


def _gelu_grad(x, t):
    return 0.5 * (1.0 + t) + 0.5 * x * (1.0 - t * t) * (_GELU_K * (1.0 + 3.0 * 0.044715 * x * x))


def _sigmoid(x):
    return 1.0 / (1.0 + jnp.exp(-x))


def _expm1(x):
    p = 1.0 + x * (1.0 / 9.0)
    for n in (8.0, 7.0, 6.0, 5.0, 4.0, 3.0, 2.0):
        p = 1.0 + (x * (1.0 / n)) * p
    return jnp.where(jnp.abs(x) < 0.5, x * p, jnp.exp(x) - 1.0)


def _softplus_neg(lam):
    return jnp.log1p(jnp.exp(-jnp.abs(lam))) + jnp.maximum(-lam, 0.0)


def _layernorm_parts(x):
    mu = jnp.mean(x, axis=-1, keepdims=True)
    xc = x - mu
    rstd = lax.rsqrt(jnp.mean(xc * xc, axis=-1, keepdims=True) + EPS)
    return xc * rstd, rstd


def _rg_gates(u, wbd, gbias, lam):
    sp = _softplus_neg(lam)
    parts = {}
    for h in range(2):
        uh = u[:, h * RH:(h + 1) * RH]
        g = jnp.dot(uh.astype(BF16), wbd[h], preferred_element_type=F32) + gbias[:, h * NQ:(h + 1) * NQ]
        for d in range(2):
            r = _sigmoid(g[:, (2 * d) * RH:(2 * d + 1) * RH])
            i = _sigmoid(g[:, (2 * d + 1) * RH:(2 * d + 2) * RH])
            sph = sp[d:d + 1, h * RH:(h + 1) * RH]
            la = (-RG_C) * r * sph
            e2 = _expm1(2.0 * la)
            parts[(d, h)] = dict(r=r, i=i, la=la, a=jnp.exp(la), e2=e2, mult=jnp.sqrt(-e2), uh=uh, sp=sph)
    return parts


def _rg_fwd_fn(ids, tin, vin):
    (u,) = tin
    wbd = vin[0]
    parts = _rg_gates(u, wbd, vin[1][...], vin[2][...])
    outs = []
    for d in range(2):
        a = jnp.concatenate([parts[(d, h)]["a"] for h in range(2)], axis=1)
        b = jnp.concatenate([parts[(d, h)]["mult"] * parts[(d, h)]["i"] * parts[(d, h)]["uh"]
                             for h in range(2)], axis=1)
        outs += [a, b]
    return outs, []


def _rg_bwd_fn(ids, tin, vin):
    u, da_f, db_f, da_r, db_r = tin
    wbd, lam = vin[0], vin[2][...]
    parts = _rg_gates(u, wbd, vin[1][...], lam)
    dab = ((da_f, db_f), (da_r, db_r))
    dsig_lam = -_sigmoid(-lam)
    du_halves, dpre_halves, dlam = [], [], [[None, None], [None, None]]
    for h in range(2):
        du = jnp.zeros_like(parts[(0, h)]["uh"])
        dpre = []
        for d in range(2):
            p = parts[(d, h)]
            da = dab[d][0][:, h * RH:(h + 1) * RH]
            db = dab[d][1][:, h * RH:(h + 1) * RH]
            d_mult = db * p["i"] * p["uh"]
            d_i = db * p["mult"] * p["uh"]
            du = du + db * p["mult"] * p["i"]
            d_la = da * p["a"] - d_mult * (p["e2"] + 1.0) / p["mult"]
            d_r = d_la * ((-RG_C) * p["sp"])
            dlam[d][h] = _sum0(d_la * ((-RG_C) * p["r"])) * dsig_lam[d:d + 1, h * RH:(h + 1) * RH]
            dpre += [d_r * p["r"] * (1.0 - p["r"]), d_i * p["i"] * (1.0 - p["i"])]
        dpre = jnp.concatenate(dpre, axis=1)
        du = du + lax.dot_general(dpre.astype(BF16), wbd[h], _NT, preferred_element_type=F32)
        du_halves.append(du)
        dpre_halves.append(dpre)
    dpre_all = jnp.concatenate(dpre_halves, axis=1)
    dlam_row = jnp.concatenate([dlam[0][0], dlam[0][1], dlam[1][0], dlam[1][1]], axis=1)
    return [dpre_all, jnp.concatenate(du_halves, axis=1)], [_sum0(dpre_all), dlam_row]


def _tile_flags(i, n_tiles, seq_starts):
    starts_here = functools.reduce(jnp.logical_or, [i == s for s in seq_starts])
    ends_here = functools.reduce(jnp.logical_or, [i + 1 == s for s in seq_starts] + [i + 1 == n_tiles])
    return jnp.logical_not(starts_here), jnp.logical_not(ends_here)


def _halo_specs(col0, cw):
    hb = ROW_TILE // HALO
    prev = pl.BlockSpec((HALO, cw), lambda i, c: (jnp.maximum(i * hb - 1, 0), col0 + c))
    cur = pl.BlockSpec((ROW_TILE, cw), lambda i, c: (i, col0 + c))
    return prev, cur, hb


def _window(prev_ref, cur_ref, next_ref, has_prev, has_next):
    prev = jnp.where(has_prev, prev_ref[...], 0.0)
    nxt = jnp.where(has_next, next_ref[...], 0.0)
    return jnp.concatenate([prev, cur_ref[...], nxt], axis=0)


def _dwconv(name, x, col0, w, bias, pad_left, seq_starts, n_ch, cw=256):
    n_rows = x.shape[0]
    n_tiles = n_rows // ROW_TILE
    n_taps = w.shape[0]
    prev_spec, cur_spec, hb = _halo_specs(col0, cw)
    last_hb = n_rows // HALO - 1
    next_spec = pl.BlockSpec((HALO, cw), lambda i, c: (jnp.minimum((i + 1) * hb, last_hb), col0 + c))

    def kern(prev_ref, cur_ref, next_ref, w_ref, b_ref, o_ref):
        has_prev, has_next = _tile_flags(pl.program_id(0), n_tiles, seq_starts)
        win = _window(prev_ref, cur_ref, next_ref, has_prev, has_next)
        wv = w_ref[...]
        acc = jnp.zeros((ROW_TILE, cw), F32) + b_ref[...]
        for k in range(n_taps):
            off = HALO + k - pad_left
            acc = acc + wv[k:k + 1, :] * win[off:off + ROW_TILE, :]
        o_ref[...] = acc

    return _pcall(
        kern, name=name, grid=(n_tiles, n_ch // cw),
        in_specs=[prev_spec, cur_spec, next_spec,
                  pl.BlockSpec((n_taps, cw), lambda i, c: (0, c)), pl.BlockSpec((1, cw), lambda i, c: (0, c))],
        out_specs=pl.BlockSpec((ROW_TILE, cw), lambda i, c: (i, c)),
        out_shape=_sds((n_rows, n_ch), F32), compiler_params=_cparams(),
    )(x, x, x, w, bias)


def _dwconv_wgrad(name, dy, x, col0, n_taps, pad_left, seq_starts, n_ch, cw=256):
    n_rows = dy.shape[0]
    n_tiles = n_rows // ROW_TILE
    n_out = -(-(n_taps + 1) // 8) * 8
    prev_spec, cur_spec, hb = _halo_specs(col0, cw)
    last_hb = n_rows // HALO - 1
    next_spec = pl.BlockSpec((HALO, cw), lambda c, i: (jnp.minimum((i + 1) * hb, last_hb), col0 + c))
    prev_spec = pl.BlockSpec((HALO, cw), lambda c, i: (jnp.maximum(i * hb - 1, 0), col0 + c))
    cur_spec = pl.BlockSpec((ROW_TILE, cw), lambda c, i: (i, col0 + c))

    def kern(dy_ref, prev_ref, cur_ref, next_ref, o_ref):
        i = pl.program_id(1)
        has_prev, has_next = _tile_flags(i, n_tiles, seq_starts)
        win = _window(prev_ref, cur_ref, next_ref, has_prev, has_next)
        dyv = dy_ref[...]
        rid = lax.broadcasted_iota(jnp.int32, (n_out, cw), 0)
        inc = jnp.where(rid == n_taps, _sum0(dyv), 0.0)
        for k in range(n_taps):
            off = HALO + k - pad_left
            inc = inc + jnp.where(rid == k, _sum0(dyv * win[off:off + ROW_TILE, :]), 0.0)

        @pl.when(i == 0)
        def _():
            o_ref[...] = jnp.zeros_like(o_ref)

        o_ref[...] += inc

    return _pcall(
        kern, name=name, grid=(n_ch // cw, n_tiles),
        in_specs=[pl.BlockSpec((ROW_TILE, cw), lambda c, i: (i, c)), prev_spec, cur_spec, next_spec],
        out_specs=pl.BlockSpec((n_out, cw), lambda c, i: (0, c)),
        out_shape=_sds((n_out, n_ch), F32), compiler_params=_cparams(),
    )(dy, x, x, x)


N_SCAN = TA // ROW_TILE


def _rev_block(j):
    return jnp.where(j == 0, 0, N_SCAN - j)


def _scan_fwd(a_f, b_f, a_r, b_r):
    fwd_spec = pl.BlockSpec((ROW_TILE, R), lambda i: (i, 0))
    rev_spec = pl.BlockSpec((ROW_TILE, R), lambda i: (_rev_block(i), 0))
    hin_spec = pl.BlockSpec((None, 1, R), lambda i: (i, 0, 0))

    def kern(af, bf, ar, br, yf, yr, hin_f, hin_r, hf_s, hr_s):
        @pl.when(pl.program_id(0) == 0)
        def _():
            hf_s[...] = jnp.zeros_like(hf_s)
            hr_s[...] = jnp.zeros_like(hr_s)

        hin_f[...] = hf_s[...]
        hin_r[...] = hr_s[...]

        def step(s8, carry):
            hf, hr = carry
            t0 = pl.multiple_of(s8 * 8, 8)
            for q in range(8):
                tf = t0 + q
                hf = af[pl.ds(tf, 1), :] * hf + bf[pl.ds(tf, 1), :]
                yf[pl.ds(tf, 1), :] = hf
                tr = ROW_TILE - 1 - tf
                hr = ar[pl.ds(tr, 1), :] * hr + br[pl.ds(tr, 1), :]
                yr[pl.ds(tr, 1), :] = hr
            return hf, hr

        hf, hr = lax.fori_loop(0, ROW_TILE // 8, step, (hf_s[...], hr_s[...]))
        hf_s[...] = hf
        hr_s[...] = hr

    return _pcall(
        kern, name="scan_fwd", grid=(N_SCAN,),
        in_specs=[fwd_spec, fwd_spec, rev_spec, rev_spec],
        out_specs=[fwd_spec, rev_spec, hin_spec, hin_spec],
        out_shape=[_sds((TA, R), F32), _sds((TA, R), F32), _sds((N_SCAN, 1, R), F32), _sds((N_SCAN, 1, R), F32)],
        scratch_shapes=[pltpu.VMEM((1, R), F32), pltpu.VMEM((1, R), F32)], compiler_params=_cparams(),
    )(a_f, b_f, a_r, b_r)


def _scan_bwd(dy, a_f, y_f, hin_f, a_r, y_r, hin_r):
    fwd_spec = pl.BlockSpec((ROW_TILE, R), lambda i: (N_SCAN - 1 - i, 0))
    rev_spec = pl.BlockSpec((ROW_TILE, R), lambda i: (_rev_block(N_SCAN - 1 - i), 0))
    hin_spec = pl.BlockSpec((None, 1, R), lambda i: (N_SCAN - 1 - i, 0, 0))
    last = ROW_TILE - 1

    def kern(dyf, af, yf, hf0, dyr, ar, yr, hr0, daf, dbf, dar, dbr, gf_s, anf_s, gr_s, anr_s):
        @pl.when(pl.program_id(0) == 0)
        def _():
            for r in (gf_s, anf_s, gr_s, anr_s):
                r[...] = jnp.zeros_like(r)

        def one(dy_ref, a_ref, y_ref, da_ref, db_ref, g, an, p, pprev):
            gnew = dy_ref[pl.ds(p, 1), :] + an * g
            db_ref[pl.ds(p, 1), :] = gnew
            da_ref[pl.ds(p, 1), :] = gnew * y_ref[pl.ds(pprev, 1), :]
            return gnew, a_ref[pl.ds(p, 1), :]

        def step(s8, carry):
            gf, anf, gr, anr = carry
            base = s8 * 8
            for q in range(8):
                s = last - (base + q)
                gf, anf = one(dyf, af, yf, daf, dbf, gf, anf, s, s - 1)
                gr, anr = one(dyr, ar, yr, dar, dbr, gr, anr, last - s, last - s + 1)
            return gf, anf, gr, anr

        carry = (gf_s[...], anf_s[...], gr_s[...], anr_s[...])
        carry = lax.fori_loop(0, ROW_TILE // 8 - 1, step, carry)
        gf, anf, gr, anr = carry
        for s in range(7, 0, -1):
            gf, anf = one(dyf, af, yf, daf, dbf, gf, anf, s, s - 1)
            gr, anr = one(dyr, ar, yr, dar, dbr, gr, anr, last - s, last - s + 1)
        gf0 = dyf[0:1, :] + anf * gf
        dbf[0:1, :] = gf0
        daf[0:1, :] = gf0 * hf0[...]
        gr0 = dyr[last:last + 1, :] + anr * gr
        dbr[last:last + 1, :] = gr0
        dar[last:last + 1, :] = gr0 * hr0[...]
        gf_s[...] = gf0
        anf_s[...] = af[0:1, :]
        gr_s[...] = gr0
        anr_s[...] = ar[last:last + 1, :]

    return _pcall(
        kern, name="scan_bwd", grid=(N_SCAN,),
        in_specs=[fwd_spec, fwd_spec, fwd_spec, hin_spec, rev_spec, rev_spec, rev_spec, hin_spec],
        out_specs=[fwd_spec, fwd_spec, rev_spec, rev_spec],
        out_shape=[_sds((TA, R), F32)] * 4,
        scratch_shapes=[pltpu.VMEM((1, R), F32)] * 4, compiler_params=_cparams(),
    )(dy, a_f, y_f, hin_f, dy, a_r, y_r, hin_r)


def _me():
    return lax.axis_index("x"), lax.axis_index("y"), lax.axis_index("c")


def _other_chips(mx, my):
    return [(1 - mx, my), (mx, 1 - my), (1 - mx, 1 - my)]


def _rcopy(src, dst, ssem, rsem, dev):
    return pltpu.make_async_remote_copy(src_ref=src, dst_ref=dst, send_sem=ssem, recv_sem=rsem,
                                        device_id=dev, device_id_type=MESH)


def _allgather8(name, x):
    rows, cols = x.shape

    def kern(x_ref, o_ref, ssem, rsem, lsem):
        mx, my, mc = _me()
        me = 4 * mx + 2 * my + mc
        peers = []
        for k in range(1, 8):
            px = 1 - mx if (k >> 2) & 1 else mx
            py = 1 - my if (k >> 1) & 1 else my
            pc = 1 - mc if k & 1 else mc
            peers.append((px, py, pc))
        mine = pltpu.make_async_copy(x_ref, o_ref.at[me], lsem)
        mine.start()
        sends = [_rcopy(x_ref, o_ref.at[me], ssem.at[k], rsem.at[k], p) for k, p in enumerate(peers)]
        for cp in sends:
            cp.start()
        for k, (px, py, pc) in enumerate(peers):
            _rcopy(x_ref, o_ref.at[4 * px + 2 * py + pc], ssem.at[k], rsem.at[k], (px, py, pc)).wait_recv()
        for cp in sends:
            cp.wait_send()
        mine.wait()

    return _pcall(
        kern, name=name, in_specs=[ANY], out_specs=ANY, out_shape=_sds((8, rows, cols), F32),
        scratch_shapes=[pltpu.SemaphoreType.DMA((7,)), pltpu.SemaphoreType.DMA((7,)), pltpu.SemaphoreType.DMA(())],
    )(x)


def _gather_chips(name, ws):
    n = len(ws)

    def kern(*refs):
        o = refs[n:2 * n]
        s1, r1, s2, r2 = refs[2 * n:]
        mx, my, mc = _me()
        j0 = 2 * mx + my
        chips = _other_chips(mx, my)
        sib = (mx, my, 1 - mc)
        firsts = []
        for t in range(n):
            for q, (qx, qy) in enumerate(chips):
                cp = _rcopy(o[t].at[j0, mc], o[t].at[j0, mc], s1.at[3 * t + q], r1.at[3 * t + q], (qx, qy, mc))
                cp.start()
                firsts.append(cp)
        passed = []
        for t in range(n):
            for q, (qx, qy) in enumerate(chips):
                jq = 2 * qx + qy
                _rcopy(o[t].at[jq, mc], o[t].at[jq, mc], s1.at[3 * t + q], r1.at[3 * t + q], (qx, qy, mc)).wait_recv()
                fw = _rcopy(o[t].at[jq, mc], o[t].at[jq, mc], s2.at[3 * t + q], r2.at[3 * t + q], sib)
                fw.start()
                passed.append(fw)
        for t in range(n):
            for q, (qx, qy) in enumerate(chips):
                jq = 2 * qx + qy
                _rcopy(o[t].at[jq, 1 - mc], o[t].at[jq, 1 - mc], s2.at[3 * t + q], r2.at[3 * t + q], sib).wait_recv()
        for cp in firsts + passed:
            cp.wait_send()

    dma = pltpu.SemaphoreType.DMA
    return _pcall(
        kern, name=name, in_specs=[ANY] * n, out_specs=[ANY] * n,
        out_shape=[_sds(w.shape, w.dtype) for w in ws], input_output_aliases={t: t for t in range(n)},
        scratch_shapes=[dma((3 * n,)), dma((3 * n,)), dma((3 * n,)), dma((3 * n,))],
    )(*ws)


def _reduce_pair(name, gs):
    n = len(gs)

    def kern(*refs):
        g, o = refs[:n], refs[n:2 * n]
        ss, rs = refs[2 * n:]
        mx, my, mc = _me()
        sib = (mx, my, 1 - mc)
        sends = []
        for t in range(n):
            for j in range(4):
                cp = _rcopy(g[t].at[j, 1 - mc], o[t].at[j], ss.at[4 * t + j], rs.at[4 * t + j], sib)
                cp.start()
                sends.append(cp)
        for cp in sends:
            cp.wait_recv()
        for cp in sends:
            cp.wait_send()

    dma = pltpu.SemaphoreType.DMA
    return _pcall(
        kern, name=name, in_specs=[ANY] * n, out_specs=[ANY] * n,
        out_shape=[_sds((4,) + g.shape[2:], g.dtype) for g in gs],
        scratch_shapes=[dma((4 * n,)), dma((4 * n,))],
    )(*gs)


def _share_halves(name, fulls):
    n = len(fulls)

    def kern(*refs):
        o = refs[n:2 * n]
        ss, rs = refs[2 * n:]
        mx, my, mc = _me()
        sib = (mx, my, 1 - mc)
        sends = []
        for t in range(n):
            cp = _rcopy(o[t].at[mc], o[t].at[mc], ss.at[t], rs.at[t], sib)
            cp.start()
            sends.append(cp)
        for t in range(n):
            _rcopy(o[t].at[1 - mc], o[t].at[1 - mc], ss.at[t], rs.at[t], sib).wait_recv()
        for cp in sends:
            cp.wait_send()

    dma = pltpu.SemaphoreType.DMA
    return _pcall(
        kern, name=name, in_specs=[ANY] * n, out_specs=[ANY] * n,
        out_shape=[_sds(f.shape, f.dtype) for f in fulls], input_output_aliases={t: t for t in range(n)},
        scratch_shapes=[dma((n,)), dma((n,))],
    )(*fulls)


def _tiled_sp(name, fn, grid, sp, ins, outs):
    n_in = len(ins)

    def kern(sp_ref, *refs):
        tout = fn([r[...] for r in refs[:n_in]])
        for r, v in zip(refs[n_in:], tout):
            r[...] = v.astype(r.dtype)

    gs = pltpu.PrefetchScalarGridSpec(num_scalar_prefetch=1, grid=tuple(grid),
                                      in_specs=[s for _, s in ins], out_specs=[s for _, s in outs])
    res = _pcall(kern, name=name, grid_spec=gs, out_shape=[o for o, _ in outs], compiler_params=_cparams(),
                 )(sp, *[a for a, _ in ins])
    return list(res)


def _row_tile(rows, cols, itemsize=4, budget=2 * 1024 * 1024):
    tr = rows
    while tr * cols * itemsize > budget and tr % 32 == 0:
        tr //= 2
    return tr


def _place_big(shards, place):
    slots = []
    for t, s in enumerate(shards):
        rr, cc = s.shape[1], s.shape[2]
        tr = _row_tile(rr, cc)
        (slot,) = _tiled_sp(
            f"place{t}", lambda tin: [tin[0]], (2, rr // tr), place,
            [(s, pl.BlockSpec((None, tr, cc), lambda h, i, sp: (h, i, 0)))],
            [(_sds((4, 2, rr, cc), BF16), pl.BlockSpec((None, None, tr, cc), lambda h, i, sp: (sp[0], h, i, 0)))])
        slots.append(slot)
    return slots


def _allreduce_small(vec, place):
    hr = vec.shape[0] // 2
    tr = _row_tile(hr, LANE)
    blk = (None, None, tr, LANE)
    (pair,) = _tiled_sp(
        "small_place", lambda tin: [tin[0]], (2, hr // tr), place,
        [(vec.reshape(2, hr, LANE), pl.BlockSpec((None, tr, LANE), lambda h, i, sp: (h, i, 0)))],
        [(_sds((2, 2, hr, LANE), F32), pl.BlockSpec(blk, lambda h, i, sp: (sp[1], h, i, 0)))])
    (pair,) = _share_halves("small_share", [pair])
    (slot,) = _tiled_sp(
        "small_pair_add", lambda tin: [tin[0] + tin[1]], (2, hr // tr), place,
        [(pair, pl.BlockSpec(blk, lambda h, i, sp: (0, h, i, 0))),
         (pair, pl.BlockSpec(blk, lambda h, i, sp: (1, h, i, 0)))],
        [(_sds((4, 2, hr, LANE), F32), pl.BlockSpec(blk, lambda h, i, sp: (sp[0], h, i, 0)))])
    (chips,) = _gather_chips("small_gather", [slot])
    (total,) = _tiled(
        "small_chip_sum", lambda ids, tin, vin: ([((tin[0] + tin[1]) + tin[2]) + tin[3]], []), (2, hr // tr),
        [(chips, pl.BlockSpec(blk, lambda h, i, _j=j: (_j, h, i, 0))) for j in range(4)], [],
        [(_sds((2, hr, LANE), F32), pl.BlockSpec((None, tr, LANE), lambda h, i: (h, i, 0)))])
    return total.reshape(2 * hr, LANE)


SEM =pl.BlockSpec(memory_space=pltpu.SEMAPHORE)
_DATAFLOW = pltpu.SideEffectType.DATAFLOW_SIDE_EFFECTING


def _gather_start(slots, groups):
    n = len(slots)

    def kern(*refs):
        o = refs[n:2 * n]
        sems = refs[2 * n:]
        mx, my, mc = _me()
        j0 = 2 * mx + my
        for gi, grp in enumerate(groups):
            for k, t in enumerate(grp):
                for q, (qx, qy) in enumerate(_other_chips(mx, my)):
                    _rcopy(o[t].at[j0, mc], o[t].at[j0, mc], sems[2 * gi].at[3 * k + q],
                           sems[2 * gi + 1].at[3 * k + q], (qx, qy, mc)).start()

    sem_shapes = []
    for grp in groups:
        sem_shapes += [pltpu.SemaphoreType.DMA((3 * len(grp),))] * 2
    res = _pcall(
        kern, name="gather_start", in_specs=[ANY] * n, out_specs=[ANY] * n + [SEM] * len(sem_shapes),
        out_shape=[_sds(w.shape, w.dtype) for w in slots] + sem_shapes,
        input_output_aliases={t: t for t in range(n)},
        compiler_params=pltpu.CompilerParams(has_side_effects=_DATAFLOW),
    )(*slots)
    return list(res[:n]), list(res[n:])


def _gather_wait(name, bufs, ssem, rsem, after):
    n = len(bufs)

    def kern(*refs):
        b = refs[:n]
        ssem_ref, rsem_ref = refs[n], refs[n + 1]
        mx, my, mc = _me()
        j0 = 2 * mx + my
        for k in range(n):
            for q, (qx, qy) in enumerate(_other_chips(mx, my)):
                jq = 2 * qx + qy
                _rcopy(b[k].at[jq, mc], b[k].at[jq, mc], ssem_ref.at[3 * k + q], rsem_ref.at[3 * k + q],
                       (qx, qy, mc)).wait_recv()
                _rcopy(b[k].at[j0, mc], b[k].at[j0, mc], ssem_ref.at[3 * k + q], rsem_ref.at[3 * k + q],
                       (qx, qy, mc)).wait_send()

    return list(_pcall(
        kern, name=name, in_specs=[ANY] * n + [SEM, SEM, ANY], out_specs=[ANY] * n,
        out_shape=[_sds(w.shape, w.dtype) for w in bufs], input_output_aliases={k: k for k in range(n)},
        compiler_params=pltpu.CompilerParams(has_side_effects=_DATAFLOW),
    )(*bufs, ssem, rsem, after))


def _swap_halves(name, bufs):
    n = len(bufs)

    def kern(*refs):
        o = refs[n:2 * n]
        ss, rs = refs[2 * n:]
        mx, my, mc = _me()
        sib = (mx, my, 1 - mc)
        sends = []
        for k in range(n):
            for q, (qx, qy) in enumerate(_other_chips(mx, my)):
                jq = 2 * qx + qy
                cp = _rcopy(o[k].at[jq, mc], o[k].at[jq, mc], ss.at[3 * k + q], rs.at[3 * k + q], sib)
                cp.start()
                sends.append(cp)
        for k in range(n):
            for q, (qx, qy) in enumerate(_other_chips(mx, my)):
                jq = 2 * qx + qy
                _rcopy(o[k].at[jq, 1 - mc], o[k].at[jq, 1 - mc], ss.at[3 * k + q], rs.at[3 * k + q], sib).wait_recv()
        for cp in sends:
            cp.wait_send()

    dma = pltpu.SemaphoreType.DMA
    return list(_pcall(
        kern, name=name, in_specs=[ANY] * n, out_specs=[ANY] * n,
        out_shape=[_sds(w.shape, w.dtype) for w in bufs], input_output_aliases={k: k for k in range(n)},
        scratch_shapes=[dma((3 * n,)), dma((3 * n,))],
    )(*bufs))


def _chips_start(name, sums):
    n = len(sums)

    def kern(*refs):
        s, land = refs[n:2 * n], refs[2 * n:3 * n]
        ssem, rsem, token = refs[3 * n:]
        mx, my, mc = _me()
        for k in range(n):
            for q, (qx, qy) in enumerate(_other_chips(mx, my)):
                _rcopy(s[k].at[2 * qx + qy], land[k].at[q], ssem.at[3 * k + q], rsem.at[3 * k + q], (qx, qy, mc)).start()
        token[...] = jnp.zeros_like(token)

    dma = pltpu.SemaphoreType.DMA
    res = _pcall(
        kern, name=name, in_specs=[ANY] * n,
        out_specs=[ANY] * (2 * n) + [SEM, SEM, pl.BlockSpec(memory_space=pltpu.VMEM)],
        out_shape=[_sds(s.shape, s.dtype) for s in sums] + [_sds((3,) + s.shape[1:], s.dtype) for s in sums]
        + [dma((3 * n,)), dma((3 * n,)), _sds((8, LANE), F32)],
        input_output_aliases={k: k for k in range(n)},
        compiler_params=pltpu.CompilerParams(has_side_effects=_DATAFLOW),
    )(*sums)
    return (list(res[:n]), list(res[n:2 * n]), res[2 * n], res[2 * n + 1]), res[2 * n + 2]


def _chips_wait(name, sums, lands, ssem, rsem, after):
    n = len(sums)

    def kern(*refs):
        s, land = refs[:n], refs[n:2 * n]
        ssem_ref, rsem_ref = refs[2 * n], refs[2 * n + 1]
        mx, my, mc = _me()
        for k in range(n):
            for q, (qx, qy) in enumerate(_other_chips(mx, my)):
                cp = _rcopy(s[k].at[2 * qx + qy], land[k].at[q], ssem_ref.at[3 * k + q], rsem_ref.at[3 * k + q],
                            (qx, qy, mc))
                cp.wait_recv()
                cp.wait_send()

    res = _pcall(
        kern, name=name, in_specs=[ANY] * (2 * n) + [SEM, SEM, ANY], out_specs=[ANY] * (2 * n),
        out_shape=[_sds(a.shape, a.dtype) for a in list(sums) + list(lands)],
        input_output_aliases={k: k for k in range(2 * n)},
        compiler_params=pltpu.CompilerParams(has_side_effects=_DATAFLOW),
    )(*sums, *lands, ssem, rsem, after)
    return list(res[:n]), list(res[n:])


def _reduce_begin(tag, parts, place):
    theirs = _reduce_pair(f"reduce_pair_{tag}", parts)
    sums = []
    for k, (p, o) in enumerate(zip(parts, theirs)):
        rr, cc = p.shape[2], p.shape[3]
        tr = _row_tile(rr, cc)
        (s_k,) = _tiled_sp(
            f"pair_add_{tag}{k}", lambda tin: [tin[0].astype(F32) + tin[1].astype(F32)], (4, rr // tr), place,
            [(p, pl.BlockSpec((None, None, tr, cc), lambda j, i, sp: (j, sp[1], i, 0))),
             (o, pl.BlockSpec((None, tr, cc), lambda j, i, sp: (j, i, 0)))],
            [(_sds((4, rr, cc), BF16), pl.BlockSpec((None, tr, cc), lambda j, i, sp: (j, i, 0)))])
        sums.append(s_k)
    return _chips_start(f"chips_start_{tag}", sums)


def _reduce_end(tag, flying, place, after):
    sums, lands = _chips_wait(f"chips_wait_{tag}", *flying, after)
    fulls = []
    for k, (s, q) in enumerate(zip(sums, lands)):
        rr, cc = q.shape[1], q.shape[2]
        tr = _row_tile(rr, cc)

        def add4(tin):
            return [((tin[0].astype(F32) + tin[1].astype(F32)) + tin[2].astype(F32)) + tin[3].astype(F32)]

        ins = [(s, pl.BlockSpec((None, tr, cc), lambda i, sp: (sp[0], i, 0)))]
        ins += [(q, pl.BlockSpec((None, tr, cc), lambda i, sp, _k=kk: (_k, i, 0))) for kk in range(3)]
        (f_k,) = _tiled_sp(f"chip_add_{tag}{k}", add4, (rr // tr,), place, ins,
                           [(_sds((2, rr, cc), F32), pl.BlockSpec((None, tr, cc), lambda i, sp: (sp[1], i, 0)))])
        fulls.append(f_k)
    return fulls


def _pack(parts, PACK_ROWS=PACK_ROWS):
    flat, offs, pos = [], [], 0
    for p in parts:
        v = p.reshape(-1).astype(F32)
        n = -(-v.shape[0] // LANE) * LANE
        flat.append(jnp.pad(v, (0, n - v.shape[0])))
        offs.append((pos, v.shape[0], p.shape))
        pos += n
    total = -(-pos // (PACK_ROWS * LANE)) * PACK_ROWS * LANE
    flat.append(jnp.zeros((total - pos,), F32))
    return jnp.concatenate(flat).reshape(-1, LANE), offs


def _unpack(vec, offs):
    v = vec.reshape(-1)
    return [v[p:p + n].reshape(shape) for p, n, shape in offs]


def _adamw(name, w, g, m, v):
    rows, cols = w.shape
    tr = rows
    for cand in (512, 256, 128, 64, 32, 16, 8):
        if rows % cand == 0 and cand * cols * 4 <= 2 * 1024 * 1024:
            tr = cand
            break
    bc1 = 1.0 - ADAM_B1 ** ADAM_STEP
    bc2 = 1.0 - ADAM_B2 ** ADAM_STEP

    def fn(ids, tin, vin):
        wv, gv, mv, vv = tin
        mn = ADAM_B1 * mv + (1.0 - ADAM_B1) * gv
        vn = ADAM_B2 * vv + (1.0 - ADAM_B2) * (gv * gv)
        delta = -ADAM_LR * ((mn / bc1) / (jnp.sqrt(vn / bc2) + ADAM_EPS) + ADAM_WD * wv)
        return [delta, mn, vn], []

    spec = pl.BlockSpec((tr, cols), lambda i: (i, 0))
    outs = [(_sds((rows, cols), F32), spec)] * 3
    return _tiled(name, fn, (rows // tr,), [(a, spec) for a in (w, g, m, v)], [], outs)


def _pos_embed():
    n_rows = T // GRID_W
    q = D // 4
    omega = 1.0 / (10000.0 ** (jnp.arange(q, dtype=F32) / q))
    er = jnp.arange(n_rows, dtype=jnp.int32).astype(F32)[:, None] * omega[None, :]
    ec = jnp.arange(GRID_W, dtype=jnp.int32).astype(F32)[:, None] * omega[None, :]
    by_row = jnp.concatenate([jnp.sin(er), jnp.cos(er)], axis=-1)
    by_col = jnp.concatenate([jnp.sin(ec), jnp.cos(ec)], axis=-1)
    return jnp.concatenate([jnp.repeat(by_row, GRID_W, axis=0), jnp.tile(by_col, (n_rows, 1))], axis=-1)


def _dense_gates(w_a, w_x):
    per = N_BLK // 2
    on_diag = _on_diag()
    halves = []
    for h in range(2):
        cols = []
        for src in (w_a[0], w_x[0], w_a[1], w_x[1]):
            rows = src[h * per:(h + 1) * per].reshape(RH, BLK)
            cols.append(jnp.where(on_diag, jnp.tile(rows, (1, per)), 0.0))
        halves.append(jnp.concatenate(cols, axis=1))
    return jnp.stack(halves).astype(BF16)


def _on_diag():
    r = lax.broadcasted_iota(jnp.int32, (RH, RH), 0) // BLK
    c = lax.broadcasted_iota(jnp.int32, (RH, RH), 1) // BLK
    return r == c


def _gate_block_grads(dwbd):
    per = N_BLK // 2
    on_diag = _on_diag()
    kinds = []
    for q in range(4):
        per_half = []
        for h in range(2):
            dq = jnp.where(on_diag, dwbd[h][:, q * RH:(q + 1) * RH], 0.0)
            per_half.append(dq.reshape(RH, per, BLK).sum(axis=1).reshape(per, BLK, BLK))
        kinds.append(jnp.concatenate(per_half, axis=0))
    return jnp.stack([kinds[0], kinds[2]]), jnp.stack([kinds[1], kinds[3]])


def _gate_bias_dense(b_a, b_x):
    cols = []
    for h in range(2):
        for src in (b_a[0], b_x[0], b_a[1], b_x[1]):
            cols.append(src.reshape(R)[h * RH:(h + 1) * RH])
    return jnp.concatenate(cols).reshape(1, 2 * NQ)


def _gate_bias_grads(dgb):
    v = dgb.reshape(2, 4, RH)
    kinds = [jnp.concatenate([v[0, q], v[1, q]]).reshape(N_BLK, BLK) for q in range(4)]
    return jnp.stack([kinds[0], kinds[2]]), jnp.stack([kinds[1], kinds[3]])


def _mlp_fwd(tag, x_in, g_norm, sh, sc, gate, w_in, w_out):
    n_t = T // ROW_TILE
    (h,) = _tiled(f"{tag}_norm", lambda ids, t, v: ([_norm_mod(t[0], v[0], v[1], v[2])], []), (n_t,),
                  [_rows(x_in)], [g_norm, sc, sh], [_orow(T, D, BF16)])
    tm = 512
    (r,) = _mm(f"{tag}_in", h, w_in, _NN, (T // tm, 4, 1),
               pl.BlockSpec((tm, D), lambda i, j, k: (i, 0)), pl.BlockSpec((None, D, D), lambda i, j, k: (j, 0, 0)),
               [(_sds((T, FF), BF16), pl.BlockSpec((tm, D), lambda i, j, k: (i, j)))], (tm, D),
               epi=lambda acc, ex: [jnp.maximum(acc, 0.0)])
    o, x_out = _mm(f"{tag}_out", r, w_out, _NN, (T // tm, 1, FF // D),
                   pl.BlockSpec((tm, D), lambda i, j, k: (i, k)), pl.BlockSpec((D, D), lambda i, j, k: (k, 0)),
                   [(_sds((T, D), F32), pl.BlockSpec((tm, D), lambda i, j, k: (i, 0)))] * 2, (tm, D),
                   extra=[(x_in, pl.BlockSpec((tm, D), lambda i, j, k: (i, 0))), (gate, _full_spec(gate))],
                   a_pre=lambda a: a * a, epi=lambda acc, ex: [acc, ex[0] + ex[1] * acc])
    return dict(h=h, r=r, o=o, x_in=x_in), x_out


def _behind(dep):
    return [] if dep is None else [dep]


def _gate_bwd(tag, dx, o, gate, dep=None):
    def fn(ids, t, v):
        d_o = t[0] * v[0]
        return [d_o], [_sum0(t[0] * t[1]), _sum0(d_o)]
    return _tiled(f"{tag}_gate_bwd", fn, (T // ROW_TILE,), [_rows(dx), _rows(o)], [gate] + _behind(dep),
                  [_orow(T, D, BF16)], [(1, D), (1, D)])


def _norm_bwd(tag, dx_res, dh, dh_off, x, g_norm, sc, with_dx=True, dep=None):
    n_t = x.shape[0] // ROW_TILE

    def fn(ids, t, v):
        if with_dx:
            dres, dhv, xv = t
        else:
            dhv, xv = t
        dxv, d_sh, d_sc, d_g = _norm_mod_bwd(dhv, xv, v[0], v[1])
        return ([dres + dxv] if with_dx else []), [d_sh, d_sc, d_g]

    ins = ([_rows(dx_res)] if with_dx else []) + [_rows(dh, off=dh_off), _rows(x)]
    outs = [_orow(x.shape[0], D, F32)] if with_dx else []
    return _tiled(f"{tag}_norm_bwd", fn, (n_t,), ins, [g_norm, sc] + _behind(dep), outs, [(1, D)] * 3)


def _mlp_bwd(tag, dx, saved, g_norm, sc, gate, w_in, w_out):
    d_o, d_gate, _ = _gate_bwd(tag, dx, saved["o"], gate)
    tm = 512
    r = saved["r"]
    (da,) = _mm(f"{tag}_dz", d_o, w_out, _NT, (T // tm, FF // D, 1),
                pl.BlockSpec((tm, D), lambda i, j, k: (i, 0)), pl.BlockSpec((D, D), lambda i, j, k: (j, 0)),
                [(_sds((T, FF), BF16), pl.BlockSpec((tm, D), lambda i, j, k: (i, j)))], (tm, D),
                extra=[(r, pl.BlockSpec((tm, D), lambda i, j, k: (i, j)))],
                epi=lambda acc, ex: [acc * (2.0 * ex[0].astype(F32))])
    tk = 512
    (dw_out,) = _mm(f"{tag}_dwout", r, d_o, _TN, (FF // tm, 1, T // tk),
                    pl.BlockSpec((tk, tm), lambda i, j, k: (k, i)), pl.BlockSpec((tk, D), lambda i, j, k: (k, 0)),
                    [(_sds((FF, D), BF16), pl.BlockSpec((tm, D), lambda i, j, k: (i, 0)))], (tm, D),
                    a_pre=lambda a: a * a)
    (dh,) = _mm(f"{tag}_dh", da, w_in, _NT, (T // tm, 1, 4),
                pl.BlockSpec((tm, D), lambda i, j, k: (i, k)), pl.BlockSpec((None, D, D), lambda i, j, k: (k, 0, 0)),
                [(_sds((T, D), F32), pl.BlockSpec((tm, D), lambda i, j, k: (i, 0)))], (tm, D))
    (dw_in,) = _mm(f"{tag}_dwin", saved["h"], da, _TN, (D // tm, 4, T // tk),
                   pl.BlockSpec((tk, tm), lambda i, j, k: (k, i)), pl.BlockSpec((tk, D), lambda i, j, k: (k, j)),
                   [(_sds((4, D, D), BF16), pl.BlockSpec((None, tm, D), lambda i, j, k: (j, i, 0)))], (tm, D))
    dx_in, d_sh, d_sc, d_g = _norm_bwd(tag, dx, dh, 0, saved["x_in"], g_norm, sc)
    return dx_in, dw_in, dw_out, dict(sh=d_sh, sc=d_sc, gate=d_gate, g_norm=d_g)


def _local_step(x, ctx, tgt, mods, cmods, norm_g, final_g, rec, conf, wg, on_grads=None):
    on_grads = on_grads or (lambda group, dws: None)
    n_t = T // ROW_TILE
    row = lambda v: v.reshape(1, -1)
    m0 = [row(mods[0, q]) for q in range(6)]
    m1 = [row(mods[1, q]) for q in range(6)]
    g00, g01, g10, g11 = (row(norm_g[0, 0]), row(norm_g[0, 1]), row(norm_g[1, 0]), row(norm_g[1, 1]))
    csh, csc = row(cmods[0]), row(cmods[1])
    pos = _pos_embed()

    def prep0(ids, t, v):
        cx, xv, pv = t
        is_ctx = ids[0] == 0
        xin = jnp.where(is_ctx, cx, xv + pv)
        sh = jnp.where(is_ctx, v[3], v[1])
        sc = jnp.where(is_ctx, v[4], v[2])
        return [_norm_mod(xin, v[0], sc, sh), xv + pv], []

    hcat, x0 = _tiled(
        "prep0", prep0, (N_SCAN,),
        [(ctx, pl.BlockSpec((ROW_TILE, D), lambda i: (0, 0))), _rows(x, off=-1, clamp_lo=True),
         _rows(pos, off=-1, clamp_lo=True)],
        [g00, m0[0], m0[1], csh, csc],
        [_orow(TA, D, BF16), _orow(T, D, F32, off=-1, clamp_lo=True)])

    tm_a = 768
    w_rec = wg("rec", hcat)
    (a_in,) = _mm("rec_in", hcat, w_rec["rec_w_in"], _NN, (TA // tm_a, 4, 1),
                  pl.BlockSpec((tm_a, D), lambda i, j, k: (i, 0)),
                  pl.BlockSpec((None, D, RH), lambda i, j, k: (j, 0, 0)),
                  [(_sds((TA, 2 * R), F32), pl.BlockSpec((tm_a, RH), lambda i, j, k: (i, j)))], (tm_a, RH))
    cw = 256
    rec_starts = (0, 1)
    u = _dwconv("rec_conv", a_in, R // cw, rec["conv_w"], row(rec["conv_b"]), 1, rec_starts, R, cw)
    wbd = _dense_gates(rec["w_a"], rec["w_x"])
    gbias = _gate_bias_dense(rec["b_a"], rec["b_x"])
    lam = rec["lam"]
    a_f, b_f, a_r, b_r = _tiled("rg_fwd", _rg_fwd_fn, (TA // RG_TILE,), [_rows(u, tm=RG_TILE)], [wbd, gbias, lam],
                                [_orow(TA, R, F32, tm=RG_TILE)] * 4, vec_refs=True)
    y_f, y_r, hin_f, hin_r = _scan_fwd(a_f, b_f, a_r, b_r)

    def rec_mid(ids, t, v):
        gp, yf, yr = t
        g, _ = _gelu(gp)
        return [g * (yf + yr)], []

    (m_rec,) = _tiled("rec_mid", rec_mid, (n_t,),
                      [_rows(a_in, R, off=1), _rows(y_f, off=1), _rows(y_r, off=1)], [], [_orow(T, R, BF16)])
    tm = 512
    o_rec, x1 = _mm("rec_out", m_rec, w_rec["rec_w_out"], _NN, (T // tm, 1, 1),
                    pl.BlockSpec((tm, R), lambda i, j, k: (i, 0)), pl.BlockSpec((R, D), lambda i, j, k: (0, 0)),
                    [(_sds((T, D), F32), pl.BlockSpec((tm, D), lambda i, j, k: (i, 0)))] * 2, (tm, D),
                    extra=[(x0, pl.BlockSpec((tm, D), lambda i, j, k: (i, 0))), (m0[2], _full_spec(m0[2]))],
                    epi=lambda acc, ex: [acc, ex[0] + ex[1] * acc])
    w_m0 = wg("mlp0", x1)
    mlp0, x2 = _mlp_fwd("mlp0", x1, g01, m0[3], m0[4], m0[5], w_m0["w_in"], w_m0["w_out"])

    (h1,) = _tiled("conf_norm", lambda ids, t, v: ([_norm_mod(t[0], v[0], v[1], v[2])], []), (n_t,),
                   [_rows(x2)], [g10, m1[1], m1[0]], [_orow(T, D, BF16)])
    b_pw1 = row(conf["b_pw1"])
    w_cf = wg("conf", x2)
    (pre,) = _mm("conf_pw1", h1, w_cf["conf_w_pw1"], _NN, (T // tm, 4, 1),
                 pl.BlockSpec((tm, D), lambda i, j, k: (i, 0)),
                 pl.BlockSpec((None, D, D // 2), lambda i, j, k: (j, 0, 0)),
                 [(_sds((T, 2 * D), F32), pl.BlockSpec((tm, D // 2), lambda i, j, k: (i, j)))], (tm, D // 2),
                 extra=[(b_pw1, pl.BlockSpec((1, D // 2), lambda i, j, k: (0, j)))],
                 epi=lambda acc, ex: [acc + ex[0]])
    (zg,) = _tiled("conf_glu", lambda ids, t, v: ([t[0] * _sigmoid(t[1])], []), (n_t,),
                   [_rows(pre, D, col=0), _rows(pre, D, col=1)], [], [_orow(T, D, F32)])
    conf_starts = (0,)
    zc = _dwconv("conf_conv", zg, 0, conf["conv_w"], row(conf["conv_b"]), CONF_KW // 2, conf_starts, D, cw)
    ln_g, ln_b = row(conf["ln_g"]), row(conf["ln_b"])

    def ln_silu(ids, t, v):
        nh, _ = _layernorm_parts(t[0])
        ln = nh * v[0] + v[1]
        return [ln * _sigmoid(ln)], []

    (s_conf,) = _tiled("conf_ln", ln_silu, (n_t,), [_rows(zc)], [ln_g, ln_b], [_orow(T, D, BF16)])
    b_pw2 = row(conf["b_pw2"])
    y_conf, x3 = _mm("conf_pw2", s_conf, w_cf["conf_w_pw2"], _NN, (T // tm, 1, 1),
                     pl.BlockSpec((tm, D), lambda i, j, k: (i, 0)), pl.BlockSpec((D, D), lambda i, j, k: (0, 0)),
                     [(_sds((T, D), F32), pl.BlockSpec((tm, D), lambda i, j, k: (i, 0)))] * 2, (tm, D),
                     extra=[(x2, pl.BlockSpec((tm, D), lambda i, j, k: (i, 0))), (m1[2], _full_spec(m1[2])),
                            (b_pw2, _full_spec(b_pw2))],
                     epi=lambda acc, ex: [acc + ex[2], ex[0] + ex[1] * (acc + ex[2])])
    w_m1 = wg("mlp1", x3)
    mlp1, x4 = _mlp_fwd("mlp1", x3, g11, m1[3], m1[4], m1[5], w_m1["w_in"], w_m1["w_out"])

    fg = row(final_g)

    def head(ids, t, v):
        n, r = _rms(t[0])
        err = n * v[0] - t[1]
        d_out = err * (1.0 / D)
        dn = d_out * v[0]
        dxv = r * (dn - n * jnp.mean(dn * n, axis=-1, keepdims=True))
        part = jnp.sum(_sum0(err * err), axis=1, keepdims=True) * (0.5 / D)
        return [dxv], [part, _sum0(d_out * n)]

    dx4, loss, d_fg = _tiled("head", head, (n_t,), [_rows(x4), _rows(tgt)], [fg], [_orow(T, D, F32)],
                             [(1, 1), (1, D)])

    dx3, dw_in1, dw_out1, dm_mlp1 = _mlp_bwd("mlp1", dx4, mlp1, g11, m1[4], m1[5],
                                             w_m1["w_in"], w_m1["w_out"])
    dep = on_grads("mlp1", (dw_in1, dw_out1))
    d_y, d_g1c, d_bpw2 = _gate_bwd("conf", dx3, y_conf, m1[2], dep)
    tk = 512
    (dw_pw2,) = _mm("conf_dwpw2", s_conf, d_y, _TN, (D // tm, 1, T // tk),
                    pl.BlockSpec((tk, tm), lambda i, j, k: (k, i)), pl.BlockSpec((tk, D), lambda i, j, k: (k, 0)),
                    [(_sds((D, D), BF16), pl.BlockSpec((tm, D), lambda i, j, k: (i, 0)))], (tm, D))
    (ds,) = _mm("conf_ds", d_y, w_cf["conf_w_pw2"], _NT, (T // tm, 1, 1),
                pl.BlockSpec((tm, D), lambda i, j, k: (i, 0)), pl.BlockSpec((D, D), lambda i, j, k: (0, 0)),
                [(_sds((T, D), F32), pl.BlockSpec((tm, D), lambda i, j, k: (i, 0)))], (tm, D))

    def ln_silu_bwd(ids, t, v):
        dsv, zcv = t
        nh, rstd = _layernorm_parts(zcv)
        ln = nh * v[0] + v[1]
        sg = _sigmoid(ln)
        d_ln = dsv * (sg * (1.0 + ln * (1.0 - sg)))
        d_nh = d_ln * v[0]
        d_zc = rstd * (d_nh - jnp.mean(d_nh, axis=-1, keepdims=True)
                       - nh * jnp.mean(d_nh * nh, axis=-1, keepdims=True))
        return [d_zc], [_sum0(d_ln * nh), _sum0(d_ln)]

    d_zc, d_lng, d_lnb = _tiled("conf_ln_bwd", ln_silu_bwd, (n_t,), [_rows(ds), _rows(zc)], [ln_g, ln_b],
                                [_orow(T, D, F32)], [(1, D), (1, D)])
    d_zg = _dwconv("conf_conv_dx", d_zc, 0, conf["conv_w"][::-1], jnp.zeros((1, D), F32),
                   CONF_KW - 1 - CONF_KW // 2, conf_starts, D, cw)
    d_cw_conf = _dwconv_wgrad("conf_conv_dw", d_zc, zg, 0, CONF_KW, CONF_KW // 2, conf_starts, D, cw)

    def glu_bwd(ids, t, v):
        dz, pa, pb = t
        sg = _sigmoid(pb)
        d_a = dz * sg
        d_b = dz * pa * sg * (1.0 - sg)
        return [d_a, d_b], [_sum0(d_a), _sum0(d_b)]

    d_pre_a, d_pre_b, d_b1a, d_b1b = _tiled(
        "conf_glu_bwd", glu_bwd, (n_t,), [_rows(d_zg), _rows(pre, D, col=0), _rows(pre, D, col=1)], [],
        [_orow(T, D, BF16), _orow(T, D, BF16)], [(1, D), (1, D)])
    d_pre = jnp.concatenate([d_pre_a, d_pre_b], axis=1)
    (dw_pw1,) = _mm("conf_dwpw1", h1, d_pre, _TN, (D // tm, 4, T // tk),
                    pl.BlockSpec((tk, tm), lambda i, j, k: (k, i)),
                    pl.BlockSpec((tk, D // 2), lambda i, j, k: (k, j)),
                    [(_sds((4, D, D // 2), BF16), pl.BlockSpec((None, tm, D // 2), lambda i, j, k: (j, i, 0)))],
                    (tm, D // 2))
    dep = on_grads("conf", (dw_pw1, dw_pw2))
    (dh1,) = _mm("conf_dh", d_pre, w_cf["conf_w_pw1"], _NT, (T // tm, 1, 4),
                 pl.BlockSpec((tm, D // 2), lambda i, j, k: (i, k)),
                 pl.BlockSpec((None, D, D // 2), lambda i, j, k: (k, 0, 0)),
                 [(_sds((T, D), F32), pl.BlockSpec((tm, D), lambda i, j, k: (i, 0)))], (tm, D))
    dx2, d_sh1c, d_sc1c, d_g10 = _norm_bwd("conf", dx3, dh1, 0, x2, g10, m1[1], dep=dep)

    dx1, dw_in0, dw_out0, dm_mlp0 = _mlp_bwd("mlp0", dx2, mlp0, g01, m0[4], m0[5],
                                             w_m0["w_in"], w_m0["w_out"])
    dep = on_grads("mlp0", (dw_in0, dw_out0))
    d_orec, d_g1r, _ = _gate_bwd("rec", dx1, o_rec, m0[2], dep)
    (dw_rout,) = _mm("rec_dwout", m_rec, d_orec, _TN, (R // RH, 1, T // tk),
                     pl.BlockSpec((tk, RH), lambda i, j, k: (k, i)), pl.BlockSpec((tk, D), lambda i, j, k: (k, 0)),
                     [(_sds((R, D), BF16), pl.BlockSpec((RH, D), lambda i, j, k: (i, 0)))], (RH, D))
    (dm_rec,) = _mm("rec_dm", d_orec, w_rec["rec_w_out"], _NT, (T // tm, 1, 1),
                    pl.BlockSpec((tm, D), lambda i, j, k: (i, 0)), pl.BlockSpec((R, D), lambda i, j, k: (0, 0)),
                    [(_sds((T, R), F32), pl.BlockSpec((tm, R), lambda i, j, k: (i, 0)))], (tm, R))

    def rec_mid_bwd(ids, t, v):
        dmv, gp, yf, yr = t
        g, th = _gelu(gp)
        lat = ids[0] > 0
        d_gp = jnp.where(lat, dmv * (yf + yr) * _gelu_grad(gp, th), 0.0)
        dy = jnp.where(lat, dmv * g, 0.0)
        return [d_gp, dy], []

    d_gp, dy = _tiled("rec_mid_bwd", rec_mid_bwd, (N_SCAN,),
                      [_rows(dm_rec, off=-1, clamp_lo=True), _rows(a_in, R), _rows(y_f), _rows(y_r)], [],
                      [_orow(TA, R, BF16), _orow(TA, R, F32)])
    da_f, db_f, da_r, db_r = _scan_bwd(dy, a_f, y_f, hin_f, a_r, y_r, hin_r)
    d_gpre, d_u, d_gbias, d_lam = _tiled(
        "rg_bwd", _rg_bwd_fn, (TA // RG_TILE,), [_rows(a, tm=RG_TILE) for a in (u, da_f, db_f, da_r, db_r)],
        [wbd, gbias, lam], [_orow(TA, 2 * NQ, BF16, tm=RG_TILE), _orow(TA, R, F32, tm=RG_TILE)],
        [(1, 2 * NQ), (1, 2 * R)], vec_refs=True)
    tk_a = 768
    (d_wbd,) = _mm("rg_dw", u, d_gpre, _TN, (2, 2, TA // tk_a),
                   pl.BlockSpec((tk_a, RH), lambda i, j, k: (k, i)),
                   pl.BlockSpec((tk_a, NQ // 2), lambda i, j, k: (k, 2 * i + j)),
                   [(_sds((2, RH, NQ), F32), pl.BlockSpec((None, RH, NQ // 2), lambda i, j, k: (i, 0, j)))],
                   (RH, NQ // 2))
    d_p = _dwconv("rec_conv_dx", d_u, 0, rec["conv_w"][::-1], jnp.zeros((1, R), F32), REC_KW - 1 - 1,
                  rec_starts, R, cw)
    d_cw_rec = _dwconv_wgrad("rec_conv_dw", d_u, a_in, R // cw, REC_KW, 1, rec_starts, R, cw)
    d_a = jnp.concatenate([d_gp, d_p.astype(BF16)], axis=1)
    (dw_rin,) = _mm("rec_dwin", hcat, d_a, _TN, (D // tm, 4, TA // tk_a),
                    pl.BlockSpec((tk_a, tm), lambda i, j, k: (k, i)), pl.BlockSpec((tk_a, RH), lambda i, j, k: (k, j)),
                    [(_sds((4, D, RH), BF16), pl.BlockSpec((None, tm, RH), lambda i, j, k: (j, i, 0)))], (tm, RH))
    dep = on_grads("rec", (dw_rin, dw_rout))
    (dhcat,) = _mm("rec_dh", d_a, w_rec["rec_w_in"], _NT, (TA // tm_a, 1, 4),
                   pl.BlockSpec((tm_a, RH), lambda i, j, k: (i, k)),
                   pl.BlockSpec((None, D, RH), lambda i, j, k: (k, 0, 0)),
                   [(_sds((TA, D), F32), pl.BlockSpec((tm_a, D), lambda i, j, k: (i, 0)))], (tm_a, D))
    dx0, d_sh1r, d_sc1r, d_g00 = _norm_bwd("rec", dx1, dhcat, 1, x0, g00, m0[1], dep=dep)
    d_csh, d_csc, d_g00c = _norm_bwd("ctx", None, dhcat, 0, ctx, g00, csc, with_dx=False)

    big = dict(rec_w_in=dw_rin, rec_w_out=dw_rout, conf_w_pw1=dw_pw1, conf_w_pw2=dw_pw2,
               mlp_w_in=(dw_in0, dw_in1), mlp_w_out=(dw_out0, dw_out1))
    d_wa, d_wx = _gate_block_grads(d_wbd)
    d_ba, d_bx = _gate_bias_grads(d_gbias)
    d_mod = jnp.concatenate([
        d_sh1r, d_sc1r, d_g1r, dm_mlp0["sh"], dm_mlp0["sc"], dm_mlp0["gate"],
        d_sh1c, d_sc1c, d_g1c, dm_mlp1["sh"], dm_mlp1["sc"], dm_mlp1["gate"]], axis=1).reshape(2, 6 * D)
    small = dict(
        d_mod=d_mod, d_cmod=jnp.concatenate([d_csh, d_csc], axis=1),
        norm_g=jnp.concatenate([d_g00 + d_g00c, dm_mlp0["g_norm"], d_g10, dm_mlp1["g_norm"]], axis=1),
        rec_conv_w=d_cw_rec[:REC_KW], rec_conv_b=d_cw_rec[REC_KW], rec_lambda=d_lam.reshape(2, R),
        rec_w_a=d_wa, rec_b_a=d_ba, rec_w_x=d_wx, rec_b_x=d_bx,
        conf_b_pw1=jnp.concatenate([d_b1a, d_b1b], axis=1), conf_conv_w=d_cw_conf[:CONF_KW],
        conf_conv_b=d_cw_conf[CONF_KW], conf_ln_g=d_lng, conf_ln_b=d_lnb, conf_b_pw2=d_bpw2, final_g=d_fg)
    return loss.reshape(()), dx0, big, small


_BIG = ("rec_w_in", "rec_w_out", "conf_w_pw1", "conf_w_pw2", "mlp_w_in", "mlp_w_out")


def _halves(w):
    return w.reshape(2, w.shape[0] // 2, w.shape[1])


def _ada_fwd(c16, w_ada, b_shard):
    ns = w_ada.shape[2]
    tn = 512

    def kern(c_ref, w_ref, b_ref, o_ref):
        cv = c_ref[...]
        s = (cv * _sigmoid(cv)).astype(BF16)
        o_ref[...] = jnp.dot(s, w_ref[...].astype(BF16), preferred_element_type=F32) + b_ref[...]

    return _pcall(
        kern, name="ada_fwd", grid=(2, ns // tn),
        in_specs=[pl.BlockSpec((16, D), lambda l, j: (0, 0)), pl.BlockSpec((None, D, tn), lambda l, j: (l, 0, j)),
                  pl.BlockSpec((None, 1, tn), lambda l, j: (l, 0, j))],
        out_specs=pl.BlockSpec((None, 16, tn), lambda l, j: (l, 0, j)),
        out_shape=_sds((2, 16, ns), F32), compiler_params=_cparams(),
    )(c16, w_ada, b_shard)


def _ada_bwd(c16, dm16, w_ada):
    ns = w_ada.shape[2]
    tn = 512

    def kern(c_ref, dm_ref, w_ref, gw_ref, ds_ref):
        cv = c_ref[...]
        s = (cv * _sigmoid(cv)).astype(BF16)
        dm = dm_ref[...].astype(BF16)
        gw_ref[...] = lax.dot_general(s, dm, _TN, preferred_element_type=F32)

        @pl.when(jnp.logical_and(pl.program_id(0) == 0, pl.program_id(1) == 0))
        def _():
            ds_ref[...] = jnp.zeros_like(ds_ref)

        ds_ref[...] += lax.dot_general(dm, w_ref[...].astype(BF16), _NT, preferred_element_type=F32)

    return _pcall(
        kern, name="ada_bwd", grid=(2, ns // tn),
        in_specs=[pl.BlockSpec((16, D), lambda l, j: (0, 0)), pl.BlockSpec((None, 16, tn), lambda l, j: (l, 0, j)),
                  pl.BlockSpec((None, D, tn), lambda l, j: (l, 0, j))],
        out_specs=[pl.BlockSpec((None, D, tn), lambda l, j: (l, 0, j)), pl.BlockSpec((16, D), lambda l, j: (0, 0))],
        out_shape=[_sds((2, D, ns), F32), _sds((16, D), F32)], compiler_params=_cparams(),
    )(c16, dm16, w_ada)


def _cctx_grad(ds8, c_ctx):
    def kern(d_ref, c_ref, o_ref):
        tot = d_ref[0, 8:9, :] + d_ref[2, 8:9, :] + d_ref[4, 8:9, :] + d_ref[6, 8:9, :]
        cv = c_ref[...]
        sg = _sigmoid(cv)
        o_ref[...] = tot * (sg * (1.0 + cv * (1.0 - sg)))

    return _pcall(kern, name="cctx_grad", out_shape=_sds((1, D), F32))(ds8, c_ctx.reshape(1, D))


def kernel(x, c, ctx, c_ctx, w_ada, b_ada, norm_g, rec_w_in, rec_conv_w, rec_conv_b, rec_lambda, rec_w_a, rec_b_a, rec_w_x, rec_b_x, rec_w_out, conf_w_pw1, conf_b_pw1, conf_conv_w, conf_conv_b, conf_ln_g, conf_ln_b, conf_w_pw2, conf_b_pw2, mlp_w_in, mlp_w_out, final_g, loss_target, m_c_ctx, m_w_ada, m_b_ada, m_norm_g, m_rec_w_in, m_rec_conv_w, m_rec_conv_b, m_rec_lambda, m_rec_w_a, m_rec_b_a, m_rec_w_x, m_rec_b_x, m_rec_w_out, m_conf_w_pw1, m_conf_b_pw1, m_conf_conv_w, m_conf_conv_b, m_conf_ln_g, m_conf_ln_b, m_conf_w_pw2, m_conf_b_pw2, m_mlp_w_in, m_mlp_w_out, m_final_g, v_c_ctx, v_w_ada, v_b_ada, v_norm_g, v_rec_w_in, v_rec_conv_w, v_rec_conv_b, v_rec_lambda, v_rec_w_a, v_rec_b_a, v_rec_w_x, v_rec_b_x, v_rec_w_out, v_conf_w_pw1, v_conf_b_pw1, v_conf_conv_w, v_conf_conv_b, v_conf_ln_g, v_conf_ln_b, v_conf_w_pw2, v_conf_b_pw2, v_mlp_w_in, v_mlp_w_out, v_final_g):
    names = ["c_ctx", "w_ada", "b_ada", "norm_g", "rec_w_in", "rec_conv_w", "rec_conv_b", "rec_lambda", "rec_w_a",
             "rec_b_a", "rec_w_x", "rec_b_x", "rec_w_out", "conf_w_pw1", "conf_b_pw1", "conf_conv_w", "conf_conv_b",
             "conf_ln_g", "conf_ln_b", "conf_w_pw2", "conf_b_pw2", "mlp_w_in", "mlp_w_out", "final_g"]
    w = dict(zip(names, [c_ctx, w_ada, b_ada, norm_g, rec_w_in, rec_conv_w, rec_conv_b, rec_lambda, rec_w_a,
                         rec_b_a, rec_w_x, rec_b_x, rec_w_out, conf_w_pw1, conf_b_pw1, conf_conv_w, conf_conv_b,
                         conf_ln_g, conf_ln_b, conf_w_pw2, conf_b_pw2, mlp_w_in, mlp_w_out, final_g]))
    m = dict(zip(names, [m_c_ctx, m_w_ada, m_b_ada, m_norm_g, m_rec_w_in, m_rec_conv_w, m_rec_conv_b, m_rec_lambda,
                         m_rec_w_a, m_rec_b_a, m_rec_w_x, m_rec_b_x, m_rec_w_out, m_conf_w_pw1, m_conf_b_pw1,
                         m_conf_conv_w, m_conf_conv_b, m_conf_ln_g, m_conf_ln_b, m_conf_w_pw2, m_conf_b_pw2,
                         m_mlp_w_in, m_mlp_w_out, m_final_g]))
    v = dict(zip(names, [v_c_ctx, v_w_ada, v_b_ada, v_norm_g, v_rec_w_in, v_rec_conv_w, v_rec_conv_b, v_rec_lambda,
                         v_rec_w_a, v_rec_b_a, v_rec_w_x, v_rec_b_x, v_rec_w_out, v_conf_w_pw1, v_conf_b_pw1,
                         v_conf_conv_w, v_conf_conv_b, v_conf_ln_g, v_conf_ln_b, v_conf_w_pw2, v_conf_b_pw2,
                         v_mlp_w_in, v_mlp_w_out, v_final_g]))
    mx, my, mc = _me()
    chip = 2 * mx + my
    me = 4 * mx + 2 * my + mc

    place = jnp.stack([chip, mc]).astype(jnp.int32)
    shards = [_halves(rec_w_in[0]), _halves(rec_w_out[0]), _halves(conf_w_pw1[0]), _halves(conf_w_pw2[0]),
              _halves(mlp_w_in[0]), _halves(mlp_w_in[1]), _halves(mlp_w_out[0]), _halves(mlp_w_out[1])]
    use_order = dict(rec=(0, 1), mlp0=(4, 6), conf=(2, 3), mlp1=(5, 7))
    flying, gsems = _gather_start(_place_big(shards, place), tuple(use_order.values()))

    def wg(group, after):
        gi = list(use_order).index(group)
        bufs = _gather_wait(f"gather_wait_{group}", [flying[t] for t in use_order[group]], gsems[2 * gi],
                            gsems[2 * gi + 1], after)
        a, b = _swap_halves(f"swap_{group}", bufs)
        if group == "rec":
            return dict(rec_w_in=a.reshape(4, D, RH), rec_w_out=b.reshape(R, D))
        if group == "conf":
            return dict(conf_w_pw1=a.reshape(4, D, D // 2), conf_w_pw2=b.reshape(D, D))
        return dict(w_in=a.reshape(4, D, D), w_out=b.reshape(FF, D))

    sharded_small = ["norm_g", "rec_conv_w", "rec_lambda", "conf_b_pw1", "conf_conv_w", "conf_conv_b", "conf_ln_g",
                     "conf_ln_b", "conf_b_pw2"]
    packed, offs = _pack([c] + [w[k] for k in sharded_small], 8)
    got = _allgather8("gather_small", packed)
    per_dev = [_unpack(got[d], offs) for d in range(8)]
    c_rows = jnp.concatenate([per_dev[d][0].reshape(1, D) for d in range(8)], axis=0)
    full = {k: jnp.concatenate([per_dev[2 * j][1 + i] for j in range(4)], axis=-1)
            for i, k in enumerate(sharded_small)}
    c16 = jnp.concatenate([c_rows, c_ctx.reshape(1, D), jnp.zeros((7, D), F32)], axis=0)

    ns = w_ada.shape[2]
    b_shard = lax.dynamic_slice_in_dim(b_ada, chip * ns, ns, axis=1).reshape(2, 1, ns)
    prod = _ada_fwd(c16, w_ada, b_shard)
    prod8 = _allgather8("gather_mod", prod.reshape(32, ns)).reshape(8, 2, 16, ns)
    mod_all = jnp.concatenate([prod8[2 * j] for j in range(4)], axis=-1)
    mods = lax.dynamic_index_in_dim(mod_all, me, axis=1, keepdims=False).reshape(2, 6, D)
    cmods = mod_all[0, 8].reshape(6, D)[:2]

    rec = dict(conv_w=full["rec_conv_w"][0], conv_b=rec_conv_b[0], lam=full["rec_lambda"][0],
               w_a=rec_w_a[0], b_a=rec_b_a[0], w_x=rec_w_x[0], b_x=rec_b_x[0])
    conf = dict(b_pw1=full["conf_b_pw1"][0], conv_w=full["conf_conv_w"][0], conv_b=full["conf_conv_b"][0],
                ln_g=full["conf_ln_g"][0], ln_b=full["conf_ln_b"][0], b_pw2=full["conf_b_pw2"][0])
    sent = {}

    def on_grads(group, dws):
        parts = [dw.reshape(4, 2, shards[t].shape[1], shards[t].shape[2]) for dw, t in zip(dws, use_order[group])]
        sent[group], token = _reduce_begin(group, parts, place)
        return token

    loss_local, grad_x, _, small = _local_step(x[0], ctx[0], loss_target[0], mods, cmods, full["norm_g"], final_g,
                                               rec, conf, wg, on_grads)
    loss = lax.psum(loss_local, ("x", "y", "c"))

    small_names = ["d_mod", "d_cmod", "norm_g", "rec_conv_w", "rec_conv_b", "rec_lambda", "rec_w_a", "rec_b_a",
                   "rec_w_x", "rec_b_x", "conf_b_pw1", "conf_conv_w", "conf_conv_b", "conf_ln_g", "conf_ln_b",
                   "conf_b_pw2", "final_g"]
    mod_slots = lax.dynamic_update_slice(jnp.zeros((8, 2 * 6 * D), F32), small["d_mod"].reshape(1, -1), (me, 0))
    spacked, soffs = _pack([small[k] for k in small_names] + [mod_slots])
    stotal = _allreduce_small(spacked, place)
    unpacked = _unpack(stotal, soffs)
    ssum = dict(zip(small_names, unpacked[:-1]))
    dmod_rows = unpacked[-1].reshape(8, 2, 6 * D).transpose(1, 0, 2)

    fulls = {}
    for group in ("mlp1", "conf", "mlp0", "rec"):
        for t, f in zip(use_order[group], _reduce_end(group, sent[group], place, stotal)):
            fulls[t] = f
    whole = _share_halves("share_grads", [fulls[t] for t in range(8)])
    g_big = dict(rec_w_in=whole[0].reshape(rec_w_in.shape), rec_w_out=whole[1].reshape(rec_w_out.shape),
                 conf_w_pw1=whole[2].reshape(conf_w_pw1.shape), conf_w_pw2=whole[3].reshape(conf_w_pw2.shape),
                 mlp_w_in=jnp.stack([whole[4].reshape(D, D), whole[5].reshape(D, D)]),
                 mlp_w_out=jnp.stack([whole[6].reshape(D, D), whole[7].reshape(D, D)]))

    d_cmod_full =jnp.concatenate([ssum["d_cmod"].reshape(1, 2 * D), jnp.zeros((1, 4 * D), F32)], axis=1)
    dm16 = jnp.concatenate([dmod_rows, jnp.stack([d_cmod_full, jnp.zeros((1, 6 * D), F32)]),
                            jnp.zeros((2, 7, 6 * D), F32)], axis=1)
    dm16_shard = lax.dynamic_slice_in_dim(dm16, chip * ns, ns, axis=2)
    g_w_ada, ds_part = _ada_bwd(c16, dm16_shard, w_ada)
    ds8 = _allgather8("gather_dsilu", ds_part)
    g_c_ctx = _cctx_grad(ds8, c_ctx).reshape(D)
    g_b_ada = ssum["d_mod"] + jnp.stack([d_cmod_full[0], jnp.zeros((6 * D,), F32)])

    def shard_of(a, axis):
        n = a.shape[axis] // 4
        return lax.dynamic_slice_in_dim(a, chip * n, n, axis=axis)

    grads = dict(
        c_ctx=g_c_ctx, w_ada=g_w_ada, b_ada=g_b_ada,
        norm_g=shard_of(ssum["norm_g"].reshape(2, 2, D), 2),
        rec_w_in=g_big["rec_w_in"], rec_conv_w=shard_of(ssum["rec_conv_w"].reshape(1, REC_KW, R), 2),
        rec_conv_b=ssum["rec_conv_b"].reshape(1, R), rec_lambda=shard_of(ssum["rec_lambda"].reshape(1, 2, R), 2),
        rec_w_a=ssum["rec_w_a"].reshape(rec_w_a.shape), rec_b_a=ssum["rec_b_a"].reshape(rec_b_a.shape),
        rec_w_x=ssum["rec_w_x"].reshape(rec_w_x.shape), rec_b_x=ssum["rec_b_x"].reshape(rec_b_x.shape),
        rec_w_out=g_big["rec_w_out"], conf_w_pw1=g_big["conf_w_pw1"],
        conf_b_pw1=shard_of(ssum["conf_b_pw1"].reshape(1, 2 * D), 1),
        conf_conv_w=shard_of(ssum["conf_conv_w"].reshape(1, CONF_KW, D), 2),
        conf_conv_b=shard_of(ssum["conf_conv_b"].reshape(1, D), 1),
        conf_ln_g=shard_of(ssum["conf_ln_g"].reshape(1, D), 1), conf_ln_b=shard_of(ssum["conf_ln_b"].reshape(1, D), 1),
        conf_w_pw2=g_big["conf_w_pw2"], conf_b_pw2=shard_of(ssum["conf_b_pw2"].reshape(1, D), 1),
        mlp_w_in=g_big["mlp_w_in"], mlp_w_out=g_big["mlp_w_out"], final_g=ssum["final_g"].reshape(D))

    delta, new_m, new_v = {}, {}, {}
    big_names = ("w_ada",) + _BIG
    for k in big_names:
        cols = w[k].shape[-1]
        d_, m_, v_ = _adamw(f"adamw_{k}", w[k].reshape(-1, cols), grads[k].reshape(-1, cols),
                            m[k].reshape(-1, cols), v[k].reshape(-1, cols))
        delta[k], new_m[k], new_v[k] = (a.reshape(w[k].shape) for a in (d_, m_, v_))
    rest = [k for k in names if k not in big_names]
    pw, poffs = _pack([w[k] for k in rest])
    pg, _ = _pack([grads[k] for k in rest])
    pm, _ = _pack([m[k] for k in rest])
    pv, _ = _pack([v[k] for k in rest])
    d_, m_, v_ = _adamw("adamw_small", pw, pg, pm, pv)
    for k, dd, mm, vv in zip(rest, _unpack(d_, poffs), _unpack(m_, poffs), _unpack(v_, poffs)):
        delta[k], new_m[k], new_v[k] = dd, mm, vv

    return (loss, grad_x[None], *[grads[k] for k in names], *[delta[k] for k in names],
            *[new_m[k] for k in names], *[new_v[k] for k in names])
```

```python
import functools
import math

import jax
import jax.numpy as jnp
from jax import lax
from jax.experimental import pallas as pl
from jax.experimental.pallas import tpu as pltpu

F32 = jnp.float32
BF16 = jnp.bfloat16

D = 1024
T = 2048
TC = 256
TA = T + TC
R = 1280
RH = R // 2
NQ = 4 * RH
FF = 4096
N_BLK = 16
BLK = R // N_BLK
GRID_W = 64
EPS = 1e-6
RG_C = 8.0
CONF_KW = 31
REC_KW = 4
LANE = 128
ROW_TILE = 256
HALO = 16
RG_TILE = 128
PACK_ROWS = 512
MM_TILE = 1024
REC_TILE = TA // 2
CW_REC = 640
CW_CONF = 512
V7X_VMEM_BYTES = 64 * 1024 * 1024
VMEM_LIMIT = V7X_VMEM_BYTES - 8 * 1024 * 1024

ADAM_LR = 0.001
ADAM_B1 = 0.9
ADAM_B2 = 0.999
ADAM_EPS = 1e-08
ADAM_WD = 0.01
ADAM_STEP = 10

MESH = pl.DeviceIdType.MESH
ANY = pl.BlockSpec(memory_space=pl.ANY)


def _sds(shape, dtype):
    return jax.ShapeDtypeStruct(tuple(shape), dtype)


def _pcall(body, **kw):
    return pl.pallas_call(body, **kw)


def _cparams():
    return pltpu.CompilerParams(vmem_limit_bytes=VMEM_LIMIT)


def _full_spec(arr):
    nd = arr.ndim
    return pl.BlockSpec(arr.shape, lambda *ids, _n=nd: (0,) * _n)


def _sum0(v):
    return jnp.sum(v, axis=0, keepdims=True)


def _tiled(name, fn, grid, ins, vecs, outs, vec_outs=(), vec_refs=False):
    n_in, n_vec, n_out = len(ins), len(vecs), len(outs)
    n_grid = len(grid)

    def kern(*refs):
        ids = [pl.program_id(a) for a in range(n_grid)]
        tin = [r[...] for r in refs[:n_in]]
        vin = list(refs[n_in:n_in + n_vec]) if vec_refs else [r[...] for r in refs[n_in:n_in + n_vec]]
        o_refs = refs[n_in + n_vec:n_in + n_vec + n_out]
        a_refs = refs[n_in + n_vec + n_out:]
        tout, incs = fn(ids, tin, vin)
        for r, v in zip(o_refs, tout):
            r[...] = v.astype(r.dtype)
        if a_refs:
            first = functools.reduce(jnp.logical_and, [i == 0 for i in ids])

            @pl.when(first)
            def _():
                for r in a_refs:
                    r[...] = jnp.zeros_like(r)

            for r, v in zip(a_refs, incs):
                r[...] += v

    out_shape = [o for o, _ in outs] + [_sds(s, F32) for s in vec_outs]
    out_specs = [s for _, s in outs] + [
        pl.BlockSpec(tuple(s), lambda *ids, _n=len(s): (0,) * _n) for s in vec_outs]
    res = _pcall(
        kern, name=name, grid=tuple(grid),
        in_specs=[s for _, s in ins] + [_full_spec(v) for v in vecs],
        out_specs=out_specs, out_shape=out_shape, compiler_params=_cparams(),
    )(*[a for a, _ in ins], *vecs)
    return list(res)


def _rows(arr, ncols=None, tm=ROW_TILE, off=0, col=0, clamp_lo=False):
    ncols = arr.shape[1] if ncols is None else ncols
    if clamp_lo:
        return arr, pl.BlockSpec((tm, ncols), lambda i: (jnp.maximum(i + off, 0), col))
    return arr, pl.BlockSpec((tm, ncols), lambda i: (i + off, col))


def _orow(nrows, ncols, dtype, tm=ROW_TILE, off=0, clamp_lo=False):
    if clamp_lo:
        return _sds((nrows, ncols), dtype), pl.BlockSpec((tm, ncols), lambda i: (jnp.maximum(i + off, 0), 0))
    return _sds((nrows, ncols), dtype), pl.BlockSpec((tm, ncols), lambda i: (i + off, 0))


_NN = (((1,), (0,)), ((), ()))
_TN = (((0,), (0,)), ((), ()))
_NT = (((1,), (1,)), ((), ()))


def _mm(name, a, b, dims, grid, a_spec, b_spec, out, acc_shape, extra=(), a_pre=None, epi=None):
    n_k = grid[2]
    n_ex = len(extra)

    def kern(a_ref, b_ref, *rest):
        ex = rest[:n_ex]
        o_refs = rest[n_ex:n_ex + len(out)]
        k = pl.program_id(2)
        av = a_ref[...]
        if a_pre is not None:
            av = a_pre(av)
        part = lax.dot_general(av.astype(BF16), b_ref[...].astype(BF16), dims, preferred_element_type=F32)

        def finish(total):
            vals = [total] if epi is None else epi(total, [e[...] for e in ex])
            for r, v in zip(o_refs, vals):
                r[...] = v.astype(r.dtype)

        if n_k == 1:
            finish(part)
        else:
            acc = rest[-1]

            @pl.when(k == 0)
            def _():
                acc[...] = part

            @pl.when(jnp.logical_and(k > 0, k < n_k - 1))
            def _():
                acc[...] += part

            @pl.when(k == n_k - 1)
            def _():
                finish(acc[...] + part)

    res = _pcall(
        kern, name=name, grid=tuple(grid),
        in_specs=[a_spec, b_spec] + [s for _, s in extra],
        out_specs=[s for _, s in out], out_shape=[o for o, _ in out],
        scratch_shapes=[] if n_k == 1 else [pltpu.VMEM(tuple(acc_shape), F32)], compiler_params=_cparams(),
    )(a, b, *[e for e, _ in extra])
    return list(res)


def _rms(x):
    r = lax.rsqrt(jnp.mean(x * x, axis=-1, keepdims=True) + EPS)
    return x * r, r


def _norm_mod(x, g, sc, sh):
    n, _ = _rms(x)
    return (n * g) * (1.0 + sc) + sh


def _norm_mod_bwd(dh, x, g, sc):
    n, r = _rms(x)
    d_sh = _sum0(dh)
    d_sc = _sum0(dh * (n * g))
    d_g = _sum0(dh * (1.0 + sc) * n)
    dn = dh * (g * (1.0 + sc))
    dx = r * (dn - n * jnp.mean(dn * n, axis=-1, keepdims=True))
    return dx, d_sh, d_sc, d_g


_GELU_K = math.sqrt(2.0 / math.pi)


def _gelu(x):
    t = jnp.tanh(_GELU_K * (x + 0.044715 * x * x * x))
    return 0.5 * x * (1.0 + t), t


def _gelu_grad(x, t):
    return 0.5 * (1.0 + t) + 0.5 * x * (1.0 - t * t) * (_GELU_K * (1.0 + 3.0 * 0.044715 * x * x))


def _sigmoid(x):
    return 1.0 / (1.0 + jnp.exp(-x))


def _expm1(x):
    p = 1.0 + x * (1.0 / 9.0)
    for n in (8.0, 7.0, 6.0, 5.0, 4.0, 3.0, 2.0):
        p = 1.0 + (x * (1.0 / n)) * p
    return jnp.where(jnp.abs(x) < 0.5, x * p, jnp.exp(x) - 1.0)


def _softplus_neg(lam):
    return jnp.log1p(jnp.exp(-jnp.abs(lam))) + jnp.maximum(-lam, 0.0)


def _layernorm_parts(x):
    mu = jnp.mean(x, axis=-1, keepdims=True)
    xc = x - mu
    rstd = lax.rsqrt(jnp.mean(xc * xc, axis=-1, keepdims=True) + EPS)
    return xc * rstd, rstd


def _rg_gates(u, wbd, gbias, lam):
    sp = _softplus_neg(lam)
    parts = {}
    for h in range(2):
        uh = u[:, h * RH:(h + 1) * RH]
        g = jnp.dot(uh.astype(BF16), wbd[h], preferred_element_type=F32) + gbias[:, h * NQ:(h + 1) * NQ]
        for d in range(2):
            r = _sigmoid(g[:, (2 * d) * RH:(2 * d + 1) * RH])
            i = _sigmoid(g[:, (2 * d + 1) * RH:(2 * d + 2) * RH])
            sph = sp[d:d + 1, h * RH:(h + 1) * RH]
            la = (-RG_C) * r * sph
            e2 = _expm1(2.0 * la)
            parts[(d, h)] = dict(r=r, i=i, la=la, a=jnp.exp(la), e2=e2, mult=jnp.sqrt(-e2), uh=uh, sp=sph)
    return parts


def _rg_fwd_fn(ids, tin, vin):
    (u,) = tin
    wbd = vin[0]
    parts = _rg_gates(u, wbd, vin[1][...], vin[2][...])
    outs = []
    for d in range(2):
        a = jnp.concatenate([parts[(d, h)]["a"] for h in range(2)], axis=1)
        b = jnp.concatenate([parts[(d, h)]["mult"] * parts[(d, h)]["i"] * parts[(d, h)]["uh"]
                             for h in range(2)], axis=1)
        outs += [a, b]
    return outs, []


def _rg_bwd_fn(ids, tin, vin):
    u, da_f, db_f, da_r, db_r = tin
    wbd, lam = vin[0], vin[2][...]
    parts = _rg_gates(u, wbd, vin[1][...], lam)
    dab = ((da_f, db_f), (da_r, db_r))
    dsig_lam = -_sigmoid(-lam)
    du_halves, dpre_halves, dlam = [], [], [[None, None], [None, None]]
    for h in range(2):
        du = jnp.zeros_like(parts[(0, h)]["uh"])
        dpre = []
        for d in range(2):
            p = parts[(d, h)]
            da = dab[d][0][:, h * RH:(h + 1) * RH]
            db = dab[d][1][:, h * RH:(h + 1) * RH]
            d_mult = db * p["i"] * p["uh"]
            d_i = db * p["mult"] * p["uh"]
            du = du + db * p["mult"] * p["i"]
            d_la = da * p["a"] - d_mult * (p["e2"] + 1.0) / p["mult"]
            d_r = d_la * ((-RG_C) * p["sp"])
            dlam[d][h] = _sum0(d_la * ((-RG_C) * p["r"])) * dsig_lam[d:d + 1, h * RH:(h + 1) * RH]
            dpre += [d_r * p["r"] * (1.0 - p["r"]), d_i * p["i"] * (1.0 - p["i"])]
        dpre = jnp.concatenate(dpre, axis=1)
        du = du + lax.dot_general(dpre.astype(BF16), wbd[h], _NT, preferred_element_type=F32)
        du_halves.append(du)
        dpre_halves.append(dpre)
    dpre_all = jnp.concatenate(dpre_halves, axis=1)
    dlam_row = jnp.concatenate([dlam[0][0], dlam[0][1], dlam[1][0], dlam[1][1]], axis=1)
    return [dpre_all, jnp.concatenate(du_halves, axis=1)], [_sum0(dpre_all), dlam_row]


def _tile_flags(i, n_tiles, seq_starts):
    starts_here = functools.reduce(jnp.logical_or, [i == s for s in seq_starts])
    ends_here = functools.reduce(jnp.logical_or, [i + 1 == s for s in seq_starts] + [i + 1 == n_tiles])
    return jnp.logical_not(starts_here), jnp.logical_not(ends_here)


def _halo_specs(col0, cw):
    hb = ROW_TILE // HALO
    prev = pl.BlockSpec((HALO, cw), lambda i, c: (jnp.maximum(i * hb - 1, 0), col0 + c))
    cur = pl.BlockSpec((ROW_TILE, cw), lambda i, c: (i, col0 + c))
    return prev, cur, hb


def _window(prev_ref, cur_ref, next_ref, has_prev, has_next):
    prev = jnp.where(has_prev, prev_ref[...], 0.0)
    nxt = jnp.where(has_next, next_ref[...], 0.0)
    return jnp.concatenate([prev, cur_ref[...], nxt], axis=0)


def _dwconv(name, x, col0, w, bias, pad_left, seq_starts, n_ch, cw=256):
    n_rows = x.shape[0]
    n_tiles = n_rows // ROW_TILE
    n_taps = w.shape[0]
    prev_spec, cur_spec, hb = _halo_specs(col0, cw)
    last_hb = n_rows // HALO - 1
    next_spec = pl.BlockSpec((HALO, cw), lambda i, c: (jnp.minimum((i + 1) * hb, last_hb), col0 + c))

    def kern(prev_ref, cur_ref, next_ref, w_ref, b_ref, o_ref):
        has_prev, has_next = _tile_flags(pl.program_id(0), n_tiles, seq_starts)
        win = _window(prev_ref, cur_ref, next_ref, has_prev, has_next)
        wv = w_ref[...]
        acc = jnp.zeros((ROW_TILE, cw), F32) + b_ref[...]
        for k in range(n_taps):
            off = HALO + k - pad_left
            acc = acc + wv[k:k + 1, :] * win[off:off + ROW_TILE, :]
        o_ref[...] = acc

    return _pcall(
        kern, name=name, grid=(n_tiles, n_ch // cw),
        in_specs=[prev_spec, cur_spec, next_spec,
                  pl.BlockSpec((n_taps, cw), lambda i, c: (0, c)), pl.BlockSpec((1, cw), lambda i, c: (0, c))],
        out_specs=pl.BlockSpec((ROW_TILE, cw), lambda i, c: (i, c)),
        out_shape=_sds((n_rows, n_ch), F32), compiler_params=_cparams(),
    )(x, x, x, w, bias)


def _dwconv_wgrad(name, dy, x, col0, n_taps, pad_left, seq_starts, n_ch, cw=256):
    n_rows = dy.shape[0]
    n_tiles = n_rows // ROW_TILE
    n_out = -(-(n_taps + 1) // 8) * 8
    prev_spec, cur_spec, hb = _halo_specs(col0, cw)
    last_hb = n_rows // HALO - 1
    next_spec = pl.BlockSpec((HALO, cw), lambda c, i: (jnp.minimum((i + 1) * hb, last_hb), col0 + c))
    prev_spec = pl.BlockSpec((HALO, cw), lambda c, i: (jnp.maximum(i * hb - 1, 0), col0 + c))
    cur_spec = pl.BlockSpec((ROW_TILE, cw), lambda c, i: (i, col0 + c))

    def kern(dy_ref, prev_ref, cur_ref, next_ref, o_ref):
        i = pl.program_id(1)
        has_prev, has_next = _tile_flags(i, n_tiles, seq_starts)
        win = _window(prev_ref, cur_ref, next_ref, has_prev, has_next)
        dyv = dy_ref[...]
        rid = lax.broadcasted_iota(jnp.int32, (n_out, cw), 0)
        inc = jnp.where(rid == n_taps, _sum0(dyv), 0.0)
        for k in range(n_taps):
            off = HALO + k - pad_left
            inc = inc + jnp.where(rid == k, _sum0(dyv * win[off:off + ROW_TILE, :]), 0.0)

        @pl.when(i == 0)
        def _():
            o_ref[...] = jnp.zeros_like(o_ref)

        o_ref[...] += inc

    return _pcall(
        kern, name=name, grid=(n_ch // cw, n_tiles),
        in_specs=[pl.BlockSpec((ROW_TILE, cw), lambda c, i: (i, c)), prev_spec, cur_spec, next_spec],
        out_specs=pl.BlockSpec((n_out, cw), lambda c, i: (0, c)),
        out_shape=_sds((n_out, n_ch), F32), compiler_params=_cparams(),
    )(dy, x, x, x)


N_SCAN = TA // ROW_TILE


def _rev_block(j):
    return jnp.where(j == 0, 0, N_SCAN - j)


def _scan_fwd(a_f, b_f, a_r, b_r):
    fwd_spec = pl.BlockSpec((ROW_TILE, R), lambda i: (i, 0))
    rev_spec = pl.BlockSpec((ROW_TILE, R), lambda i: (_rev_block(i), 0))
    hin_spec = pl.BlockSpec((None, 1, R), lambda i: (i, 0, 0))

    def kern(af, bf, ar, br, yf, yr, hin_f, hin_r, hf_s, hr_s):
        @pl.when(pl.program_id(0) == 0)
        def _():
            hf_s[...] = jnp.zeros_like(hf_s)
            hr_s[...] = jnp.zeros_like(hr_s)

        hin_f[...] = hf_s[...]
        hin_r[...] = hr_s[...]

        def step(s8, carry):
            hf, hr = carry
            t0 = pl.multiple_of(s8 * 8, 8)
            for q in range(8):
                tf = t0 + q
                hf = af[pl.ds(tf, 1), :] * hf + bf[pl.ds(tf, 1), :]
                yf[pl.ds(tf, 1), :] = hf
                tr = ROW_TILE - 1 - tf
                hr = ar[pl.ds(tr, 1), :] * hr + br[pl.ds(tr, 1), :]
                yr[pl.ds(tr, 1), :] = hr
            return hf, hr

        hf, hr = lax.fori_loop(0, ROW_TILE // 8, step, (hf_s[...], hr_s[...]))
        hf_s[...] = hf
        hr_s[...] = hr

    return _pcall(
        kern, name="scan_fwd", grid=(N_SCAN,),
        in_specs=[fwd_spec, fwd_spec, rev_spec, rev_spec],
        out_specs=[fwd_spec, rev_spec, hin_spec, hin_spec],
        out_shape=[_sds((TA, R), F32), _sds((TA, R), F32), _sds((N_SCAN, 1, R), F32), _sds((N_SCAN, 1, R), F32)],
        scratch_shapes=[pltpu.VMEM((1, R), F32), pltpu.VMEM((1, R), F32)], compiler_params=_cparams(),
    )(a_f, b_f, a_r, b_r)


def _scan_bwd(dy, a_f, y_f, hin_f, a_r, y_r, hin_r):
    fwd_spec = pl.BlockSpec((ROW_TILE, R), lambda i: (N_SCAN - 1 - i, 0))
    rev_spec = pl.BlockSpec((ROW_TILE, R), lambda i: (_rev_block(N_SCAN - 1 - i), 0))
    hin_spec = pl.BlockSpec((None, 1, R), lambda i: (N_SCAN - 1 - i, 0, 0))
    last = ROW_TILE - 1

    def kern(dyf, af, yf, hf0, dyr, ar, yr, hr0, daf, dbf, dar, dbr, gf_s, anf_s, gr_s, anr_s):
        @pl.when(pl.program_id(0) == 0)
        def _():
            for r in (gf_s, anf_s, gr_s, anr_s):
                r[...] = jnp.zeros_like(r)

        def one(dy_ref, a_ref, y_ref, da_ref, db_ref, g, an, p, pprev):
            gnew = dy_ref[pl.ds(p, 1), :] + an * g
            db_ref[pl.ds(p, 1), :] = gnew
            da_ref[pl.ds(p, 1), :] = gnew * y_ref[pl.ds(pprev, 1), :]
            return gnew, a_ref[pl.ds(p, 1), :]

        def step(s8, carry):
            gf, anf, gr, anr = carry
            base = s8 * 8
            for q in range(8):
                s = last - (base + q)
                gf, anf = one(dyf, af, yf, daf, dbf, gf, anf, s, s - 1)
                gr, anr = one(dyr, ar, yr, dar, dbr, gr, anr, last - s, last - s + 1)
            return gf, anf, gr, anr

        carry = (gf_s[...], anf_s[...], gr_s[...], anr_s[...])
        carry = lax.fori_loop(0, ROW_TILE // 8 - 1, step, carry)
        gf, anf, gr, anr = carry
        for s in range(7, 0, -1):
            gf, anf = one(dyf, af, yf, daf, dbf, gf, anf, s, s - 1)
            gr, anr = one(dyr, ar, yr, dar, dbr, gr, anr, last - s, last - s + 1)
        gf0 = dyf[0:1, :] + anf * gf
        dbf[0:1, :] = gf0
        daf[0:1, :] = gf0 * hf0[...]
        gr0 = dyr[last:last + 1, :] + anr * gr
        dbr[last:last + 1, :] = gr0
        dar[last:last + 1, :] = gr0 * hr0[...]
        gf_s[...] = gf0
        anf_s[...] = af[0:1, :]
        gr_s[...] = gr0
        anr_s[...] = ar[last:last + 1, :]

    return _pcall(
        kern, name="scan_bwd", grid=(N_SCAN,),
        in_specs=[fwd_spec, fwd_spec, fwd_spec, hin_spec, rev_spec, rev_spec, rev_spec, hin_spec],
        out_specs=[fwd_spec, fwd_spec, rev_spec, rev_spec],
        out_shape=[_sds((TA, R), F32)] * 4,
        scratch_shapes=[pltpu.VMEM((1, R), F32)] * 4, compiler_params=_cparams(),
    )(dy, a_f, y_f, hin_f, dy, a_r, y_r, hin_r)


def _me():
    return lax.axis_index("x"), lax.axis_index("y"), lax.axis_index("c")


def _other_chips(mx, my):
    return [(1 - mx, my), (mx, 1 - my), (1 - mx, 1 - my)]


def _rcopy(src, dst, ssem, rsem, dev):
    return pltpu.make_async_remote_copy(src_ref=src, dst_ref=dst, send_sem=ssem, recv_sem=rsem,
                                        device_id=dev, device_id_type=MESH)


def _allgather8(name, x, dep=None):
    rows, cols = x.shape
    n_dep = len(_behind(dep))

    def kern(x_ref, *rest):
        o_ref, ssem, rsem, lsem = rest[n_dep:]
        mx, my, mc = _me()
        me = 4 * mx + 2 * my + mc
        peers = []
        for k in range(1, 8):
            px = 1 - mx if (k >> 2) & 1 else mx
            py = 1 - my if (k >> 1) & 1 else my
            pc = 1 - mc if k & 1 else mc
            peers.append((px, py, pc))
        mine = pltpu.make_async_copy(x_ref, o_ref.at[me], lsem)
        mine.start()
        sends = [_rcopy(x_ref, o_ref.at[me], ssem.at[k], rsem.at[k], p) for k, p in enumerate(peers)]
        for cp in sends:
            cp.start()
        for k, (px, py, pc) in enumerate(peers):
            _rcopy(x_ref, o_ref.at[4 * px + 2 * py + pc], ssem.at[k], rsem.at[k], (px, py, pc)).wait_recv()
        for cp in sends:
            cp.wait_send()
        mine.wait()

    return _pcall(
        kern, name=name, in_specs=[ANY] * (1 + n_dep), out_specs=ANY, out_shape=_sds((8, rows, cols), F32),
        scratch_shapes=[pltpu.SemaphoreType.DMA((7,)), pltpu.SemaphoreType.DMA((7,)), pltpu.SemaphoreType.DMA(())],
    )(x, *_behind(dep))


def _gather_chips(name, ws):
    n = len(ws)

    def kern(*refs):
        o = refs[n:2 * n]
        s1, r1, s2, r2 = refs[2 * n:]
        mx, my, mc = _me()
        j0 = 2 * mx + my
        chips = _other_chips(mx, my)
        sib = (mx, my, 1 - mc)
        firsts = []
        for t in range(n):
            for q, (qx, qy) in enumerate(chips):
                cp = _rcopy(o[t].at[j0, mc], o[t].at[j0, mc], s1.at[3 * t + q], r1.at[3 * t + q], (qx, qy, mc))
                cp.start()
                firsts.append(cp)
        passed = []
        for t in range(n):
            for q, (qx, qy) in enumerate(chips):
                jq = 2 * qx + qy
                _rcopy(o[t].at[jq, mc], o[t].at[jq, mc], s1.at[3 * t + q], r1.at[3 * t + q], (qx, qy, mc)).wait_recv()
                fw = _rcopy(o[t].at[jq, mc], o[t].at[jq, mc], s2.at[3 * t + q], r2.at[3 * t + q], sib)
                fw.start()
                passed.append(fw)
        for t in range(n):
            for q, (qx, qy) in enumerate(chips):
                jq = 2 * qx + qy
                _rcopy(o[t].at[jq, 1 - mc], o[t].at[jq, 1 - mc], s2.at[3 * t + q], r2.at[3 * t + q], sib).wait_recv()
        for cp in firsts + passed:
            cp.wait_send()

    dma = pltpu.SemaphoreType.DMA
    return _pcall(
        kern, name=name, in_specs=[ANY] * n, out_specs=[ANY] * n,
        out_shape=[_sds(w.shape, w.dtype) for w in ws], input_output_aliases={t: t for t in range(n)},
        scratch_shapes=[dma((3 * n,)), dma((3 * n,)), dma((3 * n,)), dma((3 * n,))],
    )(*ws)


def _reduce_pair(name, gs):
    n = len(gs)

    def kern(*refs):
        g, o = refs[:n], refs[n:2 * n]
        ss, rs = refs[2 * n:]
        mx, my, mc = _me()
        sib = (mx, my, 1 - mc)
        sends = []
        for t in range(n):
            for j in range(4):
                cp = _rcopy(g[t].at[j, 1 - mc], o[t].at[j], ss.at[4 * t + j], rs.at[4 * t + j], sib)
                cp.start()
                sends.append(cp)
        for cp in sends:
            cp.wait_recv()
        for cp in sends:
            cp.wait_send()

    dma = pltpu.SemaphoreType.DMA
    return _pcall(
        kern, name=name, in_specs=[ANY] * n, out_specs=[ANY] * n,
        out_shape=[_sds((4,) + g.shape[2:], g.dtype) for g in gs],
        scratch_shapes=[dma((4 * n,)), dma((4 * n,))],
    )(*gs)


def _share_halves(name, fulls):
    n = len(fulls)

    def kern(*refs):
        o = refs[n:2 * n]
        ss, rs = refs[2 * n:]
        mx, my, mc = _me()
        sib = (mx, my, 1 - mc)
        sends = []
        for t in range(n):
            cp = _rcopy(o[t].at[mc], o[t].at[mc], ss.at[t], rs.at[t], sib)
            cp.start()
            sends.append(cp)
        for t in range(n):
            _rcopy(o[t].at[1 - mc], o[t].at[1 - mc], ss.at[t], rs.at[t], sib).wait_recv()
        for cp in sends:
            cp.wait_send()

    dma = pltpu.SemaphoreType.DMA
    return _pcall(
        kern, name=name, in_specs=[ANY] * n, out_specs=[ANY] * n,
        out_shape=[_sds(f.shape, f.dtype) for f in fulls], input_output_aliases={t: t for t in range(n)},
        scratch_shapes=[dma((n,)), dma((n,))],
    )(*fulls)


def _tiled_sp(name, fn, grid, sp, ins, outs):
    n_in = len(ins)

    def kern(sp_ref, *refs):
        tout = fn([r[...] for r in refs[:n_in]])
        for r, v in zip(refs[n_in:], tout):
            r[...] = v.astype(r.dtype)

    gs = pltpu.PrefetchScalarGridSpec(num_scalar_prefetch=1, grid=tuple(grid),
                                      in_specs=[s for _, s in ins], out_specs=[s for _, s in outs])
    res = _pcall(kern, name=name, grid_spec=gs, out_shape=[o for o, _ in outs], compiler_params=_cparams(),
                 )(sp, *[a for a, _ in ins])
    return list(res)


def _row_tile(rows, cols, itemsize=4, budget=2 * 1024 * 1024):
    tr = rows
    while tr * cols * itemsize > budget and tr % 32 == 0:
        tr //= 2
    return tr


def _place_big(shards, place):
    slots = []
    for t, s in enumerate(shards):
        rr, cc = s.shape[1], s.shape[2]
        tr = _row_tile(rr, cc)
        (slot,) = _tiled_sp(
            f"place{t}", lambda tin: [tin[0]], (2, rr // tr), place,
            [(s, pl.BlockSpec((None, tr, cc), lambda h, i, sp: (h, i, 0)))],
            [(_sds((4, 2, rr, cc), BF16), pl.BlockSpec((None, None, tr, cc), lambda h, i, sp: (sp[0], h, i, 0)))])
        slots.append(slot)
    return slots


def _allreduce_small(vec, place):
    hr = vec.shape[0] // 2
    tr = _row_tile(hr, LANE)
    blk = (None, None, tr, LANE)
    (pair,) = _tiled_sp(
        "small_place", lambda tin: [tin[0]], (2, hr // tr), place,
        [(vec.reshape(2, hr, LANE), pl.BlockSpec((None, tr, LANE), lambda h, i, sp: (h, i, 0)))],
        [(_sds((2, 2, hr, LANE), F32), pl.BlockSpec(blk, lambda h, i, sp: (sp[1], h, i, 0)))])
    (pair,) = _share_halves("small_share", [pair])
    (slot,) = _tiled_sp(
        "small_pair_add", lambda tin: [tin[0] + tin[1]], (2, hr // tr), place,
        [(pair, pl.BlockSpec(blk, lambda h, i, sp: (0, h, i, 0))),
         (pair, pl.BlockSpec(blk, lambda h, i, sp: (1, h, i, 0)))],
        [(_sds((4, 2, hr, LANE), F32), pl.BlockSpec(blk, lambda h, i, sp: (sp[0], h, i, 0)))])
    (chips,) = _gather_chips("small_gather", [slot])
    (total,) = _tiled(
        "small_chip_sum", lambda ids, tin, vin: ([((tin[0] + tin[1]) + tin[2]) + tin[3]], []), (2, hr // tr),
        [(chips, pl.BlockSpec(blk, lambda h, i, _j=j: (_j, h, i, 0))) for j in range(4)], [],
        [(_sds((2, hr, LANE), F32), pl.BlockSpec((None, tr, LANE), lambda h, i: (h, i, 0)))])
    return total.reshape(2 * hr, LANE)


SEM =pl.BlockSpec(memory_space=pltpu.SEMAPHORE)
_DATAFLOW = pltpu.SideEffectType.DATAFLOW_SIDE_EFFECTING


def _gather_start(slots, groups):
    n = len(slots)

    def kern(*refs):
        o = refs[n:2 * n]
        sems, token = refs[2 * n:-1], refs[-1]
        mx, my, mc = _me()
        j0 = 2 * mx + my
        for gi, grp in enumerate(groups):
            for k, t in enumerate(grp):
                for q, (qx, qy) in enumerate(_other_chips(mx, my)):
                    _rcopy(o[t].at[j0, mc], o[t].at[j0, mc], sems[2 * gi].at[3 * k + q],
                           sems[2 * gi + 1].at[3 * k + q], (qx, qy, mc)).start()
        token[...] = jnp.zeros_like(token)

    sem_shapes = []
    for grp in groups:
        sem_shapes += [pltpu.SemaphoreType.DMA((3 * len(grp),))] * 2
    res = _pcall(
        kern, name="gather_start", in_specs=[ANY] * n,
        out_specs=[ANY] * n + [SEM] * len(sem_shapes) + [pl.BlockSpec(memory_space=pltpu.VMEM)],
        out_shape=[_sds(w.shape, w.dtype) for w in slots] + sem_shapes + [_sds((8, LANE), F32)],
        input_output_aliases={t: t for t in range(n)},
        compiler_params=pltpu.CompilerParams(has_side_effects=_DATAFLOW),
    )(*slots)
    return list(res[:n]), list(res[n:-1]), res[-1]


def _gather_wait(name, bufs, ssem, rsem, after):
    n = len(bufs)

    def kern(*refs):
        b = refs[:n]
        ssem_ref, rsem_ref = refs[n], refs[n + 1]
        mx, my, mc = _me()
        j0 = 2 * mx + my
        for k in range(n):
            for q, (qx, qy) in enumerate(_other_chips(mx, my)):
                jq = 2 * qx + qy
                _rcopy(b[k].at[jq, mc], b[k].at[jq, mc], ssem_ref.at[3 * k + q], rsem_ref.at[3 * k + q],
                       (qx, qy, mc)).wait_recv()
                _rcopy(b[k].at[j0, mc], b[k].at[j0, mc], ssem_ref.at[3 * k + q], rsem_ref.at[3 * k + q],
                       (qx, qy, mc)).wait_send()

    return list(_pcall(
        kern, name=name, in_specs=[ANY] * n + [SEM, SEM, ANY], out_specs=[ANY] * n,
        out_shape=[_sds(w.shape, w.dtype) for w in bufs], input_output_aliases={k: k for k in range(n)},
        compiler_params=pltpu.CompilerParams(has_side_effects=_DATAFLOW),
    )(*bufs, ssem, rsem, after))


def _swap_halves(name, bufs):
    n = len(bufs)

    def kern(*refs):
        o = refs[n:2 * n]
        ss, rs = refs[2 * n:]
        mx, my, mc = _me()
        sib = (mx, my, 1 - mc)
        sends = []
        for k in range(n):
            for q, (qx, qy) in enumerate(_other_chips(mx, my)):
                jq = 2 * qx + qy
                cp = _rcopy(o[k].at[jq, mc], o[k].at[jq, mc], ss.at[3 * k + q], rs.at[3 * k + q], sib)
                cp.start()
                sends.append(cp)
        for k in range(n):
            for q, (qx, qy) in enumerate(_other_chips(mx, my)):
                jq = 2 * qx + qy
                _rcopy(o[k].at[jq, 1 - mc], o[k].at[jq, 1 - mc], ss.at[3 * k + q], rs.at[3 * k + q], sib).wait_recv()
        for cp in sends:
            cp.wait_send()

    dma = pltpu.SemaphoreType.DMA
    return list(_pcall(
        kern, name=name, in_specs=[ANY] * n, out_specs=[ANY] * n,
        out_shape=[_sds(w.shape, w.dtype) for w in bufs], input_output_aliases={k: k for k in range(n)},
        scratch_shapes=[dma((3 * n,)), dma((3 * n,))],
    )(*bufs))


def _chips_start(name, sums):
    n = len(sums)

    def kern(*refs):
        s, land = refs[n:2 * n], refs[2 * n:3 * n]
        ssem, rsem, token = refs[3 * n:]
        mx, my, mc = _me()
        for k in range(n):
            for q, (qx, qy) in enumerate(_other_chips(mx, my)):
                _rcopy(s[k].at[2 * qx + qy], land[k].at[q], ssem.at[3 * k + q], rsem.at[3 * k + q], (qx, qy, mc)).start()
        token[...] = jnp.zeros_like(token)

    dma = pltpu.SemaphoreType.DMA
    res = _pcall(
        kern, name=name, in_specs=[ANY] * n,
        out_specs=[ANY] * (2 * n) + [SEM, SEM, pl.BlockSpec(memory_space=pltpu.VMEM)],
        out_shape=[_sds(s.shape, s.dtype) for s in sums] + [_sds((3,) + s.shape[1:], s.dtype) for s in sums]
        + [dma((3 * n,)), dma((3 * n,)), _sds((8, LANE), F32)],
        input_output_aliases={k: k for k in range(n)},
        compiler_params=pltpu.CompilerParams(has_side_effects=_DATAFLOW),
    )(*sums)
    return (list(res[:n]), list(res[n:2 * n]), res[2 * n], res[2 * n + 1]), res[2 * n + 2]


def _chips_wait(name, sums, lands, ssem, rsem, after):
    n = len(sums)

    def kern(*refs):
        s, land = refs[:n], refs[n:2 * n]
        ssem_ref, rsem_ref = refs[2 * n], refs[2 * n + 1]
        mx, my, mc = _me()
        for k in range(n):
            for q, (qx, qy) in enumerate(_other_chips(mx, my)):
                cp = _rcopy(s[k].at[2 * qx + qy], land[k].at[q], ssem_ref.at[3 * k + q], rsem_ref.at[3 * k + q],
                            (qx, qy, mc))
                cp.wait_recv()
                cp.wait_send()

    res = _pcall(
        kern, name=name, in_specs=[ANY] * (2 * n) + [SEM, SEM, ANY], out_specs=[ANY] * (2 * n),
        out_shape=[_sds(a.shape, a.dtype) for a in list(sums) + list(lands)],
        input_output_aliases={k: k for k in range(2 * n)},
        compiler_params=pltpu.CompilerParams(has_side_effects=_DATAFLOW),
    )(*sums, *lands, ssem, rsem, after)
    return list(res[:n]), list(res[n:])


def _reduce_begin(tag, parts, place):
    theirs = _reduce_pair(f"reduce_pair_{tag}", parts)
    sums = []
    for k, (p, o) in enumerate(zip(parts, theirs)):
        rr, cc = p.shape[2], p.shape[3]
        tr = _row_tile(rr, cc)
        (s_k,) = _tiled_sp(
            f"pair_add_{tag}{k}", lambda tin: [tin[0].astype(F32) + tin[1].astype(F32)], (4, rr // tr), place,
            [(p, pl.BlockSpec((None, None, tr, cc), lambda j, i, sp: (j, sp[1], i, 0))),
             (o, pl.BlockSpec((None, tr, cc), lambda j, i, sp: (j, i, 0)))],
            [(_sds((4, rr, cc), BF16), pl.BlockSpec((None, tr, cc), lambda j, i, sp: (j, i, 0)))])
        sums.append(s_k)
    return _chips_start(f"chips_start_{tag}", sums)


def _reduce_end(tag, flying, place, after):
    sums, lands = _chips_wait(f"chips_wait_{tag}", *flying, after)
    fulls = []
    for k, (s, q) in enumerate(zip(sums, lands)):
        rr, cc = q.shape[1], q.shape[2]
        tr = _row_tile(rr, cc)

        def add4(tin):
            return [((tin[0].astype(F32) + tin[1].astype(F32)) + tin[2].astype(F32)) + tin[3].astype(F32)]

        ins = [(s, pl.BlockSpec((None, tr, cc), lambda i, sp: (sp[0], i, 0)))]
        ins += [(q, pl.BlockSpec((None, tr, cc), lambda i, sp, _k=kk: (_k, i, 0))) for kk in range(3)]
        (f_k,) = _tiled_sp(f"chip_add_{tag}{k}", add4, (rr // tr,), place, ins,
                           [(_sds((2, rr, cc), F32), pl.BlockSpec((None, tr, cc), lambda i, sp: (sp[1], i, 0)))])
        fulls.append(f_k)
    return fulls


def _pack(parts, PACK_ROWS=PACK_ROWS):
    flat, offs, pos = [], [], 0
    for p in parts:
        v = p.reshape(-1).astype(F32)
        n = -(-v.shape[0] // LANE) * LANE
        flat.append(jnp.pad(v, (0, n - v.shape[0])))
        offs.append((pos, v.shape[0], p.shape))
        pos += n
    total = -(-pos // (PACK_ROWS * LANE)) * PACK_ROWS * LANE
    flat.append(jnp.zeros((total - pos,), F32))
    return jnp.concatenate(flat).reshape(-1, LANE), offs


def _unpack(vec, offs):
    v = vec.reshape(-1)
    return [v[p:p + n].reshape(shape) for p, n, shape in offs]


def _adamw(name, w, g, m, v):
    rows, cols = w.shape
    tr = rows
    for cand in (512, 256, 128, 64, 32, 16, 8):
        if rows % cand == 0 and cand * cols * 4 <= 2 * 1024 * 1024:
            tr = cand
            break
    bc1 = 1.0 - ADAM_B1 ** ADAM_STEP
    bc2 = 1.0 - ADAM_B2 ** ADAM_STEP

    def fn(ids, tin, vin):
        wv, gv, mv, vv = tin
        mn = ADAM_B1 * mv + (1.0 - ADAM_B1) * gv
        vn = ADAM_B2 * vv + (1.0 - ADAM_B2) * (gv * gv)
        delta = -ADAM_LR * ((mn / bc1) / (jnp.sqrt(vn / bc2) + ADAM_EPS) + ADAM_WD * wv)
        return [delta, mn, vn], []

    spec = pl.BlockSpec((tr, cols), lambda i: (i, 0))
    outs = [(_sds((rows, cols), F32), spec)] * 3
    return _tiled(name, fn, (rows // tr,), [(a, spec) for a in (w, g, m, v)], [], outs)


def _pos_embed():
    n_rows = T // GRID_W
    q = D // 4
    omega = 1.0 / (10000.0 ** (jnp.arange(q, dtype=F32) / q))
    er = jnp.arange(n_rows, dtype=jnp.int32).astype(F32)[:, None] * omega[None, :]
    ec = jnp.arange(GRID_W, dtype=jnp.int32).astype(F32)[:, None] * omega[None, :]
    by_row = jnp.concatenate([jnp.sin(er), jnp.cos(er)], axis=-1)
    by_col = jnp.concatenate([jnp.sin(ec), jnp.cos(ec)], axis=-1)
    return jnp.concatenate([jnp.repeat(by_row, GRID_W, axis=0), jnp.tile(by_col, (n_rows, 1))], axis=-1)


def _dense_gates(w_a, w_x):
    per = N_BLK // 2
    on_diag = _on_diag()
    halves = []
    for h in range(2):
        cols = []
        for src in (w_a[0], w_x[0], w_a[1], w_x[1]):
            rows = src[h * per:(h + 1) * per].reshape(RH, BLK)
            cols.append(jnp.where(on_diag, jnp.tile(rows, (1, per)), 0.0))
        halves.append(jnp.concatenate(cols, axis=1))
    return jnp.stack(halves).astype(BF16)


def _on_diag():
    r = lax.broadcasted_iota(jnp.int32, (RH, RH), 0) // BLK
    c = lax.broadcasted_iota(jnp.int32, (RH, RH), 1) // BLK
    return r == c


def _gate_block_grads(dwbd):
    per = N_BLK // 2
    on_diag = _on_diag()
    kinds = []
    for q in range(4):
        per_half = []
        for h in range(2):
            dq = jnp.where(on_diag, dwbd[h][:, q * RH:(q + 1) * RH], 0.0)
            per_half.append(dq.reshape(RH, per, BLK).sum(axis=1).reshape(per, BLK, BLK))
        kinds.append(jnp.concatenate(per_half, axis=0))
    return jnp.stack([kinds[0], kinds[2]]), jnp.stack([kinds[1], kinds[3]])


def _gate_bias_dense(b_a, b_x):
    cols = []
    for h in range(2):
        for src in (b_a[0], b_x[0], b_a[1], b_x[1]):
            cols.append(src.reshape(R)[h * RH:(h + 1) * RH])
    return jnp.concatenate(cols).reshape(1, 2 * NQ)


def _gate_bias_grads(dgb):
    v = dgb.reshape(2, 4, RH)
    kinds = [jnp.concatenate([v[0, q], v[1, q]]).reshape(N_BLK, BLK) for q in range(4)]
    return jnp.stack([kinds[0], kinds[2]]), jnp.stack([kinds[1], kinds[3]])


def _mlp_fwd(tag, x_in, g_norm, sh, sc, gate, w_in, w_out):
    n_t = T // ROW_TILE
    (h,) = _tiled(f"{tag}_norm", lambda ids, t, v: ([_norm_mod(t[0], v[0], v[1], v[2])], []), (n_t,),
                  [_rows(x_in)], [g_norm, sc, sh], [_orow(T, D, BF16)])
    tm = MM_TILE
    (r,) = _mm(f"{tag}_in", h, w_in, _NN, (T // tm, 4, 1),
               pl.BlockSpec((tm, D), lambda i, j, k: (i, 0)), pl.BlockSpec((None, D, D), lambda i, j, k: (j, 0, 0)),
               [(_sds((T, FF), BF16), pl.BlockSpec((tm, D), lambda i, j, k: (i, j)))], (tm, D),
               epi=lambda acc, ex: [jnp.maximum(acc, 0.0)])
    o, x_out = _mm(f"{tag}_out", r, w_out, _NN, (T // tm, 1, FF // D),
                   pl.BlockSpec((tm, D), lambda i, j, k: (i, k)), pl.BlockSpec((D, D), lambda i, j, k: (k, 0)),
                   [(_sds((T, D), F32), pl.BlockSpec((tm, D), lambda i, j, k: (i, 0)))] * 2, (tm, D),
                   extra=[(x_in, pl.BlockSpec((tm, D), lambda i, j, k: (i, 0))), (gate, _full_spec(gate))],
                   a_pre=lambda a: a * a, epi=lambda acc, ex: [acc, ex[0] + ex[1] * acc])
    return dict(h=h, r=r, o=o, x_in=x_in), x_out


def _behind(dep):
    return [] if dep is None else [dep]


def _gate_bwd(tag, dx, o, gate, dep=None):
    def fn(ids, t, v):
        d_o = t[0] * v[0]
        return [d_o], [_sum0(t[0] * t[1]), _sum0(d_o)]
    return _tiled(f"{tag}_gate_bwd", fn, (T // ROW_TILE,), [_rows(dx), _rows(o)], [gate] + _behind(dep),
                  [_orow(T, D, BF16)], [(1, D), (1, D)])


def _norm_bwd(tag, dx_res, dh, dh_off, x, g_norm, sc, with_dx=True, dep=None):
    n_t = x.shape[0] // ROW_TILE

    def fn(ids, t, v):
        if with_dx:
            dres, dhv, xv = t
        else:
            dhv, xv = t
        dxv, d_sh, d_sc, d_g = _norm_mod_bwd(dhv, xv, v[0], v[1])
        return ([dres + dxv] if with_dx else []), [d_sh, d_sc, d_g]

    ins = ([_rows(dx_res)] if with_dx else []) + [_rows(dh, off=dh_off), _rows(x)]
    outs = [_orow(x.shape[0], D, F32)] if with_dx else []
    return _tiled(f"{tag}_norm_bwd", fn, (n_t,), ins, [g_norm, sc] + _behind(dep), outs, [(1, D)] * 3)


def _mlp_bwd(tag, dx, saved, g_norm, sc, gate, w_in, w_out):
    d_o, d_gate, _ = _gate_bwd(tag, dx, saved["o"], gate)
    tm = MM_TILE
    r = saved["r"]
    (da,) = _mm(f"{tag}_dz", d_o, w_out, _NT, (T // tm, FF // D, 1),
                pl.BlockSpec((tm, D), lambda i, j, k: (i, 0)), pl.BlockSpec((D, D), lambda i, j, k: (j, 0)),
                [(_sds((T, FF), BF16), pl.BlockSpec((tm, D), lambda i, j, k: (i, j)))], (tm, D),
                extra=[(r, pl.BlockSpec((tm, D), lambda i, j, k: (i, j)))],
                epi=lambda acc, ex: [acc * (2.0 * ex[0].astype(F32))])
    tk = MM_TILE
    (dw_out,) = _mm(f"{tag}_dwout", r, d_o, _TN, (FF // tm, 1, T // tk),
                    pl.BlockSpec((tk, tm), lambda i, j, k: (k, i)), pl.BlockSpec((tk, D), lambda i, j, k: (k, 0)),
                    [(_sds((FF, D), BF16), pl.BlockSpec((tm, D), lambda i, j, k: (i, 0)))], (tm, D),
                    a_pre=lambda a: a * a)
    (dh,) = _mm(f"{tag}_dh", da, w_in, _NT, (T // tm, 1, 4),
                pl.BlockSpec((tm, D), lambda i, j, k: (i, k)), pl.BlockSpec((None, D, D), lambda i, j, k: (k, 0, 0)),
                [(_sds((T, D), F32), pl.BlockSpec((tm, D), lambda i, j, k: (i, 0)))], (tm, D))
    (dw_in,) = _mm(f"{tag}_dwin", saved["h"], da, _TN, (D // tm, 4, T // tk),
                   pl.BlockSpec((tk, tm), lambda i, j, k: (k, i)), pl.BlockSpec((tk, D), lambda i, j, k: (k, j)),
                   [(_sds((4, D, D), BF16), pl.BlockSpec((None, tm, D), lambda i, j, k: (j, i, 0)))], (tm, D))
    dx_in, d_sh, d_sc, d_g = _norm_bwd(tag, dx, dh, 0, saved["x_in"], g_norm, sc)
    return dx_in, dw_in, dw_out, dict(sh=d_sh, sc=d_sc, gate=d_gate, g_norm=d_g)


def _local_step(x, ctx, tgt, mods, cmods, norm_g, final_g, rec, conf, wg, on_grads=None):
    on_grads = on_grads or (lambda group, dws: None)
    n_t = T // ROW_TILE
    row = lambda v: v.reshape(1, -1)
    m0 = [row(mods[0, q]) for q in range(6)]
    m1 = [row(mods[1, q]) for q in range(6)]
    g00, g01, g10, g11 = (row(norm_g[0, 0]), row(norm_g[0, 1]), row(norm_g[1, 0]), row(norm_g[1, 1]))
    csh, csc = row(cmods[0]), row(cmods[1])
    pos = _pos_embed()

    def prep0(ids, t, v):
        cx, xv, pv = t
        is_ctx = ids[0] == 0
        xin = jnp.where(is_ctx, cx, xv + pv)
        sh = jnp.where(is_ctx, v[3], v[1])
        sc = jnp.where(is_ctx, v[4], v[2])
        return [_norm_mod(xin, v[0], sc, sh), xv + pv], []

    hcat, x0 = _tiled(
        "prep0", prep0, (N_SCAN,),
        [(ctx, pl.BlockSpec((ROW_TILE, D), lambda i: (0, 0))), _rows(x, off=-1, clamp_lo=True),
         _rows(pos, off=-1, clamp_lo=True)],
        [g00, m0[0], m0[1], csh, csc],
        [_orow(TA, D, BF16), _orow(T, D, F32, off=-1, clamp_lo=True)])

    tm_a = REC_TILE
    w_rec = wg("rec", hcat)
    (a_in,) = _mm("rec_in", hcat, w_rec["rec_w_in"], _NN, (TA // tm_a, 4, 1),
                  pl.BlockSpec((tm_a, D), lambda i, j, k: (i, 0)),
                  pl.BlockSpec((None, D, RH), lambda i, j, k: (j, 0, 0)),
                  [(_sds((TA, 2 * R), F32), pl.BlockSpec((tm_a, RH), lambda i, j, k: (i, j)))], (tm_a, RH))
    rec_starts = (0, 1)
    u = _dwconv("rec_conv", a_in, R // CW_REC, rec["conv_w"], row(rec["conv_b"]), 1, rec_starts, R, CW_REC)
    wbd = _dense_gates(rec["w_a"], rec["w_x"])
    gbias = _gate_bias_dense(rec["b_a"], rec["b_x"])
    lam = rec["lam"]
    a_f, b_f, a_r, b_r = _tiled("rg_fwd", _rg_fwd_fn, (TA // RG_TILE,), [_rows(u, tm=RG_TILE)], [wbd, gbias, lam],
                                [_orow(TA, R, F32, tm=RG_TILE)] * 4, vec_refs=True)
    y_f, y_r, hin_f, hin_r = _scan_fwd(a_f, b_f, a_r, b_r)

    def rec_mid(ids, t, v):
        gp, yf, yr = t
        g, _ = _gelu(gp)
        return [g * (yf + yr)], []

    (m_rec,) = _tiled("rec_mid", rec_mid, (n_t,),
                      [_rows(a_in, R, off=1), _rows(y_f, off=1), _rows(y_r, off=1)], [], [_orow(T, R, BF16)])
    tm = MM_TILE
    o_rec, x1 = _mm("rec_out", m_rec, w_rec["rec_w_out"], _NN, (T // tm, 1, 1),
                    pl.BlockSpec((tm, R), lambda i, j, k: (i, 0)), pl.BlockSpec((R, D), lambda i, j, k: (0, 0)),
                    [(_sds((T, D), F32), pl.BlockSpec((tm, D), lambda i, j, k: (i, 0)))] * 2, (tm, D),
                    extra=[(x0, pl.BlockSpec((tm, D), lambda i, j, k: (i, 0))), (m0[2], _full_spec(m0[2]))],
                    epi=lambda acc, ex: [acc, ex[0] + ex[1] * acc])
    w_m0 = wg("mlp0", x1)
    mlp0, x2 = _mlp_fwd("mlp0", x1, g01, m0[3], m0[4], m0[5], w_m0["w_in"], w_m0["w_out"])

    (h1,) = _tiled("conf_norm", lambda ids, t, v: ([_norm_mod(t[0], v[0], v[1], v[2])], []), (n_t,),
                   [_rows(x2)], [g10, m1[1], m1[0]], [_orow(T, D, BF16)])
    b_pw1 = row(conf["b_pw1"])
    w_cf = wg("conf", x2)
    (pre,) = _mm("conf_pw1", h1, w_cf["conf_w_pw1"], _NN, (T // tm, 4, 1),
                 pl.BlockSpec((tm, D), lambda i, j, k: (i, 0)),
                 pl.BlockSpec((None, D, D // 2), lambda i, j, k: (j, 0, 0)),
                 [(_sds((T, 2 * D), F32), pl.BlockSpec((tm, D // 2), lambda i, j, k: (i, j)))], (tm, D // 2),
                 extra=[(b_pw1, pl.BlockSpec((1, D // 2), lambda i, j, k: (0, j)))],
                 epi=lambda acc, ex: [acc + ex[0]])
    (zg,) = _tiled("conf_glu", lambda ids, t, v: ([t[0] * _sigmoid(t[1])], []), (n_t,),
                   [_rows(pre, D, col=0), _rows(pre, D, col=1)], [], [_orow(T, D, F32)])
    conf_starts = (0,)
    zc = _dwconv("conf_conv", zg, 0, conf["conv_w"], row(conf["conv_b"]), CONF_KW // 2, conf_starts, D, CW_CONF)
    ln_g, ln_b = row(conf["ln_g"]), row(conf["ln_b"])

    def ln_silu(ids, t, v):
        nh, _ = _layernorm_parts(t[0])
        ln = nh * v[0] + v[1]
        return [ln * _sigmoid(ln)], []

    (s_conf,) = _tiled("conf_ln", ln_silu, (n_t,), [_rows(zc)], [ln_g, ln_b], [_orow(T, D, BF16)])
    b_pw2 = row(conf["b_pw2"])
    y_conf, x3 = _mm("conf_pw2", s_conf, w_cf["conf_w_pw2"], _NN, (T // tm, 1, 1),
                     pl.BlockSpec((tm, D), lambda i, j, k: (i, 0)), pl.BlockSpec((D, D), lambda i, j, k: (0, 0)),
                     [(_sds((T, D), F32), pl.BlockSpec((tm, D), lambda i, j, k: (i, 0)))] * 2, (tm, D),
                     extra=[(x2, pl.BlockSpec((tm, D), lambda i, j, k: (i, 0))), (m1[2], _full_spec(m1[2])),
                            (b_pw2, _full_spec(b_pw2))],
                     epi=lambda acc, ex: [acc + ex[2], ex[0] + ex[1] * (acc + ex[2])])
    w_m1 = wg("mlp1", x3)
    mlp1, x4 = _mlp_fwd("mlp1", x3, g11, m1[3], m1[4], m1[5], w_m1["w_in"], w_m1["w_out"])

    fg = row(final_g)

    def head(ids, t, v):
        n, r = _rms(t[0])
        err = n * v[0] - t[1]
        d_out = err * (1.0 / D)
        dn = d_out * v[0]
        dxv = r * (dn - n * jnp.mean(dn * n, axis=-1, keepdims=True))
        part = jnp.sum(_sum0(err * err), axis=1, keepdims=True) * (0.5 / D)
        return [dxv], [part, _sum0(d_out * n)]

    dx4, loss, d_fg = _tiled("head", head, (n_t,), [_rows(x4), _rows(tgt)], [fg], [_orow(T, D, F32)],
                             [(1, 1), (1, D)])

    dx3, dw_in1, dw_out1, dm_mlp1 = _mlp_bwd("mlp1", dx4, mlp1, g11, m1[4], m1[5],
                                             w_m1["w_in"], w_m1["w_out"])
    dep = on_grads("mlp1", (dw_in1, dw_out1))
    d_y, d_g1c, d_bpw2 = _gate_bwd("conf", dx3, y_conf, m1[2], dep)
    tk = MM_TILE
    (dw_pw2,) = _mm("conf_dwpw2", s_conf, d_y, _TN, (D // tm, 1, T // tk),
                    pl.BlockSpec((tk, tm), lambda i, j, k: (k, i)), pl.BlockSpec((tk, D), lambda i, j, k: (k, 0)),
                    [(_sds((D, D), BF16), pl.BlockSpec((tm, D), lambda i, j, k: (i, 0)))], (tm, D))
    (ds,) = _mm("conf_ds", d_y, w_cf["conf_w_pw2"], _NT, (T // tm, 1, 1),
                pl.BlockSpec((tm, D), lambda i, j, k: (i, 0)), pl.BlockSpec((D, D), lambda i, j, k: (0, 0)),
                [(_sds((T, D), F32), pl.BlockSpec((tm, D), lambda i, j, k: (i, 0)))], (tm, D))

    def ln_silu_bwd(ids, t, v):
        dsv, zcv = t
        nh, rstd = _layernorm_parts(zcv)
        ln = nh * v[0] + v[1]
        sg = _sigmoid(ln)
        d_ln = dsv * (sg * (1.0 + ln * (1.0 - sg)))
        d_nh = d_ln * v[0]
        d_zc = rstd * (d_nh - jnp.mean(d_nh, axis=-1, keepdims=True)
                       - nh * jnp.mean(d_nh * nh, axis=-1, keepdims=True))
        return [d_zc], [_sum0(d_ln * nh), _sum0(d_ln)]

    d_zc, d_lng, d_lnb = _tiled("conf_ln_bwd", ln_silu_bwd, (n_t,), [_rows(ds), _rows(zc)], [ln_g, ln_b],
                                [_orow(T, D, F32)], [(1, D), (1, D)])
    d_zg = _dwconv("conf_conv_dx", d_zc, 0, conf["conv_w"][::-1], jnp.zeros((1, D), F32),
                   CONF_KW - 1 - CONF_KW // 2, conf_starts, D, CW_CONF)
    d_cw_conf = _dwconv_wgrad("conf_conv_dw", d_zc, zg, 0, CONF_KW, CONF_KW // 2, conf_starts, D, CW_CONF)

    def glu_bwd(ids, t, v):
        dz, pa, pb = t
        sg = _sigmoid(pb)
        d_a = dz * sg
        d_b = dz * pa * sg * (1.0 - sg)
        return [d_a, d_b], [_sum0(d_a), _sum0(d_b)]

    d_pre_a, d_pre_b, d_b1a, d_b1b = _tiled(
        "conf_glu_bwd", glu_bwd, (n_t,), [_rows(d_zg), _rows(pre, D, col=0), _rows(pre, D, col=1)], [],
        [_orow(T, D, BF16), _orow(T, D, BF16)], [(1, D), (1, D)])
    d_pre = jnp.concatenate([d_pre_a, d_pre_b], axis=1)
    (dw_pw1,) = _mm("conf_dwpw1", h1, d_pre, _TN, (D // tm, 4, T // tk),
                    pl.BlockSpec((tk, tm), lambda i, j, k: (k, i)),
                    pl.BlockSpec((tk, D // 2), lambda i, j, k: (k, j)),
                    [(_sds((4, D, D // 2), BF16), pl.BlockSpec((None, tm, D // 2), lambda i, j, k: (j, i, 0)))],
                    (tm, D // 2))
    dep = on_grads("conf", (dw_pw1, dw_pw2))
    (dh1,) = _mm("conf_dh", d_pre, w_cf["conf_w_pw1"], _NT, (T // tm, 1, 4),
                 pl.BlockSpec((tm, D // 2), lambda i, j, k: (i, k)),
                 pl.BlockSpec((None, D, D // 2), lambda i, j, k: (k, 0, 0)),
                 [(_sds((T, D), F32), pl.BlockSpec((tm, D), lambda i, j, k: (i, 0)))], (tm, D))
    dx2, d_sh1c, d_sc1c, d_g10 = _norm_bwd("conf", dx3, dh1, 0, x2, g10, m1[1], dep=dep)

    dx1, dw_in0, dw_out0, dm_mlp0 = _mlp_bwd("mlp0", dx2, mlp0, g01, m0[4], m0[5],
                                             w_m0["w_in"], w_m0["w_out"])
    dep = on_grads("mlp0", (dw_in0, dw_out0))
    d_orec, d_g1r, _ = _gate_bwd("rec", dx1, o_rec, m0[2], dep)
    (dw_rout,) = _mm("rec_dwout", m_rec, d_orec, _TN, (R // RH, 1, T // tk),
                     pl.BlockSpec((tk, RH), lambda i, j, k: (k, i)), pl.BlockSpec((tk, D), lambda i, j, k: (k, 0)),
                     [(_sds((R, D), BF16), pl.BlockSpec((RH, D), lambda i, j, k: (i, 0)))], (RH, D))
    (dm_rec,) = _mm("rec_dm", d_orec, w_rec["rec_w_out"], _NT, (T // tm, 1, 1),
                    pl.BlockSpec((tm, D), lambda i, j, k: (i, 0)), pl.BlockSpec((R, D), lambda i, j, k: (0, 0)),
                    [(_sds((T, R), F32), pl.BlockSpec((tm, R), lambda i, j, k: (i, 0)))], (tm, R))

    def rec_mid_bwd(ids, t, v):
        dmv, gp, yf, yr = t
        g, th = _gelu(gp)
        lat = ids[0] > 0
        d_gp = jnp.where(lat, dmv * (yf + yr) * _gelu_grad(gp, th), 0.0)
        dy = jnp.where(lat, dmv * g, 0.0)
        return [d_gp, dy], []

    d_gp, dy = _tiled("rec_mid_bwd", rec_mid_bwd, (N_SCAN,),
                      [_rows(dm_rec, off=-1, clamp_lo=True), _rows(a_in, R), _rows(y_f), _rows(y_r)], [],
                      [_orow(TA, R, BF16), _orow(TA, R, F32)])
    da_f, db_f, da_r, db_r = _scan_bwd(dy, a_f, y_f, hin_f, a_r, y_r, hin_r)
    d_gpre, d_u, d_gbias, d_lam = _tiled(
        "rg_bwd", _rg_bwd_fn, (TA // RG_TILE,), [_rows(a, tm=RG_TILE) for a in (u, da_f, db_f, da_r, db_r)],
        [wbd, gbias, lam], [_orow(TA, 2 * NQ, BF16, tm=RG_TILE), _orow(TA, R, F32, tm=RG_TILE)],
        [(1, 2 * NQ), (1, 2 * R)], vec_refs=True)
    tk_a = REC_TILE
    (d_wbd,) = _mm("rg_dw", u, d_gpre, _TN, (2, 2, TA // tk_a),
                   pl.BlockSpec((tk_a, RH), lambda i, j, k: (k, i)),
                   pl.BlockSpec((tk_a, NQ // 2), lambda i, j, k: (k, 2 * i + j)),
                   [(_sds((2, RH, NQ), F32), pl.BlockSpec((None, RH, NQ // 2), lambda i, j, k: (i, 0, j)))],
                   (RH, NQ // 2))
    d_p = _dwconv("rec_conv_dx", d_u, 0, rec["conv_w"][::-1], jnp.zeros((1, R), F32), REC_KW - 1 - 1,
                  rec_starts, R, CW_REC)
    d_cw_rec = _dwconv_wgrad("rec_conv_dw", d_u, a_in, R // CW_REC, REC_KW, 1, rec_starts, R, CW_REC)
    d_a = jnp.concatenate([d_gp, d_p.astype(BF16)], axis=1)
    (dw_rin,) = _mm("rec_dwin", hcat, d_a, _TN, (D // tm, 4, TA // tk_a),
                    pl.BlockSpec((tk_a, tm), lambda i, j, k: (k, i)), pl.BlockSpec((tk_a, RH), lambda i, j, k: (k, j)),
                    [(_sds((4, D, RH), BF16), pl.BlockSpec((None, tm, RH), lambda i, j, k: (j, i, 0)))], (tm, RH))
    dep = on_grads("rec", (dw_rin, dw_rout))
    (dhcat,) = _mm("rec_dh", d_a, w_rec["rec_w_in"], _NT, (TA // tm_a, 1, 4),
                   pl.BlockSpec((tm_a, RH), lambda i, j, k: (i, k)),
                   pl.BlockSpec((None, D, RH), lambda i, j, k: (k, 0, 0)),
                   [(_sds((TA, D), F32), pl.BlockSpec((tm_a, D), lambda i, j, k: (i, 0)))], (tm_a, D))
    dx0, d_sh1r, d_sc1r, d_g00 = _norm_bwd("rec", dx1, dhcat, 1, x0, g00, m0[1], dep=dep)
    d_csh, d_csc, d_g00c = _norm_bwd("ctx", None, dhcat, 0, ctx, g00, csc, with_dx=False)

    big = dict(rec_w_in=dw_rin, rec_w_out=dw_rout, conf_w_pw1=dw_pw1, conf_w_pw2=dw_pw2,
               mlp_w_in=(dw_in0, dw_in1), mlp_w_out=(dw_out0, dw_out1))
    d_wa, d_wx = _gate_block_grads(d_wbd)
    d_ba, d_bx = _gate_bias_grads(d_gbias)
    d_mod = jnp.concatenate([
        d_sh1r, d_sc1r, d_g1r, dm_mlp0["sh"], dm_mlp0["sc"], dm_mlp0["gate"],
        d_sh1c, d_sc1c, d_g1c, dm_mlp1["sh"], dm_mlp1["sc"], dm_mlp1["gate"]], axis=1).reshape(2, 6 * D)
    small = dict(
        d_mod=d_mod, d_cmod=jnp.concatenate([d_csh, d_csc], axis=1),
        norm_g=jnp.concatenate([d_g00 + d_g00c, dm_mlp0["g_norm"], d_g10, dm_mlp1["g_norm"]], axis=1),
        rec_conv_w=d_cw_rec[:REC_KW], rec_conv_b=d_cw_rec[REC_KW], rec_lambda=d_lam.reshape(2, R),
        rec_w_a=d_wa, rec_b_a=d_ba, rec_w_x=d_wx, rec_b_x=d_bx,
        conf_b_pw1=jnp.concatenate([d_b1a, d_b1b], axis=1), conf_conv_w=d_cw_conf[:CONF_KW],
        conf_conv_b=d_cw_conf[CONF_KW], conf_ln_g=d_lng, conf_ln_b=d_lnb, conf_b_pw2=d_bpw2, final_g=d_fg)
    return loss.reshape(()), dx0, big, small


_BIG = ("rec_w_in", "rec_w_out", "conf_w_pw1", "conf_w_pw2", "mlp_w_in", "mlp_w_out")


def _halves(w):
    return w.reshape(2, w.shape[0] // 2, w.shape[1])


def _ada_fwd(c16, w_ada, b_shard):
    ns = w_ada.shape[2]
    tn = 512

    def kern(c_ref, w_ref, b_ref, o_ref):
        cv = c_ref[...]
        s = (cv * _sigmoid(cv)).astype(BF16)
        o_ref[...] = jnp.dot(s, w_ref[...].astype(BF16), preferred_element_type=F32) + b_ref[...]

    return _pcall(
        kern, name="ada_fwd", grid=(2, ns // tn),
        in_specs=[pl.BlockSpec((16, D), lambda l, j: (0, 0)), pl.BlockSpec((None, D, tn), lambda l, j: (l, 0, j)),
                  pl.BlockSpec((None, 1, tn), lambda l, j: (l, 0, j))],
        out_specs=pl.BlockSpec((None, 16, tn), lambda l, j: (l, 0, j)),
        out_shape=_sds((2, 16, ns), F32), compiler_params=_cparams(),
    )(c16, w_ada, b_shard)


def _ada_bwd(c16, dm16, w_ada):
    ns = w_ada.shape[2]
    tn = 512

    def kern(c_ref, dm_ref, w_ref, gw_ref, ds_ref):
        cv = c_ref[...]
        s = (cv * _sigmoid(cv)).astype(BF16)
        dm = dm_ref[...].astype(BF16)
        gw_ref[...] = lax.dot_general(s, dm, _TN, preferred_element_type=F32)

        @pl.when(jnp.logical_and(pl.program_id(0) == 0, pl.program_id(1) == 0))
        def _():
            ds_ref[...] = jnp.zeros_like(ds_ref)

        ds_ref[...] += lax.dot_general(dm, w_ref[...].astype(BF16), _NT, preferred_element_type=F32)

    return _pcall(
        kern, name="ada_bwd", grid=(2, ns // tn),
        in_specs=[pl.BlockSpec((16, D), lambda l, j: (0, 0)), pl.BlockSpec((None, 16, tn), lambda l, j: (l, 0, j)),
                  pl.BlockSpec((None, D, tn), lambda l, j: (l, 0, j))],
        out_specs=[pl.BlockSpec((None, D, tn), lambda l, j: (l, 0, j)), pl.BlockSpec((16, D), lambda l, j: (0, 0))],
        out_shape=[_sds((2, D, ns), F32), _sds((16, D), F32)], compiler_params=_cparams(),
    )(c16, dm16, w_ada)


def _cctx_grad(ds8, c_ctx):
    def kern(d_ref, c_ref, o_ref):
        tot = d_ref[0, 8:9, :] + d_ref[2, 8:9, :] + d_ref[4, 8:9, :] + d_ref[6, 8:9, :]
        cv = c_ref[...]
        sg = _sigmoid(cv)
        o_ref[...] = tot * (sg * (1.0 + cv * (1.0 - sg)))

    return _pcall(kern, name="cctx_grad", out_shape=_sds((1, D), F32))(ds8, c_ctx.reshape(1, D))


def kernel(x, c, ctx, c_ctx, w_ada, b_ada, norm_g, rec_w_in, rec_conv_w, rec_conv_b, rec_lambda, rec_w_a, rec_b_a, rec_w_x, rec_b_x, rec_w_out, conf_w_pw1, conf_b_pw1, conf_conv_w, conf_conv_b, conf_ln_g, conf_ln_b, conf_w_pw2, conf_b_pw2, mlp_w_in, mlp_w_out, final_g, loss_target, m_c_ctx, m_w_ada, m_b_ada, m_norm_g, m_rec_w_in, m_rec_conv_w, m_rec_conv_b, m_rec_lambda, m_rec_w_a, m_rec_b_a, m_rec_w_x, m_rec_b_x, m_rec_w_out, m_conf_w_pw1, m_conf_b_pw1, m_conf_conv_w, m_conf_conv_b, m_conf_ln_g, m_conf_ln_b, m_conf_w_pw2, m_conf_b_pw2, m_mlp_w_in, m_mlp_w_out, m_final_g, v_c_ctx, v_w_ada, v_b_ada, v_norm_g, v_rec_w_in, v_rec_conv_w, v_rec_conv_b, v_rec_lambda, v_rec_w_a, v_rec_b_a, v_rec_w_x, v_rec_b_x, v_rec_w_out, v_conf_w_pw1, v_conf_b_pw1, v_conf_conv_w, v_conf_conv_b, v_conf_ln_g, v_conf_ln_b, v_conf_w_pw2, v_conf_b_pw2, v_mlp_w_in, v_mlp_w_out, v_final_g):
    names = ["c_ctx", "w_ada", "b_ada", "norm_g", "rec_w_in", "rec_conv_w", "rec_conv_b", "rec_lambda", "rec_w_a",
             "rec_b_a", "rec_w_x", "rec_b_x", "rec_w_out", "conf_w_pw1", "conf_b_pw1", "conf_conv_w", "conf_conv_b",
             "conf_ln_g", "conf_ln_b", "conf_w_pw2", "conf_b_pw2", "mlp_w_in", "mlp_w_out", "final_g"]
    w = dict(zip(names, [c_ctx, w_ada, b_ada, norm_g, rec_w_in, rec_conv_w, rec_conv_b, rec_lambda, rec_w_a,
                         rec_b_a, rec_w_x, rec_b_x, rec_w_out, conf_w_pw1, conf_b_pw1, conf_conv_w, conf_conv_b,
                         conf_ln_g, conf_ln_b, conf_w_pw2, conf_b_pw2, mlp_w_in, mlp_w_out, final_g]))
    m = dict(zip(names, [m_c_ctx, m_w_ada, m_b_ada, m_norm_g, m_rec_w_in, m_rec_conv_w, m_rec_conv_b, m_rec_lambda,
                         m_rec_w_a, m_rec_b_a, m_rec_w_x, m_rec_b_x, m_rec_w_out, m_conf_w_pw1, m_conf_b_pw1,
                         m_conf_conv_w, m_conf_conv_b, m_conf_ln_g, m_conf_ln_b, m_conf_w_pw2, m_conf_b_pw2,
                         m_mlp_w_in, m_mlp_w_out, m_final_g]))
    v = dict(zip(names, [v_c_ctx, v_w_ada, v_b_ada, v_norm_g, v_rec_w_in, v_rec_conv_w, v_rec_conv_b, v_rec_lambda,
                         v_rec_w_a, v_rec_b_a, v_rec_w_x, v_rec_b_x, v_rec_w_out, v_conf_w_pw1, v_conf_b_pw1,
                         v_conf_conv_w, v_conf_conv_b, v_conf_ln_g, v_conf_ln_b, v_conf_w_pw2, v_conf_b_pw2,
                         v_mlp_w_in, v_mlp_w_out, v_final_g]))
    mx, my, mc = _me()
    chip = 2 * mx + my
    me = 4 * mx + 2 * my + mc

    place = jnp.stack([chip, mc]).astype(jnp.int32)
    shards = [_halves(rec_w_in[0]), _halves(rec_w_out[0]), _halves(conf_w_pw1[0]), _halves(conf_w_pw2[0]),
              _halves(mlp_w_in[0]), _halves(mlp_w_in[1]), _halves(mlp_w_out[0]), _halves(mlp_w_out[1])]
    use_order = dict(rec=(0, 1), mlp0=(4, 6), conf=(2, 3), mlp1=(5, 7))
    flying, gsems, started = _gather_start(_place_big(shards, place), tuple(use_order.values()))

    def wg(group, after):
        gi = list(use_order).index(group)
        bufs = _gather_wait(f"gather_wait_{group}", [flying[t] for t in use_order[group]], gsems[2 * gi],
                            gsems[2 * gi + 1], after)
        a, b = _swap_halves(f"swap_{group}", bufs)
        if group == "rec":
            return dict(rec_w_in=a.reshape(4, D, RH), rec_w_out=b.reshape(R, D))
        if group == "conf":
            return dict(conf_w_pw1=a.reshape(4, D, D // 2), conf_w_pw2=b.reshape(D, D))
        return dict(w_in=a.reshape(4, D, D), w_out=b.reshape(FF, D))

    sharded_small = ["norm_g", "rec_conv_w", "rec_lambda", "conf_b_pw1", "conf_conv_w", "conf_conv_b", "conf_ln_g",
                     "conf_ln_b", "conf_b_pw2"]
    packed, offs = _pack([c] + [w[k] for k in sharded_small], 8)
    got = _allgather8("gather_small", packed, started)
    per_dev = [_unpack(got[d], offs) for d in range(8)]
    c_rows = jnp.concatenate([per_dev[d][0].reshape(1, D) for d in range(8)], axis=0)
    full = {k: jnp.concatenate([per_dev[2 * j][1 + i] for j in range(4)], axis=-1)
            for i, k in enumerate(sharded_small)}
    c16 = jnp.concatenate([c_rows, c_ctx.reshape(1, D), jnp.zeros((7, D), F32)], axis=0)

    ns = w_ada.shape[2]
    b_shard = lax.dynamic_slice_in_dim(b_ada, chip * ns, ns, axis=1).reshape(2, 1, ns)
    prod = _ada_fwd(c16, w_ada, b_shard)
    prod8 = _allgather8("gather_mod", prod.reshape(32, ns)).reshape(8, 2, 16, ns)
    mod_all = jnp.concatenate([prod8[2 * j] for j in range(4)], axis=-1)
    mods = lax.dynamic_index_in_dim(mod_all, me, axis=1, keepdims=False).reshape(2, 6, D)
    cmods = mod_all[0, 8].reshape(6, D)[:2]

    rec = dict(conv_w=full["rec_conv_w"][0], conv_b=rec_conv_b[0], lam=full["rec_lambda"][0],
               w_a=rec_w_a[0], b_a=rec_b_a[0], w_x=rec_w_x[0], b_x=rec_b_x[0])
    conf = dict(b_pw1=full["conf_b_pw1"][0], conv_w=full["conf_conv_w"][0], conv_b=full["conf_conv_b"][0],
                ln_g=full["conf_ln_g"][0], ln_b=full["conf_ln_b"][0], b_pw2=full["conf_b_pw2"][0])
    sent = {}

    def on_grads(group, dws):
        parts = [dw.reshape(4, 2, shards[t].shape[1], shards[t].shape[2]) for dw, t in zip(dws, use_order[group])]
        sent[group], token = _reduce_begin(group, parts, place)
        return token

    loss_local, grad_x, _, small = _local_step(x[0], ctx[0], loss_target[0], mods, cmods, full["norm_g"], final_g,
                                               rec, conf, wg, on_grads)
    loss = lax.psum(loss_local, ("x", "y", "c"))

    small_names = ["d_mod", "d_cmod", "norm_g", "rec_conv_w", "rec_conv_b", "rec_lambda", "rec_w_a", "rec_b_a",
                   "rec_w_x", "rec_b_x", "conf_b_pw1", "conf_conv_w", "conf_conv_b", "conf_ln_g", "conf_ln_b",
                   "conf_b_pw2", "final_g"]
    mod_slots = lax.dynamic_update_slice(jnp.zeros((8, 2 * 6 * D), F32), small["d_mod"].reshape(1, -1), (me, 0))
    spacked, soffs = _pack([small[k] for k in small_names] + [mod_slots])
    stotal = _allreduce_small(spacked, place)
    unpacked = _unpack(stotal, soffs)
    ssum = dict(zip(small_names, unpacked[:-1]))
    dmod_rows = unpacked[-1].reshape(8, 2, 6 * D).transpose(1, 0, 2)

    fulls = {}
    for group in ("mlp1", "conf", "mlp0", "rec"):
        for t, f in zip(use_order[group], _reduce_end(group, sent[group], place, stotal)):
            fulls[t] = f
    whole = _share_halves("share_grads", [fulls[t] for t in range(8)])
    g_big = dict(rec_w_in=whole[0].reshape(rec_w_in.shape), rec_w_out=whole[1].reshape(rec_w_out.shape),
                 conf_w_pw1=whole[2].reshape(conf_w_pw1.shape), conf_w_pw2=whole[3].reshape(conf_w_pw2.shape),
                 mlp_w_in=jnp.stack([whole[4].reshape(D, D), whole[5].reshape(D, D)]),
                 mlp_w_out=jnp.stack([whole[6].reshape(D, D), whole[7].reshape(D, D)]))

    d_cmod_full =jnp.concatenate([ssum["d_cmod"].reshape(1, 2 * D), jnp.zeros((1, 4 * D), F32)], axis=1)
    dm16 = jnp.concatenate([dmod_rows, jnp.stack([d_cmod_full, jnp.zeros((1, 6 * D), F32)]),
                            jnp.zeros((2, 7, 6 * D), F32)], axis=1)
    dm16_shard = lax.dynamic_slice_in_dim(dm16, chip * ns, ns, axis=2)
    g_w_ada, ds_part = _ada_bwd(c16, dm16_shard, w_ada)
    ds8 = _allgather8("gather_dsilu", ds_part)
    g_c_ctx = _cctx_grad(ds8, c_ctx).reshape(D)
    g_b_ada = ssum["d_mod"] + jnp.stack([d_cmod_full[0], jnp.zeros((6 * D,), F32)])

    def shard_of(a, axis):
        n = a.shape[axis] // 4
        return lax.dynamic_slice_in_dim(a, chip * n, n, axis=axis)

    grads = dict(
        c_ctx=g_c_ctx, w_ada=g_w_ada, b_ada=g_b_ada,
        norm_g=shard_of(ssum["norm_g"].reshape(2, 2, D), 2),
        rec_w_in=g_big["rec_w_in"], rec_conv_w=shard_of(ssum["rec_conv_w"].reshape(1, REC_KW, R), 2),
        rec_conv_b=ssum["rec_conv_b"].reshape(1, R), rec_lambda=shard_of(ssum["rec_lambda"].reshape(1, 2, R), 2),
        rec_w_a=ssum["rec_w_a"].reshape(rec_w_a.shape), rec_b_a=ssum["rec_b_a"].reshape(rec_b_a.shape),
        rec_w_x=ssum["rec_w_x"].reshape(rec_w_x.shape), rec_b_x=ssum["rec_b_x"].reshape(rec_b_x.shape),
        rec_w_out=g_big["rec_w_out"], conf_w_pw1=g_big["conf_w_pw1"],
        conf_b_pw1=shard_of(ssum["conf_b_pw1"].reshape(1, 2 * D), 1),
        conf_conv_w=shard_of(ssum["conf_conv_w"].reshape(1, CONF_KW, D), 2),
        conf_conv_b=shard_of(ssum["conf_conv_b"].reshape(1, D), 1),
        conf_ln_g=shard_of(ssum["conf_ln_g"].reshape(1, D), 1), conf_ln_b=shard_of(ssum["conf_ln_b"].reshape(1, D), 1),
        conf_w_pw2=g_big["conf_w_pw2"], conf_b_pw2=shard_of(ssum["conf_b_pw2"].reshape(1, D), 1),
        mlp_w_in=g_big["mlp_w_in"], mlp_w_out=g_big["mlp_w_out"], final_g=ssum["final_g"].reshape(D))

    delta, new_m, new_v = {}, {}, {}
    big_names = ("w_ada",) + _BIG
    for k in big_names:
        cols = w[k].shape[-1]
        d_, m_, v_ = _adamw(f"adamw_{k}", w[k].reshape(-1, cols), grads[k].reshape(-1, cols),
                            m[k].reshape(-1, cols), v[k].reshape(-1, cols))
        delta[k], new_m[k], new_v[k] = (a.reshape(w[k].shape) for a in (d_, m_, v_))
    rest = [k for k in names if k not in big_names]
    pw, poffs = _pack([w[k] for k in rest])
    pg, _ = _pack([grads[k] for k in rest])
    pm, _ = _pack([m[k] for k in rest])
    pv, _ = _pack([v[k] for k in rest])
    d_, m_, v_ = _adamw("adamw_small", pw, pg, pm, pv)
    for k, dd, mm, vv in zip(rest, _unpack(d_, poffs), _unpack(m_, poffs), _unpack(v_, poffs)):
        delta[k], new_m[k], new_v[k] = dd, mm, vv

    return (loss, grad_x[None], *[grads[k] for k in names], *[delta[k] for k in names],
            *[new_m[k] for k in names], *[new_v[k] for k in names])
```

```python
import functools
import math

import jax
import jax.numpy as jnp
from jax import lax
from jax.experimental import pallas as pl
from jax.experimental.pallas import tpu as pltpu

F32 = jnp.float32
BF16 = jnp.bfloat16

D = 1024
T = 2048
TC = 256
TA = T + TC
R = 1280
RH = R // 2
NQ = 4 * RH
FF = 4096
N_BLK = 16
BLK = R // N_BLK
GRID_W = 64
EPS = 1e-6
RG_C = 8.0
CONF_KW = 31
REC_KW = 4
LANE = 128
ROW_TILE = 256
HALO = 16
RG_TILE = 128
PACK_ROWS = 512
MM_TILE = 1024
REC_TILE = TA // 2
CW_REC = 640
CW_CONF = 512
V7X_VMEM_BYTES = 64 * 1024 * 1024
VMEM_LIMIT = V7X_VMEM_BYTES - 8 * 1024 * 1024

ADAM_LR = 0.001
ADAM_B1 = 0.9
ADAM_B2 = 0.999
ADAM_EPS = 1e-08
ADAM_WD = 0.01
ADAM_STEP = 10

MESH = pl.DeviceIdType.MESH
ANY = pl.BlockSpec(memory_space=pl.ANY)


def _sds(shape, dtype):
    return jax.ShapeDtypeStruct(tuple(shape), dtype)


def _pcall(body, **kw):
    return pl.pallas_call(body, **kw)


def _cparams():
    return pltpu.CompilerParams(vmem_limit_bytes=VMEM_LIMIT)


def _full_spec(arr):
    nd = arr.ndim
    return pl.BlockSpec(arr.shape, lambda *ids, _n=nd: (0,) * _n)


def _sum0(v):
    return jnp.sum(v, axis=0, keepdims=True)


def _tiled(name, fn, grid, ins, vecs, outs, vec_outs=(), vec_refs=False):
    n_in, n_vec, n_out = len(ins), len(vecs), len(outs)
    n_grid = len(grid)

    def kern(*refs):
        ids = [pl.program_id(a) for a in range(n_grid)]
        tin = [r[...] for r in refs[:n_in]]
        vin = list(refs[n_in:n_in + n_vec]) if vec_refs else [r[...] for r in refs[n_in:n_in + n_vec]]
        o_refs = refs[n_in + n_vec:n_in + n_vec + n_out]
        a_refs = refs[n_in + n_vec + n_out:]
        tout, incs = fn(ids, tin, vin)
        for r, v in zip(o_refs, tout):
            r[...] = v.astype(r.dtype)
        if a_refs:
            first = functools.reduce(jnp.logical_and, [i == 0 for i in ids])

            @pl.when(first)
            def _():
                for r in a_refs:
                    r[...] = jnp.zeros_like(r)

            for r, v in zip(a_refs, incs):
                r[...] += v

    out_shape = [o for o, _ in outs] + [_sds(s, F32) for s in vec_outs]
    out_specs = [s for _, s in outs] + [
        pl.BlockSpec(tuple(s), lambda *ids, _n=len(s): (0,) * _n) for s in vec_outs]
    res = _pcall(
        kern, name=name, grid=tuple(grid),
        in_specs=[s for _, s in ins] + [_full_spec(v) for v in vecs],
        out_specs=out_specs, out_shape=out_shape, compiler_params=_cparams(),
    )(*[a for a, _ in ins], *vecs)
    return list(res)


def _rows(arr, ncols=None, tm=ROW_TILE, off=0, col=0, clamp_lo=False):
    ncols = arr.shape[1] if ncols is None else ncols
    if clamp_lo:
        return arr, pl.BlockSpec((tm, ncols), lambda i: (jnp.maximum(i + off, 0), col))
    return arr, pl.BlockSpec((tm, ncols), lambda i: (i + off, col))


def _orow(nrows, ncols, dtype, tm=ROW_TILE, off=0, clamp_lo=False):
    if clamp_lo:
        return _sds((nrows, ncols), dtype), pl.BlockSpec((tm, ncols), lambda i: (jnp.maximum(i + off, 0), 0))
    return _sds((nrows, ncols), dtype), pl.BlockSpec((tm, ncols), lambda i: (i + off, 0))


_NN = (((1,), (0,)), ((), ()))
_TN = (((0,), (0,)), ((), ()))
_NT = (((1,), (1,)), ((), ()))


def _mm(name, a, b, dims, grid, a_spec, b_spec, out, acc_shape, extra=(), a_pre=None, epi=None):
    n_k = grid[2]
    n_ex = len(extra)

    def kern(a_ref, b_ref, *rest):
        ex = rest[:n_ex]
        o_refs = rest[n_ex:n_ex + len(out)]
        k = pl.program_id(2)
        av = a_ref[...]
        if a_pre is not None:
            av = a_pre(av)
        part = lax.dot_general(av.astype(BF16), b_ref[...].astype(BF16), dims, preferred_element_type=F32)

        def finish(total):
            vals = [total] if epi is None else epi(total, [e[...] for e in ex])
            for r, v in zip(o_refs, vals):
                r[...] = v.astype(r.dtype)

        if n_k == 1:
            finish(part)
        else:
            acc = rest[-1]

            @pl.when(k == 0)
            def _():
                acc[...] = part

            @pl.when(jnp.logical_and(k > 0, k < n_k - 1))
            def _():
                acc[...] += part

            @pl.when(k == n_k - 1)
            def _():
                finish(acc[...] + part)

    res = _pcall(
        kern, name=name, grid=tuple(grid),
        in_specs=[a_spec, b_spec] + [s for _, s in extra],
        out_specs=[s for _, s in out], out_shape=[o for o, _ in out],
        scratch_shapes=[] if n_k == 1 else [pltpu.VMEM(tuple(acc_shape), F32)], compiler_params=_cparams(),
    )(a, b, *[e for e, _ in extra])
    return list(res)


def _rms(x):
    r = lax.rsqrt(jnp.mean(x * x, axis=-1, keepdims=True) + EPS)
    return x * r, r


def _norm_mod(x, g, sc, sh):
    n, _ = _rms(x)
    return (n * g) * (1.0 + sc) + sh


def _norm_mod_bwd(dh, x, g, sc):
    n, r = _rms(x)
    d_sh = _sum0(dh)
    d_sc = _sum0(dh * (n * g))
    d_g = _sum0(dh * (1.0 + sc) * n)
    dn = dh * (g * (1.0 + sc))
    dx = r * (dn - n * jnp.mean(dn * n, axis=-1, keepdims=True))
    return dx, d_sh, d_sc, d_g


_GELU_K = math.sqrt(2.0 / math.pi)


def _gelu(x):
    t = jnp.tanh(_GELU_K * (x + 0.044715 * x * x * x))
    return 0.5 * x * (1.0 + t), t


def _gelu_grad(x, t):
    return 0.5 * (1.0 + t) + 0.5 * x * (1.0 - t * t) * (_GELU_K * (1.0 + 3.0 * 0.044715 * x * x))


def _sigmoid(x):
    return 1.0 / (1.0 + jnp.exp(-x))


def _expm1(x):
    p = 1.0 + x * (1.0 / 9.0)
    for n in (8.0, 7.0, 6.0, 5.0, 4.0, 3.0, 2.0):
        p = 1.0 + (x * (1.0 / n)) * p
    return jnp.where(jnp.abs(x) < 0.5, x * p, jnp.exp(x) - 1.0)


def _softplus_neg(lam):
    return jnp.log1p(jnp.exp(-jnp.abs(lam))) + jnp.maximum(-lam, 0.0)


def _layernorm_parts(x):
    mu = jnp.mean(x, axis=-1, keepdims=True)
    xc = x - mu
    rstd = lax.rsqrt(jnp.mean(xc * xc, axis=-1, keepdims=True) + EPS)
    return xc * rstd, rstd


def _rg_gates(u, wbd, gbias, lam):
    sp = _softplus_neg(lam)
    parts = {}
    for h in range(2):
        uh = u[:, h * RH:(h + 1) * RH]
        g = jnp.dot(uh.astype(BF16), wbd[h], preferred_element_type=F32) + gbias[:, h * NQ:(h + 1) * NQ]
        for d in range(2):
            r = _sigmoid(g[:, (2 * d) * RH:(2 * d + 1) * RH])
            i = _sigmoid(g[:, (2 * d + 1) * RH:(2 * d + 2) * RH])
            sph = sp[d:d + 1, h * RH:(h + 1) * RH]
            la = (-RG_C) * r * sph
            e2 = _expm1(2.0 * la)
            parts[(d, h)] = dict(r=r, i=i, la=la, a=jnp.exp(la), e2=e2, mult=jnp.sqrt(-e2), uh=uh, sp=sph)
    return parts


def _rg_fwd_fn(ids, tin, vin):
    (u,) = tin
    wbd = vin[0]
    parts = _rg_gates(u, wbd, vin[1][...], vin[2][...])
    outs = []
    for d in range(2):
        a = jnp.concatenate([parts[(d, h)]["a"] for h in range(2)], axis=1)
        b = jnp.concatenate([parts[(d, h)]["mult"] * parts[(d, h)]["i"] * parts[(d, h)]["uh"]
                             for h in range(2)], axis=1)
        outs += [a, b]
    return outs, []


def _rg_bwd_fn(ids, tin, vin):
    u, da_f, db_f, da_r, db_r = tin
    wbd, lam = vin[0], vin[2][...]
    parts = _rg_gates(u, wbd, vin[1][...], lam)
    dab = ((da_f, db_f), (da_r, db_r))
    dsig_lam = -_sigmoid(-lam)
    du_halves, dpre_halves, dlam = [], [], [[None, None], [None, None]]
    for h in range(2):
        du = jnp.zeros_like(parts[(0, h)]["uh"])
        dpre = []
        for d in range(2):
            p = parts[(d, h)]
            da = dab[d][0][:, h * RH:(h + 1) * RH]
            db = dab[d][1][:, h * RH:(h + 1) * RH]
            d_mult = db * p["i"] * p["uh"]
            d_i = db * p["mult"] * p["uh"]
            du = du + db * p["mult"] * p["i"]
            d_la = da * p["a"] - d_mult * (p["e2"] + 1.0) / p["mult"]
            d_r = d_la * ((-RG_C) * p["sp"])
            dlam[d][h] = _sum0(d_la * ((-RG_C) * p["r"])) * dsig_lam[d:d + 1, h * RH:(h + 1) * RH]
            dpre += [d_r * p["r"] * (1.0 - p["r"]), d_i * p["i"] * (1.0 - p["i"])]
        dpre = jnp.concatenate(dpre, axis=1)
        du = du + lax.dot_general(dpre.astype(BF16), wbd[h], _NT, preferred_element_type=F32)
        du_halves.append(du)
        dpre_halves.append(dpre)
    dpre_all = jnp.concatenate(dpre_halves, axis=1)
    dlam_row = jnp.concatenate([dlam[0][0], dlam[0][1], dlam[1][0], dlam[1][1]], axis=1)
    return [dpre_all, jnp.concatenate(du_halves, axis=1)], [_sum0(dpre_all), dlam_row]


def _tile_flags(i, n_tiles, seq_starts):
    starts_here = functools.reduce(jnp.logical_or, [i == s for s in seq_starts])
    ends_here = functools.reduce(jnp.logical_or, [i + 1 == s for s in seq_starts] + [i + 1 == n_tiles])
    return jnp.logical_not(starts_here), jnp.logical_not(ends_here)


def _halo_specs(col0, cw):
    hb = ROW_TILE // HALO
    prev = pl.BlockSpec((HALO, cw), lambda i, c: (jnp.maximum(i * hb - 1, 0), col0 + c))
    cur = pl.BlockSpec((ROW_TILE, cw), lambda i, c: (i, col0 + c))
    return prev, cur, hb


def _window(prev_ref, cur_ref, next_ref, has_prev, has_next):
    prev = jnp.where(has_prev, prev_ref[...], 0.0)
    nxt = jnp.where(has_next, next_ref[...], 0.0)
    return jnp.concatenate([prev, cur_ref[...], nxt], axis=0)


def _tap_reader(win):
    sub = 8
    shifted = {0: win}
    keep = ROW_TILE + 2 * HALO - sub

    def tap(off):
        s = off % sub
        if s not in shifted:
            shifted[s] = win[s:s + keep, :]
        return shifted[s][off - s:off - s + ROW_TILE, :]

    return tap


def _dwconv(name, x, col0, w, bias, pad_left, seq_starts, n_ch, cw=256):
    n_rows = x.shape[0]
    n_tiles = n_rows // ROW_TILE
    n_taps = w.shape[0]
    prev_spec, cur_spec, hb = _halo_specs(col0, cw)
    last_hb = n_rows // HALO - 1
    next_spec = pl.BlockSpec((HALO, cw), lambda i, c: (jnp.minimum((i + 1) * hb, last_hb), col0 + c))

    def kern(prev_ref, cur_ref, next_ref, w_ref, b_ref, o_ref):
        has_prev, has_next = _tile_flags(pl.program_id(0), n_tiles, seq_starts)
        win = _window(prev_ref, cur_ref, next_ref, has_prev, has_next)
        tap = _tap_reader(win)
        wv = w_ref[...]
        acc = jnp.zeros((ROW_TILE, cw), F32) + b_ref[...]
        for k in range(n_taps):
            acc = acc + wv[k:k + 1, :] * tap(HALO + k - pad_left)
        o_ref[...] = acc

    return _pcall(
        kern, name=name, grid=(n_tiles, n_ch // cw),
        in_specs=[prev_spec, cur_spec, next_spec,
                  pl.BlockSpec((n_taps, cw), lambda i, c: (0, c)), pl.BlockSpec((1, cw), lambda i, c: (0, c))],
        out_specs=pl.BlockSpec((ROW_TILE, cw), lambda i, c: (i, c)),
        out_shape=_sds((n_rows, n_ch), F32), compiler_params=_cparams(),
    )(x, x, x, w, bias)


def _dwconv_wgrad(name, dy, x, col0, n_taps, pad_left, seq_starts, n_ch, cw=256):
    n_rows = dy.shape[0]
    n_tiles = n_rows // ROW_TILE
    n_out = -(-(n_taps + 1) // 8) * 8
    prev_spec, cur_spec, hb = _halo_specs(col0, cw)
    last_hb = n_rows // HALO - 1
    next_spec = pl.BlockSpec((HALO, cw), lambda c, i: (jnp.minimum((i + 1) * hb, last_hb), col0 + c))
    prev_spec = pl.BlockSpec((HALO, cw), lambda c, i: (jnp.maximum(i * hb - 1, 0), col0 + c))
    cur_spec = pl.BlockSpec((ROW_TILE, cw), lambda c, i: (i, col0 + c))

    def kern(dy_ref, prev_ref, cur_ref, next_ref, o_ref):
        i = pl.program_id(1)
        has_prev, has_next = _tile_flags(i, n_tiles, seq_starts)
        win = _window(prev_ref, cur_ref, next_ref, has_prev, has_next)
        dyv = dy_ref[...]
        tap = _tap_reader(win)
        rid = lax.broadcasted_iota(jnp.int32, (n_out, cw), 0)
        inc = jnp.where(rid == n_taps, _sum0(dyv), 0.0)
        for k in range(n_taps):
            inc = inc + jnp.where(rid == k, _sum0(dyv * tap(HALO + k - pad_left)), 0.0)

        @pl.when(i == 0)
        def _():
            o_ref[...] = jnp.zeros_like(o_ref)

        o_ref[...] += inc

    return _pcall(
        kern, name=name, grid=(n_ch // cw, n_tiles),
        in_specs=[pl.BlockSpec((ROW_TILE, cw), lambda c, i: (i, c)), prev_spec, cur_spec, next_spec],
        out_specs=pl.BlockSpec((n_out, cw), lambda c, i: (0, c)),
        out_shape=_sds((n_out, n_ch), F32), compiler_params=_cparams(),
    )(dy, x, x, x)


N_SCAN = TA // ROW_TILE


def _rev_block(j):
    return jnp.where(j == 0, 0, N_SCAN - j)


def _scan_fwd(a_f, b_f, a_r, b_r):
    fwd_spec = pl.BlockSpec((ROW_TILE, R), lambda i: (i, 0))
    rev_spec = pl.BlockSpec((ROW_TILE, R), lambda i: (_rev_block(i), 0))
    hin_spec = pl.BlockSpec((None, 1, R), lambda i: (i, 0, 0))

    def kern(af, bf, ar, br, yf, yr, hin_f, hin_r, hf_s, hr_s):
        @pl.when(pl.program_id(0) == 0)
        def _():
            hf_s[...] = jnp.zeros_like(hf_s)
            hr_s[...] = jnp.zeros_like(hr_s)

        hin_f[...] = hf_s[...]
        hin_r[...] = hr_s[...]

        def step(s8, carry):
            hf, hr = carry
            t0 = pl.multiple_of(s8 * 8, 8)
            for q in range(8):
                tf = t0 + q
                hf = af[pl.ds(tf, 1), :] * hf + bf[pl.ds(tf, 1), :]
                yf[pl.ds(tf, 1), :] = hf
                tr = ROW_TILE - 1 - tf
                hr = ar[pl.ds(tr, 1), :] * hr + br[pl.ds(tr, 1), :]
                yr[pl.ds(tr, 1), :] = hr
            return hf, hr

        hf, hr = lax.fori_loop(0, ROW_TILE // 8, step, (hf_s[...], hr_s[...]))
        hf_s[...] = hf
        hr_s[...] = hr

    return _pcall(
        kern, name="scan_fwd", grid=(N_SCAN,),
        in_specs=[fwd_spec, fwd_spec, rev_spec, rev_spec],
        out_specs=[fwd_spec, rev_spec, hin_spec, hin_spec],
        out_shape=[_sds((TA, R), F32), _sds((TA, R), F32), _sds((N_SCAN, 1, R), F32), _sds((N_SCAN, 1, R), F32)],
        scratch_shapes=[pltpu.VMEM((1, R), F32), pltpu.VMEM((1, R), F32)], compiler_params=_cparams(),
    )(a_f, b_f, a_r, b_r)


def _scan_bwd(dy, a_f, y_f, hin_f, a_r, y_r, hin_r):
    fwd_spec = pl.BlockSpec((ROW_TILE, R), lambda i: (N_SCAN - 1 - i, 0))
    rev_spec = pl.BlockSpec((ROW_TILE, R), lambda i: (_rev_block(N_SCAN - 1 - i), 0))
    hin_spec = pl.BlockSpec((None, 1, R), lambda i: (N_SCAN - 1 - i, 0, 0))
    last = ROW_TILE - 1

    def kern(dyf, af, yf, hf0, dyr, ar, yr, hr0, daf, dbf, dar, dbr, gf_s, anf_s, gr_s, anr_s):
        @pl.when(pl.program_id(0) == 0)
        def _():
            for r in (gf_s, anf_s, gr_s, anr_s):
                r[...] = jnp.zeros_like(r)

        def one(dy_ref, a_ref, y_ref, da_ref, db_ref, g, an, p, pprev):
            gnew = dy_ref[pl.ds(p, 1), :] + an * g
            db_ref[pl.ds(p, 1), :] = gnew
            da_ref[pl.ds(p, 1), :] = gnew * y_ref[pl.ds(pprev, 1), :]
            return gnew, a_ref[pl.ds(p, 1), :]

        def step(s8, carry):
            gf, anf, gr, anr = carry
            base = s8 * 8
            for q in range(8):
                s = last - (base + q)
                gf, anf = one(dyf, af, yf, daf, dbf, gf, anf, s, s - 1)
                gr, anr = one(dyr, ar, yr, dar, dbr, gr, anr, last - s, last - s + 1)
            return gf, anf, gr, anr

        carry = (gf_s[...], anf_s[...], gr_s[...], anr_s[...])
        carry = lax.fori_loop(0, ROW_TILE // 8 - 1, step, carry)
        gf, anf, gr, anr = carry
        for s in range(7, 0, -1):
            gf, anf = one(dyf, af, yf, daf, dbf, gf, anf, s, s - 1)
            gr, anr = one(dyr, ar, yr, dar, dbr, gr, anr, last - s, last - s + 1)
        gf0 = dyf[0:1, :] + anf * gf
        dbf[0:1, :] = gf0
        daf[0:1, :] = gf0 * hf0[...]
        gr0 = dyr[last:last + 1, :] + anr * gr
        dbr[last:last + 1, :] = gr0
        dar[last:last + 1, :] = gr0 * hr0[...]
        gf_s[...] = gf0
        anf_s[...] = af[0:1, :]
        gr_s[...] = gr0
        anr_s[...] = ar[last:last + 1, :]

    return _pcall(
        kern, name="scan_bwd", grid=(N_SCAN,),
        in_specs=[fwd_spec, fwd_spec, fwd_spec, hin_spec, rev_spec, rev_spec, rev_spec, hin_spec],
        out_specs=[fwd_spec, fwd_spec, rev_spec, rev_spec],
        out_shape=[_sds((TA, R), F32)] * 4,
        scratch_shapes=[pltpu.VMEM((1, R), F32)] * 4, compiler_params=_cparams(),
    )(dy, a_f, y_f, hin_f, dy, a_r, y_r, hin_r)


def _me():
    return lax.axis_index("x"), lax.axis_index("y"), lax.axis_index("c")


def _other_chips(mx, my):
    return [(1 - mx, my), (mx, 1 - my), (1 - mx, 1 - my)]


def _rcopy(src, dst, ssem, rsem, dev):
    return pltpu.make_async_remote_copy(src_ref=src, dst_ref=dst, send_sem=ssem, recv_sem=rsem,
                                        device_id=dev, device_id_type=MESH)


def _allgather8(name, x, dep=None):
    rows, cols = x.shape
    n_dep = len(_behind(dep))

    def kern(x_ref, *rest):
        o_ref, ssem, rsem, lsem = rest[n_dep:]
        mx, my, mc = _me()
        me = 4 * mx + 2 * my + mc
        peers = []
        for k in range(1, 8):
            px = 1 - mx if (k >> 2) & 1 else mx
            py = 1 - my if (k >> 1) & 1 else my
            pc = 1 - mc if k & 1 else mc
            peers.append((px, py, pc))
        mine = pltpu.make_async_copy(x_ref, o_ref.at[me], lsem)
        mine.start()
        sends = [_rcopy(x_ref, o_ref.at[me], ssem.at[k], rsem.at[k], p) for k, p in enumerate(peers)]
        for cp in sends:
            cp.start()
        for k, (px, py, pc) in enumerate(peers):
            _rcopy(x_ref, o_ref.at[4 * px + 2 * py + pc], ssem.at[k], rsem.at[k], (px, py, pc)).wait_recv()
        for cp in sends:
            cp.wait_send()
        mine.wait()

    return _pcall(
        kern, name=name, in_specs=[ANY] * (1 + n_dep), out_specs=ANY, out_shape=_sds((8, rows, cols), F32),
        scratch_shapes=[pltpu.SemaphoreType.DMA((7,)), pltpu.SemaphoreType.DMA((7,)), pltpu.SemaphoreType.DMA(())],
    )(x, *_behind(dep))


def _gather_chips(name, ws):
    n = len(ws)

    def kern(*refs):
        o = refs[n:2 * n]
        s1, r1, s2, r2 = refs[2 * n:]
        mx, my, mc = _me()
        j0 = 2 * mx + my
        chips = _other_chips(mx, my)
        sib = (mx, my, 1 - mc)
        firsts = []
        for t in range(n):
            for q, (qx, qy) in enumerate(chips):
                cp = _rcopy(o[t].at[j0, mc], o[t].at[j0, mc], s1.at[3 * t + q], r1.at[3 * t + q], (qx, qy, mc))
                cp.start()
                firsts.append(cp)
        passed = []
        for t in range(n):
            for q, (qx, qy) in enumerate(chips):
                jq = 2 * qx + qy
                _rcopy(o[t].at[jq, mc], o[t].at[jq, mc], s1.at[3 * t + q], r1.at[3 * t + q], (qx, qy, mc)).wait_recv()
                fw = _rcopy(o[t].at[jq, mc], o[t].at[jq, mc], s2.at[3 * t + q], r2.at[3 * t + q], sib)
                fw.start()
                passed.append(fw)
        for t in range(n):
            for q, (qx, qy) in enumerate(chips):
                jq = 2 * qx + qy
                _rcopy(o[t].at[jq, 1 - mc], o[t].at[jq, 1 - mc], s2.at[3 * t + q], r2.at[3 * t + q], sib).wait_recv()
        for cp in firsts + passed:
            cp.wait_send()

    dma = pltpu.SemaphoreType.DMA
    return _pcall(
        kern, name=name, in_specs=[ANY] * n, out_specs=[ANY] * n,
        out_shape=[_sds(w.shape, w.dtype) for w in ws], input_output_aliases={t: t for t in range(n)},
        scratch_shapes=[dma((3 * n,)), dma((3 * n,)), dma((3 * n,)), dma((3 * n,))],
    )(*ws)


def _reduce_pair(name, gs):
    n = len(gs)

    def kern(*refs):
        g, o = refs[:n], refs[n:2 * n]
        ss, rs = refs[2 * n:]
        mx, my, mc = _me()
        sib = (mx, my, 1 - mc)
        sends = []
        for t in range(n):
            for j in range(4):
                cp = _rcopy(g[t].at[j, 1 - mc], o[t].at[j], ss.at[4 * t + j], rs.at[4 * t + j], sib)
                cp.start()
                sends.append(cp)
        for cp in sends:
            cp.wait_recv()
        for cp in sends:
            cp.wait_send()

    dma = pltpu.SemaphoreType.DMA
    return _pcall(
        kern, name=name, in_specs=[ANY] * n, out_specs=[ANY] * n,
        out_shape=[_sds((4,) + g.shape[2:], g.dtype) for g in gs],
        scratch_shapes=[dma((4 * n,)), dma((4 * n,))],
    )(*gs)


def _share_halves(name, fulls):
    n = len(fulls)

    def kern(*refs):
        o = refs[n:2 * n]
        ss, rs = refs[2 * n:]
        mx, my, mc = _me()
        sib = (mx, my, 1 - mc)
        sends = []
        for t in range(n):
            cp = _rcopy(o[t].at[mc], o[t].at[mc], ss.at[t], rs.at[t], sib)
            cp.start()
            sends.append(cp)
        for t in range(n):
            _rcopy(o[t].at[1 - mc], o[t].at[1 - mc], ss.at[t], rs.at[t], sib).wait_recv()
        for cp in sends:
            cp.wait_send()

    dma = pltpu.SemaphoreType.DMA
    return _pcall(
        kern, name=name, in_specs=[ANY] * n, out_specs=[ANY] * n,
        out_shape=[_sds(f.shape, f.dtype) for f in fulls], input_output_aliases={t: t for t in range(n)},
        scratch_shapes=[dma((n,)), dma((n,))],
    )(*fulls)


def _tiled_sp(name, fn, grid, sp, ins, outs):
    n_in = len(ins)

    def kern(sp_ref, *refs):
        tout = fn([r[...] for r in refs[:n_in]])
        for r, v in zip(refs[n_in:], tout):
            r[...] = v.astype(r.dtype)

    gs = pltpu.PrefetchScalarGridSpec(num_scalar_prefetch=1, grid=tuple(grid),
                                      in_specs=[s for _, s in ins], out_specs=[s for _, s in outs])
    res = _pcall(kern, name=name, grid_spec=gs, out_shape=[o for o, _ in outs], compiler_params=_cparams(),
                 )(sp, *[a for a, _ in ins])
    return list(res)


def _row_tile(rows, cols, itemsize=4, budget=2 * 1024 * 1024):
    tr = rows
    while tr * cols * itemsize > budget and tr % 32 == 0:
        tr //= 2
    return tr


def _place_big(shards, place):
    slots = []
    for t, s in enumerate(shards):
        rr, cc = s.shape[1], s.shape[2]
        tr = _row_tile(rr, cc)
        (slot,) = _tiled_sp(
            f"place{t}", lambda tin: [tin[0]], (2, rr // tr), place,
            [(s, pl.BlockSpec((None, tr, cc), lambda h, i, sp: (h, i, 0)))],
            [(_sds((4, 2, rr, cc), BF16), pl.BlockSpec((None, None, tr, cc), lambda h, i, sp: (sp[0], h, i, 0)))])
        slots.append(slot)
    return slots


def _allreduce_small(vec, place):
    hr = vec.shape[0] // 2
    tr = _row_tile(hr, LANE)
    blk = (None, None, tr, LANE)
    (pair,) = _tiled_sp(
        "small_place", lambda tin: [tin[0]], (2, hr // tr), place,
        [(vec.reshape(2, hr, LANE), pl.BlockSpec((None, tr, LANE), lambda h, i, sp: (h, i, 0)))],
        [(_sds((2, 2, hr, LANE), F32), pl.BlockSpec(blk, lambda h, i, sp: (sp[1], h, i, 0)))])
    (pair,) = _share_halves("small_share", [pair])
    (slot,) = _tiled_sp(
        "small_pair_add", lambda tin: [tin[0] + tin[1]], (2, hr // tr), place,
        [(pair, pl.BlockSpec(blk, lambda h, i, sp: (0, h, i, 0))),
         (pair, pl.BlockSpec(blk, lambda h, i, sp: (1, h, i, 0)))],
        [(_sds((4, 2, hr, LANE), F32), pl.BlockSpec(blk, lambda h, i, sp: (sp[0], h, i, 0)))])
    (chips,) = _gather_chips("small_gather", [slot])
    (total,) = _tiled(
        "small_chip_sum", lambda ids, tin, vin: ([((tin[0] + tin[1]) + tin[2]) + tin[3]], []), (2, hr // tr),
        [(chips, pl.BlockSpec(blk, lambda h, i, _j=j: (_j, h, i, 0))) for j in range(4)], [],
        [(_sds((2, hr, LANE), F32), pl.BlockSpec((None, tr, LANE), lambda h, i: (h, i, 0)))])
    return total.reshape(2 * hr, LANE)


SEM =pl.BlockSpec(memory_space=pltpu.SEMAPHORE)
_DATAFLOW = pltpu.SideEffectType.DATAFLOW_SIDE_EFFECTING


def _gather_start(name, slots, groups, after):
    n = len(slots)

    def kern(*refs):
        o = refs[n + 1:2 * n + 1]
        sems, token = refs[2 * n + 1:-1], refs[-1]
        mx, my, mc = _me()
        j0 = 2 * mx + my
        for gi, grp in enumerate(groups):
            for k, t in enumerate(grp):
                for q, (qx, qy) in enumerate(_other_chips(mx, my)):
                    _rcopy(o[t].at[j0, mc], o[t].at[j0, mc], sems[2 * gi].at[3 * k + q],
                           sems[2 * gi + 1].at[3 * k + q], (qx, qy, mc)).start()
        token[...] = jnp.zeros_like(token)

    sem_shapes = []
    for grp in groups:
        sem_shapes += [pltpu.SemaphoreType.DMA((3 * len(grp),))] * 2
    res = _pcall(
        kern, name=name, in_specs=[ANY] * (n + 1),
        out_specs=[ANY] * n + [SEM] * len(sem_shapes) + [pl.BlockSpec(memory_space=pltpu.VMEM)],
        out_shape=[_sds(w.shape, w.dtype) for w in slots] + sem_shapes + [_sds((8, LANE), F32)],
        input_output_aliases={t: t for t in range(n)},
        compiler_params=pltpu.CompilerParams(has_side_effects=_DATAFLOW),
    )(*slots, after)
    return list(res[:n]), list(res[n:-1]), res[-1]


def _gather_wait(name, bufs, ssem, rsem, after):
    n = len(bufs)

    def kern(*refs):
        b = refs[:n]
        ssem_ref, rsem_ref = refs[n], refs[n + 1]
        mx, my, mc = _me()
        j0 = 2 * mx + my
        for k in range(n):
            for q, (qx, qy) in enumerate(_other_chips(mx, my)):
                jq = 2 * qx + qy
                _rcopy(b[k].at[jq, mc], b[k].at[jq, mc], ssem_ref.at[3 * k + q], rsem_ref.at[3 * k + q],
                       (qx, qy, mc)).wait_recv()
                _rcopy(b[k].at[j0, mc], b[k].at[j0, mc], ssem_ref.at[3 * k + q], rsem_ref.at[3 * k + q],
                       (qx, qy, mc)).wait_send()

    return list(_pcall(
        kern, name=name, in_specs=[ANY] * n + [SEM, SEM, ANY], out_specs=[ANY] * n,
        out_shape=[_sds(w.shape, w.dtype) for w in bufs], input_output_aliases={k: k for k in range(n)},
        compiler_params=pltpu.CompilerParams(has_side_effects=_DATAFLOW),
    )(*bufs, ssem, rsem, after))


def _swap_halves(name, bufs):
    n = len(bufs)

    def kern(*refs):
        o = refs[n:2 * n]
        ss, rs = refs[2 * n:]
        mx, my, mc = _me()
        sib = (mx, my, 1 - mc)
        sends = []
        for k in range(n):
            for q, (qx, qy) in enumerate(_other_chips(mx, my)):
                jq = 2 * qx + qy
                cp = _rcopy(o[k].at[jq, mc], o[k].at[jq, mc], ss.at[3 * k + q], rs.at[3 * k + q], sib)
                cp.start()
                sends.append(cp)
        for k in range(n):
            for q, (qx, qy) in enumerate(_other_chips(mx, my)):
                jq = 2 * qx + qy
                _rcopy(o[k].at[jq, 1 - mc], o[k].at[jq, 1 - mc], ss.at[3 * k + q], rs.at[3 * k + q], sib).wait_recv()
        for cp in sends:
            cp.wait_send()

    dma = pltpu.SemaphoreType.DMA
    return list(_pcall(
        kern, name=name, in_specs=[ANY] * n, out_specs=[ANY] * n,
        out_shape=[_sds(w.shape, w.dtype) for w in bufs], input_output_aliases={k: k for k in range(n)},
        scratch_shapes=[dma((3 * n,)), dma((3 * n,))],
    )(*bufs))


def _chips_start(name, sums):
    n = len(sums)

    def kern(*refs):
        s, land = refs[n:2 * n], refs[2 * n:3 * n]
        ssem, rsem, token = refs[3 * n:]
        mx, my, mc = _me()
        for k in range(n):
            for q, (qx, qy) in enumerate(_other_chips(mx, my)):
                _rcopy(s[k].at[2 * qx + qy], land[k].at[q], ssem.at[3 * k + q], rsem.at[3 * k + q], (qx, qy, mc)).start()
        token[...] = jnp.zeros_like(token)

    dma = pltpu.SemaphoreType.DMA
    res = _pcall(
        kern, name=name, in_specs=[ANY] * n,
        out_specs=[ANY] * (2 * n) + [SEM, SEM, pl.BlockSpec(memory_space=pltpu.VMEM)],
        out_shape=[_sds(s.shape, s.dtype) for s in sums] + [_sds((3,) + s.shape[1:], s.dtype) for s in sums]
        + [dma((3 * n,)), dma((3 * n,)), _sds((8, LANE), F32)],
        input_output_aliases={k: k for k in range(n)},
        compiler_params=pltpu.CompilerParams(has_side_effects=_DATAFLOW),
    )(*sums)
    return (list(res[:n]), list(res[n:2 * n]), res[2 * n], res[2 * n + 1]), res[2 * n + 2]


def _chips_wait(name, sums, lands, ssem, rsem, after):
    n = len(sums)

    def kern(*refs):
        s, land = refs[:n], refs[n:2 * n]
        ssem_ref, rsem_ref = refs[2 * n], refs[2 * n + 1]
        mx, my, mc = _me()
        for k in range(n):
            for q, (qx, qy) in enumerate(_other_chips(mx, my)):
                cp = _rcopy(s[k].at[2 * qx + qy], land[k].at[q], ssem_ref.at[3 * k + q], rsem_ref.at[3 * k + q],
                            (qx, qy, mc))
                cp.wait_recv()
                cp.wait_send()

    res = _pcall(
        kern, name=name, in_specs=[ANY] * (2 * n) + [SEM, SEM, ANY], out_specs=[ANY] * (2 * n),
        out_shape=[_sds(a.shape, a.dtype) for a in list(sums) + list(lands)],
        input_output_aliases={k: k for k in range(2 * n)},
        compiler_params=pltpu.CompilerParams(has_side_effects=_DATAFLOW),
    )(*sums, *lands, ssem, rsem, after)
    return list(res[:n]), list(res[n:])


def _reduce_begin(tag, parts, place):
    theirs = _reduce_pair(f"reduce_pair_{tag}", parts)
    sums = []
    for k, (p, o) in enumerate(zip(parts, theirs)):
        rr, cc = p.shape[2], p.shape[3]
        tr = _row_tile(rr, cc)
        (s_k,) = _tiled_sp(
            f"pair_add_{tag}{k}", lambda tin: [tin[0].astype(F32) + tin[1].astype(F32)], (4, rr // tr), place,
            [(p, pl.BlockSpec((None, None, tr, cc), lambda j, i, sp: (j, sp[1], i, 0))),
             (o, pl.BlockSpec((None, tr, cc), lambda j, i, sp: (j, i, 0)))],
            [(_sds((4, rr, cc), BF16), pl.BlockSpec((None, tr, cc), lambda j, i, sp: (j, i, 0)))])
        sums.append(s_k)
    return _chips_start(f"chips_start_{tag}", sums)


def _reduce_end(tag, flying, place, after):
    sums, lands = _chips_wait(f"chips_wait_{tag}", *flying, after)
    fulls = []
    for k, (s, q) in enumerate(zip(sums, lands)):
        rr, cc = q.shape[1], q.shape[2]
        tr = _row_tile(rr, cc)

        def add4(tin):
            return [((tin[0].astype(F32) + tin[1].astype(F32)) + tin[2].astype(F32)) + tin[3].astype(F32)]

        ins = [(s, pl.BlockSpec((None, tr, cc), lambda i, sp: (sp[0], i, 0)))]
        ins += [(q, pl.BlockSpec((None, tr, cc), lambda i, sp, _k=kk: (_k, i, 0))) for kk in range(3)]
        (f_k,) = _tiled_sp(f"chip_add_{tag}{k}", add4, (rr // tr,), place, ins,
                           [(_sds((2, rr, cc), F32), pl.BlockSpec((None, tr, cc), lambda i, sp: (sp[1], i, 0)))])
        fulls.append(f_k)
    return fulls


def _pack(parts, PACK_ROWS=PACK_ROWS):
    flat, offs, pos = [], [], 0
    for p in parts:
        v = p.reshape(-1).astype(F32)
        n = -(-v.shape[0] // LANE) * LANE
        flat.append(jnp.pad(v, (0, n - v.shape[0])))
        offs.append((pos, v.shape[0], p.shape))
        pos += n
    total = -(-pos // (PACK_ROWS * LANE)) * PACK_ROWS * LANE
    flat.append(jnp.zeros((total - pos,), F32))
    return jnp.concatenate(flat).reshape(-1, LANE), offs


def _unpack(vec, offs):
    v = vec.reshape(-1)
    return [v[p:p + n].reshape(shape) for p, n, shape in offs]


def _adamw(name, w, g, m, v):
    rows, cols = w.shape
    tr = rows
    for cand in (512, 256, 128, 64, 32, 16, 8):
        if rows % cand == 0 and cand * cols * 4 <= 2 * 1024 * 1024:
            tr = cand
            break
    bc1 = 1.0 - ADAM_B1 ** ADAM_STEP
    bc2 = 1.0 - ADAM_B2 ** ADAM_STEP

    def fn(ids, tin, vin):
        wv, gv, mv, vv = tin
        mn = ADAM_B1 * mv + (1.0 - ADAM_B1) * gv
        vn = ADAM_B2 * vv + (1.0 - ADAM_B2) * (gv * gv)
        delta = -ADAM_LR * ((mn / bc1) / (jnp.sqrt(vn / bc2) + ADAM_EPS) + ADAM_WD * wv)
        return [delta, mn, vn], []

    spec = pl.BlockSpec((tr, cols), lambda i: (i, 0))
    outs = [(_sds((rows, cols), F32), spec)] * 3
    return _tiled(name, fn, (rows // tr,), [(a, spec) for a in (w, g, m, v)], [], outs)


def _pos_embed():
    n_rows = T // GRID_W
    q = D // 4
    omega = 1.0 / (10000.0 ** (jnp.arange(q, dtype=F32) / q))
    er = jnp.arange(n_rows, dtype=jnp.int32).astype(F32)[:, None] * omega[None, :]
    ec = jnp.arange(GRID_W, dtype=jnp.int32).astype(F32)[:, None] * omega[None, :]
    by_row = jnp.concatenate([jnp.sin(er), jnp.cos(er)], axis=-1)
    by_col = jnp.concatenate([jnp.sin(ec), jnp.cos(ec)], axis=-1)
    return jnp.concatenate([jnp.repeat(by_row, GRID_W, axis=0), jnp.tile(by_col, (n_rows, 1))], axis=-1)


def _dense_gates(w_a, w_x):
    per = N_BLK // 2
    on_diag = _on_diag()
    halves = []
    for h in range(2):
        cols = []
        for src in (w_a[0], w_x[0], w_a[1], w_x[1]):
            rows = src[h * per:(h + 1) * per].reshape(RH, BLK)
            cols.append(jnp.where(on_diag, jnp.tile(rows, (1, per)), 0.0))
        halves.append(jnp.concatenate(cols, axis=1))
    return jnp.stack(halves).astype(BF16)


def _on_diag():
    r = lax.broadcasted_iota(jnp.int32, (RH, RH), 0) // BLK
    c = lax.broadcasted_iota(jnp.int32, (RH, RH), 1) // BLK
    return r == c


def _gate_block_grads(dwbd):
    per = N_BLK // 2
    on_diag = _on_diag()
    kinds = []
    for q in range(4):
        per_half = []
        for h in range(2):
            dq = jnp.where(on_diag, dwbd[h][:, q * RH:(q + 1) * RH], 0.0)
            per_half.append(dq.reshape(RH, per, BLK).sum(axis=1).reshape(per, BLK, BLK))
        kinds.append(jnp.concatenate(per_half, axis=0))
    return jnp.stack([kinds[0], kinds[2]]), jnp.stack([kinds[1], kinds[3]])


def _gate_bias_dense(b_a, b_x):
    cols = []
    for h in range(2):
        for src in (b_a[0], b_x[0], b_a[1], b_x[1]):
            cols.append(src.reshape(R)[h * RH:(h + 1) * RH])
    return jnp.concatenate(cols).reshape(1, 2 * NQ)


def _gate_bias_grads(dgb):
    v = dgb.reshape(2, 4, RH)
    kinds = [jnp.concatenate([v[0, q], v[1, q]]).reshape(N_BLK, BLK) for q in range(4)]
    return jnp.stack([kinds[0], kinds[2]]), jnp.stack([kinds[1], kinds[3]])


def _mlp_fwd(tag, x_in, g_norm, sh, sc, gate, w_in, w_out):
    n_t = T // ROW_TILE
    (h,) = _tiled(f"{tag}_norm", lambda ids, t, v: ([_norm_mod(t[0], v[0], v[1], v[2])], []), (n_t,),
                  [_rows(x_in)], [g_norm, sc, sh], [_orow(T, D, BF16)])
    tm = MM_TILE
    (r,) = _mm(f"{tag}_in", h, w_in, _NN, (T // tm, 4, 1),
               pl.BlockSpec((tm, D), lambda i, j, k: (i, 0)), pl.BlockSpec((None, D, D), lambda i, j, k: (j, 0, 0)),
               [(_sds((T, FF), BF16), pl.BlockSpec((tm, D), lambda i, j, k: (i, j)))], (tm, D),
               epi=lambda acc, ex: [jnp.maximum(acc, 0.0)])
    o, x_out = _mm(f"{tag}_out", r, w_out, _NN, (T // tm, 1, FF // D),
                   pl.BlockSpec((tm, D), lambda i, j, k: (i, k)), pl.BlockSpec((D, D), lambda i, j, k: (k, 0)),
                   [(_sds((T, D), F32), pl.BlockSpec((tm, D), lambda i, j, k: (i, 0)))] * 2, (tm, D),
                   extra=[(x_in, pl.BlockSpec((tm, D), lambda i, j, k: (i, 0))), (gate, _full_spec(gate))],
                   a_pre=lambda a: a * a, epi=lambda acc, ex: [acc, ex[0] + ex[1] * acc])
    return dict(h=h, r=r, o=o, x_in=x_in), x_out


def _behind(dep):
    return [] if dep is None else [dep]


def _gate_bwd(tag, dx, o, gate, dep=None):
    def fn(ids, t, v):
        d_o = t[0] * v[0]
        return [d_o], [_sum0(t[0] * t[1]), _sum0(d_o)]
    return _tiled(f"{tag}_gate_bwd", fn, (T // ROW_TILE,), [_rows(dx), _rows(o)], [gate] + _behind(dep),
                  [_orow(T, D, BF16)], [(1, D), (1, D)])


def _norm_bwd(tag, dx_res, dh, dh_off, x, g_norm, sc, with_dx=True, dep=None):
    n_t = x.shape[0] // ROW_TILE

    def fn(ids, t, v):
        if with_dx:
            dres, dhv, xv = t
        else:
            dhv, xv = t
        dxv, d_sh, d_sc, d_g = _norm_mod_bwd(dhv, xv, v[0], v[1])
        return ([dres + dxv] if with_dx else []), [d_sh, d_sc, d_g]

    ins = ([_rows(dx_res)] if with_dx else []) + [_rows(dh, off=dh_off), _rows(x)]
    outs = [_orow(x.shape[0], D, F32)] if with_dx else []
    return _tiled(f"{tag}_norm_bwd", fn, (n_t,), ins, [g_norm, sc] + _behind(dep), outs, [(1, D)] * 3)


def _mlp_bwd(tag, dx, saved, g_norm, sc, gate, w_in, w_out):
    d_o, d_gate, _ = _gate_bwd(tag, dx, saved["o"], gate)
    tm = MM_TILE
    r = saved["r"]
    (da,) = _mm(f"{tag}_dz", d_o, w_out, _NT, (T // tm, FF // D, 1),
                pl.BlockSpec((tm, D), lambda i, j, k: (i, 0)), pl.BlockSpec((D, D), lambda i, j, k: (j, 0)),
                [(_sds((T, FF), BF16), pl.BlockSpec((tm, D), lambda i, j, k: (i, j)))], (tm, D),
                extra=[(r, pl.BlockSpec((tm, D), lambda i, j, k: (i, j)))],
                epi=lambda acc, ex: [acc * (2.0 * ex[0].astype(F32))])
    tk = MM_TILE
    (dw_out,) = _mm(f"{tag}_dwout", r, d_o, _TN, (FF // tm, 1, T // tk),
                    pl.BlockSpec((tk, tm), lambda i, j, k: (k, i)), pl.BlockSpec((tk, D), lambda i, j, k: (k, 0)),
                    [(_sds((FF, D), BF16), pl.BlockSpec((tm, D), lambda i, j, k: (i, 0)))], (tm, D),
                    a_pre=lambda a: a * a)
    (dh,) = _mm(f"{tag}_dh", da, w_in, _NT, (T // tm, 1, 4),
                pl.BlockSpec((tm, D), lambda i, j, k: (i, k)), pl.BlockSpec((None, D, D), lambda i, j, k: (k, 0, 0)),
                [(_sds((T, D), F32), pl.BlockSpec((tm, D), lambda i, j, k: (i, 0)))], (tm, D))
    (dw_in,) = _mm(f"{tag}_dwin", saved["h"], da, _TN, (D // tm, 4, T // tk),
                   pl.BlockSpec((tk, tm), lambda i, j, k: (k, i)), pl.BlockSpec((tk, D), lambda i, j, k: (k, j)),
                   [(_sds((4, D, D), BF16), pl.BlockSpec((None, tm, D), lambda i, j, k: (j, i, 0)))], (tm, D))
    dx_in, d_sh, d_sc, d_g = _norm_bwd(tag, dx, dh, 0, saved["x_in"], g_norm, sc)
    return dx_in, dw_in, dw_out, dict(sh=d_sh, sc=d_sc, gate=d_gate, g_norm=d_g)


def _local_step(x, ctx, tgt, mods, cmods, norm_g, final_g, rec, conf, wg, on_grads=None, start_dep=None):
    on_grads = on_grads or (lambda group, dws: None)
    n_t = T // ROW_TILE
    row = lambda v: v.reshape(1, -1)
    m0 = [row(mods[0, q]) for q in range(6)]
    m1 = [row(mods[1, q]) for q in range(6)]
    g00, g01, g10, g11 = (row(norm_g[0, 0]), row(norm_g[0, 1]), row(norm_g[1, 0]), row(norm_g[1, 1]))
    csh, csc = row(cmods[0]), row(cmods[1])
    pos = _pos_embed()

    def prep0(ids, t, v):
        cx, xv, pv = t
        is_ctx = ids[0] == 0
        xin = jnp.where(is_ctx, cx, xv + pv)
        sh = jnp.where(is_ctx, v[3], v[1])
        sc = jnp.where(is_ctx, v[4], v[2])
        return [_norm_mod(xin, v[0], sc, sh), xv + pv], []

    hcat, x0 = _tiled(
        "prep0", prep0, (N_SCAN,),
        [(ctx, pl.BlockSpec((ROW_TILE, D), lambda i: (0, 0))), _rows(x, off=-1, clamp_lo=True),
         _rows(pos, off=-1, clamp_lo=True)],
        [g00, m0[0], m0[1], csh, csc] + _behind(start_dep),
        [_orow(TA, D, BF16), _orow(T, D, F32, off=-1, clamp_lo=True)])

    tm_a = REC_TILE
    w_rec = wg("rec", hcat)
    (a_in,) = _mm("rec_in", hcat, w_rec["rec_w_in"], _NN, (TA // tm_a, 4, 1),
                  pl.BlockSpec((tm_a, D), lambda i, j, k: (i, 0)),
                  pl.BlockSpec((None, D, RH), lambda i, j, k: (j, 0, 0)),
                  [(_sds((TA, 2 * R), F32), pl.BlockSpec((tm_a, RH), lambda i, j, k: (i, j)))], (tm_a, RH))
    rec_starts = (0, 1)
    u = _dwconv("rec_conv", a_in, R // CW_REC, rec["conv_w"], row(rec["conv_b"]), 1, rec_starts, R, CW_REC)
    wbd = _dense_gates(rec["w_a"], rec["w_x"])
    gbias = _gate_bias_dense(rec["b_a"], rec["b_x"])
    lam = rec["lam"]
    a_f, b_f, a_r, b_r = _tiled("rg_fwd", _rg_fwd_fn, (TA // RG_TILE,), [_rows(u, tm=RG_TILE)], [wbd, gbias, lam],
                                [_orow(TA, R, F32, tm=RG_TILE)] * 4, vec_refs=True)
    y_f, y_r, hin_f, hin_r = _scan_fwd(a_f, b_f, a_r, b_r)

    def rec_mid(ids, t, v):
        gp, yf, yr = t
        g, _ = _gelu(gp)
        return [g * (yf + yr)], []

    (m_rec,) = _tiled("rec_mid", rec_mid, (n_t,),
                      [_rows(a_in, R, off=1), _rows(y_f, off=1), _rows(y_r, off=1)], [], [_orow(T, R, BF16)])
    tm = MM_TILE
    o_rec, x1 = _mm("rec_out", m_rec, w_rec["rec_w_out"], _NN, (T // tm, 1, 1),
                    pl.BlockSpec((tm, R), lambda i, j, k: (i, 0)), pl.BlockSpec((R, D), lambda i, j, k: (0, 0)),
                    [(_sds((T, D), F32), pl.BlockSpec((tm, D), lambda i, j, k: (i, 0)))] * 2, (tm, D),
                    extra=[(x0, pl.BlockSpec((tm, D), lambda i, j, k: (i, 0))), (m0[2], _full_spec(m0[2]))],
                    epi=lambda acc, ex: [acc, ex[0] + ex[1] * acc])
    w_m0 = wg("mlp0", x1)
    mlp0, x2 = _mlp_fwd("mlp0", x1, g01, m0[3], m0[4], m0[5], w_m0["w_in"], w_m0["w_out"])

    (h1,) = _tiled("conf_norm", lambda ids, t, v: ([_norm_mod(t[0], v[0], v[1], v[2])], []), (n_t,),
                   [_rows(x2)], [g10, m1[1], m1[0]], [_orow(T, D, BF16)])
    b_pw1 = row(conf["b_pw1"])
    w_cf = wg("conf", x2)
    (pre,) = _mm("conf_pw1", h1, w_cf["conf_w_pw1"], _NN, (T // tm, 4, 1),
                 pl.BlockSpec((tm, D), lambda i, j, k: (i, 0)),
                 pl.BlockSpec((None, D, D // 2), lambda i, j, k: (j, 0, 0)),
                 [(_sds((T, 2 * D), F32), pl.BlockSpec((tm, D // 2), lambda i, j, k: (i, j)))], (tm, D // 2),
                 extra=[(b_pw1, pl.BlockSpec((1, D // 2), lambda i, j, k: (0, j)))],
                 epi=lambda acc, ex: [acc + ex[0]])
    (zg,) = _tiled("conf_glu", lambda ids, t, v: ([t[0] * _sigmoid(t[1])], []), (n_t,),
                   [_rows(pre, D, col=0), _rows(pre, D, col=1)], [], [_orow(T, D, F32)])
    conf_starts = (0,)
    zc = _dwconv("conf_conv", zg, 0, conf["conv_w"], row(conf["conv_b"]), CONF_KW // 2, conf_starts, D, CW_CONF)
    ln_g, ln_b = row(conf["ln_g"]), row(conf["ln_b"])

    def ln_silu(ids, t, v):
        nh, _ = _layernorm_parts(t[0])
        ln = nh * v[0] + v[1]
        return [ln * _sigmoid(ln)], []

    (s_conf,) = _tiled("conf_ln", ln_silu, (n_t,), [_rows(zc)], [ln_g, ln_b], [_orow(T, D, BF16)])
    b_pw2 = row(conf["b_pw2"])
    y_conf, x3 = _mm("conf_pw2", s_conf, w_cf["conf_w_pw2"], _NN, (T // tm, 1, 1),
                     pl.BlockSpec((tm, D), lambda i, j, k: (i, 0)), pl.BlockSpec((D, D), lambda i, j, k: (0, 0)),
                     [(_sds((T, D), F32), pl.BlockSpec((tm, D), lambda i, j, k: (i, 0)))] * 2, (tm, D),
                     extra=[(x2, pl.BlockSpec((tm, D), lambda i, j, k: (i, 0))), (m1[2], _full_spec(m1[2])),
                            (b_pw2, _full_spec(b_pw2))],
                     epi=lambda acc, ex: [acc + ex[2], ex[0] + ex[1] * (acc + ex[2])])
    w_m1 = wg("mlp1", x3)
    mlp1, x4 = _mlp_fwd("mlp1", x3, g11, m1[3], m1[4], m1[5], w_m1["w_in"], w_m1["w_out"])

    fg = row(final_g)

    def head(ids, t, v):
        n, r = _rms(t[0])
        err = n * v[0] - t[1]
        d_out = err * (1.0 / D)
        dn = d_out * v[0]
        dxv = r * (dn - n * jnp.mean(dn * n, axis=-1, keepdims=True))
        part = jnp.sum(_sum0(err * err), axis=1, keepdims=True) * (0.5 / D)
        return [dxv], [part, _sum0(d_out * n)]

    dx4, loss, d_fg = _tiled("head", head, (n_t,), [_rows(x4), _rows(tgt)], [fg], [_orow(T, D, F32)],
                             [(1, 1), (1, D)])

    dx3, dw_in1, dw_out1, dm_mlp1 = _mlp_bwd("mlp1", dx4, mlp1, g11, m1[4], m1[5],
                                             w_m1["w_in"], w_m1["w_out"])
    dep = on_grads("mlp1", (dw_in1, dw_out1))
    d_y, d_g1c, d_bpw2 = _gate_bwd("conf", dx3, y_conf, m1[2], dep)
    tk = MM_TILE
    (dw_pw2,) = _mm("conf_dwpw2", s_conf, d_y, _TN, (D // tm, 1, T // tk),
                    pl.BlockSpec((tk, tm), lambda i, j, k: (k, i)), pl.BlockSpec((tk, D), lambda i, j, k: (k, 0)),
                    [(_sds((D, D), BF16), pl.BlockSpec((tm, D), lambda i, j, k: (i, 0)))], (tm, D))
    (ds,) = _mm("conf_ds", d_y, w_cf["conf_w_pw2"], _NT, (T // tm, 1, 1),
                pl.BlockSpec((tm, D), lambda i, j, k: (i, 0)), pl.BlockSpec((D, D), lambda i, j, k: (0, 0)),
                [(_sds((T, D), F32), pl.BlockSpec((tm, D), lambda i, j, k: (i, 0)))], (tm, D))

    def ln_silu_bwd(ids, t, v):
        dsv, zcv = t
        nh, rstd = _layernorm_parts(zcv)
        ln = nh * v[0] + v[1]
        sg = _sigmoid(ln)
        d_ln = dsv * (sg * (1.0 + ln * (1.0 - sg)))
        d_nh = d_ln * v[0]
        d_zc = rstd * (d_nh - jnp.mean(d_nh, axis=-1, keepdims=True)
                       - nh * jnp.mean(d_nh * nh, axis=-1, keepdims=True))
        return [d_zc], [_sum0(d_ln * nh), _sum0(d_ln)]

    d_zc, d_lng, d_lnb = _tiled("conf_ln_bwd", ln_silu_bwd, (n_t,), [_rows(ds), _rows(zc)], [ln_g, ln_b],
                                [_orow(T, D, F32)], [(1, D), (1, D)])
    d_zg = _dwconv("conf_conv_dx", d_zc, 0, conf["conv_w"][::-1], jnp.zeros((1, D), F32),
                   CONF_KW - 1 - CONF_KW // 2, conf_starts, D, CW_CONF)
    d_cw_conf = _dwconv_wgrad("conf_conv_dw", d_zc, zg, 0, CONF_KW, CONF_KW // 2, conf_starts, D, CW_CONF)

    def glu_bwd(ids, t, v):
        dz, pa, pb = t
        sg = _sigmoid(pb)
        d_a = dz * sg
        d_b = dz * pa * sg * (1.0 - sg)
        return [d_a, d_b], [_sum0(d_a), _sum0(d_b)]

    d_pre_a, d_pre_b, d_b1a, d_b1b = _tiled(
        "conf_glu_bwd", glu_bwd, (n_t,), [_rows(d_zg), _rows(pre, D, col=0), _rows(pre, D, col=1)], [],
        [_orow(T, D, BF16), _orow(T, D, BF16)], [(1, D), (1, D)])
    d_pre = jnp.concatenate([d_pre_a, d_pre_b], axis=1)
    (dw_pw1,) = _mm("conf_dwpw1", h1, d_pre, _TN, (D // tm, 4, T // tk),
                    pl.BlockSpec((tk, tm), lambda i, j, k: (k, i)),
                    pl.BlockSpec((tk, D // 2), lambda i, j, k: (k, j)),
                    [(_sds((4, D, D // 2), BF16), pl.BlockSpec((None, tm, D // 2), lambda i, j, k: (j, i, 0)))],
                    (tm, D // 2))
    dep = on_grads("conf", (dw_pw1, dw_pw2))
    (dh1,) = _mm("conf_dh", d_pre, w_cf["conf_w_pw1"], _NT, (T // tm, 1, 4),
                 pl.BlockSpec((tm, D // 2), lambda i, j, k: (i, k)),
                 pl.BlockSpec((None, D, D // 2), lambda i, j, k: (k, 0, 0)),
                 [(_sds((T, D), F32), pl.BlockSpec((tm, D), lambda i, j, k: (i, 0)))], (tm, D))
    dx2, d_sh1c, d_sc1c, d_g10 = _norm_bwd("conf", dx3, dh1, 0, x2, g10, m1[1], dep=dep)

    dx1, dw_in0, dw_out0, dm_mlp0 = _mlp_bwd("mlp0", dx2, mlp0, g01, m0[4], m0[5],
                                             w_m0["w_in"], w_m0["w_out"])
    dep = on_grads("mlp0", (dw_in0, dw_out0))
    d_orec, d_g1r, _ = _gate_bwd("rec", dx1, o_rec, m0[2], dep)
    (dw_rout,) = _mm("rec_dwout", m_rec, d_orec, _TN, (R // RH, 1, T // tk),
                     pl.BlockSpec((tk, RH), lambda i, j, k: (k, i)), pl.BlockSpec((tk, D), lambda i, j, k: (k, 0)),
                     [(_sds((R, D), BF16), pl.BlockSpec((RH, D), lambda i, j, k: (i, 0)))], (RH, D))
    (dm_rec,) = _mm("rec_dm", d_orec, w_rec["rec_w_out"], _NT, (T // tm, 1, 1),
                    pl.BlockSpec((tm, D), lambda i, j, k: (i, 0)), pl.BlockSpec((R, D), lambda i, j, k: (0, 0)),
                    [(_sds((T, R), F32), pl.BlockSpec((tm, R), lambda i, j, k: (i, 0)))], (tm, R))

    def rec_mid_bwd(ids, t, v):
        dmv, gp, yf, yr = t
        g, th = _gelu(gp)
        lat = ids[0] > 0
        d_gp = jnp.where(lat, dmv * (yf + yr) * _gelu_grad(gp, th), 0.0)
        dy = jnp.where(lat, dmv * g, 0.0)
        return [d_gp, dy], []

    d_gp, dy = _tiled("rec_mid_bwd", rec_mid_bwd, (N_SCAN,),
                      [_rows(dm_rec, off=-1, clamp_lo=True), _rows(a_in, R), _rows(y_f), _rows(y_r)], [],
                      [_orow(TA, R, BF16), _orow(TA, R, F32)])
    da_f, db_f, da_r, db_r = _scan_bwd(dy, a_f, y_f, hin_f, a_r, y_r, hin_r)
    d_gpre, d_u, d_gbias, d_lam = _tiled(
        "rg_bwd", _rg_bwd_fn, (TA // RG_TILE,), [_rows(a, tm=RG_TILE) for a in (u, da_f, db_f, da_r, db_r)],
        [wbd, gbias, lam], [_orow(TA, 2 * NQ, BF16, tm=RG_TILE), _orow(TA, R, F32, tm=RG_TILE)],
        [(1, 2 * NQ), (1, 2 * R)], vec_refs=True)
    tk_a = REC_TILE
    (d_wbd,) = _mm("rg_dw", u, d_gpre, _TN, (2, 2, TA // tk_a),
                   pl.BlockSpec((tk_a, RH), lambda i, j, k: (k, i)),
                   pl.BlockSpec((tk_a, NQ // 2), lambda i, j, k: (k, 2 * i + j)),
                   [(_sds((2, RH, NQ), F32), pl.BlockSpec((None, RH, NQ // 2), lambda i, j, k: (i, 0, j)))],
                   (RH, NQ // 2))
    d_p = _dwconv("rec_conv_dx", d_u, 0, rec["conv_w"][::-1], jnp.zeros((1, R), F32), REC_KW - 1 - 1,
                  rec_starts, R, CW_REC)
    d_cw_rec = _dwconv_wgrad("rec_conv_dw", d_u, a_in, R // CW_REC, REC_KW, 1, rec_starts, R, CW_REC)
    d_a = jnp.concatenate([d_gp, d_p.astype(BF16)], axis=1)
    (dw_rin,) = _mm("rec_dwin", hcat, d_a, _TN, (D // tm, 4, TA // tk_a),
                    pl.BlockSpec((tk_a, tm), lambda i, j, k: (k, i)), pl.BlockSpec((tk_a, RH), lambda i, j, k: (k, j)),
                    [(_sds((4, D, RH), BF16), pl.BlockSpec((None, tm, RH), lambda i, j, k: (j, i, 0)))], (tm, RH))
    dep = on_grads("rec", (dw_rin, dw_rout))
    (dhcat,) = _mm("rec_dh", d_a, w_rec["rec_w_in"], _NT, (TA // tm_a, 1, 4),
                   pl.BlockSpec((tm_a, RH), lambda i, j, k: (i, k)),
                   pl.BlockSpec((None, D, RH), lambda i, j, k: (k, 0, 0)),
                   [(_sds((TA, D), F32), pl.BlockSpec((tm_a, D), lambda i, j, k: (i, 0)))], (tm_a, D))
    dx0, d_sh1r, d_sc1r, d_g00 = _norm_bwd("rec", dx1, dhcat, 1, x0, g00, m0[1], dep=dep)
    d_csh, d_csc, d_g00c = _norm_bwd("ctx", None, dhcat, 0, ctx, g00, csc, with_dx=False)

    big = dict(rec_w_in=dw_rin, rec_w_out=dw_rout, conf_w_pw1=dw_pw1, conf_w_pw2=dw_pw2,
               mlp_w_in=(dw_in0, dw_in1), mlp_w_out=(dw_out0, dw_out1))
    d_wa, d_wx = _gate_block_grads(d_wbd)
    d_ba, d_bx = _gate_bias_grads(d_gbias)
    d_mod = jnp.concatenate([
        d_sh1r, d_sc1r, d_g1r, dm_mlp0["sh"], dm_mlp0["sc"], dm_mlp0["gate"],
        d_sh1c, d_sc1c, d_g1c, dm_mlp1["sh"], dm_mlp1["sc"], dm_mlp1["gate"]], axis=1).reshape(2, 6 * D)
    small = dict(
        d_mod=d_mod, d_cmod=jnp.concatenate([d_csh, d_csc], axis=1),
        norm_g=jnp.concatenate([d_g00 + d_g00c, dm_mlp0["g_norm"], d_g10, dm_mlp1["g_norm"]], axis=1),
        rec_conv_w=d_cw_rec[:REC_KW], rec_conv_b=d_cw_rec[REC_KW], rec_lambda=d_lam.reshape(2, R),
        rec_w_a=d_wa, rec_b_a=d_ba, rec_w_x=d_wx, rec_b_x=d_bx,
        conf_b_pw1=jnp.concatenate([d_b1a, d_b1b], axis=1), conf_conv_w=d_cw_conf[:CONF_KW],
        conf_conv_b=d_cw_conf[CONF_KW], conf_ln_g=d_lng, conf_ln_b=d_lnb, conf_b_pw2=d_bpw2, final_g=d_fg)
    return loss.reshape(()), dx0, big, small


_BIG = ("rec_w_in", "rec_w_out", "conf_w_pw1", "conf_w_pw2", "mlp_w_in", "mlp_w_out")


def _halves(w):
    return w.reshape(2, w.shape[0] // 2, w.shape[1])


def _ada_fwd(c16, w_ada, b_shard):
    ns = w_ada.shape[2]
    tn = 512

    def kern(c_ref, w_ref, b_ref, o_ref):
        cv = c_ref[...]
        s = (cv * _sigmoid(cv)).astype(BF16)
        o_ref[...] = jnp.dot(s, w_ref[...].astype(BF16), preferred_element_type=F32) + b_ref[...]

    return _pcall(
        kern, name="ada_fwd", grid=(2, ns // tn),
        in_specs=[pl.BlockSpec((16, D), lambda l, j: (0, 0)), pl.BlockSpec((None, D, tn), lambda l, j: (l, 0, j)),
                  pl.BlockSpec((None, 1, tn), lambda l, j: (l, 0, j))],
        out_specs=pl.BlockSpec((None, 16, tn), lambda l, j: (l, 0, j)),
        out_shape=_sds((2, 16, ns), F32), compiler_params=_cparams(),
    )(c16, w_ada, b_shard)


def _ada_bwd(c16, dm16, w_ada):
    ns = w_ada.shape[2]
    tn = 512

    def kern(c_ref, dm_ref, w_ref, gw_ref, ds_ref):
        cv = c_ref[...]
        s = (cv * _sigmoid(cv)).astype(BF16)
        dm = dm_ref[...].astype(BF16)
        gw_ref[...] = lax.dot_general(s, dm, _TN, preferred_element_type=F32)

        @pl.when(jnp.logical_and(pl.program_id(0) == 0, pl.program_id(1) == 0))
        def _():
            ds_ref[...] = jnp.zeros_like(ds_ref)

        ds_ref[...] += lax.dot_general(dm, w_ref[...].astype(BF16), _NT, preferred_element_type=F32)

    return _pcall(
        kern, name="ada_bwd", grid=(2, ns // tn),
        in_specs=[pl.BlockSpec((16, D), lambda l, j: (0, 0)), pl.BlockSpec((None, 16, tn), lambda l, j: (l, 0, j)),
                  pl.BlockSpec((None, D, tn), lambda l, j: (l, 0, j))],
        out_specs=[pl.BlockSpec((None, D, tn), lambda l, j: (l, 0, j)), pl.BlockSpec((16, D), lambda l, j: (0, 0))],
        out_shape=[_sds((2, D, ns), F32), _sds((16, D), F32)], compiler_params=_cparams(),
    )(c16, dm16, w_ada)


def _cctx_grad(ds8, c_ctx):
    def kern(d_ref, c_ref, o_ref):
        tot = d_ref[0, 8:9, :] + d_ref[2, 8:9, :] + d_ref[4, 8:9, :] + d_ref[6, 8:9, :]
        cv = c_ref[...]
        sg = _sigmoid(cv)
        o_ref[...] = tot * (sg * (1.0 + cv * (1.0 - sg)))

    return _pcall(kern, name="cctx_grad", out_shape=_sds((1, D), F32))(ds8, c_ctx.reshape(1, D))


def kernel(x, c, ctx, c_ctx, w_ada, b_ada, norm_g, rec_w_in, rec_conv_w, rec_conv_b, rec_lambda, rec_w_a, rec_b_a, rec_w_x, rec_b_x, rec_w_out, conf_w_pw1, conf_b_pw1, conf_conv_w, conf_conv_b, conf_ln_g, conf_ln_b, conf_w_pw2, conf_b_pw2, mlp_w_in, mlp_w_out, final_g, loss_target, m_c_ctx, m_w_ada, m_b_ada, m_norm_g, m_rec_w_in, m_rec_conv_w, m_rec_conv_b, m_rec_lambda, m_rec_w_a, m_rec_b_a, m_rec_w_x, m_rec_b_x, m_rec_w_out, m_conf_w_pw1, m_conf_b_pw1, m_conf_conv_w, m_conf_conv_b, m_conf_ln_g, m_conf_ln_b, m_conf_w_pw2, m_conf_b_pw2, m_mlp_w_in, m_mlp_w_out, m_final_g, v_c_ctx, v_w_ada, v_b_ada, v_norm_g, v_rec_w_in, v_rec_conv_w, v_rec_conv_b, v_rec_lambda, v_rec_w_a, v_rec_b_a, v_rec_w_x, v_rec_b_x, v_rec_w_out, v_conf_w_pw1, v_conf_b_pw1, v_conf_conv_w, v_conf_conv_b, v_conf_ln_g, v_conf_ln_b, v_conf_w_pw2, v_conf_b_pw2, v_mlp_w_in, v_mlp_w_out, v_final_g):
    names = ["c_ctx", "w_ada", "b_ada", "norm_g", "rec_w_in", "rec_conv_w", "rec_conv_b", "rec_lambda", "rec_w_a",
             "rec_b_a", "rec_w_x", "rec_b_x", "rec_w_out", "conf_w_pw1", "conf_b_pw1", "conf_conv_w", "conf_conv_b",
             "conf_ln_g", "conf_ln_b", "conf_w_pw2", "conf_b_pw2", "mlp_w_in", "mlp_w_out", "final_g"]
    w = dict(zip(names, [c_ctx, w_ada, b_ada, norm_g, rec_w_in, rec_conv_w, rec_conv_b, rec_lambda, rec_w_a,
                         rec_b_a, rec_w_x, rec_b_x, rec_w_out, conf_w_pw1, conf_b_pw1, conf_conv_w, conf_conv_b,
                         conf_ln_g, conf_ln_b, conf_w_pw2, conf_b_pw2, mlp_w_in, mlp_w_out, final_g]))
    m = dict(zip(names, [m_c_ctx, m_w_ada, m_b_ada, m_norm_g, m_rec_w_in, m_rec_conv_w, m_rec_conv_b, m_rec_lambda,
                         m_rec_w_a, m_rec_b_a, m_rec_w_x, m_rec_b_x, m_rec_w_out, m_conf_w_pw1, m_conf_b_pw1,
                         m_conf_conv_w, m_conf_conv_b, m_conf_ln_g, m_conf_ln_b, m_conf_w_pw2, m_conf_b_pw2,
                         m_mlp_w_in, m_mlp_w_out, m_final_g]))
    v = dict(zip(names, [v_c_ctx, v_w_ada, v_b_ada, v_norm_g, v_rec_w_in, v_rec_conv_w, v_rec_conv_b, v_rec_lambda,
                         v_rec_w_a, v_rec_b_a, v_rec_w_x, v_rec_b_x, v_rec_w_out, v_conf_w_pw1, v_conf_b_pw1,
                         v_conf_conv_w, v_conf_conv_b, v_conf_ln_g, v_conf_ln_b, v_conf_w_pw2, v_conf_b_pw2,
                         v_mlp_w_in, v_mlp_w_out, v_final_g]))
    mx, my, mc = _me()
    chip = 2 * mx + my
    me = 4 * mx + 2 * my + mc

    sharded_small = ["norm_g", "rec_conv_w", "rec_lambda", "conf_b_pw1", "conf_conv_w", "conf_conv_b", "conf_ln_g",
                     "conf_ln_b", "conf_b_pw2"]
    packed, offs = _pack([c] + [w[k] for k in sharded_small], 8)
    got = _allgather8("gather_small", packed)

    place = jnp.stack([chip, mc]).astype(jnp.int32)
    shards = [_halves(rec_w_in[0]), _halves(rec_w_out[0]), _halves(conf_w_pw1[0]), _halves(conf_w_pw2[0]),
              _halves(mlp_w_in[0]), _halves(mlp_w_in[1]), _halves(mlp_w_out[0]), _halves(mlp_w_out[1])]
    use_order = dict(rec=(0, 1), mlp0=(4, 6), conf=(2, 3), mlp1=(5, 7))
    slots = _place_big(shards, place)
    flying, gsems = {}, {}
    fly, sems, rec_started = _gather_start("gather_start_rec", [slots[t] for t in use_order["rec"]], ((0, 1),), got)
    flying["rec"], gsems["rec"] = fly, sems

    def wg(group, after):
        bufs = _gather_wait(f"gather_wait_{group}", flying[group], *gsems[group], after)
        a, b = _swap_halves(f"swap_{group}", bufs)
        if group == "rec":
            return dict(rec_w_in=a.reshape(4, D, RH), rec_w_out=b.reshape(R, D))
        if group == "conf":
            return dict(conf_w_pw1=a.reshape(4, D, D // 2), conf_w_pw2=b.reshape(D, D))
        return dict(w_in=a.reshape(4, D, D), w_out=b.reshape(FF, D))

    per_dev =[_unpack(got[d], offs) for d in range(8)]
    c_rows = jnp.concatenate([per_dev[d][0].reshape(1, D) for d in range(8)], axis=0)
    full = {k: jnp.concatenate([per_dev[2 * j][1 + i] for j in range(4)], axis=-1)
            for i, k in enumerate(sharded_small)}
    c16 = jnp.concatenate([c_rows, c_ctx.reshape(1, D), jnp.zeros((7, D), F32)], axis=0)

    ns = w_ada.shape[2]
    b_shard = lax.dynamic_slice_in_dim(b_ada, chip * ns, ns, axis=1).reshape(2, 1, ns)
    prod = _ada_fwd(c16, w_ada, b_shard)
    prod8 = _allgather8("gather_mod", prod.reshape(32, ns), rec_started)
    later = ("mlp0", "conf", "mlp1")
    fly, sems, all_started = _gather_start("gather_start_rest", [slots[t] for g in later for t in use_order[g]],
                                           ((0, 1), (2, 3), (4, 5)), prod8)
    for gi, g in enumerate(later):
        flying[g], gsems[g] = fly[2 * gi:2 * gi + 2], sems[2 * gi:2 * gi + 2]
    prod8 = prod8.reshape(8, 2, 16, ns)
    mod_all = jnp.concatenate([prod8[2 * j] for j in range(4)], axis=-1)
    mods = lax.dynamic_index_in_dim(mod_all, me, axis=1, keepdims=False).reshape(2, 6, D)
    cmods = mod_all[0, 8].reshape(6, D)[:2]

    rec = dict(conv_w=full["rec_conv_w"][0], conv_b=rec_conv_b[0], lam=full["rec_lambda"][0],
               w_a=rec_w_a[0], b_a=rec_b_a[0], w_x=rec_w_x[0], b_x=rec_b_x[0])
    conf = dict(b_pw1=full["conf_b_pw1"][0], conv_w=full["conf_conv_w"][0], conv_b=full["conf_conv_b"][0],
                ln_g=full["conf_ln_g"][0], ln_b=full["conf_ln_b"][0], b_pw2=full["conf_b_pw2"][0])
    sent = {}

    def on_grads(group, dws):
        parts = [dw.reshape(4, 2, shards[t].shape[1], shards[t].shape[2]) for dw, t in zip(dws, use_order[group])]
        sent[group], token = _reduce_begin(group, parts, place)
        return token

    loss_local, grad_x, _, small = _local_step(x[0], ctx[0], loss_target[0], mods, cmods, full["norm_g"], final_g,
                                               rec, conf, wg, on_grads, all_started)
    loss = lax.psum(loss_local, ("x", "y", "c"))

    small_names = ["d_mod", "d_cmod", "norm_g", "rec_conv_w", "rec_conv_b", "rec_lambda", "rec_w_a", "rec_b_a",
                   "rec_w_x", "rec_b_x", "conf_b_pw1", "conf_conv_w", "conf_conv_b", "conf_ln_g", "conf_ln_b",
                   "conf_b_pw2", "final_g"]
    mine = lax.broadcasted_iota(jnp.int32, (8, 1), 0) == me
    mod_slots = jnp.where(mine, small["d_mod"].reshape(1, -1), 0.0)
    spacked, soffs = _pack([small[k] for k in small_names] + [mod_slots])
    stotal = _allreduce_small(spacked, place)
    unpacked = _unpack(stotal, soffs)
    ssum = dict(zip(small_names, unpacked[:-1]))
    dmod_rows = unpacked[-1].reshape(8, 2, 6 * D).transpose(1, 0, 2)

    fulls = {}
    for group in ("mlp1", "conf", "mlp0", "rec"):
        for t, f in zip(use_order[group], _reduce_end(group, sent[group], place, stotal)):
            fulls[t] = f
    whole = _share_halves("share_grads", [fulls[t] for t in range(8)])
    g_big = dict(rec_w_in=whole[0].reshape(rec_w_in.shape), rec_w_out=whole[1].reshape(rec_w_out.shape),
                 conf_w_pw1=whole[2].reshape(conf_w_pw1.shape), conf_w_pw2=whole[3].reshape(conf_w_pw2.shape),
                 mlp_w_in=jnp.stack([whole[4].reshape(D, D), whole[5].reshape(D, D)]),
                 mlp_w_out=jnp.stack([whole[6].reshape(D, D), whole[7].reshape(D, D)]))

    d_cmod_full =jnp.concatenate([ssum["d_cmod"].reshape(1, 2 * D), jnp.zeros((1, 4 * D), F32)], axis=1)
    dm16 = jnp.concatenate([dmod_rows, jnp.stack([d_cmod_full, jnp.zeros((1, 6 * D), F32)]),
                            jnp.zeros((2, 7, 6 * D), F32)], axis=1)
    dm16_shard = lax.dynamic_slice_in_dim(dm16, chip * ns, ns, axis=2)
    g_w_ada, ds_part = _ada_bwd(c16, dm16_shard, w_ada)
    ds8 = _allgather8("gather_dsilu", ds_part)
    g_c_ctx = _cctx_grad(ds8, c_ctx).reshape(D)
    g_b_ada = ssum["d_mod"] + jnp.stack([d_cmod_full[0], jnp.zeros((6 * D,), F32)])

    def shard_of(a, axis):
        n = a.shape[axis] // 4
        return lax.dynamic_slice_in_dim(a, chip * n, n, axis=axis)

    grads = dict(
        c_ctx=g_c_ctx, w_ada=g_w_ada, b_ada=g_b_ada,
        norm_g=shard_of(ssum["norm_g"].reshape(2, 2, D), 2),
        rec_w_in=g_big["rec_w_in"], rec_conv_w=shard_of(ssum["rec_conv_w"].reshape(1, REC_KW, R), 2),
        rec_conv_b=ssum["rec_conv_b"].reshape(1, R), rec_lambda=shard_of(ssum["rec_lambda"].reshape(1, 2, R), 2),
        rec_w_a=ssum["rec_w_a"].reshape(rec_w_a.shape), rec_b_a=ssum["rec_b_a"].reshape(rec_b_a.shape),
        rec_w_x=ssum["rec_w_x"].reshape(rec_w_x.shape), rec_b_x=ssum["rec_b_x"].reshape(rec_b_x.shape),
        rec_w_out=g_big["rec_w_out"], conf_w_pw1=g_big["conf_w_pw1"],
        conf_b_pw1=shard_of(ssum["conf_b_pw1"].reshape(1, 2 * D), 1),
        conf_conv_w=shard_of(ssum["conf_conv_w"].reshape(1, CONF_KW, D), 2),
        conf_conv_b=shard_of(ssum["conf_conv_b"].reshape(1, D), 1),
        conf_ln_g=shard_of(ssum["conf_ln_g"].reshape(1, D), 1), conf_ln_b=shard_of(ssum["conf_ln_b"].reshape(1, D), 1),
        conf_w_pw2=g_big["conf_w_pw2"], conf_b_pw2=shard_of(ssum["conf_b_pw2"].reshape(1, D), 1),
        mlp_w_in=g_big["mlp_w_in"], mlp_w_out=g_big["mlp_w_out"], final_g=ssum["final_g"].reshape(D))

    delta, new_m, new_v = {}, {}, {}
    big_names = ("w_ada",) + _BIG
    for k in big_names:
        cols = w[k].shape[-1]
        d_, m_, v_ = _adamw(f"adamw_{k}", w[k].reshape(-1, cols), grads[k].reshape(-1, cols),
                            m[k].reshape(-1, cols), v[k].reshape(-1, cols))
        delta[k], new_m[k], new_v[k] = (a.reshape(w[k].shape) for a in (d_, m_, v_))
    rest = [k for k in names if k not in big_names]
    pw, poffs = _pack([w[k] for k in rest])
    pg, _ = _pack([grads[k] for k in rest])
    pm, _ = _pack([m[k] for k in rest])
    pv, _ = _pack([v[k] for k in rest])
    d_, m_, v_ = _adamw("adamw_small", pw, pg, pm, pv)
    for k, dd, mm, vv in zip(rest, _unpack(d_, poffs), _unpack(m_, poffs), _unpack(v_, poffs)):
        delta[k], new_m[k], new_v[k] = dd, mm, vv

    return (loss, grad_x[None], *[grads[k] for k in names], *[delta[k] for k in names],
            *[new_m[k] for k in names], *[new_v[k] for k in names])
```

```python
import functools
import math

import jax
import jax.numpy as jnp
from jax import lax
from jax.experimental import pallas as pl
from jax.experimental.pallas import tpu as pltpu

F32 = jnp.float32
BF16 = jnp.bfloat16

D = 1024
T = 2048
TC = 256
TA = T + TC
R = 1280
RH = R // 2
NQ = 4 * RH
FF = 4096
N_BLK = 16
BLK = R // N_BLK
GRID_W = 64
EPS = 1e-6
RG_C = 8.0
CONF_KW = 31
REC_KW = 4
LANE = 128
ROW_TILE = 256
HALO = 16
RG_TILE = 128
PACK_ROWS = 512
MM_TILE = 1024
REC_TILE = TA // 2
CW_REC = 640
CW_CONF = 512
V7X_VMEM_BYTES = 64 * 1024 * 1024
VMEM_LIMIT = V7X_VMEM_BYTES - 8 * 1024 * 1024

ADAM_LR = 0.001
ADAM_B1 = 0.9
ADAM_B2 = 0.999
ADAM_EPS = 1e-08
ADAM_WD = 0.01
ADAM_STEP = 10

MESH = pl.DeviceIdType.MESH
ANY = pl.BlockSpec(memory_space=pl.ANY)


def _sds(shape, dtype):
    return jax.ShapeDtypeStruct(tuple(shape), dtype)


def _pcall(body, **kw):
    return pl.pallas_call(body, **kw)


def _cparams():
    return pltpu.CompilerParams(vmem_limit_bytes=VMEM_LIMIT)


def _full_spec(arr):
    nd = arr.ndim
    return pl.BlockSpec(arr.shape, lambda *ids, _n=nd: (0,) * _n)


def _sum0(v):
    return jnp.sum(v, axis=0, keepdims=True)


def _tiled(name, fn, grid, ins, vecs, outs, vec_outs=(), vec_refs=False):
    n_in, n_vec, n_out = len(ins), len(vecs), len(outs)
    n_grid = len(grid)

    def kern(*refs):
        ids = [pl.program_id(a) for a in range(n_grid)]
        tin = [r[...] for r in refs[:n_in]]
        vin = list(refs[n_in:n_in + n_vec]) if vec_refs else [r[...] for r in refs[n_in:n_in + n_vec]]
        o_refs = refs[n_in + n_vec:n_in + n_vec + n_out]
        a_refs = refs[n_in + n_vec + n_out:]
        tout, incs = fn(ids, tin, vin)
        for r, v in zip(o_refs, tout):
            r[...] = v.astype(r.dtype)
        if a_refs:
            first = functools.reduce(jnp.logical_and, [i == 0 for i in ids])

            @pl.when(first)
            def _():
                for r in a_refs:
                    r[...] = jnp.zeros_like(r)

            for r, v in zip(a_refs, incs):
                r[...] += v

    out_shape = [o for o, _ in outs] + [_sds(s, F32) for s in vec_outs]
    out_specs = [s for _, s in outs] + [
        pl.BlockSpec(tuple(s), lambda *ids, _n=len(s): (0,) * _n) for s in vec_outs]
    res = _pcall(
        kern, name=name, grid=tuple(grid),
        in_specs=[s for _, s in ins] + [_full_spec(v) for v in vecs],
        out_specs=out_specs, out_shape=out_shape, compiler_params=_cparams(),
    )(*[a for a, _ in ins], *vecs)
    return list(res)


def _rows(arr, ncols=None, tm=ROW_TILE, off=0, col=0, clamp_lo=False):
    ncols = arr.shape[1] if ncols is None else ncols
    if clamp_lo:
        return arr, pl.BlockSpec((tm, ncols), lambda i: (jnp.maximum(i + off, 0), col))
    return arr, pl.BlockSpec((tm, ncols), lambda i: (i + off, col))


def _orow(nrows, ncols, dtype, tm=ROW_TILE, off=0, clamp_lo=False):
    if clamp_lo:
        return _sds((nrows, ncols), dtype), pl.BlockSpec((tm, ncols), lambda i: (jnp.maximum(i + off, 0), 0))
    return _sds((nrows, ncols), dtype), pl.BlockSpec((tm, ncols), lambda i: (i + off, 0))


_NN = (((1,), (0,)), ((), ()))
_TN = (((0,), (0,)), ((), ()))
_NT = (((1,), (1,)), ((), ()))


def _mm(name, a, b, dims, grid, a_spec, b_spec, out, acc_shape, extra=(), a_pre=None, epi=None):
    n_k = grid[2]
    n_ex = len(extra)

    def kern(a_ref, b_ref, *rest):
        ex = rest[:n_ex]
        o_refs = rest[n_ex:n_ex + len(out)]
        k = pl.program_id(2)
        av = a_ref[...]
        if a_pre is not None:
            av = a_pre(av)
        part = lax.dot_general(av.astype(BF16), b_ref[...].astype(BF16), dims, preferred_element_type=F32)

        def finish(total):
            vals = [total] if epi is None else epi(total, [e[...] for e in ex])
            for r, v in zip(o_refs, vals):
                r[...] = v.astype(r.dtype)

        if n_k == 1:
            finish(part)
        else:
            acc = rest[-1]

            @pl.when(k == 0)
            def _():
                acc[...] = part

            @pl.when(jnp.logical_and(k > 0, k < n_k - 1))
            def _():
                acc[...] += part

            @pl.when(k == n_k - 1)
            def _():
                finish(acc[...] + part)

    res = _pcall(
        kern, name=name, grid=tuple(grid),
        in_specs=[a_spec, b_spec] + [s for _, s in extra],
        out_specs=[s for _, s in out], out_shape=[o for o, _ in out],
        scratch_shapes=[] if n_k == 1 else [pltpu.VMEM(tuple(acc_shape), F32)], compiler_params=_cparams(),
    )(a, b, *[e for e, _ in extra])
    return list(res)


def _rms(x):
    r = lax.rsqrt(jnp.mean(x * x, axis=-1, keepdims=True) + EPS)
    return x * r, r


def _norm_mod(x, g, sc, sh):
    n, _ = _rms(x)
    return (n * g) * (1.0 + sc) + sh


def _norm_mod_bwd(dh, x, g, sc):
    n, r = _rms(x)
    d_sh = _sum0(dh)
    d_sc = _sum0(dh * (n * g))
    d_g = _sum0(dh * (1.0 + sc) * n)
    dn = dh * (g * (1.0 + sc))
    dx = r * (dn - n * jnp.mean(dn * n, axis=-1, keepdims=True))
    return dx, d_sh, d_sc, d_g


_GELU_K = math.sqrt(2.0 / math.pi)


def _gelu(x):
    t = jnp.tanh(_GELU_K * (x + 0.044715 * x * x * x))
    return 0.5 * x * (1.0 + t), t


def _gelu_grad(x, t):
    return 0.5 * (1.0 + t) + 0.5 * x * (1.0 - t * t) * (_GELU_K * (1.0 + 3.0 * 0.044715 * x * x))


def _sigmoid(x):
    return 0.5 * jnp.tanh(0.5 * x) + 0.5


def _expm1(x):
    p = jnp.full_like(x, 1.0 / 5040.0)
    for c in (1.0 / 720.0, 1.0 / 120.0, 1.0 / 24.0, 1.0 / 6.0, 0.5, 1.0):
        p = p * x + c
    return jnp.where(jnp.abs(x) < 0.3, x * p, jnp.exp(x) - 1.0)


def _softplus_neg(lam):
    return jnp.log1p(jnp.exp(-jnp.abs(lam))) + jnp.maximum(-lam, 0.0)


def _layernorm_parts(x):
    mu = jnp.mean(x, axis=-1, keepdims=True)
    xc = x - mu
    rstd = lax.rsqrt(jnp.mean(xc * xc, axis=-1, keepdims=True) + EPS)
    return xc * rstd, rstd


def _rg_gates(u, wbd, gbias, lam):
    sp = _softplus_neg(lam)
    parts = {}
    for h in range(2):
        uh = u[:, h * RH:(h + 1) * RH]
        g = jnp.dot(uh.astype(BF16), wbd[h], preferred_element_type=F32) + gbias[:, h * NQ:(h + 1) * NQ]
        for d in range(2):
            r = _sigmoid(g[:, (2 * d) * RH:(2 * d + 1) * RH])
            i = _sigmoid(g[:, (2 * d + 1) * RH:(2 * d + 2) * RH])
            sph = sp[d:d + 1, h * RH:(h + 1) * RH]
            la = (-RG_C) * r * sph
            e2 = _expm1(2.0 * la)
            inv_mult = jnp.where(e2 < 0.0, lax.rsqrt(-e2), 0.0)
            parts[(d, h)] = dict(r=r, i=i, la=la, a=jnp.exp(la), e2=e2, mult=-e2 * inv_mult, inv_mult=inv_mult,
                                 uh=uh, sp=sph)
    return parts


def _rg_fwd_fn(ids, tin, vin):
    (u,) = tin
    wbd = vin[0]
    parts = _rg_gates(u, wbd, vin[1][...], vin[2][...])
    outs = []
    for d in range(2):
        a = jnp.concatenate([parts[(d, h)]["a"] for h in range(2)], axis=1)
        b = jnp.concatenate([parts[(d, h)]["mult"] * parts[(d, h)]["i"] * parts[(d, h)]["uh"]
                             for h in range(2)], axis=1)
        outs += [a, b]
    return outs, []


def _rg_bwd_fn(ids, tin, vin):
    u, da_f, db_f, da_r, db_r = tin
    wbd, lam = vin[0], vin[2][...]
    parts = _rg_gates(u, wbd, vin[1][...], lam)
    dab = ((da_f, db_f), (da_r, db_r))
    dsig_lam = -1.0 / (1.0 + jnp.exp(lam))
    du_halves, dpre_halves, dlam = [], [], [[None, None], [None, None]]
    for h in range(2):
        du = jnp.zeros_like(parts[(0, h)]["uh"])
        dpre = []
        for d in range(2):
            p = parts[(d, h)]
            da = dab[d][0][:, h * RH:(h + 1) * RH]
            db = dab[d][1][:, h * RH:(h + 1) * RH]
            d_mult = db * p["i"] * p["uh"]
            d_i = db * p["mult"] * p["uh"]
            du = du + db * p["mult"] * p["i"]
            d_la = da * p["a"] - d_mult * (p["e2"] + 1.0) * p["inv_mult"]
            d_r = d_la * ((-RG_C) * p["sp"])
            dlam[d][h] = _sum0(d_la * ((-RG_C) * p["r"])) * dsig_lam[d:d + 1, h * RH:(h + 1) * RH]
            dpre += [d_r * p["r"] * (1.0 - p["r"]), d_i * p["i"] * (1.0 - p["i"])]
        dpre = jnp.concatenate(dpre, axis=1)
        du = du + lax.dot_general(dpre.astype(BF16), wbd[h], _NT, preferred_element_type=F32)
        du_halves.append(du)
        dpre_halves.append(dpre)
    dpre_all = jnp.concatenate(dpre_halves, axis=1)
    dlam_row = jnp.concatenate([dlam[0][0], dlam[0][1], dlam[1][0], dlam[1][1]], axis=1)
    return [dpre_all, jnp.concatenate(du_halves, axis=1)], [_sum0(dpre_all), dlam_row]


def _tile_flags(i, n_tiles, seq_starts):
    starts_here = functools.reduce(jnp.logical_or, [i == s for s in seq_starts])
    ends_here = functools.reduce(jnp.logical_or, [i + 1 == s for s in seq_starts] + [i + 1 == n_tiles])
    return jnp.logical_not(starts_here), jnp.logical_not(ends_here)


def _halo_specs(col0, cw):
    hb = ROW_TILE // HALO
    prev = pl.BlockSpec((HALO, cw), lambda i, c: (jnp.maximum(i * hb - 1, 0), col0 + c))
    cur = pl.BlockSpec((ROW_TILE, cw), lambda i, c: (i, col0 + c))
    return prev, cur, hb


def _window(prev_ref, cur_ref, next_ref, has_prev, has_next):
    prev = jnp.where(has_prev, prev_ref[...], 0.0)
    nxt = jnp.where(has_next, next_ref[...], 0.0)
    return jnp.concatenate([prev, cur_ref[...], nxt], axis=0)


def _tap_reader(win):
    sub = 8
    n = win.shape[0]
    shifted = {0: win}

    def tap(off):
        s = off % sub
        if s not in shifted:
            shifted[s] = pltpu.roll(win, n - s, axis=0)
        return shifted[s][off - s:off - s + ROW_TILE, :]

    return tap


def _dwconv(name, x, col0, w, bias, pad_left, seq_starts, n_ch, cw=256):
    n_rows = x.shape[0]
    n_tiles = n_rows // ROW_TILE
    n_taps = w.shape[0]
    prev_spec, cur_spec, hb = _halo_specs(col0, cw)
    last_hb = n_rows // HALO - 1
    next_spec = pl.BlockSpec((HALO, cw), lambda i, c: (jnp.minimum((i + 1) * hb, last_hb), col0 + c))

    def kern(prev_ref, cur_ref, next_ref, w_ref, b_ref, o_ref):
        has_prev, has_next = _tile_flags(pl.program_id(0), n_tiles, seq_starts)
        win = _window(prev_ref, cur_ref, next_ref, has_prev, has_next)
        tap = _tap_reader(win)
        wv = w_ref[...]
        acc = jnp.zeros((ROW_TILE, cw), F32) + b_ref[...]
        for k in range(n_taps):
            acc = acc + wv[k:k + 1, :] * tap(HALO + k - pad_left)
        o_ref[...] = acc

    return _pcall(
        kern, name=name, grid=(n_tiles, n_ch // cw),
        in_specs=[prev_spec, cur_spec, next_spec,
                  pl.BlockSpec((n_taps, cw), lambda i, c: (0, c)), pl.BlockSpec((1, cw), lambda i, c: (0, c))],
        out_specs=pl.BlockSpec((ROW_TILE, cw), lambda i, c: (i, c)),
        out_shape=_sds((n_rows, n_ch), F32), compiler_params=_cparams(),
    )(x, x, x, w, bias)


def _dwconv_wgrad(name, dy, x, col0, n_taps, pad_left, seq_starts, n_ch, cw=256):
    n_rows = dy.shape[0]
    n_tiles = n_rows // ROW_TILE
    n_out = -(-(n_taps + 1) // 8) * 8
    prev_spec, cur_spec, hb = _halo_specs(col0, cw)
    last_hb = n_rows // HALO - 1
    next_spec = pl.BlockSpec((HALO, cw), lambda c, i: (jnp.minimum((i + 1) * hb, last_hb), col0 + c))
    prev_spec = pl.BlockSpec((HALO, cw), lambda c, i: (jnp.maximum(i * hb - 1, 0), col0 + c))
    cur_spec = pl.BlockSpec((ROW_TILE, cw), lambda c, i: (i, col0 + c))

    def kern(dy_ref, prev_ref, cur_ref, next_ref, o_ref):
        i = pl.program_id(1)
        has_prev, has_next = _tile_flags(i, n_tiles, seq_starts)
        win = _window(prev_ref, cur_ref, next_ref, has_prev, has_next)
        dyv = dy_ref[...]
        tap = _tap_reader(win)
        rid = lax.broadcasted_iota(jnp.int32, (n_out, cw), 0)
        inc = jnp.where(rid == n_taps, _sum0(dyv), 0.0)
        for k in range(n_taps):
            inc = inc + jnp.where(rid == k, _sum0(dyv * tap(HALO + k - pad_left)), 0.0)

        @pl.when(i == 0)
        def _():
            o_ref[...] = jnp.zeros_like(o_ref)

        o_ref[...] += inc

    return _pcall(
        kern, name=name, grid=(n_ch // cw, n_tiles),
        in_specs=[pl.BlockSpec((ROW_TILE, cw), lambda c, i: (i, c)), prev_spec, cur_spec, next_spec],
        out_specs=pl.BlockSpec((n_out, cw), lambda c, i: (0, c)),
        out_shape=_sds((n_out, n_ch), F32), compiler_params=_cparams(),
    )(dy, x, x, x)


N_SCAN = TA // ROW_TILE


def _rev_block(j):
    return jnp.where(j == 0, 0, N_SCAN - j)


def _scan_fwd(a_f, b_f, a_r, b_r):
    fwd_spec = pl.BlockSpec((ROW_TILE, R), lambda i: (i, 0))
    rev_spec = pl.BlockSpec((ROW_TILE, R), lambda i: (_rev_block(i), 0))
    hin_spec = pl.BlockSpec((None, 1, R), lambda i: (i, 0, 0))

    def kern(af, bf, ar, br, yf, yr, hin_f, hin_r, hf_s, hr_s):
        @pl.when(pl.program_id(0) == 0)
        def _():
            hf_s[...] = jnp.zeros_like(hf_s)
            hr_s[...] = jnp.zeros_like(hr_s)

        hin_f[...] = hf_s[...]
        hin_r[...] = hr_s[...]

        def step(s8, carry):
            hf, hr = carry
            t0 = pl.multiple_of(s8 * 8, 8)
            for q in range(8):
                tf = t0 + q
                hf = af[pl.ds(tf, 1), :] * hf + bf[pl.ds(tf, 1), :]
                yf[pl.ds(tf, 1), :] = hf
                tr = ROW_TILE - 1 - tf
                hr = ar[pl.ds(tr, 1), :] * hr + br[pl.ds(tr, 1), :]
                yr[pl.ds(tr, 1), :] = hr
            return hf, hr

        hf, hr = lax.fori_loop(0, ROW_TILE // 8, step, (hf_s[...], hr_s[...]))
        hf_s[...] = hf
        hr_s[...] = hr

    return _pcall(
        kern, name="scan_fwd", grid=(N_SCAN,),
        in_specs=[fwd_spec, fwd_spec, rev_spec, rev_spec],
        out_specs=[fwd_spec, rev_spec, hin_spec, hin_spec],
        out_shape=[_sds((TA, R), F32), _sds((TA, R), F32), _sds((N_SCAN, 1, R), F32), _sds((N_SCAN, 1, R), F32)],
        scratch_shapes=[pltpu.VMEM((1, R), F32), pltpu.VMEM((1, R), F32)], compiler_params=_cparams(),
    )(a_f, b_f, a_r, b_r)


def _scan_bwd(dy, a_f, y_f, hin_f, a_r, y_r, hin_r):
    fwd_spec = pl.BlockSpec((ROW_TILE, R), lambda i: (N_SCAN - 1 - i, 0))
    rev_spec = pl.BlockSpec((ROW_TILE, R), lambda i: (_rev_block(N_SCAN - 1 - i), 0))
    hin_spec = pl.BlockSpec((None, 1, R), lambda i: (N_SCAN - 1 - i, 0, 0))
    last = ROW_TILE - 1

    def kern(dyf, af, yf, hf0, dyr, ar, yr, hr0, daf, dbf, dar, dbr, gf_s, anf_s, gr_s, anr_s):
        @pl.when(pl.program_id(0) == 0)
        def _():
            for r in (gf_s, anf_s, gr_s, anr_s):
                r[...] = jnp.zeros_like(r)

        def one(dy_ref, a_ref, y_ref, da_ref, db_ref, g, an, p, pprev):
            gnew = dy_ref[pl.ds(p, 1), :] + an * g
            db_ref[pl.ds(p, 1), :] = gnew
            da_ref[pl.ds(p, 1), :] = gnew * y_ref[pl.ds(pprev, 1), :]
            return gnew, a_ref[pl.ds(p, 1), :]

        def step(s8, carry):
            gf, anf, gr, anr = carry
            base = s8 * 8
            for q in range(8):
                s = last - (base + q)
                gf, anf = one(dyf, af, yf, daf, dbf, gf, anf, s, s - 1)
                gr, anr = one(dyr, ar, yr, dar, dbr, gr, anr, last - s, last - s + 1)
            return gf, anf, gr, anr

        carry = (gf_s[...], anf_s[...], gr_s[...], anr_s[...])
        carry = lax.fori_loop(0, ROW_TILE // 8 - 1, step, carry)
        gf, anf, gr, anr = carry
        for s in range(7, 0, -1):
            gf, anf = one(dyf, af, yf, daf, dbf, gf, anf, s, s - 1)
            gr, anr = one(dyr, ar, yr, dar, dbr, gr, anr, last - s, last - s + 1)
        gf0 = dyf[0:1, :] + anf * gf
        dbf[0:1, :] = gf0
        daf[0:1, :] = gf0 * hf0[...]
        gr0 = dyr[last:last + 1, :] + anr * gr
        dbr[last:last + 1, :] = gr0
        dar[last:last + 1, :] = gr0 * hr0[...]
        gf_s[...] = gf0
        anf_s[...] = af[0:1, :]
        gr_s[...] = gr0
        anr_s[...] = ar[last:last + 1, :]

    return _pcall(
        kern, name="scan_bwd", grid=(N_SCAN,),
        in_specs=[fwd_spec, fwd_spec, fwd_spec, hin_spec, rev_spec, rev_spec, rev_spec, hin_spec],
        out_specs=[fwd_spec, fwd_spec, rev_spec, rev_spec],
        out_shape=[_sds((TA, R), F32)] * 4,
        scratch_shapes=[pltpu.VMEM((1, R), F32)] * 4, compiler_params=_cparams(),
    )(dy, a_f, y_f, hin_f, dy, a_r, y_r, hin_r)


def _me():
    return lax.axis_index("x"), lax.axis_index("y"), lax.axis_index("c")


def _other_chips(mx, my):
    return [(1 - mx, my), (mx, 1 - my), (1 - mx, 1 - my)]


def _rcopy(src, dst, ssem, rsem, dev):
    return pltpu.make_async_remote_copy(src_ref=src, dst_ref=dst, send_sem=ssem, recv_sem=rsem,
                                        device_id=dev, device_id_type=MESH)


def _allgather8(name, x, dep=None):
    rows, cols = x.shape
    n_dep = len(_behind(dep))

    def kern(x_ref, *rest):
        o_ref, ssem, rsem, lsem = rest[n_dep:]
        mx, my, mc = _me()
        me = 4 * mx + 2 * my + mc
        peers = []
        for k in range(1, 8):
            px = 1 - mx if (k >> 2) & 1 else mx
            py = 1 - my if (k >> 1) & 1 else my
            pc = 1 - mc if k & 1 else mc
            peers.append((px, py, pc))
        mine = pltpu.make_async_copy(x_ref, o_ref.at[me], lsem)
        mine.start()
        sends = [_rcopy(x_ref, o_ref.at[me], ssem.at[k], rsem.at[k], p) for k, p in enumerate(peers)]
        for cp in sends:
            cp.start()
        for k, (px, py, pc) in enumerate(peers):
            _rcopy(x_ref, o_ref.at[4 * px + 2 * py + pc], ssem.at[k], rsem.at[k], (px, py, pc)).wait_recv()
        for cp in sends:
            cp.wait_send()
        mine.wait()

    return _pcall(
        kern, name=name, in_specs=[ANY] * (1 + n_dep), out_specs=ANY, out_shape=_sds((8, rows, cols), F32),
        scratch_shapes=[pltpu.SemaphoreType.DMA((7,)), pltpu.SemaphoreType.DMA((7,)), pltpu.SemaphoreType.DMA(())],
    )(x, *_behind(dep))


def _reduce_pair(name, gs):
    n = len(gs)

    def kern(*refs):
        g, o = refs[:n], refs[n:2 * n]
        ss, rs = refs[2 * n:]
        mx, my, mc = _me()
        sib = (mx, my, 1 - mc)
        sends = []
        for t in range(n):
            for j in range(4):
                cp = _rcopy(g[t].at[j, 1 - mc], o[t].at[j], ss.at[4 * t + j], rs.at[4 * t + j], sib)
                cp.start()
                sends.append(cp)
        for cp in sends:
            cp.wait_recv()
        for cp in sends:
            cp.wait_send()

    dma = pltpu.SemaphoreType.DMA
    return _pcall(
        kern, name=name, in_specs=[ANY] * n, out_specs=[ANY] * n,
        out_shape=[_sds((4,) + g.shape[2:], g.dtype) for g in gs],
        scratch_shapes=[dma((4 * n,)), dma((4 * n,))],
    )(*gs)


def _share_halves(name, fulls):
    n = len(fulls)

    def kern(*refs):
        o = refs[n:2 * n]
        ss, rs = refs[2 * n:]
        mx, my, mc = _me()
        sib = (mx, my, 1 - mc)
        sends = []
        for t in range(n):
            cp = _rcopy(o[t].at[mc], o[t].at[mc], ss.at[t], rs.at[t], sib)
            cp.start()
            sends.append(cp)
        for t in range(n):
            _rcopy(o[t].at[1 - mc], o[t].at[1 - mc], ss.at[t], rs.at[t], sib).wait_recv()
        for cp in sends:
            cp.wait_send()

    dma = pltpu.SemaphoreType.DMA
    return _pcall(
        kern, name=name, in_specs=[ANY] * n, out_specs=[ANY] * n,
        out_shape=[_sds(f.shape, f.dtype) for f in fulls], input_output_aliases={t: t for t in range(n)},
        scratch_shapes=[dma((n,)), dma((n,))],
    )(*fulls)


def _tiled_sp(name, fn, grid, sp, ins, outs):
    n_in = len(ins)

    def kern(sp_ref, *refs):
        tout = fn([r[...] for r in refs[:n_in]])
        for r, v in zip(refs[n_in:], tout):
            r[...] = v.astype(r.dtype)

    gs = pltpu.PrefetchScalarGridSpec(num_scalar_prefetch=1, grid=tuple(grid),
                                      in_specs=[s for _, s in ins], out_specs=[s for _, s in outs])
    res = _pcall(kern, name=name, grid_spec=gs, out_shape=[o for o, _ in outs], compiler_params=_cparams(),
                 )(sp, *[a for a, _ in ins])
    return list(res)


def _row_tile(rows, cols, itemsize=4, budget=2 * 1024 * 1024):
    tr = rows
    while tr * cols * itemsize > budget and tr % 32 == 0:
        tr //= 2
    return tr


def _place_big(shards, place):
    slots = []
    for t, s in enumerate(shards):
        rr, cc = s.shape[1], s.shape[2]
        tr = _row_tile(rr, cc)
        (slot,) = _tiled_sp(
            f"place{t}", lambda tin: [tin[0]], (2, rr // tr), place,
            [(s, pl.BlockSpec((None, tr, cc), lambda h, i, sp: (h, i, 0)))],
            [(_sds((4, 2, rr, cc), BF16), pl.BlockSpec((None, None, tr, cc), lambda h, i, sp: (sp[0], h, i, 0)))])
        slots.append(slot)
    return slots


def _allreduce_small_begin(vec, place):
    hr = vec.shape[0] // 2
    tr = _row_tile(hr, LANE)
    blk = (None, None, tr, LANE)
    (pair,) = _tiled_sp(
        "small_place", lambda tin: [tin[0]], (2, hr // tr), place,
        [(vec.reshape(2, hr, LANE), pl.BlockSpec((None, tr, LANE), lambda h, i, sp: (h, i, 0)))],
        [(_sds((2, 2, hr, LANE), F32), pl.BlockSpec(blk, lambda h, i, sp: (sp[1], h, i, 0)))])
    (pair,) = _share_halves("small_share", [pair])
    (slot,) = _tiled_sp(
        "small_pair_add", lambda tin: [tin[0] + tin[1]], (2, hr // tr), place,
        [(pair, pl.BlockSpec(blk, lambda h, i, sp: (0, h, i, 0))),
         (pair, pl.BlockSpec(blk, lambda h, i, sp: (1, h, i, 0)))],
        [(_sds((4, 2, hr, LANE), F32), pl.BlockSpec(blk, lambda h, i, sp: (sp[0], h, i, 0)))])
    fly, sems, token = _gather_start("small_start", [slot], ((0,),), pair)
    return (fly, sems), token


def _allreduce_small_end(state, after):
    fly, sems = state
    (chips,) = _swap_halves("small_swap", _gather_wait("small_wait", fly, *sems, after))
    hr = chips.shape[2]
    tr = _row_tile(hr, LANE)
    blk = (None, None, tr, LANE)
    (total,) = _tiled(
        "small_chip_sum", lambda ids, tin, vin: ([((tin[0] + tin[1]) + tin[2]) + tin[3]], []), (2, hr // tr),
        [(chips, pl.BlockSpec(blk, lambda h, i, _j=j: (_j, h, i, 0))) for j in range(4)], [],
        [(_sds((2, hr, LANE), F32), pl.BlockSpec((None, tr, LANE), lambda h, i: (h, i, 0)))])
    return total.reshape(2 * hr, LANE)


SEM =pl.BlockSpec(memory_space=pltpu.SEMAPHORE)
_DATAFLOW = pltpu.SideEffectType.DATAFLOW_SIDE_EFFECTING


def _gather_start(name, slots, groups, after):
    n = len(slots)

    def kern(*refs):
        o = refs[n + 1:2 * n + 1]
        sems, token = refs[2 * n + 1:-1], refs[-1]
        mx, my, mc = _me()
        j0 = 2 * mx + my
        for gi, grp in enumerate(groups):
            for k, t in enumerate(grp):
                for q, (qx, qy) in enumerate(_other_chips(mx, my)):
                    _rcopy(o[t].at[j0, mc], o[t].at[j0, mc], sems[2 * gi].at[3 * k + q],
                           sems[2 * gi + 1].at[3 * k + q], (qx, qy, mc)).start()
        token[...] = jnp.zeros_like(token)

    sem_shapes = []
    for grp in groups:
        sem_shapes += [pltpu.SemaphoreType.DMA((3 * len(grp),))] * 2
    res = _pcall(
        kern, name=name, in_specs=[ANY] * (n + 1),
        out_specs=[ANY] * n + [SEM] * len(sem_shapes) + [pl.BlockSpec(memory_space=pltpu.VMEM)],
        out_shape=[_sds(w.shape, w.dtype) for w in slots] + sem_shapes + [_sds((8, LANE), F32)],
        input_output_aliases={t: t for t in range(n)},
        compiler_params=pltpu.CompilerParams(has_side_effects=_DATAFLOW),
    )(*slots, after)
    return list(res[:n]), list(res[n:-1]), res[-1]


def _gather_wait(name, bufs, ssem, rsem, after):
    n = len(bufs)

    def kern(*refs):
        b = refs[:n]
        ssem_ref, rsem_ref = refs[n], refs[n + 1]
        mx, my, mc = _me()
        j0 = 2 * mx + my
        for k in range(n):
            for q, (qx, qy) in enumerate(_other_chips(mx, my)):
                jq = 2 * qx + qy
                _rcopy(b[k].at[jq, mc], b[k].at[jq, mc], ssem_ref.at[3 * k + q], rsem_ref.at[3 * k + q],
                       (qx, qy, mc)).wait_recv()
                _rcopy(b[k].at[j0, mc], b[k].at[j0, mc], ssem_ref.at[3 * k + q], rsem_ref.at[3 * k + q],
                       (qx, qy, mc)).wait_send()

    return list(_pcall(
        kern, name=name, in_specs=[ANY] * n + [SEM, SEM, ANY], out_specs=[ANY] * n,
        out_shape=[_sds(w.shape, w.dtype) for w in bufs], input_output_aliases={k: k for k in range(n)},
        compiler_params=pltpu.CompilerParams(has_side_effects=_DATAFLOW),
    )(*bufs, ssem, rsem, after))


def _swap_halves(name, bufs):
    n = len(bufs)

    def kern(*refs):
        o = refs[n:2 * n]
        ss, rs = refs[2 * n:]
        mx, my, mc = _me()
        sib = (mx, my, 1 - mc)
        sends = []
        for k in range(n):
            for q, (qx, qy) in enumerate(_other_chips(mx, my)):
                jq = 2 * qx + qy
                cp = _rcopy(o[k].at[jq, mc], o[k].at[jq, mc], ss.at[3 * k + q], rs.at[3 * k + q], sib)
                cp.start()
                sends.append(cp)
        for k in range(n):
            for q, (qx, qy) in enumerate(_other_chips(mx, my)):
                jq = 2 * qx + qy
                _rcopy(o[k].at[jq, 1 - mc], o[k].at[jq, 1 - mc], ss.at[3 * k + q], rs.at[3 * k + q], sib).wait_recv()
        for cp in sends:
            cp.wait_send()

    dma = pltpu.SemaphoreType.DMA
    return list(_pcall(
        kern, name=name, in_specs=[ANY] * n, out_specs=[ANY] * n,
        out_shape=[_sds(w.shape, w.dtype) for w in bufs], input_output_aliases={k: k for k in range(n)},
        scratch_shapes=[dma((3 * n,)), dma((3 * n,))],
    )(*bufs))


def _chips_start(name, sums):
    n = len(sums)

    def kern(*refs):
        s, land = refs[n:2 * n], refs[2 * n:3 * n]
        ssem, rsem, token = refs[3 * n:]
        mx, my, mc = _me()
        for k in range(n):
            for q, (qx, qy) in enumerate(_other_chips(mx, my)):
                _rcopy(s[k].at[2 * qx + qy], land[k].at[q], ssem.at[3 * k + q], rsem.at[3 * k + q], (qx, qy, mc)).start()
        token[...] = jnp.zeros_like(token)

    dma = pltpu.SemaphoreType.DMA
    res = _pcall(
        kern, name=name, in_specs=[ANY] * n,
        out_specs=[ANY] * (2 * n) + [SEM, SEM, pl.BlockSpec(memory_space=pltpu.VMEM)],
        out_shape=[_sds(s.shape, s.dtype) for s in sums] + [_sds((3,) + s.shape[1:], s.dtype) for s in sums]
        + [dma((3 * n,)), dma((3 * n,)), _sds((8, LANE), F32)],
        input_output_aliases={k: k for k in range(n)},
        compiler_params=pltpu.CompilerParams(has_side_effects=_DATAFLOW),
    )(*sums)
    return (list(res[:n]), list(res[n:2 * n]), res[2 * n], res[2 * n + 1]), res[2 * n + 2]


def _chips_wait(name, sums, lands, ssem, rsem, after):
    n = len(sums)

    def kern(*refs):
        s, land = refs[:n], refs[n:2 * n]
        ssem_ref, rsem_ref = refs[2 * n], refs[2 * n + 1]
        mx, my, mc = _me()
        for k in range(n):
            for q, (qx, qy) in enumerate(_other_chips(mx, my)):
                cp = _rcopy(s[k].at[2 * qx + qy], land[k].at[q], ssem_ref.at[3 * k + q], rsem_ref.at[3 * k + q],
                            (qx, qy, mc))
                cp.wait_recv()
                cp.wait_send()

    res = _pcall(
        kern, name=name, in_specs=[ANY] * (2 * n) + [SEM, SEM, ANY], out_specs=[ANY] * (2 * n),
        out_shape=[_sds(a.shape, a.dtype) for a in list(sums) + list(lands)],
        input_output_aliases={k: k for k in range(2 * n)},
        compiler_params=pltpu.CompilerParams(has_side_effects=_DATAFLOW),
    )(*sums, *lands, ssem, rsem, after)
    return list(res[:n]), list(res[n:])


def _reduce_begin(tag, parts, place):
    theirs = _reduce_pair(f"reduce_pair_{tag}", parts)
    sums = []
    for k, (p, o) in enumerate(zip(parts, theirs)):
        rr, cc = p.shape[2], p.shape[3]
        tr = _row_tile(rr, cc)
        (s_k,) = _tiled_sp(
            f"pair_add_{tag}{k}", lambda tin: [tin[0].astype(F32) + tin[1].astype(F32)], (4, rr // tr), place,
            [(p, pl.BlockSpec((None, None, tr, cc), lambda j, i, sp: (j, sp[1], i, 0))),
             (o, pl.BlockSpec((None, tr, cc), lambda j, i, sp: (j, i, 0)))],
            [(_sds((4, rr, cc), BF16), pl.BlockSpec((None, tr, cc), lambda j, i, sp: (j, i, 0)))])
        sums.append(s_k)
    return _chips_start(f"chips_start_{tag}", sums)


def _reduce_end(tag, flying, place, after):
    sums, lands = _chips_wait(f"chips_wait_{tag}", *flying, after)
    fulls = []
    for k, (s, q) in enumerate(zip(sums, lands)):
        rr, cc = q.shape[1], q.shape[2]
        tr = _row_tile(rr, cc)

        def add4(tin):
            return [((tin[0].astype(F32) + tin[1].astype(F32)) + tin[2].astype(F32)) + tin[3].astype(F32)]

        ins = [(s, pl.BlockSpec((None, tr, cc), lambda i, sp: (sp[0], i, 0)))]
        ins += [(q, pl.BlockSpec((None, tr, cc), lambda i, sp, _k=kk: (_k, i, 0))) for kk in range(3)]
        (f_k,) = _tiled_sp(f"chip_add_{tag}{k}", add4, (rr // tr,), place, ins,
                           [(_sds((2, rr, cc), F32), pl.BlockSpec((None, tr, cc), lambda i, sp: (sp[1], i, 0)))])
        fulls.append(f_k)
    return fulls


def _pack(parts, PACK_ROWS=PACK_ROWS):
    flat, offs, pos = [], [], 0
    for p in parts:
        v = p.reshape(-1).astype(F32)
        n = -(-v.shape[0] // LANE) * LANE
        flat.append(jnp.pad(v, (0, n - v.shape[0])))
        offs.append((pos, v.shape[0], p.shape))
        pos += n
    total = -(-pos // (PACK_ROWS * LANE)) * PACK_ROWS * LANE
    flat.append(jnp.zeros((total - pos,), F32))
    return jnp.concatenate(flat).reshape(-1, LANE), offs


def _unpack(vec, offs):
    v = vec.reshape(-1)
    return [v[p:p + n].reshape(shape) for p, n, shape in offs]


def _adamw(name, w, g, m, v):
    rows, cols = w.shape
    tr = rows
    for cand in (512, 256, 128, 64, 32, 16, 8):
        if rows % cand == 0 and cand * cols * 4 <= 2 * 1024 * 1024:
            tr = cand
            break
    bc1 = 1.0 - ADAM_B1 ** ADAM_STEP
    bc2 = 1.0 - ADAM_B2 ** ADAM_STEP

    def fn(ids, tin, vin):
        wv, gv, mv, vv = tin
        mn = ADAM_B1 * mv + (1.0 - ADAM_B1) * gv
        vn = ADAM_B2 * vv + (1.0 - ADAM_B2) * (gv * gv)
        delta = -ADAM_LR * ((mn / bc1) / (jnp.sqrt(vn / bc2) + ADAM_EPS) + ADAM_WD * wv)
        return [delta, mn, vn], []

    spec = pl.BlockSpec((tr, cols), lambda i: (i, 0))
    outs = [(_sds((rows, cols), F32), spec)] * 3
    return _tiled(name, fn, (rows // tr,), [(a, spec) for a in (w, g, m, v)], [], outs)


def _pos_embed():
    n_rows = T // GRID_W
    q = D // 4
    omega = 1.0 / (10000.0 ** (jnp.arange(q, dtype=F32) / q))
    er = jnp.arange(n_rows, dtype=jnp.int32).astype(F32)[:, None] * omega[None, :]
    ec = jnp.arange(GRID_W, dtype=jnp.int32).astype(F32)[:, None] * omega[None, :]
    by_row = jnp.concatenate([jnp.sin(er), jnp.cos(er)], axis=-1)
    by_col = jnp.concatenate([jnp.sin(ec), jnp.cos(ec)], axis=-1)
    return jnp.concatenate([jnp.repeat(by_row, GRID_W, axis=0), jnp.tile(by_col, (n_rows, 1))], axis=-1)


def _dense_gates(w_a, w_x):
    per = N_BLK // 2
    on_diag = _on_diag()
    halves = []
    for h in range(2):
        cols = []
        for src in (w_a[0], w_x[0], w_a[1], w_x[1]):
            rows = src[h * per:(h + 1) * per].reshape(RH, BLK)
            cols.append(jnp.where(on_diag, jnp.tile(rows, (1, per)), 0.0))
        halves.append(jnp.concatenate(cols, axis=1))
    return jnp.stack(halves).astype(BF16)


def _on_diag():
    r = lax.broadcasted_iota(jnp.int32, (RH, RH), 0) // BLK
    c = lax.broadcasted_iota(jnp.int32, (RH, RH), 1) // BLK
    return r == c


def _gate_block_grads(dwbd):
    per = N_BLK // 2
    on_diag = _on_diag()
    kinds = []
    for q in range(4):
        per_half = []
        for h in range(2):
            dq = jnp.where(on_diag, dwbd[h][:, q * RH:(q + 1) * RH], 0.0)
            per_half.append(dq.reshape(RH, per, BLK).sum(axis=1).reshape(per, BLK, BLK))
        kinds.append(jnp.concatenate(per_half, axis=0))
    return jnp.stack([kinds[0], kinds[2]]), jnp.stack([kinds[1], kinds[3]])


def _gate_bias_dense(b_a, b_x):
    cols = []
    for h in range(2):
        for src in (b_a[0], b_x[0], b_a[1], b_x[1]):
            cols.append(src.reshape(R)[h * RH:(h + 1) * RH])
    return jnp.concatenate(cols).reshape(1, 2 * NQ)


def _gate_bias_grads(dgb):
    v = dgb.reshape(2, 4, RH)
    kinds = [jnp.concatenate([v[0, q], v[1, q]]).reshape(N_BLK, BLK) for q in range(4)]
    return jnp.stack([kinds[0], kinds[2]]), jnp.stack([kinds[1], kinds[3]])


def _mlp_fwd(tag, x_in, g_norm, sh, sc, gate, w_in, w_out):
    n_t = T // ROW_TILE
    (h,) = _tiled(f"{tag}_norm", lambda ids, t, v: ([_norm_mod(t[0], v[0], v[1], v[2])], []), (n_t,),
                  [_rows(x_in)], [g_norm, sc, sh], [_orow(T, D, BF16)])
    tm = MM_TILE
    (r,) = _mm(f"{tag}_in", h, w_in, _NN, (T // tm, 4, 1),
               pl.BlockSpec((tm, D), lambda i, j, k: (i, 0)), pl.BlockSpec((None, D, D), lambda i, j, k: (j, 0, 0)),
               [(_sds((T, FF), BF16), pl.BlockSpec((tm, D), lambda i, j, k: (i, j)))], (tm, D),
               epi=lambda acc, ex: [jnp.maximum(acc, 0.0)])
    o, x_out = _mm(f"{tag}_out", r, w_out, _NN, (T // tm, 1, FF // D),
                   pl.BlockSpec((tm, D), lambda i, j, k: (i, k)), pl.BlockSpec((D, D), lambda i, j, k: (k, 0)),
                   [(_sds((T, D), F32), pl.BlockSpec((tm, D), lambda i, j, k: (i, 0)))] * 2, (tm, D),
                   extra=[(x_in, pl.BlockSpec((tm, D), lambda i, j, k: (i, 0))), (gate, _full_spec(gate))],
                   a_pre=lambda a: a * a, epi=lambda acc, ex: [acc, ex[0] + ex[1] * acc])
    return dict(h=h, r=r, o=o, x_in=x_in), x_out


def _behind(dep):
    return [] if dep is None else [dep]


def _gate_bwd(tag, dx, o, gate, dep=None):
    def fn(ids, t, v):
        d_o = t[0] * v[0]
        return [d_o], [_sum0(t[0] * t[1]), _sum0(d_o)]
    return _tiled(f"{tag}_gate_bwd", fn, (T // ROW_TILE,), [_rows(dx), _rows(o)], [gate] + _behind(dep),
                  [_orow(T, D, BF16)], [(1, D), (1, D)])


def _norm_bwd(tag, dx_res, dh, dh_off, x, g_norm, sc, with_dx=True, dep=None):
    n_t = x.shape[0] // ROW_TILE

    def fn(ids, t, v):
        if with_dx:
            dres, dhv, xv = t
        else:
            dhv, xv = t
        dxv, d_sh, d_sc, d_g = _norm_mod_bwd(dhv, xv, v[0], v[1])
        return ([dres + dxv] if with_dx else []), [d_sh, d_sc, d_g]

    ins = ([_rows(dx_res)] if with_dx else []) + [_rows(dh, off=dh_off), _rows(x)]
    outs = [_orow(x.shape[0], D, F32)] if with_dx else []
    return _tiled(f"{tag}_norm_bwd", fn, (n_t,), ins, [g_norm, sc] + _behind(dep), outs, [(1, D)] * 3)


def _mlp_bwd(tag, dx, saved, g_norm, sc, gate, w_in, w_out):
    d_o, d_gate, _ = _gate_bwd(tag, dx, saved["o"], gate)
    tm = MM_TILE
    r = saved["r"]
    (da,) = _mm(f"{tag}_dz", d_o, w_out, _NT, (T // tm, FF // D, 1),
                pl.BlockSpec((tm, D), lambda i, j, k: (i, 0)), pl.BlockSpec((D, D), lambda i, j, k: (j, 0)),
                [(_sds((T, FF), BF16), pl.BlockSpec((tm, D), lambda i, j, k: (i, j)))], (tm, D),
                extra=[(r, pl.BlockSpec((tm, D), lambda i, j, k: (i, j)))],
                epi=lambda acc, ex: [acc * (2.0 * ex[0].astype(F32))])
    tk = MM_TILE
    (dw_out,) = _mm(f"{tag}_dwout", r, d_o, _TN, (FF // tm, 1, T // tk),
                    pl.BlockSpec((tk, tm), lambda i, j, k: (k, i)), pl.BlockSpec((tk, D), lambda i, j, k: (k, 0)),
                    [(_sds((FF, D), BF16), pl.BlockSpec((tm, D), lambda i, j, k: (i, 0)))], (tm, D),
                    a_pre=lambda a: a * a)
    (dh,) = _mm(f"{tag}_dh", da, w_in, _NT, (T // tm, 1, 4),
                pl.BlockSpec((tm, D), lambda i, j, k: (i, k)), pl.BlockSpec((None, D, D), lambda i, j, k: (k, 0, 0)),
                [(_sds((T, D), F32), pl.BlockSpec((tm, D), lambda i, j, k: (i, 0)))], (tm, D))
    (dw_in,) = _mm(f"{tag}_dwin", saved["h"], da, _TN, (D // tm, 4, T // tk),
                   pl.BlockSpec((tk, tm), lambda i, j, k: (k, i)), pl.BlockSpec((tk, D), lambda i, j, k: (k, j)),
                   [(_sds((4, D, D), BF16), pl.BlockSpec((None, tm, D), lambda i, j, k: (j, i, 0)))], (tm, D))
    dx_in, d_sh, d_sc, d_g = _norm_bwd(tag, dx, dh, 0, saved["x_in"], g_norm, sc)
    return dx_in, dw_in, dw_out, dict(sh=d_sh, sc=d_sc, gate=d_gate, g_norm=d_g)


def _local_step(x, ctx, tgt, mods, cmods, norm_g, final_g, rec, conf, wg, on_grads=None, start_dep=None):
    on_grads = on_grads or (lambda group, dws: None)
    n_t = T // ROW_TILE
    row = lambda v: v.reshape(1, -1)
    m0 = [row(mods[0, q]) for q in range(6)]
    m1 = [row(mods[1, q]) for q in range(6)]
    g00, g01, g10, g11 = (row(norm_g[0, 0]), row(norm_g[0, 1]), row(norm_g[1, 0]), row(norm_g[1, 1]))
    csh, csc = row(cmods[0]), row(cmods[1])
    pos = _pos_embed()

    def prep0(ids, t, v):
        cx, xv, pv = t
        is_ctx = ids[0] == 0
        xin = jnp.where(is_ctx, cx, xv + pv)
        sh = jnp.where(is_ctx, v[3], v[1])
        sc = jnp.where(is_ctx, v[4], v[2])
        return [_norm_mod(xin, v[0], sc, sh), xv + pv], []

    hcat, x0 = _tiled(
        "prep0", prep0, (N_SCAN,),
        [(ctx, pl.BlockSpec((ROW_TILE, D), lambda i: (0, 0))), _rows(x, off=-1, clamp_lo=True),
         _rows(pos, off=-1, clamp_lo=True)],
        [g00, m0[0], m0[1], csh, csc] + _behind(start_dep),
        [_orow(TA, D, BF16), _orow(T, D, F32, off=-1, clamp_lo=True)])

    tm_a = REC_TILE
    w_rec = wg("rec", hcat)
    (a_in,) = _mm("rec_in", hcat, w_rec["rec_w_in"], _NN, (TA // tm_a, 4, 1),
                  pl.BlockSpec((tm_a, D), lambda i, j, k: (i, 0)),
                  pl.BlockSpec((None, D, RH), lambda i, j, k: (j, 0, 0)),
                  [(_sds((TA, 2 * R), F32), pl.BlockSpec((tm_a, RH), lambda i, j, k: (i, j)))], (tm_a, RH))
    rec_starts = (0, 1)
    u = _dwconv("rec_conv", a_in, R // CW_REC, rec["conv_w"], row(rec["conv_b"]), 1, rec_starts, R, CW_REC)
    wbd = _dense_gates(rec["w_a"], rec["w_x"])
    gbias = _gate_bias_dense(rec["b_a"], rec["b_x"])
    lam = rec["lam"]
    a_f, b_f, a_r, b_r = _tiled("rg_fwd", _rg_fwd_fn, (TA // RG_TILE,), [_rows(u, tm=RG_TILE)], [wbd, gbias, lam],
                                [_orow(TA, R, F32, tm=RG_TILE)] * 4, vec_refs=True)
    y_f, y_r, hin_f, hin_r = _scan_fwd(a_f, b_f, a_r, b_r)

    def rec_mid(ids, t, v):
        gp, yf, yr = t
        g, _ = _gelu(gp)
        return [g * (yf + yr)], []

    (m_rec,) = _tiled("rec_mid", rec_mid, (n_t,),
                      [_rows(a_in, R, off=1), _rows(y_f, off=1), _rows(y_r, off=1)], [], [_orow(T, R, BF16)])
    tm = MM_TILE
    o_rec, x1 = _mm("rec_out", m_rec, w_rec["rec_w_out"], _NN, (T // tm, 1, 1),
                    pl.BlockSpec((tm, R), lambda i, j, k: (i, 0)), pl.BlockSpec((R, D), lambda i, j, k: (0, 0)),
                    [(_sds((T, D), F32), pl.BlockSpec((tm, D), lambda i, j, k: (i, 0)))] * 2, (tm, D),
                    extra=[(x0, pl.BlockSpec((tm, D), lambda i, j, k: (i, 0))), (m0[2], _full_spec(m0[2]))],
                    epi=lambda acc, ex: [acc, ex[0] + ex[1] * acc])
    w_m0 = wg("mlp0", x1)
    mlp0, x2 = _mlp_fwd("mlp0", x1, g01, m0[3], m0[4], m0[5], w_m0["w_in"], w_m0["w_out"])

    (h1,) = _tiled("conf_norm", lambda ids, t, v: ([_norm_mod(t[0], v[0], v[1], v[2])], []), (n_t,),
                   [_rows(x2)], [g10, m1[1], m1[0]], [_orow(T, D, BF16)])
    b_pw1 = row(conf["b_pw1"])
    w_cf = wg("conf", x2)
    (pre,) = _mm("conf_pw1", h1, w_cf["conf_w_pw1"], _NN, (T // tm, 4, 1),
                 pl.BlockSpec((tm, D), lambda i, j, k: (i, 0)),
                 pl.BlockSpec((None, D, D // 2), lambda i, j, k: (j, 0, 0)),
                 [(_sds((T, 2 * D), F32), pl.BlockSpec((tm, D // 2), lambda i, j, k: (i, j)))], (tm, D // 2),
                 extra=[(b_pw1, pl.BlockSpec((1, D // 2), lambda i, j, k: (0, j)))],
                 epi=lambda acc, ex: [acc + ex[0]])
    (zg,) = _tiled("conf_glu", lambda ids, t, v: ([t[0] * _sigmoid(t[1])], []), (n_t,),
                   [_rows(pre, D, col=0), _rows(pre, D, col=1)], [], [_orow(T, D, F32)])
    conf_starts = (0,)
    zc = _dwconv("conf_conv", zg, 0, conf["conv_w"], row(conf["conv_b"]), CONF_KW // 2, conf_starts, D, CW_CONF)
    ln_g, ln_b = row(conf["ln_g"]), row(conf["ln_b"])

    def ln_silu(ids, t, v):
        nh, _ = _layernorm_parts(t[0])
        ln = nh * v[0] + v[1]
        return [ln * _sigmoid(ln)], []

    (s_conf,) = _tiled("conf_ln", ln_silu, (n_t,), [_rows(zc)], [ln_g, ln_b], [_orow(T, D, BF16)])
    b_pw2 = row(conf["b_pw2"])
    y_conf, x3 = _mm("conf_pw2", s_conf, w_cf["conf_w_pw2"], _NN, (T // tm, 1, 1),
                     pl.BlockSpec((tm, D), lambda i, j, k: (i, 0)), pl.BlockSpec((D, D), lambda i, j, k: (0, 0)),
                     [(_sds((T, D), F32), pl.BlockSpec((tm, D), lambda i, j, k: (i, 0)))] * 2, (tm, D),
                     extra=[(x2, pl.BlockSpec((tm, D), lambda i, j, k: (i, 0))), (m1[2], _full_spec(m1[2])),
                            (b_pw2, _full_spec(b_pw2))],
                     epi=lambda acc, ex: [acc + ex[2], ex[0] + ex[1] * (acc + ex[2])])
    w_m1 = wg("mlp1", x3)
    mlp1, x4 = _mlp_fwd("mlp1", x3, g11, m1[3], m1[4], m1[5], w_m1["w_in"], w_m1["w_out"])

    fg = row(final_g)

    def head(ids, t, v):
        n, r = _rms(t[0])
        err = n * v[0] - t[1]
        d_out = err * (1.0 / D)
        dn = d_out * v[0]
        dxv = r * (dn - n * jnp.mean(dn * n, axis=-1, keepdims=True))
        part = jnp.sum(_sum0(err * err), axis=1, keepdims=True) * (0.5 / D)
        return [dxv], [part, _sum0(d_out * n)]

    dx4, loss, d_fg = _tiled("head", head, (n_t,), [_rows(x4), _rows(tgt)], [fg], [_orow(T, D, F32)],
                             [(1, 1), (1, D)])

    dx3, dw_in1, dw_out1, dm_mlp1 = _mlp_bwd("mlp1", dx4, mlp1, g11, m1[4], m1[5],
                                             w_m1["w_in"], w_m1["w_out"])
    dep = on_grads("mlp1", (dw_in1, dw_out1))
    d_y, d_g1c, d_bpw2 = _gate_bwd("conf", dx3, y_conf, m1[2], dep)
    tk = MM_TILE
    (dw_pw2,) = _mm("conf_dwpw2", s_conf, d_y, _TN, (D // tm, 1, T // tk),
                    pl.BlockSpec((tk, tm), lambda i, j, k: (k, i)), pl.BlockSpec((tk, D), lambda i, j, k: (k, 0)),
                    [(_sds((D, D), BF16), pl.BlockSpec((tm, D), lambda i, j, k: (i, 0)))], (tm, D))
    (ds,) = _mm("conf_ds", d_y, w_cf["conf_w_pw2"], _NT, (T // tm, 1, 1),
                pl.BlockSpec((tm, D), lambda i, j, k: (i, 0)), pl.BlockSpec((D, D), lambda i, j, k: (0, 0)),
                [(_sds((T, D), F32), pl.BlockSpec((tm, D), lambda i, j, k: (i, 0)))], (tm, D))

    def ln_silu_bwd(ids, t, v):
        dsv, zcv = t
        nh, rstd = _layernorm_parts(zcv)
        ln = nh * v[0] + v[1]
        sg = _sigmoid(ln)
        d_ln = dsv * (sg * (1.0 + ln * (1.0 - sg)))
        d_nh = d_ln * v[0]
        d_zc = rstd * (d_nh - jnp.mean(d_nh, axis=-1, keepdims=True)
                       - nh * jnp.mean(d_nh * nh, axis=-1, keepdims=True))
        return [d_zc], [_sum0(d_ln * nh), _sum0(d_ln)]

    d_zc, d_lng, d_lnb = _tiled("conf_ln_bwd", ln_silu_bwd, (n_t,), [_rows(ds), _rows(zc)], [ln_g, ln_b],
                                [_orow(T, D, F32)], [(1, D), (1, D)])
    d_zg = _dwconv("conf_conv_dx", d_zc, 0, conf["conv_w"][::-1], jnp.zeros((1, D), F32),
                   CONF_KW - 1 - CONF_KW // 2, conf_starts, D, CW_CONF)
    d_cw_conf = _dwconv_wgrad("conf_conv_dw", d_zc, zg, 0, CONF_KW, CONF_KW // 2, conf_starts, D, CW_CONF)

    def glu_bwd(ids, t, v):
        dz, pa, pb = t
        sg = _sigmoid(pb)
        d_a = dz * sg
        d_b = dz * pa * sg * (1.0 - sg)
        return [d_a, d_b], [_sum0(d_a), _sum0(d_b)]

    d_pre_a, d_pre_b, d_b1a, d_b1b = _tiled(
        "conf_glu_bwd", glu_bwd, (n_t,), [_rows(d_zg), _rows(pre, D, col=0), _rows(pre, D, col=1)], [],
        [_orow(T, D, BF16), _orow(T, D, BF16)], [(1, D), (1, D)])
    d_pre = jnp.concatenate([d_pre_a, d_pre_b], axis=1)
    (dw_pw1,) = _mm("conf_dwpw1", h1, d_pre, _TN, (D // tm, 4, T // tk),
                    pl.BlockSpec((tk, tm), lambda i, j, k: (k, i)),
                    pl.BlockSpec((tk, D // 2), lambda i, j, k: (k, j)),
                    [(_sds((4, D, D // 2), BF16), pl.BlockSpec((None, tm, D // 2), lambda i, j, k: (j, i, 0)))],
                    (tm, D // 2))
    dep = on_grads("conf", (dw_pw1, dw_pw2))
    (dh1,) = _mm("conf_dh", d_pre, w_cf["conf_w_pw1"], _NT, (T // tm, 1, 4),
                 pl.BlockSpec((tm, D // 2), lambda i, j, k: (i, k)),
                 pl.BlockSpec((None, D, D // 2), lambda i, j, k: (k, 0, 0)),
                 [(_sds((T, D), F32), pl.BlockSpec((tm, D), lambda i, j, k: (i, 0)))], (tm, D))
    dx2, d_sh1c, d_sc1c, d_g10 = _norm_bwd("conf", dx3, dh1, 0, x2, g10, m1[1], dep=dep)

    dx1, dw_in0, dw_out0, dm_mlp0 = _mlp_bwd("mlp0", dx2, mlp0, g01, m0[4], m0[5],
                                             w_m0["w_in"], w_m0["w_out"])
    dep = on_grads("mlp0", (dw_in0, dw_out0))
    d_orec, d_g1r, _ = _gate_bwd("rec", dx1, o_rec, m0[2], dep)
    (dw_rout,) = _mm("rec_dwout", m_rec, d_orec, _TN, (R // RH, 1, T // tk),
                     pl.BlockSpec((tk, RH), lambda i, j, k: (k, i)), pl.BlockSpec((tk, D), lambda i, j, k: (k, 0)),
                     [(_sds((R, D), BF16), pl.BlockSpec((RH, D), lambda i, j, k: (i, 0)))], (RH, D))
    (dm_rec,) = _mm("rec_dm", d_orec, w_rec["rec_w_out"], _NT, (T // tm, 1, 1),
                    pl.BlockSpec((tm, D), lambda i, j, k: (i, 0)), pl.BlockSpec((R, D), lambda i, j, k: (0, 0)),
                    [(_sds((T, R), F32), pl.BlockSpec((tm, R), lambda i, j, k: (i, 0)))], (tm, R))

    def rec_mid_bwd(ids, t, v):
        dmv, gp, yf, yr = t
        g, th = _gelu(gp)
        lat = ids[0] > 0
        d_gp = jnp.where(lat, dmv * (yf + yr) * _gelu_grad(gp, th), 0.0)
        dy = jnp.where(lat, dmv * g, 0.0)
        return [d_gp, dy], []

    d_gp, dy = _tiled("rec_mid_bwd", rec_mid_bwd, (N_SCAN,),
                      [_rows(dm_rec, off=-1, clamp_lo=True), _rows(a_in, R), _rows(y_f), _rows(y_r)], [],
                      [_orow(TA, R, BF16), _orow(TA, R, F32)])
    da_f, db_f, da_r, db_r = _scan_bwd(dy, a_f, y_f, hin_f, a_r, y_r, hin_r)
    d_gpre, d_u, d_gbias, d_lam = _tiled(
        "rg_bwd", _rg_bwd_fn, (TA // RG_TILE,), [_rows(a, tm=RG_TILE) for a in (u, da_f, db_f, da_r, db_r)],
        [wbd, gbias, lam], [_orow(TA, 2 * NQ, BF16, tm=RG_TILE), _orow(TA, R, F32, tm=RG_TILE)],
        [(1, 2 * NQ), (1, 2 * R)], vec_refs=True)
    tk_a = REC_TILE
    (d_wbd,) = _mm("rg_dw", u, d_gpre, _TN, (2, 2, TA // tk_a),
                   pl.BlockSpec((tk_a, RH), lambda i, j, k: (k, i)),
                   pl.BlockSpec((tk_a, NQ // 2), lambda i, j, k: (k, 2 * i + j)),
                   [(_sds((2, RH, NQ), F32), pl.BlockSpec((None, RH, NQ // 2), lambda i, j, k: (i, 0, j)))],
                   (RH, NQ // 2))
    d_p = _dwconv("rec_conv_dx", d_u, 0, rec["conv_w"][::-1], jnp.zeros((1, R), F32), REC_KW - 1 - 1,
                  rec_starts, R, CW_REC)
    d_cw_rec = _dwconv_wgrad("rec_conv_dw", d_u, a_in, R // CW_REC, REC_KW, 1, rec_starts, R, CW_REC)
    d_a = jnp.concatenate([d_gp, d_p.astype(BF16)], axis=1)
    (dw_rin,) = _mm("rec_dwin", hcat, d_a, _TN, (D // tm, 4, TA // tk_a),
                    pl.BlockSpec((tk_a, tm), lambda i, j, k: (k, i)), pl.BlockSpec((tk_a, RH), lambda i, j, k: (k, j)),
                    [(_sds((4, D, RH), BF16), pl.BlockSpec((None, tm, RH), lambda i, j, k: (j, i, 0)))], (tm, RH))
    dep = on_grads("rec", (dw_rin, dw_rout))
    (dhcat,) = _mm("rec_dh", d_a, w_rec["rec_w_in"], _NT, (TA // tm_a, 1, 4),
                   pl.BlockSpec((tm_a, RH), lambda i, j, k: (i, k)),
                   pl.BlockSpec((None, D, RH), lambda i, j, k: (k, 0, 0)),
                   [(_sds((TA, D), F32), pl.BlockSpec((tm_a, D), lambda i, j, k: (i, 0)))], (tm_a, D))
    dx0, d_sh1r, d_sc1r, d_g00 = _norm_bwd("rec", dx1, dhcat, 1, x0, g00, m0[1], dep=dep)
    d_csh, d_csc, d_g00c = _norm_bwd("ctx", None, dhcat, 0, ctx, g00, csc, with_dx=False)

    big = dict(rec_w_in=dw_rin, rec_w_out=dw_rout, conf_w_pw1=dw_pw1, conf_w_pw2=dw_pw2,
               mlp_w_in=(dw_in0, dw_in1), mlp_w_out=(dw_out0, dw_out1))
    d_wa, d_wx = _gate_block_grads(d_wbd)
    d_ba, d_bx = _gate_bias_grads(d_gbias)
    d_mod = jnp.concatenate([
        d_sh1r, d_sc1r, d_g1r, dm_mlp0["sh"], dm_mlp0["sc"], dm_mlp0["gate"],
        d_sh1c, d_sc1c, d_g1c, dm_mlp1["sh"], dm_mlp1["sc"], dm_mlp1["gate"]], axis=1).reshape(2, 6 * D)
    small = dict(
        d_mod=d_mod, d_cmod=jnp.concatenate([d_csh, d_csc], axis=1),
        norm_g=jnp.concatenate([d_g00 + d_g00c, dm_mlp0["g_norm"], d_g10, dm_mlp1["g_norm"]], axis=1),
        rec_conv_w=d_cw_rec[:REC_KW], rec_conv_b=d_cw_rec[REC_KW], rec_lambda=d_lam.reshape(2, R),
        rec_w_a=d_wa, rec_b_a=d_ba, rec_w_x=d_wx, rec_b_x=d_bx,
        conf_b_pw1=jnp.concatenate([d_b1a, d_b1b], axis=1), conf_conv_w=d_cw_conf[:CONF_KW],
        conf_conv_b=d_cw_conf[CONF_KW], conf_ln_g=d_lng, conf_ln_b=d_lnb, conf_b_pw2=d_bpw2, final_g=d_fg)
    return loss.reshape(()), dx0, big, small


_BIG = ("rec_w_in", "rec_w_out", "conf_w_pw1", "conf_w_pw2", "mlp_w_in", "mlp_w_out")


def _halves(w):
    return w.reshape(2, w.shape[0] // 2, w.shape[1])


def _ada_fwd(c16, w_ada, b_shard):
    ns = w_ada.shape[2]
    tn = 512

    def kern(c_ref, w_ref, b_ref, o_ref):
        cv = c_ref[...]
        s = (cv * _sigmoid(cv)).astype(BF16)
        o_ref[...] = jnp.dot(s, w_ref[...].astype(BF16), preferred_element_type=F32) + b_ref[...]

    return _pcall(
        kern, name="ada_fwd", grid=(2, ns // tn),
        in_specs=[pl.BlockSpec((16, D), lambda l, j: (0, 0)), pl.BlockSpec((None, D, tn), lambda l, j: (l, 0, j)),
                  pl.BlockSpec((None, 1, tn), lambda l, j: (l, 0, j))],
        out_specs=pl.BlockSpec((None, 16, tn), lambda l, j: (l, 0, j)),
        out_shape=_sds((2, 16, ns), F32), compiler_params=_cparams(),
    )(c16, w_ada, b_shard)


def _ada_bwd(c16, dm16, w_ada):
    ns = w_ada.shape[2]
    tn = 512

    def kern(c_ref, dm_ref, w_ref, gw_ref, ds_ref):
        cv = c_ref[...]
        s = (cv * _sigmoid(cv)).astype(BF16)
        dm = dm_ref[...].astype(BF16)
        gw_ref[...] = lax.dot_general(s, dm, _TN, preferred_element_type=F32)

        @pl.when(jnp.logical_and(pl.program_id(0) == 0, pl.program_id(1) == 0))
        def _():
            ds_ref[...] = jnp.zeros_like(ds_ref)

        ds_ref[...] += lax.dot_general(dm, w_ref[...].astype(BF16), _NT, preferred_element_type=F32)

    return _pcall(
        kern, name="ada_bwd", grid=(2, ns // tn),
        in_specs=[pl.BlockSpec((16, D), lambda l, j: (0, 0)), pl.BlockSpec((None, 16, tn), lambda l, j: (l, 0, j)),
                  pl.BlockSpec((None, D, tn), lambda l, j: (l, 0, j))],
        out_specs=[pl.BlockSpec((None, D, tn), lambda l, j: (l, 0, j)), pl.BlockSpec((16, D), lambda l, j: (0, 0))],
        out_shape=[_sds((2, D, ns), F32), _sds((16, D), F32)], compiler_params=_cparams(),
    )(c16, dm16, w_ada)


def _cctx_grad(ds8, c_ctx):
    def kern(d_ref, c_ref, o_ref):
        tot = d_ref[0, 8:9, :] + d_ref[2, 8:9, :] + d_ref[4, 8:9, :] + d_ref[6, 8:9, :]
        cv = c_ref[...]
        sg = _sigmoid(cv)
        o_ref[...] = tot * (sg * (1.0 + cv * (1.0 - sg)))

    return _pcall(kern, name="cctx_grad", out_shape=_sds((1, D), F32))(ds8, c_ctx.reshape(1, D))


def kernel(x, c, ctx, c_ctx, w_ada, b_ada, norm_g, rec_w_in, rec_conv_w, rec_conv_b, rec_lambda, rec_w_a, rec_b_a, rec_w_x, rec_b_x, rec_w_out, conf_w_pw1, conf_b_pw1, conf_conv_w, conf_conv_b, conf_ln_g, conf_ln_b, conf_w_pw2, conf_b_pw2, mlp_w_in, mlp_w_out, final_g, loss_target, m_c_ctx, m_w_ada, m_b_ada, m_norm_g, m_rec_w_in, m_rec_conv_w, m_rec_conv_b, m_rec_lambda, m_rec_w_a, m_rec_b_a, m_rec_w_x, m_rec_b_x, m_rec_w_out, m_conf_w_pw1, m_conf_b_pw1, m_conf_conv_w, m_conf_conv_b, m_conf_ln_g, m_conf_ln_b, m_conf_w_pw2, m_conf_b_pw2, m_mlp_w_in, m_mlp_w_out, m_final_g, v_c_ctx, v_w_ada, v_b_ada, v_norm_g, v_rec_w_in, v_rec_conv_w, v_rec_conv_b, v_rec_lambda, v_rec_w_a, v_rec_b_a, v_rec_w_x, v_rec_b_x, v_rec_w_out, v_conf_w_pw1, v_conf_b_pw1, v_conf_conv_w, v_conf_conv_b, v_conf_ln_g, v_conf_ln_b, v_conf_w_pw2, v_conf_b_pw2, v_mlp_w_in, v_mlp_w_out, v_final_g):
    names = ["c_ctx", "w_ada", "b_ada", "norm_g", "rec_w_in", "rec_conv_w", "rec_conv_b", "rec_lambda", "rec_w_a",
             "rec_b_a", "rec_w_x", "rec_b_x", "rec_w_out", "conf_w_pw1", "conf_b_pw1", "conf_conv_w", "conf_conv_b",
             "conf_ln_g", "conf_ln_b", "conf_w_pw2", "conf_b_pw2", "mlp_w_in", "mlp_w_out", "final_g"]
    w = dict(zip(names, [c_ctx, w_ada, b_ada, norm_g, rec_w_in, rec_conv_w, rec_conv_b, rec_lambda, rec_w_a,
                         rec_b_a, rec_w_x, rec_b_x, rec_w_out, conf_w_pw1, conf_b_pw1, conf_conv_w, conf_conv_b,
                         conf_ln_g, conf_ln_b, conf_w_pw2, conf_b_pw2, mlp_w_in, mlp_w_out, final_g]))
    m = dict(zip(names, [m_c_ctx, m_w_ada, m_b_ada, m_norm_g, m_rec_w_in, m_rec_conv_w, m_rec_conv_b, m_rec_lambda,
                         m_rec_w_a, m_rec_b_a, m_rec_w_x, m_rec_b_x, m_rec_w_out, m_conf_w_pw1, m_conf_b_pw1,
                         m_conf_conv_w, m_conf_conv_b, m_conf_ln_g, m_conf_ln_b, m_conf_w_pw2, m_conf_b_pw2,
                         m_mlp_w_in, m_mlp_w_out, m_final_g]))
    v = dict(zip(names, [v_c_ctx, v_w_ada, v_b_ada, v_norm_g, v_rec_w_in, v_rec_conv_w, v_rec_conv_b, v_rec_lambda,
                         v_rec_w_a, v_rec_b_a, v_rec_w_x, v_rec_b_x, v_rec_w_out, v_conf_w_pw1, v_conf_b_pw1,
                         v_conf_conv_w, v_conf_conv_b, v_conf_ln_g, v_conf_ln_b, v_conf_w_pw2, v_conf_b_pw2,
                         v_mlp_w_in, v_mlp_w_out, v_final_g]))
    mx, my, mc = _me()
    chip = 2 * mx + my
    me = 4 * mx + 2 * my + mc

    sharded_small = ["norm_g", "rec_conv_w", "rec_lambda", "conf_b_pw1", "conf_conv_w", "conf_conv_b", "conf_ln_g",
                     "conf_ln_b", "conf_b_pw2"]
    packed, offs = _pack([c] + [w[k] for k in sharded_small], 8)
    got = _allgather8("gather_small", packed)

    place = jnp.stack([chip, mc]).astype(jnp.int32)
    shards = [_halves(rec_w_in[0]), _halves(rec_w_out[0]), _halves(conf_w_pw1[0]), _halves(conf_w_pw2[0]),
              _halves(mlp_w_in[0]), _halves(mlp_w_in[1]), _halves(mlp_w_out[0]), _halves(mlp_w_out[1])]
    use_order = dict(rec=(0, 1), mlp0=(4, 6), conf=(2, 3), mlp1=(5, 7))
    slots = _place_big(shards, place)
    flying, gsems = {}, {}
    fly, sems, rec_started = _gather_start("gather_start_rec", [slots[t] for t in use_order["rec"]], ((0, 1),), got)
    flying["rec"], gsems["rec"] = fly, sems

    def wg(group, after):
        bufs = _gather_wait(f"gather_wait_{group}", flying[group], *gsems[group], after)
        a, b = _swap_halves(f"swap_{group}", bufs)
        if group == "rec":
            return dict(rec_w_in=a.reshape(4, D, RH), rec_w_out=b.reshape(R, D))
        if group == "conf":
            return dict(conf_w_pw1=a.reshape(4, D, D // 2), conf_w_pw2=b.reshape(D, D))
        return dict(w_in=a.reshape(4, D, D), w_out=b.reshape(FF, D))

    per_dev =[_unpack(got[d], offs) for d in range(8)]
    c_rows = jnp.concatenate([per_dev[d][0].reshape(1, D) for d in range(8)], axis=0)
    full = {k: jnp.concatenate([per_dev[2 * j][1 + i] for j in range(4)], axis=-1)
            for i, k in enumerate(sharded_small)}
    c16 = jnp.concatenate([c_rows, c_ctx.reshape(1, D), jnp.zeros((7, D), F32)], axis=0)

    ns = w_ada.shape[2]
    b_shard = lax.dynamic_slice_in_dim(b_ada, chip * ns, ns, axis=1).reshape(2, 1, ns)
    prod = _ada_fwd(c16, w_ada, b_shard)
    prod8 = _allgather8("gather_mod", prod.reshape(32, ns), rec_started)
    later = ("mlp0", "conf", "mlp1")
    fly, sems, all_started = _gather_start("gather_start_rest", [slots[t] for g in later for t in use_order[g]],
                                           ((0, 1), (2, 3), (4, 5)), prod8)
    for gi, g in enumerate(later):
        flying[g], gsems[g] = fly[2 * gi:2 * gi + 2], sems[2 * gi:2 * gi + 2]
    prod8 = prod8.reshape(8, 2, 16, ns)
    mod_all = jnp.concatenate([prod8[2 * j] for j in range(4)], axis=-1)
    mods = lax.dynamic_index_in_dim(mod_all, me, axis=1, keepdims=False).reshape(2, 6, D)
    cmods = mod_all[0, 8].reshape(6, D)[:2]

    rec = dict(conv_w=full["rec_conv_w"][0], conv_b=rec_conv_b[0], lam=full["rec_lambda"][0],
               w_a=rec_w_a[0], b_a=rec_b_a[0], w_x=rec_w_x[0], b_x=rec_b_x[0])
    conf = dict(b_pw1=full["conf_b_pw1"][0], conv_w=full["conf_conv_w"][0], conv_b=full["conf_conv_b"][0],
                ln_g=full["conf_ln_g"][0], ln_b=full["conf_ln_b"][0], b_pw2=full["conf_b_pw2"][0])
    sent = {}

    def on_grads(group, dws):
        parts = [dw.reshape(4, 2, shards[t].shape[1], shards[t].shape[2]) for dw, t in zip(dws, use_order[group])]
        sent[group], token = _reduce_begin(group, parts, place)
        return token

    loss_local, grad_x, _, small = _local_step(x[0], ctx[0], loss_target[0], mods, cmods, full["norm_g"], final_g,
                                               rec, conf, wg, on_grads, all_started)
    loss = lax.psum(loss_local, ("x", "y", "c"))

    small_names = ["d_mod", "d_cmod", "norm_g", "rec_conv_w", "rec_conv_b", "rec_lambda", "rec_w_a", "rec_b_a",
                   "rec_w_x", "rec_b_x", "conf_b_pw1", "conf_conv_w", "conf_conv_b", "conf_ln_g", "conf_ln_b",
                   "conf_b_pw2", "final_g"]
    mine = lax.broadcasted_iota(jnp.int32, (8, 1), 0) == me
    mod_slots = jnp.where(mine, small["d_mod"].reshape(1, -1), 0.0)
    spacked, soffs = _pack([small[k] for k in small_names] + [mod_slots])
    small_state, small_started = _allreduce_small_begin(spacked, place)

    fulls = {}
    for group in ("mlp1", "conf", "mlp0", "rec"):
        for t, f in zip(use_order[group], _reduce_end(group, sent[group], place, small_started)):
            fulls[t] = f
    whole = _share_halves("share_grads", [fulls[t] for t in range(8)])
    g_big = dict(rec_w_in=whole[0].reshape(rec_w_in.shape), rec_w_out=whole[1].reshape(rec_w_out.shape),
                 conf_w_pw1=whole[2].reshape(conf_w_pw1.shape), conf_w_pw2=whole[3].reshape(conf_w_pw2.shape),
                 mlp_w_in=jnp.stack([whole[4].reshape(D, D), whole[5].reshape(D, D)]),
                 mlp_w_out=jnp.stack([whole[6].reshape(D, D), whole[7].reshape(D, D)]))
    delta, new_m, new_v = {}, {}, {}

    def adamw_of(k, g):
        cols = w[k].shape[-1]
        d_, m_, v_ = _adamw(f"adamw_{k}", w[k].reshape(-1, cols), g.reshape(-1, cols),
                            m[k].reshape(-1, cols), v[k].reshape(-1, cols))
        delta[k], new_m[k], new_v[k] = (a.reshape(w[k].shape) for a in (d_, m_, v_))

    for k in _BIG:
        adamw_of(k, g_big[k])

    unpacked = _unpack(_allreduce_small_end(small_state, new_v[_BIG[-1]]), soffs)
    ssum = dict(zip(small_names, unpacked[:-1]))
    dmod_rows = unpacked[-1].reshape(8, 2, 6 * D).transpose(1, 0, 2)

    d_cmod_full =jnp.concatenate([ssum["d_cmod"].reshape(1, 2 * D), jnp.zeros((1, 4 * D), F32)], axis=1)
    dm16 = jnp.concatenate([dmod_rows, jnp.stack([d_cmod_full, jnp.zeros((1, 6 * D), F32)]),
                            jnp.zeros((2, 7, 6 * D), F32)], axis=1)
    dm16_shard = lax.dynamic_slice_in_dim(dm16, chip * ns, ns, axis=2)
    g_w_ada, ds_part = _ada_bwd(c16, dm16_shard, w_ada)
    ds8 = _allgather8("gather_dsilu", ds_part)
    g_c_ctx = _cctx_grad(ds8, c_ctx).reshape(D)
    g_b_ada = ssum["d_mod"] + jnp.stack([d_cmod_full[0], jnp.zeros((6 * D,), F32)])

    def shard_of(a, axis):
        n = a.shape[axis] // 4
        return lax.dynamic_slice_in_dim(a, chip * n, n, axis=axis)

    grads = dict(
        c_ctx=g_c_ctx, w_ada=g_w_ada, b_ada=g_b_ada,
        norm_g=shard_of(ssum["norm_g"].reshape(2, 2, D), 2),
        rec_w_in=g_big["rec_w_in"], rec_conv_w=shard_of(ssum["rec_conv_w"].reshape(1, REC_KW, R), 2),
        rec_conv_b=ssum["rec_conv_b"].reshape(1, R), rec_lambda=shard_of(ssum["rec_lambda"].reshape(1, 2, R), 2),
        rec_w_a=ssum["rec_w_a"].reshape(rec_w_a.shape), rec_b_a=ssum["rec_b_a"].reshape(rec_b_a.shape),
        rec_w_x=ssum["rec_w_x"].reshape(rec_w_x.shape), rec_b_x=ssum["rec_b_x"].reshape(rec_b_x.shape),
        rec_w_out=g_big["rec_w_out"], conf_w_pw1=g_big["conf_w_pw1"],
        conf_b_pw1=shard_of(ssum["conf_b_pw1"].reshape(1, 2 * D), 1),
        conf_conv_w=shard_of(ssum["conf_conv_w"].reshape(1, CONF_KW, D), 2),
        conf_conv_b=shard_of(ssum["conf_conv_b"].reshape(1, D), 1),
        conf_ln_g=shard_of(ssum["conf_ln_g"].reshape(1, D), 1), conf_ln_b=shard_of(ssum["conf_ln_b"].reshape(1, D), 1),
        conf_w_pw2=g_big["conf_w_pw2"], conf_b_pw2=shard_of(ssum["conf_b_pw2"].reshape(1, D), 1),
        mlp_w_in=g_big["mlp_w_in"], mlp_w_out=g_big["mlp_w_out"], final_g=ssum["final_g"].reshape(D))

    adamw_of("w_ada", g_w_ada)
    rest = [k for k in names if k not in ("w_ada",) + _BIG]
    pw, poffs = _pack([w[k] for k in rest])
    pg, _ = _pack([grads[k] for k in rest])
    pm, _ = _pack([m[k] for k in rest])
    pv, _ = _pack([v[k] for k in rest])
    d_, m_, v_ = _adamw("adamw_small", pw, pg, pm, pv)
    for k, dd, mm, vv in zip(rest, _unpack(d_, poffs), _unpack(m_, poffs), _unpack(v_, poffs)):
        delta[k], new_m[k], new_v[k] = dd, mm, vv

    return (loss, grad_x[None], *[grads[k] for k in names], *[delta[k] for k in names],
            *[new_m[k] for k in names], *[new_v[k] for k in names])
```

```python
import functools
import math

import jax
import jax.numpy as jnp
from jax import lax
from jax.experimental import pallas as pl
from jax.experimental.pallas import tpu as pltpu

F32 = jnp.float32
BF16 = jnp.bfloat16

D = 1024
T = 2048
TC = 256
TA = T + TC
R = 1280
RH = R // 2
NQ = 4 * RH
FF = 4096
N_BLK = 16
BLK = R // N_BLK
GRID_W = 64
EPS = 1e-6
RG_C = 8.0
CONF_KW = 31
REC_KW = 4
LANE = 128
ROW_TILE = 256
HALO = 16
RG_TILE = 128
PACK_ROWS = 512
MM_TILE = 1024
REC_TILE = TA // 2
CW_REC = 640
CW_CONF = 512
V7X_VMEM_BYTES = 64 * 1024 * 1024
VMEM_LIMIT = V7X_VMEM_BYTES - 8 * 1024 * 1024

ADAM_LR = 0.001
ADAM_B1 = 0.9
ADAM_B2 = 0.999
ADAM_EPS = 1e-08
ADAM_WD = 0.01
ADAM_STEP = 10

MESH = pl.DeviceIdType.MESH
ANY = pl.BlockSpec(memory_space=pl.ANY)


def _sds(shape, dtype):
    return jax.ShapeDtypeStruct(tuple(shape), dtype)


def _pcall(body, **kw):
    return pl.pallas_call(body, **kw)


def _cparams():
    return pltpu.CompilerParams(vmem_limit_bytes=VMEM_LIMIT)


def _full_spec(arr):
    nd = arr.ndim
    return pl.BlockSpec(arr.shape, lambda *ids, _n=nd: (0,) * _n)


def _sum0(v):
    return jnp.sum(v, axis=0, keepdims=True)


def _tiled(name, fn, grid, ins, vecs, outs, vec_outs=(), vec_refs=False):
    n_in, n_vec, n_out = len(ins), len(vecs), len(outs)
    n_grid = len(grid)

    def kern(*refs):
        ids = [pl.program_id(a) for a in range(n_grid)]
        tin = [r[...] for r in refs[:n_in]]
        vin = list(refs[n_in:n_in + n_vec]) if vec_refs else [r[...] for r in refs[n_in:n_in + n_vec]]
        o_refs = refs[n_in + n_vec:n_in + n_vec + n_out]
        a_refs = refs[n_in + n_vec + n_out:]
        tout, incs = fn(ids, tin, vin)
        for r, v in zip(o_refs, tout):
            r[...] = v.astype(r.dtype)
        if a_refs:
            first = functools.reduce(jnp.logical_and, [i == 0 for i in ids])

            @pl.when(first)
            def _():
                for r in a_refs:
                    r[...] = jnp.zeros_like(r)

            for r, v in zip(a_refs, incs):
                r[...] += v

    out_shape = [o for o, _ in outs] + [_sds(s, F32) for s in vec_outs]
    out_specs = [s for _, s in outs] + [
        pl.BlockSpec(tuple(s), lambda *ids, _n=len(s): (0,) * _n) for s in vec_outs]
    res = _pcall(
        kern, name=name, grid=tuple(grid),
        in_specs=[s for _, s in ins] + [_full_spec(v) for v in vecs],
        out_specs=out_specs, out_shape=out_shape, compiler_params=_cparams(),
    )(*[a for a, _ in ins], *vecs)
    return list(res)


def _rows(arr, ncols=None, tm=ROW_TILE, off=0, col=0, clamp_lo=False):
    ncols = arr.shape[1] if ncols is None else ncols
    if clamp_lo:
        return arr, pl.BlockSpec((tm, ncols), lambda i: (jnp.maximum(i + off, 0), col))
    return arr, pl.BlockSpec((tm, ncols), lambda i: (i + off, col))


def _orow(nrows, ncols, dtype, tm=ROW_TILE, off=0, clamp_lo=False):
    if clamp_lo:
        return _sds((nrows, ncols), dtype), pl.BlockSpec((tm, ncols), lambda i: (jnp.maximum(i + off, 0), 0))
    return _sds((nrows, ncols), dtype), pl.BlockSpec((tm, ncols), lambda i: (i + off, 0))


_NN = (((1,), (0,)), ((), ()))
_TN = (((0,), (0,)), ((), ()))
_NT = (((1,), (1,)), ((), ()))


def _mm(name, a, b, dims, grid, a_spec, b_spec, out, acc_shape, extra=(), a_pre=None, epi=None):
    n_k = grid[2]
    n_ex = len(extra)

    def kern(a_ref, b_ref, *rest):
        ex = rest[:n_ex]
        o_refs = rest[n_ex:n_ex + len(out)]
        k = pl.program_id(2)
        av = a_ref[...]
        if a_pre is not None:
            av = a_pre(av)
        part = lax.dot_general(av.astype(BF16), b_ref[...].astype(BF16), dims, preferred_element_type=F32)

        def finish(total):
            vals = [total] if epi is None else epi(total, [e[...] for e in ex])
            for r, v in zip(o_refs, vals):
                r[...] = v.astype(r.dtype)

        if n_k == 1:
            finish(part)
        else:
            acc = rest[-1]

            @pl.when(k == 0)
            def _():
                acc[...] = part

            @pl.when(jnp.logical_and(k > 0, k < n_k - 1))
            def _():
                acc[...] += part

            @pl.when(k == n_k - 1)
            def _():
                finish(acc[...] + part)

    res = _pcall(
        kern, name=name, grid=tuple(grid),
        in_specs=[a_spec, b_spec] + [s for _, s in extra],
        out_specs=[s for _, s in out], out_shape=[o for o, _ in out],
        scratch_shapes=[] if n_k == 1 else [pltpu.VMEM(tuple(acc_shape), F32)], compiler_params=_cparams(),
    )(a, b, *[e for e, _ in extra])
    return list(res)


def _rms(x):
    r = lax.rsqrt(jnp.mean(x * x, axis=-1, keepdims=True) + EPS)
    return x * r, r


def _norm_mod(x, g, sc, sh):
    n, _ = _rms(x)
    return (n * g) * (1.0 + sc) + sh


def _norm_mod_bwd(dh, x, g, sc):
    n, r = _rms(x)
    d_sh = _sum0(dh)
    d_sc = _sum0(dh * (n * g))
    d_g = _sum0(dh * (1.0 + sc) * n)
    dn = dh * (g * (1.0 + sc))
    dx = r * (dn - n * jnp.mean(dn * n, axis=-1, keepdims=True))
    return dx, d_sh, d_sc, d_g


_GELU_K = math.sqrt(2.0 / math.pi)


def _gelu(x):
    t = jnp.tanh(_GELU_K * (x + 0.044715 * x * x * x))
    return 0.5 * x * (1.0 + t), t


def _gelu_grad(x, t):
    return 0.5 * (1.0 + t) + 0.5 * x * (1.0 - t * t) * (_GELU_K * (1.0 + 3.0 * 0.044715 * x * x))


def _sigmoid(x):
    return 0.5 * jnp.tanh(0.5 * x) + 0.5


def _expm1(x):
    p = jnp.full_like(x, 1.0 / 5040.0)
    for c in (1.0 / 720.0, 1.0 / 120.0, 1.0 / 24.0, 1.0 / 6.0, 0.5, 1.0):
        p = p * x + c
    return jnp.where(jnp.abs(x) < 0.3, x * p, jnp.exp(x) - 1.0)


def _softplus_neg(lam):
    return jnp.log1p(jnp.exp(-jnp.abs(lam))) + jnp.maximum(-lam, 0.0)


def _layernorm_parts(x):
    mu = jnp.mean(x, axis=-1, keepdims=True)
    xc = x - mu
    rstd = lax.rsqrt(jnp.mean(xc * xc, axis=-1, keepdims=True) + EPS)
    return xc * rstd, rstd


def _rg_gates(u, wbd, gbias, lam):
    sp = _softplus_neg(lam)
    parts = {}
    for h in range(2):
        uh = u[:, h * RH:(h + 1) * RH]
        g = jnp.dot(uh.astype(BF16), wbd[h], preferred_element_type=F32) + gbias[:, h * NQ:(h + 1) * NQ]
        for d in range(2):
            r = _sigmoid(g[:, (2 * d) * RH:(2 * d + 1) * RH])
            i = _sigmoid(g[:, (2 * d + 1) * RH:(2 * d + 2) * RH])
            sph = sp[d:d + 1, h * RH:(h + 1) * RH]
            la = (-RG_C) * r * sph
            e2 = _expm1(2.0 * la)
            inv_mult = jnp.where(e2 < 0.0, lax.rsqrt(-e2), 0.0)
            parts[(d, h)] = dict(r=r, i=i, la=la, a=jnp.exp(la), e2=e2, mult=-e2 * inv_mult, inv_mult=inv_mult,
                                 uh=uh, sp=sph)
    return parts


def _rg_fwd_fn(ids, tin, vin):
    (u,) = tin
    wbd = vin[0]
    parts = _rg_gates(u, wbd, vin[1][...], vin[2][...])
    outs = []
    for d in range(2):
        a = jnp.concatenate([parts[(d, h)]["a"] for h in range(2)], axis=1)
        b = jnp.concatenate([parts[(d, h)]["mult"] * parts[(d, h)]["i"] * parts[(d, h)]["uh"]
                             for h in range(2)], axis=1)
        outs += [a, b]
    return outs, []


def _rg_bwd_fn(ids, tin, vin):
    u, da_f, db_f, da_r, db_r = tin
    wbd, lam = vin[0], vin[2][...]
    parts = _rg_gates(u, wbd, vin[1][...], lam)
    dab = ((da_f, db_f), (da_r, db_r))
    dsig_lam = -1.0 / (1.0 + jnp.exp(lam))
    du_halves, dpre_halves, dlam = [], [], [[None, None], [None, None]]
    for h in range(2):
        du = jnp.zeros_like(parts[(0, h)]["uh"])
        dpre = []
        for d in range(2):
            p = parts[(d, h)]
            da = dab[d][0][:, h * RH:(h + 1) * RH]
            db = dab[d][1][:, h * RH:(h + 1) * RH]
            d_mult = db * p["i"] * p["uh"]
            d_i = db * p["mult"] * p["uh"]
            du = du + db * p["mult"] * p["i"]
            d_la = da * p["a"] - d_mult * (p["e2"] + 1.0) * p["inv_mult"]
            d_r = d_la * ((-RG_C) * p["sp"])
            dlam[d][h] = _sum0(d_la * ((-RG_C) * p["r"])) * dsig_lam[d:d + 1, h * RH:(h + 1) * RH]
            dpre += [d_r * p["r"] * (1.0 - p["r"]), d_i * p["i"] * (1.0 - p["i"])]
        dpre = jnp.concatenate(dpre, axis=1)
        du = du + lax.dot_general(dpre.astype(BF16), wbd[h], _NT, preferred_element_type=F32)
        du_halves.append(du)
        dpre_halves.append(dpre)
    dpre_all = jnp.concatenate(dpre_halves, axis=1)
    dlam_row = jnp.concatenate([dlam[0][0], dlam[0][1], dlam[1][0], dlam[1][1]], axis=1)
    return [dpre_all, jnp.concatenate(du_halves, axis=1)], [_sum0(dpre_all), dlam_row]


def _tile_flags(i, n_tiles, seq_starts):
    starts_here = functools.reduce(jnp.logical_or, [i == s for s in seq_starts])
    ends_here = functools.reduce(jnp.logical_or, [i + 1 == s for s in seq_starts] + [i + 1 == n_tiles])
    return jnp.logical_not(starts_here), jnp.logical_not(ends_here)


def _halo_specs(col0, cw):
    hb = ROW_TILE // HALO
    prev = pl.BlockSpec((HALO, cw), lambda i, c: (jnp.maximum(i * hb - 1, 0), col0 + c))
    cur = pl.BlockSpec((ROW_TILE, cw), lambda i, c: (i, col0 + c))
    return prev, cur, hb


def _window(prev_ref, cur_ref, next_ref, has_prev, has_next):
    prev = jnp.where(has_prev, prev_ref[...], 0.0)
    nxt = jnp.where(has_next, next_ref[...], 0.0)
    return jnp.concatenate([prev, cur_ref[...], nxt], axis=0)


def _tap_reader(win):
    sub = 8
    n = win.shape[0]
    shifted = {0: win}

    def tap(off):
        s = off % sub
        if s not in shifted:
            shifted[s] = pltpu.roll(win, n - s, axis=0)
        return shifted[s][off - s:off - s + ROW_TILE, :]

    return tap


def _dwconv(name, x, col0, w, bias, pad_left, seq_starts, n_ch, cw=256):
    n_rows = x.shape[0]
    n_tiles = n_rows // ROW_TILE
    n_taps = w.shape[0]
    prev_spec, cur_spec, hb = _halo_specs(col0, cw)
    last_hb = n_rows // HALO - 1
    next_spec = pl.BlockSpec((HALO, cw), lambda i, c: (jnp.minimum((i + 1) * hb, last_hb), col0 + c))

    def kern(prev_ref, cur_ref, next_ref, w_ref, b_ref, o_ref):
        has_prev, has_next = _tile_flags(pl.program_id(0), n_tiles, seq_starts)
        win = _window(prev_ref, cur_ref, next_ref, has_prev, has_next)
        tap = _tap_reader(win)
        wv = w_ref[...]
        acc = jnp.zeros((ROW_TILE, cw), F32) + b_ref[...]
        for k in range(n_taps):
            acc = acc + wv[k:k + 1, :] * tap(HALO + k - pad_left)
        o_ref[...] = acc

    return _pcall(
        kern, name=name, grid=(n_tiles, n_ch // cw),
        in_specs=[prev_spec, cur_spec, next_spec,
                  pl.BlockSpec((n_taps, cw), lambda i, c: (0, c)), pl.BlockSpec((1, cw), lambda i, c: (0, c))],
        out_specs=pl.BlockSpec((ROW_TILE, cw), lambda i, c: (i, c)),
        out_shape=_sds((n_rows, n_ch), F32), compiler_params=_cparams(),
    )(x, x, x, w, bias)


def _dwconv_wgrad(name, dy, x, col0, n_taps, pad_left, seq_starts, n_ch, cw=256):
    n_rows = dy.shape[0]
    n_tiles = n_rows // ROW_TILE
    n_out = -(-(n_taps + 1) // 8) * 8
    prev_spec, cur_spec, hb = _halo_specs(col0, cw)
    last_hb = n_rows // HALO - 1
    next_spec = pl.BlockSpec((HALO, cw), lambda c, i: (jnp.minimum((i + 1) * hb, last_hb), col0 + c))
    prev_spec = pl.BlockSpec((HALO, cw), lambda c, i: (jnp.maximum(i * hb - 1, 0), col0 + c))
    cur_spec = pl.BlockSpec((ROW_TILE, cw), lambda c, i: (i, col0 + c))

    def kern(dy_ref, prev_ref, cur_ref, next_ref, o_ref):
        i = pl.program_id(1)
        has_prev, has_next = _tile_flags(i, n_tiles, seq_starts)
        win = _window(prev_ref, cur_ref, next_ref, has_prev, has_next)
        dyv = dy_ref[...]
        tap = _tap_reader(win)
        rid = lax.broadcasted_iota(jnp.int32, (n_out, cw), 0)
        inc = jnp.where(rid == n_taps, _sum0(dyv), 0.0)
        for k in range(n_taps):
            inc = inc + jnp.where(rid == k, _sum0(dyv * tap(HALO + k - pad_left)), 0.0)

        @pl.when(i == 0)
        def _():
            o_ref[...] = jnp.zeros_like(o_ref)

        o_ref[...] += inc

    return _pcall(
        kern, name=name, grid=(n_ch // cw, n_tiles),
        in_specs=[pl.BlockSpec((ROW_TILE, cw), lambda c, i: (i, c)), prev_spec, cur_spec, next_spec],
        out_specs=pl.BlockSpec((n_out, cw), lambda c, i: (0, c)),
        out_shape=_sds((n_out, n_ch), F32), compiler_params=_cparams(),
    )(dy, x, x, x)


N_SCAN = TA // ROW_TILE


def _rev_block(j):
    return jnp.where(j == 0, 0, N_SCAN - j)


def _scan_fwd(a_f, b_f, a_r, b_r):
    fwd_spec = pl.BlockSpec((ROW_TILE, R), lambda i: (i, 0))
    rev_spec = pl.BlockSpec((ROW_TILE, R), lambda i: (_rev_block(i), 0))
    hin_spec = pl.BlockSpec((None, 1, R), lambda i: (i, 0, 0))

    def kern(af, bf, ar, br, yf, yr, hin_f, hin_r, hf_s, hr_s):
        @pl.when(pl.program_id(0) == 0)
        def _():
            hf_s[...] = jnp.zeros_like(hf_s)
            hr_s[...] = jnp.zeros_like(hr_s)

        hin_f[...] = hf_s[...]
        hin_r[...] = hr_s[...]

        def step(s8, carry):
            hf, hr = carry
            t0 = pl.multiple_of(s8 * 8, 8)
            for q in range(8):
                tf = t0 + q
                hf = af[pl.ds(tf, 1), :] * hf + bf[pl.ds(tf, 1), :]
                yf[pl.ds(tf, 1), :] = hf
                tr = ROW_TILE - 1 - tf
                hr = ar[pl.ds(tr, 1), :] * hr + br[pl.ds(tr, 1), :]
                yr[pl.ds(tr, 1), :] = hr
            return hf, hr

        hf, hr = lax.fori_loop(0, ROW_TILE // 8, step, (hf_s[...], hr_s[...]))
        hf_s[...] = hf
        hr_s[...] = hr

    return _pcall(
        kern, name="scan_fwd", grid=(N_SCAN,),
        in_specs=[fwd_spec, fwd_spec, rev_spec, rev_spec],
        out_specs=[fwd_spec, rev_spec, hin_spec, hin_spec],
        out_shape=[_sds((TA, R), F32), _sds((TA, R), F32), _sds((N_SCAN, 1, R), F32), _sds((N_SCAN, 1, R), F32)],
        scratch_shapes=[pltpu.VMEM((1, R), F32), pltpu.VMEM((1, R), F32)], compiler_params=_cparams(),
    )(a_f, b_f, a_r, b_r)


def _scan_bwd(dy, a_f, y_f, hin_f, a_r, y_r, hin_r):
    fwd_spec = pl.BlockSpec((ROW_TILE, R), lambda i: (N_SCAN - 1 - i, 0))
    rev_spec = pl.BlockSpec((ROW_TILE, R), lambda i: (_rev_block(N_SCAN - 1 - i), 0))
    hin_spec = pl.BlockSpec((None, 1, R), lambda i: (N_SCAN - 1 - i, 0, 0))
    last = ROW_TILE - 1

    def kern(dyf, af, yf, hf0, dyr, ar, yr, hr0, daf, dbf, dar, dbr, gf_s, anf_s, gr_s, anr_s):
        @pl.when(pl.program_id(0) == 0)
        def _():
            for r in (gf_s, anf_s, gr_s, anr_s):
                r[...] = jnp.zeros_like(r)

        def one(dy_ref, a_ref, y_ref, da_ref, db_ref, g, an, p, pprev):
            gnew = dy_ref[pl.ds(p, 1), :] + an * g
            db_ref[pl.ds(p, 1), :] = gnew
            da_ref[pl.ds(p, 1), :] = gnew * y_ref[pl.ds(pprev, 1), :]
            return gnew, a_ref[pl.ds(p, 1), :]

        def step(s8, carry):
            gf, anf, gr, anr = carry
            base = s8 * 8
            for q in range(8):
                s = last - (base + q)
                gf, anf = one(dyf, af, yf, daf, dbf, gf, anf, s, s - 1)
                gr, anr = one(dyr, ar, yr, dar, dbr, gr, anr, last - s, last - s + 1)
            return gf, anf, gr, anr

        carry = (gf_s[...], anf_s[...], gr_s[...], anr_s[...])
        carry = lax.fori_loop(0, ROW_TILE // 8 - 1, step, carry)
        gf, anf, gr, anr = carry
        for s in range(7, 0, -1):
            gf, anf = one(dyf, af, yf, daf, dbf, gf, anf, s, s - 1)
            gr, anr = one(dyr, ar, yr, dar, dbr, gr, anr, last - s, last - s + 1)
        gf0 = dyf[0:1, :] + anf * gf
        dbf[0:1, :] = gf0
        daf[0:1, :] = gf0 * hf0[...]
        gr0 = dyr[last:last + 1, :] + anr * gr
        dbr[last:last + 1, :] = gr0
        dar[last:last + 1, :] = gr0 * hr0[...]
        gf_s[...] = gf0
        anf_s[...] = af[0:1, :]
        gr_s[...] = gr0
        anr_s[...] = ar[last:last + 1, :]

    return _pcall(
        kern, name="scan_bwd", grid=(N_SCAN,),
        in_specs=[fwd_spec, fwd_spec, fwd_spec, hin_spec, rev_spec, rev_spec, rev_spec, hin_spec],
        out_specs=[fwd_spec, fwd_spec, rev_spec, rev_spec],
        out_shape=[_sds((TA, R), F32)] * 4,
        scratch_shapes=[pltpu.VMEM((1, R), F32)] * 4, compiler_params=_cparams(),
    )(dy, a_f, y_f, hin_f, dy, a_r, y_r, hin_r)


def _me():
    return lax.axis_index("x"), lax.axis_index("y"), lax.axis_index("c")


def _other_chips(mx, my):
    return [(1 - mx, my), (mx, 1 - my), (1 - mx, 1 - my)]


def _rcopy(src, dst, ssem, rsem, dev):
    return pltpu.make_async_remote_copy(src_ref=src, dst_ref=dst, send_sem=ssem, recv_sem=rsem,
                                        device_id=dev, device_id_type=MESH)


def _allgather8(name, x, dep=None):
    rows, cols = x.shape
    n_dep = len(_behind(dep))

    def kern(x_ref, *rest):
        o_ref, ssem, rsem, lsem = rest[n_dep:]
        mx, my, mc = _me()
        me = 4 * mx + 2 * my + mc
        peers = []
        for k in range(1, 8):
            px = 1 - mx if (k >> 2) & 1 else mx
            py = 1 - my if (k >> 1) & 1 else my
            pc = 1 - mc if k & 1 else mc
            peers.append((px, py, pc))
        mine = pltpu.make_async_copy(x_ref, o_ref.at[me], lsem)
        mine.start()
        sends = [_rcopy(x_ref, o_ref.at[me], ssem.at[k], rsem.at[k], p) for k, p in enumerate(peers)]
        for cp in sends:
            cp.start()
        for k, (px, py, pc) in enumerate(peers):
            _rcopy(x_ref, o_ref.at[4 * px + 2 * py + pc], ssem.at[k], rsem.at[k], (px, py, pc)).wait_recv()
        for cp in sends:
            cp.wait_send()
        mine.wait()

    return _pcall(
        kern, name=name, in_specs=[ANY] * (1 + n_dep), out_specs=ANY, out_shape=_sds((8, rows, cols), F32),
        scratch_shapes=[pltpu.SemaphoreType.DMA((7,)), pltpu.SemaphoreType.DMA((7,)), pltpu.SemaphoreType.DMA(())],
    )(x, *_behind(dep))


def _reduce_pair(name, gs):
    n = len(gs)

    def kern(*refs):
        g, o = refs[:n], refs[n:2 * n]
        ss, rs = refs[2 * n:]
        mx, my, mc = _me()
        sib = (mx, my, 1 - mc)
        sends = []
        for t in range(n):
            for j in range(4):
                cp = _rcopy(g[t].at[j, 1 - mc], o[t].at[j], ss.at[4 * t + j], rs.at[4 * t + j], sib)
                cp.start()
                sends.append(cp)
        for cp in sends:
            cp.wait_recv()
        for cp in sends:
            cp.wait_send()

    dma = pltpu.SemaphoreType.DMA
    return _pcall(
        kern, name=name, in_specs=[ANY] * n, out_specs=[ANY] * n,
        out_shape=[_sds((4,) + g.shape[2:], g.dtype) for g in gs],
        scratch_shapes=[dma((4 * n,)), dma((4 * n,))],
    )(*gs)


def _share_halves(name, fulls):
    n = len(fulls)

    def kern(*refs):
        o = refs[n:2 * n]
        ss, rs = refs[2 * n:]
        mx, my, mc = _me()
        sib = (mx, my, 1 - mc)
        sends = []
        for t in range(n):
            cp = _rcopy(o[t].at[mc], o[t].at[mc], ss.at[t], rs.at[t], sib)
            cp.start()
            sends.append(cp)
        for t in range(n):
            _rcopy(o[t].at[1 - mc], o[t].at[1 - mc], ss.at[t], rs.at[t], sib).wait_recv()
        for cp in sends:
            cp.wait_send()

    dma = pltpu.SemaphoreType.DMA
    return _pcall(
        kern, name=name, in_specs=[ANY] * n, out_specs=[ANY] * n,
        out_shape=[_sds(f.shape, f.dtype) for f in fulls], input_output_aliases={t: t for t in range(n)},
        scratch_shapes=[dma((n,)), dma((n,))],
    )(*fulls)


def _tiled_sp(name, fn, grid, sp, ins, outs):
    n_in = len(ins)

    def kern(sp_ref, *refs):
        tout = fn([r[...] for r in refs[:n_in]])
        for r, v in zip(refs[n_in:], tout):
            r[...] = v.astype(r.dtype)

    gs = pltpu.PrefetchScalarGridSpec(num_scalar_prefetch=1, grid=tuple(grid),
                                      in_specs=[s for _, s in ins], out_specs=[s for _, s in outs])
    res = _pcall(kern, name=name, grid_spec=gs, out_shape=[o for o, _ in outs], compiler_params=_cparams(),
                 )(sp, *[a for a, _ in ins])
    return list(res)


def _row_tile(rows, cols, itemsize=4, budget=2 * 1024 * 1024):
    tr = rows
    while tr * cols * itemsize > budget and tr % 32 == 0:
        tr //= 2
    return tr


def _place_big(shards, place):
    slots = []
    for t, s in enumerate(shards):
        rr, cc = s.shape[1], s.shape[2]
        tr = _row_tile(rr, cc)
        (slot,) = _tiled_sp(
            f"place{t}", lambda tin: [tin[0]], (2, rr // tr), place,
            [(s, pl.BlockSpec((None, tr, cc), lambda h, i, sp: (h, i, 0)))],
            [(_sds((4, 2, rr, cc), BF16), pl.BlockSpec((None, None, tr, cc), lambda h, i, sp: (sp[0], h, i, 0)))])
        slots.append(slot)
    return slots


def _allreduce_small_begin(vec, place):
    hr = vec.shape[0] // 2
    tr = _row_tile(hr, LANE)
    blk = (None, None, tr, LANE)
    (pair,) = _tiled_sp(
        "small_place", lambda tin: [tin[0]], (2, hr // tr), place,
        [(vec.reshape(2, hr, LANE), pl.BlockSpec((None, tr, LANE), lambda h, i, sp: (h, i, 0)))],
        [(_sds((2, 2, hr, LANE), F32), pl.BlockSpec(blk, lambda h, i, sp: (sp[1], h, i, 0)))])
    (pair,) = _share_halves("small_share", [pair])
    (slot,) = _tiled_sp(
        "small_pair_add", lambda tin: [tin[0] + tin[1]], (2, hr // tr), place,
        [(pair, pl.BlockSpec(blk, lambda h, i, sp: (0, h, i, 0))),
         (pair, pl.BlockSpec(blk, lambda h, i, sp: (1, h, i, 0)))],
        [(_sds((4, 2, hr, LANE), F32), pl.BlockSpec(blk, lambda h, i, sp: (sp[0], h, i, 0)))])
    fly, sems, token = _gather_start("small_start", [slot], ((0,),), pair)
    return (fly, sems), token


def _allreduce_small_end(state, after):
    fly, sems = state
    (chips,) = _swap_halves("small_swap", _gather_wait("small_wait", fly, *sems, after))
    hr = chips.shape[2]
    tr = _row_tile(hr, LANE)
    blk = (None, None, tr, LANE)
    (total,) = _tiled(
        "small_chip_sum", lambda ids, tin, vin: ([((tin[0] + tin[1]) + tin[2]) + tin[3]], []), (2, hr // tr),
        [(chips, pl.BlockSpec(blk, lambda h, i, _j=j: (_j, h, i, 0))) for j in range(4)], [],
        [(_sds((2, hr, LANE), F32), pl.BlockSpec((None, tr, LANE), lambda h, i: (h, i, 0)))])
    return total.reshape(2 * hr, LANE)


SEM =pl.BlockSpec(memory_space=pltpu.SEMAPHORE)
_DATAFLOW = pltpu.SideEffectType.DATAFLOW_SIDE_EFFECTING


def _gather_start(name, slots, groups, after):
    n = len(slots)

    def kern(*refs):
        o = refs[n + 1:2 * n + 1]
        sems, token = refs[2 * n + 1:-1], refs[-1]
        mx, my, mc = _me()
        j0 = 2 * mx + my
        for gi, grp in enumerate(groups):
            for k, t in enumerate(grp):
                for q, (qx, qy) in enumerate(_other_chips(mx, my)):
                    _rcopy(o[t].at[j0, mc], o[t].at[j0, mc], sems[2 * gi].at[3 * k + q],
                           sems[2 * gi + 1].at[3 * k + q], (qx, qy, mc)).start()
        token[...] = jnp.zeros_like(token)

    sem_shapes = []
    for grp in groups:
        sem_shapes += [pltpu.SemaphoreType.DMA((3 * len(grp),))] * 2
    res = _pcall(
        kern, name=name, in_specs=[ANY] * (n + 1),
        out_specs=[ANY] * n + [SEM] * len(sem_shapes) + [pl.BlockSpec(memory_space=pltpu.VMEM)],
        out_shape=[_sds(w.shape, w.dtype) for w in slots] + sem_shapes + [_sds((8, LANE), F32)],
        input_output_aliases={t: t for t in range(n)},
        compiler_params=pltpu.CompilerParams(has_side_effects=_DATAFLOW),
    )(*slots, after)
    return list(res[:n]), list(res[n:-1]), res[-1]


def _gather_wait(name, bufs, ssem, rsem, after):
    n = len(bufs)

    def kern(*refs):
        b = refs[:n]
        ssem_ref, rsem_ref = refs[n], refs[n + 1]
        mx, my, mc = _me()
        j0 = 2 * mx + my
        for k in range(n):
            for q, (qx, qy) in enumerate(_other_chips(mx, my)):
                jq = 2 * qx + qy
                _rcopy(b[k].at[jq, mc], b[k].at[jq, mc], ssem_ref.at[3 * k + q], rsem_ref.at[3 * k + q],
                       (qx, qy, mc)).wait_recv()
                _rcopy(b[k].at[j0, mc], b[k].at[j0, mc], ssem_ref.at[3 * k + q], rsem_ref.at[3 * k + q],
                       (qx, qy, mc)).wait_send()

    return list(_pcall(
        kern, name=name, in_specs=[ANY] * n + [SEM, SEM, ANY], out_specs=[ANY] * n,
        out_shape=[_sds(w.shape, w.dtype) for w in bufs], input_output_aliases={k: k for k in range(n)},
        compiler_params=pltpu.CompilerParams(has_side_effects=_DATAFLOW),
    )(*bufs, ssem, rsem, after))


def _swap_halves(name, bufs):
    n = len(bufs)

    def kern(*refs):
        o = refs[n:2 * n]
        ss, rs = refs[2 * n:]
        mx, my, mc = _me()
        sib = (mx, my, 1 - mc)
        sends = []
        for k in range(n):
            for q, (qx, qy) in enumerate(_other_chips(mx, my)):
                jq = 2 * qx + qy
                cp = _rcopy(o[k].at[jq, mc], o[k].at[jq, mc], ss.at[3 * k + q], rs.at[3 * k + q], sib)
                cp.start()
                sends.append(cp)
        for k in range(n):
            for q, (qx, qy) in enumerate(_other_chips(mx, my)):
                jq = 2 * qx + qy
                _rcopy(o[k].at[jq, 1 - mc], o[k].at[jq, 1 - mc], ss.at[3 * k + q], rs.at[3 * k + q], sib).wait_recv()
        for cp in sends:
            cp.wait_send()

    dma = pltpu.SemaphoreType.DMA
    return list(_pcall(
        kern, name=name, in_specs=[ANY] * n, out_specs=[ANY] * n,
        out_shape=[_sds(w.shape, w.dtype) for w in bufs], input_output_aliases={k: k for k in range(n)},
        scratch_shapes=[dma((3 * n,)), dma((3 * n,))],
    )(*bufs))


def _chips_start(name, sums):
    n = len(sums)

    def kern(*refs):
        s, land = refs[n:2 * n], refs[2 * n:3 * n]
        ssem, rsem, token = refs[3 * n:]
        mx, my, mc = _me()
        for k in range(n):
            for q, (qx, qy) in enumerate(_other_chips(mx, my)):
                _rcopy(s[k].at[2 * qx + qy], land[k].at[q], ssem.at[3 * k + q], rsem.at[3 * k + q], (qx, qy, mc)).start()
        token[...] = jnp.zeros_like(token)

    dma = pltpu.SemaphoreType.DMA
    res = _pcall(
        kern, name=name, in_specs=[ANY] * n,
        out_specs=[ANY] * (2 * n) + [SEM, SEM, pl.BlockSpec(memory_space=pltpu.VMEM)],
        out_shape=[_sds(s.shape, s.dtype) for s in sums] + [_sds((3,) + s.shape[1:], s.dtype) for s in sums]
        + [dma((3 * n,)), dma((3 * n,)), _sds((8, LANE), F32)],
        input_output_aliases={k: k for k in range(n)},
        compiler_params=pltpu.CompilerParams(has_side_effects=_DATAFLOW),
    )(*sums)
    return (list(res[:n]), list(res[n:2 * n]), res[2 * n], res[2 * n + 1]), res[2 * n + 2]


def _chips_wait(name, sums, lands, ssem, rsem, after):
    n = len(sums)

    def kern(*refs):
        s, land = refs[:n], refs[n:2 * n]
        ssem_ref, rsem_ref = refs[2 * n], refs[2 * n + 1]
        mx, my, mc = _me()
        for k in range(n):
            for q, (qx, qy) in enumerate(_other_chips(mx, my)):
                cp = _rcopy(s[k].at[2 * qx + qy], land[k].at[q], ssem_ref.at[3 * k + q], rsem_ref.at[3 * k + q],
                            (qx, qy, mc))
                cp.wait_recv()
                cp.wait_send()

    res = _pcall(
        kern, name=name, in_specs=[ANY] * (2 * n) + [SEM, SEM, ANY], out_specs=[ANY] * (2 * n),
        out_shape=[_sds(a.shape, a.dtype) for a in list(sums) + list(lands)],
        input_output_aliases={k: k for k in range(2 * n)},
        compiler_params=pltpu.CompilerParams(has_side_effects=_DATAFLOW),
    )(*sums, *lands, ssem, rsem, after)
    return list(res[:n]), list(res[n:])


def _reduce_begin(tag, parts, place):
    theirs = _reduce_pair(f"reduce_pair_{tag}", parts)
    sums = []
    for k, (p, o) in enumerate(zip(parts, theirs)):
        rr, cc = p.shape[2], p.shape[3]
        tr = _row_tile(rr, cc)
        (s_k,) = _tiled_sp(
            f"pair_add_{tag}{k}", lambda tin: [tin[0].astype(F32) + tin[1].astype(F32)], (4, rr // tr), place,
            [(p, pl.BlockSpec((None, None, tr, cc), lambda j, i, sp: (j, sp[1], i, 0))),
             (o, pl.BlockSpec((None, tr, cc), lambda j, i, sp: (j, i, 0)))],
            [(_sds((4, rr, cc), BF16), pl.BlockSpec((None, tr, cc), lambda j, i, sp: (j, i, 0)))])
        sums.append(s_k)
    return _chips_start(f"chips_start_{tag}", sums)


def _reduce_end(tag, flying, place, after):
    sums, lands = _chips_wait(f"chips_wait_{tag}", *flying, after)
    fulls = []
    for k, (s, q) in enumerate(zip(sums, lands)):
        rr, cc = q.shape[1], q.shape[2]
        tr = _row_tile(rr, cc)

        def add4(tin):
            return [((tin[0].astype(F32) + tin[1].astype(F32)) + tin[2].astype(F32)) + tin[3].astype(F32)]

        ins = [(s, pl.BlockSpec((None, tr, cc), lambda i, sp: (sp[0], i, 0)))]
        ins += [(q, pl.BlockSpec((None, tr, cc), lambda i, sp, _k=kk: (_k, i, 0))) for kk in range(3)]
        (f_k,) = _tiled_sp(f"chip_add_{tag}{k}", add4, (rr // tr,), place, ins,
                           [(_sds((2, rr, cc), F32), pl.BlockSpec((None, tr, cc), lambda i, sp: (sp[1], i, 0)))])
        fulls.append(f_k)
    return fulls


def _pack(parts, PACK_ROWS=PACK_ROWS):
    flat, offs, pos = [], [], 0
    for p in parts:
        v = p.reshape(-1).astype(F32)
        n = -(-v.shape[0] // LANE) * LANE
        flat.append(jnp.pad(v, (0, n - v.shape[0])))
        offs.append((pos, v.shape[0], p.shape))
        pos += n
    total = -(-pos // (PACK_ROWS * LANE)) * PACK_ROWS * LANE
    flat.append(jnp.zeros((total - pos,), F32))
    return jnp.concatenate(flat).reshape(-1, LANE), offs


def _unpack(vec, offs):
    v = vec.reshape(-1)
    return [v[p:p + n].reshape(shape) for p, n, shape in offs]


def _adamw_math(wv, gv, mv, vv):
    bc1 = 1.0 - ADAM_B1 ** ADAM_STEP
    bc2 = 1.0 - ADAM_B2 ** ADAM_STEP
    mn = ADAM_B1 * mv + (1.0 - ADAM_B1) * gv
    vn = ADAM_B2 * vv + (1.0 - ADAM_B2) * (gv * gv)
    delta = -ADAM_LR * ((mn / bc1) / (jnp.sqrt(vn / bc2) + ADAM_EPS) + ADAM_WD * wv)
    return delta, mn, vn


def _adamw(name, w, g, m, v):
    rows, cols = w.shape
    tr = rows
    for cand in (512, 256, 128, 64, 32, 16, 8):
        if rows % cand == 0 and cand * cols * 4 <= 2 * 1024 * 1024:
            tr = cand
            break

    def fn(ids, tin, vin):
        return list(_adamw_math(*tin)), []

    spec = pl.BlockSpec((tr, cols), lambda i: (i, 0))
    outs = [(_sds((rows, cols), F32), spec)] * 3
    return _tiled(name, fn, (rows // tr,), [(a, spec) for a in (w, g, m, v)], [], outs)


def _adamw_many(name, ws, gs, ms, vs):
    n = len(ws)
    views = [(-1, a.shape[-1]) if a.ndim > 1 else (1, -1) for a in ws]
    flat = lambda arrs: [a.reshape(vw) for a, vw in zip(arrs, views)]

    def kern(*refs):
        ins, outs = refs[:4 * n], refs[4 * n:]
        for t in range(n):
            res = _adamw_math(*[ins[q * n + t][...] for q in range(4)])
            for q in range(3):
                outs[q * n + t][...] = res[q]

    shapes = [_sds(a.shape, F32) for a in flat(ws)]
    res = _pcall(kern, name=name, out_shape=shapes * 3, compiler_params=_cparams(),
                 )(*flat(ws), *flat(gs), *flat(ms), *flat(vs))
    back = lambda part: [a.reshape(w.shape) for a, w in zip(part, ws)]
    return back(res[:n]), back(res[n:2 * n]), back(res[2 * n:])


def _pos_embed():
    n_rows = T // GRID_W
    q = D // 4
    omega = 1.0 / (10000.0 ** (jnp.arange(q, dtype=F32) / q))
    er = jnp.arange(n_rows, dtype=jnp.int32).astype(F32)[:, None] * omega[None, :]
    ec = jnp.arange(GRID_W, dtype=jnp.int32).astype(F32)[:, None] * omega[None, :]
    by_row = jnp.concatenate([jnp.sin(er), jnp.cos(er)], axis=-1)
    by_col = jnp.concatenate([jnp.sin(ec), jnp.cos(ec)], axis=-1)
    return jnp.concatenate([jnp.repeat(by_row, GRID_W, axis=0), jnp.tile(by_col, (n_rows, 1))], axis=-1)


def _dense_gates(w_a, w_x):
    rows = jnp.stack([w_a[0], w_x[0], w_a[1], w_x[1]]).reshape(4, 2, RH, BLK)
    mask, spread = _block_mask(), _block_spread().T.astype(BF16)

    def kern(r_ref, m_ref, s_ref, o_ref):
        tiled = jnp.dot(r_ref[...].astype(BF16), s_ref[...], preferred_element_type=F32)
        o_ref[...] = (tiled * m_ref[...]).astype(o_ref.dtype)

    return _pcall(
        kern, name="gates_dense", grid=(2, 4),
        in_specs=[pl.BlockSpec((None, None, RH, BLK), lambda h, q: (q, h, 0, 0)),
                  pl.BlockSpec((RH, RH), lambda h, q: (0, 0)), pl.BlockSpec((BLK, RH), lambda h, q: (0, 0))],
        out_specs=pl.BlockSpec((None, RH, RH), lambda h, q: (h, 0, q)),
        out_shape=_sds((2, RH, NQ), BF16),
    )(rows, mask, spread)


def _block_mask():
    r = lax.broadcasted_iota(jnp.int32, (RH, RH), 0) // BLK
    c = lax.broadcasted_iota(jnp.int32, (RH, RH), 1) // BLK
    return (r == c).astype(F32)


def _block_spread():
    c = lax.broadcasted_iota(jnp.int32, (RH, BLK), 0) % BLK
    j = lax.broadcasted_iota(jnp.int32, (RH, BLK), 1)
    return (c == j).astype(F32)


def _fold_blocks(dense, mask, spread):
    return jnp.dot(dense * mask, spread, preferred_element_type=F32, precision=lax.Precision.HIGHEST)


def _gate_block_grads(folded):
    per = N_BLK // 2
    kinds = [jnp.concatenate([folded[h, q].reshape(per, BLK, BLK) for h in range(2)], axis=0) for q in range(4)]
    return jnp.stack([kinds[0], kinds[2]]), jnp.stack([kinds[1], kinds[3]])


def _gate_bias_dense(b_a, b_x):
    cols = []
    for h in range(2):
        for src in (b_a[0], b_x[0], b_a[1], b_x[1]):
            cols.append(src.reshape(R)[h * RH:(h + 1) * RH])
    return jnp.concatenate(cols).reshape(1, 2 * NQ)


def _gate_bias_grads(dgb):
    v = dgb.reshape(2, 4, RH)
    kinds = [jnp.concatenate([v[0, q], v[1, q]]).reshape(N_BLK, BLK) for q in range(4)]
    return jnp.stack([kinds[0], kinds[2]]), jnp.stack([kinds[1], kinds[3]])


def _mlp_fwd(tag, x_in, g_norm, sh, sc, gate, w_in, w_out):
    n_t = T // ROW_TILE
    (h,) = _tiled(f"{tag}_norm", lambda ids, t, v: ([_norm_mod(t[0], v[0], v[1], v[2])], []), (n_t,),
                  [_rows(x_in)], [g_norm, sc, sh], [_orow(T, D, BF16)])
    tm = MM_TILE
    (r,) = _mm(f"{tag}_in", h, w_in, _NN, (T // tm, 4, 1),
               pl.BlockSpec((tm, D), lambda i, j, k: (i, 0)), pl.BlockSpec((None, D, D), lambda i, j, k: (j, 0, 0)),
               [(_sds((T, FF), BF16), pl.BlockSpec((tm, D), lambda i, j, k: (i, j)))], (tm, D),
               epi=lambda acc, ex: [jnp.maximum(acc, 0.0)])
    o, x_out = _mm(f"{tag}_out", r, w_out, _NN, (T // tm, 1, FF // D),
                   pl.BlockSpec((tm, D), lambda i, j, k: (i, k)), pl.BlockSpec((D, D), lambda i, j, k: (k, 0)),
                   [(_sds((T, D), F32), pl.BlockSpec((tm, D), lambda i, j, k: (i, 0)))] * 2, (tm, D),
                   extra=[(x_in, pl.BlockSpec((tm, D), lambda i, j, k: (i, 0))), (gate, _full_spec(gate))],
                   a_pre=lambda a: a * a, epi=lambda acc, ex: [acc, ex[0] + ex[1] * acc])
    return dict(h=h, r=r, o=o, x_in=x_in), x_out


def _behind(dep):
    return [] if dep is None else [dep]


def _gate_bwd(tag, dx, o, gate, dep=None):
    def fn(ids, t, v):
        d_o = t[0] * v[0]
        return [d_o], [_sum0(t[0] * t[1]), _sum0(d_o)]
    return _tiled(f"{tag}_gate_bwd", fn, (T // ROW_TILE,), [_rows(dx), _rows(o)], [gate] + _behind(dep),
                  [_orow(T, D, BF16)], [(1, D), (1, D)])


def _norm_bwd(tag, dx_res, dh, dh_off, x, g_norm, sc, with_dx=True, dep=None):
    n_t = x.shape[0] // ROW_TILE

    def fn(ids, t, v):
        if with_dx:
            dres, dhv, xv = t
        else:
            dhv, xv = t
        dxv, d_sh, d_sc, d_g = _norm_mod_bwd(dhv, xv, v[0], v[1])
        return ([dres + dxv] if with_dx else []), [d_sh, d_sc, d_g]

    ins = ([_rows(dx_res)] if with_dx else []) + [_rows(dh, off=dh_off), _rows(x)]
    outs = [_orow(x.shape[0], D, F32)] if with_dx else []
    return _tiled(f"{tag}_norm_bwd", fn, (n_t,), ins, [g_norm, sc] + _behind(dep), outs, [(1, D)] * 3)


def _mlp_bwd(tag, dx, saved, g_norm, sc, gate, w_in, w_out):
    d_o, d_gate, _ = _gate_bwd(tag, dx, saved["o"], gate)
    tm = MM_TILE
    r = saved["r"]
    (da,) = _mm(f"{tag}_dz", d_o, w_out, _NT, (T // tm, FF // D, 1),
                pl.BlockSpec((tm, D), lambda i, j, k: (i, 0)), pl.BlockSpec((D, D), lambda i, j, k: (j, 0)),
                [(_sds((T, FF), BF16), pl.BlockSpec((tm, D), lambda i, j, k: (i, j)))], (tm, D),
                extra=[(r, pl.BlockSpec((tm, D), lambda i, j, k: (i, j)))],
                epi=lambda acc, ex: [acc * (2.0 * ex[0].astype(F32))])
    tk = MM_TILE
    (dw_out,) = _mm(f"{tag}_dwout", r, d_o, _TN, (FF // tm, 1, T // tk),
                    pl.BlockSpec((tk, tm), lambda i, j, k: (k, i)), pl.BlockSpec((tk, D), lambda i, j, k: (k, 0)),
                    [(_sds((FF, D), BF16), pl.BlockSpec((tm, D), lambda i, j, k: (i, 0)))], (tm, D),
                    a_pre=lambda a: a * a)
    (dh,) = _mm(f"{tag}_dh", da, w_in, _NT, (T // tm, 1, 4),
                pl.BlockSpec((tm, D), lambda i, j, k: (i, k)), pl.BlockSpec((None, D, D), lambda i, j, k: (k, 0, 0)),
                [(_sds((T, D), F32), pl.BlockSpec((tm, D), lambda i, j, k: (i, 0)))], (tm, D))
    (dw_in,) = _mm(f"{tag}_dwin", saved["h"], da, _TN, (D // tm, 4, T // tk),
                   pl.BlockSpec((tk, tm), lambda i, j, k: (k, i)), pl.BlockSpec((tk, D), lambda i, j, k: (k, j)),
                   [(_sds((4, D, D), BF16), pl.BlockSpec((None, tm, D), lambda i, j, k: (j, i, 0)))], (tm, D))
    dx_in, d_sh, d_sc, d_g = _norm_bwd(tag, dx, dh, 0, saved["x_in"], g_norm, sc)
    return dx_in, dw_in, dw_out, dict(sh=d_sh, sc=d_sc, gate=d_gate, g_norm=d_g)


def _local_step(x, ctx, tgt, mods, cmods, norm_g, final_g, rec, conf, wg, on_grads=None, start_dep=None):
    on_grads = on_grads or (lambda group, dws: None)
    n_t = T // ROW_TILE
    row = lambda v: v.reshape(1, -1)
    m0 = [row(mods[0, q]) for q in range(6)]
    m1 = [row(mods[1, q]) for q in range(6)]
    g00, g01, g10, g11 = (row(norm_g[0, 0]), row(norm_g[0, 1]), row(norm_g[1, 0]), row(norm_g[1, 1]))
    csh, csc = row(cmods[0]), row(cmods[1])
    pos = _pos_embed()

    def prep0(ids, t, v):
        cx, xv, pv = t
        is_ctx = ids[0] == 0
        xin = jnp.where(is_ctx, cx, xv + pv)
        sh = jnp.where(is_ctx, v[3], v[1])
        sc = jnp.where(is_ctx, v[4], v[2])
        return [_norm_mod(xin, v[0], sc, sh), xv + pv], []

    hcat, x0 = _tiled(
        "prep0", prep0, (N_SCAN,),
        [(ctx, pl.BlockSpec((ROW_TILE, D), lambda i: (0, 0))), _rows(x, off=-1, clamp_lo=True),
         _rows(pos, off=-1, clamp_lo=True)],
        [g00, m0[0], m0[1], csh, csc] + _behind(start_dep),
        [_orow(TA, D, BF16), _orow(T, D, F32, off=-1, clamp_lo=True)])

    tm_a = REC_TILE
    w_rec = wg("rec", hcat)
    (a_in,) = _mm("rec_in", hcat, w_rec["rec_w_in"], _NN, (TA // tm_a, 4, 1),
                  pl.BlockSpec((tm_a, D), lambda i, j, k: (i, 0)),
                  pl.BlockSpec((None, D, RH), lambda i, j, k: (j, 0, 0)),
                  [(_sds((TA, 2 * R), F32), pl.BlockSpec((tm_a, RH), lambda i, j, k: (i, j)))], (tm_a, RH))
    rec_starts = (0, 1)
    u = _dwconv("rec_conv", a_in, R // CW_REC, rec["conv_w"], row(rec["conv_b"]), 1, rec_starts, R, CW_REC)
    wbd = _dense_gates(rec["w_a"], rec["w_x"])
    gbias = _gate_bias_dense(rec["b_a"], rec["b_x"])
    lam = rec["lam"]
    a_f, b_f, a_r, b_r = _tiled("rg_fwd", _rg_fwd_fn, (TA // RG_TILE,), [_rows(u, tm=RG_TILE)], [wbd, gbias, lam],
                                [_orow(TA, R, F32, tm=RG_TILE)] * 4, vec_refs=True)
    y_f, y_r, hin_f, hin_r = _scan_fwd(a_f, b_f, a_r, b_r)

    def rec_mid(ids, t, v):
        gp, yf, yr = t
        g, _ = _gelu(gp)
        return [g * (yf + yr)], []

    (m_rec,) = _tiled("rec_mid", rec_mid, (n_t,),
                      [_rows(a_in, R, off=1), _rows(y_f, off=1), _rows(y_r, off=1)], [], [_orow(T, R, BF16)])
    tm = MM_TILE
    o_rec, x1 = _mm("rec_out", m_rec, w_rec["rec_w_out"], _NN, (T // tm, 1, 1),
                    pl.BlockSpec((tm, R), lambda i, j, k: (i, 0)), pl.BlockSpec((R, D), lambda i, j, k: (0, 0)),
                    [(_sds((T, D), F32), pl.BlockSpec((tm, D), lambda i, j, k: (i, 0)))] * 2, (tm, D),
                    extra=[(x0, pl.BlockSpec((tm, D), lambda i, j, k: (i, 0))), (m0[2], _full_spec(m0[2]))],
                    epi=lambda acc, ex: [acc, ex[0] + ex[1] * acc])
    w_m0 = wg("mlp0", x1)
    mlp0, x2 = _mlp_fwd("mlp0", x1, g01, m0[3], m0[4], m0[5], w_m0["w_in"], w_m0["w_out"])

    (h1,) = _tiled("conf_norm", lambda ids, t, v: ([_norm_mod(t[0], v[0], v[1], v[2])], []), (n_t,),
                   [_rows(x2)], [g10, m1[1], m1[0]], [_orow(T, D, BF16)])
    b_pw1 = row(conf["b_pw1"])
    w_cf = wg("conf", x2)
    (pre,) = _mm("conf_pw1", h1, w_cf["conf_w_pw1"], _NN, (T // tm, 4, 1),
                 pl.BlockSpec((tm, D), lambda i, j, k: (i, 0)),
                 pl.BlockSpec((None, D, D // 2), lambda i, j, k: (j, 0, 0)),
                 [(_sds((T, 2 * D), F32), pl.BlockSpec((tm, D // 2), lambda i, j, k: (i, j)))], (tm, D // 2),
                 extra=[(b_pw1, pl.BlockSpec((1, D // 2), lambda i, j, k: (0, j)))],
                 epi=lambda acc, ex: [acc + ex[0]])
    (zg,) = _tiled("conf_glu", lambda ids, t, v: ([t[0] * _sigmoid(t[1])], []), (n_t,),
                   [_rows(pre, D, col=0), _rows(pre, D, col=1)], [], [_orow(T, D, F32)])
    conf_starts = (0,)
    zc = _dwconv("conf_conv", zg, 0, conf["conv_w"], row(conf["conv_b"]), CONF_KW // 2, conf_starts, D, CW_CONF)
    ln_g, ln_b = row(conf["ln_g"]), row(conf["ln_b"])

    def ln_silu(ids, t, v):
        nh, _ = _layernorm_parts(t[0])
        ln = nh * v[0] + v[1]
        return [ln * _sigmoid(ln)], []

    (s_conf,) = _tiled("conf_ln", ln_silu, (n_t,), [_rows(zc)], [ln_g, ln_b], [_orow(T, D, BF16)])
    b_pw2 = row(conf["b_pw2"])
    y_conf, x3 = _mm("conf_pw2", s_conf, w_cf["conf_w_pw2"], _NN, (T // tm, 1, 1),
                     pl.BlockSpec((tm, D), lambda i, j, k: (i, 0)), pl.BlockSpec((D, D), lambda i, j, k: (0, 0)),
                     [(_sds((T, D), F32), pl.BlockSpec((tm, D), lambda i, j, k: (i, 0)))] * 2, (tm, D),
                     extra=[(x2, pl.BlockSpec((tm, D), lambda i, j, k: (i, 0))), (m1[2], _full_spec(m1[2])),
                            (b_pw2, _full_spec(b_pw2))],
                     epi=lambda acc, ex: [acc + ex[2], ex[0] + ex[1] * (acc + ex[2])])
    w_m1 = wg("mlp1", x3)
    mlp1, x4 = _mlp_fwd("mlp1", x3, g11, m1[3], m1[4], m1[5], w_m1["w_in"], w_m1["w_out"])

    fg = row(final_g)

    def head(ids, t, v):
        n, r = _rms(t[0])
        err = n * v[0] - t[1]
        d_out = err * (1.0 / D)
        dn = d_out * v[0]
        dxv = r * (dn - n * jnp.mean(dn * n, axis=-1, keepdims=True))
        part = jnp.sum(_sum0(err * err), axis=1, keepdims=True) * (0.5 / D)
        return [dxv], [part, _sum0(d_out * n)]

    dx4, loss, d_fg = _tiled("head", head, (n_t,), [_rows(x4), _rows(tgt)], [fg], [_orow(T, D, F32)],
                             [(1, 1), (1, D)])

    dx3, dw_in1, dw_out1, dm_mlp1 = _mlp_bwd("mlp1", dx4, mlp1, g11, m1[4], m1[5],
                                             w_m1["w_in"], w_m1["w_out"])
    dep = on_grads("mlp1", (dw_in1, dw_out1))
    d_y, d_g1c, d_bpw2 = _gate_bwd("conf", dx3, y_conf, m1[2], dep)
    tk = MM_TILE
    (dw_pw2,) = _mm("conf_dwpw2", s_conf, d_y, _TN, (D // tm, 1, T // tk),
                    pl.BlockSpec((tk, tm), lambda i, j, k: (k, i)), pl.BlockSpec((tk, D), lambda i, j, k: (k, 0)),
                    [(_sds((D, D), BF16), pl.BlockSpec((tm, D), lambda i, j, k: (i, 0)))], (tm, D))
    (ds,) = _mm("conf_ds", d_y, w_cf["conf_w_pw2"], _NT, (T // tm, 1, 1),
                pl.BlockSpec((tm, D), lambda i, j, k: (i, 0)), pl.BlockSpec((D, D), lambda i, j, k: (0, 0)),
                [(_sds((T, D), F32), pl.BlockSpec((tm, D), lambda i, j, k: (i, 0)))], (tm, D))

    def ln_silu_bwd(ids, t, v):
        dsv, zcv = t
        nh, rstd = _layernorm_parts(zcv)
        ln = nh * v[0] + v[1]
        sg = _sigmoid(ln)
        d_ln = dsv * (sg * (1.0 + ln * (1.0 - sg)))
        d_nh = d_ln * v[0]
        d_zc = rstd * (d_nh - jnp.mean(d_nh, axis=-1, keepdims=True)
                       - nh * jnp.mean(d_nh * nh, axis=-1, keepdims=True))
        return [d_zc], [_sum0(d_ln * nh), _sum0(d_ln)]

    d_zc, d_lng, d_lnb = _tiled("conf_ln_bwd", ln_silu_bwd, (n_t,), [_rows(ds), _rows(zc)], [ln_g, ln_b],
                                [_orow(T, D, F32)], [(1, D), (1, D)])
    d_zg = _dwconv("conf_conv_dx", d_zc, 0, conf["conv_w"][::-1], jnp.zeros((1, D), F32),
                   CONF_KW - 1 - CONF_KW // 2, conf_starts, D, CW_CONF)
    d_cw_conf = _dwconv_wgrad("conf_conv_dw", d_zc, zg, 0, CONF_KW, CONF_KW // 2, conf_starts, D, CW_CONF)

    def glu_bwd(ids, t, v):
        dz, pa, pb = t
        sg = _sigmoid(pb)
        d_a = dz * sg
        d_b = dz * pa * sg * (1.0 - sg)
        return [d_a, d_b], [_sum0(d_a), _sum0(d_b)]

    d_pre_a, d_pre_b, d_b1a, d_b1b = _tiled(
        "conf_glu_bwd", glu_bwd, (n_t,), [_rows(d_zg), _rows(pre, D, col=0), _rows(pre, D, col=1)], [],
        [_orow(T, D, BF16), _orow(T, D, BF16)], [(1, D), (1, D)])
    d_pre = jnp.concatenate([d_pre_a, d_pre_b], axis=1)
    (dw_pw1,) = _mm("conf_dwpw1", h1, d_pre, _TN, (D // tm, 4, T // tk),
                    pl.BlockSpec((tk, tm), lambda i, j, k: (k, i)),
                    pl.BlockSpec((tk, D // 2), lambda i, j, k: (k, j)),
                    [(_sds((4, D, D // 2), BF16), pl.BlockSpec((None, tm, D // 2), lambda i, j, k: (j, i, 0)))],
                    (tm, D // 2))
    dep = on_grads("conf", (dw_pw1, dw_pw2))
    (dh1,) = _mm("conf_dh", d_pre, w_cf["conf_w_pw1"], _NT, (T // tm, 1, 4),
                 pl.BlockSpec((tm, D // 2), lambda i, j, k: (i, k)),
                 pl.BlockSpec((None, D, D // 2), lambda i, j, k: (k, 0, 0)),
                 [(_sds((T, D), F32), pl.BlockSpec((tm, D), lambda i, j, k: (i, 0)))], (tm, D))
    dx2, d_sh1c, d_sc1c, d_g10 = _norm_bwd("conf", dx3, dh1, 0, x2, g10, m1[1], dep=dep)

    dx1, dw_in0, dw_out0, dm_mlp0 = _mlp_bwd("mlp0", dx2, mlp0, g01, m0[4], m0[5],
                                             w_m0["w_in"], w_m0["w_out"])
    dep = on_grads("mlp0", (dw_in0, dw_out0))
    d_orec, d_g1r, _ = _gate_bwd("rec", dx1, o_rec, m0[2], dep)
    (dw_rout,) = _mm("rec_dwout", m_rec, d_orec, _TN, (R // RH, 1, T // tk),
                     pl.BlockSpec((tk, RH), lambda i, j, k: (k, i)), pl.BlockSpec((tk, D), lambda i, j, k: (k, 0)),
                     [(_sds((R, D), BF16), pl.BlockSpec((RH, D), lambda i, j, k: (i, 0)))], (RH, D))
    (dm_rec,) = _mm("rec_dm", d_orec, w_rec["rec_w_out"], _NT, (T // tm, 1, 1),
                    pl.BlockSpec((tm, D), lambda i, j, k: (i, 0)), pl.BlockSpec((R, D), lambda i, j, k: (0, 0)),
                    [(_sds((T, R), F32), pl.BlockSpec((tm, R), lambda i, j, k: (i, 0)))], (tm, R))

    def rec_mid_bwd(ids, t, v):
        dmv, gp, yf, yr = t
        g, th = _gelu(gp)
        lat = ids[0] > 0
        d_gp = jnp.where(lat, dmv * (yf + yr) * _gelu_grad(gp, th), 0.0)
        dy = jnp.where(lat, dmv * g, 0.0)
        return [d_gp, dy], []

    d_gp, dy = _tiled("rec_mid_bwd", rec_mid_bwd, (N_SCAN,),
                      [_rows(dm_rec, off=-1, clamp_lo=True), _rows(a_in, R), _rows(y_f), _rows(y_r)], [],
                      [_orow(TA, R, BF16), _orow(TA, R, F32)])
    da_f, db_f, da_r, db_r = _scan_bwd(dy, a_f, y_f, hin_f, a_r, y_r, hin_r)
    d_gpre, d_u, d_gbias, d_lam = _tiled(
        "rg_bwd", _rg_bwd_fn, (TA // RG_TILE,), [_rows(a, tm=RG_TILE) for a in (u, da_f, db_f, da_r, db_r)],
        [wbd, gbias, lam], [_orow(TA, 2 * NQ, BF16, tm=RG_TILE), _orow(TA, R, F32, tm=RG_TILE)],
        [(1, 2 * NQ), (1, 2 * R)], vec_refs=True)
    tk_a = REC_TILE
    blk_mask, blk_spread = _block_mask(), _block_spread()
    (d_wbd,) = _mm("rg_dw", u, d_gpre, _TN, (2, 2, TA // tk_a),
                   pl.BlockSpec((tk_a, RH), lambda i, j, k: (k, i)),
                   pl.BlockSpec((tk_a, NQ // 2), lambda i, j, k: (k, 2 * i + j)),
                   [(_sds((2, 4, RH, BLK), F32), pl.BlockSpec((None, 2, RH, BLK), lambda i, j, k: (i, j, 0, 0)))],
                   (RH, NQ // 2),
                   extra=[(blk_mask, _full_spec(blk_mask)), (blk_spread, _full_spec(blk_spread))],
                   epi=lambda acc, ex: [jnp.stack([_fold_blocks(acc[:, s * RH:(s + 1) * RH], ex[0], ex[1])
                                                   for s in range(2)])])
    d_p = _dwconv("rec_conv_dx", d_u, 0, rec["conv_w"][::-1], jnp.zeros((1, R), F32), REC_KW - 1 - 1,
                  rec_starts, R, CW_REC)
    d_cw_rec = _dwconv_wgrad("rec_conv_dw", d_u, a_in, R // CW_REC, REC_KW, 1, rec_starts, R, CW_REC)
    d_a = jnp.concatenate([d_gp, d_p.astype(BF16)], axis=1)
    (dw_rin,) = _mm("rec_dwin", hcat, d_a, _TN, (D // tm, 4, TA // tk_a),
                    pl.BlockSpec((tk_a, tm), lambda i, j, k: (k, i)), pl.BlockSpec((tk_a, RH), lambda i, j, k: (k, j)),
                    [(_sds((4, D, RH), BF16), pl.BlockSpec((None, tm, RH), lambda i, j, k: (j, i, 0)))], (tm, RH))
    dep = on_grads("rec", (dw_rin, dw_rout))
    (dhcat,) = _mm("rec_dh", d_a, w_rec["rec_w_in"], _NT, (TA // tm_a, 1, 4),
                   pl.BlockSpec((tm_a, RH), lambda i, j, k: (i, k)),
                   pl.BlockSpec((None, D, RH), lambda i, j, k: (k, 0, 0)),
                   [(_sds((TA, D), F32), pl.BlockSpec((tm_a, D), lambda i, j, k: (i, 0)))], (tm_a, D))
    dx0, d_sh1r, d_sc1r, d_g00 = _norm_bwd("rec", dx1, dhcat, 1, x0, g00, m0[1], dep=dep)
    d_csh, d_csc, d_g00c = _norm_bwd("ctx", None, dhcat, 0, ctx, g00, csc, with_dx=False)

    big = dict(rec_w_in=dw_rin, rec_w_out=dw_rout, conf_w_pw1=dw_pw1, conf_w_pw2=dw_pw2,
               mlp_w_in=(dw_in0, dw_in1), mlp_w_out=(dw_out0, dw_out1))
    d_wa, d_wx = _gate_block_grads(d_wbd)
    d_ba, d_bx = _gate_bias_grads(d_gbias)
    d_mod = jnp.concatenate([
        d_sh1r, d_sc1r, d_g1r, dm_mlp0["sh"], dm_mlp0["sc"], dm_mlp0["gate"],
        d_sh1c, d_sc1c, d_g1c, dm_mlp1["sh"], dm_mlp1["sc"], dm_mlp1["gate"]], axis=1).reshape(2, 6 * D)
    small = dict(
        d_mod=d_mod, d_cmod=jnp.concatenate([d_csh, d_csc], axis=1),
        norm_g=jnp.concatenate([d_g00 + d_g00c, dm_mlp0["g_norm"], d_g10, dm_mlp1["g_norm"]], axis=1),
        rec_conv_w=d_cw_rec[:REC_KW], rec_conv_b=d_cw_rec[REC_KW], rec_lambda=d_lam.reshape(2, R),
        rec_w_a=d_wa, rec_b_a=d_ba, rec_w_x=d_wx, rec_b_x=d_bx,
        conf_b_pw1=jnp.concatenate([d_b1a, d_b1b], axis=1), conf_conv_w=d_cw_conf[:CONF_KW],
        conf_conv_b=d_cw_conf[CONF_KW], conf_ln_g=d_lng, conf_ln_b=d_lnb, conf_b_pw2=d_bpw2, final_g=d_fg)
    return loss.reshape(()), dx0, big, small


_BIG = ("rec_w_in", "rec_w_out", "conf_w_pw1", "conf_w_pw2", "mlp_w_in", "mlp_w_out")


def _halves(w):
    return w.reshape(2, w.shape[0] // 2, w.shape[1])


def _ada_fwd(c16, w_ada, b_shard):
    ns = w_ada.shape[2]
    tn = 512

    def kern(c_ref, w_ref, b_ref, o_ref):
        cv = c_ref[...]
        s = (cv * _sigmoid(cv)).astype(BF16)
        o_ref[...] = jnp.dot(s, w_ref[...].astype(BF16), preferred_element_type=F32) + b_ref[...]

    return _pcall(
        kern, name="ada_fwd", grid=(2, ns // tn),
        in_specs=[pl.BlockSpec((16, D), lambda l, j: (0, 0)), pl.BlockSpec((None, D, tn), lambda l, j: (l, 0, j)),
                  pl.BlockSpec((None, 1, tn), lambda l, j: (l, 0, j))],
        out_specs=pl.BlockSpec((None, 16, tn), lambda l, j: (l, 0, j)),
        out_shape=_sds((2, 16, ns), F32), compiler_params=_cparams(),
    )(c16, w_ada, b_shard)


def _ada_bwd(c16, dm16, w_ada):
    ns = w_ada.shape[2]
    tn = 512

    def kern(c_ref, dm_ref, w_ref, gw_ref, ds_ref):
        cv = c_ref[...]
        s = (cv * _sigmoid(cv)).astype(BF16)
        dm = dm_ref[...].astype(BF16)
        gw_ref[...] = lax.dot_general(s, dm, _TN, preferred_element_type=F32)

        @pl.when(jnp.logical_and(pl.program_id(0) == 0, pl.program_id(1) == 0))
        def _():
            ds_ref[...] = jnp.zeros_like(ds_ref)

        ds_ref[...] += lax.dot_general(dm, w_ref[...].astype(BF16), _NT, preferred_element_type=F32)

    return _pcall(
        kern, name="ada_bwd", grid=(2, ns // tn),
        in_specs=[pl.BlockSpec((16, D), lambda l, j: (0, 0)), pl.BlockSpec((None, 16, tn), lambda l, j: (l, 0, j)),
                  pl.BlockSpec((None, D, tn), lambda l, j: (l, 0, j))],
        out_specs=[pl.BlockSpec((None, D, tn), lambda l, j: (l, 0, j)), pl.BlockSpec((16, D), lambda l, j: (0, 0))],
        out_shape=[_sds((2, D, ns), F32), _sds((16, D), F32)], compiler_params=_cparams(),
    )(c16, dm16, w_ada)


def _cctx_grad(ds8, c_ctx):
    def kern(d_ref, c_ref, o_ref):
        tot = d_ref[0, 8:9, :] + d_ref[2, 8:9, :] + d_ref[4, 8:9, :] + d_ref[6, 8:9, :]
        cv = c_ref[...]
        sg = _sigmoid(cv)
        o_ref[...] = tot * (sg * (1.0 + cv * (1.0 - sg)))

    return _pcall(kern, name="cctx_grad", out_shape=_sds((1, D), F32))(ds8, c_ctx.reshape(1, D))


def kernel(x, c, ctx, c_ctx, w_ada, b_ada, norm_g, rec_w_in, rec_conv_w, rec_conv_b, rec_lambda, rec_w_a, rec_b_a, rec_w_x, rec_b_x, rec_w_out, conf_w_pw1, conf_b_pw1, conf_conv_w, conf_conv_b, conf_ln_g, conf_ln_b, conf_w_pw2, conf_b_pw2, mlp_w_in, mlp_w_out, final_g, loss_target, m_c_ctx, m_w_ada, m_b_ada, m_norm_g, m_rec_w_in, m_rec_conv_w, m_rec_conv_b, m_rec_lambda, m_rec_w_a, m_rec_b_a, m_rec_w_x, m_rec_b_x, m_rec_w_out, m_conf_w_pw1, m_conf_b_pw1, m_conf_conv_w, m_conf_conv_b, m_conf_ln_g, m_conf_ln_b, m_conf_w_pw2, m_conf_b_pw2, m_mlp_w_in, m_mlp_w_out, m_final_g, v_c_ctx, v_w_ada, v_b_ada, v_norm_g, v_rec_w_in, v_rec_conv_w, v_rec_conv_b, v_rec_lambda, v_rec_w_a, v_rec_b_a, v_rec_w_x, v_rec_b_x, v_rec_w_out, v_conf_w_pw1, v_conf_b_pw1, v_conf_conv_w, v_conf_conv_b, v_conf_ln_g, v_conf_ln_b, v_conf_w_pw2, v_conf_b_pw2, v_mlp_w_in, v_mlp_w_out, v_final_g):
    names = ["c_ctx", "w_ada", "b_ada", "norm_g", "rec_w_in", "rec_conv_w", "rec_conv_b", "rec_lambda", "rec_w_a",
             "rec_b_a", "rec_w_x", "rec_b_x", "rec_w_out", "conf_w_pw1", "conf_b_pw1", "conf_conv_w", "conf_conv_b",
             "conf_ln_g", "conf_ln_b", "conf_w_pw2", "conf_b_pw2", "mlp_w_in", "mlp_w_out", "final_g"]
    w = dict(zip(names, [c_ctx, w_ada, b_ada, norm_g, rec_w_in, rec_conv_w, rec_conv_b, rec_lambda, rec_w_a,
                         rec_b_a, rec_w_x, rec_b_x, rec_w_out, conf_w_pw1, conf_b_pw1, conf_conv_w, conf_conv_b,
                         conf_ln_g, conf_ln_b, conf_w_pw2, conf_b_pw2, mlp_w_in, mlp_w_out, final_g]))
    m = dict(zip(names, [m_c_ctx, m_w_ada, m_b_ada, m_norm_g, m_rec_w_in, m_rec_conv_w, m_rec_conv_b, m_rec_lambda,
                         m_rec_w_a, m_rec_b_a, m_rec_w_x, m_rec_b_x, m_rec_w_out, m_conf_w_pw1, m_conf_b_pw1,
                         m_conf_conv_w, m_conf_conv_b, m_conf_ln_g, m_conf_ln_b, m_conf_w_pw2, m_conf_b_pw2,
                         m_mlp_w_in, m_mlp_w_out, m_final_g]))
    v = dict(zip(names, [v_c_ctx, v_w_ada, v_b_ada, v_norm_g, v_rec_w_in, v_rec_conv_w, v_rec_conv_b, v_rec_lambda,
                         v_rec_w_a, v_rec_b_a, v_rec_w_x, v_rec_b_x, v_rec_w_out, v_conf_w_pw1, v_conf_b_pw1,
                         v_conf_conv_w, v_conf_conv_b, v_conf_ln_g, v_conf_ln_b, v_conf_w_pw2, v_conf_b_pw2,
                         v_mlp_w_in, v_mlp_w_out, v_final_g]))
    mx, my, mc = _me()
    chip = 2 * mx + my
    me = 4 * mx + 2 * my + mc

    sharded_small = ["norm_g", "rec_conv_w", "rec_lambda", "conf_b_pw1", "conf_conv_w", "conf_conv_b", "conf_ln_g",
                     "conf_ln_b", "conf_b_pw2"]
    packed, offs = _pack([c] + [w[k] for k in sharded_small], 8)
    got = _allgather8("gather_small", packed)

    place = jnp.stack([chip, mc]).astype(jnp.int32)
    shards = [_halves(rec_w_in[0]), _halves(rec_w_out[0]), _halves(conf_w_pw1[0]), _halves(conf_w_pw2[0]),
              _halves(mlp_w_in[0]), _halves(mlp_w_in[1]), _halves(mlp_w_out[0]), _halves(mlp_w_out[1])]
    use_order = dict(rec=(0, 1), mlp0=(4, 6), conf=(2, 3), mlp1=(5, 7))
    slots = _place_big(shards, place)
    flying, gsems = {}, {}
    fly, sems, rec_started = _gather_start("gather_start_rec", [slots[t] for t in use_order["rec"]], ((0, 1),), got)
    flying["rec"], gsems["rec"] = fly, sems

    def wg(group, after):
        bufs = _gather_wait(f"gather_wait_{group}", flying[group], *gsems[group], after)
        a, b = _swap_halves(f"swap_{group}", bufs)
        if group == "rec":
            return dict(rec_w_in=a.reshape(4, D, RH), rec_w_out=b.reshape(R, D))
        if group == "conf":
            return dict(conf_w_pw1=a.reshape(4, D, D // 2), conf_w_pw2=b.reshape(D, D))
        return dict(w_in=a.reshape(4, D, D), w_out=b.reshape(FF, D))

    got_flat = got.reshape(8, -1)

    def piece(i):
        p, n, shape = offs[i]
        return got_flat[:, p:p + n].reshape((8,) + tuple(shape))

    c_rows = piece(0).reshape(8, D)
    full = {}
    for i, k in enumerate(sharded_small):
        per_chip = jnp.moveaxis(piece(1 + i)[0::2], 0, -2)
        full[k] = per_chip.reshape(per_chip.shape[:-2] + (4 * per_chip.shape[-1],))
    c16 = jnp.concatenate([c_rows, c_ctx.reshape(1, D), jnp.zeros((7, D), F32)], axis=0)

    ns = w_ada.shape[2]
    b_shard = lax.dynamic_slice_in_dim(b_ada, chip * ns, ns, axis=1).reshape(2, 1, ns)
    prod = _ada_fwd(c16, w_ada, b_shard)
    prod8 = _allgather8("gather_mod", prod.reshape(32, ns), rec_started)
    later = ("mlp0", "conf", "mlp1")
    fly, sems, all_started = _gather_start("gather_start_rest", [slots[t] for g in later for t in use_order[g]],
                                           ((0, 1), (2, 3), (4, 5)), prod8)
    for gi, g in enumerate(later):
        flying[g], gsems[g] = fly[2 * gi:2 * gi + 2], sems[2 * gi:2 * gi + 2]
    prod8 = prod8.reshape(8, 2, 16, ns)
    mod_all = jnp.concatenate([prod8[2 * j] for j in range(4)], axis=-1)
    mods = lax.dynamic_index_in_dim(mod_all, me, axis=1, keepdims=False).reshape(2, 6, D)
    cmods = mod_all[0, 8].reshape(6, D)[:2]

    rec = dict(conv_w=full["rec_conv_w"][0], conv_b=rec_conv_b[0], lam=full["rec_lambda"][0],
               w_a=rec_w_a[0], b_a=rec_b_a[0], w_x=rec_w_x[0], b_x=rec_b_x[0])
    conf = dict(b_pw1=full["conf_b_pw1"][0], conv_w=full["conf_conv_w"][0], conv_b=full["conf_conv_b"][0],
                ln_g=full["conf_ln_g"][0], ln_b=full["conf_ln_b"][0], b_pw2=full["conf_b_pw2"][0])
    sent = {}

    def on_grads(group, dws):
        parts = [dw.reshape(4, 2, shards[t].shape[1], shards[t].shape[2]) for dw, t in zip(dws, use_order[group])]
        sent[group], token = _reduce_begin(group, parts, place)
        return token

    loss_local, grad_x, _, small = _local_step(x[0], ctx[0], loss_target[0], mods, cmods, full["norm_g"], final_g,
                                               rec, conf, wg, on_grads, all_started)
    loss = lax.psum(loss_local, ("x", "y", "c"))

    small_names = ["d_mod", "d_cmod", "norm_g", "rec_conv_w", "rec_conv_b", "rec_lambda", "rec_w_a", "rec_b_a",
                   "rec_w_x", "rec_b_x", "conf_b_pw1", "conf_conv_w", "conf_conv_b", "conf_ln_g", "conf_ln_b",
                   "conf_b_pw2", "final_g"]
    mine = lax.broadcasted_iota(jnp.int32, (8, 1), 0) == me
    mod_slots = jnp.where(mine, small["d_mod"].reshape(1, -1), 0.0)
    spacked, soffs = _pack([small[k] for k in small_names] + [mod_slots])
    small_state, small_started = _allreduce_small_begin(spacked, place)

    fulls = {}
    for group in ("mlp1", "conf", "mlp0", "rec"):
        for t, f in zip(use_order[group], _reduce_end(group, sent[group], place, small_started)):
            fulls[t] = f
    whole = _share_halves("share_grads", [fulls[t] for t in range(8)])
    g_big = dict(rec_w_in=whole[0].reshape(rec_w_in.shape), rec_w_out=whole[1].reshape(rec_w_out.shape),
                 conf_w_pw1=whole[2].reshape(conf_w_pw1.shape), conf_w_pw2=whole[3].reshape(conf_w_pw2.shape),
                 mlp_w_in=jnp.stack([whole[4].reshape(D, D), whole[5].reshape(D, D)]),
                 mlp_w_out=jnp.stack([whole[6].reshape(D, D), whole[7].reshape(D, D)]))
    delta, new_m, new_v = {}, {}, {}

    def adamw_of(k, g):
        cols = w[k].shape[-1]
        d_, m_, v_ = _adamw(f"adamw_{k}", w[k].reshape(-1, cols), g.reshape(-1, cols),
                            m[k].reshape(-1, cols), v[k].reshape(-1, cols))
        delta[k], new_m[k], new_v[k] = (a.reshape(w[k].shape) for a in (d_, m_, v_))

    for k in _BIG:
        adamw_of(k, g_big[k])

    unpacked = _unpack(_allreduce_small_end(small_state, new_v[_BIG[-1]]), soffs)
    ssum = dict(zip(small_names, unpacked[:-1]))
    dmod_rows = unpacked[-1].reshape(8, 2, 6 * D).transpose(1, 0, 2)

    d_cmod_full =jnp.concatenate([ssum["d_cmod"].reshape(1, 2 * D), jnp.zeros((1, 4 * D), F32)], axis=1)
    dm16 = jnp.concatenate([dmod_rows, jnp.stack([d_cmod_full, jnp.zeros((1, 6 * D), F32)]),
                            jnp.zeros((2, 7, 6 * D), F32)], axis=1)
    dm16_shard = lax.dynamic_slice_in_dim(dm16, chip * ns, ns, axis=2)
    g_w_ada, ds_part = _ada_bwd(c16, dm16_shard, w_ada)
    ds8 = _allgather8("gather_dsilu", ds_part)
    g_c_ctx = _cctx_grad(ds8, c_ctx).reshape(D)
    g_b_ada = ssum["d_mod"] + jnp.stack([d_cmod_full[0], jnp.zeros((6 * D,), F32)])

    def shard_of(a, axis):
        n = a.shape[axis] // 4
        return lax.dynamic_slice_in_dim(a, chip * n, n, axis=axis)

    grads = dict(
        c_ctx=g_c_ctx, w_ada=g_w_ada, b_ada=g_b_ada,
        norm_g=shard_of(ssum["norm_g"].reshape(2, 2, D), 2),
        rec_w_in=g_big["rec_w_in"], rec_conv_w=shard_of(ssum["rec_conv_w"].reshape(1, REC_KW, R), 2),
        rec_conv_b=ssum["rec_conv_b"].reshape(1, R), rec_lambda=shard_of(ssum["rec_lambda"].reshape(1, 2, R), 2),
        rec_w_a=ssum["rec_w_a"].reshape(rec_w_a.shape), rec_b_a=ssum["rec_b_a"].reshape(rec_b_a.shape),
        rec_w_x=ssum["rec_w_x"].reshape(rec_w_x.shape), rec_b_x=ssum["rec_b_x"].reshape(rec_b_x.shape),
        rec_w_out=g_big["rec_w_out"], conf_w_pw1=g_big["conf_w_pw1"],
        conf_b_pw1=shard_of(ssum["conf_b_pw1"].reshape(1, 2 * D), 1),
        conf_conv_w=shard_of(ssum["conf_conv_w"].reshape(1, CONF_KW, D), 2),
        conf_conv_b=shard_of(ssum["conf_conv_b"].reshape(1, D), 1),
        conf_ln_g=shard_of(ssum["conf_ln_g"].reshape(1, D), 1), conf_ln_b=shard_of(ssum["conf_ln_b"].reshape(1, D), 1),
        conf_w_pw2=g_big["conf_w_pw2"], conf_b_pw2=shard_of(ssum["conf_b_pw2"].reshape(1, D), 1),
        mlp_w_in=g_big["mlp_w_in"], mlp_w_out=g_big["mlp_w_out"], final_g=ssum["final_g"].reshape(D))

    adamw_of("w_ada", g_w_ada)
    rest = [k for k in names if k not in ("w_ada",) + _BIG]
    d_, m_, v_ = _adamw_many("adamw_small", [w[k] for k in rest], [grads[k] for k in rest],
                             [m[k] for k in rest], [v[k] for k in rest])
    for k, dd, mm, vv in zip(rest, d_, m_, v_):
        delta[k], new_m[k], new_v[k] = dd, mm, vv

    return (loss, grad_x[None], *[grads[k] for k in names], *[delta[k] for k in names],
            *[new_m[k] for k in names], *[new_v[k] for k in names])
```

```python
import functools
import math

import jax
import jax.numpy as jnp
from jax import lax
from jax.experimental import pallas as pl
from jax.experimental.pallas import tpu as pltpu

F32 = jnp.float32
BF16 = jnp.bfloat16

D = 1024
T = 2048
TC = 256
TA = T + TC
R = 1280
RH = R // 2
NQ = 4 * RH
FF = 4096
N_BLK = 16
BLK = R // N_BLK
GRID_W = 64
EPS = 1e-6
RG_C = 8.0
CONF_KW = 31
REC_KW = 4
LANE = 128
ROW_TILE = 256
HALO = 16
RG_TILE = 128
PACK_ROWS = 512
MM_TILE = 1024
REC_TILE = TA // 2
CW_REC = 640
CW_CONF = 512
V7X_VMEM_BYTES = 64 * 1024 * 1024
VMEM_LIMIT = V7X_VMEM_BYTES - 8 * 1024 * 1024

ADAM_LR = 0.001
ADAM_B1 = 0.9
ADAM_B2 = 0.999
ADAM_EPS = 1e-08
ADAM_WD = 0.01
ADAM_STEP = 10

MESH = pl.DeviceIdType.MESH
ANY = pl.BlockSpec(memory_space=pl.ANY)


def _sds(shape, dtype):
    return jax.ShapeDtypeStruct(tuple(shape), dtype)


def _pcall(body, **kw):
    return pl.pallas_call(body, **kw)


def _cparams():
    return pltpu.CompilerParams(vmem_limit_bytes=VMEM_LIMIT)


def _full_spec(arr):
    nd = arr.ndim
    return pl.BlockSpec(arr.shape, lambda *ids, _n=nd: (0,) * _n)


def _sum0(v):
    return jnp.sum(v, axis=0, keepdims=True)


def _tiled(name, fn, grid, ins, vecs, outs, vec_outs=(), vec_refs=False):
    n_in, n_vec, n_out = len(ins), len(vecs), len(outs)
    n_grid = len(grid)

    def kern(*refs):
        ids = [pl.program_id(a) for a in range(n_grid)]
        tin = [r[...] for r in refs[:n_in]]
        vin = list(refs[n_in:n_in + n_vec]) if vec_refs else [r[...] for r in refs[n_in:n_in + n_vec]]
        o_refs = refs[n_in + n_vec:n_in + n_vec + n_out]
        a_refs = refs[n_in + n_vec + n_out:]
        tout, incs = fn(ids, tin, vin)
        for r, v in zip(o_refs, tout):
            r[...] = v.astype(r.dtype)
        if a_refs:
            first = functools.reduce(jnp.logical_and, [i == 0 for i in ids])

            @pl.when(first)
            def _():
                for r in a_refs:
                    r[...] = jnp.zeros_like(r)

            for r, v in zip(a_refs, incs):
                r[...] += v

    out_shape = [o for o, _ in outs] + [_sds(s, F32) for s in vec_outs]
    out_specs = [s for _, s in outs] + [
        pl.BlockSpec(tuple(s), lambda *ids, _n=len(s): (0,) * _n) for s in vec_outs]
    res = _pcall(
        kern, name=name, grid=tuple(grid),
        in_specs=[s for _, s in ins] + [_full_spec(v) for v in vecs],
        out_specs=out_specs, out_shape=out_shape, compiler_params=_cparams(),
    )(*[a for a, _ in ins], *vecs)
    return list(res)


def _rows(arr, ncols=None, tm=ROW_TILE, off=0, col=0, clamp_lo=False):
    ncols = arr.shape[1] if ncols is None else ncols
    if clamp_lo:
        return arr, pl.BlockSpec((tm, ncols), lambda i: (jnp.maximum(i + off, 0), col))
    return arr, pl.BlockSpec((tm, ncols), lambda i: (i + off, col))


def _orow(nrows, ncols, dtype, tm=ROW_TILE, off=0, clamp_lo=False):
    if clamp_lo:
        return _sds((nrows, ncols), dtype), pl.BlockSpec((tm, ncols), lambda i: (jnp.maximum(i + off, 0), 0))
    return _sds((nrows, ncols), dtype), pl.BlockSpec((tm, ncols), lambda i: (i + off, 0))


_NN = (((1,), (0,)), ((), ()))
_TN = (((0,), (0,)), ((), ()))
_NT = (((1,), (1,)), ((), ()))


def _mm(name, a, b, dims, grid, a_spec, b_spec, out, acc_shape, extra=(), a_pre=None, epi=None):
    n_k = grid[2]
    n_ex = len(extra)

    def kern(a_ref, b_ref, *rest):
        ex = rest[:n_ex]
        o_refs = rest[n_ex:n_ex + len(out)]
        k = pl.program_id(2)
        av = a_ref[...]
        if a_pre is not None:
            av = a_pre(av)
        part = lax.dot_general(av.astype(BF16), b_ref[...].astype(BF16), dims, preferred_element_type=F32)

        def finish(total):
            vals = [total] if epi is None else epi(total, [e[...] for e in ex])
            for r, v in zip(o_refs, vals):
                r[...] = v.astype(r.dtype)

        if n_k == 1:
            finish(part)
        else:
            acc = rest[-1]

            @pl.when(k == 0)
            def _():
                acc[...] = part

            @pl.when(jnp.logical_and(k > 0, k < n_k - 1))
            def _():
                acc[...] += part

            @pl.when(k == n_k - 1)
            def _():
                finish(acc[...] + part)

    res = _pcall(
        kern, name=name, grid=tuple(grid),
        in_specs=[a_spec, b_spec] + [s for _, s in extra],
        out_specs=[s for _, s in out], out_shape=[o for o, _ in out],
        scratch_shapes=[] if n_k == 1 else [pltpu.VMEM(tuple(acc_shape), F32)], compiler_params=_cparams(),
    )(a, b, *[e for e, _ in extra])
    return list(res)


def _rms(x):
    r = lax.rsqrt(jnp.mean(x * x, axis=-1, keepdims=True) + EPS)
    return x * r, r


def _norm_mod(x, g, sc, sh):
    n, _ = _rms(x)
    return (n * g) * (1.0 + sc) + sh


def _norm_mod_bwd(dh, x, g, sc):
    n, r = _rms(x)
    d_sh = _sum0(dh)
    d_sc = _sum0(dh * (n * g))
    d_g = _sum0(dh * (1.0 + sc) * n)
    dn = dh * (g * (1.0 + sc))
    dx = r * (dn - n * jnp.mean(dn * n, axis=-1, keepdims=True))
    return dx, d_sh, d_sc, d_g


_GELU_K = math.sqrt(2.0 / math.pi)


def _gelu(x):
    t = jnp.tanh(_GELU_K * (x + 0.044715 * x * x * x))
    return 0.5 * x * (1.0 + t), t


def _gelu_grad(x, t):
    return 0.5 * (1.0 + t) + 0.5 * x * (1.0 - t * t) * (_GELU_K * (1.0 + 3.0 * 0.044715 * x * x))


def _sigmoid(x):
    return 0.5 * jnp.tanh(0.5 * x) + 0.5


def _expm1(x):
    p = jnp.full_like(x, 1.0 / 5040.0)
    for c in (1.0 / 720.0, 1.0 / 120.0, 1.0 / 24.0, 1.0 / 6.0, 0.5, 1.0):
        p = p * x + c
    return jnp.where(jnp.abs(x) < 0.3, x * p, jnp.exp(x) - 1.0)


def _softplus_neg(lam):
    return jnp.log1p(jnp.exp(-jnp.abs(lam))) + jnp.maximum(-lam, 0.0)


def _layernorm_parts(x):
    mu = jnp.mean(x, axis=-1, keepdims=True)
    xc = x - mu
    rstd = lax.rsqrt(jnp.mean(xc * xc, axis=-1, keepdims=True) + EPS)
    return xc * rstd, rstd


def _rg_gates(u, wbd, gbias, lam):
    sp = _softplus_neg(lam)
    parts = {}
    for h in range(2):
        uh = u[:, h * RH:(h + 1) * RH]
        g = jnp.dot(uh.astype(BF16), wbd[h], preferred_element_type=F32) + gbias[:, h * NQ:(h + 1) * NQ]
        for d in range(2):
            r = _sigmoid(g[:, (2 * d) * RH:(2 * d + 1) * RH])
            i = _sigmoid(g[:, (2 * d + 1) * RH:(2 * d + 2) * RH])
            sph = sp[d:d + 1, h * RH:(h + 1) * RH]
            la = (-RG_C) * r * sph
            e2 = _expm1(2.0 * la)
            inv_mult = jnp.where(e2 < 0.0, lax.rsqrt(-e2), 0.0)
            parts[(d, h)] = dict(r=r, i=i, la=la, a=jnp.exp(la), e2=e2, mult=-e2 * inv_mult, inv_mult=inv_mult,
                                 uh=uh, sp=sph)
    return parts


def _rg_fwd_fn(ids, tin, vin):
    (u,) = tin
    wbd = vin[0]
    parts = _rg_gates(u, wbd, vin[1][...], vin[2][...])
    outs = []
    for d in range(2):
        a = jnp.concatenate([parts[(d, h)]["a"] for h in range(2)], axis=1)
        b = jnp.concatenate([parts[(d, h)]["mult"] * parts[(d, h)]["i"] * parts[(d, h)]["uh"]
                             for h in range(2)], axis=1)
        outs += [a, b]
    return outs, []


def _rg_bwd_fn(ids, tin, vin):
    u, da_f, db_f, da_r, db_r = tin
    wbd, lam = vin[0], vin[2][...]
    parts = _rg_gates(u, wbd, vin[1][...], lam)
    dab = ((da_f, db_f), (da_r, db_r))
    dsig_lam = -1.0 / (1.0 + jnp.exp(lam))
    du_halves, dpre_halves, dlam = [], [], [[None, None], [None, None]]
    for h in range(2):
        du = jnp.zeros_like(parts[(0, h)]["uh"])
        dpre = []
        for d in range(2):
            p = parts[(d, h)]
            da = dab[d][0][:, h * RH:(h + 1) * RH]
            db = dab[d][1][:, h * RH:(h + 1) * RH]
            d_mult = db * p["i"] * p["uh"]
            d_i = db * p["mult"] * p["uh"]
            du = du + db * p["mult"] * p["i"]
            d_la = da * p["a"] - d_mult * (p["e2"] + 1.0) * p["inv_mult"]
            d_r = d_la * ((-RG_C) * p["sp"])
            dlam[d][h] = _sum0(d_la * ((-RG_C) * p["r"])) * dsig_lam[d:d + 1, h * RH:(h + 1) * RH]
            dpre += [d_r * p["r"] * (1.0 - p["r"]), d_i * p["i"] * (1.0 - p["i"])]
        dpre = jnp.concatenate(dpre, axis=1)
        du = du + lax.dot_general(dpre.astype(BF16), wbd[h], _NT, preferred_element_type=F32)
        du_halves.append(du)
        dpre_halves.append(dpre)
    dpre_all = jnp.concatenate(dpre_halves, axis=1)
    dlam_row = jnp.concatenate([dlam[0][0], dlam[0][1], dlam[1][0], dlam[1][1]], axis=1)
    return [dpre_all, jnp.concatenate(du_halves, axis=1)], [_sum0(dpre_all), dlam_row]


def _tile_flags(i, n_tiles, seq_starts):
    starts_here = functools.reduce(jnp.logical_or, [i == s for s in seq_starts])
    ends_here = functools.reduce(jnp.logical_or, [i + 1 == s for s in seq_starts] + [i + 1 == n_tiles])
    return jnp.logical_not(starts_here), jnp.logical_not(ends_here)


def _halo_specs(col0, cw):
    hb = ROW_TILE // HALO
    prev = pl.BlockSpec((HALO, cw), lambda i, c: (jnp.maximum(i * hb - 1, 0), col0 + c))
    cur = pl.BlockSpec((ROW_TILE, cw), lambda i, c: (i, col0 + c))
    return prev, cur, hb


def _window(prev_ref, cur_ref, next_ref, has_prev, has_next):
    prev = jnp.where(has_prev, prev_ref[...], 0.0)
    nxt = jnp.where(has_next, next_ref[...], 0.0)
    return jnp.concatenate([prev, cur_ref[...], nxt], axis=0)


def _tap_reader(win):
    sub = 8
    n = win.shape[0]
    shifted = {0: win}

    def tap(off):
        s = off % sub
        if s not in shifted:
            shifted[s] = pltpu.roll(win, n - s, axis=0)
        return shifted[s][off - s:off - s + ROW_TILE, :]

    return tap


def _dwconv(name, x, col0, w, bias, pad_left, seq_starts, n_ch, cw=256):
    n_rows = x.shape[0]
    n_tiles = n_rows // ROW_TILE
    n_taps = w.shape[0]
    prev_spec, cur_spec, hb = _halo_specs(col0, cw)
    last_hb = n_rows // HALO - 1
    next_spec = pl.BlockSpec((HALO, cw), lambda i, c: (jnp.minimum((i + 1) * hb, last_hb), col0 + c))

    def kern(prev_ref, cur_ref, next_ref, w_ref, b_ref, o_ref):
        has_prev, has_next = _tile_flags(pl.program_id(0), n_tiles, seq_starts)
        win = _window(prev_ref, cur_ref, next_ref, has_prev, has_next)
        tap = _tap_reader(win)
        wv = w_ref[...]
        acc = jnp.zeros((ROW_TILE, cw), F32) + b_ref[...]
        for k in range(n_taps):
            acc = acc + wv[k:k + 1, :] * tap(HALO + k - pad_left)
        o_ref[...] = acc

    return _pcall(
        kern, name=name, grid=(n_tiles, n_ch // cw),
        in_specs=[prev_spec, cur_spec, next_spec,
                  pl.BlockSpec((n_taps, cw), lambda i, c: (0, c)), pl.BlockSpec((1, cw), lambda i, c: (0, c))],
        out_specs=pl.BlockSpec((ROW_TILE, cw), lambda i, c: (i, c)),
        out_shape=_sds((n_rows, n_ch), F32), compiler_params=_cparams(),
    )(x, x, x, w, bias)


def _dwconv_wgrad(name, dy, x, col0, n_taps, pad_left, seq_starts, n_ch, cw=256):
    n_rows = dy.shape[0]
    n_tiles = n_rows // ROW_TILE
    n_out = -(-(n_taps + 1) // 8) * 8
    prev_spec, cur_spec, hb = _halo_specs(col0, cw)
    last_hb = n_rows // HALO - 1
    next_spec = pl.BlockSpec((HALO, cw), lambda c, i: (jnp.minimum((i + 1) * hb, last_hb), col0 + c))
    prev_spec = pl.BlockSpec((HALO, cw), lambda c, i: (jnp.maximum(i * hb - 1, 0), col0 + c))
    cur_spec = pl.BlockSpec((ROW_TILE, cw), lambda c, i: (i, col0 + c))

    sub = 8

    def fold(v):
        return jnp.sum(v.reshape(ROW_TILE // sub, sub, cw), axis=0)

    def kern(dy_ref, prev_ref, cur_ref, next_ref, o_ref, acc):
        i = pl.program_id(1)
        has_prev, has_next = _tile_flags(i, n_tiles, seq_starts)
        win = _window(prev_ref, cur_ref, next_ref, has_prev, has_next)
        dyv = dy_ref[...]
        tap = _tap_reader(win)

        @pl.when(i == 0)
        def _():
            acc[...] = jnp.zeros_like(acc)

        for k in range(n_taps):
            acc[k] += fold(dyv * tap(HALO + k - pad_left))
        acc[n_taps] += fold(dyv)

        @pl.when(i == n_tiles - 1)
        def _():
            rid = lax.broadcasted_iota(jnp.int32, (n_out, cw), 0)
            res = jnp.zeros((n_out, cw), F32)
            for k in range(n_taps + 1):
                res = res + jnp.where(rid == k, _sum0(acc[k]), 0.0)
            o_ref[...] = res

    return _pcall(
        kern, name=name, grid=(n_ch // cw, n_tiles),
        in_specs=[pl.BlockSpec((ROW_TILE, cw), lambda c, i: (i, c)), prev_spec, cur_spec, next_spec],
        out_specs=pl.BlockSpec((n_out, cw), lambda c, i: (0, c)),
        out_shape=_sds((n_out, n_ch), F32), scratch_shapes=[pltpu.VMEM((n_out, sub, cw), F32)],
        compiler_params=_cparams(),
    )(dy, x, x, x)


N_SCAN = TA // ROW_TILE


def _rev_block(j):
    return jnp.where(j == 0, 0, N_SCAN - j)


def _scan_fwd(a_f, b_f, a_r, b_r):
    fwd_spec = pl.BlockSpec((ROW_TILE, R), lambda i: (i, 0))
    rev_spec = pl.BlockSpec((ROW_TILE, R), lambda i: (_rev_block(i), 0))
    hin_spec = pl.BlockSpec((None, 1, R), lambda i: (i, 0, 0))

    def kern(af, bf, ar, br, yf, yr, hin_f, hin_r, hf_s, hr_s):
        @pl.when(pl.program_id(0) == 0)
        def _():
            hf_s[...] = jnp.zeros_like(hf_s)
            hr_s[...] = jnp.zeros_like(hr_s)

        hin_f[...] = hf_s[...]
        hin_r[...] = hr_s[...]

        def step(s8, carry):
            hf, hr = carry
            t0 = pl.multiple_of(s8 * 8, 8)
            for q in range(8):
                tf = t0 + q
                hf = af[pl.ds(tf, 1), :] * hf + bf[pl.ds(tf, 1), :]
                yf[pl.ds(tf, 1), :] = hf
                tr = ROW_TILE - 1 - tf
                hr = ar[pl.ds(tr, 1), :] * hr + br[pl.ds(tr, 1), :]
                yr[pl.ds(tr, 1), :] = hr
            return hf, hr

        hf, hr = lax.fori_loop(0, ROW_TILE // 8, step, (hf_s[...], hr_s[...]))
        hf_s[...] = hf
        hr_s[...] = hr

    return _pcall(
        kern, name="scan_fwd", grid=(N_SCAN,),
        in_specs=[fwd_spec, fwd_spec, rev_spec, rev_spec],
        out_specs=[fwd_spec, rev_spec, hin_spec, hin_spec],
        out_shape=[_sds((TA, R), F32), _sds((TA, R), F32), _sds((N_SCAN, 1, R), F32), _sds((N_SCAN, 1, R), F32)],
        scratch_shapes=[pltpu.VMEM((1, R), F32), pltpu.VMEM((1, R), F32)], compiler_params=_cparams(),
    )(a_f, b_f, a_r, b_r)


def _scan_bwd(dy, a_f, y_f, hin_f, a_r, y_r, hin_r):
    fwd_spec = pl.BlockSpec((ROW_TILE, R), lambda i: (N_SCAN - 1 - i, 0))
    rev_spec = pl.BlockSpec((ROW_TILE, R), lambda i: (_rev_block(N_SCAN - 1 - i), 0))
    hin_spec = pl.BlockSpec((None, 1, R), lambda i: (N_SCAN - 1 - i, 0, 0))
    last = ROW_TILE - 1

    def kern(dyf, af, yf, hf0, dyr, ar, yr, hr0, daf, dbf, dar, dbr, gf_s, anf_s, gr_s, anr_s):
        @pl.when(pl.program_id(0) == 0)
        def _():
            for r in (gf_s, anf_s, gr_s, anr_s):
                r[...] = jnp.zeros_like(r)

        def one(dy_ref, a_ref, y_ref, da_ref, db_ref, g, an, p, pprev):
            gnew = dy_ref[pl.ds(p, 1), :] + an * g
            db_ref[pl.ds(p, 1), :] = gnew
            da_ref[pl.ds(p, 1), :] = gnew * y_ref[pl.ds(pprev, 1), :]
            return gnew, a_ref[pl.ds(p, 1), :]

        def step(s8, carry):
            gf, anf, gr, anr = carry
            base = s8 * 8
            for q in range(8):
                s = last - (base + q)
                gf, anf = one(dyf, af, yf, daf, dbf, gf, anf, s, s - 1)
                gr, anr = one(dyr, ar, yr, dar, dbr, gr, anr, last - s, last - s + 1)
            return gf, anf, gr, anr

        carry = (gf_s[...], anf_s[...], gr_s[...], anr_s[...])
        carry = lax.fori_loop(0, ROW_TILE // 8 - 1, step, carry)
        gf, anf, gr, anr = carry
        for s in range(7, 0, -1):
            gf, anf = one(dyf, af, yf, daf, dbf, gf, anf, s, s - 1)
            gr, anr = one(dyr, ar, yr, dar, dbr, gr, anr, last - s, last - s + 1)
        gf0 = dyf[0:1, :] + anf * gf
        dbf[0:1, :] = gf0
        daf[0:1, :] = gf0 * hf0[...]
        gr0 = dyr[last:last + 1, :] + anr * gr
        dbr[last:last + 1, :] = gr0
        dar[last:last + 1, :] = gr0 * hr0[...]
        gf_s[...] = gf0
        anf_s[...] = af[0:1, :]
        gr_s[...] = gr0
        anr_s[...] = ar[last:last + 1, :]

    return _pcall(
        kern, name="scan_bwd", grid=(N_SCAN,),
        in_specs=[fwd_spec, fwd_spec, fwd_spec, hin_spec, rev_spec, rev_spec, rev_spec, hin_spec],
        out_specs=[fwd_spec, fwd_spec, rev_spec, rev_spec],
        out_shape=[_sds((TA, R), F32)] * 4,
        scratch_shapes=[pltpu.VMEM((1, R), F32)] * 4, compiler_params=_cparams(),
    )(dy, a_f, y_f, hin_f, dy, a_r, y_r, hin_r)


def _me():
    return lax.axis_index("x"), lax.axis_index("y"), lax.axis_index("c")


def _other_chips(mx, my):
    return [(1 - mx, my), (mx, 1 - my), (1 - mx, 1 - my)]


def _rcopy(src, dst, ssem, rsem, dev):
    return pltpu.make_async_remote_copy(src_ref=src, dst_ref=dst, send_sem=ssem, recv_sem=rsem,
                                        device_id=dev, device_id_type=MESH)


def _allgather8(name, x, dep=None):
    rows, cols = x.shape
    n_dep = len(_behind(dep))

    def kern(x_ref, *rest):
        o_ref, ssem, rsem, lsem = rest[n_dep:]
        mx, my, mc = _me()
        me = 4 * mx + 2 * my + mc
        peers = []
        for k in range(1, 8):
            px = 1 - mx if (k >> 2) & 1 else mx
            py = 1 - my if (k >> 1) & 1 else my
            pc = 1 - mc if k & 1 else mc
            peers.append((px, py, pc))
        mine = pltpu.make_async_copy(x_ref, o_ref.at[me], lsem)
        mine.start()
        sends = [_rcopy(x_ref, o_ref.at[me], ssem.at[k], rsem.at[k], p) for k, p in enumerate(peers)]
        for cp in sends:
            cp.start()
        for k, (px, py, pc) in enumerate(peers):
            _rcopy(x_ref, o_ref.at[4 * px + 2 * py + pc], ssem.at[k], rsem.at[k], (px, py, pc)).wait_recv()
        for cp in sends:
            cp.wait_send()
        mine.wait()

    return _pcall(
        kern, name=name, in_specs=[ANY] * (1 + n_dep), out_specs=ANY, out_shape=_sds((8, rows, cols), F32),
        scratch_shapes=[pltpu.SemaphoreType.DMA((7,)), pltpu.SemaphoreType.DMA((7,)), pltpu.SemaphoreType.DMA(())],
    )(x, *_behind(dep))


def _reduce_pair(name, gs):
    n = len(gs)

    def kern(*refs):
        g, o = refs[:n], refs[n:2 * n]
        ss, rs = refs[2 * n:]
        mx, my, mc = _me()
        sib = (mx, my, 1 - mc)
        sends = []
        for t in range(n):
            for j in range(4):
                cp = _rcopy(g[t].at[j, 1 - mc], o[t].at[j], ss.at[4 * t + j], rs.at[4 * t + j], sib)
                cp.start()
                sends.append(cp)
        for cp in sends:
            cp.wait_recv()
        for cp in sends:
            cp.wait_send()

    dma = pltpu.SemaphoreType.DMA
    return _pcall(
        kern, name=name, in_specs=[ANY] * n, out_specs=[ANY] * n,
        out_shape=[_sds((4,) + g.shape[2:], g.dtype) for g in gs],
        scratch_shapes=[dma((4 * n,)), dma((4 * n,))],
    )(*gs)


def _share_halves(name, fulls):
    n = len(fulls)

    def kern(*refs):
        o = refs[n:2 * n]
        ss, rs = refs[2 * n:]
        mx, my, mc = _me()
        sib = (mx, my, 1 - mc)
        sends = []
        for t in range(n):
            cp = _rcopy(o[t].at[mc], o[t].at[mc], ss.at[t], rs.at[t], sib)
            cp.start()
            sends.append(cp)
        for t in range(n):
            _rcopy(o[t].at[1 - mc], o[t].at[1 - mc], ss.at[t], rs.at[t], sib).wait_recv()
        for cp in sends:
            cp.wait_send()

    dma = pltpu.SemaphoreType.DMA
    return _pcall(
        kern, name=name, in_specs=[ANY] * n, out_specs=[ANY] * n,
        out_shape=[_sds(f.shape, f.dtype) for f in fulls], input_output_aliases={t: t for t in range(n)},
        scratch_shapes=[dma((n,)), dma((n,))],
    )(*fulls)


def _tiled_sp(name, fn, grid, sp, ins, outs):
    n_in = len(ins)

    def kern(sp_ref, *refs):
        tout = fn([r[...] for r in refs[:n_in]])
        for r, v in zip(refs[n_in:], tout):
            r[...] = v.astype(r.dtype)

    gs = pltpu.PrefetchScalarGridSpec(num_scalar_prefetch=1, grid=tuple(grid),
                                      in_specs=[s for _, s in ins], out_specs=[s for _, s in outs])
    res = _pcall(kern, name=name, grid_spec=gs, out_shape=[o for o, _ in outs], compiler_params=_cparams(),
                 )(sp, *[a for a, _ in ins])
    return list(res)


def _row_tile(rows, cols, itemsize=4, budget=2 * 1024 * 1024):
    tr = rows
    while tr * cols * itemsize > budget and tr % 32 == 0:
        tr //= 2
    return tr


def _place_big(shards, place):
    slots = []
    for t, s in enumerate(shards):
        rr, cc = s.shape[1], s.shape[2]
        tr = _row_tile(rr, cc)
        (slot,) = _tiled_sp(
            f"place{t}", lambda tin: [tin[0]], (2, rr // tr), place,
            [(s, pl.BlockSpec((None, tr, cc), lambda h, i, sp: (h, i, 0)))],
            [(_sds((4, 2, rr, cc), BF16), pl.BlockSpec((None, None, tr, cc), lambda h, i, sp: (sp[0], h, i, 0)))])
        slots.append(slot)
    return slots


def _allreduce_small_begin(vec, place):
    hr = vec.shape[0] // 2
    tr = _row_tile(hr, LANE)
    blk = (None, None, tr, LANE)
    (pair,) = _tiled_sp(
        "small_place", lambda tin: [tin[0]], (2, hr // tr), place,
        [(vec.reshape(2, hr, LANE), pl.BlockSpec((None, tr, LANE), lambda h, i, sp: (h, i, 0)))],
        [(_sds((2, 2, hr, LANE), F32), pl.BlockSpec(blk, lambda h, i, sp: (sp[1], h, i, 0)))])
    (pair,) = _share_halves("small_share", [pair])
    (slot,) = _tiled_sp(
        "small_pair_add", lambda tin: [tin[0] + tin[1]], (2, hr // tr), place,
        [(pair, pl.BlockSpec(blk, lambda h, i, sp: (0, h, i, 0))),
         (pair, pl.BlockSpec(blk, lambda h, i, sp: (1, h, i, 0)))],
        [(_sds((4, 2, hr, LANE), F32), pl.BlockSpec(blk, lambda h, i, sp: (sp[0], h, i, 0)))])
    fly, sems, token = _gather_start("small_start", [slot], ((0,),), pair)
    return (fly, sems), token


def _allreduce_small_end(state, after):
    fly, sems = state
    (chips,) = _swap_halves("small_swap", _gather_wait("small_wait", fly, *sems, after))
    hr = chips.shape[2]
    tr = _row_tile(hr, LANE)
    blk = (None, None, tr, LANE)
    (total,) = _tiled(
        "small_chip_sum", lambda ids, tin, vin: ([((tin[0] + tin[1]) + tin[2]) + tin[3]], []), (2, hr // tr),
        [(chips, pl.BlockSpec(blk, lambda h, i, _j=j: (_j, h, i, 0))) for j in range(4)], [],
        [(_sds((2, hr, LANE), F32), pl.BlockSpec((None, tr, LANE), lambda h, i: (h, i, 0)))])
    return total.reshape(2 * hr, LANE)


SEM =pl.BlockSpec(memory_space=pltpu.SEMAPHORE)
_DATAFLOW = pltpu.SideEffectType.DATAFLOW_SIDE_EFFECTING


def _gather_start(name, slots, groups, after):
    n = len(slots)

    def kern(*refs):
        o = refs[n + 1:2 * n + 1]
        sems, token = refs[2 * n + 1:-1], refs[-1]
        mx, my, mc = _me()
        j0 = 2 * mx + my
        for gi, grp in enumerate(groups):
            for k, t in enumerate(grp):
                for q, (qx, qy) in enumerate(_other_chips(mx, my)):
                    _rcopy(o[t].at[j0, mc], o[t].at[j0, mc], sems[2 * gi].at[3 * k + q],
                           sems[2 * gi + 1].at[3 * k + q], (qx, qy, mc)).start()
        token[...] = jnp.zeros_like(token)

    sem_shapes = []
    for grp in groups:
        sem_shapes += [pltpu.SemaphoreType.DMA((3 * len(grp),))] * 2
    res = _pcall(
        kern, name=name, in_specs=[ANY] * (n + 1),
        out_specs=[ANY] * n + [SEM] * len(sem_shapes) + [pl.BlockSpec(memory_space=pltpu.VMEM)],
        out_shape=[_sds(w.shape, w.dtype) for w in slots] + sem_shapes + [_sds((8, LANE), F32)],
        input_output_aliases={t: t for t in range(n)},
        compiler_params=pltpu.CompilerParams(has_side_effects=_DATAFLOW),
    )(*slots, after)
    return list(res[:n]), list(res[n:-1]), res[-1]


def _gather_wait(name, bufs, ssem, rsem, after):
    n = len(bufs)

    def kern(*refs):
        b = refs[:n]
        ssem_ref, rsem_ref = refs[n], refs[n + 1]
        mx, my, mc = _me()
        j0 = 2 * mx + my
        for k in range(n):
            for q, (qx, qy) in enumerate(_other_chips(mx, my)):
                jq = 2 * qx + qy
                _rcopy(b[k].at[jq, mc], b[k].at[jq, mc], ssem_ref.at[3 * k + q], rsem_ref.at[3 * k + q],
                       (qx, qy, mc)).wait_recv()
                _rcopy(b[k].at[j0, mc], b[k].at[j0, mc], ssem_ref.at[3 * k + q], rsem_ref.at[3 * k + q],
                       (qx, qy, mc)).wait_send()

    return list(_pcall(
        kern, name=name, in_specs=[ANY] * n + [SEM, SEM, ANY], out_specs=[ANY] * n,
        out_shape=[_sds(w.shape, w.dtype) for w in bufs], input_output_aliases={k: k for k in range(n)},
        compiler_params=pltpu.CompilerParams(has_side_effects=_DATAFLOW),
    )(*bufs, ssem, rsem, after))


def _swap_halves(name, bufs):
    n = len(bufs)

    def kern(*refs):
        o = refs[n:2 * n]
        ss, rs = refs[2 * n:]
        mx, my, mc = _me()
        sib = (mx, my, 1 - mc)
        sends = []
        for k in range(n):
            for q, (qx, qy) in enumerate(_other_chips(mx, my)):
                jq = 2 * qx + qy
                cp = _rcopy(o[k].at[jq, mc], o[k].at[jq, mc], ss.at[3 * k + q], rs.at[3 * k + q], sib)
                cp.start()
                sends.append(cp)
        for k in range(n):
            for q, (qx, qy) in enumerate(_other_chips(mx, my)):
                jq = 2 * qx + qy
                _rcopy(o[k].at[jq, 1 - mc], o[k].at[jq, 1 - mc], ss.at[3 * k + q], rs.at[3 * k + q], sib).wait_recv()
        for cp in sends:
            cp.wait_send()

    dma = pltpu.SemaphoreType.DMA
    return list(_pcall(
        kern, name=name, in_specs=[ANY] * n, out_specs=[ANY] * n,
        out_shape=[_sds(w.shape, w.dtype) for w in bufs], input_output_aliases={k: k for k in range(n)},
        scratch_shapes=[dma((3 * n,)), dma((3 * n,))],
    )(*bufs))


def _chips_start(name, sums):
    n = len(sums)

    def kern(*refs):
        s, land = refs[n:2 * n], refs[2 * n:3 * n]
        ssem, rsem, token = refs[3 * n:]
        mx, my, mc = _me()
        for k in range(n):
            for q, (qx, qy) in enumerate(_other_chips(mx, my)):
                _rcopy(s[k].at[2 * qx + qy], land[k].at[q], ssem.at[3 * k + q], rsem.at[3 * k + q], (qx, qy, mc)).start()
        token[...] = jnp.zeros_like(token)

    dma = pltpu.SemaphoreType.DMA
    res = _pcall(
        kern, name=name, in_specs=[ANY] * n,
        out_specs=[ANY] * (2 * n) + [SEM, SEM, pl.BlockSpec(memory_space=pltpu.VMEM)],
        out_shape=[_sds(s.shape, s.dtype) for s in sums] + [_sds((3,) + s.shape[1:], s.dtype) for s in sums]
        + [dma((3 * n,)), dma((3 * n,)), _sds((8, LANE), F32)],
        input_output_aliases={k: k for k in range(n)},
        compiler_params=pltpu.CompilerParams(has_side_effects=_DATAFLOW),
    )(*sums)
    return (list(res[:n]), list(res[n:2 * n]), res[2 * n], res[2 * n + 1]), res[2 * n + 2]


def _chips_wait(name, sums, lands, ssem, rsem, after):
    n = len(sums)

    def kern(*refs):
        s, land = refs[:n], refs[n:2 * n]
        ssem_ref, rsem_ref = refs[2 * n], refs[2 * n + 1]
        mx, my, mc = _me()
        for k in range(n):
            for q, (qx, qy) in enumerate(_other_chips(mx, my)):
                cp = _rcopy(s[k].at[2 * qx + qy], land[k].at[q], ssem_ref.at[3 * k + q], rsem_ref.at[3 * k + q],
                            (qx, qy, mc))
                cp.wait_recv()
                cp.wait_send()

    res = _pcall(
        kern, name=name, in_specs=[ANY] * (2 * n) + [SEM, SEM, ANY], out_specs=[ANY] * (2 * n),
        out_shape=[_sds(a.shape, a.dtype) for a in list(sums) + list(lands)],
        input_output_aliases={k: k for k in range(2 * n)},
        compiler_params=pltpu.CompilerParams(has_side_effects=_DATAFLOW),
    )(*sums, *lands, ssem, rsem, after)
    return list(res[:n]), list(res[n:])


def _reduce_begin(tag, parts, place):
    theirs = _reduce_pair(f"reduce_pair_{tag}", parts)
    sums = []
    for k, (p, o) in enumerate(zip(parts, theirs)):
        rr, cc = p.shape[2], p.shape[3]
        tr = _row_tile(rr, cc)
        (s_k,) = _tiled_sp(
            f"pair_add_{tag}{k}", lambda tin: [tin[0].astype(F32) + tin[1].astype(F32)], (4, rr // tr), place,
            [(p, pl.BlockSpec((None, None, tr, cc), lambda j, i, sp: (j, sp[1], i, 0))),
             (o, pl.BlockSpec((None, tr, cc), lambda j, i, sp: (j, i, 0)))],
            [(_sds((4, rr, cc), BF16), pl.BlockSpec((None, tr, cc), lambda j, i, sp: (j, i, 0)))])
        sums.append(s_k)
    return _chips_start(f"chips_start_{tag}", sums)


def _reduce_end(tag, flying, place, after):
    sums, lands = _chips_wait(f"chips_wait_{tag}", *flying, after)
    fulls = []
    for k, (s, q) in enumerate(zip(sums, lands)):
        rr, cc = q.shape[1], q.shape[2]
        tr = _row_tile(rr, cc)

        def add4(tin):
            return [((tin[0].astype(F32) + tin[1].astype(F32)) + tin[2].astype(F32)) + tin[3].astype(F32)]

        ins = [(s, pl.BlockSpec((None, tr, cc), lambda i, sp: (sp[0], i, 0)))]
        ins += [(q, pl.BlockSpec((None, tr, cc), lambda i, sp, _k=kk: (_k, i, 0))) for kk in range(3)]
        (f_k,) = _tiled_sp(f"chip_add_{tag}{k}", add4, (rr // tr,), place, ins,
                           [(_sds((2, rr, cc), F32), pl.BlockSpec((None, tr, cc), lambda i, sp: (sp[1], i, 0)))])
        fulls.append(f_k)
    return fulls


def _pack(parts, PACK_ROWS=PACK_ROWS):
    flat, offs, pos = [], [], 0
    for p in parts:
        v = p.reshape(-1).astype(F32)
        n = -(-v.shape[0] // LANE) * LANE
        flat.append(jnp.pad(v, (0, n - v.shape[0])))
        offs.append((pos, v.shape[0], p.shape))
        pos += n
    total = -(-pos // (PACK_ROWS * LANE)) * PACK_ROWS * LANE
    flat.append(jnp.zeros((total - pos,), F32))
    return jnp.concatenate(flat).reshape(-1, LANE), offs


def _unpack(vec, offs):
    v = vec.reshape(-1)
    return [v[p:p + n].reshape(shape) for p, n, shape in offs]


def _adamw_math(wv, gv, mv, vv):
    bc1 = 1.0 - ADAM_B1 ** ADAM_STEP
    bc2 = 1.0 - ADAM_B2 ** ADAM_STEP
    mn = ADAM_B1 * mv + (1.0 - ADAM_B1) * gv
    vn = ADAM_B2 * vv + (1.0 - ADAM_B2) * (gv * gv)
    delta = -ADAM_LR * ((mn / bc1) / (jnp.sqrt(vn / bc2) + ADAM_EPS) + ADAM_WD * wv)
    return delta, mn, vn


def _adamw(name, w, g, m, v):
    rows, cols = w.shape
    tr = rows
    for cand in (512, 256, 128, 64, 32, 16, 8):
        if rows % cand == 0 and cand * cols * 4 <= 2 * 1024 * 1024:
            tr = cand
            break

    def fn(ids, tin, vin):
        return list(_adamw_math(*tin)), []

    spec = pl.BlockSpec((tr, cols), lambda i: (i, 0))
    outs = [(_sds((rows, cols), F32), spec)] * 3
    return _tiled(name, fn, (rows // tr,), [(a, spec) for a in (w, g, m, v)], [], outs)


def _adamw_many(name, ws, gs, ms, vs):
    n = len(ws)
    views = [(-1, a.shape[-1]) if a.ndim > 1 else (1, -1) for a in ws]
    flat = lambda arrs: [a.reshape(vw) for a, vw in zip(arrs, views)]

    def kern(*refs):
        ins, outs = refs[:4 * n], refs[4 * n:]
        for t in range(n):
            res = _adamw_math(*[ins[q * n + t][...] for q in range(4)])
            for q in range(3):
                outs[q * n + t][...] = res[q]

    shapes = [_sds(a.shape, F32) for a in flat(ws)]
    res = _pcall(kern, name=name, out_shape=shapes * 3, compiler_params=_cparams(),
                 )(*flat(ws), *flat(gs), *flat(ms), *flat(vs))
    back = lambda part: [a.reshape(w.shape) for a, w in zip(part, ws)]
    return back(res[:n]), back(res[n:2 * n]), back(res[2 * n:])


def _pos_embed():
    n_rows = T // GRID_W
    q = D // 4
    omega = 1.0 / (10000.0 ** (jnp.arange(q, dtype=F32) / q))
    er = jnp.arange(n_rows, dtype=jnp.int32).astype(F32)[:, None] * omega[None, :]
    ec = jnp.arange(GRID_W, dtype=jnp.int32).astype(F32)[:, None] * omega[None, :]
    by_row = jnp.concatenate([jnp.sin(er), jnp.cos(er)], axis=-1)
    by_col = jnp.concatenate([jnp.sin(ec), jnp.cos(ec)], axis=-1)
    return jnp.concatenate([jnp.repeat(by_row, GRID_W, axis=0), jnp.tile(by_col, (n_rows, 1))], axis=-1)


def _dense_gates(w_a, w_x):
    rows = jnp.stack([w_a[0], w_x[0], w_a[1], w_x[1]]).reshape(4, 2, RH, BLK)
    mask, spread = _block_mask(), _block_spread().T.astype(BF16)

    def kern(r_ref, m_ref, s_ref, o_ref):
        tiled = jnp.dot(r_ref[...].astype(BF16), s_ref[...], preferred_element_type=F32)
        o_ref[...] = (tiled * m_ref[...]).astype(o_ref.dtype)

    return _pcall(
        kern, name="gates_dense", grid=(2, 4),
        in_specs=[pl.BlockSpec((None, None, RH, BLK), lambda h, q: (q, h, 0, 0)),
                  pl.BlockSpec((RH, RH), lambda h, q: (0, 0)), pl.BlockSpec((BLK, RH), lambda h, q: (0, 0))],
        out_specs=pl.BlockSpec((None, RH, RH), lambda h, q: (h, 0, q)),
        out_shape=_sds((2, RH, NQ), BF16),
    )(rows, mask, spread)


def _block_mask():
    r = lax.broadcasted_iota(jnp.int32, (RH, RH), 0) // BLK
    c = lax.broadcasted_iota(jnp.int32, (RH, RH), 1) // BLK
    return (r == c).astype(F32)


def _block_spread():
    c = lax.broadcasted_iota(jnp.int32, (RH, BLK), 0) % BLK
    j = lax.broadcasted_iota(jnp.int32, (RH, BLK), 1)
    return (c == j).astype(F32)


def _fold_blocks(dense, mask, spread):
    return jnp.dot(dense * mask, spread, preferred_element_type=F32, precision=lax.Precision.HIGHEST)


def _gate_block_grads(folded):
    per = N_BLK // 2
    kinds = [jnp.concatenate([folded[h, q].reshape(per, BLK, BLK) for h in range(2)], axis=0) for q in range(4)]
    return jnp.stack([kinds[0], kinds[2]]), jnp.stack([kinds[1], kinds[3]])


def _gate_bias_dense(b_a, b_x):
    cols = []
    for h in range(2):
        for src in (b_a[0], b_x[0], b_a[1], b_x[1]):
            cols.append(src.reshape(R)[h * RH:(h + 1) * RH])
    return jnp.concatenate(cols).reshape(1, 2 * NQ)


def _gate_bias_grads(dgb):
    v = dgb.reshape(2, 4, RH)
    kinds = [jnp.concatenate([v[0, q], v[1, q]]).reshape(N_BLK, BLK) for q in range(4)]
    return jnp.stack([kinds[0], kinds[2]]), jnp.stack([kinds[1], kinds[3]])


def _residual_epilogue(next_norm):
    def epi(acc, ex):
        x_new = ex[0] + ex[1] * acc
        outs = [acc, x_new]
        if next_norm:
            outs.append(_norm_mod(x_new, ex[-3], ex[-2], ex[-1]))
        return outs
    return epi


def _mlp_fwd(tag, x_in, h, gate, w_in, w_out, next_norm=None):
    tm = MM_TILE
    (r,) = _mm(f"{tag}_in", h, w_in, _NN, (T // tm, 4, 1),
               pl.BlockSpec((tm, D), lambda i, j, k: (i, 0)), pl.BlockSpec((None, D, D), lambda i, j, k: (j, 0, 0)),
               [(_sds((T, FF), BF16), pl.BlockSpec((tm, D), lambda i, j, k: (i, j)))], (tm, D),
               epi=lambda acc, ex: [jnp.maximum(acc, 0.0)])
    row_spec = pl.BlockSpec((tm, D), lambda i, j, k: (i, 0))
    outs = [(_sds((T, D), F32), row_spec)] * 2 + ([(_sds((T, D), BF16), row_spec)] if next_norm else [])
    res = _mm(f"{tag}_out", r, w_out, _NN, (T // tm, 1, FF // D),
              pl.BlockSpec((tm, D), lambda i, j, k: (i, k)), pl.BlockSpec((D, D), lambda i, j, k: (k, 0)),
              outs, (tm, D),
              extra=[(x_in, row_spec), (gate, _full_spec(gate))] + [(v, _full_spec(v)) for v in next_norm or ()],
              a_pre=lambda a: a * a, epi=_residual_epilogue(next_norm))
    return dict(h=h, r=r, o=res[0], x_in=x_in), res[1], (res[2] if next_norm else None)


def _behind(dep):
    return [] if dep is None else [dep]


def _gate_bwd(tag, dx, o, gate, dep=None):
    def fn(ids, t, v):
        d_o = t[0] * v[0]
        return [d_o], [_sum0(t[0] * t[1]), _sum0(d_o)]
    return _tiled(f"{tag}_gate_bwd", fn, (T // ROW_TILE,), [_rows(dx), _rows(o)], [gate] + _behind(dep),
                  [_orow(T, D, BF16)], [(1, D), (1, D)])


def _norm_bwd(tag, dx_res, dh, dh_off, x, g_norm, sc, with_dx=True, dep=None):
    n_t = x.shape[0] // ROW_TILE

    def fn(ids, t, v):
        if with_dx:
            dres, dhv, xv = t
        else:
            dhv, xv = t
        dxv, d_sh, d_sc, d_g = _norm_mod_bwd(dhv, xv, v[0], v[1])
        return ([dres + dxv] if with_dx else []), [d_sh, d_sc, d_g]

    ins = ([_rows(dx_res)] if with_dx else []) + [_rows(dh, off=dh_off), _rows(x)]
    outs = [_orow(x.shape[0], D, F32)] if with_dx else []
    return _tiled(f"{tag}_norm_bwd", fn, (n_t,), ins, [g_norm, sc] + _behind(dep), outs, [(1, D)] * 3)


def _mlp_bwd(tag, dx, saved, g_norm, sc, gate, w_in, w_out):
    d_o, d_gate, _ = _gate_bwd(tag, dx, saved["o"], gate)
    tm = MM_TILE
    r = saved["r"]
    (da,) = _mm(f"{tag}_dz", d_o, w_out, _NT, (T // tm, FF // D, 1),
                pl.BlockSpec((tm, D), lambda i, j, k: (i, 0)), pl.BlockSpec((D, D), lambda i, j, k: (j, 0)),
                [(_sds((T, FF), BF16), pl.BlockSpec((tm, D), lambda i, j, k: (i, j)))], (tm, D),
                extra=[(r, pl.BlockSpec((tm, D), lambda i, j, k: (i, j)))],
                epi=lambda acc, ex: [acc * (2.0 * ex[0].astype(F32))])
    tk = MM_TILE
    (dw_out,) = _mm(f"{tag}_dwout", r, d_o, _TN, (FF // tm, 1, T // tk),
                    pl.BlockSpec((tk, tm), lambda i, j, k: (k, i)), pl.BlockSpec((tk, D), lambda i, j, k: (k, 0)),
                    [(_sds((FF, D), BF16), pl.BlockSpec((tm, D), lambda i, j, k: (i, 0)))], (tm, D),
                    a_pre=lambda a: a * a)
    (dh,) = _mm(f"{tag}_dh", da, w_in, _NT, (T // tm, 1, 4),
                pl.BlockSpec((tm, D), lambda i, j, k: (i, k)), pl.BlockSpec((None, D, D), lambda i, j, k: (k, 0, 0)),
                [(_sds((T, D), F32), pl.BlockSpec((tm, D), lambda i, j, k: (i, 0)))], (tm, D))
    (dw_in,) = _mm(f"{tag}_dwin", saved["h"], da, _TN, (D // tm, 4, T // tk),
                   pl.BlockSpec((tk, tm), lambda i, j, k: (k, i)), pl.BlockSpec((tk, D), lambda i, j, k: (k, j)),
                   [(_sds((4, D, D), BF16), pl.BlockSpec((None, tm, D), lambda i, j, k: (j, i, 0)))], (tm, D))
    dx_in, d_sh, d_sc, d_g = _norm_bwd(tag, dx, dh, 0, saved["x_in"], g_norm, sc)
    return dx_in, dw_in, dw_out, dict(sh=d_sh, sc=d_sc, gate=d_gate, g_norm=d_g)


def _local_step(x, ctx, tgt, mods, cmods, norm_g, final_g, rec, conf, wg, on_grads=None, start_dep=None):
    on_grads = on_grads or (lambda group, dws: None)
    n_t = T // ROW_TILE
    row = lambda v: v.reshape(1, -1)
    m0 = [row(mods[0, q]) for q in range(6)]
    m1 = [row(mods[1, q]) for q in range(6)]
    g00, g01, g10, g11 = (row(norm_g[0, 0]), row(norm_g[0, 1]), row(norm_g[1, 0]), row(norm_g[1, 1]))
    csh, csc = row(cmods[0]), row(cmods[1])
    pos = _pos_embed()

    def prep0(ids, t, v):
        cx, xv, pv = t
        is_ctx = ids[0] == 0
        xin = jnp.where(is_ctx, cx, xv + pv)
        sh = jnp.where(is_ctx, v[3], v[1])
        sc = jnp.where(is_ctx, v[4], v[2])
        return [_norm_mod(xin, v[0], sc, sh), xv + pv], []

    hcat, x0 = _tiled(
        "prep0", prep0, (N_SCAN,),
        [(ctx, pl.BlockSpec((ROW_TILE, D), lambda i: (0, 0))), _rows(x, off=-1, clamp_lo=True),
         _rows(pos, off=-1, clamp_lo=True)],
        [g00, m0[0], m0[1], csh, csc] + _behind(start_dep),
        [_orow(TA, D, BF16), _orow(T, D, F32, off=-1, clamp_lo=True)])

    tm_a = REC_TILE
    w_rec = wg("rec", hcat)
    (a_in,) = _mm("rec_in", hcat, w_rec["rec_w_in"], _NN, (TA // tm_a, 4, 1),
                  pl.BlockSpec((tm_a, D), lambda i, j, k: (i, 0)),
                  pl.BlockSpec((None, D, RH), lambda i, j, k: (j, 0, 0)),
                  [(_sds((TA, 2 * R), F32), pl.BlockSpec((tm_a, RH), lambda i, j, k: (i, j)))], (tm_a, RH))
    rec_starts = (0, 1)
    u = _dwconv("rec_conv", a_in, R // CW_REC, rec["conv_w"], row(rec["conv_b"]), 1, rec_starts, R, CW_REC)
    wbd = _dense_gates(rec["w_a"], rec["w_x"])
    gbias = _gate_bias_dense(rec["b_a"], rec["b_x"])
    lam = rec["lam"]
    a_f, b_f, a_r, b_r = _tiled("rg_fwd", _rg_fwd_fn, (TA // RG_TILE,), [_rows(u, tm=RG_TILE)], [wbd, gbias, lam],
                                [_orow(TA, R, F32, tm=RG_TILE)] * 4, vec_refs=True)
    y_f, y_r, hin_f, hin_r = _scan_fwd(a_f, b_f, a_r, b_r)

    def rec_mid(ids, t, v):
        gp, yf, yr = t
        g, _ = _gelu(gp)
        return [g * (yf + yr)], []

    (m_rec,) = _tiled("rec_mid", rec_mid, (n_t,),
                      [_rows(a_in, R, off=1), _rows(y_f, off=1), _rows(y_r, off=1)], [], [_orow(T, R, BF16)])
    tm = MM_TILE
    row_spec = pl.BlockSpec((tm, D), lambda i, j, k: (i, 0))
    norm_mlp0 = (g01, m0[4], m0[3])
    o_rec, x1, h_mlp0 = _mm(
        "rec_out", m_rec, w_rec["rec_w_out"], _NN, (T // tm, 1, 1),
        pl.BlockSpec((tm, R), lambda i, j, k: (i, 0)), pl.BlockSpec((R, D), lambda i, j, k: (0, 0)),
        [(_sds((T, D), F32), row_spec)] * 2 + [(_sds((T, D), BF16), row_spec)], (tm, D),
        extra=[(x0, row_spec), (m0[2], _full_spec(m0[2]))] + [(v, _full_spec(v)) for v in norm_mlp0],
        epi=_residual_epilogue(norm_mlp0))
    w_m0 = wg("mlp0", x1)
    mlp0, x2, h1 = _mlp_fwd("mlp0", x1, h_mlp0, m0[5], w_m0["w_in"], w_m0["w_out"], (g10, m1[1], m1[0]))

    b_pw1 = row(conf["b_pw1"])
    w_cf = wg("conf", x2)
    (pre,) = _mm("conf_pw1", h1, w_cf["conf_w_pw1"], _NN, (T // tm, 4, 1),
                 pl.BlockSpec((tm, D), lambda i, j, k: (i, 0)),
                 pl.BlockSpec((None, D, D // 2), lambda i, j, k: (j, 0, 0)),
                 [(_sds((T, 2 * D), F32), pl.BlockSpec((tm, D // 2), lambda i, j, k: (i, j)))], (tm, D // 2),
                 extra=[(b_pw1, pl.BlockSpec((1, D // 2), lambda i, j, k: (0, j)))],
                 epi=lambda acc, ex: [acc + ex[0]])
    (zg,) = _tiled("conf_glu", lambda ids, t, v: ([t[0] * _sigmoid(t[1])], []), (n_t,),
                   [_rows(pre, D, col=0), _rows(pre, D, col=1)], [], [_orow(T, D, F32)])
    conf_starts = (0,)
    zc = _dwconv("conf_conv", zg, 0, conf["conv_w"], row(conf["conv_b"]), CONF_KW // 2, conf_starts, D, CW_CONF)
    ln_g, ln_b = row(conf["ln_g"]), row(conf["ln_b"])

    def ln_silu(ids, t, v):
        nh, _ = _layernorm_parts(t[0])
        ln = nh * v[0] + v[1]
        return [ln * _sigmoid(ln)], []

    (s_conf,) = _tiled("conf_ln", ln_silu, (n_t,), [_rows(zc)], [ln_g, ln_b], [_orow(T, D, BF16)])
    b_pw2 = row(conf["b_pw2"])
    norm_mlp1 = (g11, m1[4], m1[3])
    pw2_epi = _residual_epilogue(norm_mlp1)
    y_conf, x3, h_mlp1 = _mm(
        "conf_pw2", s_conf, w_cf["conf_w_pw2"], _NN, (T // tm, 1, 1),
        row_spec, pl.BlockSpec((D, D), lambda i, j, k: (0, 0)),
        [(_sds((T, D), F32), row_spec)] * 2 + [(_sds((T, D), BF16), row_spec)], (tm, D),
        extra=[(x2, row_spec), (m1[2], _full_spec(m1[2])), (b_pw2, _full_spec(b_pw2))]
        + [(v, _full_spec(v)) for v in norm_mlp1],
        epi=lambda acc, ex: pw2_epi(acc + ex[2], ex))
    w_m1 = wg("mlp1", x3)
    mlp1, x4, _ = _mlp_fwd("mlp1", x3, h_mlp1, m1[5], w_m1["w_in"], w_m1["w_out"])

    fg = row(final_g)

    def head(ids, t, v):
        n, r = _rms(t[0])
        err = n * v[0] - t[1]
        d_out = err * (1.0 / D)
        dn = d_out * v[0]
        dxv = r * (dn - n * jnp.mean(dn * n, axis=-1, keepdims=True))
        part = jnp.sum(_sum0(err * err), axis=1, keepdims=True) * (0.5 / D)
        return [dxv], [part, _sum0(d_out * n)]

    dx4, loss, d_fg = _tiled("head", head, (n_t,), [_rows(x4), _rows(tgt)], [fg], [_orow(T, D, F32)],
                             [(1, 1), (1, D)])

    dx3, dw_in1, dw_out1, dm_mlp1 = _mlp_bwd("mlp1", dx4, mlp1, g11, m1[4], m1[5],
                                             w_m1["w_in"], w_m1["w_out"])
    dep = on_grads("mlp1", (dw_in1, dw_out1))
    d_y, d_g1c, d_bpw2 = _gate_bwd("conf", dx3, y_conf, m1[2], dep)
    tk = MM_TILE
    (dw_pw2,) = _mm("conf_dwpw2", s_conf, d_y, _TN, (D // tm, 1, T // tk),
                    pl.BlockSpec((tk, tm), lambda i, j, k: (k, i)), pl.BlockSpec((tk, D), lambda i, j, k: (k, 0)),
                    [(_sds((D, D), BF16), pl.BlockSpec((tm, D), lambda i, j, k: (i, 0)))], (tm, D))
    (ds,) = _mm("conf_ds", d_y, w_cf["conf_w_pw2"], _NT, (T // tm, 1, 1),
                pl.BlockSpec((tm, D), lambda i, j, k: (i, 0)), pl.BlockSpec((D, D), lambda i, j, k: (0, 0)),
                [(_sds((T, D), F32), pl.BlockSpec((tm, D), lambda i, j, k: (i, 0)))], (tm, D))

    def ln_silu_bwd(ids, t, v):
        dsv, zcv = t
        nh, rstd = _layernorm_parts(zcv)
        ln = nh * v[0] + v[1]
        sg = _sigmoid(ln)
        d_ln = dsv * (sg * (1.0 + ln * (1.0 - sg)))
        d_nh = d_ln * v[0]
        d_zc = rstd * (d_nh - jnp.mean(d_nh, axis=-1, keepdims=True)
                       - nh * jnp.mean(d_nh * nh, axis=-1, keepdims=True))
        return [d_zc], [_sum0(d_ln * nh), _sum0(d_ln)]

    d_zc, d_lng, d_lnb = _tiled("conf_ln_bwd", ln_silu_bwd, (n_t,), [_rows(ds), _rows(zc)], [ln_g, ln_b],
                                [_orow(T, D, F32)], [(1, D), (1, D)])
    d_zg = _dwconv("conf_conv_dx", d_zc, 0, conf["conv_w"][::-1], jnp.zeros((1, D), F32),
                   CONF_KW - 1 - CONF_KW // 2, conf_starts, D, CW_CONF)
    d_cw_conf = _dwconv_wgrad("conf_conv_dw", d_zc, zg, 0, CONF_KW, CONF_KW // 2, conf_starts, D, CW_CONF)

    def glu_bwd(ids, t, v):
        dz, pa, pb = t
        sg = _sigmoid(pb)
        d_a = dz * sg
        d_b = dz * pa * sg * (1.0 - sg)
        return [jnp.concatenate([d_a, d_b], axis=1)], [_sum0(d_a), _sum0(d_b)]

    d_pre, d_b1a, d_b1b = _tiled(
        "conf_glu_bwd", glu_bwd, (n_t,), [_rows(d_zg), _rows(pre, D, col=0), _rows(pre, D, col=1)], [],
        [_orow(T, 2 * D, BF16)], [(1, D), (1, D)])
    (dw_pw1,) = _mm("conf_dwpw1", h1, d_pre, _TN, (D // tm, 4, T // tk),
                    pl.BlockSpec((tk, tm), lambda i, j, k: (k, i)),
                    pl.BlockSpec((tk, D // 2), lambda i, j, k: (k, j)),
                    [(_sds((4, D, D // 2), BF16), pl.BlockSpec((None, tm, D // 2), lambda i, j, k: (j, i, 0)))],
                    (tm, D // 2))
    dep = on_grads("conf", (dw_pw1, dw_pw2))
    (dh1,) = _mm("conf_dh", d_pre, w_cf["conf_w_pw1"], _NT, (T // tm, 1, 4),
                 pl.BlockSpec((tm, D // 2), lambda i, j, k: (i, k)),
                 pl.BlockSpec((None, D, D // 2), lambda i, j, k: (k, 0, 0)),
                 [(_sds((T, D), F32), pl.BlockSpec((tm, D), lambda i, j, k: (i, 0)))], (tm, D))
    dx2, d_sh1c, d_sc1c, d_g10 = _norm_bwd("conf", dx3, dh1, 0, x2, g10, m1[1], dep=dep)

    dx1, dw_in0, dw_out0, dm_mlp0 = _mlp_bwd("mlp0", dx2, mlp0, g01, m0[4], m0[5],
                                             w_m0["w_in"], w_m0["w_out"])
    dep = on_grads("mlp0", (dw_in0, dw_out0))
    d_orec, d_g1r, _ = _gate_bwd("rec", dx1, o_rec, m0[2], dep)
    (dw_rout,) = _mm("rec_dwout", m_rec, d_orec, _TN, (R // RH, 1, T // tk),
                     pl.BlockSpec((tk, RH), lambda i, j, k: (k, i)), pl.BlockSpec((tk, D), lambda i, j, k: (k, 0)),
                     [(_sds((R, D), BF16), pl.BlockSpec((RH, D), lambda i, j, k: (i, 0)))], (RH, D))
    (dm_rec,) = _mm("rec_dm", d_orec, w_rec["rec_w_out"], _NT, (T // tm, 1, 1),
                    pl.BlockSpec((tm, D), lambda i, j, k: (i, 0)), pl.BlockSpec((R, D), lambda i, j, k: (0, 0)),
                    [(_sds((T, R), F32), pl.BlockSpec((tm, R), lambda i, j, k: (i, 0)))], (tm, R))

    def rec_mid_bwd(ids, t, v):
        dmv, gp, yf, yr = t
        g, th = _gelu(gp)
        lat = ids[0] > 0
        d_gp = jnp.where(lat, dmv * (yf + yr) * _gelu_grad(gp, th), 0.0)
        dy = jnp.where(lat, dmv * g, 0.0)
        return [d_gp, dy], []

    d_gp, dy = _tiled("rec_mid_bwd", rec_mid_bwd, (N_SCAN,),
                      [_rows(dm_rec, off=-1, clamp_lo=True), _rows(a_in, R), _rows(y_f), _rows(y_r)], [],
                      [_orow(TA, R, BF16), _orow(TA, R, F32)])
    da_f, db_f, da_r, db_r = _scan_bwd(dy, a_f, y_f, hin_f, a_r, y_r, hin_r)
    d_gpre, d_u, d_gbias, d_lam = _tiled(
        "rg_bwd", _rg_bwd_fn, (TA // RG_TILE,), [_rows(a, tm=RG_TILE) for a in (u, da_f, db_f, da_r, db_r)],
        [wbd, gbias, lam], [_orow(TA, 2 * NQ, BF16, tm=RG_TILE), _orow(TA, R, F32, tm=RG_TILE)],
        [(1, 2 * NQ), (1, 2 * R)], vec_refs=True)
    tk_a = REC_TILE
    blk_mask, blk_spread = _block_mask(), _block_spread()
    (d_wbd,) = _mm("rg_dw", u, d_gpre, _TN, (2, 2, TA // tk_a),
                   pl.BlockSpec((tk_a, RH), lambda i, j, k: (k, i)),
                   pl.BlockSpec((tk_a, NQ // 2), lambda i, j, k: (k, 2 * i + j)),
                   [(_sds((2, 4, RH, BLK), F32), pl.BlockSpec((None, 2, RH, BLK), lambda i, j, k: (i, j, 0, 0)))],
                   (RH, NQ // 2),
                   extra=[(blk_mask, _full_spec(blk_mask)), (blk_spread, _full_spec(blk_spread))],
                   epi=lambda acc, ex: [jnp.stack([_fold_blocks(acc[:, s * RH:(s + 1) * RH], ex[0], ex[1])
                                                   for s in range(2)])])
    d_p = _dwconv("rec_conv_dx", d_u, 0, rec["conv_w"][::-1], jnp.zeros((1, R), F32), REC_KW - 1 - 1,
                  rec_starts, R, CW_REC)
    d_cw_rec = _dwconv_wgrad("rec_conv_dw", d_u, a_in, R // CW_REC, REC_KW, 1, rec_starts, R, CW_REC)
    d_a = jnp.concatenate([d_gp, d_p.astype(BF16)], axis=1)
    (dw_rin,) = _mm("rec_dwin", hcat, d_a, _TN, (D // tm, 4, TA // tk_a),
                    pl.BlockSpec((tk_a, tm), lambda i, j, k: (k, i)), pl.BlockSpec((tk_a, RH), lambda i, j, k: (k, j)),
                    [(_sds((4, D, RH), BF16), pl.BlockSpec((None, tm, RH), lambda i, j, k: (j, i, 0)))], (tm, RH))
    dep = on_grads("rec", (dw_rin, dw_rout))
    (dhcat,) = _mm("rec_dh", d_a, w_rec["rec_w_in"], _NT, (TA // tm_a, 1, 4),
                   pl.BlockSpec((tm_a, RH), lambda i, j, k: (i, k)),
                   pl.BlockSpec((None, D, RH), lambda i, j, k: (k, 0, 0)),
                   [(_sds((TA, D), F32), pl.BlockSpec((tm_a, D), lambda i, j, k: (i, 0)))], (tm_a, D))
    dx0, d_sh1r, d_sc1r, d_g00 = _norm_bwd("rec", dx1, dhcat, 1, x0, g00, m0[1], dep=dep)
    d_csh, d_csc, d_g00c = _norm_bwd("ctx", None, dhcat, 0, ctx, g00, csc, with_dx=False)

    big = dict(rec_w_in=dw_rin, rec_w_out=dw_rout, conf_w_pw1=dw_pw1, conf_w_pw2=dw_pw2,
               mlp_w_in=(dw_in0, dw_in1), mlp_w_out=(dw_out0, dw_out1))
    d_wa, d_wx = _gate_block_grads(d_wbd)
    d_ba, d_bx = _gate_bias_grads(d_gbias)
    d_mod = jnp.concatenate([
        d_sh1r, d_sc1r, d_g1r, dm_mlp0["sh"], dm_mlp0["sc"], dm_mlp0["gate"],
        d_sh1c, d_sc1c, d_g1c, dm_mlp1["sh"], dm_mlp1["sc"], dm_mlp1["gate"]], axis=1).reshape(2, 6 * D)
    small = dict(
        d_mod=d_mod, d_cmod=jnp.concatenate([d_csh, d_csc], axis=1),
        norm_g=jnp.concatenate([d_g00 + d_g00c, dm_mlp0["g_norm"], d_g10, dm_mlp1["g_norm"]], axis=1),
        rec_conv_w=d_cw_rec[:REC_KW], rec_conv_b=d_cw_rec[REC_KW], rec_lambda=d_lam.reshape(2, R),
        rec_w_a=d_wa, rec_b_a=d_ba, rec_w_x=d_wx, rec_b_x=d_bx,
        conf_b_pw1=jnp.concatenate([d_b1a, d_b1b], axis=1), conf_conv_w=d_cw_conf[:CONF_KW],
        conf_conv_b=d_cw_conf[CONF_KW], conf_ln_g=d_lng, conf_ln_b=d_lnb, conf_b_pw2=d_bpw2, final_g=d_fg)
    return loss.reshape(()), dx0, big, small


_BIG = ("rec_w_in", "rec_w_out", "conf_w_pw1", "conf_w_pw2", "mlp_w_in", "mlp_w_out")


def _halves(w):
    return w.reshape(2, w.shape[0] // 2, w.shape[1])


def _ada_fwd(c16, w_ada, b_shard):
    ns = w_ada.shape[2]
    tn = 512

    def kern(c_ref, w_ref, b_ref, o_ref):
        cv = c_ref[...]
        s = (cv * _sigmoid(cv)).astype(BF16)
        o_ref[...] = jnp.dot(s, w_ref[...].astype(BF16), preferred_element_type=F32) + b_ref[...]

    return _pcall(
        kern, name="ada_fwd", grid=(2, ns // tn),
        in_specs=[pl.BlockSpec((16, D), lambda l, j: (0, 0)), pl.BlockSpec((None, D, tn), lambda l, j: (l, 0, j)),
                  pl.BlockSpec((None, 1, tn), lambda l, j: (l, 0, j))],
        out_specs=pl.BlockSpec((None, 16, tn), lambda l, j: (l, 0, j)),
        out_shape=_sds((2, 16, ns), F32), compiler_params=_cparams(),
    )(c16, w_ada, b_shard)


def _ada_bwd(c16, dm16, w_ada):
    ns = w_ada.shape[2]
    tn = 512

    def kern(c_ref, dm_ref, w_ref, gw_ref, ds_ref):
        cv = c_ref[...]
        s = (cv * _sigmoid(cv)).astype(BF16)
        dm = dm_ref[...].astype(BF16)
        gw_ref[...] = lax.dot_general(s, dm, _TN, preferred_element_type=F32)

        @pl.when(jnp.logical_and(pl.program_id(0) == 0, pl.program_id(1) == 0))
        def _():
            ds_ref[...] = jnp.zeros_like(ds_ref)

        ds_ref[...] += lax.dot_general(dm, w_ref[...].astype(BF16), _NT, preferred_element_type=F32)

    return _pcall(
        kern, name="ada_bwd", grid=(2, ns // tn),
        in_specs=[pl.BlockSpec((16, D), lambda l, j: (0, 0)), pl.BlockSpec((None, 16, tn), lambda l, j: (l, 0, j)),
                  pl.BlockSpec((None, D, tn), lambda l, j: (l, 0, j))],
        out_specs=[pl.BlockSpec((None, D, tn), lambda l, j: (l, 0, j)), pl.BlockSpec((16, D), lambda l, j: (0, 0))],
        out_shape=[_sds((2, D, ns), F32), _sds((16, D), F32)], compiler_params=_cparams(),
    )(c16, dm16, w_ada)


def _cctx_grad(ds8, c_ctx):
    def kern(d_ref, c_ref, o_ref):
        tot = d_ref[0, 8:9, :] + d_ref[2, 8:9, :] + d_ref[4, 8:9, :] + d_ref[6, 8:9, :]
        cv = c_ref[...]
        sg = _sigmoid(cv)
        o_ref[...] = tot * (sg * (1.0 + cv * (1.0 - sg)))

    return _pcall(kern, name="cctx_grad", out_shape=_sds((1, D), F32))(ds8, c_ctx.reshape(1, D))


def kernel(x, c, ctx, c_ctx, w_ada, b_ada, norm_g, rec_w_in, rec_conv_w, rec_conv_b, rec_lambda, rec_w_a, rec_b_a, rec_w_x, rec_b_x, rec_w_out, conf_w_pw1, conf_b_pw1, conf_conv_w, conf_conv_b, conf_ln_g, conf_ln_b, conf_w_pw2, conf_b_pw2, mlp_w_in, mlp_w_out, final_g, loss_target, m_c_ctx, m_w_ada, m_b_ada, m_norm_g, m_rec_w_in, m_rec_conv_w, m_rec_conv_b, m_rec_lambda, m_rec_w_a, m_rec_b_a, m_rec_w_x, m_rec_b_x, m_rec_w_out, m_conf_w_pw1, m_conf_b_pw1, m_conf_conv_w, m_conf_conv_b, m_conf_ln_g, m_conf_ln_b, m_conf_w_pw2, m_conf_b_pw2, m_mlp_w_in, m_mlp_w_out, m_final_g, v_c_ctx, v_w_ada, v_b_ada, v_norm_g, v_rec_w_in, v_rec_conv_w, v_rec_conv_b, v_rec_lambda, v_rec_w_a, v_rec_b_a, v_rec_w_x, v_rec_b_x, v_rec_w_out, v_conf_w_pw1, v_conf_b_pw1, v_conf_conv_w, v_conf_conv_b, v_conf_ln_g, v_conf_ln_b, v_conf_w_pw2, v_conf_b_pw2, v_mlp_w_in, v_mlp_w_out, v_final_g):
    names = ["c_ctx", "w_ada", "b_ada", "norm_g", "rec_w_in", "rec_conv_w", "rec_conv_b", "rec_lambda", "rec_w_a",
             "rec_b_a", "rec_w_x", "rec_b_x", "rec_w_out", "conf_w_pw1", "conf_b_pw1", "conf_conv_w", "conf_conv_b",
             "conf_ln_g", "conf_ln_b", "conf_w_pw2", "conf_b_pw2", "mlp_w_in", "mlp_w_out", "final_g"]
    w = dict(zip(names, [c_ctx, w_ada, b_ada, norm_g, rec_w_in, rec_conv_w, rec_conv_b, rec_lambda, rec_w_a,
                         rec_b_a, rec_w_x, rec_b_x, rec_w_out, conf_w_pw1, conf_b_pw1, conf_conv_w, conf_conv_b,
                         conf_ln_g, conf_ln_b, conf_w_pw2, conf_b_pw2, mlp_w_in, mlp_w_out, final_g]))
    m = dict(zip(names, [m_c_ctx, m_w_ada, m_b_ada, m_norm_g, m_rec_w_in, m_rec_conv_w, m_rec_conv_b, m_rec_lambda,
                         m_rec_w_a, m_rec_b_a, m_rec_w_x, m_rec_b_x, m_rec_w_out, m_conf_w_pw1, m_conf_b_pw1,
                         m_conf_conv_w, m_conf_conv_b, m_conf_ln_g, m_conf_ln_b, m_conf_w_pw2, m_conf_b_pw2,
                         m_mlp_w_in, m_mlp_w_out, m_final_g]))
    v = dict(zip(names, [v_c_ctx, v_w_ada, v_b_ada, v_norm_g, v_rec_w_in, v_rec_conv_w, v_rec_conv_b, v_rec_lambda,
                         v_rec_w_a, v_rec_b_a, v_rec_w_x, v_rec_b_x, v_rec_w_out, v_conf_w_pw1, v_conf_b_pw1,
                         v_conf_conv_w, v_conf_conv_b, v_conf_ln_g, v_conf_ln_b, v_conf_w_pw2, v_conf_b_pw2,
                         v_mlp_w_in, v_mlp_w_out, v_final_g]))
    mx, my, mc = _me()
    chip = 2 * mx + my
    me = 4 * mx + 2 * my + mc

    sharded_small = ["norm_g", "rec_conv_w", "rec_lambda", "conf_b_pw1", "conf_conv_w", "conf_conv_b", "conf_ln_g",
                     "conf_ln_b", "conf_b_pw2"]
    packed, offs = _pack([c] + [w[k] for k in sharded_small], 8)
    got = _allgather8("gather_small", packed)

    place = jnp.stack([chip, mc]).astype(jnp.int32)
    shards = [_halves(rec_w_in[0]), _halves(rec_w_out[0]), _halves(conf_w_pw1[0]), _halves(conf_w_pw2[0]),
              _halves(mlp_w_in[0]), _halves(mlp_w_in[1]), _halves(mlp_w_out[0]), _halves(mlp_w_out[1])]
    use_order = dict(rec=(0, 1), mlp0=(4, 6), conf=(2, 3), mlp1=(5, 7))
    slots = _place_big(shards, place)
    flying, gsems = {}, {}
    fly, sems, rec_started = _gather_start("gather_start_rec", [slots[t] for t in use_order["rec"]], ((0, 1),), got)
    flying["rec"], gsems["rec"] = fly, sems

    def wg(group, after):
        bufs = _gather_wait(f"gather_wait_{group}", flying[group], *gsems[group], after)
        a, b = _swap_halves(f"swap_{group}", bufs)
        if group == "rec":
            return dict(rec_w_in=a.reshape(4, D, RH), rec_w_out=b.reshape(R, D))
        if group == "conf":
            return dict(conf_w_pw1=a.reshape(4, D, D // 2), conf_w_pw2=b.reshape(D, D))
        return dict(w_in=a.reshape(4, D, D), w_out=b.reshape(FF, D))

    got_flat = got.reshape(8, -1)

    def piece(i):
        p, n, shape = offs[i]
        return got_flat[:, p:p + n].reshape((8,) + tuple(shape))

    c_rows = piece(0).reshape(8, D)
    full = {}
    for i, k in enumerate(sharded_small):
        per_chip = jnp.moveaxis(piece(1 + i)[0::2], 0, -2)
        full[k] = per_chip.reshape(per_chip.shape[:-2] + (4 * per_chip.shape[-1],))
    c16 = jnp.concatenate([c_rows, c_ctx.reshape(1, D), jnp.zeros((7, D), F32)], axis=0)

    ns = w_ada.shape[2]
    b_shard = lax.dynamic_slice_in_dim(b_ada, chip * ns, ns, axis=1).reshape(2, 1, ns)
    prod = _ada_fwd(c16, w_ada, b_shard)
    prod8 = _allgather8("gather_mod", prod.reshape(32, ns), rec_started)
    later = ("mlp0", "conf", "mlp1")
    fly, sems, all_started = _gather_start("gather_start_rest", [slots[t] for g in later for t in use_order[g]],
                                           ((0, 1), (2, 3), (4, 5)), prod8)
    for gi, g in enumerate(later):
        flying[g], gsems[g] = fly[2 * gi:2 * gi + 2], sems[2 * gi:2 * gi + 2]
    prod8 = prod8.reshape(8, 2, 16, ns)
    mod_all = jnp.concatenate([prod8[2 * j] for j in range(4)], axis=-1)
    mods = lax.dynamic_index_in_dim(mod_all, me, axis=1, keepdims=False).reshape(2, 6, D)
    cmods = mod_all[0, 8].reshape(6, D)[:2]

    rec = dict(conv_w=full["rec_conv_w"][0], conv_b=rec_conv_b[0], lam=full["rec_lambda"][0],
               w_a=rec_w_a[0], b_a=rec_b_a[0], w_x=rec_w_x[0], b_x=rec_b_x[0])
    conf = dict(b_pw1=full["conf_b_pw1"][0], conv_w=full["conf_conv_w"][0], conv_b=full["conf_conv_b"][0],
                ln_g=full["conf_ln_g"][0], ln_b=full["conf_ln_b"][0], b_pw2=full["conf_b_pw2"][0])
    sent = {}

    def on_grads(group, dws):
        parts = [dw.reshape(4, 2, shards[t].shape[1], shards[t].shape[2]) for dw, t in zip(dws, use_order[group])]
        sent[group], token = _reduce_begin(group, parts, place)
        return token

    loss_local, grad_x, _, small = _local_step(x[0], ctx[0], loss_target[0], mods, cmods, full["norm_g"], final_g,
                                               rec, conf, wg, on_grads, all_started)
    small["loss"] = loss_local.reshape(1)

    small_names = ["loss", "d_mod", "d_cmod", "norm_g", "rec_conv_w", "rec_conv_b", "rec_lambda", "rec_w_a", "rec_b_a",
                   "rec_w_x", "rec_b_x", "conf_b_pw1", "conf_conv_w", "conf_conv_b", "conf_ln_g", "conf_ln_b",
                   "conf_b_pw2", "final_g"]
    mine = lax.broadcasted_iota(jnp.int32, (8, 1), 0) == me
    mod_slots = jnp.where(mine, small["d_mod"].reshape(1, -1), 0.0)
    spacked, soffs = _pack([small[k] for k in small_names] + [mod_slots])
    small_state, small_started = _allreduce_small_begin(spacked, place)

    fulls = {}
    for group in ("mlp1", "conf", "mlp0", "rec"):
        for t, f in zip(use_order[group], _reduce_end(group, sent[group], place, small_started)):
            fulls[t] = f
    whole = _share_halves("share_grads", [fulls[t] for t in range(8)])
    g_big = dict(rec_w_in=whole[0].reshape(rec_w_in.shape), rec_w_out=whole[1].reshape(rec_w_out.shape),
                 conf_w_pw1=whole[2].reshape(conf_w_pw1.shape), conf_w_pw2=whole[3].reshape(conf_w_pw2.shape),
                 mlp_w_in=jnp.stack([whole[4].reshape(D, D), whole[5].reshape(D, D)]),
                 mlp_w_out=jnp.stack([whole[6].reshape(D, D), whole[7].reshape(D, D)]))
    delta, new_m, new_v = {}, {}, {}

    def adamw_of(k, g):
        cols = w[k].shape[-1]
        d_, m_, v_ = _adamw(f"adamw_{k}", w[k].reshape(-1, cols), g.reshape(-1, cols),
                            m[k].reshape(-1, cols), v[k].reshape(-1, cols))
        delta[k], new_m[k], new_v[k] = (a.reshape(w[k].shape) for a in (d_, m_, v_))

    for k in _BIG:
        adamw_of(k, g_big[k])

    unpacked = _unpack(_allreduce_small_end(small_state, new_v[_BIG[-1]]), soffs)
    ssum = dict(zip(small_names, unpacked[:-1]))
    loss = ssum["loss"].reshape(())
    dmod_rows = unpacked[-1].reshape(8, 2, 6 * D).transpose(1, 0, 2)

    d_cmod_full =jnp.concatenate([ssum["d_cmod"].reshape(1, 2 * D), jnp.zeros((1, 4 * D), F32)], axis=1)
    dm16 = jnp.concatenate([dmod_rows, jnp.stack([d_cmod_full, jnp.zeros((1, 6 * D), F32)]),
                            jnp.zeros((2, 7, 6 * D), F32)], axis=1)
    dm16_shard = lax.dynamic_slice_in_dim(dm16, chip * ns, ns, axis=2)
    g_w_ada, ds_part = _ada_bwd(c16, dm16_shard, w_ada)
    ds8 = _allgather8("gather_dsilu", ds_part)
    g_c_ctx = _cctx_grad(ds8, c_ctx).reshape(D)
    g_b_ada = ssum["d_mod"] + jnp.stack([d_cmod_full[0], jnp.zeros((6 * D,), F32)])

    def shard_of(a, axis):
        n = a.shape[axis] // 4
        return lax.dynamic_slice_in_dim(a, chip * n, n, axis=axis)

    grads = dict(
        c_ctx=g_c_ctx, w_ada=g_w_ada, b_ada=g_b_ada,
        norm_g=shard_of(ssum["norm_g"].reshape(2, 2, D), 2),
        rec_w_in=g_big["rec_w_in"], rec_conv_w=shard_of(ssum["rec_conv_w"].reshape(1, REC_KW, R), 2),
        rec_conv_b=ssum["rec_conv_b"].reshape(1, R), rec_lambda=shard_of(ssum["rec_lambda"].reshape(1, 2, R), 2),
        rec_w_a=ssum["rec_w_a"].reshape(rec_w_a.shape), rec_b_a=ssum["rec_b_a"].reshape(rec_b_a.shape),
        rec_w_x=ssum["rec_w_x"].reshape(rec_w_x.shape), rec_b_x=ssum["rec_b_x"].reshape(rec_b_x.shape),
        rec_w_out=g_big["rec_w_out"], conf_w_pw1=g_big["conf_w_pw1"],
        conf_b_pw1=shard_of(ssum["conf_b_pw1"].reshape(1, 2 * D), 1),
        conf_conv_w=shard_of(ssum["conf_conv_w"].reshape(1, CONF_KW, D), 2),
        conf_conv_b=shard_of(ssum["conf_conv_b"].reshape(1, D), 1),
        conf_ln_g=shard_of(ssum["conf_ln_g"].reshape(1, D), 1), conf_ln_b=shard_of(ssum["conf_ln_b"].reshape(1, D), 1),
        conf_w_pw2=g_big["conf_w_pw2"], conf_b_pw2=shard_of(ssum["conf_b_pw2"].reshape(1, D), 1),
        mlp_w_in=g_big["mlp_w_in"], mlp_w_out=g_big["mlp_w_out"], final_g=ssum["final_g"].reshape(D))

    adamw_of("w_ada", g_w_ada)
    rest = [k for k in names if k not in ("w_ada",) + _BIG]
    d_, m_, v_ = _adamw_many("adamw_small", [w[k] for k in rest], [grads[k] for k in rest],
                             [m[k] for k in rest], [v[k] for k in rest])
    for k, dd, mm, vv in zip(rest, d_, m_, v_):
        delta[k], new_m[k], new_v[k] = dd, mm, vv

    return (loss, grad_x[None], *[grads[k] for k in names], *[delta[k] for k in names],
            *[new_m[k] for k in names], *[new_v[k] for k in names])
```

```python
import functools
import math

import jax
import jax.numpy as jnp
from jax import lax
from jax.experimental import pallas as pl
from jax.experimental.pallas import tpu as pltpu

F32 = jnp.float32
BF16 = jnp.bfloat16

D = 1024
T = 2048
TC = 256
TA = T + TC
R = 1280
RH = R // 2
NQ = 4 * RH
FF = 4096
N_BLK = 16
BLK = R // N_BLK
GRID_W = 64
EPS = 1e-6
RG_C = 8.0
CONF_KW = 31
REC_KW = 4
LANE = 128
ROW_TILE = 256
HALO = 16
RG_TILE = 128
PACK_ROWS = 512
MM_TILE = 1024
REC_TILE = TA // 2
CW_REC = 640
CW_CONF = 512
V7X_VMEM_BYTES = 64 * 1024 * 1024
VMEM_LIMIT = V7X_VMEM_BYTES - 8 * 1024 * 1024

ADAM_LR = 0.001
ADAM_B1 = 0.9
ADAM_B2 = 0.999
ADAM_EPS = 1e-08
ADAM_WD = 0.01
ADAM_STEP = 10

MESH = pl.DeviceIdType.MESH
ANY = pl.BlockSpec(memory_space=pl.ANY)


def _sds(shape, dtype):
    return jax.ShapeDtypeStruct(tuple(shape), dtype)


def _pcall(body, **kw):
    return pl.pallas_call(body, **kw)


def _cparams():
    return pltpu.CompilerParams(vmem_limit_bytes=VMEM_LIMIT)


def _full_spec(arr):
    nd = arr.ndim
    return pl.BlockSpec(arr.shape, lambda *ids, _n=nd: (0,) * _n)


def _sum0(v):
    return jnp.sum(v, axis=0, keepdims=True)


def _tiled(name, fn, grid, ins, vecs, outs, vec_outs=(), vec_refs=False):
    n_in, n_vec, n_out = len(ins), len(vecs), len(outs)
    n_grid = len(grid)

    def kern(*refs):
        ids = [pl.program_id(a) for a in range(n_grid)]
        tin = [r[...] for r in refs[:n_in]]
        vin = list(refs[n_in:n_in + n_vec]) if vec_refs else [r[...] for r in refs[n_in:n_in + n_vec]]
        o_refs = refs[n_in + n_vec:n_in + n_vec + n_out]
        a_refs = refs[n_in + n_vec + n_out:]
        tout, incs = fn(ids, tin, vin)
        for r, v in zip(o_refs, tout):
            r[...] = v.astype(r.dtype)
        if a_refs:
            first = functools.reduce(jnp.logical_and, [i == 0 for i in ids])

            @pl.when(first)
            def _():
                for r in a_refs:
                    r[...] = jnp.zeros_like(r)

            for r, v in zip(a_refs, incs):
                r[...] += v

    out_shape = [o for o, _ in outs] + [_sds(s, F32) for s in vec_outs]
    out_specs = [s for _, s in outs] + [
        pl.BlockSpec(tuple(s), lambda *ids, _n=len(s): (0,) * _n) for s in vec_outs]
    res = _pcall(
        kern, name=name, grid=tuple(grid),
        in_specs=[s for _, s in ins] + [_full_spec(v) for v in vecs],
        out_specs=out_specs, out_shape=out_shape, compiler_params=_cparams(),
    )(*[a for a, _ in ins], *vecs)
    return list(res)


def _rows(arr, ncols=None, tm=ROW_TILE, off=0, col=0, clamp_lo=False):
    ncols = arr.shape[1] if ncols is None else ncols
    if clamp_lo:
        return arr, pl.BlockSpec((tm, ncols), lambda i: (jnp.maximum(i + off, 0), col))
    return arr, pl.BlockSpec((tm, ncols), lambda i: (i + off, col))


def _orow(nrows, ncols, dtype, tm=ROW_TILE, off=0, clamp_lo=False):
    if clamp_lo:
        return _sds((nrows, ncols), dtype), pl.BlockSpec((tm, ncols), lambda i: (jnp.maximum(i + off, 0), 0))
    return _sds((nrows, ncols), dtype), pl.BlockSpec((tm, ncols), lambda i: (i + off, 0))


_NN = (((1,), (0,)), ((), ()))
_TN = (((0,), (0,)), ((), ()))
_NT = (((1,), (1,)), ((), ()))


def _mm(name, a, b, dims, grid, a_spec, b_spec, out, acc_shape, extra=(), a_pre=None, epi=None):
    n_k = grid[2]
    n_ex = len(extra)

    def kern(a_ref, b_ref, *rest):
        ex = rest[:n_ex]
        o_refs = rest[n_ex:n_ex + len(out)]
        k = pl.program_id(2)
        av = a_ref[...]
        if a_pre is not None:
            av = a_pre(av)
        part = lax.dot_general(av.astype(BF16), b_ref[...].astype(BF16), dims, preferred_element_type=F32)

        def finish(total):
            vals = [total] if epi is None else epi(total, [e[...] for e in ex])
            for r, v in zip(o_refs, vals):
                r[...] = v.astype(r.dtype)

        if n_k == 1:
            finish(part)
        else:
            acc = rest[-1]

            @pl.when(k == 0)
            def _():
                acc[...] = part

            @pl.when(jnp.logical_and(k > 0, k < n_k - 1))
            def _():
                acc[...] += part

            @pl.when(k == n_k - 1)
            def _():
                finish(acc[...] + part)

    res = _pcall(
        kern, name=name, grid=tuple(grid),
        in_specs=[a_spec, b_spec] + [s for _, s in extra],
        out_specs=[s for _, s in out], out_shape=[o for o, _ in out],
        scratch_shapes=[] if n_k == 1 else [pltpu.VMEM(tuple(acc_shape), F32)], compiler_params=_cparams(),
    )(a, b, *[e for e, _ in extra])
    return list(res)


def _rms(x):
    r = lax.rsqrt(jnp.mean(x * x, axis=-1, keepdims=True) + EPS)
    return x * r, r


def _norm_mod(x, g, sc, sh):
    n, _ = _rms(x)
    return (n * g) * (1.0 + sc) + sh


def _norm_mod_bwd(dh, x, g, sc):
    n, r = _rms(x)
    d_sh = _sum0(dh)
    d_sc = _sum0(dh * (n * g))
    d_g = _sum0(dh * (1.0 + sc) * n)
    dn = dh * (g * (1.0 + sc))
    dx = r * (dn - n * jnp.mean(dn * n, axis=-1, keepdims=True))
    return dx, d_sh, d_sc, d_g


_GELU_K = math.sqrt(2.0 / math.pi)


def _gelu(x):
    t = jnp.tanh(_GELU_K * (x + 0.044715 * x * x * x))
    return 0.5 * x * (1.0 + t), t


def _gelu_grad(x, t):
    return 0.5 * (1.0 + t) + 0.5 * x * (1.0 - t * t) * (_GELU_K * (1.0 + 3.0 * 0.044715 * x * x))


def _sigmoid(x):
    return 0.5 * jnp.tanh(0.5 * x) + 0.5


def _expm1(x):
    p = jnp.full_like(x, 1.0 / 5040.0)
    for c in (1.0 / 720.0, 1.0 / 120.0, 1.0 / 24.0, 1.0 / 6.0, 0.5, 1.0):
        p = p * x + c
    return jnp.where(jnp.abs(x) < 0.3, x * p, jnp.exp(x) - 1.0)


def _softplus_neg(lam):
    return jnp.log1p(jnp.exp(-jnp.abs(lam))) + jnp.maximum(-lam, 0.0)


def _layernorm_parts(x):
    mu = jnp.mean(x, axis=-1, keepdims=True)
    xc = x - mu
    rstd = lax.rsqrt(jnp.mean(xc * xc, axis=-1, keepdims=True) + EPS)
    return xc * rstd, rstd


def _rg_gates(u, wbd, gbias, lam):
    sp = _softplus_neg(lam)
    parts = {}
    for h in range(2):
        uh = u[:, h * RH:(h + 1) * RH]
        g = jnp.dot(uh.astype(BF16), wbd[h], preferred_element_type=F32) + gbias[:, h * NQ:(h + 1) * NQ]
        for d in range(2):
            r = _sigmoid(g[:, (2 * d) * RH:(2 * d + 1) * RH])
            i = _sigmoid(g[:, (2 * d + 1) * RH:(2 * d + 2) * RH])
            sph = sp[d:d + 1, h * RH:(h + 1) * RH]
            la = (-RG_C) * r * sph
            e2 = _expm1(2.0 * la)
            inv_mult = jnp.where(e2 < 0.0, lax.rsqrt(-e2), 0.0)
            parts[(d, h)] = dict(r=r, i=i, la=la, a=jnp.exp(la), e2=e2, mult=-e2 * inv_mult, inv_mult=inv_mult,
                                 uh=uh, sp=sph)
    return parts


def _rg_fwd_fn(ids, tin, vin):
    (u,) = tin
    wbd = vin[0]
    parts = _rg_gates(u, wbd, vin[1][...], vin[2][...])
    outs = []
    for d in range(2):
        a = jnp.concatenate([parts[(d, h)]["a"] for h in range(2)], axis=1)
        b = jnp.concatenate([parts[(d, h)]["mult"] * parts[(d, h)]["i"] * parts[(d, h)]["uh"]
                             for h in range(2)], axis=1)
        outs += [a, b]
    return outs, []


def _rg_bwd_fn(ids, tin, vin):
    u, da_f, db_f, da_r, db_r = tin
    wbd, lam = vin[0], vin[2][...]
    parts = _rg_gates(u, wbd, vin[1][...], lam)
    dab = ((da_f, db_f), (da_r, db_r))
    dsig_lam = -1.0 / (1.0 + jnp.exp(lam))
    du_halves, dpre_halves, dlam = [], [], [[None, None], [None, None]]
    for h in range(2):
        du = jnp.zeros_like(parts[(0, h)]["uh"])
        dpre = []
        for d in range(2):
            p = parts[(d, h)]
            da = dab[d][0][:, h * RH:(h + 1) * RH]
            db = dab[d][1][:, h * RH:(h + 1) * RH]
            d_mult = db * p["i"] * p["uh"]
            d_i = db * p["mult"] * p["uh"]
            du = du + db * p["mult"] * p["i"]
            d_la = da * p["a"] - d_mult * (p["e2"] + 1.0) * p["inv_mult"]
            d_r = d_la * ((-RG_C) * p["sp"])
            dlam[d][h] = _sum0(d_la * ((-RG_C) * p["r"])) * dsig_lam[d:d + 1, h * RH:(h + 1) * RH]
            dpre += [d_r * p["r"] * (1.0 - p["r"]), d_i * p["i"] * (1.0 - p["i"])]
        dpre = jnp.concatenate(dpre, axis=1)
        du = du + lax.dot_general(dpre.astype(BF16), wbd[h], _NT, preferred_element_type=F32)
        du_halves.append(du)
        dpre_halves.append(dpre)
    dpre_all = jnp.concatenate(dpre_halves, axis=1)
    dlam_row = jnp.concatenate([dlam[0][0], dlam[0][1], dlam[1][0], dlam[1][1]], axis=1)
    return [dpre_all, jnp.concatenate(du_halves, axis=1)], [_sum0(dpre_all), dlam_row]


def _tile_flags(i, n_tiles, seq_starts):
    starts_here = functools.reduce(jnp.logical_or, [i == s for s in seq_starts])
    ends_here = functools.reduce(jnp.logical_or, [i + 1 == s for s in seq_starts] + [i + 1 == n_tiles])
    return jnp.logical_not(starts_here), jnp.logical_not(ends_here)


def _halo_specs(col0, cw):
    hb = ROW_TILE // HALO
    prev = pl.BlockSpec((HALO, cw), lambda i, c: (jnp.maximum(i * hb - 1, 0), col0 + c))
    cur = pl.BlockSpec((ROW_TILE, cw), lambda i, c: (i, col0 + c))
    return prev, cur, hb


def _window(prev_ref, cur_ref, next_ref, has_prev, has_next):
    prev = jnp.where(has_prev, prev_ref[...], 0.0)
    nxt = jnp.where(has_next, next_ref[...], 0.0)
    return jnp.concatenate([prev, cur_ref[...], nxt], axis=0)


def _tap_reader(win):
    sub = 8
    n = win.shape[0]
    shifted = {0: win}

    def tap(off):
        s = off % sub
        if s not in shifted:
            shifted[s] = pltpu.roll(win, n - s, axis=0)
        return shifted[s][off - s:off - s + ROW_TILE, :]

    return tap


def _dwconv(name, x, col0, w, bias, pad_left, seq_starts, n_ch, cw=256):
    n_rows = x.shape[0]
    n_tiles = n_rows // ROW_TILE
    n_taps = w.shape[0]
    prev_spec, cur_spec, hb = _halo_specs(col0, cw)
    last_hb = n_rows // HALO - 1
    next_spec = pl.BlockSpec((HALO, cw), lambda i, c: (jnp.minimum((i + 1) * hb, last_hb), col0 + c))

    def kern(prev_ref, cur_ref, next_ref, w_ref, b_ref, o_ref):
        has_prev, has_next = _tile_flags(pl.program_id(0), n_tiles, seq_starts)
        win = _window(prev_ref, cur_ref, next_ref, has_prev, has_next)
        tap = _tap_reader(win)
        wv = w_ref[...]
        acc = jnp.zeros((ROW_TILE, cw), F32) + b_ref[...]
        for k in range(n_taps):
            acc = acc + wv[k:k + 1, :] * tap(HALO + k - pad_left)
        o_ref[...] = acc

    return _pcall(
        kern, name=name, grid=(n_tiles, n_ch // cw),
        in_specs=[prev_spec, cur_spec, next_spec,
                  pl.BlockSpec((n_taps, cw), lambda i, c: (0, c)), pl.BlockSpec((1, cw), lambda i, c: (0, c))],
        out_specs=pl.BlockSpec((ROW_TILE, cw), lambda i, c: (i, c)),
        out_shape=_sds((n_rows, n_ch), F32), compiler_params=_cparams(),
    )(x, x, x, w, bias)


def _dwconv_wgrad(name, dy, x, col0, n_taps, pad_left, seq_starts, n_ch, cw=256):
    n_rows = dy.shape[0]
    n_tiles = n_rows // ROW_TILE
    n_out = -(-(n_taps + 1) // 8) * 8
    prev_spec, cur_spec, hb = _halo_specs(col0, cw)
    last_hb = n_rows // HALO - 1
    next_spec = pl.BlockSpec((HALO, cw), lambda c, i: (jnp.minimum((i + 1) * hb, last_hb), col0 + c))
    prev_spec = pl.BlockSpec((HALO, cw), lambda c, i: (jnp.maximum(i * hb - 1, 0), col0 + c))
    cur_spec = pl.BlockSpec((ROW_TILE, cw), lambda c, i: (i, col0 + c))

    def kern(dy_ref, prev_ref, cur_ref, next_ref, o_ref):
        i = pl.program_id(1)
        has_prev, has_next = _tile_flags(i, n_tiles, seq_starts)
        win = _window(prev_ref, cur_ref, next_ref, has_prev, has_next)
        dyv = dy_ref[...]
        tap = _tap_reader(win)
        rid = lax.broadcasted_iota(jnp.int32, (n_out, cw), 0)
        inc = jnp.where(rid == n_taps, _sum0(dyv), 0.0)
        for k in range(n_taps):
            inc = inc + jnp.where(rid == k, _sum0(dyv * tap(HALO + k - pad_left)), 0.0)

        @pl.when(i == 0)
        def _():
            o_ref[...] = jnp.zeros_like(o_ref)

        o_ref[...] += inc

    return _pcall(
        kern, name=name, grid=(n_ch // cw, n_tiles),
        in_specs=[pl.BlockSpec((ROW_TILE, cw), lambda c, i: (i, c)), prev_spec, cur_spec, next_spec],
        out_specs=pl.BlockSpec((n_out, cw), lambda c, i: (0, c)),
        out_shape=_sds((n_out, n_ch), F32), compiler_params=_cparams(),
    )(dy, x, x, x)


N_SCAN = TA // ROW_TILE


def _rev_block(j):
    return jnp.where(j == 0, 0, N_SCAN - j)


def _scan_fwd(a_f, b_f, a_r, b_r):
    fwd_spec = pl.BlockSpec((ROW_TILE, R), lambda i: (i, 0))
    rev_spec = pl.BlockSpec((ROW_TILE, R), lambda i: (_rev_block(i), 0))
    hin_spec = pl.BlockSpec((None, 1, R), lambda i: (i, 0, 0))

    def kern(af, bf, ar, br, yf, yr, hin_f, hin_r, hf_s, hr_s):
        @pl.when(pl.program_id(0) == 0)
        def _():
            hf_s[...] = jnp.zeros_like(hf_s)
            hr_s[...] = jnp.zeros_like(hr_s)

        hin_f[...] = hf_s[...]
        hin_r[...] = hr_s[...]

        def step(s8, carry):
            hf, hr = carry
            t0 = pl.multiple_of(s8 * 8, 8)
            for q in range(8):
                tf = t0 + q
                hf = af[pl.ds(tf, 1), :] * hf + bf[pl.ds(tf, 1), :]
                yf[pl.ds(tf, 1), :] = hf
                tr = ROW_TILE - 1 - tf
                hr = ar[pl.ds(tr, 1), :] * hr + br[pl.ds(tr, 1), :]
                yr[pl.ds(tr, 1), :] = hr
            return hf, hr

        hf, hr = lax.fori_loop(0, ROW_TILE // 8, step, (hf_s[...], hr_s[...]))
        hf_s[...] = hf
        hr_s[...] = hr

    return _pcall(
        kern, name="scan_fwd", grid=(N_SCAN,),
        in_specs=[fwd_spec, fwd_spec, rev_spec, rev_spec],
        out_specs=[fwd_spec, rev_spec, hin_spec, hin_spec],
        out_shape=[_sds((TA, R), F32), _sds((TA, R), F32), _sds((N_SCAN, 1, R), F32), _sds((N_SCAN, 1, R), F32)],
        scratch_shapes=[pltpu.VMEM((1, R), F32), pltpu.VMEM((1, R), F32)], compiler_params=_cparams(),
    )(a_f, b_f, a_r, b_r)


def _scan_bwd(dy, a_f, y_f, hin_f, a_r, y_r, hin_r):
    fwd_spec = pl.BlockSpec((ROW_TILE, R), lambda i: (N_SCAN - 1 - i, 0))
    rev_spec = pl.BlockSpec((ROW_TILE, R), lambda i: (_rev_block(N_SCAN - 1 - i), 0))
    hin_spec = pl.BlockSpec((None, 1, R), lambda i: (N_SCAN - 1 - i, 0, 0))
    last = ROW_TILE - 1

    def kern(dyf, af, yf, hf0, dyr, ar, yr, hr0, daf, dbf, dar, dbr, gf_s, anf_s, gr_s, anr_s):
        @pl.when(pl.program_id(0) == 0)
        def _():
            for r in (gf_s, anf_s, gr_s, anr_s):
                r[...] = jnp.zeros_like(r)

        def one(dy_ref, a_ref, y_ref, da_ref, db_ref, g, an, p, pprev):
            gnew = dy_ref[pl.ds(p, 1), :] + an * g
            db_ref[pl.ds(p, 1), :] = gnew
            da_ref[pl.ds(p, 1), :] = gnew * y_ref[pl.ds(pprev, 1), :]
            return gnew, a_ref[pl.ds(p, 1), :]

        def step(s8, carry):
            gf, anf, gr, anr = carry
            base = s8 * 8
            for q in range(8):
                s = last - (base + q)
                gf, anf = one(dyf, af, yf, daf, dbf, gf, anf, s, s - 1)
                gr, anr = one(dyr, ar, yr, dar, dbr, gr, anr, last - s, last - s + 1)
            return gf, anf, gr, anr

        carry = (gf_s[...], anf_s[...], gr_s[...], anr_s[...])
        carry = lax.fori_loop(0, ROW_TILE // 8 - 1, step, carry)
        gf, anf, gr, anr = carry
        for s in range(7, 0, -1):
            gf, anf = one(dyf, af, yf, daf, dbf, gf, anf, s, s - 1)
            gr, anr = one(dyr, ar, yr, dar, dbr, gr, anr, last - s, last - s + 1)
        gf0 = dyf[0:1, :] + anf * gf
        dbf[0:1, :] = gf0
        daf[0:1, :] = gf0 * hf0[...]
        gr0 = dyr[last:last + 1, :] + anr * gr
        dbr[last:last + 1, :] = gr0
        dar[last:last + 1, :] = gr0 * hr0[...]
        gf_s[...] = gf0
        anf_s[...] = af[0:1, :]
        gr_s[...] = gr0
        anr_s[...] = ar[last:last + 1, :]

    return _pcall(
        kern, name="scan_bwd", grid=(N_SCAN,),
        in_specs=[fwd_spec, fwd_spec, fwd_spec, hin_spec, rev_spec, rev_spec, rev_spec, hin_spec],
        out_specs=[fwd_spec, fwd_spec, rev_spec, rev_spec],
        out_shape=[_sds((TA, R), F32)] * 4,
        scratch_shapes=[pltpu.VMEM((1, R), F32)] * 4, compiler_params=_cparams(),
    )(dy, a_f, y_f, hin_f, dy, a_r, y_r, hin_r)


def _me():
    return lax.axis_index("x"), lax.axis_index("y"), lax.axis_index("c")


def _other_chips(mx, my):
    return [(1 - mx, my), (mx, 1 - my), (1 - mx, 1 - my)]


def _rcopy(src, dst, ssem, rsem, dev):
    return pltpu.make_async_remote_copy(src_ref=src, dst_ref=dst, send_sem=ssem, recv_sem=rsem,
                                        device_id=dev, device_id_type=MESH)


def _allgather8(name, x, dep=None):
    rows, cols = x.shape
    n_dep = len(_behind(dep))

    def kern(x_ref, *rest):
        o_ref, ssem, rsem, lsem = rest[n_dep:]
        mx, my, mc = _me()
        me = 4 * mx + 2 * my + mc
        peers = []
        for k in range(1, 8):
            px = 1 - mx if (k >> 2) & 1 else mx
            py = 1 - my if (k >> 1) & 1 else my
            pc = 1 - mc if k & 1 else mc
            peers.append((px, py, pc))
        mine = pltpu.make_async_copy(x_ref, o_ref.at[me], lsem)
        mine.start()
        sends = [_rcopy(x_ref, o_ref.at[me], ssem.at[k], rsem.at[k], p) for k, p in enumerate(peers)]
        for cp in sends:
            cp.start()
        for k, (px, py, pc) in enumerate(peers):
            _rcopy(x_ref, o_ref.at[4 * px + 2 * py + pc], ssem.at[k], rsem.at[k], (px, py, pc)).wait_recv()
        for cp in sends:
            cp.wait_send()
        mine.wait()

    return _pcall(
        kern, name=name, in_specs=[ANY] * (1 + n_dep), out_specs=ANY, out_shape=_sds((8, rows, cols), F32),
        scratch_shapes=[pltpu.SemaphoreType.DMA((7,)), pltpu.SemaphoreType.DMA((7,)), pltpu.SemaphoreType.DMA(())],
    )(x, *_behind(dep))


def _gather8_start(name, x, after):
    def kern(x_in, after_ref, x_ref, o_ref, ssem, rsem, token):
        mx, my, mc = _me()
        me = 4 * mx + 2 * my + mc
        for k, p in enumerate(_peers7(mx, my, mc)):
            _rcopy(x_ref, o_ref.at[me], ssem.at[k], rsem.at[k], p).start()
        token[...] = jnp.zeros_like(token)

    dma = pltpu.SemaphoreType.DMA
    res = _pcall(
        kern, name=name, in_specs=[ANY, ANY],
        out_specs=[ANY, ANY, SEM, SEM, pl.BlockSpec(memory_space=pltpu.VMEM)],
        out_shape=[_sds(x.shape, x.dtype), _sds((8,) + x.shape, x.dtype), dma((7,)), dma((7,)), _sds((8, LANE), F32)],
        input_output_aliases={0: 0}, compiler_params=pltpu.CompilerParams(has_side_effects=_DATAFLOW),
    )(x, after)
    return tuple(res[:4]), res[4]


def _gather8_wait(name, x, out, ssem, rsem, after):
    def kern(x_ref, o_ref, ssem_ref, rsem_ref, after_ref, x_out, o_out):
        mx, my, mc = _me()
        for k, (px, py, pc) in enumerate(_peers7(mx, my, mc)):
            cp = _rcopy(x_ref, o_ref.at[4 * px + 2 * py + pc], ssem_ref.at[k], rsem_ref.at[k], (px, py, pc))
            cp.wait_recv()
            cp.wait_send()

    res = _pcall(
        kern, name=name, in_specs=[ANY, ANY, SEM, SEM, ANY], out_specs=[ANY, ANY],
        out_shape=[_sds(x.shape, x.dtype), _sds(out.shape, out.dtype)], input_output_aliases={0: 0, 1: 1},
        compiler_params=pltpu.CompilerParams(has_side_effects=_DATAFLOW),
    )(x, out, ssem, rsem, after)
    return res[0], res[1]


def _peers7(mx, my, mc):
    peers = []
    for k in range(1, 8):
        peers.append((1 - mx if (k >> 2) & 1 else mx, 1 - my if (k >> 1) & 1 else my, 1 - mc if k & 1 else mc))
    return peers


def _share_halves(name, fulls):
    n = len(fulls)

    def kern(*refs):
        o = refs[n:2 * n]
        ss, rs = refs[2 * n:]
        mx, my, mc = _me()
        sib = (mx, my, 1 - mc)
        sends = []
        for t in range(n):
            cp = _rcopy(o[t].at[mc], o[t].at[mc], ss.at[t], rs.at[t], sib)
            cp.start()
            sends.append(cp)
        for t in range(n):
            _rcopy(o[t].at[1 - mc], o[t].at[1 - mc], ss.at[t], rs.at[t], sib).wait_recv()
        for cp in sends:
            cp.wait_send()

    dma = pltpu.SemaphoreType.DMA
    return _pcall(
        kern, name=name, in_specs=[ANY] * n, out_specs=[ANY] * n,
        out_shape=[_sds(f.shape, f.dtype) for f in fulls], input_output_aliases={t: t for t in range(n)},
        scratch_shapes=[dma((n,)), dma((n,))],
    )(*fulls)


def _tiled_sp(name, fn, grid, sp, ins, outs):
    n_in = len(ins)

    def kern(sp_ref, *refs):
        tout = fn([r[...] for r in refs[:n_in]])
        for r, v in zip(refs[n_in:], tout):
            r[...] = v.astype(r.dtype)

    gs = pltpu.PrefetchScalarGridSpec(num_scalar_prefetch=1, grid=tuple(grid),
                                      in_specs=[s for _, s in ins], out_specs=[s for _, s in outs])
    res = _pcall(kern, name=name, grid_spec=gs, out_shape=[o for o, _ in outs], compiler_params=_cparams(),
                 )(sp, *[a for a, _ in ins])
    return list(res)


def _row_tile(rows, cols, itemsize=4, budget=2 * 1024 * 1024):
    tr = rows
    while tr * cols * itemsize > budget and tr % 32 == 0:
        tr //= 2
    return tr


def _place_big(shards, place):
    slots = []
    for t, s in enumerate(shards):
        rr, cc = s.shape[1], s.shape[2]
        tr = _row_tile(rr, cc)
        (slot,) = _tiled_sp(
            f"place{t}", lambda tin: [tin[0]], (2, rr // tr), place,
            [(s, pl.BlockSpec((None, tr, cc), lambda h, i, sp: (h, i, 0)))],
            [(_sds((4, 2, rr, cc), BF16), pl.BlockSpec((None, None, tr, cc), lambda h, i, sp: (sp[0], h, i, 0)))])
        slots.append(slot)
    return slots


def _allreduce_small_begin(vec, place):
    hr = vec.shape[0] // 2
    tr = _row_tile(hr, LANE)
    blk = (None, None, tr, LANE)
    (pair,) = _tiled_sp(
        "small_place", lambda tin: [tin[0]], (2, hr // tr), place,
        [(vec.reshape(2, hr, LANE), pl.BlockSpec((None, tr, LANE), lambda h, i, sp: (h, i, 0)))],
        [(_sds((2, 2, hr, LANE), F32), pl.BlockSpec(blk, lambda h, i, sp: (sp[1], h, i, 0)))])
    (pair,) = _share_halves("small_share", [pair])
    (slot,) = _tiled_sp(
        "small_pair_add", lambda tin: [tin[0] + tin[1]], (2, hr // tr), place,
        [(pair, pl.BlockSpec(blk, lambda h, i, sp: (0, h, i, 0))),
         (pair, pl.BlockSpec(blk, lambda h, i, sp: (1, h, i, 0)))],
        [(_sds((4, 2, hr, LANE), F32), pl.BlockSpec(blk, lambda h, i, sp: (sp[0], h, i, 0)))])
    fly, sems, token = _gather_start("small_start", [slot], ((0,),), pair)
    return (fly, sems), token


def _allreduce_small_end(state, after):
    fly, sems = state
    (chips,) = _swap_halves("small_swap", _gather_wait("small_wait", fly, *sems, after))
    hr = chips.shape[2]
    tr = _row_tile(hr, LANE)
    blk = (None, None, tr, LANE)
    (total,) = _tiled(
        "small_chip_sum", lambda ids, tin, vin: ([((tin[0] + tin[1]) + tin[2]) + tin[3]], []), (2, hr // tr),
        [(chips, pl.BlockSpec(blk, lambda h, i, _j=j: (_j, h, i, 0))) for j in range(4)], [],
        [(_sds((2, hr, LANE), F32), pl.BlockSpec((None, tr, LANE), lambda h, i: (h, i, 0)))])
    return total.reshape(2 * hr, LANE)


SEM =pl.BlockSpec(memory_space=pltpu.SEMAPHORE)
_DATAFLOW = pltpu.SideEffectType.DATAFLOW_SIDE_EFFECTING


def _gather_start(name, slots, groups, after):
    n = len(slots)

    def kern(*refs):
        o = refs[n + 1:2 * n + 1]
        sems, token = refs[2 * n + 1:-1], refs[-1]
        mx, my, mc = _me()
        j0 = 2 * mx + my
        for gi, grp in enumerate(groups):
            for k, t in enumerate(grp):
                for q, (qx, qy) in enumerate(_other_chips(mx, my)):
                    _rcopy(o[t].at[j0, mc], o[t].at[j0, mc], sems[2 * gi].at[3 * k + q],
                           sems[2 * gi + 1].at[3 * k + q], (qx, qy, mc)).start()
        token[...] = jnp.zeros_like(token)

    sem_shapes = []
    for grp in groups:
        sem_shapes += [pltpu.SemaphoreType.DMA((3 * len(grp),))] * 2
    res = _pcall(
        kern, name=name, in_specs=[ANY] * (n + 1),
        out_specs=[ANY] * n + [SEM] * len(sem_shapes) + [pl.BlockSpec(memory_space=pltpu.VMEM)],
        out_shape=[_sds(w.shape, w.dtype) for w in slots] + sem_shapes + [_sds((8, LANE), F32)],
        input_output_aliases={t: t for t in range(n)},
        compiler_params=pltpu.CompilerParams(has_side_effects=_DATAFLOW),
    )(*slots, after)
    return list(res[:n]), list(res[n:-1]), res[-1]


def _gather_wait(name, bufs, ssem, rsem, after):
    n = len(bufs)

    def kern(*refs):
        b = refs[:n]
        ssem_ref, rsem_ref = refs[n], refs[n + 1]
        mx, my, mc = _me()
        j0 = 2 * mx + my
        for k in range(n):
            for q, (qx, qy) in enumerate(_other_chips(mx, my)):
                jq = 2 * qx + qy
                _rcopy(b[k].at[jq, mc], b[k].at[jq, mc], ssem_ref.at[3 * k + q], rsem_ref.at[3 * k + q],
                       (qx, qy, mc)).wait_recv()
                _rcopy(b[k].at[j0, mc], b[k].at[j0, mc], ssem_ref.at[3 * k + q], rsem_ref.at[3 * k + q],
                       (qx, qy, mc)).wait_send()

    return list(_pcall(
        kern, name=name, in_specs=[ANY] * n + [SEM, SEM, ANY], out_specs=[ANY] * n,
        out_shape=[_sds(w.shape, w.dtype) for w in bufs], input_output_aliases={k: k for k in range(n)},
        compiler_params=pltpu.CompilerParams(has_side_effects=_DATAFLOW),
    )(*bufs, ssem, rsem, after))


def _swap_halves(name, bufs):
    n = len(bufs)

    def kern(*refs):
        o = refs[n:2 * n]
        ss, rs = refs[2 * n:]
        mx, my, mc = _me()
        sib = (mx, my, 1 - mc)
        sends = []
        for k in range(n):
            for q, (qx, qy) in enumerate(_other_chips(mx, my)):
                jq = 2 * qx + qy
                cp = _rcopy(o[k].at[jq, mc], o[k].at[jq, mc], ss.at[3 * k + q], rs.at[3 * k + q], sib)
                cp.start()
                sends.append(cp)
        for k in range(n):
            for q, (qx, qy) in enumerate(_other_chips(mx, my)):
                jq = 2 * qx + qy
                _rcopy(o[k].at[jq, 1 - mc], o[k].at[jq, 1 - mc], ss.at[3 * k + q], rs.at[3 * k + q], sib).wait_recv()
        for cp in sends:
            cp.wait_send()

    dma = pltpu.SemaphoreType.DMA
    return list(_pcall(
        kern, name=name, in_specs=[ANY] * n, out_specs=[ANY] * n,
        out_shape=[_sds(w.shape, w.dtype) for w in bufs], input_output_aliases={k: k for k in range(n)},
        scratch_shapes=[dma((3 * n,)), dma((3 * n,))],
    )(*bufs))


def _swap_start(name, bufs, after):
    n = len(bufs)

    def kern(*refs):
        o = refs[n + 1:2 * n + 1]
        ssem, rsem, token = refs[2 * n + 1:]
        mx, my, mc = _me()
        for k in range(n):
            for q, (qx, qy) in enumerate(_other_chips(mx, my)):
                jq = 2 * qx + qy
                _rcopy(o[k].at[jq, mc], o[k].at[jq, mc], ssem.at[3 * k + q], rsem.at[3 * k + q], (mx, my, 1 - mc)).start()
        token[...] = jnp.zeros_like(token)

    dma = pltpu.SemaphoreType.DMA
    res = _pcall(
        kern, name=name, in_specs=[ANY] * (n + 1),
        out_specs=[ANY] * n + [SEM, SEM, pl.BlockSpec(memory_space=pltpu.VMEM)],
        out_shape=[_sds(w.shape, w.dtype) for w in bufs] + [dma((3 * n,)), dma((3 * n,)), _sds((8, LANE), F32)],
        input_output_aliases={k: k for k in range(n)},
        compiler_params=pltpu.CompilerParams(has_side_effects=_DATAFLOW),
    )(*bufs, after)
    return (list(res[:n]), res[n], res[n + 1]), res[n + 2]


def _swap_wait(name, bufs, ssem, rsem, after):
    n = len(bufs)

    def kern(*refs):
        b = refs[:n]
        ssem_ref, rsem_ref = refs[n], refs[n + 1]
        mx, my, mc = _me()
        sib = (mx, my, 1 - mc)
        for k in range(n):
            for q, (qx, qy) in enumerate(_other_chips(mx, my)):
                jq = 2 * qx + qy
                _rcopy(b[k].at[jq, 1 - mc], b[k].at[jq, 1 - mc], ssem_ref.at[3 * k + q], rsem_ref.at[3 * k + q],
                       sib).wait_recv()
                _rcopy(b[k].at[jq, mc], b[k].at[jq, mc], ssem_ref.at[3 * k + q], rsem_ref.at[3 * k + q],
                       sib).wait_send()

    return list(_pcall(
        kern, name=name, in_specs=[ANY] * n + [SEM, SEM, ANY], out_specs=[ANY] * n,
        out_shape=[_sds(w.shape, w.dtype) for w in bufs], input_output_aliases={k: k for k in range(n)},
        compiler_params=pltpu.CompilerParams(has_side_effects=_DATAFLOW),
    )(*bufs, ssem, rsem, after))


def _to_sibling(mx, my, mc):
    return [((j, 1 - mc), j, (mx, my, 1 - mc)) for j in range(4)]


def _to_chips(mx, my, mc):
    return [((2 * qx + qy,), q, (qx, qy, mc)) for q, (qx, qy) in enumerate(_other_chips(mx, my))]


def _send_start(name, srcs, plan, land_shapes, after):
    n = len(srcs)
    per = len(plan(0, 0, 0))

    def kern(*refs):
        s, land = refs[n + 1:2 * n + 1], refs[2 * n + 1:3 * n + 1]
        ssem, rsem, token = refs[3 * n + 1:]
        for k in range(n):
            for q, (idx, slot, dev) in enumerate(plan(*_me())):
                _rcopy(s[k].at[idx], land[k].at[slot], ssem.at[per * k + q], rsem.at[per * k + q], dev).start()
        token[...] = jnp.zeros_like(token)

    dma = pltpu.SemaphoreType.DMA
    res = _pcall(
        kern, name=name, in_specs=[ANY] * (n + 1),
        out_specs=[ANY] * (2 * n) + [SEM, SEM, pl.BlockSpec(memory_space=pltpu.VMEM)],
        out_shape=[_sds(s.shape, s.dtype) for s in srcs] + [_sds(ls, s.dtype) for ls, s in zip(land_shapes, srcs)]
        + [dma((per * n,)), dma((per * n,)), _sds((8, LANE), F32)],
        input_output_aliases={k: k for k in range(n)},
        compiler_params=pltpu.CompilerParams(has_side_effects=_DATAFLOW),
    )(*srcs, after)
    return (list(res[:n]), list(res[n:2 * n]), res[2 * n], res[2 * n + 1]), res[2 * n + 2]


def _send_wait(name, srcs, lands, ssem, rsem, plan, after):
    n = len(srcs)
    per = len(plan(0, 0, 0))

    def kern(*refs):
        s, land = refs[:n], refs[n:2 * n]
        ssem_ref, rsem_ref = refs[2 * n], refs[2 * n + 1]
        for k in range(n):
            for q, (idx, slot, dev) in enumerate(plan(*_me())):
                cp = _rcopy(s[k].at[idx], land[k].at[slot], ssem_ref.at[per * k + q], rsem_ref.at[per * k + q], dev)
                cp.wait_recv()
                cp.wait_send()

    res = _pcall(
        kern, name=name, in_specs=[ANY] * (2 * n) + [SEM, SEM, ANY], out_specs=[ANY] * (2 * n),
        out_shape=[_sds(a.shape, a.dtype) for a in list(srcs) + list(lands)],
        input_output_aliases={k: k for k in range(2 * n)},
        compiler_params=pltpu.CompilerParams(has_side_effects=_DATAFLOW),
    )(*srcs, *lands, ssem, rsem, after)
    return list(res[:n]), list(res[n:])


def _reduce_begin(tag, parts, after):
    return _send_start(f"pair_start_{tag}", parts, _to_sibling, [(4,) + p.shape[2:] for p in parts], after)


def _reduce_mid(tag, pairing, place, after):
    parts, theirs = _send_wait(f"pair_wait_{tag}", *pairing, _to_sibling, after)
    sums = []
    for k, (p, o) in enumerate(zip(parts, theirs)):
        rr, cc = p.shape[2], p.shape[3]
        tr = _row_tile(rr, cc)
        (s_k,) = _tiled_sp(
            f"pair_add_{tag}{k}", lambda tin: [tin[0].astype(F32) + tin[1].astype(F32)], (4, rr // tr), place,
            [(p, pl.BlockSpec((None, None, tr, cc), lambda j, i, sp: (j, sp[1], i, 0))),
             (o, pl.BlockSpec((None, tr, cc), lambda j, i, sp: (j, i, 0)))],
            [(_sds((4, rr, cc), BF16), pl.BlockSpec((None, tr, cc), lambda j, i, sp: (j, i, 0)))])
        sums.append(s_k)
    return _send_start(f"chips_start_{tag}", sums, _to_chips, [(3,) + s.shape[1:] for s in sums], theirs[0])


def _reduce_end(tag, flying, place, after):
    sums, lands = _send_wait(f"chips_wait_{tag}", *flying, _to_chips, after)
    fulls = []
    for k, (s, q) in enumerate(zip(sums, lands)):
        rr, cc = q.shape[1], q.shape[2]
        tr = _row_tile(rr, cc)

        def add4(tin):
            return [((tin[0].astype(F32) + tin[1].astype(F32)) + tin[2].astype(F32)) + tin[3].astype(F32)]

        ins = [(s, pl.BlockSpec((None, tr, cc), lambda i, sp: (sp[0], i, 0)))]
        ins += [(q, pl.BlockSpec((None, tr, cc), lambda i, sp, _k=kk: (_k, i, 0))) for kk in range(3)]
        (f_k,) = _tiled_sp(f"chip_add_{tag}{k}", add4, (rr // tr,), place, ins,
                           [(_sds((2, rr, cc), F32), pl.BlockSpec((None, tr, cc), lambda i, sp: (sp[1], i, 0)))])
        fulls.append(f_k)
    return fulls


def _pack(parts, PACK_ROWS=PACK_ROWS):
    flat, offs, pos = [], [], 0
    for p in parts:
        v = p.reshape(-1).astype(F32)
        n = -(-v.shape[0] // LANE) * LANE
        flat.append(jnp.pad(v, (0, n - v.shape[0])))
        offs.append((pos, v.shape[0], p.shape))
        pos += n
    total = -(-pos // (PACK_ROWS * LANE)) * PACK_ROWS * LANE
    flat.append(jnp.zeros((total - pos,), F32))
    return jnp.concatenate(flat).reshape(-1, LANE), offs


def _unpack(vec, offs):
    v = vec.reshape(-1)
    return [v[p:p + n].reshape(shape) for p, n, shape in offs]


def _adamw_math(wv, gv, mv, vv):
    bc1 = 1.0 - ADAM_B1 ** ADAM_STEP
    bc2 = 1.0 - ADAM_B2 ** ADAM_STEP
    mn = ADAM_B1 * mv + (1.0 - ADAM_B1) * gv
    vn = ADAM_B2 * vv + (1.0 - ADAM_B2) * (gv * gv)
    delta = -ADAM_LR * ((mn / bc1) / (jnp.sqrt(vn / bc2) + ADAM_EPS) + ADAM_WD * wv)
    return delta, mn, vn


def _adamw(name, w, g, m, v):
    rows, cols = w.shape
    tr = rows
    for cand in (512, 256, 128, 64, 32, 16, 8):
        if rows % cand == 0 and cand * cols * 4 <= 2 * 1024 * 1024:
            tr = cand
            break

    def fn(ids, tin, vin):
        return list(_adamw_math(*tin)), []

    spec = pl.BlockSpec((tr, cols), lambda i: (i, 0))
    outs = [(_sds((rows, cols), F32), spec)] * 3
    return _tiled(name, fn, (rows // tr,), [(a, spec) for a in (w, g, m, v)], [], outs)


def _adamw_many(name, ws, gs, ms, vs):
    n = len(ws)
    views = [(-1, a.shape[-1]) if a.ndim > 1 else (1, -1) for a in ws]
    flat = lambda arrs: [a.reshape(vw) for a, vw in zip(arrs, views)]

    def kern(*refs):
        ins, outs = refs[:4 * n], refs[4 * n:]
        for t in range(n):
            res = _adamw_math(*[ins[q * n + t][...] for q in range(4)])
            for q in range(3):
                outs[q * n + t][...] = res[q]

    shapes = [_sds(a.shape, F32) for a in flat(ws)]
    res = _pcall(kern, name=name, out_shape=shapes * 3, compiler_params=_cparams(),
                 )(*flat(ws), *flat(gs), *flat(ms), *flat(vs))
    back = lambda part: [a.reshape(w.shape) for a, w in zip(part, ws)]
    return back(res[:n]), back(res[n:2 * n]), back(res[2 * n:])


def _pos_embed():
    n_rows = T // GRID_W
    q = D // 4
    omega = 1.0 / (10000.0 ** (jnp.arange(q, dtype=F32) / q))
    er = jnp.arange(n_rows, dtype=jnp.int32).astype(F32)[:, None] * omega[None, :]
    ec = jnp.arange(GRID_W, dtype=jnp.int32).astype(F32)[:, None] * omega[None, :]
    by_row = jnp.concatenate([jnp.sin(er), jnp.cos(er)], axis=-1)
    by_col = jnp.concatenate([jnp.sin(ec), jnp.cos(ec)], axis=-1)
    return jnp.concatenate([jnp.repeat(by_row, GRID_W, axis=0), jnp.tile(by_col, (n_rows, 1))], axis=-1)


def _dense_gates(w_a, w_x):
    rows = jnp.stack([w_a[0], w_x[0], w_a[1], w_x[1]]).reshape(4, 2, RH, BLK)
    mask, spread = _block_mask(), _block_spread().T.astype(BF16)

    def kern(r_ref, m_ref, s_ref, o_ref):
        tiled = jnp.dot(r_ref[...].astype(BF16), s_ref[...], preferred_element_type=F32)
        o_ref[...] = (tiled * m_ref[...]).astype(o_ref.dtype)

    return _pcall(
        kern, name="gates_dense", grid=(2, 4),
        in_specs=[pl.BlockSpec((None, None, RH, BLK), lambda h, q: (q, h, 0, 0)),
                  pl.BlockSpec((RH, RH), lambda h, q: (0, 0)), pl.BlockSpec((BLK, RH), lambda h, q: (0, 0))],
        out_specs=pl.BlockSpec((None, RH, RH), lambda h, q: (h, 0, q)),
        out_shape=_sds((2, RH, NQ), BF16),
    )(rows, mask, spread)


def _block_mask():
    r = lax.broadcasted_iota(jnp.int32, (RH, RH), 0) // BLK
    c = lax.broadcasted_iota(jnp.int32, (RH, RH), 1) // BLK
    return (r == c).astype(F32)


def _block_spread():
    c = lax.broadcasted_iota(jnp.int32, (RH, BLK), 0) % BLK
    j = lax.broadcasted_iota(jnp.int32, (RH, BLK), 1)
    return (c == j).astype(F32)


def _fold_blocks(dense, mask, spread):
    return jnp.dot(dense * mask, spread, preferred_element_type=F32, precision=lax.Precision.HIGHEST)


def _gate_block_grads(folded):
    per = N_BLK // 2
    kinds = [jnp.concatenate([folded[h, q].reshape(per, BLK, BLK) for h in range(2)], axis=0) for q in range(4)]
    return jnp.stack([kinds[0], kinds[2]]), jnp.stack([kinds[1], kinds[3]])


def _gate_bias_dense(b_a, b_x):
    cols = []
    for h in range(2):
        for src in (b_a[0], b_x[0], b_a[1], b_x[1]):
            cols.append(src.reshape(R)[h * RH:(h + 1) * RH])
    return jnp.concatenate(cols).reshape(1, 2 * NQ)


def _gate_bias_grads(dgb):
    v = dgb.reshape(2, 4, RH)
    kinds = [jnp.concatenate([v[0, q], v[1, q]]).reshape(N_BLK, BLK) for q in range(4)]
    return jnp.stack([kinds[0], kinds[2]]), jnp.stack([kinds[1], kinds[3]])


def _residual_epilogue(next_norm):
    def epi(acc, ex):
        x_new = ex[0] + ex[1] * acc
        outs = [acc, x_new]
        if next_norm:
            outs.append(_norm_mod(x_new, ex[-3], ex[-2], ex[-1]))
        return outs
    return epi


def _mlp_fwd(tag, x_in, h, gate, w_in, w_out, next_norm=None, dep=None):
    tm = MM_TILE
    (r,) = _mm(f"{tag}_in", h, w_in, _NN, (T // tm, 4, 1),
               pl.BlockSpec((tm, D), lambda i, j, k: (i, 0)), pl.BlockSpec((None, D, D), lambda i, j, k: (j, 0, 0)),
               [(_sds((T, FF), BF16), pl.BlockSpec((tm, D), lambda i, j, k: (i, j)))], (tm, D),
               extra=[(d_, _full_spec(d_)) for d_ in _behind(dep)], epi=lambda acc, ex: [jnp.maximum(acc, 0.0)])
    row_spec = pl.BlockSpec((tm, D), lambda i, j, k: (i, 0))
    outs = [(_sds((T, D), F32), row_spec)] * 2 + ([(_sds((T, D), BF16), row_spec)] if next_norm else [])
    res = _mm(f"{tag}_out", r, w_out, _NN, (T // tm, 1, FF // D),
              pl.BlockSpec((tm, D), lambda i, j, k: (i, k)), pl.BlockSpec((D, D), lambda i, j, k: (k, 0)),
              outs, (tm, D),
              extra=[(x_in, row_spec), (gate, _full_spec(gate))] + [(v, _full_spec(v)) for v in next_norm or ()],
              a_pre=lambda a: a * a, epi=_residual_epilogue(next_norm))
    return dict(h=h, r=r, o=res[0], x_in=x_in), res[1], (res[2] if next_norm else None)


def _behind(dep):
    return [] if dep is None else [dep]


def _gate_bwd(tag, dx, o, gate, dep=None):
    def fn(ids, t, v):
        d_o = t[0] * v[0]
        return [d_o], [_sum0(t[0] * t[1]), _sum0(d_o)]
    return _tiled(f"{tag}_gate_bwd", fn, (T // ROW_TILE,), [_rows(dx), _rows(o)], [gate] + _behind(dep),
                  [_orow(T, D, BF16)], [(1, D), (1, D)])


def _norm_bwd(tag, dx_res, dh, dh_off, x, g_norm, sc, with_dx=True, dep=None):
    n_t = x.shape[0] // ROW_TILE

    def fn(ids, t, v):
        if with_dx:
            dres, dhv, xv = t
        else:
            dhv, xv = t
        dxv, d_sh, d_sc, d_g = _norm_mod_bwd(dhv, xv, v[0], v[1])
        return ([dres + dxv] if with_dx else []), [d_sh, d_sc, d_g]

    ins = ([_rows(dx_res)] if with_dx else []) + [_rows(dh, off=dh_off), _rows(x)]
    outs = [_orow(x.shape[0], D, F32)] if with_dx else []
    return _tiled(f"{tag}_norm_bwd", fn, (n_t,), ins, [g_norm, sc] + _behind(dep), outs, [(1, D)] * 3)


def _mlp_bwd(tag, dx, saved, g_norm, sc, gate, w_in, w_out):
    d_o, d_gate, _ = _gate_bwd(tag, dx, saved["o"], gate)
    tm = MM_TILE
    r = saved["r"]
    (da,) = _mm(f"{tag}_dz", d_o, w_out, _NT, (T // tm, FF // D, 1),
                pl.BlockSpec((tm, D), lambda i, j, k: (i, 0)), pl.BlockSpec((D, D), lambda i, j, k: (j, 0)),
                [(_sds((T, FF), BF16), pl.BlockSpec((tm, D), lambda i, j, k: (i, j)))], (tm, D),
                extra=[(r, pl.BlockSpec((tm, D), lambda i, j, k: (i, j)))],
                epi=lambda acc, ex: [acc * (2.0 * ex[0].astype(F32))])
    tk = MM_TILE
    (dw_out,) = _mm(f"{tag}_dwout", r, d_o, _TN, (FF // tm, 1, T // tk),
                    pl.BlockSpec((tk, tm), lambda i, j, k: (k, i)), pl.BlockSpec((tk, D), lambda i, j, k: (k, 0)),
                    [(_sds((FF, D), BF16), pl.BlockSpec((tm, D), lambda i, j, k: (i, 0)))], (tm, D),
                    a_pre=lambda a: a * a)
    (dh,) = _mm(f"{tag}_dh", da, w_in, _NT, (T // tm, 1, 4),
                pl.BlockSpec((tm, D), lambda i, j, k: (i, k)), pl.BlockSpec((None, D, D), lambda i, j, k: (k, 0, 0)),
                [(_sds((T, D), F32), pl.BlockSpec((tm, D), lambda i, j, k: (i, 0)))], (tm, D))
    (dw_in,) = _mm(f"{tag}_dwin", saved["h"], da, _TN, (D // tm, 4, T // tk),
                   pl.BlockSpec((tk, tm), lambda i, j, k: (k, i)), pl.BlockSpec((tk, D), lambda i, j, k: (k, j)),
                   [(_sds((4, D, D), BF16), pl.BlockSpec((None, tm, D), lambda i, j, k: (j, i, 0)))], (tm, D))
    dx_in, d_sh, d_sc, d_g = _norm_bwd(tag, dx, dh, 0, saved["x_in"], g_norm, sc)
    return dx_in, dw_in, dw_out, dict(sh=d_sh, sc=d_sc, gate=d_gate, g_norm=d_g)


def _local_step(x, ctx, tgt, mods, cmods, norm_g, final_g, rec, conf, wg, on_grads=None, wg_pre=None):
    on_grads = on_grads or (lambda group, dws: None)
    wg_pre = wg_pre or (lambda group, after: None)
    n_t = T // ROW_TILE
    row = lambda v: v.reshape(1, -1)
    m0 = [row(mods[0, q]) for q in range(6)]
    m1 = [row(mods[1, q]) for q in range(6)]
    g00, g01, g10, g11 = (row(norm_g[0, 0]), row(norm_g[0, 1]), row(norm_g[1, 0]), row(norm_g[1, 1]))
    csh, csc = row(cmods[0]), row(cmods[1])
    pos = _pos_embed()

    def prep0(ids, t, v):
        cx, xv, pv = t
        is_ctx = ids[0] == 0
        xin = jnp.where(is_ctx, cx, xv + pv)
        sh = jnp.where(is_ctx, v[3], v[1])
        sc = jnp.where(is_ctx, v[4], v[2])
        return [_norm_mod(xin, v[0], sc, sh), xv + pv], []

    hcat, x0 = _tiled(
        "prep0", prep0, (N_SCAN,),
        [(ctx, pl.BlockSpec((ROW_TILE, D), lambda i: (0, 0))), _rows(x, off=-1, clamp_lo=True),
         _rows(pos, off=-1, clamp_lo=True)],
        [g00, m0[0], m0[1], csh, csc],
        [_orow(TA, D, BF16), _orow(T, D, F32, off=-1, clamp_lo=True)])

    tm_a = REC_TILE
    w_rec = wg("rec", hcat)
    (a_in,) = _mm("rec_in", hcat, w_rec["rec_w_in"], _NN, (TA // tm_a, 4, 1),
                  pl.BlockSpec((tm_a, D), lambda i, j, k: (i, 0)),
                  pl.BlockSpec((None, D, RH), lambda i, j, k: (j, 0, 0)),
                  [(_sds((TA, 2 * R), F32), pl.BlockSpec((tm_a, RH), lambda i, j, k: (i, j)))], (tm_a, RH))
    rec_starts = (0, 1)
    u = _dwconv("rec_conv", a_in, R // CW_REC, rec["conv_w"], row(rec["conv_b"]), 1, rec_starts, R, CW_REC)
    wbd = _dense_gates(rec["w_a"], rec["w_x"])
    gbias = _gate_bias_dense(rec["b_a"], rec["b_x"])
    lam = rec["lam"]
    a_f, b_f, a_r, b_r = _tiled("rg_fwd", _rg_fwd_fn, (TA // RG_TILE,), [_rows(u, tm=RG_TILE)], [wbd, gbias, lam],
                                [_orow(TA, R, F32, tm=RG_TILE)] * 4, vec_refs=True)
    dep = wg_pre("mlp0", a_f)
    y_f, y_r, hin_f, hin_r = _scan_fwd(a_f, b_f, a_r, b_r)

    def rec_mid(ids, t, v):
        gp, yf, yr = t
        g, _ = _gelu(gp)
        return [g * (yf + yr)], []

    (m_rec,) = _tiled("rec_mid", rec_mid, (n_t,),
                      [_rows(a_in, R, off=1), _rows(y_f, off=1), _rows(y_r, off=1)], _behind(dep),
                      [_orow(T, R, BF16)])
    tm = MM_TILE
    row_spec = pl.BlockSpec((tm, D), lambda i, j, k: (i, 0))
    norm_mlp0 = (g01, m0[4], m0[3])
    o_rec, x1, h_mlp0 = _mm(
        "rec_out", m_rec, w_rec["rec_w_out"], _NN, (T // tm, 1, 1),
        pl.BlockSpec((tm, R), lambda i, j, k: (i, 0)), pl.BlockSpec((R, D), lambda i, j, k: (0, 0)),
        [(_sds((T, D), F32), row_spec)] * 2 + [(_sds((T, D), BF16), row_spec)], (tm, D),
        extra=[(x0, row_spec), (m0[2], _full_spec(m0[2]))] + [(v, _full_spec(v)) for v in norm_mlp0],
        epi=_residual_epilogue(norm_mlp0))
    w_m0 = wg("mlp0", x1)
    dep = wg_pre("conf", x1)
    mlp0, x2, h1 = _mlp_fwd("mlp0", x1, h_mlp0, m0[5], w_m0["w_in"], w_m0["w_out"], (g10, m1[1], m1[0]), dep)

    b_pw1 = row(conf["b_pw1"])
    w_cf = wg("conf", x2)
    dep = wg_pre("mlp1", x2)
    (pre,) = _mm("conf_pw1", h1, w_cf["conf_w_pw1"], _NN, (T // tm, 4, 1),
                 pl.BlockSpec((tm, D), lambda i, j, k: (i, 0)),
                 pl.BlockSpec((None, D, D // 2), lambda i, j, k: (j, 0, 0)),
                 [(_sds((T, 2 * D), F32), pl.BlockSpec((tm, D // 2), lambda i, j, k: (i, j)))], (tm, D // 2),
                 extra=[(b_pw1, pl.BlockSpec((1, D // 2), lambda i, j, k: (0, j)))]
                 + [(d_, _full_spec(d_)) for d_ in _behind(dep)],
                 epi=lambda acc, ex: [acc + ex[0]])
    (zg,) = _tiled("conf_glu", lambda ids, t, v: ([t[0] * _sigmoid(t[1])], []), (n_t,),
                   [_rows(pre, D, col=0), _rows(pre, D, col=1)], [], [_orow(T, D, F32)])
    conf_starts = (0,)
    zc = _dwconv("conf_conv", zg, 0, conf["conv_w"], row(conf["conv_b"]), CONF_KW // 2, conf_starts, D, CW_CONF)
    ln_g, ln_b = row(conf["ln_g"]), row(conf["ln_b"])

    def ln_silu(ids, t, v):
        nh, _ = _layernorm_parts(t[0])
        ln = nh * v[0] + v[1]
        return [ln * _sigmoid(ln)], []

    (s_conf,) = _tiled("conf_ln", ln_silu, (n_t,), [_rows(zc)], [ln_g, ln_b], [_orow(T, D, BF16)])
    b_pw2 = row(conf["b_pw2"])
    norm_mlp1 = (g11, m1[4], m1[3])
    pw2_epi = _residual_epilogue(norm_mlp1)
    y_conf, x3, h_mlp1 = _mm(
        "conf_pw2", s_conf, w_cf["conf_w_pw2"], _NN, (T // tm, 1, 1),
        row_spec, pl.BlockSpec((D, D), lambda i, j, k: (0, 0)),
        [(_sds((T, D), F32), row_spec)] * 2 + [(_sds((T, D), BF16), row_spec)], (tm, D),
        extra=[(x2, row_spec), (m1[2], _full_spec(m1[2])), (b_pw2, _full_spec(b_pw2))]
        + [(v, _full_spec(v)) for v in norm_mlp1],
        epi=lambda acc, ex: pw2_epi(acc + ex[2], ex))
    w_m1 = wg("mlp1", x3)
    mlp1, x4, _ = _mlp_fwd("mlp1", x3, h_mlp1, m1[5], w_m1["w_in"], w_m1["w_out"])

    fg = row(final_g)

    def head(ids, t, v):
        n, r = _rms(t[0])
        err = n * v[0] - t[1]
        d_out = err * (1.0 / D)
        dn = d_out * v[0]
        dxv = r * (dn - n * jnp.mean(dn * n, axis=-1, keepdims=True))
        part = jnp.sum(_sum0(err * err), axis=1, keepdims=True) * (0.5 / D)
        return [dxv], [part, _sum0(d_out * n)]

    dx4, loss, d_fg = _tiled("head", head, (n_t,), [_rows(x4), _rows(tgt)], [fg], [_orow(T, D, F32)],
                             [(1, 1), (1, D)])

    dx3, dw_in1, dw_out1, dm_mlp1 = _mlp_bwd("mlp1", dx4, mlp1, g11, m1[4], m1[5],
                                             w_m1["w_in"], w_m1["w_out"])
    dep = on_grads("mlp1", (dw_in1, dw_out1))
    d_y, d_g1c, d_bpw2 = _gate_bwd("conf", dx3, y_conf, m1[2], dep)
    tk = MM_TILE
    (dw_pw2,) = _mm("conf_dwpw2", s_conf, d_y, _TN, (D // tm, 1, T // tk),
                    pl.BlockSpec((tk, tm), lambda i, j, k: (k, i)), pl.BlockSpec((tk, D), lambda i, j, k: (k, 0)),
                    [(_sds((D, D), BF16), pl.BlockSpec((tm, D), lambda i, j, k: (i, 0)))], (tm, D))
    (ds,) = _mm("conf_ds", d_y, w_cf["conf_w_pw2"], _NT, (T // tm, 1, 1),
                pl.BlockSpec((tm, D), lambda i, j, k: (i, 0)), pl.BlockSpec((D, D), lambda i, j, k: (0, 0)),
                [(_sds((T, D), F32), pl.BlockSpec((tm, D), lambda i, j, k: (i, 0)))], (tm, D))

    def ln_silu_bwd(ids, t, v):
        dsv, zcv = t
        nh, rstd = _layernorm_parts(zcv)
        ln = nh * v[0] + v[1]
        sg = _sigmoid(ln)
        d_ln = dsv * (sg * (1.0 + ln * (1.0 - sg)))
        d_nh = d_ln * v[0]
        d_zc = rstd * (d_nh - jnp.mean(d_nh, axis=-1, keepdims=True)
                       - nh * jnp.mean(d_nh * nh, axis=-1, keepdims=True))
        return [d_zc], [_sum0(d_ln * nh), _sum0(d_ln)]

    d_zc, d_lng, d_lnb = _tiled("conf_ln_bwd", ln_silu_bwd, (n_t,), [_rows(ds), _rows(zc)], [ln_g, ln_b],
                                [_orow(T, D, F32)], [(1, D), (1, D)])
    d_zg = _dwconv("conf_conv_dx", d_zc, 0, conf["conv_w"][::-1], jnp.zeros((1, D), F32),
                   CONF_KW - 1 - CONF_KW // 2, conf_starts, D, CW_CONF)
    d_cw_conf = _dwconv_wgrad("conf_conv_dw", d_zc, zg, 0, CONF_KW, CONF_KW // 2, conf_starts, D, CW_CONF)

    def glu_bwd(ids, t, v):
        dz, pa, pb = t
        sg = _sigmoid(pb)
        d_a = dz * sg
        d_b = dz * pa * sg * (1.0 - sg)
        return [jnp.concatenate([d_a, d_b], axis=1)], [_sum0(d_a), _sum0(d_b)]

    d_pre, d_b1a, d_b1b = _tiled(
        "conf_glu_bwd", glu_bwd, (n_t,), [_rows(d_zg), _rows(pre, D, col=0), _rows(pre, D, col=1)], [],
        [_orow(T, 2 * D, BF16)], [(1, D), (1, D)])
    (dw_pw1,) = _mm("conf_dwpw1", h1, d_pre, _TN, (D // tm, 4, T // tk),
                    pl.BlockSpec((tk, tm), lambda i, j, k: (k, i)),
                    pl.BlockSpec((tk, D // 2), lambda i, j, k: (k, j)),
                    [(_sds((4, D, D // 2), BF16), pl.BlockSpec((None, tm, D // 2), lambda i, j, k: (j, i, 0)))],
                    (tm, D // 2))
    dep = on_grads("conf", (dw_pw1, dw_pw2))
    (dh1,) = _mm("conf_dh", d_pre, w_cf["conf_w_pw1"], _NT, (T // tm, 1, 4),
                 pl.BlockSpec((tm, D // 2), lambda i, j, k: (i, k)),
                 pl.BlockSpec((None, D, D // 2), lambda i, j, k: (k, 0, 0)),
                 [(_sds((T, D), F32), pl.BlockSpec((tm, D), lambda i, j, k: (i, 0)))], (tm, D))
    dx2, d_sh1c, d_sc1c, d_g10 = _norm_bwd("conf", dx3, dh1, 0, x2, g10, m1[1], dep=dep)

    dx1, dw_in0, dw_out0, dm_mlp0 = _mlp_bwd("mlp0", dx2, mlp0, g01, m0[4], m0[5],
                                             w_m0["w_in"], w_m0["w_out"])
    dep = on_grads("mlp0", (dw_in0, dw_out0))
    d_orec, d_g1r, _ = _gate_bwd("rec", dx1, o_rec, m0[2], dep)
    (dw_rout,) = _mm("rec_dwout", m_rec, d_orec, _TN, (R // RH, 1, T // tk),
                     pl.BlockSpec((tk, RH), lambda i, j, k: (k, i)), pl.BlockSpec((tk, D), lambda i, j, k: (k, 0)),
                     [(_sds((R, D), BF16), pl.BlockSpec((RH, D), lambda i, j, k: (i, 0)))], (RH, D))
    (dm_rec,) = _mm("rec_dm", d_orec, w_rec["rec_w_out"], _NT, (T // tm, 1, 1),
                    pl.BlockSpec((tm, D), lambda i, j, k: (i, 0)), pl.BlockSpec((R, D), lambda i, j, k: (0, 0)),
                    [(_sds((T, R), F32), pl.BlockSpec((tm, R), lambda i, j, k: (i, 0)))], (tm, R))

    def rec_mid_bwd(ids, t, v):
        dmv, gp, yf, yr = t
        g, th = _gelu(gp)
        lat = ids[0] > 0
        d_gp = jnp.where(lat, dmv * (yf + yr) * _gelu_grad(gp, th), 0.0)
        dy = jnp.where(lat, dmv * g, 0.0)
        return [d_gp, dy], []

    d_gp, dy = _tiled("rec_mid_bwd", rec_mid_bwd, (N_SCAN,),
                      [_rows(dm_rec, off=-1, clamp_lo=True), _rows(a_in, R), _rows(y_f), _rows(y_r)], [],
                      [_orow(TA, R, BF16), _orow(TA, R, F32)])
    da_f, db_f, da_r, db_r = _scan_bwd(dy, a_f, y_f, hin_f, a_r, y_r, hin_r)
    d_gpre, d_u, d_gbias, d_lam = _tiled(
        "rg_bwd", _rg_bwd_fn, (TA // RG_TILE,), [_rows(a, tm=RG_TILE) for a in (u, da_f, db_f, da_r, db_r)],
        [wbd, gbias, lam], [_orow(TA, 2 * NQ, BF16, tm=RG_TILE), _orow(TA, R, F32, tm=RG_TILE)],
        [(1, 2 * NQ), (1, 2 * R)], vec_refs=True)
    tk_a = REC_TILE
    blk_mask, blk_spread = _block_mask(), _block_spread()
    (d_wbd,) = _mm("rg_dw", u, d_gpre, _TN, (2, 2, TA // tk_a),
                   pl.BlockSpec((tk_a, RH), lambda i, j, k: (k, i)),
                   pl.BlockSpec((tk_a, NQ // 2), lambda i, j, k: (k, 2 * i + j)),
                   [(_sds((2, 4, RH, BLK), F32), pl.BlockSpec((None, 2, RH, BLK), lambda i, j, k: (i, j, 0, 0)))],
                   (RH, NQ // 2),
                   extra=[(blk_mask, _full_spec(blk_mask)), (blk_spread, _full_spec(blk_spread))],
                   epi=lambda acc, ex: [jnp.stack([_fold_blocks(acc[:, s * RH:(s + 1) * RH], ex[0], ex[1])
                                                   for s in range(2)])])
    d_p = _dwconv("rec_conv_dx", d_u, 0, rec["conv_w"][::-1], jnp.zeros((1, R), F32), REC_KW - 1 - 1,
                  rec_starts, R, CW_REC)
    d_cw_rec = _dwconv_wgrad("rec_conv_dw", d_u, a_in, R // CW_REC, REC_KW, 1, rec_starts, R, CW_REC)
    d_a = jnp.concatenate([d_gp, d_p.astype(BF16)], axis=1)
    (dw_rin,) = _mm("rec_dwin", hcat, d_a, _TN, (D // tm, 4, TA // tk_a),
                    pl.BlockSpec((tk_a, tm), lambda i, j, k: (k, i)), pl.BlockSpec((tk_a, RH), lambda i, j, k: (k, j)),
                    [(_sds((4, D, RH), BF16), pl.BlockSpec((None, tm, RH), lambda i, j, k: (j, i, 0)))], (tm, RH))
    dep = on_grads("rec", (dw_rin, dw_rout))
    (dhcat,) = _mm("rec_dh", d_a, w_rec["rec_w_in"], _NT, (TA // tm_a, 1, 4),
                   pl.BlockSpec((tm_a, RH), lambda i, j, k: (i, k)),
                   pl.BlockSpec((None, D, RH), lambda i, j, k: (k, 0, 0)),
                   [(_sds((TA, D), F32), pl.BlockSpec((tm_a, D), lambda i, j, k: (i, 0)))], (tm_a, D))
    dx0, d_sh1r, d_sc1r, d_g00 = _norm_bwd("rec", dx1, dhcat, 1, x0, g00, m0[1], dep=dep)
    d_csh, d_csc, d_g00c = _norm_bwd("ctx", None, dhcat, 0, ctx, g00, csc, with_dx=False)

    big = dict(rec_w_in=dw_rin, rec_w_out=dw_rout, conf_w_pw1=dw_pw1, conf_w_pw2=dw_pw2,
               mlp_w_in=(dw_in0, dw_in1), mlp_w_out=(dw_out0, dw_out1))
    d_wa, d_wx = _gate_block_grads(d_wbd)
    d_ba, d_bx = _gate_bias_grads(d_gbias)
    d_mod = jnp.concatenate([
        d_sh1r, d_sc1r, d_g1r, dm_mlp0["sh"], dm_mlp0["sc"], dm_mlp0["gate"],
        d_sh1c, d_sc1c, d_g1c, dm_mlp1["sh"], dm_mlp1["sc"], dm_mlp1["gate"]], axis=1).reshape(2, 6 * D)
    small = dict(
        d_mod=d_mod, d_cmod=jnp.concatenate([d_csh, d_csc], axis=1),
        norm_g=jnp.concatenate([d_g00 + d_g00c, dm_mlp0["g_norm"], d_g10, dm_mlp1["g_norm"]], axis=1),
        rec_conv_w=d_cw_rec[:REC_KW], rec_conv_b=d_cw_rec[REC_KW], rec_lambda=d_lam.reshape(2, R),
        rec_w_a=d_wa, rec_b_a=d_ba, rec_w_x=d_wx, rec_b_x=d_bx,
        conf_b_pw1=jnp.concatenate([d_b1a, d_b1b], axis=1), conf_conv_w=d_cw_conf[:CONF_KW],
        conf_conv_b=d_cw_conf[CONF_KW], conf_ln_g=d_lng, conf_ln_b=d_lnb, conf_b_pw2=d_bpw2, final_g=d_fg)
    return loss.reshape(()), dx0, big, small


_BIG = ("rec_w_in", "rec_w_out", "conf_w_pw1", "conf_w_pw2", "mlp_w_in", "mlp_w_out")


def _halves(w):
    return w.reshape(2, w.shape[0] // 2, w.shape[1])


def _ada_fwd(c16, w_ada, b_shard):
    ns = w_ada.shape[2]
    tn = 512

    def kern(c_ref, w_ref, b_ref, o_ref):
        cv = c_ref[...]
        s = (cv * _sigmoid(cv)).astype(BF16)
        o_ref[...] = jnp.dot(s, w_ref[...].astype(BF16), preferred_element_type=F32) + b_ref[...]

    return _pcall(
        kern, name="ada_fwd", grid=(2, ns // tn),
        in_specs=[pl.BlockSpec((16, D), lambda l, j: (0, 0)), pl.BlockSpec((None, D, tn), lambda l, j: (l, 0, j)),
                  pl.BlockSpec((None, 1, tn), lambda l, j: (l, 0, j))],
        out_specs=pl.BlockSpec((None, 16, tn), lambda l, j: (l, 0, j)),
        out_shape=_sds((2, 16, ns), F32), compiler_params=_cparams(),
    )(c16, w_ada, b_shard)


def _ada_bwd(c16, dm16, w_ada):
    ns = w_ada.shape[2]
    tn = 512

    def kern(c_ref, dm_ref, w_ref, gw_ref, ds_ref):
        cv = c_ref[...]
        s = (cv * _sigmoid(cv)).astype(BF16)
        dm = dm_ref[...].astype(BF16)
        gw_ref[...] = lax.dot_general(s, dm, _TN, preferred_element_type=F32)

        @pl.when(jnp.logical_and(pl.program_id(0) == 0, pl.program_id(1) == 0))
        def _():
            ds_ref[...] = jnp.zeros_like(ds_ref)

        ds_ref[...] += lax.dot_general(dm, w_ref[...].astype(BF16), _NT, preferred_element_type=F32)

    return _pcall(
        kern, name="ada_bwd", grid=(2, ns // tn),
        in_specs=[pl.BlockSpec((16, D), lambda l, j: (0, 0)), pl.BlockSpec((None, 16, tn), lambda l, j: (l, 0, j)),
                  pl.BlockSpec((None, D, tn), lambda l, j: (l, 0, j))],
        out_specs=[pl.BlockSpec((None, D, tn), lambda l, j: (l, 0, j)), pl.BlockSpec((16, D), lambda l, j: (0, 0))],
        out_shape=[_sds((2, D, ns), F32), _sds((16, D), F32)], compiler_params=_cparams(),
    )(c16, dm16, w_ada)


def _cctx_grad(ds8, c_ctx):
    def kern(d_ref, c_ref, o_ref):
        tot = d_ref[0, 8:9, :] + d_ref[2, 8:9, :] + d_ref[4, 8:9, :] + d_ref[6, 8:9, :]
        cv = c_ref[...]
        sg = _sigmoid(cv)
        o_ref[...] = tot * (sg * (1.0 + cv * (1.0 - sg)))

    return _pcall(kern, name="cctx_grad", out_shape=_sds((1, D), F32))(ds8, c_ctx.reshape(1, D))


def kernel(x, c, ctx, c_ctx, w_ada, b_ada, norm_g, rec_w_in, rec_conv_w, rec_conv_b, rec_lambda, rec_w_a, rec_b_a, rec_w_x, rec_b_x, rec_w_out, conf_w_pw1, conf_b_pw1, conf_conv_w, conf_conv_b, conf_ln_g, conf_ln_b, conf_w_pw2, conf_b_pw2, mlp_w_in, mlp_w_out, final_g, loss_target, m_c_ctx, m_w_ada, m_b_ada, m_norm_g, m_rec_w_in, m_rec_conv_w, m_rec_conv_b, m_rec_lambda, m_rec_w_a, m_rec_b_a, m_rec_w_x, m_rec_b_x, m_rec_w_out, m_conf_w_pw1, m_conf_b_pw1, m_conf_conv_w, m_conf_conv_b, m_conf_ln_g, m_conf_ln_b, m_conf_w_pw2, m_conf_b_pw2, m_mlp_w_in, m_mlp_w_out, m_final_g, v_c_ctx, v_w_ada, v_b_ada, v_norm_g, v_rec_w_in, v_rec_conv_w, v_rec_conv_b, v_rec_lambda, v_rec_w_a, v_rec_b_a, v_rec_w_x, v_rec_b_x, v_rec_w_out, v_conf_w_pw1, v_conf_b_pw1, v_conf_conv_w, v_conf_conv_b, v_conf_ln_g, v_conf_ln_b, v_conf_w_pw2, v_conf_b_pw2, v_mlp_w_in, v_mlp_w_out, v_final_g):
    names = ["c_ctx", "w_ada", "b_ada", "norm_g", "rec_w_in", "rec_conv_w", "rec_conv_b", "rec_lambda", "rec_w_a",
             "rec_b_a", "rec_w_x", "rec_b_x", "rec_w_out", "conf_w_pw1", "conf_b_pw1", "conf_conv_w", "conf_conv_b",
             "conf_ln_g", "conf_ln_b", "conf_w_pw2", "conf_b_pw2", "mlp_w_in", "mlp_w_out", "final_g"]
    w = dict(zip(names, [c_ctx, w_ada, b_ada, norm_g, rec_w_in, rec_conv_w, rec_conv_b, rec_lambda, rec_w_a,
                         rec_b_a, rec_w_x, rec_b_x, rec_w_out, conf_w_pw1, conf_b_pw1, conf_conv_w, conf_conv_b,
                         conf_ln_g, conf_ln_b, conf_w_pw2, conf_b_pw2, mlp_w_in, mlp_w_out, final_g]))
    m = dict(zip(names, [m_c_ctx, m_w_ada, m_b_ada, m_norm_g, m_rec_w_in, m_rec_conv_w, m_rec_conv_b, m_rec_lambda,
                         m_rec_w_a, m_rec_b_a, m_rec_w_x, m_rec_b_x, m_rec_w_out, m_conf_w_pw1, m_conf_b_pw1,
                         m_conf_conv_w, m_conf_conv_b, m_conf_ln_g, m_conf_ln_b, m_conf_w_pw2, m_conf_b_pw2,
                         m_mlp_w_in, m_mlp_w_out, m_final_g]))
    v = dict(zip(names, [v_c_ctx, v_w_ada, v_b_ada, v_norm_g, v_rec_w_in, v_rec_conv_w, v_rec_conv_b, v_rec_lambda,
                         v_rec_w_a, v_rec_b_a, v_rec_w_x, v_rec_b_x, v_rec_w_out, v_conf_w_pw1, v_conf_b_pw1,
                         v_conf_conv_w, v_conf_conv_b, v_conf_ln_g, v_conf_ln_b, v_conf_w_pw2, v_conf_b_pw2,
                         v_mlp_w_in, v_mlp_w_out, v_final_g]))
    mx, my, mc = _me()
    chip = 2 * mx + my
    me = 4 * mx + 2 * my + mc

    sharded_small = ["norm_g", "rec_conv_w", "rec_lambda", "conf_b_pw1", "conf_conv_w", "conf_conv_b", "conf_ln_g",
                     "conf_ln_b", "conf_b_pw2"]
    packed, offs = _pack([c] + [w[k] for k in sharded_small], 8)
    got = _allgather8("gather_small", packed)

    got_flat = got.reshape(8, -1)

    def piece(i):
        p, n, shape = offs[i]
        return got_flat[:, p:p + n].reshape((8,) + tuple(shape))

    c_rows = piece(0).reshape(8, D)
    full = {}
    for i, k in enumerate(sharded_small):
        per_chip = jnp.moveaxis(piece(1 + i)[0::2], 0, -2)
        full[k] = per_chip.reshape(per_chip.shape[:-2] + (4 * per_chip.shape[-1],))
    c16 = jnp.concatenate([c_rows, c_ctx.reshape(1, D), jnp.zeros((7, D), F32)], axis=0)

    ns = w_ada.shape[2]
    b_shard = lax.dynamic_slice_in_dim(b_ada, chip * ns, ns, axis=1).reshape(2, 1, ns)
    prod = _ada_fwd(c16, w_ada, b_shard).reshape(32, ns)

    mod_state, mod_started = _gather8_start("gather_mod_start", prod, got)
    place = jnp.stack([chip, mc]).astype(jnp.int32)
    shards = [_halves(rec_w_in[0]), _halves(rec_w_out[0]), _halves(conf_w_pw1[0]), _halves(conf_w_pw2[0]),
              _halves(mlp_w_in[0]), _halves(mlp_w_in[1]), _halves(mlp_w_out[0]), _halves(mlp_w_out[1])]
    use_order = dict(rec=(0, 1), mlp0=(4, 6), conf=(2, 3), mlp1=(5, 7))
    slots = _place_big(shards, place)
    flying, gsems, swapping = {}, {}, {}
    fly, sems, rec_started = _gather_start("gather_start_rec", [slots[t] for t in use_order["rec"]], ((0, 1),),
                                           mod_started)
    flying["rec"], gsems["rec"] = fly, sems
    later = ("mlp0", "conf", "mlp1")
    fly, sems, all_started = _gather_start("gather_start_rest", [slots[t] for g in later for t in use_order[g]],
                                           ((0, 1), (2, 3), (4, 5)), rec_started)
    for gi, g in enumerate(later):
        flying[g], gsems[g] = fly[2 * gi:2 * gi + 2], sems[2 * gi:2 * gi + 2]

    def wg_pre(group, after):
        bufs = _gather_wait(f"gather_wait_{group}", flying[group], *gsems[group], after)
        swapping[group], token = _swap_start(f"swap_start_{group}", bufs, after)
        return token

    def wg(group, after):
        if group in swapping:
            a, b = _swap_wait(f"swap_wait_{group}", *swapping[group], after)
        else:
            a, b = _swap_halves(f"swap_{group}",
                                _gather_wait(f"gather_wait_{group}", flying[group], *gsems[group], after))
        if group == "rec":
            return dict(rec_w_in=a.reshape(4, D, RH), rec_w_out=b.reshape(R, D))
        if group == "conf":
            return dict(conf_w_pw1=a.reshape(4, D, D // 2), conf_w_pw2=b.reshape(D, D))
        return dict(w_in=a.reshape(4, D, D), w_out=b.reshape(FF, D))

    prod, prod8 = _gather8_wait("gather_mod_wait", *mod_state, all_started)
    prod8 = lax.dynamic_update_slice(prod8, prod[None], (me, 0, 0)).reshape(8, 2, 16, ns)
    mod_all = jnp.concatenate([prod8[2 * j] for j in range(4)], axis=-1)
    mods = lax.dynamic_index_in_dim(mod_all, me, axis=1, keepdims=False).reshape(2, 6, D)
    cmods = mod_all[0, 8].reshape(6, D)[:2]

    rec = dict(conv_w=full["rec_conv_w"][0], conv_b=rec_conv_b[0], lam=full["rec_lambda"][0],
               w_a=rec_w_a[0], b_a=rec_b_a[0], w_x=rec_w_x[0], b_x=rec_b_x[0])
    conf = dict(b_pw1=full["conf_b_pw1"][0], conv_w=full["conf_conv_w"][0], conv_b=full["conf_conv_b"][0],
                ln_g=full["conf_ln_g"][0], ln_b=full["conf_ln_b"][0], b_pw2=full["conf_b_pw2"][0])
    pairing, sent = {}, {}

    def finish_pair(after):
        (group, state), = pairing.items()
        pairing.clear()
        sent[group], token = _reduce_mid(group, state, place, after)
        return token

    def on_grads(group, dws):
        parts = [dw.reshape(4, 2, shards[t].shape[1], shards[t].shape[2]) for dw, t in zip(dws, use_order[group])]
        behind = finish_pair(parts[0]) if pairing else place
        pairing[group], token = _reduce_begin(group, parts, behind)
        return token

    loss_local, grad_x, _, small = _local_step(x[0], ctx[0], loss_target[0], mods, cmods, full["norm_g"], final_g,
                                               rec, conf, wg, on_grads, wg_pre)
    finish_pair(grad_x)
    small["loss"] = loss_local.reshape(1)

    small_names = ["loss", "d_mod", "d_cmod", "norm_g", "rec_conv_w", "rec_conv_b", "rec_lambda", "rec_w_a", "rec_b_a",
                   "rec_w_x", "rec_b_x", "conf_b_pw1", "conf_conv_w", "conf_conv_b", "conf_ln_g", "conf_ln_b",
                   "conf_b_pw2", "final_g"]
    mine = lax.broadcasted_iota(jnp.int32, (8, 1), 0) == me
    mod_slots = jnp.where(mine, small["d_mod"].reshape(1, -1), 0.0)
    spacked, soffs = _pack([small[k] for k in small_names] + [mod_slots])
    small_state, small_started = _allreduce_small_begin(spacked, place)

    fulls = {}
    for group in ("mlp1", "conf", "mlp0", "rec"):
        for t, f in zip(use_order[group], _reduce_end(group, sent[group], place, small_started)):
            fulls[t] = f
    whole = _share_halves("share_grads", [fulls[t] for t in range(8)])
    g_big = dict(rec_w_in=whole[0].reshape(rec_w_in.shape), rec_w_out=whole[1].reshape(rec_w_out.shape),
                 conf_w_pw1=whole[2].reshape(conf_w_pw1.shape), conf_w_pw2=whole[3].reshape(conf_w_pw2.shape),
                 mlp_w_in=jnp.stack([whole[4].reshape(D, D), whole[5].reshape(D, D)]),
                 mlp_w_out=jnp.stack([whole[6].reshape(D, D), whole[7].reshape(D, D)]))
    delta, new_m, new_v = {}, {}, {}

    def adamw_of(k, g):
        cols = w[k].shape[-1]
        d_, m_, v_ = _adamw(f"adamw_{k}", w[k].reshape(-1, cols), g.reshape(-1, cols),
                            m[k].reshape(-1, cols), v[k].reshape(-1, cols))
        delta[k], new_m[k], new_v[k] = (a.reshape(w[k].shape) for a in (d_, m_, v_))

    for k in _BIG:
        adamw_of(k, g_big[k])

    unpacked = _unpack(_allreduce_small_end(small_state, new_v[_BIG[-1]]), soffs)
    ssum = dict(zip(small_names, unpacked[:-1]))
    loss = ssum["loss"].reshape(())
    dmod_rows = unpacked[-1].reshape(8, 2, 6 * D).transpose(1, 0, 2)

    d_cmod_full =jnp.concatenate([ssum["d_cmod"].reshape(1, 2 * D), jnp.zeros((1, 4 * D), F32)], axis=1)
    dm16 = jnp.concatenate([dmod_rows, jnp.stack([d_cmod_full, jnp.zeros((1, 6 * D), F32)]),
                            jnp.zeros((2, 7, 6 * D), F32)], axis=1)
    dm16_shard = lax.dynamic_slice_in_dim(dm16, chip * ns, ns, axis=2)
    g_w_ada, ds_part = _ada_bwd(c16, dm16_shard, w_ada)
    ds8 = _allgather8("gather_dsilu", ds_part)
    g_c_ctx = _cctx_grad(ds8, c_ctx).reshape(D)
    g_b_ada = ssum["d_mod"] + jnp.stack([d_cmod_full[0], jnp.zeros((6 * D,), F32)])

    def shard_of(a, axis):
        n = a.shape[axis] // 4
        return lax.dynamic_slice_in_dim(a, chip * n, n, axis=axis)

    grads = dict(
        c_ctx=g_c_ctx, w_ada=g_w_ada, b_ada=g_b_ada,
        norm_g=shard_of(ssum["norm_g"].reshape(2, 2, D), 2),
        rec_w_in=g_big["rec_w_in"], rec_conv_w=shard_of(ssum["rec_conv_w"].reshape(1, REC_KW, R), 2),
        rec_conv_b=ssum["rec_conv_b"].reshape(1, R), rec_lambda=shard_of(ssum["rec_lambda"].reshape(1, 2, R), 2),
        rec_w_a=ssum["rec_w_a"].reshape(rec_w_a.shape), rec_b_a=ssum["rec_b_a"].reshape(rec_b_a.shape),
        rec_w_x=ssum["rec_w_x"].reshape(rec_w_x.shape), rec_b_x=ssum["rec_b_x"].reshape(rec_b_x.shape),
        rec_w_out=g_big["rec_w_out"], conf_w_pw1=g_big["conf_w_pw1"],
        conf_b_pw1=shard_of(ssum["conf_b_pw1"].reshape(1, 2 * D), 1),
        conf_conv_w=shard_of(ssum["conf_conv_w"].reshape(1, CONF_KW, D), 2),
        conf_conv_b=shard_of(ssum["conf_conv_b"].reshape(1, D), 1),
        conf_ln_g=shard_of(ssum["conf_ln_g"].reshape(1, D), 1), conf_ln_b=shard_of(ssum["conf_ln_b"].reshape(1, D), 1),
        conf_w_pw2=g_big["conf_w_pw2"], conf_b_pw2=shard_of(ssum["conf_b_pw2"].reshape(1, D), 1),
        mlp_w_in=g_big["mlp_w_in"], mlp_w_out=g_big["mlp_w_out"], final_g=ssum["final_g"].reshape(D))

    adamw_of("w_ada", g_w_ada)
    rest = [k for k in names if k not in ("w_ada",) + _BIG]
    d_, m_, v_ = _adamw_many("adamw_small", [w[k] for k in rest], [grads[k] for k in rest],
                             [m[k] for k in rest], [v[k] for k in rest])
    for k, dd, mm, vv in zip(rest, d_, m_, v_):
        delta[k], new_m[k], new_v[k] = dd, mm, vv

    return (loss, grad_x[None], *[grads[k] for k in names], *[delta[k] for k in names],
            *[new_m[k] for k in names], *[new_v[k] for k in names])
```

```python
import functools
import math

import jax
import jax.numpy as jnp
from jax import lax
from jax.experimental import pallas as pl
from jax.experimental.pallas import tpu as pltpu

F32 = jnp.float32
BF16 = jnp.bfloat16

D = 1024
T = 2048
TC = 256
TA = T + TC
R = 1280
RH = R // 2
NQ = 4 * RH
FF = 4096
N_BLK = 16
BLK = R // N_BLK
GRID_W = 64
EPS = 1e-6
RG_C = 8.0
CONF_KW = 31
REC_KW = 4
LANE = 128
ROW_TILE = 256
HALO = 16
RG_TILE = 128
PACK_ROWS = 512
MM_TILE = 1024
REC_TILE = TA // 2
CW_REC = 640
CW_CONF = 512
V7X_VMEM_BYTES = 64 * 1024 * 1024
VMEM_LIMIT = V7X_VMEM_BYTES - 8 * 1024 * 1024

ADAM_LR = 0.001
ADAM_B1 = 0.9
ADAM_B2 = 0.999
ADAM_EPS = 1e-08
ADAM_WD = 0.01
ADAM_STEP = 10

MESH = pl.DeviceIdType.MESH
ANY = pl.BlockSpec(memory_space=pl.ANY)


def _sds(shape, dtype):
    return jax.ShapeDtypeStruct(tuple(shape), dtype)


def _pcall(body, **kw):
    return pl.pallas_call(body, **kw)


def _cparams():
    return pltpu.CompilerParams(vmem_limit_bytes=VMEM_LIMIT)


def _full_spec(arr):
    nd = arr.ndim
    return pl.BlockSpec(arr.shape, lambda *ids, _n=nd: (0,) * _n)


def _sum0(v):
    return jnp.sum(v, axis=0, keepdims=True)


def _tiled(name, fn, grid, ins, vecs, outs, vec_outs=(), vec_refs=False):
    n_in, n_vec, n_out = len(ins), len(vecs), len(outs)
    n_grid = len(grid)

    def kern(*refs):
        ids = [pl.program_id(a) for a in range(n_grid)]
        tin = [r[...] for r in refs[:n_in]]
        vin = list(refs[n_in:n_in + n_vec]) if vec_refs else [r[...] for r in refs[n_in:n_in + n_vec]]
        o_refs = refs[n_in + n_vec:n_in + n_vec + n_out]
        a_refs = refs[n_in + n_vec + n_out:]
        tout, incs = fn(ids, tin, vin)
        for r, v in zip(o_refs, tout):
            r[...] = v.astype(r.dtype)
        if a_refs:
            first = functools.reduce(jnp.logical_and, [i == 0 for i in ids])

            @pl.when(first)
            def _():
                for r in a_refs:
                    r[...] = jnp.zeros_like(r)

            for r, v in zip(a_refs, incs):
                r[...] += v

    out_shape = [o for o, _ in outs] + [_sds(s, F32) for s in vec_outs]
    out_specs = [s for _, s in outs] + [
        pl.BlockSpec(tuple(s), lambda *ids, _n=len(s): (0,) * _n) for s in vec_outs]
    res = _pcall(
        kern, name=name, grid=tuple(grid),
        in_specs=[s for _, s in ins] + [_full_spec(v) for v in vecs],
        out_specs=out_specs, out_shape=out_shape, compiler_params=_cparams(),
    )(*[a for a, _ in ins], *vecs)
    return list(res)


def _rows(arr, ncols=None, tm=ROW_TILE, off=0, col=0, clamp_lo=False):
    ncols = arr.shape[1] if ncols is None else ncols
    if clamp_lo:
        return arr, pl.BlockSpec((tm, ncols), lambda i: (jnp.maximum(i + off, 0), col))
    return arr, pl.BlockSpec((tm, ncols), lambda i: (i + off, col))


def _orow(nrows, ncols, dtype, tm=ROW_TILE, off=0, clamp_lo=False):
    if clamp_lo:
        return _sds((nrows, ncols), dtype), pl.BlockSpec((tm, ncols), lambda i: (jnp.maximum(i + off, 0), 0))
    return _sds((nrows, ncols), dtype), pl.BlockSpec((tm, ncols), lambda i: (i + off, 0))


_NN = (((1,), (0,)), ((), ()))
_TN = (((0,), (0,)), ((), ()))
_NT = (((1,), (1,)), ((), ()))


def _mm(name, a, b, dims, grid, a_spec, b_spec, out, acc_shape, extra=(), a_pre=None, epi=None):
    n_k = grid[2]
    n_ex = len(extra)

    def kern(a_ref, b_ref, *rest):
        ex = rest[:n_ex]
        o_refs = rest[n_ex:n_ex + len(out)]
        k = pl.program_id(2)
        av = a_ref[...]
        if a_pre is not None:
            av = a_pre(av)
        part = lax.dot_general(av.astype(BF16), b_ref[...].astype(BF16), dims, preferred_element_type=F32)

        def finish(total):
            vals = [total] if epi is None else epi(total, [e[...] for e in ex])
            for r, v in zip(o_refs, vals):
                r[...] = v.astype(r.dtype)

        if n_k == 1:
            finish(part)
        else:
            acc = rest[-1]

            @pl.when(k == 0)
            def _():
                acc[...] = part

            @pl.when(jnp.logical_and(k > 0, k < n_k - 1))
            def _():
                acc[...] += part

            @pl.when(k == n_k - 1)
            def _():
                finish(acc[...] + part)

    res = _pcall(
        kern, name=name, grid=tuple(grid),
        in_specs=[a_spec, b_spec] + [s for _, s in extra],
        out_specs=[s for _, s in out], out_shape=[o for o, _ in out],
        scratch_shapes=[] if n_k == 1 else [pltpu.VMEM(tuple(acc_shape), F32)], compiler_params=_cparams(),
    )(a, b, *[e for e, _ in extra])
    return list(res)


def _rms(x):
    r = lax.rsqrt(jnp.mean(x * x, axis=-1, keepdims=True) + EPS)
    return x * r, r


def _norm_mod(x, g, sc, sh):
    n, _ = _rms(x)
    return (n * g) * (1.0 + sc) + sh


def _norm_mod_bwd(dh, x, g, sc):
    n, r = _rms(x)
    d_sh = _sum0(dh)
    d_sc = _sum0(dh * (n * g))
    d_g = _sum0(dh * (1.0 + sc) * n)
    dn = dh * (g * (1.0 + sc))
    dx = r * (dn - n * jnp.mean(dn * n, axis=-1, keepdims=True))
    return dx, d_sh, d_sc, d_g


_GELU_K = math.sqrt(2.0 / math.pi)


def _gelu(x):
    t = jnp.tanh(_GELU_K * (x + 0.044715 * x * x * x))
    return 0.5 * x * (1.0 + t), t


def _gelu_grad(x, t):
    return 0.5 * (1.0 + t) + 0.5 * x * (1.0 - t * t) * (_GELU_K * (1.0 + 3.0 * 0.044715 * x * x))


def _sigmoid(x):
    return 0.5 * jnp.tanh(0.5 * x) + 0.5


def _expm1(x):
    p = jnp.full_like(x, 1.0 / 5040.0)
    for c in (1.0 / 720.0, 1.0 / 120.0, 1.0 / 24.0, 1.0 / 6.0, 0.5, 1.0):
        p = p * x + c
    return jnp.where(jnp.abs(x) < 0.3, x * p, jnp.exp(x) - 1.0)


def _softplus_neg(lam):
    return jnp.log1p(jnp.exp(-jnp.abs(lam))) + jnp.maximum(-lam, 0.0)


def _layernorm_parts(x):
    mu = jnp.mean(x, axis=-1, keepdims=True)
    xc = x - mu
    rstd = lax.rsqrt(jnp.mean(xc * xc, axis=-1, keepdims=True) + EPS)
    return xc * rstd, rstd


def _rg_gates(u, wbd, gbias, lam):
    sp = _softplus_neg(lam)
    parts = {}
    for h in range(2):
        uh = u[:, h * RH:(h + 1) * RH]
        g = jnp.dot(uh.astype(BF16), wbd[h], preferred_element_type=F32) + gbias[:, h * NQ:(h + 1) * NQ]
        for d in range(2):
            r = _sigmoid(g[:, (2 * d) * RH:(2 * d + 1) * RH])
            i = _sigmoid(g[:, (2 * d + 1) * RH:(2 * d + 2) * RH])
            sph = sp[d:d + 1, h * RH:(h + 1) * RH]
            la = (-RG_C) * r * sph
            e2 = _expm1(2.0 * la)
            inv_mult = jnp.where(e2 < 0.0, lax.rsqrt(-e2), 0.0)
            parts[(d, h)] = dict(r=r, i=i, la=la, a=jnp.exp(la), e2=e2, mult=-e2 * inv_mult, inv_mult=inv_mult,
                                 uh=uh, sp=sph)
    return parts


def _rg_fwd_fn(ids, tin, vin):
    (u,) = tin
    wbd = vin[0]
    parts = _rg_gates(u, wbd, vin[1][...], vin[2][...])
    outs = []
    for d in range(2):
        a = jnp.concatenate([parts[(d, h)]["a"] for h in range(2)], axis=1)
        b = jnp.concatenate([parts[(d, h)]["mult"] * parts[(d, h)]["i"] * parts[(d, h)]["uh"]
                             for h in range(2)], axis=1)
        outs += [a, b]
    return outs, []


def _rg_bwd_fn(ids, tin, vin):
    u, da_f, db_f, da_r, db_r = tin
    wbd, lam = vin[0], vin[2][...]
    parts = _rg_gates(u, wbd, vin[1][...], lam)
    dab = ((da_f, db_f), (da_r, db_r))
    dsig_lam = -1.0 / (1.0 + jnp.exp(lam))
    du_halves, dpre_halves, dlam = [], [], [[None, None], [None, None]]
    for h in range(2):
        du = jnp.zeros_like(parts[(0, h)]["uh"])
        dpre = []
        for d in range(2):
            p = parts[(d, h)]
            da = dab[d][0][:, h * RH:(h + 1) * RH]
            db = dab[d][1][:, h * RH:(h + 1) * RH]
            d_mult = db * p["i"] * p["uh"]
            d_i = db * p["mult"] * p["uh"]
            du = du + db * p["mult"] * p["i"]
            d_la = da * p["a"] - d_mult * (p["e2"] + 1.0) * p["inv_mult"]
            d_r = d_la * ((-RG_C) * p["sp"])
            dlam[d][h] = _sum0(d_la * ((-RG_C) * p["r"])) * dsig_lam[d:d + 1, h * RH:(h + 1) * RH]
            dpre += [d_r * p["r"] * (1.0 - p["r"]), d_i * p["i"] * (1.0 - p["i"])]
        dpre = jnp.concatenate(dpre, axis=1)
        du = du + lax.dot_general(dpre.astype(BF16), wbd[h], _NT, preferred_element_type=F32)
        du_halves.append(du)
        dpre_halves.append(dpre)
    dpre_all = jnp.concatenate(dpre_halves, axis=1)
    dlam_row = jnp.concatenate([dlam[0][0], dlam[0][1], dlam[1][0], dlam[1][1]], axis=1)
    return [dpre_all, jnp.concatenate(du_halves, axis=1)], [_sum0(dpre_all), dlam_row]


def _tile_flags(i, n_tiles, seq_starts):
    starts_here = functools.reduce(jnp.logical_or, [i == s for s in seq_starts])
    ends_here = functools.reduce(jnp.logical_or, [i + 1 == s for s in seq_starts] + [i + 1 == n_tiles])
    return jnp.logical_not(starts_here), jnp.logical_not(ends_here)


def _halo_specs(col0, cw):
    hb = ROW_TILE // HALO
    prev = pl.BlockSpec((HALO, cw), lambda i, c: (jnp.maximum(i * hb - 1, 0), col0 + c))
    cur = pl.BlockSpec((ROW_TILE, cw), lambda i, c: (i, col0 + c))
    return prev, cur, hb


def _window(prev_ref, cur_ref, next_ref, has_prev, has_next):
    prev = jnp.where(has_prev, prev_ref[...], 0.0)
    nxt = jnp.where(has_next, next_ref[...], 0.0)
    return jnp.concatenate([prev, cur_ref[...], nxt], axis=0)


def _tap_reader(win):
    sub = 8
    n = win.shape[0]
    shifted = {0: win}

    def tap(off):
        s = off % sub
        if s not in shifted:
            shifted[s] = pltpu.roll(win, n - s, axis=0)
        return shifted[s][off - s:off - s + ROW_TILE, :]

    return tap


def _dwconv(name, x, col0, w, bias, pad_left, seq_starts, n_ch, cw=256):
    n_rows = x.shape[0]
    n_tiles = n_rows // ROW_TILE
    n_taps = w.shape[0]
    prev_spec, cur_spec, hb = _halo_specs(col0, cw)
    last_hb = n_rows // HALO - 1
    next_spec = pl.BlockSpec((HALO, cw), lambda i, c: (jnp.minimum((i + 1) * hb, last_hb), col0 + c))

    def kern(prev_ref, cur_ref, next_ref, w_ref, b_ref, o_ref):
        has_prev, has_next = _tile_flags(pl.program_id(0), n_tiles, seq_starts)
        win = _window(prev_ref, cur_ref, next_ref, has_prev, has_next)
        tap = _tap_reader(win)
        wv = w_ref[...]
        acc = jnp.zeros((ROW_TILE, cw), F32) + b_ref[...]
        for k in range(n_taps):
            acc = acc + wv[k:k + 1, :] * tap(HALO + k - pad_left)
        o_ref[...] = acc

    return _pcall(
        kern, name=name, grid=(n_tiles, n_ch // cw),
        in_specs=[prev_spec, cur_spec, next_spec,
                  pl.BlockSpec((n_taps, cw), lambda i, c: (0, c)), pl.BlockSpec((1, cw), lambda i, c: (0, c))],
        out_specs=pl.BlockSpec((ROW_TILE, cw), lambda i, c: (i, c)),
        out_shape=_sds((n_rows, n_ch), F32), compiler_params=_cparams(),
    )(x, x, x, w, bias)


def _dwconv_wgrad(name, dy, x, col0, n_taps, pad_left, seq_starts, n_ch, cw=256):
    n_rows = dy.shape[0]
    n_tiles = n_rows // ROW_TILE
    n_out = -(-(n_taps + 1) // 8) * 8
    prev_spec, cur_spec, hb = _halo_specs(col0, cw)
    last_hb = n_rows // HALO - 1
    next_spec = pl.BlockSpec((HALO, cw), lambda c, i: (jnp.minimum((i + 1) * hb, last_hb), col0 + c))
    prev_spec = pl.BlockSpec((HALO, cw), lambda c, i: (jnp.maximum(i * hb - 1, 0), col0 + c))
    cur_spec = pl.BlockSpec((ROW_TILE, cw), lambda c, i: (i, col0 + c))

    def kern(dy_ref, prev_ref, cur_ref, next_ref, o_ref):
        i = pl.program_id(1)
        has_prev, has_next = _tile_flags(i, n_tiles, seq_starts)
        win = _window(prev_ref, cur_ref, next_ref, has_prev, has_next)
        dyv = dy_ref[...]
        tap = _tap_reader(win)
        rid = lax.broadcasted_iota(jnp.int32, (n_out, cw), 0)
        inc = jnp.where(rid == n_taps, _sum0(dyv), 0.0)
        for k in range(n_taps):
            inc = inc + jnp.where(rid == k, _sum0(dyv * tap(HALO + k - pad_left)), 0.0)

        @pl.when(i == 0)
        def _():
            o_ref[...] = jnp.zeros_like(o_ref)

        o_ref[...] += inc

    return _pcall(
        kern, name=name, grid=(n_ch // cw, n_tiles),
        in_specs=[pl.BlockSpec((ROW_TILE, cw), lambda c, i: (i, c)), prev_spec, cur_spec, next_spec],
        out_specs=pl.BlockSpec((n_out, cw), lambda c, i: (0, c)),
        out_shape=_sds((n_out, n_ch), F32), compiler_params=_cparams(),
    )(dy, x, x, x)


N_SCAN = TA // ROW_TILE


def _rev_block(j):
    return jnp.where(j == 0, 0, N_SCAN - j)


def _scan_fwd(a_f, b_f, a_r, b_r):
    fwd_spec = pl.BlockSpec((ROW_TILE, R), lambda i: (i, 0))
    rev_spec = pl.BlockSpec((ROW_TILE, R), lambda i: (_rev_block(i), 0))
    hin_spec = pl.BlockSpec((None, 1, R), lambda i: (i, 0, 0))

    def kern(af, bf, ar, br, yf, yr, hin_f, hin_r, hf_s, hr_s):
        @pl.when(pl.program_id(0) == 0)
        def _():
            hf_s[...] = jnp.zeros_like(hf_s)
            hr_s[...] = jnp.zeros_like(hr_s)

        hin_f[...] = hf_s[...]
        hin_r[...] = hr_s[...]

        def step(s8, carry):
            hf, hr = carry
            t0 = pl.multiple_of(s8 * 8, 8)
            for q in range(8):
                tf = t0 + q
                hf = af[pl.ds(tf, 1), :] * hf + bf[pl.ds(tf, 1), :]
                yf[pl.ds(tf, 1), :] = hf
                tr = ROW_TILE - 1 - tf
                hr = ar[pl.ds(tr, 1), :] * hr + br[pl.ds(tr, 1), :]
                yr[pl.ds(tr, 1), :] = hr
            return hf, hr

        hf, hr = lax.fori_loop(0, ROW_TILE // 8, step, (hf_s[...], hr_s[...]))
        hf_s[...] = hf
        hr_s[...] = hr

    return _pcall(
        kern, name="scan_fwd", grid=(N_SCAN,),
        in_specs=[fwd_spec, fwd_spec, rev_spec, rev_spec],
        out_specs=[fwd_spec, rev_spec, hin_spec, hin_spec],
        out_shape=[_sds((TA, R), F32), _sds((TA, R), F32), _sds((N_SCAN, 1, R), F32), _sds((N_SCAN, 1, R), F32)],
        scratch_shapes=[pltpu.VMEM((1, R), F32), pltpu.VMEM((1, R), F32)], compiler_params=_cparams(),
    )(a_f, b_f, a_r, b_r)


def _scan_bwd(dy, a_f, y_f, hin_f, a_r, y_r, hin_r):
    fwd_spec = pl.BlockSpec((ROW_TILE, R), lambda i: (N_SCAN - 1 - i, 0))
    rev_spec = pl.BlockSpec((ROW_TILE, R), lambda i: (_rev_block(N_SCAN - 1 - i), 0))
    hin_spec = pl.BlockSpec((None, 1, R), lambda i: (N_SCAN - 1 - i, 0, 0))
    last = ROW_TILE - 1

    def kern(dyf, af, yf, hf0, dyr, ar, yr, hr0, daf, dbf, dar, dbr, gf_s, anf_s, gr_s, anr_s):
        @pl.when(pl.program_id(0) == 0)
        def _():
            for r in (gf_s, anf_s, gr_s, anr_s):
                r[...] = jnp.zeros_like(r)

        def one(dy_ref, a_ref, y_ref, da_ref, db_ref, g, an, p, pprev):
            gnew = dy_ref[pl.ds(p, 1), :] + an * g
            db_ref[pl.ds(p, 1), :] = gnew
            da_ref[pl.ds(p, 1), :] = gnew * y_ref[pl.ds(pprev, 1), :]
            return gnew, a_ref[pl.ds(p, 1), :]

        def step(s8, carry):
            gf, anf, gr, anr = carry
            base = s8 * 8
            for q in range(8):
                s = last - (base + q)
                gf, anf = one(dyf, af, yf, daf, dbf, gf, anf, s, s - 1)
                gr, anr = one(dyr, ar, yr, dar, dbr, gr, anr, last - s, last - s + 1)
            return gf, anf, gr, anr

        carry = (gf_s[...], anf_s[...], gr_s[...], anr_s[...])
        carry = lax.fori_loop(0, ROW_TILE // 8 - 1, step, carry)
        gf, anf, gr, anr = carry
        for s in range(7, 0, -1):
            gf, anf = one(dyf, af, yf, daf, dbf, gf, anf, s, s - 1)
            gr, anr = one(dyr, ar, yr, dar, dbr, gr, anr, last - s, last - s + 1)
        gf0 = dyf[0:1, :] + anf * gf
        dbf[0:1, :] = gf0
        daf[0:1, :] = gf0 * hf0[...]
        gr0 = dyr[last:last + 1, :] + anr * gr
        dbr[last:last + 1, :] = gr0
        dar[last:last + 1, :] = gr0 * hr0[...]
        gf_s[...] = gf0
        anf_s[...] = af[0:1, :]
        gr_s[...] = gr0
        anr_s[...] = ar[last:last + 1, :]

    return _pcall(
        kern, name="scan_bwd", grid=(N_SCAN,),
        in_specs=[fwd_spec, fwd_spec, fwd_spec, hin_spec, rev_spec, rev_spec, rev_spec, hin_spec],
        out_specs=[fwd_spec, fwd_spec, rev_spec, rev_spec],
        out_shape=[_sds((TA, R), F32)] * 4,
        scratch_shapes=[pltpu.VMEM((1, R), F32)] * 4, compiler_params=_cparams(),
    )(dy, a_f, y_f, hin_f, dy, a_r, y_r, hin_r)


def _me():
    return lax.axis_index("x"), lax.axis_index("y"), lax.axis_index("c")


def _other_chips(mx, my):
    return [(1 - mx, my), (mx, 1 - my), (1 - mx, 1 - my)]


def _rcopy(src, dst, ssem, rsem, dev):
    return pltpu.make_async_remote_copy(src_ref=src, dst_ref=dst, send_sem=ssem, recv_sem=rsem,
                                        device_id=dev, device_id_type=MESH)


def _allgather8(name, x, dep=None):
    rows, cols = x.shape
    n_dep = len(_behind(dep))

    def kern(x_ref, *rest):
        o_ref, ssem, rsem, lsem = rest[n_dep:]
        mx, my, mc = _me()
        me = 4 * mx + 2 * my + mc
        peers = []
        for k in range(1, 8):
            px = 1 - mx if (k >> 2) & 1 else mx
            py = 1 - my if (k >> 1) & 1 else my
            pc = 1 - mc if k & 1 else mc
            peers.append((px, py, pc))
        mine = pltpu.make_async_copy(x_ref, o_ref.at[me], lsem)
        mine.start()
        sends = [_rcopy(x_ref, o_ref.at[me], ssem.at[k], rsem.at[k], p) for k, p in enumerate(peers)]
        for cp in sends:
            cp.start()
        for k, (px, py, pc) in enumerate(peers):
            _rcopy(x_ref, o_ref.at[4 * px + 2 * py + pc], ssem.at[k], rsem.at[k], (px, py, pc)).wait_recv()
        for cp in sends:
            cp.wait_send()
        mine.wait()

    return _pcall(
        kern, name=name, in_specs=[ANY] * (1 + n_dep), out_specs=ANY, out_shape=_sds((8, rows, cols), F32),
        scratch_shapes=[pltpu.SemaphoreType.DMA((7,)), pltpu.SemaphoreType.DMA((7,)), pltpu.SemaphoreType.DMA(())],
    )(x, *_behind(dep))


def _gather8_start(name, x, after):
    def kern(x_in, after_ref, x_ref, o_ref, ssem, rsem, token):
        mx, my, mc = _me()
        me = 4 * mx + 2 * my + mc
        for k, p in enumerate(_peers7(mx, my, mc)):
            _rcopy(x_ref, o_ref.at[me], ssem.at[k], rsem.at[k], p).start()
        token[...] = jnp.zeros_like(token)

    dma = pltpu.SemaphoreType.DMA
    res = _pcall(
        kern, name=name, in_specs=[ANY, ANY],
        out_specs=[ANY, ANY, SEM, SEM, pl.BlockSpec(memory_space=pltpu.VMEM)],
        out_shape=[_sds(x.shape, x.dtype), _sds((8,) + x.shape, x.dtype), dma((7,)), dma((7,)), _sds((8, LANE), F32)],
        input_output_aliases={0: 0}, compiler_params=pltpu.CompilerParams(has_side_effects=_DATAFLOW),
    )(x, after)
    return tuple(res[:4]), res[4]


def _gather8_wait(name, x, out, ssem, rsem, after):
    def kern(x_ref, o_ref, ssem_ref, rsem_ref, after_ref, x_out, o_out):
        mx, my, mc = _me()
        for k, (px, py, pc) in enumerate(_peers7(mx, my, mc)):
            cp = _rcopy(x_ref, o_ref.at[4 * px + 2 * py + pc], ssem_ref.at[k], rsem_ref.at[k], (px, py, pc))
            cp.wait_recv()
            cp.wait_send()

    res = _pcall(
        kern, name=name, in_specs=[ANY, ANY, SEM, SEM, ANY], out_specs=[ANY, ANY],
        out_shape=[_sds(x.shape, x.dtype), _sds(out.shape, out.dtype)], input_output_aliases={0: 0, 1: 1},
        compiler_params=pltpu.CompilerParams(has_side_effects=_DATAFLOW),
    )(x, out, ssem, rsem, after)
    return res[0], res[1]


def _peers7(mx, my, mc):
    peers = []
    for k in range(1, 8):
        peers.append((1 - mx if (k >> 2) & 1 else mx, 1 - my if (k >> 1) & 1 else my, 1 - mc if k & 1 else mc))
    return peers


def _share_halves(name, fulls):
    n = len(fulls)

    def kern(*refs):
        o = refs[n:2 * n]
        ss, rs = refs[2 * n:]
        mx, my, mc = _me()
        sib = (mx, my, 1 - mc)
        sends = []
        for t in range(n):
            cp = _rcopy(o[t].at[mc], o[t].at[mc], ss.at[t], rs.at[t], sib)
            cp.start()
            sends.append(cp)
        for t in range(n):
            _rcopy(o[t].at[1 - mc], o[t].at[1 - mc], ss.at[t], rs.at[t], sib).wait_recv()
        for cp in sends:
            cp.wait_send()

    dma = pltpu.SemaphoreType.DMA
    return _pcall(
        kern, name=name, in_specs=[ANY] * n, out_specs=[ANY] * n,
        out_shape=[_sds(f.shape, f.dtype) for f in fulls], input_output_aliases={t: t for t in range(n)},
        scratch_shapes=[dma((n,)), dma((n,))],
    )(*fulls)


def _tiled_sp(name, fn, grid, sp, ins, outs):
    n_in = len(ins)

    def kern(sp_ref, *refs):
        tout = fn([r[...] for r in refs[:n_in]])
        for r, v in zip(refs[n_in:], tout):
            r[...] = v.astype(r.dtype)

    gs = pltpu.PrefetchScalarGridSpec(num_scalar_prefetch=1, grid=tuple(grid),
                                      in_specs=[s for _, s in ins], out_specs=[s for _, s in outs])
    res = _pcall(kern, name=name, grid_spec=gs, out_shape=[o for o, _ in outs], compiler_params=_cparams(),
                 )(sp, *[a for a, _ in ins])
    return list(res)


def _row_tile(rows, cols, itemsize=4, budget=2 * 1024 * 1024):
    tr = rows
    while tr * cols * itemsize > budget and tr % 32 == 0:
        tr //= 2
    return tr


def _place_big(shards, place):
    slots = []
    for t, s in enumerate(shards):
        rr, cc = s.shape[1], s.shape[2]
        tr = _row_tile(rr, cc)
        (slot,) = _tiled_sp(
            f"place{t}", lambda tin: [tin[0]], (2, rr // tr), place,
            [(s, pl.BlockSpec((None, tr, cc), lambda h, i, sp: (h, i, 0)))],
            [(_sds((4, 2, rr, cc), BF16), pl.BlockSpec((None, None, tr, cc), lambda h, i, sp: (sp[0], h, i, 0)))])
        slots.append(slot)
    return slots


def _allreduce_small_begin(vec, place):
    hr = vec.shape[0] // 2
    tr = _row_tile(hr, LANE)
    blk = (None, None, tr, LANE)
    (pair,) = _tiled_sp(
        "small_place", lambda tin: [tin[0]], (2, hr // tr), place,
        [(vec.reshape(2, hr, LANE), pl.BlockSpec((None, tr, LANE), lambda h, i, sp: (h, i, 0)))],
        [(_sds((2, 2, hr, LANE), F32), pl.BlockSpec(blk, lambda h, i, sp: (sp[1], h, i, 0)))])
    (pair,) = _share_halves("small_share", [pair])
    (slot,) = _tiled_sp(
        "small_pair_add", lambda tin: [tin[0] + tin[1]], (2, hr // tr), place,
        [(pair, pl.BlockSpec(blk, lambda h, i, sp: (0, h, i, 0))),
         (pair, pl.BlockSpec(blk, lambda h, i, sp: (1, h, i, 0)))],
        [(_sds((4, 2, hr, LANE), F32), pl.BlockSpec(blk, lambda h, i, sp: (sp[0], h, i, 0)))])
    fly, sems, token = _gather_start("small_start", [slot], ((0,),), pair)
    return (fly, sems), token


def _allreduce_small_end(state, after):
    fly, sems = state
    (chips,) = _swap_halves("small_swap", _gather_wait("small_wait", fly, *sems, after))
    hr = chips.shape[2]
    tr = _row_tile(hr, LANE)
    blk = (None, None, tr, LANE)
    (total,) = _tiled(
        "small_chip_sum", lambda ids, tin, vin: ([((tin[0] + tin[1]) + tin[2]) + tin[3]], []), (2, hr // tr),
        [(chips, pl.BlockSpec(blk, lambda h, i, _j=j: (_j, h, i, 0))) for j in range(4)], [],
        [(_sds((2, hr, LANE), F32), pl.BlockSpec((None, tr, LANE), lambda h, i: (h, i, 0)))])
    return total.reshape(2 * hr, LANE)


SEM =pl.BlockSpec(memory_space=pltpu.SEMAPHORE)
_DATAFLOW = pltpu.SideEffectType.DATAFLOW_SIDE_EFFECTING


def _gather_start(name, slots, groups, after):
    n = len(slots)

    def kern(*refs):
        o = refs[n + 1:2 * n + 1]
        sems, token = refs[2 * n + 1:-1], refs[-1]
        mx, my, mc = _me()
        j0 = 2 * mx + my
        for gi, grp in enumerate(groups):
            for k, t in enumerate(grp):
                for q, (qx, qy) in enumerate(_other_chips(mx, my)):
                    _rcopy(o[t].at[j0, mc], o[t].at[j0, mc], sems[2 * gi].at[3 * k + q],
                           sems[2 * gi + 1].at[3 * k + q], (qx, qy, mc)).start()
        token[...] = jnp.zeros_like(token)

    sem_shapes = []
    for grp in groups:
        sem_shapes += [pltpu.SemaphoreType.DMA((3 * len(grp),))] * 2
    res = _pcall(
        kern, name=name, in_specs=[ANY] * (n + 1),
        out_specs=[ANY] * n + [SEM] * len(sem_shapes) + [pl.BlockSpec(memory_space=pltpu.VMEM)],
        out_shape=[_sds(w.shape, w.dtype) for w in slots] + sem_shapes + [_sds((8, LANE), F32)],
        input_output_aliases={t: t for t in range(n)},
        compiler_params=pltpu.CompilerParams(has_side_effects=_DATAFLOW),
    )(*slots, after)
    return list(res[:n]), list(res[n:-1]), res[-1]


def _gather_wait(name, bufs, ssem, rsem, after):
    n = len(bufs)

    def kern(*refs):
        b = refs[:n]
        ssem_ref, rsem_ref = refs[n], refs[n + 1]
        mx, my, mc = _me()
        j0 = 2 * mx + my
        for k in range(n):
            for q, (qx, qy) in enumerate(_other_chips(mx, my)):
                jq = 2 * qx + qy
                _rcopy(b[k].at[jq, mc], b[k].at[jq, mc], ssem_ref.at[3 * k + q], rsem_ref.at[3 * k + q],
                       (qx, qy, mc)).wait_recv()
                _rcopy(b[k].at[j0, mc], b[k].at[j0, mc], ssem_ref.at[3 * k + q], rsem_ref.at[3 * k + q],
                       (qx, qy, mc)).wait_send()

    return list(_pcall(
        kern, name=name, in_specs=[ANY] * n + [SEM, SEM, ANY], out_specs=[ANY] * n,
        out_shape=[_sds(w.shape, w.dtype) for w in bufs], input_output_aliases={k: k for k in range(n)},
        compiler_params=pltpu.CompilerParams(has_side_effects=_DATAFLOW),
    )(*bufs, ssem, rsem, after))


def _swap_halves(name, bufs):
    n = len(bufs)

    def kern(*refs):
        o = refs[n:2 * n]
        ss, rs = refs[2 * n:]
        mx, my, mc = _me()
        sib = (mx, my, 1 - mc)
        sends = []
        for k in range(n):
            for q, (qx, qy) in enumerate(_other_chips(mx, my)):
                jq = 2 * qx + qy
                cp = _rcopy(o[k].at[jq, mc], o[k].at[jq, mc], ss.at[3 * k + q], rs.at[3 * k + q], sib)
                cp.start()
                sends.append(cp)
        for k in range(n):
            for q, (qx, qy) in enumerate(_other_chips(mx, my)):
                jq = 2 * qx + qy
                _rcopy(o[k].at[jq, 1 - mc], o[k].at[jq, 1 - mc], ss.at[3 * k + q], rs.at[3 * k + q], sib).wait_recv()
        for cp in sends:
            cp.wait_send()

    dma = pltpu.SemaphoreType.DMA
    return list(_pcall(
        kern, name=name, in_specs=[ANY] * n, out_specs=[ANY] * n,
        out_shape=[_sds(w.shape, w.dtype) for w in bufs], input_output_aliases={k: k for k in range(n)},
        scratch_shapes=[dma((3 * n,)), dma((3 * n,))],
    )(*bufs))


def _swap_start(name, bufs, after):
    n = len(bufs)

    def kern(*refs):
        o = refs[n + 1:2 * n + 1]
        ssem, rsem, token = refs[2 * n + 1:]
        mx, my, mc = _me()
        for k in range(n):
            for q, (qx, qy) in enumerate(_other_chips(mx, my)):
                jq = 2 * qx + qy
                _rcopy(o[k].at[jq, mc], o[k].at[jq, mc], ssem.at[3 * k + q], rsem.at[3 * k + q], (mx, my, 1 - mc)).start()
        token[...] = jnp.zeros_like(token)

    dma = pltpu.SemaphoreType.DMA
    res = _pcall(
        kern, name=name, in_specs=[ANY] * (n + 1),
        out_specs=[ANY] * n + [SEM, SEM, pl.BlockSpec(memory_space=pltpu.VMEM)],
        out_shape=[_sds(w.shape, w.dtype) for w in bufs] + [dma((3 * n,)), dma((3 * n,)), _sds((8, LANE), F32)],
        input_output_aliases={k: k for k in range(n)},
        compiler_params=pltpu.CompilerParams(has_side_effects=_DATAFLOW),
    )(*bufs, after)
    return (list(res[:n]), res[n], res[n + 1]), res[n + 2]


def _swap_wait(name, bufs, ssem, rsem, after):
    n = len(bufs)

    def kern(*refs):
        b = refs[:n]
        ssem_ref, rsem_ref = refs[n], refs[n + 1]
        mx, my, mc = _me()
        sib = (mx, my, 1 - mc)
        for k in range(n):
            for q, (qx, qy) in enumerate(_other_chips(mx, my)):
                jq = 2 * qx + qy
                _rcopy(b[k].at[jq, 1 - mc], b[k].at[jq, 1 - mc], ssem_ref.at[3 * k + q], rsem_ref.at[3 * k + q],
                       sib).wait_recv()
                _rcopy(b[k].at[jq, mc], b[k].at[jq, mc], ssem_ref.at[3 * k + q], rsem_ref.at[3 * k + q],
                       sib).wait_send()

    return list(_pcall(
        kern, name=name, in_specs=[ANY] * n + [SEM, SEM, ANY], out_specs=[ANY] * n,
        out_shape=[_sds(w.shape, w.dtype) for w in bufs], input_output_aliases={k: k for k in range(n)},
        compiler_params=pltpu.CompilerParams(has_side_effects=_DATAFLOW),
    )(*bufs, ssem, rsem, after))


def _to_sibling(mx, my, mc):
    return [((j, 1 - mc), j, (mx, my, 1 - mc)) for j in range(4)]


def _to_chips(mx, my, mc):
    return [((2 * qx + qy,), q, (qx, qy, mc)) for q, (qx, qy) in enumerate(_other_chips(mx, my))]


def _send_start(name, srcs, plan, land_shapes, after):
    n = len(srcs)
    per = len(plan(0, 0, 0))

    def kern(*refs):
        s, land = refs[n + 1:2 * n + 1], refs[2 * n + 1:3 * n + 1]
        ssem, rsem, token = refs[3 * n + 1:]
        for k in range(n):
            for q, (idx, slot, dev) in enumerate(plan(*_me())):
                _rcopy(s[k].at[idx], land[k].at[slot], ssem.at[per * k + q], rsem.at[per * k + q], dev).start()
        token[...] = jnp.zeros_like(token)

    dma = pltpu.SemaphoreType.DMA
    res = _pcall(
        kern, name=name, in_specs=[ANY] * (n + 1),
        out_specs=[ANY] * (2 * n) + [SEM, SEM, pl.BlockSpec(memory_space=pltpu.VMEM)],
        out_shape=[_sds(s.shape, s.dtype) for s in srcs] + [_sds(ls, s.dtype) for ls, s in zip(land_shapes, srcs)]
        + [dma((per * n,)), dma((per * n,)), _sds((8, LANE), F32)],
        input_output_aliases={k: k for k in range(n)},
        compiler_params=pltpu.CompilerParams(has_side_effects=_DATAFLOW),
    )(*srcs, after)
    return (list(res[:n]), list(res[n:2 * n]), res[2 * n], res[2 * n + 1]), res[2 * n + 2]


def _send_wait(name, srcs, lands, ssem, rsem, plan, after):
    n = len(srcs)
    per = len(plan(0, 0, 0))

    def kern(*refs):
        s, land = refs[:n], refs[n:2 * n]
        ssem_ref, rsem_ref = refs[2 * n], refs[2 * n + 1]
        for k in range(n):
            for q, (idx, slot, dev) in enumerate(plan(*_me())):
                cp = _rcopy(s[k].at[idx], land[k].at[slot], ssem_ref.at[per * k + q], rsem_ref.at[per * k + q], dev)
                cp.wait_recv()
                cp.wait_send()

    res = _pcall(
        kern, name=name, in_specs=[ANY] * (2 * n) + [SEM, SEM, ANY], out_specs=[ANY] * (2 * n),
        out_shape=[_sds(a.shape, a.dtype) for a in list(srcs) + list(lands)],
        input_output_aliases={k: k for k in range(2 * n)},
        compiler_params=pltpu.CompilerParams(has_side_effects=_DATAFLOW),
    )(*srcs, *lands, ssem, rsem, after)
    return list(res[:n]), list(res[n:])


def _reduce_begin(tag, parts, after):
    return _send_start(f"pair_start_{tag}", parts, _to_sibling, [(4,) + p.shape[2:] for p in parts], after)


def _reduce_mid(tag, pairing, place, after):
    parts, theirs = _send_wait(f"pair_wait_{tag}", *pairing, _to_sibling, after)
    sums = []
    for k, (p, o) in enumerate(zip(parts, theirs)):
        rr, cc = p.shape[2], p.shape[3]
        tr = _row_tile(rr, cc)
        (s_k,) = _tiled_sp(
            f"pair_add_{tag}{k}", lambda tin: [tin[0].astype(F32) + tin[1].astype(F32)], (4, rr // tr), place,
            [(p, pl.BlockSpec((None, None, tr, cc), lambda j, i, sp: (j, sp[1], i, 0))),
             (o, pl.BlockSpec((None, tr, cc), lambda j, i, sp: (j, i, 0)))],
            [(_sds((4, rr, cc), BF16), pl.BlockSpec((None, tr, cc), lambda j, i, sp: (j, i, 0)))])
        sums.append(s_k)
    return _send_start(f"chips_start_{tag}", sums, _to_chips, [(3,) + s.shape[1:] for s in sums], theirs[0])


def _reduce_end(tag, flying, place, after):
    sums, lands = _send_wait(f"chips_wait_{tag}", *flying, _to_chips, after)
    fulls = []
    for k, (s, q) in enumerate(zip(sums, lands)):
        rr, cc = q.shape[1], q.shape[2]
        tr = _row_tile(rr, cc)

        def add4(tin):
            return [((tin[0].astype(F32) + tin[1].astype(F32)) + tin[2].astype(F32)) + tin[3].astype(F32)]

        ins = [(s, pl.BlockSpec((None, tr, cc), lambda i, sp: (sp[0], i, 0)))]
        ins += [(q, pl.BlockSpec((None, tr, cc), lambda i, sp, _k=kk: (_k, i, 0))) for kk in range(3)]
        (f_k,) = _tiled_sp(f"chip_add_{tag}{k}", add4, (rr // tr,), place, ins,
                           [(_sds((2, rr, cc), F32), pl.BlockSpec((None, tr, cc), lambda i, sp: (sp[1], i, 0)))])
        fulls.append(f_k)
    return fulls


def _pack(parts, PACK_ROWS=PACK_ROWS):
    flat, offs, pos = [], [], 0
    for p in parts:
        v = p.reshape(-1).astype(F32)
        n = -(-v.shape[0] // LANE) * LANE
        flat.append(jnp.pad(v, (0, n - v.shape[0])))
        offs.append((pos, v.shape[0], p.shape))
        pos += n
    total = -(-pos // (PACK_ROWS * LANE)) * PACK_ROWS * LANE
    flat.append(jnp.zeros((total - pos,), F32))
    return jnp.concatenate(flat).reshape(-1, LANE), offs


def _unpack(vec, offs):
    v = vec.reshape(-1)
    return [v[p:p + n].reshape(shape) for p, n, shape in offs]


def _adamw_math(wv, gv, mv, vv):
    bc1 = 1.0 - ADAM_B1 ** ADAM_STEP
    bc2 = 1.0 - ADAM_B2 ** ADAM_STEP
    mn = ADAM_B1 * mv + (1.0 - ADAM_B1) * gv
    vn = ADAM_B2 * vv + (1.0 - ADAM_B2) * (gv * gv)
    delta = -ADAM_LR * ((mn / bc1) / (jnp.sqrt(vn / bc2) + ADAM_EPS) + ADAM_WD * wv)
    return delta, mn, vn


def _adamw(name, w, g, m, v):
    rows, cols = w.shape
    tr = rows
    for cand in (512, 256, 128, 64, 32, 16, 8):
        if rows % cand == 0 and cand * cols * 4 <= 2 * 1024 * 1024:
            tr = cand
            break

    def fn(ids, tin, vin):
        return list(_adamw_math(*tin)), []

    spec = pl.BlockSpec((tr, cols), lambda i: (i, 0))
    outs = [(_sds((rows, cols), F32), spec)] * 3
    return _tiled(name, fn, (rows // tr,), [(a, spec) for a in (w, g, m, v)], [], outs)


def _adamw_many(name, ws, gs, ms, vs):
    n = len(ws)
    views = [(-1, a.shape[-1]) if a.ndim > 1 else (1, -1) for a in ws]
    flat = lambda arrs: [a.reshape(vw) for a, vw in zip(arrs, views)]

    def kern(*refs):
        ins, outs = refs[:4 * n], refs[4 * n:]
        for t in range(n):
            res = _adamw_math(*[ins[q * n + t][...] for q in range(4)])
            for q in range(3):
                outs[q * n + t][...] = res[q]

    shapes = [_sds(a.shape, F32) for a in flat(ws)]
    res = _pcall(kern, name=name, out_shape=shapes * 3, compiler_params=_cparams(),
                 )(*flat(ws), *flat(gs), *flat(ms), *flat(vs))
    back = lambda part: [a.reshape(w.shape) for a, w in zip(part, ws)]
    return back(res[:n]), back(res[n:2 * n]), back(res[2 * n:])


def _pos_embed():
    n_rows = T // GRID_W
    q = D // 4
    omega = 1.0 / (10000.0 ** (jnp.arange(q, dtype=F32) / q))
    er = jnp.arange(n_rows, dtype=jnp.int32).astype(F32)[:, None] * omega[None, :]
    ec = jnp.arange(GRID_W, dtype=jnp.int32).astype(F32)[:, None] * omega[None, :]
    by_row = jnp.concatenate([jnp.sin(er), jnp.cos(er)], axis=-1)
    by_col = jnp.concatenate([jnp.sin(ec), jnp.cos(ec)], axis=-1)
    return jnp.concatenate([jnp.repeat(by_row, GRID_W, axis=0), jnp.tile(by_col, (n_rows, 1))], axis=-1)


def _dense_gates(w_a, w_x):
    rows = jnp.stack([w_a[0], w_x[0], w_a[1], w_x[1]]).reshape(4, 2, RH, BLK)
    mask, spread = _block_mask(), _block_spread().T.astype(BF16)

    def kern(r_ref, m_ref, s_ref, o_ref):
        tiled = jnp.dot(r_ref[...].astype(BF16), s_ref[...], preferred_element_type=F32)
        o_ref[...] = (tiled * m_ref[...]).astype(o_ref.dtype)

    return _pcall(
        kern, name="gates_dense", grid=(2, 4),
        in_specs=[pl.BlockSpec((None, None, RH, BLK), lambda h, q: (q, h, 0, 0)),
                  pl.BlockSpec((RH, RH), lambda h, q: (0, 0)), pl.BlockSpec((BLK, RH), lambda h, q: (0, 0))],
        out_specs=pl.BlockSpec((None, RH, RH), lambda h, q: (h, 0, q)),
        out_shape=_sds((2, RH, NQ), BF16),
    )(rows, mask, spread)


def _block_mask():
    r = lax.broadcasted_iota(jnp.int32, (RH, RH), 0) // BLK
    c = lax.broadcasted_iota(jnp.int32, (RH, RH), 1) // BLK
    return (r == c).astype(F32)


def _block_spread():
    c = lax.broadcasted_iota(jnp.int32, (RH, BLK), 0) % BLK
    j = lax.broadcasted_iota(jnp.int32, (RH, BLK), 1)
    return (c == j).astype(F32)


def _fold_blocks(dense, mask, spread):
    return jnp.dot(dense * mask, spread, preferred_element_type=F32, precision=lax.Precision.HIGHEST)


def _gate_block_grads(folded):
    per = N_BLK // 2
    kinds = [jnp.concatenate([folded[h, q].reshape(per, BLK, BLK) for h in range(2)], axis=0) for q in range(4)]
    return jnp.stack([kinds[0], kinds[2]]), jnp.stack([kinds[1], kinds[3]])


def _gate_bias_dense(b_a, b_x):
    cols = []
    for h in range(2):
        for src in (b_a[0], b_x[0], b_a[1], b_x[1]):
            cols.append(src.reshape(R)[h * RH:(h + 1) * RH])
    return jnp.concatenate(cols).reshape(1, 2 * NQ)


def _gate_bias_grads(dgb):
    v = dgb.reshape(2, 4, RH)
    kinds = [jnp.concatenate([v[0, q], v[1, q]]).reshape(N_BLK, BLK) for q in range(4)]
    return jnp.stack([kinds[0], kinds[2]]), jnp.stack([kinds[1], kinds[3]])


def _residual_epilogue(next_norm):
    def epi(acc, ex):
        x_new = ex[0] + ex[1] * acc
        outs = [acc, x_new]
        if next_norm:
            outs.append(_norm_mod(x_new, ex[-3], ex[-2], ex[-1]))
        return outs
    return epi


def _mlp_fwd(tag, x_in, h, gate, w_in, w_out, next_norm=None, dep=None):
    tm = MM_TILE
    (r,) = _mm(f"{tag}_in", h, w_in, _NN, (T // tm, 4, 1),
               pl.BlockSpec((tm, D), lambda i, j, k: (i, 0)), pl.BlockSpec((None, D, D), lambda i, j, k: (j, 0, 0)),
               [(_sds((T, FF), BF16), pl.BlockSpec((tm, D), lambda i, j, k: (i, j)))], (tm, D),
               extra=[(d_, _full_spec(d_)) for d_ in _behind(dep)], epi=lambda acc, ex: [jnp.maximum(acc, 0.0)])
    row_spec = pl.BlockSpec((tm, D), lambda i, j, k: (i, 0))
    outs = [(_sds((T, D), F32), row_spec)] * 2 + ([(_sds((T, D), BF16), row_spec)] if next_norm else [])
    res = _mm(f"{tag}_out", r, w_out, _NN, (T // tm, 1, FF // D),
              pl.BlockSpec((tm, D), lambda i, j, k: (i, k)), pl.BlockSpec((D, D), lambda i, j, k: (k, 0)),
              outs, (tm, D),
              extra=[(x_in, row_spec), (gate, _full_spec(gate))] + [(v, _full_spec(v)) for v in next_norm or ()],
              a_pre=lambda a: a * a, epi=_residual_epilogue(next_norm))
    return dict(h=h, r=r, o=res[0], x_in=x_in), res[1], (res[2] if next_norm else None)


def _behind(dep):
    return [] if dep is None else [dep]


def _gate_bwd(tag, dx, o, gate, dep=None):
    def fn(ids, t, v):
        d_o = t[0] * v[0]
        return [d_o], [_sum0(t[0] * t[1]), _sum0(d_o)]
    return _tiled(f"{tag}_gate_bwd", fn, (T // ROW_TILE,), [_rows(dx), _rows(o)], [gate] + _behind(dep),
                  [_orow(T, D, BF16)], [(1, D), (1, D)])


def _norm_bwd(tag, dx_res, dh, dh_off, x, g_norm, sc, with_dx=True, dep=None):
    n_t = x.shape[0] // ROW_TILE

    def fn(ids, t, v):
        if with_dx:
            dres, dhv, xv = t
        else:
            dhv, xv = t
        dxv, d_sh, d_sc, d_g = _norm_mod_bwd(dhv, xv, v[0], v[1])
        return ([dres + dxv] if with_dx else []), [d_sh, d_sc, d_g]

    ins = ([_rows(dx_res)] if with_dx else []) + [_rows(dh, off=dh_off), _rows(x)]
    outs = [_orow(x.shape[0], D, F32)] if with_dx else []
    return _tiled(f"{tag}_norm_bwd", fn, (n_t,), ins, [g_norm, sc] + _behind(dep), outs, [(1, D)] * 3)


def _mlp_bwd(tag, dx, saved, g_norm, sc, gate, w_in, w_out, dep=None):
    d_o, d_gate, _ = _gate_bwd(tag, dx, saved["o"], gate, dep)
    tm = MM_TILE
    r = saved["r"]
    (da,) = _mm(f"{tag}_dz", d_o, w_out, _NT, (T // tm, FF // D, 1),
                pl.BlockSpec((tm, D), lambda i, j, k: (i, 0)), pl.BlockSpec((D, D), lambda i, j, k: (j, 0)),
                [(_sds((T, FF), BF16), pl.BlockSpec((tm, D), lambda i, j, k: (i, j)))], (tm, D),
                extra=[(r, pl.BlockSpec((tm, D), lambda i, j, k: (i, j)))],
                epi=lambda acc, ex: [acc * (2.0 * ex[0].astype(F32))])
    tk = MM_TILE
    (dw_out,) = _mm(f"{tag}_dwout", r, d_o, _TN, (FF // tm, 1, T // tk),
                    pl.BlockSpec((tk, tm), lambda i, j, k: (k, i)), pl.BlockSpec((tk, D), lambda i, j, k: (k, 0)),
                    [(_sds((FF, D), BF16), pl.BlockSpec((tm, D), lambda i, j, k: (i, 0)))], (tm, D),
                    a_pre=lambda a: a * a)
    (dh,) = _mm(f"{tag}_dh", da, w_in, _NT, (T // tm, 1, 4),
                pl.BlockSpec((tm, D), lambda i, j, k: (i, k)), pl.BlockSpec((None, D, D), lambda i, j, k: (k, 0, 0)),
                [(_sds((T, D), F32), pl.BlockSpec((tm, D), lambda i, j, k: (i, 0)))], (tm, D))
    (dw_in,) = _mm(f"{tag}_dwin", saved["h"], da, _TN, (D // tm, 4, T // tk),
                   pl.BlockSpec((tk, tm), lambda i, j, k: (k, i)), pl.BlockSpec((tk, D), lambda i, j, k: (k, j)),
                   [(_sds((4, D, D), BF16), pl.BlockSpec((None, tm, D), lambda i, j, k: (j, i, 0)))], (tm, D))
    dx_in, d_sh, d_sc, d_g = _norm_bwd(tag, dx, dh, 0, saved["x_in"], g_norm, sc)
    return dx_in, dw_in, dw_out, dict(sh=d_sh, sc=d_sc, gate=d_gate, g_norm=d_g)


def _local_step(x, ctx, tgt, mods, cmods, norm_g, final_g, rec, conf, wg, on_grads=None, wg_pre=None, on_later=None):
    on_grads = on_grads or (lambda group, dws: None)
    wg_pre = wg_pre or (lambda group, after: None)
    on_later = on_later or (lambda after: None)
    n_t = T // ROW_TILE
    row = lambda v: v.reshape(1, -1)
    m0 = [row(mods[0, q]) for q in range(6)]
    m1 = [row(mods[1, q]) for q in range(6)]
    g00, g01, g10, g11 = (row(norm_g[0, 0]), row(norm_g[0, 1]), row(norm_g[1, 0]), row(norm_g[1, 1]))
    csh, csc = row(cmods[0]), row(cmods[1])
    pos = _pos_embed()

    def prep0(ids, t, v):
        cx, xv, pv = t
        is_ctx = ids[0] == 0
        xin = jnp.where(is_ctx, cx, xv + pv)
        sh = jnp.where(is_ctx, v[3], v[1])
        sc = jnp.where(is_ctx, v[4], v[2])
        return [_norm_mod(xin, v[0], sc, sh), xv + pv], []

    hcat, x0 = _tiled(
        "prep0", prep0, (N_SCAN,),
        [(ctx, pl.BlockSpec((ROW_TILE, D), lambda i: (0, 0))), _rows(x, off=-1, clamp_lo=True),
         _rows(pos, off=-1, clamp_lo=True)],
        [g00, m0[0], m0[1], csh, csc],
        [_orow(TA, D, BF16), _orow(T, D, F32, off=-1, clamp_lo=True)])

    tm_a = REC_TILE
    w_rec = wg("rec", hcat)
    (a_in,) = _mm("rec_in", hcat, w_rec["rec_w_in"], _NN, (TA // tm_a, 4, 1),
                  pl.BlockSpec((tm_a, D), lambda i, j, k: (i, 0)),
                  pl.BlockSpec((None, D, RH), lambda i, j, k: (j, 0, 0)),
                  [(_sds((TA, 2 * R), F32), pl.BlockSpec((tm_a, RH), lambda i, j, k: (i, j)))], (tm_a, RH))
    rec_starts = (0, 1)
    u = _dwconv("rec_conv", a_in, R // CW_REC, rec["conv_w"], row(rec["conv_b"]), 1, rec_starts, R, CW_REC)
    wbd = _dense_gates(rec["w_a"], rec["w_x"])
    gbias = _gate_bias_dense(rec["b_a"], rec["b_x"])
    lam = rec["lam"]
    a_f, b_f, a_r, b_r = _tiled("rg_fwd", _rg_fwd_fn, (TA // RG_TILE,), [_rows(u, tm=RG_TILE)], [wbd, gbias, lam],
                                [_orow(TA, R, F32, tm=RG_TILE)] * 4, vec_refs=True)
    dep = wg_pre("mlp0", a_f)
    y_f, y_r, hin_f, hin_r = _scan_fwd(a_f, b_f, a_r, b_r)

    def rec_mid(ids, t, v):
        gp, yf, yr = t
        g, _ = _gelu(gp)
        return [g * (yf + yr)], []

    (m_rec,) = _tiled("rec_mid", rec_mid, (n_t,),
                      [_rows(a_in, R, off=1), _rows(y_f, off=1), _rows(y_r, off=1)], _behind(dep),
                      [_orow(T, R, BF16)])
    tm = MM_TILE
    row_spec = pl.BlockSpec((tm, D), lambda i, j, k: (i, 0))
    norm_mlp0 = (g01, m0[4], m0[3])
    o_rec, x1, h_mlp0 = _mm(
        "rec_out", m_rec, w_rec["rec_w_out"], _NN, (T // tm, 1, 1),
        pl.BlockSpec((tm, R), lambda i, j, k: (i, 0)), pl.BlockSpec((R, D), lambda i, j, k: (0, 0)),
        [(_sds((T, D), F32), row_spec)] * 2 + [(_sds((T, D), BF16), row_spec)], (tm, D),
        extra=[(x0, row_spec), (m0[2], _full_spec(m0[2]))] + [(v, _full_spec(v)) for v in norm_mlp0],
        epi=_residual_epilogue(norm_mlp0))
    w_m0 = wg("mlp0", x1)
    dep = wg_pre("conf", x1)
    mlp0, x2, h1 = _mlp_fwd("mlp0", x1, h_mlp0, m0[5], w_m0["w_in"], w_m0["w_out"], (g10, m1[1], m1[0]), dep)

    b_pw1 = row(conf["b_pw1"])
    w_cf = wg("conf", x2)
    dep = wg_pre("mlp1", x2)
    (pre,) = _mm("conf_pw1", h1, w_cf["conf_w_pw1"], _NN, (T // tm, 4, 1),
                 pl.BlockSpec((tm, D), lambda i, j, k: (i, 0)),
                 pl.BlockSpec((None, D, D // 2), lambda i, j, k: (j, 0, 0)),
                 [(_sds((T, 2 * D), F32), pl.BlockSpec((tm, D // 2), lambda i, j, k: (i, j)))], (tm, D // 2),
                 extra=[(b_pw1, pl.BlockSpec((1, D // 2), lambda i, j, k: (0, j)))]
                 + [(d_, _full_spec(d_)) for d_ in _behind(dep)],
                 epi=lambda acc, ex: [acc + ex[0]])
    (zg,) = _tiled("conf_glu", lambda ids, t, v: ([t[0] * _sigmoid(t[1])], []), (n_t,),
                   [_rows(pre, D, col=0), _rows(pre, D, col=1)], [], [_orow(T, D, F32)])
    conf_starts = (0,)
    zc = _dwconv("conf_conv", zg, 0, conf["conv_w"], row(conf["conv_b"]), CONF_KW // 2, conf_starts, D, CW_CONF)
    ln_g, ln_b = row(conf["ln_g"]), row(conf["ln_b"])

    def ln_silu(ids, t, v):
        nh, _ = _layernorm_parts(t[0])
        ln = nh * v[0] + v[1]
        return [ln * _sigmoid(ln)], []

    (s_conf,) = _tiled("conf_ln", ln_silu, (n_t,), [_rows(zc)], [ln_g, ln_b], [_orow(T, D, BF16)])
    b_pw2 = row(conf["b_pw2"])
    norm_mlp1 = (g11, m1[4], m1[3])
    pw2_epi = _residual_epilogue(norm_mlp1)
    y_conf, x3, h_mlp1 = _mm(
        "conf_pw2", s_conf, w_cf["conf_w_pw2"], _NN, (T // tm, 1, 1),
        row_spec, pl.BlockSpec((D, D), lambda i, j, k: (0, 0)),
        [(_sds((T, D), F32), row_spec)] * 2 + [(_sds((T, D), BF16), row_spec)], (tm, D),
        extra=[(x2, row_spec), (m1[2], _full_spec(m1[2])), (b_pw2, _full_spec(b_pw2))]
        + [(v, _full_spec(v)) for v in norm_mlp1],
        epi=lambda acc, ex: pw2_epi(acc + ex[2], ex))
    w_m1 = wg("mlp1", x3)
    mlp1, x4, _ = _mlp_fwd("mlp1", x3, h_mlp1, m1[5], w_m1["w_in"], w_m1["w_out"])

    fg = row(final_g)

    def head(ids, t, v):
        n, r = _rms(t[0])
        err = n * v[0] - t[1]
        d_out = err * (1.0 / D)
        dn = d_out * v[0]
        dxv = r * (dn - n * jnp.mean(dn * n, axis=-1, keepdims=True))
        part = jnp.sum(_sum0(err * err), axis=1, keepdims=True) * (0.5 / D)
        return [dxv], [part, _sum0(d_out * n)]

    dx4, loss, d_fg = _tiled("head", head, (n_t,), [_rows(x4), _rows(tgt)], [fg], [_orow(T, D, F32)],
                             [(1, 1), (1, D)])

    dx3, dw_in1, dw_out1, dm_mlp1 = _mlp_bwd("mlp1", dx4, mlp1, g11, m1[4], m1[5],
                                             w_m1["w_in"], w_m1["w_out"])
    dep = on_grads("mlp1", (dw_in1, dw_out1))
    d_y, d_g1c, d_bpw2 = _gate_bwd("conf", dx3, y_conf, m1[2], dep)
    tk = MM_TILE
    (dw_pw2,) = _mm("conf_dwpw2", s_conf, d_y, _TN, (D // tm, 1, T // tk),
                    pl.BlockSpec((tk, tm), lambda i, j, k: (k, i)), pl.BlockSpec((tk, D), lambda i, j, k: (k, 0)),
                    [(_sds((D, D), BF16), pl.BlockSpec((tm, D), lambda i, j, k: (i, 0)))], (tm, D))
    (ds,) = _mm("conf_ds", d_y, w_cf["conf_w_pw2"], _NT, (T // tm, 1, 1),
                pl.BlockSpec((tm, D), lambda i, j, k: (i, 0)), pl.BlockSpec((D, D), lambda i, j, k: (0, 0)),
                [(_sds((T, D), F32), pl.BlockSpec((tm, D), lambda i, j, k: (i, 0)))], (tm, D))
    dep = on_later(ds)

    def ln_silu_bwd(ids, t, v):
        dsv, zcv = t
        nh, rstd = _layernorm_parts(zcv)
        ln = nh * v[0] + v[1]
        sg = _sigmoid(ln)
        d_ln = dsv * (sg * (1.0 + ln * (1.0 - sg)))
        d_nh = d_ln * v[0]
        d_zc = rstd * (d_nh - jnp.mean(d_nh, axis=-1, keepdims=True)
                       - nh * jnp.mean(d_nh * nh, axis=-1, keepdims=True))
        return [d_zc], [_sum0(d_ln * nh), _sum0(d_ln)]

    d_zc, d_lng, d_lnb = _tiled("conf_ln_bwd", ln_silu_bwd, (n_t,), [_rows(ds), _rows(zc)],
                                [ln_g, ln_b] + _behind(dep), [_orow(T, D, F32)], [(1, D), (1, D)])
    d_zg = _dwconv("conf_conv_dx", d_zc, 0, conf["conv_w"][::-1], jnp.zeros((1, D), F32),
                   CONF_KW - 1 - CONF_KW // 2, conf_starts, D, CW_CONF)
    d_cw_conf = _dwconv_wgrad("conf_conv_dw", d_zc, zg, 0, CONF_KW, CONF_KW // 2, conf_starts, D, CW_CONF)

    def glu_bwd(ids, t, v):
        dz, pa, pb = t
        sg = _sigmoid(pb)
        d_a = dz * sg
        d_b = dz * pa * sg * (1.0 - sg)
        return [jnp.concatenate([d_a, d_b], axis=1)], [_sum0(d_a), _sum0(d_b)]

    d_pre, d_b1a, d_b1b = _tiled(
        "conf_glu_bwd", glu_bwd, (n_t,), [_rows(d_zg), _rows(pre, D, col=0), _rows(pre, D, col=1)], [],
        [_orow(T, 2 * D, BF16)], [(1, D), (1, D)])
    (dw_pw1,) = _mm("conf_dwpw1", h1, d_pre, _TN, (D // tm, 4, T // tk),
                    pl.BlockSpec((tk, tm), lambda i, j, k: (k, i)),
                    pl.BlockSpec((tk, D // 2), lambda i, j, k: (k, j)),
                    [(_sds((4, D, D // 2), BF16), pl.BlockSpec((None, tm, D // 2), lambda i, j, k: (j, i, 0)))],
                    (tm, D // 2))
    dep = on_grads("conf", (dw_pw1, dw_pw2))
    (dh1,) = _mm("conf_dh", d_pre, w_cf["conf_w_pw1"], _NT, (T // tm, 1, 4),
                 pl.BlockSpec((tm, D // 2), lambda i, j, k: (i, k)),
                 pl.BlockSpec((None, D, D // 2), lambda i, j, k: (k, 0, 0)),
                 [(_sds((T, D), F32), pl.BlockSpec((tm, D), lambda i, j, k: (i, 0)))], (tm, D))
    dx2, d_sh1c, d_sc1c, d_g10 = _norm_bwd("conf", dx3, dh1, 0, x2, g10, m1[1], dep=dep)
    dep = on_later(dx2)

    dx1, dw_in0, dw_out0, dm_mlp0 = _mlp_bwd("mlp0", dx2, mlp0, g01, m0[4], m0[5],
                                             w_m0["w_in"], w_m0["w_out"], dep)
    dep = on_grads("mlp0", (dw_in0, dw_out0))
    d_orec, d_g1r, _ = _gate_bwd("rec", dx1, o_rec, m0[2], dep)
    (dw_rout,) = _mm("rec_dwout", m_rec, d_orec, _TN, (R // RH, 1, T // tk),
                     pl.BlockSpec((tk, RH), lambda i, j, k: (k, i)), pl.BlockSpec((tk, D), lambda i, j, k: (k, 0)),
                     [(_sds((R, D), BF16), pl.BlockSpec((RH, D), lambda i, j, k: (i, 0)))], (RH, D))
    (dm_rec,) = _mm("rec_dm", d_orec, w_rec["rec_w_out"], _NT, (T // tm, 1, 1),
                    pl.BlockSpec((tm, D), lambda i, j, k: (i, 0)), pl.BlockSpec((R, D), lambda i, j, k: (0, 0)),
                    [(_sds((T, R), F32), pl.BlockSpec((tm, R), lambda i, j, k: (i, 0)))], (tm, R))
    dep = on_later(dm_rec)

    def rec_mid_bwd(ids, t, v):
        dmv, gp, yf, yr = t
        g, th = _gelu(gp)
        lat = ids[0] > 0
        d_gp = jnp.where(lat, dmv * (yf + yr) * _gelu_grad(gp, th), 0.0)
        dy = jnp.where(lat, dmv * g, 0.0)
        return [d_gp, dy], []

    d_gp, dy = _tiled("rec_mid_bwd", rec_mid_bwd, (N_SCAN,),
                      [_rows(dm_rec, off=-1, clamp_lo=True), _rows(a_in, R), _rows(y_f), _rows(y_r)], _behind(dep),
                      [_orow(TA, R, BF16), _orow(TA, R, F32)])
    da_f, db_f, da_r, db_r = _scan_bwd(dy, a_f, y_f, hin_f, a_r, y_r, hin_r)
    d_gpre, d_u, d_gbias, d_lam = _tiled(
        "rg_bwd", _rg_bwd_fn, (TA // RG_TILE,), [_rows(a, tm=RG_TILE) for a in (u, da_f, db_f, da_r, db_r)],
        [wbd, gbias, lam], [_orow(TA, 2 * NQ, BF16, tm=RG_TILE), _orow(TA, R, F32, tm=RG_TILE)],
        [(1, 2 * NQ), (1, 2 * R)], vec_refs=True)
    tk_a = REC_TILE
    blk_mask, blk_spread = _block_mask(), _block_spread()
    (d_wbd,) = _mm("rg_dw", u, d_gpre, _TN, (2, 2, TA // tk_a),
                   pl.BlockSpec((tk_a, RH), lambda i, j, k: (k, i)),
                   pl.BlockSpec((tk_a, NQ // 2), lambda i, j, k: (k, 2 * i + j)),
                   [(_sds((2, 4, RH, BLK), F32), pl.BlockSpec((None, 2, RH, BLK), lambda i, j, k: (i, j, 0, 0)))],
                   (RH, NQ // 2),
                   extra=[(blk_mask, _full_spec(blk_mask)), (blk_spread, _full_spec(blk_spread))],
                   epi=lambda acc, ex: [jnp.stack([_fold_blocks(acc[:, s * RH:(s + 1) * RH], ex[0], ex[1])
                                                   for s in range(2)])])
    d_p = _dwconv("rec_conv_dx", d_u, 0, rec["conv_w"][::-1], jnp.zeros((1, R), F32), REC_KW - 1 - 1,
                  rec_starts, R, CW_REC)
    d_cw_rec = _dwconv_wgrad("rec_conv_dw", d_u, a_in, R // CW_REC, REC_KW, 1, rec_starts, R, CW_REC)
    d_a = jnp.concatenate([d_gp, d_p.astype(BF16)], axis=1)
    (dw_rin,) = _mm("rec_dwin", hcat, d_a, _TN, (D // tm, 4, TA // tk_a),
                    pl.BlockSpec((tk_a, tm), lambda i, j, k: (k, i)), pl.BlockSpec((tk_a, RH), lambda i, j, k: (k, j)),
                    [(_sds((4, D, RH), BF16), pl.BlockSpec((None, tm, RH), lambda i, j, k: (j, i, 0)))], (tm, RH))
    dep = on_grads("rec", (dw_rin, dw_rout))
    (dhcat,) = _mm("rec_dh", d_a, w_rec["rec_w_in"], _NT, (TA // tm_a, 1, 4),
                   pl.BlockSpec((tm_a, RH), lambda i, j, k: (i, k)),
                   pl.BlockSpec((None, D, RH), lambda i, j, k: (k, 0, 0)),
                   [(_sds((TA, D), F32), pl.BlockSpec((tm_a, D), lambda i, j, k: (i, 0)))], (tm_a, D))
    dx0, d_sh1r, d_sc1r, d_g00 = _norm_bwd("rec", dx1, dhcat, 1, x0, g00, m0[1], dep=dep)
    d_csh, d_csc, d_g00c = _norm_bwd("ctx", None, dhcat, 0, ctx, g00, csc, with_dx=False)

    big = dict(rec_w_in=dw_rin, rec_w_out=dw_rout, conf_w_pw1=dw_pw1, conf_w_pw2=dw_pw2,
               mlp_w_in=(dw_in0, dw_in1), mlp_w_out=(dw_out0, dw_out1))
    d_wa, d_wx = _gate_block_grads(d_wbd)
    d_ba, d_bx = _gate_bias_grads(d_gbias)
    d_mod = jnp.concatenate([
        d_sh1r, d_sc1r, d_g1r, dm_mlp0["sh"], dm_mlp0["sc"], dm_mlp0["gate"],
        d_sh1c, d_sc1c, d_g1c, dm_mlp1["sh"], dm_mlp1["sc"], dm_mlp1["gate"]], axis=1).reshape(2, 6 * D)
    small = dict(
        d_mod=d_mod, d_cmod=jnp.concatenate([d_csh, d_csc], axis=1),
        norm_g=jnp.concatenate([d_g00 + d_g00c, dm_mlp0["g_norm"], d_g10, dm_mlp1["g_norm"]], axis=1),
        rec_conv_w=d_cw_rec[:REC_KW], rec_conv_b=d_cw_rec[REC_KW], rec_lambda=d_lam.reshape(2, R),
        rec_w_a=d_wa, rec_b_a=d_ba, rec_w_x=d_wx, rec_b_x=d_bx,
        conf_b_pw1=jnp.concatenate([d_b1a, d_b1b], axis=1), conf_conv_w=d_cw_conf[:CONF_KW],
        conf_conv_b=d_cw_conf[CONF_KW], conf_ln_g=d_lng, conf_ln_b=d_lnb, conf_b_pw2=d_bpw2, final_g=d_fg)
    return loss.reshape(()), dx0, big, small


_BIG = ("rec_w_in", "rec_w_out", "conf_w_pw1", "conf_w_pw2", "mlp_w_in", "mlp_w_out")


def _halves(w):
    return w.reshape(2, w.shape[0] // 2, w.shape[1])


def _ada_fwd(c16, w_ada, b_shard):
    ns = w_ada.shape[2]
    tn = 512

    def kern(c_ref, w_ref, b_ref, o_ref):
        cv = c_ref[...]
        s = (cv * _sigmoid(cv)).astype(BF16)
        o_ref[...] = jnp.dot(s, w_ref[...].astype(BF16), preferred_element_type=F32) + b_ref[...]

    return _pcall(
        kern, name="ada_fwd", grid=(2, ns // tn),
        in_specs=[pl.BlockSpec((16, D), lambda l, j: (0, 0)), pl.BlockSpec((None, D, tn), lambda l, j: (l, 0, j)),
                  pl.BlockSpec((None, 1, tn), lambda l, j: (l, 0, j))],
        out_specs=pl.BlockSpec((None, 16, tn), lambda l, j: (l, 0, j)),
        out_shape=_sds((2, 16, ns), F32), compiler_params=_cparams(),
    )(c16, w_ada, b_shard)


def _ada_bwd(c16, dm16, w_ada):
    ns = w_ada.shape[2]
    tn = 512

    def kern(c_ref, dm_ref, w_ref, gw_ref, ds_ref):
        cv = c_ref[...]
        s = (cv * _sigmoid(cv)).astype(BF16)
        dm = dm_ref[...].astype(BF16)
        gw_ref[...] = lax.dot_general(s, dm, _TN, preferred_element_type=F32)

        @pl.when(jnp.logical_and(pl.program_id(0) == 0, pl.program_id(1) == 0))
        def _():
            ds_ref[...] = jnp.zeros_like(ds_ref)

        ds_ref[...] += lax.dot_general(dm, w_ref[...].astype(BF16), _NT, preferred_element_type=F32)

    return _pcall(
        kern, name="ada_bwd", grid=(2, ns // tn),
        in_specs=[pl.BlockSpec((16, D), lambda l, j: (0, 0)), pl.BlockSpec((None, 16, tn), lambda l, j: (l, 0, j)),
                  pl.BlockSpec((None, D, tn), lambda l, j: (l, 0, j))],
        out_specs=[pl.BlockSpec((None, D, tn), lambda l, j: (l, 0, j)), pl.BlockSpec((16, D), lambda l, j: (0, 0))],
        out_shape=[_sds((2, D, ns), F32), _sds((16, D), F32)], compiler_params=_cparams(),
    )(c16, dm16, w_ada)


def _cctx_grad(ds8, c_ctx):
    def kern(d_ref, c_ref, o_ref):
        tot = d_ref[0, 8:9, :] + d_ref[2, 8:9, :] + d_ref[4, 8:9, :] + d_ref[6, 8:9, :]
        cv = c_ref[...]
        sg = _sigmoid(cv)
        o_ref[...] = tot * (sg * (1.0 + cv * (1.0 - sg)))

    return _pcall(kern, name="cctx_grad", out_shape=_sds((1, D), F32))(ds8, c_ctx.reshape(1, D))


def kernel(x, c, ctx, c_ctx, w_ada, b_ada, norm_g, rec_w_in, rec_conv_w, rec_conv_b, rec_lambda, rec_w_a, rec_b_a, rec_w_x, rec_b_x, rec_w_out, conf_w_pw1, conf_b_pw1, conf_conv_w, conf_conv_b, conf_ln_g, conf_ln_b, conf_w_pw2, conf_b_pw2, mlp_w_in, mlp_w_out, final_g, loss_target, m_c_ctx, m_w_ada, m_b_ada, m_norm_g, m_rec_w_in, m_rec_conv_w, m_rec_conv_b, m_rec_lambda, m_rec_w_a, m_rec_b_a, m_rec_w_x, m_rec_b_x, m_rec_w_out, m_conf_w_pw1, m_conf_b_pw1, m_conf_conv_w, m_conf_conv_b, m_conf_ln_g, m_conf_ln_b, m_conf_w_pw2, m_conf_b_pw2, m_mlp_w_in, m_mlp_w_out, m_final_g, v_c_ctx, v_w_ada, v_b_ada, v_norm_g, v_rec_w_in, v_rec_conv_w, v_rec_conv_b, v_rec_lambda, v_rec_w_a, v_rec_b_a, v_rec_w_x, v_rec_b_x, v_rec_w_out, v_conf_w_pw1, v_conf_b_pw1, v_conf_conv_w, v_conf_conv_b, v_conf_ln_g, v_conf_ln_b, v_conf_w_pw2, v_conf_b_pw2, v_mlp_w_in, v_mlp_w_out, v_final_g):
    names = ["c_ctx", "w_ada", "b_ada", "norm_g", "rec_w_in", "rec_conv_w", "rec_conv_b", "rec_lambda", "rec_w_a",
             "rec_b_a", "rec_w_x", "rec_b_x", "rec_w_out", "conf_w_pw1", "conf_b_pw1", "conf_conv_w", "conf_conv_b",
             "conf_ln_g", "conf_ln_b", "conf_w_pw2", "conf_b_pw2", "mlp_w_in", "mlp_w_out", "final_g"]
    w = dict(zip(names, [c_ctx, w_ada, b_ada, norm_g, rec_w_in, rec_conv_w, rec_conv_b, rec_lambda, rec_w_a,
                         rec_b_a, rec_w_x, rec_b_x, rec_w_out, conf_w_pw1, conf_b_pw1, conf_conv_w, conf_conv_b,
                         conf_ln_g, conf_ln_b, conf_w_pw2, conf_b_pw2, mlp_w_in, mlp_w_out, final_g]))
    m = dict(zip(names, [m_c_ctx, m_w_ada, m_b_ada, m_norm_g, m_rec_w_in, m_rec_conv_w, m_rec_conv_b, m_rec_lambda,
                         m_rec_w_a, m_rec_b_a, m_rec_w_x, m_rec_b_x, m_rec_w_out, m_conf_w_pw1, m_conf_b_pw1,
                         m_conf_conv_w, m_conf_conv_b, m_conf_ln_g, m_conf_ln_b, m_conf_w_pw2, m_conf_b_pw2,
                         m_mlp_w_in, m_mlp_w_out, m_final_g]))
    v = dict(zip(names, [v_c_ctx, v_w_ada, v_b_ada, v_norm_g, v_rec_w_in, v_rec_conv_w, v_rec_conv_b, v_rec_lambda,
                         v_rec_w_a, v_rec_b_a, v_rec_w_x, v_rec_b_x, v_rec_w_out, v_conf_w_pw1, v_conf_b_pw1,
                         v_conf_conv_w, v_conf_conv_b, v_conf_ln_g, v_conf_ln_b, v_conf_w_pw2, v_conf_b_pw2,
                         v_mlp_w_in, v_mlp_w_out, v_final_g]))
    mx, my, mc = _me()
    chip = 2 * mx + my
    me = 4 * mx + 2 * my + mc

    sharded_small = ["norm_g", "rec_conv_w", "rec_lambda", "conf_b_pw1", "conf_conv_w", "conf_conv_b", "conf_ln_g",
                     "conf_ln_b", "conf_b_pw2"]
    packed, offs = _pack([c] + [w[k] for k in sharded_small], 8)
    got = _allgather8("gather_small", packed)

    got_flat = got.reshape(8, -1)

    def piece(i):
        p, n, shape = offs[i]
        return got_flat[:, p:p + n].reshape((8,) + tuple(shape))

    c_rows = piece(0).reshape(8, D)
    full = {}
    for i, k in enumerate(sharded_small):
        per_chip = jnp.moveaxis(piece(1 + i)[0::2], 0, -2)
        full[k] = per_chip.reshape(per_chip.shape[:-2] + (4 * per_chip.shape[-1],))
    c16 = jnp.concatenate([c_rows, c_ctx.reshape(1, D), jnp.zeros((7, D), F32)], axis=0)

    ns = w_ada.shape[2]
    b_shard = lax.dynamic_slice_in_dim(b_ada, chip * ns, ns, axis=1).reshape(2, 1, ns)
    prod = _ada_fwd(c16, w_ada, b_shard).reshape(32, ns)

    mod_state, mod_started = _gather8_start("gather_mod_start", prod, got)
    place = jnp.stack([chip, mc]).astype(jnp.int32)
    shards = [_halves(rec_w_in[0]), _halves(rec_w_out[0]), _halves(conf_w_pw1[0]), _halves(conf_w_pw2[0]),
              _halves(mlp_w_in[0]), _halves(mlp_w_in[1]), _halves(mlp_w_out[0]), _halves(mlp_w_out[1])]
    use_order = dict(rec=(0, 1), mlp0=(4, 6), conf=(2, 3), mlp1=(5, 7))
    slots = _place_big(shards, place)
    flying, gsems, swapping = {}, {}, {}
    fly, sems, rec_started = _gather_start("gather_start_rec", [slots[t] for t in use_order["rec"]], ((0, 1),),
                                           mod_started)
    flying["rec"], gsems["rec"] = fly, sems
    later = ("mlp0", "conf", "mlp1")
    fly, sems, all_started = _gather_start("gather_start_rest", [slots[t] for g in later for t in use_order[g]],
                                           ((0, 1), (2, 3), (4, 5)), rec_started)
    for gi, g in enumerate(later):
        flying[g], gsems[g] = fly[2 * gi:2 * gi + 2], sems[2 * gi:2 * gi + 2]

    def wg_pre(group, after):
        bufs = _gather_wait(f"gather_wait_{group}", flying[group], *gsems[group], after)
        swapping[group], token = _swap_start(f"swap_start_{group}", bufs, after)
        return token

    def wg(group, after):
        if group in swapping:
            a, b = _swap_wait(f"swap_wait_{group}", *swapping[group], after)
        else:
            a, b = _swap_halves(f"swap_{group}",
                                _gather_wait(f"gather_wait_{group}", flying[group], *gsems[group], after))
        if group == "rec":
            return dict(rec_w_in=a.reshape(4, D, RH), rec_w_out=b.reshape(R, D))
        if group == "conf":
            return dict(conf_w_pw1=a.reshape(4, D, D // 2), conf_w_pw2=b.reshape(D, D))
        return dict(w_in=a.reshape(4, D, D), w_out=b.reshape(FF, D))

    prod, prod8 = _gather8_wait("gather_mod_wait", *mod_state, all_started)
    prod8 = lax.dynamic_update_slice(prod8, prod[None], (me, 0, 0)).reshape(8, 2, 16, ns)
    mod_all = jnp.concatenate([prod8[2 * j] for j in range(4)], axis=-1)
    mods = lax.dynamic_index_in_dim(mod_all, me, axis=1, keepdims=False).reshape(2, 6, D)
    cmods = mod_all[0, 8].reshape(6, D)[:2]

    rec = dict(conv_w=full["rec_conv_w"][0], conv_b=rec_conv_b[0], lam=full["rec_lambda"][0],
               w_a=rec_w_a[0], b_a=rec_b_a[0], w_x=rec_w_x[0], b_x=rec_b_x[0])
    conf = dict(b_pw1=full["conf_b_pw1"][0], conv_w=full["conf_conv_w"][0], conv_b=full["conf_conv_b"][0],
                ln_g=full["conf_ln_g"][0], ln_b=full["conf_ln_b"][0], b_pw2=full["conf_b_pw2"][0])
    pairing, sent = {}, {}

    def on_grads(group, dws):
        parts = [dw.reshape(4, 2, shards[t].shape[1], shards[t].shape[2]) for dw, t in zip(dws, use_order[group])]
        pairing[group], token = _reduce_begin(group, parts, place)
        return token

    def finish_pair(after):
        (group, state), = pairing.items()
        pairing.clear()
        sent[group], token = _reduce_mid(group, state, place, after)
        return token

    loss_local, grad_x, _, small = _local_step(x[0], ctx[0], loss_target[0], mods, cmods, full["norm_g"], final_g,
                                               rec, conf, wg, on_grads, wg_pre, finish_pair)
    finish_pair(grad_x)
    small["loss"] = loss_local.reshape(1)

    small_names = ["loss", "d_mod", "d_cmod", "norm_g", "rec_conv_w", "rec_conv_b", "rec_lambda", "rec_w_a", "rec_b_a",
                   "rec_w_x", "rec_b_x", "conf_b_pw1", "conf_conv_w", "conf_conv_b", "conf_ln_g", "conf_ln_b",
                   "conf_b_pw2", "final_g"]
    mine = lax.broadcasted_iota(jnp.int32, (8, 1), 0) == me
    mod_slots = jnp.where(mine, small["d_mod"].reshape(1, -1), 0.0)
    spacked, soffs = _pack([small[k] for k in small_names] + [mod_slots])
    small_state, small_started = _allreduce_small_begin(spacked, place)

    fulls = {}
    for group in ("mlp1", "conf", "mlp0", "rec"):
        for t, f in zip(use_order[group], _reduce_end(group, sent[group], place, small_started)):
            fulls[t] = f
    whole = _share_halves("share_grads", [fulls[t] for t in range(8)])
    g_big = dict(rec_w_in=whole[0].reshape(rec_w_in.shape), rec_w_out=whole[1].reshape(rec_w_out.shape),
                 conf_w_pw1=whole[2].reshape(conf_w_pw1.shape), conf_w_pw2=whole[3].reshape(conf_w_pw2.shape),
                 mlp_w_in=jnp.stack([whole[4].reshape(D, D), whole[5].reshape(D, D)]),
                 mlp_w_out=jnp.stack([whole[6].reshape(D, D), whole[7].reshape(D, D)]))
    delta, new_m, new_v = {}, {}, {}

    def adamw_of(k, g):
        cols = w[k].shape[-1]
        d_, m_, v_ = _adamw(f"adamw_{k}", w[k].reshape(-1, cols), g.reshape(-1, cols),
                            m[k].reshape(-1, cols), v[k].reshape(-1, cols))
        delta[k], new_m[k], new_v[k] = (a.reshape(w[k].shape) for a in (d_, m_, v_))

    for k in _BIG:
        adamw_of(k, g_big[k])

    unpacked = _unpack(_allreduce_small_end(small_state, new_v[_BIG[-1]]), soffs)
    ssum = dict(zip(small_names, unpacked[:-1]))
    loss = ssum["loss"].reshape(())
    dmod_rows = unpacked[-1].reshape(8, 2, 6 * D).transpose(1, 0, 2)

    d_cmod_full =jnp.concatenate([ssum["d_cmod"].reshape(1, 2 * D), jnp.zeros((1, 4 * D), F32)], axis=1)
    dm16 = jnp.concatenate([dmod_rows, jnp.stack([d_cmod_full, jnp.zeros((1, 6 * D), F32)]),
                            jnp.zeros((2, 7, 6 * D), F32)], axis=1)
    dm16_shard = lax.dynamic_slice_in_dim(dm16, chip * ns, ns, axis=2)
    g_w_ada, ds_part = _ada_bwd(c16, dm16_shard, w_ada)
    ds8 = _allgather8("gather_dsilu", ds_part)
    g_c_ctx = _cctx_grad(ds8, c_ctx).reshape(D)
    g_b_ada = ssum["d_mod"] + jnp.stack([d_cmod_full[0], jnp.zeros((6 * D,), F32)])

    def shard_of(a, axis):
        n = a.shape[axis] // 4
        return lax.dynamic_slice_in_dim(a, chip * n, n, axis=axis)

    grads = dict(
        c_ctx=g_c_ctx, w_ada=g_w_ada, b_ada=g_b_ada,
        norm_g=shard_of(ssum["norm_g"].reshape(2, 2, D), 2),
        rec_w_in=g_big["rec_w_in"], rec_conv_w=shard_of(ssum["rec_conv_w"].reshape(1, REC_KW, R), 2),
        rec_conv_b=ssum["rec_conv_b"].reshape(1, R), rec_lambda=shard_of(ssum["rec_lambda"].reshape(1, 2, R), 2),
        rec_w_a=ssum["rec_w_a"].reshape(rec_w_a.shape), rec_b_a=ssum["rec_b_a"].reshape(rec_b_a.shape),
        rec_w_x=ssum["rec_w_x"].reshape(rec_w_x.shape), rec_b_x=ssum["rec_b_x"].reshape(rec_b_x.shape),
        rec_w_out=g_big["rec_w_out"], conf_w_pw1=g_big["conf_w_pw1"],
        conf_b_pw1=shard_of(ssum["conf_b_pw1"].reshape(1, 2 * D), 1),
        conf_conv_w=shard_of(ssum["conf_conv_w"].reshape(1, CONF_KW, D), 2),
        conf_conv_b=shard_of(ssum["conf_conv_b"].reshape(1, D), 1),
        conf_ln_g=shard_of(ssum["conf_ln_g"].reshape(1, D), 1), conf_ln_b=shard_of(ssum["conf_ln_b"].reshape(1, D), 1),
        conf_w_pw2=g_big["conf_w_pw2"], conf_b_pw2=shard_of(ssum["conf_b_pw2"].reshape(1, D), 1),
        mlp_w_in=g_big["mlp_w_in"], mlp_w_out=g_big["mlp_w_out"], final_g=ssum["final_g"].reshape(D))

    adamw_of("w_ada", g_w_ada)
    rest = [k for k in names if k not in ("w_ada",) + _BIG]
    d_, m_, v_ = _adamw_many("adamw_small", [w[k] for k in rest], [grads[k] for k in rest],
                             [m[k] for k in rest], [v[k] for k in rest])
    for k, dd, mm, vv in zip(rest, d_, m_, v_):
        delta[k], new_m[k], new_v[k] = dd, mm, vv

    return (loss, grad_x[None], *[grads[k] for k in names], *[delta[k] for k in names],
            *[new_m[k] for k in names], *[new_v[k] for k in names])
```

```python
import functools
import math

import jax
import jax.numpy as jnp
from jax import lax
from jax.experimental import pallas as pl
from jax.experimental.pallas import tpu as pltpu

F32 = jnp.float32
BF16 = jnp.bfloat16

D = 1024
T = 2048
TC = 256
TA = T + TC
R = 1280
RH = R // 2
NQ = 4 * RH
FF = 4096
N_BLK = 16
BLK = R // N_BLK
GRID_W = 64
EPS = 1e-6
RG_C = 8.0
CONF_KW = 31
REC_KW = 4
LANE = 128
ROW_TILE = 256
HALO = 16
RG_TILE = 128
PACK_ROWS = 512
MM_TILE = 1024
REC_TILE = TA // 2
CW_REC = 640
CW_CONF = 512
V7X_VMEM_BYTES = 64 * 1024 * 1024
VMEM_LIMIT = V7X_VMEM_BYTES - 8 * 1024 * 1024

ADAM_LR = 0.001
ADAM_B1 = 0.9
ADAM_B2 = 0.999
ADAM_EPS = 1e-08
ADAM_WD = 0.01
ADAM_STEP = 10

MESH = pl.DeviceIdType.MESH
ANY = pl.BlockSpec(memory_space=pl.ANY)


def _sds(shape, dtype):
    return jax.ShapeDtypeStruct(tuple(shape), dtype)


def _pcall(body, **kw):
    return pl.pallas_call(body, **kw)


def _cparams():
    return pltpu.CompilerParams(vmem_limit_bytes=VMEM_LIMIT)


def _full_spec(arr):
    nd = arr.ndim
    return pl.BlockSpec(arr.shape, lambda *ids, _n=nd: (0,) * _n)


def _sum0(v):
    return jnp.sum(v, axis=0, keepdims=True)


def _tiled(name, fn, grid, ins, vecs, outs, vec_outs=(), vec_refs=False):
    n_in, n_vec, n_out = len(ins), len(vecs), len(outs)
    n_grid = len(grid)

    def kern(*refs):
        ids = [pl.program_id(a) for a in range(n_grid)]
        tin = [r[...] for r in refs[:n_in]]
        vin = list(refs[n_in:n_in + n_vec]) if vec_refs else [r[...] for r in refs[n_in:n_in + n_vec]]
        o_refs = refs[n_in + n_vec:n_in + n_vec + n_out]
        a_refs = refs[n_in + n_vec + n_out:]
        tout, incs = fn(ids, tin, vin)
        for r, v in zip(o_refs, tout):
            r[...] = v.astype(r.dtype)
        if a_refs:
            first = functools.reduce(jnp.logical_and, [i == 0 for i in ids])

            @pl.when(first)
            def _():
                for r in a_refs:
                    r[...] = jnp.zeros_like(r)

            for r, v in zip(a_refs, incs):
                r[...] += v

    out_shape = [o for o, _ in outs] + [_sds(s, F32) for s in vec_outs]
    out_specs = [s for _, s in outs] + [
        pl.BlockSpec(tuple(s), lambda *ids, _n=len(s): (0,) * _n) for s in vec_outs]
    res = _pcall(
        kern, name=name, grid=tuple(grid),
        in_specs=[s for _, s in ins] + [_full_spec(v) for v in vecs],
        out_specs=out_specs, out_shape=out_shape, compiler_params=_cparams(),
    )(*[a for a, _ in ins], *vecs)
    return list(res)


def _rows(arr, ncols=None, tm=ROW_TILE, off=0, col=0, clamp_lo=False):
    ncols = arr.shape[1] if ncols is None else ncols
    if clamp_lo:
        return arr, pl.BlockSpec((tm, ncols), lambda i: (jnp.maximum(i + off, 0), col))
    return arr, pl.BlockSpec((tm, ncols), lambda i: (i + off, col))


def _orow(nrows, ncols, dtype, tm=ROW_TILE, off=0, clamp_lo=False):
    if clamp_lo:
        return _sds((nrows, ncols), dtype), pl.BlockSpec((tm, ncols), lambda i: (jnp.maximum(i + off, 0), 0))
    return _sds((nrows, ncols), dtype), pl.BlockSpec((tm, ncols), lambda i: (i + off, 0))


_NN = (((1,), (0,)), ((), ()))
_TN = (((0,), (0,)), ((), ()))
_NT = (((1,), (1,)), ((), ()))


def _mm(name, a, b, dims, grid, a_spec, b_spec, out, acc_shape, extra=(), a_pre=None, epi=None):
    n_k = grid[2]
    n_ex = len(extra)

    def kern(a_ref, b_ref, *rest):
        ex = rest[:n_ex]
        o_refs = rest[n_ex:n_ex + len(out)]
        k = pl.program_id(2)
        av = a_ref[...]
        if a_pre is not None:
            av = a_pre(av)
        part = lax.dot_general(av.astype(BF16), b_ref[...].astype(BF16), dims, preferred_element_type=F32)

        def finish(total):
            vals = [total] if epi is None else epi(total, [e[...] for e in ex])
            for r, v in zip(o_refs, vals):
                r[...] = v.astype(r.dtype)

        if n_k == 1:
            finish(part)
        else:
            acc = rest[-1]

            @pl.when(k == 0)
            def _():
                acc[...] = part

            @pl.when(jnp.logical_and(k > 0, k < n_k - 1))
            def _():
                acc[...] += part

            @pl.when(k == n_k - 1)
            def _():
                finish(acc[...] + part)

    res = _pcall(
        kern, name=name, grid=tuple(grid),
        in_specs=[a_spec, b_spec] + [s for _, s in extra],
        out_specs=[s for _, s in out], out_shape=[o for o, _ in out],
        scratch_shapes=[] if n_k == 1 else [pltpu.VMEM(tuple(acc_shape), F32)], compiler_params=_cparams(),
    )(a, b, *[e for e, _ in extra])
    return list(res)


def _rms(x):
    r = lax.rsqrt(jnp.mean(x * x, axis=-1, keepdims=True) + EPS)
    return x * r, r


def _norm_mod(x, g, sc, sh):
    n, _ = _rms(x)
    return (n * g) * (1.0 + sc) + sh


def _norm_mod_bwd(dh, x, g, sc):
    n, r = _rms(x)
    d_sh = _sum0(dh)
    d_sc = _sum0(dh * (n * g))
    d_g = _sum0(dh * (1.0 + sc) * n)
    dn = dh * (g * (1.0 + sc))
    dx = r * (dn - n * jnp.mean(dn * n, axis=-1, keepdims=True))
    return dx, d_sh, d_sc, d_g


_GELU_K = math.sqrt(2.0 / math.pi)


def _gelu(x):
    t = jnp.tanh(_GELU_K * (x + 0.044715 * x * x * x))
    return 0.5 * x * (1.0 + t), t


def _gelu_grad(x, t):
    return 0.5 * (1.0 + t) + 0.5 * x * (1.0 - t * t) * (_GELU_K * (1.0 + 3.0 * 0.044715 * x * x))


def _sigmoid(x):
    return 0.5 * jnp.tanh(0.5 * x) + 0.5


def _expm1(x):
    p = jnp.full_like(x, 1.0 / 5040.0)
    for c in (1.0 / 720.0, 1.0 / 120.0, 1.0 / 24.0, 1.0 / 6.0, 0.5, 1.0):
        p = p * x + c
    return jnp.where(jnp.abs(x) < 0.3, x * p, jnp.exp(x) - 1.0)


def _softplus_neg(lam):
    return jnp.log1p(jnp.exp(-jnp.abs(lam))) + jnp.maximum(-lam, 0.0)


def _layernorm_parts(x):
    mu = jnp.mean(x, axis=-1, keepdims=True)
    xc = x - mu
    rstd = lax.rsqrt(jnp.mean(xc * xc, axis=-1, keepdims=True) + EPS)
    return xc * rstd, rstd


def _rg_gates(u, wbd, gbias, lam):
    sp = _softplus_neg(lam)
    parts = {}
    for h in range(2):
        uh = u[:, h * RH:(h + 1) * RH]
        g = jnp.dot(uh.astype(BF16), wbd[h], preferred_element_type=F32) + gbias[:, h * NQ:(h + 1) * NQ]
        for d in range(2):
            r = _sigmoid(g[:, (2 * d) * RH:(2 * d + 1) * RH])
            i = _sigmoid(g[:, (2 * d + 1) * RH:(2 * d + 2) * RH])
            sph = sp[d:d + 1, h * RH:(h + 1) * RH]
            la = (-RG_C) * r * sph
            e2 = _expm1(2.0 * la)
            inv_mult = jnp.where(e2 < 0.0, lax.rsqrt(-e2), 0.0)
            parts[(d, h)] = dict(r=r, i=i, la=la, a=jnp.exp(la), e2=e2, mult=-e2 * inv_mult, inv_mult=inv_mult,
                                 uh=uh, sp=sph)
    return parts


def _rg_fwd_fn(ids, tin, vin):
    (u,) = tin
    wbd = vin[0]
    parts = _rg_gates(u, wbd, vin[1][...], vin[2][...])
    outs = []
    for d in range(2):
        a = jnp.concatenate([parts[(d, h)]["a"] for h in range(2)], axis=1)
        b = jnp.concatenate([parts[(d, h)]["mult"] * parts[(d, h)]["i"] * parts[(d, h)]["uh"]
                             for h in range(2)], axis=1)
        outs += [a, b]
    return outs, []


def _rg_bwd_fn(ids, tin, vin):
    u, da_f, db_f, da_r, db_r = tin
    wbd, lam = vin[0], vin[2][...]
    parts = _rg_gates(u, wbd, vin[1][...], lam)
    dab = ((da_f, db_f), (da_r, db_r))
    dsig_lam = -1.0 / (1.0 + jnp.exp(lam))
    du_halves, dpre_halves, dlam = [], [], [[None, None], [None, None]]
    for h in range(2):
        du = jnp.zeros_like(parts[(0, h)]["uh"])
        dpre = []
        for d in range(2):
            p = parts[(d, h)]
            da = dab[d][0][:, h * RH:(h + 1) * RH]
            db = dab[d][1][:, h * RH:(h + 1) * RH]
            d_mult = db * p["i"] * p["uh"]
            d_i = db * p["mult"] * p["uh"]
            du = du + db * p["mult"] * p["i"]
            d_la = da * p["a"] - d_mult * (p["e2"] + 1.0) * p["inv_mult"]
            d_r = d_la * ((-RG_C) * p["sp"])
            dlam[d][h] = _sum0(d_la * ((-RG_C) * p["r"])) * dsig_lam[d:d + 1, h * RH:(h + 1) * RH]
            dpre += [d_r * p["r"] * (1.0 - p["r"]), d_i * p["i"] * (1.0 - p["i"])]
        dpre = jnp.concatenate(dpre, axis=1)
        du = du + lax.dot_general(dpre.astype(BF16), wbd[h], _NT, preferred_element_type=F32)
        du_halves.append(du)
        dpre_halves.append(dpre)
    dpre_all = jnp.concatenate(dpre_halves, axis=1)
    dlam_row = jnp.concatenate([dlam[0][0], dlam[0][1], dlam[1][0], dlam[1][1]], axis=1)
    return [dpre_all, jnp.concatenate(du_halves, axis=1)], [_sum0(dpre_all), dlam_row]


def _tile_flags(i, n_tiles, seq_starts):
    starts_here = functools.reduce(jnp.logical_or, [i == s for s in seq_starts])
    ends_here = functools.reduce(jnp.logical_or, [i + 1 == s for s in seq_starts] + [i + 1 == n_tiles])
    return jnp.logical_not(starts_here), jnp.logical_not(ends_here)


def _halo_specs(col0, cw):
    hb = ROW_TILE // HALO
    prev = pl.BlockSpec((HALO, cw), lambda i, c: (jnp.maximum(i * hb - 1, 0), col0 + c))
    cur = pl.BlockSpec((ROW_TILE, cw), lambda i, c: (i, col0 + c))
    return prev, cur, hb


def _window(prev_ref, cur_ref, next_ref, has_prev, has_next):
    prev = jnp.where(has_prev, prev_ref[...], 0.0)
    nxt = jnp.where(has_next, next_ref[...], 0.0)
    return jnp.concatenate([prev, cur_ref[...], nxt], axis=0)


def _tap_reader(win):
    sub = 8
    n = win.shape[0]
    shifted = {0: win}

    def tap(off):
        s = off % sub
        if s not in shifted:
            shifted[s] = pltpu.roll(win, n - s, axis=0)
        return shifted[s][off - s:off - s + ROW_TILE, :]

    return tap


def _dwconv(name, x, col0, w, bias, pad_left, seq_starts, n_ch, cw=256):
    n_rows = x.shape[0]
    n_tiles = n_rows // ROW_TILE
    n_taps = w.shape[0]
    prev_spec, cur_spec, hb = _halo_specs(col0, cw)
    last_hb = n_rows // HALO - 1
    next_spec = pl.BlockSpec((HALO, cw), lambda i, c: (jnp.minimum((i + 1) * hb, last_hb), col0 + c))

    def kern(prev_ref, cur_ref, next_ref, w_ref, b_ref, o_ref):
        has_prev, has_next = _tile_flags(pl.program_id(0), n_tiles, seq_starts)
        win = _window(prev_ref, cur_ref, next_ref, has_prev, has_next)
        tap = _tap_reader(win)
        wv = w_ref[...]
        acc = jnp.zeros((ROW_TILE, cw), F32) + b_ref[...]
        for k in range(n_taps):
            acc = acc + wv[k:k + 1, :] * tap(HALO + k - pad_left)
        o_ref[...] = acc

    return _pcall(
        kern, name=name, grid=(n_tiles, n_ch // cw),
        in_specs=[prev_spec, cur_spec, next_spec,
                  pl.BlockSpec((n_taps, cw), lambda i, c: (0, c)), pl.BlockSpec((1, cw), lambda i, c: (0, c))],
        out_specs=pl.BlockSpec((ROW_TILE, cw), lambda i, c: (i, c)),
        out_shape=_sds((n_rows, n_ch), F32), compiler_params=_cparams(),
    )(x, x, x, w, bias)


def _dwconv_wgrad(name, dy, x, col0, n_taps, pad_left, seq_starts, n_ch, cw=256):
    n_rows = dy.shape[0]
    n_tiles = n_rows // ROW_TILE
    n_out = -(-(n_taps + 1) // 8) * 8
    prev_spec, cur_spec, hb = _halo_specs(col0, cw)
    last_hb = n_rows // HALO - 1
    next_spec = pl.BlockSpec((HALO, cw), lambda c, i: (jnp.minimum((i + 1) * hb, last_hb), col0 + c))
    prev_spec = pl.BlockSpec((HALO, cw), lambda c, i: (jnp.maximum(i * hb - 1, 0), col0 + c))
    cur_spec = pl.BlockSpec((ROW_TILE, cw), lambda c, i: (i, col0 + c))

    def kern(dy_ref, prev_ref, cur_ref, next_ref, o_ref):
        i = pl.program_id(1)
        has_prev, has_next = _tile_flags(i, n_tiles, seq_starts)
        win = _window(prev_ref, cur_ref, next_ref, has_prev, has_next)
        dyv = dy_ref[...]
        tap = _tap_reader(win)
        rid = lax.broadcasted_iota(jnp.int32, (n_out, cw), 0)
        inc = jnp.where(rid == n_taps, _sum0(dyv), 0.0)
        for k in range(n_taps):
            inc = inc + jnp.where(rid == k, _sum0(dyv * tap(HALO + k - pad_left)), 0.0)

        @pl.when(i == 0)
        def _():
            o_ref[...] = jnp.zeros_like(o_ref)

        o_ref[...] += inc

    return _pcall(
        kern, name=name, grid=(n_ch // cw, n_tiles),
        in_specs=[pl.BlockSpec((ROW_TILE, cw), lambda c, i: (i, c)), prev_spec, cur_spec, next_spec],
        out_specs=pl.BlockSpec((n_out, cw), lambda c, i: (0, c)),
        out_shape=_sds((n_out, n_ch), F32), compiler_params=_cparams(),
    )(dy, x, x, x)


N_SCAN = TA // ROW_TILE


def _rev_block(j):
    return jnp.where(j == 0, 0, N_SCAN - j)


def _scan_fwd(a_f, b_f, a_r, b_r):
    fwd_spec = pl.BlockSpec((ROW_TILE, R), lambda i: (i, 0))
    rev_spec = pl.BlockSpec((ROW_TILE, R), lambda i: (_rev_block(i), 0))
    hin_spec = pl.BlockSpec((None, 1, R), lambda i: (i, 0, 0))

    def kern(af, bf, ar, br, yf, yr, hin_f, hin_r, hf_s, hr_s):
        @pl.when(pl.program_id(0) == 0)
        def _():
            hf_s[...] = jnp.zeros_like(hf_s)
            hr_s[...] = jnp.zeros_like(hr_s)

        hin_f[...] = hf_s[...]
        hin_r[...] = hr_s[...]

        def step(s8, carry):
            hf, hr = carry
            t0 = pl.multiple_of(s8 * 8, 8)
            for q in range(8):
                tf = t0 + q
                hf = af[pl.ds(tf, 1), :] * hf + bf[pl.ds(tf, 1), :]
                yf[pl.ds(tf, 1), :] = hf
                tr = ROW_TILE - 1 - tf
                hr = ar[pl.ds(tr, 1), :] * hr + br[pl.ds(tr, 1), :]
                yr[pl.ds(tr, 1), :] = hr
            return hf, hr

        hf, hr = lax.fori_loop(0, ROW_TILE // 8, step, (hf_s[...], hr_s[...]))
        hf_s[...] = hf
        hr_s[...] = hr

    return _pcall(
        kern, name="scan_fwd", grid=(N_SCAN,),
        in_specs=[fwd_spec, fwd_spec, rev_spec, rev_spec],
        out_specs=[fwd_spec, rev_spec, hin_spec, hin_spec],
        out_shape=[_sds((TA, R), F32), _sds((TA, R), F32), _sds((N_SCAN, 1, R), F32), _sds((N_SCAN, 1, R), F32)],
        scratch_shapes=[pltpu.VMEM((1, R), F32), pltpu.VMEM((1, R), F32)], compiler_params=_cparams(),
    )(a_f, b_f, a_r, b_r)


def _scan_bwd(dy, a_f, y_f, hin_f, a_r, y_r, hin_r):
    fwd_spec = pl.BlockSpec((ROW_TILE, R), lambda i: (N_SCAN - 1 - i, 0))
    rev_spec = pl.BlockSpec((ROW_TILE, R), lambda i: (_rev_block(N_SCAN - 1 - i), 0))
    hin_spec = pl.BlockSpec((None, 1, R), lambda i: (N_SCAN - 1 - i, 0, 0))
    last = ROW_TILE - 1

    def kern(dyf, af, yf, hf0, dyr, ar, yr, hr0, daf, dbf, dar, dbr, gf_s, anf_s, gr_s, anr_s):
        @pl.when(pl.program_id(0) == 0)
        def _():
            for r in (gf_s, anf_s, gr_s, anr_s):
                r[...] = jnp.zeros_like(r)

        def one(dy_ref, a_ref, y_ref, da_ref, db_ref, g, an, p, pprev):
            gnew = dy_ref[pl.ds(p, 1), :] + an * g
            db_ref[pl.ds(p, 1), :] = gnew
            da_ref[pl.ds(p, 1), :] = gnew * y_ref[pl.ds(pprev, 1), :]
            return gnew, a_ref[pl.ds(p, 1), :]

        def step(s8, carry):
            gf, anf, gr, anr = carry
            base = s8 * 8
            for q in range(8):
                s = last - (base + q)
                gf, anf = one(dyf, af, yf, daf, dbf, gf, anf, s, s - 1)
                gr, anr = one(dyr, ar, yr, dar, dbr, gr, anr, last - s, last - s + 1)
            return gf, anf, gr, anr

        carry = (gf_s[...], anf_s[...], gr_s[...], anr_s[...])
        carry = lax.fori_loop(0, ROW_TILE // 8 - 1, step, carry)
        gf, anf, gr, anr = carry
        for s in range(7, 0, -1):
            gf, anf = one(dyf, af, yf, daf, dbf, gf, anf, s, s - 1)
            gr, anr = one(dyr, ar, yr, dar, dbr, gr, anr, last - s, last - s + 1)
        gf0 = dyf[0:1, :] + anf * gf
        dbf[0:1, :] = gf0
        daf[0:1, :] = gf0 * hf0[...]
        gr0 = dyr[last:last + 1, :] + anr * gr
        dbr[last:last + 1, :] = gr0
        dar[last:last + 1, :] = gr0 * hr0[...]
        gf_s[...] = gf0
        anf_s[...] = af[0:1, :]
        gr_s[...] = gr0
        anr_s[...] = ar[last:last + 1, :]

    return _pcall(
        kern, name="scan_bwd", grid=(N_SCAN,),
        in_specs=[fwd_spec, fwd_spec, fwd_spec, hin_spec, rev_spec, rev_spec, rev_spec, hin_spec],
        out_specs=[fwd_spec, fwd_spec, rev_spec, rev_spec],
        out_shape=[_sds((TA, R), F32)] * 4,
        scratch_shapes=[pltpu.VMEM((1, R), F32)] * 4, compiler_params=_cparams(),
    )(dy, a_f, y_f, hin_f, dy, a_r, y_r, hin_r)


def _me():
    return lax.axis_index("x"), lax.axis_index("y"), lax.axis_index("c")


def _other_chips(mx, my):
    return [(1 - mx, my), (mx, 1 - my), (1 - mx, 1 - my)]


def _rcopy(src, dst, ssem, rsem, dev):
    return pltpu.make_async_remote_copy(src_ref=src, dst_ref=dst, send_sem=ssem, recv_sem=rsem,
                                        device_id=dev, device_id_type=MESH)


def _allgather8(name, x, dep=None):
    rows, cols = x.shape
    n_dep = len(_behind(dep))

    def kern(x_ref, *rest):
        o_ref, ssem, rsem, lsem = rest[n_dep:]
        mx, my, mc = _me()
        me = 4 * mx + 2 * my + mc
        peers = []
        for k in range(1, 8):
            px = 1 - mx if (k >> 2) & 1 else mx
            py = 1 - my if (k >> 1) & 1 else my
            pc = 1 - mc if k & 1 else mc
            peers.append((px, py, pc))
        mine = pltpu.make_async_copy(x_ref, o_ref.at[me], lsem)
        mine.start()
        sends = [_rcopy(x_ref, o_ref.at[me], ssem.at[k], rsem.at[k], p) for k, p in enumerate(peers)]
        for cp in sends:
            cp.start()
        for k, (px, py, pc) in enumerate(peers):
            _rcopy(x_ref, o_ref.at[4 * px + 2 * py + pc], ssem.at[k], rsem.at[k], (px, py, pc)).wait_recv()
        for cp in sends:
            cp.wait_send()
        mine.wait()

    return _pcall(
        kern, name=name, in_specs=[ANY] * (1 + n_dep), out_specs=ANY, out_shape=_sds((8, rows, cols), F32),
        scratch_shapes=[pltpu.SemaphoreType.DMA((7,)), pltpu.SemaphoreType.DMA((7,)), pltpu.SemaphoreType.DMA(())],
    )(x, *_behind(dep))


def _gather8_start(name, x, after):
    def kern(x_in, after_ref, x_ref, o_ref, ssem, rsem, token):
        mx, my, mc = _me()
        me = 4 * mx + 2 * my + mc
        for k, p in enumerate(_peers7(mx, my, mc)):
            _rcopy(x_ref, o_ref.at[me], ssem.at[k], rsem.at[k], p).start()
        token[...] = jnp.zeros_like(token)

    dma = pltpu.SemaphoreType.DMA
    res = _pcall(
        kern, name=name, in_specs=[ANY, ANY],
        out_specs=[ANY, ANY, SEM, SEM, pl.BlockSpec(memory_space=pltpu.VMEM)],
        out_shape=[_sds(x.shape, x.dtype), _sds((8,) + x.shape, x.dtype), dma((7,)), dma((7,)), _sds((8, LANE), F32)],
        input_output_aliases={0: 0}, compiler_params=pltpu.CompilerParams(has_side_effects=_DATAFLOW),
    )(x, after)
    return tuple(res[:4]), res[4]


def _gather8_wait(name, x, out, ssem, rsem, after):
    def kern(x_ref, o_ref, ssem_ref, rsem_ref, after_ref, x_out, o_out):
        mx, my, mc = _me()
        for k, (px, py, pc) in enumerate(_peers7(mx, my, mc)):
            cp = _rcopy(x_ref, o_ref.at[4 * px + 2 * py + pc], ssem_ref.at[k], rsem_ref.at[k], (px, py, pc))
            cp.wait_recv()
            cp.wait_send()

    res = _pcall(
        kern, name=name, in_specs=[ANY, ANY, SEM, SEM, ANY], out_specs=[ANY, ANY],
        out_shape=[_sds(x.shape, x.dtype), _sds(out.shape, out.dtype)], input_output_aliases={0: 0, 1: 1},
        compiler_params=pltpu.CompilerParams(has_side_effects=_DATAFLOW),
    )(x, out, ssem, rsem, after)
    return res[0], res[1]


def _peers7(mx, my, mc):
    peers = []
    for k in range(1, 8):
        peers.append((1 - mx if (k >> 2) & 1 else mx, 1 - my if (k >> 1) & 1 else my, 1 - mc if k & 1 else mc))
    return peers


def _share_halves(name, fulls):
    n = len(fulls)

    def kern(*refs):
        o = refs[n:2 * n]
        ss, rs = refs[2 * n:]
        mx, my, mc = _me()
        sib = (mx, my, 1 - mc)
        sends = []
        for t in range(n):
            cp = _rcopy(o[t].at[mc], o[t].at[mc], ss.at[t], rs.at[t], sib)
            cp.start()
            sends.append(cp)
        for t in range(n):
            _rcopy(o[t].at[1 - mc], o[t].at[1 - mc], ss.at[t], rs.at[t], sib).wait_recv()
        for cp in sends:
            cp.wait_send()

    dma = pltpu.SemaphoreType.DMA
    return _pcall(
        kern, name=name, in_specs=[ANY] * n, out_specs=[ANY] * n,
        out_shape=[_sds(f.shape, f.dtype) for f in fulls], input_output_aliases={t: t for t in range(n)},
        scratch_shapes=[dma((n,)), dma((n,))],
    )(*fulls)


def _tiled_sp(name, fn, grid, sp, ins, outs):
    n_in = len(ins)

    def kern(sp_ref, *refs):
        tout = fn([r[...] for r in refs[:n_in]])
        for r, v in zip(refs[n_in:], tout):
            r[...] = v.astype(r.dtype)

    gs = pltpu.PrefetchScalarGridSpec(num_scalar_prefetch=1, grid=tuple(grid),
                                      in_specs=[s for _, s in ins], out_specs=[s for _, s in outs])
    res = _pcall(kern, name=name, grid_spec=gs, out_shape=[o for o, _ in outs], compiler_params=_cparams(),
                 )(sp, *[a for a, _ in ins])
    return list(res)


def _row_tile(rows, cols, itemsize=4, budget=2 * 1024 * 1024):
    tr = rows
    while tr * cols * itemsize > budget and tr % 32 == 0:
        tr //= 2
    return tr


def _place_big(shards, place):
    slots = []
    for t, s in enumerate(shards):
        rr, cc = s.shape[1], s.shape[2]
        tr = _row_tile(rr, cc)
        (slot,) = _tiled_sp(
            f"place{t}", lambda tin: [tin[0]], (2, rr // tr), place,
            [(s, pl.BlockSpec((None, tr, cc), lambda h, i, sp: (h, i, 0)))],
            [(_sds((4, 2, rr, cc), BF16), pl.BlockSpec((None, None, tr, cc), lambda h, i, sp: (sp[0], h, i, 0)))])
        slots.append(slot)
    return slots


def _allreduce_small_begin(vec, place, after):
    hr = vec.shape[0] // 2
    tr = _row_tile(hr, LANE)
    blk = (None, None, tr, LANE)
    (pair,) = _tiled_sp(
        "small_place", lambda tin: [tin[0]], (2, hr // tr), place,
        [(vec.reshape(2, hr, LANE), pl.BlockSpec((None, tr, LANE), lambda h, i, sp: (h, i, 0)))],
        [(_sds((2, 2, hr, LANE), F32), pl.BlockSpec(blk, lambda h, i, sp: (sp[1], h, i, 0)))])
    (pair,) = _share_halves("small_share", [pair])
    (slot,) = _tiled_sp(
        "small_pair_add", lambda tin: [tin[0] + tin[1]], (2, hr // tr), place,
        [(pair, pl.BlockSpec(blk, lambda h, i, sp: (0, h, i, 0))),
         (pair, pl.BlockSpec(blk, lambda h, i, sp: (1, h, i, 0)))],
        [(_sds((4, 2, hr, LANE), F32), pl.BlockSpec(blk, lambda h, i, sp: (sp[0], h, i, 0)))])
    fly, sems, token = _gather_start("small_start", [slot], ((0,),), after)
    return (fly, sems), token


def _allreduce_small_end(state, after):
    fly, sems = state
    (chips,) = _swap_halves("small_swap", _gather_wait("small_wait", fly, *sems, after))
    hr = chips.shape[2]
    tr = _row_tile(hr, LANE)
    blk = (None, None, tr, LANE)
    (total,) = _tiled(
        "small_chip_sum", lambda ids, tin, vin: ([((tin[0] + tin[1]) + tin[2]) + tin[3]], []), (2, hr // tr),
        [(chips, pl.BlockSpec(blk, lambda h, i, _j=j: (_j, h, i, 0))) for j in range(4)], [],
        [(_sds((2, hr, LANE), F32), pl.BlockSpec((None, tr, LANE), lambda h, i: (h, i, 0)))])
    return total.reshape(2 * hr, LANE)


SEM =pl.BlockSpec(memory_space=pltpu.SEMAPHORE)
_DATAFLOW = pltpu.SideEffectType.DATAFLOW_SIDE_EFFECTING


def _gather_start(name, slots, groups, after):
    n = len(slots)

    def kern(*refs):
        o = refs[n + 1:2 * n + 1]
        sems, token = refs[2 * n + 1:-1], refs[-1]
        mx, my, mc = _me()
        j0 = 2 * mx + my
        for gi, grp in enumerate(groups):
            for k, t in enumerate(grp):
                for q, (qx, qy) in enumerate(_other_chips(mx, my)):
                    _rcopy(o[t].at[j0, mc], o[t].at[j0, mc], sems[2 * gi].at[3 * k + q],
                           sems[2 * gi + 1].at[3 * k + q], (qx, qy, mc)).start()
        token[...] = jnp.zeros_like(token)

    sem_shapes = []
    for grp in groups:
        sem_shapes += [pltpu.SemaphoreType.DMA((3 * len(grp),))] * 2
    res = _pcall(
        kern, name=name, in_specs=[ANY] * (n + 1),
        out_specs=[ANY] * n + [SEM] * len(sem_shapes) + [pl.BlockSpec(memory_space=pltpu.VMEM)],
        out_shape=[_sds(w.shape, w.dtype) for w in slots] + sem_shapes + [_sds((8, LANE), F32)],
        input_output_aliases={t: t for t in range(n)},
        compiler_params=pltpu.CompilerParams(has_side_effects=_DATAFLOW),
    )(*slots, after)
    return list(res[:n]), list(res[n:-1]), res[-1]


def _gather_wait(name, bufs, ssem, rsem, after):
    n = len(bufs)

    def kern(*refs):
        b = refs[:n]
        ssem_ref, rsem_ref = refs[n], refs[n + 1]
        mx, my, mc = _me()
        j0 = 2 * mx + my
        for k in range(n):
            for q, (qx, qy) in enumerate(_other_chips(mx, my)):
                jq = 2 * qx + qy
                _rcopy(b[k].at[jq, mc], b[k].at[jq, mc], ssem_ref.at[3 * k + q], rsem_ref.at[3 * k + q],
                       (qx, qy, mc)).wait_recv()
                _rcopy(b[k].at[j0, mc], b[k].at[j0, mc], ssem_ref.at[3 * k + q], rsem_ref.at[3 * k + q],
                       (qx, qy, mc)).wait_send()

    return list(_pcall(
        kern, name=name, in_specs=[ANY] * n + [SEM, SEM, ANY], out_specs=[ANY] * n,
        out_shape=[_sds(w.shape, w.dtype) for w in bufs], input_output_aliases={k: k for k in range(n)},
        compiler_params=pltpu.CompilerParams(has_side_effects=_DATAFLOW),
    )(*bufs, ssem, rsem, after))


def _swap_halves(name, bufs):
    n = len(bufs)

    def kern(*refs):
        o = refs[n:2 * n]
        ss, rs = refs[2 * n:]
        mx, my, mc = _me()
        sib = (mx, my, 1 - mc)
        sends = []
        for k in range(n):
            for q, (qx, qy) in enumerate(_other_chips(mx, my)):
                jq = 2 * qx + qy
                cp = _rcopy(o[k].at[jq, mc], o[k].at[jq, mc], ss.at[3 * k + q], rs.at[3 * k + q], sib)
                cp.start()
                sends.append(cp)
        for k in range(n):
            for q, (qx, qy) in enumerate(_other_chips(mx, my)):
                jq = 2 * qx + qy
                _rcopy(o[k].at[jq, 1 - mc], o[k].at[jq, 1 - mc], ss.at[3 * k + q], rs.at[3 * k + q], sib).wait_recv()
        for cp in sends:
            cp.wait_send()

    dma = pltpu.SemaphoreType.DMA
    return list(_pcall(
        kern, name=name, in_specs=[ANY] * n, out_specs=[ANY] * n,
        out_shape=[_sds(w.shape, w.dtype) for w in bufs], input_output_aliases={k: k for k in range(n)},
        scratch_shapes=[dma((3 * n,)), dma((3 * n,))],
    )(*bufs))


def _swap_start(name, bufs, after):
    n = len(bufs)

    def kern(*refs):
        o = refs[n + 1:2 * n + 1]
        ssem, rsem, token = refs[2 * n + 1:]
        mx, my, mc = _me()
        for k in range(n):
            for q, (qx, qy) in enumerate(_other_chips(mx, my)):
                jq = 2 * qx + qy
                _rcopy(o[k].at[jq, mc], o[k].at[jq, mc], ssem.at[3 * k + q], rsem.at[3 * k + q], (mx, my, 1 - mc)).start()
        token[...] = jnp.zeros_like(token)

    dma = pltpu.SemaphoreType.DMA
    res = _pcall(
        kern, name=name, in_specs=[ANY] * (n + 1),
        out_specs=[ANY] * n + [SEM, SEM, pl.BlockSpec(memory_space=pltpu.VMEM)],
        out_shape=[_sds(w.shape, w.dtype) for w in bufs] + [dma((3 * n,)), dma((3 * n,)), _sds((8, LANE), F32)],
        input_output_aliases={k: k for k in range(n)},
        compiler_params=pltpu.CompilerParams(has_side_effects=_DATAFLOW),
    )(*bufs, after)
    return (list(res[:n]), res[n], res[n + 1]), res[n + 2]


def _swap_wait(name, bufs, ssem, rsem, after):
    n = len(bufs)

    def kern(*refs):
        b = refs[:n]
        ssem_ref, rsem_ref = refs[n], refs[n + 1]
        mx, my, mc = _me()
        sib = (mx, my, 1 - mc)
        for k in range(n):
            for q, (qx, qy) in enumerate(_other_chips(mx, my)):
                jq = 2 * qx + qy
                _rcopy(b[k].at[jq, 1 - mc], b[k].at[jq, 1 - mc], ssem_ref.at[3 * k + q], rsem_ref.at[3 * k + q],
                       sib).wait_recv()
                _rcopy(b[k].at[jq, mc], b[k].at[jq, mc], ssem_ref.at[3 * k + q], rsem_ref.at[3 * k + q],
                       sib).wait_send()

    return list(_pcall(
        kern, name=name, in_specs=[ANY] * n + [SEM, SEM, ANY], out_specs=[ANY] * n,
        out_shape=[_sds(w.shape, w.dtype) for w in bufs], input_output_aliases={k: k for k in range(n)},
        compiler_params=pltpu.CompilerParams(has_side_effects=_DATAFLOW),
    )(*bufs, ssem, rsem, after))


def _to_sibling(mx, my, mc):
    return [((j, 1 - mc), j, (mx, my, 1 - mc)) for j in range(4)]


def _to_chips(mx, my, mc):
    return [((2 * qx + qy,), q, (qx, qy, mc)) for q, (qx, qy) in enumerate(_other_chips(mx, my))]


def _send_start(name, srcs, plan, land_shapes, after):
    n = len(srcs)
    per = len(plan(0, 0, 0))

    def kern(*refs):
        s, land = refs[n + 1:2 * n + 1], refs[2 * n + 1:3 * n + 1]
        ssem, rsem, token = refs[3 * n + 1:]
        for k in range(n):
            for q, (idx, slot, dev) in enumerate(plan(*_me())):
                _rcopy(s[k].at[idx], land[k].at[slot], ssem.at[per * k + q], rsem.at[per * k + q], dev).start()
        token[...] = jnp.zeros_like(token)

    dma = pltpu.SemaphoreType.DMA
    res = _pcall(
        kern, name=name, in_specs=[ANY] * (n + 1),
        out_specs=[ANY] * (2 * n) + [SEM, SEM, pl.BlockSpec(memory_space=pltpu.VMEM)],
        out_shape=[_sds(s.shape, s.dtype) for s in srcs] + [_sds(ls, s.dtype) for ls, s in zip(land_shapes, srcs)]
        + [dma((per * n,)), dma((per * n,)), _sds((8, LANE), F32)],
        input_output_aliases={k: k for k in range(n)},
        compiler_params=pltpu.CompilerParams(has_side_effects=_DATAFLOW),
    )(*srcs, after)
    return (list(res[:n]), list(res[n:2 * n]), res[2 * n], res[2 * n + 1]), res[2 * n + 2]


def _send_wait(name, srcs, lands, ssem, rsem, plan, after):
    n = len(srcs)
    per = len(plan(0, 0, 0))

    def kern(*refs):
        s, land = refs[:n], refs[n:2 * n]
        ssem_ref, rsem_ref = refs[2 * n], refs[2 * n + 1]
        for k in range(n):
            for q, (idx, slot, dev) in enumerate(plan(*_me())):
                cp = _rcopy(s[k].at[idx], land[k].at[slot], ssem_ref.at[per * k + q], rsem_ref.at[per * k + q], dev)
                cp.wait_recv()
                cp.wait_send()

    res = _pcall(
        kern, name=name, in_specs=[ANY] * (2 * n) + [SEM, SEM, ANY], out_specs=[ANY] * (2 * n),
        out_shape=[_sds(a.shape, a.dtype) for a in list(srcs) + list(lands)],
        input_output_aliases={k: k for k in range(2 * n)},
        compiler_params=pltpu.CompilerParams(has_side_effects=_DATAFLOW),
    )(*srcs, *lands, ssem, rsem, after)
    return list(res[:n]), list(res[n:])


def _reduce_begin(tag, parts, after):
    return _send_start(f"pair_start_{tag}", parts, _to_sibling, [(4,) + p.shape[2:] for p in parts], after)


def _reduce_mid(tag, pairing, place, after):
    parts, theirs = _send_wait(f"pair_wait_{tag}", *pairing, _to_sibling, after)
    sums = []
    for k, (p, o) in enumerate(zip(parts, theirs)):
        rr, cc = p.shape[2], p.shape[3]
        tr = _row_tile(rr, cc)
        (s_k,) = _tiled_sp(
            f"pair_add_{tag}{k}", lambda tin: [tin[0].astype(F32) + tin[1].astype(F32)], (4, rr // tr), place,
            [(p, pl.BlockSpec((None, None, tr, cc), lambda j, i, sp: (j, sp[1], i, 0))),
             (o, pl.BlockSpec((None, tr, cc), lambda j, i, sp: (j, i, 0)))],
            [(_sds((4, rr, cc), BF16), pl.BlockSpec((None, tr, cc), lambda j, i, sp: (j, i, 0)))])
        sums.append(s_k)
    return _send_start(f"chips_start_{tag}", sums, _to_chips, [(3,) + s.shape[1:] for s in sums], theirs[0])


def _reduce_end(tag, flying, place, after):
    sums, lands = _send_wait(f"chips_wait_{tag}", *flying, _to_chips, after)
    fulls = []
    for k, (s, q) in enumerate(zip(sums, lands)):
        rr, cc = q.shape[1], q.shape[2]
        tr = _row_tile(rr, cc)

        def add4(tin):
            return [((tin[0].astype(F32) + tin[1].astype(F32)) + tin[2].astype(F32)) + tin[3].astype(F32)]

        ins = [(s, pl.BlockSpec((None, tr, cc), lambda i, sp: (sp[0], i, 0)))]
        ins += [(q, pl.BlockSpec((None, tr, cc), lambda i, sp, _k=kk: (_k, i, 0))) for kk in range(3)]
        (f_k,) = _tiled_sp(f"chip_add_{tag}{k}", add4, (rr // tr,), place, ins,
                           [(_sds((2, rr, cc), F32), pl.BlockSpec((None, tr, cc), lambda i, sp: (sp[1], i, 0)))])
        fulls.append(f_k)
    return fulls


def _pack(parts, PACK_ROWS=PACK_ROWS):
    flat, offs, pos = [], [], 0
    for p in parts:
        v = p.reshape(-1).astype(F32)
        n = -(-v.shape[0] // LANE) * LANE
        flat.append(jnp.pad(v, (0, n - v.shape[0])))
        offs.append((pos, v.shape[0], p.shape))
        pos += n
    total = -(-pos // (PACK_ROWS * LANE)) * PACK_ROWS * LANE
    flat.append(jnp.zeros((total - pos,), F32))
    return jnp.concatenate(flat).reshape(-1, LANE), offs


def _unpack(vec, offs):
    v = vec.reshape(-1)
    return [v[p:p + n].reshape(shape) for p, n, shape in offs]


def _adamw_math(wv, gv, mv, vv):
    bc1 = 1.0 - ADAM_B1 ** ADAM_STEP
    bc2 = 1.0 - ADAM_B2 ** ADAM_STEP
    mn = ADAM_B1 * mv + (1.0 - ADAM_B1) * gv
    vn = ADAM_B2 * vv + (1.0 - ADAM_B2) * (gv * gv)
    delta = -ADAM_LR * ((mn / bc1) / (jnp.sqrt(vn / bc2) + ADAM_EPS) + ADAM_WD * wv)
    return delta, mn, vn


def _adamw(name, w, g, m, v):
    rows, cols = w.shape
    tr = rows
    for cand in (512, 256, 128, 64, 32, 16, 8):
        if rows % cand == 0 and cand * cols * 4 <= 2 * 1024 * 1024:
            tr = cand
            break

    def fn(ids, tin, vin):
        return list(_adamw_math(*tin)), []

    spec = pl.BlockSpec((tr, cols), lambda i: (i, 0))
    outs = [(_sds((rows, cols), F32), spec)] * 3
    return _tiled(name, fn, (rows // tr,), [(a, spec) for a in (w, g, m, v)], [], outs)


def _adamw_many(name, ws, gs, ms, vs):
    n = len(ws)
    views = [(-1, a.shape[-1]) if a.ndim > 1 else (1, -1) for a in ws]
    flat = lambda arrs: [a.reshape(vw) for a, vw in zip(arrs, views)]

    def kern(*refs):
        ins, outs = refs[:4 * n], refs[4 * n:]
        for t in range(n):
            res = _adamw_math(*[ins[q * n + t][...] for q in range(4)])
            for q in range(3):
                outs[q * n + t][...] = res[q]

    shapes = [_sds(a.shape, F32) for a in flat(ws)]
    res = _pcall(kern, name=name, out_shape=shapes * 3, compiler_params=_cparams(),
                 )(*flat(ws), *flat(gs), *flat(ms), *flat(vs))
    back = lambda part: [a.reshape(w.shape) for a, w in zip(part, ws)]
    return back(res[:n]), back(res[n:2 * n]), back(res[2 * n:])


def _pos_embed():
    n_rows = T // GRID_W
    q = D // 4
    omega = 1.0 / (10000.0 ** (jnp.arange(q, dtype=F32) / q))
    er = jnp.arange(n_rows, dtype=jnp.int32).astype(F32)[:, None] * omega[None, :]
    ec = jnp.arange(GRID_W, dtype=jnp.int32).astype(F32)[:, None] * omega[None, :]
    by_row = jnp.concatenate([jnp.sin(er), jnp.cos(er)], axis=-1)
    by_col = jnp.concatenate([jnp.sin(ec), jnp.cos(ec)], axis=-1)
    return jnp.concatenate([jnp.repeat(by_row, GRID_W, axis=0), jnp.tile(by_col, (n_rows, 1))], axis=-1)


def _dense_gates(w_a, w_x):
    rows = jnp.stack([w_a[0], w_x[0], w_a[1], w_x[1]]).reshape(4, 2, RH, BLK)
    mask, spread = _block_mask(), _block_spread().T.astype(BF16)

    def kern(r_ref, m_ref, s_ref, o_ref):
        tiled = jnp.dot(r_ref[...].astype(BF16), s_ref[...], preferred_element_type=F32)
        o_ref[...] = (tiled * m_ref[...]).astype(o_ref.dtype)

    return _pcall(
        kern, name="gates_dense", grid=(2, 4),
        in_specs=[pl.BlockSpec((None, None, RH, BLK), lambda h, q: (q, h, 0, 0)),
                  pl.BlockSpec((RH, RH), lambda h, q: (0, 0)), pl.BlockSpec((BLK, RH), lambda h, q: (0, 0))],
        out_specs=pl.BlockSpec((None, RH, RH), lambda h, q: (h, 0, q)),
        out_shape=_sds((2, RH, NQ), BF16),
    )(rows, mask, spread)


def _block_mask():
    r = lax.broadcasted_iota(jnp.int32, (RH, RH), 0) // BLK
    c = lax.broadcasted_iota(jnp.int32, (RH, RH), 1) // BLK
    return (r == c).astype(F32)


def _block_spread():
    c = lax.broadcasted_iota(jnp.int32, (RH, BLK), 0) % BLK
    j = lax.broadcasted_iota(jnp.int32, (RH, BLK), 1)
    return (c == j).astype(F32)


def _fold_blocks(dense, mask, spread):
    return jnp.dot(dense * mask, spread, preferred_element_type=F32, precision=lax.Precision.HIGHEST)


def _gate_block_grads(folded):
    per = N_BLK // 2
    kinds = [jnp.concatenate([folded[h, q].reshape(per, BLK, BLK) for h in range(2)], axis=0) for q in range(4)]
    return jnp.stack([kinds[0], kinds[2]]), jnp.stack([kinds[1], kinds[3]])


def _gate_bias_dense(b_a, b_x):
    cols = []
    for h in range(2):
        for src in (b_a[0], b_x[0], b_a[1], b_x[1]):
            cols.append(src.reshape(R)[h * RH:(h + 1) * RH])
    return jnp.concatenate(cols).reshape(1, 2 * NQ)


def _gate_bias_grads(dgb):
    v = dgb.reshape(2, 4, RH)
    kinds = [jnp.concatenate([v[0, q], v[1, q]]).reshape(N_BLK, BLK) for q in range(4)]
    return jnp.stack([kinds[0], kinds[2]]), jnp.stack([kinds[1], kinds[3]])


def _residual_epilogue(next_norm):
    def epi(acc, ex):
        x_new = ex[0] + ex[1] * acc
        outs = [acc, x_new]
        if next_norm:
            outs.append(_norm_mod(x_new, ex[-3], ex[-2], ex[-1]))
        return outs
    return epi


def _mlp_fwd(tag, x_in, h, gate, w_in, w_out, next_norm=None, dep=None):
    tm = MM_TILE
    (r,) = _mm(f"{tag}_in", h, w_in, _NN, (T // tm, 4, 1),
               pl.BlockSpec((tm, D), lambda i, j, k: (i, 0)), pl.BlockSpec((None, D, D), lambda i, j, k: (j, 0, 0)),
               [(_sds((T, FF), BF16), pl.BlockSpec((tm, D), lambda i, j, k: (i, j)))], (tm, D),
               extra=[(d_, _full_spec(d_)) for d_ in _behind(dep)], epi=lambda acc, ex: [jnp.maximum(acc, 0.0)])
    row_spec = pl.BlockSpec((tm, D), lambda i, j, k: (i, 0))
    outs = [(_sds((T, D), F32), row_spec)] * 2 + ([(_sds((T, D), BF16), row_spec)] if next_norm else [])
    res = _mm(f"{tag}_out", r, w_out, _NN, (T // tm, 1, FF // D),
              pl.BlockSpec((tm, D), lambda i, j, k: (i, k)), pl.BlockSpec((D, D), lambda i, j, k: (k, 0)),
              outs, (tm, D),
              extra=[(x_in, row_spec), (gate, _full_spec(gate))] + [(v, _full_spec(v)) for v in next_norm or ()],
              a_pre=lambda a: a * a, epi=_residual_epilogue(next_norm))
    return dict(h=h, r=r, o=res[0], x_in=x_in), res[1], (res[2] if next_norm else None)


def _behind(dep):
    return [] if dep is None else [dep]


def _gate_bwd(tag, dx, o, gate, dep=None):
    def fn(ids, t, v):
        d_o = t[0] * v[0]
        return [d_o], [_sum0(t[0] * t[1]), _sum0(d_o)]
    return _tiled(f"{tag}_gate_bwd", fn, (T // ROW_TILE,), [_rows(dx), _rows(o)], [gate] + _behind(dep),
                  [_orow(T, D, BF16)], [(1, D), (1, D)])


def _norm_bwd(tag, dx_res, dh, dh_off, x, g_norm, sc, with_dx=True, dep=None):
    n_t = x.shape[0] // ROW_TILE

    def fn(ids, t, v):
        if with_dx:
            dres, dhv, xv = t
        else:
            dhv, xv = t
        dxv, d_sh, d_sc, d_g = _norm_mod_bwd(dhv, xv, v[0], v[1])
        return ([dres + dxv] if with_dx else []), [d_sh, d_sc, d_g]

    ins = ([_rows(dx_res)] if with_dx else []) + [_rows(dh, off=dh_off), _rows(x)]
    outs = [_orow(x.shape[0], D, F32)] if with_dx else []
    return _tiled(f"{tag}_norm_bwd", fn, (n_t,), ins, [g_norm, sc] + _behind(dep), outs, [(1, D)] * 3)


def _mlp_bwd(tag, dx, saved, g_norm, sc, gate, w_in, w_out, dep=None):
    d_o, d_gate, _ = _gate_bwd(tag, dx, saved["o"], gate, dep)
    tm = MM_TILE
    r = saved["r"]
    (da,) = _mm(f"{tag}_dz", d_o, w_out, _NT, (T // tm, FF // D, 1),
                pl.BlockSpec((tm, D), lambda i, j, k: (i, 0)), pl.BlockSpec((D, D), lambda i, j, k: (j, 0)),
                [(_sds((T, FF), BF16), pl.BlockSpec((tm, D), lambda i, j, k: (i, j)))], (tm, D),
                extra=[(r, pl.BlockSpec((tm, D), lambda i, j, k: (i, j)))],
                epi=lambda acc, ex: [acc * (2.0 * ex[0].astype(F32))])
    tk = MM_TILE
    (dw_out,) = _mm(f"{tag}_dwout", r, d_o, _TN, (FF // tm, 1, T // tk),
                    pl.BlockSpec((tk, tm), lambda i, j, k: (k, i)), pl.BlockSpec((tk, D), lambda i, j, k: (k, 0)),
                    [(_sds((FF, D), BF16), pl.BlockSpec((tm, D), lambda i, j, k: (i, 0)))], (tm, D),
                    a_pre=lambda a: a * a)
    (dh,) = _mm(f"{tag}_dh", da, w_in, _NT, (T // tm, 1, 4),
                pl.BlockSpec((tm, D), lambda i, j, k: (i, k)), pl.BlockSpec((None, D, D), lambda i, j, k: (k, 0, 0)),
                [(_sds((T, D), F32), pl.BlockSpec((tm, D), lambda i, j, k: (i, 0)))], (tm, D))
    (dw_in,) = _mm(f"{tag}_dwin", saved["h"], da, _TN, (D // tm, 4, T // tk),
                   pl.BlockSpec((tk, tm), lambda i, j, k: (k, i)), pl.BlockSpec((tk, D), lambda i, j, k: (k, j)),
                   [(_sds((4, D, D), BF16), pl.BlockSpec((None, tm, D), lambda i, j, k: (j, i, 0)))], (tm, D))
    dx_in, d_sh, d_sc, d_g = _norm_bwd(tag, dx, dh, 0, saved["x_in"], g_norm, sc)
    return dx_in, dw_in, dw_out, dict(sh=d_sh, sc=d_sc, gate=d_gate, g_norm=d_g)


def _local_step(x, ctx, tgt, mods, cmods, norm_g, final_g, rec, conf, wg, on_grads=None, wg_pre=None, on_later=None):
    on_grads = on_grads or (lambda group, dws: None)
    wg_pre = wg_pre or (lambda group, after: None)
    on_later = on_later or (lambda after: None)
    n_t = T // ROW_TILE
    row = lambda v: v.reshape(1, -1)
    m0 = [row(mods[0, q]) for q in range(6)]
    m1 = [row(mods[1, q]) for q in range(6)]
    g00, g01, g10, g11 = (row(norm_g[0, 0]), row(norm_g[0, 1]), row(norm_g[1, 0]), row(norm_g[1, 1]))
    csh, csc = row(cmods[0]), row(cmods[1])
    pos = _pos_embed()

    def prep0(ids, t, v):
        cx, xv, pv = t
        is_ctx = ids[0] == 0
        xin = jnp.where(is_ctx, cx, xv + pv)
        sh = jnp.where(is_ctx, v[3], v[1])
        sc = jnp.where(is_ctx, v[4], v[2])
        return [_norm_mod(xin, v[0], sc, sh), xv + pv], []

    hcat, x0 = _tiled(
        "prep0", prep0, (N_SCAN,),
        [(ctx, pl.BlockSpec((ROW_TILE, D), lambda i: (0, 0))), _rows(x, off=-1, clamp_lo=True),
         _rows(pos, off=-1, clamp_lo=True)],
        [g00, m0[0], m0[1], csh, csc],
        [_orow(TA, D, BF16), _orow(T, D, F32, off=-1, clamp_lo=True)])

    tm_a = REC_TILE
    w_rec = wg("rec", hcat)
    (a_in,) = _mm("rec_in", hcat, w_rec["rec_w_in"], _NN, (TA // tm_a, 4, 1),
                  pl.BlockSpec((tm_a, D), lambda i, j, k: (i, 0)),
                  pl.BlockSpec((None, D, RH), lambda i, j, k: (j, 0, 0)),
                  [(_sds((TA, 2 * R), F32), pl.BlockSpec((tm_a, RH), lambda i, j, k: (i, j)))], (tm_a, RH))
    rec_starts = (0, 1)
    u = _dwconv("rec_conv", a_in, R // CW_REC, rec["conv_w"], row(rec["conv_b"]), 1, rec_starts, R, CW_REC)
    wbd = _dense_gates(rec["w_a"], rec["w_x"])
    gbias = _gate_bias_dense(rec["b_a"], rec["b_x"])
    lam = rec["lam"]
    a_f, b_f, a_r, b_r = _tiled("rg_fwd", _rg_fwd_fn, (TA // RG_TILE,), [_rows(u, tm=RG_TILE)], [wbd, gbias, lam],
                                [_orow(TA, R, F32, tm=RG_TILE)] * 4, vec_refs=True)
    dep = wg_pre("mlp0", a_f)
    y_f, y_r, hin_f, hin_r = _scan_fwd(a_f, b_f, a_r, b_r)

    def rec_mid(ids, t, v):
        gp, yf, yr = t
        g, _ = _gelu(gp)
        return [g * (yf + yr)], []

    (m_rec,) = _tiled("rec_mid", rec_mid, (n_t,),
                      [_rows(a_in, R, off=1), _rows(y_f, off=1), _rows(y_r, off=1)], _behind(dep),
                      [_orow(T, R, BF16)])
    tm = MM_TILE
    row_spec = pl.BlockSpec((tm, D), lambda i, j, k: (i, 0))
    norm_mlp0 = (g01, m0[4], m0[3])
    o_rec, x1, h_mlp0 = _mm(
        "rec_out", m_rec, w_rec["rec_w_out"], _NN, (T // tm, 1, 1),
        pl.BlockSpec((tm, R), lambda i, j, k: (i, 0)), pl.BlockSpec((R, D), lambda i, j, k: (0, 0)),
        [(_sds((T, D), F32), row_spec)] * 2 + [(_sds((T, D), BF16), row_spec)], (tm, D),
        extra=[(x0, row_spec), (m0[2], _full_spec(m0[2]))] + [(v, _full_spec(v)) for v in norm_mlp0],
        epi=_residual_epilogue(norm_mlp0))
    w_m0 = wg("mlp0", x1)
    dep = wg_pre("conf", x1)
    mlp0, x2, h1 = _mlp_fwd("mlp0", x1, h_mlp0, m0[5], w_m0["w_in"], w_m0["w_out"], (g10, m1[1], m1[0]), dep)

    b_pw1 = row(conf["b_pw1"])
    w_cf = wg("conf", x2)
    dep = wg_pre("mlp1", x2)
    (pre,) = _mm("conf_pw1", h1, w_cf["conf_w_pw1"], _NN, (T // tm, 4, 1),
                 pl.BlockSpec((tm, D), lambda i, j, k: (i, 0)),
                 pl.BlockSpec((None, D, D // 2), lambda i, j, k: (j, 0, 0)),
                 [(_sds((T, 2 * D), F32), pl.BlockSpec((tm, D // 2), lambda i, j, k: (i, j)))], (tm, D // 2),
                 extra=[(b_pw1, pl.BlockSpec((1, D // 2), lambda i, j, k: (0, j)))]
                 + [(d_, _full_spec(d_)) for d_ in _behind(dep)],
                 epi=lambda acc, ex: [acc + ex[0]])
    (zg,) = _tiled("conf_glu", lambda ids, t, v: ([t[0] * _sigmoid(t[1])], []), (n_t,),
                   [_rows(pre, D, col=0), _rows(pre, D, col=1)], [], [_orow(T, D, F32)])
    conf_starts = (0,)
    zc = _dwconv("conf_conv", zg, 0, conf["conv_w"], row(conf["conv_b"]), CONF_KW // 2, conf_starts, D, CW_CONF)
    ln_g, ln_b = row(conf["ln_g"]), row(conf["ln_b"])

    def ln_silu(ids, t, v):
        nh, _ = _layernorm_parts(t[0])
        ln = nh * v[0] + v[1]
        return [ln * _sigmoid(ln)], []

    (s_conf,) = _tiled("conf_ln", ln_silu, (n_t,), [_rows(zc)], [ln_g, ln_b], [_orow(T, D, BF16)])
    b_pw2 = row(conf["b_pw2"])
    norm_mlp1 = (g11, m1[4], m1[3])
    pw2_epi = _residual_epilogue(norm_mlp1)
    y_conf, x3, h_mlp1 = _mm(
        "conf_pw2", s_conf, w_cf["conf_w_pw2"], _NN, (T // tm, 1, 1),
        row_spec, pl.BlockSpec((D, D), lambda i, j, k: (0, 0)),
        [(_sds((T, D), F32), row_spec)] * 2 + [(_sds((T, D), BF16), row_spec)], (tm, D),
        extra=[(x2, row_spec), (m1[2], _full_spec(m1[2])), (b_pw2, _full_spec(b_pw2))]
        + [(v, _full_spec(v)) for v in norm_mlp1],
        epi=lambda acc, ex: pw2_epi(acc + ex[2], ex))
    w_m1 = wg("mlp1", x3)
    mlp1, x4, _ = _mlp_fwd("mlp1", x3, h_mlp1, m1[5], w_m1["w_in"], w_m1["w_out"])

    fg = row(final_g)

    def head(ids, t, v):
        n, r = _rms(t[0])
        err = n * v[0] - t[1]
        d_out = err * (1.0 / D)
        dn = d_out * v[0]
        dxv = r * (dn - n * jnp.mean(dn * n, axis=-1, keepdims=True))
        part = jnp.sum(_sum0(err * err), axis=1, keepdims=True) * (0.5 / D)
        return [dxv], [part, _sum0(d_out * n)]

    dx4, loss, d_fg = _tiled("head", head, (n_t,), [_rows(x4), _rows(tgt)], [fg], [_orow(T, D, F32)],
                             [(1, 1), (1, D)])

    dx3, dw_in1, dw_out1, dm_mlp1 = _mlp_bwd("mlp1", dx4, mlp1, g11, m1[4], m1[5],
                                             w_m1["w_in"], w_m1["w_out"])
    dep = on_grads("mlp1", (dw_in1, dw_out1))
    d_y, d_g1c, d_bpw2 = _gate_bwd("conf", dx3, y_conf, m1[2], dep)
    tk = MM_TILE
    (dw_pw2,) = _mm("conf_dwpw2", s_conf, d_y, _TN, (D // tm, 1, T // tk),
                    pl.BlockSpec((tk, tm), lambda i, j, k: (k, i)), pl.BlockSpec((tk, D), lambda i, j, k: (k, 0)),
                    [(_sds((D, D), BF16), pl.BlockSpec((tm, D), lambda i, j, k: (i, 0)))], (tm, D))
    (ds,) = _mm("conf_ds", d_y, w_cf["conf_w_pw2"], _NT, (T // tm, 1, 1),
                pl.BlockSpec((tm, D), lambda i, j, k: (i, 0)), pl.BlockSpec((D, D), lambda i, j, k: (0, 0)),
                [(_sds((T, D), F32), pl.BlockSpec((tm, D), lambda i, j, k: (i, 0)))], (tm, D))
    dep = on_later(ds)

    def ln_silu_bwd(ids, t, v):
        dsv, zcv = t
        nh, rstd = _layernorm_parts(zcv)
        ln = nh * v[0] + v[1]
        sg = _sigmoid(ln)
        d_ln = dsv * (sg * (1.0 + ln * (1.0 - sg)))
        d_nh = d_ln * v[0]
        d_zc = rstd * (d_nh - jnp.mean(d_nh, axis=-1, keepdims=True)
                       - nh * jnp.mean(d_nh * nh, axis=-1, keepdims=True))
        return [d_zc], [_sum0(d_ln * nh), _sum0(d_ln)]

    d_zc, d_lng, d_lnb = _tiled("conf_ln_bwd", ln_silu_bwd, (n_t,), [_rows(ds), _rows(zc)],
                                [ln_g, ln_b] + _behind(dep), [_orow(T, D, F32)], [(1, D), (1, D)])
    d_zg = _dwconv("conf_conv_dx", d_zc, 0, conf["conv_w"][::-1], jnp.zeros((1, D), F32),
                   CONF_KW - 1 - CONF_KW // 2, conf_starts, D, CW_CONF)
    d_cw_conf = _dwconv_wgrad("conf_conv_dw", d_zc, zg, 0, CONF_KW, CONF_KW // 2, conf_starts, D, CW_CONF)

    def glu_bwd(ids, t, v):
        dz, pa, pb = t
        sg = _sigmoid(pb)
        d_a = dz * sg
        d_b = dz * pa * sg * (1.0 - sg)
        return [jnp.concatenate([d_a, d_b], axis=1)], [_sum0(d_a), _sum0(d_b)]

    d_pre, d_b1a, d_b1b = _tiled(
        "conf_glu_bwd", glu_bwd, (n_t,), [_rows(d_zg), _rows(pre, D, col=0), _rows(pre, D, col=1)], [],
        [_orow(T, 2 * D, BF16)], [(1, D), (1, D)])
    (dw_pw1,) = _mm("conf_dwpw1", h1, d_pre, _TN, (D // tm, 4, T // tk),
                    pl.BlockSpec((tk, tm), lambda i, j, k: (k, i)),
                    pl.BlockSpec((tk, D // 2), lambda i, j, k: (k, j)),
                    [(_sds((4, D, D // 2), BF16), pl.BlockSpec((None, tm, D // 2), lambda i, j, k: (j, i, 0)))],
                    (tm, D // 2))
    dep = on_grads("conf", (dw_pw1, dw_pw2))
    (dh1,) = _mm("conf_dh", d_pre, w_cf["conf_w_pw1"], _NT, (T // tm, 1, 4),
                 pl.BlockSpec((tm, D // 2), lambda i, j, k: (i, k)),
                 pl.BlockSpec((None, D, D // 2), lambda i, j, k: (k, 0, 0)),
                 [(_sds((T, D), F32), pl.BlockSpec((tm, D), lambda i, j, k: (i, 0)))], (tm, D))
    dx2, d_sh1c, d_sc1c, d_g10 = _norm_bwd("conf", dx3, dh1, 0, x2, g10, m1[1], dep=dep)
    dep = on_later(dx2)

    dx1, dw_in0, dw_out0, dm_mlp0 = _mlp_bwd("mlp0", dx2, mlp0, g01, m0[4], m0[5],
                                             w_m0["w_in"], w_m0["w_out"], dep)
    dep = on_grads("mlp0", (dw_in0, dw_out0))
    d_orec, d_g1r, _ = _gate_bwd("rec", dx1, o_rec, m0[2], dep)
    (dw_rout,) = _mm("rec_dwout", m_rec, d_orec, _TN, (R // RH, 1, T // tk),
                     pl.BlockSpec((tk, RH), lambda i, j, k: (k, i)), pl.BlockSpec((tk, D), lambda i, j, k: (k, 0)),
                     [(_sds((R, D), BF16), pl.BlockSpec((RH, D), lambda i, j, k: (i, 0)))], (RH, D))
    (dm_rec,) = _mm("rec_dm", d_orec, w_rec["rec_w_out"], _NT, (T // tm, 1, 1),
                    pl.BlockSpec((tm, D), lambda i, j, k: (i, 0)), pl.BlockSpec((R, D), lambda i, j, k: (0, 0)),
                    [(_sds((T, R), F32), pl.BlockSpec((tm, R), lambda i, j, k: (i, 0)))], (tm, R))
    dep = on_later(dm_rec)

    def rec_mid_bwd(ids, t, v):
        dmv, gp, yf, yr = t
        g, th = _gelu(gp)
        lat = ids[0] > 0
        d_gp = jnp.where(lat, dmv * (yf + yr) * _gelu_grad(gp, th), 0.0)
        dy = jnp.where(lat, dmv * g, 0.0)
        return [d_gp, dy], []

    d_gp, dy = _tiled("rec_mid_bwd", rec_mid_bwd, (N_SCAN,),
                      [_rows(dm_rec, off=-1, clamp_lo=True), _rows(a_in, R), _rows(y_f), _rows(y_r)], _behind(dep),
                      [_orow(TA, R, BF16), _orow(TA, R, F32)])
    da_f, db_f, da_r, db_r = _scan_bwd(dy, a_f, y_f, hin_f, a_r, y_r, hin_r)
    d_gpre, d_u, d_gbias, d_lam = _tiled(
        "rg_bwd", _rg_bwd_fn, (TA // RG_TILE,), [_rows(a, tm=RG_TILE) for a in (u, da_f, db_f, da_r, db_r)],
        [wbd, gbias, lam], [_orow(TA, 2 * NQ, BF16, tm=RG_TILE), _orow(TA, R, F32, tm=RG_TILE)],
        [(1, 2 * NQ), (1, 2 * R)], vec_refs=True)
    tk_a = REC_TILE
    blk_mask, blk_spread = _block_mask(), _block_spread()
    (d_wbd,) = _mm("rg_dw", u, d_gpre, _TN, (2, 2, TA // tk_a),
                   pl.BlockSpec((tk_a, RH), lambda i, j, k: (k, i)),
                   pl.BlockSpec((tk_a, NQ // 2), lambda i, j, k: (k, 2 * i + j)),
                   [(_sds((2, 4, RH, BLK), F32), pl.BlockSpec((None, 2, RH, BLK), lambda i, j, k: (i, j, 0, 0)))],
                   (RH, NQ // 2),
                   extra=[(blk_mask, _full_spec(blk_mask)), (blk_spread, _full_spec(blk_spread))],
                   epi=lambda acc, ex: [jnp.stack([_fold_blocks(acc[:, s * RH:(s + 1) * RH], ex[0], ex[1])
                                                   for s in range(2)])])
    d_p = _dwconv("rec_conv_dx", d_u, 0, rec["conv_w"][::-1], jnp.zeros((1, R), F32), REC_KW - 1 - 1,
                  rec_starts, R, CW_REC)
    d_cw_rec = _dwconv_wgrad("rec_conv_dw", d_u, a_in, R // CW_REC, REC_KW, 1, rec_starts, R, CW_REC)
    d_a = jnp.concatenate([d_gp, d_p.astype(BF16)], axis=1)
    (dw_rin,) = _mm("rec_dwin", hcat, d_a, _TN, (D // tm, 4, TA // tk_a),
                    pl.BlockSpec((tk_a, tm), lambda i, j, k: (k, i)), pl.BlockSpec((tk_a, RH), lambda i, j, k: (k, j)),
                    [(_sds((4, D, RH), BF16), pl.BlockSpec((None, tm, RH), lambda i, j, k: (j, i, 0)))], (tm, RH))
    dep = on_grads("rec", (dw_rin, dw_rout))
    (dhcat,) = _mm("rec_dh", d_a, w_rec["rec_w_in"], _NT, (TA // tm_a, 1, 4),
                   pl.BlockSpec((tm_a, RH), lambda i, j, k: (i, k)),
                   pl.BlockSpec((None, D, RH), lambda i, j, k: (k, 0, 0)),
                   [(_sds((TA, D), F32), pl.BlockSpec((tm_a, D), lambda i, j, k: (i, 0)))], (tm_a, D))
    dx0, d_sh1r, d_sc1r, d_g00 = _norm_bwd("rec", dx1, dhcat, 1, x0, g00, m0[1], dep=dep)
    d_csh, d_csc, d_g00c = _norm_bwd("ctx", None, dhcat, 0, ctx, g00, csc, with_dx=False)

    big = dict(rec_w_in=dw_rin, rec_w_out=dw_rout, conf_w_pw1=dw_pw1, conf_w_pw2=dw_pw2,
               mlp_w_in=(dw_in0, dw_in1), mlp_w_out=(dw_out0, dw_out1))
    d_wa, d_wx = _gate_block_grads(d_wbd)
    d_ba, d_bx = _gate_bias_grads(d_gbias)
    d_mod = jnp.concatenate([
        d_sh1r, d_sc1r, d_g1r, dm_mlp0["sh"], dm_mlp0["sc"], dm_mlp0["gate"],
        d_sh1c, d_sc1c, d_g1c, dm_mlp1["sh"], dm_mlp1["sc"], dm_mlp1["gate"]], axis=1).reshape(2, 6 * D)
    small = dict(
        d_mod=d_mod, d_cmod=jnp.concatenate([d_csh, d_csc], axis=1),
        norm_g=jnp.concatenate([d_g00 + d_g00c, dm_mlp0["g_norm"], d_g10, dm_mlp1["g_norm"]], axis=1),
        rec_conv_w=d_cw_rec[:REC_KW], rec_conv_b=d_cw_rec[REC_KW], rec_lambda=d_lam.reshape(2, R),
        rec_w_a=d_wa, rec_b_a=d_ba, rec_w_x=d_wx, rec_b_x=d_bx,
        conf_b_pw1=jnp.concatenate([d_b1a, d_b1b], axis=1), conf_conv_w=d_cw_conf[:CONF_KW],
        conf_conv_b=d_cw_conf[CONF_KW], conf_ln_g=d_lng, conf_ln_b=d_lnb, conf_b_pw2=d_bpw2, final_g=d_fg)
    return loss.reshape(()), dx0, big, small


_BIG = ("rec_w_in", "rec_w_out", "conf_w_pw1", "conf_w_pw2", "mlp_w_in", "mlp_w_out")


def _halves(w):
    return w.reshape(2, w.shape[0] // 2, w.shape[1])


def _ada_fwd(c16, w_ada, b_shard):
    ns = w_ada.shape[2]
    tn = 512

    def kern(c_ref, w_ref, b_ref, o_ref):
        cv = c_ref[...]
        s = (cv * _sigmoid(cv)).astype(BF16)
        o_ref[...] = jnp.dot(s, w_ref[...].astype(BF16), preferred_element_type=F32) + b_ref[...]

    return _pcall(
        kern, name="ada_fwd", grid=(2, ns // tn),
        in_specs=[pl.BlockSpec((16, D), lambda l, j: (0, 0)), pl.BlockSpec((None, D, tn), lambda l, j: (l, 0, j)),
                  pl.BlockSpec((None, 1, tn), lambda l, j: (l, 0, j))],
        out_specs=pl.BlockSpec((None, 16, tn), lambda l, j: (l, 0, j)),
        out_shape=_sds((2, 16, ns), F32), compiler_params=_cparams(),
    )(c16, w_ada, b_shard)


def _ada_bwd(c16, dm16, w_ada):
    ns = w_ada.shape[2]
    tn = 512

    def kern(c_ref, dm_ref, w_ref, gw_ref, ds_ref):
        cv = c_ref[...]
        s = (cv * _sigmoid(cv)).astype(BF16)
        dm = dm_ref[...].astype(BF16)
        gw_ref[...] = lax.dot_general(s, dm, _TN, preferred_element_type=F32)

        @pl.when(jnp.logical_and(pl.program_id(0) == 0, pl.program_id(1) == 0))
        def _():
            ds_ref[...] = jnp.zeros_like(ds_ref)

        ds_ref[...] += lax.dot_general(dm, w_ref[...].astype(BF16), _NT, preferred_element_type=F32)

    return _pcall(
        kern, name="ada_bwd", grid=(2, ns // tn),
        in_specs=[pl.BlockSpec((16, D), lambda l, j: (0, 0)), pl.BlockSpec((None, 16, tn), lambda l, j: (l, 0, j)),
                  pl.BlockSpec((None, D, tn), lambda l, j: (l, 0, j))],
        out_specs=[pl.BlockSpec((None, D, tn), lambda l, j: (l, 0, j)), pl.BlockSpec((16, D), lambda l, j: (0, 0))],
        out_shape=[_sds((2, D, ns), F32), _sds((16, D), F32)], compiler_params=_cparams(),
    )(c16, dm16, w_ada)


def _cctx_grad(ds8, c_ctx):
    def kern(d_ref, c_ref, o_ref):
        tot = d_ref[0, 8:9, :] + d_ref[2, 8:9, :] + d_ref[4, 8:9, :] + d_ref[6, 8:9, :]
        cv = c_ref[...]
        sg = _sigmoid(cv)
        o_ref[...] = tot * (sg * (1.0 + cv * (1.0 - sg)))

    return _pcall(kern, name="cctx_grad", out_shape=_sds((1, D), F32))(ds8, c_ctx.reshape(1, D))


def kernel(x, c, ctx, c_ctx, w_ada, b_ada, norm_g, rec_w_in, rec_conv_w, rec_conv_b, rec_lambda, rec_w_a, rec_b_a, rec_w_x, rec_b_x, rec_w_out, conf_w_pw1, conf_b_pw1, conf_conv_w, conf_conv_b, conf_ln_g, conf_ln_b, conf_w_pw2, conf_b_pw2, mlp_w_in, mlp_w_out, final_g, loss_target, m_c_ctx, m_w_ada, m_b_ada, m_norm_g, m_rec_w_in, m_rec_conv_w, m_rec_conv_b, m_rec_lambda, m_rec_w_a, m_rec_b_a, m_rec_w_x, m_rec_b_x, m_rec_w_out, m_conf_w_pw1, m_conf_b_pw1, m_conf_conv_w, m_conf_conv_b, m_conf_ln_g, m_conf_ln_b, m_conf_w_pw2, m_conf_b_pw2, m_mlp_w_in, m_mlp_w_out, m_final_g, v_c_ctx, v_w_ada, v_b_ada, v_norm_g, v_rec_w_in, v_rec_conv_w, v_rec_conv_b, v_rec_lambda, v_rec_w_a, v_rec_b_a, v_rec_w_x, v_rec_b_x, v_rec_w_out, v_conf_w_pw1, v_conf_b_pw1, v_conf_conv_w, v_conf_conv_b, v_conf_ln_g, v_conf_ln_b, v_conf_w_pw2, v_conf_b_pw2, v_mlp_w_in, v_mlp_w_out, v_final_g):
    names = ["c_ctx", "w_ada", "b_ada", "norm_g", "rec_w_in", "rec_conv_w", "rec_conv_b", "rec_lambda", "rec_w_a",
             "rec_b_a", "rec_w_x", "rec_b_x", "rec_w_out", "conf_w_pw1", "conf_b_pw1", "conf_conv_w", "conf_conv_b",
             "conf_ln_g", "conf_ln_b", "conf_w_pw2", "conf_b_pw2", "mlp_w_in", "mlp_w_out", "final_g"]
    w = dict(zip(names, [c_ctx, w_ada, b_ada, norm_g, rec_w_in, rec_conv_w, rec_conv_b, rec_lambda, rec_w_a,
                         rec_b_a, rec_w_x, rec_b_x, rec_w_out, conf_w_pw1, conf_b_pw1, conf_conv_w, conf_conv_b,
                         conf_ln_g, conf_ln_b, conf_w_pw2, conf_b_pw2, mlp_w_in, mlp_w_out, final_g]))
    m = dict(zip(names, [m_c_ctx, m_w_ada, m_b_ada, m_norm_g, m_rec_w_in, m_rec_conv_w, m_rec_conv_b, m_rec_lambda,
                         m_rec_w_a, m_rec_b_a, m_rec_w_x, m_rec_b_x, m_rec_w_out, m_conf_w_pw1, m_conf_b_pw1,
                         m_conf_conv_w, m_conf_conv_b, m_conf_ln_g, m_conf_ln_b, m_conf_w_pw2, m_conf_b_pw2,
                         m_mlp_w_in, m_mlp_w_out, m_final_g]))
    v = dict(zip(names, [v_c_ctx, v_w_ada, v_b_ada, v_norm_g, v_rec_w_in, v_rec_conv_w, v_rec_conv_b, v_rec_lambda,
                         v_rec_w_a, v_rec_b_a, v_rec_w_x, v_rec_b_x, v_rec_w_out, v_conf_w_pw1, v_conf_b_pw1,
                         v_conf_conv_w, v_conf_conv_b, v_conf_ln_g, v_conf_ln_b, v_conf_w_pw2, v_conf_b_pw2,
                         v_mlp_w_in, v_mlp_w_out, v_final_g]))
    mx, my, mc = _me()
    chip = 2 * mx + my
    me = 4 * mx + 2 * my + mc

    sharded_small = ["norm_g", "rec_conv_w", "rec_lambda", "conf_b_pw1", "conf_conv_w", "conf_conv_b", "conf_ln_g",
                     "conf_ln_b", "conf_b_pw2"]
    packed, offs = _pack([c] + [w[k] for k in sharded_small], 8)
    place = jnp.stack([chip, mc]).astype(jnp.int32)
    shards = [_halves(rec_w_in[0]), _halves(rec_w_out[0]), _halves(conf_w_pw1[0]), _halves(conf_w_pw2[0]),
              _halves(mlp_w_in[0]), _halves(mlp_w_in[1]), _halves(mlp_w_out[0]), _halves(mlp_w_out[1])]
    slots = _place_big(shards, place)
    placed = jnp.stack([lax.dynamic_slice(s, (chip, 0, 0, 0), (1, 1, 1, 1)).reshape(1) for s in slots])
    got = _allgather8("gather_small", packed, placed)

    got_flat = got.reshape(8, -1)

    def piece(i):
        p, n, shape = offs[i]
        return got_flat[:, p:p + n].reshape((8,) + tuple(shape))

    c_rows = piece(0).reshape(8, D)
    full = {}
    for i, k in enumerate(sharded_small):
        per_chip = jnp.moveaxis(piece(1 + i)[0::2], 0, -2)
        full[k] = per_chip.reshape(per_chip.shape[:-2] + (4 * per_chip.shape[-1],))
    c16 = jnp.concatenate([c_rows, c_ctx.reshape(1, D), jnp.zeros((7, D), F32)], axis=0)

    ns = w_ada.shape[2]
    b_shard = lax.dynamic_slice_in_dim(b_ada, chip * ns, ns, axis=1).reshape(2, 1, ns)
    prod = _ada_fwd(c16, w_ada, b_shard).reshape(32, ns)

    mod_state, mod_started = _gather8_start("gather_mod_start", prod, got)
    use_order = dict(rec=(0, 1), mlp0=(4, 6), conf=(2, 3), mlp1=(5, 7))
    flying, gsems, swapping = {}, {}, {}
    fly, sems, rec_started = _gather_start("gather_start_rec", [slots[t] for t in use_order["rec"]], ((0, 1),),
                                           mod_started)
    flying["rec"], gsems["rec"] = fly, sems
    later = ("mlp0", "conf", "mlp1")
    fly, sems, all_started = _gather_start("gather_start_rest", [slots[t] for g in later for t in use_order[g]],
                                           ((0, 1), (2, 3), (4, 5)), rec_started)
    for gi, g in enumerate(later):
        flying[g], gsems[g] = fly[2 * gi:2 * gi + 2], sems[2 * gi:2 * gi + 2]

    def wg_pre(group, after):
        bufs = _gather_wait(f"gather_wait_{group}", flying[group], *gsems[group], after)
        swapping[group], token = _swap_start(f"swap_start_{group}", bufs, after)
        return token

    def wg(group, after):
        if group in swapping:
            a, b = _swap_wait(f"swap_wait_{group}", *swapping[group], after)
        else:
            a, b = _swap_halves(f"swap_{group}",
                                _gather_wait(f"gather_wait_{group}", flying[group], *gsems[group], after))
        if group == "rec":
            return dict(rec_w_in=a.reshape(4, D, RH), rec_w_out=b.reshape(R, D))
        if group == "conf":
            return dict(conf_w_pw1=a.reshape(4, D, D // 2), conf_w_pw2=b.reshape(D, D))
        return dict(w_in=a.reshape(4, D, D), w_out=b.reshape(FF, D))

    prod, prod8 = _gather8_wait("gather_mod_wait", *mod_state, all_started)
    prod8 = lax.dynamic_update_slice(prod8, prod[None], (me, 0, 0)).reshape(8, 2, 16, ns)
    mod_all = jnp.concatenate([prod8[2 * j] for j in range(4)], axis=-1)
    mods = lax.dynamic_index_in_dim(mod_all, me, axis=1, keepdims=False).reshape(2, 6, D)
    cmods = mod_all[0, 8].reshape(6, D)[:2]

    rec = dict(conv_w=full["rec_conv_w"][0], conv_b=rec_conv_b[0], lam=full["rec_lambda"][0],
               w_a=rec_w_a[0], b_a=rec_b_a[0], w_x=rec_w_x[0], b_x=rec_b_x[0])
    conf = dict(b_pw1=full["conf_b_pw1"][0], conv_w=full["conf_conv_w"][0], conv_b=full["conf_conv_b"][0],
                ln_g=full["conf_ln_g"][0], ln_b=full["conf_ln_b"][0], b_pw2=full["conf_b_pw2"][0])
    pairing, sent = {}, {}

    def on_grads(group, dws):
        parts = [dw.reshape(4, 2, shards[t].shape[1], shards[t].shape[2]) for dw, t in zip(dws, use_order[group])]
        pairing[group], token = _reduce_begin(group, parts, place)
        return token

    def finish_pair(after):
        (group, state), = pairing.items()
        pairing.clear()
        sent[group], token = _reduce_mid(group, state, place, after)
        return token

    loss_local, grad_x, _, small = _local_step(x[0], ctx[0], loss_target[0], mods, cmods, full["norm_g"], final_g,
                                               rec, conf, wg, on_grads, wg_pre, finish_pair)
    rec_sent = finish_pair(grad_x)
    small["loss"] = loss_local.reshape(1)

    small_names = ["loss", "d_mod", "d_cmod", "norm_g", "rec_conv_w", "rec_conv_b", "rec_lambda", "rec_w_a", "rec_b_a",
                   "rec_w_x", "rec_b_x", "conf_b_pw1", "conf_conv_w", "conf_conv_b", "conf_ln_g", "conf_ln_b",
                   "conf_b_pw2", "final_g"]
    mine = lax.broadcasted_iota(jnp.int32, (8, 1), 0) == me
    mod_slots = jnp.where(mine, small["d_mod"].reshape(1, -1), 0.0)
    spacked, soffs = _pack([small[k] for k in small_names] + [mod_slots])
    small_state, small_started = _allreduce_small_begin(spacked, place, rec_sent)

    fulls = {}
    for group in ("mlp1", "conf", "mlp0", "rec"):
        for t, f in zip(use_order[group], _reduce_end(group, sent[group], place, small_started)):
            fulls[t] = f
    whole = _share_halves("share_grads", [fulls[t] for t in range(8)])
    g_big = dict(rec_w_in=whole[0].reshape(rec_w_in.shape), rec_w_out=whole[1].reshape(rec_w_out.shape),
                 conf_w_pw1=whole[2].reshape(conf_w_pw1.shape), conf_w_pw2=whole[3].reshape(conf_w_pw2.shape),
                 mlp_w_in=jnp.stack([whole[4].reshape(D, D), whole[5].reshape(D, D)]),
                 mlp_w_out=jnp.stack([whole[6].reshape(D, D), whole[7].reshape(D, D)]))
    delta, new_m, new_v = {}, {}, {}

    def adamw_of(k, g):
        cols = w[k].shape[-1]
        d_, m_, v_ = _adamw(f"adamw_{k}", w[k].reshape(-1, cols), g.reshape(-1, cols),
                            m[k].reshape(-1, cols), v[k].reshape(-1, cols))
        delta[k], new_m[k], new_v[k] = (a.reshape(w[k].shape) for a in (d_, m_, v_))

    for k in _BIG:
        adamw_of(k, g_big[k])

    unpacked = _unpack(_allreduce_small_end(small_state, new_v[_BIG[-1]]), soffs)
    ssum = dict(zip(small_names, unpacked[:-1]))
    loss = ssum["loss"].reshape(())
    dmod_rows = unpacked[-1].reshape(8, 2, 6 * D).transpose(1, 0, 2)

    d_cmod_full =jnp.concatenate([ssum["d_cmod"].reshape(1, 2 * D), jnp.zeros((1, 4 * D), F32)], axis=1)
    dm16 = jnp.concatenate([dmod_rows, jnp.stack([d_cmod_full, jnp.zeros((1, 6 * D), F32)]),
                            jnp.zeros((2, 7, 6 * D), F32)], axis=1)
    dm16_shard = lax.dynamic_slice_in_dim(dm16, chip * ns, ns, axis=2)
    g_w_ada, ds_part = _ada_bwd(c16, dm16_shard, w_ada)
    ds8 = _allgather8("gather_dsilu", ds_part)
    g_c_ctx = _cctx_grad(ds8, c_ctx).reshape(D)
    g_b_ada = ssum["d_mod"] + jnp.stack([d_cmod_full[0], jnp.zeros((6 * D,), F32)])

    def shard_of(a, axis):
        n = a.shape[axis] // 4
        return lax.dynamic_slice_in_dim(a, chip * n, n, axis=axis)

    grads = dict(
        c_ctx=g_c_ctx, w_ada=g_w_ada, b_ada=g_b_ada,
        norm_g=shard_of(ssum["norm_g"].reshape(2, 2, D), 2),
        rec_w_in=g_big["rec_w_in"], rec_conv_w=shard_of(ssum["rec_conv_w"].reshape(1, REC_KW, R), 2),
        rec_conv_b=ssum["rec_conv_b"].reshape(1, R), rec_lambda=shard_of(ssum["rec_lambda"].reshape(1, 2, R), 2),
        rec_w_a=ssum["rec_w_a"].reshape(rec_w_a.shape), rec_b_a=ssum["rec_b_a"].reshape(rec_b_a.shape),
        rec_w_x=ssum["rec_w_x"].reshape(rec_w_x.shape), rec_b_x=ssum["rec_b_x"].reshape(rec_b_x.shape),
        rec_w_out=g_big["rec_w_out"], conf_w_pw1=g_big["conf_w_pw1"],
        conf_b_pw1=shard_of(ssum["conf_b_pw1"].reshape(1, 2 * D), 1),
        conf_conv_w=shard_of(ssum["conf_conv_w"].reshape(1, CONF_KW, D), 2),
        conf_conv_b=shard_of(ssum["conf_conv_b"].reshape(1, D), 1),
        conf_ln_g=shard_of(ssum["conf_ln_g"].reshape(1, D), 1), conf_ln_b=shard_of(ssum["conf_ln_b"].reshape(1, D), 1),
        conf_w_pw2=g_big["conf_w_pw2"], conf_b_pw2=shard_of(ssum["conf_b_pw2"].reshape(1, D), 1),
        mlp_w_in=g_big["mlp_w_in"], mlp_w_out=g_big["mlp_w_out"], final_g=ssum["final_g"].reshape(D))

    adamw_of("w_ada", g_w_ada)
    rest = [k for k in names if k not in ("w_ada",) + _BIG]
    d_, m_, v_ = _adamw_many("adamw_small", [w[k] for k in rest], [grads[k] for k in rest],
                             [m[k] for k in rest], [v[k] for k in rest])
    for k, dd, mm, vv in zip(rest, d_, m_, v_):
        delta[k], new_m[k], new_v[k] = dd, mm, vv

    return (loss, grad_x[None], *[grads[k] for k in names], *[delta[k] for k in names],
            *[new_m[k] for k in names], *[new_v[k] for k in names])
```

```python
import functools
import math

import jax
import jax.numpy as jnp
from jax import lax
from jax.experimental import pallas as pl
from jax.experimental.pallas import tpu as pltpu

F32 = jnp.float32
BF16 = jnp.bfloat16

D = 1024
T = 2048
TC = 256
TA = T + TC
R = 1280
RH = R // 2
NQ = 4 * RH
FF = 4096
N_BLK = 16
BLK = R // N_BLK
GRID_W = 64
EPS = 1e-6
RG_C = 8.0
CONF_KW = 31
REC_KW = 4
LANE = 128
ROW_TILE = 256
HALO = 16
RG_TILE = 128
PACK_ROWS = 512
MM_TILE = 1024
REC_TILE = TA // 2
CW_REC = 640
CW_CONF = 512
V7X_VMEM_BYTES = 64 * 1024 * 1024
VMEM_LIMIT = V7X_VMEM_BYTES - 8 * 1024 * 1024

ADAM_LR = 0.001
ADAM_B1 = 0.9
ADAM_B2 = 0.999
ADAM_EPS = 1e-08
ADAM_WD = 0.01
ADAM_STEP = 10

MESH = pl.DeviceIdType.MESH
ANY = pl.BlockSpec(memory_space=pl.ANY)


def _sds(shape, dtype):
    return jax.ShapeDtypeStruct(tuple(shape), dtype)


def _pcall(body, **kw):
    return pl.pallas_call(body, **kw)


def _cparams():
    return pltpu.CompilerParams(vmem_limit_bytes=VMEM_LIMIT)


def _full_spec(arr):
    nd = arr.ndim
    return pl.BlockSpec(arr.shape, lambda *ids, _n=nd: (0,) * _n)


def _sum0(v):
    return jnp.sum(v, axis=0, keepdims=True)


def _tiled(name, fn, grid, ins, vecs, outs, vec_outs=(), vec_refs=False):
    n_in, n_vec, n_out = len(ins), len(vecs), len(outs)
    n_grid = len(grid)

    def kern(*refs):
        ids = [pl.program_id(a) for a in range(n_grid)]
        tin = [r[...] for r in refs[:n_in]]
        vin = list(refs[n_in:n_in + n_vec]) if vec_refs else [r[...] for r in refs[n_in:n_in + n_vec]]
        o_refs = refs[n_in + n_vec:n_in + n_vec + n_out]
        a_refs = refs[n_in + n_vec + n_out:]
        tout, incs = fn(ids, tin, vin)
        for r, v in zip(o_refs, tout):
            r[...] = v.astype(r.dtype)
        if a_refs:
            first = functools.reduce(jnp.logical_and, [i == 0 for i in ids])

            @pl.when(first)
            def _():
                for r in a_refs:
                    r[...] = jnp.zeros_like(r)

            for r, v in zip(a_refs, incs):
                r[...] += v

    out_shape = [o for o, _ in outs] + [_sds(s, F32) for s in vec_outs]
    out_specs = [s for _, s in outs] + [
        pl.BlockSpec(tuple(s), lambda *ids, _n=len(s): (0,) * _n) for s in vec_outs]
    res = _pcall(
        kern, name=name, grid=tuple(grid),
        in_specs=[s for _, s in ins] + [_full_spec(v) for v in vecs],
        out_specs=out_specs, out_shape=out_shape, compiler_params=_cparams(),
    )(*[a for a, _ in ins], *vecs)
    return list(res)


def _rows(arr, ncols=None, tm=ROW_TILE, off=0, col=0, clamp_lo=False):
    ncols = arr.shape[1] if ncols is None else ncols
    if clamp_lo:
        return arr, pl.BlockSpec((tm, ncols), lambda i: (jnp.maximum(i + off, 0), col))
    return arr, pl.BlockSpec((tm, ncols), lambda i: (i + off, col))


def _orow(nrows, ncols, dtype, tm=ROW_TILE, off=0, clamp_lo=False):
    if clamp_lo:
        return _sds((nrows, ncols), dtype), pl.BlockSpec((tm, ncols), lambda i: (jnp.maximum(i + off, 0), 0))
    return _sds((nrows, ncols), dtype), pl.BlockSpec((tm, ncols), lambda i: (i + off, 0))


_NN = (((1,), (0,)), ((), ()))
_TN = (((0,), (0,)), ((), ()))
_NT = (((1,), (1,)), ((), ()))


def _mm(name, a, b, dims, grid, a_spec, b_spec, out, acc_shape, extra=(), a_pre=None, epi=None):
    n_k = grid[2]
    n_ex = len(extra)

    def kern(a_ref, b_ref, *rest):
        ex = rest[:n_ex]
        o_refs = rest[n_ex:n_ex + len(out)]
        k = pl.program_id(2)
        av = a_ref[...]
        if a_pre is not None:
            av = a_pre(av)
        part = lax.dot_general(av.astype(BF16), b_ref[...].astype(BF16), dims, preferred_element_type=F32)

        def finish(total):
            vals = [total] if epi is None else epi(total, [e[...] for e in ex])
            for r, v in zip(o_refs, vals):
                r[...] = v.astype(r.dtype)

        if n_k == 1:
            finish(part)
        else:
            acc = rest[-1]

            @pl.when(k == 0)
            def _():
                acc[...] = part

            @pl.when(jnp.logical_and(k > 0, k < n_k - 1))
            def _():
                acc[...] += part

            @pl.when(k == n_k - 1)
            def _():
                finish(acc[...] + part)

    res = _pcall(
        kern, name=name, grid=tuple(grid),
        in_specs=[a_spec, b_spec] + [s for _, s in extra],
        out_specs=[s for _, s in out], out_shape=[o for o, _ in out],
        scratch_shapes=[] if n_k == 1 else [pltpu.VMEM(tuple(acc_shape), F32)], compiler_params=_cparams(),
    )(a, b, *[e for e, _ in extra])
    return list(res)


def _rms(x):
    r = lax.rsqrt(jnp.mean(x * x, axis=-1, keepdims=True) + EPS)
    return x * r, r


def _norm_mod(x, g, sc, sh):
    n, _ = _rms(x)
    return (n * g) * (1.0 + sc) + sh


def _norm_mod_bwd(dh, x, g, sc):
    n, r = _rms(x)
    d_sh = _sum0(dh)
    d_sc = _sum0(dh * (n * g))
    d_g = _sum0(dh * (1.0 + sc) * n)
    dn = dh * (g * (1.0 + sc))
    dx = r * (dn - n * jnp.mean(dn * n, axis=-1, keepdims=True))
    return dx, d_sh, d_sc, d_g


_GELU_K = math.sqrt(2.0 / math.pi)


def _gelu(x):
    t = jnp.tanh(_GELU_K * (x + 0.044715 * x * x * x))
    return 0.5 * x * (1.0 + t), t


def _gelu_grad(x, t):
    return 0.5 * (1.0 + t) + 0.5 * x * (1.0 - t * t) * (_GELU_K * (1.0 + 3.0 * 0.044715 * x * x))


def _sigmoid(x):
    return 0.5 * jnp.tanh(0.5 * x) + 0.5


def _expm1(x):
    p = jnp.full_like(x, 1.0 / 5040.0)
    for c in (1.0 / 720.0, 1.0 / 120.0, 1.0 / 24.0, 1.0 / 6.0, 0.5, 1.0):
        p = p * x + c
    return jnp.where(jnp.abs(x) < 0.3, x * p, jnp.exp(x) - 1.0)


def _softplus_neg(lam):
    return jnp.log1p(jnp.exp(-jnp.abs(lam))) + jnp.maximum(-lam, 0.0)


def _layernorm_parts(x):
    mu = jnp.mean(x, axis=-1, keepdims=True)
    xc = x - mu
    rstd = lax.rsqrt(jnp.mean(xc * xc, axis=-1, keepdims=True) + EPS)
    return xc * rstd, rstd


def _rg_gates(u, wbd, gbias, lam):
    sp = _softplus_neg(lam)
    parts = {}
    for h in range(2):
        uh = u[:, h * RH:(h + 1) * RH]
        g = jnp.dot(uh.astype(BF16), wbd[h], preferred_element_type=F32) + gbias[:, h * NQ:(h + 1) * NQ]
        for d in range(2):
            r = _sigmoid(g[:, (2 * d) * RH:(2 * d + 1) * RH])
            i = _sigmoid(g[:, (2 * d + 1) * RH:(2 * d + 2) * RH])
            sph = sp[d:d + 1, h * RH:(h + 1) * RH]
            la = (-RG_C) * r * sph
            e2 = _expm1(2.0 * la)
            inv_mult = jnp.where(e2 < 0.0, lax.rsqrt(-e2), 0.0)
            parts[(d, h)] = dict(r=r, i=i, la=la, a=jnp.exp(la), e2=e2, mult=-e2 * inv_mult, inv_mult=inv_mult,
                                 uh=uh, sp=sph)
    return parts


def _rg_fwd_fn(ids, tin, vin):
    (u,) = tin
    wbd = vin[0]
    parts = _rg_gates(u, wbd, vin[1][...], vin[2][...])
    outs = []
    for d in range(2):
        a = jnp.concatenate([parts[(d, h)]["a"] for h in range(2)], axis=1)
        b = jnp.concatenate([parts[(d, h)]["mult"] * parts[(d, h)]["i"] * parts[(d, h)]["uh"]
                             for h in range(2)], axis=1)
        outs += [a, b]
    return outs, []


def _rg_bwd_fn(ids, tin, vin):
    u, da_f, db_f, da_r, db_r = tin
    wbd, lam = vin[0], vin[2][...]
    parts = _rg_gates(u, wbd, vin[1][...], lam)
    dab = ((da_f, db_f), (da_r, db_r))
    dsig_lam = -1.0 / (1.0 + jnp.exp(lam))
    du_halves, dpre_halves, dlam = [], [], [[None, None], [None, None]]
    for h in range(2):
        du = jnp.zeros_like(parts[(0, h)]["uh"])
        dpre = []
        for d in range(2):
            p = parts[(d, h)]
            da = dab[d][0][:, h * RH:(h + 1) * RH]
            db = dab[d][1][:, h * RH:(h + 1) * RH]
            d_mult = db * p["i"] * p["uh"]
            d_i = db * p["mult"] * p["uh"]
            du = du + db * p["mult"] * p["i"]
            d_la = da * p["a"] - d_mult * (p["e2"] + 1.0) * p["inv_mult"]
            d_r = d_la * ((-RG_C) * p["sp"])
            dlam[d][h] = _sum0(d_la * ((-RG_C) * p["r"])) * dsig_lam[d:d + 1, h * RH:(h + 1) * RH]
            dpre += [d_r * p["r"] * (1.0 - p["r"]), d_i * p["i"] * (1.0 - p["i"])]
        dpre = jnp.concatenate(dpre, axis=1)
        du = du + lax.dot_general(dpre.astype(BF16), wbd[h], _NT, preferred_element_type=F32)
        du_halves.append(du)
        dpre_halves.append(dpre)
    dpre_all = jnp.concatenate(dpre_halves, axis=1)
    dlam_row = jnp.concatenate([dlam[0][0], dlam[0][1], dlam[1][0], dlam[1][1]], axis=1)
    return [dpre_all, jnp.concatenate(du_halves, axis=1)], [_sum0(dpre_all), dlam_row]


def _tile_flags(i, n_tiles, seq_starts):
    starts_here = functools.reduce(jnp.logical_or, [i == s for s in seq_starts])
    ends_here = functools.reduce(jnp.logical_or, [i + 1 == s for s in seq_starts] + [i + 1 == n_tiles])
    return jnp.logical_not(starts_here), jnp.logical_not(ends_here)


def _halo_specs(col0, cw):
    hb = ROW_TILE // HALO
    prev = pl.BlockSpec((HALO, cw), lambda i, c: (jnp.maximum(i * hb - 1, 0), col0 + c))
    cur = pl.BlockSpec((ROW_TILE, cw), lambda i, c: (i, col0 + c))
    return prev, cur, hb


def _window(prev_ref, cur_ref, next_ref, has_prev, has_next):
    prev = jnp.where(has_prev, prev_ref[...], 0.0)
    nxt = jnp.where(has_next, next_ref[...], 0.0)
    return jnp.concatenate([prev, cur_ref[...], nxt], axis=0)


def _tap_reader(win):
    sub = 8
    n = win.shape[0]
    shifted = {0: win}

    def tap(off):
        s = off % sub
        if s not in shifted:
            shifted[s] = pltpu.roll(win, n - s, axis=0)
        return shifted[s][off - s:off - s + ROW_TILE, :]

    return tap


def _dwconv(name, x, col0, w, bias, pad_left, seq_starts, n_ch, cw=256):
    n_rows = x.shape[0]
    n_tiles = n_rows // ROW_TILE
    n_taps = w.shape[0]
    prev_spec, cur_spec, hb = _halo_specs(col0, cw)
    last_hb = n_rows // HALO - 1
    next_spec = pl.BlockSpec((HALO, cw), lambda i, c: (jnp.minimum((i + 1) * hb, last_hb), col0 + c))

    def kern(prev_ref, cur_ref, next_ref, w_ref, b_ref, o_ref):
        has_prev, has_next = _tile_flags(pl.program_id(0), n_tiles, seq_starts)
        win = _window(prev_ref, cur_ref, next_ref, has_prev, has_next)
        tap = _tap_reader(win)
        wv = w_ref[...]
        acc = jnp.zeros((ROW_TILE, cw), F32) + b_ref[...]
        for k in range(n_taps):
            acc = acc + wv[k:k + 1, :] * tap(HALO + k - pad_left)
        o_ref[...] = acc

    return _pcall(
        kern, name=name, grid=(n_tiles, n_ch // cw),
        in_specs=[prev_spec, cur_spec, next_spec,
                  pl.BlockSpec((n_taps, cw), lambda i, c: (0, c)), pl.BlockSpec((1, cw), lambda i, c: (0, c))],
        out_specs=pl.BlockSpec((ROW_TILE, cw), lambda i, c: (i, c)),
        out_shape=_sds((n_rows, n_ch), F32), compiler_params=_cparams(),
    )(x, x, x, w, bias)


def _dwconv_wgrad(name, dy, x, col0, n_taps, pad_left, seq_starts, n_ch, cw=256, dep=None):
    deps = [] if dep is None else [dep]
    n_rows = dy.shape[0]
    n_tiles = n_rows // ROW_TILE
    n_out = -(-(n_taps + 1) // 8) * 8
    prev_spec, cur_spec, hb = _halo_specs(col0, cw)
    last_hb = n_rows // HALO - 1
    next_spec = pl.BlockSpec((HALO, cw), lambda c, i: (jnp.minimum((i + 1) * hb, last_hb), col0 + c))
    prev_spec = pl.BlockSpec((HALO, cw), lambda c, i: (jnp.maximum(i * hb - 1, 0), col0 + c))
    cur_spec = pl.BlockSpec((ROW_TILE, cw), lambda c, i: (i, col0 + c))

    def kern(dy_ref, prev_ref, cur_ref, next_ref, *rest):
        o_ref = rest[-1]
        i = pl.program_id(1)
        has_prev, has_next = _tile_flags(i, n_tiles, seq_starts)
        win = _window(prev_ref, cur_ref, next_ref, has_prev, has_next)
        dyv = dy_ref[...]
        tap = _tap_reader(win)
        rid = lax.broadcasted_iota(jnp.int32, (n_out, cw), 0)
        inc = jnp.where(rid == n_taps, _sum0(dyv), 0.0)
        for k in range(n_taps):
            inc = inc + jnp.where(rid == k, _sum0(dyv * tap(HALO + k - pad_left)), 0.0)

        @pl.when(i == 0)
        def _():
            o_ref[...] = jnp.zeros_like(o_ref)

        o_ref[...] += inc

    return _pcall(
        kern, name=name, grid=(n_ch // cw, n_tiles),
        in_specs=[pl.BlockSpec((ROW_TILE, cw), lambda c, i: (i, c)), prev_spec, cur_spec, next_spec]
        + [pl.BlockSpec(d.shape, lambda c, i: (0, 0)) for d in deps],
        out_specs=pl.BlockSpec((n_out, cw), lambda c, i: (0, c)),
        out_shape=_sds((n_out, n_ch), F32), compiler_params=_cparams(),
    )(dy, x, x, x, *deps)


N_SCAN = TA // ROW_TILE


def _rev_block(j):
    return jnp.where(j == 0, 0, N_SCAN - j)


def _scan_fwd(a_f, b_f, a_r, b_r):
    fwd_spec = pl.BlockSpec((ROW_TILE, R), lambda i: (i, 0))
    rev_spec = pl.BlockSpec((ROW_TILE, R), lambda i: (_rev_block(i), 0))
    hin_spec = pl.BlockSpec((None, 1, R), lambda i: (i, 0, 0))

    def kern(af, bf, ar, br, yf, yr, hin_f, hin_r, hf_s, hr_s):
        @pl.when(pl.program_id(0) == 0)
        def _():
            hf_s[...] = jnp.zeros_like(hf_s)
            hr_s[...] = jnp.zeros_like(hr_s)

        hin_f[...] = hf_s[...]
        hin_r[...] = hr_s[...]

        def step(s8, carry):
            hf, hr = carry
            t0 = pl.multiple_of(s8 * 8, 8)
            for q in range(8):
                tf = t0 + q
                hf = af[pl.ds(tf, 1), :] * hf + bf[pl.ds(tf, 1), :]
                yf[pl.ds(tf, 1), :] = hf
                tr = ROW_TILE - 1 - tf
                hr = ar[pl.ds(tr, 1), :] * hr + br[pl.ds(tr, 1), :]
                yr[pl.ds(tr, 1), :] = hr
            return hf, hr

        hf, hr = lax.fori_loop(0, ROW_TILE // 8, step, (hf_s[...], hr_s[...]))
        hf_s[...] = hf
        hr_s[...] = hr

    return _pcall(
        kern, name="scan_fwd", grid=(N_SCAN,),
        in_specs=[fwd_spec, fwd_spec, rev_spec, rev_spec],
        out_specs=[fwd_spec, rev_spec, hin_spec, hin_spec],
        out_shape=[_sds((TA, R), F32), _sds((TA, R), F32), _sds((N_SCAN, 1, R), F32), _sds((N_SCAN, 1, R), F32)],
        scratch_shapes=[pltpu.VMEM((1, R), F32), pltpu.VMEM((1, R), F32)], compiler_params=_cparams(),
    )(a_f, b_f, a_r, b_r)


def _scan_bwd(dy, a_f, y_f, hin_f, a_r, y_r, hin_r):
    fwd_spec = pl.BlockSpec((ROW_TILE, R), lambda i: (N_SCAN - 1 - i, 0))
    rev_spec = pl.BlockSpec((ROW_TILE, R), lambda i: (_rev_block(N_SCAN - 1 - i), 0))
    hin_spec = pl.BlockSpec((None, 1, R), lambda i: (N_SCAN - 1 - i, 0, 0))
    last = ROW_TILE - 1

    def kern(dyf, af, yf, hf0, dyr, ar, yr, hr0, daf, dbf, dar, dbr, gf_s, anf_s, gr_s, anr_s):
        @pl.when(pl.program_id(0) == 0)
        def _():
            for r in (gf_s, anf_s, gr_s, anr_s):
                r[...] = jnp.zeros_like(r)

        def one(dy_ref, a_ref, y_ref, da_ref, db_ref, g, an, p, pprev):
            gnew = dy_ref[pl.ds(p, 1), :] + an * g
            db_ref[pl.ds(p, 1), :] = gnew
            da_ref[pl.ds(p, 1), :] = gnew * y_ref[pl.ds(pprev, 1), :]
            return gnew, a_ref[pl.ds(p, 1), :]

        def step(s8, carry):
            gf, anf, gr, anr = carry
            base = s8 * 8
            for q in range(8):
                s = last - (base + q)
                gf, anf = one(dyf, af, yf, daf, dbf, gf, anf, s, s - 1)
                gr, anr = one(dyr, ar, yr, dar, dbr, gr, anr, last - s, last - s + 1)
            return gf, anf, gr, anr

        carry = (gf_s[...], anf_s[...], gr_s[...], anr_s[...])
        carry = lax.fori_loop(0, ROW_TILE // 8 - 1, step, carry)
        gf, anf, gr, anr = carry
        for s in range(7, 0, -1):
            gf, anf = one(dyf, af, yf, daf, dbf, gf, anf, s, s - 1)
            gr, anr = one(dyr, ar, yr, dar, dbr, gr, anr, last - s, last - s + 1)
        gf0 = dyf[0:1, :] + anf * gf
        dbf[0:1, :] = gf0
        daf[0:1, :] = gf0 * hf0[...]
        gr0 = dyr[last:last + 1, :] + anr * gr
        dbr[last:last + 1, :] = gr0
        dar[last:last + 1, :] = gr0 * hr0[...]
        gf_s[...] = gf0
        anf_s[...] = af[0:1, :]
        gr_s[...] = gr0
        anr_s[...] = ar[last:last + 1, :]

    return _pcall(
        kern, name="scan_bwd", grid=(N_SCAN,),
        in_specs=[fwd_spec, fwd_spec, fwd_spec, hin_spec, rev_spec, rev_spec, rev_spec, hin_spec],
        out_specs=[fwd_spec, fwd_spec, rev_spec, rev_spec],
        out_shape=[_sds((TA, R), F32)] * 4,
        scratch_shapes=[pltpu.VMEM((1, R), F32)] * 4, compiler_params=_cparams(),
    )(dy, a_f, y_f, hin_f, dy, a_r, y_r, hin_r)


def _me():
    return lax.axis_index("x"), lax.axis_index("y"), lax.axis_index("c")


def _other_chips(mx, my):
    return [(1 - mx, my), (mx, 1 - my), (1 - mx, 1 - my)]


def _rcopy(src, dst, ssem, rsem, dev):
    return pltpu.make_async_remote_copy(src_ref=src, dst_ref=dst, send_sem=ssem, recv_sem=rsem,
                                        device_id=dev, device_id_type=MESH)


def _allgather8(name, x, dep=None):
    rows, cols = x.shape
    n_dep = len(_behind(dep))

    def kern(x_ref, *rest):
        o_ref, ssem, rsem, lsem = rest[n_dep:]
        mx, my, mc = _me()
        me = 4 * mx + 2 * my + mc
        peers = []
        for k in range(1, 8):
            px = 1 - mx if (k >> 2) & 1 else mx
            py = 1 - my if (k >> 1) & 1 else my
            pc = 1 - mc if k & 1 else mc
            peers.append((px, py, pc))
        mine = pltpu.make_async_copy(x_ref, o_ref.at[me], lsem)
        mine.start()
        sends = [_rcopy(x_ref, o_ref.at[me], ssem.at[k], rsem.at[k], p) for k, p in enumerate(peers)]
        for cp in sends:
            cp.start()
        for k, (px, py, pc) in enumerate(peers):
            _rcopy(x_ref, o_ref.at[4 * px + 2 * py + pc], ssem.at[k], rsem.at[k], (px, py, pc)).wait_recv()
        for cp in sends:
            cp.wait_send()
        mine.wait()

    return _pcall(
        kern, name=name, in_specs=[ANY] * (1 + n_dep), out_specs=ANY, out_shape=_sds((8, rows, cols), F32),
        scratch_shapes=[pltpu.SemaphoreType.DMA((7,)), pltpu.SemaphoreType.DMA((7,)), pltpu.SemaphoreType.DMA(())],
    )(x, *_behind(dep))


def _gather8_start(name, x, after):
    def kern(x_in, after_ref, x_ref, o_ref, ssem, rsem, token):
        mx, my, mc = _me()
        me = 4 * mx + 2 * my + mc
        for k, p in enumerate(_peers7(mx, my, mc)):
            _rcopy(x_ref, o_ref.at[me], ssem.at[k], rsem.at[k], p).start()
        token[...] = jnp.zeros_like(token)

    dma = pltpu.SemaphoreType.DMA
    res = _pcall(
        kern, name=name, in_specs=[ANY, ANY],
        out_specs=[ANY, ANY, SEM, SEM, pl.BlockSpec(memory_space=pltpu.VMEM)],
        out_shape=[_sds(x.shape, x.dtype), _sds((8,) + x.shape, x.dtype), dma((7,)), dma((7,)), _sds((8, LANE), F32)],
        input_output_aliases={0: 0}, compiler_params=pltpu.CompilerParams(has_side_effects=_DATAFLOW),
    )(x, after)
    return tuple(res[:4]), res[4]


def _gather8_wait(name, x, out, ssem, rsem, after):
    def kern(x_ref, o_ref, ssem_ref, rsem_ref, after_ref, x_out, o_out):
        mx, my, mc = _me()
        for k, (px, py, pc) in enumerate(_peers7(mx, my, mc)):
            cp = _rcopy(x_ref, o_ref.at[4 * px + 2 * py + pc], ssem_ref.at[k], rsem_ref.at[k], (px, py, pc))
            cp.wait_recv()
            cp.wait_send()

    res = _pcall(
        kern, name=name, in_specs=[ANY, ANY, SEM, SEM, ANY], out_specs=[ANY, ANY],
        out_shape=[_sds(x.shape, x.dtype), _sds(out.shape, out.dtype)], input_output_aliases={0: 0, 1: 1},
        compiler_params=pltpu.CompilerParams(has_side_effects=_DATAFLOW),
    )(x, out, ssem, rsem, after)
    return res[0], res[1]


def _peers7(mx, my, mc):
    peers = []
    for k in range(1, 8):
        peers.append((1 - mx if (k >> 2) & 1 else mx, 1 - my if (k >> 1) & 1 else my, 1 - mc if k & 1 else mc))
    return peers


def _share_halves(name, fulls):
    n = len(fulls)

    def kern(*refs):
        o = refs[n:2 * n]
        ss, rs = refs[2 * n:]
        mx, my, mc = _me()
        sib = (mx, my, 1 - mc)
        sends = []
        for t in range(n):
            cp = _rcopy(o[t].at[mc], o[t].at[mc], ss.at[t], rs.at[t], sib)
            cp.start()
            sends.append(cp)
        for t in range(n):
            _rcopy(o[t].at[1 - mc], o[t].at[1 - mc], ss.at[t], rs.at[t], sib).wait_recv()
        for cp in sends:
            cp.wait_send()

    dma = pltpu.SemaphoreType.DMA
    return _pcall(
        kern, name=name, in_specs=[ANY] * n, out_specs=[ANY] * n,
        out_shape=[_sds(f.shape, f.dtype) for f in fulls], input_output_aliases={t: t for t in range(n)},
        scratch_shapes=[dma((n,)), dma((n,))],
    )(*fulls)


def _tiled_sp(name, fn, grid, sp, ins, outs):
    n_in = len(ins)

    def kern(sp_ref, *refs):
        tout = fn([r[...] for r in refs[:n_in]])
        for r, v in zip(refs[n_in:], tout):
            r[...] = v.astype(r.dtype)

    gs = pltpu.PrefetchScalarGridSpec(num_scalar_prefetch=1, grid=tuple(grid),
                                      in_specs=[s for _, s in ins], out_specs=[s for _, s in outs])
    res = _pcall(kern, name=name, grid_spec=gs, out_shape=[o for o, _ in outs], compiler_params=_cparams(),
                 )(sp, *[a for a, _ in ins])
    return list(res)


def _row_tile(rows, cols, itemsize=4, budget=2 * 1024 * 1024):
    tr = rows
    while tr * cols * itemsize > budget and tr % 32 == 0:
        tr //= 2
    return tr


def _place_big(shards, place):
    slots = []
    for t, (s, layer) in enumerate(shards):
        rr, cc = s.shape[2], s.shape[3]
        tr = _row_tile(rr, cc)
        (slot,) = _tiled_sp(
            f"place{t}", lambda tin: [tin[0]], (2, rr // tr), place,
            [(s, pl.BlockSpec((None, None, tr, cc), lambda h, i, sp, layer=layer: (layer, h, i, 0)))],
            [(_sds((4, 2, rr, cc), BF16), pl.BlockSpec((None, None, tr, cc), lambda h, i, sp: (sp[0], h, i, 0)))])
        slots.append(slot)
    return slots


def _allreduce_small_begin(vec, place, after):
    hr = vec.shape[0] // 2
    tr = _row_tile(hr, LANE)
    blk = (None, None, tr, LANE)
    (pair,) = _tiled_sp(
        "small_place", lambda tin: [tin[0]], (2, hr // tr), place,
        [(vec.reshape(2, hr, LANE), pl.BlockSpec((None, tr, LANE), lambda h, i, sp: (h, i, 0)))],
        [(_sds((2, 2, hr, LANE), F32), pl.BlockSpec(blk, lambda h, i, sp: (sp[1], h, i, 0)))])
    (pair,) = _share_halves("small_share", [pair])
    (slot,) = _tiled_sp(
        "small_pair_add", lambda tin: [tin[0] + tin[1]], (2, hr // tr), place,
        [(pair, pl.BlockSpec(blk, lambda h, i, sp: (0, h, i, 0))),
         (pair, pl.BlockSpec(blk, lambda h, i, sp: (1, h, i, 0)))],
        [(_sds((4, 2, hr, LANE), F32), pl.BlockSpec(blk, lambda h, i, sp: (sp[0], h, i, 0)))])
    fly, sems, token = _gather_start("small_start", [slot], ((0,),), after)
    return (fly, sems), token


def _allreduce_small_end(state, after):
    fly, sems = state
    (chips,) = _swap_halves("small_swap", _gather_wait("small_wait", fly, *sems, after))
    hr = chips.shape[2]
    tr = _row_tile(hr, LANE)
    blk = (None, None, tr, LANE)
    (total,) = _tiled(
        "small_chip_sum", lambda ids, tin, vin: ([((tin[0] + tin[1]) + tin[2]) + tin[3]], []), (2, hr // tr),
        [(chips, pl.BlockSpec(blk, lambda h, i, _j=j: (_j, h, i, 0))) for j in range(4)], [],
        [(_sds((2, hr, LANE), F32), pl.BlockSpec((None, tr, LANE), lambda h, i: (h, i, 0)))])
    return total.reshape(2 * hr, LANE)


SEM =pl.BlockSpec(memory_space=pltpu.SEMAPHORE)
_DATAFLOW = pltpu.SideEffectType.DATAFLOW_SIDE_EFFECTING


def _gather_start(name, slots, groups, after):
    n = len(slots)

    def kern(*refs):
        o = refs[n + 1:2 * n + 1]
        sems, token = refs[2 * n + 1:-1], refs[-1]
        mx, my, mc = _me()
        j0 = 2 * mx + my
        for gi, grp in enumerate(groups):
            for k, t in enumerate(grp):
                for q, (qx, qy) in enumerate(_other_chips(mx, my)):
                    _rcopy(o[t].at[j0, mc], o[t].at[j0, mc], sems[2 * gi].at[3 * k + q],
                           sems[2 * gi + 1].at[3 * k + q], (qx, qy, mc)).start()
        token[...] = jnp.zeros_like(token)

    sem_shapes = []
    for grp in groups:
        sem_shapes += [pltpu.SemaphoreType.DMA((3 * len(grp),))] * 2
    res = _pcall(
        kern, name=name, in_specs=[ANY] * (n + 1),
        out_specs=[ANY] * n + [SEM] * len(sem_shapes) + [pl.BlockSpec(memory_space=pltpu.VMEM)],
        out_shape=[_sds(w.shape, w.dtype) for w in slots] + sem_shapes + [_sds((8, LANE), F32)],
        input_output_aliases={t: t for t in range(n)},
        compiler_params=pltpu.CompilerParams(has_side_effects=_DATAFLOW),
    )(*slots, after)
    return list(res[:n]), list(res[n:-1]), res[-1]


def _gather_wait(name, bufs, ssem, rsem, after):
    n = len(bufs)

    def kern(*refs):
        b = refs[:n]
        ssem_ref, rsem_ref = refs[n], refs[n + 1]
        mx, my, mc = _me()
        j0 = 2 * mx + my
        for k in range(n):
            for q, (qx, qy) in enumerate(_other_chips(mx, my)):
                jq = 2 * qx + qy
                _rcopy(b[k].at[jq, mc], b[k].at[jq, mc], ssem_ref.at[3 * k + q], rsem_ref.at[3 * k + q],
                       (qx, qy, mc)).wait_recv()
                _rcopy(b[k].at[j0, mc], b[k].at[j0, mc], ssem_ref.at[3 * k + q], rsem_ref.at[3 * k + q],
                       (qx, qy, mc)).wait_send()

    return list(_pcall(
        kern, name=name, in_specs=[ANY] * n + [SEM, SEM, ANY], out_specs=[ANY] * n,
        out_shape=[_sds(w.shape, w.dtype) for w in bufs], input_output_aliases={k: k for k in range(n)},
        compiler_params=pltpu.CompilerParams(has_side_effects=_DATAFLOW),
    )(*bufs, ssem, rsem, after))


def _swap_halves(name, bufs):
    n = len(bufs)

    def kern(*refs):
        o = refs[n:2 * n]
        ss, rs = refs[2 * n:]
        mx, my, mc = _me()
        sib = (mx, my, 1 - mc)
        sends = []
        for k in range(n):
            for q, (qx, qy) in enumerate(_other_chips(mx, my)):
                jq = 2 * qx + qy
                cp = _rcopy(o[k].at[jq, mc], o[k].at[jq, mc], ss.at[3 * k + q], rs.at[3 * k + q], sib)
                cp.start()
                sends.append(cp)
        for k in range(n):
            for q, (qx, qy) in enumerate(_other_chips(mx, my)):
                jq = 2 * qx + qy
                _rcopy(o[k].at[jq, 1 - mc], o[k].at[jq, 1 - mc], ss.at[3 * k + q], rs.at[3 * k + q], sib).wait_recv()
        for cp in sends:
            cp.wait_send()

    dma = pltpu.SemaphoreType.DMA
    return list(_pcall(
        kern, name=name, in_specs=[ANY] * n, out_specs=[ANY] * n,
        out_shape=[_sds(w.shape, w.dtype) for w in bufs], input_output_aliases={k: k for k in range(n)},
        scratch_shapes=[dma((3 * n,)), dma((3 * n,))],
    )(*bufs))


def _swap_start(name, bufs, after):
    n = len(bufs)

    def kern(*refs):
        o = refs[n + 1:2 * n + 1]
        ssem, rsem, token = refs[2 * n + 1:]
        mx, my, mc = _me()
        for k in range(n):
            for q, (qx, qy) in enumerate(_other_chips(mx, my)):
                jq = 2 * qx + qy
                _rcopy(o[k].at[jq, mc], o[k].at[jq, mc], ssem.at[3 * k + q], rsem.at[3 * k + q], (mx, my, 1 - mc)).start()
        token[...] = jnp.zeros_like(token)

    dma = pltpu.SemaphoreType.DMA
    res = _pcall(
        kern, name=name, in_specs=[ANY] * (n + 1),
        out_specs=[ANY] * n + [SEM, SEM, pl.BlockSpec(memory_space=pltpu.VMEM)],
        out_shape=[_sds(w.shape, w.dtype) for w in bufs] + [dma((3 * n,)), dma((3 * n,)), _sds((8, LANE), F32)],
        input_output_aliases={k: k for k in range(n)},
        compiler_params=pltpu.CompilerParams(has_side_effects=_DATAFLOW),
    )(*bufs, after)
    return (list(res[:n]), res[n], res[n + 1]), res[n + 2]


def _swap_wait(name, bufs, ssem, rsem, after):
    n = len(bufs)

    def kern(*refs):
        b = refs[:n]
        ssem_ref, rsem_ref = refs[n], refs[n + 1]
        mx, my, mc = _me()
        sib = (mx, my, 1 - mc)
        for k in range(n):
            for q, (qx, qy) in enumerate(_other_chips(mx, my)):
                jq = 2 * qx + qy
                _rcopy(b[k].at[jq, 1 - mc], b[k].at[jq, 1 - mc], ssem_ref.at[3 * k + q], rsem_ref.at[3 * k + q],
                       sib).wait_recv()
                _rcopy(b[k].at[jq, mc], b[k].at[jq, mc], ssem_ref.at[3 * k + q], rsem_ref.at[3 * k + q],
                       sib).wait_send()

    return list(_pcall(
        kern, name=name, in_specs=[ANY] * n + [SEM, SEM, ANY], out_specs=[ANY] * n,
        out_shape=[_sds(w.shape, w.dtype) for w in bufs], input_output_aliases={k: k for k in range(n)},
        compiler_params=pltpu.CompilerParams(has_side_effects=_DATAFLOW),
    )(*bufs, ssem, rsem, after))


def _to_sibling(mx, my, mc):
    return [((j, 1 - mc), j, (mx, my, 1 - mc)) for j in range(4)]


def _to_chips(mx, my, mc):
    return [((2 * qx + qy,), q, (qx, qy, mc)) for q, (qx, qy) in enumerate(_other_chips(mx, my))]


def _send_start(name, srcs, plan, land_shapes, after):
    n = len(srcs)
    per = len(plan(0, 0, 0))

    def kern(*refs):
        s, land = refs[n + 1:2 * n + 1], refs[2 * n + 1:3 * n + 1]
        ssem, rsem, token = refs[3 * n + 1:]
        for k in range(n):
            for q, (idx, slot, dev) in enumerate(plan(*_me())):
                _rcopy(s[k].at[idx], land[k].at[slot], ssem.at[per * k + q], rsem.at[per * k + q], dev).start()
        token[...] = jnp.zeros_like(token)

    dma = pltpu.SemaphoreType.DMA
    res = _pcall(
        kern, name=name, in_specs=[ANY] * (n + 1),
        out_specs=[ANY] * (2 * n) + [SEM, SEM, pl.BlockSpec(memory_space=pltpu.VMEM)],
        out_shape=[_sds(s.shape, s.dtype) for s in srcs] + [_sds(ls, s.dtype) for ls, s in zip(land_shapes, srcs)]
        + [dma((per * n,)), dma((per * n,)), _sds((8, LANE), F32)],
        input_output_aliases={k: k for k in range(n)},
        compiler_params=pltpu.CompilerParams(has_side_effects=_DATAFLOW),
    )(*srcs, after)
    return (list(res[:n]), list(res[n:2 * n]), res[2 * n], res[2 * n + 1]), res[2 * n + 2]


def _send_wait(name, srcs, lands, ssem, rsem, plan, after):
    n = len(srcs)
    per = len(plan(0, 0, 0))

    def kern(*refs):
        s, land = refs[:n], refs[n:2 * n]
        ssem_ref, rsem_ref = refs[2 * n], refs[2 * n + 1]
        for k in range(n):
            for q, (idx, slot, dev) in enumerate(plan(*_me())):
                cp = _rcopy(s[k].at[idx], land[k].at[slot], ssem_ref.at[per * k + q], rsem_ref.at[per * k + q], dev)
                cp.wait_recv()
                cp.wait_send()

    res = _pcall(
        kern, name=name, in_specs=[ANY] * (2 * n) + [SEM, SEM, ANY], out_specs=[ANY] * (2 * n),
        out_shape=[_sds(a.shape, a.dtype) for a in list(srcs) + list(lands)],
        input_output_aliases={k: k for k in range(2 * n)},
        compiler_params=pltpu.CompilerParams(has_side_effects=_DATAFLOW),
    )(*srcs, *lands, ssem, rsem, after)
    return list(res[:n]), list(res[n:])


def _reduce_begin(tag, parts, after):
    return _send_start(f"pair_start_{tag}", parts, _to_sibling, [(4,) + p.shape[2:] for p in parts], after)


def _reduce_mid(tag, pairing, place, after):
    parts, theirs = _send_wait(f"pair_wait_{tag}", *pairing, _to_sibling, after)
    sums = []
    for k, (p, o) in enumerate(zip(parts, theirs)):
        rr, cc = p.shape[2], p.shape[3]
        tr = _row_tile(rr, cc)
        (s_k,) = _tiled_sp(
            f"pair_add_{tag}{k}", lambda tin: [tin[0].astype(F32) + tin[1].astype(F32)], (4, rr // tr), place,
            [(p, pl.BlockSpec((None, None, tr, cc), lambda j, i, sp: (j, sp[1], i, 0))),
             (o, pl.BlockSpec((None, tr, cc), lambda j, i, sp: (j, i, 0)))],
            [(_sds((4, rr, cc), BF16), pl.BlockSpec((None, tr, cc), lambda j, i, sp: (j, i, 0)))])
        sums.append(s_k)
    return _send_start(f"chips_start_{tag}", sums, _to_chips, [(3,) + s.shape[1:] for s in sums], theirs[0])


def _reduce_end(tag, flying, place, after):
    sums, lands = _send_wait(f"chips_wait_{tag}", *flying, _to_chips, after)
    fulls = []
    for k, (s, q) in enumerate(zip(sums, lands)):
        rr, cc = q.shape[1], q.shape[2]
        tr = _row_tile(rr, cc)

        def add4(tin):
            return [((tin[0].astype(F32) + tin[1].astype(F32)) + tin[2].astype(F32)) + tin[3].astype(F32)]

        ins = [(s, pl.BlockSpec((None, tr, cc), lambda i, sp: (sp[0], i, 0)))]
        ins += [(q, pl.BlockSpec((None, tr, cc), lambda i, sp, _k=kk: (_k, i, 0))) for kk in range(3)]
        (f_k,) = _tiled_sp(f"chip_add_{tag}{k}", add4, (rr // tr,), place, ins,
                           [(_sds((2, rr, cc), F32), pl.BlockSpec((None, tr, cc), lambda i, sp: (sp[1], i, 0)))])
        fulls.append(f_k)
    return fulls


def _pack(parts, PACK_ROWS=PACK_ROWS):
    flat, offs, pos = [], [], 0
    for p in parts:
        v = p.reshape(-1).astype(F32)
        n = -(-v.shape[0] // LANE) * LANE
        flat.append(jnp.pad(v, (0, n - v.shape[0])))
        offs.append((pos, v.shape[0], p.shape))
        pos += n
    total = -(-pos // (PACK_ROWS * LANE)) * PACK_ROWS * LANE
    flat.append(jnp.zeros((total - pos,), F32))
    return jnp.concatenate(flat).reshape(-1, LANE), offs


def _unpack(vec, offs):
    v = vec.reshape(-1)
    return [v[p:p + n].reshape(shape) for p, n, shape in offs]


def _adamw_math(wv, gv, mv, vv):
    bc1 = 1.0 - ADAM_B1 ** ADAM_STEP
    bc2 = 1.0 - ADAM_B2 ** ADAM_STEP
    mn = ADAM_B1 * mv + (1.0 - ADAM_B1) * gv
    vn = ADAM_B2 * vv + (1.0 - ADAM_B2) * (gv * gv)
    delta = -ADAM_LR * ((mn / bc1) / (jnp.sqrt(vn / bc2) + ADAM_EPS) + ADAM_WD * wv)
    return delta, mn, vn


def _adamw(name, w, g, m, v):
    rows, cols = w.shape
    tr = rows
    for cand in (512, 256, 128, 64, 32, 16, 8):
        if rows % cand == 0 and cand * cols * 4 <= 2 * 1024 * 1024:
            tr = cand
            break

    def fn(ids, tin, vin):
        return list(_adamw_math(*tin)), []

    spec = pl.BlockSpec((tr, cols), lambda i: (i, 0))
    outs = [(_sds((rows, cols), F32), spec)] * 3
    return _tiled(name, fn, (rows // tr,), [(a, spec) for a in (w, g, m, v)], [], outs)


def _adamw_many(name, ws, gs, ms, vs):
    n = len(ws)
    views = [(-1, a.shape[-1]) if a.ndim > 1 else (1, -1) for a in ws]
    flat = lambda arrs: [a.reshape(vw) for a, vw in zip(arrs, views)]

    def kern(*refs):
        ins, outs = refs[:4 * n], refs[4 * n:]
        for t in range(n):
            res = _adamw_math(*[ins[q * n + t][...] for q in range(4)])
            for q in range(3):
                outs[q * n + t][...] = res[q]

    shapes = [_sds(a.shape, F32) for a in flat(ws)]
    res = _pcall(kern, name=name, out_shape=shapes * 3, compiler_params=_cparams(),
                 )(*flat(ws), *flat(gs), *flat(ms), *flat(vs))
    back = lambda part: [a.reshape(w.shape) for a, w in zip(part, ws)]
    return back(res[:n]), back(res[n:2 * n]), back(res[2 * n:])


def _pos_embed():
    n_rows = T // GRID_W
    q = D // 4
    omega = 1.0 / (10000.0 ** (jnp.arange(q, dtype=F32) / q))
    er = jnp.arange(n_rows, dtype=jnp.int32).astype(F32)[:, None] * omega[None, :]
    ec = jnp.arange(GRID_W, dtype=jnp.int32).astype(F32)[:, None] * omega[None, :]
    by_row = jnp.concatenate([jnp.sin(er), jnp.cos(er)], axis=-1)
    by_col = jnp.concatenate([jnp.sin(ec), jnp.cos(ec)], axis=-1)
    return jnp.concatenate([jnp.repeat(by_row, GRID_W, axis=0), jnp.tile(by_col, (n_rows, 1))], axis=-1)


def _dense_gates(w_a, w_x):
    rows = jnp.stack([w_a[0], w_x[0], w_a[1], w_x[1]]).reshape(4, 2, RH, BLK)
    mask, spread = _block_mask(), _block_spread().T.astype(BF16)

    def kern(r_ref, m_ref, s_ref, o_ref):
        tiled = jnp.dot(r_ref[...].astype(BF16), s_ref[...], preferred_element_type=F32)
        o_ref[...] = (tiled * m_ref[...]).astype(o_ref.dtype)

    return _pcall(
        kern, name="gates_dense", grid=(2, 4),
        in_specs=[pl.BlockSpec((None, None, RH, BLK), lambda h, q: (q, h, 0, 0)),
                  pl.BlockSpec((RH, RH), lambda h, q: (0, 0)), pl.BlockSpec((BLK, RH), lambda h, q: (0, 0))],
        out_specs=pl.BlockSpec((None, RH, RH), lambda h, q: (h, 0, q)),
        out_shape=_sds((2, RH, NQ), BF16),
    )(rows, mask, spread)


def _block_mask():
    r = lax.broadcasted_iota(jnp.int32, (RH, RH), 0) // BLK
    c = lax.broadcasted_iota(jnp.int32, (RH, RH), 1) // BLK
    return (r == c).astype(F32)


def _block_spread():
    c = lax.broadcasted_iota(jnp.int32, (RH, BLK), 0) % BLK
    j = lax.broadcasted_iota(jnp.int32, (RH, BLK), 1)
    return (c == j).astype(F32)


def _fold_blocks(dense, mask, spread):
    return jnp.dot(dense * mask, spread, preferred_element_type=F32, precision=lax.Precision.HIGHEST)


def _gate_block_grads(folded):
    per = N_BLK // 2
    kinds = [jnp.concatenate([folded[h, q].reshape(per, BLK, BLK) for h in range(2)], axis=0) for q in range(4)]
    return jnp.stack([kinds[0], kinds[2]]), jnp.stack([kinds[1], kinds[3]])


def _gate_bias_dense(b_a, b_x):
    cols = []
    for h in range(2):
        for src in (b_a[0], b_x[0], b_a[1], b_x[1]):
            cols.append(src.reshape(R)[h * RH:(h + 1) * RH])
    return jnp.concatenate(cols).reshape(1, 2 * NQ)


def _gate_bias_grads(dgb):
    v = dgb.reshape(2, 4, RH)
    kinds = [jnp.concatenate([v[0, q], v[1, q]]).reshape(N_BLK, BLK) for q in range(4)]
    return jnp.stack([kinds[0], kinds[2]]), jnp.stack([kinds[1], kinds[3]])


def _residual_epilogue(next_norm):
    def epi(acc, ex):
        x_new = ex[0] + ex[1] * acc
        outs = [acc, x_new]
        if next_norm:
            outs.append(_norm_mod(x_new, ex[-3], ex[-2], ex[-1]))
        return outs
    return epi


def _mlp_fwd(tag, x_in, h, gate, w_in, w_out, next_norm=None, dep=None):
    tm = MM_TILE
    (r,) = _mm(f"{tag}_in", h, w_in, _NN, (T // tm, 4, 1),
               pl.BlockSpec((tm, D), lambda i, j, k: (i, 0)), pl.BlockSpec((None, D, D), lambda i, j, k: (j, 0, 0)),
               [(_sds((T, FF), BF16), pl.BlockSpec((tm, D), lambda i, j, k: (i, j)))], (tm, D),
               extra=[(d_, _full_spec(d_)) for d_ in _behind(dep)], epi=lambda acc, ex: [jnp.maximum(acc, 0.0)])
    row_spec = pl.BlockSpec((tm, D), lambda i, j, k: (i, 0))
    outs = [(_sds((T, D), F32), row_spec)] * 2 + ([(_sds((T, D), BF16), row_spec)] if next_norm else [])
    res = _mm(f"{tag}_out", r, w_out, _NN, (T // tm, 1, FF // D),
              pl.BlockSpec((tm, D), lambda i, j, k: (i, k)), pl.BlockSpec((D, D), lambda i, j, k: (k, 0)),
              outs, (tm, D),
              extra=[(x_in, row_spec), (gate, _full_spec(gate))] + [(v, _full_spec(v)) for v in next_norm or ()],
              a_pre=lambda a: a * a, epi=_residual_epilogue(next_norm))
    return dict(h=h, r=r, o=res[0], x_in=x_in), res[1], (res[2] if next_norm else None)


def _behind(dep):
    return [] if dep is None else [dep]


def _gate_bwd(tag, dx, o, gate, dep=None):
    def fn(ids, t, v):
        d_o = t[0] * v[0]
        return [d_o], [_sum0(t[0] * t[1]), _sum0(d_o)]
    return _tiled(f"{tag}_gate_bwd", fn, (T // ROW_TILE,), [_rows(dx), _rows(o)], [gate] + _behind(dep),
                  [_orow(T, D, BF16)], [(1, D), (1, D)])


def _norm_bwd(tag, dx_res, dh, dh_off, x, g_norm, sc, with_dx=True, dep=None):
    n_t = x.shape[0] // ROW_TILE

    def fn(ids, t, v):
        if with_dx:
            dres, dhv, xv = t
        else:
            dhv, xv = t
        dxv, d_sh, d_sc, d_g = _norm_mod_bwd(dhv, xv, v[0], v[1])
        return ([dres + dxv] if with_dx else []), [d_sh, d_sc, d_g]

    ins = ([_rows(dx_res)] if with_dx else []) + [_rows(dh, off=dh_off), _rows(x)]
    outs = [_orow(x.shape[0], D, F32)] if with_dx else []
    return _tiled(f"{tag}_norm_bwd", fn, (n_t,), ins, [g_norm, sc] + _behind(dep), outs, [(1, D)] * 3)


def _mlp_bwd(tag, dx, saved, g_norm, sc, gate, w_in, w_out, dep=None):
    d_o, d_gate, _ = _gate_bwd(tag, dx, saved["o"], gate, dep)
    tm = MM_TILE
    r = saved["r"]
    (da,) = _mm(f"{tag}_dz", d_o, w_out, _NT, (T // tm, FF // D, 1),
                pl.BlockSpec((tm, D), lambda i, j, k: (i, 0)), pl.BlockSpec((D, D), lambda i, j, k: (j, 0)),
                [(_sds((T, FF), BF16), pl.BlockSpec((tm, D), lambda i, j, k: (i, j)))], (tm, D),
                extra=[(r, pl.BlockSpec((tm, D), lambda i, j, k: (i, j)))],
                epi=lambda acc, ex: [acc * (2.0 * ex[0].astype(F32))])
    tk = MM_TILE
    (dw_out,) = _mm(f"{tag}_dwout", r, d_o, _TN, (FF // tm, 1, T // tk),
                    pl.BlockSpec((tk, tm), lambda i, j, k: (k, i)), pl.BlockSpec((tk, D), lambda i, j, k: (k, 0)),
                    [(_sds((FF, D), BF16), pl.BlockSpec((tm, D), lambda i, j, k: (i, 0)))], (tm, D),
                    a_pre=lambda a: a * a)
    (dh,) = _mm(f"{tag}_dh", da, w_in, _NT, (T // tm, 1, 4),
                pl.BlockSpec((tm, D), lambda i, j, k: (i, k)), pl.BlockSpec((None, D, D), lambda i, j, k: (k, 0, 0)),
                [(_sds((T, D), F32), pl.BlockSpec((tm, D), lambda i, j, k: (i, 0)))], (tm, D))
    (dw_in,) = _mm(f"{tag}_dwin", saved["h"], da, _TN, (D // tm, 4, T // tk),
                   pl.BlockSpec((tk, tm), lambda i, j, k: (k, i)), pl.BlockSpec((tk, D), lambda i, j, k: (k, j)),
                   [(_sds((4, D, D), BF16), pl.BlockSpec((None, tm, D), lambda i, j, k: (j, i, 0)))], (tm, D))
    dx_in, d_sh, d_sc, d_g = _norm_bwd(tag, dx, dh, 0, saved["x_in"], g_norm, sc)
    return dx_in, dw_in, dw_out, dict(sh=d_sh, sc=d_sc, gate=d_gate, g_norm=d_g)


def _local_step(x, ctx, tgt, mods, cmods, norm_g, final_g, rec, conf, wg, on_grads=None, wg_pre=None, on_later=None):
    on_grads = on_grads or (lambda group, dws: None)
    wg_pre = wg_pre or (lambda group, after: None)
    on_later = on_later or (lambda after: None)
    n_t = T // ROW_TILE
    row = lambda v: v.reshape(1, -1)
    m0 = [row(mods[0, q]) for q in range(6)]
    m1 = [row(mods[1, q]) for q in range(6)]
    g00, g01, g10, g11 = (row(norm_g[0, 0]), row(norm_g[0, 1]), row(norm_g[1, 0]), row(norm_g[1, 1]))
    csh, csc = row(cmods[0]), row(cmods[1])
    pos = _pos_embed()

    def prep0(ids, t, v):
        cx, xv, pv = t
        is_ctx = ids[0] == 0
        xin = jnp.where(is_ctx, cx, xv + pv)
        sh = jnp.where(is_ctx, v[3], v[1])
        sc = jnp.where(is_ctx, v[4], v[2])
        return [_norm_mod(xin, v[0], sc, sh), xv + pv], []

    hcat, x0 = _tiled(
        "prep0", prep0, (N_SCAN,),
        [(ctx, pl.BlockSpec((ROW_TILE, D), lambda i: (0, 0))), _rows(x, off=-1, clamp_lo=True),
         _rows(pos, off=-1, clamp_lo=True)],
        [g00, m0[0], m0[1], csh, csc],
        [_orow(TA, D, BF16), _orow(T, D, F32, off=-1, clamp_lo=True)])

    tm_a = REC_TILE
    w_rec = wg("rec", hcat)
    (a_in,) = _mm("rec_in", hcat, w_rec["rec_w_in"], _NN, (TA // tm_a, 4, 1),
                  pl.BlockSpec((tm_a, D), lambda i, j, k: (i, 0)),
                  pl.BlockSpec((None, D, RH), lambda i, j, k: (j, 0, 0)),
                  [(_sds((TA, 2 * R), F32), pl.BlockSpec((tm_a, RH), lambda i, j, k: (i, j)))], (tm_a, RH))
    rec_starts = (0, 1)
    u = _dwconv("rec_conv", a_in, R // CW_REC, rec["conv_w"], row(rec["conv_b"]), 1, rec_starts, R, CW_REC)
    wbd = _dense_gates(rec["w_a"], rec["w_x"])
    gbias = _gate_bias_dense(rec["b_a"], rec["b_x"])
    lam = rec["lam"]
    a_f, b_f, a_r, b_r = _tiled("rg_fwd", _rg_fwd_fn, (TA // RG_TILE,), [_rows(u, tm=RG_TILE)], [wbd, gbias, lam],
                                [_orow(TA, R, F32, tm=RG_TILE)] * 4, vec_refs=True)
    dep = wg_pre("mlp0", a_f)
    y_f, y_r, hin_f, hin_r = _scan_fwd(a_f, b_f, a_r, b_r)

    def rec_mid(ids, t, v):
        gp, yf, yr = t
        g, _ = _gelu(gp)
        return [g * (yf + yr)], []

    (m_rec,) = _tiled("rec_mid", rec_mid, (n_t,),
                      [_rows(a_in, R, off=1), _rows(y_f, off=1), _rows(y_r, off=1)], _behind(dep),
                      [_orow(T, R, BF16)])
    tm = MM_TILE
    row_spec = pl.BlockSpec((tm, D), lambda i, j, k: (i, 0))
    norm_mlp0 = (g01, m0[4], m0[3])
    o_rec, x1, h_mlp0 = _mm(
        "rec_out", m_rec, w_rec["rec_w_out"], _NN, (T // tm, 1, 1),
        pl.BlockSpec((tm, R), lambda i, j, k: (i, 0)), pl.BlockSpec((R, D), lambda i, j, k: (0, 0)),
        [(_sds((T, D), F32), row_spec)] * 2 + [(_sds((T, D), BF16), row_spec)], (tm, D),
        extra=[(x0, row_spec), (m0[2], _full_spec(m0[2]))] + [(v, _full_spec(v)) for v in norm_mlp0],
        epi=_residual_epilogue(norm_mlp0))
    w_m0 = wg("mlp0", x1)
    dep = wg_pre("conf", x1)
    mlp0, x2, h1 = _mlp_fwd("mlp0", x1, h_mlp0, m0[5], w_m0["w_in"], w_m0["w_out"], (g10, m1[1], m1[0]), dep)

    b_pw1 = row(conf["b_pw1"])
    w_cf = wg("conf", x2)
    dep = wg_pre("mlp1", x2)
    (pre,) = _mm("conf_pw1", h1, w_cf["conf_w_pw1"], _NN, (T // tm, 4, 1),
                 pl.BlockSpec((tm, D), lambda i, j, k: (i, 0)),
                 pl.BlockSpec((None, D, D // 2), lambda i, j, k: (j, 0, 0)),
                 [(_sds((T, 2 * D), F32), pl.BlockSpec((tm, D // 2), lambda i, j, k: (i, j)))], (tm, D // 2),
                 extra=[(b_pw1, pl.BlockSpec((1, D // 2), lambda i, j, k: (0, j)))]
                 + [(d_, _full_spec(d_)) for d_ in _behind(dep)],
                 epi=lambda acc, ex: [acc + ex[0]])
    (zg,) = _tiled("conf_glu", lambda ids, t, v: ([t[0] * _sigmoid(t[1])], []), (n_t,),
                   [_rows(pre, D, col=0), _rows(pre, D, col=1)], [], [_orow(T, D, F32)])
    conf_starts = (0,)
    zc = _dwconv("conf_conv", zg, 0, conf["conv_w"], row(conf["conv_b"]), CONF_KW // 2, conf_starts, D, CW_CONF)
    ln_g, ln_b = row(conf["ln_g"]), row(conf["ln_b"])

    def ln_silu(ids, t, v):
        nh, _ = _layernorm_parts(t[0])
        ln = nh * v[0] + v[1]
        return [ln * _sigmoid(ln)], []

    (s_conf,) = _tiled("conf_ln", ln_silu, (n_t,), [_rows(zc)], [ln_g, ln_b], [_orow(T, D, BF16)])
    b_pw2 = row(conf["b_pw2"])
    norm_mlp1 = (g11, m1[4], m1[3])
    pw2_epi = _residual_epilogue(norm_mlp1)
    y_conf, x3, h_mlp1 = _mm(
        "conf_pw2", s_conf, w_cf["conf_w_pw2"], _NN, (T // tm, 1, 1),
        row_spec, pl.BlockSpec((D, D), lambda i, j, k: (0, 0)),
        [(_sds((T, D), F32), row_spec)] * 2 + [(_sds((T, D), BF16), row_spec)], (tm, D),
        extra=[(x2, row_spec), (m1[2], _full_spec(m1[2])), (b_pw2, _full_spec(b_pw2))]
        + [(v, _full_spec(v)) for v in norm_mlp1],
        epi=lambda acc, ex: pw2_epi(acc + ex[2], ex))
    w_m1 = wg("mlp1", x3)
    mlp1, x4, _ = _mlp_fwd("mlp1", x3, h_mlp1, m1[5], w_m1["w_in"], w_m1["w_out"])

    fg = row(final_g)

    def head(ids, t, v):
        n, r = _rms(t[0])
        err = n * v[0] - t[1]
        d_out = err * (1.0 / D)
        dn = d_out * v[0]
        dxv = r * (dn - n * jnp.mean(dn * n, axis=-1, keepdims=True))
        part = jnp.sum(_sum0(err * err), axis=1, keepdims=True) * (0.5 / D)
        return [dxv], [part, _sum0(d_out * n)]

    dx4, loss, d_fg = _tiled("head", head, (n_t,), [_rows(x4), _rows(tgt)], [fg], [_orow(T, D, F32)],
                             [(1, 1), (1, D)])

    dx3, dw_in1, dw_out1, dm_mlp1 = _mlp_bwd("mlp1", dx4, mlp1, g11, m1[4], m1[5],
                                             w_m1["w_in"], w_m1["w_out"])
    dep = on_grads("mlp1", (dw_in1, dw_out1))
    d_y, d_g1c, d_bpw2 = _gate_bwd("conf", dx3, y_conf, m1[2], dep)
    tk = MM_TILE
    (dw_pw2,) = _mm("conf_dwpw2", s_conf, d_y, _TN, (D // tm, 1, T // tk),
                    pl.BlockSpec((tk, tm), lambda i, j, k: (k, i)), pl.BlockSpec((tk, D), lambda i, j, k: (k, 0)),
                    [(_sds((D, D), BF16), pl.BlockSpec((tm, D), lambda i, j, k: (i, 0)))], (tm, D))
    (ds,) = _mm("conf_ds", d_y, w_cf["conf_w_pw2"], _NT, (T // tm, 1, 1),
                pl.BlockSpec((tm, D), lambda i, j, k: (i, 0)), pl.BlockSpec((D, D), lambda i, j, k: (0, 0)),
                [(_sds((T, D), F32), pl.BlockSpec((tm, D), lambda i, j, k: (i, 0)))], (tm, D))
    dep = on_later(ds)

    def ln_silu_bwd(ids, t, v):
        dsv, zcv = t
        nh, rstd = _layernorm_parts(zcv)
        ln = nh * v[0] + v[1]
        sg = _sigmoid(ln)
        d_ln = dsv * (sg * (1.0 + ln * (1.0 - sg)))
        d_nh = d_ln * v[0]
        d_zc = rstd * (d_nh - jnp.mean(d_nh, axis=-1, keepdims=True)
                       - nh * jnp.mean(d_nh * nh, axis=-1, keepdims=True))
        return [d_zc], [_sum0(d_ln * nh), _sum0(d_ln)]

    d_zc, d_lng, d_lnb = _tiled("conf_ln_bwd", ln_silu_bwd, (n_t,), [_rows(ds), _rows(zc)],
                                [ln_g, ln_b] + _behind(dep), [_orow(T, D, F32)], [(1, D), (1, D)])
    d_zg = _dwconv("conf_conv_dx", d_zc, 0, conf["conv_w"][::-1], jnp.zeros((1, D), F32),
                   CONF_KW - 1 - CONF_KW // 2, conf_starts, D, CW_CONF)

    def glu_bwd(ids, t, v):
        dz, pa, pb = t
        sg = _sigmoid(pb)
        d_a = dz * sg
        d_b = dz * pa * sg * (1.0 - sg)
        return [jnp.concatenate([d_a, d_b], axis=1)], [_sum0(d_a), _sum0(d_b)]

    d_pre, d_b1a, d_b1b = _tiled(
        "conf_glu_bwd", glu_bwd, (n_t,), [_rows(d_zg), _rows(pre, D, col=0), _rows(pre, D, col=1)], [],
        [_orow(T, 2 * D, BF16)], [(1, D), (1, D)])
    (dw_pw1,) = _mm("conf_dwpw1", h1, d_pre, _TN, (D // tm, 4, T // tk),
                    pl.BlockSpec((tk, tm), lambda i, j, k: (k, i)),
                    pl.BlockSpec((tk, D // 2), lambda i, j, k: (k, j)),
                    [(_sds((4, D, D // 2), BF16), pl.BlockSpec((None, tm, D // 2), lambda i, j, k: (j, i, 0)))],
                    (tm, D // 2))
    dep = on_grads("conf", (dw_pw1, dw_pw2))
    (dh1,) = _mm("conf_dh", d_pre, w_cf["conf_w_pw1"], _NT, (T // tm, 1, 4),
                 pl.BlockSpec((tm, D // 2), lambda i, j, k: (i, k)),
                 pl.BlockSpec((None, D, D // 2), lambda i, j, k: (k, 0, 0)),
                 [(_sds((T, D), F32), pl.BlockSpec((tm, D), lambda i, j, k: (i, 0)))], (tm, D))
    dx2, d_sh1c, d_sc1c, d_g10 = _norm_bwd("conf", dx3, dh1, 0, x2, g10, m1[1], dep=dep)
    dep = on_later(dx2)

    dx1, dw_in0, dw_out0, dm_mlp0 = _mlp_bwd("mlp0", dx2, mlp0, g01, m0[4], m0[5],
                                             w_m0["w_in"], w_m0["w_out"], dep)
    dep = on_grads("mlp0", (dw_in0, dw_out0))
    d_orec, d_g1r, _ = _gate_bwd("rec", dx1, o_rec, m0[2], dep)
    (dw_rout,) = _mm("rec_dwout", m_rec, d_orec, _TN, (R // RH, 1, T // tk),
                     pl.BlockSpec((tk, RH), lambda i, j, k: (k, i)), pl.BlockSpec((tk, D), lambda i, j, k: (k, 0)),
                     [(_sds((R, D), BF16), pl.BlockSpec((RH, D), lambda i, j, k: (i, 0)))], (RH, D))
    (dm_rec,) = _mm("rec_dm", d_orec, w_rec["rec_w_out"], _NT, (T // tm, 1, 1),
                    pl.BlockSpec((tm, D), lambda i, j, k: (i, 0)), pl.BlockSpec((R, D), lambda i, j, k: (0, 0)),
                    [(_sds((T, R), F32), pl.BlockSpec((tm, R), lambda i, j, k: (i, 0)))], (tm, R))
    dep = on_later(dm_rec)

    def rec_mid_bwd(ids, t, v):
        dmv, gp, yf, yr = t
        g, th = _gelu(gp)
        lat = ids[0] > 0
        d_gp = jnp.where(lat, dmv * (yf + yr) * _gelu_grad(gp, th), 0.0)
        dy = jnp.where(lat, dmv * g, 0.0)
        return [d_gp, dy], []

    d_gp, dy = _tiled("rec_mid_bwd", rec_mid_bwd, (N_SCAN,),
                      [_rows(dm_rec, off=-1, clamp_lo=True), _rows(a_in, R), _rows(y_f), _rows(y_r)], _behind(dep),
                      [_orow(TA, R, BF16), _orow(TA, R, F32)])
    da_f, db_f, da_r, db_r = _scan_bwd(dy, a_f, y_f, hin_f, a_r, y_r, hin_r)
    d_gpre, d_u, d_gbias, d_lam = _tiled(
        "rg_bwd", _rg_bwd_fn, (TA // RG_TILE,), [_rows(a, tm=RG_TILE) for a in (u, da_f, db_f, da_r, db_r)],
        [wbd, gbias, lam], [_orow(TA, 2 * NQ, BF16, tm=RG_TILE), _orow(TA, R, F32, tm=RG_TILE)],
        [(1, 2 * NQ), (1, 2 * R)], vec_refs=True)
    tk_a = REC_TILE
    d_p = _dwconv("rec_conv_dx", d_u, 0, rec["conv_w"][::-1], jnp.zeros((1, R), F32), REC_KW - 1 - 1,
                  rec_starts, R, CW_REC)
    d_a = jnp.concatenate([d_gp, d_p.astype(BF16)], axis=1)
    (dw_rin,) = _mm("rec_dwin", hcat, d_a, _TN, (D // tm, 4, TA // tk_a),
                    pl.BlockSpec((tk_a, tm), lambda i, j, k: (k, i)), pl.BlockSpec((tk_a, RH), lambda i, j, k: (k, j)),
                    [(_sds((4, D, RH), BF16), pl.BlockSpec((None, tm, RH), lambda i, j, k: (j, i, 0)))], (tm, RH))
    dep = on_grads("rec", (dw_rin, dw_rout))
    (dhcat,) = _mm("rec_dh", d_a, w_rec["rec_w_in"], _NT, (TA // tm_a, 1, 4),
                   pl.BlockSpec((tm_a, RH), lambda i, j, k: (i, k)),
                   pl.BlockSpec((None, D, RH), lambda i, j, k: (k, 0, 0)),
                   [(_sds((TA, D), F32), pl.BlockSpec((tm_a, D), lambda i, j, k: (i, 0)))], (tm_a, D))
    dx0, d_sh1r, d_sc1r, d_g00 = _norm_bwd("rec", dx1, dhcat, 1, x0, g00, m0[1], dep=dep)
    dep = on_later(dx0)

    d_csh, d_csc, d_g00c = _norm_bwd("ctx", None, dhcat, 0, ctx, g00, csc, with_dx=False, dep=dep)
    blk_mask, blk_spread = _block_mask(), _block_spread()
    (d_wbd,) = _mm("rg_dw", u, d_gpre, _TN, (2, 2, TA // tk_a),
                   pl.BlockSpec((tk_a, RH), lambda i, j, k: (k, i)),
                   pl.BlockSpec((tk_a, NQ // 2), lambda i, j, k: (k, 2 * i + j)),
                   [(_sds((2, 4, RH, BLK), F32), pl.BlockSpec((None, 2, RH, BLK), lambda i, j, k: (i, j, 0, 0)))],
                   (RH, NQ // 2),
                   extra=[(blk_mask, _full_spec(blk_mask)), (blk_spread, _full_spec(blk_spread))]
                   + [(d, _full_spec(d)) for d in _behind(dep)],
                   epi=lambda acc, ex: [jnp.stack([_fold_blocks(acc[:, s * RH:(s + 1) * RH], ex[0], ex[1])
                                                   for s in range(2)])])
    d_cw_rec = _dwconv_wgrad("rec_conv_dw", d_u, a_in, R // CW_REC, REC_KW, 1, rec_starts, R, CW_REC, dep)
    d_cw_conf = _dwconv_wgrad("conf_conv_dw", d_zc, zg, 0, CONF_KW, CONF_KW // 2, conf_starts, D, CW_CONF, dep)

    big = dict(rec_w_in=dw_rin, rec_w_out=dw_rout, conf_w_pw1=dw_pw1, conf_w_pw2=dw_pw2,
               mlp_w_in=(dw_in0, dw_in1), mlp_w_out=(dw_out0, dw_out1))
    d_wa, d_wx = _gate_block_grads(d_wbd)
    d_ba, d_bx = _gate_bias_grads(d_gbias)
    d_mod = jnp.concatenate([
        d_sh1r, d_sc1r, d_g1r, dm_mlp0["sh"], dm_mlp0["sc"], dm_mlp0["gate"],
        d_sh1c, d_sc1c, d_g1c, dm_mlp1["sh"], dm_mlp1["sc"], dm_mlp1["gate"]], axis=1).reshape(2, 6 * D)
    small = dict(
        d_mod=d_mod, d_cmod=jnp.concatenate([d_csh, d_csc], axis=1),
        norm_g=jnp.concatenate([d_g00 + d_g00c, dm_mlp0["g_norm"], d_g10, dm_mlp1["g_norm"]], axis=1),
        rec_conv_w=d_cw_rec[:REC_KW], rec_conv_b=d_cw_rec[REC_KW], rec_lambda=d_lam.reshape(2, R),
        rec_w_a=d_wa, rec_b_a=d_ba, rec_w_x=d_wx, rec_b_x=d_bx,
        conf_b_pw1=jnp.concatenate([d_b1a, d_b1b], axis=1), conf_conv_w=d_cw_conf[:CONF_KW],
        conf_conv_b=d_cw_conf[CONF_KW], conf_ln_g=d_lng, conf_ln_b=d_lnb, conf_b_pw2=d_bpw2, final_g=d_fg)
    return loss.reshape(()), dx0, big, small


_BIG = ("rec_w_in", "rec_w_out", "conf_w_pw1", "conf_w_pw2", "mlp_w_in", "mlp_w_out")


def _halves(w):
    return w.reshape(w.shape[0], 2, w.shape[1] // 2, w.shape[2])


def _ada_fwd(c16, w_ada, b_shard):
    ns = w_ada.shape[2]
    tn = 512

    def kern(c_ref, w_ref, b_ref, o_ref):
        cv = c_ref[...]
        s = (cv * _sigmoid(cv)).astype(BF16)
        o_ref[...] = jnp.dot(s, w_ref[...].astype(BF16), preferred_element_type=F32) + b_ref[...]

    return _pcall(
        kern, name="ada_fwd", grid=(2, ns // tn),
        in_specs=[pl.BlockSpec((16, D), lambda l, j: (0, 0)), pl.BlockSpec((None, D, tn), lambda l, j: (l, 0, j)),
                  pl.BlockSpec((None, 1, tn), lambda l, j: (l, 0, j))],
        out_specs=pl.BlockSpec((None, 16, tn), lambda l, j: (l, 0, j)),
        out_shape=_sds((2, 16, ns), F32), compiler_params=_cparams(),
    )(c16, w_ada, b_shard)


def _ada_bwd(c16, dm16, w_ada):
    ns = w_ada.shape[2]
    tn = 512

    def kern(c_ref, dm_ref, w_ref, gw_ref, ds_ref):
        cv = c_ref[...]
        s = (cv * _sigmoid(cv)).astype(BF16)
        dm = dm_ref[...].astype(BF16)
        gw_ref[...] = lax.dot_general(s, dm, _TN, preferred_element_type=F32)

        @pl.when(jnp.logical_and(pl.program_id(0) == 0, pl.program_id(1) == 0))
        def _():
            ds_ref[...] = jnp.zeros_like(ds_ref)

        ds_ref[...] += lax.dot_general(dm, w_ref[...].astype(BF16), _NT, preferred_element_type=F32)

    return _pcall(
        kern, name="ada_bwd", grid=(2, ns // tn),
        in_specs=[pl.BlockSpec((16, D), lambda l, j: (0, 0)), pl.BlockSpec((None, 16, tn), lambda l, j: (l, 0, j)),
                  pl.BlockSpec((None, D, tn), lambda l, j: (l, 0, j))],
        out_specs=[pl.BlockSpec((None, D, tn), lambda l, j: (l, 0, j)), pl.BlockSpec((16, D), lambda l, j: (0, 0))],
        out_shape=[_sds((2, D, ns), F32), _sds((16, D), F32)], compiler_params=_cparams(),
    )(c16, dm16, w_ada)


def _cctx_grad(ds8, c_ctx):
    def kern(d_ref, c_ref, o_ref):
        tot = d_ref[0, 8:9, :] + d_ref[2, 8:9, :] + d_ref[4, 8:9, :] + d_ref[6, 8:9, :]
        cv = c_ref[...]
        sg = _sigmoid(cv)
        o_ref[...] = tot * (sg * (1.0 + cv * (1.0 - sg)))

    return _pcall(kern, name="cctx_grad", out_shape=_sds((1, D), F32))(ds8, c_ctx.reshape(1, D))


def kernel(x, c, ctx, c_ctx, w_ada, b_ada, norm_g, rec_w_in, rec_conv_w, rec_conv_b, rec_lambda, rec_w_a, rec_b_a, rec_w_x, rec_b_x, rec_w_out, conf_w_pw1, conf_b_pw1, conf_conv_w, conf_conv_b, conf_ln_g, conf_ln_b, conf_w_pw2, conf_b_pw2, mlp_w_in, mlp_w_out, final_g, loss_target, m_c_ctx, m_w_ada, m_b_ada, m_norm_g, m_rec_w_in, m_rec_conv_w, m_rec_conv_b, m_rec_lambda, m_rec_w_a, m_rec_b_a, m_rec_w_x, m_rec_b_x, m_rec_w_out, m_conf_w_pw1, m_conf_b_pw1, m_conf_conv_w, m_conf_conv_b, m_conf_ln_g, m_conf_ln_b, m_conf_w_pw2, m_conf_b_pw2, m_mlp_w_in, m_mlp_w_out, m_final_g, v_c_ctx, v_w_ada, v_b_ada, v_norm_g, v_rec_w_in, v_rec_conv_w, v_rec_conv_b, v_rec_lambda, v_rec_w_a, v_rec_b_a, v_rec_w_x, v_rec_b_x, v_rec_w_out, v_conf_w_pw1, v_conf_b_pw1, v_conf_conv_w, v_conf_conv_b, v_conf_ln_g, v_conf_ln_b, v_conf_w_pw2, v_conf_b_pw2, v_mlp_w_in, v_mlp_w_out, v_final_g):
    names = ["c_ctx", "w_ada", "b_ada", "norm_g", "rec_w_in", "rec_conv_w", "rec_conv_b", "rec_lambda", "rec_w_a",
             "rec_b_a", "rec_w_x", "rec_b_x", "rec_w_out", "conf_w_pw1", "conf_b_pw1", "conf_conv_w", "conf_conv_b",
             "conf_ln_g", "conf_ln_b", "conf_w_pw2", "conf_b_pw2", "mlp_w_in", "mlp_w_out", "final_g"]
    w = dict(zip(names, [c_ctx, w_ada, b_ada, norm_g, rec_w_in, rec_conv_w, rec_conv_b, rec_lambda, rec_w_a,
                         rec_b_a, rec_w_x, rec_b_x, rec_w_out, conf_w_pw1, conf_b_pw1, conf_conv_w, conf_conv_b,
                         conf_ln_g, conf_ln_b, conf_w_pw2, conf_b_pw2, mlp_w_in, mlp_w_out, final_g]))
    m = dict(zip(names, [m_c_ctx, m_w_ada, m_b_ada, m_norm_g, m_rec_w_in, m_rec_conv_w, m_rec_conv_b, m_rec_lambda,
                         m_rec_w_a, m_rec_b_a, m_rec_w_x, m_rec_b_x, m_rec_w_out, m_conf_w_pw1, m_conf_b_pw1,
                         m_conf_conv_w, m_conf_conv_b, m_conf_ln_g, m_conf_ln_b, m_conf_w_pw2, m_conf_b_pw2,
                         m_mlp_w_in, m_mlp_w_out, m_final_g]))
    v = dict(zip(names, [v_c_ctx, v_w_ada, v_b_ada, v_norm_g, v_rec_w_in, v_rec_conv_w, v_rec_conv_b, v_rec_lambda,
                         v_rec_w_a, v_rec_b_a, v_rec_w_x, v_rec_b_x, v_rec_w_out, v_conf_w_pw1, v_conf_b_pw1,
                         v_conf_conv_w, v_conf_conv_b, v_conf_ln_g, v_conf_ln_b, v_conf_w_pw2, v_conf_b_pw2,
                         v_mlp_w_in, v_mlp_w_out, v_final_g]))
    mx, my, mc = _me()
    chip = 2 * mx + my
    me = 4 * mx + 2 * my + mc

    sharded_small = ["norm_g", "rec_conv_w", "rec_lambda", "conf_b_pw1", "conf_conv_w", "conf_conv_b", "conf_ln_g",
                     "conf_ln_b", "conf_b_pw2"]
    packed, offs = _pack([c] + [w[k] for k in sharded_small], 8)
    place = jnp.stack([chip, mc]).astype(jnp.int32)
    shards = [(_halves(rec_w_in), 0), (_halves(rec_w_out), 0), (_halves(conf_w_pw1), 0), (_halves(conf_w_pw2), 0),
              (_halves(mlp_w_in), 0), (_halves(mlp_w_in), 1), (_halves(mlp_w_out), 0), (_halves(mlp_w_out), 1)]
    slots = _place_big(shards, place)
    placed = jnp.stack([lax.dynamic_slice(s, (chip, 0, 0, 0), (1, 1, 1, 1)).reshape(1) for s in slots])
    got = _allgather8("gather_small", packed, placed)

    got_flat = got.reshape(8, -1)

    def piece(i):
        p, n, shape = offs[i]
        return got_flat[:, p:p + n].reshape((8,) + tuple(shape))

    c_rows = piece(0).reshape(8, D)
    full = {}
    for i, k in enumerate(sharded_small):
        per_chip = jnp.moveaxis(piece(1 + i)[0::2], 0, -2)
        full[k] = per_chip.reshape(per_chip.shape[:-2] + (4 * per_chip.shape[-1],))
    c16 = jnp.concatenate([c_rows, c_ctx.reshape(1, D), jnp.zeros((7, D), F32)], axis=0)

    ns = w_ada.shape[2]
    b_shard = lax.dynamic_slice_in_dim(b_ada, chip * ns, ns, axis=1).reshape(2, 1, ns)
    prod = _ada_fwd(c16, w_ada, b_shard).reshape(32, ns)

    mod_state, mod_started = _gather8_start("gather_mod_start", prod, got)
    use_order = dict(rec=(0, 1), mlp0=(4, 6), conf=(2, 3), mlp1=(5, 7))
    flying, gsems, swapping = {}, {}, {}
    fly, sems, rec_started = _gather_start("gather_start_rec", [slots[t] for t in use_order["rec"]], ((0, 1),),
                                           mod_started)
    flying["rec"], gsems["rec"] = fly, sems
    later = ("mlp0", "conf", "mlp1")
    fly, sems, all_started = _gather_start("gather_start_rest", [slots[t] for g in later for t in use_order[g]],
                                           ((0, 1), (2, 3), (4, 5)), rec_started)
    for gi, g in enumerate(later):
        flying[g], gsems[g] = fly[2 * gi:2 * gi + 2], sems[2 * gi:2 * gi + 2]

    def wg_pre(group, after):
        bufs = _gather_wait(f"gather_wait_{group}", flying[group], *gsems[group], after)
        swapping[group], token = _swap_start(f"swap_start_{group}", bufs, after)
        return token

    def wg(group, after):
        if group in swapping:
            a, b = _swap_wait(f"swap_wait_{group}", *swapping[group], after)
        else:
            a, b = _swap_halves(f"swap_{group}",
                                _gather_wait(f"gather_wait_{group}", flying[group], *gsems[group], after))
        if group == "rec":
            return dict(rec_w_in=a.reshape(4, D, RH), rec_w_out=b.reshape(R, D))
        if group == "conf":
            return dict(conf_w_pw1=a.reshape(4, D, D // 2), conf_w_pw2=b.reshape(D, D))
        return dict(w_in=a.reshape(4, D, D), w_out=b.reshape(FF, D))

    prod, prod8 = _gather8_wait("gather_mod_wait", *mod_state, all_started)
    prod8 = lax.dynamic_update_slice(prod8, prod[None], (me, 0, 0)).reshape(8, 2, 16, ns)
    mod_all = jnp.concatenate([prod8[2 * j] for j in range(4)], axis=-1)
    mods = lax.dynamic_index_in_dim(mod_all, me, axis=1, keepdims=False).reshape(2, 6, D)
    cmods = mod_all[0, 8].reshape(6, D)[:2]

    rec = dict(conv_w=full["rec_conv_w"][0], conv_b=rec_conv_b[0], lam=full["rec_lambda"][0],
               w_a=rec_w_a[0], b_a=rec_b_a[0], w_x=rec_w_x[0], b_x=rec_b_x[0])
    conf = dict(b_pw1=full["conf_b_pw1"][0], conv_w=full["conf_conv_w"][0], conv_b=full["conf_conv_b"][0],
                ln_g=full["conf_ln_g"][0], ln_b=full["conf_ln_b"][0], b_pw2=full["conf_b_pw2"][0])
    pairing, sent = {}, {}

    def on_grads(group, dws):
        parts = [dw.reshape((4,) + shards[t][0].shape[1:]) for dw, t in zip(dws, use_order[group])]
        pairing[group], token = _reduce_begin(group, parts, place)
        return token

    def finish_pair(after):
        (group, state), = pairing.items()
        pairing.clear()
        sent[group], sent["token"] = _reduce_mid(group, state, place, after)
        return sent["token"]

    loss_local, grad_x, _, small = _local_step(x[0], ctx[0], loss_target[0], mods, cmods, full["norm_g"], final_g,
                                               rec, conf, wg, on_grads, wg_pre, finish_pair)
    rec_sent = sent["token"]
    small["loss"] = loss_local.reshape(1)

    small_names = ["loss", "d_mod", "d_cmod", "norm_g", "rec_conv_w", "rec_conv_b", "rec_lambda", "rec_w_a", "rec_b_a",
                   "rec_w_x", "rec_b_x", "conf_b_pw1", "conf_conv_w", "conf_conv_b", "conf_ln_g", "conf_ln_b",
                   "conf_b_pw2", "final_g"]
    mine = lax.broadcasted_iota(jnp.int32, (8, 1), 0) == me
    mod_slots = jnp.where(mine, small["d_mod"].reshape(1, -1), 0.0)
    spacked, soffs = _pack([small[k] for k in small_names] + [mod_slots])
    small_state, small_started = _allreduce_small_begin(spacked, place, rec_sent)

    fulls = {}
    for group in ("mlp1", "conf", "mlp0", "rec"):
        for t, f in zip(use_order[group], _reduce_end(group, sent[group], place, small_started)):
            fulls[t] = f
    whole = _share_halves("share_grads", [fulls[t] for t in range(8)])
    g_big = dict(rec_w_in=whole[0].reshape(rec_w_in.shape), rec_w_out=whole[1].reshape(rec_w_out.shape),
                 conf_w_pw1=whole[2].reshape(conf_w_pw1.shape), conf_w_pw2=whole[3].reshape(conf_w_pw2.shape),
                 mlp_w_in=jnp.stack([whole[4].reshape(D, D), whole[5].reshape(D, D)]),
                 mlp_w_out=jnp.stack([whole[6].reshape(D, D), whole[7].reshape(D, D)]))
    delta, new_m, new_v = {}, {}, {}

    def adamw_of(k, g):
        cols = w[k].shape[-1]
        d_, m_, v_ = _adamw(f"adamw_{k}", w[k].reshape(-1, cols), g.reshape(-1, cols),
                            m[k].reshape(-1, cols), v[k].reshape(-1, cols))
        delta[k], new_m[k], new_v[k] = (a.reshape(w[k].shape) for a in (d_, m_, v_))

    for k in _BIG:
        adamw_of(k, g_big[k])

    unpacked = _unpack(_allreduce_small_end(small_state, new_v[_BIG[-1]]), soffs)
    ssum = dict(zip(small_names, unpacked[:-1]))
    loss = ssum["loss"].reshape(())
    dmod_rows = unpacked[-1].reshape(8, 2, 6 * D).transpose(1, 0, 2)

    d_cmod_full =jnp.concatenate([ssum["d_cmod"].reshape(1, 2 * D), jnp.zeros((1, 4 * D), F32)], axis=1)
    dm16 = jnp.concatenate([dmod_rows, jnp.stack([d_cmod_full, jnp.zeros((1, 6 * D), F32)]),
                            jnp.zeros((2, 7, 6 * D), F32)], axis=1)
    dm16_shard = lax.dynamic_slice_in_dim(dm16, chip * ns, ns, axis=2)
    g_w_ada, ds_part = _ada_bwd(c16, dm16_shard, w_ada)
    ds8 = _allgather8("gather_dsilu", ds_part)
    g_c_ctx = _cctx_grad(ds8, c_ctx).reshape(D)
    g_b_ada = ssum["d_mod"] + jnp.stack([d_cmod_full[0], jnp.zeros((6 * D,), F32)])

    def shard_of(a, axis):
        n = a.shape[axis] // 4
        return lax.dynamic_slice_in_dim(a, chip * n, n, axis=axis)

    grads = dict(
        c_ctx=g_c_ctx, w_ada=g_w_ada, b_ada=g_b_ada,
        norm_g=shard_of(ssum["norm_g"].reshape(2, 2, D), 2),
        rec_w_in=g_big["rec_w_in"], rec_conv_w=shard_of(ssum["rec_conv_w"].reshape(1, REC_KW, R), 2),
        rec_conv_b=ssum["rec_conv_b"].reshape(1, R), rec_lambda=shard_of(ssum["rec_lambda"].reshape(1, 2, R), 2),
        rec_w_a=ssum["rec_w_a"].reshape(rec_w_a.shape), rec_b_a=ssum["rec_b_a"].reshape(rec_b_a.shape),
        rec_w_x=ssum["rec_w_x"].reshape(rec_w_x.shape), rec_b_x=ssum["rec_b_x"].reshape(rec_b_x.shape),
        rec_w_out=g_big["rec_w_out"], conf_w_pw1=g_big["conf_w_pw1"],
        conf_b_pw1=shard_of(ssum["conf_b_pw1"].reshape(1, 2 * D), 1),
        conf_conv_w=shard_of(ssum["conf_conv_w"].reshape(1, CONF_KW, D), 2),
        conf_conv_b=shard_of(ssum["conf_conv_b"].reshape(1, D), 1),
        conf_ln_g=shard_of(ssum["conf_ln_g"].reshape(1, D), 1), conf_ln_b=shard_of(ssum["conf_ln_b"].reshape(1, D), 1),
        conf_w_pw2=g_big["conf_w_pw2"], conf_b_pw2=shard_of(ssum["conf_b_pw2"].reshape(1, D), 1),
        mlp_w_in=g_big["mlp_w_in"], mlp_w_out=g_big["mlp_w_out"], final_g=ssum["final_g"].reshape(D))

    adamw_of("w_ada", g_w_ada)
    rest = [k for k in names if k not in ("w_ada",) + _BIG]
    d_, m_, v_ = _adamw_many("adamw_small", [w[k] for k in rest], [grads[k] for k in rest],
                             [m[k] for k in rest], [v[k] for k in rest])
    for k, dd, mm, vv in zip(rest, d_, m_, v_):
        delta[k], new_m[k], new_v[k] = dd, mm, vv

    return (loss, grad_x[None], *[grads[k] for k in names], *[delta[k] for k in names],
            *[new_m[k] for k in names], *[new_v[k] for k in names])
```

```python
import functools
import math

import jax
import jax.numpy as jnp
from jax import lax
from jax.experimental import pallas as pl
from jax.experimental.pallas import tpu as pltpu

F32 = jnp.float32
BF16 = jnp.bfloat16

D = 1024
T = 2048
TC = 256
TA = T + TC
R = 1280
RH = R // 2
NQ = 4 * RH
FF = 4096
N_BLK = 16
BLK = R // N_BLK
GRID_W = 64
EPS = 1e-6
RG_C = 8.0
CONF_KW = 31
REC_KW = 4
LANE = 128
ROW_TILE = 256
HALO = 16
RG_TILE = 128
PACK_ROWS = 512
MM_TILE = 1024
REC_TILE = TA // 2
CW_REC = 640
CW_CONF = 512
V7X_VMEM_BYTES = 64 * 1024 * 1024
VMEM_LIMIT = V7X_VMEM_BYTES - 8 * 1024 * 1024

ADAM_LR = 0.001
ADAM_B1 = 0.9
ADAM_B2 = 0.999
ADAM_EPS = 1e-08
ADAM_WD = 0.01
ADAM_STEP = 10

MESH = pl.DeviceIdType.MESH
ANY = pl.BlockSpec(memory_space=pl.ANY)


def _sds(shape, dtype):
    return jax.ShapeDtypeStruct(tuple(shape), dtype)


def _pcall(body, **kw):
    return pl.pallas_call(body, **kw)


def _cparams():
    return pltpu.CompilerParams(vmem_limit_bytes=VMEM_LIMIT)


def _full_spec(arr):
    nd = arr.ndim
    return pl.BlockSpec(arr.shape, lambda *ids, _n=nd: (0,) * _n)


def _sum0(v):
    return jnp.sum(v, axis=0, keepdims=True)


def _tiled(name, fn, grid, ins, vecs, outs, vec_outs=(), vec_refs=False):
    n_in, n_vec, n_out = len(ins), len(vecs), len(outs)
    n_grid = len(grid)

    def kern(*refs):
        ids = [pl.program_id(a) for a in range(n_grid)]
        tin = [r[...] for r in refs[:n_in]]
        vin = list(refs[n_in:n_in + n_vec]) if vec_refs else [r[...] for r in refs[n_in:n_in + n_vec]]
        o_refs = refs[n_in + n_vec:n_in + n_vec + n_out]
        a_refs = refs[n_in + n_vec + n_out:]
        tout, incs = fn(ids, tin, vin)
        for r, v in zip(o_refs, tout):
            r[...] = v.astype(r.dtype)
        if a_refs:
            first = functools.reduce(jnp.logical_and, [i == 0 for i in ids])

            @pl.when(first)
            def _():
                for r in a_refs:
                    r[...] = jnp.zeros_like(r)

            for r, v in zip(a_refs, incs):
                r[...] += v

    out_shape = [o for o, _ in outs] + [_sds(s, F32) for s in vec_outs]
    out_specs = [s for _, s in outs] + [
        pl.BlockSpec(tuple(s), lambda *ids, _n=len(s): (0,) * _n) for s in vec_outs]
    res = _pcall(
        kern, name=name, grid=tuple(grid),
        in_specs=[s for _, s in ins] + [_full_spec(v) for v in vecs],
        out_specs=out_specs, out_shape=out_shape, compiler_params=_cparams(),
    )(*[a for a, _ in ins], *vecs)
    return list(res)


def _rows(arr, ncols=None, tm=ROW_TILE, off=0, col=0, clamp_lo=False):
    ncols = arr.shape[1] if ncols is None else ncols
    if clamp_lo:
        return arr, pl.BlockSpec((tm, ncols), lambda i: (jnp.maximum(i + off, 0), col))
    return arr, pl.BlockSpec((tm, ncols), lambda i: (i + off, col))


def _orow(nrows, ncols, dtype, tm=ROW_TILE, off=0, clamp_lo=False):
    if clamp_lo:
        return _sds((nrows, ncols), dtype), pl.BlockSpec((tm, ncols), lambda i: (jnp.maximum(i + off, 0), 0))
    return _sds((nrows, ncols), dtype), pl.BlockSpec((tm, ncols), lambda i: (i + off, 0))


_NN = (((1,), (0,)), ((), ()))
_TN = (((0,), (0,)), ((), ()))
_NT = (((1,), (1,)), ((), ()))


def _mm(name, a, b, dims, grid, a_spec, b_spec, out, acc_shape, extra=(), a_pre=None, epi=None):
    n_k = grid[2]
    n_ex = len(extra)

    def kern(a_ref, b_ref, *rest):
        ex = rest[:n_ex]
        o_refs = rest[n_ex:n_ex + len(out)]
        k = pl.program_id(2)
        av = a_ref[...]
        if a_pre is not None:
            av = a_pre(av)
        part = lax.dot_general(av.astype(BF16), b_ref[...].astype(BF16), dims, preferred_element_type=F32)

        def finish(total):
            vals = [total] if epi is None else epi(total, [e[...] for e in ex])
            for r, v in zip(o_refs, vals):
                r[...] = v.astype(r.dtype)

        if n_k == 1:
            finish(part)
        else:
            acc = rest[-1]

            @pl.when(k == 0)
            def _():
                acc[...] = part

            @pl.when(jnp.logical_and(k > 0, k < n_k - 1))
            def _():
                acc[...] += part

            @pl.when(k == n_k - 1)
            def _():
                finish(acc[...] + part)

    res = _pcall(
        kern, name=name, grid=tuple(grid),
        in_specs=[a_spec, b_spec] + [s for _, s in extra],
        out_specs=[s for _, s in out], out_shape=[o for o, _ in out],
        scratch_shapes=[] if n_k == 1 else [pltpu.VMEM(tuple(acc_shape), F32)], compiler_params=_cparams(),
    )(a, b, *[e for e, _ in extra])
    return list(res)


def _rms(x):
    r = lax.rsqrt(jnp.mean(x * x, axis=-1, keepdims=True) + EPS)
    return x * r, r


def _norm_mod(x, g, sc, sh):
    n, _ = _rms(x)
    return (n * g) * (1.0 + sc) + sh


def _norm_mod_bwd(dh, x, g, sc):
    n, r = _rms(x)
    d_sh = _sum0(dh)
    d_sc = _sum0(dh * (n * g))
    d_g = _sum0(dh * (1.0 + sc) * n)
    dn = dh * (g * (1.0 + sc))
    dx = r * (dn - n * jnp.mean(dn * n, axis=-1, keepdims=True))
    return dx, d_sh, d_sc, d_g


_GELU_K = math.sqrt(2.0 / math.pi)


def _gelu(x):
    t = jnp.tanh(_GELU_K * (x + 0.044715 * x * x * x))
    return 0.5 * x * (1.0 + t), t


def _gelu_grad(x, t):
    return 0.5 * (1.0 + t) + 0.5 * x * (1.0 - t * t) * (_GELU_K * (1.0 + 3.0 * 0.044715 * x * x))


def _sigmoid(x):
    return 0.5 * jnp.tanh(0.5 * x) + 0.5


def _expm1(x):
    p = jnp.full_like(x, 1.0 / 5040.0)
    for c in (1.0 / 720.0, 1.0 / 120.0, 1.0 / 24.0, 1.0 / 6.0, 0.5, 1.0):
        p = p * x + c
    return jnp.where(jnp.abs(x) < 0.3, x * p, jnp.exp(x) - 1.0)


def _softplus_neg(lam):
    return jnp.log1p(jnp.exp(-jnp.abs(lam))) + jnp.maximum(-lam, 0.0)


def _layernorm_parts(x):
    mu = jnp.mean(x, axis=-1, keepdims=True)
    xc = x - mu
    rstd = lax.rsqrt(jnp.mean(xc * xc, axis=-1, keepdims=True) + EPS)
    return xc * rstd, rstd


def _rg_gates(u, wbd, gbias, lam):
    sp = _softplus_neg(lam)
    parts = {}
    for h in range(2):
        uh = u[:, h * RH:(h + 1) * RH]
        g = jnp.dot(uh.astype(BF16), wbd[h], preferred_element_type=F32) + gbias[:, h * NQ:(h + 1) * NQ]
        for d in range(2):
            r = _sigmoid(g[:, (2 * d) * RH:(2 * d + 1) * RH])
            i = _sigmoid(g[:, (2 * d + 1) * RH:(2 * d + 2) * RH])
            sph = sp[d:d + 1, h * RH:(h + 1) * RH]
            la = (-RG_C) * r * sph
            e2 = _expm1(2.0 * la)
            inv_mult = jnp.where(e2 < 0.0, lax.rsqrt(-e2), 0.0)
            parts[(d, h)] = dict(r=r, i=i, la=la, a=jnp.exp(la), e2=e2, mult=-e2 * inv_mult, inv_mult=inv_mult,
                                 uh=uh, sp=sph)
    return parts


def _rg_fwd_fn(ids, tin, vin):
    (u,) = tin
    wbd = vin[0]
    parts = _rg_gates(u, wbd, vin[1][...], vin[2][...])
    outs = []
    for d in range(2):
        a = jnp.concatenate([parts[(d, h)]["a"] for h in range(2)], axis=1)
        b = jnp.concatenate([parts[(d, h)]["mult"] * parts[(d, h)]["i"] * parts[(d, h)]["uh"]
                             for h in range(2)], axis=1)
        outs += [a, b]
    return outs, []


def _rg_bwd_fn(ids, tin, vin):
    u, da_f, db_f, da_r, db_r = tin
    wbd, lam = vin[0], vin[2][...]
    parts = _rg_gates(u, wbd, vin[1][...], lam)
    dab = ((da_f, db_f), (da_r, db_r))
    dsig_lam = -1.0 / (1.0 + jnp.exp(lam))
    du_halves, dpre_halves, dlam = [], [], [[None, None], [None, None]]
    for h in range(2):
        du = jnp.zeros_like(parts[(0, h)]["uh"])
        dpre = []
        for d in range(2):
            p = parts[(d, h)]
            da = dab[d][0][:, h * RH:(h + 1) * RH]
            db = dab[d][1][:, h * RH:(h + 1) * RH]
            d_mult = db * p["i"] * p["uh"]
            d_i = db * p["mult"] * p["uh"]
            du = du + db * p["mult"] * p["i"]
            d_la = da * p["a"] - d_mult * (p["e2"] + 1.0) * p["inv_mult"]
            d_r = d_la * ((-RG_C) * p["sp"])
            dlam[d][h] = _sum0(d_la * ((-RG_C) * p["r"])) * dsig_lam[d:d + 1, h * RH:(h + 1) * RH]
            dpre += [d_r * p["r"] * (1.0 - p["r"]), d_i * p["i"] * (1.0 - p["i"])]
        dpre = jnp.concatenate(dpre, axis=1)
        du = du + lax.dot_general(dpre.astype(BF16), wbd[h], _NT, preferred_element_type=F32)
        du_halves.append(du)
        dpre_halves.append(dpre)
    dpre_all = jnp.concatenate(dpre_halves, axis=1)
    dlam_row = jnp.concatenate([dlam[0][0], dlam[0][1], dlam[1][0], dlam[1][1]], axis=1)
    return [dpre_all, jnp.concatenate(du_halves, axis=1)], [_sum0(dpre_all), dlam_row]


def _tile_flags(i, n_tiles, seq_starts):
    starts_here = functools.reduce(jnp.logical_or, [i == s for s in seq_starts])
    ends_here = functools.reduce(jnp.logical_or, [i + 1 == s for s in seq_starts] + [i + 1 == n_tiles])
    return jnp.logical_not(starts_here), jnp.logical_not(ends_here)


def _halo_specs(col0, cw):
    hb = ROW_TILE // HALO
    prev = pl.BlockSpec((HALO, cw), lambda i, c: (jnp.maximum(i * hb - 1, 0), col0 + c))
    cur = pl.BlockSpec((ROW_TILE, cw), lambda i, c: (i, col0 + c))
    return prev, cur, hb


def _window(prev_ref, cur_ref, next_ref, has_prev, has_next):
    prev = jnp.where(has_prev, prev_ref[...], 0.0)
    nxt = jnp.where(has_next, next_ref[...], 0.0)
    return jnp.concatenate([prev, cur_ref[...], nxt], axis=0)


def _tap_reader(win):
    sub = 8
    n = win.shape[0]
    shifted = {0: win}

    def tap(off):
        s = off % sub
        if s not in shifted:
            shifted[s] = pltpu.roll(win, n - s, axis=0)
        return shifted[s][off - s:off - s + ROW_TILE, :]

    return tap


def _dwconv(name, x, col0, w, bias, pad_left, seq_starts, n_ch, cw=256):
    n_rows = x.shape[0]
    n_tiles = n_rows // ROW_TILE
    n_taps = w.shape[0]
    prev_spec, cur_spec, hb = _halo_specs(col0, cw)
    last_hb = n_rows // HALO - 1
    next_spec = pl.BlockSpec((HALO, cw), lambda i, c: (jnp.minimum((i + 1) * hb, last_hb), col0 + c))

    def kern(prev_ref, cur_ref, next_ref, w_ref, b_ref, o_ref):
        has_prev, has_next = _tile_flags(pl.program_id(0), n_tiles, seq_starts)
        win = _window(prev_ref, cur_ref, next_ref, has_prev, has_next)
        tap = _tap_reader(win)
        wv = w_ref[...]
        acc = jnp.zeros((ROW_TILE, cw), F32) + b_ref[...]
        for k in range(n_taps):
            acc = acc + wv[k:k + 1, :] * tap(HALO + k - pad_left)
        o_ref[...] = acc

    return _pcall(
        kern, name=name, grid=(n_tiles, n_ch // cw),
        in_specs=[prev_spec, cur_spec, next_spec,
                  pl.BlockSpec((n_taps, cw), lambda i, c: (0, c)), pl.BlockSpec((1, cw), lambda i, c: (0, c))],
        out_specs=pl.BlockSpec((ROW_TILE, cw), lambda i, c: (i, c)),
        out_shape=_sds((n_rows, n_ch), F32), compiler_params=_cparams(),
    )(x, x, x, w, bias)


def _dwconv_wgrad(name, dy, x, col0, n_taps, pad_left, seq_starts, n_ch, cw=256, dep=None):
    deps = [] if dep is None else [dep]
    n_rows = dy.shape[0]
    n_tiles = n_rows // ROW_TILE
    n_out = -(-(n_taps + 1) // 8) * 8
    prev_spec, cur_spec, hb = _halo_specs(col0, cw)
    last_hb = n_rows // HALO - 1
    next_spec = pl.BlockSpec((HALO, cw), lambda c, i: (jnp.minimum((i + 1) * hb, last_hb), col0 + c))
    prev_spec = pl.BlockSpec((HALO, cw), lambda c, i: (jnp.maximum(i * hb - 1, 0), col0 + c))
    cur_spec = pl.BlockSpec((ROW_TILE, cw), lambda c, i: (i, col0 + c))

    def kern(dy_ref, prev_ref, cur_ref, next_ref, *rest):
        o_ref = rest[-1]
        i = pl.program_id(1)
        has_prev, has_next = _tile_flags(i, n_tiles, seq_starts)
        win = _window(prev_ref, cur_ref, next_ref, has_prev, has_next)
        dyv = dy_ref[...]
        tap = _tap_reader(win)
        rid = lax.broadcasted_iota(jnp.int32, (n_out, cw), 0)
        inc = jnp.where(rid == n_taps, _sum0(dyv), 0.0)
        for k in range(n_taps):
            inc = inc + jnp.where(rid == k, _sum0(dyv * tap(HALO + k - pad_left)), 0.0)

        @pl.when(i == 0)
        def _():
            o_ref[...] = jnp.zeros_like(o_ref)

        o_ref[...] += inc

    return _pcall(
        kern, name=name, grid=(n_ch // cw, n_tiles),
        in_specs=[pl.BlockSpec((ROW_TILE, cw), lambda c, i: (i, c)), prev_spec, cur_spec, next_spec]
        + [pl.BlockSpec(d.shape, lambda c, i: (0, 0)) for d in deps],
        out_specs=pl.BlockSpec((n_out, cw), lambda c, i: (0, c)),
        out_shape=_sds((n_out, n_ch), F32), compiler_params=_cparams(),
    )(dy, x, x, x, *deps)


N_SCAN = TA // ROW_TILE


def _rev_block(j):
    return jnp.where(j == 0, 0, N_SCAN - j)


def _scan_fwd(a_f, b_f, a_r, b_r):
    fwd_spec = pl.BlockSpec((ROW_TILE, R), lambda i: (i, 0))
    rev_spec = pl.BlockSpec((ROW_TILE, R), lambda i: (_rev_block(i), 0))
    hin_spec = pl.BlockSpec((None, 1, R), lambda i: (i, 0, 0))

    def kern(af, bf, ar, br, yf, yr, hin_f, hin_r, hf_s, hr_s):
        @pl.when(pl.program_id(0) == 0)
        def _():
            hf_s[...] = jnp.zeros_like(hf_s)
            hr_s[...] = jnp.zeros_like(hr_s)

        hin_f[...] = hf_s[...]
        hin_r[...] = hr_s[...]

        def step(s8, carry):
            hf, hr = carry
            t0 = pl.multiple_of(s8 * 8, 8)
            for q in range(8):
                tf = t0 + q
                hf = af[pl.ds(tf, 1), :] * hf + bf[pl.ds(tf, 1), :]
                yf[pl.ds(tf, 1), :] = hf
                tr = ROW_TILE - 1 - tf
                hr = ar[pl.ds(tr, 1), :] * hr + br[pl.ds(tr, 1), :]
                yr[pl.ds(tr, 1), :] = hr
            return hf, hr

        hf, hr = lax.fori_loop(0, ROW_TILE // 8, step, (hf_s[...], hr_s[...]))
        hf_s[...] = hf
        hr_s[...] = hr

    return _pcall(
        kern, name="scan_fwd", grid=(N_SCAN,),
        in_specs=[fwd_spec, fwd_spec, rev_spec, rev_spec],
        out_specs=[fwd_spec, rev_spec, hin_spec, hin_spec],
        out_shape=[_sds((TA, R), F32), _sds((TA, R), F32), _sds((N_SCAN, 1, R), F32), _sds((N_SCAN, 1, R), F32)],
        scratch_shapes=[pltpu.VMEM((1, R), F32), pltpu.VMEM((1, R), F32)], compiler_params=_cparams(),
    )(a_f, b_f, a_r, b_r)


def _scan_bwd(dy, a_f, y_f, hin_f, a_r, y_r, hin_r):
    fwd_spec = pl.BlockSpec((ROW_TILE, R), lambda i: (N_SCAN - 1 - i, 0))
    rev_spec = pl.BlockSpec((ROW_TILE, R), lambda i: (_rev_block(N_SCAN - 1 - i), 0))
    hin_spec = pl.BlockSpec((None, 1, R), lambda i: (N_SCAN - 1 - i, 0, 0))
    last = ROW_TILE - 1

    def kern(dyf, af, yf, hf0, dyr, ar, yr, hr0, daf, dbf, dar, dbr, gf_s, anf_s, gr_s, anr_s):
        @pl.when(pl.program_id(0) == 0)
        def _():
            for r in (gf_s, anf_s, gr_s, anr_s):
                r[...] = jnp.zeros_like(r)

        def one(dy_ref, a_ref, y_ref, da_ref, db_ref, g, an, p, pprev):
            gnew = dy_ref[pl.ds(p, 1), :] + an * g
            db_ref[pl.ds(p, 1), :] = gnew
            da_ref[pl.ds(p, 1), :] = gnew * y_ref[pl.ds(pprev, 1), :]
            return gnew, a_ref[pl.ds(p, 1), :]

        def step(s8, carry):
            gf, anf, gr, anr = carry
            base = s8 * 8
            for q in range(8):
                s = last - (base + q)
                gf, anf = one(dyf, af, yf, daf, dbf, gf, anf, s, s - 1)
                gr, anr = one(dyr, ar, yr, dar, dbr, gr, anr, last - s, last - s + 1)
            return gf, anf, gr, anr

        carry = (gf_s[...], anf_s[...], gr_s[...], anr_s[...])
        carry = lax.fori_loop(0, ROW_TILE // 8 - 1, step, carry)
        gf, anf, gr, anr = carry
        for s in range(7, 0, -1):
            gf, anf = one(dyf, af, yf, daf, dbf, gf, anf, s, s - 1)
            gr, anr = one(dyr, ar, yr, dar, dbr, gr, anr, last - s, last - s + 1)
        gf0 = dyf[0:1, :] + anf * gf
        dbf[0:1, :] = gf0
        daf[0:1, :] = gf0 * hf0[...]
        gr0 = dyr[last:last + 1, :] + anr * gr
        dbr[last:last + 1, :] = gr0
        dar[last:last + 1, :] = gr0 * hr0[...]
        gf_s[...] = gf0
        anf_s[...] = af[0:1, :]
        gr_s[...] = gr0
        anr_s[...] = ar[last:last + 1, :]

    return _pcall(
        kern, name="scan_bwd", grid=(N_SCAN,),
        in_specs=[fwd_spec, fwd_spec, fwd_spec, hin_spec, rev_spec, rev_spec, rev_spec, hin_spec],
        out_specs=[fwd_spec, fwd_spec, rev_spec, rev_spec],
        out_shape=[_sds((TA, R), F32)] * 4,
        scratch_shapes=[pltpu.VMEM((1, R), F32)] * 4, compiler_params=_cparams(),
    )(dy, a_f, y_f, hin_f, dy, a_r, y_r, hin_r)


def _me():
    return lax.axis_index("x"), lax.axis_index("y"), lax.axis_index("c")


def _other_chips(mx, my):
    return [(1 - mx, my), (mx, 1 - my), (1 - mx, 1 - my)]


def _rcopy(src, dst, ssem, rsem, dev):
    return pltpu.make_async_remote_copy(src_ref=src, dst_ref=dst, send_sem=ssem, recv_sem=rsem,
                                        device_id=dev, device_id_type=MESH)


def _allgather8(name, x, dep=None):
    rows, cols = x.shape
    n_dep = len(_behind(dep))

    def kern(x_ref, *rest):
        o_ref, ssem, rsem, lsem = rest[n_dep:]
        mx, my, mc = _me()
        me = 4 * mx + 2 * my + mc
        peers = []
        for k in range(1, 8):
            px = 1 - mx if (k >> 2) & 1 else mx
            py = 1 - my if (k >> 1) & 1 else my
            pc = 1 - mc if k & 1 else mc
            peers.append((px, py, pc))
        mine = pltpu.make_async_copy(x_ref, o_ref.at[me], lsem)
        mine.start()
        sends = [_rcopy(x_ref, o_ref.at[me], ssem.at[k], rsem.at[k], p) for k, p in enumerate(peers)]
        for cp in sends:
            cp.start()
        for k, (px, py, pc) in enumerate(peers):
            _rcopy(x_ref, o_ref.at[4 * px + 2 * py + pc], ssem.at[k], rsem.at[k], (px, py, pc)).wait_recv()
        for cp in sends:
            cp.wait_send()
        mine.wait()

    return _pcall(
        kern, name=name, in_specs=[ANY] * (1 + n_dep), out_specs=ANY, out_shape=_sds((8, rows, cols), F32),
        scratch_shapes=[pltpu.SemaphoreType.DMA((7,)), pltpu.SemaphoreType.DMA((7,)), pltpu.SemaphoreType.DMA(())],
    )(x, *_behind(dep))


def _peers7(mx, my, mc):
    peers = []
    for k in range(1, 8):
        peers.append((1 - mx if (k >> 2) & 1 else mx, 1 - my if (k >> 1) & 1 else my, 1 - mc if k & 1 else mc))
    return peers


def _share_halves(name, fulls):
    n = len(fulls)

    def kern(*refs):
        o = refs[n:2 * n]
        ss, rs = refs[2 * n:]
        mx, my, mc = _me()
        sib = (mx, my, 1 - mc)
        sends = []
        for t in range(n):
            cp = _rcopy(o[t].at[mc], o[t].at[mc], ss.at[t], rs.at[t], sib)
            cp.start()
            sends.append(cp)
        for t in range(n):
            _rcopy(o[t].at[1 - mc], o[t].at[1 - mc], ss.at[t], rs.at[t], sib).wait_recv()
        for cp in sends:
            cp.wait_send()

    dma = pltpu.SemaphoreType.DMA
    return _pcall(
        kern, name=name, in_specs=[ANY] * n, out_specs=[ANY] * n,
        out_shape=[_sds(f.shape, f.dtype) for f in fulls], input_output_aliases={t: t for t in range(n)},
        scratch_shapes=[dma((n,)), dma((n,))],
    )(*fulls)


def _tiled_sp(name, fn, grid, sp, ins, outs):
    n_in = len(ins)

    def kern(sp_ref, *refs):
        tout = fn([r[...] for r in refs[:n_in]])
        for r, v in zip(refs[n_in:], tout):
            r[...] = v.astype(r.dtype)

    gs = pltpu.PrefetchScalarGridSpec(num_scalar_prefetch=1, grid=tuple(grid),
                                      in_specs=[s for _, s in ins], out_specs=[s for _, s in outs])
    res = _pcall(kern, name=name, grid_spec=gs, out_shape=[o for o, _ in outs], compiler_params=_cparams(),
                 )(sp, *[a for a, _ in ins])
    return list(res)


def _row_tile(rows, cols, itemsize=4, budget=2 * 1024 * 1024):
    tr = rows
    while tr * cols * itemsize > budget and tr % 32 == 0:
        tr //= 2
    return tr


def _place_big(shards, place, dep=None):
    slots = []
    for tag, s, layer in shards:
        rr, cc = s.shape[2], s.shape[3]
        tr = _row_tile(rr, cc)
        (slot,) = _tiled_sp(
            f"place_{tag}", lambda tin: [tin[0]], (2, rr // tr), place,
            [(s, pl.BlockSpec((None, None, tr, cc), lambda h, i, sp, layer=layer: (layer, h, i, 0)))]
            + [(d, pl.BlockSpec(d.shape, lambda h, i, sp: (0, 0))) for d in _behind(dep)],
            [(_sds((4, 2, rr, cc), BF16), pl.BlockSpec((None, None, tr, cc), lambda h, i, sp: (sp[0], h, i, 0)))])
        slots.append(slot)
    return slots


def _allreduce_small_begin(vec, place, after):
    hr = vec.shape[0] // 2
    tr = _row_tile(hr, LANE)
    blk = (None, None, tr, LANE)
    (pair,) = _tiled_sp(
        "small_place", lambda tin: [tin[0]], (2, hr // tr), place,
        [(vec.reshape(2, hr, LANE), pl.BlockSpec((None, tr, LANE), lambda h, i, sp: (h, i, 0)))],
        [(_sds((2, 2, hr, LANE), F32), pl.BlockSpec(blk, lambda h, i, sp: (sp[1], h, i, 0)))])
    (pair,) = _share_halves("small_share", [pair])
    (slot,) = _tiled_sp(
        "small_pair_add", lambda tin: [tin[0] + tin[1]], (2, hr // tr), place,
        [(pair, pl.BlockSpec(blk, lambda h, i, sp: (0, h, i, 0))),
         (pair, pl.BlockSpec(blk, lambda h, i, sp: (1, h, i, 0)))],
        [(_sds((4, 2, hr, LANE), F32), pl.BlockSpec(blk, lambda h, i, sp: (sp[0], h, i, 0)))])
    fly, sems, token = _gather_start("small_start", [slot], ((0,),), after)
    return (fly, sems), token


def _allreduce_small_end(state, after):
    fly, sems = state
    (chips,) = _swap_halves("small_swap", _gather_wait("small_wait", fly, *sems, after))
    hr = chips.shape[2]
    tr = _row_tile(hr, LANE)
    blk = (None, None, tr, LANE)
    (total,) = _tiled(
        "small_chip_sum", lambda ids, tin, vin: ([((tin[0] + tin[1]) + tin[2]) + tin[3]], []), (2, hr // tr),
        [(chips, pl.BlockSpec(blk, lambda h, i, _j=j: (_j, h, i, 0))) for j in range(4)], [],
        [(_sds((2, hr, LANE), F32), pl.BlockSpec((None, tr, LANE), lambda h, i: (h, i, 0)))])
    return total.reshape(2 * hr, LANE)


SEM =pl.BlockSpec(memory_space=pltpu.SEMAPHORE)
_DATAFLOW = pltpu.SideEffectType.DATAFLOW_SIDE_EFFECTING


def _gather_start(name, slots, groups, after):
    n = len(slots)

    def kern(*refs):
        o = refs[n + 1:2 * n + 1]
        sems, token = refs[2 * n + 1:-1], refs[-1]
        mx, my, mc = _me()
        j0 = 2 * mx + my
        for gi, grp in enumerate(groups):
            for k, t in enumerate(grp):
                for q, (qx, qy) in enumerate(_other_chips(mx, my)):
                    _rcopy(o[t].at[j0, mc], o[t].at[j0, mc], sems[2 * gi].at[3 * k + q],
                           sems[2 * gi + 1].at[3 * k + q], (qx, qy, mc)).start()
        token[...] = jnp.zeros_like(token)

    sem_shapes = []
    for grp in groups:
        sem_shapes += [pltpu.SemaphoreType.DMA((3 * len(grp),))] * 2
    res = _pcall(
        kern, name=name, in_specs=[ANY] * (n + 1),
        out_specs=[ANY] * n + [SEM] * len(sem_shapes) + [pl.BlockSpec(memory_space=pltpu.VMEM)],
        out_shape=[_sds(w.shape, w.dtype) for w in slots] + sem_shapes + [_sds((8, LANE), F32)],
        input_output_aliases={t: t for t in range(n)},
        compiler_params=pltpu.CompilerParams(has_side_effects=_DATAFLOW),
    )(*slots, after)
    return list(res[:n]), list(res[n:-1]), res[-1]


def _gather_wait(name, bufs, ssem, rsem, after):
    n = len(bufs)

    def kern(*refs):
        b = refs[:n]
        ssem_ref, rsem_ref = refs[n], refs[n + 1]
        mx, my, mc = _me()
        j0 = 2 * mx + my
        for k in range(n):
            for q, (qx, qy) in enumerate(_other_chips(mx, my)):
                jq = 2 * qx + qy
                _rcopy(b[k].at[jq, mc], b[k].at[jq, mc], ssem_ref.at[3 * k + q], rsem_ref.at[3 * k + q],
                       (qx, qy, mc)).wait_recv()
                _rcopy(b[k].at[j0, mc], b[k].at[j0, mc], ssem_ref.at[3 * k + q], rsem_ref.at[3 * k + q],
                       (qx, qy, mc)).wait_send()

    return list(_pcall(
        kern, name=name, in_specs=[ANY] * n + [SEM, SEM, ANY], out_specs=[ANY] * n,
        out_shape=[_sds(w.shape, w.dtype) for w in bufs], input_output_aliases={k: k for k in range(n)},
        compiler_params=pltpu.CompilerParams(has_side_effects=_DATAFLOW),
    )(*bufs, ssem, rsem, after))


def _swap_halves(name, bufs):
    n = len(bufs)

    def kern(*refs):
        o = refs[n:2 * n]
        ss, rs = refs[2 * n:]
        mx, my, mc = _me()
        sib = (mx, my, 1 - mc)
        sends = []
        for k in range(n):
            for q, (qx, qy) in enumerate(_other_chips(mx, my)):
                jq = 2 * qx + qy
                cp = _rcopy(o[k].at[jq, mc], o[k].at[jq, mc], ss.at[3 * k + q], rs.at[3 * k + q], sib)
                cp.start()
                sends.append(cp)
        for k in range(n):
            for q, (qx, qy) in enumerate(_other_chips(mx, my)):
                jq = 2 * qx + qy
                _rcopy(o[k].at[jq, 1 - mc], o[k].at[jq, 1 - mc], ss.at[3 * k + q], rs.at[3 * k + q], sib).wait_recv()
        for cp in sends:
            cp.wait_send()

    dma = pltpu.SemaphoreType.DMA
    return list(_pcall(
        kern, name=name, in_specs=[ANY] * n, out_specs=[ANY] * n,
        out_shape=[_sds(w.shape, w.dtype) for w in bufs], input_output_aliases={k: k for k in range(n)},
        scratch_shapes=[dma((3 * n,)), dma((3 * n,))],
    )(*bufs))


def _swap_start(name, bufs, after):
    n = len(bufs)

    def kern(*refs):
        o = refs[n + 1:2 * n + 1]
        ssem, rsem, token = refs[2 * n + 1:]
        mx, my, mc = _me()
        for k in range(n):
            for q, (qx, qy) in enumerate(_other_chips(mx, my)):
                jq = 2 * qx + qy
                _rcopy(o[k].at[jq, mc], o[k].at[jq, mc], ssem.at[3 * k + q], rsem.at[3 * k + q], (mx, my, 1 - mc)).start()
        token[...] = jnp.zeros_like(token)

    dma = pltpu.SemaphoreType.DMA
    res = _pcall(
        kern, name=name, in_specs=[ANY] * (n + 1),
        out_specs=[ANY] * n + [SEM, SEM, pl.BlockSpec(memory_space=pltpu.VMEM)],
        out_shape=[_sds(w.shape, w.dtype) for w in bufs] + [dma((3 * n,)), dma((3 * n,)), _sds((8, LANE), F32)],
        input_output_aliases={k: k for k in range(n)},
        compiler_params=pltpu.CompilerParams(has_side_effects=_DATAFLOW),
    )(*bufs, after)
    return (list(res[:n]), res[n], res[n + 1]), res[n + 2]


def _swap_wait(name, bufs, ssem, rsem, after):
    n = len(bufs)

    def kern(*refs):
        b = refs[:n]
        ssem_ref, rsem_ref = refs[n], refs[n + 1]
        mx, my, mc = _me()
        sib = (mx, my, 1 - mc)
        for k in range(n):
            for q, (qx, qy) in enumerate(_other_chips(mx, my)):
                jq = 2 * qx + qy
                _rcopy(b[k].at[jq, 1 - mc], b[k].at[jq, 1 - mc], ssem_ref.at[3 * k + q], rsem_ref.at[3 * k + q],
                       sib).wait_recv()
                _rcopy(b[k].at[jq, mc], b[k].at[jq, mc], ssem_ref.at[3 * k + q], rsem_ref.at[3 * k + q],
                       sib).wait_send()

    return list(_pcall(
        kern, name=name, in_specs=[ANY] * n + [SEM, SEM, ANY], out_specs=[ANY] * n,
        out_shape=[_sds(w.shape, w.dtype) for w in bufs], input_output_aliases={k: k for k in range(n)},
        compiler_params=pltpu.CompilerParams(has_side_effects=_DATAFLOW),
    )(*bufs, ssem, rsem, after))


def _to_sibling(mx, my, mc):
    return [((j, 1 - mc), j, (mx, my, 1 - mc)) for j in range(4)]


def _to_chips(mx, my, mc):
    return [((2 * qx + qy,), q, (qx, qy, mc)) for q, (qx, qy) in enumerate(_other_chips(mx, my))]


def _send_start(name, srcs, plan, land_shapes, after):
    n = len(srcs)
    per = len(plan(0, 0, 0))

    def kern(*refs):
        s, land = refs[n + 1:2 * n + 1], refs[2 * n + 1:3 * n + 1]
        ssem, rsem, token = refs[3 * n + 1:]
        for k in range(n):
            for q, (idx, slot, dev) in enumerate(plan(*_me())):
                _rcopy(s[k].at[idx], land[k].at[slot], ssem.at[per * k + q], rsem.at[per * k + q], dev).start()
        token[...] = jnp.zeros_like(token)

    dma = pltpu.SemaphoreType.DMA
    res = _pcall(
        kern, name=name, in_specs=[ANY] * (n + 1),
        out_specs=[ANY] * (2 * n) + [SEM, SEM, pl.BlockSpec(memory_space=pltpu.VMEM)],
        out_shape=[_sds(s.shape, s.dtype) for s in srcs] + [_sds(ls, s.dtype) for ls, s in zip(land_shapes, srcs)]
        + [dma((per * n,)), dma((per * n,)), _sds((8, LANE), F32)],
        input_output_aliases={k: k for k in range(n)},
        compiler_params=pltpu.CompilerParams(has_side_effects=_DATAFLOW),
    )(*srcs, after)
    return (list(res[:n]), list(res[n:2 * n]), res[2 * n], res[2 * n + 1]), res[2 * n + 2]


def _send_wait(name, srcs, lands, ssem, rsem, plan, after):
    n = len(srcs)
    per = len(plan(0, 0, 0))

    def kern(*refs):
        s, land = refs[:n], refs[n:2 * n]
        ssem_ref, rsem_ref = refs[2 * n], refs[2 * n + 1]
        for k in range(n):
            for q, (idx, slot, dev) in enumerate(plan(*_me())):
                cp = _rcopy(s[k].at[idx], land[k].at[slot], ssem_ref.at[per * k + q], rsem_ref.at[per * k + q], dev)
                cp.wait_recv()
                cp.wait_send()

    res = _pcall(
        kern, name=name, in_specs=[ANY] * (2 * n) + [SEM, SEM, ANY], out_specs=[ANY] * (2 * n),
        out_shape=[_sds(a.shape, a.dtype) for a in list(srcs) + list(lands)],
        input_output_aliases={k: k for k in range(2 * n)},
        compiler_params=pltpu.CompilerParams(has_side_effects=_DATAFLOW),
    )(*srcs, *lands, ssem, rsem, after)
    return list(res[:n]), list(res[n:])


def _reduce_begin(tag, parts, after):
    return _send_start(f"pair_start_{tag}", parts, _to_sibling, [(4,) + p.shape[2:] for p in parts], after)


def _reduce_mid(tag, pairing, place, after):
    parts, theirs = _send_wait(f"pair_wait_{tag}", *pairing, _to_sibling, after)
    sums = []
    for k, (p, o) in enumerate(zip(parts, theirs)):
        rr, cc = p.shape[2], p.shape[3]
        tr = _row_tile(rr, cc)
        (s_k,) = _tiled_sp(
            f"pair_add_{tag}{k}", lambda tin: [tin[0].astype(F32) + tin[1].astype(F32)], (4, rr // tr), place,
            [(p, pl.BlockSpec((None, None, tr, cc), lambda j, i, sp: (j, sp[1], i, 0))),
             (o, pl.BlockSpec((None, tr, cc), lambda j, i, sp: (j, i, 0)))],
            [(_sds((4, rr, cc), BF16), pl.BlockSpec((None, tr, cc), lambda j, i, sp: (j, i, 0)))])
        sums.append(s_k)
    return _send_start(f"chips_start_{tag}", sums, _to_chips, [(3,) + s.shape[1:] for s in sums], theirs[0])


def _reduce_end(tag, flying, place, after):
    sums, lands = _send_wait(f"chips_wait_{tag}", *flying, _to_chips, after)
    fulls = []
    for k, (s, q) in enumerate(zip(sums, lands)):
        rr, cc = q.shape[1], q.shape[2]
        tr = _row_tile(rr, cc)

        def add4(tin):
            return [((tin[0].astype(F32) + tin[1].astype(F32)) + tin[2].astype(F32)) + tin[3].astype(F32)]

        ins = [(s, pl.BlockSpec((None, tr, cc), lambda i, sp: (sp[0], i, 0)))]
        ins += [(q, pl.BlockSpec((None, tr, cc), lambda i, sp, _k=kk: (_k, i, 0))) for kk in range(3)]
        (f_k,) = _tiled_sp(f"chip_add_{tag}{k}", add4, (rr // tr,), place, ins,
                           [(_sds((2, rr, cc), F32), pl.BlockSpec((None, tr, cc), lambda i, sp: (sp[1], i, 0)))])
        fulls.append(f_k)
    return fulls


def _pack(parts, PACK_ROWS=PACK_ROWS):
    flat, offs, pos = [], [], 0
    for p in parts:
        v = p.reshape(-1).astype(F32)
        n = -(-v.shape[0] // LANE) * LANE
        flat.append(jnp.pad(v, (0, n - v.shape[0])))
        offs.append((pos, v.shape[0], p.shape))
        pos += n
    total = -(-pos // (PACK_ROWS * LANE)) * PACK_ROWS * LANE
    flat.append(jnp.zeros((total - pos,), F32))
    return jnp.concatenate(flat).reshape(-1, LANE), offs


def _unpack(vec, offs):
    v = vec.reshape(-1)
    return [v[p:p + n].reshape(shape) for p, n, shape in offs]


def _adamw_math(wv, gv, mv, vv):
    bc1 = 1.0 - ADAM_B1 ** ADAM_STEP
    bc2 = 1.0 - ADAM_B2 ** ADAM_STEP
    mn = ADAM_B1 * mv + (1.0 - ADAM_B1) * gv
    vn = ADAM_B2 * vv + (1.0 - ADAM_B2) * (gv * gv)
    delta = -ADAM_LR * ((mn / bc1) / (jnp.sqrt(vn / bc2) + ADAM_EPS) + ADAM_WD * wv)
    return delta, mn, vn


def _adamw(name, w, g, m, v):
    rows, cols = w.shape
    tr = rows
    for cand in (512, 256, 128, 64, 32, 16, 8):
        if rows % cand == 0 and cand * cols * 4 <= 2 * 1024 * 1024:
            tr = cand
            break

    def fn(ids, tin, vin):
        return list(_adamw_math(*tin)), []

    spec = pl.BlockSpec((tr, cols), lambda i: (i, 0))
    outs = [(_sds((rows, cols), F32), spec)] * 3
    return _tiled(name, fn, (rows // tr,), [(a, spec) for a in (w, g, m, v)], [], outs)


def _adamw_many(name, ws, gs, ms, vs):
    n = len(ws)
    views = [(-1, a.shape[-1]) if a.ndim > 1 else (1, -1) for a in ws]
    flat = lambda arrs: [a.reshape(vw) for a, vw in zip(arrs, views)]

    def kern(*refs):
        ins, outs = refs[:4 * n], refs[4 * n:]
        for t in range(n):
            res = _adamw_math(*[ins[q * n + t][...] for q in range(4)])
            for q in range(3):
                outs[q * n + t][...] = res[q]

    shapes = [_sds(a.shape, F32) for a in flat(ws)]
    res = _pcall(kern, name=name, out_shape=shapes * 3, compiler_params=_cparams(),
                 )(*flat(ws), *flat(gs), *flat(ms), *flat(vs))
    back = lambda part: [a.reshape(w.shape) for a, w in zip(part, ws)]
    return back(res[:n]), back(res[n:2 * n]), back(res[2 * n:])


def _pos_embed():
    n_rows = T // GRID_W
    q = D // 4
    omega = 1.0 / (10000.0 ** (jnp.arange(q, dtype=F32) / q))
    er = jnp.arange(n_rows, dtype=jnp.int32).astype(F32)[:, None] * omega[None, :]
    ec = jnp.arange(GRID_W, dtype=jnp.int32).astype(F32)[:, None] * omega[None, :]
    by_row = jnp.concatenate([jnp.sin(er), jnp.cos(er)], axis=-1)
    by_col = jnp.concatenate([jnp.sin(ec), jnp.cos(ec)], axis=-1)
    return jnp.concatenate([jnp.repeat(by_row, GRID_W, axis=0), jnp.tile(by_col, (n_rows, 1))], axis=-1)


def _dense_gates(w_a, w_x):
    rows = jnp.stack([w_a[0], w_x[0], w_a[1], w_x[1]]).reshape(4, 2, RH, BLK)
    mask, spread = _block_mask(), _block_spread().T.astype(BF16)

    def kern(r_ref, m_ref, s_ref, o_ref):
        tiled = jnp.dot(r_ref[...].astype(BF16), s_ref[...], preferred_element_type=F32)
        o_ref[...] = (tiled * m_ref[...]).astype(o_ref.dtype)

    return _pcall(
        kern, name="gates_dense", grid=(2, 4),
        in_specs=[pl.BlockSpec((None, None, RH, BLK), lambda h, q: (q, h, 0, 0)),
                  pl.BlockSpec((RH, RH), lambda h, q: (0, 0)), pl.BlockSpec((BLK, RH), lambda h, q: (0, 0))],
        out_specs=pl.BlockSpec((None, RH, RH), lambda h, q: (h, 0, q)),
        out_shape=_sds((2, RH, NQ), BF16),
    )(rows, mask, spread)


def _block_mask():
    r = lax.broadcasted_iota(jnp.int32, (RH, RH), 0) // BLK
    c = lax.broadcasted_iota(jnp.int32, (RH, RH), 1) // BLK
    return (r == c).astype(F32)


def _block_spread():
    c = lax.broadcasted_iota(jnp.int32, (RH, BLK), 0) % BLK
    j = lax.broadcasted_iota(jnp.int32, (RH, BLK), 1)
    return (c == j).astype(F32)


def _fold_blocks(dense, mask, spread):
    return jnp.dot(dense * mask, spread, preferred_element_type=F32, precision=lax.Precision.HIGHEST)


def _gate_block_grads(folded):
    per = N_BLK // 2
    kinds = [jnp.concatenate([folded[h, q].reshape(per, BLK, BLK) for h in range(2)], axis=0) for q in range(4)]
    return jnp.stack([kinds[0], kinds[2]]), jnp.stack([kinds[1], kinds[3]])


def _gate_bias_dense(b_a, b_x):
    cols = []
    for h in range(2):
        for src in (b_a[0], b_x[0], b_a[1], b_x[1]):
            cols.append(src.reshape(R)[h * RH:(h + 1) * RH])
    return jnp.concatenate(cols).reshape(1, 2 * NQ)


def _gate_bias_grads(dgb):
    v = dgb.reshape(2, 4, RH)
    kinds = [jnp.concatenate([v[0, q], v[1, q]]).reshape(N_BLK, BLK) for q in range(4)]
    return jnp.stack([kinds[0], kinds[2]]), jnp.stack([kinds[1], kinds[3]])


def _residual_epilogue(next_norm):
    def epi(acc, ex):
        x_new = ex[0] + ex[1] * acc
        outs = [acc, x_new]
        if next_norm:
            outs.append(_norm_mod(x_new, ex[-3], ex[-2], ex[-1]))
        return outs
    return epi


def _mlp_fwd(tag, x_in, h, gate, w_in, w_out, next_norm=None, dep=None):
    tm = MM_TILE
    (r,) = _mm(f"{tag}_in", h, w_in, _NN, (T // tm, 4, 1),
               pl.BlockSpec((tm, D), lambda i, j, k: (i, 0)), pl.BlockSpec((None, D, D), lambda i, j, k: (j, 0, 0)),
               [(_sds((T, FF), BF16), pl.BlockSpec((tm, D), lambda i, j, k: (i, j)))], (tm, D),
               extra=[(d_, _full_spec(d_)) for d_ in _behind(dep)], epi=lambda acc, ex: [jnp.maximum(acc, 0.0)])
    row_spec = pl.BlockSpec((tm, D), lambda i, j, k: (i, 0))
    outs = [(_sds((T, D), F32), row_spec)] * 2 + ([(_sds((T, D), BF16), row_spec)] if next_norm else [])
    res = _mm(f"{tag}_out", r, w_out, _NN, (T // tm, 1, FF // D),
              pl.BlockSpec((tm, D), lambda i, j, k: (i, k)), pl.BlockSpec((D, D), lambda i, j, k: (k, 0)),
              outs, (tm, D),
              extra=[(x_in, row_spec), (gate, _full_spec(gate))] + [(v, _full_spec(v)) for v in next_norm or ()],
              a_pre=lambda a: a * a, epi=_residual_epilogue(next_norm))
    return dict(h=h, r=r, o=res[0], x_in=x_in), res[1], (res[2] if next_norm else None)


def _behind(dep):
    return [] if dep is None else [dep]


def _gate_bwd(tag, dx, o, gate, dep=None):
    def fn(ids, t, v):
        d_o = t[0] * v[0]
        return [d_o], [_sum0(t[0] * t[1]), _sum0(d_o)]
    return _tiled(f"{tag}_gate_bwd", fn, (T // ROW_TILE,), [_rows(dx), _rows(o)], [gate] + _behind(dep),
                  [_orow(T, D, BF16)], [(1, D), (1, D)])


def _norm_bwd(tag, dx_res, dh, dh_off, x, g_norm, sc, with_dx=True, dep=None):
    n_t = x.shape[0] // ROW_TILE

    def fn(ids, t, v):
        if with_dx:
            dres, dhv, xv = t
        else:
            dhv, xv = t
        dxv, d_sh, d_sc, d_g = _norm_mod_bwd(dhv, xv, v[0], v[1])
        return ([dres + dxv] if with_dx else []), [d_sh, d_sc, d_g]

    ins = ([_rows(dx_res)] if with_dx else []) + [_rows(dh, off=dh_off), _rows(x)]
    outs = [_orow(x.shape[0], D, F32)] if with_dx else []
    return _tiled(f"{tag}_norm_bwd", fn, (n_t,), ins, [g_norm, sc] + _behind(dep), outs, [(1, D)] * 3)


def _mlp_bwd(tag, dx, saved, g_norm, sc, gate, w_in, w_out, dep=None):
    d_o, d_gate, _ = _gate_bwd(tag, dx, saved["o"], gate, dep)
    tm = MM_TILE
    r = saved["r"]
    (da,) = _mm(f"{tag}_dz", d_o, w_out, _NT, (T // tm, FF // D, 1),
                pl.BlockSpec((tm, D), lambda i, j, k: (i, 0)), pl.BlockSpec((D, D), lambda i, j, k: (j, 0)),
                [(_sds((T, FF), BF16), pl.BlockSpec((tm, D), lambda i, j, k: (i, j)))], (tm, D),
                extra=[(r, pl.BlockSpec((tm, D), lambda i, j, k: (i, j)))],
                epi=lambda acc, ex: [acc * (2.0 * ex[0].astype(F32))])
    tk = MM_TILE
    (dw_out,) = _mm(f"{tag}_dwout", r, d_o, _TN, (FF // tm, 1, T // tk),
                    pl.BlockSpec((tk, tm), lambda i, j, k: (k, i)), pl.BlockSpec((tk, D), lambda i, j, k: (k, 0)),
                    [(_sds((FF, D), BF16), pl.BlockSpec((tm, D), lambda i, j, k: (i, 0)))], (tm, D),
                    a_pre=lambda a: a * a)
    (dh,) = _mm(f"{tag}_dh", da, w_in, _NT, (T // tm, 1, 4),
                pl.BlockSpec((tm, D), lambda i, j, k: (i, k)), pl.BlockSpec((None, D, D), lambda i, j, k: (k, 0, 0)),
                [(_sds((T, D), F32), pl.BlockSpec((tm, D), lambda i, j, k: (i, 0)))], (tm, D))
    (dw_in,) = _mm(f"{tag}_dwin", saved["h"], da, _TN, (D // tm, 4, T // tk),
                   pl.BlockSpec((tk, tm), lambda i, j, k: (k, i)), pl.BlockSpec((tk, D), lambda i, j, k: (k, j)),
                   [(_sds((4, D, D), BF16), pl.BlockSpec((None, tm, D), lambda i, j, k: (j, i, 0)))], (tm, D))
    dx_in, d_sh, d_sc, d_g = _norm_bwd(tag, dx, dh, 0, saved["x_in"], g_norm, sc)
    return dx_in, dw_in, dw_out, dict(sh=d_sh, sc=d_sc, gate=d_gate, g_norm=d_g)


def _local_step(x, ctx, tgt, mods, cmods, norm_g, final_g, rec, conf, wg, on_grads=None, wg_pre=None, on_later=None):
    on_grads = on_grads or (lambda group, dws: None)
    wg_pre = wg_pre or (lambda group, after: None)
    on_later = on_later or (lambda after: None)
    n_t = T // ROW_TILE
    row = lambda v: v.reshape(1, -1)
    m0 = [row(mods[0, q]) for q in range(6)]
    m1 = [row(mods[1, q]) for q in range(6)]
    g00, g01, g10, g11 = (row(norm_g[0, 0]), row(norm_g[0, 1]), row(norm_g[1, 0]), row(norm_g[1, 1]))
    csh, csc = row(cmods[0]), row(cmods[1])
    pos = _pos_embed()

    def prep0(ids, t, v):
        cx, xv, pv = t
        is_ctx = ids[0] == 0
        xin = jnp.where(is_ctx, cx, xv + pv)
        sh = jnp.where(is_ctx, v[3], v[1])
        sc = jnp.where(is_ctx, v[4], v[2])
        return [_norm_mod(xin, v[0], sc, sh), xv + pv], []

    hcat, x0 = _tiled(
        "prep0", prep0, (N_SCAN,),
        [(ctx, pl.BlockSpec((ROW_TILE, D), lambda i: (0, 0))), _rows(x, off=-1, clamp_lo=True),
         _rows(pos, off=-1, clamp_lo=True)],
        [g00, m0[0], m0[1], csh, csc],
        [_orow(TA, D, BF16), _orow(T, D, F32, off=-1, clamp_lo=True)])

    tm_a = REC_TILE
    w_rin = wg("rec_in", hcat)["rec_w_in"]
    (a_in,) = _mm("rec_in", hcat, w_rin, _NN, (TA // tm_a, 4, 1),
                  pl.BlockSpec((tm_a, D), lambda i, j, k: (i, 0)),
                  pl.BlockSpec((None, D, RH), lambda i, j, k: (j, 0, 0)),
                  [(_sds((TA, 2 * R), F32), pl.BlockSpec((tm_a, RH), lambda i, j, k: (i, j)))], (tm_a, RH))
    rec_starts = (0, 1)
    u = _dwconv("rec_conv", a_in, R // CW_REC, rec["conv_w"], row(rec["conv_b"]), 1, rec_starts, R, CW_REC)
    wbd = _dense_gates(rec["w_a"], rec["w_x"])
    gbias = _gate_bias_dense(rec["b_a"], rec["b_x"])
    lam = rec["lam"]
    a_f, b_f, a_r, b_r = _tiled("rg_fwd", _rg_fwd_fn, (TA // RG_TILE,), [_rows(u, tm=RG_TILE)], [wbd, gbias, lam],
                                [_orow(TA, R, F32, tm=RG_TILE)] * 4, vec_refs=True)
    dep = wg_pre("rec_out", a_f)
    dep = wg_pre("mlp0", a_f if dep is None else dep)
    y_f, y_r, hin_f, hin_r = _scan_fwd(a_f, b_f, a_r, b_r)

    def rec_mid(ids, t, v):
        gp, yf, yr = t
        g, _ = _gelu(gp)
        return [g * (yf + yr)], []

    (m_rec,) = _tiled("rec_mid", rec_mid, (n_t,),
                      [_rows(a_in, R, off=1), _rows(y_f, off=1), _rows(y_r, off=1)], _behind(dep),
                      [_orow(T, R, BF16)])
    tm = MM_TILE
    row_spec = pl.BlockSpec((tm, D), lambda i, j, k: (i, 0))
    norm_mlp0 = (g01, m0[4], m0[3])
    w_rout = wg("rec_out", m_rec)["rec_w_out"]
    o_rec, x1, h_mlp0 = _mm(
        "rec_out", m_rec, w_rout, _NN, (T // tm, 1, 1),
        pl.BlockSpec((tm, R), lambda i, j, k: (i, 0)), pl.BlockSpec((R, D), lambda i, j, k: (0, 0)),
        [(_sds((T, D), F32), row_spec)] * 2 + [(_sds((T, D), BF16), row_spec)], (tm, D),
        extra=[(x0, row_spec), (m0[2], _full_spec(m0[2]))] + [(v, _full_spec(v)) for v in norm_mlp0],
        epi=_residual_epilogue(norm_mlp0))
    w_m0 = wg("mlp0", x1)
    dep = wg_pre("conf", x1)
    mlp0, x2, h1 = _mlp_fwd("mlp0", x1, h_mlp0, m0[5], w_m0["w_in"], w_m0["w_out"], (g10, m1[1], m1[0]), dep)

    b_pw1 = row(conf["b_pw1"])
    w_cf = wg("conf", x2)
    dep = wg_pre("mlp1", x2)
    (pre,) = _mm("conf_pw1", h1, w_cf["conf_w_pw1"], _NN, (T // tm, 4, 1),
                 pl.BlockSpec((tm, D), lambda i, j, k: (i, 0)),
                 pl.BlockSpec((None, D, D // 2), lambda i, j, k: (j, 0, 0)),
                 [(_sds((T, 2 * D), F32), pl.BlockSpec((tm, D // 2), lambda i, j, k: (i, j)))], (tm, D // 2),
                 extra=[(b_pw1, pl.BlockSpec((1, D // 2), lambda i, j, k: (0, j)))]
                 + [(d_, _full_spec(d_)) for d_ in _behind(dep)],
                 epi=lambda acc, ex: [acc + ex[0]])
    (zg,) = _tiled("conf_glu", lambda ids, t, v: ([t[0] * _sigmoid(t[1])], []), (n_t,),
                   [_rows(pre, D, col=0), _rows(pre, D, col=1)], [], [_orow(T, D, F32)])
    conf_starts = (0,)
    zc = _dwconv("conf_conv", zg, 0, conf["conv_w"], row(conf["conv_b"]), CONF_KW // 2, conf_starts, D, CW_CONF)
    ln_g, ln_b = row(conf["ln_g"]), row(conf["ln_b"])

    def ln_silu(ids, t, v):
        nh, _ = _layernorm_parts(t[0])
        ln = nh * v[0] + v[1]
        return [ln * _sigmoid(ln)], []

    (s_conf,) = _tiled("conf_ln", ln_silu, (n_t,), [_rows(zc)], [ln_g, ln_b], [_orow(T, D, BF16)])
    b_pw2 = row(conf["b_pw2"])
    norm_mlp1 = (g11, m1[4], m1[3])
    pw2_epi = _residual_epilogue(norm_mlp1)
    y_conf, x3, h_mlp1 = _mm(
        "conf_pw2", s_conf, w_cf["conf_w_pw2"], _NN, (T // tm, 1, 1),
        row_spec, pl.BlockSpec((D, D), lambda i, j, k: (0, 0)),
        [(_sds((T, D), F32), row_spec)] * 2 + [(_sds((T, D), BF16), row_spec)], (tm, D),
        extra=[(x2, row_spec), (m1[2], _full_spec(m1[2])), (b_pw2, _full_spec(b_pw2))]
        + [(v, _full_spec(v)) for v in norm_mlp1],
        epi=lambda acc, ex: pw2_epi(acc + ex[2], ex))
    w_m1 = wg("mlp1", x3)
    mlp1, x4, _ = _mlp_fwd("mlp1", x3, h_mlp1, m1[5], w_m1["w_in"], w_m1["w_out"])

    fg = row(final_g)

    def head(ids, t, v):
        n, r = _rms(t[0])
        err = n * v[0] - t[1]
        d_out = err * (1.0 / D)
        dn = d_out * v[0]
        dxv = r * (dn - n * jnp.mean(dn * n, axis=-1, keepdims=True))
        part = jnp.sum(_sum0(err * err), axis=1, keepdims=True) * (0.5 / D)
        return [dxv], [part, _sum0(d_out * n)]

    dx4, loss, d_fg = _tiled("head", head, (n_t,), [_rows(x4), _rows(tgt)], [fg], [_orow(T, D, F32)],
                             [(1, 1), (1, D)])

    dx3, dw_in1, dw_out1, dm_mlp1 = _mlp_bwd("mlp1", dx4, mlp1, g11, m1[4], m1[5],
                                             w_m1["w_in"], w_m1["w_out"])
    dep = on_grads("mlp1", (dw_in1, dw_out1))
    d_y, d_g1c, d_bpw2 = _gate_bwd("conf", dx3, y_conf, m1[2], dep)
    tk = MM_TILE
    (dw_pw2,) = _mm("conf_dwpw2", s_conf, d_y, _TN, (D // tm, 1, T // tk),
                    pl.BlockSpec((tk, tm), lambda i, j, k: (k, i)), pl.BlockSpec((tk, D), lambda i, j, k: (k, 0)),
                    [(_sds((D, D), BF16), pl.BlockSpec((tm, D), lambda i, j, k: (i, 0)))], (tm, D))
    (ds,) = _mm("conf_ds", d_y, w_cf["conf_w_pw2"], _NT, (T // tm, 1, 1),
                pl.BlockSpec((tm, D), lambda i, j, k: (i, 0)), pl.BlockSpec((D, D), lambda i, j, k: (0, 0)),
                [(_sds((T, D), F32), pl.BlockSpec((tm, D), lambda i, j, k: (i, 0)))], (tm, D))
    dep = on_later(ds)

    def ln_silu_bwd(ids, t, v):
        dsv, zcv = t
        nh, rstd = _layernorm_parts(zcv)
        ln = nh * v[0] + v[1]
        sg = _sigmoid(ln)
        d_ln = dsv * (sg * (1.0 + ln * (1.0 - sg)))
        d_nh = d_ln * v[0]
        d_zc = rstd * (d_nh - jnp.mean(d_nh, axis=-1, keepdims=True)
                       - nh * jnp.mean(d_nh * nh, axis=-1, keepdims=True))
        return [d_zc], [_sum0(d_ln * nh), _sum0(d_ln)]

    d_zc, d_lng, d_lnb = _tiled("conf_ln_bwd", ln_silu_bwd, (n_t,), [_rows(ds), _rows(zc)],
                                [ln_g, ln_b] + _behind(dep), [_orow(T, D, F32)], [(1, D), (1, D)])
    d_zg = _dwconv("conf_conv_dx", d_zc, 0, conf["conv_w"][::-1], jnp.zeros((1, D), F32),
                   CONF_KW - 1 - CONF_KW // 2, conf_starts, D, CW_CONF)

    def glu_bwd(ids, t, v):
        dz, pa, pb = t
        sg = _sigmoid(pb)
        d_a = dz * sg
        d_b = dz * pa * sg * (1.0 - sg)
        return [jnp.concatenate([d_a, d_b], axis=1)], [_sum0(d_a), _sum0(d_b)]

    d_pre, d_b1a, d_b1b = _tiled(
        "conf_glu_bwd", glu_bwd, (n_t,), [_rows(d_zg), _rows(pre, D, col=0), _rows(pre, D, col=1)], [],
        [_orow(T, 2 * D, BF16)], [(1, D), (1, D)])
    (dw_pw1,) = _mm("conf_dwpw1", h1, d_pre, _TN, (D // tm, 4, T // tk),
                    pl.BlockSpec((tk, tm), lambda i, j, k: (k, i)),
                    pl.BlockSpec((tk, D // 2), lambda i, j, k: (k, j)),
                    [(_sds((4, D, D // 2), BF16), pl.BlockSpec((None, tm, D // 2), lambda i, j, k: (j, i, 0)))],
                    (tm, D // 2))
    dep = on_grads("conf", (dw_pw1, dw_pw2))
    (dh1,) = _mm("conf_dh", d_pre, w_cf["conf_w_pw1"], _NT, (T // tm, 1, 4),
                 pl.BlockSpec((tm, D // 2), lambda i, j, k: (i, k)),
                 pl.BlockSpec((None, D, D // 2), lambda i, j, k: (k, 0, 0)),
                 [(_sds((T, D), F32), pl.BlockSpec((tm, D), lambda i, j, k: (i, 0)))], (tm, D))
    dx2, d_sh1c, d_sc1c, d_g10 = _norm_bwd("conf", dx3, dh1, 0, x2, g10, m1[1], dep=dep)
    dep = on_later(dx2)

    dx1, dw_in0, dw_out0, dm_mlp0 = _mlp_bwd("mlp0", dx2, mlp0, g01, m0[4], m0[5],
                                             w_m0["w_in"], w_m0["w_out"], dep)
    dep = on_grads("mlp0", (dw_in0, dw_out0))
    d_orec, d_g1r, _ = _gate_bwd("rec", dx1, o_rec, m0[2], dep)
    (dw_rout,) = _mm("rec_dwout", m_rec, d_orec, _TN, (R // RH, 1, T // tk),
                     pl.BlockSpec((tk, RH), lambda i, j, k: (k, i)), pl.BlockSpec((tk, D), lambda i, j, k: (k, 0)),
                     [(_sds((R, D), BF16), pl.BlockSpec((RH, D), lambda i, j, k: (i, 0)))], (RH, D))
    (dm_rec,) = _mm("rec_dm", d_orec, w_rout, _NT, (T // tm, 1, 1),
                    pl.BlockSpec((tm, D), lambda i, j, k: (i, 0)), pl.BlockSpec((R, D), lambda i, j, k: (0, 0)),
                    [(_sds((T, R), F32), pl.BlockSpec((tm, R), lambda i, j, k: (i, 0)))], (tm, R))
    dep = on_later(dm_rec)

    def rec_mid_bwd(ids, t, v):
        dmv, gp, yf, yr = t
        g, th = _gelu(gp)
        lat = ids[0] > 0
        d_gp = jnp.where(lat, dmv * (yf + yr) * _gelu_grad(gp, th), 0.0)
        dy = jnp.where(lat, dmv * g, 0.0)
        return [d_gp, dy], []

    d_gp, dy = _tiled("rec_mid_bwd", rec_mid_bwd, (N_SCAN,),
                      [_rows(dm_rec, off=-1, clamp_lo=True), _rows(a_in, R), _rows(y_f), _rows(y_r)], _behind(dep),
                      [_orow(TA, R, BF16), _orow(TA, R, F32)])
    da_f, db_f, da_r, db_r = _scan_bwd(dy, a_f, y_f, hin_f, a_r, y_r, hin_r)
    d_gpre, d_u, d_gbias, d_lam = _tiled(
        "rg_bwd", _rg_bwd_fn, (TA // RG_TILE,), [_rows(a, tm=RG_TILE) for a in (u, da_f, db_f, da_r, db_r)],
        [wbd, gbias, lam], [_orow(TA, 2 * NQ, BF16, tm=RG_TILE), _orow(TA, R, F32, tm=RG_TILE)],
        [(1, 2 * NQ), (1, 2 * R)], vec_refs=True)
    tk_a = REC_TILE
    d_p = _dwconv("rec_conv_dx", d_u, 0, rec["conv_w"][::-1], jnp.zeros((1, R), F32), REC_KW - 1 - 1,
                  rec_starts, R, CW_REC)
    d_a = jnp.concatenate([d_gp, d_p.astype(BF16)], axis=1)
    (dw_rin,) = _mm("rec_dwin", hcat, d_a, _TN, (D // tm, 4, TA // tk_a),
                    pl.BlockSpec((tk_a, tm), lambda i, j, k: (k, i)), pl.BlockSpec((tk_a, RH), lambda i, j, k: (k, j)),
                    [(_sds((4, D, RH), BF16), pl.BlockSpec((None, tm, RH), lambda i, j, k: (j, i, 0)))], (tm, RH))
    dep = on_grads("rec", (dw_rin, dw_rout))
    (dhcat,) = _mm("rec_dh", d_a, w_rin, _NT, (TA // tm_a, 1, 4),
                   pl.BlockSpec((tm_a, RH), lambda i, j, k: (i, k)),
                   pl.BlockSpec((None, D, RH), lambda i, j, k: (k, 0, 0)),
                   [(_sds((TA, D), F32), pl.BlockSpec((tm_a, D), lambda i, j, k: (i, 0)))], (tm_a, D))
    dx0, d_sh1r, d_sc1r, d_g00 = _norm_bwd("rec", dx1, dhcat, 1, x0, g00, m0[1], dep=dep)
    dep = on_later(dx0)

    d_csh, d_csc, d_g00c = _norm_bwd("ctx", None, dhcat, 0, ctx, g00, csc, with_dx=False, dep=dep)
    blk_mask, blk_spread = _block_mask(), _block_spread()
    (d_wbd,) = _mm("rg_dw", u, d_gpre, _TN, (2, 2, TA // tk_a),
                   pl.BlockSpec((tk_a, RH), lambda i, j, k: (k, i)),
                   pl.BlockSpec((tk_a, NQ // 2), lambda i, j, k: (k, 2 * i + j)),
                   [(_sds((2, 4, RH, BLK), F32), pl.BlockSpec((None, 2, RH, BLK), lambda i, j, k: (i, j, 0, 0)))],
                   (RH, NQ // 2),
                   extra=[(blk_mask, _full_spec(blk_mask)), (blk_spread, _full_spec(blk_spread))]
                   + [(d, _full_spec(d)) for d in _behind(dep)],
                   epi=lambda acc, ex: [jnp.stack([_fold_blocks(acc[:, s * RH:(s + 1) * RH], ex[0], ex[1])
                                                   for s in range(2)])])
    d_cw_rec = _dwconv_wgrad("rec_conv_dw", d_u, a_in, R // CW_REC, REC_KW, 1, rec_starts, R, CW_REC, dep)
    d_cw_conf = _dwconv_wgrad("conf_conv_dw", d_zc, zg, 0, CONF_KW, CONF_KW // 2, conf_starts, D, CW_CONF, dep)

    big = dict(rec_w_in=dw_rin, rec_w_out=dw_rout, conf_w_pw1=dw_pw1, conf_w_pw2=dw_pw2,
               mlp_w_in=(dw_in0, dw_in1), mlp_w_out=(dw_out0, dw_out1))
    d_wa, d_wx = _gate_block_grads(d_wbd)
    d_ba, d_bx = _gate_bias_grads(d_gbias)
    d_mod = jnp.concatenate([
        d_sh1r, d_sc1r, d_g1r, dm_mlp0["sh"], dm_mlp0["sc"], dm_mlp0["gate"],
        d_sh1c, d_sc1c, d_g1c, dm_mlp1["sh"], dm_mlp1["sc"], dm_mlp1["gate"]], axis=1).reshape(2, 6 * D)
    small = dict(
        d_mod=d_mod, d_cmod=jnp.concatenate([d_csh, d_csc], axis=1),
        norm_g=jnp.concatenate([d_g00 + d_g00c, dm_mlp0["g_norm"], d_g10, dm_mlp1["g_norm"]], axis=1),
        rec_conv_w=d_cw_rec[:REC_KW], rec_conv_b=d_cw_rec[REC_KW], rec_lambda=d_lam.reshape(2, R),
        rec_w_a=d_wa, rec_b_a=d_ba, rec_w_x=d_wx, rec_b_x=d_bx,
        conf_b_pw1=jnp.concatenate([d_b1a, d_b1b], axis=1), conf_conv_w=d_cw_conf[:CONF_KW],
        conf_conv_b=d_cw_conf[CONF_KW], conf_ln_g=d_lng, conf_ln_b=d_lnb, conf_b_pw2=d_bpw2, final_g=d_fg)
    return loss.reshape(()), dx0, big, small


_BIG = ("rec_w_in", "rec_w_out", "conf_w_pw1", "conf_w_pw2", "mlp_w_in", "mlp_w_out")


def _halves(w):
    return w.reshape(w.shape[0], 2, w.shape[1] // 2, w.shape[2])


def _ada_fwd(c16, w_ada, b_shard):
    ns = w_ada.shape[2]
    tn = 512

    def kern(c_ref, w_ref, b_ref, o_ref):
        cv = c_ref[...]
        s = (cv * _sigmoid(cv)).astype(BF16)
        o_ref[...] = jnp.dot(s, w_ref[...].astype(BF16), preferred_element_type=F32) + b_ref[...]

    return _pcall(
        kern, name="ada_fwd", grid=(2, ns // tn),
        in_specs=[pl.BlockSpec((16, D), lambda l, j: (0, 0)), pl.BlockSpec((None, D, tn), lambda l, j: (l, 0, j)),
                  pl.BlockSpec((None, 1, tn), lambda l, j: (l, 0, j))],
        out_specs=pl.BlockSpec((None, 16, tn), lambda l, j: (l, 0, j)),
        out_shape=_sds((2, 16, ns), F32), compiler_params=_cparams(),
    )(c16, w_ada, b_shard)


def _ada_bwd(c16, dm16, w_ada):
    ns = w_ada.shape[2]
    tn = 512

    def kern(c_ref, dm_ref, w_ref, gw_ref, ds_ref):
        cv = c_ref[...]
        s = (cv * _sigmoid(cv)).astype(BF16)
        dm = dm_ref[...].astype(BF16)
        gw_ref[...] = lax.dot_general(s, dm, _TN, preferred_element_type=F32)

        @pl.when(jnp.logical_and(pl.program_id(0) == 0, pl.program_id(1) == 0))
        def _():
            ds_ref[...] = jnp.zeros_like(ds_ref)

        ds_ref[...] += lax.dot_general(dm, w_ref[...].astype(BF16), _NT, preferred_element_type=F32)

    return _pcall(
        kern, name="ada_bwd", grid=(2, ns // tn),
        in_specs=[pl.BlockSpec((16, D), lambda l, j: (0, 0)), pl.BlockSpec((None, 16, tn), lambda l, j: (l, 0, j)),
                  pl.BlockSpec((None, D, tn), lambda l, j: (l, 0, j))],
        out_specs=[pl.BlockSpec((None, D, tn), lambda l, j: (l, 0, j)), pl.BlockSpec((16, D), lambda l, j: (0, 0))],
        out_shape=[_sds((2, D, ns), F32), _sds((16, D), F32)], compiler_params=_cparams(),
    )(c16, dm16, w_ada)


def _cctx_grad(ds8, c_ctx):
    def kern(d_ref, c_ref, o_ref):
        tot = d_ref[0, 8:9, :] + d_ref[2, 8:9, :] + d_ref[4, 8:9, :] + d_ref[6, 8:9, :]
        cv = c_ref[...]
        sg = _sigmoid(cv)
        o_ref[...] = tot * (sg * (1.0 + cv * (1.0 - sg)))

    return _pcall(kern, name="cctx_grad", out_shape=_sds((1, D), F32))(ds8, c_ctx.reshape(1, D))


def kernel(x, c, ctx, c_ctx, w_ada, b_ada, norm_g, rec_w_in, rec_conv_w, rec_conv_b, rec_lambda, rec_w_a, rec_b_a, rec_w_x, rec_b_x, rec_w_out, conf_w_pw1, conf_b_pw1, conf_conv_w, conf_conv_b, conf_ln_g, conf_ln_b, conf_w_pw2, conf_b_pw2, mlp_w_in, mlp_w_out, final_g, loss_target, m_c_ctx, m_w_ada, m_b_ada, m_norm_g, m_rec_w_in, m_rec_conv_w, m_rec_conv_b, m_rec_lambda, m_rec_w_a, m_rec_b_a, m_rec_w_x, m_rec_b_x, m_rec_w_out, m_conf_w_pw1, m_conf_b_pw1, m_conf_conv_w, m_conf_conv_b, m_conf_ln_g, m_conf_ln_b, m_conf_w_pw2, m_conf_b_pw2, m_mlp_w_in, m_mlp_w_out, m_final_g, v_c_ctx, v_w_ada, v_b_ada, v_norm_g, v_rec_w_in, v_rec_conv_w, v_rec_conv_b, v_rec_lambda, v_rec_w_a, v_rec_b_a, v_rec_w_x, v_rec_b_x, v_rec_w_out, v_conf_w_pw1, v_conf_b_pw1, v_conf_conv_w, v_conf_conv_b, v_conf_ln_g, v_conf_ln_b, v_conf_w_pw2, v_conf_b_pw2, v_mlp_w_in, v_mlp_w_out, v_final_g):
    names = ["c_ctx", "w_ada", "b_ada", "norm_g", "rec_w_in", "rec_conv_w", "rec_conv_b", "rec_lambda", "rec_w_a",
             "rec_b_a", "rec_w_x", "rec_b_x", "rec_w_out", "conf_w_pw1", "conf_b_pw1", "conf_conv_w", "conf_conv_b",
             "conf_ln_g", "conf_ln_b", "conf_w_pw2", "conf_b_pw2", "mlp_w_in", "mlp_w_out", "final_g"]
    w = dict(zip(names, [c_ctx, w_ada, b_ada, norm_g, rec_w_in, rec_conv_w, rec_conv_b, rec_lambda, rec_w_a,
                         rec_b_a, rec_w_x, rec_b_x, rec_w_out, conf_w_pw1, conf_b_pw1, conf_conv_w, conf_conv_b,
                         conf_ln_g, conf_ln_b, conf_w_pw2, conf_b_pw2, mlp_w_in, mlp_w_out, final_g]))
    m = dict(zip(names, [m_c_ctx, m_w_ada, m_b_ada, m_norm_g, m_rec_w_in, m_rec_conv_w, m_rec_conv_b, m_rec_lambda,
                         m_rec_w_a, m_rec_b_a, m_rec_w_x, m_rec_b_x, m_rec_w_out, m_conf_w_pw1, m_conf_b_pw1,
                         m_conf_conv_w, m_conf_conv_b, m_conf_ln_g, m_conf_ln_b, m_conf_w_pw2, m_conf_b_pw2,
                         m_mlp_w_in, m_mlp_w_out, m_final_g]))
    v = dict(zip(names, [v_c_ctx, v_w_ada, v_b_ada, v_norm_g, v_rec_w_in, v_rec_conv_w, v_rec_conv_b, v_rec_lambda,
                         v_rec_w_a, v_rec_b_a, v_rec_w_x, v_rec_b_x, v_rec_w_out, v_conf_w_pw1, v_conf_b_pw1,
                         v_conf_conv_w, v_conf_conv_b, v_conf_ln_g, v_conf_ln_b, v_conf_w_pw2, v_conf_b_pw2,
                         v_mlp_w_in, v_mlp_w_out, v_final_g]))
    mx, my, mc = _me()
    chip = 2 * mx + my
    me = 4 * mx + 2 * my + mc

    sharded_small = ["norm_g", "rec_conv_w", "rec_lambda", "conf_b_pw1", "conf_conv_w", "conf_conv_b", "conf_ln_g",
                     "conf_ln_b", "conf_b_pw2"]
    packed, offs = _pack([c] + [w[k] for k in sharded_small], 8)
    place = jnp.stack([chip, mc]).astype(jnp.int32)
    shards = [("rec_in", _halves(rec_w_in), 0), ("rec_out", _halves(rec_w_out), 0),
              ("pw1", _halves(conf_w_pw1), 0), ("pw2", _halves(conf_w_pw2), 0),
              ("mlp_in0", _halves(mlp_w_in), 0), ("mlp_in1", _halves(mlp_w_in), 1),
              ("mlp_out0", _halves(mlp_w_out), 0), ("mlp_out1", _halves(mlp_w_out), 1)]
    (slot_rin,) = _place_big(shards[:1], place)
    placed = jnp.broadcast_to(lax.dynamic_slice(slot_rin, (chip, 0, 0, 0), (1, 1, 1, 1)).reshape(1, 1), (8, 1))
    got = _allgather8("gather_small", packed, placed)

    got_flat = got.reshape(8, -1)

    def piece(i):
        p, n, shape = offs[i]
        return got_flat[:, p:p + n].reshape((8,) + tuple(shape))

    c_rows = piece(0).reshape(8, D)
    full = {}
    for i, k in enumerate(sharded_small):
        per_chip = jnp.moveaxis(piece(1 + i)[0::2], 0, -2)
        full[k] = per_chip.reshape(per_chip.shape[:-2] + (4 * per_chip.shape[-1],))
    c16 = jnp.concatenate([c_rows, c_ctx.reshape(1, D), jnp.zeros((7, D), F32)], axis=0)

    ns = w_ada.shape[2]
    b_shard = lax.dynamic_slice_in_dim(b_ada, chip * ns, ns, axis=1).reshape(2, 1, ns)
    prod = _ada_fwd(c16, w_ada, b_shard)

    own_rows = lax.dynamic_index_in_dim(prod[:, :8].reshape(2, 4, 2, ns), mc, axis=2, keepdims=False)
    rows = jnp.concatenate([own_rows.transpose(1, 0, 2), jnp.broadcast_to(prod[0, 8], (4, 1, ns)),
                            jnp.zeros((4, 5, ns), F32)], axis=1)
    mod_state, mod_started = _send_start("mod_start", [rows], _to_chips, [(3, 8, ns)], place)
    use_order = dict(rec=(0, 1), mlp0=(4, 6), conf=(2, 3), mlp1=(5, 7))
    fetch_order = dict(rec_out=(1,), mlp0=(4, 6), conf=(2, 3), mlp1=(5, 7))
    flying, gsems, swapping = {}, {}, {}
    flying["rec_in"], gsems["rec_in"], rec_started = _gather_start("gather_start_rec", [slot_rin], ((0,),),
                                                                   mod_started)
    slots = [slot_rin] + _place_big(shards[1:], place, rec_started)
    order = [t for g in fetch_order for t in fetch_order[g]]
    groups = [tuple(order.index(t) for t in fetch_order[g]) for g in fetch_order]
    fly, sems, all_started = _gather_start("gather_start_rest", [slots[t] for t in order], tuple(groups), rec_started)
    for gi, g in enumerate(fetch_order):
        flying[g], gsems[g] = [fly[k] for k in groups[gi]], sems[2 * gi:2 * gi + 2]

    def wg_pre(group, after):
        bufs = _gather_wait(f"gather_wait_{group}", flying[group], *gsems[group], after)
        swapping[group], token = _swap_start(f"swap_start_{group}", bufs, after)
        return token

    def wg(group, after):
        if group in swapping:
            bufs = _swap_wait(f"swap_wait_{group}", *swapping[group], after)
        else:
            bufs = _swap_halves(f"swap_{group}",
                                _gather_wait(f"gather_wait_{group}", flying[group], *gsems[group], after))
        if group == "rec_in":
            return dict(rec_w_in=bufs[0].reshape(4, D, RH))
        if group == "rec_out":
            return dict(rec_w_out=bufs[0].reshape(R, D))
        if group == "conf":
            return dict(conf_w_pw1=bufs[0].reshape(4, D, D // 2), conf_w_pw2=bufs[1].reshape(D, D))
        return dict(w_in=bufs[0].reshape(4, D, D), w_out=bufs[1].reshape(FF, D))

    (rows,), (landed,) = _send_wait("mod_wait", *mod_state, _to_chips, all_started)
    own = lax.dynamic_index_in_dim(rows, chip, axis=0, keepdims=True)
    by_flip = jnp.concatenate([own, landed[1:2], landed[0:1], landed[2:3]], axis=0)
    by_chip = jnp.take(by_flip, jnp.arange(4) ^ chip, axis=0)
    mods = by_chip[:, :2].transpose(1, 0, 2).reshape(2, 6, D)
    cmods = by_chip[:, 2].reshape(6, D)[:2]

    rec = dict(conv_w=full["rec_conv_w"][0], conv_b=rec_conv_b[0], lam=full["rec_lambda"][0],
               w_a=rec_w_a[0], b_a=rec_b_a[0], w_x=rec_w_x[0], b_x=rec_b_x[0])
    conf = dict(b_pw1=full["conf_b_pw1"][0], conv_w=full["conf_conv_w"][0], conv_b=full["conf_conv_b"][0],
                ln_g=full["conf_ln_g"][0], ln_b=full["conf_ln_b"][0], b_pw2=full["conf_b_pw2"][0])
    pairing, sent = {}, {}

    def on_grads(group, dws):
        parts = [dw.reshape((4,) + shards[t][1].shape[1:]) for dw, t in zip(dws, use_order[group])]
        pairing[group], token = _reduce_begin(group, parts, place)
        return token

    def finish_pair(after):
        (group, state), = pairing.items()
        pairing.clear()
        sent[group], sent["token"] = _reduce_mid(group, state, place, after)
        return sent["token"]

    loss_local, grad_x, _, small = _local_step(x[0], ctx[0], loss_target[0], mods, cmods, full["norm_g"], final_g,
                                               rec, conf, wg, on_grads, wg_pre, finish_pair)
    rec_sent = sent["token"]
    small["loss"] = loss_local.reshape(1)

    small_names = ["loss", "d_mod", "d_cmod", "norm_g", "rec_conv_w", "rec_conv_b", "rec_lambda", "rec_w_a", "rec_b_a",
                   "rec_w_x", "rec_b_x", "conf_b_pw1", "conf_conv_w", "conf_conv_b", "conf_ln_g", "conf_ln_b",
                   "conf_b_pw2", "final_g"]
    mine = lax.broadcasted_iota(jnp.int32, (8, 1), 0) == me
    mod_slots = jnp.where(mine, small["d_mod"].reshape(1, -1), 0.0)
    spacked, soffs = _pack([small[k] for k in small_names] + [mod_slots])
    small_state, small_started = _allreduce_small_begin(spacked, place, rec_sent)

    fulls = {}
    for group in ("mlp1", "conf", "mlp0", "rec"):
        for t, f in zip(use_order[group], _reduce_end(group, sent[group], place, small_started)):
            fulls[t] = f
    whole = _share_halves("share_grads", [fulls[t] for t in range(8)])
    g_big = dict(rec_w_in=whole[0].reshape(rec_w_in.shape), rec_w_out=whole[1].reshape(rec_w_out.shape),
                 conf_w_pw1=whole[2].reshape(conf_w_pw1.shape), conf_w_pw2=whole[3].reshape(conf_w_pw2.shape),
                 mlp_w_in=jnp.stack([whole[4].reshape(D, D), whole[5].reshape(D, D)]),
                 mlp_w_out=jnp.stack([whole[6].reshape(D, D), whole[7].reshape(D, D)]))
    delta, new_m, new_v = {}, {}, {}

    def adamw_of(k, g):
        cols = w[k].shape[-1]
        d_, m_, v_ = _adamw(f"adamw_{k}", w[k].reshape(-1, cols), g.reshape(-1, cols),
                            m[k].reshape(-1, cols), v[k].reshape(-1, cols))
        delta[k], new_m[k], new_v[k] = (a.reshape(w[k].shape) for a in (d_, m_, v_))

    for k in _BIG:
        adamw_of(k, g_big[k])

    unpacked = _unpack(_allreduce_small_end(small_state, new_v[_BIG[-1]]), soffs)
    ssum = dict(zip(small_names, unpacked[:-1]))
    loss = ssum["loss"].reshape(())
    dmod_rows = unpacked[-1].reshape(8, 2, 6 * D).transpose(1, 0, 2)

    d_cmod_full =jnp.concatenate([ssum["d_cmod"].reshape(1, 2 * D), jnp.zeros((1, 4 * D), F32)], axis=1)
    dm16 = jnp.concatenate([dmod_rows, jnp.stack([d_cmod_full, jnp.zeros((1, 6 * D), F32)]),
                            jnp.zeros((2, 7, 6 * D), F32)], axis=1)
    dm16_shard = lax.dynamic_slice_in_dim(dm16, chip * ns, ns, axis=2)
    g_w_ada, ds_part = _ada_bwd(c16, dm16_shard, w_ada)
    ds8 = _allgather8("gather_dsilu", ds_part)
    g_c_ctx = _cctx_grad(ds8, c_ctx).reshape(D)
    g_b_ada = ssum["d_mod"] + jnp.stack([d_cmod_full[0], jnp.zeros((6 * D,), F32)])

    def shard_of(a, axis):
        n = a.shape[axis] // 4
        return lax.dynamic_slice_in_dim(a, chip * n, n, axis=axis)

    grads = dict(
        c_ctx=g_c_ctx, w_ada=g_w_ada, b_ada=g_b_ada,
        norm_g=shard_of(ssum["norm_g"].reshape(2, 2, D), 2),
        rec_w_in=g_big["rec_w_in"], rec_conv_w=shard_of(ssum["rec_conv_w"].reshape(1, REC_KW, R), 2),
        rec_conv_b=ssum["rec_conv_b"].reshape(1, R), rec_lambda=shard_of(ssum["rec_lambda"].reshape(1, 2, R), 2),
        rec_w_a=ssum["rec_w_a"].reshape(rec_w_a.shape), rec_b_a=ssum["rec_b_a"].reshape(rec_b_a.shape),
        rec_w_x=ssum["rec_w_x"].reshape(rec_w_x.shape), rec_b_x=ssum["rec_b_x"].reshape(rec_b_x.shape),
        rec_w_out=g_big["rec_w_out"], conf_w_pw1=g_big["conf_w_pw1"],
        conf_b_pw1=shard_of(ssum["conf_b_pw1"].reshape(1, 2 * D), 1),
        conf_conv_w=shard_of(ssum["conf_conv_w"].reshape(1, CONF_KW, D), 2),
        conf_conv_b=shard_of(ssum["conf_conv_b"].reshape(1, D), 1),
        conf_ln_g=shard_of(ssum["conf_ln_g"].reshape(1, D), 1), conf_ln_b=shard_of(ssum["conf_ln_b"].reshape(1, D), 1),
        conf_w_pw2=g_big["conf_w_pw2"], conf_b_pw2=shard_of(ssum["conf_b_pw2"].reshape(1, D), 1),
        mlp_w_in=g_big["mlp_w_in"], mlp_w_out=g_big["mlp_w_out"], final_g=ssum["final_g"].reshape(D))

    adamw_of("w_ada", g_w_ada)
    rest = [k for k in names if k not in ("w_ada",) + _BIG]
    d_, m_, v_ = _adamw_many("adamw_small", [w[k] for k in rest], [grads[k] for k in rest],
                             [m[k] for k in rest], [v[k] for k in rest])
    for k, dd, mm, vv in zip(rest, d_, m_, v_):
        delta[k], new_m[k], new_v[k] = dd, mm, vv

    return (loss, grad_x[None], *[grads[k] for k in names], *[delta[k] for k in names],
            *[new_m[k] for k in names], *[new_v[k] for k in names])
```

```python
import functools
import math

import jax
import jax.numpy as jnp
from jax import lax
from jax.experimental import pallas as pl
from jax.experimental.pallas import tpu as pltpu

F32 = jnp.float32
BF16 = jnp.bfloat16

D = 1024
T = 2048
TC = 256
TA = T + TC
R = 1280
RH = R // 2
NQ = 4 * RH
FF = 4096
N_BLK = 16
BLK = R // N_BLK
GRID_W = 64
EPS = 1e-6
RG_C = 8.0
CONF_KW = 31
REC_KW = 4
LANE = 128
ROW_TILE = 256
HALO = 16
RG_TILE = 128
PACK_ROWS = 512
MM_TILE = 1024
REC_TILE = TA // 2
CW_REC = 640
CW_CONF = 512
V7X_VMEM_BYTES = 64 * 1024 * 1024
VMEM_LIMIT = V7X_VMEM_BYTES - 8 * 1024 * 1024

ADAM_LR = 0.001
ADAM_B1 = 0.9
ADAM_B2 = 0.999
ADAM_EPS = 1e-08
ADAM_WD = 0.01
ADAM_STEP = 10

MESH = pl.DeviceIdType.MESH
ANY = pl.BlockSpec(memory_space=pl.ANY)


def _sds(shape, dtype):
    return jax.ShapeDtypeStruct(tuple(shape), dtype)


def _pcall(body, **kw):
    return pl.pallas_call(body, **kw)


def _cparams():
    return pltpu.CompilerParams(vmem_limit_bytes=VMEM_LIMIT)


def _full_spec(arr):
    nd = arr.ndim
    return pl.BlockSpec(arr.shape, lambda *ids, _n=nd: (0,) * _n)


def _sum0(v):
    return jnp.sum(v, axis=0, keepdims=True)


def _tiled(name, fn, grid, ins, vecs, outs, vec_outs=(), vec_refs=False):
    n_in, n_vec, n_out = len(ins), len(vecs), len(outs)
    n_grid = len(grid)

    def kern(*refs):
        ids = [pl.program_id(a) for a in range(n_grid)]
        tin = [r[...] for r in refs[:n_in]]
        vin = list(refs[n_in:n_in + n_vec]) if vec_refs else [r[...] for r in refs[n_in:n_in + n_vec]]
        o_refs = refs[n_in + n_vec:n_in + n_vec + n_out]
        a_refs = refs[n_in + n_vec + n_out:]
        tout, incs = fn(ids, tin, vin)
        for r, v in zip(o_refs, tout):
            r[...] = v.astype(r.dtype)
        if a_refs:
            first = functools.reduce(jnp.logical_and, [i == 0 for i in ids])

            @pl.when(first)
            def _():
                for r in a_refs:
                    r[...] = jnp.zeros_like(r)

            for r, v in zip(a_refs, incs):
                r[...] += v

    out_shape = [o for o, _ in outs] + [_sds(s, F32) for s in vec_outs]
    out_specs = [s for _, s in outs] + [
        pl.BlockSpec(tuple(s), lambda *ids, _n=len(s): (0,) * _n) for s in vec_outs]
    res = _pcall(
        kern, name=name, grid=tuple(grid),
        in_specs=[s for _, s in ins] + [_full_spec(v) for v in vecs],
        out_specs=out_specs, out_shape=out_shape, compiler_params=_cparams(),
    )(*[a for a, _ in ins], *vecs)
    return list(res)


def _rows(arr, ncols=None, tm=ROW_TILE, off=0, col=0, clamp_lo=False):
    ncols = arr.shape[1] if ncols is None else ncols
    if clamp_lo:
        return arr, pl.BlockSpec((tm, ncols), lambda i: (jnp.maximum(i + off, 0), col))
    return arr, pl.BlockSpec((tm, ncols), lambda i: (i + off, col))


def _orow(nrows, ncols, dtype, tm=ROW_TILE, off=0, clamp_lo=False):
    if clamp_lo:
        return _sds((nrows, ncols), dtype), pl.BlockSpec((tm, ncols), lambda i: (jnp.maximum(i + off, 0), 0))
    return _sds((nrows, ncols), dtype), pl.BlockSpec((tm, ncols), lambda i: (i + off, 0))


_NN = (((1,), (0,)), ((), ()))
_TN = (((0,), (0,)), ((), ()))
_NT = (((1,), (1,)), ((), ()))


def _mm(name, a, b, dims, grid, a_spec, b_spec, out, acc_shape, extra=(), a_pre=None, epi=None):
    n_k = grid[2]
    n_ex = len(extra)

    def kern(a_ref, b_ref, *rest):
        ex = rest[:n_ex]
        o_refs = rest[n_ex:n_ex + len(out)]
        k = pl.program_id(2)
        av = a_ref[...]
        if a_pre is not None:
            av = a_pre(av)
        part = lax.dot_general(av.astype(BF16), b_ref[...].astype(BF16), dims, preferred_element_type=F32)

        def finish(total):
            vals = [total] if epi is None else epi(total, [e[...] for e in ex])
            for r, v in zip(o_refs, vals):
                r[...] = v.astype(r.dtype)

        if n_k == 1:
            finish(part)
        else:
            acc = rest[-1]

            @pl.when(k == 0)
            def _():
                acc[...] = part

            @pl.when(jnp.logical_and(k > 0, k < n_k - 1))
            def _():
                acc[...] += part

            @pl.when(k == n_k - 1)
            def _():
                finish(acc[...] + part)

    res = _pcall(
        kern, name=name, grid=tuple(grid),
        in_specs=[a_spec, b_spec] + [s for _, s in extra],
        out_specs=[s for _, s in out], out_shape=[o for o, _ in out],
        scratch_shapes=[] if n_k == 1 else [pltpu.VMEM(tuple(acc_shape), F32)], compiler_params=_cparams(),
    )(a, b, *[e for e, _ in extra])
    return list(res)


def _rms(x):
    r = lax.rsqrt(jnp.mean(x * x, axis=-1, keepdims=True) + EPS)
    return x * r, r


def _norm_mod(x, g, sc, sh):
    n, _ = _rms(x)
    return (n * g) * (1.0 + sc) + sh


def _norm_mod_bwd(dh, x, g, sc):
    n, r = _rms(x)
    d_sh = _sum0(dh)
    d_sc = _sum0(dh * (n * g))
    d_g = _sum0(dh * (1.0 + sc) * n)
    dn = dh * (g * (1.0 + sc))
    dx = r * (dn - n * jnp.mean(dn * n, axis=-1, keepdims=True))
    return dx, d_sh, d_sc, d_g


_GELU_K = math.sqrt(2.0 / math.pi)


def _gelu(x):
    t = jnp.tanh(_GELU_K * (x + 0.044715 * x * x * x))
    return 0.5 * x * (1.0 + t), t


def _gelu_grad(x, t):
    return 0.5 * (1.0 + t) + 0.5 * x * (1.0 - t * t) * (_GELU_K * (1.0 + 3.0 * 0.044715 * x * x))


def _sigmoid(x):
    return 0.5 * jnp.tanh(0.5 * x) + 0.5


def _expm1(x):
    p = jnp.full_like(x, 1.0 / 5040.0)
    for c in (1.0 / 720.0, 1.0 / 120.0, 1.0 / 24.0, 1.0 / 6.0, 0.5, 1.0):
        p = p * x + c
    return jnp.where(jnp.abs(x) < 0.3, x * p, jnp.exp(x) - 1.0)


def _softplus_neg(lam):
    return jnp.log1p(jnp.exp(-jnp.abs(lam))) + jnp.maximum(-lam, 0.0)


def _layernorm_parts(x):
    mu = jnp.mean(x, axis=-1, keepdims=True)
    xc = x - mu
    rstd = lax.rsqrt(jnp.mean(xc * xc, axis=-1, keepdims=True) + EPS)
    return xc * rstd, rstd


def _rg_gates(u, wbd, gbias, lam):
    sp = _softplus_neg(lam)
    parts = {}
    for h in range(2):
        uh = u[:, h * RH:(h + 1) * RH]
        g = jnp.dot(uh.astype(BF16), wbd[h], preferred_element_type=F32) + gbias[:, h * NQ:(h + 1) * NQ]
        for d in range(2):
            r = _sigmoid(g[:, (2 * d) * RH:(2 * d + 1) * RH])
            i = _sigmoid(g[:, (2 * d + 1) * RH:(2 * d + 2) * RH])
            sph = sp[d:d + 1, h * RH:(h + 1) * RH]
            la = (-RG_C) * r * sph
            e2 = _expm1(2.0 * la)
            inv_mult = jnp.where(e2 < 0.0, lax.rsqrt(-e2), 0.0)
            parts[(d, h)] = dict(r=r, i=i, la=la, a=jnp.exp(la), e2=e2, mult=-e2 * inv_mult, inv_mult=inv_mult,
                                 uh=uh, sp=sph)
    return parts


def _rg_fwd_fn(ids, tin, vin):
    (u,) = tin
    wbd = vin[0]
    parts = _rg_gates(u, wbd, vin[1][...], vin[2][...])
    outs = []
    for d in range(2):
        a = jnp.concatenate([parts[(d, h)]["a"] for h in range(2)], axis=1)
        b = jnp.concatenate([parts[(d, h)]["mult"] * parts[(d, h)]["i"] * parts[(d, h)]["uh"]
                             for h in range(2)], axis=1)
        outs += [a, b]
    return outs, []


def _rg_bwd_fn(ids, tin, vin):
    u, da_f, db_f, da_r, db_r = tin
    wbd, lam = vin[0], vin[2][...]
    parts = _rg_gates(u, wbd, vin[1][...], lam)
    dab = ((da_f, db_f), (da_r, db_r))
    dsig_lam = -1.0 / (1.0 + jnp.exp(lam))
    du_halves, dpre_halves, dlam = [], [], [[None, None], [None, None]]
    for h in range(2):
        du = jnp.zeros_like(parts[(0, h)]["uh"])
        dpre = []
        for d in range(2):
            p = parts[(d, h)]
            da = dab[d][0][:, h * RH:(h + 1) * RH]
            db = dab[d][1][:, h * RH:(h + 1) * RH]
            d_mult = db * p["i"] * p["uh"]
            d_i = db * p["mult"] * p["uh"]
            du = du + db * p["mult"] * p["i"]
            d_la = da * p["a"] - d_mult * (p["e2"] + 1.0) * p["inv_mult"]
            d_r = d_la * ((-RG_C) * p["sp"])
            dlam[d][h] = _sum0(d_la * ((-RG_C) * p["r"])) * dsig_lam[d:d + 1, h * RH:(h + 1) * RH]
            dpre += [d_r * p["r"] * (1.0 - p["r"]), d_i * p["i"] * (1.0 - p["i"])]
        dpre = jnp.concatenate(dpre, axis=1)
        du = du + lax.dot_general(dpre.astype(BF16), wbd[h], _NT, preferred_element_type=F32)
        du_halves.append(du)
        dpre_halves.append(dpre)
    dpre_all = jnp.concatenate(dpre_halves, axis=1)
    dlam_row = jnp.concatenate([dlam[0][0], dlam[0][1], dlam[1][0], dlam[1][1]], axis=1)
    return [dpre_all, jnp.concatenate(du_halves, axis=1)], [_sum0(dpre_all), dlam_row]


def _tile_flags(i, n_tiles, seq_starts):
    starts_here = functools.reduce(jnp.logical_or, [i == s for s in seq_starts])
    ends_here = functools.reduce(jnp.logical_or, [i + 1 == s for s in seq_starts] + [i + 1 == n_tiles])
    return jnp.logical_not(starts_here), jnp.logical_not(ends_here)


def _halo_specs(col0, cw):
    hb = ROW_TILE // HALO
    prev = pl.BlockSpec((HALO, cw), lambda i, c: (jnp.maximum(i * hb - 1, 0), col0 + c))
    cur = pl.BlockSpec((ROW_TILE, cw), lambda i, c: (i, col0 + c))
    return prev, cur, hb


def _window(prev_ref, cur_ref, next_ref, has_prev, has_next):
    prev = jnp.where(has_prev, prev_ref[...], 0.0)
    nxt = jnp.where(has_next, next_ref[...], 0.0)
    return jnp.concatenate([prev, cur_ref[...], nxt], axis=0)


def _tap_reader(win):
    sub = 8
    n = win.shape[0]
    shifted = {0: win}

    def tap(off):
        s = off % sub
        if s not in shifted:
            shifted[s] = pltpu.roll(win, n - s, axis=0)
        return shifted[s][off - s:off - s + ROW_TILE, :]

    return tap


def _dwconv(name, x, col0, w, bias, pad_left, seq_starts, n_ch, cw=256):
    n_rows = x.shape[0]
    n_tiles = n_rows // ROW_TILE
    n_taps = w.shape[0]
    prev_spec, cur_spec, hb = _halo_specs(col0, cw)
    last_hb = n_rows // HALO - 1
    next_spec = pl.BlockSpec((HALO, cw), lambda i, c: (jnp.minimum((i + 1) * hb, last_hb), col0 + c))

    def kern(prev_ref, cur_ref, next_ref, w_ref, b_ref, o_ref):
        has_prev, has_next = _tile_flags(pl.program_id(0), n_tiles, seq_starts)
        win = _window(prev_ref, cur_ref, next_ref, has_prev, has_next)
        tap = _tap_reader(win)
        wv = w_ref[...]
        acc = jnp.zeros((ROW_TILE, cw), F32) + b_ref[...]
        for k in range(n_taps):
            acc = acc + wv[k:k + 1, :] * tap(HALO + k - pad_left)
        o_ref[...] = acc

    return _pcall(
        kern, name=name, grid=(n_tiles, n_ch // cw),
        in_specs=[prev_spec, cur_spec, next_spec,
                  pl.BlockSpec((n_taps, cw), lambda i, c: (0, c)), pl.BlockSpec((1, cw), lambda i, c: (0, c))],
        out_specs=pl.BlockSpec((ROW_TILE, cw), lambda i, c: (i, c)),
        out_shape=_sds((n_rows, n_ch), F32), compiler_params=_cparams(),
    )(x, x, x, w, bias)


def _dwconv_wgrad(name, dy, x, col0, n_taps, pad_left, seq_starts, n_ch, cw=256, dep=None):
    deps = [] if dep is None else [dep]
    n_rows = dy.shape[0]
    n_tiles = n_rows // ROW_TILE
    n_out = -(-(n_taps + 1) // 8) * 8
    prev_spec, cur_spec, hb = _halo_specs(col0, cw)
    last_hb = n_rows // HALO - 1
    next_spec = pl.BlockSpec((HALO, cw), lambda c, i: (jnp.minimum((i + 1) * hb, last_hb), col0 + c))
    prev_spec = pl.BlockSpec((HALO, cw), lambda c, i: (jnp.maximum(i * hb - 1, 0), col0 + c))
    cur_spec = pl.BlockSpec((ROW_TILE, cw), lambda c, i: (i, col0 + c))

    def kern(dy_ref, prev_ref, cur_ref, next_ref, *rest):
        o_ref = rest[-1]
        i = pl.program_id(1)
        has_prev, has_next = _tile_flags(i, n_tiles, seq_starts)
        win = _window(prev_ref, cur_ref, next_ref, has_prev, has_next)
        dyv = dy_ref[...]
        tap = _tap_reader(win)
        rid = lax.broadcasted_iota(jnp.int32, (n_out, cw), 0)
        inc = jnp.where(rid == n_taps, _sum0(dyv), 0.0)
        for k in range(n_taps):
            inc = inc + jnp.where(rid == k, _sum0(dyv * tap(HALO + k - pad_left)), 0.0)

        @pl.when(i == 0)
        def _():
            o_ref[...] = jnp.zeros_like(o_ref)

        o_ref[...] += inc

    return _pcall(
        kern, name=name, grid=(n_ch // cw, n_tiles),
        in_specs=[pl.BlockSpec((ROW_TILE, cw), lambda c, i: (i, c)), prev_spec, cur_spec, next_spec]
        + [pl.BlockSpec(d.shape, lambda c, i: (0, 0)) for d in deps],
        out_specs=pl.BlockSpec((n_out, cw), lambda c, i: (0, c)),
        out_shape=_sds((n_out, n_ch), F32), compiler_params=_cparams(),
    )(dy, x, x, x, *deps)


N_SCAN = TA // ROW_TILE


def _rev_block(j):
    return jnp.where(j == 0, 0, N_SCAN - j)


def _scan_fwd(a_f, b_f, a_r, b_r):
    fwd_spec = pl.BlockSpec((ROW_TILE, R), lambda i: (i, 0))
    rev_spec = pl.BlockSpec((ROW_TILE, R), lambda i: (_rev_block(i), 0))
    hin_spec = pl.BlockSpec((None, 1, R), lambda i: (i, 0, 0))

    def kern(af, bf, ar, br, yf, yr, hin_f, hin_r, hf_s, hr_s):
        @pl.when(pl.program_id(0) == 0)
        def _():
            hf_s[...] = jnp.zeros_like(hf_s)
            hr_s[...] = jnp.zeros_like(hr_s)

        hin_f[...] = hf_s[...]
        hin_r[...] = hr_s[...]

        def step(s8, carry):
            hf, hr = carry
            t0 = pl.multiple_of(s8 * 8, 8)
            for q in range(8):
                tf = t0 + q
                hf = af[pl.ds(tf, 1), :] * hf + bf[pl.ds(tf, 1), :]
                yf[pl.ds(tf, 1), :] = hf
                tr = ROW_TILE - 1 - tf
                hr = ar[pl.ds(tr, 1), :] * hr + br[pl.ds(tr, 1), :]
                yr[pl.ds(tr, 1), :] = hr
            return hf, hr

        hf, hr = lax.fori_loop(0, ROW_TILE // 8, step, (hf_s[...], hr_s[...]))
        hf_s[...] = hf
        hr_s[...] = hr

    return _pcall(
        kern, name="scan_fwd", grid=(N_SCAN,),
        in_specs=[fwd_spec, fwd_spec, rev_spec, rev_spec],
        out_specs=[fwd_spec, rev_spec, hin_spec, hin_spec],
        out_shape=[_sds((TA, R), F32), _sds((TA, R), F32), _sds((N_SCAN, 1, R), F32), _sds((N_SCAN, 1, R), F32)],
        scratch_shapes=[pltpu.VMEM((1, R), F32), pltpu.VMEM((1, R), F32)], compiler_params=_cparams(),
    )(a_f, b_f, a_r, b_r)


def _scan_bwd(dy, a_f, y_f, hin_f, a_r, y_r, hin_r):
    fwd_spec = pl.BlockSpec((ROW_TILE, R), lambda i: (N_SCAN - 1 - i, 0))
    rev_spec = pl.BlockSpec((ROW_TILE, R), lambda i: (_rev_block(N_SCAN - 1 - i), 0))
    hin_spec = pl.BlockSpec((None, 1, R), lambda i: (N_SCAN - 1 - i, 0, 0))
    last = ROW_TILE - 1

    def kern(dyf, af, yf, hf0, dyr, ar, yr, hr0, daf, dbf, dar, dbr, gf_s, anf_s, gr_s, anr_s):
        @pl.when(pl.program_id(0) == 0)
        def _():
            for r in (gf_s, anf_s, gr_s, anr_s):
                r[...] = jnp.zeros_like(r)

        def one(dy_ref, a_ref, y_ref, da_ref, db_ref, g, an, p, pprev):
            gnew = dy_ref[pl.ds(p, 1), :] + an * g
            db_ref[pl.ds(p, 1), :] = gnew
            da_ref[pl.ds(p, 1), :] = gnew * y_ref[pl.ds(pprev, 1), :]
            return gnew, a_ref[pl.ds(p, 1), :]

        def step(s8, carry):
            gf, anf, gr, anr = carry
            base = s8 * 8
            for q in range(8):
                s = last - (base + q)
                gf, anf = one(dyf, af, yf, daf, dbf, gf, anf, s, s - 1)
                gr, anr = one(dyr, ar, yr, dar, dbr, gr, anr, last - s, last - s + 1)
            return gf, anf, gr, anr

        carry = (gf_s[...], anf_s[...], gr_s[...], anr_s[...])
        carry = lax.fori_loop(0, ROW_TILE // 8 - 1, step, carry)
        gf, anf, gr, anr = carry
        for s in range(7, 0, -1):
            gf, anf = one(dyf, af, yf, daf, dbf, gf, anf, s, s - 1)
            gr, anr = one(dyr, ar, yr, dar, dbr, gr, anr, last - s, last - s + 1)
        gf0 = dyf[0:1, :] + anf * gf
        dbf[0:1, :] = gf0
        daf[0:1, :] = gf0 * hf0[...]
        gr0 = dyr[last:last + 1, :] + anr * gr
        dbr[last:last + 1, :] = gr0
        dar[last:last + 1, :] = gr0 * hr0[...]
        gf_s[...] = gf0
        anf_s[...] = af[0:1, :]
        gr_s[...] = gr0
        anr_s[...] = ar[last:last + 1, :]

    return _pcall(
        kern, name="scan_bwd", grid=(N_SCAN,),
        in_specs=[fwd_spec, fwd_spec, fwd_spec, hin_spec, rev_spec, rev_spec, rev_spec, hin_spec],
        out_specs=[fwd_spec, fwd_spec, rev_spec, rev_spec],
        out_shape=[_sds((TA, R), F32)] * 4,
        scratch_shapes=[pltpu.VMEM((1, R), F32)] * 4, compiler_params=_cparams(),
    )(dy, a_f, y_f, hin_f, dy, a_r, y_r, hin_r)


def _me():
    return lax.axis_index("x"), lax.axis_index("y"), lax.axis_index("c")


def _other_chips(mx, my):
    return [(1 - mx, my), (mx, 1 - my), (1 - mx, 1 - my)]


def _rcopy(src, dst, ssem, rsem, dev):
    return pltpu.make_async_remote_copy(src_ref=src, dst_ref=dst, send_sem=ssem, recv_sem=rsem,
                                        device_id=dev, device_id_type=MESH)


def _peers7(mx, my, mc):
    peers = []
    for k in range(1, 8):
        peers.append((1 - mx if (k >> 2) & 1 else mx, 1 - my if (k >> 1) & 1 else my, 1 - mc if k & 1 else mc))
    return peers


def _share_halves(name, fulls):
    n = len(fulls)

    def kern(*refs):
        o = refs[n:2 * n]
        ss, rs = refs[2 * n:]
        mx, my, mc = _me()
        sib = (mx, my, 1 - mc)
        sends = []
        for t in range(n):
            cp = _rcopy(o[t].at[mc], o[t].at[mc], ss.at[t], rs.at[t], sib)
            cp.start()
            sends.append(cp)
        for t in range(n):
            _rcopy(o[t].at[1 - mc], o[t].at[1 - mc], ss.at[t], rs.at[t], sib).wait_recv()
        for cp in sends:
            cp.wait_send()

    dma = pltpu.SemaphoreType.DMA
    return _pcall(
        kern, name=name, in_specs=[ANY] * n, out_specs=[ANY] * n,
        out_shape=[_sds(f.shape, f.dtype) for f in fulls], input_output_aliases={t: t for t in range(n)},
        scratch_shapes=[dma((n,)), dma((n,))],
    )(*fulls)


def _tiled_sp(name, fn, grid, sp, ins, outs):
    n_in = len(ins)

    def kern(sp_ref, *refs):
        tout = fn([r[...] for r in refs[:n_in]])
        for r, v in zip(refs[n_in:], tout):
            r[...] = v.astype(r.dtype)

    gs = pltpu.PrefetchScalarGridSpec(num_scalar_prefetch=1, grid=tuple(grid),
                                      in_specs=[s for _, s in ins], out_specs=[s for _, s in outs])
    res = _pcall(kern, name=name, grid_spec=gs, out_shape=[o for o, _ in outs], compiler_params=_cparams(),
                 )(sp, *[a for a, _ in ins])
    return list(res)


def _row_tile(rows, cols, itemsize=4, budget=2 * 1024 * 1024):
    tr = rows
    while tr * cols * itemsize > budget and tr % 32 == 0:
        tr //= 2
    return tr


def _place_big(shards, place, dep=None):
    slots = []
    for tag, s, layer in shards:
        rr, cc = s.shape[2], s.shape[3]
        tr = _row_tile(rr, cc)
        (slot,) = _tiled_sp(
            f"place_{tag}", lambda tin: [tin[0]], (2, rr // tr), place,
            [(s, pl.BlockSpec((None, None, tr, cc), lambda h, i, sp, layer=layer: (layer, h, i, 0)))]
            + [(d, pl.BlockSpec(d.shape, lambda h, i, sp: (0, 0))) for d in _behind(dep)],
            [(_sds((4, 2, rr, cc), BF16), pl.BlockSpec((None, None, tr, cc), lambda h, i, sp: (sp[0], h, i, 0)))])
        slots.append(slot)
    return slots


def _allreduce_small_begin(vec, place, after):
    hr = vec.shape[0] // 2
    tr = _row_tile(hr, LANE)
    blk = (None, None, tr, LANE)
    (pair,) = _tiled_sp(
        "small_place", lambda tin: [tin[0]], (2, hr // tr), place,
        [(vec.reshape(2, hr, LANE), pl.BlockSpec((None, tr, LANE), lambda h, i, sp: (h, i, 0)))],
        [(_sds((2, 2, hr, LANE), F32), pl.BlockSpec(blk, lambda h, i, sp: (sp[1], h, i, 0)))])
    (pair,) = _share_halves("small_share", [pair])
    (slot,) = _tiled_sp(
        "small_pair_add", lambda tin: [tin[0] + tin[1]], (2, hr // tr), place,
        [(pair, pl.BlockSpec(blk, lambda h, i, sp: (0, h, i, 0))),
         (pair, pl.BlockSpec(blk, lambda h, i, sp: (1, h, i, 0)))],
        [(_sds((4, 2, hr, LANE), F32), pl.BlockSpec(blk, lambda h, i, sp: (sp[0], h, i, 0)))])
    fly, sems, token = _gather_start("small_start", [slot], ((0,),), after)
    return (fly, sems), token


def _allreduce_small_end(state, after):
    fly, sems = state
    (chips,) = _swap_halves("small_swap", _gather_wait("small_wait", fly, *sems, after))
    hr = chips.shape[2]
    tr = _row_tile(hr, LANE)
    blk = (None, None, tr, LANE)
    (total,) = _tiled(
        "small_chip_sum", lambda ids, tin, vin: ([((tin[0] + tin[1]) + tin[2]) + tin[3]], []), (2, hr // tr),
        [(chips, pl.BlockSpec(blk, lambda h, i, _j=j: (_j, h, i, 0))) for j in range(4)], [],
        [(_sds((2, hr, LANE), F32), pl.BlockSpec((None, tr, LANE), lambda h, i: (h, i, 0)))])
    return total.reshape(2 * hr, LANE)


SEM =pl.BlockSpec(memory_space=pltpu.SEMAPHORE)
_DATAFLOW = pltpu.SideEffectType.DATAFLOW_SIDE_EFFECTING


def _gather_start(name, slots, groups, after):
    n = len(slots)

    def kern(*refs):
        o = refs[n + 1:2 * n + 1]
        sems, token = refs[2 * n + 1:-1], refs[-1]
        mx, my, mc = _me()
        j0 = 2 * mx + my
        for gi, grp in enumerate(groups):
            for k, t in enumerate(grp):
                for q, (qx, qy) in enumerate(_other_chips(mx, my)):
                    _rcopy(o[t].at[j0, mc], o[t].at[j0, mc], sems[2 * gi].at[3 * k + q],
                           sems[2 * gi + 1].at[3 * k + q], (qx, qy, mc)).start()
        token[...] = jnp.zeros_like(token)

    sem_shapes = []
    for grp in groups:
        sem_shapes += [pltpu.SemaphoreType.DMA((3 * len(grp),))] * 2
    res = _pcall(
        kern, name=name, in_specs=[ANY] * (n + 1),
        out_specs=[ANY] * n + [SEM] * len(sem_shapes) + [pl.BlockSpec(memory_space=pltpu.VMEM)],
        out_shape=[_sds(w.shape, w.dtype) for w in slots] + sem_shapes + [_sds((8, LANE), F32)],
        input_output_aliases={t: t for t in range(n)},
        compiler_params=pltpu.CompilerParams(has_side_effects=_DATAFLOW),
    )(*slots, after)
    return list(res[:n]), list(res[n:-1]), res[-1]


def _gather_wait(name, bufs, ssem, rsem, after):
    n = len(bufs)

    def kern(*refs):
        b = refs[:n]
        ssem_ref, rsem_ref = refs[n], refs[n + 1]
        mx, my, mc = _me()
        j0 = 2 * mx + my
        for k in range(n):
            for q, (qx, qy) in enumerate(_other_chips(mx, my)):
                jq = 2 * qx + qy
                _rcopy(b[k].at[jq, mc], b[k].at[jq, mc], ssem_ref.at[3 * k + q], rsem_ref.at[3 * k + q],
                       (qx, qy, mc)).wait_recv()
                _rcopy(b[k].at[j0, mc], b[k].at[j0, mc], ssem_ref.at[3 * k + q], rsem_ref.at[3 * k + q],
                       (qx, qy, mc)).wait_send()

    return list(_pcall(
        kern, name=name, in_specs=[ANY] * n + [SEM, SEM, ANY], out_specs=[ANY] * n,
        out_shape=[_sds(w.shape, w.dtype) for w in bufs], input_output_aliases={k: k for k in range(n)},
        compiler_params=pltpu.CompilerParams(has_side_effects=_DATAFLOW),
    )(*bufs, ssem, rsem, after))


def _swap_halves(name, bufs):
    n = len(bufs)

    def kern(*refs):
        o = refs[n:2 * n]
        ss, rs = refs[2 * n:]
        mx, my, mc = _me()
        sib = (mx, my, 1 - mc)
        sends = []
        for k in range(n):
            for q, (qx, qy) in enumerate(_other_chips(mx, my)):
                jq = 2 * qx + qy
                cp = _rcopy(o[k].at[jq, mc], o[k].at[jq, mc], ss.at[3 * k + q], rs.at[3 * k + q], sib)
                cp.start()
                sends.append(cp)
        for k in range(n):
            for q, (qx, qy) in enumerate(_other_chips(mx, my)):
                jq = 2 * qx + qy
                _rcopy(o[k].at[jq, 1 - mc], o[k].at[jq, 1 - mc], ss.at[3 * k + q], rs.at[3 * k + q], sib).wait_recv()
        for cp in sends:
            cp.wait_send()

    dma = pltpu.SemaphoreType.DMA
    return list(_pcall(
        kern, name=name, in_specs=[ANY] * n, out_specs=[ANY] * n,
        out_shape=[_sds(w.shape, w.dtype) for w in bufs], input_output_aliases={k: k for k in range(n)},
        scratch_shapes=[dma((3 * n,)), dma((3 * n,))],
    )(*bufs))


def _swap_start(name, bufs, after):
    n = len(bufs)

    def kern(*refs):
        o = refs[n + 1:2 * n + 1]
        ssem, rsem, token = refs[2 * n + 1:]
        mx, my, mc = _me()
        for k in range(n):
            for q, (qx, qy) in enumerate(_other_chips(mx, my)):
                jq = 2 * qx + qy
                _rcopy(o[k].at[jq, mc], o[k].at[jq, mc], ssem.at[3 * k + q], rsem.at[3 * k + q], (mx, my, 1 - mc)).start()
        token[...] = jnp.zeros_like(token)

    dma = pltpu.SemaphoreType.DMA
    res = _pcall(
        kern, name=name, in_specs=[ANY] * (n + 1),
        out_specs=[ANY] * n + [SEM, SEM, pl.BlockSpec(memory_space=pltpu.VMEM)],
        out_shape=[_sds(w.shape, w.dtype) for w in bufs] + [dma((3 * n,)), dma((3 * n,)), _sds((8, LANE), F32)],
        input_output_aliases={k: k for k in range(n)},
        compiler_params=pltpu.CompilerParams(has_side_effects=_DATAFLOW),
    )(*bufs, after)
    return (list(res[:n]), res[n], res[n + 1]), res[n + 2]


def _swap_wait(name, bufs, ssem, rsem, after):
    n = len(bufs)

    def kern(*refs):
        b = refs[:n]
        ssem_ref, rsem_ref = refs[n], refs[n + 1]
        mx, my, mc = _me()
        sib = (mx, my, 1 - mc)
        for k in range(n):
            for q, (qx, qy) in enumerate(_other_chips(mx, my)):
                jq = 2 * qx + qy
                _rcopy(b[k].at[jq, 1 - mc], b[k].at[jq, 1 - mc], ssem_ref.at[3 * k + q], rsem_ref.at[3 * k + q],
                       sib).wait_recv()
                _rcopy(b[k].at[jq, mc], b[k].at[jq, mc], ssem_ref.at[3 * k + q], rsem_ref.at[3 * k + q],
                       sib).wait_send()

    return list(_pcall(
        kern, name=name, in_specs=[ANY] * n + [SEM, SEM, ANY], out_specs=[ANY] * n,
        out_shape=[_sds(w.shape, w.dtype) for w in bufs], input_output_aliases={k: k for k in range(n)},
        compiler_params=pltpu.CompilerParams(has_side_effects=_DATAFLOW),
    )(*bufs, ssem, rsem, after))


def _to_sibling(mx, my, mc):
    return [((j, 1 - mc), j, (mx, my, 1 - mc)) for j in range(4)]


def _to_chips(mx, my, mc):
    return [((2 * qx + qy,), q, (qx, qy, mc)) for q, (qx, qy) in enumerate(_other_chips(mx, my))]


def _to_all7(mx, my, mc):
    return [((0,), k, dev) for k, dev in enumerate(_peers7(mx, my, mc))]


def _send_start(name, srcs, plan, land_shapes, after):
    n = len(srcs)
    per = len(plan(0, 0, 0))

    def kern(*refs):
        s, land = refs[n + 1:2 * n + 1], refs[2 * n + 1:3 * n + 1]
        ssem, rsem, token = refs[3 * n + 1:]
        for k in range(n):
            for q, (idx, slot, dev) in enumerate(plan(*_me())):
                _rcopy(s[k].at[idx], land[k].at[slot], ssem.at[per * k + q], rsem.at[per * k + q], dev).start()
        token[...] = jnp.zeros_like(token)

    dma = pltpu.SemaphoreType.DMA
    res = _pcall(
        kern, name=name, in_specs=[ANY] * (n + 1),
        out_specs=[ANY] * (2 * n) + [SEM, SEM, pl.BlockSpec(memory_space=pltpu.VMEM)],
        out_shape=[_sds(s.shape, s.dtype) for s in srcs] + [_sds(ls, s.dtype) for ls, s in zip(land_shapes, srcs)]
        + [dma((per * n,)), dma((per * n,)), _sds((8, LANE), F32)],
        input_output_aliases={k: k for k in range(n)},
        compiler_params=pltpu.CompilerParams(has_side_effects=_DATAFLOW),
    )(*srcs, after)
    return (list(res[:n]), list(res[n:2 * n]), res[2 * n], res[2 * n + 1]), res[2 * n + 2]


def _send_wait(name, srcs, lands, ssem, rsem, plan, after):
    n = len(srcs)
    per = len(plan(0, 0, 0))

    def kern(*refs):
        s, land = refs[:n], refs[n:2 * n]
        ssem_ref, rsem_ref = refs[2 * n], refs[2 * n + 1]
        for k in range(n):
            for q, (idx, slot, dev) in enumerate(plan(*_me())):
                cp = _rcopy(s[k].at[idx], land[k].at[slot], ssem_ref.at[per * k + q], rsem_ref.at[per * k + q], dev)
                cp.wait_recv()
                cp.wait_send()

    res = _pcall(
        kern, name=name, in_specs=[ANY] * (2 * n) + [SEM, SEM, ANY], out_specs=[ANY] * (2 * n),
        out_shape=[_sds(a.shape, a.dtype) for a in list(srcs) + list(lands)],
        input_output_aliases={k: k for k in range(2 * n)},
        compiler_params=pltpu.CompilerParams(has_side_effects=_DATAFLOW),
    )(*srcs, *lands, ssem, rsem, after)
    return list(res[:n]), list(res[n:])


def _reduce_begin(tag, parts, after):
    return _send_start(f"pair_start_{tag}", parts, _to_sibling, [(4,) + p.shape[2:] for p in parts], after)


def _reduce_mid(tag, pairing, place, after):
    parts, theirs = _send_wait(f"pair_wait_{tag}", *pairing, _to_sibling, after)
    sums = []
    for k, (p, o) in enumerate(zip(parts, theirs)):
        rr, cc = p.shape[2], p.shape[3]
        tr = _row_tile(rr, cc)
        (s_k,) = _tiled_sp(
            f"pair_add_{tag}{k}", lambda tin: [tin[0].astype(F32) + tin[1].astype(F32)], (4, rr // tr), place,
            [(p, pl.BlockSpec((None, None, tr, cc), lambda j, i, sp: (j, sp[1], i, 0))),
             (o, pl.BlockSpec((None, tr, cc), lambda j, i, sp: (j, i, 0)))],
            [(_sds((4, rr, cc), BF16), pl.BlockSpec((None, tr, cc), lambda j, i, sp: (j, i, 0)))])
        sums.append(s_k)
    return _send_start(f"chips_start_{tag}", sums, _to_chips, [(3,) + s.shape[1:] for s in sums], theirs[0])


def _reduce_end(tag, flying, place, after):
    sums, lands = _send_wait(f"chips_wait_{tag}", *flying, _to_chips, after)
    fulls = []
    for k, (s, q) in enumerate(zip(sums, lands)):
        rr, cc = q.shape[1], q.shape[2]
        tr = _row_tile(rr, cc)

        def add4(tin):
            return [((tin[0].astype(F32) + tin[1].astype(F32)) + tin[2].astype(F32)) + tin[3].astype(F32)]

        ins = [(s, pl.BlockSpec((None, tr, cc), lambda i, sp: (sp[0], i, 0)))]
        ins += [(q, pl.BlockSpec((None, tr, cc), lambda i, sp, _k=kk: (_k, i, 0))) for kk in range(3)]
        (f_k,) = _tiled_sp(f"chip_add_{tag}{k}", add4, (rr // tr,), place, ins,
                           [(_sds((2, rr, cc), F32), pl.BlockSpec((None, tr, cc), lambda i, sp: (sp[1], i, 0)))])
        fulls.append(f_k)
    return fulls


def _pack(parts, PACK_ROWS=PACK_ROWS):
    flat, offs, pos = [], [], 0
    for p in parts:
        v = p.reshape(-1).astype(F32)
        n = -(-v.shape[0] // LANE) * LANE
        flat.append(jnp.pad(v, (0, n - v.shape[0])))
        offs.append((pos, v.shape[0], p.shape))
        pos += n
    total = -(-pos // (PACK_ROWS * LANE)) * PACK_ROWS * LANE
    flat.append(jnp.zeros((total - pos,), F32))
    return jnp.concatenate(flat).reshape(-1, LANE), offs


def _unpack(vec, offs):
    v = vec.reshape(-1)
    return [v[p:p + n].reshape(shape) for p, n, shape in offs]


def _adamw_math(wv, gv, mv, vv):
    bc1 = 1.0 - ADAM_B1 ** ADAM_STEP
    bc2 = 1.0 - ADAM_B2 ** ADAM_STEP
    mn = ADAM_B1 * mv + (1.0 - ADAM_B1) * gv
    vn = ADAM_B2 * vv + (1.0 - ADAM_B2) * (gv * gv)
    delta = -ADAM_LR * ((mn / bc1) / (jnp.sqrt(vn / bc2) + ADAM_EPS) + ADAM_WD * wv)
    return delta, mn, vn


def _adamw(name, w, g, m, v, dep=None):
    rows, cols = w.shape
    tr = rows
    for cand in (512, 256, 128, 64, 32, 16, 8):
        if rows % cand == 0 and cand * cols * 4 <= 2 * 1024 * 1024:
            tr = cand
            break

    def fn(ids, tin, vin):
        return list(_adamw_math(*tin)), []

    spec = pl.BlockSpec((tr, cols), lambda i: (i, 0))
    outs = [(_sds((rows, cols), F32), spec)] * 3
    return _tiled(name, fn, (rows // tr,), [(a, spec) for a in (w, g, m, v)], _behind(dep), outs)


def _adamw_many(name, ws, gs, ms, vs):
    n = len(ws)
    views = [(-1, a.shape[-1]) if a.ndim > 1 else (1, -1) for a in ws]
    flat = lambda arrs: [a.reshape(vw) for a, vw in zip(arrs, views)]

    def kern(*refs):
        ins, outs = refs[:4 * n], refs[4 * n:]
        for t in range(n):
            res = _adamw_math(*[ins[q * n + t][...] for q in range(4)])
            for q in range(3):
                outs[q * n + t][...] = res[q]

    shapes = [_sds(a.shape, F32) for a in flat(ws)]
    res = _pcall(kern, name=name, out_shape=shapes * 3, compiler_params=_cparams(),
                 )(*flat(ws), *flat(gs), *flat(ms), *flat(vs))
    back = lambda part: [a.reshape(w.shape) for a, w in zip(part, ws)]
    return back(res[:n]), back(res[n:2 * n]), back(res[2 * n:])


def _pos_embed():
    n_rows = T // GRID_W
    q = D // 4
    omega = 1.0 / (10000.0 ** (jnp.arange(q, dtype=F32) / q))
    er = jnp.arange(n_rows, dtype=jnp.int32).astype(F32)[:, None] * omega[None, :]
    ec = jnp.arange(GRID_W, dtype=jnp.int32).astype(F32)[:, None] * omega[None, :]
    by_row = jnp.concatenate([jnp.sin(er), jnp.cos(er)], axis=-1)
    by_col = jnp.concatenate([jnp.sin(ec), jnp.cos(ec)], axis=-1)
    return jnp.concatenate([jnp.repeat(by_row, GRID_W, axis=0), jnp.tile(by_col, (n_rows, 1))], axis=-1)


def _dense_gates(w_a, w_x):
    rows = jnp.stack([w_a[0], w_x[0], w_a[1], w_x[1]]).reshape(4, 2, RH, BLK)
    mask, spread = _block_mask(), _block_spread().T.astype(BF16)

    def kern(r_ref, m_ref, s_ref, o_ref):
        tiled = jnp.dot(r_ref[...].astype(BF16), s_ref[...], preferred_element_type=F32)
        o_ref[...] = (tiled * m_ref[...]).astype(o_ref.dtype)

    return _pcall(
        kern, name="gates_dense", grid=(2, 4),
        in_specs=[pl.BlockSpec((None, None, RH, BLK), lambda h, q: (q, h, 0, 0)),
                  pl.BlockSpec((RH, RH), lambda h, q: (0, 0)), pl.BlockSpec((BLK, RH), lambda h, q: (0, 0))],
        out_specs=pl.BlockSpec((None, RH, RH), lambda h, q: (h, 0, q)),
        out_shape=_sds((2, RH, NQ), BF16),
    )(rows, mask, spread)


def _block_mask():
    r = lax.broadcasted_iota(jnp.int32, (RH, RH), 0) // BLK
    c = lax.broadcasted_iota(jnp.int32, (RH, RH), 1) // BLK
    return (r == c).astype(F32)


def _block_spread():
    c = lax.broadcasted_iota(jnp.int32, (RH, BLK), 0) % BLK
    j = lax.broadcasted_iota(jnp.int32, (RH, BLK), 1)
    return (c == j).astype(F32)


def _fold_blocks(dense, mask, spread):
    return jnp.dot(dense * mask, spread, preferred_element_type=F32, precision=lax.Precision.HIGHEST)


def _gate_block_grads(folded):
    per = N_BLK // 2
    kinds = [jnp.concatenate([folded[h, q].reshape(per, BLK, BLK) for h in range(2)], axis=0) for q in range(4)]
    return jnp.stack([kinds[0], kinds[2]]), jnp.stack([kinds[1], kinds[3]])


def _gate_bias_dense(b_a, b_x):
    cols = []
    for h in range(2):
        for src in (b_a[0], b_x[0], b_a[1], b_x[1]):
            cols.append(src.reshape(R)[h * RH:(h + 1) * RH])
    return jnp.concatenate(cols).reshape(1, 2 * NQ)


def _gate_bias_grads(dgb):
    v = dgb.reshape(2, 4, RH)
    kinds = [jnp.concatenate([v[0, q], v[1, q]]).reshape(N_BLK, BLK) for q in range(4)]
    return jnp.stack([kinds[0], kinds[2]]), jnp.stack([kinds[1], kinds[3]])


def _residual_epilogue(next_norm):
    def epi(acc, ex):
        x_new = ex[0] + ex[1] * acc
        outs = [acc, x_new]
        if next_norm:
            outs.append(_norm_mod(x_new, ex[-3], ex[-2], ex[-1]))
        return outs
    return epi


def _mlp_fwd(tag, x_in, h, gate, w_in, w_out, next_norm=None, dep=None):
    tm = MM_TILE
    (r,) = _mm(f"{tag}_in", h, w_in, _NN, (T // tm, 4, 1),
               pl.BlockSpec((tm, D), lambda i, j, k: (i, 0)), pl.BlockSpec((None, D, D), lambda i, j, k: (j, 0, 0)),
               [(_sds((T, FF), BF16), pl.BlockSpec((tm, D), lambda i, j, k: (i, j)))], (tm, D),
               extra=[(d_, _full_spec(d_)) for d_ in _behind(dep)], epi=lambda acc, ex: [jnp.maximum(acc, 0.0)])
    row_spec = pl.BlockSpec((tm, D), lambda i, j, k: (i, 0))
    outs = [(_sds((T, D), F32), row_spec)] * 2 + ([(_sds((T, D), BF16), row_spec)] if next_norm else [])
    res = _mm(f"{tag}_out", r, w_out, _NN, (T // tm, 1, FF // D),
              pl.BlockSpec((tm, D), lambda i, j, k: (i, k)), pl.BlockSpec((D, D), lambda i, j, k: (k, 0)),
              outs, (tm, D),
              extra=[(x_in, row_spec), (gate, _full_spec(gate))] + [(v, _full_spec(v)) for v in next_norm or ()],
              a_pre=lambda a: a * a, epi=_residual_epilogue(next_norm))
    return dict(h=h, r=r, o=res[0], x_in=x_in), res[1], (res[2] if next_norm else None)


def _behind(dep):
    return [] if dep is None else [dep]


def _gate_bwd(tag, dx, o, gate, dep=None):
    def fn(ids, t, v):
        d_o = t[0] * v[0]
        return [d_o], [_sum0(t[0] * t[1]), _sum0(d_o)]
    return _tiled(f"{tag}_gate_bwd", fn, (T // ROW_TILE,), [_rows(dx), _rows(o)], [gate] + _behind(dep),
                  [_orow(T, D, BF16)], [(1, D), (1, D)])


def _norm_bwd(tag, dx_res, dh, dh_off, x, g_norm, sc, with_dx=True, dep=None):
    n_t = x.shape[0] // ROW_TILE

    def fn(ids, t, v):
        if with_dx:
            dres, dhv, xv = t
        else:
            dhv, xv = t
        dxv, d_sh, d_sc, d_g = _norm_mod_bwd(dhv, xv, v[0], v[1])
        return ([dres + dxv] if with_dx else []), [d_sh, d_sc, d_g]

    ins = ([_rows(dx_res)] if with_dx else []) + [_rows(dh, off=dh_off), _rows(x)]
    outs = [_orow(x.shape[0], D, F32)] if with_dx else []
    return _tiled(f"{tag}_norm_bwd", fn, (n_t,), ins, [g_norm, sc] + _behind(dep), outs, [(1, D)] * 3)


def _mlp_bwd(tag, dx, saved, g_norm, sc, gate, w_in, w_out, dep=None):
    d_o, d_gate, _ = _gate_bwd(tag, dx, saved["o"], gate, dep)
    tm = MM_TILE
    r = saved["r"]
    (da,) = _mm(f"{tag}_dz", d_o, w_out, _NT, (T // tm, FF // D, 1),
                pl.BlockSpec((tm, D), lambda i, j, k: (i, 0)), pl.BlockSpec((D, D), lambda i, j, k: (j, 0)),
                [(_sds((T, FF), BF16), pl.BlockSpec((tm, D), lambda i, j, k: (i, j)))], (tm, D),
                extra=[(r, pl.BlockSpec((tm, D), lambda i, j, k: (i, j)))],
                epi=lambda acc, ex: [acc * (2.0 * ex[0].astype(F32))])
    tk = MM_TILE
    (dw_out,) = _mm(f"{tag}_dwout", r, d_o, _TN, (FF // tm, 1, T // tk),
                    pl.BlockSpec((tk, tm), lambda i, j, k: (k, i)), pl.BlockSpec((tk, D), lambda i, j, k: (k, 0)),
                    [(_sds((FF, D), BF16), pl.BlockSpec((tm, D), lambda i, j, k: (i, 0)))], (tm, D),
                    a_pre=lambda a: a * a)
    (dh,) = _mm(f"{tag}_dh", da, w_in, _NT, (T // tm, 1, 4),
                pl.BlockSpec((tm, D), lambda i, j, k: (i, k)), pl.BlockSpec((None, D, D), lambda i, j, k: (k, 0, 0)),
                [(_sds((T, D), F32), pl.BlockSpec((tm, D), lambda i, j, k: (i, 0)))], (tm, D))
    (dw_in,) = _mm(f"{tag}_dwin", saved["h"], da, _TN, (D // tm, 4, T // tk),
                   pl.BlockSpec((tk, tm), lambda i, j, k: (k, i)), pl.BlockSpec((tk, D), lambda i, j, k: (k, j)),
                   [(_sds((4, D, D), BF16), pl.BlockSpec((None, tm, D), lambda i, j, k: (j, i, 0)))], (tm, D))
    dx_in, d_sh, d_sc, d_g = _norm_bwd(tag, dx, dh, 0, saved["x_in"], g_norm, sc)
    return dx_in, dw_in, dw_out, dict(sh=d_sh, sc=d_sc, gate=d_gate, g_norm=d_g)


def _local_step(x, ctx, tgt, mods, cmods, norm_g, final_g, rec, conf, wg, on_grads=None, wg_pre=None, on_later=None):
    on_grads = on_grads or (lambda group, dws: None)
    wg_pre = wg_pre or (lambda group, after: None)
    on_later = on_later or (lambda after: None)
    n_t = T // ROW_TILE
    row = lambda v: v.reshape(1, -1)
    m0 = [row(mods[0, q]) for q in range(6)]
    m1 = [row(mods[1, q]) for q in range(6)]
    g00, g01, g10, g11 = (row(norm_g[0, 0]), row(norm_g[0, 1]), row(norm_g[1, 0]), row(norm_g[1, 1]))
    csh, csc = row(cmods[0]), row(cmods[1])
    pos = _pos_embed()

    def prep0(ids, t, v):
        cx, xv, pv = t
        is_ctx = ids[0] == 0
        xin = jnp.where(is_ctx, cx, xv + pv)
        sh = jnp.where(is_ctx, v[3], v[1])
        sc = jnp.where(is_ctx, v[4], v[2])
        return [_norm_mod(xin, v[0], sc, sh), xv + pv], []

    dep = wg_pre("rec_in", csh)
    hcat, x0 = _tiled(
        "prep0", prep0, (N_SCAN,),
        [(ctx, pl.BlockSpec((ROW_TILE, D), lambda i: (0, 0))), _rows(x, off=-1, clamp_lo=True),
         _rows(pos, off=-1, clamp_lo=True)],
        [g00, m0[0], m0[1], csh, csc] + _behind(dep),
        [_orow(TA, D, BF16), _orow(T, D, F32, off=-1, clamp_lo=True)])

    tm_a = REC_TILE
    w_rin = wg("rec_in", hcat)["rec_w_in"]
    (a_in,) = _mm("rec_in", hcat, w_rin, _NN, (TA // tm_a, 4, 1),
                  pl.BlockSpec((tm_a, D), lambda i, j, k: (i, 0)),
                  pl.BlockSpec((None, D, RH), lambda i, j, k: (j, 0, 0)),
                  [(_sds((TA, 2 * R), F32), pl.BlockSpec((tm_a, RH), lambda i, j, k: (i, j)))], (tm_a, RH))
    rec_starts = (0, 1)
    u = _dwconv("rec_conv", a_in, R // CW_REC, rec["conv_w"], row(rec["conv_b"]), 1, rec_starts, R, CW_REC)
    wbd = _dense_gates(rec["w_a"], rec["w_x"])
    gbias = _gate_bias_dense(rec["b_a"], rec["b_x"])
    lam = rec["lam"]
    a_f, b_f, a_r, b_r = _tiled("rg_fwd", _rg_fwd_fn, (TA // RG_TILE,), [_rows(u, tm=RG_TILE)], [wbd, gbias, lam],
                                [_orow(TA, R, F32, tm=RG_TILE)] * 4, vec_refs=True)
    dep = wg_pre("rec_out", a_f)
    dep = wg_pre("mlp0", a_f if dep is None else dep)
    y_f, y_r, hin_f, hin_r = _scan_fwd(a_f, b_f, a_r, b_r)

    def rec_mid(ids, t, v):
        gp, yf, yr = t
        g, _ = _gelu(gp)
        return [g * (yf + yr)], []

    (m_rec,) = _tiled("rec_mid", rec_mid, (n_t,),
                      [_rows(a_in, R, off=1), _rows(y_f, off=1), _rows(y_r, off=1)], _behind(dep),
                      [_orow(T, R, BF16)])
    tm = MM_TILE
    row_spec = pl.BlockSpec((tm, D), lambda i, j, k: (i, 0))
    norm_mlp0 = (g01, m0[4], m0[3])
    w_rout = wg("rec_out", m_rec)["rec_w_out"]
    o_rec, x1, h_mlp0 = _mm(
        "rec_out", m_rec, w_rout, _NN, (T // tm, 1, 1),
        pl.BlockSpec((tm, R), lambda i, j, k: (i, 0)), pl.BlockSpec((R, D), lambda i, j, k: (0, 0)),
        [(_sds((T, D), F32), row_spec)] * 2 + [(_sds((T, D), BF16), row_spec)], (tm, D),
        extra=[(x0, row_spec), (m0[2], _full_spec(m0[2]))] + [(v, _full_spec(v)) for v in norm_mlp0],
        epi=_residual_epilogue(norm_mlp0))
    w_m0 = wg("mlp0", x1)
    dep = wg_pre("conf", x1)
    mlp0, x2, h1 = _mlp_fwd("mlp0", x1, h_mlp0, m0[5], w_m0["w_in"], w_m0["w_out"], (g10, m1[1], m1[0]), dep)

    b_pw1 = row(conf["b_pw1"])
    w_cf = wg("conf", x2)
    dep = wg_pre("mlp1", x2)
    (pre,) = _mm("conf_pw1", h1, w_cf["conf_w_pw1"], _NN, (T // tm, 4, 1),
                 pl.BlockSpec((tm, D), lambda i, j, k: (i, 0)),
                 pl.BlockSpec((None, D, D // 2), lambda i, j, k: (j, 0, 0)),
                 [(_sds((T, 2 * D), F32), pl.BlockSpec((tm, D // 2), lambda i, j, k: (i, j)))], (tm, D // 2),
                 extra=[(b_pw1, pl.BlockSpec((1, D // 2), lambda i, j, k: (0, j)))]
                 + [(d_, _full_spec(d_)) for d_ in _behind(dep)],
                 epi=lambda acc, ex: [acc + ex[0]])
    (zg,) = _tiled("conf_glu", lambda ids, t, v: ([t[0] * _sigmoid(t[1])], []), (n_t,),
                   [_rows(pre, D, col=0), _rows(pre, D, col=1)], [], [_orow(T, D, F32)])
    conf_starts = (0,)
    zc = _dwconv("conf_conv", zg, 0, conf["conv_w"], row(conf["conv_b"]), CONF_KW // 2, conf_starts, D, CW_CONF)
    ln_g, ln_b = row(conf["ln_g"]), row(conf["ln_b"])

    def ln_silu(ids, t, v):
        nh, _ = _layernorm_parts(t[0])
        ln = nh * v[0] + v[1]
        return [ln * _sigmoid(ln)], []

    (s_conf,) = _tiled("conf_ln", ln_silu, (n_t,), [_rows(zc)], [ln_g, ln_b], [_orow(T, D, BF16)])
    b_pw2 = row(conf["b_pw2"])
    norm_mlp1 = (g11, m1[4], m1[3])
    pw2_epi = _residual_epilogue(norm_mlp1)
    y_conf, x3, h_mlp1 = _mm(
        "conf_pw2", s_conf, w_cf["conf_w_pw2"], _NN, (T // tm, 1, 1),
        row_spec, pl.BlockSpec((D, D), lambda i, j, k: (0, 0)),
        [(_sds((T, D), F32), row_spec)] * 2 + [(_sds((T, D), BF16), row_spec)], (tm, D),
        extra=[(x2, row_spec), (m1[2], _full_spec(m1[2])), (b_pw2, _full_spec(b_pw2))]
        + [(v, _full_spec(v)) for v in norm_mlp1],
        epi=lambda acc, ex: pw2_epi(acc + ex[2], ex))
    w_m1 = wg("mlp1", x3)
    mlp1, x4, _ = _mlp_fwd("mlp1", x3, h_mlp1, m1[5], w_m1["w_in"], w_m1["w_out"])

    fg = row(final_g)

    def head(ids, t, v):
        n, r = _rms(t[0])
        err = n * v[0] - t[1]
        d_out = err * (1.0 / D)
        dn = d_out * v[0]
        dxv = r * (dn - n * jnp.mean(dn * n, axis=-1, keepdims=True))
        part = jnp.sum(_sum0(err * err), axis=1, keepdims=True) * (0.5 / D)
        return [dxv], [part, _sum0(d_out * n)]

    dx4, loss, d_fg = _tiled("head", head, (n_t,), [_rows(x4), _rows(tgt)], [fg], [_orow(T, D, F32)],
                             [(1, 1), (1, D)])

    dx3, dw_in1, dw_out1, dm_mlp1 = _mlp_bwd("mlp1", dx4, mlp1, g11, m1[4], m1[5],
                                             w_m1["w_in"], w_m1["w_out"])
    dep = on_grads("mlp1", (dw_in1, dw_out1))
    d_y, d_g1c, d_bpw2 = _gate_bwd("conf", dx3, y_conf, m1[2], dep)
    tk = MM_TILE
    (dw_pw2,) = _mm("conf_dwpw2", s_conf, d_y, _TN, (D // tm, 1, T // tk),
                    pl.BlockSpec((tk, tm), lambda i, j, k: (k, i)), pl.BlockSpec((tk, D), lambda i, j, k: (k, 0)),
                    [(_sds((D, D), BF16), pl.BlockSpec((tm, D), lambda i, j, k: (i, 0)))], (tm, D))
    (ds,) = _mm("conf_ds", d_y, w_cf["conf_w_pw2"], _NT, (T // tm, 1, 1),
                pl.BlockSpec((tm, D), lambda i, j, k: (i, 0)), pl.BlockSpec((D, D), lambda i, j, k: (0, 0)),
                [(_sds((T, D), F32), pl.BlockSpec((tm, D), lambda i, j, k: (i, 0)))], (tm, D))
    dep = on_later(ds)

    def ln_silu_bwd(ids, t, v):
        dsv, zcv = t
        nh, rstd = _layernorm_parts(zcv)
        ln = nh * v[0] + v[1]
        sg = _sigmoid(ln)
        d_ln = dsv * (sg * (1.0 + ln * (1.0 - sg)))
        d_nh = d_ln * v[0]
        d_zc = rstd * (d_nh - jnp.mean(d_nh, axis=-1, keepdims=True)
                       - nh * jnp.mean(d_nh * nh, axis=-1, keepdims=True))
        return [d_zc], [_sum0(d_ln * nh), _sum0(d_ln)]

    d_zc, d_lng, d_lnb = _tiled("conf_ln_bwd", ln_silu_bwd, (n_t,), [_rows(ds), _rows(zc)],
                                [ln_g, ln_b] + _behind(dep), [_orow(T, D, F32)], [(1, D), (1, D)])
    d_zg = _dwconv("conf_conv_dx", d_zc, 0, conf["conv_w"][::-1], jnp.zeros((1, D), F32),
                   CONF_KW - 1 - CONF_KW // 2, conf_starts, D, CW_CONF)

    def glu_bwd(ids, t, v):
        dz, pa, pb = t
        sg = _sigmoid(pb)
        d_a = dz * sg
        d_b = dz * pa * sg * (1.0 - sg)
        return [jnp.concatenate([d_a, d_b], axis=1)], [_sum0(d_a), _sum0(d_b)]

    d_pre, d_b1a, d_b1b = _tiled(
        "conf_glu_bwd", glu_bwd, (n_t,), [_rows(d_zg), _rows(pre, D, col=0), _rows(pre, D, col=1)], [],
        [_orow(T, 2 * D, BF16)], [(1, D), (1, D)])
    (dw_pw1,) = _mm("conf_dwpw1", h1, d_pre, _TN, (D // tm, 4, T // tk),
                    pl.BlockSpec((tk, tm), lambda i, j, k: (k, i)),
                    pl.BlockSpec((tk, D // 2), lambda i, j, k: (k, j)),
                    [(_sds((4, D, D // 2), BF16), pl.BlockSpec((None, tm, D // 2), lambda i, j, k: (j, i, 0)))],
                    (tm, D // 2))
    dep = on_grads("conf", (dw_pw1, dw_pw2))
    (dh1,) = _mm("conf_dh", d_pre, w_cf["conf_w_pw1"], _NT, (T // tm, 1, 4),
                 pl.BlockSpec((tm, D // 2), lambda i, j, k: (i, k)),
                 pl.BlockSpec((None, D, D // 2), lambda i, j, k: (k, 0, 0)),
                 [(_sds((T, D), F32), pl.BlockSpec((tm, D), lambda i, j, k: (i, 0)))], (tm, D))
    dx2, d_sh1c, d_sc1c, d_g10 = _norm_bwd("conf", dx3, dh1, 0, x2, g10, m1[1], dep=dep)
    dep = on_later(dx2)

    dx1, dw_in0, dw_out0, dm_mlp0 = _mlp_bwd("mlp0", dx2, mlp0, g01, m0[4], m0[5],
                                             w_m0["w_in"], w_m0["w_out"], dep)
    dep = on_grads("mlp0", (dw_in0, dw_out0))
    d_orec, d_g1r, _ = _gate_bwd("rec", dx1, o_rec, m0[2], dep)
    (dw_rout,) = _mm("rec_dwout", m_rec, d_orec, _TN, (R // RH, 1, T // tk),
                     pl.BlockSpec((tk, RH), lambda i, j, k: (k, i)), pl.BlockSpec((tk, D), lambda i, j, k: (k, 0)),
                     [(_sds((R, D), BF16), pl.BlockSpec((RH, D), lambda i, j, k: (i, 0)))], (RH, D))
    (dm_rec,) = _mm("rec_dm", d_orec, w_rout, _NT, (T // tm, 1, 1),
                    pl.BlockSpec((tm, D), lambda i, j, k: (i, 0)), pl.BlockSpec((R, D), lambda i, j, k: (0, 0)),
                    [(_sds((T, R), F32), pl.BlockSpec((tm, R), lambda i, j, k: (i, 0)))], (tm, R))
    dep = on_later(dm_rec)

    def rec_mid_bwd(ids, t, v):
        dmv, gp, yf, yr = t
        g, th = _gelu(gp)
        lat = ids[0] > 0
        d_gp = jnp.where(lat, dmv * (yf + yr) * _gelu_grad(gp, th), 0.0)
        dy = jnp.where(lat, dmv * g, 0.0)
        return [d_gp, dy], []

    d_gp, dy = _tiled("rec_mid_bwd", rec_mid_bwd, (N_SCAN,),
                      [_rows(dm_rec, off=-1, clamp_lo=True), _rows(a_in, R), _rows(y_f), _rows(y_r)], _behind(dep),
                      [_orow(TA, R, BF16), _orow(TA, R, F32)])
    da_f, db_f, da_r, db_r = _scan_bwd(dy, a_f, y_f, hin_f, a_r, y_r, hin_r)
    d_gpre, d_u, d_gbias, d_lam = _tiled(
        "rg_bwd", _rg_bwd_fn, (TA // RG_TILE,), [_rows(a, tm=RG_TILE) for a in (u, da_f, db_f, da_r, db_r)],
        [wbd, gbias, lam], [_orow(TA, 2 * NQ, BF16, tm=RG_TILE), _orow(TA, R, F32, tm=RG_TILE)],
        [(1, 2 * NQ), (1, 2 * R)], vec_refs=True)
    tk_a = REC_TILE
    d_p = _dwconv("rec_conv_dx", d_u, 0, rec["conv_w"][::-1], jnp.zeros((1, R), F32), REC_KW - 1 - 1,
                  rec_starts, R, CW_REC)
    d_a = jnp.concatenate([d_gp, d_p.astype(BF16)], axis=1)
    (dw_rin,) = _mm("rec_dwin", hcat, d_a, _TN, (D // tm, 4, TA // tk_a),
                    pl.BlockSpec((tk_a, tm), lambda i, j, k: (k, i)), pl.BlockSpec((tk_a, RH), lambda i, j, k: (k, j)),
                    [(_sds((4, D, RH), BF16), pl.BlockSpec((None, tm, RH), lambda i, j, k: (j, i, 0)))], (tm, RH))
    dep = on_grads("rec", (dw_rin, dw_rout))
    (dhcat,) = _mm("rec_dh", d_a, w_rin, _NT, (TA // tm_a, 1, 4),
                   pl.BlockSpec((tm_a, RH), lambda i, j, k: (i, k)),
                   pl.BlockSpec((None, D, RH), lambda i, j, k: (k, 0, 0)),
                   [(_sds((TA, D), F32), pl.BlockSpec((tm_a, D), lambda i, j, k: (i, 0)))], (tm_a, D))
    dx0, d_sh1r, d_sc1r, d_g00 = _norm_bwd("rec", dx1, dhcat, 1, x0, g00, m0[1], dep=dep)
    dep = on_later(dx0)

    d_csh, d_csc, d_g00c = _norm_bwd("ctx", None, dhcat, 0, ctx, g00, csc, with_dx=False, dep=dep)
    blk_mask, blk_spread = _block_mask(), _block_spread()
    (d_wbd,) = _mm("rg_dw", u, d_gpre, _TN, (2, 2, TA // tk_a),
                   pl.BlockSpec((tk_a, RH), lambda i, j, k: (k, i)),
                   pl.BlockSpec((tk_a, NQ // 2), lambda i, j, k: (k, 2 * i + j)),
                   [(_sds((2, 4, RH, BLK), F32), pl.BlockSpec((None, 2, RH, BLK), lambda i, j, k: (i, j, 0, 0)))],
                   (RH, NQ // 2),
                   extra=[(blk_mask, _full_spec(blk_mask)), (blk_spread, _full_spec(blk_spread))]
                   + [(d, _full_spec(d)) for d in _behind(dep)],
                   epi=lambda acc, ex: [jnp.stack([_fold_blocks(acc[:, s * RH:(s + 1) * RH], ex[0], ex[1])
                                                   for s in range(2)])])
    d_cw_rec = _dwconv_wgrad("rec_conv_dw", d_u, a_in, R // CW_REC, REC_KW, 1, rec_starts, R, CW_REC, dep)
    d_cw_conf = _dwconv_wgrad("conf_conv_dw", d_zc, zg, 0, CONF_KW, CONF_KW // 2, conf_starts, D, CW_CONF, dep)

    big = dict(rec_w_in=dw_rin, rec_w_out=dw_rout, conf_w_pw1=dw_pw1, conf_w_pw2=dw_pw2,
               mlp_w_in=(dw_in0, dw_in1), mlp_w_out=(dw_out0, dw_out1))
    d_wa, d_wx = _gate_block_grads(d_wbd)
    d_ba, d_bx = _gate_bias_grads(d_gbias)
    d_mod = jnp.concatenate([
        d_sh1r, d_sc1r, d_g1r, dm_mlp0["sh"], dm_mlp0["sc"], dm_mlp0["gate"],
        d_sh1c, d_sc1c, d_g1c, dm_mlp1["sh"], dm_mlp1["sc"], dm_mlp1["gate"]], axis=1).reshape(2, 6 * D)
    small = dict(
        d_mod=d_mod, d_cmod=jnp.concatenate([d_csh, d_csc], axis=1),
        norm_g=jnp.concatenate([d_g00 + d_g00c, dm_mlp0["g_norm"], d_g10, dm_mlp1["g_norm"]], axis=1),
        rec_conv_w=d_cw_rec[:REC_KW], rec_conv_b=d_cw_rec[REC_KW], rec_lambda=d_lam.reshape(2, R),
        rec_w_a=d_wa, rec_b_a=d_ba, rec_w_x=d_wx, rec_b_x=d_bx,
        conf_b_pw1=jnp.concatenate([d_b1a, d_b1b], axis=1), conf_conv_w=d_cw_conf[:CONF_KW],
        conf_conv_b=d_cw_conf[CONF_KW], conf_ln_g=d_lng, conf_ln_b=d_lnb, conf_b_pw2=d_bpw2, final_g=d_fg)
    return loss.reshape(()), dx0, big, small


_BIG = ("rec_w_in", "rec_w_out", "conf_w_pw1", "conf_w_pw2", "mlp_w_in", "mlp_w_out")


def _halves(w):
    return w.reshape(w.shape[0], 2, w.shape[1] // 2, w.shape[2])


def _ada_fwd(c16, w_ada, b_shard):
    ns = w_ada.shape[2]
    tn = 512

    def kern(c_ref, w_ref, b_ref, o_ref):
        cv = c_ref[...]
        s = (cv * _sigmoid(cv)).astype(BF16)
        o_ref[...] = jnp.dot(s, w_ref[...].astype(BF16), preferred_element_type=F32) + b_ref[...]

    return _pcall(
        kern, name="ada_fwd", grid=(2, ns // tn),
        in_specs=[pl.BlockSpec((16, D), lambda l, j: (0, 0)), pl.BlockSpec((None, D, tn), lambda l, j: (l, 0, j)),
                  pl.BlockSpec((None, 1, tn), lambda l, j: (l, 0, j))],
        out_specs=pl.BlockSpec((None, 16, tn), lambda l, j: (l, 0, j)),
        out_shape=_sds((2, 16, ns), F32), compiler_params=_cparams(),
    )(c16, w_ada, b_shard)


def _ada_bwd(c16, dm16, w_ada):
    ns = w_ada.shape[2]
    tn = 512

    def kern(c_ref, dm_ref, w_ref, gw_ref, ds_ref):
        cv = c_ref[...]
        s = (cv * _sigmoid(cv)).astype(BF16)
        dm = dm_ref[...].astype(BF16)
        gw_ref[...] = lax.dot_general(s, dm, _TN, preferred_element_type=F32)

        @pl.when(jnp.logical_and(pl.program_id(0) == 0, pl.program_id(1) == 0))
        def _():
            ds_ref[...] = jnp.zeros_like(ds_ref)

        ds_ref[...] += lax.dot_general(dm, w_ref[...].astype(BF16), _NT, preferred_element_type=F32)

    return _pcall(
        kern, name="ada_bwd", grid=(2, ns // tn),
        in_specs=[pl.BlockSpec((16, D), lambda l, j: (0, 0)), pl.BlockSpec((None, 16, tn), lambda l, j: (l, 0, j)),
                  pl.BlockSpec((None, D, tn), lambda l, j: (l, 0, j))],
        out_specs=[pl.BlockSpec((None, D, tn), lambda l, j: (l, 0, j)), pl.BlockSpec((16, D), lambda l, j: (0, 0))],
        out_shape=[_sds((2, D, ns), F32), _sds((16, D), F32)], compiler_params=_cparams(),
    )(c16, dm16, w_ada)


def _cctx_grad(ds4, c_ctx):
    def kern(d_ref, c_ref, o_ref):
        tot = d_ref[0, 0:1, :] + d_ref[1, 0:1, :] + d_ref[2, 0:1, :] + d_ref[3, 0:1, :]
        cv = c_ref[...]
        sg = _sigmoid(cv)
        o_ref[...] = tot * (sg * (1.0 + cv * (1.0 - sg)))

    return _pcall(kern, name="cctx_grad", out_shape=_sds((1, D), F32))(ds4, c_ctx.reshape(1, D))


def kernel(x, c, ctx, c_ctx, w_ada, b_ada, norm_g, rec_w_in, rec_conv_w, rec_conv_b, rec_lambda, rec_w_a, rec_b_a, rec_w_x, rec_b_x, rec_w_out, conf_w_pw1, conf_b_pw1, conf_conv_w, conf_conv_b, conf_ln_g, conf_ln_b, conf_w_pw2, conf_b_pw2, mlp_w_in, mlp_w_out, final_g, loss_target, m_c_ctx, m_w_ada, m_b_ada, m_norm_g, m_rec_w_in, m_rec_conv_w, m_rec_conv_b, m_rec_lambda, m_rec_w_a, m_rec_b_a, m_rec_w_x, m_rec_b_x, m_rec_w_out, m_conf_w_pw1, m_conf_b_pw1, m_conf_conv_w, m_conf_conv_b, m_conf_ln_g, m_conf_ln_b, m_conf_w_pw2, m_conf_b_pw2, m_mlp_w_in, m_mlp_w_out, m_final_g, v_c_ctx, v_w_ada, v_b_ada, v_norm_g, v_rec_w_in, v_rec_conv_w, v_rec_conv_b, v_rec_lambda, v_rec_w_a, v_rec_b_a, v_rec_w_x, v_rec_b_x, v_rec_w_out, v_conf_w_pw1, v_conf_b_pw1, v_conf_conv_w, v_conf_conv_b, v_conf_ln_g, v_conf_ln_b, v_conf_w_pw2, v_conf_b_pw2, v_mlp_w_in, v_mlp_w_out, v_final_g):
    names = ["c_ctx", "w_ada", "b_ada", "norm_g", "rec_w_in", "rec_conv_w", "rec_conv_b", "rec_lambda", "rec_w_a",
             "rec_b_a", "rec_w_x", "rec_b_x", "rec_w_out", "conf_w_pw1", "conf_b_pw1", "conf_conv_w", "conf_conv_b",
             "conf_ln_g", "conf_ln_b", "conf_w_pw2", "conf_b_pw2", "mlp_w_in", "mlp_w_out", "final_g"]
    w = dict(zip(names, [c_ctx, w_ada, b_ada, norm_g, rec_w_in, rec_conv_w, rec_conv_b, rec_lambda, rec_w_a,
                         rec_b_a, rec_w_x, rec_b_x, rec_w_out, conf_w_pw1, conf_b_pw1, conf_conv_w, conf_conv_b,
                         conf_ln_g, conf_ln_b, conf_w_pw2, conf_b_pw2, mlp_w_in, mlp_w_out, final_g]))
    m = dict(zip(names, [m_c_ctx, m_w_ada, m_b_ada, m_norm_g, m_rec_w_in, m_rec_conv_w, m_rec_conv_b, m_rec_lambda,
                         m_rec_w_a, m_rec_b_a, m_rec_w_x, m_rec_b_x, m_rec_w_out, m_conf_w_pw1, m_conf_b_pw1,
                         m_conf_conv_w, m_conf_conv_b, m_conf_ln_g, m_conf_ln_b, m_conf_w_pw2, m_conf_b_pw2,
                         m_mlp_w_in, m_mlp_w_out, m_final_g]))
    v = dict(zip(names, [v_c_ctx, v_w_ada, v_b_ada, v_norm_g, v_rec_w_in, v_rec_conv_w, v_rec_conv_b, v_rec_lambda,
                         v_rec_w_a, v_rec_b_a, v_rec_w_x, v_rec_b_x, v_rec_w_out, v_conf_w_pw1, v_conf_b_pw1,
                         v_conf_conv_w, v_conf_conv_b, v_conf_ln_g, v_conf_ln_b, v_conf_w_pw2, v_conf_b_pw2,
                         v_mlp_w_in, v_mlp_w_out, v_final_g]))
    mx, my, mc = _me()
    chip = 2 * mx + my
    me = 4 * mx + 2 * my + mc

    sharded_small = ["norm_g", "rec_conv_w", "rec_lambda", "conf_b_pw1", "conf_conv_w", "conf_conv_b", "conf_ln_g",
                     "conf_ln_b", "conf_b_pw2"]
    packed, offs = _pack([c] + [w[k] for k in sharded_small], 8)
    place = jnp.stack([chip, mc]).astype(jnp.int32)
    shards = [("rec_in", _halves(rec_w_in), 0), ("rec_out", _halves(rec_w_out), 0),
              ("pw1", _halves(conf_w_pw1), 0), ("pw2", _halves(conf_w_pw2), 0),
              ("mlp_in0", _halves(mlp_w_in), 0), ("mlp_in1", _halves(mlp_w_in), 1),
              ("mlp_out0", _halves(mlp_w_out), 0), ("mlp_out1", _halves(mlp_w_out), 1)]
    small_state, small_sent = _send_start("gather_small_start", [packed[None]], _to_all7, [(7,) + packed.shape], place)
    (slot_rin,) = _place_big(shards[:1], place, small_sent)
    flying, gsems, swapping = {}, {}, {}
    flying["rec_in"], gsems["rec_in"], rec_started = _gather_start("gather_start_rec", [slot_rin], ((0,),), small_sent)
    slots = [slot_rin] + _place_big(shards[1:], place, rec_started)
    placed = jnp.broadcast_to(lax.dynamic_slice(slots[-1], (chip, 0, 0, 0), (1, 1, 1, 1)).reshape(1, 1), (8, 1))
    (own,), (landed,) = _send_wait("gather_small_wait", *small_state, _to_all7, placed)
    by_flip = jnp.concatenate([own, landed], axis=0)
    got_flat = jnp.take(by_flip, jnp.arange(8) ^ me, axis=0).reshape(8, -1)

    def piece(i):
        p, n, shape = offs[i]
        return got_flat[:, p:p + n].reshape((8,) + tuple(shape))

    c_rows = piece(0).reshape(8, D)
    full = {}
    for i, k in enumerate(sharded_small):
        per_chip = jnp.moveaxis(piece(1 + i)[0::2], 0, -2)
        full[k] = per_chip.reshape(per_chip.shape[:-2] + (4 * per_chip.shape[-1],))
    c16 = jnp.concatenate([c_rows, c_ctx.reshape(1, D), jnp.zeros((7, D), F32)], axis=0)

    ns = w_ada.shape[2]
    b_shard = lax.dynamic_slice_in_dim(b_ada, chip * ns, ns, axis=1).reshape(2, 1, ns)
    prod = _ada_fwd(c16, w_ada, b_shard)

    own_rows = lax.dynamic_index_in_dim(prod[:, :8].reshape(2, 4, 2, ns), mc, axis=2, keepdims=False)
    rows = jnp.concatenate([own_rows.transpose(1, 0, 2), jnp.broadcast_to(prod[0, 8], (4, 1, ns)),
                            jnp.zeros((4, 5, ns), F32)], axis=1)
    mod_state, mod_started = _send_start("mod_start", [rows], _to_chips, [(3, 8, ns)], place)
    use_order = dict(rec=(0, 1), mlp0=(4, 6), conf=(2, 3), mlp1=(5, 7))
    fetch_order = dict(rec_out=(1,), mlp0=(4, 6), conf=(2, 3), mlp1=(5, 7))
    order = [t for g in fetch_order for t in fetch_order[g]]
    groups = [tuple(order.index(t) for t in fetch_order[g]) for g in fetch_order]
    fly, sems, all_started = _gather_start("gather_start_rest", [slots[t] for t in order], tuple(groups), mod_started)
    for gi, g in enumerate(fetch_order):
        flying[g], gsems[g] = [fly[k] for k in groups[gi]], sems[2 * gi:2 * gi + 2]

    def wg_pre(group, after):
        bufs = _gather_wait(f"gather_wait_{group}", flying[group], *gsems[group], after)
        swapping[group], token = _swap_start(f"swap_start_{group}", bufs, after)
        return token

    def wg(group, after):
        bufs = _swap_wait(f"swap_wait_{group}", *swapping[group], after)
        if group == "rec_in":
            return dict(rec_w_in=bufs[0].reshape(4, D, RH))
        if group == "rec_out":
            return dict(rec_w_out=bufs[0].reshape(R, D))
        if group == "conf":
            return dict(conf_w_pw1=bufs[0].reshape(4, D, D // 2), conf_w_pw2=bufs[1].reshape(D, D))
        return dict(w_in=bufs[0].reshape(4, D, D), w_out=bufs[1].reshape(FF, D))

    (rows,), (landed,) = _send_wait("mod_wait", *mod_state, _to_chips, all_started)
    own = lax.dynamic_index_in_dim(rows, chip, axis=0, keepdims=True)
    by_flip = jnp.concatenate([own, landed[1:2], landed[0:1], landed[2:3]], axis=0)
    by_chip = jnp.take(by_flip, jnp.arange(4) ^ chip, axis=0)
    mods = by_chip[:, :2].transpose(1, 0, 2).reshape(2, 6, D)
    cmods = by_chip[:, 2].reshape(6, D)[:2]

    rec = dict(conv_w=full["rec_conv_w"][0], conv_b=rec_conv_b[0], lam=full["rec_lambda"][0],
               w_a=rec_w_a[0], b_a=rec_b_a[0], w_x=rec_w_x[0], b_x=rec_b_x[0])
    conf = dict(b_pw1=full["conf_b_pw1"][0], conv_w=full["conf_conv_w"][0], conv_b=full["conf_conv_b"][0],
                ln_g=full["conf_ln_g"][0], ln_b=full["conf_ln_b"][0], b_pw2=full["conf_b_pw2"][0])
    pairing, sent = {}, {}

    def on_grads(group, dws):
        parts = [dw.reshape((4,) + shards[t][1].shape[1:]) for dw, t in zip(dws, use_order[group])]
        pairing[group], token = _reduce_begin(group, parts, place)
        return token

    def finish_pair(after):
        (group, state), = pairing.items()
        pairing.clear()
        sent[group], sent["token"] = _reduce_mid(group, state, place, after)
        return sent["token"]

    loss_local, grad_x, _, small = _local_step(x[0], ctx[0], loss_target[0], mods, cmods, full["norm_g"], final_g,
                                               rec, conf, wg, on_grads, wg_pre, finish_pair)
    rec_sent = sent["token"]
    small["loss"] = loss_local.reshape(1)

    small_names = ["loss", "d_mod", "d_cmod", "norm_g", "rec_conv_w", "rec_conv_b", "rec_lambda", "rec_w_a", "rec_b_a",
                   "rec_w_x", "rec_b_x", "conf_b_pw1", "conf_conv_w", "conf_conv_b", "conf_ln_g", "conf_ln_b",
                   "conf_b_pw2", "final_g"]
    mine = lax.broadcasted_iota(jnp.int32, (8, 1), 0) == me
    mod_slots = jnp.where(mine, small["d_mod"].reshape(1, -1), 0.0)
    spacked, soffs = _pack([small[k] for k in small_names] + [mod_slots])
    small_state, small_started = _allreduce_small_begin(spacked, place, rec_sent)

    fulls = {}
    for group in ("mlp1", "conf", "mlp0", "rec"):
        for t, f in zip(use_order[group], _reduce_end(group, sent[group], place, small_started)):
            fulls[t] = f
    whole = _share_halves("share_grads", [fulls[t] for t in range(8)])
    g_big = dict(rec_w_in=whole[0].reshape(rec_w_in.shape), rec_w_out=whole[1].reshape(rec_w_out.shape),
                 conf_w_pw1=whole[2].reshape(conf_w_pw1.shape), conf_w_pw2=whole[3].reshape(conf_w_pw2.shape),
                 mlp_w_in=jnp.stack([whole[4].reshape(D, D), whole[5].reshape(D, D)]),
                 mlp_w_out=jnp.stack([whole[6].reshape(D, D), whole[7].reshape(D, D)]))
    delta, new_m, new_v = {}, {}, {}

    def adamw_of(k, g, dep=None):
        cols = w[k].shape[-1]
        d_, m_, v_ = _adamw(f"adamw_{k}", w[k].reshape(-1, cols), g.reshape(-1, cols),
                            m[k].reshape(-1, cols), v[k].reshape(-1, cols), dep)
        delta[k], new_m[k], new_v[k] = (a.reshape(w[k].shape) for a in (d_, m_, v_))

    for k in _BIG:
        adamw_of(k, g_big[k])

    unpacked = _unpack(_allreduce_small_end(small_state, new_v[_BIG[-1]]), soffs)
    ssum = dict(zip(small_names, unpacked[:-1]))
    loss = ssum["loss"].reshape(())
    dmod_rows = unpacked[-1].reshape(8, 2, 6 * D).transpose(1, 0, 2)

    d_cmod_full =jnp.concatenate([ssum["d_cmod"].reshape(1, 2 * D), jnp.zeros((1, 4 * D), F32)], axis=1)
    dm16 = jnp.concatenate([dmod_rows, jnp.stack([d_cmod_full, jnp.zeros((1, 6 * D), F32)]),
                            jnp.zeros((2, 7, 6 * D), F32)], axis=1)
    dm16_shard = lax.dynamic_slice_in_dim(dm16, chip * ns, ns, axis=2)
    g_w_ada, ds_part = _ada_bwd(c16, dm16_shard, w_ada)
    ds_state, ds_sent = _send_start("dsilu_start", [jnp.broadcast_to(ds_part[8:16], (4, 8, D))], _to_chips,
                                    [(3, 8, D)], place)
    adamw_of("w_ada", g_w_ada, ds_sent)
    (ds_own,), (ds_landed,) = _send_wait("dsilu_wait", *ds_state, _to_chips, new_v["w_ada"])
    ds_flip = jnp.concatenate([ds_own[:1], ds_landed[1:2], ds_landed[0:1], ds_landed[2:3]], axis=0)
    g_c_ctx = _cctx_grad(jnp.take(ds_flip, jnp.arange(4) ^ chip, axis=0), c_ctx).reshape(D)
    g_b_ada = ssum["d_mod"] + jnp.stack([d_cmod_full[0], jnp.zeros((6 * D,), F32)])

    def shard_of(a, axis):
        n = a.shape[axis] // 4
        return lax.dynamic_slice_in_dim(a, chip * n, n, axis=axis)

    grads = dict(
        c_ctx=g_c_ctx, w_ada=g_w_ada, b_ada=g_b_ada,
        norm_g=shard_of(ssum["norm_g"].reshape(2, 2, D), 2),
        rec_w_in=g_big["rec_w_in"], rec_conv_w=shard_of(ssum["rec_conv_w"].reshape(1, REC_KW, R), 2),
        rec_conv_b=ssum["rec_conv_b"].reshape(1, R), rec_lambda=shard_of(ssum["rec_lambda"].reshape(1, 2, R), 2),
        rec_w_a=ssum["rec_w_a"].reshape(rec_w_a.shape), rec_b_a=ssum["rec_b_a"].reshape(rec_b_a.shape),
        rec_w_x=ssum["rec_w_x"].reshape(rec_w_x.shape), rec_b_x=ssum["rec_b_x"].reshape(rec_b_x.shape),
        rec_w_out=g_big["rec_w_out"], conf_w_pw1=g_big["conf_w_pw1"],
        conf_b_pw1=shard_of(ssum["conf_b_pw1"].reshape(1, 2 * D), 1),
        conf_conv_w=shard_of(ssum["conf_conv_w"].reshape(1, CONF_KW, D), 2),
        conf_conv_b=shard_of(ssum["conf_conv_b"].reshape(1, D), 1),
        conf_ln_g=shard_of(ssum["conf_ln_g"].reshape(1, D), 1), conf_ln_b=shard_of(ssum["conf_ln_b"].reshape(1, D), 1),
        conf_w_pw2=g_big["conf_w_pw2"], conf_b_pw2=shard_of(ssum["conf_b_pw2"].reshape(1, D), 1),
        mlp_w_in=g_big["mlp_w_in"], mlp_w_out=g_big["mlp_w_out"], final_g=ssum["final_g"].reshape(D))

    rest =[k for k in names if k not in ("w_ada",) + _BIG]
    d_, m_, v_ = _adamw_many("adamw_small", [w[k] for k in rest], [grads[k] for k in rest],
                             [m[k] for k in rest], [v[k] for k in rest])
    for k, dd, mm, vv in zip(rest, d_, m_, v_):
        delta[k], new_m[k], new_v[k] = dd, mm, vv

    return (loss, grad_x[None], *[grads[k] for k in names], *[delta[k] for k in names],
            *[new_m[k] for k in names], *[new_v[k] for k in names])
```

```python
import functools
import math

import jax
import jax.numpy as jnp
from jax import lax
from jax.experimental import pallas as pl
from jax.experimental.pallas import tpu as pltpu

F32 = jnp.float32
BF16 = jnp.bfloat16

D = 1024
T = 2048
TC = 256
TA = T + TC
R = 1280
RH = R // 2
NQ = 4 * RH
FF = 4096
N_BLK = 16
BLK = R // N_BLK
GRID_W = 64
EPS = 1e-6
RG_C = 8.0
CONF_KW = 31
REC_KW = 4
LANE = 128
ROW_TILE = 256
HALO = 16
RG_TILE = 128
PACK_ROWS = 512
MM_TILE = 1024
REC_TILE = TA // 2
CW_REC = 640
CW_CONF = 512
V7X_VMEM_BYTES = 64 * 1024 * 1024
VMEM_LIMIT = V7X_VMEM_BYTES - 8 * 1024 * 1024

ADAM_LR = 0.001
ADAM_B1 = 0.9
ADAM_B2 = 0.999
ADAM_EPS = 1e-08
ADAM_WD = 0.01
ADAM_STEP = 10

MESH = pl.DeviceIdType.MESH
ANY = pl.BlockSpec(memory_space=pl.ANY)


def _sds(shape, dtype):
    return jax.ShapeDtypeStruct(tuple(shape), dtype)


def _pcall(body, **kw):
    return pl.pallas_call(body, **kw)


def _cparams():
    return pltpu.CompilerParams(vmem_limit_bytes=VMEM_LIMIT)


def _full_spec(arr):
    nd = arr.ndim
    return pl.BlockSpec(arr.shape, lambda *ids, _n=nd: (0,) * _n)


def _sum0(v):
    return jnp.sum(v, axis=0, keepdims=True)


def _tiled(name, fn, grid, ins, vecs, outs, vec_outs=(), vec_refs=False):
    n_in, n_vec, n_out = len(ins), len(vecs), len(outs)
    n_grid = len(grid)

    def kern(*refs):
        ids = [pl.program_id(a) for a in range(n_grid)]
        tin = [r[...] for r in refs[:n_in]]
        vin = list(refs[n_in:n_in + n_vec]) if vec_refs else [r[...] for r in refs[n_in:n_in + n_vec]]
        o_refs = refs[n_in + n_vec:n_in + n_vec + n_out]
        a_refs = refs[n_in + n_vec + n_out:]
        tout, incs = fn(ids, tin, vin)
        for r, v in zip(o_refs, tout):
            r[...] = v.astype(r.dtype)
        if a_refs:
            first = functools.reduce(jnp.logical_and, [i == 0 for i in ids])

            @pl.when(first)
            def _():
                for r in a_refs:
                    r[...] = jnp.zeros_like(r)

            for r, v in zip(a_refs, incs):
                r[...] += v

    out_shape = [o for o, _ in outs] + [_sds(s, F32) for s in vec_outs]
    out_specs = [s for _, s in outs] + [
        pl.BlockSpec(tuple(s), lambda *ids, _n=len(s): (0,) * _n) for s in vec_outs]
    res = _pcall(
        kern, name=name, grid=tuple(grid),
        in_specs=[s for _, s in ins] + [_full_spec(v) for v in vecs],
        out_specs=out_specs, out_shape=out_shape, compiler_params=_cparams(),
    )(*[a for a, _ in ins], *vecs)
    return list(res)


def _rows(arr, ncols=None, tm=ROW_TILE, off=0, col=0, clamp_lo=False):
    ncols = arr.shape[1] if ncols is None else ncols
    if clamp_lo:
        return arr, pl.BlockSpec((tm, ncols), lambda i: (jnp.maximum(i + off, 0), col))
    return arr, pl.BlockSpec((tm, ncols), lambda i: (i + off, col))


def _orow(nrows, ncols, dtype, tm=ROW_TILE, off=0, clamp_lo=False):
    if clamp_lo:
        return _sds((nrows, ncols), dtype), pl.BlockSpec((tm, ncols), lambda i: (jnp.maximum(i + off, 0), 0))
    return _sds((nrows, ncols), dtype), pl.BlockSpec((tm, ncols), lambda i: (i + off, 0))


_NN = (((1,), (0,)), ((), ()))
_TN = (((0,), (0,)), ((), ()))
_NT = (((1,), (1,)), ((), ()))


def _mm(name, a, b, dims, grid, a_spec, b_spec, out, acc_shape, extra=(), a_pre=None, epi=None):
    n_k = grid[2]
    n_ex = len(extra)

    def kern(a_ref, b_ref, *rest):
        ex = rest[:n_ex]
        o_refs = rest[n_ex:n_ex + len(out)]
        k = pl.program_id(2)
        av = a_ref[...]
        if a_pre is not None:
            av = a_pre(av)
        part = lax.dot_general(av.astype(BF16), b_ref[...].astype(BF16), dims, preferred_element_type=F32)

        def finish(total):
            vals = [total] if epi is None else epi(total, [e[...] for e in ex])
            for r, v in zip(o_refs, vals):
                r[...] = v.astype(r.dtype)

        if n_k == 1:
            finish(part)
        else:
            acc = rest[-1]

            @pl.when(k == 0)
            def _():
                acc[...] = part

            @pl.when(jnp.logical_and(k > 0, k < n_k - 1))
            def _():
                acc[...] += part

            @pl.when(k == n_k - 1)
            def _():
                finish(acc[...] + part)

    res = _pcall(
        kern, name=name, grid=tuple(grid),
        in_specs=[a_spec, b_spec] + [s for _, s in extra],
        out_specs=[s for _, s in out], out_shape=[o for o, _ in out],
        scratch_shapes=[] if n_k == 1 else [pltpu.VMEM(tuple(acc_shape), F32)], compiler_params=_cparams(),
    )(a, b, *[e for e, _ in extra])
    return list(res)


def _rms(x):
    r = lax.rsqrt(jnp.mean(x * x, axis=-1, keepdims=True) + EPS)
    return x * r, r


def _norm_mod(x, g, sc, sh):
    n, _ = _rms(x)
    return (n * g) * (1.0 + sc) + sh


def _norm_mod_bwd(dh, x, g, sc):
    n, r = _rms(x)
    d_sh = _sum0(dh)
    d_sc = _sum0(dh * (n * g))
    d_g = _sum0(dh * (1.0 + sc) * n)
    dn = dh * (g * (1.0 + sc))
    dx = r * (dn - n * jnp.mean(dn * n, axis=-1, keepdims=True))
    return dx, d_sh, d_sc, d_g


_GELU_K = math.sqrt(2.0 / math.pi)


def _gelu(x):
    t = jnp.tanh(_GELU_K * (x + 0.044715 * x * x * x))
    return 0.5 * x * (1.0 + t), t


def _gelu_grad(x, t):
    return 0.5 * (1.0 + t) + 0.5 * x * (1.0 - t * t) * (_GELU_K * (1.0 + 3.0 * 0.044715 * x * x))


def _sigmoid(x):
    return 0.5 * jnp.tanh(0.5 * x) + 0.5


def _expm1(x):
    p = jnp.full_like(x, 1.0 / 5040.0)
    for c in (1.0 / 720.0, 1.0 / 120.0, 1.0 / 24.0, 1.0 / 6.0, 0.5, 1.0):
        p = p * x + c
    return jnp.where(jnp.abs(x) < 0.3, x * p, jnp.exp(x) - 1.0)


def _softplus_neg(lam):
    return jnp.log1p(jnp.exp(-jnp.abs(lam))) + jnp.maximum(-lam, 0.0)


def _layernorm_parts(x):
    mu = jnp.mean(x, axis=-1, keepdims=True)
    xc = x - mu
    rstd = lax.rsqrt(jnp.mean(xc * xc, axis=-1, keepdims=True) + EPS)
    return xc * rstd, rstd


def _rg_gates(u, wbd, gbias, lam):
    sp = _softplus_neg(lam)
    parts = {}
    for h in range(2):
        uh = u[:, h * RH:(h + 1) * RH]
        g = jnp.dot(uh.astype(BF16), wbd[h], preferred_element_type=F32) + gbias[:, h * NQ:(h + 1) * NQ]
        for d in range(2):
            r = _sigmoid(g[:, (2 * d) * RH:(2 * d + 1) * RH])
            i = _sigmoid(g[:, (2 * d + 1) * RH:(2 * d + 2) * RH])
            sph = sp[d:d + 1, h * RH:(h + 1) * RH]
            la = (-RG_C) * r * sph
            e2 = _expm1(2.0 * la)
            inv_mult = jnp.where(e2 < 0.0, lax.rsqrt(-e2), 0.0)
            parts[(d, h)] = dict(r=r, i=i, la=la, a=jnp.exp(la), e2=e2, mult=-e2 * inv_mult, inv_mult=inv_mult,
                                 uh=uh, sp=sph)
    return parts


def _rg_fwd_fn(ids, tin, vin):
    (u,) = tin
    wbd = vin[0]
    parts = _rg_gates(u, wbd, vin[1][...], vin[2][...])
    outs = []
    for d in range(2):
        a = jnp.concatenate([parts[(d, h)]["a"] for h in range(2)], axis=1)
        b = jnp.concatenate([parts[(d, h)]["mult"] * parts[(d, h)]["i"] * parts[(d, h)]["uh"]
                             for h in range(2)], axis=1)
        outs += [a, b]
    return outs, []


def _rg_bwd_fn(ids, tin, vin):
    u, da_f, db_f, da_r, db_r = tin
    wbd, lam = vin[0], vin[2][...]
    parts = _rg_gates(u, wbd, vin[1][...], lam)
    dab = ((da_f, db_f), (da_r, db_r))
    dsig_lam = -1.0 / (1.0 + jnp.exp(lam))
    du_halves, dpre_halves, dlam = [], [], [[None, None], [None, None]]
    for h in range(2):
        du = jnp.zeros_like(parts[(0, h)]["uh"])
        dpre = []
        for d in range(2):
            p = parts[(d, h)]
            da = dab[d][0][:, h * RH:(h + 1) * RH]
            db = dab[d][1][:, h * RH:(h + 1) * RH]
            d_mult = db * p["i"] * p["uh"]
            d_i = db * p["mult"] * p["uh"]
            du = du + db * p["mult"] * p["i"]
            d_la = da * p["a"] - d_mult * (p["e2"] + 1.0) * p["inv_mult"]
            d_r = d_la * ((-RG_C) * p["sp"])
            dlam[d][h] = _sum0(d_la * ((-RG_C) * p["r"])) * dsig_lam[d:d + 1, h * RH:(h + 1) * RH]
            dpre += [d_r * p["r"] * (1.0 - p["r"]), d_i * p["i"] * (1.0 - p["i"])]
        dpre = jnp.concatenate(dpre, axis=1)
        du = du + lax.dot_general(dpre.astype(BF16), wbd[h], _NT, preferred_element_type=F32)
        du_halves.append(du)
        dpre_halves.append(dpre)
    dpre_all = jnp.concatenate(dpre_halves, axis=1)
    dlam_row = jnp.concatenate([dlam[0][0], dlam[0][1], dlam[1][0], dlam[1][1]], axis=1)
    return [dpre_all, jnp.concatenate(du_halves, axis=1)], [_sum0(dpre_all), dlam_row]


def _tile_flags(i, n_tiles, seq_starts):
    starts_here = functools.reduce(jnp.logical_or, [i == s for s in seq_starts])
    ends_here = functools.reduce(jnp.logical_or, [i + 1 == s for s in seq_starts] + [i + 1 == n_tiles])
    return jnp.logical_not(starts_here), jnp.logical_not(ends_here)


def _halo_specs(col0, cw):
    hb = ROW_TILE // HALO
    prev = pl.BlockSpec((HALO, cw), lambda i, c: (jnp.maximum(i * hb - 1, 0), col0 + c))
    cur = pl.BlockSpec((ROW_TILE, cw), lambda i, c: (i, col0 + c))
    return prev, cur, hb


def _window(prev_ref, cur_ref, next_ref, has_prev, has_next):
    prev = jnp.where(has_prev, prev_ref[...], 0.0)
    nxt = jnp.where(has_next, next_ref[...], 0.0)
    return jnp.concatenate([prev, cur_ref[...], nxt], axis=0)


def _tap_reader(win):
    sub = 8
    n = win.shape[0]
    shifted = {0: win}

    def tap(off):
        s = off % sub
        if s not in shifted:
            shifted[s] = pltpu.roll(win, n - s, axis=0)
        return shifted[s][off - s:off - s + ROW_TILE, :]

    return tap


def _dwconv(name, x, col0, w, bias, pad_left, seq_starts, n_ch, cw=256, flip=False, into=None):
    n_rows = x.shape[0]
    n_tiles = n_rows // ROW_TILE
    n_taps = w.shape[0]
    prev_spec, cur_spec, hb = _halo_specs(col0, cw)
    last_hb = n_rows // HALO - 1
    next_spec = pl.BlockSpec((HALO, cw), lambda i, c: (jnp.minimum((i + 1) * hb, last_hb), col0 + c))
    dest, out_col0 = (None, 0) if into is None else into

    def kern(prev_ref, cur_ref, next_ref, w_ref, b_ref, *rest):
        o_ref = rest[-1]
        has_prev, has_next = _tile_flags(pl.program_id(0), n_tiles, seq_starts)
        win = _window(prev_ref, cur_ref, next_ref, has_prev, has_next)
        tap = _tap_reader(win)
        wv = w_ref[...]
        acc = jnp.zeros((ROW_TILE, cw), F32) + b_ref[...]
        for k in range(n_taps):
            kw = n_taps - 1 - k if flip else k
            acc = acc + wv[kw:kw + 1, :] * tap(HALO + k - pad_left)
        o_ref[...] = acc.astype(o_ref.dtype)

    return _pcall(
        kern, name=name, grid=(n_tiles, n_ch // cw),
        in_specs=[prev_spec, cur_spec, next_spec,
                  pl.BlockSpec((n_taps, cw), lambda i, c: (0, c)), pl.BlockSpec((1, cw), lambda i, c: (0, c))]
        + ([] if dest is None else [ANY]),
        out_specs=pl.BlockSpec((ROW_TILE, cw), lambda i, c: (i, out_col0 + c)),
        out_shape=_sds((n_rows, n_ch), F32) if dest is None else _sds(dest.shape, dest.dtype),
        input_output_aliases={} if dest is None else {5: 0}, compiler_params=_cparams(),
    )(x, x, x, w, bias, *([] if dest is None else [dest]))


def _dwconv_wgrad(name, dy, x, col0, n_taps, pad_left, seq_starts, n_ch, cw=256, dep=None):
    deps = [] if dep is None else [dep]
    n_rows = dy.shape[0]
    n_tiles = n_rows // ROW_TILE
    n_out = -(-(n_taps + 1) // 8) * 8
    prev_spec, cur_spec, hb = _halo_specs(col0, cw)
    last_hb = n_rows // HALO - 1
    next_spec = pl.BlockSpec((HALO, cw), lambda c, i: (jnp.minimum((i + 1) * hb, last_hb), col0 + c))
    prev_spec = pl.BlockSpec((HALO, cw), lambda c, i: (jnp.maximum(i * hb - 1, 0), col0 + c))
    cur_spec = pl.BlockSpec((ROW_TILE, cw), lambda c, i: (i, col0 + c))

    def kern(dy_ref, prev_ref, cur_ref, next_ref, *rest):
        o_ref = rest[-1]
        i = pl.program_id(1)
        has_prev, has_next = _tile_flags(i, n_tiles, seq_starts)
        win = _window(prev_ref, cur_ref, next_ref, has_prev, has_next)
        dyv = dy_ref[...]
        tap = _tap_reader(win)
        rid = lax.broadcasted_iota(jnp.int32, (n_out, cw), 0)
        inc = jnp.where(rid == n_taps, _sum0(dyv), 0.0)
        for k in range(n_taps):
            inc = inc + jnp.where(rid == k, _sum0(dyv * tap(HALO + k - pad_left)), 0.0)

        @pl.when(i == 0)
        def _():
            o_ref[...] = jnp.zeros_like(o_ref)

        o_ref[...] += inc

    return _pcall(
        kern, name=name, grid=(n_ch // cw, n_tiles),
        in_specs=[pl.BlockSpec((ROW_TILE, cw), lambda c, i: (i, c)), prev_spec, cur_spec, next_spec]
        + [pl.BlockSpec(d.shape, lambda c, i: (0, 0)) for d in deps],
        out_specs=pl.BlockSpec((n_out, cw), lambda c, i: (0, c)),
        out_shape=_sds((n_out, n_ch), F32), compiler_params=_cparams(),
    )(dy, x, x, x, *deps)


N_SCAN = TA // ROW_TILE


def _rev_block(j):
    return jnp.where(j == 0, 0, N_SCAN - j)


def _scan_fwd(a_f, b_f, a_r, b_r):
    fwd_spec = pl.BlockSpec((ROW_TILE, R), lambda i: (i, 0))
    rev_spec = pl.BlockSpec((ROW_TILE, R), lambda i: (_rev_block(i), 0))
    hin_spec = pl.BlockSpec((None, 1, R), lambda i: (i, 0, 0))

    def kern(af, bf, ar, br, yf, yr, hin_f, hin_r, hf_s, hr_s):
        @pl.when(pl.program_id(0) == 0)
        def _():
            hf_s[...] = jnp.zeros_like(hf_s)
            hr_s[...] = jnp.zeros_like(hr_s)

        hin_f[...] = hf_s[...]
        hin_r[...] = hr_s[...]

        def step(s8, carry):
            hf, hr = carry
            t0 = pl.multiple_of(s8 * 8, 8)
            for q in range(8):
                tf = t0 + q
                hf = af[pl.ds(tf, 1), :] * hf + bf[pl.ds(tf, 1), :]
                yf[pl.ds(tf, 1), :] = hf
                tr = ROW_TILE - 1 - tf
                hr = ar[pl.ds(tr, 1), :] * hr + br[pl.ds(tr, 1), :]
                yr[pl.ds(tr, 1), :] = hr
            return hf, hr

        hf, hr = lax.fori_loop(0, ROW_TILE // 8, step, (hf_s[...], hr_s[...]))
        hf_s[...] = hf
        hr_s[...] = hr

    return _pcall(
        kern, name="scan_fwd", grid=(N_SCAN,),
        in_specs=[fwd_spec, fwd_spec, rev_spec, rev_spec],
        out_specs=[fwd_spec, rev_spec, hin_spec, hin_spec],
        out_shape=[_sds((TA, R), F32), _sds((TA, R), F32), _sds((N_SCAN, 1, R), F32), _sds((N_SCAN, 1, R), F32)],
        scratch_shapes=[pltpu.VMEM((1, R), F32), pltpu.VMEM((1, R), F32)], compiler_params=_cparams(),
    )(a_f, b_f, a_r, b_r)


def _scan_bwd(dy, a_f, y_f, hin_f, a_r, y_r, hin_r):
    fwd_spec = pl.BlockSpec((ROW_TILE, R), lambda i: (N_SCAN - 1 - i, 0))
    rev_spec = pl.BlockSpec((ROW_TILE, R), lambda i: (_rev_block(N_SCAN - 1 - i), 0))
    hin_spec = pl.BlockSpec((None, 1, R), lambda i: (N_SCAN - 1 - i, 0, 0))
    last = ROW_TILE - 1

    def kern(dyf, af, yf, hf0, dyr, ar, yr, hr0, daf, dbf, dar, dbr, gf_s, anf_s, gr_s, anr_s):
        @pl.when(pl.program_id(0) == 0)
        def _():
            for r in (gf_s, anf_s, gr_s, anr_s):
                r[...] = jnp.zeros_like(r)

        def one(dy_ref, a_ref, y_ref, da_ref, db_ref, g, an, p, pprev):
            gnew = dy_ref[pl.ds(p, 1), :] + an * g
            db_ref[pl.ds(p, 1), :] = gnew
            da_ref[pl.ds(p, 1), :] = gnew * y_ref[pl.ds(pprev, 1), :]
            return gnew, a_ref[pl.ds(p, 1), :]

        def step(s8, carry):
            gf, anf, gr, anr = carry
            base = s8 * 8
            for q in range(8):
                s = last - (base + q)
                gf, anf = one(dyf, af, yf, daf, dbf, gf, anf, s, s - 1)
                gr, anr = one(dyr, ar, yr, dar, dbr, gr, anr, last - s, last - s + 1)
            return gf, anf, gr, anr

        carry = (gf_s[...], anf_s[...], gr_s[...], anr_s[...])
        carry = lax.fori_loop(0, ROW_TILE // 8 - 1, step, carry)
        gf, anf, gr, anr = carry
        for s in range(7, 0, -1):
            gf, anf = one(dyf, af, yf, daf, dbf, gf, anf, s, s - 1)
            gr, anr = one(dyr, ar, yr, dar, dbr, gr, anr, last - s, last - s + 1)
        gf0 = dyf[0:1, :] + anf * gf
        dbf[0:1, :] = gf0
        daf[0:1, :] = gf0 * hf0[...]
        gr0 = dyr[last:last + 1, :] + anr * gr
        dbr[last:last + 1, :] = gr0
        dar[last:last + 1, :] = gr0 * hr0[...]
        gf_s[...] = gf0
        anf_s[...] = af[0:1, :]
        gr_s[...] = gr0
        anr_s[...] = ar[last:last + 1, :]

    return _pcall(
        kern, name="scan_bwd", grid=(N_SCAN,),
        in_specs=[fwd_spec, fwd_spec, fwd_spec, hin_spec, rev_spec, rev_spec, rev_spec, hin_spec],
        out_specs=[fwd_spec, fwd_spec, rev_spec, rev_spec],
        out_shape=[_sds((TA, R), F32)] * 4,
        scratch_shapes=[pltpu.VMEM((1, R), F32)] * 4, compiler_params=_cparams(),
    )(dy, a_f, y_f, hin_f, dy, a_r, y_r, hin_r)


def _me():
    return lax.axis_index("x"), lax.axis_index("y"), lax.axis_index("c")


def _other_chips(mx, my):
    return [(1 - mx, my), (mx, 1 - my), (1 - mx, 1 - my)]


def _rcopy(src, dst, ssem, rsem, dev):
    return pltpu.make_async_remote_copy(src_ref=src, dst_ref=dst, send_sem=ssem, recv_sem=rsem,
                                        device_id=dev, device_id_type=MESH)


def _peers7(mx, my, mc):
    peers = []
    for k in range(1, 8):
        peers.append((1 - mx if (k >> 2) & 1 else mx, 1 - my if (k >> 1) & 1 else my, 1 - mc if k & 1 else mc))
    return peers


def _share_halves(name, fulls):
    n = len(fulls)

    def kern(*refs):
        o = refs[n:2 * n]
        ss, rs = refs[2 * n:]
        mx, my, mc = _me()
        sib = (mx, my, 1 - mc)
        sends = []
        for t in range(n):
            cp = _rcopy(o[t].at[mc], o[t].at[mc], ss.at[t], rs.at[t], sib)
            cp.start()
            sends.append(cp)
        for t in range(n):
            _rcopy(o[t].at[1 - mc], o[t].at[1 - mc], ss.at[t], rs.at[t], sib).wait_recv()
        for cp in sends:
            cp.wait_send()

    dma = pltpu.SemaphoreType.DMA
    return _pcall(
        kern, name=name, in_specs=[ANY] * n, out_specs=[ANY] * n,
        out_shape=[_sds(f.shape, f.dtype) for f in fulls], input_output_aliases={t: t for t in range(n)},
        scratch_shapes=[dma((n,)), dma((n,))],
    )(*fulls)


def _share_start(name, fulls, after):
    n = len(fulls)

    def kern(*refs):
        o = refs[n + 1:2 * n + 1]
        ssem, rsem, token = refs[2 * n + 1:]
        mx, my, mc = _me()
        for t in range(n):
            _rcopy(o[t].at[mc], o[t].at[mc], ssem.at[t], rsem.at[t], (mx, my, 1 - mc)).start()
        token[...] = jnp.zeros_like(token)

    dma = pltpu.SemaphoreType.DMA
    res = _pcall(
        kern, name=name, in_specs=[ANY] * (n + 1),
        out_specs=[ANY] * n + [SEM, SEM, pl.BlockSpec(memory_space=pltpu.VMEM)],
        out_shape=[_sds(f.shape, f.dtype) for f in fulls] + [dma((n,)), dma((n,)), _sds((8, LANE), F32)],
        input_output_aliases={t: t for t in range(n)},
        compiler_params=pltpu.CompilerParams(has_side_effects=_DATAFLOW),
    )(*fulls, after)
    return (list(res[:n]), res[n], res[n + 1]), res[n + 2]


def _share_wait(name, fulls, ssem, rsem, after):
    n = len(fulls)

    def kern(*refs):
        o = refs[:n]
        ssem_ref, rsem_ref = refs[n], refs[n + 1]
        mx, my, mc = _me()
        sib = (mx, my, 1 - mc)
        for t in range(n):
            _rcopy(o[t].at[1 - mc], o[t].at[1 - mc], ssem_ref.at[t], rsem_ref.at[t], sib).wait_recv()
            _rcopy(o[t].at[mc], o[t].at[mc], ssem_ref.at[t], rsem_ref.at[t], sib).wait_send()

    return list(_pcall(
        kern, name=name, in_specs=[ANY] * n + [SEM, SEM, ANY], out_specs=[ANY] * n,
        out_shape=[_sds(f.shape, f.dtype) for f in fulls], input_output_aliases={t: t for t in range(n)},
        compiler_params=pltpu.CompilerParams(has_side_effects=_DATAFLOW),
    )(*fulls, ssem, rsem, after))


def _tiled_sp(name, fn, grid, sp, ins, outs):
    n_in = len(ins)

    def kern(sp_ref, *refs):
        tout = fn([r[...] for r in refs[:n_in]])
        for r, v in zip(refs[n_in:], tout):
            r[...] = v.astype(r.dtype)

    gs = pltpu.PrefetchScalarGridSpec(num_scalar_prefetch=1, grid=tuple(grid),
                                      in_specs=[s for _, s in ins], out_specs=[s for _, s in outs])
    res = _pcall(kern, name=name, grid_spec=gs, out_shape=[o for o, _ in outs], compiler_params=_cparams(),
                 )(sp, *[a for a, _ in ins])
    return list(res)


def _row_tile(rows, cols, itemsize=4, budget=2 * 1024 * 1024):
    tr = rows
    while tr * cols * itemsize > budget and tr % 32 == 0:
        tr //= 2
    return tr


def _place_big(shards, place, dep=None):
    slots = []
    for tag, s, layer in shards:
        rr, cc = s.shape[2], s.shape[3]
        tr = _row_tile(rr, cc)
        (slot,) = _tiled_sp(
            f"place_{tag}", lambda tin: [tin[0]], (2, rr // tr), place,
            [(s, pl.BlockSpec((None, None, tr, cc), lambda h, i, sp, layer=layer: (layer, h, i, 0)))]
            + [(d, pl.BlockSpec(d.shape, lambda h, i, sp: (0, 0))) for d in _behind(dep)],
            [(_sds((4, 2, rr, cc), BF16), pl.BlockSpec((None, None, tr, cc), lambda h, i, sp: (sp[0], h, i, 0)))])
        slots.append(slot)
    return slots


def _allreduce_small_begin(vec, place, after):
    hr = vec.shape[0] // 2
    tr = _row_tile(hr, LANE)
    blk = (None, None, tr, LANE)
    (pair,) = _tiled_sp(
        "small_place", lambda tin: [tin[0]], (2, hr // tr), place,
        [(vec.reshape(2, hr, LANE), pl.BlockSpec((None, tr, LANE), lambda h, i, sp: (h, i, 0)))],
        [(_sds((2, 2, hr, LANE), F32), pl.BlockSpec(blk, lambda h, i, sp: (sp[1], h, i, 0)))])
    (pair,) = _share_halves("small_share", [pair])
    (slot,) = _tiled_sp(
        "small_pair_add", lambda tin: [tin[0] + tin[1]], (2, hr // tr), place,
        [(pair, pl.BlockSpec(blk, lambda h, i, sp: (0, h, i, 0))),
         (pair, pl.BlockSpec(blk, lambda h, i, sp: (1, h, i, 0)))],
        [(_sds((4, 2, hr, LANE), F32), pl.BlockSpec(blk, lambda h, i, sp: (sp[0], h, i, 0)))])
    fly, sems, token = _gather_start("small_start", [slot], ((0,),), after)
    return (fly, sems), token


def _allreduce_small_end(state, after, then=None):
    fly, sems = state
    (chips,) = _swap_halves("small_swap", _gather_wait("small_wait", fly, *sems, after))
    dep = None if then is None else then(chips)
    hr = chips.shape[2]
    tr = _row_tile(hr, LANE)
    blk = (None, None, tr, LANE)
    (total,) = _tiled(
        "small_chip_sum", lambda ids, tin, vin: ([((tin[0] + tin[1]) + tin[2]) + tin[3]], []), (2, hr // tr),
        [(chips, pl.BlockSpec(blk, lambda h, i, _j=j: (_j, h, i, 0))) for j in range(4)], _behind(dep),
        [(_sds((2, hr, LANE), F32), pl.BlockSpec((None, tr, LANE), lambda h, i: (h, i, 0)))])
    return total.reshape(2 * hr, LANE)


SEM =pl.BlockSpec(memory_space=pltpu.SEMAPHORE)
_DATAFLOW = pltpu.SideEffectType.DATAFLOW_SIDE_EFFECTING


def _gather_start(name, slots, groups, after):
    n = len(slots)

    def kern(*refs):
        o = refs[n + 1:2 * n + 1]
        sems, token = refs[2 * n + 1:-1], refs[-1]
        mx, my, mc = _me()
        j0 = 2 * mx + my
        for gi, grp in enumerate(groups):
            for k, t in enumerate(grp):
                for q, (qx, qy) in enumerate(_other_chips(mx, my)):
                    _rcopy(o[t].at[j0, mc], o[t].at[j0, mc], sems[2 * gi].at[3 * k + q],
                           sems[2 * gi + 1].at[3 * k + q], (qx, qy, mc)).start()
        token[...] = jnp.zeros_like(token)

    sem_shapes = []
    for grp in groups:
        sem_shapes += [pltpu.SemaphoreType.DMA((3 * len(grp),))] * 2
    res = _pcall(
        kern, name=name, in_specs=[ANY] * (n + 1),
        out_specs=[ANY] * n + [SEM] * len(sem_shapes) + [pl.BlockSpec(memory_space=pltpu.VMEM)],
        out_shape=[_sds(w.shape, w.dtype) for w in slots] + sem_shapes + [_sds((8, LANE), F32)],
        input_output_aliases={t: t for t in range(n)},
        compiler_params=pltpu.CompilerParams(has_side_effects=_DATAFLOW),
    )(*slots, after)
    return list(res[:n]), list(res[n:-1]), res[-1]


def _gather_wait(name, bufs, ssem, rsem, after):
    n = len(bufs)

    def kern(*refs):
        b = refs[:n]
        ssem_ref, rsem_ref = refs[n], refs[n + 1]
        mx, my, mc = _me()
        j0 = 2 * mx + my
        for k in range(n):
            for q, (qx, qy) in enumerate(_other_chips(mx, my)):
                jq = 2 * qx + qy
                _rcopy(b[k].at[jq, mc], b[k].at[jq, mc], ssem_ref.at[3 * k + q], rsem_ref.at[3 * k + q],
                       (qx, qy, mc)).wait_recv()
                _rcopy(b[k].at[j0, mc], b[k].at[j0, mc], ssem_ref.at[3 * k + q], rsem_ref.at[3 * k + q],
                       (qx, qy, mc)).wait_send()

    return list(_pcall(
        kern, name=name, in_specs=[ANY] * n + [SEM, SEM, ANY], out_specs=[ANY] * n,
        out_shape=[_sds(w.shape, w.dtype) for w in bufs], input_output_aliases={k: k for k in range(n)},
        compiler_params=pltpu.CompilerParams(has_side_effects=_DATAFLOW),
    )(*bufs, ssem, rsem, after))


def _swap_halves(name, bufs):
    n = len(bufs)

    def kern(*refs):
        o = refs[n:2 * n]
        ss, rs = refs[2 * n:]
        mx, my, mc = _me()
        sib = (mx, my, 1 - mc)
        sends = []
        for k in range(n):
            for q, (qx, qy) in enumerate(_other_chips(mx, my)):
                jq = 2 * qx + qy
                cp = _rcopy(o[k].at[jq, mc], o[k].at[jq, mc], ss.at[3 * k + q], rs.at[3 * k + q], sib)
                cp.start()
                sends.append(cp)
        for k in range(n):
            for q, (qx, qy) in enumerate(_other_chips(mx, my)):
                jq = 2 * qx + qy
                _rcopy(o[k].at[jq, 1 - mc], o[k].at[jq, 1 - mc], ss.at[3 * k + q], rs.at[3 * k + q], sib).wait_recv()
        for cp in sends:
            cp.wait_send()

    dma = pltpu.SemaphoreType.DMA
    return list(_pcall(
        kern, name=name, in_specs=[ANY] * n, out_specs=[ANY] * n,
        out_shape=[_sds(w.shape, w.dtype) for w in bufs], input_output_aliases={k: k for k in range(n)},
        scratch_shapes=[dma((3 * n,)), dma((3 * n,))],
    )(*bufs))


def _swap_start(name, bufs, after):
    n = len(bufs)

    def kern(*refs):
        o = refs[n + 1:2 * n + 1]
        ssem, rsem, token = refs[2 * n + 1:]
        mx, my, mc = _me()
        for k in range(n):
            for q, (qx, qy) in enumerate(_other_chips(mx, my)):
                jq = 2 * qx + qy
                _rcopy(o[k].at[jq, mc], o[k].at[jq, mc], ssem.at[3 * k + q], rsem.at[3 * k + q], (mx, my, 1 - mc)).start()
        token[...] = jnp.zeros_like(token)

    dma = pltpu.SemaphoreType.DMA
    res = _pcall(
        kern, name=name, in_specs=[ANY] * (n + 1),
        out_specs=[ANY] * n + [SEM, SEM, pl.BlockSpec(memory_space=pltpu.VMEM)],
        out_shape=[_sds(w.shape, w.dtype) for w in bufs] + [dma((3 * n,)), dma((3 * n,)), _sds((8, LANE), F32)],
        input_output_aliases={k: k for k in range(n)},
        compiler_params=pltpu.CompilerParams(has_side_effects=_DATAFLOW),
    )(*bufs, after)
    return (list(res[:n]), res[n], res[n + 1]), res[n + 2]


def _swap_wait(name, bufs, ssem, rsem, after):
    n = len(bufs)

    def kern(*refs):
        b = refs[:n]
        ssem_ref, rsem_ref = refs[n], refs[n + 1]
        mx, my, mc = _me()
        sib = (mx, my, 1 - mc)
        for k in range(n):
            for q, (qx, qy) in enumerate(_other_chips(mx, my)):
                jq = 2 * qx + qy
                _rcopy(b[k].at[jq, 1 - mc], b[k].at[jq, 1 - mc], ssem_ref.at[3 * k + q], rsem_ref.at[3 * k + q],
                       sib).wait_recv()
                _rcopy(b[k].at[jq, mc], b[k].at[jq, mc], ssem_ref.at[3 * k + q], rsem_ref.at[3 * k + q],
                       sib).wait_send()

    return list(_pcall(
        kern, name=name, in_specs=[ANY] * n + [SEM, SEM, ANY], out_specs=[ANY] * n,
        out_shape=[_sds(w.shape, w.dtype) for w in bufs], input_output_aliases={k: k for k in range(n)},
        compiler_params=pltpu.CompilerParams(has_side_effects=_DATAFLOW),
    )(*bufs, ssem, rsem, after))


def _to_sibling(mx, my, mc):
    return [((j, 1 - mc), j, (mx, my, 1 - mc)) for j in range(4)]


def _to_chips(mx, my, mc):
    return [((2 * qx + qy,), q, (qx, qy, mc)) for q, (qx, qy) in enumerate(_other_chips(mx, my))]


def _to_all7(mx, my, mc):
    return [((0,), k, dev) for k, dev in enumerate(_peers7(mx, my, mc))]


def _send_start(name, srcs, plan, land_shapes, after):
    n = len(srcs)
    per = len(plan(0, 0, 0))

    def kern(*refs):
        s, land = refs[n + 1:2 * n + 1], refs[2 * n + 1:3 * n + 1]
        ssem, rsem, token = refs[3 * n + 1:]
        for k in range(n):
            for q, (idx, slot, dev) in enumerate(plan(*_me())):
                _rcopy(s[k].at[idx], land[k].at[slot], ssem.at[per * k + q], rsem.at[per * k + q], dev).start()
        token[...] = jnp.zeros_like(token)

    dma = pltpu.SemaphoreType.DMA
    res = _pcall(
        kern, name=name, in_specs=[ANY] * (n + 1),
        out_specs=[ANY] * (2 * n) + [SEM, SEM, pl.BlockSpec(memory_space=pltpu.VMEM)],
        out_shape=[_sds(s.shape, s.dtype) for s in srcs] + [_sds(ls, s.dtype) for ls, s in zip(land_shapes, srcs)]
        + [dma((per * n,)), dma((per * n,)), _sds((8, LANE), F32)],
        input_output_aliases={k: k for k in range(n)},
        compiler_params=pltpu.CompilerParams(has_side_effects=_DATAFLOW),
    )(*srcs, after)
    return (list(res[:n]), list(res[n:2 * n]), res[2 * n], res[2 * n + 1]), res[2 * n + 2]


def _send_wait(name, srcs, lands, ssem, rsem, plan, after):
    n = len(srcs)
    per = len(plan(0, 0, 0))

    def kern(*refs):
        s, land = refs[:n], refs[n:2 * n]
        ssem_ref, rsem_ref = refs[2 * n], refs[2 * n + 1]
        for k in range(n):
            for q, (idx, slot, dev) in enumerate(plan(*_me())):
                cp = _rcopy(s[k].at[idx], land[k].at[slot], ssem_ref.at[per * k + q], rsem_ref.at[per * k + q], dev)
                cp.wait_recv()
                cp.wait_send()

    res = _pcall(
        kern, name=name, in_specs=[ANY] * (2 * n) + [SEM, SEM, ANY], out_specs=[ANY] * (2 * n),
        out_shape=[_sds(a.shape, a.dtype) for a in list(srcs) + list(lands)],
        input_output_aliases={k: k for k in range(2 * n)},
        compiler_params=pltpu.CompilerParams(has_side_effects=_DATAFLOW),
    )(*srcs, *lands, ssem, rsem, after)
    return list(res[:n]), list(res[n:])


def _reduce_begin(tag, parts, after):
    return _send_start(f"pair_start_{tag}", parts, _to_sibling, [(4,) + p.shape[2:] for p in parts], after)


def _reduce_mid(tag, pairing, place, after):
    parts, theirs = _send_wait(f"pair_wait_{tag}", *pairing, _to_sibling, after)
    sums = []
    for k, (p, o) in enumerate(zip(parts, theirs)):
        rr, cc = p.shape[2], p.shape[3]
        tr = _row_tile(rr, cc)
        (s_k,) = _tiled_sp(
            f"pair_add_{tag}{k}", lambda tin: [tin[0].astype(F32) + tin[1].astype(F32)], (4, rr // tr), place,
            [(p, pl.BlockSpec((None, None, tr, cc), lambda j, i, sp: (j, sp[1], i, 0))),
             (o, pl.BlockSpec((None, tr, cc), lambda j, i, sp: (j, i, 0)))],
            [(_sds((4, rr, cc), BF16), pl.BlockSpec((None, tr, cc), lambda j, i, sp: (j, i, 0)))])
        sums.append(s_k)
    return _send_start(f"chips_start_{tag}", sums, _to_chips, [(3,) + s.shape[1:] for s in sums], theirs[0])


def _reduce_end(tag, flying, place, after):
    sums, lands = _send_wait(f"chips_wait_{tag}", *flying, _to_chips, after)
    fulls = []
    for k, (s, q) in enumerate(zip(sums, lands)):
        rr, cc = q.shape[1], q.shape[2]
        tr = _row_tile(rr, cc)

        def add4(tin):
            return [((tin[0].astype(F32) + tin[1].astype(F32)) + tin[2].astype(F32)) + tin[3].astype(F32)]

        ins = [(s, pl.BlockSpec((None, tr, cc), lambda i, sp: (sp[0], i, 0)))]
        ins += [(q, pl.BlockSpec((None, tr, cc), lambda i, sp, _k=kk: (_k, i, 0))) for kk in range(3)]
        (f_k,) = _tiled_sp(f"chip_add_{tag}{k}", add4, (rr // tr,), place, ins,
                           [(_sds((2, rr, cc), F32), pl.BlockSpec((None, tr, cc), lambda i, sp: (sp[1], i, 0)))])
        fulls.append(f_k)
    return fulls


def _pack(parts, PACK_ROWS=PACK_ROWS):
    flat, offs, pos = [], [], 0
    for p in parts:
        v = p.reshape(-1).astype(F32)
        n = -(-v.shape[0] // LANE) * LANE
        flat.append(jnp.pad(v, (0, n - v.shape[0])))
        offs.append((pos, v.shape[0], p.shape))
        pos += n
    total = -(-pos // (PACK_ROWS * LANE)) * PACK_ROWS * LANE
    flat.append(jnp.zeros((total - pos,), F32))
    return jnp.concatenate(flat).reshape(-1, LANE), offs


def _unpack(vec, offs):
    v = vec.reshape(-1)
    return [v[p:p + n].reshape(shape) for p, n, shape in offs]


def _adamw_math(wv, gv, mv, vv):
    bc1 = 1.0 - ADAM_B1 ** ADAM_STEP
    bc2 = 1.0 - ADAM_B2 ** ADAM_STEP
    mn = ADAM_B1 * mv + (1.0 - ADAM_B1) * gv
    vn = ADAM_B2 * vv + (1.0 - ADAM_B2) * (gv * gv)
    delta = -ADAM_LR * ((mn / bc1) / (jnp.sqrt(vn / bc2) + ADAM_EPS) + ADAM_WD * wv)
    return delta, mn, vn


def _adamw(name, w, g, m, v):
    rows, cols = w.shape
    tr = rows
    for cand in (512, 256, 128, 64, 32, 16, 8):
        if rows % cand == 0 and cand * cols * 4 <= 2 * 1024 * 1024:
            tr = cand
            break

    def fn(ids, tin, vin):
        return list(_adamw_math(*tin)), []

    spec = pl.BlockSpec((tr, cols), lambda i: (i, 0))
    outs = [(_sds((rows, cols), F32), spec)] * 3
    return _tiled(name, fn, (rows // tr,), [(a, spec) for a in (w, g, m, v)], [], outs)


def _adamw_many(name, ws, gs, ms, vs):
    n = len(ws)
    views = [(-1, a.shape[-1]) if a.ndim > 1 else (1, -1) for a in ws]
    flat = lambda arrs: [a.reshape(vw) for a, vw in zip(arrs, views)]

    def kern(*refs):
        ins, outs = refs[:4 * n], refs[4 * n:]
        for t in range(n):
            res = _adamw_math(*[ins[q * n + t][...] for q in range(4)])
            for q in range(3):
                outs[q * n + t][...] = res[q]

    shapes = [_sds(a.shape, F32) for a in flat(ws)]
    res = _pcall(kern, name=name, out_shape=shapes * 3, compiler_params=_cparams(),
                 )(*flat(ws), *flat(gs), *flat(ms), *flat(vs))
    back = lambda part: [a.reshape(w.shape) for a, w in zip(part, ws)]
    return back(res[:n]), back(res[n:2 * n]), back(res[2 * n:])


def _pos_embed():
    n_rows = T // GRID_W
    q = D // 4
    omega = 1.0 / (10000.0 ** (jnp.arange(q, dtype=F32) / q))
    er = jnp.arange(n_rows, dtype=jnp.int32).astype(F32)[:, None] * omega[None, :]
    ec = jnp.arange(GRID_W, dtype=jnp.int32).astype(F32)[:, None] * omega[None, :]
    by_row = jnp.concatenate([jnp.sin(er), jnp.cos(er)], axis=-1)
    by_col = jnp.concatenate([jnp.sin(ec), jnp.cos(ec)], axis=-1)
    return jnp.concatenate([jnp.repeat(by_row, GRID_W, axis=0), jnp.tile(by_col, (n_rows, 1))], axis=-1)


def _dense_gates(w_a, w_x):
    rows = jnp.stack([w_a[0], w_x[0], w_a[1], w_x[1]]).reshape(4, 2, RH, BLK)
    mask, spread = _block_mask(), _block_spread().T.astype(BF16)

    def kern(r_ref, m_ref, s_ref, o_ref):
        tiled = jnp.dot(r_ref[...].astype(BF16), s_ref[...], preferred_element_type=F32)
        o_ref[...] = (tiled * m_ref[...]).astype(o_ref.dtype)

    return _pcall(
        kern, name="gates_dense", grid=(2, 4),
        in_specs=[pl.BlockSpec((None, None, RH, BLK), lambda h, q: (q, h, 0, 0)),
                  pl.BlockSpec((RH, RH), lambda h, q: (0, 0)), pl.BlockSpec((BLK, RH), lambda h, q: (0, 0))],
        out_specs=pl.BlockSpec((None, RH, RH), lambda h, q: (h, 0, q)),
        out_shape=_sds((2, RH, NQ), BF16),
    )(rows, mask, spread)


def _block_mask():
    r = lax.broadcasted_iota(jnp.int32, (RH, RH), 0) // BLK
    c = lax.broadcasted_iota(jnp.int32, (RH, RH), 1) // BLK
    return (r == c).astype(F32)


def _block_spread():
    c = lax.broadcasted_iota(jnp.int32, (RH, BLK), 0) % BLK
    j = lax.broadcasted_iota(jnp.int32, (RH, BLK), 1)
    return (c == j).astype(F32)


def _fold_blocks(dense, mask, spread):
    return jnp.dot(dense * mask, spread, preferred_element_type=F32, precision=lax.Precision.HIGHEST)


def _gate_block_grads(folded):
    per = N_BLK // 2
    kinds = [jnp.concatenate([folded[h, q].reshape(per, BLK, BLK) for h in range(2)], axis=0) for q in range(4)]
    return jnp.stack([kinds[0], kinds[2]]), jnp.stack([kinds[1], kinds[3]])


def _gate_bias_dense(b_a, b_x):
    cols = []
    for h in range(2):
        for src in (b_a[0], b_x[0], b_a[1], b_x[1]):
            cols.append(src.reshape(R)[h * RH:(h + 1) * RH])
    return jnp.concatenate(cols).reshape(1, 2 * NQ)


def _gate_bias_grads(dgb):
    v = dgb.reshape(2, 4, RH)
    kinds = [jnp.concatenate([v[0, q], v[1, q]]).reshape(N_BLK, BLK) for q in range(4)]
    return jnp.stack([kinds[0], kinds[2]]), jnp.stack([kinds[1], kinds[3]])


def _residual_epilogue(next_norm):
    def epi(acc, ex):
        x_new = ex[0] + ex[1] * acc
        outs = [acc, x_new]
        if next_norm:
            outs.append(_norm_mod(x_new, ex[-3], ex[-2], ex[-1]))
        return outs
    return epi


def _mlp_fwd(tag, x_in, h, gate, w_in, w_out, next_norm=None, dep=None):
    tm = MM_TILE
    (r,) = _mm(f"{tag}_in", h, w_in, _NN, (T // tm, 4, 1),
               pl.BlockSpec((tm, D), lambda i, j, k: (i, 0)), pl.BlockSpec((None, D, D), lambda i, j, k: (j, 0, 0)),
               [(_sds((T, FF), BF16), pl.BlockSpec((tm, D), lambda i, j, k: (i, j)))], (tm, D),
               extra=[(d_, _full_spec(d_)) for d_ in _behind(dep)], epi=lambda acc, ex: [jnp.maximum(acc, 0.0)])
    row_spec = pl.BlockSpec((tm, D), lambda i, j, k: (i, 0))
    outs = [(_sds((T, D), F32), row_spec)] * 2 + ([(_sds((T, D), BF16), row_spec)] if next_norm else [])
    res = _mm(f"{tag}_out", r, w_out, _NN, (T // tm, 1, FF // D),
              pl.BlockSpec((tm, D), lambda i, j, k: (i, k)), pl.BlockSpec((D, D), lambda i, j, k: (k, 0)),
              outs, (tm, D),
              extra=[(x_in, row_spec), (gate, _full_spec(gate))] + [(v, _full_spec(v)) for v in next_norm or ()],
              a_pre=lambda a: a * a, epi=_residual_epilogue(next_norm))
    return dict(h=h, r=r, o=res[0], x_in=x_in), res[1], (res[2] if next_norm else None)


def _behind(dep):
    return [] if dep is None else [dep]


def _gate_bwd(tag, dx, o, gate, dep=None):
    def fn(ids, t, v):
        d_o = t[0] * v[0]
        return [d_o], [_sum0(t[0] * t[1]), _sum0(d_o)]
    return _tiled(f"{tag}_gate_bwd", fn, (T // ROW_TILE,), [_rows(dx), _rows(o)], [gate] + _behind(dep),
                  [_orow(T, D, BF16)], [(1, D), (1, D)])


def _norm_bwd(tag, dx_res, dh, dh_off, x, g_norm, sc, with_dx=True, dep=None):
    n_t = x.shape[0] // ROW_TILE

    def fn(ids, t, v):
        if with_dx:
            dres, dhv, xv = t
        else:
            dhv, xv = t
        dxv, d_sh, d_sc, d_g = _norm_mod_bwd(dhv, xv, v[0], v[1])
        return ([dres + dxv] if with_dx else []), [d_sh, d_sc, d_g]

    ins = ([_rows(dx_res)] if with_dx else []) + [_rows(dh, off=dh_off), _rows(x)]
    outs = [_orow(x.shape[0], D, F32)] if with_dx else []
    return _tiled(f"{tag}_norm_bwd", fn, (n_t,), ins, [g_norm, sc] + _behind(dep), outs, [(1, D)] * 3)


def _mlp_bwd(tag, dx, saved, g_norm, sc, gate, w_in, w_out, dep=None):
    d_o, d_gate, _ = _gate_bwd(tag, dx, saved["o"], gate, dep)
    tm = MM_TILE
    r = saved["r"]
    (da,) = _mm(f"{tag}_dz", d_o, w_out, _NT, (T // tm, FF // D, 1),
                pl.BlockSpec((tm, D), lambda i, j, k: (i, 0)), pl.BlockSpec((D, D), lambda i, j, k: (j, 0)),
                [(_sds((T, FF), BF16), pl.BlockSpec((tm, D), lambda i, j, k: (i, j)))], (tm, D),
                extra=[(r, pl.BlockSpec((tm, D), lambda i, j, k: (i, j)))],
                epi=lambda acc, ex: [acc * (2.0 * ex[0].astype(F32))])
    tk = MM_TILE
    (dw_out,) = _mm(f"{tag}_dwout", r, d_o, _TN, (FF // tm, 1, T // tk),
                    pl.BlockSpec((tk, tm), lambda i, j, k: (k, i)), pl.BlockSpec((tk, D), lambda i, j, k: (k, 0)),
                    [(_sds((FF, D), BF16), pl.BlockSpec((tm, D), lambda i, j, k: (i, 0)))], (tm, D),
                    a_pre=lambda a: a * a)
    (dh,) = _mm(f"{tag}_dh", da, w_in, _NT, (T // tm, 1, 4),
                pl.BlockSpec((tm, D), lambda i, j, k: (i, k)), pl.BlockSpec((None, D, D), lambda i, j, k: (k, 0, 0)),
                [(_sds((T, D), F32), pl.BlockSpec((tm, D), lambda i, j, k: (i, 0)))], (tm, D))
    (dw_in,) = _mm(f"{tag}_dwin", saved["h"], da, _TN, (D // tm, 4, T // tk),
                   pl.BlockSpec((tk, tm), lambda i, j, k: (k, i)), pl.BlockSpec((tk, D), lambda i, j, k: (k, j)),
                   [(_sds((4, D, D), BF16), pl.BlockSpec((None, tm, D), lambda i, j, k: (j, i, 0)))], (tm, D))
    dx_in, d_sh, d_sc, d_g = _norm_bwd(tag, dx, dh, 0, saved["x_in"], g_norm, sc)
    return dx_in, dw_in, dw_out, dict(sh=d_sh, sc=d_sc, gate=d_gate, g_norm=d_g)


def _local_step(x, ctx, tgt, mods, cmods, norm_g, final_g, rec, conf, wg, on_grads=None, wg_pre=None, on_later=None):
    on_grads = on_grads or (lambda group, dws: None)
    wg_pre = wg_pre or (lambda group, after: None)
    on_later = on_later or (lambda after: None)
    n_t = T // ROW_TILE
    row = lambda v: v.reshape(1, -1)
    m0 = [row(mods[0, q]) for q in range(6)]
    m1 = [row(mods[1, q]) for q in range(6)]
    g00, g01, g10, g11 = (row(norm_g[0, 0]), row(norm_g[0, 1]), row(norm_g[1, 0]), row(norm_g[1, 1]))
    csh, csc = row(cmods[0]), row(cmods[1])
    pos = _pos_embed()

    def prep0(ids, t, v):
        cx, xv, pv = t
        is_ctx = ids[0] == 0
        xin = jnp.where(is_ctx, cx, xv + pv)
        sh = jnp.where(is_ctx, v[3], v[1])
        sc = jnp.where(is_ctx, v[4], v[2])
        return [_norm_mod(xin, v[0], sc, sh), xv + pv], []

    dep = wg_pre("rec_in", csh)
    hcat, x0 = _tiled(
        "prep0", prep0, (N_SCAN,),
        [(ctx, pl.BlockSpec((ROW_TILE, D), lambda i: (0, 0))), _rows(x, off=-1, clamp_lo=True),
         _rows(pos, off=-1, clamp_lo=True)],
        [g00, m0[0], m0[1], csh, csc] + _behind(dep),
        [_orow(TA, D, BF16), _orow(T, D, F32, off=-1, clamp_lo=True)])

    tm_a = REC_TILE
    w_rin = wg("rec_in", hcat)["rec_w_in"]
    (a_in,) = _mm("rec_in", hcat, w_rin, _NN, (TA // tm_a, 4, 1),
                  pl.BlockSpec((tm_a, D), lambda i, j, k: (i, 0)),
                  pl.BlockSpec((None, D, RH), lambda i, j, k: (j, 0, 0)),
                  [(_sds((TA, 2 * R), F32), pl.BlockSpec((tm_a, RH), lambda i, j, k: (i, j)))], (tm_a, RH))
    rec_starts = (0, 1)
    u = _dwconv("rec_conv", a_in, R // CW_REC, rec["conv_w"], row(rec["conv_b"]), 1, rec_starts, R, CW_REC)
    wbd = _dense_gates(rec["w_a"], rec["w_x"])
    gbias = _gate_bias_dense(rec["b_a"], rec["b_x"])
    lam = rec["lam"]
    a_f, b_f, a_r, b_r = _tiled("rg_fwd", _rg_fwd_fn, (TA // RG_TILE,), [_rows(u, tm=RG_TILE)], [wbd, gbias, lam],
                                [_orow(TA, R, F32, tm=RG_TILE)] * 4, vec_refs=True)
    dep = wg_pre("rec_out", a_f)
    dep = wg_pre("mlp0", a_f if dep is None else dep)
    y_f, y_r, hin_f, hin_r = _scan_fwd(a_f, b_f, a_r, b_r)

    def rec_mid(ids, t, v):
        gp, yf, yr = t
        g, _ = _gelu(gp)
        return [g * (yf + yr)], []

    (m_rec,) = _tiled("rec_mid", rec_mid, (n_t,),
                      [_rows(a_in, R, off=1), _rows(y_f, off=1), _rows(y_r, off=1)], _behind(dep),
                      [_orow(T, R, BF16)])
    tm = MM_TILE
    row_spec = pl.BlockSpec((tm, D), lambda i, j, k: (i, 0))
    norm_mlp0 = (g01, m0[4], m0[3])
    w_rout = wg("rec_out", m_rec)["rec_w_out"]
    o_rec, x1, h_mlp0 = _mm(
        "rec_out", m_rec, w_rout, _NN, (T // tm, 1, 1),
        pl.BlockSpec((tm, R), lambda i, j, k: (i, 0)), pl.BlockSpec((R, D), lambda i, j, k: (0, 0)),
        [(_sds((T, D), F32), row_spec)] * 2 + [(_sds((T, D), BF16), row_spec)], (tm, D),
        extra=[(x0, row_spec), (m0[2], _full_spec(m0[2]))] + [(v, _full_spec(v)) for v in norm_mlp0],
        epi=_residual_epilogue(norm_mlp0))
    w_m0 = wg("mlp0", x1)
    dep = wg_pre("conf", x1)
    mlp0, x2, h1 = _mlp_fwd("mlp0", x1, h_mlp0, m0[5], w_m0["w_in"], w_m0["w_out"], (g10, m1[1], m1[0]), dep)

    b_pw1 = row(conf["b_pw1"])
    w_cf = wg("conf", x2)
    dep = wg_pre("mlp1", x2)
    (pre,) = _mm("conf_pw1", h1, w_cf["conf_w_pw1"], _NN, (T // tm, 4, 1),
                 pl.BlockSpec((tm, D), lambda i, j, k: (i, 0)),
                 pl.BlockSpec((None, D, D // 2), lambda i, j, k: (j, 0, 0)),
                 [(_sds((T, 2 * D), F32), pl.BlockSpec((tm, D // 2), lambda i, j, k: (i, j)))], (tm, D // 2),
                 extra=[(b_pw1, pl.BlockSpec((1, D // 2), lambda i, j, k: (0, j)))]
                 + [(d_, _full_spec(d_)) for d_ in _behind(dep)],
                 epi=lambda acc, ex: [acc + ex[0]])
    (zg,) = _tiled("conf_glu", lambda ids, t, v: ([t[0] * _sigmoid(t[1])], []), (n_t,),
                   [_rows(pre, D, col=0), _rows(pre, D, col=1)], [], [_orow(T, D, F32)])
    conf_starts = (0,)
    zc = _dwconv("conf_conv", zg, 0, conf["conv_w"], row(conf["conv_b"]), CONF_KW // 2, conf_starts, D, CW_CONF)
    ln_g, ln_b = row(conf["ln_g"]), row(conf["ln_b"])

    def ln_silu(ids, t, v):
        nh, _ = _layernorm_parts(t[0])
        ln = nh * v[0] + v[1]
        return [ln * _sigmoid(ln)], []

    (s_conf,) = _tiled("conf_ln", ln_silu, (n_t,), [_rows(zc)], [ln_g, ln_b], [_orow(T, D, BF16)])
    b_pw2 = row(conf["b_pw2"])
    norm_mlp1 = (g11, m1[4], m1[3])
    pw2_epi = _residual_epilogue(norm_mlp1)
    y_conf, x3, h_mlp1 = _mm(
        "conf_pw2", s_conf, w_cf["conf_w_pw2"], _NN, (T // tm, 1, 1),
        row_spec, pl.BlockSpec((D, D), lambda i, j, k: (0, 0)),
        [(_sds((T, D), F32), row_spec)] * 2 + [(_sds((T, D), BF16), row_spec)], (tm, D),
        extra=[(x2, row_spec), (m1[2], _full_spec(m1[2])), (b_pw2, _full_spec(b_pw2))]
        + [(v, _full_spec(v)) for v in norm_mlp1],
        epi=lambda acc, ex: pw2_epi(acc + ex[2], ex))
    w_m1 = wg("mlp1", x3)
    mlp1, x4, _ = _mlp_fwd("mlp1", x3, h_mlp1, m1[5], w_m1["w_in"], w_m1["w_out"])

    fg = row(final_g)

    def head(ids, t, v):
        n, r = _rms(t[0])
        err = n * v[0] - t[1]
        d_out = err * (1.0 / D)
        dn = d_out * v[0]
        dxv = r * (dn - n * jnp.mean(dn * n, axis=-1, keepdims=True))
        part = jnp.sum(_sum0(err * err), axis=1, keepdims=True) * (0.5 / D)
        return [dxv], [part, _sum0(d_out * n)]

    dx4, loss, d_fg = _tiled("head", head, (n_t,), [_rows(x4), _rows(tgt)], [fg], [_orow(T, D, F32)],
                             [(1, 1), (1, D)])

    dx3, dw_in1, dw_out1, dm_mlp1 = _mlp_bwd("mlp1", dx4, mlp1, g11, m1[4], m1[5],
                                             w_m1["w_in"], w_m1["w_out"])
    dep = on_grads("mlp1", (dw_in1, dw_out1))
    d_y, d_g1c, d_bpw2 = _gate_bwd("conf", dx3, y_conf, m1[2], dep)
    tk = MM_TILE
    (dw_pw2,) = _mm("conf_dwpw2", s_conf, d_y, _TN, (D // tm, 1, T // tk),
                    pl.BlockSpec((tk, tm), lambda i, j, k: (k, i)), pl.BlockSpec((tk, D), lambda i, j, k: (k, 0)),
                    [(_sds((D, D), BF16), pl.BlockSpec((tm, D), lambda i, j, k: (i, 0)))], (tm, D))
    (ds,) = _mm("conf_ds", d_y, w_cf["conf_w_pw2"], _NT, (T // tm, 1, 1),
                pl.BlockSpec((tm, D), lambda i, j, k: (i, 0)), pl.BlockSpec((D, D), lambda i, j, k: (0, 0)),
                [(_sds((T, D), F32), pl.BlockSpec((tm, D), lambda i, j, k: (i, 0)))], (tm, D))
    dep = on_later(ds)

    def ln_silu_bwd(ids, t, v):
        dsv, zcv = t
        nh, rstd = _layernorm_parts(zcv)
        ln = nh * v[0] + v[1]
        sg = _sigmoid(ln)
        d_ln = dsv * (sg * (1.0 + ln * (1.0 - sg)))
        d_nh = d_ln * v[0]
        d_zc = rstd * (d_nh - jnp.mean(d_nh, axis=-1, keepdims=True)
                       - nh * jnp.mean(d_nh * nh, axis=-1, keepdims=True))
        return [d_zc], [_sum0(d_ln * nh), _sum0(d_ln)]

    d_zc, d_lng, d_lnb = _tiled("conf_ln_bwd", ln_silu_bwd, (n_t,), [_rows(ds), _rows(zc)],
                                [ln_g, ln_b] + _behind(dep), [_orow(T, D, F32)], [(1, D), (1, D)])
    d_zg = _dwconv("conf_conv_dx", d_zc, 0, conf["conv_w"], jnp.zeros((1, D), F32),
                   CONF_KW - 1 - CONF_KW // 2, conf_starts, D, CW_CONF, flip=True)

    def glu_bwd(ids, t, v):
        dz, pa, pb = t
        sg = _sigmoid(pb)
        d_a = dz * sg
        d_b = dz * pa * sg * (1.0 - sg)
        return [jnp.concatenate([d_a, d_b], axis=1)], [_sum0(d_a), _sum0(d_b)]

    d_pre, d_b1a, d_b1b = _tiled(
        "conf_glu_bwd", glu_bwd, (n_t,), [_rows(d_zg), _rows(pre, D, col=0), _rows(pre, D, col=1)], [],
        [_orow(T, 2 * D, BF16)], [(1, D), (1, D)])
    (dw_pw1,) = _mm("conf_dwpw1", h1, d_pre, _TN, (D // tm, 4, T // tk),
                    pl.BlockSpec((tk, tm), lambda i, j, k: (k, i)),
                    pl.BlockSpec((tk, D // 2), lambda i, j, k: (k, j)),
                    [(_sds((4, D, D // 2), BF16), pl.BlockSpec((None, tm, D // 2), lambda i, j, k: (j, i, 0)))],
                    (tm, D // 2))
    dep = on_grads("conf", (dw_pw1, dw_pw2))
    (dh1,) = _mm("conf_dh", d_pre, w_cf["conf_w_pw1"], _NT, (T // tm, 1, 4),
                 pl.BlockSpec((tm, D // 2), lambda i, j, k: (i, k)),
                 pl.BlockSpec((None, D, D // 2), lambda i, j, k: (k, 0, 0)),
                 [(_sds((T, D), F32), pl.BlockSpec((tm, D), lambda i, j, k: (i, 0)))], (tm, D))
    dx2, d_sh1c, d_sc1c, d_g10 = _norm_bwd("conf", dx3, dh1, 0, x2, g10, m1[1], dep=dep)
    dep = on_later(dx2)

    dx1, dw_in0, dw_out0, dm_mlp0 = _mlp_bwd("mlp0", dx2, mlp0, g01, m0[4], m0[5],
                                             w_m0["w_in"], w_m0["w_out"], dep)
    dep = on_grads("mlp0", (dw_in0, dw_out0))
    d_orec, d_g1r, _ = _gate_bwd("rec", dx1, o_rec, m0[2], dep)
    (dw_rout,) = _mm("rec_dwout", m_rec, d_orec, _TN, (R // RH, 1, T // tk),
                     pl.BlockSpec((tk, RH), lambda i, j, k: (k, i)), pl.BlockSpec((tk, D), lambda i, j, k: (k, 0)),
                     [(_sds((R, D), BF16), pl.BlockSpec((RH, D), lambda i, j, k: (i, 0)))], (RH, D))
    (dm_rec,) = _mm("rec_dm", d_orec, w_rout, _NT, (T // tm, 1, 1),
                    pl.BlockSpec((tm, D), lambda i, j, k: (i, 0)), pl.BlockSpec((R, D), lambda i, j, k: (0, 0)),
                    [(_sds((T, R), F32), pl.BlockSpec((tm, R), lambda i, j, k: (i, 0)))], (tm, R))
    dep = on_later(dm_rec)

    def rec_mid_bwd(ids, t, v):
        dmv, gp, yf, yr = t
        g, th = _gelu(gp)
        lat = ids[0] > 0
        d_gp = jnp.where(lat, dmv * (yf + yr) * _gelu_grad(gp, th), 0.0)
        dy = jnp.where(lat, dmv * g, 0.0)
        return [d_gp, dy], []

    d_a, dy = _tiled("rec_mid_bwd", rec_mid_bwd, (N_SCAN,),
                     [_rows(dm_rec, off=-1, clamp_lo=True), _rows(a_in, R), _rows(y_f), _rows(y_r)], _behind(dep),
                     [(_sds((TA, 2 * R), BF16), pl.BlockSpec((ROW_TILE, R), lambda i: (i, 0))), _orow(TA, R, F32)])
    da_f, db_f, da_r, db_r = _scan_bwd(dy, a_f, y_f, hin_f, a_r, y_r, hin_r)
    d_gpre, d_u, d_gbias, d_lam = _tiled(
        "rg_bwd", _rg_bwd_fn, (TA // RG_TILE,), [_rows(a, tm=RG_TILE) for a in (u, da_f, db_f, da_r, db_r)],
        [wbd, gbias, lam], [_orow(TA, 2 * NQ, BF16, tm=RG_TILE), _orow(TA, R, F32, tm=RG_TILE)],
        [(1, 2 * NQ), (1, 2 * R)], vec_refs=True)
    tk_a = REC_TILE
    d_a = _dwconv("rec_conv_dx", d_u, 0, rec["conv_w"], jnp.zeros((1, R), F32), REC_KW - 1 - 1,
                  rec_starts, R, CW_REC, flip=True, into=(d_a, R // CW_REC))
    (dw_rin,) = _mm("rec_dwin", hcat, d_a, _TN, (D // tm, 4, TA // tk_a),
                    pl.BlockSpec((tk_a, tm), lambda i, j, k: (k, i)), pl.BlockSpec((tk_a, RH), lambda i, j, k: (k, j)),
                    [(_sds((4, D, RH), BF16), pl.BlockSpec((None, tm, RH), lambda i, j, k: (j, i, 0)))], (tm, RH))
    dep = on_grads("rec", (dw_rin, dw_rout))
    (dhcat,) = _mm("rec_dh", d_a, w_rin, _NT, (TA // tm_a, 1, 4),
                   pl.BlockSpec((tm_a, RH), lambda i, j, k: (i, k)),
                   pl.BlockSpec((None, D, RH), lambda i, j, k: (k, 0, 0)),
                   [(_sds((TA, D), F32), pl.BlockSpec((tm_a, D), lambda i, j, k: (i, 0)))], (tm_a, D))
    dx0, d_sh1r, d_sc1r, d_g00 = _norm_bwd("rec", dx1, dhcat, 1, x0, g00, m0[1], dep=dep)
    dep = on_later(dx0)

    d_csh, d_csc, d_g00c = _norm_bwd("ctx", None, dhcat, 0, ctx, g00, csc, with_dx=False, dep=dep)
    blk_mask, blk_spread = _block_mask(), _block_spread()
    (d_wbd,) = _mm("rg_dw", u, d_gpre, _TN, (2, 2, TA // tk_a),
                   pl.BlockSpec((tk_a, RH), lambda i, j, k: (k, i)),
                   pl.BlockSpec((tk_a, NQ // 2), lambda i, j, k: (k, 2 * i + j)),
                   [(_sds((2, 4, RH, BLK), F32), pl.BlockSpec((None, 2, RH, BLK), lambda i, j, k: (i, j, 0, 0)))],
                   (RH, NQ // 2),
                   extra=[(blk_mask, _full_spec(blk_mask)), (blk_spread, _full_spec(blk_spread))]
                   + [(d, _full_spec(d)) for d in _behind(dep)],
                   epi=lambda acc, ex: [jnp.stack([_fold_blocks(acc[:, s * RH:(s + 1) * RH], ex[0], ex[1])
                                                   for s in range(2)])])
    d_cw_rec = _dwconv_wgrad("rec_conv_dw", d_u, a_in, R // CW_REC, REC_KW, 1, rec_starts, R, CW_REC, dep)
    d_cw_conf = _dwconv_wgrad("conf_conv_dw", d_zc, zg, 0, CONF_KW, CONF_KW // 2, conf_starts, D, CW_CONF, dep)

    big = dict(rec_w_in=dw_rin, rec_w_out=dw_rout, conf_w_pw1=dw_pw1, conf_w_pw2=dw_pw2,
               mlp_w_in=(dw_in0, dw_in1), mlp_w_out=(dw_out0, dw_out1))
    d_wa, d_wx = _gate_block_grads(d_wbd)
    d_ba, d_bx = _gate_bias_grads(d_gbias)
    d_mod = jnp.concatenate([
        d_sh1r, d_sc1r, d_g1r, dm_mlp0["sh"], dm_mlp0["sc"], dm_mlp0["gate"],
        d_sh1c, d_sc1c, d_g1c, dm_mlp1["sh"], dm_mlp1["sc"], dm_mlp1["gate"]], axis=1).reshape(2, 6 * D)
    small = dict(
        d_mod=d_mod, d_cmod=jnp.concatenate([d_csh, d_csc], axis=1),
        norm_g=jnp.concatenate([d_g00 + d_g00c, dm_mlp0["g_norm"], d_g10, dm_mlp1["g_norm"]], axis=1),
        rec_conv_w=d_cw_rec[:REC_KW], rec_conv_b=d_cw_rec[REC_KW], rec_lambda=d_lam.reshape(2, R),
        rec_w_a=d_wa, rec_b_a=d_ba, rec_w_x=d_wx, rec_b_x=d_bx,
        conf_b_pw1=jnp.concatenate([d_b1a, d_b1b], axis=1), conf_conv_w=d_cw_conf[:CONF_KW],
        conf_conv_b=d_cw_conf[CONF_KW], conf_ln_g=d_lng, conf_ln_b=d_lnb, conf_b_pw2=d_bpw2, final_g=d_fg)
    return loss.reshape(()), dx0, big, small


_BIG = ("rec_w_in", "rec_w_out", "conf_w_pw1", "conf_w_pw2", "mlp_w_in", "mlp_w_out")


def _halves(w):
    return w.reshape(w.shape[0], 2, w.shape[1] // 2, w.shape[2])


def _ada_fwd(c16, w_ada, b_shard):
    ns = w_ada.shape[2]
    tn = 512

    def kern(c_ref, w_ref, b_ref, o_ref):
        cv = c_ref[...]
        s = (cv * _sigmoid(cv)).astype(BF16)
        o_ref[...] = jnp.dot(s, w_ref[...].astype(BF16), preferred_element_type=F32) + b_ref[...]

    return _pcall(
        kern, name="ada_fwd", grid=(2, ns // tn),
        in_specs=[pl.BlockSpec((16, D), lambda l, j: (0, 0)), pl.BlockSpec((None, D, tn), lambda l, j: (l, 0, j)),
                  pl.BlockSpec((None, 1, tn), lambda l, j: (l, 0, j))],
        out_specs=pl.BlockSpec((None, 16, tn), lambda l, j: (l, 0, j)),
        out_shape=_sds((2, 16, ns), F32), compiler_params=_cparams(),
    )(c16, w_ada, b_shard)


def _ada_bwd(c16, dm16, w_ada):
    ns = w_ada.shape[2]
    tn = 512

    def kern(c_ref, dm_ref, w_ref, gw_ref, ds_ref):
        cv = c_ref[...]
        s = (cv * _sigmoid(cv)).astype(BF16)
        dm = dm_ref[...].astype(BF16)
        gw_ref[...] = lax.dot_general(s, dm, _TN, preferred_element_type=F32)

        @pl.when(jnp.logical_and(pl.program_id(0) == 0, pl.program_id(1) == 0))
        def _():
            ds_ref[...] = jnp.zeros_like(ds_ref)

        ds_ref[...] += lax.dot_general(dm, w_ref[...].astype(BF16), _NT, preferred_element_type=F32)

    return _pcall(
        kern, name="ada_bwd", grid=(2, ns // tn),
        in_specs=[pl.BlockSpec((16, D), lambda l, j: (0, 0)), pl.BlockSpec((None, 16, tn), lambda l, j: (l, 0, j)),
                  pl.BlockSpec((None, D, tn), lambda l, j: (l, 0, j))],
        out_specs=[pl.BlockSpec((None, D, tn), lambda l, j: (l, 0, j)), pl.BlockSpec((16, D), lambda l, j: (0, 0))],
        out_shape=[_sds((2, D, ns), F32), _sds((16, D), F32)], compiler_params=_cparams(),
    )(c16, dm16, w_ada)


def _cctx_grad(ds4, c_ctx):
    def kern(d_ref, c_ref, o_ref):
        tot = d_ref[0, 0:1, :] + d_ref[1, 0:1, :] + d_ref[2, 0:1, :] + d_ref[3, 0:1, :]
        cv = c_ref[...]
        sg = _sigmoid(cv)
        o_ref[...] = tot * (sg * (1.0 + cv * (1.0 - sg)))

    return _pcall(kern, name="cctx_grad", out_shape=_sds((1, D), F32))(ds4, c_ctx.reshape(1, D))


def kernel(x, c, ctx, c_ctx, w_ada, b_ada, norm_g, rec_w_in, rec_conv_w, rec_conv_b, rec_lambda, rec_w_a, rec_b_a, rec_w_x, rec_b_x, rec_w_out, conf_w_pw1, conf_b_pw1, conf_conv_w, conf_conv_b, conf_ln_g, conf_ln_b, conf_w_pw2, conf_b_pw2, mlp_w_in, mlp_w_out, final_g, loss_target, m_c_ctx, m_w_ada, m_b_ada, m_norm_g, m_rec_w_in, m_rec_conv_w, m_rec_conv_b, m_rec_lambda, m_rec_w_a, m_rec_b_a, m_rec_w_x, m_rec_b_x, m_rec_w_out, m_conf_w_pw1, m_conf_b_pw1, m_conf_conv_w, m_conf_conv_b, m_conf_ln_g, m_conf_ln_b, m_conf_w_pw2, m_conf_b_pw2, m_mlp_w_in, m_mlp_w_out, m_final_g, v_c_ctx, v_w_ada, v_b_ada, v_norm_g, v_rec_w_in, v_rec_conv_w, v_rec_conv_b, v_rec_lambda, v_rec_w_a, v_rec_b_a, v_rec_w_x, v_rec_b_x, v_rec_w_out, v_conf_w_pw1, v_conf_b_pw1, v_conf_conv_w, v_conf_conv_b, v_conf_ln_g, v_conf_ln_b, v_conf_w_pw2, v_conf_b_pw2, v_mlp_w_in, v_mlp_w_out, v_final_g):
    names = ["c_ctx", "w_ada", "b_ada", "norm_g", "rec_w_in", "rec_conv_w", "rec_conv_b", "rec_lambda", "rec_w_a",
             "rec_b_a", "rec_w_x", "rec_b_x", "rec_w_out", "conf_w_pw1", "conf_b_pw1", "conf_conv_w", "conf_conv_b",
             "conf_ln_g", "conf_ln_b", "conf_w_pw2", "conf_b_pw2", "mlp_w_in", "mlp_w_out", "final_g"]
    w = dict(zip(names, [c_ctx, w_ada, b_ada, norm_g, rec_w_in, rec_conv_w, rec_conv_b, rec_lambda, rec_w_a,
                         rec_b_a, rec_w_x, rec_b_x, rec_w_out, conf_w_pw1, conf_b_pw1, conf_conv_w, conf_conv_b,
                         conf_ln_g, conf_ln_b, conf_w_pw2, conf_b_pw2, mlp_w_in, mlp_w_out, final_g]))
    m = dict(zip(names, [m_c_ctx, m_w_ada, m_b_ada, m_norm_g, m_rec_w_in, m_rec_conv_w, m_rec_conv_b, m_rec_lambda,
                         m_rec_w_a, m_rec_b_a, m_rec_w_x, m_rec_b_x, m_rec_w_out, m_conf_w_pw1, m_conf_b_pw1,
                         m_conf_conv_w, m_conf_conv_b, m_conf_ln_g, m_conf_ln_b, m_conf_w_pw2, m_conf_b_pw2,
                         m_mlp_w_in, m_mlp_w_out, m_final_g]))
    v = dict(zip(names, [v_c_ctx, v_w_ada, v_b_ada, v_norm_g, v_rec_w_in, v_rec_conv_w, v_rec_conv_b, v_rec_lambda,
                         v_rec_w_a, v_rec_b_a, v_rec_w_x, v_rec_b_x, v_rec_w_out, v_conf_w_pw1, v_conf_b_pw1,
                         v_conf_conv_w, v_conf_conv_b, v_conf_ln_g, v_conf_ln_b, v_conf_w_pw2, v_conf_b_pw2,
                         v_mlp_w_in, v_mlp_w_out, v_final_g]))
    mx, my, mc = _me()
    chip = 2 * mx + my
    me = 4 * mx + 2 * my + mc

    sharded_small = ["norm_g", "rec_conv_w", "rec_lambda", "conf_b_pw1", "conf_conv_w", "conf_conv_b", "conf_ln_g",
                     "conf_ln_b", "conf_b_pw2"]
    packed, offs = _pack([c] + [w[k] for k in sharded_small], 8)
    place = jnp.stack([chip, mc]).astype(jnp.int32)
    shards = [("rec_in", _halves(rec_w_in), 0), ("rec_out", _halves(rec_w_out), 0),
              ("pw1", _halves(conf_w_pw1), 0), ("pw2", _halves(conf_w_pw2), 0),
              ("mlp_in0", _halves(mlp_w_in), 0), ("mlp_in1", _halves(mlp_w_in), 1),
              ("mlp_out0", _halves(mlp_w_out), 0), ("mlp_out1", _halves(mlp_w_out), 1)]
    small_state, small_sent = _send_start("gather_small_start", [packed[None]], _to_all7, [(7,) + packed.shape], place)
    (slot_rin,) = _place_big(shards[:1], place, small_sent)
    flying, gsems, swapping = {}, {}, {}
    flying["rec_in"], gsems["rec_in"], rec_started = _gather_start("gather_start_rec", [slot_rin], ((0,),), small_sent)
    slots = [slot_rin] + _place_big(shards[1:], place, rec_started)
    placed = jnp.broadcast_to(lax.dynamic_slice(slots[-1], (chip, 0, 0, 0), (1, 1, 1, 1)).reshape(1, 1), (8, 1))
    (own,), (landed,) = _send_wait("gather_small_wait", *small_state, _to_all7, placed)
    by_flip = jnp.concatenate([own, landed], axis=0)
    got_flat = jnp.take(by_flip, jnp.arange(8) ^ me, axis=0).reshape(8, -1)

    def piece(i):
        p, n, shape = offs[i]
        return got_flat[:, p:p + n].reshape((8,) + tuple(shape))

    c_rows = piece(0).reshape(8, D)
    full = {}
    for i, k in enumerate(sharded_small):
        per_chip = jnp.moveaxis(piece(1 + i)[0::2], 0, -2)
        full[k] = per_chip.reshape(per_chip.shape[:-2] + (4 * per_chip.shape[-1],))
    c16 = jnp.concatenate([c_rows, c_ctx.reshape(1, D), jnp.zeros((7, D), F32)], axis=0)

    ns = w_ada.shape[2]
    b_shard = lax.dynamic_slice_in_dim(b_ada, chip * ns, ns, axis=1).reshape(2, 1, ns)
    prod = _ada_fwd(c16, w_ada, b_shard)

    own_rows = lax.dynamic_index_in_dim(prod[:, :8].reshape(2, 4, 2, ns), mc, axis=2, keepdims=False)
    rows = jnp.concatenate([own_rows.transpose(1, 0, 2), jnp.broadcast_to(prod[0, 8], (4, 1, ns)),
                            jnp.zeros((4, 5, ns), F32)], axis=1)
    mod_state, mod_started = _send_start("mod_start", [rows], _to_chips, [(3, 8, ns)], place)
    use_order = dict(rec=(0, 1), mlp0=(4, 6), conf=(2, 3), mlp1=(5, 7))
    fetch_order = dict(rec_out=(1,), mlp0=(4, 6), conf=(2, 3), mlp1=(5, 7))
    order = [t for g in fetch_order for t in fetch_order[g]]
    groups = [tuple(order.index(t) for t in fetch_order[g]) for g in fetch_order]
    fly, sems, all_started = _gather_start("gather_start_rest", [slots[t] for t in order], tuple(groups), mod_started)
    for gi, g in enumerate(fetch_order):
        flying[g], gsems[g] = [fly[k] for k in groups[gi]], sems[2 * gi:2 * gi + 2]

    def wg_pre(group, after):
        bufs = _gather_wait(f"gather_wait_{group}", flying[group], *gsems[group], after)
        swapping[group], token = _swap_start(f"swap_start_{group}", bufs, after)
        return token

    def wg(group, after):
        bufs = _swap_wait(f"swap_wait_{group}", *swapping[group], after)
        if group == "rec_in":
            return dict(rec_w_in=bufs[0].reshape(4, D, RH))
        if group == "rec_out":
            return dict(rec_w_out=bufs[0].reshape(R, D))
        if group == "conf":
            return dict(conf_w_pw1=bufs[0].reshape(4, D, D // 2), conf_w_pw2=bufs[1].reshape(D, D))
        return dict(w_in=bufs[0].reshape(4, D, D), w_out=bufs[1].reshape(FF, D))

    (rows,), (landed,) = _send_wait("mod_wait", *mod_state, _to_chips, all_started)
    own = lax.dynamic_index_in_dim(rows, chip, axis=0, keepdims=True)
    by_flip = jnp.concatenate([own, landed[1:2], landed[0:1], landed[2:3]], axis=0)
    by_chip = jnp.take(by_flip, jnp.arange(4) ^ chip, axis=0)
    mods = by_chip[:, :2].transpose(1, 0, 2).reshape(2, 6, D)
    cmods = by_chip[:, 2].reshape(6, D)[:2]

    rec = dict(conv_w=full["rec_conv_w"][0], conv_b=rec_conv_b[0], lam=full["rec_lambda"][0],
               w_a=rec_w_a[0], b_a=rec_b_a[0], w_x=rec_w_x[0], b_x=rec_b_x[0])
    conf = dict(b_pw1=full["conf_b_pw1"][0], conv_w=full["conf_conv_w"][0], conv_b=full["conf_conv_b"][0],
                ln_g=full["conf_ln_g"][0], ln_b=full["conf_ln_b"][0], b_pw2=full["conf_b_pw2"][0])
    pairing, sent = {}, {}

    def on_grads(group, dws):
        parts = [dw.reshape((4,) + shards[t][1].shape[1:]) for dw, t in zip(dws, use_order[group])]
        pairing[group], token = _reduce_begin(group, parts, place)
        return token

    def finish_pair(after):
        (group, state), = pairing.items()
        pairing.clear()
        sent[group], sent["token"] = _reduce_mid(group, state, place, after)
        return sent["token"]

    loss_local, grad_x, _, small = _local_step(x[0], ctx[0], loss_target[0], mods, cmods, full["norm_g"], final_g,
                                               rec, conf, wg, on_grads, wg_pre, finish_pair)
    rec_sent = sent["token"]
    small["loss"] = loss_local.reshape(1)

    small_names = ["loss", "d_mod", "d_cmod", "norm_g", "rec_conv_w", "rec_conv_b", "rec_lambda", "rec_w_a", "rec_b_a",
                   "rec_w_x", "rec_b_x", "conf_b_pw1", "conf_conv_w", "conf_conv_b", "conf_ln_g", "conf_ln_b",
                   "conf_b_pw2", "final_g"]
    mine = lax.broadcasted_iota(jnp.int32, (8, 1), 0) == me
    mod_slots = jnp.where(mine, small["d_mod"].reshape(1, -1), 0.0)
    spacked, soffs = _pack([small[k] for k in small_names] + [mod_slots])
    small_state, small_started = _allreduce_small_begin(spacked, place, rec_sent)

    fulls = {}
    for group in ("mlp1", "conf", "mlp0", "rec"):
        for t, f in zip(use_order[group], _reduce_end(group, sent[group], place, small_started)):
            fulls[t] = f
    summed = jnp.broadcast_to(lax.dynamic_slice(fulls[use_order["rec"][-1]], (mc, 0, 0), (1, 1, 1)).reshape(1, 1),
                              (8, 1))
    sharing = {}

    def start_share(after):
        sharing["state"], token = _share_start("share_start", [fulls[t] for t in range(8)], after)
        return token

    unpacked = _unpack(_allreduce_small_end(small_state, summed, start_share), soffs)
    ssum = dict(zip(small_names, unpacked[:-1]))
    loss = ssum["loss"].reshape(())
    dmod_rows = unpacked[-1].reshape(8, 2, 6 * D).transpose(1, 0, 2)

    d_cmod_full =jnp.concatenate([ssum["d_cmod"].reshape(1, 2 * D), jnp.zeros((1, 4 * D), F32)], axis=1)
    dm16 = jnp.concatenate([dmod_rows, jnp.stack([d_cmod_full, jnp.zeros((1, 6 * D), F32)]),
                            jnp.zeros((2, 7, 6 * D), F32)], axis=1)
    dm16_shard = lax.dynamic_slice_in_dim(dm16, chip * ns, ns, axis=2)
    g_w_ada, ds_part = _ada_bwd(c16, dm16_shard, w_ada)
    ds_state, ds_sent = _send_start("dsilu_start", [jnp.broadcast_to(ds_part[8:16], (4, 8, D))], _to_chips,
                                    [(3, 8, D)], place)

    whole = _share_wait("share_wait", *sharing["state"], ds_sent)
    g_big = dict(rec_w_in=whole[0].reshape(rec_w_in.shape), rec_w_out=whole[1].reshape(rec_w_out.shape),
                 conf_w_pw1=whole[2].reshape(conf_w_pw1.shape), conf_w_pw2=whole[3].reshape(conf_w_pw2.shape),
                 mlp_w_in=jnp.stack([whole[4].reshape(D, D), whole[5].reshape(D, D)]),
                 mlp_w_out=jnp.stack([whole[6].reshape(D, D), whole[7].reshape(D, D)]))
    delta, new_m, new_v = {}, {}, {}

    def adamw_of(k, g):
        cols = w[k].shape[-1]
        d_, m_, v_ = _adamw(f"adamw_{k}", w[k].reshape(-1, cols), g.reshape(-1, cols),
                            m[k].reshape(-1, cols), v[k].reshape(-1, cols))
        delta[k], new_m[k], new_v[k] = (a.reshape(w[k].shape) for a in (d_, m_, v_))

    for k in _BIG:
        adamw_of(k, g_big[k])
    adamw_of("w_ada", g_w_ada)
    (ds_own,), (ds_landed,) = _send_wait("dsilu_wait", *ds_state, _to_chips, new_v["w_ada"])
    ds_flip = jnp.concatenate([ds_own[:1], ds_landed[1:2], ds_landed[0:1], ds_landed[2:3]], axis=0)
    g_c_ctx = _cctx_grad(jnp.take(ds_flip, jnp.arange(4) ^ chip, axis=0), c_ctx).reshape(D)
    g_b_ada = ssum["d_mod"] + jnp.stack([d_cmod_full[0], jnp.zeros((6 * D,), F32)])

    def shard_of(a, axis):
        n = a.shape[axis] // 4
        return lax.dynamic_slice_in_dim(a, chip * n, n, axis=axis)

    grads = dict(
        c_ctx=g_c_ctx, w_ada=g_w_ada, b_ada=g_b_ada,
        norm_g=shard_of(ssum["norm_g"].reshape(2, 2, D), 2),
        rec_w_in=g_big["rec_w_in"], rec_conv_w=shard_of(ssum["rec_conv_w"].reshape(1, REC_KW, R), 2),
        rec_conv_b=ssum["rec_conv_b"].reshape(1, R), rec_lambda=shard_of(ssum["rec_lambda"].reshape(1, 2, R), 2),
        rec_w_a=ssum["rec_w_a"].reshape(rec_w_a.shape), rec_b_a=ssum["rec_b_a"].reshape(rec_b_a.shape),
        rec_w_x=ssum["rec_w_x"].reshape(rec_w_x.shape), rec_b_x=ssum["rec_b_x"].reshape(rec_b_x.shape),
        rec_w_out=g_big["rec_w_out"], conf_w_pw1=g_big["conf_w_pw1"],
        conf_b_pw1=shard_of(ssum["conf_b_pw1"].reshape(1, 2 * D), 1),
        conf_conv_w=shard_of(ssum["conf_conv_w"].reshape(1, CONF_KW, D), 2),
        conf_conv_b=shard_of(ssum["conf_conv_b"].reshape(1, D), 1),
        conf_ln_g=shard_of(ssum["conf_ln_g"].reshape(1, D), 1), conf_ln_b=shard_of(ssum["conf_ln_b"].reshape(1, D), 1),
        conf_w_pw2=g_big["conf_w_pw2"], conf_b_pw2=shard_of(ssum["conf_b_pw2"].reshape(1, D), 1),
        mlp_w_in=g_big["mlp_w_in"], mlp_w_out=g_big["mlp_w_out"], final_g=ssum["final_g"].reshape(D))

    rest =[k for k in names if k not in ("w_ada",) + _BIG]
    d_, m_, v_ = _adamw_many("adamw_small", [w[k] for k in rest], [grads[k] for k in rest],
                             [m[k] for k in rest], [v[k] for k in rest])
    for k, dd, mm, vv in zip(rest, d_, m_, v_):
        delta[k], new_m[k], new_v[k] = dd, mm, vv

    return (loss, grad_x[None], *[grads[k] for k in names], *[delta[k] for k in names],
            *[new_m[k] for k in names], *[new_v[k] for k in names])
```

```python
import functools
import math

import jax
import jax.numpy as jnp
from jax import lax
from jax.experimental import pallas as pl
from jax.experimental.pallas import tpu as pltpu

F32 = jnp.float32
BF16 = jnp.bfloat16

D = 1024
T = 2048
TC = 256
TA = T + TC
R = 1280
RH = R // 2
NQ = 4 * RH
FF = 4096
N_BLK = 16
BLK = R // N_BLK
GRID_W = 64
EPS = 1e-6
RG_C = 8.0
CONF_KW = 31
REC_KW = 4
LANE = 128
ROW_TILE = 256
HALO = 16
RG_TILE = 128
PACK_ROWS = 512
MM_TILE = 1024
REC_TILE = TA // 2
CW_REC = 640
CW_CONF = 512
V7X_VMEM_BYTES = 64 * 1024 * 1024
VMEM_LIMIT = V7X_VMEM_BYTES - 8 * 1024 * 1024

ADAM_LR = 0.001
ADAM_B1 = 0.9
ADAM_B2 = 0.999
ADAM_EPS = 1e-08
ADAM_WD = 0.01
ADAM_STEP = 10

MESH = pl.DeviceIdType.MESH
ANY = pl.BlockSpec(memory_space=pl.ANY)


def _sds(shape, dtype):
    return jax.ShapeDtypeStruct(tuple(shape), dtype)


def _pcall(body, **kw):
    return pl.pallas_call(body, **kw)


def _cparams():
    return pltpu.CompilerParams(vmem_limit_bytes=VMEM_LIMIT)


def _full_spec(arr):
    nd = arr.ndim
    return pl.BlockSpec(arr.shape, lambda *ids, _n=nd: (0,) * _n)


def _sum0(v):
    return jnp.sum(v, axis=0, keepdims=True)


def _tiled(name, fn, grid, ins, vecs, outs, vec_outs=(), vec_refs=False):
    n_in, n_vec, n_out = len(ins), len(vecs), len(outs)
    n_grid = len(grid)

    def kern(*refs):
        ids = [pl.program_id(a) for a in range(n_grid)]
        tin = [r[...] for r in refs[:n_in]]
        vin = list(refs[n_in:n_in + n_vec]) if vec_refs else [r[...] for r in refs[n_in:n_in + n_vec]]
        o_refs = refs[n_in + n_vec:n_in + n_vec + n_out]
        a_refs = refs[n_in + n_vec + n_out:]
        tout, incs = fn(ids, tin, vin)
        for r, v in zip(o_refs, tout):
            r[...] = v.astype(r.dtype)
        if a_refs:
            first = functools.reduce(jnp.logical_and, [i == 0 for i in ids])

            @pl.when(first)
            def _():
                for r in a_refs:
                    r[...] = jnp.zeros_like(r)

            for r, v in zip(a_refs, incs):
                r[...] += v

    out_shape = [o for o, _ in outs] + [_sds(s, F32) for s in vec_outs]
    out_specs = [s for _, s in outs] + [
        pl.BlockSpec(tuple(s), lambda *ids, _n=len(s): (0,) * _n) for s in vec_outs]
    res = _pcall(
        kern, name=name, grid=tuple(grid),
        in_specs=[s for _, s in ins] + [_full_spec(v) for v in vecs],
        out_specs=out_specs, out_shape=out_shape, compiler_params=_cparams(),
    )(*[a for a, _ in ins], *vecs)
    return list(res)


def _rows(arr, ncols=None, tm=ROW_TILE, off=0, col=0, clamp_lo=False):
    ncols = arr.shape[1] if ncols is None else ncols
    if clamp_lo:
        return arr, pl.BlockSpec((tm, ncols), lambda i: (jnp.maximum(i + off, 0), col))
    return arr, pl.BlockSpec((tm, ncols), lambda i: (i + off, col))


def _orow(nrows, ncols, dtype, tm=ROW_TILE, off=0, clamp_lo=False):
    if clamp_lo:
        return _sds((nrows, ncols), dtype), pl.BlockSpec((tm, ncols), lambda i: (jnp.maximum(i + off, 0), 0))
    return _sds((nrows, ncols), dtype), pl.BlockSpec((tm, ncols), lambda i: (i + off, 0))


_NN = (((1,), (0,)), ((), ()))
_TN = (((0,), (0,)), ((), ()))
_NT = (((1,), (1,)), ((), ()))


def _mm(name, a, b, dims, grid, a_spec, b_spec, out, acc_shape, extra=(), a_pre=None, epi=None):
    n_k = grid[2]
    n_ex = len(extra)

    def kern(a_ref, b_ref, *rest):
        ex = rest[:n_ex]
        o_refs = rest[n_ex:n_ex + len(out)]
        k = pl.program_id(2)
        av = a_ref[...]
        if a_pre is not None:
            av = a_pre(av)
        part = lax.dot_general(av.astype(BF16), b_ref[...].astype(BF16), dims, preferred_element_type=F32)

        def finish(total):
            vals = [total] if epi is None else epi(total, [e[...] for e in ex])
            for r, v in zip(o_refs, vals):
                r[...] = v.astype(r.dtype)

        if n_k == 1:
            finish(part)
        else:
            acc = rest[-1]

            @pl.when(k == 0)
            def _():
                acc[...] = part

            @pl.when(jnp.logical_and(k > 0, k < n_k - 1))
            def _():
                acc[...] += part

            @pl.when(k == n_k - 1)
            def _():
                finish(acc[...] + part)

    res = _pcall(
        kern, name=name, grid=tuple(grid),
        in_specs=[a_spec, b_spec] + [s for _, s in extra],
        out_specs=[s for _, s in out], out_shape=[o for o, _ in out],
        scratch_shapes=[] if n_k == 1 else [pltpu.VMEM(tuple(acc_shape), F32)], compiler_params=_cparams(),
    )(a, b, *[e for e, _ in extra])
    return list(res)


def _rms(x):
    r = lax.rsqrt(jnp.mean(x * x, axis=-1, keepdims=True) + EPS)
    return x * r, r


def _norm_mod(x, g, sc, sh):
    n, _ = _rms(x)
    return (n * g) * (1.0 + sc) + sh


def _norm_mod_bwd(dh, x, g, sc):
    n, r = _rms(x)
    d_sh = _sum0(dh)
    d_sc = _sum0(dh * (n * g))
    d_g = _sum0(dh * (1.0 + sc) * n)
    dn = dh * (g * (1.0 + sc))
    dx = r * (dn - n * jnp.mean(dn * n, axis=-1, keepdims=True))
    return dx, d_sh, d_sc, d_g


_GELU_K = math.sqrt(2.0 / math.pi)


def _gelu(x):
    t = jnp.tanh(_GELU_K * (x + 0.044715 * x * x * x))
    return 0.5 * x * (1.0 + t), t


def _gelu_grad(x, t):
    return 0.5 * (1.0 + t) + 0.5 * x * (1.0 - t * t) * (_GELU_K * (1.0 + 3.0 * 0.044715 * x * x))


def _sigmoid(x):
    return 0.5 * jnp.tanh(0.5 * x) + 0.5


def _expm1(x):
    p = jnp.full_like(x, 1.0 / 5040.0)
    for c in (1.0 / 720.0, 1.0 / 120.0, 1.0 / 24.0, 1.0 / 6.0, 0.5, 1.0):
        p = p * x + c
    return jnp.where(jnp.abs(x) < 0.3, x * p, jnp.exp(x) - 1.0)


def _softplus_neg(lam):
    return jnp.log1p(jnp.exp(-jnp.abs(lam))) + jnp.maximum(-lam, 0.0)


def _layernorm_parts(x):
    mu = jnp.mean(x, axis=-1, keepdims=True)
    xc = x - mu
    rstd = lax.rsqrt(jnp.mean(xc * xc, axis=-1, keepdims=True) + EPS)
    return xc * rstd, rstd


def _rg_gates(u, wbd, gbias, lam):
    sp = _softplus_neg(lam)
    parts = {}
    for h in range(2):
        uh = u[:, h * RH:(h + 1) * RH]
        g = jnp.dot(uh.astype(BF16), wbd[h], preferred_element_type=F32) + gbias[:, h * NQ:(h + 1) * NQ]
        for d in range(2):
            r = _sigmoid(g[:, (2 * d) * RH:(2 * d + 1) * RH])
            i = _sigmoid(g[:, (2 * d + 1) * RH:(2 * d + 2) * RH])
            sph = sp[d:d + 1, h * RH:(h + 1) * RH]
            la = (-RG_C) * r * sph
            e2 = _expm1(2.0 * la)
            inv_mult = jnp.where(e2 < 0.0, lax.rsqrt(-e2), 0.0)
            parts[(d, h)] = dict(r=r, i=i, la=la, a=jnp.exp(la), e2=e2, mult=-e2 * inv_mult, inv_mult=inv_mult,
                                 uh=uh, sp=sph)
    return parts


def _rg_fwd_fn(ids, tin, vin):
    (u,) = tin
    wbd = vin[0]
    parts = _rg_gates(u, wbd, vin[1][...], vin[2][...])
    outs = []
    for d in range(2):
        a = jnp.concatenate([parts[(d, h)]["a"] for h in range(2)], axis=1)
        b = jnp.concatenate([parts[(d, h)]["mult"] * parts[(d, h)]["i"] * parts[(d, h)]["uh"]
                             for h in range(2)], axis=1)
        outs += [a, b]
    return outs, []


def _rg_bwd_fn(ids, tin, vin):
    u, da_f, db_f, da_r, db_r = tin
    wbd, lam = vin[0], vin[2][...]
    parts = _rg_gates(u, wbd, vin[1][...], lam)
    dab = ((da_f, db_f), (da_r, db_r))
    dsig_lam = -1.0 / (1.0 + jnp.exp(lam))
    du_halves, dpre_halves, dlam = [], [], [[None, None], [None, None]]
    for h in range(2):
        du = jnp.zeros_like(parts[(0, h)]["uh"])
        dpre = []
        for d in range(2):
            p = parts[(d, h)]
            da = dab[d][0][:, h * RH:(h + 1) * RH]
            db = dab[d][1][:, h * RH:(h + 1) * RH]
            d_mult = db * p["i"] * p["uh"]
            d_i = db * p["mult"] * p["uh"]
            du = du + db * p["mult"] * p["i"]
            d_la = da * p["a"] - d_mult * (p["e2"] + 1.0) * p["inv_mult"]
            d_r = d_la * ((-RG_C) * p["sp"])
            dlam[d][h] = _sum0(d_la * ((-RG_C) * p["r"])) * dsig_lam[d:d + 1, h * RH:(h + 1) * RH]
            dpre += [d_r * p["r"] * (1.0 - p["r"]), d_i * p["i"] * (1.0 - p["i"])]
        dpre = jnp.concatenate(dpre, axis=1)
        du = du + lax.dot_general(dpre.astype(BF16), wbd[h], _NT, preferred_element_type=F32)
        du_halves.append(du)
        dpre_halves.append(dpre)
    dpre_all = jnp.concatenate(dpre_halves, axis=1)
    dlam_row = jnp.concatenate([dlam[0][0], dlam[0][1], dlam[1][0], dlam[1][1]], axis=1)
    return [dpre_all, jnp.concatenate(du_halves, axis=1)], [_sum0(dpre_all), dlam_row]


def _tile_flags(i, n_tiles, seq_starts):
    starts_here = functools.reduce(jnp.logical_or, [i == s for s in seq_starts])
    ends_here = functools.reduce(jnp.logical_or, [i + 1 == s for s in seq_starts] + [i + 1 == n_tiles])
    return jnp.logical_not(starts_here), jnp.logical_not(ends_here)


def _halo_specs(col0, cw):
    hb = ROW_TILE // HALO
    prev = pl.BlockSpec((HALO, cw), lambda i, c: (jnp.maximum(i * hb - 1, 0), col0 + c))
    cur = pl.BlockSpec((ROW_TILE, cw), lambda i, c: (i, col0 + c))
    return prev, cur, hb


def _window(prev_ref, cur_ref, next_ref, has_prev, has_next):
    prev = jnp.where(has_prev, prev_ref[...], 0.0)
    nxt = jnp.where(has_next, next_ref[...], 0.0)
    return jnp.concatenate([prev, cur_ref[...], nxt], axis=0)


def _tap_reader(win):
    sub = 8
    n = win.shape[0]
    shifted = {0: win}

    def tap(off):
        s = off % sub
        if s not in shifted:
            shifted[s] = pltpu.roll(win, n - s, axis=0)
        return shifted[s][off - s:off - s + ROW_TILE, :]

    return tap


def _dwconv(name, x, col0, w, bias, pad_left, seq_starts, n_ch, cw=256, flip=False, into=None):
    n_rows = x.shape[0]
    n_tiles = n_rows // ROW_TILE
    n_taps = w.shape[0]
    prev_spec, cur_spec, hb = _halo_specs(col0, cw)
    last_hb = n_rows // HALO - 1
    next_spec = pl.BlockSpec((HALO, cw), lambda i, c: (jnp.minimum((i + 1) * hb, last_hb), col0 + c))
    dest, out_col0 = (None, 0) if into is None else into

    def kern(prev_ref, cur_ref, next_ref, w_ref, b_ref, *rest):
        o_ref = rest[-1]
        has_prev, has_next = _tile_flags(pl.program_id(0), n_tiles, seq_starts)
        win = _window(prev_ref, cur_ref, next_ref, has_prev, has_next)
        tap = _tap_reader(win)
        wv = w_ref[...]
        acc = jnp.zeros((ROW_TILE, cw), F32) + b_ref[...]
        for k in range(n_taps):
            kw = n_taps - 1 - k if flip else k
            acc = acc + wv[kw:kw + 1, :] * tap(HALO + k - pad_left)
        o_ref[...] = acc.astype(o_ref.dtype)

    return _pcall(
        kern, name=name, grid=(n_tiles, n_ch // cw),
        in_specs=[prev_spec, cur_spec, next_spec,
                  pl.BlockSpec((n_taps, cw), lambda i, c: (0, c)), pl.BlockSpec((1, cw), lambda i, c: (0, c))]
        + ([] if dest is None else [ANY]),
        out_specs=pl.BlockSpec((ROW_TILE, cw), lambda i, c: (i, out_col0 + c)),
        out_shape=_sds((n_rows, n_ch), F32) if dest is None else _sds(dest.shape, dest.dtype),
        input_output_aliases={} if dest is None else {5: 0}, compiler_params=_cparams(),
    )(x, x, x, w, bias, *([] if dest is None else [dest]))


def _dwconv_wgrad(name, dy, x, col0, n_taps, pad_left, seq_starts, n_ch, cw=256, dep=None):
    deps = [] if dep is None else [dep]
    n_rows = dy.shape[0]
    n_tiles = n_rows // ROW_TILE
    n_out = -(-(n_taps + 1) // 8) * 8
    prev_spec, cur_spec, hb = _halo_specs(col0, cw)
    last_hb = n_rows // HALO - 1
    next_spec = pl.BlockSpec((HALO, cw), lambda c, i: (jnp.minimum((i + 1) * hb, last_hb), col0 + c))
    prev_spec = pl.BlockSpec((HALO, cw), lambda c, i: (jnp.maximum(i * hb - 1, 0), col0 + c))
    cur_spec = pl.BlockSpec((ROW_TILE, cw), lambda c, i: (i, col0 + c))

    def kern(dy_ref, prev_ref, cur_ref, next_ref, *rest):
        o_ref = rest[-1]
        i = pl.program_id(1)
        has_prev, has_next = _tile_flags(i, n_tiles, seq_starts)
        win = _window(prev_ref, cur_ref, next_ref, has_prev, has_next)
        dyv = dy_ref[...]
        tap = _tap_reader(win)
        rid = lax.broadcasted_iota(jnp.int32, (n_out, cw), 0)
        inc = jnp.where(rid == n_taps, _sum0(dyv), 0.0)
        for k in range(n_taps):
            inc = inc + jnp.where(rid == k, _sum0(dyv * tap(HALO + k - pad_left)), 0.0)

        @pl.when(i == 0)
        def _():
            o_ref[...] = jnp.zeros_like(o_ref)

        o_ref[...] += inc

    return _pcall(
        kern, name=name, grid=(n_ch // cw, n_tiles),
        in_specs=[pl.BlockSpec((ROW_TILE, cw), lambda c, i: (i, c)), prev_spec, cur_spec, next_spec]
        + [pl.BlockSpec(d.shape, lambda c, i: (0, 0)) for d in deps],
        out_specs=pl.BlockSpec((n_out, cw), lambda c, i: (0, c)),
        out_shape=_sds((n_out, n_ch), F32), compiler_params=_cparams(),
    )(dy, x, x, x, *deps)


N_SCAN = TA // ROW_TILE


def _rev_block(j):
    return jnp.where(j == 0, 0, N_SCAN - j)


def _scan_fwd(a_f, b_f, a_r, b_r):
    fwd_spec = pl.BlockSpec((ROW_TILE, R), lambda i: (i, 0))
    rev_spec = pl.BlockSpec((ROW_TILE, R), lambda i: (_rev_block(i), 0))
    hin_spec = pl.BlockSpec((None, 1, R), lambda i: (i, 0, 0))

    def kern(af, bf, ar, br, yf, yr, hin_f, hin_r, hf_s, hr_s):
        @pl.when(pl.program_id(0) == 0)
        def _():
            hf_s[...] = jnp.zeros_like(hf_s)
            hr_s[...] = jnp.zeros_like(hr_s)

        hin_f[...] = hf_s[...]
        hin_r[...] = hr_s[...]

        def step(s8, carry):
            hf, hr = carry
            t0 = pl.multiple_of(s8 * 8, 8)
            for q in range(8):
                tf = t0 + q
                hf = af[pl.ds(tf, 1), :] * hf + bf[pl.ds(tf, 1), :]
                yf[pl.ds(tf, 1), :] = hf
                tr = ROW_TILE - 1 - tf
                hr = ar[pl.ds(tr, 1), :] * hr + br[pl.ds(tr, 1), :]
                yr[pl.ds(tr, 1), :] = hr
            return hf, hr

        hf, hr = lax.fori_loop(0, ROW_TILE // 8, step, (hf_s[...], hr_s[...]))
        hf_s[...] = hf
        hr_s[...] = hr

    return _pcall(
        kern, name="scan_fwd", grid=(N_SCAN,),
        in_specs=[fwd_spec, fwd_spec, rev_spec, rev_spec],
        out_specs=[fwd_spec, rev_spec, hin_spec, hin_spec],
        out_shape=[_sds((TA, R), F32), _sds((TA, R), F32), _sds((N_SCAN, 1, R), F32), _sds((N_SCAN, 1, R), F32)],
        scratch_shapes=[pltpu.VMEM((1, R), F32), pltpu.VMEM((1, R), F32)], compiler_params=_cparams(),
    )(a_f, b_f, a_r, b_r)


def _scan_bwd(dy, a_f, y_f, hin_f, a_r, y_r, hin_r):
    fwd_spec = pl.BlockSpec((ROW_TILE, R), lambda i: (N_SCAN - 1 - i, 0))
    rev_spec = pl.BlockSpec((ROW_TILE, R), lambda i: (_rev_block(N_SCAN - 1 - i), 0))
    hin_spec = pl.BlockSpec((None, 1, R), lambda i: (N_SCAN - 1 - i, 0, 0))
    last = ROW_TILE - 1

    def kern(dyf, af, yf, hf0, dyr, ar, yr, hr0, daf, dbf, dar, dbr, gf_s, anf_s, gr_s, anr_s):
        @pl.when(pl.program_id(0) == 0)
        def _():
            for r in (gf_s, anf_s, gr_s, anr_s):
                r[...] = jnp.zeros_like(r)

        def one(dy_ref, a_ref, y_ref, da_ref, db_ref, g, an, p, pprev):
            gnew = dy_ref[pl.ds(p, 1), :] + an * g
            db_ref[pl.ds(p, 1), :] = gnew
            da_ref[pl.ds(p, 1), :] = gnew * y_ref[pl.ds(pprev, 1), :]
            return gnew, a_ref[pl.ds(p, 1), :]

        def step(s8, carry):
            gf, anf, gr, anr = carry
            base = s8 * 8
            for q in range(8):
                s = last - (base + q)
                gf, anf = one(dyf, af, yf, daf, dbf, gf, anf, s, s - 1)
                gr, anr = one(dyr, ar, yr, dar, dbr, gr, anr, last - s, last - s + 1)
            return gf, anf, gr, anr

        carry = (gf_s[...], anf_s[...], gr_s[...], anr_s[...])
        carry = lax.fori_loop(0, ROW_TILE // 8 - 1, step, carry)
        gf, anf, gr, anr = carry
        for s in range(7, 0, -1):
            gf, anf = one(dyf, af, yf, daf, dbf, gf, anf, s, s - 1)
            gr, anr = one(dyr, ar, yr, dar, dbr, gr, anr, last - s, last - s + 1)
        gf0 = dyf[0:1, :] + anf * gf
        dbf[0:1, :] = gf0
        daf[0:1, :] = gf0 * hf0[...]
        gr0 = dyr[last:last + 1, :] + anr * gr
        dbr[last:last + 1, :] = gr0
        dar[last:last + 1, :] = gr0 * hr0[...]
        gf_s[...] = gf0
        anf_s[...] = af[0:1, :]
        gr_s[...] = gr0
        anr_s[...] = ar[last:last + 1, :]

    return _pcall(
        kern, name="scan_bwd", grid=(N_SCAN,),
        in_specs=[fwd_spec, fwd_spec, fwd_spec, hin_spec, rev_spec, rev_spec, rev_spec, hin_spec],
        out_specs=[fwd_spec, fwd_spec, rev_spec, rev_spec],
        out_shape=[_sds((TA, R), F32)] * 4,
        scratch_shapes=[pltpu.VMEM((1, R), F32)] * 4, compiler_params=_cparams(),
    )(dy, a_f, y_f, hin_f, dy, a_r, y_r, hin_r)


def _me():
    return lax.axis_index("x"), lax.axis_index("y"), lax.axis_index("c")


def _other_chips(mx, my):
    return [(1 - mx, my), (mx, 1 - my), (1 - mx, 1 - my)]


def _rcopy(src, dst, ssem, rsem, dev):
    return pltpu.make_async_remote_copy(src_ref=src, dst_ref=dst, send_sem=ssem, recv_sem=rsem,
                                        device_id=dev, device_id_type=MESH)


def _peers7(mx, my, mc):
    peers = []
    for k in range(1, 8):
        peers.append((1 - mx if (k >> 2) & 1 else mx, 1 - my if (k >> 1) & 1 else my, 1 - mc if k & 1 else mc))
    return peers


def _share_halves(name, fulls):
    n = len(fulls)

    def kern(*refs):
        o = refs[n:2 * n]
        ss, rs = refs[2 * n:]
        mx, my, mc = _me()
        sib = (mx, my, 1 - mc)
        sends = []
        for t in range(n):
            cp = _rcopy(o[t].at[mc], o[t].at[mc], ss.at[t], rs.at[t], sib)
            cp.start()
            sends.append(cp)
        for t in range(n):
            _rcopy(o[t].at[1 - mc], o[t].at[1 - mc], ss.at[t], rs.at[t], sib).wait_recv()
        for cp in sends:
            cp.wait_send()

    dma = pltpu.SemaphoreType.DMA
    return _pcall(
        kern, name=name, in_specs=[ANY] * n, out_specs=[ANY] * n,
        out_shape=[_sds(f.shape, f.dtype) for f in fulls], input_output_aliases={t: t for t in range(n)},
        scratch_shapes=[dma((n,)), dma((n,))],
    )(*fulls)


def _share_start(name, fulls, after):
    n = len(fulls)

    def kern(*refs):
        o = refs[n + 1:2 * n + 1]
        ssem, rsem, token = refs[2 * n + 1:]
        mx, my, mc = _me()
        for t in range(n):
            _rcopy(o[t].at[mc], o[t].at[mc], ssem.at[t], rsem.at[t], (mx, my, 1 - mc)).start()
        token[...] = jnp.zeros_like(token)

    dma = pltpu.SemaphoreType.DMA
    res = _pcall(
        kern, name=name, in_specs=[ANY] * (n + 1),
        out_specs=[ANY] * n + [SEM, SEM, pl.BlockSpec(memory_space=pltpu.VMEM)],
        out_shape=[_sds(f.shape, f.dtype) for f in fulls] + [dma((n,)), dma((n,)), _sds((8, LANE), F32)],
        input_output_aliases={t: t for t in range(n)},
        compiler_params=pltpu.CompilerParams(has_side_effects=_DATAFLOW),
    )(*fulls, after)
    return (list(res[:n]), res[n], res[n + 1]), res[n + 2]


def _share_wait(name, fulls, ssem, rsem, after):
    n = len(fulls)

    def kern(*refs):
        o = refs[:n]
        ssem_ref, rsem_ref = refs[n], refs[n + 1]
        mx, my, mc = _me()
        sib = (mx, my, 1 - mc)
        for t in range(n):
            _rcopy(o[t].at[1 - mc], o[t].at[1 - mc], ssem_ref.at[t], rsem_ref.at[t], sib).wait_recv()
            _rcopy(o[t].at[mc], o[t].at[mc], ssem_ref.at[t], rsem_ref.at[t], sib).wait_send()

    return list(_pcall(
        kern, name=name, in_specs=[ANY] * n + [SEM, SEM, ANY], out_specs=[ANY] * n,
        out_shape=[_sds(f.shape, f.dtype) for f in fulls], input_output_aliases={t: t for t in range(n)},
        compiler_params=pltpu.CompilerParams(has_side_effects=_DATAFLOW),
    )(*fulls, ssem, rsem, after))


def _tiled_sp(name, fn, grid, sp, ins, outs):
    n_in = len(ins)

    def kern(sp_ref, *refs):
        tout = fn([r[...] for r in refs[:n_in]])
        for r, v in zip(refs[n_in:], tout):
            r[...] = v.astype(r.dtype)

    gs = pltpu.PrefetchScalarGridSpec(num_scalar_prefetch=1, grid=tuple(grid),
                                      in_specs=[s for _, s in ins], out_specs=[s for _, s in outs])
    res = _pcall(kern, name=name, grid_spec=gs, out_shape=[o for o, _ in outs], compiler_params=_cparams(),
                 )(sp, *[a for a, _ in ins])
    return list(res)


def _row_tile(rows, cols, itemsize=4, budget=2 * 1024 * 1024):
    tr = rows
    while tr * cols * itemsize > budget and tr % 32 == 0:
        tr //= 2
    return tr


def _place_big(shards, place, dep=None):
    slots = []
    for tag, s, layer in shards:
        rr, cc = s.shape[2], s.shape[3]
        tr = _row_tile(rr, cc)
        (slot,) = _tiled_sp(
            f"place_{tag}", lambda tin: [tin[0]], (2, rr // tr), place,
            [(s, pl.BlockSpec((None, None, tr, cc), lambda h, i, sp, layer=layer: (layer, h, i, 0)))]
            + [(d, pl.BlockSpec(d.shape, lambda h, i, sp: (0, 0))) for d in _behind(dep)],
            [(_sds((4, 2, rr, cc), BF16), pl.BlockSpec((None, None, tr, cc), lambda h, i, sp: (sp[0], h, i, 0)))])
        slots.append(slot)
    return slots


def _allreduce_small_begin(vec, place, after):
    hr = vec.shape[0] // 2
    tr = _row_tile(hr, LANE)
    blk = (None, None, tr, LANE)
    (pair,) = _tiled_sp(
        "small_place", lambda tin: [tin[0]], (2, hr // tr), place,
        [(vec.reshape(2, hr, LANE), pl.BlockSpec((None, tr, LANE), lambda h, i, sp: (h, i, 0)))],
        [(_sds((2, 2, hr, LANE), F32), pl.BlockSpec(blk, lambda h, i, sp: (sp[1], h, i, 0)))])
    (pair,) = _share_halves("small_share", [pair])
    (slot,) = _tiled_sp(
        "small_pair_add", lambda tin: [tin[0] + tin[1]], (2, hr // tr), place,
        [(pair, pl.BlockSpec(blk, lambda h, i, sp: (0, h, i, 0))),
         (pair, pl.BlockSpec(blk, lambda h, i, sp: (1, h, i, 0)))],
        [(_sds((4, 2, hr, LANE), F32), pl.BlockSpec(blk, lambda h, i, sp: (sp[0], h, i, 0)))])
    fly, sems, token = _gather_start("small_start", [slot], ((0,),), after)
    return (fly, sems), token


def _allreduce_small_end(state, after):
    fly, sems = state
    (chips,) = _swap_halves("small_swap", _gather_wait("small_wait", fly, *sems, after))
    hr = chips.shape[2]
    tr = _row_tile(hr, LANE)
    blk = (None, None, tr, LANE)
    (total,) = _tiled(
        "small_chip_sum", lambda ids, tin, vin: ([((tin[0] + tin[1]) + tin[2]) + tin[3]], []), (2, hr // tr),
        [(chips, pl.BlockSpec(blk, lambda h, i, _j=j: (_j, h, i, 0))) for j in range(4)], [],
        [(_sds((2, hr, LANE), F32), pl.BlockSpec((None, tr, LANE), lambda h, i: (h, i, 0)))])
    return total.reshape(2 * hr, LANE)


SEM =pl.BlockSpec(memory_space=pltpu.SEMAPHORE)
_DATAFLOW = pltpu.SideEffectType.DATAFLOW_SIDE_EFFECTING


def _gather_start(name, slots, groups, after):
    n = len(slots)

    def kern(*refs):
        o = refs[n + 1:2 * n + 1]
        sems, token = refs[2 * n + 1:-1], refs[-1]
        mx, my, mc = _me()
        j0 = 2 * mx + my
        for gi, grp in enumerate(groups):
            for k, t in enumerate(grp):
                for q, (qx, qy) in enumerate(_other_chips(mx, my)):
                    _rcopy(o[t].at[j0, mc], o[t].at[j0, mc], sems[2 * gi].at[3 * k + q],
                           sems[2 * gi + 1].at[3 * k + q], (qx, qy, mc)).start()
        token[...] = jnp.zeros_like(token)

    sem_shapes = []
    for grp in groups:
        sem_shapes += [pltpu.SemaphoreType.DMA((3 * len(grp),))] * 2
    res = _pcall(
        kern, name=name, in_specs=[ANY] * (n + 1),
        out_specs=[ANY] * n + [SEM] * len(sem_shapes) + [pl.BlockSpec(memory_space=pltpu.VMEM)],
        out_shape=[_sds(w.shape, w.dtype) for w in slots] + sem_shapes + [_sds((8, LANE), F32)],
        input_output_aliases={t: t for t in range(n)},
        compiler_params=pltpu.CompilerParams(has_side_effects=_DATAFLOW),
    )(*slots, after)
    return list(res[:n]), list(res[n:-1]), res[-1]


def _gather_wait(name, bufs, ssem, rsem, after):
    n = len(bufs)

    def kern(*refs):
        b = refs[:n]
        ssem_ref, rsem_ref = refs[n], refs[n + 1]
        mx, my, mc = _me()
        j0 = 2 * mx + my
        for k in range(n):
            for q, (qx, qy) in enumerate(_other_chips(mx, my)):
                jq = 2 * qx + qy
                _rcopy(b[k].at[jq, mc], b[k].at[jq, mc], ssem_ref.at[3 * k + q], rsem_ref.at[3 * k + q],
                       (qx, qy, mc)).wait_recv()
                _rcopy(b[k].at[j0, mc], b[k].at[j0, mc], ssem_ref.at[3 * k + q], rsem_ref.at[3 * k + q],
                       (qx, qy, mc)).wait_send()

    return list(_pcall(
        kern, name=name, in_specs=[ANY] * n + [SEM, SEM, ANY], out_specs=[ANY] * n,
        out_shape=[_sds(w.shape, w.dtype) for w in bufs], input_output_aliases={k: k for k in range(n)},
        compiler_params=pltpu.CompilerParams(has_side_effects=_DATAFLOW),
    )(*bufs, ssem, rsem, after))


def _swap_halves(name, bufs):
    n = len(bufs)

    def kern(*refs):
        o = refs[n:2 * n]
        ss, rs = refs[2 * n:]
        mx, my, mc = _me()
        sib = (mx, my, 1 - mc)
        sends = []
        for k in range(n):
            for q, (qx, qy) in enumerate(_other_chips(mx, my)):
                jq = 2 * qx + qy
                cp = _rcopy(o[k].at[jq, mc], o[k].at[jq, mc], ss.at[3 * k + q], rs.at[3 * k + q], sib)
                cp.start()
                sends.append(cp)
        for k in range(n):
            for q, (qx, qy) in enumerate(_other_chips(mx, my)):
                jq = 2 * qx + qy
                _rcopy(o[k].at[jq, 1 - mc], o[k].at[jq, 1 - mc], ss.at[3 * k + q], rs.at[3 * k + q], sib).wait_recv()
        for cp in sends:
            cp.wait_send()

    dma = pltpu.SemaphoreType.DMA
    return list(_pcall(
        kern, name=name, in_specs=[ANY] * n, out_specs=[ANY] * n,
        out_shape=[_sds(w.shape, w.dtype) for w in bufs], input_output_aliases={k: k for k in range(n)},
        scratch_shapes=[dma((3 * n,)), dma((3 * n,))],
    )(*bufs))


def _swap_start(name, bufs, after):
    n = len(bufs)

    def kern(*refs):
        o = refs[n + 1:2 * n + 1]
        ssem, rsem, token = refs[2 * n + 1:]
        mx, my, mc = _me()
        for k in range(n):
            for q, (qx, qy) in enumerate(_other_chips(mx, my)):
                jq = 2 * qx + qy
                _rcopy(o[k].at[jq, mc], o[k].at[jq, mc], ssem.at[3 * k + q], rsem.at[3 * k + q], (mx, my, 1 - mc)).start()
        token[...] = jnp.zeros_like(token)

    dma = pltpu.SemaphoreType.DMA
    res = _pcall(
        kern, name=name, in_specs=[ANY] * (n + 1),
        out_specs=[ANY] * n + [SEM, SEM, pl.BlockSpec(memory_space=pltpu.VMEM)],
        out_shape=[_sds(w.shape, w.dtype) for w in bufs] + [dma((3 * n,)), dma((3 * n,)), _sds((8, LANE), F32)],
        input_output_aliases={k: k for k in range(n)},
        compiler_params=pltpu.CompilerParams(has_side_effects=_DATAFLOW),
    )(*bufs, after)
    return (list(res[:n]), res[n], res[n + 1]), res[n + 2]


def _swap_wait(name, bufs, ssem, rsem, after):
    n = len(bufs)

    def kern(*refs):
        b = refs[:n]
        ssem_ref, rsem_ref = refs[n], refs[n + 1]
        mx, my, mc = _me()
        sib = (mx, my, 1 - mc)
        for k in range(n):
            for q, (qx, qy) in enumerate(_other_chips(mx, my)):
                jq = 2 * qx + qy
                _rcopy(b[k].at[jq, 1 - mc], b[k].at[jq, 1 - mc], ssem_ref.at[3 * k + q], rsem_ref.at[3 * k + q],
                       sib).wait_recv()
                _rcopy(b[k].at[jq, mc], b[k].at[jq, mc], ssem_ref.at[3 * k + q], rsem_ref.at[3 * k + q],
                       sib).wait_send()

    return list(_pcall(
        kern, name=name, in_specs=[ANY] * n + [SEM, SEM, ANY], out_specs=[ANY] * n,
        out_shape=[_sds(w.shape, w.dtype) for w in bufs], input_output_aliases={k: k for k in range(n)},
        compiler_params=pltpu.CompilerParams(has_side_effects=_DATAFLOW),
    )(*bufs, ssem, rsem, after))


def _to_sibling(mx, my, mc):
    return [((j, 1 - mc), j, (mx, my, 1 - mc)) for j in range(4)]


def _to_chips(mx, my, mc):
    return [((2 * qx + qy,), q, (qx, qy, mc)) for q, (qx, qy) in enumerate(_other_chips(mx, my))]


def _to_all7(mx, my, mc):
    return [((0,), k, dev) for k, dev in enumerate(_peers7(mx, my, mc))]


def _send_start(name, srcs, plan, land_shapes, after):
    n = len(srcs)
    per = len(plan(0, 0, 0))

    def kern(*refs):
        s, land = refs[n + 1:2 * n + 1], refs[2 * n + 1:3 * n + 1]
        ssem, rsem, token = refs[3 * n + 1:]
        for k in range(n):
            for q, (idx, slot, dev) in enumerate(plan(*_me())):
                _rcopy(s[k].at[idx], land[k].at[slot], ssem.at[per * k + q], rsem.at[per * k + q], dev).start()
        token[...] = jnp.zeros_like(token)

    dma = pltpu.SemaphoreType.DMA
    res = _pcall(
        kern, name=name, in_specs=[ANY] * (n + 1),
        out_specs=[ANY] * (2 * n) + [SEM, SEM, pl.BlockSpec(memory_space=pltpu.VMEM)],
        out_shape=[_sds(s.shape, s.dtype) for s in srcs] + [_sds(ls, s.dtype) for ls, s in zip(land_shapes, srcs)]
        + [dma((per * n,)), dma((per * n,)), _sds((8, LANE), F32)],
        input_output_aliases={k: k for k in range(n)},
        compiler_params=pltpu.CompilerParams(has_side_effects=_DATAFLOW),
    )(*srcs, after)
    return (list(res[:n]), list(res[n:2 * n]), res[2 * n], res[2 * n + 1]), res[2 * n + 2]


def _send_wait(name, srcs, lands, ssem, rsem, plan, after):
    n = len(srcs)
    per = len(plan(0, 0, 0))

    def kern(*refs):
        s, land = refs[:n], refs[n:2 * n]
        ssem_ref, rsem_ref = refs[2 * n], refs[2 * n + 1]
        for k in range(n):
            for q, (idx, slot, dev) in enumerate(plan(*_me())):
                cp = _rcopy(s[k].at[idx], land[k].at[slot], ssem_ref.at[per * k + q], rsem_ref.at[per * k + q], dev)
                cp.wait_recv()
                cp.wait_send()

    res = _pcall(
        kern, name=name, in_specs=[ANY] * (2 * n) + [SEM, SEM, ANY], out_specs=[ANY] * (2 * n),
        out_shape=[_sds(a.shape, a.dtype) for a in list(srcs) + list(lands)],
        input_output_aliases={k: k for k in range(2 * n)},
        compiler_params=pltpu.CompilerParams(has_side_effects=_DATAFLOW),
    )(*srcs, *lands, ssem, rsem, after)
    return list(res[:n]), list(res[n:])


def _reduce_begin(tag, parts, after):
    return _send_start(f"pair_start_{tag}", parts, _to_sibling, [(4,) + p.shape[2:] for p in parts], after)


def _reduce_mid(tag, pairing, place, after):
    parts, theirs = _send_wait(f"pair_wait_{tag}", *pairing, _to_sibling, after)
    sums = []
    for k, (p, o) in enumerate(zip(parts, theirs)):
        rr, cc = p.shape[2], p.shape[3]
        tr = _row_tile(rr, cc)
        (s_k,) = _tiled_sp(
            f"pair_add_{tag}{k}", lambda tin: [tin[0].astype(F32) + tin[1].astype(F32)], (4, rr // tr), place,
            [(p, pl.BlockSpec((None, None, tr, cc), lambda j, i, sp: (j, sp[1], i, 0))),
             (o, pl.BlockSpec((None, tr, cc), lambda j, i, sp: (j, i, 0)))],
            [(_sds((4, rr, cc), BF16), pl.BlockSpec((None, tr, cc), lambda j, i, sp: (j, i, 0)))])
        sums.append(s_k)
    return _send_start(f"chips_start_{tag}", sums, _to_chips, [(3,) + s.shape[1:] for s in sums], theirs[0])


def _reduce_end(tag, flying, place, after):
    sums, lands = _send_wait(f"chips_wait_{tag}", *flying, _to_chips, after)
    fulls = []
    for k, (s, q) in enumerate(zip(sums, lands)):
        rr, cc = q.shape[1], q.shape[2]
        tr = _row_tile(rr, cc)

        def add4(tin):
            return [((tin[0].astype(F32) + tin[1].astype(F32)) + tin[2].astype(F32)) + tin[3].astype(F32)]

        ins = [(s, pl.BlockSpec((None, tr, cc), lambda i, sp: (sp[0], i, 0)))]
        ins += [(q, pl.BlockSpec((None, tr, cc), lambda i, sp, _k=kk: (_k, i, 0))) for kk in range(3)]
        (f_k,) = _tiled_sp(f"chip_add_{tag}{k}", add4, (rr // tr,), place, ins,
                           [(_sds((2, rr, cc), F32), pl.BlockSpec((None, tr, cc), lambda i, sp: (sp[1], i, 0)))])
        fulls.append(f_k)
    return fulls


def _pack(parts, PACK_ROWS=PACK_ROWS):
    flat, offs, pos = [], [], 0
    for p in parts:
        v = p.reshape(-1).astype(F32)
        n = -(-v.shape[0] // LANE) * LANE
        flat.append(jnp.pad(v, (0, n - v.shape[0])))
        offs.append((pos, v.shape[0], p.shape))
        pos += n
    total = -(-pos // (PACK_ROWS * LANE)) * PACK_ROWS * LANE
    flat.append(jnp.zeros((total - pos,), F32))
    return jnp.concatenate(flat).reshape(-1, LANE), offs


def _unpack(vec, offs):
    v = vec.reshape(-1)
    return [v[p:p + n].reshape(shape) for p, n, shape in offs]


def _adamw_math(wv, gv, mv, vv):
    bc1 = 1.0 - ADAM_B1 ** ADAM_STEP
    bc2 = 1.0 - ADAM_B2 ** ADAM_STEP
    mn = ADAM_B1 * mv + (1.0 - ADAM_B1) * gv
    vn = ADAM_B2 * vv + (1.0 - ADAM_B2) * (gv * gv)
    delta = -ADAM_LR * ((mn / bc1) / (jnp.sqrt(vn / bc2) + ADAM_EPS) + ADAM_WD * wv)
    return delta, mn, vn


def _adamw(name, w, g, m, v, dep=None):
    rows, cols = w.shape
    tr = rows
    for cand in (512, 256, 128, 64, 32, 16, 8):
        if rows % cand == 0 and cand * cols * 4 <= 2 * 1024 * 1024:
            tr = cand
            break

    def fn(ids, tin, vin):
        return list(_adamw_math(*tin)), []

    spec = pl.BlockSpec((tr, cols), lambda i: (i, 0))
    outs = [(_sds((rows, cols), F32), spec)] * 3
    return _tiled(name, fn, (rows // tr,), [(a, spec) for a in (w, g, m, v)], _behind(dep), outs)


def _adamw_many(name, ws, gs, ms, vs):
    n = len(ws)
    views = [(-1, a.shape[-1]) if a.ndim > 1 else (1, -1) for a in ws]
    flat = lambda arrs: [a.reshape(vw) for a, vw in zip(arrs, views)]

    def kern(*refs):
        ins, outs = refs[:4 * n], refs[4 * n:]
        for t in range(n):
            res = _adamw_math(*[ins[q * n + t][...] for q in range(4)])
            for q in range(3):
                outs[q * n + t][...] = res[q]

    shapes = [_sds(a.shape, F32) for a in flat(ws)]
    res = _pcall(kern, name=name, out_shape=shapes * 3, compiler_params=_cparams(),
                 )(*flat(ws), *flat(gs), *flat(ms), *flat(vs))
    back = lambda part: [a.reshape(w.shape) for a, w in zip(part, ws)]
    return back(res[:n]), back(res[n:2 * n]), back(res[2 * n:])


def _pos_embed():
    n_rows = T // GRID_W
    q = D // 4
    omega = 1.0 / (10000.0 ** (jnp.arange(q, dtype=F32) / q))
    er = jnp.arange(n_rows, dtype=jnp.int32).astype(F32)[:, None] * omega[None, :]
    ec = jnp.arange(GRID_W, dtype=jnp.int32).astype(F32)[:, None] * omega[None, :]
    by_row = jnp.concatenate([jnp.sin(er), jnp.cos(er)], axis=-1)
    by_col = jnp.concatenate([jnp.sin(ec), jnp.cos(ec)], axis=-1)
    return jnp.concatenate([jnp.repeat(by_row, GRID_W, axis=0), jnp.tile(by_col, (n_rows, 1))], axis=-1)


def _dense_gates(w_a, w_x):
    rows = jnp.stack([w_a[0], w_x[0], w_a[1], w_x[1]]).reshape(4, 2, RH, BLK)
    mask, spread = _block_mask(), _block_spread().T.astype(BF16)

    def kern(r_ref, m_ref, s_ref, o_ref):
        tiled = jnp.dot(r_ref[...].astype(BF16), s_ref[...], preferred_element_type=F32)
        o_ref[...] = (tiled * m_ref[...]).astype(o_ref.dtype)

    return _pcall(
        kern, name="gates_dense", grid=(2, 4),
        in_specs=[pl.BlockSpec((None, None, RH, BLK), lambda h, q: (q, h, 0, 0)),
                  pl.BlockSpec((RH, RH), lambda h, q: (0, 0)), pl.BlockSpec((BLK, RH), lambda h, q: (0, 0))],
        out_specs=pl.BlockSpec((None, RH, RH), lambda h, q: (h, 0, q)),
        out_shape=_sds((2, RH, NQ), BF16),
    )(rows, mask, spread)


def _block_mask():
    r = lax.broadcasted_iota(jnp.int32, (RH, RH), 0) // BLK
    c = lax.broadcasted_iota(jnp.int32, (RH, RH), 1) // BLK
    return (r == c).astype(F32)


def _block_spread():
    c = lax.broadcasted_iota(jnp.int32, (RH, BLK), 0) % BLK
    j = lax.broadcasted_iota(jnp.int32, (RH, BLK), 1)
    return (c == j).astype(F32)


def _fold_blocks(dense, mask, spread):
    return jnp.dot(dense * mask, spread, preferred_element_type=F32, precision=lax.Precision.HIGHEST)


def _gate_block_grads(folded):
    per = N_BLK // 2
    kinds = [jnp.concatenate([folded[h, q].reshape(per, BLK, BLK) for h in range(2)], axis=0) for q in range(4)]
    return jnp.stack([kinds[0], kinds[2]]), jnp.stack([kinds[1], kinds[3]])


def _gate_bias_dense(b_a, b_x):
    cols = []
    for h in range(2):
        for src in (b_a[0], b_x[0], b_a[1], b_x[1]):
            cols.append(src.reshape(R)[h * RH:(h + 1) * RH])
    return jnp.concatenate(cols).reshape(1, 2 * NQ)


def _gate_bias_grads(dgb):
    v = dgb.reshape(2, 4, RH)
    kinds = [jnp.concatenate([v[0, q], v[1, q]]).reshape(N_BLK, BLK) for q in range(4)]
    return jnp.stack([kinds[0], kinds[2]]), jnp.stack([kinds[1], kinds[3]])


def _residual_epilogue(next_norm):
    def epi(acc, ex):
        x_new = ex[0] + ex[1] * acc
        outs = [acc, x_new]
        if next_norm:
            outs.append(_norm_mod(x_new, ex[-3], ex[-2], ex[-1]))
        return outs
    return epi


def _mlp_fwd(tag, x_in, h, gate, w_in, w_out, next_norm=None, dep=None):
    tm = MM_TILE
    (r,) = _mm(f"{tag}_in", h, w_in, _NN, (T // tm, 4, 1),
               pl.BlockSpec((tm, D), lambda i, j, k: (i, 0)), pl.BlockSpec((None, D, D), lambda i, j, k: (j, 0, 0)),
               [(_sds((T, FF), BF16), pl.BlockSpec((tm, D), lambda i, j, k: (i, j)))], (tm, D),
               extra=[(d_, _full_spec(d_)) for d_ in _behind(dep)], epi=lambda acc, ex: [jnp.maximum(acc, 0.0)])
    row_spec = pl.BlockSpec((tm, D), lambda i, j, k: (i, 0))
    outs = [(_sds((T, D), F32), row_spec)] * 2 + ([(_sds((T, D), BF16), row_spec)] if next_norm else [])
    res = _mm(f"{tag}_out", r, w_out, _NN, (T // tm, 1, FF // D),
              pl.BlockSpec((tm, D), lambda i, j, k: (i, k)), pl.BlockSpec((D, D), lambda i, j, k: (k, 0)),
              outs, (tm, D),
              extra=[(x_in, row_spec), (gate, _full_spec(gate))] + [(v, _full_spec(v)) for v in next_norm or ()],
              a_pre=lambda a: a * a, epi=_residual_epilogue(next_norm))
    return dict(h=h, r=r, o=res[0], x_in=x_in), res[1], (res[2] if next_norm else None)


def _behind(dep):
    return [] if dep is None else [dep]


def _gate_bwd(tag, dx, o, gate, dep=None):
    def fn(ids, t, v):
        d_o = t[0] * v[0]
        return [d_o], [_sum0(t[0] * t[1]), _sum0(d_o)]
    return _tiled(f"{tag}_gate_bwd", fn, (T // ROW_TILE,), [_rows(dx), _rows(o)], [gate] + _behind(dep),
                  [_orow(T, D, BF16)], [(1, D), (1, D)])


def _norm_bwd(tag, dx_res, dh, dh_off, x, g_norm, sc, with_dx=True, dep=None):
    n_t = x.shape[0] // ROW_TILE

    def fn(ids, t, v):
        if with_dx:
            dres, dhv, xv = t
        else:
            dhv, xv = t
        dxv, d_sh, d_sc, d_g = _norm_mod_bwd(dhv, xv, v[0], v[1])
        return ([dres + dxv] if with_dx else []), [d_sh, d_sc, d_g]

    ins = ([_rows(dx_res)] if with_dx else []) + [_rows(dh, off=dh_off), _rows(x)]
    outs = [_orow(x.shape[0], D, F32)] if with_dx else []
    return _tiled(f"{tag}_norm_bwd", fn, (n_t,), ins, [g_norm, sc] + _behind(dep), outs, [(1, D)] * 3)


def _mlp_bwd(tag, dx, saved, g_norm, sc, gate, w_in, w_out, dep=None):
    d_o, d_gate, _ = _gate_bwd(tag, dx, saved["o"], gate, dep)
    tm = MM_TILE
    r = saved["r"]
    (da,) = _mm(f"{tag}_dz", d_o, w_out, _NT, (T // tm, FF // D, 1),
                pl.BlockSpec((tm, D), lambda i, j, k: (i, 0)), pl.BlockSpec((D, D), lambda i, j, k: (j, 0)),
                [(_sds((T, FF), BF16), pl.BlockSpec((tm, D), lambda i, j, k: (i, j)))], (tm, D),
                extra=[(r, pl.BlockSpec((tm, D), lambda i, j, k: (i, j)))],
                epi=lambda acc, ex: [acc * (2.0 * ex[0].astype(F32))])
    tk = MM_TILE
    (dw_out,) = _mm(f"{tag}_dwout", r, d_o, _TN, (FF // tm, 1, T // tk),
                    pl.BlockSpec((tk, tm), lambda i, j, k: (k, i)), pl.BlockSpec((tk, D), lambda i, j, k: (k, 0)),
                    [(_sds((FF, D), BF16), pl.BlockSpec((tm, D), lambda i, j, k: (i, 0)))], (tm, D),
                    a_pre=lambda a: a * a)
    (dh,) = _mm(f"{tag}_dh", da, w_in, _NT, (T // tm, 1, 4),
                pl.BlockSpec((tm, D), lambda i, j, k: (i, k)), pl.BlockSpec((None, D, D), lambda i, j, k: (k, 0, 0)),
                [(_sds((T, D), F32), pl.BlockSpec((tm, D), lambda i, j, k: (i, 0)))], (tm, D))
    (dw_in,) = _mm(f"{tag}_dwin", saved["h"], da, _TN, (D // tm, 4, T // tk),
                   pl.BlockSpec((tk, tm), lambda i, j, k: (k, i)), pl.BlockSpec((tk, D), lambda i, j, k: (k, j)),
                   [(_sds((4, D, D), BF16), pl.BlockSpec((None, tm, D), lambda i, j, k: (j, i, 0)))], (tm, D))
    dx_in, d_sh, d_sc, d_g = _norm_bwd(tag, dx, dh, 0, saved["x_in"], g_norm, sc)
    return dx_in, dw_in, dw_out, dict(sh=d_sh, sc=d_sc, gate=d_gate, g_norm=d_g)


def _local_step(x, ctx, tgt, mods, cmods, norm_g, final_g, rec, conf, wg, on_grads=None, wg_pre=None, on_later=None):
    on_grads = on_grads or (lambda group, dws: None)
    wg_pre = wg_pre or (lambda group, after: None)
    on_later = on_later or (lambda after: None)
    n_t = T // ROW_TILE
    row = lambda v: v.reshape(1, -1)
    m0 = [row(mods[0, q]) for q in range(6)]
    m1 = [row(mods[1, q]) for q in range(6)]
    g00, g01, g10, g11 = (row(norm_g[0, 0]), row(norm_g[0, 1]), row(norm_g[1, 0]), row(norm_g[1, 1]))
    csh, csc = row(cmods[0]), row(cmods[1])
    pos = _pos_embed()

    def prep0(ids, t, v):
        cx, xv, pv = t
        is_ctx = ids[0] == 0
        xin = jnp.where(is_ctx, cx, xv + pv)
        sh = jnp.where(is_ctx, v[3], v[1])
        sc = jnp.where(is_ctx, v[4], v[2])
        return [_norm_mod(xin, v[0], sc, sh), xv + pv], []

    dep = wg_pre("rec_in", csh)
    hcat, x0 = _tiled(
        "prep0", prep0, (N_SCAN,),
        [(ctx, pl.BlockSpec((ROW_TILE, D), lambda i: (0, 0))), _rows(x, off=-1, clamp_lo=True),
         _rows(pos, off=-1, clamp_lo=True)],
        [g00, m0[0], m0[1], csh, csc] + _behind(dep),
        [_orow(TA, D, BF16), _orow(T, D, F32, off=-1, clamp_lo=True)])

    tm_a = REC_TILE
    w_rin = wg("rec_in", hcat)["rec_w_in"]
    (a_in,) = _mm("rec_in", hcat, w_rin, _NN, (TA // tm_a, 4, 1),
                  pl.BlockSpec((tm_a, D), lambda i, j, k: (i, 0)),
                  pl.BlockSpec((None, D, RH), lambda i, j, k: (j, 0, 0)),
                  [(_sds((TA, 2 * R), F32), pl.BlockSpec((tm_a, RH), lambda i, j, k: (i, j)))], (tm_a, RH))
    rec_starts = (0, 1)
    u = _dwconv("rec_conv", a_in, R // CW_REC, rec["conv_w"], row(rec["conv_b"]), 1, rec_starts, R, CW_REC)
    wbd = _dense_gates(rec["w_a"], rec["w_x"])
    gbias = _gate_bias_dense(rec["b_a"], rec["b_x"])
    lam = rec["lam"]
    a_f, b_f, a_r, b_r = _tiled("rg_fwd", _rg_fwd_fn, (TA // RG_TILE,), [_rows(u, tm=RG_TILE)], [wbd, gbias, lam],
                                [_orow(TA, R, F32, tm=RG_TILE)] * 4, vec_refs=True)
    dep = wg_pre("rec_out", a_f)
    dep = wg_pre("mlp0", a_f if dep is None else dep)
    y_f, y_r, hin_f, hin_r = _scan_fwd(a_f, b_f, a_r, b_r)

    def rec_mid(ids, t, v):
        gp, yf, yr = t
        g, _ = _gelu(gp)
        return [g * (yf + yr)], []

    (m_rec,) = _tiled("rec_mid", rec_mid, (n_t,),
                      [_rows(a_in, R, off=1), _rows(y_f, off=1), _rows(y_r, off=1)], _behind(dep),
                      [_orow(T, R, BF16)])
    tm = MM_TILE
    row_spec = pl.BlockSpec((tm, D), lambda i, j, k: (i, 0))
    norm_mlp0 = (g01, m0[4], m0[3])
    w_rout = wg("rec_out", m_rec)["rec_w_out"]
    o_rec, x1, h_mlp0 = _mm(
        "rec_out", m_rec, w_rout, _NN, (T // tm, 1, 1),
        pl.BlockSpec((tm, R), lambda i, j, k: (i, 0)), pl.BlockSpec((R, D), lambda i, j, k: (0, 0)),
        [(_sds((T, D), F32), row_spec)] * 2 + [(_sds((T, D), BF16), row_spec)], (tm, D),
        extra=[(x0, row_spec), (m0[2], _full_spec(m0[2]))] + [(v, _full_spec(v)) for v in norm_mlp0],
        epi=_residual_epilogue(norm_mlp0))
    w_m0 = wg("mlp0", x1)
    dep = wg_pre("conf", x1)
    mlp0, x2, h1 = _mlp_fwd("mlp0", x1, h_mlp0, m0[5], w_m0["w_in"], w_m0["w_out"], (g10, m1[1], m1[0]), dep)

    b_pw1 = row(conf["b_pw1"])
    w_cf = wg("conf", x2)
    dep = wg_pre("mlp1", x2)
    (pre,) = _mm("conf_pw1", h1, w_cf["conf_w_pw1"], _NN, (T // tm, 4, 1),
                 pl.BlockSpec((tm, D), lambda i, j, k: (i, 0)),
                 pl.BlockSpec((None, D, D // 2), lambda i, j, k: (j, 0, 0)),
                 [(_sds((T, 2 * D), F32), pl.BlockSpec((tm, D // 2), lambda i, j, k: (i, j)))], (tm, D // 2),
                 extra=[(b_pw1, pl.BlockSpec((1, D // 2), lambda i, j, k: (0, j)))]
                 + [(d_, _full_spec(d_)) for d_ in _behind(dep)],
                 epi=lambda acc, ex: [acc + ex[0]])
    (zg,) = _tiled("conf_glu", lambda ids, t, v: ([t[0] * _sigmoid(t[1])], []), (n_t,),
                   [_rows(pre, D, col=0), _rows(pre, D, col=1)], [], [_orow(T, D, F32)])
    conf_starts = (0,)
    zc = _dwconv("conf_conv", zg, 0, conf["conv_w"], row(conf["conv_b"]), CONF_KW // 2, conf_starts, D, CW_CONF)
    ln_g, ln_b = row(conf["ln_g"]), row(conf["ln_b"])

    def ln_silu(ids, t, v):
        nh, _ = _layernorm_parts(t[0])
        ln = nh * v[0] + v[1]
        return [ln * _sigmoid(ln)], []

    (s_conf,) = _tiled("conf_ln", ln_silu, (n_t,), [_rows(zc)], [ln_g, ln_b], [_orow(T, D, BF16)])
    b_pw2 = row(conf["b_pw2"])
    norm_mlp1 = (g11, m1[4], m1[3])
    pw2_epi = _residual_epilogue(norm_mlp1)
    y_conf, x3, h_mlp1 = _mm(
        "conf_pw2", s_conf, w_cf["conf_w_pw2"], _NN, (T // tm, 1, 1),
        row_spec, pl.BlockSpec((D, D), lambda i, j, k: (0, 0)),
        [(_sds((T, D), F32), row_spec)] * 2 + [(_sds((T, D), BF16), row_spec)], (tm, D),
        extra=[(x2, row_spec), (m1[2], _full_spec(m1[2])), (b_pw2, _full_spec(b_pw2))]
        + [(v, _full_spec(v)) for v in norm_mlp1],
        epi=lambda acc, ex: pw2_epi(acc + ex[2], ex))
    w_m1 = wg("mlp1", x3)
    mlp1, x4, _ = _mlp_fwd("mlp1", x3, h_mlp1, m1[5], w_m1["w_in"], w_m1["w_out"])

    fg = row(final_g)

    def head(ids, t, v):
        n, r = _rms(t[0])
        err = n * v[0] - t[1]
        d_out = err * (1.0 / D)
        dn = d_out * v[0]
        dxv = r * (dn - n * jnp.mean(dn * n, axis=-1, keepdims=True))
        part = jnp.sum(_sum0(err * err), axis=1, keepdims=True) * (0.5 / D)
        return [dxv], [part, _sum0(d_out * n)]

    dx4, loss, d_fg = _tiled("head", head, (n_t,), [_rows(x4), _rows(tgt)], [fg], [_orow(T, D, F32)],
                             [(1, 1), (1, D)])

    dx3, dw_in1, dw_out1, dm_mlp1 = _mlp_bwd("mlp1", dx4, mlp1, g11, m1[4], m1[5],
                                             w_m1["w_in"], w_m1["w_out"])
    dep = on_grads("mlp1", (dw_in1, dw_out1))
    d_y, d_g1c, d_bpw2 = _gate_bwd("conf", dx3, y_conf, m1[2], dep)
    tk = MM_TILE
    (dw_pw2,) = _mm("conf_dwpw2", s_conf, d_y, _TN, (D // tm, 1, T // tk),
                    pl.BlockSpec((tk, tm), lambda i, j, k: (k, i)), pl.BlockSpec((tk, D), lambda i, j, k: (k, 0)),
                    [(_sds((D, D), BF16), pl.BlockSpec((tm, D), lambda i, j, k: (i, 0)))], (tm, D))
    (ds,) = _mm("conf_ds", d_y, w_cf["conf_w_pw2"], _NT, (T // tm, 1, 1),
                pl.BlockSpec((tm, D), lambda i, j, k: (i, 0)), pl.BlockSpec((D, D), lambda i, j, k: (0, 0)),
                [(_sds((T, D), F32), pl.BlockSpec((tm, D), lambda i, j, k: (i, 0)))], (tm, D))
    dep = on_later(ds)

    def ln_silu_bwd(ids, t, v):
        dsv, zcv = t
        nh, rstd = _layernorm_parts(zcv)
        ln = nh * v[0] + v[1]
        sg = _sigmoid(ln)
        d_ln = dsv * (sg * (1.0 + ln * (1.0 - sg)))
        d_nh = d_ln * v[0]
        d_zc = rstd * (d_nh - jnp.mean(d_nh, axis=-1, keepdims=True)
                       - nh * jnp.mean(d_nh * nh, axis=-1, keepdims=True))
        return [d_zc], [_sum0(d_ln * nh), _sum0(d_ln)]

    d_zc, d_lng, d_lnb = _tiled("conf_ln_bwd", ln_silu_bwd, (n_t,), [_rows(ds), _rows(zc)],
                                [ln_g, ln_b] + _behind(dep), [_orow(T, D, F32)], [(1, D), (1, D)])
    d_zg = _dwconv("conf_conv_dx", d_zc, 0, conf["conv_w"], jnp.zeros((1, D), F32),
                   CONF_KW - 1 - CONF_KW // 2, conf_starts, D, CW_CONF, flip=True)

    def glu_bwd(ids, t, v):
        dz, pa, pb = t
        sg = _sigmoid(pb)
        d_a = dz * sg
        d_b = dz * pa * sg * (1.0 - sg)
        return [jnp.concatenate([d_a, d_b], axis=1)], [_sum0(d_a), _sum0(d_b)]

    d_pre, d_b1a, d_b1b = _tiled(
        "conf_glu_bwd", glu_bwd, (n_t,), [_rows(d_zg), _rows(pre, D, col=0), _rows(pre, D, col=1)], [],
        [_orow(T, 2 * D, BF16)], [(1, D), (1, D)])
    (dw_pw1,) = _mm("conf_dwpw1", h1, d_pre, _TN, (D // tm, 4, T // tk),
                    pl.BlockSpec((tk, tm), lambda i, j, k: (k, i)),
                    pl.BlockSpec((tk, D // 2), lambda i, j, k: (k, j)),
                    [(_sds((4, D, D // 2), BF16), pl.BlockSpec((None, tm, D // 2), lambda i, j, k: (j, i, 0)))],
                    (tm, D // 2))
    dep = on_grads("conf", (dw_pw1, dw_pw2))
    (dh1,) = _mm("conf_dh", d_pre, w_cf["conf_w_pw1"], _NT, (T // tm, 1, 4),
                 pl.BlockSpec((tm, D // 2), lambda i, j, k: (i, k)),
                 pl.BlockSpec((None, D, D // 2), lambda i, j, k: (k, 0, 0)),
                 [(_sds((T, D), F32), pl.BlockSpec((tm, D), lambda i, j, k: (i, 0)))], (tm, D))
    dx2, d_sh1c, d_sc1c, d_g10 = _norm_bwd("conf", dx3, dh1, 0, x2, g10, m1[1], dep=dep)
    dep = on_later(dx2)

    dx1, dw_in0, dw_out0, dm_mlp0 = _mlp_bwd("mlp0", dx2, mlp0, g01, m0[4], m0[5],
                                             w_m0["w_in"], w_m0["w_out"], dep)
    dep = on_grads("mlp0", (dw_in0, dw_out0))
    d_orec, d_g1r, _ = _gate_bwd("rec", dx1, o_rec, m0[2], dep)
    (dw_rout,) = _mm("rec_dwout", m_rec, d_orec, _TN, (R // RH, 1, T // tk),
                     pl.BlockSpec((tk, RH), lambda i, j, k: (k, i)), pl.BlockSpec((tk, D), lambda i, j, k: (k, 0)),
                     [(_sds((R, D), BF16), pl.BlockSpec((RH, D), lambda i, j, k: (i, 0)))], (RH, D))
    (dm_rec,) = _mm("rec_dm", d_orec, w_rout, _NT, (T // tm, 1, 1),
                    pl.BlockSpec((tm, D), lambda i, j, k: (i, 0)), pl.BlockSpec((R, D), lambda i, j, k: (0, 0)),
                    [(_sds((T, R), F32), pl.BlockSpec((tm, R), lambda i, j, k: (i, 0)))], (tm, R))
    dep = on_later(dm_rec)

    def rec_mid_bwd(ids, t, v):
        dmv, gp, yf, yr = t
        g, th = _gelu(gp)
        lat = ids[0] > 0
        d_gp = jnp.where(lat, dmv * (yf + yr) * _gelu_grad(gp, th), 0.0)
        dy = jnp.where(lat, dmv * g, 0.0)
        return [d_gp, dy], []

    d_a, dy = _tiled("rec_mid_bwd", rec_mid_bwd, (N_SCAN,),
                     [_rows(dm_rec, off=-1, clamp_lo=True), _rows(a_in, R), _rows(y_f), _rows(y_r)], _behind(dep),
                     [(_sds((TA, 2 * R), BF16), pl.BlockSpec((ROW_TILE, R), lambda i: (i, 0))), _orow(TA, R, F32)])
    da_f, db_f, da_r, db_r = _scan_bwd(dy, a_f, y_f, hin_f, a_r, y_r, hin_r)
    d_gpre, d_u, d_gbias, d_lam = _tiled(
        "rg_bwd", _rg_bwd_fn, (TA // RG_TILE,), [_rows(a, tm=RG_TILE) for a in (u, da_f, db_f, da_r, db_r)],
        [wbd, gbias, lam], [_orow(TA, 2 * NQ, BF16, tm=RG_TILE), _orow(TA, R, F32, tm=RG_TILE)],
        [(1, 2 * NQ), (1, 2 * R)], vec_refs=True)
    tk_a = REC_TILE
    d_a = _dwconv("rec_conv_dx", d_u, 0, rec["conv_w"], jnp.zeros((1, R), F32), REC_KW - 1 - 1,
                  rec_starts, R, CW_REC, flip=True, into=(d_a, R // CW_REC))
    (dw_rin,) = _mm("rec_dwin", hcat, d_a, _TN, (D // tm, 4, TA // tk_a),
                    pl.BlockSpec((tk_a, tm), lambda i, j, k: (k, i)), pl.BlockSpec((tk_a, RH), lambda i, j, k: (k, j)),
                    [(_sds((4, D, RH), BF16), pl.BlockSpec((None, tm, RH), lambda i, j, k: (j, i, 0)))], (tm, RH))
    dep = on_grads("rec", (dw_rin, dw_rout))
    (dhcat,) = _mm("rec_dh", d_a, w_rin, _NT, (TA // tm_a, 1, 4),
                   pl.BlockSpec((tm_a, RH), lambda i, j, k: (i, k)),
                   pl.BlockSpec((None, D, RH), lambda i, j, k: (k, 0, 0)),
                   [(_sds((TA, D), F32), pl.BlockSpec((tm_a, D), lambda i, j, k: (i, 0)))], (tm_a, D))
    dx0, d_sh1r, d_sc1r, d_g00 = _norm_bwd("rec", dx1, dhcat, 1, x0, g00, m0[1], dep=dep)
    dep = on_later(dx0)

    d_csh, d_csc, d_g00c = _norm_bwd("ctx", None, dhcat, 0, ctx, g00, csc, with_dx=False, dep=dep)
    blk_mask, blk_spread = _block_mask(), _block_spread()
    (d_wbd,) = _mm("rg_dw", u, d_gpre, _TN, (2, 2, TA // tk_a),
                   pl.BlockSpec((tk_a, RH), lambda i, j, k: (k, i)),
                   pl.BlockSpec((tk_a, NQ // 2), lambda i, j, k: (k, 2 * i + j)),
                   [(_sds((2, 4, RH, BLK), F32), pl.BlockSpec((None, 2, RH, BLK), lambda i, j, k: (i, j, 0, 0)))],
                   (RH, NQ // 2),
                   extra=[(blk_mask, _full_spec(blk_mask)), (blk_spread, _full_spec(blk_spread))]
                   + [(d, _full_spec(d)) for d in _behind(dep)],
                   epi=lambda acc, ex: [jnp.stack([_fold_blocks(acc[:, s * RH:(s + 1) * RH], ex[0], ex[1])
                                                   for s in range(2)])])
    d_cw_rec = _dwconv_wgrad("rec_conv_dw", d_u, a_in, R // CW_REC, REC_KW, 1, rec_starts, R, CW_REC, dep)
    d_cw_conf = _dwconv_wgrad("conf_conv_dw", d_zc, zg, 0, CONF_KW, CONF_KW // 2, conf_starts, D, CW_CONF, dep)

    big = dict(rec_w_in=dw_rin, rec_w_out=dw_rout, conf_w_pw1=dw_pw1, conf_w_pw2=dw_pw2,
               mlp_w_in=(dw_in0, dw_in1), mlp_w_out=(dw_out0, dw_out1))
    d_wa, d_wx = _gate_block_grads(d_wbd)
    d_ba, d_bx = _gate_bias_grads(d_gbias)
    d_mod = jnp.concatenate([
        d_sh1r, d_sc1r, d_g1r, dm_mlp0["sh"], dm_mlp0["sc"], dm_mlp0["gate"],
        d_sh1c, d_sc1c, d_g1c, dm_mlp1["sh"], dm_mlp1["sc"], dm_mlp1["gate"]], axis=1).reshape(2, 6 * D)
    small = dict(
        d_mod=d_mod, d_cmod=jnp.concatenate([d_csh, d_csc], axis=1),
        norm_g=jnp.concatenate([d_g00 + d_g00c, dm_mlp0["g_norm"], d_g10, dm_mlp1["g_norm"]], axis=1),
        rec_conv_w=d_cw_rec[:REC_KW], rec_conv_b=d_cw_rec[REC_KW], rec_lambda=d_lam.reshape(2, R),
        rec_w_a=d_wa, rec_b_a=d_ba, rec_w_x=d_wx, rec_b_x=d_bx,
        conf_b_pw1=jnp.concatenate([d_b1a, d_b1b], axis=1), conf_conv_w=d_cw_conf[:CONF_KW],
        conf_conv_b=d_cw_conf[CONF_KW], conf_ln_g=d_lng, conf_ln_b=d_lnb, conf_b_pw2=d_bpw2, final_g=d_fg)
    return loss.reshape(()), dx0, big, small


_BIG = ("rec_w_in", "rec_w_out", "conf_w_pw1", "conf_w_pw2", "mlp_w_in", "mlp_w_out")


def _halves(w):
    return w.reshape(w.shape[0], 2, w.shape[1] // 2, w.shape[2])


def _ada_fwd(c16, w_ada, b_shard):
    ns = w_ada.shape[2]
    tn = 512

    def kern(c_ref, w_ref, b_ref, o_ref):
        cv = c_ref[...]
        s = (cv * _sigmoid(cv)).astype(BF16)
        o_ref[...] = jnp.dot(s, w_ref[...].astype(BF16), preferred_element_type=F32) + b_ref[...]

    return _pcall(
        kern, name="ada_fwd", grid=(2, ns // tn),
        in_specs=[pl.BlockSpec((16, D), lambda l, j: (0, 0)), pl.BlockSpec((None, D, tn), lambda l, j: (l, 0, j)),
                  pl.BlockSpec((None, 1, tn), lambda l, j: (l, 0, j))],
        out_specs=pl.BlockSpec((None, 16, tn), lambda l, j: (l, 0, j)),
        out_shape=_sds((2, 16, ns), F32), compiler_params=_cparams(),
    )(c16, w_ada, b_shard)


def _ada_bwd(c16, dm16, w_ada):
    ns = w_ada.shape[2]
    tn = 512

    def kern(c_ref, dm_ref, w_ref, gw_ref, ds_ref):
        cv = c_ref[...]
        s = (cv * _sigmoid(cv)).astype(BF16)
        dm = dm_ref[...].astype(BF16)
        gw_ref[...] = lax.dot_general(s, dm, _TN, preferred_element_type=F32)

        @pl.when(jnp.logical_and(pl.program_id(0) == 0, pl.program_id(1) == 0))
        def _():
            ds_ref[...] = jnp.zeros_like(ds_ref)

        ds_ref[...] += lax.dot_general(dm, w_ref[...].astype(BF16), _NT, preferred_element_type=F32)

    return _pcall(
        kern, name="ada_bwd", grid=(2, ns // tn),
        in_specs=[pl.BlockSpec((16, D), lambda l, j: (0, 0)), pl.BlockSpec((None, 16, tn), lambda l, j: (l, 0, j)),
                  pl.BlockSpec((None, D, tn), lambda l, j: (l, 0, j))],
        out_specs=[pl.BlockSpec((None, D, tn), lambda l, j: (l, 0, j)), pl.BlockSpec((16, D), lambda l, j: (0, 0))],
        out_shape=[_sds((2, D, ns), F32), _sds((16, D), F32)], compiler_params=_cparams(),
    )(c16, dm16, w_ada)


def _cctx_grad(ds4, c_ctx):
    def kern(d_ref, c_ref, o_ref):
        tot = d_ref[0, 0:1, :] + d_ref[1, 0:1, :] + d_ref[2, 0:1, :] + d_ref[3, 0:1, :]
        cv = c_ref[...]
        sg = _sigmoid(cv)
        o_ref[...] = tot * (sg * (1.0 + cv * (1.0 - sg)))

    return _pcall(kern, name="cctx_grad", out_shape=_sds((1, D), F32))(ds4, c_ctx.reshape(1, D))


def kernel(x, c, ctx, c_ctx, w_ada, b_ada, norm_g, rec_w_in, rec_conv_w, rec_conv_b, rec_lambda, rec_w_a, rec_b_a, rec_w_x, rec_b_x, rec_w_out, conf_w_pw1, conf_b_pw1, conf_conv_w, conf_conv_b, conf_ln_g, conf_ln_b, conf_w_pw2, conf_b_pw2, mlp_w_in, mlp_w_out, final_g, loss_target, m_c_ctx, m_w_ada, m_b_ada, m_norm_g, m_rec_w_in, m_rec_conv_w, m_rec_conv_b, m_rec_lambda, m_rec_w_a, m_rec_b_a, m_rec_w_x, m_rec_b_x, m_rec_w_out, m_conf_w_pw1, m_conf_b_pw1, m_conf_conv_w, m_conf_conv_b, m_conf_ln_g, m_conf_ln_b, m_conf_w_pw2, m_conf_b_pw2, m_mlp_w_in, m_mlp_w_out, m_final_g, v_c_ctx, v_w_ada, v_b_ada, v_norm_g, v_rec_w_in, v_rec_conv_w, v_rec_conv_b, v_rec_lambda, v_rec_w_a, v_rec_b_a, v_rec_w_x, v_rec_b_x, v_rec_w_out, v_conf_w_pw1, v_conf_b_pw1, v_conf_conv_w, v_conf_conv_b, v_conf_ln_g, v_conf_ln_b, v_conf_w_pw2, v_conf_b_pw2, v_mlp_w_in, v_mlp_w_out, v_final_g):
    names = ["c_ctx", "w_ada", "b_ada", "norm_g", "rec_w_in", "rec_conv_w", "rec_conv_b", "rec_lambda", "rec_w_a",
             "rec_b_a", "rec_w_x", "rec_b_x", "rec_w_out", "conf_w_pw1", "conf_b_pw1", "conf_conv_w", "conf_conv_b",
             "conf_ln_g", "conf_ln_b", "conf_w_pw2", "conf_b_pw2", "mlp_w_in", "mlp_w_out", "final_g"]
    w = dict(zip(names, [c_ctx, w_ada, b_ada, norm_g, rec_w_in, rec_conv_w, rec_conv_b, rec_lambda, rec_w_a,
                         rec_b_a, rec_w_x, rec_b_x, rec_w_out, conf_w_pw1, conf_b_pw1, conf_conv_w, conf_conv_b,
                         conf_ln_g, conf_ln_b, conf_w_pw2, conf_b_pw2, mlp_w_in, mlp_w_out, final_g]))
    m = dict(zip(names, [m_c_ctx, m_w_ada, m_b_ada, m_norm_g, m_rec_w_in, m_rec_conv_w, m_rec_conv_b, m_rec_lambda,
                         m_rec_w_a, m_rec_b_a, m_rec_w_x, m_rec_b_x, m_rec_w_out, m_conf_w_pw1, m_conf_b_pw1,
                         m_conf_conv_w, m_conf_conv_b, m_conf_ln_g, m_conf_ln_b, m_conf_w_pw2, m_conf_b_pw2,
                         m_mlp_w_in, m_mlp_w_out, m_final_g]))
    v = dict(zip(names, [v_c_ctx, v_w_ada, v_b_ada, v_norm_g, v_rec_w_in, v_rec_conv_w, v_rec_conv_b, v_rec_lambda,
                         v_rec_w_a, v_rec_b_a, v_rec_w_x, v_rec_b_x, v_rec_w_out, v_conf_w_pw1, v_conf_b_pw1,
                         v_conf_conv_w, v_conf_conv_b, v_conf_ln_g, v_conf_ln_b, v_conf_w_pw2, v_conf_b_pw2,
                         v_mlp_w_in, v_mlp_w_out, v_final_g]))
    mx, my, mc = _me()
    chip = 2 * mx + my
    me = 4 * mx + 2 * my + mc

    sharded_small = ["norm_g", "rec_conv_w", "rec_lambda", "conf_b_pw1", "conf_conv_w", "conf_conv_b", "conf_ln_g",
                     "conf_ln_b", "conf_b_pw2"]
    packed, offs = _pack([c] + [w[k] for k in sharded_small], 8)
    place = jnp.stack([chip, mc]).astype(jnp.int32)
    shards = [("rec_in", _halves(rec_w_in), 0), ("rec_out", _halves(rec_w_out), 0),
              ("pw1", _halves(conf_w_pw1), 0), ("pw2", _halves(conf_w_pw2), 0),
              ("mlp_in0", _halves(mlp_w_in), 0), ("mlp_in1", _halves(mlp_w_in), 1),
              ("mlp_out0", _halves(mlp_w_out), 0), ("mlp_out1", _halves(mlp_w_out), 1)]
    small_state, small_sent = _send_start("gather_small_start", [packed[None]], _to_all7, [(7,) + packed.shape], place)
    (slot_rin,) = _place_big(shards[:1], place, small_sent)
    flying, gsems, swapping = {}, {}, {}
    flying["rec_in"], gsems["rec_in"], rec_started = _gather_start("gather_start_rec", [slot_rin], ((0,),), small_sent)
    slots = [slot_rin] + _place_big(shards[1:], place, rec_started)
    placed = jnp.broadcast_to(lax.dynamic_slice(slots[-1], (chip, 0, 0, 0), (1, 1, 1, 1)).reshape(1, 1), (8, 1))
    (own,), (landed,) = _send_wait("gather_small_wait", *small_state, _to_all7, placed)
    by_flip = jnp.concatenate([own, landed], axis=0)
    got_flat = jnp.take(by_flip, jnp.arange(8) ^ me, axis=0).reshape(8, -1)

    def piece(i):
        p, n, shape = offs[i]
        return got_flat[:, p:p + n].reshape((8,) + tuple(shape))

    c_rows = piece(0).reshape(8, D)
    full = {}
    for i, k in enumerate(sharded_small):
        per_chip = jnp.moveaxis(piece(1 + i)[0::2], 0, -2)
        full[k] = per_chip.reshape(per_chip.shape[:-2] + (4 * per_chip.shape[-1],))
    c16 = jnp.concatenate([c_rows, c_ctx.reshape(1, D), jnp.zeros((7, D), F32)], axis=0)

    ns = w_ada.shape[2]
    b_shard = lax.dynamic_slice_in_dim(b_ada, chip * ns, ns, axis=1).reshape(2, 1, ns)
    prod = _ada_fwd(c16, w_ada, b_shard)

    own_rows = lax.dynamic_index_in_dim(prod[:, :8].reshape(2, 4, 2, ns), mc, axis=2, keepdims=False)
    rows = jnp.concatenate([own_rows.transpose(1, 0, 2), jnp.broadcast_to(prod[0, 8], (4, 1, ns)),
                            jnp.zeros((4, 5, ns), F32)], axis=1)
    mod_state, mod_started = _send_start("mod_start", [rows], _to_chips, [(3, 8, ns)], place)
    use_order = dict(rec=(0, 1), mlp0=(4, 6), conf=(2, 3), mlp1=(5, 7))
    fetch_order = dict(rec_out=(1,), mlp0=(4, 6), conf=(2, 3), mlp1=(5, 7))
    order = [t for g in fetch_order for t in fetch_order[g]]
    groups = [tuple(order.index(t) for t in fetch_order[g]) for g in fetch_order]
    fly, sems, all_started = _gather_start("gather_start_rest", [slots[t] for t in order], tuple(groups), mod_started)
    for gi, g in enumerate(fetch_order):
        flying[g], gsems[g] = [fly[k] for k in groups[gi]], sems[2 * gi:2 * gi + 2]

    def wg_pre(group, after):
        bufs = _gather_wait(f"gather_wait_{group}", flying[group], *gsems[group], after)
        swapping[group], token = _swap_start(f"swap_start_{group}", bufs, after)
        return token

    def wg(group, after):
        bufs = _swap_wait(f"swap_wait_{group}", *swapping[group], after)
        if group == "rec_in":
            return dict(rec_w_in=bufs[0].reshape(4, D, RH))
        if group == "rec_out":
            return dict(rec_w_out=bufs[0].reshape(R, D))
        if group == "conf":
            return dict(conf_w_pw1=bufs[0].reshape(4, D, D // 2), conf_w_pw2=bufs[1].reshape(D, D))
        return dict(w_in=bufs[0].reshape(4, D, D), w_out=bufs[1].reshape(FF, D))

    (rows,), (landed,) = _send_wait("mod_wait", *mod_state, _to_chips, all_started)
    own = lax.dynamic_index_in_dim(rows, chip, axis=0, keepdims=True)
    by_flip = jnp.concatenate([own, landed[1:2], landed[0:1], landed[2:3]], axis=0)
    by_chip = jnp.take(by_flip, jnp.arange(4) ^ chip, axis=0)
    mods = by_chip[:, :2].transpose(1, 0, 2).reshape(2, 6, D)
    cmods = by_chip[:, 2].reshape(6, D)[:2]

    rec = dict(conv_w=full["rec_conv_w"][0], conv_b=rec_conv_b[0], lam=full["rec_lambda"][0],
               w_a=rec_w_a[0], b_a=rec_b_a[0], w_x=rec_w_x[0], b_x=rec_b_x[0])
    conf = dict(b_pw1=full["conf_b_pw1"][0], conv_w=full["conf_conv_w"][0], conv_b=full["conf_conv_b"][0],
                ln_g=full["conf_ln_g"][0], ln_b=full["conf_ln_b"][0], b_pw2=full["conf_b_pw2"][0])
    pairing, sent, fulls, sharing = {}, {}, {}, {}
    early = ("mlp1", "conf", "mlp0")
    early_order = [t for g in early for t in use_order[g]]

    def on_grads(group, dws):
        parts = [dw.reshape((4,) + shards[t][1].shape[1:]) for dw, t in zip(dws, use_order[group])]
        pairing[group], token = _reduce_begin(group, parts, place)
        return token

    def finish_pair(after):
        (group, state), = pairing.items()
        pairing.clear()
        sent[group], token = _reduce_mid(group, state, place, after)
        if group == "rec":
            for g in early:
                for t, f in zip(use_order[g], _reduce_end(g, sent[g], place, token)):
                    fulls[t] = f
            sharing["state"], token = _share_start("share_start", [fulls[t] for t in early_order], place)
        sent["token"] = token
        return token

    loss_local, grad_x, _, small = _local_step(x[0], ctx[0], loss_target[0], mods, cmods, full["norm_g"], final_g,
                                               rec, conf, wg, on_grads, wg_pre, finish_pair)
    rec_sent = sent["token"]
    small["loss"] = loss_local.reshape(1)

    small_names = ["loss", "d_mod", "d_cmod", "norm_g", "rec_conv_w", "rec_conv_b", "rec_lambda", "rec_w_a", "rec_b_a",
                   "rec_w_x", "rec_b_x", "conf_b_pw1", "conf_conv_w", "conf_conv_b", "conf_ln_g", "conf_ln_b",
                   "conf_b_pw2", "final_g"]
    mine = lax.broadcasted_iota(jnp.int32, (8, 1), 0) == me
    mod_slots = jnp.where(mine, small["d_mod"].reshape(1, -1), 0.0)
    spacked, soffs = _pack([small[k] for k in small_names] + [mod_slots])
    small_state, small_started = _allreduce_small_begin(spacked, place, rec_sent)

    rec_fulls = _share_halves("share_rec", _reduce_end("rec", sent["rec"], place, small_started))
    whole = dict(zip(use_order["rec"], rec_fulls))
    whole.update(zip(early_order, _share_wait("share_wait", *sharing["state"], rec_fulls[0])))
    g_big = dict(rec_w_in=whole[0].reshape(rec_w_in.shape), rec_w_out=whole[1].reshape(rec_w_out.shape),
                 conf_w_pw1=whole[2].reshape(conf_w_pw1.shape), conf_w_pw2=whole[3].reshape(conf_w_pw2.shape),
                 mlp_w_in=jnp.stack([whole[4].reshape(D, D), whole[5].reshape(D, D)]),
                 mlp_w_out=jnp.stack([whole[6].reshape(D, D), whole[7].reshape(D, D)]))
    delta, new_m, new_v = {}, {}, {}

    def adamw_of(k, g, dep=None):
        cols = w[k].shape[-1]
        d_, m_, v_ = _adamw(f"adamw_{k}", w[k].reshape(-1, cols), g.reshape(-1, cols),
                            m[k].reshape(-1, cols), v[k].reshape(-1, cols), dep)
        delta[k], new_m[k], new_v[k] = (a.reshape(w[k].shape) for a in (d_, m_, v_))

    for k in _BIG:
        adamw_of(k, g_big[k])

    unpacked = _unpack(_allreduce_small_end(small_state, new_v[_BIG[-1]]), soffs)
    ssum = dict(zip(small_names, unpacked[:-1]))
    loss = ssum["loss"].reshape(())
    dmod_rows = unpacked[-1].reshape(8, 2, 6 * D).transpose(1, 0, 2)

    d_cmod_full =jnp.concatenate([ssum["d_cmod"].reshape(1, 2 * D), jnp.zeros((1, 4 * D), F32)], axis=1)
    dm16 = jnp.concatenate([dmod_rows, jnp.stack([d_cmod_full, jnp.zeros((1, 6 * D), F32)]),
                            jnp.zeros((2, 7, 6 * D), F32)], axis=1)
    dm16_shard = lax.dynamic_slice_in_dim(dm16, chip * ns, ns, axis=2)
    g_w_ada, ds_part = _ada_bwd(c16, dm16_shard, w_ada)
    ds_state, ds_sent = _send_start("dsilu_start", [jnp.broadcast_to(ds_part[8:16], (4, 8, D))], _to_chips,
                                    [(3, 8, D)], place)
    adamw_of("w_ada", g_w_ada, ds_sent)
    (ds_own,), (ds_landed,) = _send_wait("dsilu_wait", *ds_state, _to_chips, new_v["w_ada"])
    ds_flip = jnp.concatenate([ds_own[:1], ds_landed[1:2], ds_landed[0:1], ds_landed[2:3]], axis=0)
    g_c_ctx = _cctx_grad(jnp.take(ds_flip, jnp.arange(4) ^ chip, axis=0), c_ctx).reshape(D)
    g_b_ada = ssum["d_mod"] + jnp.stack([d_cmod_full[0], jnp.zeros((6 * D,), F32)])

    def shard_of(a, axis):
        n = a.shape[axis] // 4
        return lax.dynamic_slice_in_dim(a, chip * n, n, axis=axis)

    grads = dict(
        c_ctx=g_c_ctx, w_ada=g_w_ada, b_ada=g_b_ada,
        norm_g=shard_of(ssum["norm_g"].reshape(2, 2, D), 2),
        rec_w_in=g_big["rec_w_in"], rec_conv_w=shard_of(ssum["rec_conv_w"].reshape(1, REC_KW, R), 2),
        rec_conv_b=ssum["rec_conv_b"].reshape(1, R), rec_lambda=shard_of(ssum["rec_lambda"].reshape(1, 2, R), 2),
        rec_w_a=ssum["rec_w_a"].reshape(rec_w_a.shape), rec_b_a=ssum["rec_b_a"].reshape(rec_b_a.shape),
        rec_w_x=ssum["rec_w_x"].reshape(rec_w_x.shape), rec_b_x=ssum["rec_b_x"].reshape(rec_b_x.shape),
        rec_w_out=g_big["rec_w_out"], conf_w_pw1=g_big["conf_w_pw1"],
        conf_b_pw1=shard_of(ssum["conf_b_pw1"].reshape(1, 2 * D), 1),
        conf_conv_w=shard_of(ssum["conf_conv_w"].reshape(1, CONF_KW, D), 2),
        conf_conv_b=shard_of(ssum["conf_conv_b"].reshape(1, D), 1),
        conf_ln_g=shard_of(ssum["conf_ln_g"].reshape(1, D), 1), conf_ln_b=shard_of(ssum["conf_ln_b"].reshape(1, D), 1),
        conf_w_pw2=g_big["conf_w_pw2"], conf_b_pw2=shard_of(ssum["conf_b_pw2"].reshape(1, D), 1),
        mlp_w_in=g_big["mlp_w_in"], mlp_w_out=g_big["mlp_w_out"], final_g=ssum["final_g"].reshape(D))

    rest =[k for k in names if k not in ("w_ada",) + _BIG]
    d_, m_, v_ = _adamw_many("adamw_small", [w[k] for k in rest], [grads[k] for k in rest],
                             [m[k] for k in rest], [v[k] for k in rest])
    for k, dd, mm, vv in zip(rest, d_, m_, v_):
        delta[k], new_m[k], new_v[k] = dd, mm, vv

    return (loss, grad_x[None], *[grads[k] for k in names], *[delta[k] for k in names],
            *[new_m[k] for k in names], *[new_v[k] for k in names])
```

```python
import functools
import math

import jax
import jax.numpy as jnp
from jax import lax
from jax.experimental import pallas as pl
from jax.experimental.pallas import tpu as pltpu

F32 = jnp.float32
BF16 = jnp.bfloat16

D = 1024
T = 2048
TC = 256
TA = T + TC
R = 1280
RH = R // 2
NQ = 4 * RH
FF = 4096
N_BLK = 16
BLK = R // N_BLK
GRID_W = 64
EPS = 1e-6
RG_C = 8.0
CONF_KW = 31
REC_KW = 4
LANE = 128
ROW_TILE = 256
HALO = 16
RG_TILE = 128
PACK_ROWS = 512
MM_TILE = 1024
REC_TILE = TA // 2
CW_REC = 640
CW_CONF = 512
V7X_VMEM_BYTES = 64 * 1024 * 1024
VMEM_LIMIT = V7X_VMEM_BYTES - 8 * 1024 * 1024

ADAM_LR = 0.001
ADAM_B1 = 0.9
ADAM_B2 = 0.999
ADAM_EPS = 1e-08
ADAM_WD = 0.01
ADAM_STEP = 10

MESH = pl.DeviceIdType.MESH
ANY = pl.BlockSpec(memory_space=pl.ANY)


def _sds(shape, dtype):
    return jax.ShapeDtypeStruct(tuple(shape), dtype)


def _pcall(body, **kw):
    return pl.pallas_call(body, **kw)


def _cparams():
    return pltpu.CompilerParams(vmem_limit_bytes=VMEM_LIMIT)


def _full_spec(arr):
    nd = arr.ndim
    return pl.BlockSpec(arr.shape, lambda *ids, _n=nd: (0,) * _n)


def _sum0(v):
    return jnp.sum(v, axis=0, keepdims=True)


def _tiled(name, fn, grid, ins, vecs, outs, vec_outs=(), vec_refs=False):
    n_in, n_vec, n_out = len(ins), len(vecs), len(outs)
    n_grid = len(grid)

    def kern(*refs):
        ids = [pl.program_id(a) for a in range(n_grid)]
        tin = [r[...] for r in refs[:n_in]]
        vin = list(refs[n_in:n_in + n_vec]) if vec_refs else [r[...] for r in refs[n_in:n_in + n_vec]]
        o_refs = refs[n_in + n_vec:n_in + n_vec + n_out]
        a_refs = refs[n_in + n_vec + n_out:]
        tout, incs = fn(ids, tin, vin)
        for r, v in zip(o_refs, tout):
            r[...] = v.astype(r.dtype)
        if a_refs:
            first = functools.reduce(jnp.logical_and, [i == 0 for i in ids])

            @pl.when(first)
            def _():
                for r in a_refs:
                    r[...] = jnp.zeros_like(r)

            for r, v in zip(a_refs, incs):
                r[...] += v

    out_shape = [o for o, _ in outs] + [_sds(s, F32) for s in vec_outs]
    out_specs = [s for _, s in outs] + [
        pl.BlockSpec(tuple(s), lambda *ids, _n=len(s): (0,) * _n) for s in vec_outs]
    res = _pcall(
        kern, name=name, grid=tuple(grid),
        in_specs=[s for _, s in ins] + [_full_spec(v) for v in vecs],
        out_specs=out_specs, out_shape=out_shape, compiler_params=_cparams(),
    )(*[a for a, _ in ins], *vecs)
    return list(res)


def _rows(arr, ncols=None, tm=ROW_TILE, off=0, col=0, clamp_lo=False):
    ncols = arr.shape[1] if ncols is None else ncols
    if clamp_lo:
        return arr, pl.BlockSpec((tm, ncols), lambda i: (jnp.maximum(i + off, 0), col))
    return arr, pl.BlockSpec((tm, ncols), lambda i: (i + off, col))


def _orow(nrows, ncols, dtype, tm=ROW_TILE, off=0, clamp_lo=False):
    if clamp_lo:
        return _sds((nrows, ncols), dtype), pl.BlockSpec((tm, ncols), lambda i: (jnp.maximum(i + off, 0), 0))
    return _sds((nrows, ncols), dtype), pl.BlockSpec((tm, ncols), lambda i: (i + off, 0))


_NN = (((1,), (0,)), ((), ()))
_TN = (((0,), (0,)), ((), ()))
_NT = (((1,), (1,)), ((), ()))


def _mm(name, a, b, dims, grid, a_spec, b_spec, out, acc_shape, extra=(), a_pre=None, epi=None):
    n_k = grid[2]
    n_ex = len(extra)

    def kern(a_ref, b_ref, *rest):
        ex = rest[:n_ex]
        o_refs = rest[n_ex:n_ex + len(out)]
        k = pl.program_id(2)
        av = a_ref[...]
        if a_pre is not None:
            av = a_pre(av)
        part = lax.dot_general(av.astype(BF16), b_ref[...].astype(BF16), dims, preferred_element_type=F32)

        def finish(total):
            vals = [total] if epi is None else epi(total, [e[...] for e in ex])
            for r, v in zip(o_refs, vals):
                r[...] = v.astype(r.dtype)

        if n_k == 1:
            finish(part)
        else:
            acc = rest[-1]

            @pl.when(k == 0)
            def _():
                acc[...] = part

            @pl.when(jnp.logical_and(k > 0, k < n_k - 1))
            def _():
                acc[...] += part

            @pl.when(k == n_k - 1)
            def _():
                finish(acc[...] + part)

    res = _pcall(
        kern, name=name, grid=tuple(grid),
        in_specs=[a_spec, b_spec] + [s for _, s in extra],
        out_specs=[s for _, s in out], out_shape=[o for o, _ in out],
        scratch_shapes=[] if n_k == 1 else [pltpu.VMEM(tuple(acc_shape), F32)], compiler_params=_cparams(),
    )(a, b, *[e for e, _ in extra])
    return list(res)


def _rms(x):
    r = lax.rsqrt(jnp.mean(x * x, axis=-1, keepdims=True) + EPS)
    return x * r, r


def _norm_mod(x, g, sc, sh):
    n, _ = _rms(x)
    return (n * g) * (1.0 + sc) + sh


def _norm_mod_bwd(dh, x, g, sc):
    n, r = _rms(x)
    d_sh = _sum0(dh)
    d_sc = _sum0(dh * (n * g))
    d_g = _sum0(dh * (1.0 + sc) * n)
    dn = dh * (g * (1.0 + sc))
    dx = r * (dn - n * jnp.mean(dn * n, axis=-1, keepdims=True))
    return dx, d_sh, d_sc, d_g


_GELU_K = math.sqrt(2.0 / math.pi)


def _gelu(x):
    t = jnp.tanh(_GELU_K * (x + 0.044715 * x * x * x))
    return 0.5 * x * (1.0 + t), t


def _gelu_grad(x, t):
    return 0.5 * (1.0 + t) + 0.5 * x * (1.0 - t * t) * (_GELU_K * (1.0 + 3.0 * 0.044715 * x * x))


def _sigmoid(x):
    return 0.5 * jnp.tanh(0.5 * x) + 0.5


def _expm1(x):
    p = jnp.full_like(x, 1.0 / 5040.0)
    for c in (1.0 / 720.0, 1.0 / 120.0, 1.0 / 24.0, 1.0 / 6.0, 0.5, 1.0):
        p = p * x + c
    return jnp.where(jnp.abs(x) < 0.3, x * p, jnp.exp(x) - 1.0)


def _softplus_neg(lam):
    return jnp.log1p(jnp.exp(-jnp.abs(lam))) + jnp.maximum(-lam, 0.0)


def _layernorm_parts(x):
    mu = jnp.mean(x, axis=-1, keepdims=True)
    xc = x - mu
    rstd = lax.rsqrt(jnp.mean(xc * xc, axis=-1, keepdims=True) + EPS)
    return xc * rstd, rstd


def _rg_gates(u, wbd, gbias, lam):
    sp = _softplus_neg(lam)
    parts = {}
    for h in range(2):
        uh = u[:, h * RH:(h + 1) * RH]
        g = jnp.dot(uh.astype(BF16), wbd[h], preferred_element_type=F32) + gbias[:, h * NQ:(h + 1) * NQ]
        for d in range(2):
            r = _sigmoid(g[:, (2 * d) * RH:(2 * d + 1) * RH])
            i = _sigmoid(g[:, (2 * d + 1) * RH:(2 * d + 2) * RH])
            sph = sp[d:d + 1, h * RH:(h + 1) * RH]
            la = (-RG_C) * r * sph
            e2 = _expm1(2.0 * la)
            inv_mult = jnp.where(e2 < 0.0, lax.rsqrt(-e2), 0.0)
            parts[(d, h)] = dict(r=r, i=i, la=la, a=jnp.exp(la), e2=e2, mult=-e2 * inv_mult, inv_mult=inv_mult,
                                 uh=uh, sp=sph)
    return parts


def _rg_fwd_fn(ids, tin, vin):
    (u,) = tin
    wbd = vin[0]
    parts = _rg_gates(u, wbd, vin[1][...], vin[2][...])
    outs = []
    for d in range(2):
        a = jnp.concatenate([parts[(d, h)]["a"] for h in range(2)], axis=1)
        b = jnp.concatenate([parts[(d, h)]["mult"] * parts[(d, h)]["i"] * parts[(d, h)]["uh"]
                             for h in range(2)], axis=1)
        outs += [a, b]
    return outs, []


def _rg_bwd_fn(ids, tin, vin):
    u, da_f, db_f, da_r, db_r = tin
    wbd, lam = vin[0], vin[2][...]
    parts = _rg_gates(u, wbd, vin[1][...], lam)
    dab = ((da_f, db_f), (da_r, db_r))
    dsig_lam = -1.0 / (1.0 + jnp.exp(lam))
    du_halves, dpre_halves, dlam = [], [], [[None, None], [None, None]]
    for h in range(2):
        du = jnp.zeros_like(parts[(0, h)]["uh"])
        dpre = []
        for d in range(2):
            p = parts[(d, h)]
            da = dab[d][0][:, h * RH:(h + 1) * RH]
            db = dab[d][1][:, h * RH:(h + 1) * RH]
            d_mult = db * p["i"] * p["uh"]
            d_i = db * p["mult"] * p["uh"]
            du = du + db * p["mult"] * p["i"]
            d_la = da * p["a"] - d_mult * (p["e2"] + 1.0) * p["inv_mult"]
            d_r = d_la * ((-RG_C) * p["sp"])
            dlam[d][h] = _sum0(d_la * ((-RG_C) * p["r"])) * dsig_lam[d:d + 1, h * RH:(h + 1) * RH]
            dpre += [d_r * p["r"] * (1.0 - p["r"]), d_i * p["i"] * (1.0 - p["i"])]
        dpre = jnp.concatenate(dpre, axis=1)
        du = du + lax.dot_general(dpre.astype(BF16), wbd[h], _NT, preferred_element_type=F32)
        du_halves.append(du)
        dpre_halves.append(dpre)
    dpre_all = jnp.concatenate(dpre_halves, axis=1)
    dlam_row = jnp.concatenate([dlam[0][0], dlam[0][1], dlam[1][0], dlam[1][1]], axis=1)
    return [dpre_all, jnp.concatenate(du_halves, axis=1)], [_sum0(dpre_all), dlam_row]


def _tile_flags(i, n_tiles, seq_starts):
    starts_here = functools.reduce(jnp.logical_or, [i == s for s in seq_starts])
    ends_here = functools.reduce(jnp.logical_or, [i + 1 == s for s in seq_starts] + [i + 1 == n_tiles])
    return jnp.logical_not(starts_here), jnp.logical_not(ends_here)


def _halo_specs(col0, cw):
    hb = ROW_TILE // HALO
    prev = pl.BlockSpec((HALO, cw), lambda i, c: (jnp.maximum(i * hb - 1, 0), col0 + c))
    cur = pl.BlockSpec((ROW_TILE, cw), lambda i, c: (i, col0 + c))
    return prev, cur, hb


def _window(prev_ref, cur_ref, next_ref, has_prev, has_next):
    prev = jnp.where(has_prev, prev_ref[...], 0.0)
    nxt = jnp.where(has_next, next_ref[...], 0.0)
    return jnp.concatenate([prev, cur_ref[...], nxt], axis=0)


def _tap_reader(win):
    sub = 8
    n = win.shape[0]
    shifted = {0: win}

    def tap(off):
        s = off % sub
        if s not in shifted:
            shifted[s] = pltpu.roll(win, n - s, axis=0)
        return shifted[s][off - s:off - s + ROW_TILE, :]

    return tap


def _dwconv(name, x, col0, w, bias, pad_left, seq_starts, n_ch, cw=256, flip=False, into=None):
    n_rows = x.shape[0]
    n_tiles = n_rows // ROW_TILE
    n_taps = w.shape[0]
    prev_spec, cur_spec, hb = _halo_specs(col0, cw)
    last_hb = n_rows // HALO - 1
    next_spec = pl.BlockSpec((HALO, cw), lambda i, c: (jnp.minimum((i + 1) * hb, last_hb), col0 + c))
    dest, out_col0 = (None, 0) if into is None else into

    def kern(prev_ref, cur_ref, next_ref, w_ref, b_ref, *rest):
        o_ref = rest[-1]
        has_prev, has_next = _tile_flags(pl.program_id(0), n_tiles, seq_starts)
        win = _window(prev_ref, cur_ref, next_ref, has_prev, has_next)
        tap = _tap_reader(win)
        wv = w_ref[...]
        acc = jnp.zeros((ROW_TILE, cw), F32) + b_ref[...]
        for k in range(n_taps):
            kw = n_taps - 1 - k if flip else k
            acc = acc + wv[kw:kw + 1, :] * tap(HALO + k - pad_left)
        o_ref[...] = acc.astype(o_ref.dtype)

    return _pcall(
        kern, name=name, grid=(n_tiles, n_ch // cw),
        in_specs=[prev_spec, cur_spec, next_spec,
                  pl.BlockSpec((n_taps, cw), lambda i, c: (0, c)), pl.BlockSpec((1, cw), lambda i, c: (0, c))]
        + ([] if dest is None else [ANY]),
        out_specs=pl.BlockSpec((ROW_TILE, cw), lambda i, c: (i, out_col0 + c)),
        out_shape=_sds((n_rows, n_ch), F32) if dest is None else _sds(dest.shape, dest.dtype),
        input_output_aliases={} if dest is None else {5: 0}, compiler_params=_cparams(),
    )(x, x, x, w, bias, *([] if dest is None else [dest]))


def _dwconv_wgrad(name, dy, x, col0, n_taps, pad_left, seq_starts, n_ch, cw=256, dep=None):
    deps = [] if dep is None else [dep]
    n_rows = dy.shape[0]
    n_tiles = n_rows // ROW_TILE
    n_out = -(-(n_taps + 1) // 8) * 8
    prev_spec, cur_spec, hb = _halo_specs(col0, cw)
    last_hb = n_rows // HALO - 1
    next_spec = pl.BlockSpec((HALO, cw), lambda c, i: (jnp.minimum((i + 1) * hb, last_hb), col0 + c))
    prev_spec = pl.BlockSpec((HALO, cw), lambda c, i: (jnp.maximum(i * hb - 1, 0), col0 + c))
    cur_spec = pl.BlockSpec((ROW_TILE, cw), lambda c, i: (i, col0 + c))

    def kern(dy_ref, prev_ref, cur_ref, next_ref, *rest):
        o_ref = rest[-1]
        i = pl.program_id(1)
        has_prev, has_next = _tile_flags(i, n_tiles, seq_starts)
        win = _window(prev_ref, cur_ref, next_ref, has_prev, has_next)
        dyv = dy_ref[...]
        tap = _tap_reader(win)
        rid = lax.broadcasted_iota(jnp.int32, (n_out, cw), 0)
        inc = jnp.where(rid == n_taps, _sum0(dyv), 0.0)
        for k in range(n_taps):
            inc = inc + jnp.where(rid == k, _sum0(dyv * tap(HALO + k - pad_left)), 0.0)

        @pl.when(i == 0)
        def _():
            o_ref[...] = jnp.zeros_like(o_ref)

        o_ref[...] += inc

    return _pcall(
        kern, name=name, grid=(n_ch // cw, n_tiles),
        in_specs=[pl.BlockSpec((ROW_TILE, cw), lambda c, i: (i, c)), prev_spec, cur_spec, next_spec]
        + [pl.BlockSpec(d.shape, lambda c, i: (0, 0)) for d in deps],
        out_specs=pl.BlockSpec((n_out, cw), lambda c, i: (0, c)),
        out_shape=_sds((n_out, n_ch), F32), compiler_params=_cparams(),
    )(dy, x, x, x, *deps)


N_SCAN = TA // ROW_TILE


def _rev_block(j):
    return jnp.where(j == 0, 0, N_SCAN - j)


def _scan_fwd(a_f, b_f, a_r, b_r):
    fwd_spec = pl.BlockSpec((ROW_TILE, R), lambda i: (i, 0))
    rev_spec = pl.BlockSpec((ROW_TILE, R), lambda i: (_rev_block(i), 0))
    hin_spec = pl.BlockSpec((None, 1, R), lambda i: (i, 0, 0))

    def kern(af, bf, ar, br, yf, yr, hin_f, hin_r, hf_s, hr_s):
        @pl.when(pl.program_id(0) == 0)
        def _():
            hf_s[...] = jnp.zeros_like(hf_s)
            hr_s[...] = jnp.zeros_like(hr_s)

        hin_f[...] = hf_s[...]
        hin_r[...] = hr_s[...]

        def step(s8, carry):
            hf, hr = carry
            t0 = pl.multiple_of(s8 * 8, 8)
            for q in range(8):
                tf = t0 + q
                hf = af[pl.ds(tf, 1), :] * hf + bf[pl.ds(tf, 1), :]
                yf[pl.ds(tf, 1), :] = hf
                tr = ROW_TILE - 1 - tf
                hr = ar[pl.ds(tr, 1), :] * hr + br[pl.ds(tr, 1), :]
                yr[pl.ds(tr, 1), :] = hr
            return hf, hr

        hf, hr = lax.fori_loop(0, ROW_TILE // 8, step, (hf_s[...], hr_s[...]))
        hf_s[...] = hf
        hr_s[...] = hr

    return _pcall(
        kern, name="scan_fwd", grid=(N_SCAN,),
        in_specs=[fwd_spec, fwd_spec, rev_spec, rev_spec],
        out_specs=[fwd_spec, rev_spec, hin_spec, hin_spec],
        out_shape=[_sds((TA, R), F32), _sds((TA, R), F32), _sds((N_SCAN, 1, R), F32), _sds((N_SCAN, 1, R), F32)],
        scratch_shapes=[pltpu.VMEM((1, R), F32), pltpu.VMEM((1, R), F32)], compiler_params=_cparams(),
    )(a_f, b_f, a_r, b_r)


def _scan_bwd(dy, a_f, y_f, hin_f, a_r, y_r, hin_r):
    fwd_spec = pl.BlockSpec((ROW_TILE, R), lambda i: (N_SCAN - 1 - i, 0))
    rev_spec = pl.BlockSpec((ROW_TILE, R), lambda i: (_rev_block(N_SCAN - 1 - i), 0))
    hin_spec = pl.BlockSpec((None, 1, R), lambda i: (N_SCAN - 1 - i, 0, 0))
    last = ROW_TILE - 1

    def kern(dyf, af, yf, hf0, dyr, ar, yr, hr0, daf, dbf, dar, dbr, gf_s, anf_s, gr_s, anr_s):
        @pl.when(pl.program_id(0) == 0)
        def _():
            for r in (gf_s, anf_s, gr_s, anr_s):
                r[...] = jnp.zeros_like(r)

        def one(dy_ref, a_ref, y_ref, da_ref, db_ref, g, an, p, pprev):
            gnew = dy_ref[pl.ds(p, 1), :] + an * g
            db_ref[pl.ds(p, 1), :] = gnew
            da_ref[pl.ds(p, 1), :] = gnew * y_ref[pl.ds(pprev, 1), :]
            return gnew, a_ref[pl.ds(p, 1), :]

        def step(s8, carry):
            gf, anf, gr, anr = carry
            base = s8 * 8
            for q in range(8):
                s = last - (base + q)
                gf, anf = one(dyf, af, yf, daf, dbf, gf, anf, s, s - 1)
                gr, anr = one(dyr, ar, yr, dar, dbr, gr, anr, last - s, last - s + 1)
            return gf, anf, gr, anr

        carry = (gf_s[...], anf_s[...], gr_s[...], anr_s[...])
        carry = lax.fori_loop(0, ROW_TILE // 8 - 1, step, carry)
        gf, anf, gr, anr = carry
        for s in range(7, 0, -1):
            gf, anf = one(dyf, af, yf, daf, dbf, gf, anf, s, s - 1)
            gr, anr = one(dyr, ar, yr, dar, dbr, gr, anr, last - s, last - s + 1)
        gf0 = dyf[0:1, :] + anf * gf
        dbf[0:1, :] = gf0
        daf[0:1, :] = gf0 * hf0[...]
        gr0 = dyr[last:last + 1, :] + anr * gr
        dbr[last:last + 1, :] = gr0
        dar[last:last + 1, :] = gr0 * hr0[...]
        gf_s[...] = gf0
        anf_s[...] = af[0:1, :]
        gr_s[...] = gr0
        anr_s[...] = ar[last:last + 1, :]

    return _pcall(
        kern, name="scan_bwd", grid=(N_SCAN,),
        in_specs=[fwd_spec, fwd_spec, fwd_spec, hin_spec, rev_spec, rev_spec, rev_spec, hin_spec],
        out_specs=[fwd_spec, fwd_spec, rev_spec, rev_spec],
        out_shape=[_sds((TA, R), F32)] * 4,
        scratch_shapes=[pltpu.VMEM((1, R), F32)] * 4, compiler_params=_cparams(),
    )(dy, a_f, y_f, hin_f, dy, a_r, y_r, hin_r)


def _me():
    return lax.axis_index("x"), lax.axis_index("y"), lax.axis_index("c")


def _other_chips(mx, my):
    return [(1 - mx, my), (mx, 1 - my), (1 - mx, 1 - my)]


def _rcopy(src, dst, ssem, rsem, dev):
    return pltpu.make_async_remote_copy(src_ref=src, dst_ref=dst, send_sem=ssem, recv_sem=rsem,
                                        device_id=dev, device_id_type=MESH)


def _peers7(mx, my, mc):
    peers = []
    for k in range(1, 8):
        peers.append((1 - mx if (k >> 2) & 1 else mx, 1 - my if (k >> 1) & 1 else my, 1 - mc if k & 1 else mc))
    return peers


def _share_halves(name, fulls):
    n = len(fulls)

    def kern(*refs):
        o = refs[n:2 * n]
        ss, rs = refs[2 * n:]
        mx, my, mc = _me()
        sib = (mx, my, 1 - mc)
        sends = []
        for t in range(n):
            cp = _rcopy(o[t].at[mc], o[t].at[mc], ss.at[t], rs.at[t], sib)
            cp.start()
            sends.append(cp)
        for t in range(n):
            _rcopy(o[t].at[1 - mc], o[t].at[1 - mc], ss.at[t], rs.at[t], sib).wait_recv()
        for cp in sends:
            cp.wait_send()

    dma = pltpu.SemaphoreType.DMA
    return _pcall(
        kern, name=name, in_specs=[ANY] * n, out_specs=[ANY] * n,
        out_shape=[_sds(f.shape, f.dtype) for f in fulls], input_output_aliases={t: t for t in range(n)},
        scratch_shapes=[dma((n,)), dma((n,))],
    )(*fulls)


def _halves_of(refs, c):
    out = []
    for r in refs:
        out += [r.at[c]] if len(r.shape) == 3 else [r.at[l, c] for l in range(r.shape[0])]
    return out


def _share_start(name, fulls, after):
    n = len(fulls)
    n_cp = sum(1 if f.ndim == 3 else f.shape[0] for f in fulls)

    def kern(*refs):
        o = refs[n + 1:2 * n + 1]
        ssem, rsem, token = refs[2 * n + 1:]
        mx, my, mc = _me()
        for q, half in enumerate(_halves_of(o, mc)):
            _rcopy(half, half, ssem.at[q], rsem.at[q], (mx, my, 1 - mc)).start()
        token[...] = jnp.zeros_like(token)

    dma = pltpu.SemaphoreType.DMA
    res = _pcall(
        kern, name=name, in_specs=[ANY] * (n + 1),
        out_specs=[ANY] * n + [SEM, SEM, pl.BlockSpec(memory_space=pltpu.VMEM)],
        out_shape=[_sds(f.shape, f.dtype) for f in fulls] + [dma((n_cp,)), dma((n_cp,)), _sds((8, LANE), F32)],
        input_output_aliases={t: t for t in range(n)},
        compiler_params=pltpu.CompilerParams(has_side_effects=_DATAFLOW),
    )(*fulls, after)
    return (list(res[:n]), res[n], res[n + 1]), res[n + 2]


def _share_wait(name, fulls, ssem, rsem, after):
    n = len(fulls)

    def kern(*refs):
        o = refs[:n]
        ssem_ref, rsem_ref = refs[n], refs[n + 1]
        mx, my, mc = _me()
        sib = (mx, my, 1 - mc)
        for q, (theirs, mine) in enumerate(zip(_halves_of(o, 1 - mc), _halves_of(o, mc))):
            _rcopy(theirs, theirs, ssem_ref.at[q], rsem_ref.at[q], sib).wait_recv()
            _rcopy(mine, mine, ssem_ref.at[q], rsem_ref.at[q], sib).wait_send()

    return list(_pcall(
        kern, name=name, in_specs=[ANY] * n + [SEM, SEM, ANY], out_specs=[ANY] * n,
        out_shape=[_sds(f.shape, f.dtype) for f in fulls], input_output_aliases={t: t for t in range(n)},
        compiler_params=pltpu.CompilerParams(has_side_effects=_DATAFLOW),
    )(*fulls, ssem, rsem, after))


def _tiled_sp(name, fn, grid, sp, ins, outs, into=None):
    n_in = len(ins)
    dest = [] if into is None else [into]

    def kern(sp_ref, *refs):
        tout = fn([r[...] for r in refs[:n_in]])
        for r, v in zip(refs[n_in + len(dest):], tout):
            r[...] = v.astype(r.dtype)

    gs = pltpu.PrefetchScalarGridSpec(num_scalar_prefetch=1, grid=tuple(grid),
                                      in_specs=[s for _, s in ins] + [ANY] * len(dest), out_specs=[s for _, s in outs])
    res = _pcall(kern, name=name, grid_spec=gs, out_shape=[o for o, _ in outs], compiler_params=_cparams(),
                 input_output_aliases={1 + n_in: 0} if dest else {})(sp, *[a for a, _ in ins], *dest)
    return list(res)


def _row_tile(rows, cols, itemsize=4, budget=2 * 1024 * 1024):
    tr = rows
    while tr * cols * itemsize > budget and tr % 32 == 0:
        tr //= 2
    return tr


def _place_big(shards, place, dep=None):
    slots = []
    for tag, s, layer in shards:
        rr, cc = s.shape[2], s.shape[3]
        tr = _row_tile(rr, cc)
        (slot,) = _tiled_sp(
            f"place_{tag}", lambda tin: [tin[0]], (2, rr // tr), place,
            [(s, pl.BlockSpec((None, None, tr, cc), lambda h, i, sp, layer=layer: (layer, h, i, 0)))]
            + [(d, pl.BlockSpec(d.shape, lambda h, i, sp: (0, 0))) for d in _behind(dep)],
            [(_sds((4, 2, rr, cc), BF16), pl.BlockSpec((None, None, tr, cc), lambda h, i, sp: (sp[0], h, i, 0)))])
        slots.append(slot)
    return slots


def _allreduce_small_begin(vec, place, after):
    hr = vec.shape[0] // 2
    tr = _row_tile(hr, LANE)
    blk = (None, None, tr, LANE)
    (pair,) = _tiled_sp(
        "small_place", lambda tin: [tin[0]], (2, hr // tr), place,
        [(vec.reshape(2, hr, LANE), pl.BlockSpec((None, tr, LANE), lambda h, i, sp: (h, i, 0)))],
        [(_sds((2, 2, hr, LANE), F32), pl.BlockSpec(blk, lambda h, i, sp: (sp[1], h, i, 0)))])
    (pair,) = _share_halves("small_share", [pair])
    (slot,) = _tiled_sp(
        "small_pair_add", lambda tin: [tin[0] + tin[1]], (2, hr // tr), place,
        [(pair, pl.BlockSpec(blk, lambda h, i, sp: (0, h, i, 0))),
         (pair, pl.BlockSpec(blk, lambda h, i, sp: (1, h, i, 0)))],
        [(_sds((4, 2, hr, LANE), F32), pl.BlockSpec(blk, lambda h, i, sp: (sp[0], h, i, 0)))])
    fly, sems, token = _gather_start("small_start", [slot], ((0,),), after)
    return (fly, sems), token


def _allreduce_small_end(state, after):
    fly, sems = state
    (chips,) = _swap_halves("small_swap", _gather_wait("small_wait", fly, *sems, after))
    hr = chips.shape[2]
    tr = _row_tile(hr, LANE)
    blk = (None, None, tr, LANE)
    (total,) = _tiled(
        "small_chip_sum", lambda ids, tin, vin: ([((tin[0] + tin[1]) + tin[2]) + tin[3]], []), (2, hr // tr),
        [(chips, pl.BlockSpec(blk, lambda h, i, _j=j: (_j, h, i, 0))) for j in range(4)], [],
        [(_sds((2, hr, LANE), F32), pl.BlockSpec((None, tr, LANE), lambda h, i: (h, i, 0)))])
    return total.reshape(2 * hr, LANE)


SEM =pl.BlockSpec(memory_space=pltpu.SEMAPHORE)
_DATAFLOW = pltpu.SideEffectType.DATAFLOW_SIDE_EFFECTING


def _gather_start(name, slots, groups, after):
    n = len(slots)

    def kern(*refs):
        o = refs[n + 1:2 * n + 1]
        sems, token = refs[2 * n + 1:-1], refs[-1]
        mx, my, mc = _me()
        j0 = 2 * mx + my
        for gi, grp in enumerate(groups):
            for k, t in enumerate(grp):
                for q, (qx, qy) in enumerate(_other_chips(mx, my)):
                    _rcopy(o[t].at[j0, mc], o[t].at[j0, mc], sems[2 * gi].at[3 * k + q],
                           sems[2 * gi + 1].at[3 * k + q], (qx, qy, mc)).start()
        token[...] = jnp.zeros_like(token)

    sem_shapes = []
    for grp in groups:
        sem_shapes += [pltpu.SemaphoreType.DMA((3 * len(grp),))] * 2
    res = _pcall(
        kern, name=name, in_specs=[ANY] * (n + 1),
        out_specs=[ANY] * n + [SEM] * len(sem_shapes) + [pl.BlockSpec(memory_space=pltpu.VMEM)],
        out_shape=[_sds(w.shape, w.dtype) for w in slots] + sem_shapes + [_sds((8, LANE), F32)],
        input_output_aliases={t: t for t in range(n)},
        compiler_params=pltpu.CompilerParams(has_side_effects=_DATAFLOW),
    )(*slots, after)
    return list(res[:n]), list(res[n:-1]), res[-1]


def _gather_wait(name, bufs, ssem, rsem, after):
    n = len(bufs)

    def kern(*refs):
        b = refs[:n]
        ssem_ref, rsem_ref = refs[n], refs[n + 1]
        mx, my, mc = _me()
        j0 = 2 * mx + my
        for k in range(n):
            for q, (qx, qy) in enumerate(_other_chips(mx, my)):
                jq = 2 * qx + qy
                _rcopy(b[k].at[jq, mc], b[k].at[jq, mc], ssem_ref.at[3 * k + q], rsem_ref.at[3 * k + q],
                       (qx, qy, mc)).wait_recv()
                _rcopy(b[k].at[j0, mc], b[k].at[j0, mc], ssem_ref.at[3 * k + q], rsem_ref.at[3 * k + q],
                       (qx, qy, mc)).wait_send()

    return list(_pcall(
        kern, name=name, in_specs=[ANY] * n + [SEM, SEM, ANY], out_specs=[ANY] * n,
        out_shape=[_sds(w.shape, w.dtype) for w in bufs], input_output_aliases={k: k for k in range(n)},
        compiler_params=pltpu.CompilerParams(has_side_effects=_DATAFLOW),
    )(*bufs, ssem, rsem, after))


def _swap_halves(name, bufs):
    n = len(bufs)

    def kern(*refs):
        o = refs[n:2 * n]
        ss, rs = refs[2 * n:]
        mx, my, mc = _me()
        sib = (mx, my, 1 - mc)
        sends = []
        for k in range(n):
            for q, (qx, qy) in enumerate(_other_chips(mx, my)):
                jq = 2 * qx + qy
                cp = _rcopy(o[k].at[jq, mc], o[k].at[jq, mc], ss.at[3 * k + q], rs.at[3 * k + q], sib)
                cp.start()
                sends.append(cp)
        for k in range(n):
            for q, (qx, qy) in enumerate(_other_chips(mx, my)):
                jq = 2 * qx + qy
                _rcopy(o[k].at[jq, 1 - mc], o[k].at[jq, 1 - mc], ss.at[3 * k + q], rs.at[3 * k + q], sib).wait_recv()
        for cp in sends:
            cp.wait_send()

    dma = pltpu.SemaphoreType.DMA
    return list(_pcall(
        kern, name=name, in_specs=[ANY] * n, out_specs=[ANY] * n,
        out_shape=[_sds(w.shape, w.dtype) for w in bufs], input_output_aliases={k: k for k in range(n)},
        scratch_shapes=[dma((3 * n,)), dma((3 * n,))],
    )(*bufs))


def _swap_start(name, bufs, after):
    n = len(bufs)

    def kern(*refs):
        o = refs[n + 1:2 * n + 1]
        ssem, rsem, token = refs[2 * n + 1:]
        mx, my, mc = _me()
        for k in range(n):
            for q, (qx, qy) in enumerate(_other_chips(mx, my)):
                jq = 2 * qx + qy
                _rcopy(o[k].at[jq, mc], o[k].at[jq, mc], ssem.at[3 * k + q], rsem.at[3 * k + q], (mx, my, 1 - mc)).start()
        token[...] = jnp.zeros_like(token)

    dma = pltpu.SemaphoreType.DMA
    res = _pcall(
        kern, name=name, in_specs=[ANY] * (n + 1),
        out_specs=[ANY] * n + [SEM, SEM, pl.BlockSpec(memory_space=pltpu.VMEM)],
        out_shape=[_sds(w.shape, w.dtype) for w in bufs] + [dma((3 * n,)), dma((3 * n,)), _sds((8, LANE), F32)],
        input_output_aliases={k: k for k in range(n)},
        compiler_params=pltpu.CompilerParams(has_side_effects=_DATAFLOW),
    )(*bufs, after)
    return (list(res[:n]), res[n], res[n + 1]), res[n + 2]


def _swap_wait(name, bufs, ssem, rsem, after):
    n = len(bufs)

    def kern(*refs):
        b = refs[:n]
        ssem_ref, rsem_ref = refs[n], refs[n + 1]
        mx, my, mc = _me()
        sib = (mx, my, 1 - mc)
        for k in range(n):
            for q, (qx, qy) in enumerate(_other_chips(mx, my)):
                jq = 2 * qx + qy
                _rcopy(b[k].at[jq, 1 - mc], b[k].at[jq, 1 - mc], ssem_ref.at[3 * k + q], rsem_ref.at[3 * k + q],
                       sib).wait_recv()
                _rcopy(b[k].at[jq, mc], b[k].at[jq, mc], ssem_ref.at[3 * k + q], rsem_ref.at[3 * k + q],
                       sib).wait_send()

    return list(_pcall(
        kern, name=name, in_specs=[ANY] * n + [SEM, SEM, ANY], out_specs=[ANY] * n,
        out_shape=[_sds(w.shape, w.dtype) for w in bufs], input_output_aliases={k: k for k in range(n)},
        compiler_params=pltpu.CompilerParams(has_side_effects=_DATAFLOW),
    )(*bufs, ssem, rsem, after))


def _to_sibling(mx, my, mc):
    return [((j, 1 - mc), j, (mx, my, 1 - mc)) for j in range(4)]


def _to_chips(mx, my, mc):
    return [((2 * qx + qy,), q, (qx, qy, mc)) for q, (qx, qy) in enumerate(_other_chips(mx, my))]


def _to_all7(mx, my, mc):
    return [((0,), k, dev) for k, dev in enumerate(_peers7(mx, my, mc))]


def _send_start(name, srcs, plan, land_shapes, after):
    n = len(srcs)
    per = len(plan(0, 0, 0))

    def kern(*refs):
        s, land = refs[n + 1:2 * n + 1], refs[2 * n + 1:3 * n + 1]
        ssem, rsem, token = refs[3 * n + 1:]
        for k in range(n):
            for q, (idx, slot, dev) in enumerate(plan(*_me())):
                _rcopy(s[k].at[idx], land[k].at[slot], ssem.at[per * k + q], rsem.at[per * k + q], dev).start()
        token[...] = jnp.zeros_like(token)

    dma = pltpu.SemaphoreType.DMA
    res = _pcall(
        kern, name=name, in_specs=[ANY] * (n + 1),
        out_specs=[ANY] * (2 * n) + [SEM, SEM, pl.BlockSpec(memory_space=pltpu.VMEM)],
        out_shape=[_sds(s.shape, s.dtype) for s in srcs] + [_sds(ls, s.dtype) for ls, s in zip(land_shapes, srcs)]
        + [dma((per * n,)), dma((per * n,)), _sds((8, LANE), F32)],
        input_output_aliases={k: k for k in range(n)},
        compiler_params=pltpu.CompilerParams(has_side_effects=_DATAFLOW),
    )(*srcs, after)
    return (list(res[:n]), list(res[n:2 * n]), res[2 * n], res[2 * n + 1]), res[2 * n + 2]


def _send_wait(name, srcs, lands, ssem, rsem, plan, after):
    n = len(srcs)
    per = len(plan(0, 0, 0))

    def kern(*refs):
        s, land = refs[:n], refs[n:2 * n]
        ssem_ref, rsem_ref = refs[2 * n], refs[2 * n + 1]
        for k in range(n):
            for q, (idx, slot, dev) in enumerate(plan(*_me())):
                cp = _rcopy(s[k].at[idx], land[k].at[slot], ssem_ref.at[per * k + q], rsem_ref.at[per * k + q], dev)
                cp.wait_recv()
                cp.wait_send()

    res = _pcall(
        kern, name=name, in_specs=[ANY] * (2 * n) + [SEM, SEM, ANY], out_specs=[ANY] * (2 * n),
        out_shape=[_sds(a.shape, a.dtype) for a in list(srcs) + list(lands)],
        input_output_aliases={k: k for k in range(2 * n)},
        compiler_params=pltpu.CompilerParams(has_side_effects=_DATAFLOW),
    )(*srcs, *lands, ssem, rsem, after)
    return list(res[:n]), list(res[n:])


def _reduce_begin(tag, parts, after):
    return _send_start(f"pair_start_{tag}", parts, _to_sibling, [(4,) + p.shape[2:] for p in parts], after)


def _reduce_mid(tag, pairing, place, after):
    parts, theirs = _send_wait(f"pair_wait_{tag}", *pairing, _to_sibling, after)
    sums = []
    for k, (p, o) in enumerate(zip(parts, theirs)):
        rr, cc = p.shape[2], p.shape[3]
        tr = _row_tile(rr, cc)
        (s_k,) = _tiled_sp(
            f"pair_add_{tag}{k}", lambda tin: [tin[0].astype(F32) + tin[1].astype(F32)], (4, rr // tr), place,
            [(p, pl.BlockSpec((None, None, tr, cc), lambda j, i, sp: (j, sp[1], i, 0))),
             (o, pl.BlockSpec((None, tr, cc), lambda j, i, sp: (j, i, 0)))],
            [(_sds((4, rr, cc), BF16), pl.BlockSpec((None, tr, cc), lambda j, i, sp: (j, i, 0)))])
        sums.append(s_k)
    return _send_start(f"chips_start_{tag}", sums, _to_chips, [(3,) + s.shape[1:] for s in sums], theirs[0])


def _reduce_end(tag, flying, place, after, layer=None, into=None):
    sums, lands = _send_wait(f"chips_wait_{tag}", *flying, _to_chips, after)
    fulls = []
    for k, (s, q) in enumerate(zip(sums, lands)):
        rr, cc = q.shape[1], q.shape[2]
        tr = _row_tile(rr, cc)

        def add4(tin):
            return [((tin[0].astype(F32) + tin[1].astype(F32)) + tin[2].astype(F32)) + tin[3].astype(F32)]

        ins = [(s, pl.BlockSpec((None, tr, cc), lambda i, sp: (sp[0], i, 0)))]
        ins += [(q, pl.BlockSpec((None, tr, cc), lambda i, sp, _k=kk: (_k, i, 0))) for kk in range(3)]
        if layer is None:
            out = (_sds((2, rr, cc), F32), pl.BlockSpec((None, tr, cc), lambda i, sp: (sp[1], i, 0)))
        else:
            out = (_sds((layer[1], 2, rr, cc), F32),
                   pl.BlockSpec((None, None, tr, cc), lambda i, sp, _l=layer[0]: (_l, sp[1], i, 0)))
        (f_k,) = _tiled_sp(f"chip_add_{tag}{k}", add4, (rr // tr,), place, ins, [out],
                           None if into is None else into[k])
        fulls.append(f_k)
    return fulls


def _pack(parts, PACK_ROWS=PACK_ROWS):
    flat, offs, pos = [], [], 0
    for p in parts:
        v = p.reshape(-1).astype(F32)
        n = -(-v.shape[0] // LANE) * LANE
        flat.append(jnp.pad(v, (0, n - v.shape[0])))
        offs.append((pos, v.shape[0], p.shape))
        pos += n
    total = -(-pos // (PACK_ROWS * LANE)) * PACK_ROWS * LANE
    flat.append(jnp.zeros((total - pos,), F32))
    return jnp.concatenate(flat).reshape(-1, LANE), offs


def _unpack(vec, offs):
    v = vec.reshape(-1)
    return [v[p:p + n].reshape(shape) for p, n, shape in offs]


def _adamw_math(wv, gv, mv, vv):
    bc1 = 1.0 - ADAM_B1 ** ADAM_STEP
    bc2 = 1.0 - ADAM_B2 ** ADAM_STEP
    mn = ADAM_B1 * mv + (1.0 - ADAM_B1) * gv
    vn = ADAM_B2 * vv + (1.0 - ADAM_B2) * (gv * gv)
    delta = -ADAM_LR * ((mn / bc1) / (jnp.sqrt(vn / bc2) + ADAM_EPS) + ADAM_WD * wv)
    return delta, mn, vn


def _adamw(name, w, g, m, v, dep=None):
    rows, cols = w.shape
    tr = rows
    for cand in (512, 256, 128, 64, 32, 16, 8):
        if rows % cand == 0 and cand * cols * 4 <= 2 * 1024 * 1024:
            tr = cand
            break

    def fn(ids, tin, vin):
        return list(_adamw_math(*tin)), []

    spec = pl.BlockSpec((tr, cols), lambda i: (i, 0))
    outs = [(_sds((rows, cols), F32), spec)] * 3
    return _tiled(name, fn, (rows // tr,), [(a, spec) for a in (w, g, m, v)], _behind(dep), outs)


def _adamw_many(name, ws, gs, ms, vs):
    n = len(ws)
    views = [(-1, a.shape[-1]) if a.ndim > 1 else (1, -1) for a in ws]
    flat = lambda arrs: [a.reshape(vw) for a, vw in zip(arrs, views)]

    def kern(*refs):
        ins, outs = refs[:4 * n], refs[4 * n:]
        for t in range(n):
            res = _adamw_math(*[ins[q * n + t][...] for q in range(4)])
            for q in range(3):
                outs[q * n + t][...] = res[q]

    shapes = [_sds(a.shape, F32) for a in flat(ws)]
    res = _pcall(kern, name=name, out_shape=shapes * 3, compiler_params=_cparams(),
                 )(*flat(ws), *flat(gs), *flat(ms), *flat(vs))
    back = lambda part: [a.reshape(w.shape) for a, w in zip(part, ws)]
    return back(res[:n]), back(res[n:2 * n]), back(res[2 * n:])


def _pos_embed():
    n_rows = T // GRID_W
    q = D // 4
    omega = 1.0 / (10000.0 ** (jnp.arange(q, dtype=F32) / q))
    er = jnp.arange(n_rows, dtype=jnp.int32).astype(F32)[:, None] * omega[None, :]
    ec = jnp.arange(GRID_W, dtype=jnp.int32).astype(F32)[:, None] * omega[None, :]
    by_row = jnp.concatenate([jnp.sin(er), jnp.cos(er)], axis=-1)
    by_col = jnp.concatenate([jnp.sin(ec), jnp.cos(ec)], axis=-1)
    return jnp.concatenate([jnp.repeat(by_row, GRID_W, axis=0), jnp.tile(by_col, (n_rows, 1))], axis=-1)


def _dense_gates(w_a, w_x):
    rows = jnp.stack([w_a[0], w_x[0], w_a[1], w_x[1]]).reshape(4, 2, RH, BLK)
    mask, spread = _block_mask(), _block_spread().T.astype(BF16)

    def kern(r_ref, m_ref, s_ref, o_ref):
        tiled = jnp.dot(r_ref[...].astype(BF16), s_ref[...], preferred_element_type=F32)
        o_ref[...] = (tiled * m_ref[...]).astype(o_ref.dtype)

    return _pcall(
        kern, name="gates_dense", grid=(2, 4),
        in_specs=[pl.BlockSpec((None, None, RH, BLK), lambda h, q: (q, h, 0, 0)),
                  pl.BlockSpec((RH, RH), lambda h, q: (0, 0)), pl.BlockSpec((BLK, RH), lambda h, q: (0, 0))],
        out_specs=pl.BlockSpec((None, RH, RH), lambda h, q: (h, 0, q)),
        out_shape=_sds((2, RH, NQ), BF16),
    )(rows, mask, spread)


def _block_mask():
    r = lax.broadcasted_iota(jnp.int32, (RH, RH), 0) // BLK
    c = lax.broadcasted_iota(jnp.int32, (RH, RH), 1) // BLK
    return (r == c).astype(F32)


def _block_spread():
    c = lax.broadcasted_iota(jnp.int32, (RH, BLK), 0) % BLK
    j = lax.broadcasted_iota(jnp.int32, (RH, BLK), 1)
    return (c == j).astype(F32)


def _fold_blocks(dense, mask, spread):
    return jnp.dot(dense * mask, spread, preferred_element_type=F32, precision=lax.Precision.HIGHEST)


def _gate_block_grads(folded):
    per = N_BLK // 2
    kinds = [jnp.concatenate([folded[h, q].reshape(per, BLK, BLK) for h in range(2)], axis=0) for q in range(4)]
    return jnp.stack([kinds[0], kinds[2]]), jnp.stack([kinds[1], kinds[3]])


def _gate_bias_dense(b_a, b_x):
    cols = []
    for h in range(2):
        for src in (b_a[0], b_x[0], b_a[1], b_x[1]):
            cols.append(src.reshape(R)[h * RH:(h + 1) * RH])
    return jnp.concatenate(cols).reshape(1, 2 * NQ)


def _gate_bias_grads(dgb):
    v = dgb.reshape(2, 4, RH)
    kinds = [jnp.concatenate([v[0, q], v[1, q]]).reshape(N_BLK, BLK) for q in range(4)]
    return jnp.stack([kinds[0], kinds[2]]), jnp.stack([kinds[1], kinds[3]])


def _residual_epilogue(next_norm):
    def epi(acc, ex):
        x_new = ex[0] + ex[1] * acc
        outs = [acc, x_new]
        if next_norm:
            outs.append(_norm_mod(x_new, ex[-3], ex[-2], ex[-1]))
        return outs
    return epi


def _mlp_fwd(tag, x_in, h, gate, w_in, w_out, next_norm=None, dep=None):
    tm = MM_TILE
    (r,) = _mm(f"{tag}_in", h, w_in, _NN, (T // tm, 4, 1),
               pl.BlockSpec((tm, D), lambda i, j, k: (i, 0)), pl.BlockSpec((None, D, D), lambda i, j, k: (j, 0, 0)),
               [(_sds((T, FF), BF16), pl.BlockSpec((tm, D), lambda i, j, k: (i, j)))], (tm, D),
               extra=[(d_, _full_spec(d_)) for d_ in _behind(dep)], epi=lambda acc, ex: [jnp.maximum(acc, 0.0)])
    row_spec = pl.BlockSpec((tm, D), lambda i, j, k: (i, 0))
    outs = [(_sds((T, D), F32), row_spec)] * 2 + ([(_sds((T, D), BF16), row_spec)] if next_norm else [])
    res = _mm(f"{tag}_out", r, w_out, _NN, (T // tm, 1, FF // D),
              pl.BlockSpec((tm, D), lambda i, j, k: (i, k)), pl.BlockSpec((D, D), lambda i, j, k: (k, 0)),
              outs, (tm, D),
              extra=[(x_in, row_spec), (gate, _full_spec(gate))] + [(v, _full_spec(v)) for v in next_norm or ()],
              a_pre=lambda a: a * a, epi=_residual_epilogue(next_norm))
    return dict(h=h, r=r, o=res[0], x_in=x_in), res[1], (res[2] if next_norm else None)


def _behind(dep):
    return [] if dep is None else [dep]


def _gate_bwd(tag, dx, o, gate, dep=None):
    def fn(ids, t, v):
        d_o = t[0] * v[0]
        return [d_o], [_sum0(t[0] * t[1]), _sum0(d_o)]
    return _tiled(f"{tag}_gate_bwd", fn, (T // ROW_TILE,), [_rows(dx), _rows(o)], [gate] + _behind(dep),
                  [_orow(T, D, BF16)], [(1, D), (1, D)])


def _norm_bwd(tag, dx_res, dh, dh_off, x, g_norm, sc, with_dx=True, dep=None):
    n_t = x.shape[0] // ROW_TILE

    def fn(ids, t, v):
        if with_dx:
            dres, dhv, xv = t
        else:
            dhv, xv = t
        dxv, d_sh, d_sc, d_g = _norm_mod_bwd(dhv, xv, v[0], v[1])
        return ([dres + dxv] if with_dx else []), [d_sh, d_sc, d_g]

    ins = ([_rows(dx_res)] if with_dx else []) + [_rows(dh, off=dh_off), _rows(x)]
    outs = [_orow(x.shape[0], D, F32)] if with_dx else []
    return _tiled(f"{tag}_norm_bwd", fn, (n_t,), ins, [g_norm, sc] + _behind(dep), outs, [(1, D)] * 3)


def _mlp_bwd(tag, dx, saved, g_norm, sc, gate, w_in, w_out, dep=None):
    d_o, d_gate, _ = _gate_bwd(tag, dx, saved["o"], gate, dep)
    tm = MM_TILE
    r = saved["r"]
    (da,) = _mm(f"{tag}_dz", d_o, w_out, _NT, (T // tm, FF // D, 1),
                pl.BlockSpec((tm, D), lambda i, j, k: (i, 0)), pl.BlockSpec((D, D), lambda i, j, k: (j, 0)),
                [(_sds((T, FF), BF16), pl.BlockSpec((tm, D), lambda i, j, k: (i, j)))], (tm, D),
                extra=[(r, pl.BlockSpec((tm, D), lambda i, j, k: (i, j)))],
                epi=lambda acc, ex: [acc * (2.0 * ex[0].astype(F32))])
    tk = MM_TILE
    (dw_out,) = _mm(f"{tag}_dwout", r, d_o, _TN, (FF // tm, 1, T // tk),
                    pl.BlockSpec((tk, tm), lambda i, j, k: (k, i)), pl.BlockSpec((tk, D), lambda i, j, k: (k, 0)),
                    [(_sds((FF, D), BF16), pl.BlockSpec((tm, D), lambda i, j, k: (i, 0)))], (tm, D),
                    a_pre=lambda a: a * a)
    (dh,) = _mm(f"{tag}_dh", da, w_in, _NT, (T // tm, 1, 4),
                pl.BlockSpec((tm, D), lambda i, j, k: (i, k)), pl.BlockSpec((None, D, D), lambda i, j, k: (k, 0, 0)),
                [(_sds((T, D), F32), pl.BlockSpec((tm, D), lambda i, j, k: (i, 0)))], (tm, D))
    (dw_in,) = _mm(f"{tag}_dwin", saved["h"], da, _TN, (D // tm, 4, T // tk),
                   pl.BlockSpec((tk, tm), lambda i, j, k: (k, i)), pl.BlockSpec((tk, D), lambda i, j, k: (k, j)),
                   [(_sds((4, D, D), BF16), pl.BlockSpec((None, tm, D), lambda i, j, k: (j, i, 0)))], (tm, D))
    dx_in, d_sh, d_sc, d_g = _norm_bwd(tag, dx, dh, 0, saved["x_in"], g_norm, sc)
    return dx_in, dw_in, dw_out, dict(sh=d_sh, sc=d_sc, gate=d_gate, g_norm=d_g)


def _local_step(x, ctx, tgt, mods, cmods, norm_g, final_g, rec, conf, wg, on_grads=None, wg_pre=None, on_later=None):
    on_grads = on_grads or (lambda group, dws: None)
    wg_pre = wg_pre or (lambda group, after: None)
    on_later = on_later or (lambda after: None)
    n_t = T // ROW_TILE
    row = lambda v: v.reshape(1, -1)
    m0 = [row(mods[0, q]) for q in range(6)]
    m1 = [row(mods[1, q]) for q in range(6)]
    g00, g01, g10, g11 = (row(norm_g[0, 0]), row(norm_g[0, 1]), row(norm_g[1, 0]), row(norm_g[1, 1]))
    csh, csc = row(cmods[0]), row(cmods[1])
    pos = _pos_embed()

    def prep0(ids, t, v):
        cx, xv, pv = t
        is_ctx = ids[0] == 0
        xin = jnp.where(is_ctx, cx, xv + pv)
        sh = jnp.where(is_ctx, v[3], v[1])
        sc = jnp.where(is_ctx, v[4], v[2])
        return [_norm_mod(xin, v[0], sc, sh), xv + pv], []

    dep = wg_pre("rec_in", csh)
    hcat, x0 = _tiled(
        "prep0", prep0, (N_SCAN,),
        [(ctx, pl.BlockSpec((ROW_TILE, D), lambda i: (0, 0))), _rows(x, off=-1, clamp_lo=True),
         _rows(pos, off=-1, clamp_lo=True)],
        [g00, m0[0], m0[1], csh, csc] + _behind(dep),
        [_orow(TA, D, BF16), _orow(T, D, F32, off=-1, clamp_lo=True)])

    tm_a = REC_TILE
    w_rin = wg("rec_in", hcat)["rec_w_in"]
    (a_in,) = _mm("rec_in", hcat, w_rin, _NN, (TA // tm_a, 4, 1),
                  pl.BlockSpec((tm_a, D), lambda i, j, k: (i, 0)),
                  pl.BlockSpec((None, D, RH), lambda i, j, k: (j, 0, 0)),
                  [(_sds((TA, 2 * R), F32), pl.BlockSpec((tm_a, RH), lambda i, j, k: (i, j)))], (tm_a, RH))
    rec_starts = (0, 1)
    u = _dwconv("rec_conv", a_in, R // CW_REC, rec["conv_w"], row(rec["conv_b"]), 1, rec_starts, R, CW_REC)
    wbd = _dense_gates(rec["w_a"], rec["w_x"])
    gbias = _gate_bias_dense(rec["b_a"], rec["b_x"])
    lam = rec["lam"]
    a_f, b_f, a_r, b_r = _tiled("rg_fwd", _rg_fwd_fn, (TA // RG_TILE,), [_rows(u, tm=RG_TILE)], [wbd, gbias, lam],
                                [_orow(TA, R, F32, tm=RG_TILE)] * 4, vec_refs=True)
    dep = wg_pre("rec_out", a_f)
    dep = wg_pre("mlp0", a_f if dep is None else dep)
    y_f, y_r, hin_f, hin_r = _scan_fwd(a_f, b_f, a_r, b_r)

    def rec_mid(ids, t, v):
        gp, yf, yr = t
        g, _ = _gelu(gp)
        return [g * (yf + yr)], []

    (m_rec,) = _tiled("rec_mid", rec_mid, (n_t,),
                      [_rows(a_in, R, off=1), _rows(y_f, off=1), _rows(y_r, off=1)], _behind(dep),
                      [_orow(T, R, BF16)])
    tm = MM_TILE
    row_spec = pl.BlockSpec((tm, D), lambda i, j, k: (i, 0))
    norm_mlp0 = (g01, m0[4], m0[3])
    w_rout = wg("rec_out", m_rec)["rec_w_out"]
    o_rec, x1, h_mlp0 = _mm(
        "rec_out", m_rec, w_rout, _NN, (T // tm, 1, 1),
        pl.BlockSpec((tm, R), lambda i, j, k: (i, 0)), pl.BlockSpec((R, D), lambda i, j, k: (0, 0)),
        [(_sds((T, D), F32), row_spec)] * 2 + [(_sds((T, D), BF16), row_spec)], (tm, D),
        extra=[(x0, row_spec), (m0[2], _full_spec(m0[2]))] + [(v, _full_spec(v)) for v in norm_mlp0],
        epi=_residual_epilogue(norm_mlp0))
    w_m0 = wg("mlp0", x1)
    dep = wg_pre("conf", x1)
    mlp0, x2, h1 = _mlp_fwd("mlp0", x1, h_mlp0, m0[5], w_m0["w_in"], w_m0["w_out"], (g10, m1[1], m1[0]), dep)

    b_pw1 = row(conf["b_pw1"])
    w_cf = wg("conf", x2)
    dep = wg_pre("mlp1", x2)
    (pre,) = _mm("conf_pw1", h1, w_cf["conf_w_pw1"], _NN, (T // tm, 4, 1),
                 pl.BlockSpec((tm, D), lambda i, j, k: (i, 0)),
                 pl.BlockSpec((None, D, D // 2), lambda i, j, k: (j, 0, 0)),
                 [(_sds((T, 2 * D), F32), pl.BlockSpec((tm, D // 2), lambda i, j, k: (i, j)))], (tm, D // 2),
                 extra=[(b_pw1, pl.BlockSpec((1, D // 2), lambda i, j, k: (0, j)))]
                 + [(d_, _full_spec(d_)) for d_ in _behind(dep)],
                 epi=lambda acc, ex: [acc + ex[0]])
    (zg,) = _tiled("conf_glu", lambda ids, t, v: ([t[0] * _sigmoid(t[1])], []), (n_t,),
                   [_rows(pre, D, col=0), _rows(pre, D, col=1)], [], [_orow(T, D, F32)])
    conf_starts = (0,)
    zc = _dwconv("conf_conv", zg, 0, conf["conv_w"], row(conf["conv_b"]), CONF_KW // 2, conf_starts, D, CW_CONF)
    ln_g, ln_b = row(conf["ln_g"]), row(conf["ln_b"])

    def ln_silu(ids, t, v):
        nh, _ = _layernorm_parts(t[0])
        ln = nh * v[0] + v[1]
        return [ln * _sigmoid(ln)], []

    (s_conf,) = _tiled("conf_ln", ln_silu, (n_t,), [_rows(zc)], [ln_g, ln_b], [_orow(T, D, BF16)])
    b_pw2 = row(conf["b_pw2"])
    norm_mlp1 = (g11, m1[4], m1[3])
    pw2_epi = _residual_epilogue(norm_mlp1)
    y_conf, x3, h_mlp1 = _mm(
        "conf_pw2", s_conf, w_cf["conf_w_pw2"], _NN, (T // tm, 1, 1),
        row_spec, pl.BlockSpec((D, D), lambda i, j, k: (0, 0)),
        [(_sds((T, D), F32), row_spec)] * 2 + [(_sds((T, D), BF16), row_spec)], (tm, D),
        extra=[(x2, row_spec), (m1[2], _full_spec(m1[2])), (b_pw2, _full_spec(b_pw2))]
        + [(v, _full_spec(v)) for v in norm_mlp1],
        epi=lambda acc, ex: pw2_epi(acc + ex[2], ex))
    w_m1 = wg("mlp1", x3)
    mlp1, x4, _ = _mlp_fwd("mlp1", x3, h_mlp1, m1[5], w_m1["w_in"], w_m1["w_out"])

    fg = row(final_g)

    def head(ids, t, v):
        n, r = _rms(t[0])
        err = n * v[0] - t[1]
        d_out = err * (1.0 / D)
        dn = d_out * v[0]
        dxv = r * (dn - n * jnp.mean(dn * n, axis=-1, keepdims=True))
        part = jnp.sum(_sum0(err * err), axis=1, keepdims=True) * (0.5 / D)
        return [dxv], [part, _sum0(d_out * n)]

    dx4, loss, d_fg = _tiled("head", head, (n_t,), [_rows(x4), _rows(tgt)], [fg], [_orow(T, D, F32)],
                             [(1, 1), (1, D)])

    dx3, dw_in1, dw_out1, dm_mlp1 = _mlp_bwd("mlp1", dx4, mlp1, g11, m1[4], m1[5],
                                             w_m1["w_in"], w_m1["w_out"])
    dep = on_grads("mlp1", (dw_in1, dw_out1))
    d_y, d_g1c, d_bpw2 = _gate_bwd("conf", dx3, y_conf, m1[2], dep)
    tk = MM_TILE
    (dw_pw2,) = _mm("conf_dwpw2", s_conf, d_y, _TN, (D // tm, 1, T // tk),
                    pl.BlockSpec((tk, tm), lambda i, j, k: (k, i)), pl.BlockSpec((tk, D), lambda i, j, k: (k, 0)),
                    [(_sds((D, D), BF16), pl.BlockSpec((tm, D), lambda i, j, k: (i, 0)))], (tm, D))
    (ds,) = _mm("conf_ds", d_y, w_cf["conf_w_pw2"], _NT, (T // tm, 1, 1),
                pl.BlockSpec((tm, D), lambda i, j, k: (i, 0)), pl.BlockSpec((D, D), lambda i, j, k: (0, 0)),
                [(_sds((T, D), F32), pl.BlockSpec((tm, D), lambda i, j, k: (i, 0)))], (tm, D))
    dep = on_later(ds)

    def ln_silu_bwd(ids, t, v):
        dsv, zcv = t
        nh, rstd = _layernorm_parts(zcv)
        ln = nh * v[0] + v[1]
        sg = _sigmoid(ln)
        d_ln = dsv * (sg * (1.0 + ln * (1.0 - sg)))
        d_nh = d_ln * v[0]
        d_zc = rstd * (d_nh - jnp.mean(d_nh, axis=-1, keepdims=True)
                       - nh * jnp.mean(d_nh * nh, axis=-1, keepdims=True))
        return [d_zc], [_sum0(d_ln * nh), _sum0(d_ln)]

    d_zc, d_lng, d_lnb = _tiled("conf_ln_bwd", ln_silu_bwd, (n_t,), [_rows(ds), _rows(zc)],
                                [ln_g, ln_b] + _behind(dep), [_orow(T, D, F32)], [(1, D), (1, D)])
    d_zg = _dwconv("conf_conv_dx", d_zc, 0, conf["conv_w"], jnp.zeros((1, D), F32),
                   CONF_KW - 1 - CONF_KW // 2, conf_starts, D, CW_CONF, flip=True)

    def glu_bwd(ids, t, v):
        dz, pa, pb = t
        sg = _sigmoid(pb)
        d_a = dz * sg
        d_b = dz * pa * sg * (1.0 - sg)
        return [jnp.concatenate([d_a, d_b], axis=1)], [_sum0(d_a), _sum0(d_b)]

    d_pre, d_b1a, d_b1b = _tiled(
        "conf_glu_bwd", glu_bwd, (n_t,), [_rows(d_zg), _rows(pre, D, col=0), _rows(pre, D, col=1)], [],
        [_orow(T, 2 * D, BF16)], [(1, D), (1, D)])
    (dw_pw1,) = _mm("conf_dwpw1", h1, d_pre, _TN, (D // tm, 4, T // tk),
                    pl.BlockSpec((tk, tm), lambda i, j, k: (k, i)),
                    pl.BlockSpec((tk, D // 2), lambda i, j, k: (k, j)),
                    [(_sds((4, D, D // 2), BF16), pl.BlockSpec((None, tm, D // 2), lambda i, j, k: (j, i, 0)))],
                    (tm, D // 2))
    dep = on_grads("conf", (dw_pw1, dw_pw2))
    (dh1,) = _mm("conf_dh", d_pre, w_cf["conf_w_pw1"], _NT, (T // tm, 1, 4),
                 pl.BlockSpec((tm, D // 2), lambda i, j, k: (i, k)),
                 pl.BlockSpec((None, D, D // 2), lambda i, j, k: (k, 0, 0)),
                 [(_sds((T, D), F32), pl.BlockSpec((tm, D), lambda i, j, k: (i, 0)))], (tm, D))
    dx2, d_sh1c, d_sc1c, d_g10 = _norm_bwd("conf", dx3, dh1, 0, x2, g10, m1[1], dep=dep)
    dep = on_later(dx2)

    dx1, dw_in0, dw_out0, dm_mlp0 = _mlp_bwd("mlp0", dx2, mlp0, g01, m0[4], m0[5],
                                             w_m0["w_in"], w_m0["w_out"], dep)
    dep = on_grads("mlp0", (dw_in0, dw_out0))
    d_orec, d_g1r, _ = _gate_bwd("rec", dx1, o_rec, m0[2], dep)
    (dw_rout,) = _mm("rec_dwout", m_rec, d_orec, _TN, (R // RH, 1, T // tk),
                     pl.BlockSpec((tk, RH), lambda i, j, k: (k, i)), pl.BlockSpec((tk, D), lambda i, j, k: (k, 0)),
                     [(_sds((R, D), BF16), pl.BlockSpec((RH, D), lambda i, j, k: (i, 0)))], (RH, D))
    (dm_rec,) = _mm("rec_dm", d_orec, w_rout, _NT, (T // tm, 1, 1),
                    pl.BlockSpec((tm, D), lambda i, j, k: (i, 0)), pl.BlockSpec((R, D), lambda i, j, k: (0, 0)),
                    [(_sds((T, R), F32), pl.BlockSpec((tm, R), lambda i, j, k: (i, 0)))], (tm, R))
    dep = on_later(dm_rec)

    def rec_mid_bwd(ids, t, v):
        dmv, gp, yf, yr = t
        g, th = _gelu(gp)
        lat = ids[0] > 0
        d_gp = jnp.where(lat, dmv * (yf + yr) * _gelu_grad(gp, th), 0.0)
        dy = jnp.where(lat, dmv * g, 0.0)
        return [d_gp, dy], []

    d_a, dy = _tiled("rec_mid_bwd", rec_mid_bwd, (N_SCAN,),
                     [_rows(dm_rec, off=-1, clamp_lo=True), _rows(a_in, R), _rows(y_f), _rows(y_r)], _behind(dep),
                     [(_sds((TA, 2 * R), BF16), pl.BlockSpec((ROW_TILE, R), lambda i: (i, 0))), _orow(TA, R, F32)])
    da_f, db_f, da_r, db_r = _scan_bwd(dy, a_f, y_f, hin_f, a_r, y_r, hin_r)
    d_gpre, d_u, d_gbias, d_lam = _tiled(
        "rg_bwd", _rg_bwd_fn, (TA // RG_TILE,), [_rows(a, tm=RG_TILE) for a in (u, da_f, db_f, da_r, db_r)],
        [wbd, gbias, lam], [_orow(TA, 2 * NQ, BF16, tm=RG_TILE), _orow(TA, R, F32, tm=RG_TILE)],
        [(1, 2 * NQ), (1, 2 * R)], vec_refs=True)
    tk_a = REC_TILE
    d_a = _dwconv("rec_conv_dx", d_u, 0, rec["conv_w"], jnp.zeros((1, R), F32), REC_KW - 1 - 1,
                  rec_starts, R, CW_REC, flip=True, into=(d_a, R // CW_REC))
    (dw_rin,) = _mm("rec_dwin", hcat, d_a, _TN, (D // tm, 4, TA // tk_a),
                    pl.BlockSpec((tk_a, tm), lambda i, j, k: (k, i)), pl.BlockSpec((tk_a, RH), lambda i, j, k: (k, j)),
                    [(_sds((4, D, RH), BF16), pl.BlockSpec((None, tm, RH), lambda i, j, k: (j, i, 0)))], (tm, RH))
    dep = on_grads("rec", (dw_rin, dw_rout))
    (dhcat,) = _mm("rec_dh", d_a, w_rin, _NT, (TA // tm_a, 1, 4),
                   pl.BlockSpec((tm_a, RH), lambda i, j, k: (i, k)),
                   pl.BlockSpec((None, D, RH), lambda i, j, k: (k, 0, 0)),
                   [(_sds((TA, D), F32), pl.BlockSpec((tm_a, D), lambda i, j, k: (i, 0)))], (tm_a, D))
    dx0, d_sh1r, d_sc1r, d_g00 = _norm_bwd("rec", dx1, dhcat, 1, x0, g00, m0[1], dep=dep)
    dep = on_later(dx0)

    d_csh, d_csc, d_g00c = _norm_bwd("ctx", None, dhcat, 0, ctx, g00, csc, with_dx=False, dep=dep)
    blk_mask, blk_spread = _block_mask(), _block_spread()
    (d_wbd,) = _mm("rg_dw", u, d_gpre, _TN, (2, 2, TA // tk_a),
                   pl.BlockSpec((tk_a, RH), lambda i, j, k: (k, i)),
                   pl.BlockSpec((tk_a, NQ // 2), lambda i, j, k: (k, 2 * i + j)),
                   [(_sds((2, 4, RH, BLK), F32), pl.BlockSpec((None, 2, RH, BLK), lambda i, j, k: (i, j, 0, 0)))],
                   (RH, NQ // 2),
                   extra=[(blk_mask, _full_spec(blk_mask)), (blk_spread, _full_spec(blk_spread))]
                   + [(d, _full_spec(d)) for d in _behind(dep)],
                   epi=lambda acc, ex: [jnp.stack([_fold_blocks(acc[:, s * RH:(s + 1) * RH], ex[0], ex[1])
                                                   for s in range(2)])])
    d_cw_rec = _dwconv_wgrad("rec_conv_dw", d_u, a_in, R // CW_REC, REC_KW, 1, rec_starts, R, CW_REC, dep)
    d_cw_conf = _dwconv_wgrad("conf_conv_dw", d_zc, zg, 0, CONF_KW, CONF_KW // 2, conf_starts, D, CW_CONF, dep)

    big = dict(rec_w_in=dw_rin, rec_w_out=dw_rout, conf_w_pw1=dw_pw1, conf_w_pw2=dw_pw2,
               mlp_w_in=(dw_in0, dw_in1), mlp_w_out=(dw_out0, dw_out1))
    d_wa, d_wx = _gate_block_grads(d_wbd)
    d_ba, d_bx = _gate_bias_grads(d_gbias)
    d_mod = jnp.concatenate([
        d_sh1r, d_sc1r, d_g1r, dm_mlp0["sh"], dm_mlp0["sc"], dm_mlp0["gate"],
        d_sh1c, d_sc1c, d_g1c, dm_mlp1["sh"], dm_mlp1["sc"], dm_mlp1["gate"]], axis=1).reshape(2, 6 * D)
    small = dict(
        d_mod=d_mod, d_cmod=jnp.concatenate([d_csh, d_csc], axis=1),
        norm_g=jnp.concatenate([d_g00 + d_g00c, dm_mlp0["g_norm"], d_g10, dm_mlp1["g_norm"]], axis=1),
        rec_conv_w=d_cw_rec[:REC_KW], rec_conv_b=d_cw_rec[REC_KW], rec_lambda=d_lam.reshape(2, R),
        rec_w_a=d_wa, rec_b_a=d_ba, rec_w_x=d_wx, rec_b_x=d_bx,
        conf_b_pw1=jnp.concatenate([d_b1a, d_b1b], axis=1), conf_conv_w=d_cw_conf[:CONF_KW],
        conf_conv_b=d_cw_conf[CONF_KW], conf_ln_g=d_lng, conf_ln_b=d_lnb, conf_b_pw2=d_bpw2, final_g=d_fg)
    return loss.reshape(()), dx0, big, small


_BIG = ("rec_w_in", "rec_w_out", "conf_w_pw1", "conf_w_pw2", "mlp_w_in", "mlp_w_out")


def _halves(w):
    return w.reshape(w.shape[0], 2, w.shape[1] // 2, w.shape[2])


def _ada_fwd(c16, w_ada, b_shard):
    ns = w_ada.shape[2]
    tn = 512

    def kern(c_ref, w_ref, b_ref, o_ref):
        cv = c_ref[...]
        s = (cv * _sigmoid(cv)).astype(BF16)
        o_ref[...] = jnp.dot(s, w_ref[...].astype(BF16), preferred_element_type=F32) + b_ref[...]

    return _pcall(
        kern, name="ada_fwd", grid=(2, ns // tn),
        in_specs=[pl.BlockSpec((16, D), lambda l, j: (0, 0)), pl.BlockSpec((None, D, tn), lambda l, j: (l, 0, j)),
                  pl.BlockSpec((None, 1, tn), lambda l, j: (l, 0, j))],
        out_specs=pl.BlockSpec((None, 16, tn), lambda l, j: (l, 0, j)),
        out_shape=_sds((2, 16, ns), F32), compiler_params=_cparams(),
    )(c16, w_ada, b_shard)


def _ada_bwd(c16, dm16, w_ada):
    ns = w_ada.shape[2]
    tn = 512

    def kern(c_ref, dm_ref, w_ref, gw_ref, ds_ref):
        cv = c_ref[...]
        s = (cv * _sigmoid(cv)).astype(BF16)
        dm = dm_ref[...].astype(BF16)
        gw_ref[...] = lax.dot_general(s, dm, _TN, preferred_element_type=F32)

        @pl.when(jnp.logical_and(pl.program_id(0) == 0, pl.program_id(1) == 0))
        def _():
            ds_ref[...] = jnp.zeros_like(ds_ref)

        ds_ref[...] += lax.dot_general(dm, w_ref[...].astype(BF16), _NT, preferred_element_type=F32)

    return _pcall(
        kern, name="ada_bwd", grid=(2, ns // tn),
        in_specs=[pl.BlockSpec((16, D), lambda l, j: (0, 0)), pl.BlockSpec((None, 16, tn), lambda l, j: (l, 0, j)),
                  pl.BlockSpec((None, D, tn), lambda l, j: (l, 0, j))],
        out_specs=[pl.BlockSpec((None, D, tn), lambda l, j: (l, 0, j)), pl.BlockSpec((16, D), lambda l, j: (0, 0))],
        out_shape=[_sds((2, D, ns), F32), _sds((16, D), F32)], compiler_params=_cparams(),
    )(c16, dm16, w_ada)


def _cctx_grad(ds4, c_ctx):
    def kern(d_ref, c_ref, o_ref):
        tot = d_ref[0, 0:1, :] + d_ref[1, 0:1, :] + d_ref[2, 0:1, :] + d_ref[3, 0:1, :]
        cv = c_ref[...]
        sg = _sigmoid(cv)
        o_ref[...] = tot * (sg * (1.0 + cv * (1.0 - sg)))

    return _pcall(kern, name="cctx_grad", out_shape=_sds((1, D), F32))(ds4, c_ctx.reshape(1, D))


def kernel(x, c, ctx, c_ctx, w_ada, b_ada, norm_g, rec_w_in, rec_conv_w, rec_conv_b, rec_lambda, rec_w_a, rec_b_a, rec_w_x, rec_b_x, rec_w_out, conf_w_pw1, conf_b_pw1, conf_conv_w, conf_conv_b, conf_ln_g, conf_ln_b, conf_w_pw2, conf_b_pw2, mlp_w_in, mlp_w_out, final_g, loss_target, m_c_ctx, m_w_ada, m_b_ada, m_norm_g, m_rec_w_in, m_rec_conv_w, m_rec_conv_b, m_rec_lambda, m_rec_w_a, m_rec_b_a, m_rec_w_x, m_rec_b_x, m_rec_w_out, m_conf_w_pw1, m_conf_b_pw1, m_conf_conv_w, m_conf_conv_b, m_conf_ln_g, m_conf_ln_b, m_conf_w_pw2, m_conf_b_pw2, m_mlp_w_in, m_mlp_w_out, m_final_g, v_c_ctx, v_w_ada, v_b_ada, v_norm_g, v_rec_w_in, v_rec_conv_w, v_rec_conv_b, v_rec_lambda, v_rec_w_a, v_rec_b_a, v_rec_w_x, v_rec_b_x, v_rec_w_out, v_conf_w_pw1, v_conf_b_pw1, v_conf_conv_w, v_conf_conv_b, v_conf_ln_g, v_conf_ln_b, v_conf_w_pw2, v_conf_b_pw2, v_mlp_w_in, v_mlp_w_out, v_final_g):
    names = ["c_ctx", "w_ada", "b_ada", "norm_g", "rec_w_in", "rec_conv_w", "rec_conv_b", "rec_lambda", "rec_w_a",
             "rec_b_a", "rec_w_x", "rec_b_x", "rec_w_out", "conf_w_pw1", "conf_b_pw1", "conf_conv_w", "conf_conv_b",
             "conf_ln_g", "conf_ln_b", "conf_w_pw2", "conf_b_pw2", "mlp_w_in", "mlp_w_out", "final_g"]
    w = dict(zip(names, [c_ctx, w_ada, b_ada, norm_g, rec_w_in, rec_conv_w, rec_conv_b, rec_lambda, rec_w_a,
                         rec_b_a, rec_w_x, rec_b_x, rec_w_out, conf_w_pw1, conf_b_pw1, conf_conv_w, conf_conv_b,
                         conf_ln_g, conf_ln_b, conf_w_pw2, conf_b_pw2, mlp_w_in, mlp_w_out, final_g]))
    m = dict(zip(names, [m_c_ctx, m_w_ada, m_b_ada, m_norm_g, m_rec_w_in, m_rec_conv_w, m_rec_conv_b, m_rec_lambda,
                         m_rec_w_a, m_rec_b_a, m_rec_w_x, m_rec_b_x, m_rec_w_out, m_conf_w_pw1, m_conf_b_pw1,
                         m_conf_conv_w, m_conf_conv_b, m_conf_ln_g, m_conf_ln_b, m_conf_w_pw2, m_conf_b_pw2,
                         m_mlp_w_in, m_mlp_w_out, m_final_g]))
    v = dict(zip(names, [v_c_ctx, v_w_ada, v_b_ada, v_norm_g, v_rec_w_in, v_rec_conv_w, v_rec_conv_b, v_rec_lambda,
                         v_rec_w_a, v_rec_b_a, v_rec_w_x, v_rec_b_x, v_rec_w_out, v_conf_w_pw1, v_conf_b_pw1,
                         v_conf_conv_w, v_conf_conv_b, v_conf_ln_g, v_conf_ln_b, v_conf_w_pw2, v_conf_b_pw2,
                         v_mlp_w_in, v_mlp_w_out, v_final_g]))
    mx, my, mc = _me()
    chip = 2 * mx + my
    me = 4 * mx + 2 * my + mc

    sharded_small = ["norm_g", "rec_conv_w", "rec_lambda", "conf_b_pw1", "conf_conv_w", "conf_conv_b", "conf_ln_g",
                     "conf_ln_b", "conf_b_pw2"]
    packed, offs = _pack([c] + [w[k] for k in sharded_small], 8)
    place = jnp.stack([chip, mc]).astype(jnp.int32)
    shards = [("rec_in", _halves(rec_w_in), 0), ("rec_out", _halves(rec_w_out), 0),
              ("pw1", _halves(conf_w_pw1), 0), ("pw2", _halves(conf_w_pw2), 0),
              ("mlp_in0", _halves(mlp_w_in), 0), ("mlp_in1", _halves(mlp_w_in), 1),
              ("mlp_out0", _halves(mlp_w_out), 0), ("mlp_out1", _halves(mlp_w_out), 1)]
    small_state, small_sent = _send_start("gather_small_start", [packed[None]], _to_all7, [(7,) + packed.shape], place)
    (slot_rin,) = _place_big(shards[:1], place, small_sent)
    flying, gsems, swapping = {}, {}, {}
    flying["rec_in"], gsems["rec_in"], rec_started = _gather_start("gather_start_rec", [slot_rin], ((0,),), small_sent)
    slots = [slot_rin] + _place_big(shards[1:], place, rec_started)
    placed = jnp.broadcast_to(lax.dynamic_slice(slots[-1], (chip, 0, 0, 0), (1, 1, 1, 1)).reshape(1, 1), (8, 1))
    (own,), (landed,) = _send_wait("gather_small_wait", *small_state, _to_all7, placed)
    by_flip = jnp.concatenate([own, landed], axis=0)
    got_flat = jnp.take(by_flip, jnp.arange(8) ^ me, axis=0).reshape(8, -1)

    def piece(i):
        p, n, shape = offs[i]
        return got_flat[:, p:p + n].reshape((8,) + tuple(shape))

    c_rows = piece(0).reshape(8, D)
    full = {}
    for i, k in enumerate(sharded_small):
        per_chip = jnp.moveaxis(piece(1 + i)[0::2], 0, -2)
        full[k] = per_chip.reshape(per_chip.shape[:-2] + (4 * per_chip.shape[-1],))
    c16 = jnp.concatenate([c_rows, c_ctx.reshape(1, D), jnp.zeros((7, D), F32)], axis=0)

    ns = w_ada.shape[2]
    b_shard = lax.dynamic_slice_in_dim(b_ada, chip * ns, ns, axis=1).reshape(2, 1, ns)
    prod = _ada_fwd(c16, w_ada, b_shard)

    own_rows = lax.dynamic_index_in_dim(prod[:, :8].reshape(2, 4, 2, ns), mc, axis=2, keepdims=False)
    rows = jnp.concatenate([own_rows.transpose(1, 0, 2), jnp.broadcast_to(prod[0, 8], (4, 1, ns)),
                            jnp.zeros((4, 5, ns), F32)], axis=1)
    mod_state, mod_started = _send_start("mod_start", [rows], _to_chips, [(3, 8, ns)], place)
    use_order = dict(rec=(0, 1), mlp0=(4, 6), conf=(2, 3), mlp1=(5, 7))
    fetch_order = dict(rec_out=(1,), mlp0=(4, 6), conf=(2, 3), mlp1=(5, 7))
    order = [t for g in fetch_order for t in fetch_order[g]]
    groups = [tuple(order.index(t) for t in fetch_order[g]) for g in fetch_order]
    fly, sems, all_started = _gather_start("gather_start_rest", [slots[t] for t in order], tuple(groups), mod_started)
    for gi, g in enumerate(fetch_order):
        flying[g], gsems[g] = [fly[k] for k in groups[gi]], sems[2 * gi:2 * gi + 2]

    def wg_pre(group, after):
        bufs = _gather_wait(f"gather_wait_{group}", flying[group], *gsems[group], after)
        swapping[group], token = _swap_start(f"swap_start_{group}", bufs, after)
        return token

    def wg(group, after):
        bufs = _swap_wait(f"swap_wait_{group}", *swapping[group], after)
        if group == "rec_in":
            return dict(rec_w_in=bufs[0].reshape(4, D, RH))
        if group == "rec_out":
            return dict(rec_w_out=bufs[0].reshape(R, D))
        if group == "conf":
            return dict(conf_w_pw1=bufs[0].reshape(4, D, D // 2), conf_w_pw2=bufs[1].reshape(D, D))
        return dict(w_in=bufs[0].reshape(4, D, D), w_out=bufs[1].reshape(FF, D))

    (rows,), (landed,) = _send_wait("mod_wait", *mod_state, _to_chips, all_started)
    own = lax.dynamic_index_in_dim(rows, chip, axis=0, keepdims=True)
    by_flip = jnp.concatenate([own, landed[1:2], landed[0:1], landed[2:3]], axis=0)
    by_chip = jnp.take(by_flip, jnp.arange(4) ^ chip, axis=0)
    mods = by_chip[:, :2].transpose(1, 0, 2).reshape(2, 6, D)
    cmods = by_chip[:, 2].reshape(6, D)[:2]

    rec = dict(conv_w=full["rec_conv_w"][0], conv_b=rec_conv_b[0], lam=full["rec_lambda"][0],
               w_a=rec_w_a[0], b_a=rec_b_a[0], w_x=rec_w_x[0], b_x=rec_b_x[0])
    conf = dict(b_pw1=full["conf_b_pw1"][0], conv_w=full["conf_conv_w"][0], conv_b=full["conf_conv_b"][0],
                ln_g=full["conf_ln_g"][0], ln_b=full["conf_ln_b"][0], b_pw2=full["conf_b_pw2"][0])
    pairing, sent, sharing = {}, {}, {}

    def on_grads(group, dws):
        parts = [dw.reshape((4,) + shards[t][1].shape[1:]) for dw, t in zip(dws, use_order[group])]
        pairing[group], token = _reduce_begin(group, parts, place)
        return token

    def finish_pair(after):
        (group, state), = pairing.items()
        pairing.clear()
        sent[group], token = _reduce_mid(group, state, place, after)
        if group == "rec":
            mlp = _reduce_end("mlp1", sent["mlp1"], place, token, layer=(1, 2))
            cf = _reduce_end("conf", sent["conf"], place, token)
            mlp = _reduce_end("mlp0", sent["mlp0"], place, token, layer=(0, 2), into=mlp)
            sharing["state"], token = _share_start("share_start", cf + mlp, place)
        sent["token"] = token
        return token

    loss_local, grad_x, _, small = _local_step(x[0], ctx[0], loss_target[0], mods, cmods, full["norm_g"], final_g,
                                               rec, conf, wg, on_grads, wg_pre, finish_pair)
    rec_sent = sent["token"]
    small["loss"] = loss_local.reshape(1)

    small_names = ["loss", "d_mod", "d_cmod", "norm_g", "rec_conv_w", "rec_conv_b", "rec_lambda", "rec_w_a", "rec_b_a",
                   "rec_w_x", "rec_b_x", "conf_b_pw1", "conf_conv_w", "conf_conv_b", "conf_ln_g", "conf_ln_b",
                   "conf_b_pw2", "final_g"]
    mine = lax.broadcasted_iota(jnp.int32, (8, 1), 0) == me
    mod_slots = jnp.where(mine, small["d_mod"].reshape(1, -1), 0.0)
    spacked, soffs = _pack([small[k] for k in small_names] + [mod_slots])
    small_state, small_started = _allreduce_small_begin(spacked, place, rec_sent)

    rec_fulls = _share_halves("share_rec", _reduce_end("rec", sent["rec"], place, small_started))
    shared = _share_wait("share_wait", *sharing["state"], rec_fulls[0])
    g_big = dict(zip(_BIG, [g.reshape(w[k].shape) for k, g in zip(_BIG, list(rec_fulls) + list(shared))]))
    delta, new_m, new_v = {}, {}, {}

    def adamw_of(k, g, dep=None):
        cols = w[k].shape[-1]
        d_, m_, v_ = _adamw(f"adamw_{k}", w[k].reshape(-1, cols), g.reshape(-1, cols),
                            m[k].reshape(-1, cols), v[k].reshape(-1, cols), dep)
        delta[k], new_m[k], new_v[k] = (a.reshape(w[k].shape) for a in (d_, m_, v_))

    for k in _BIG:
        adamw_of(k, g_big[k])

    updated = jnp.broadcast_to((new_v["mlp_w_in"][:1, :1, :1] + new_v["mlp_w_out"][:1, :1, :1]).reshape(1, 1), (8, 1))
    unpacked = _unpack(_allreduce_small_end(small_state, updated), soffs)
    ssum = dict(zip(small_names, unpacked[:-1]))
    loss = ssum["loss"].reshape(())
    dmod_rows = unpacked[-1].reshape(8, 2, 6 * D).transpose(1, 0, 2)

    d_cmod_full =jnp.concatenate([ssum["d_cmod"].reshape(1, 2 * D), jnp.zeros((1, 4 * D), F32)], axis=1)
    dm16 = jnp.concatenate([dmod_rows, jnp.stack([d_cmod_full, jnp.zeros((1, 6 * D), F32)]),
                            jnp.zeros((2, 7, 6 * D), F32)], axis=1)
    dm16_shard = lax.dynamic_slice_in_dim(dm16, chip * ns, ns, axis=2)
    g_w_ada, ds_part = _ada_bwd(c16, dm16_shard, w_ada)
    ds_state, ds_sent = _send_start("dsilu_start", [jnp.broadcast_to(ds_part[8:16], (4, 8, D))], _to_chips,
                                    [(3, 8, D)], place)
    adamw_of("w_ada", g_w_ada, ds_sent)
    (ds_own,), (ds_landed,) = _send_wait("dsilu_wait", *ds_state, _to_chips, new_v["w_ada"])
    ds_flip = jnp.concatenate([ds_own[:1], ds_landed[1:2], ds_landed[0:1], ds_landed[2:3]], axis=0)
    g_c_ctx = _cctx_grad(jnp.take(ds_flip, jnp.arange(4) ^ chip, axis=0), c_ctx).reshape(D)
    g_b_ada = ssum["d_mod"] + jnp.stack([d_cmod_full[0], jnp.zeros((6 * D,), F32)])

    def shard_of(a, axis):
        n = a.shape[axis] // 4
        return lax.dynamic_slice_in_dim(a, chip * n, n, axis=axis)

    grads = dict(
        c_ctx=g_c_ctx, w_ada=g_w_ada, b_ada=g_b_ada,
        norm_g=shard_of(ssum["norm_g"].reshape(2, 2, D), 2),
        rec_w_in=g_big["rec_w_in"], rec_conv_w=shard_of(ssum["rec_conv_w"].reshape(1, REC_KW, R), 2),
        rec_conv_b=ssum["rec_conv_b"].reshape(1, R), rec_lambda=shard_of(ssum["rec_lambda"].reshape(1, 2, R), 2),
        rec_w_a=ssum["rec_w_a"].reshape(rec_w_a.shape), rec_b_a=ssum["rec_b_a"].reshape(rec_b_a.shape),
        rec_w_x=ssum["rec_w_x"].reshape(rec_w_x.shape), rec_b_x=ssum["rec_b_x"].reshape(rec_b_x.shape),
        rec_w_out=g_big["rec_w_out"], conf_w_pw1=g_big["conf_w_pw1"],
        conf_b_pw1=shard_of(ssum["conf_b_pw1"].reshape(1, 2 * D), 1),
        conf_conv_w=shard_of(ssum["conf_conv_w"].reshape(1, CONF_KW, D), 2),
        conf_conv_b=shard_of(ssum["conf_conv_b"].reshape(1, D), 1),
        conf_ln_g=shard_of(ssum["conf_ln_g"].reshape(1, D), 1), conf_ln_b=shard_of(ssum["conf_ln_b"].reshape(1, D), 1),
        conf_w_pw2=g_big["conf_w_pw2"], conf_b_pw2=shard_of(ssum["conf_b_pw2"].reshape(1, D), 1),
        mlp_w_in=g_big["mlp_w_in"], mlp_w_out=g_big["mlp_w_out"], final_g=ssum["final_g"].reshape(D))

    rest =[k for k in names if k not in ("w_ada",) + _BIG]
    d_, m_, v_ = _adamw_many("adamw_small", [w[k] for k in rest], [grads[k] for k in rest],
                             [m[k] for k in rest], [v[k] for k in rest])
    for k, dd, mm, vv in zip(rest, d_, m_, v_):
        delta[k], new_m[k], new_v[k] = dd, mm, vv

    return (loss, grad_x[None], *[grads[k] for k in names], *[delta[k] for k in names],
            *[new_m[k] for k in names], *[new_v[k] for k in names])
```

```python
import functools
import math

import jax
import jax.numpy as jnp
from jax import lax
from jax.experimental import pallas as pl
from jax.experimental.pallas import tpu as pltpu

F32 = jnp.float32
BF16 = jnp.bfloat16

D = 1024
T = 2048
TC = 256
TA = T + TC
R = 1280
RH = R // 2
NQ = 4 * RH
FF = 4096
N_BLK = 16
BLK = R // N_BLK
GRID_W = 64
EPS = 1e-6
RG_C = 8.0
CONF_KW = 31
REC_KW = 4
LANE = 128
ROW_TILE = 256
HALO = 16
RG_TILE = 128
PACK_ROWS = 512
MM_TILE = 1024
REC_TILE = TA // 2
CW_REC = 640
CW_CONF = 512
V7X_VMEM_BYTES = 64 * 1024 * 1024
VMEM_LIMIT = V7X_VMEM_BYTES - 8 * 1024 * 1024

ADAM_LR = 0.001
ADAM_B1 = 0.9
ADAM_B2 = 0.999
ADAM_EPS = 1e-08
ADAM_WD = 0.01
ADAM_STEP = 10

MESH = pl.DeviceIdType.MESH
ANY = pl.BlockSpec(memory_space=pl.ANY)


def _sds(shape, dtype):
    return jax.ShapeDtypeStruct(tuple(shape), dtype)


def _pcall(body, **kw):
    return pl.pallas_call(body, **kw)


def _cparams():
    return pltpu.CompilerParams(vmem_limit_bytes=VMEM_LIMIT)


def _full_spec(arr):
    nd = arr.ndim
    return pl.BlockSpec(arr.shape, lambda *ids, _n=nd: (0,) * _n)


def _sum0(v):
    return jnp.sum(v, axis=0, keepdims=True)


def _tiled(name, fn, grid, ins, vecs, outs, vec_outs=(), vec_refs=False):
    n_in, n_vec, n_out = len(ins), len(vecs), len(outs)
    n_grid = len(grid)

    def kern(*refs):
        ids = [pl.program_id(a) for a in range(n_grid)]
        tin = [r[...] for r in refs[:n_in]]
        vin = list(refs[n_in:n_in + n_vec]) if vec_refs else [r[...] for r in refs[n_in:n_in + n_vec]]
        o_refs = refs[n_in + n_vec:n_in + n_vec + n_out]
        a_refs = refs[n_in + n_vec + n_out:]
        tout, incs = fn(ids, tin, vin)
        for r, v in zip(o_refs, tout):
            r[...] = v.astype(r.dtype)
        if a_refs:
            first = functools.reduce(jnp.logical_and, [i == 0 for i in ids])

            @pl.when(first)
            def _():
                for r in a_refs:
                    r[...] = jnp.zeros_like(r)

            for r, v in zip(a_refs, incs):
                r[...] += v

    out_shape = [o for o, _ in outs] + [_sds(s, F32) for s in vec_outs]
    out_specs = [s for _, s in outs] + [
        pl.BlockSpec(tuple(s), lambda *ids, _n=len(s): (0,) * _n) for s in vec_outs]
    res = _pcall(
        kern, name=name, grid=tuple(grid),
        in_specs=[s for _, s in ins] + [_full_spec(v) for v in vecs],
        out_specs=out_specs, out_shape=out_shape, compiler_params=_cparams(),
    )(*[a for a, _ in ins], *vecs)
    return list(res)


def _rows(arr, ncols=None, tm=ROW_TILE, off=0, col=0, clamp_lo=False):
    ncols = arr.shape[1] if ncols is None else ncols
    if clamp_lo:
        return arr, pl.BlockSpec((tm, ncols), lambda i: (jnp.maximum(i + off, 0), col))
    return arr, pl.BlockSpec((tm, ncols), lambda i: (i + off, col))


def _orow(nrows, ncols, dtype, tm=ROW_TILE, off=0, clamp_lo=False):
    if clamp_lo:
        return _sds((nrows, ncols), dtype), pl.BlockSpec((tm, ncols), lambda i: (jnp.maximum(i + off, 0), 0))
    return _sds((nrows, ncols), dtype), pl.BlockSpec((tm, ncols), lambda i: (i + off, 0))


_NN = (((1,), (0,)), ((), ()))
_TN = (((0,), (0,)), ((), ()))
_NT = (((1,), (1,)), ((), ()))


def _mm(name, a, b, dims, grid, a_spec, b_spec, out, acc_shape, extra=(), a_pre=None, epi=None):
    n_k = grid[2]
    n_ex = len(extra)

    def kern(a_ref, b_ref, *rest):
        ex = rest[:n_ex]
        o_refs = rest[n_ex:n_ex + len(out)]
        k = pl.program_id(2)
        av = a_ref[...]
        if a_pre is not None:
            av = a_pre(av)
        part = lax.dot_general(av.astype(BF16), b_ref[...].astype(BF16), dims, preferred_element_type=F32)

        def finish(total):
            vals = [total] if epi is None else epi(total, [e[...] for e in ex])
            for r, v in zip(o_refs, vals):
                r[...] = v.astype(r.dtype)

        if n_k == 1:
            finish(part)
        else:
            acc = rest[-1]

            @pl.when(k == 0)
            def _():
                acc[...] = part

            @pl.when(jnp.logical_and(k > 0, k < n_k - 1))
            def _():
                acc[...] += part

            @pl.when(k == n_k - 1)
            def _():
                finish(acc[...] + part)

    res = _pcall(
        kern, name=name, grid=tuple(grid),
        in_specs=[a_spec, b_spec] + [s for _, s in extra],
        out_specs=[s for _, s in out], out_shape=[o for o, _ in out],
        scratch_shapes=[] if n_k == 1 else [pltpu.VMEM(tuple(acc_shape), F32)], compiler_params=_cparams(),
    )(a, b, *[e for e, _ in extra])
    return list(res)


def _rms(x):
    r = lax.rsqrt(jnp.mean(x * x, axis=-1, keepdims=True) + EPS)
    return x * r, r


def _norm_mod(x, g, sc, sh):
    n, _ = _rms(x)
    return (n * g) * (1.0 + sc) + sh


def _norm_mod_bwd(dh, x, g, sc):
    n, r = _rms(x)
    d_sh = _sum0(dh)
    d_sc = _sum0(dh * (n * g))
    d_g = _sum0(dh * (1.0 + sc) * n)
    dn = dh * (g * (1.0 + sc))
    dx = r * (dn - n * jnp.mean(dn * n, axis=-1, keepdims=True))
    return dx, d_sh, d_sc, d_g


_GELU_K = math.sqrt(2.0 / math.pi)


def _gelu(x):
    t = jnp.tanh(_GELU_K * (x + 0.044715 * x * x * x))
    return 0.5 * x * (1.0 + t), t


def _gelu_grad(x, t):
    return 0.5 * (1.0 + t) + 0.5 * x * (1.0 - t * t) * (_GELU_K * (1.0 + 3.0 * 0.044715 * x * x))


def _sigmoid(x):
    return 0.5 * jnp.tanh(0.5 * x) + 0.5


def _expm1(x):
    p = jnp.full_like(x, 1.0 / 5040.0)
    for c in (1.0 / 720.0, 1.0 / 120.0, 1.0 / 24.0, 1.0 / 6.0, 0.5, 1.0):
        p = p * x + c
    return jnp.where(jnp.abs(x) < 0.3, x * p, jnp.exp(x) - 1.0)


def _softplus_neg(lam):
    return jnp.log1p(jnp.exp(-jnp.abs(lam))) + jnp.maximum(-lam, 0.0)


def _layernorm_parts(x):
    mu = jnp.mean(x, axis=-1, keepdims=True)
    xc = x - mu
    rstd = lax.rsqrt(jnp.mean(xc * xc, axis=-1, keepdims=True) + EPS)
    return xc * rstd, rstd


def _rg_gates(u, wbd, gbias, lam):
    sp = _softplus_neg(lam)
    parts = {}
    for h in range(2):
        uh = u[:, h * RH:(h + 1) * RH]
        g = jnp.dot(uh.astype(BF16), wbd[h], preferred_element_type=F32) + gbias[:, h * NQ:(h + 1) * NQ]
        for d in range(2):
            r = _sigmoid(g[:, (2 * d) * RH:(2 * d + 1) * RH])
            i = _sigmoid(g[:, (2 * d + 1) * RH:(2 * d + 2) * RH])
            sph = sp[d:d + 1, h * RH:(h + 1) * RH]
            la = (-RG_C) * r * sph
            e2 = _expm1(2.0 * la)
            inv_mult = jnp.where(e2 < 0.0, lax.rsqrt(-e2), 0.0)
            parts[(d, h)] = dict(r=r, i=i, la=la, a=jnp.exp(la), e2=e2, mult=-e2 * inv_mult, inv_mult=inv_mult,
                                 uh=uh, sp=sph)
    return parts


def _rg_fwd_fn(ids, tin, vin):
    (u,) = tin
    wbd = vin[0]
    parts = _rg_gates(u, wbd, vin[1][...], vin[2][...])
    outs = []
    for d in range(2):
        a = jnp.concatenate([parts[(d, h)]["a"] for h in range(2)], axis=1)
        b = jnp.concatenate([parts[(d, h)]["mult"] * parts[(d, h)]["i"] * parts[(d, h)]["uh"]
                             for h in range(2)], axis=1)
        outs += [a, b]
    return outs, []


def _rg_bwd_fn(ids, tin, vin):
    u, da_f, db_f, da_r, db_r = tin
    wbd, lam = vin[0], vin[2][...]
    parts = _rg_gates(u, wbd, vin[1][...], lam)
    dab = ((da_f, db_f), (da_r, db_r))
    dsig_lam = -1.0 / (1.0 + jnp.exp(lam))
    du_halves, dpre_halves, dlam = [], [], [[None, None], [None, None]]
    for h in range(2):
        du = jnp.zeros_like(parts[(0, h)]["uh"])
        dpre = []
        for d in range(2):
            p = parts[(d, h)]
            da = dab[d][0][:, h * RH:(h + 1) * RH]
            db = dab[d][1][:, h * RH:(h + 1) * RH]
            d_mult = db * p["i"] * p["uh"]
            d_i = db * p["mult"] * p["uh"]
            du = du + db * p["mult"] * p["i"]
            d_la = da * p["a"] - d_mult * (p["e2"] + 1.0) * p["inv_mult"]
            d_r = d_la * ((-RG_C) * p["sp"])
            dlam[d][h] = _sum0(d_la * ((-RG_C) * p["r"])) * dsig_lam[d:d + 1, h * RH:(h + 1) * RH]
            dpre += [d_r * p["r"] * (1.0 - p["r"]), d_i * p["i"] * (1.0 - p["i"])]
        dpre = jnp.concatenate(dpre, axis=1)
        du = du + lax.dot_general(dpre.astype(BF16), wbd[h], _NT, preferred_element_type=F32)
        du_halves.append(du)
        dpre_halves.append(dpre)
    dpre_all = jnp.concatenate(dpre_halves, axis=1)
    dlam_row = jnp.concatenate([dlam[0][0], dlam[0][1], dlam[1][0], dlam[1][1]], axis=1)
    return [dpre_all, jnp.concatenate(du_halves, axis=1)], [_sum0(dpre_all), dlam_row]


def _tile_flags(i, n_tiles, seq_starts):
    starts_here = functools.reduce(jnp.logical_or, [i == s for s in seq_starts])
    ends_here = functools.reduce(jnp.logical_or, [i + 1 == s for s in seq_starts] + [i + 1 == n_tiles])
    return jnp.logical_not(starts_here), jnp.logical_not(ends_here)


def _halo_specs(col0, cw):
    hb = ROW_TILE // HALO
    prev = pl.BlockSpec((HALO, cw), lambda i, c: (jnp.maximum(i * hb - 1, 0), col0 + c))
    cur = pl.BlockSpec((ROW_TILE, cw), lambda i, c: (i, col0 + c))
    return prev, cur, hb


def _window(prev_ref, cur_ref, next_ref, has_prev, has_next):
    prev = jnp.where(has_prev, prev_ref[...], 0.0)
    nxt = jnp.where(has_next, next_ref[...], 0.0)
    return jnp.concatenate([prev, cur_ref[...], nxt], axis=0)


def _tap_reader(win):
    sub = 8
    n = win.shape[0]
    shifted = {0: win}

    def tap(off):
        s = off % sub
        if s not in shifted:
            shifted[s] = pltpu.roll(win, n - s, axis=0)
        return shifted[s][off - s:off - s + ROW_TILE, :]

    return tap


def _dwconv(name, x, col0, w, bias, pad_left, seq_starts, n_ch, cw=256, flip=False, into=None):
    n_rows = x.shape[0]
    n_tiles = n_rows // ROW_TILE
    n_taps = w.shape[0]
    prev_spec, cur_spec, hb = _halo_specs(col0, cw)
    last_hb = n_rows // HALO - 1
    next_spec = pl.BlockSpec((HALO, cw), lambda i, c: (jnp.minimum((i + 1) * hb, last_hb), col0 + c))
    dest, out_col0 = (None, 0) if into is None else into

    def kern(prev_ref, cur_ref, next_ref, w_ref, b_ref, *rest):
        o_ref = rest[-1]
        has_prev, has_next = _tile_flags(pl.program_id(0), n_tiles, seq_starts)
        win = _window(prev_ref, cur_ref, next_ref, has_prev, has_next)
        tap = _tap_reader(win)
        wv = w_ref[...]
        acc = jnp.zeros((ROW_TILE, cw), F32) + b_ref[...]
        for k in range(n_taps):
            kw = n_taps - 1 - k if flip else k
            acc = acc + wv[kw:kw + 1, :] * tap(HALO + k - pad_left)
        o_ref[...] = acc.astype(o_ref.dtype)

    return _pcall(
        kern, name=name, grid=(n_tiles, n_ch // cw),
        in_specs=[prev_spec, cur_spec, next_spec,
                  pl.BlockSpec((n_taps, cw), lambda i, c: (0, c)), pl.BlockSpec((1, cw), lambda i, c: (0, c))]
        + ([] if dest is None else [ANY]),
        out_specs=pl.BlockSpec((ROW_TILE, cw), lambda i, c: (i, out_col0 + c)),
        out_shape=_sds((n_rows, n_ch), F32) if dest is None else _sds(dest.shape, dest.dtype),
        input_output_aliases={} if dest is None else {5: 0}, compiler_params=_cparams(),
    )(x, x, x, w, bias, *([] if dest is None else [dest]))


def _dwconv_wgrad(name, dy, x, col0, n_taps, pad_left, seq_starts, n_ch, cw=256, dep=None):
    deps = [] if dep is None else [dep]
    n_rows = dy.shape[0]
    n_tiles = n_rows // ROW_TILE
    n_out = -(-(n_taps + 1) // 8) * 8
    prev_spec, cur_spec, hb = _halo_specs(col0, cw)
    last_hb = n_rows // HALO - 1
    next_spec = pl.BlockSpec((HALO, cw), lambda c, i: (jnp.minimum((i + 1) * hb, last_hb), col0 + c))
    prev_spec = pl.BlockSpec((HALO, cw), lambda c, i: (jnp.maximum(i * hb - 1, 0), col0 + c))
    cur_spec = pl.BlockSpec((ROW_TILE, cw), lambda c, i: (i, col0 + c))

    def kern(dy_ref, prev_ref, cur_ref, next_ref, *rest):
        o_ref = rest[-1]
        i = pl.program_id(1)
        has_prev, has_next = _tile_flags(i, n_tiles, seq_starts)
        win = _window(prev_ref, cur_ref, next_ref, has_prev, has_next)
        dyv = dy_ref[...]
        tap = _tap_reader(win)
        rid = lax.broadcasted_iota(jnp.int32, (n_out, cw), 0)
        inc = jnp.where(rid == n_taps, _sum0(dyv), 0.0)
        for k in range(n_taps):
            inc = inc + jnp.where(rid == k, _sum0(dyv * tap(HALO + k - pad_left)), 0.0)

        @pl.when(i == 0)
        def _():
            o_ref[...] = jnp.zeros_like(o_ref)

        o_ref[...] += inc

    return _pcall(
        kern, name=name, grid=(n_ch // cw, n_tiles),
        in_specs=[pl.BlockSpec((ROW_TILE, cw), lambda c, i: (i, c)), prev_spec, cur_spec, next_spec]
        + [pl.BlockSpec(d.shape, lambda c, i: (0, 0)) for d in deps],
        out_specs=pl.BlockSpec((n_out, cw), lambda c, i: (0, c)),
        out_shape=_sds((n_out, n_ch), F32), compiler_params=_cparams(),
    )(dy, x, x, x, *deps)


N_SCAN = TA // ROW_TILE


def _rev_block(j):
    return jnp.where(j == 0, 0, N_SCAN - j)


def _scan_fwd(a_f, b_f, a_r, b_r):
    fwd_spec = pl.BlockSpec((ROW_TILE, R), lambda i: (i, 0))
    rev_spec = pl.BlockSpec((ROW_TILE, R), lambda i: (_rev_block(i), 0))
    hin_spec = pl.BlockSpec((None, 1, R), lambda i: (i, 0, 0))

    def kern(af, bf, ar, br, yf, yr, hin_f, hin_r, hf_s, hr_s):
        @pl.when(pl.program_id(0) == 0)
        def _():
            hf_s[...] = jnp.zeros_like(hf_s)
            hr_s[...] = jnp.zeros_like(hr_s)

        hin_f[...] = hf_s[...]
        hin_r[...] = hr_s[...]

        def step(s8, carry):
            hf, hr = carry
            t0 = pl.multiple_of(s8 * 8, 8)
            for q in range(8):
                tf = t0 + q
                hf = af[pl.ds(tf, 1), :] * hf + bf[pl.ds(tf, 1), :]
                yf[pl.ds(tf, 1), :] = hf
                tr = ROW_TILE - 1 - tf
                hr = ar[pl.ds(tr, 1), :] * hr + br[pl.ds(tr, 1), :]
                yr[pl.ds(tr, 1), :] = hr
            return hf, hr

        hf, hr = lax.fori_loop(0, ROW_TILE // 8, step, (hf_s[...], hr_s[...]))
        hf_s[...] = hf
        hr_s[...] = hr

    return _pcall(
        kern, name="scan_fwd", grid=(N_SCAN,),
        in_specs=[fwd_spec, fwd_spec, rev_spec, rev_spec],
        out_specs=[fwd_spec, rev_spec, hin_spec, hin_spec],
        out_shape=[_sds((TA, R), F32), _sds((TA, R), F32), _sds((N_SCAN, 1, R), F32), _sds((N_SCAN, 1, R), F32)],
        scratch_shapes=[pltpu.VMEM((1, R), F32), pltpu.VMEM((1, R), F32)], compiler_params=_cparams(),
    )(a_f, b_f, a_r, b_r)


def _scan_bwd(dy, a_f, y_f, hin_f, a_r, y_r, hin_r):
    fwd_spec = pl.BlockSpec((ROW_TILE, R), lambda i: (N_SCAN - 1 - i, 0))
    rev_spec = pl.BlockSpec((ROW_TILE, R), lambda i: (_rev_block(N_SCAN - 1 - i), 0))
    hin_spec = pl.BlockSpec((None, 1, R), lambda i: (N_SCAN - 1 - i, 0, 0))
    last = ROW_TILE - 1

    def kern(dyf, af, yf, hf0, dyr, ar, yr, hr0, daf, dbf, dar, dbr, gf_s, anf_s, gr_s, anr_s):
        @pl.when(pl.program_id(0) == 0)
        def _():
            for r in (gf_s, anf_s, gr_s, anr_s):
                r[...] = jnp.zeros_like(r)

        def one(dy_ref, a_ref, y_ref, da_ref, db_ref, g, an, p, pprev):
            gnew = dy_ref[pl.ds(p, 1), :] + an * g
            db_ref[pl.ds(p, 1), :] = gnew
            da_ref[pl.ds(p, 1), :] = gnew * y_ref[pl.ds(pprev, 1), :]
            return gnew, a_ref[pl.ds(p, 1), :]

        def step(s8, carry):
            gf, anf, gr, anr = carry
            base = s8 * 8
            for q in range(8):
                s = last - (base + q)
                gf, anf = one(dyf, af, yf, daf, dbf, gf, anf, s, s - 1)
                gr, anr = one(dyr, ar, yr, dar, dbr, gr, anr, last - s, last - s + 1)
            return gf, anf, gr, anr

        carry = (gf_s[...], anf_s[...], gr_s[...], anr_s[...])
        carry = lax.fori_loop(0, ROW_TILE // 8 - 1, step, carry)
        gf, anf, gr, anr = carry
        for s in range(7, 0, -1):
            gf, anf = one(dyf, af, yf, daf, dbf, gf, anf, s, s - 1)
            gr, anr = one(dyr, ar, yr, dar, dbr, gr, anr, last - s, last - s + 1)
        gf0 = dyf[0:1, :] + anf * gf
        dbf[0:1, :] = gf0
        daf[0:1, :] = gf0 * hf0[...]
        gr0 = dyr[last:last + 1, :] + anr * gr
        dbr[last:last + 1, :] = gr0
        dar[last:last + 1, :] = gr0 * hr0[...]
        gf_s[...] = gf0
        anf_s[...] = af[0:1, :]
        gr_s[...] = gr0
        anr_s[...] = ar[last:last + 1, :]

    return _pcall(
        kern, name="scan_bwd", grid=(N_SCAN,),
        in_specs=[fwd_spec, fwd_spec, fwd_spec, hin_spec, rev_spec, rev_spec, rev_spec, hin_spec],
        out_specs=[fwd_spec, fwd_spec, rev_spec, rev_spec],
        out_shape=[_sds((TA, R), F32)] * 4,
        scratch_shapes=[pltpu.VMEM((1, R), F32)] * 4, compiler_params=_cparams(),
    )(dy, a_f, y_f, hin_f, dy, a_r, y_r, hin_r)


def _me():
    return lax.axis_index("x"), lax.axis_index("y"), lax.axis_index("c")


def _other_chips(mx, my):
    return [(1 - mx, my), (mx, 1 - my), (1 - mx, 1 - my)]


def _rcopy(src, dst, ssem, rsem, dev):
    return pltpu.make_async_remote_copy(src_ref=src, dst_ref=dst, send_sem=ssem, recv_sem=rsem,
                                        device_id=dev, device_id_type=MESH)


def _peers7(mx, my, mc):
    peers = []
    for k in range(1, 8):
        peers.append((1 - mx if (k >> 2) & 1 else mx, 1 - my if (k >> 1) & 1 else my, 1 - mc if k & 1 else mc))
    return peers


def _halves_of(refs, c):
    out = []
    for r in refs:
        out += [r.at[c]] if len(r.shape) == 3 else [r.at[l, c] for l in range(r.shape[0])]
    return out


def _share_start(name, fulls, after):
    n = len(fulls)
    n_cp = sum(1 if f.ndim == 3 else f.shape[0] for f in fulls)

    def kern(*refs):
        o = refs[n + 1:2 * n + 1]
        ssem, rsem, token = refs[2 * n + 1:]
        mx, my, mc = _me()
        for q, half in enumerate(_halves_of(o, mc)):
            _rcopy(half, half, ssem.at[q], rsem.at[q], (mx, my, 1 - mc)).start()
        token[...] = jnp.zeros_like(token)

    dma = pltpu.SemaphoreType.DMA
    res = _pcall(
        kern, name=name, in_specs=[ANY] * (n + 1),
        out_specs=[ANY] * n + [SEM, SEM, pl.BlockSpec(memory_space=pltpu.VMEM)],
        out_shape=[_sds(f.shape, f.dtype) for f in fulls] + [dma((n_cp,)), dma((n_cp,)), _sds((8, LANE), F32)],
        input_output_aliases={t: t for t in range(n)},
        compiler_params=pltpu.CompilerParams(has_side_effects=_DATAFLOW),
    )(*fulls, after)
    return (list(res[:n]), res[n], res[n + 1]), res[n + 2]


def _share_wait(name, fulls, ssem, rsem, after):
    n = len(fulls)

    def kern(*refs):
        o = refs[:n]
        ssem_ref, rsem_ref = refs[n], refs[n + 1]
        mx, my, mc = _me()
        sib = (mx, my, 1 - mc)
        for q, (theirs, mine) in enumerate(zip(_halves_of(o, 1 - mc), _halves_of(o, mc))):
            _rcopy(theirs, theirs, ssem_ref.at[q], rsem_ref.at[q], sib).wait_recv()
            _rcopy(mine, mine, ssem_ref.at[q], rsem_ref.at[q], sib).wait_send()

    return list(_pcall(
        kern, name=name, in_specs=[ANY] * n + [SEM, SEM, ANY], out_specs=[ANY] * n,
        out_shape=[_sds(f.shape, f.dtype) for f in fulls], input_output_aliases={t: t for t in range(n)},
        compiler_params=pltpu.CompilerParams(has_side_effects=_DATAFLOW),
    )(*fulls, ssem, rsem, after))


def _tiled_sp(name, fn, grid, sp, ins, outs, into=None):
    n_in = len(ins)
    dest = [] if into is None else [into]

    def kern(sp_ref, *refs):
        tout = fn([r[...] for r in refs[:n_in]])
        for r, v in zip(refs[n_in + len(dest):], tout):
            r[...] = v.astype(r.dtype)

    gs = pltpu.PrefetchScalarGridSpec(num_scalar_prefetch=1, grid=tuple(grid),
                                      in_specs=[s for _, s in ins] + [ANY] * len(dest), out_specs=[s for _, s in outs])
    res = _pcall(kern, name=name, grid_spec=gs, out_shape=[o for o, _ in outs], compiler_params=_cparams(),
                 input_output_aliases={1 + n_in: 0} if dest else {})(sp, *[a for a, _ in ins], *dest)
    return list(res)


def _row_tile(rows, cols, itemsize=4, budget=2 * 1024 * 1024):
    tr = rows
    while tr * cols * itemsize > budget and tr % 32 == 0:
        tr //= 2
    return tr


def _place_big(shards, place, dep=None):
    slots = []
    for tag, s, layer in shards:
        rr, cc = s.shape[2], s.shape[3]
        tr = _row_tile(rr, cc)
        (slot,) = _tiled_sp(
            f"place_{tag}", lambda tin: [tin[0]], (2, rr // tr), place,
            [(s, pl.BlockSpec((None, None, tr, cc), lambda h, i, sp, layer=layer: (layer, h, i, 0)))]
            + [(d, pl.BlockSpec(d.shape, lambda h, i, sp: (0, 0))) for d in _behind(dep)],
            [(_sds((4, 2, rr, cc), BF16), pl.BlockSpec((None, None, tr, cc), lambda h, i, sp: (sp[0], h, i, 0)))])
        slots.append(slot)
    return slots


def _allreduce_small_begin(vec, place, after, during):
    hr = vec.shape[0] // 2
    tr = _row_tile(hr, LANE)
    blk = (None, None, tr, LANE)
    (pair,) = _tiled_sp(
        "small_place", lambda tin: [tin[0]], (2, hr // tr), place,
        [(vec.reshape(2, hr, LANE), pl.BlockSpec((None, tr, LANE), lambda h, i, sp: (h, i, 0)))],
        [(_sds((2, 2, hr, LANE), F32), pl.BlockSpec(blk, lambda h, i, sp: (sp[1], h, i, 0)))])
    crossing, token = _share_start("small_share_start", [pair.reshape(2, 2 * hr, LANE)], place)
    (pair,) = _share_wait("small_share_wait", *crossing, during(token))
    pair = pair.reshape(2, 2, hr, LANE)
    (slot,) = _tiled_sp(
        "small_pair_add", lambda tin: [tin[0] + tin[1]], (2, hr // tr), place,
        [(pair, pl.BlockSpec(blk, lambda h, i, sp: (0, h, i, 0))),
         (pair, pl.BlockSpec(blk, lambda h, i, sp: (1, h, i, 0)))],
        [(_sds((4, 2, hr, LANE), F32), pl.BlockSpec(blk, lambda h, i, sp: (sp[0], h, i, 0)))])
    fly, sems, token = _gather_start("small_start", [slot], ((0,),), after)
    return (fly, sems), token


def _allreduce_small_end(state, after):
    fly, sems = state
    (chips,) = _swap_halves("small_swap", _gather_wait("small_wait", fly, *sems, after))
    hr = chips.shape[2]
    tr = _row_tile(hr, LANE)
    blk = (None, None, tr, LANE)
    (total,) = _tiled(
        "small_chip_sum", lambda ids, tin, vin: ([((tin[0] + tin[1]) + tin[2]) + tin[3]], []), (2, hr // tr),
        [(chips, pl.BlockSpec(blk, lambda h, i, _j=j: (_j, h, i, 0))) for j in range(4)], [],
        [(_sds((2, hr, LANE), F32), pl.BlockSpec((None, tr, LANE), lambda h, i: (h, i, 0)))])
    return total.reshape(2 * hr, LANE)


SEM =pl.BlockSpec(memory_space=pltpu.SEMAPHORE)
_DATAFLOW = pltpu.SideEffectType.DATAFLOW_SIDE_EFFECTING


def _gather_start(name, slots, groups, after):
    n = len(slots)

    def kern(*refs):
        o = refs[n + 1:2 * n + 1]
        sems, token = refs[2 * n + 1:-1], refs[-1]
        mx, my, mc = _me()
        j0 = 2 * mx + my
        for gi, grp in enumerate(groups):
            for k, t in enumerate(grp):
                for q, (qx, qy) in enumerate(_other_chips(mx, my)):
                    _rcopy(o[t].at[j0, mc], o[t].at[j0, mc], sems[2 * gi].at[3 * k + q],
                           sems[2 * gi + 1].at[3 * k + q], (qx, qy, mc)).start()
        token[...] = jnp.zeros_like(token)

    sem_shapes = []
    for grp in groups:
        sem_shapes += [pltpu.SemaphoreType.DMA((3 * len(grp),))] * 2
    res = _pcall(
        kern, name=name, in_specs=[ANY] * (n + 1),
        out_specs=[ANY] * n + [SEM] * len(sem_shapes) + [pl.BlockSpec(memory_space=pltpu.VMEM)],
        out_shape=[_sds(w.shape, w.dtype) for w in slots] + sem_shapes + [_sds((8, LANE), F32)],
        input_output_aliases={t: t for t in range(n)},
        compiler_params=pltpu.CompilerParams(has_side_effects=_DATAFLOW),
    )(*slots, after)
    return list(res[:n]), list(res[n:-1]), res[-1]


def _gather_wait(name, bufs, ssem, rsem, after):
    n = len(bufs)

    def kern(*refs):
        b = refs[:n]
        ssem_ref, rsem_ref = refs[n], refs[n + 1]
        mx, my, mc = _me()
        j0 = 2 * mx + my
        for k in range(n):
            for q, (qx, qy) in enumerate(_other_chips(mx, my)):
                jq = 2 * qx + qy
                _rcopy(b[k].at[jq, mc], b[k].at[jq, mc], ssem_ref.at[3 * k + q], rsem_ref.at[3 * k + q],
                       (qx, qy, mc)).wait_recv()
                _rcopy(b[k].at[j0, mc], b[k].at[j0, mc], ssem_ref.at[3 * k + q], rsem_ref.at[3 * k + q],
                       (qx, qy, mc)).wait_send()

    return list(_pcall(
        kern, name=name, in_specs=[ANY] * n + [SEM, SEM, ANY], out_specs=[ANY] * n,
        out_shape=[_sds(w.shape, w.dtype) for w in bufs], input_output_aliases={k: k for k in range(n)},
        compiler_params=pltpu.CompilerParams(has_side_effects=_DATAFLOW),
    )(*bufs, ssem, rsem, after))


def _swap_halves(name, bufs):
    n = len(bufs)

    def kern(*refs):
        o = refs[n:2 * n]
        ss, rs = refs[2 * n:]
        mx, my, mc = _me()
        sib = (mx, my, 1 - mc)
        sends = []
        for k in range(n):
            for q, (qx, qy) in enumerate(_other_chips(mx, my)):
                jq = 2 * qx + qy
                cp = _rcopy(o[k].at[jq, mc], o[k].at[jq, mc], ss.at[3 * k + q], rs.at[3 * k + q], sib)
                cp.start()
                sends.append(cp)
        for k in range(n):
            for q, (qx, qy) in enumerate(_other_chips(mx, my)):
                jq = 2 * qx + qy
                _rcopy(o[k].at[jq, 1 - mc], o[k].at[jq, 1 - mc], ss.at[3 * k + q], rs.at[3 * k + q], sib).wait_recv()
        for cp in sends:
            cp.wait_send()

    dma = pltpu.SemaphoreType.DMA
    return list(_pcall(
        kern, name=name, in_specs=[ANY] * n, out_specs=[ANY] * n,
        out_shape=[_sds(w.shape, w.dtype) for w in bufs], input_output_aliases={k: k for k in range(n)},
        scratch_shapes=[dma((3 * n,)), dma((3 * n,))],
    )(*bufs))


def _swap_start(name, bufs, after):
    n = len(bufs)

    def kern(*refs):
        o = refs[n + 1:2 * n + 1]
        ssem, rsem, token = refs[2 * n + 1:]
        mx, my, mc = _me()
        for k in range(n):
            for q, (qx, qy) in enumerate(_other_chips(mx, my)):
                jq = 2 * qx + qy
                _rcopy(o[k].at[jq, mc], o[k].at[jq, mc], ssem.at[3 * k + q], rsem.at[3 * k + q], (mx, my, 1 - mc)).start()
        token[...] = jnp.zeros_like(token)

    dma = pltpu.SemaphoreType.DMA
    res = _pcall(
        kern, name=name, in_specs=[ANY] * (n + 1),
        out_specs=[ANY] * n + [SEM, SEM, pl.BlockSpec(memory_space=pltpu.VMEM)],
        out_shape=[_sds(w.shape, w.dtype) for w in bufs] + [dma((3 * n,)), dma((3 * n,)), _sds((8, LANE), F32)],
        input_output_aliases={k: k for k in range(n)},
        compiler_params=pltpu.CompilerParams(has_side_effects=_DATAFLOW),
    )(*bufs, after)
    return (list(res[:n]), res[n], res[n + 1]), res[n + 2]


def _swap_wait(name, bufs, ssem, rsem, after):
    n = len(bufs)

    def kern(*refs):
        b = refs[:n]
        ssem_ref, rsem_ref = refs[n], refs[n + 1]
        mx, my, mc = _me()
        sib = (mx, my, 1 - mc)
        for k in range(n):
            for q, (qx, qy) in enumerate(_other_chips(mx, my)):
                jq = 2 * qx + qy
                _rcopy(b[k].at[jq, 1 - mc], b[k].at[jq, 1 - mc], ssem_ref.at[3 * k + q], rsem_ref.at[3 * k + q],
                       sib).wait_recv()
                _rcopy(b[k].at[jq, mc], b[k].at[jq, mc], ssem_ref.at[3 * k + q], rsem_ref.at[3 * k + q],
                       sib).wait_send()

    return list(_pcall(
        kern, name=name, in_specs=[ANY] * n + [SEM, SEM, ANY], out_specs=[ANY] * n,
        out_shape=[_sds(w.shape, w.dtype) for w in bufs], input_output_aliases={k: k for k in range(n)},
        compiler_params=pltpu.CompilerParams(has_side_effects=_DATAFLOW),
    )(*bufs, ssem, rsem, after))


def _to_sibling(mx, my, mc):
    return [((j, 1 - mc), j, (mx, my, 1 - mc)) for j in range(4)]


def _to_chips(mx, my, mc):
    return [((2 * qx + qy,), q, (qx, qy, mc)) for q, (qx, qy) in enumerate(_other_chips(mx, my))]


def _to_all7(mx, my, mc):
    return [((0,), k, dev) for k, dev in enumerate(_peers7(mx, my, mc))]


def _send_start(name, srcs, plan, land_shapes, after):
    n = len(srcs)
    per = len(plan(0, 0, 0))

    def kern(*refs):
        s, land = refs[n + 1:2 * n + 1], refs[2 * n + 1:3 * n + 1]
        ssem, rsem, token = refs[3 * n + 1:]
        for k in range(n):
            for q, (idx, slot, dev) in enumerate(plan(*_me())):
                _rcopy(s[k].at[idx], land[k].at[slot], ssem.at[per * k + q], rsem.at[per * k + q], dev).start()
        token[...] = jnp.zeros_like(token)

    dma = pltpu.SemaphoreType.DMA
    res = _pcall(
        kern, name=name, in_specs=[ANY] * (n + 1),
        out_specs=[ANY] * (2 * n) + [SEM, SEM, pl.BlockSpec(memory_space=pltpu.VMEM)],
        out_shape=[_sds(s.shape, s.dtype) for s in srcs] + [_sds(ls, s.dtype) for ls, s in zip(land_shapes, srcs)]
        + [dma((per * n,)), dma((per * n,)), _sds((8, LANE), F32)],
        input_output_aliases={k: k for k in range(n)},
        compiler_params=pltpu.CompilerParams(has_side_effects=_DATAFLOW),
    )(*srcs, after)
    return (list(res[:n]), list(res[n:2 * n]), res[2 * n], res[2 * n + 1]), res[2 * n + 2]


def _send_wait(name, srcs, lands, ssem, rsem, plan, after):
    n = len(srcs)
    per = len(plan(0, 0, 0))

    def kern(*refs):
        s, land = refs[:n], refs[n:2 * n]
        ssem_ref, rsem_ref = refs[2 * n], refs[2 * n + 1]
        for k in range(n):
            for q, (idx, slot, dev) in enumerate(plan(*_me())):
                cp = _rcopy(s[k].at[idx], land[k].at[slot], ssem_ref.at[per * k + q], rsem_ref.at[per * k + q], dev)
                cp.wait_recv()
                cp.wait_send()

    res = _pcall(
        kern, name=name, in_specs=[ANY] * (2 * n) + [SEM, SEM, ANY], out_specs=[ANY] * (2 * n),
        out_shape=[_sds(a.shape, a.dtype) for a in list(srcs) + list(lands)],
        input_output_aliases={k: k for k in range(2 * n)},
        compiler_params=pltpu.CompilerParams(has_side_effects=_DATAFLOW),
    )(*srcs, *lands, ssem, rsem, after)
    return list(res[:n]), list(res[n:])


def _reduce_begin(tag, parts, after):
    return _send_start(f"pair_start_{tag}", parts, _to_sibling, [(4,) + p.shape[2:] for p in parts], after)


def _reduce_mid(tag, pairing, place, after):
    parts, theirs = _send_wait(f"pair_wait_{tag}", *pairing, _to_sibling, after)
    sums = []
    for k, (p, o) in enumerate(zip(parts, theirs)):
        rr, cc = p.shape[2], p.shape[3]
        tr = _row_tile(rr, cc)
        (s_k,) = _tiled_sp(
            f"pair_add_{tag}{k}", lambda tin: [tin[0].astype(F32) + tin[1].astype(F32)], (4, rr // tr), place,
            [(p, pl.BlockSpec((None, None, tr, cc), lambda j, i, sp: (j, sp[1], i, 0))),
             (o, pl.BlockSpec((None, tr, cc), lambda j, i, sp: (j, i, 0)))],
            [(_sds((4, rr, cc), BF16), pl.BlockSpec((None, tr, cc), lambda j, i, sp: (j, i, 0)))])
        sums.append(s_k)
    return _send_start(f"chips_start_{tag}", sums, _to_chips, [(3,) + s.shape[1:] for s in sums], theirs[0])


def _reduce_end(tag, flying, place, after, layer=None, into=None):
    sums, lands = _send_wait(f"chips_wait_{tag}", *flying, _to_chips, after)
    fulls = []
    for k, (s, q) in enumerate(zip(sums, lands)):
        rr, cc = q.shape[1], q.shape[2]
        tr = _row_tile(rr, cc)

        def add4(tin):
            return [((tin[0].astype(F32) + tin[1].astype(F32)) + tin[2].astype(F32)) + tin[3].astype(F32)]

        ins = [(s, pl.BlockSpec((None, tr, cc), lambda i, sp: (sp[0], i, 0)))]
        ins += [(q, pl.BlockSpec((None, tr, cc), lambda i, sp, _k=kk: (_k, i, 0))) for kk in range(3)]
        if layer is None:
            out = (_sds((2, rr, cc), F32), pl.BlockSpec((None, tr, cc), lambda i, sp: (sp[1], i, 0)))
        else:
            out = (_sds((layer[1], 2, rr, cc), F32),
                   pl.BlockSpec((None, None, tr, cc), lambda i, sp, _l=layer[0]: (_l, sp[1], i, 0)))
        (f_k,) = _tiled_sp(f"chip_add_{tag}{k}", add4, (rr // tr,), place, ins, [out],
                           None if into is None else into[k])
        fulls.append(f_k)
    return fulls


def _pack(parts, PACK_ROWS=PACK_ROWS):
    flat, offs, pos = [], [], 0
    for p in parts:
        v = p.reshape(-1).astype(F32)
        n = -(-v.shape[0] // LANE) * LANE
        flat.append(jnp.pad(v, (0, n - v.shape[0])))
        offs.append((pos, v.shape[0], p.shape))
        pos += n
    total = -(-pos // (PACK_ROWS * LANE)) * PACK_ROWS * LANE
    flat.append(jnp.zeros((total - pos,), F32))
    return jnp.concatenate(flat).reshape(-1, LANE), offs


def _unpack(vec, offs):
    v = vec.reshape(-1)
    return [v[p:p + n].reshape(shape) for p, n, shape in offs]


def _adamw_math(wv, gv, mv, vv):
    bc1 = 1.0 - ADAM_B1 ** ADAM_STEP
    bc2 = 1.0 - ADAM_B2 ** ADAM_STEP
    mn = ADAM_B1 * mv + (1.0 - ADAM_B1) * gv
    vn = ADAM_B2 * vv + (1.0 - ADAM_B2) * (gv * gv)
    delta = -ADAM_LR * ((mn / bc1) / (jnp.sqrt(vn / bc2) + ADAM_EPS) + ADAM_WD * wv)
    return delta, mn, vn


def _adamw(name, w, g, m, v, dep=None):
    rows, cols = w.shape
    tr = rows
    for cand in (512, 256, 128, 64, 32, 16, 8):
        if rows % cand == 0 and cand * cols * 4 <= 2 * 1024 * 1024:
            tr = cand
            break

    def fn(ids, tin, vin):
        return list(_adamw_math(*tin)), []

    spec = pl.BlockSpec((tr, cols), lambda i: (i, 0))
    outs = [(_sds((rows, cols), F32), spec)] * 3
    return _tiled(name, fn, (rows // tr,), [(a, spec) for a in (w, g, m, v)], _behind(dep), outs)


def _adamw_many(name, ws, gs, ms, vs):
    n = len(ws)
    views = [(-1, a.shape[-1]) if a.ndim > 1 else (1, -1) for a in ws]
    flat = lambda arrs: [a.reshape(vw) for a, vw in zip(arrs, views)]

    def kern(*refs):
        ins, outs = refs[:4 * n], refs[4 * n:]
        for t in range(n):
            res = _adamw_math(*[ins[q * n + t][...] for q in range(4)])
            for q in range(3):
                outs[q * n + t][...] = res[q]

    shapes = [_sds(a.shape, F32) for a in flat(ws)]
    res = _pcall(kern, name=name, out_shape=shapes * 3, compiler_params=_cparams(),
                 )(*flat(ws), *flat(gs), *flat(ms), *flat(vs))
    back = lambda part: [a.reshape(w.shape) for a, w in zip(part, ws)]
    return back(res[:n]), back(res[n:2 * n]), back(res[2 * n:])


def _pos_embed():
    n_rows = T // GRID_W
    q = D // 4
    omega = 1.0 / (10000.0 ** (jnp.arange(q, dtype=F32) / q))
    er = jnp.arange(n_rows, dtype=jnp.int32).astype(F32)[:, None] * omega[None, :]
    ec = jnp.arange(GRID_W, dtype=jnp.int32).astype(F32)[:, None] * omega[None, :]
    by_row = jnp.concatenate([jnp.sin(er), jnp.cos(er)], axis=-1)
    by_col = jnp.concatenate([jnp.sin(ec), jnp.cos(ec)], axis=-1)
    return jnp.concatenate([jnp.repeat(by_row, GRID_W, axis=0), jnp.tile(by_col, (n_rows, 1))], axis=-1)


def _dense_gates(w_a, w_x):
    rows = jnp.stack([w_a[0], w_x[0], w_a[1], w_x[1]]).reshape(4, 2, RH, BLK)
    mask, spread = _block_mask(), _block_spread().T.astype(BF16)

    def kern(r_ref, m_ref, s_ref, o_ref):
        tiled = jnp.dot(r_ref[...].astype(BF16), s_ref[...], preferred_element_type=F32)
        o_ref[...] = (tiled * m_ref[...]).astype(o_ref.dtype)

    return _pcall(
        kern, name="gates_dense", grid=(2, 4),
        in_specs=[pl.BlockSpec((None, None, RH, BLK), lambda h, q: (q, h, 0, 0)),
                  pl.BlockSpec((RH, RH), lambda h, q: (0, 0)), pl.BlockSpec((BLK, RH), lambda h, q: (0, 0))],
        out_specs=pl.BlockSpec((None, RH, RH), lambda h, q: (h, 0, q)),
        out_shape=_sds((2, RH, NQ), BF16),
    )(rows, mask, spread)


def _block_mask():
    r = lax.broadcasted_iota(jnp.int32, (RH, RH), 0) // BLK
    c = lax.broadcasted_iota(jnp.int32, (RH, RH), 1) // BLK
    return (r == c).astype(F32)


def _block_spread():
    c = lax.broadcasted_iota(jnp.int32, (RH, BLK), 0) % BLK
    j = lax.broadcasted_iota(jnp.int32, (RH, BLK), 1)
    return (c == j).astype(F32)


def _fold_blocks(dense, mask, spread):
    return jnp.dot(dense * mask, spread, preferred_element_type=F32, precision=lax.Precision.HIGHEST)


def _gate_block_grads(folded):
    per = N_BLK // 2
    kinds = [jnp.concatenate([folded[h, q].reshape(per, BLK, BLK) for h in range(2)], axis=0) for q in range(4)]
    return jnp.stack([kinds[0], kinds[2]]), jnp.stack([kinds[1], kinds[3]])


def _gate_bias_dense(b_a, b_x):
    cols = []
    for h in range(2):
        for src in (b_a[0], b_x[0], b_a[1], b_x[1]):
            cols.append(src.reshape(R)[h * RH:(h + 1) * RH])
    return jnp.concatenate(cols).reshape(1, 2 * NQ)


def _gate_bias_grads(dgb):
    v = dgb.reshape(2, 4, RH)
    kinds = [jnp.concatenate([v[0, q], v[1, q]]).reshape(N_BLK, BLK) for q in range(4)]
    return jnp.stack([kinds[0], kinds[2]]), jnp.stack([kinds[1], kinds[3]])


def _residual_epilogue(next_norm):
    def epi(acc, ex):
        x_new = ex[0] + ex[1] * acc
        outs = [acc, x_new]
        if next_norm:
            outs.append(_norm_mod(x_new, ex[-3], ex[-2], ex[-1]))
        return outs
    return epi


def _mlp_fwd(tag, x_in, h, gate, w_in, w_out, next_norm=None, dep=None):
    tm = MM_TILE
    (r,) = _mm(f"{tag}_in", h, w_in, _NN, (T // tm, 4, 1),
               pl.BlockSpec((tm, D), lambda i, j, k: (i, 0)), pl.BlockSpec((None, D, D), lambda i, j, k: (j, 0, 0)),
               [(_sds((T, FF), BF16), pl.BlockSpec((tm, D), lambda i, j, k: (i, j)))], (tm, D),
               extra=[(d_, _full_spec(d_)) for d_ in _behind(dep)], epi=lambda acc, ex: [jnp.maximum(acc, 0.0)])
    row_spec = pl.BlockSpec((tm, D), lambda i, j, k: (i, 0))
    outs = [(_sds((T, D), F32), row_spec)] * 2 + ([(_sds((T, D), BF16), row_spec)] if next_norm else [])
    res = _mm(f"{tag}_out", r, w_out, _NN, (T // tm, 1, FF // D),
              pl.BlockSpec((tm, D), lambda i, j, k: (i, k)), pl.BlockSpec((D, D), lambda i, j, k: (k, 0)),
              outs, (tm, D),
              extra=[(x_in, row_spec), (gate, _full_spec(gate))] + [(v, _full_spec(v)) for v in next_norm or ()],
              a_pre=lambda a: a * a, epi=_residual_epilogue(next_norm))
    return dict(h=h, r=r, o=res[0], x_in=x_in), res[1], (res[2] if next_norm else None)


def _behind(dep):
    return [] if dep is None else [dep]


def _gate_bwd(tag, dx, o, gate, dep=None):
    def fn(ids, t, v):
        d_o = t[0] * v[0]
        return [d_o], [_sum0(t[0] * t[1]), _sum0(d_o)]
    return _tiled(f"{tag}_gate_bwd", fn, (T // ROW_TILE,), [_rows(dx), _rows(o)], [gate] + _behind(dep),
                  [_orow(T, D, BF16)], [(1, D), (1, D)])


def _norm_bwd(tag, dx_res, dh, dh_off, x, g_norm, sc, with_dx=True, dep=None):
    n_t = x.shape[0] // ROW_TILE

    def fn(ids, t, v):
        if with_dx:
            dres, dhv, xv = t
        else:
            dhv, xv = t
        dxv, d_sh, d_sc, d_g = _norm_mod_bwd(dhv, xv, v[0], v[1])
        return ([dres + dxv] if with_dx else []), [d_sh, d_sc, d_g]

    ins = ([_rows(dx_res)] if with_dx else []) + [_rows(dh, off=dh_off), _rows(x)]
    outs = [_orow(x.shape[0], D, F32)] if with_dx else []
    return _tiled(f"{tag}_norm_bwd", fn, (n_t,), ins, [g_norm, sc] + _behind(dep), outs, [(1, D)] * 3)


def _mlp_bwd(tag, dx, saved, g_norm, sc, gate, w_in, w_out, dep=None):
    d_o, d_gate, _ = _gate_bwd(tag, dx, saved["o"], gate, dep)
    tm = MM_TILE
    r = saved["r"]
    (da,) = _mm(f"{tag}_dz", d_o, w_out, _NT, (T // tm, FF // D, 1),
                pl.BlockSpec((tm, D), lambda i, j, k: (i, 0)), pl.BlockSpec((D, D), lambda i, j, k: (j, 0)),
                [(_sds((T, FF), BF16), pl.BlockSpec((tm, D), lambda i, j, k: (i, j)))], (tm, D),
                extra=[(r, pl.BlockSpec((tm, D), lambda i, j, k: (i, j)))],
                epi=lambda acc, ex: [acc * (2.0 * ex[0].astype(F32))])
    tk = MM_TILE
    (dw_out,) = _mm(f"{tag}_dwout", r, d_o, _TN, (FF // tm, 1, T // tk),
                    pl.BlockSpec((tk, tm), lambda i, j, k: (k, i)), pl.BlockSpec((tk, D), lambda i, j, k: (k, 0)),
                    [(_sds((FF, D), BF16), pl.BlockSpec((tm, D), lambda i, j, k: (i, 0)))], (tm, D),
                    a_pre=lambda a: a * a)
    (dh,) = _mm(f"{tag}_dh", da, w_in, _NT, (T // tm, 1, 4),
                pl.BlockSpec((tm, D), lambda i, j, k: (i, k)), pl.BlockSpec((None, D, D), lambda i, j, k: (k, 0, 0)),
                [(_sds((T, D), F32), pl.BlockSpec((tm, D), lambda i, j, k: (i, 0)))], (tm, D))
    (dw_in,) = _mm(f"{tag}_dwin", saved["h"], da, _TN, (D // tm, 4, T // tk),
                   pl.BlockSpec((tk, tm), lambda i, j, k: (k, i)), pl.BlockSpec((tk, D), lambda i, j, k: (k, j)),
                   [(_sds((4, D, D), BF16), pl.BlockSpec((None, tm, D), lambda i, j, k: (j, i, 0)))], (tm, D))
    dx_in, d_sh, d_sc, d_g = _norm_bwd(tag, dx, dh, 0, saved["x_in"], g_norm, sc)
    return dx_in, dw_in, dw_out, dict(sh=d_sh, sc=d_sc, gate=d_gate, g_norm=d_g)


def _local_step(x, ctx, tgt, mods, cmods, norm_g, final_g, rec, conf, wg, on_grads=None, wg_pre=None, on_later=None):
    on_grads = on_grads or (lambda group, dws: None)
    wg_pre = wg_pre or (lambda group, after: None)
    on_later = on_later or (lambda after: None)
    n_t = T // ROW_TILE
    row = lambda v: v.reshape(1, -1)
    m0 = [row(mods[0, q]) for q in range(6)]
    m1 = [row(mods[1, q]) for q in range(6)]
    g00, g01, g10, g11 = (row(norm_g[0, 0]), row(norm_g[0, 1]), row(norm_g[1, 0]), row(norm_g[1, 1]))
    csh, csc = row(cmods[0]), row(cmods[1])
    pos = _pos_embed()

    def prep0(ids, t, v):
        cx, xv, pv = t
        is_ctx = ids[0] == 0
        xin = jnp.where(is_ctx, cx, xv + pv)
        sh = jnp.where(is_ctx, v[3], v[1])
        sc = jnp.where(is_ctx, v[4], v[2])
        return [_norm_mod(xin, v[0], sc, sh), xv + pv], []

    dep = wg_pre("rec_in", csh)
    hcat, x0 = _tiled(
        "prep0", prep0, (N_SCAN,),
        [(ctx, pl.BlockSpec((ROW_TILE, D), lambda i: (0, 0))), _rows(x, off=-1, clamp_lo=True),
         _rows(pos, off=-1, clamp_lo=True)],
        [g00, m0[0], m0[1], csh, csc] + _behind(dep),
        [_orow(TA, D, BF16), _orow(T, D, F32, off=-1, clamp_lo=True)])

    tm_a = REC_TILE
    w_rin = wg("rec_in", hcat)["rec_w_in"]
    (a_in,) = _mm("rec_in", hcat, w_rin, _NN, (TA // tm_a, 4, 1),
                  pl.BlockSpec((tm_a, D), lambda i, j, k: (i, 0)),
                  pl.BlockSpec((None, D, RH), lambda i, j, k: (j, 0, 0)),
                  [(_sds((TA, 2 * R), F32), pl.BlockSpec((tm_a, RH), lambda i, j, k: (i, j)))], (tm_a, RH))
    rec_starts = (0, 1)
    u = _dwconv("rec_conv", a_in, R // CW_REC, rec["conv_w"], row(rec["conv_b"]), 1, rec_starts, R, CW_REC)
    wbd = _dense_gates(rec["w_a"], rec["w_x"])
    gbias = _gate_bias_dense(rec["b_a"], rec["b_x"])
    lam = rec["lam"]
    a_f, b_f, a_r, b_r = _tiled("rg_fwd", _rg_fwd_fn, (TA // RG_TILE,), [_rows(u, tm=RG_TILE)], [wbd, gbias, lam],
                                [_orow(TA, R, F32, tm=RG_TILE)] * 4, vec_refs=True)
    dep = wg_pre("rec_out", a_f)
    dep = wg_pre("mlp0", a_f if dep is None else dep)
    y_f, y_r, hin_f, hin_r = _scan_fwd(a_f, b_f, a_r, b_r)

    def rec_mid(ids, t, v):
        gp, yf, yr = t
        g, _ = _gelu(gp)
        return [g * (yf + yr)], []

    (m_rec,) = _tiled("rec_mid", rec_mid, (n_t,),
                      [_rows(a_in, R, off=1), _rows(y_f, off=1), _rows(y_r, off=1)], _behind(dep),
                      [_orow(T, R, BF16)])
    tm = MM_TILE
    row_spec = pl.BlockSpec((tm, D), lambda i, j, k: (i, 0))
    norm_mlp0 = (g01, m0[4], m0[3])
    w_rout = wg("rec_out", m_rec)["rec_w_out"]
    o_rec, x1, h_mlp0 = _mm(
        "rec_out", m_rec, w_rout, _NN, (T // tm, 1, 1),
        pl.BlockSpec((tm, R), lambda i, j, k: (i, 0)), pl.BlockSpec((R, D), lambda i, j, k: (0, 0)),
        [(_sds((T, D), F32), row_spec)] * 2 + [(_sds((T, D), BF16), row_spec)], (tm, D),
        extra=[(x0, row_spec), (m0[2], _full_spec(m0[2]))] + [(v, _full_spec(v)) for v in norm_mlp0],
        epi=_residual_epilogue(norm_mlp0))
    w_m0 = wg("mlp0", x1)
    dep = wg_pre("conf", x1)
    mlp0, x2, h1 = _mlp_fwd("mlp0", x1, h_mlp0, m0[5], w_m0["w_in"], w_m0["w_out"], (g10, m1[1], m1[0]), dep)

    b_pw1 = row(conf["b_pw1"])
    w_cf = wg("conf", x2)
    dep = wg_pre("mlp1", x2)
    (pre,) = _mm("conf_pw1", h1, w_cf["conf_w_pw1"], _NN, (T // tm, 4, 1),
                 pl.BlockSpec((tm, D), lambda i, j, k: (i, 0)),
                 pl.BlockSpec((None, D, D // 2), lambda i, j, k: (j, 0, 0)),
                 [(_sds((T, 2 * D), F32), pl.BlockSpec((tm, D // 2), lambda i, j, k: (i, j)))], (tm, D // 2),
                 extra=[(b_pw1, pl.BlockSpec((1, D // 2), lambda i, j, k: (0, j)))]
                 + [(d_, _full_spec(d_)) for d_ in _behind(dep)],
                 epi=lambda acc, ex: [acc + ex[0]])
    (zg,) = _tiled("conf_glu", lambda ids, t, v: ([t[0] * _sigmoid(t[1])], []), (n_t,),
                   [_rows(pre, D, col=0), _rows(pre, D, col=1)], [], [_orow(T, D, F32)])
    conf_starts = (0,)
    zc = _dwconv("conf_conv", zg, 0, conf["conv_w"], row(conf["conv_b"]), CONF_KW // 2, conf_starts, D, CW_CONF)
    ln_g, ln_b = row(conf["ln_g"]), row(conf["ln_b"])

    def ln_silu(ids, t, v):
        nh, _ = _layernorm_parts(t[0])
        ln = nh * v[0] + v[1]
        return [ln * _sigmoid(ln)], []

    (s_conf,) = _tiled("conf_ln", ln_silu, (n_t,), [_rows(zc)], [ln_g, ln_b], [_orow(T, D, BF16)])
    b_pw2 = row(conf["b_pw2"])
    norm_mlp1 = (g11, m1[4], m1[3])
    pw2_epi = _residual_epilogue(norm_mlp1)
    y_conf, x3, h_mlp1 = _mm(
        "conf_pw2", s_conf, w_cf["conf_w_pw2"], _NN, (T // tm, 1, 1),
        row_spec, pl.BlockSpec((D, D), lambda i, j, k: (0, 0)),
        [(_sds((T, D), F32), row_spec)] * 2 + [(_sds((T, D), BF16), row_spec)], (tm, D),
        extra=[(x2, row_spec), (m1[2], _full_spec(m1[2])), (b_pw2, _full_spec(b_pw2))]
        + [(v, _full_spec(v)) for v in norm_mlp1],
        epi=lambda acc, ex: pw2_epi(acc + ex[2], ex))
    w_m1 = wg("mlp1", x3)
    mlp1, x4, _ = _mlp_fwd("mlp1", x3, h_mlp1, m1[5], w_m1["w_in"], w_m1["w_out"])

    fg = row(final_g)

    def head(ids, t, v):
        n, r = _rms(t[0])
        err = n * v[0] - t[1]
        d_out = err * (1.0 / D)
        dn = d_out * v[0]
        dxv = r * (dn - n * jnp.mean(dn * n, axis=-1, keepdims=True))
        part = jnp.sum(_sum0(err * err), axis=1, keepdims=True) * (0.5 / D)
        return [dxv], [part, _sum0(d_out * n)]

    dx4, loss, d_fg = _tiled("head", head, (n_t,), [_rows(x4), _rows(tgt)], [fg], [_orow(T, D, F32)],
                             [(1, 1), (1, D)])

    dx3, dw_in1, dw_out1, dm_mlp1 = _mlp_bwd("mlp1", dx4, mlp1, g11, m1[4], m1[5],
                                             w_m1["w_in"], w_m1["w_out"])
    dep = on_grads("mlp1", (dw_in1, dw_out1))
    d_y, d_g1c, d_bpw2 = _gate_bwd("conf", dx3, y_conf, m1[2], dep)
    tk = MM_TILE
    (dw_pw2,) = _mm("conf_dwpw2", s_conf, d_y, _TN, (D // tm, 1, T // tk),
                    pl.BlockSpec((tk, tm), lambda i, j, k: (k, i)), pl.BlockSpec((tk, D), lambda i, j, k: (k, 0)),
                    [(_sds((D, D), BF16), pl.BlockSpec((tm, D), lambda i, j, k: (i, 0)))], (tm, D))
    (ds,) = _mm("conf_ds", d_y, w_cf["conf_w_pw2"], _NT, (T // tm, 1, 1),
                pl.BlockSpec((tm, D), lambda i, j, k: (i, 0)), pl.BlockSpec((D, D), lambda i, j, k: (0, 0)),
                [(_sds((T, D), F32), pl.BlockSpec((tm, D), lambda i, j, k: (i, 0)))], (tm, D))
    dep = on_later(ds)

    def ln_silu_bwd(ids, t, v):
        dsv, zcv = t
        nh, rstd = _layernorm_parts(zcv)
        ln = nh * v[0] + v[1]
        sg = _sigmoid(ln)
        d_ln = dsv * (sg * (1.0 + ln * (1.0 - sg)))
        d_nh = d_ln * v[0]
        d_zc = rstd * (d_nh - jnp.mean(d_nh, axis=-1, keepdims=True)
                       - nh * jnp.mean(d_nh * nh, axis=-1, keepdims=True))
        return [d_zc], [_sum0(d_ln * nh), _sum0(d_ln)]

    d_zc, d_lng, d_lnb = _tiled("conf_ln_bwd", ln_silu_bwd, (n_t,), [_rows(ds), _rows(zc)],
                                [ln_g, ln_b] + _behind(dep), [_orow(T, D, F32)], [(1, D), (1, D)])
    d_zg = _dwconv("conf_conv_dx", d_zc, 0, conf["conv_w"], jnp.zeros((1, D), F32),
                   CONF_KW - 1 - CONF_KW // 2, conf_starts, D, CW_CONF, flip=True)

    def glu_bwd(ids, t, v):
        dz, pa, pb = t
        sg = _sigmoid(pb)
        d_a = dz * sg
        d_b = dz * pa * sg * (1.0 - sg)
        return [jnp.concatenate([d_a, d_b], axis=1)], [_sum0(d_a), _sum0(d_b)]

    d_pre, d_b1a, d_b1b = _tiled(
        "conf_glu_bwd", glu_bwd, (n_t,), [_rows(d_zg), _rows(pre, D, col=0), _rows(pre, D, col=1)], [],
        [_orow(T, 2 * D, BF16)], [(1, D), (1, D)])
    (dw_pw1,) = _mm("conf_dwpw1", h1, d_pre, _TN, (D // tm, 4, T // tk),
                    pl.BlockSpec((tk, tm), lambda i, j, k: (k, i)),
                    pl.BlockSpec((tk, D // 2), lambda i, j, k: (k, j)),
                    [(_sds((4, D, D // 2), BF16), pl.BlockSpec((None, tm, D // 2), lambda i, j, k: (j, i, 0)))],
                    (tm, D // 2))
    dep = on_grads("conf", (dw_pw1, dw_pw2))
    (dh1,) = _mm("conf_dh", d_pre, w_cf["conf_w_pw1"], _NT, (T // tm, 1, 4),
                 pl.BlockSpec((tm, D // 2), lambda i, j, k: (i, k)),
                 pl.BlockSpec((None, D, D // 2), lambda i, j, k: (k, 0, 0)),
                 [(_sds((T, D), F32), pl.BlockSpec((tm, D), lambda i, j, k: (i, 0)))], (tm, D))
    dx2, d_sh1c, d_sc1c, d_g10 = _norm_bwd("conf", dx3, dh1, 0, x2, g10, m1[1], dep=dep)
    dep = on_later(dx2)

    dx1, dw_in0, dw_out0, dm_mlp0 = _mlp_bwd("mlp0", dx2, mlp0, g01, m0[4], m0[5],
                                             w_m0["w_in"], w_m0["w_out"], dep)
    dep = on_grads("mlp0", (dw_in0, dw_out0))
    d_orec, d_g1r, _ = _gate_bwd("rec", dx1, o_rec, m0[2], dep)
    (dw_rout,) = _mm("rec_dwout", m_rec, d_orec, _TN, (R // RH, 1, T // tk),
                     pl.BlockSpec((tk, RH), lambda i, j, k: (k, i)), pl.BlockSpec((tk, D), lambda i, j, k: (k, 0)),
                     [(_sds((R, D), BF16), pl.BlockSpec((RH, D), lambda i, j, k: (i, 0)))], (RH, D))
    (dm_rec,) = _mm("rec_dm", d_orec, w_rout, _NT, (T // tm, 1, 1),
                    pl.BlockSpec((tm, D), lambda i, j, k: (i, 0)), pl.BlockSpec((R, D), lambda i, j, k: (0, 0)),
                    [(_sds((T, R), F32), pl.BlockSpec((tm, R), lambda i, j, k: (i, 0)))], (tm, R))
    dep = on_later(dm_rec)

    def rec_mid_bwd(ids, t, v):
        dmv, gp, yf, yr = t
        g, th = _gelu(gp)
        lat = ids[0] > 0
        d_gp = jnp.where(lat, dmv * (yf + yr) * _gelu_grad(gp, th), 0.0)
        dy = jnp.where(lat, dmv * g, 0.0)
        return [d_gp, dy], []

    d_a, dy = _tiled("rec_mid_bwd", rec_mid_bwd, (N_SCAN,),
                     [_rows(dm_rec, off=-1, clamp_lo=True), _rows(a_in, R), _rows(y_f), _rows(y_r)], _behind(dep),
                     [(_sds((TA, 2 * R), BF16), pl.BlockSpec((ROW_TILE, R), lambda i: (i, 0))), _orow(TA, R, F32)])
    da_f, db_f, da_r, db_r = _scan_bwd(dy, a_f, y_f, hin_f, a_r, y_r, hin_r)
    d_gpre, d_u, d_gbias, d_lam = _tiled(
        "rg_bwd", _rg_bwd_fn, (TA // RG_TILE,), [_rows(a, tm=RG_TILE) for a in (u, da_f, db_f, da_r, db_r)],
        [wbd, gbias, lam], [_orow(TA, 2 * NQ, BF16, tm=RG_TILE), _orow(TA, R, F32, tm=RG_TILE)],
        [(1, 2 * NQ), (1, 2 * R)], vec_refs=True)
    tk_a = REC_TILE
    d_a = _dwconv("rec_conv_dx", d_u, 0, rec["conv_w"], jnp.zeros((1, R), F32), REC_KW - 1 - 1,
                  rec_starts, R, CW_REC, flip=True, into=(d_a, R // CW_REC))
    (dw_rin,) = _mm("rec_dwin", hcat, d_a, _TN, (D // tm, 4, TA // tk_a),
                    pl.BlockSpec((tk_a, tm), lambda i, j, k: (k, i)), pl.BlockSpec((tk_a, RH), lambda i, j, k: (k, j)),
                    [(_sds((4, D, RH), BF16), pl.BlockSpec((None, tm, RH), lambda i, j, k: (j, i, 0)))], (tm, RH))
    dep = on_grads("rec", (dw_rin, dw_rout))
    (dhcat,) = _mm("rec_dh", d_a, w_rin, _NT, (TA // tm_a, 1, 4),
                   pl.BlockSpec((tm_a, RH), lambda i, j, k: (i, k)),
                   pl.BlockSpec((None, D, RH), lambda i, j, k: (k, 0, 0)),
                   [(_sds((TA, D), F32), pl.BlockSpec((tm_a, D), lambda i, j, k: (i, 0)))], (tm_a, D))
    dx0, d_sh1r, d_sc1r, d_g00 = _norm_bwd("rec", dx1, dhcat, 1, x0, g00, m0[1], dep=dep)
    dep = on_later(dx0)

    d_csh, d_csc, d_g00c = _norm_bwd("ctx", None, dhcat, 0, ctx, g00, csc, with_dx=False, dep=dep)
    blk_mask, blk_spread = _block_mask(), _block_spread()
    (d_wbd,) = _mm("rg_dw", u, d_gpre, _TN, (2, 2, TA // tk_a),
                   pl.BlockSpec((tk_a, RH), lambda i, j, k: (k, i)),
                   pl.BlockSpec((tk_a, NQ // 2), lambda i, j, k: (k, 2 * i + j)),
                   [(_sds((2, 4, RH, BLK), F32), pl.BlockSpec((None, 2, RH, BLK), lambda i, j, k: (i, j, 0, 0)))],
                   (RH, NQ // 2),
                   extra=[(blk_mask, _full_spec(blk_mask)), (blk_spread, _full_spec(blk_spread))]
                   + [(d, _full_spec(d)) for d in _behind(dep)],
                   epi=lambda acc, ex: [jnp.stack([_fold_blocks(acc[:, s * RH:(s + 1) * RH], ex[0], ex[1])
                                                   for s in range(2)])])
    d_cw_rec = _dwconv_wgrad("rec_conv_dw", d_u, a_in, R // CW_REC, REC_KW, 1, rec_starts, R, CW_REC, dep)
    d_cw_conf = _dwconv_wgrad("conf_conv_dw", d_zc, zg, 0, CONF_KW, CONF_KW // 2, conf_starts, D, CW_CONF, dep)

    big = dict(rec_w_in=dw_rin, rec_w_out=dw_rout, conf_w_pw1=dw_pw1, conf_w_pw2=dw_pw2,
               mlp_w_in=(dw_in0, dw_in1), mlp_w_out=(dw_out0, dw_out1))
    d_wa, d_wx = _gate_block_grads(d_wbd)
    d_ba, d_bx = _gate_bias_grads(d_gbias)
    d_mod = jnp.concatenate([
        d_sh1r, d_sc1r, d_g1r, dm_mlp0["sh"], dm_mlp0["sc"], dm_mlp0["gate"],
        d_sh1c, d_sc1c, d_g1c, dm_mlp1["sh"], dm_mlp1["sc"], dm_mlp1["gate"]], axis=1).reshape(2, 6 * D)
    small = dict(
        d_mod=d_mod, d_cmod=jnp.concatenate([d_csh, d_csc], axis=1),
        norm_g=jnp.concatenate([d_g00 + d_g00c, dm_mlp0["g_norm"], d_g10, dm_mlp1["g_norm"]], axis=1),
        rec_conv_w=d_cw_rec[:REC_KW], rec_conv_b=d_cw_rec[REC_KW], rec_lambda=d_lam.reshape(2, R),
        rec_w_a=d_wa, rec_b_a=d_ba, rec_w_x=d_wx, rec_b_x=d_bx,
        conf_b_pw1=jnp.concatenate([d_b1a, d_b1b], axis=1), conf_conv_w=d_cw_conf[:CONF_KW],
        conf_conv_b=d_cw_conf[CONF_KW], conf_ln_g=d_lng, conf_ln_b=d_lnb, conf_b_pw2=d_bpw2, final_g=d_fg)
    return loss.reshape(()), dx0, big, small


_BIG = ("rec_w_in", "rec_w_out", "conf_w_pw1", "conf_w_pw2", "mlp_w_in", "mlp_w_out")


def _halves(w):
    return w.reshape(w.shape[0], 2, w.shape[1] // 2, w.shape[2])


def _ada_fwd(c16, w_ada, b_shard):
    ns = w_ada.shape[2]
    tn = 512

    def kern(c_ref, w_ref, b_ref, o_ref):
        cv = c_ref[...]
        s = (cv * _sigmoid(cv)).astype(BF16)
        o_ref[...] = jnp.dot(s, w_ref[...].astype(BF16), preferred_element_type=F32) + b_ref[...]

    return _pcall(
        kern, name="ada_fwd", grid=(2, ns // tn),
        in_specs=[pl.BlockSpec((16, D), lambda l, j: (0, 0)), pl.BlockSpec((None, D, tn), lambda l, j: (l, 0, j)),
                  pl.BlockSpec((None, 1, tn), lambda l, j: (l, 0, j))],
        out_specs=pl.BlockSpec((None, 16, tn), lambda l, j: (l, 0, j)),
        out_shape=_sds((2, 16, ns), F32), compiler_params=_cparams(),
    )(c16, w_ada, b_shard)


def _ada_bwd(c16, dm16, w_ada):
    ns = w_ada.shape[2]
    tn = 512

    def kern(c_ref, dm_ref, w_ref, gw_ref, ds_ref):
        cv = c_ref[...]
        s = (cv * _sigmoid(cv)).astype(BF16)
        dm = dm_ref[...].astype(BF16)
        gw_ref[...] = lax.dot_general(s, dm, _TN, preferred_element_type=F32)

        @pl.when(jnp.logical_and(pl.program_id(0) == 0, pl.program_id(1) == 0))
        def _():
            ds_ref[...] = jnp.zeros_like(ds_ref)

        ds_ref[...] += lax.dot_general(dm, w_ref[...].astype(BF16), _NT, preferred_element_type=F32)

    return _pcall(
        kern, name="ada_bwd", grid=(2, ns // tn),
        in_specs=[pl.BlockSpec((16, D), lambda l, j: (0, 0)), pl.BlockSpec((None, 16, tn), lambda l, j: (l, 0, j)),
                  pl.BlockSpec((None, D, tn), lambda l, j: (l, 0, j))],
        out_specs=[pl.BlockSpec((None, D, tn), lambda l, j: (l, 0, j)), pl.BlockSpec((16, D), lambda l, j: (0, 0))],
        out_shape=[_sds((2, D, ns), F32), _sds((16, D), F32)], compiler_params=_cparams(),
    )(c16, dm16, w_ada)


def _cctx_grad(ds4, c_ctx):
    def kern(d_ref, c_ref, o_ref):
        tot = d_ref[0, 0:1, :] + d_ref[1, 0:1, :] + d_ref[2, 0:1, :] + d_ref[3, 0:1, :]
        cv = c_ref[...]
        sg = _sigmoid(cv)
        o_ref[...] = tot * (sg * (1.0 + cv * (1.0 - sg)))

    return _pcall(kern, name="cctx_grad", out_shape=_sds((1, D), F32))(ds4, c_ctx.reshape(1, D))


def kernel(x, c, ctx, c_ctx, w_ada, b_ada, norm_g, rec_w_in, rec_conv_w, rec_conv_b, rec_lambda, rec_w_a, rec_b_a, rec_w_x, rec_b_x, rec_w_out, conf_w_pw1, conf_b_pw1, conf_conv_w, conf_conv_b, conf_ln_g, conf_ln_b, conf_w_pw2, conf_b_pw2, mlp_w_in, mlp_w_out, final_g, loss_target, m_c_ctx, m_w_ada, m_b_ada, m_norm_g, m_rec_w_in, m_rec_conv_w, m_rec_conv_b, m_rec_lambda, m_rec_w_a, m_rec_b_a, m_rec_w_x, m_rec_b_x, m_rec_w_out, m_conf_w_pw1, m_conf_b_pw1, m_conf_conv_w, m_conf_conv_b, m_conf_ln_g, m_conf_ln_b, m_conf_w_pw2, m_conf_b_pw2, m_mlp_w_in, m_mlp_w_out, m_final_g, v_c_ctx, v_w_ada, v_b_ada, v_norm_g, v_rec_w_in, v_rec_conv_w, v_rec_conv_b, v_rec_lambda, v_rec_w_a, v_rec_b_a, v_rec_w_x, v_rec_b_x, v_rec_w_out, v_conf_w_pw1, v_conf_b_pw1, v_conf_conv_w, v_conf_conv_b, v_conf_ln_g, v_conf_ln_b, v_conf_w_pw2, v_conf_b_pw2, v_mlp_w_in, v_mlp_w_out, v_final_g):
    names = ["c_ctx", "w_ada", "b_ada", "norm_g", "rec_w_in", "rec_conv_w", "rec_conv_b", "rec_lambda", "rec_w_a",
             "rec_b_a", "rec_w_x", "rec_b_x", "rec_w_out", "conf_w_pw1", "conf_b_pw1", "conf_conv_w", "conf_conv_b",
             "conf_ln_g", "conf_ln_b", "conf_w_pw2", "conf_b_pw2", "mlp_w_in", "mlp_w_out", "final_g"]
    w = dict(zip(names, [c_ctx, w_ada, b_ada, norm_g, rec_w_in, rec_conv_w, rec_conv_b, rec_lambda, rec_w_a,
                         rec_b_a, rec_w_x, rec_b_x, rec_w_out, conf_w_pw1, conf_b_pw1, conf_conv_w, conf_conv_b,
                         conf_ln_g, conf_ln_b, conf_w_pw2, conf_b_pw2, mlp_w_in, mlp_w_out, final_g]))
    m = dict(zip(names, [m_c_ctx, m_w_ada, m_b_ada, m_norm_g, m_rec_w_in, m_rec_conv_w, m_rec_conv_b, m_rec_lambda,
                         m_rec_w_a, m_rec_b_a, m_rec_w_x, m_rec_b_x, m_rec_w_out, m_conf_w_pw1, m_conf_b_pw1,
                         m_conf_conv_w, m_conf_conv_b, m_conf_ln_g, m_conf_ln_b, m_conf_w_pw2, m_conf_b_pw2,
                         m_mlp_w_in, m_mlp_w_out, m_final_g]))
    v = dict(zip(names, [v_c_ctx, v_w_ada, v_b_ada, v_norm_g, v_rec_w_in, v_rec_conv_w, v_rec_conv_b, v_rec_lambda,
                         v_rec_w_a, v_rec_b_a, v_rec_w_x, v_rec_b_x, v_rec_w_out, v_conf_w_pw1, v_conf_b_pw1,
                         v_conf_conv_w, v_conf_conv_b, v_conf_ln_g, v_conf_ln_b, v_conf_w_pw2, v_conf_b_pw2,
                         v_mlp_w_in, v_mlp_w_out, v_final_g]))
    mx, my, mc = _me()
    chip = 2 * mx + my
    me = 4 * mx + 2 * my + mc

    sharded_small = ["norm_g", "rec_conv_w", "rec_lambda", "conf_b_pw1", "conf_conv_w", "conf_conv_b", "conf_ln_g",
                     "conf_ln_b", "conf_b_pw2"]
    packed, offs = _pack([c] + [w[k] for k in sharded_small], 8)
    place = jnp.stack([chip, mc]).astype(jnp.int32)
    shards = [("rec_in", _halves(rec_w_in), 0), ("rec_out", _halves(rec_w_out), 0),
              ("pw1", _halves(conf_w_pw1), 0), ("pw2", _halves(conf_w_pw2), 0),
              ("mlp_in0", _halves(mlp_w_in), 0), ("mlp_in1", _halves(mlp_w_in), 1),
              ("mlp_out0", _halves(mlp_w_out), 0), ("mlp_out1", _halves(mlp_w_out), 1)]
    small_state, small_sent = _send_start("gather_small_start", [packed[None]], _to_all7, [(7,) + packed.shape], place)
    (slot_rin,) = _place_big(shards[:1], place, small_sent)
    flying, gsems, swapping = {}, {}, {}
    flying["rec_in"], gsems["rec_in"], rec_started = _gather_start("gather_start_rec", [slot_rin], ((0,),), small_sent)
    slots = [slot_rin] + _place_big(shards[1:], place, rec_started)
    placed = jnp.broadcast_to(lax.dynamic_slice(slots[-1], (chip, 0, 0, 0), (1, 1, 1, 1)).reshape(1, 1), (8, 1))
    (own,), (landed,) = _send_wait("gather_small_wait", *small_state, _to_all7, placed)
    by_flip = jnp.concatenate([own, landed], axis=0)
    got_flat = jnp.take(by_flip, jnp.arange(8) ^ me, axis=0).reshape(8, -1)

    def piece(i):
        p, n, shape = offs[i]
        return got_flat[:, p:p + n].reshape((8,) + tuple(shape))

    c_rows = piece(0).reshape(8, D)
    full = {}
    for i, k in enumerate(sharded_small):
        per_chip = jnp.moveaxis(piece(1 + i)[0::2], 0, -2)
        full[k] = per_chip.reshape(per_chip.shape[:-2] + (4 * per_chip.shape[-1],))
    c16 = jnp.concatenate([c_rows, c_ctx.reshape(1, D), jnp.zeros((7, D), F32)], axis=0)

    ns = w_ada.shape[2]
    b_shard = lax.dynamic_slice_in_dim(b_ada, chip * ns, ns, axis=1).reshape(2, 1, ns)
    prod = _ada_fwd(c16, w_ada, b_shard)

    own_rows = lax.dynamic_index_in_dim(prod[:, :8].reshape(2, 4, 2, ns), mc, axis=2, keepdims=False)
    rows = jnp.concatenate([own_rows.transpose(1, 0, 2), jnp.broadcast_to(prod[0, 8], (4, 1, ns)),
                            jnp.zeros((4, 5, ns), F32)], axis=1)
    mod_state, mod_started = _send_start("mod_start", [rows], _to_chips, [(3, 8, ns)], place)
    use_order = dict(rec=(0, 1), mlp0=(4, 6), conf=(2, 3), mlp1=(5, 7))
    fetch_order = dict(rec_out=(1,), mlp0=(4, 6), conf=(2, 3), mlp1=(5, 7))
    order = [t for g in fetch_order for t in fetch_order[g]]
    groups = [tuple(order.index(t) for t in fetch_order[g]) for g in fetch_order]
    fly, sems, all_started = _gather_start("gather_start_rest", [slots[t] for t in order], tuple(groups), mod_started)
    for gi, g in enumerate(fetch_order):
        flying[g], gsems[g] = [fly[k] for k in groups[gi]], sems[2 * gi:2 * gi + 2]

    def wg_pre(group, after):
        bufs = _gather_wait(f"gather_wait_{group}", flying[group], *gsems[group], after)
        swapping[group], token = _swap_start(f"swap_start_{group}", bufs, after)
        return token

    def wg(group, after):
        bufs = _swap_wait(f"swap_wait_{group}", *swapping[group], after)
        if group == "rec_in":
            return dict(rec_w_in=bufs[0].reshape(4, D, RH))
        if group == "rec_out":
            return dict(rec_w_out=bufs[0].reshape(R, D))
        if group == "conf":
            return dict(conf_w_pw1=bufs[0].reshape(4, D, D // 2), conf_w_pw2=bufs[1].reshape(D, D))
        return dict(w_in=bufs[0].reshape(4, D, D), w_out=bufs[1].reshape(FF, D))

    (rows,), (landed,) = _send_wait("mod_wait", *mod_state, _to_chips, all_started)
    own = lax.dynamic_index_in_dim(rows, chip, axis=0, keepdims=True)
    by_flip = jnp.concatenate([own, landed[1:2], landed[0:1], landed[2:3]], axis=0)
    by_chip = jnp.take(by_flip, jnp.arange(4) ^ chip, axis=0)
    mods = by_chip[:, :2].transpose(1, 0, 2).reshape(2, 6, D)
    cmods = by_chip[:, 2].reshape(6, D)[:2]

    rec = dict(conv_w=full["rec_conv_w"][0], conv_b=rec_conv_b[0], lam=full["rec_lambda"][0],
               w_a=rec_w_a[0], b_a=rec_b_a[0], w_x=rec_w_x[0], b_x=rec_b_x[0])
    conf = dict(b_pw1=full["conf_b_pw1"][0], conv_w=full["conf_conv_w"][0], conv_b=full["conf_conv_b"][0],
                ln_g=full["conf_ln_g"][0], ln_b=full["conf_ln_b"][0], b_pw2=full["conf_b_pw2"][0])
    pairing, sent, sharing = {}, {}, {}

    def on_grads(group, dws):
        parts = [dw.reshape((4,) + shards[t][1].shape[1:]) for dw, t in zip(dws, use_order[group])]
        pairing[group], token = _reduce_begin(group, parts, place)
        return token

    def finish_pair(after):
        (group, state), = pairing.items()
        pairing.clear()
        sent[group], token = _reduce_mid(group, state, place, after)
        if group == "rec":
            mlp = _reduce_end("mlp1", sent["mlp1"], place, token, layer=(1, 2))
            cf = _reduce_end("conf", sent["conf"], place, token)
            mlp = _reduce_end("mlp0", sent["mlp0"], place, token, layer=(0, 2), into=mlp)
            sharing["state"], token = _share_start("share_start", cf + mlp, place)
        sent["token"] = token
        return token

    loss_local, grad_x, _, small = _local_step(x[0], ctx[0], loss_target[0], mods, cmods, full["norm_g"], final_g,
                                               rec, conf, wg, on_grads, wg_pre, finish_pair)
    rec_sent = sent["token"]
    small["loss"] = loss_local.reshape(1)

    small_names = ["loss", "d_mod", "d_cmod", "norm_g", "rec_conv_w", "rec_conv_b", "rec_lambda", "rec_w_a", "rec_b_a",
                   "rec_w_x", "rec_b_x", "conf_b_pw1", "conf_conv_w", "conf_conv_b", "conf_ln_g", "conf_ln_b",
                   "conf_b_pw2", "final_g"]
    mine = lax.broadcasted_iota(jnp.int32, (8, 1), 0) == me
    mod_slots = jnp.where(mine, small["d_mod"].reshape(1, -1), 0.0)
    spacked, soffs = _pack([small[k] for k in small_names] + [mod_slots])
    def sum_rec(after):
        sharing["rec"], token = _share_start("share_rec_start", _reduce_end("rec", sent["rec"], place, after), place)
        return token

    small_state, small_started = _allreduce_small_begin(spacked, place, rec_sent, sum_rec)

    shared = _share_wait("share_wait", *sharing["state"], small_started)
    delta, new_m, new_v = {}, {}, {}

    def adamw_of(k, g, dep=None):
        cols = w[k].shape[-1]
        d_, m_, v_ = _adamw(f"adamw_{k}", w[k].reshape(-1, cols), g.reshape(-1, cols),
                            m[k].reshape(-1, cols), v[k].reshape(-1, cols), dep)
        delta[k], new_m[k], new_v[k] = (a.reshape(w[k].shape) for a in (d_, m_, v_))

    g_big = {}
    for k, g in zip(_BIG[2:], shared):
        g_big[k] = g.reshape(w[k].shape)
        adamw_of(k, g_big[k])
    updated = jnp.broadcast_to((new_v["mlp_w_in"][:1, :1, :1] + new_v["mlp_w_out"][:1, :1, :1]).reshape(1, 1), (8, 1))
    for k, g in zip(_BIG[:2], _share_wait("share_rec_wait", *sharing["rec"], updated)):
        g_big[k] = g.reshape(w[k].shape)
        adamw_of(k, g_big[k])
    unpacked = _unpack(_allreduce_small_end(small_state, updated), soffs)
    ssum = dict(zip(small_names, unpacked[:-1]))
    loss = ssum["loss"].reshape(())
    dmod_rows = unpacked[-1].reshape(8, 2, 6 * D).transpose(1, 0, 2)

    d_cmod_full =jnp.concatenate([ssum["d_cmod"].reshape(1, 2 * D), jnp.zeros((1, 4 * D), F32)], axis=1)
    dm16 = jnp.concatenate([dmod_rows, jnp.stack([d_cmod_full, jnp.zeros((1, 6 * D), F32)]),
                            jnp.zeros((2, 7, 6 * D), F32)], axis=1)
    dm16_shard = lax.dynamic_slice_in_dim(dm16, chip * ns, ns, axis=2)
    g_w_ada, ds_part = _ada_bwd(c16, dm16_shard, w_ada)
    ds_state, ds_sent = _send_start("dsilu_start", [jnp.broadcast_to(ds_part[8:16], (4, 8, D))], _to_chips,
                                    [(3, 8, D)], place)
    adamw_of("w_ada", g_w_ada, ds_sent)
    (ds_own,), (ds_landed,) = _send_wait("dsilu_wait", *ds_state, _to_chips, new_v["w_ada"])
    ds_flip = jnp.concatenate([ds_own[:1], ds_landed[1:2], ds_landed[0:1], ds_landed[2:3]], axis=0)
    g_c_ctx = _cctx_grad(jnp.take(ds_flip, jnp.arange(4) ^ chip, axis=0), c_ctx).reshape(D)
    g_b_ada = ssum["d_mod"] + jnp.stack([d_cmod_full[0], jnp.zeros((6 * D,), F32)])

    def shard_of(a, axis):
        n = a.shape[axis] // 4
        return lax.dynamic_slice_in_dim(a, chip * n, n, axis=axis)

    grads = dict(
        c_ctx=g_c_ctx, w_ada=g_w_ada, b_ada=g_b_ada,
        norm_g=shard_of(ssum["norm_g"].reshape(2, 2, D), 2),
        rec_w_in=g_big["rec_w_in"], rec_conv_w=shard_of(ssum["rec_conv_w"].reshape(1, REC_KW, R), 2),
        rec_conv_b=ssum["rec_conv_b"].reshape(1, R), rec_lambda=shard_of(ssum["rec_lambda"].reshape(1, 2, R), 2),
        rec_w_a=ssum["rec_w_a"].reshape(rec_w_a.shape), rec_b_a=ssum["rec_b_a"].reshape(rec_b_a.shape),
        rec_w_x=ssum["rec_w_x"].reshape(rec_w_x.shape), rec_b_x=ssum["rec_b_x"].reshape(rec_b_x.shape),
        rec_w_out=g_big["rec_w_out"], conf_w_pw1=g_big["conf_w_pw1"],
        conf_b_pw1=shard_of(ssum["conf_b_pw1"].reshape(1, 2 * D), 1),
        conf_conv_w=shard_of(ssum["conf_conv_w"].reshape(1, CONF_KW, D), 2),
        conf_conv_b=shard_of(ssum["conf_conv_b"].reshape(1, D), 1),
        conf_ln_g=shard_of(ssum["conf_ln_g"].reshape(1, D), 1), conf_ln_b=shard_of(ssum["conf_ln_b"].reshape(1, D), 1),
        conf_w_pw2=g_big["conf_w_pw2"], conf_b_pw2=shard_of(ssum["conf_b_pw2"].reshape(1, D), 1),
        mlp_w_in=g_big["mlp_w_in"], mlp_w_out=g_big["mlp_w_out"], final_g=ssum["final_g"].reshape(D))

    rest =[k for k in names if k not in ("w_ada",) + _BIG]
    d_, m_, v_ = _adamw_many("adamw_small", [w[k] for k in rest], [grads[k] for k in rest],
                             [m[k] for k in rest], [v[k] for k in rest])
    for k, dd, mm, vv in zip(rest, d_, m_, v_):
        delta[k], new_m[k], new_v[k] = dd, mm, vv

    return (loss, grad_x[None], *[grads[k] for k in names], *[delta[k] for k in names],
            *[new_m[k] for k in names], *[new_v[k] for k in names])
```

```python
import functools
import math

import jax
import jax.numpy as jnp
from jax import lax
from jax.experimental import pallas as pl
from jax.experimental.pallas import tpu as pltpu

F32 = jnp.float32
BF16 = jnp.bfloat16

D = 1024
T = 2048
TC = 256
TA = T + TC
R = 1280
RH = R // 2
NQ = 4 * RH
FF = 4096
N_BLK = 16
BLK = R // N_BLK
GRID_W = 64
EPS = 1e-6
RG_C = 8.0
CONF_KW = 31
REC_KW = 4
LANE = 128
ROW_TILE = 256
HALO = 16
RG_TILE = 128
PACK_ROWS = 512
MM_TILE = 1024
REC_TILE = TA // 2
CW_REC = 640
CW_CONF = 512
V7X_VMEM_BYTES = 64 * 1024 * 1024
VMEM_LIMIT = V7X_VMEM_BYTES - 8 * 1024 * 1024

ADAM_LR = 0.001
ADAM_B1 = 0.9
ADAM_B2 = 0.999
ADAM_EPS = 1e-08
ADAM_WD = 0.01
ADAM_STEP = 10

MESH = pl.DeviceIdType.MESH
ANY = pl.BlockSpec(memory_space=pl.ANY)


def _sds(shape, dtype):
    return jax.ShapeDtypeStruct(tuple(shape), dtype)


def _pcall(body, **kw):
    return pl.pallas_call(body, **kw)


def _cparams():
    return pltpu.CompilerParams(vmem_limit_bytes=VMEM_LIMIT)


def _full_spec(arr):
    nd = arr.ndim
    return pl.BlockSpec(arr.shape, lambda *ids, _n=nd: (0,) * _n)


def _sum0(v):
    return jnp.sum(v, axis=0, keepdims=True)


def _tiled(name, fn, grid, ins, vecs, outs, vec_outs=(), vec_refs=False):
    n_in, n_vec, n_out = len(ins), len(vecs), len(outs)
    n_grid = len(grid)

    def kern(*refs):
        ids = [pl.program_id(a) for a in range(n_grid)]
        tin = [r[...] for r in refs[:n_in]]
        vin = list(refs[n_in:n_in + n_vec]) if vec_refs else [r[...] for r in refs[n_in:n_in + n_vec]]
        o_refs = refs[n_in + n_vec:n_in + n_vec + n_out]
        a_refs = refs[n_in + n_vec + n_out:]
        tout, incs = fn(ids, tin, vin)
        for r, v in zip(o_refs, tout):
            r[...] = v.astype(r.dtype)
        if a_refs:
            first = functools.reduce(jnp.logical_and, [i == 0 for i in ids])

            @pl.when(first)
            def _():
                for r in a_refs:
                    r[...] = jnp.zeros_like(r)

            for r, v in zip(a_refs, incs):
                r[...] += v

    out_shape = [o for o, _ in outs] + [_sds(s, F32) for s in vec_outs]
    out_specs = [s for _, s in outs] + [
        pl.BlockSpec(tuple(s), lambda *ids, _n=len(s): (0,) * _n) for s in vec_outs]
    res = _pcall(
        kern, name=name, grid=tuple(grid),
        in_specs=[s for _, s in ins] + [_full_spec(v) for v in vecs],
        out_specs=out_specs, out_shape=out_shape, compiler_params=_cparams(),
    )(*[a for a, _ in ins], *vecs)
    return list(res)


def _rows(arr, ncols=None, tm=ROW_TILE, off=0, col=0, clamp_lo=False):
    ncols = arr.shape[1] if ncols is None else ncols
    if clamp_lo:
        return arr, pl.BlockSpec((tm, ncols), lambda i: (jnp.maximum(i + off, 0), col))
    return arr, pl.BlockSpec((tm, ncols), lambda i: (i + off, col))


def _orow(nrows, ncols, dtype, tm=ROW_TILE, off=0, clamp_lo=False):
    if clamp_lo:
        return _sds((nrows, ncols), dtype), pl.BlockSpec((tm, ncols), lambda i: (jnp.maximum(i + off, 0), 0))
    return _sds((nrows, ncols), dtype), pl.BlockSpec((tm, ncols), lambda i: (i + off, 0))


_NN = (((1,), (0,)), ((), ()))
_TN = (((0,), (0,)), ((), ()))
_NT = (((1,), (1,)), ((), ()))


def _mm(name, a, b, dims, grid, a_spec, b_spec, out, acc_shape, extra=(), a_pre=None, epi=None):
    n_k = grid[2]
    n_ex = len(extra)

    def kern(a_ref, b_ref, *rest):
        ex = rest[:n_ex]
        o_refs = rest[n_ex:n_ex + len(out)]
        k = pl.program_id(2)
        av = a_ref[...]
        if a_pre is not None:
            av = a_pre(av)
        part = lax.dot_general(av.astype(BF16), b_ref[...].astype(BF16), dims, preferred_element_type=F32)

        def finish(total):
            vals = [total] if epi is None else epi(total, [e[...] for e in ex])
            for r, v in zip(o_refs, vals):
                r[...] = v.astype(r.dtype)

        if n_k == 1:
            finish(part)
        else:
            acc = rest[-1]

            @pl.when(k == 0)
            def _():
                acc[...] = part

            @pl.when(jnp.logical_and(k > 0, k < n_k - 1))
            def _():
                acc[...] += part

            @pl.when(k == n_k - 1)
            def _():
                finish(acc[...] + part)

    res = _pcall(
        kern, name=name, grid=tuple(grid),
        in_specs=[a_spec, b_spec] + [s for _, s in extra],
        out_specs=[s for _, s in out], out_shape=[o for o, _ in out],
        scratch_shapes=[] if n_k == 1 else [pltpu.VMEM(tuple(acc_shape), F32)], compiler_params=_cparams(),
    )(a, b, *[e for e, _ in extra])
    return list(res)


def _rms(x):
    r = lax.rsqrt(jnp.mean(x * x, axis=-1, keepdims=True) + EPS)
    return x * r, r


def _norm_mod(x, g, sc, sh):
    n, _ = _rms(x)
    return (n * g) * (1.0 + sc) + sh


def _norm_mod_bwd(dh, x, g, sc):
    n, r = _rms(x)
    d_sh = _sum0(dh)
    d_sc = _sum0(dh * (n * g))
    d_g = _sum0(dh * (1.0 + sc) * n)
    dn = dh * (g * (1.0 + sc))
    dx = r * (dn - n * jnp.mean(dn * n, axis=-1, keepdims=True))
    return dx, d_sh, d_sc, d_g


_GELU_K = math.sqrt(2.0 / math.pi)


def _gelu(x):
    t = jnp.tanh(_GELU_K * (x + 0.044715 * x * x * x))
    return 0.5 * x * (1.0 + t), t


def _gelu_grad(x, t):
    return 0.5 * (1.0 + t) + 0.5 * x * (1.0 - t * t) * (_GELU_K * (1.0 + 3.0 * 0.044715 * x * x))


def _sigmoid(x):
    return 0.5 * jnp.tanh(0.5 * x) + 0.5


def _expm1(x):
    p = jnp.full_like(x, 1.0 / 5040.0)
    for c in (1.0 / 720.0, 1.0 / 120.0, 1.0 / 24.0, 1.0 / 6.0, 0.5, 1.0):
        p = p * x + c
    return jnp.where(jnp.abs(x) < 0.3, x * p, jnp.exp(x) - 1.0)


def _softplus_neg(lam):
    return jnp.log1p(jnp.exp(-jnp.abs(lam))) + jnp.maximum(-lam, 0.0)


def _layernorm_parts(x):
    mu = jnp.mean(x, axis=-1, keepdims=True)
    xc = x - mu
    rstd = lax.rsqrt(jnp.mean(xc * xc, axis=-1, keepdims=True) + EPS)
    return xc * rstd, rstd


def _rg_gates(u, wbd, gbias, lam):
    sp = _softplus_neg(lam)
    parts = {}
    for h in range(2):
        uh = u[:, h * RH:(h + 1) * RH]
        g = jnp.dot(uh.astype(BF16), wbd[h], preferred_element_type=F32) + gbias[:, h * NQ:(h + 1) * NQ]
        for d in range(2):
            r = _sigmoid(g[:, (2 * d) * RH:(2 * d + 1) * RH])
            i = _sigmoid(g[:, (2 * d + 1) * RH:(2 * d + 2) * RH])
            sph = sp[d:d + 1, h * RH:(h + 1) * RH]
            la = (-RG_C) * r * sph
            e2 = _expm1(2.0 * la)
            inv_mult = jnp.where(e2 < 0.0, lax.rsqrt(-e2), 0.0)
            parts[(d, h)] = dict(r=r, i=i, la=la, a=jnp.exp(la), e2=e2, mult=-e2 * inv_mult, inv_mult=inv_mult,
                                 uh=uh, sp=sph)
    return parts


def _rg_fwd_fn(ids, tin, vin):
    (u,) = tin
    wbd = vin[0]
    parts = _rg_gates(u, wbd, vin[1][...], vin[2][...])
    outs = []
    for d in range(2):
        a = jnp.concatenate([parts[(d, h)]["a"] for h in range(2)], axis=1)
        b = jnp.concatenate([parts[(d, h)]["mult"] * parts[(d, h)]["i"] * parts[(d, h)]["uh"]
                             for h in range(2)], axis=1)
        outs += [a, b]
    return outs, []


def _rg_bwd_fn(ids, tin, vin):
    u, da_f, db_f, da_r, db_r = tin
    wbd, lam = vin[0], vin[2][...]
    parts = _rg_gates(u, wbd, vin[1][...], lam)
    dab = ((da_f, db_f), (da_r, db_r))
    dsig_lam = -1.0 / (1.0 + jnp.exp(lam))
    du_halves, dpre_halves, dlam = [], [], [[None, None], [None, None]]
    for h in range(2):
        du = jnp.zeros_like(parts[(0, h)]["uh"])
        dpre = []
        for d in range(2):
            p = parts[(d, h)]
            da = dab[d][0][:, h * RH:(h + 1) * RH]
            db = dab[d][1][:, h * RH:(h + 1) * RH]
            d_mult = db * p["i"] * p["uh"]
            d_i = db * p["mult"] * p["uh"]
            du = du + db * p["mult"] * p["i"]
            d_la = da * p["a"] - d_mult * (p["e2"] + 1.0) * p["inv_mult"]
            d_r = d_la * ((-RG_C) * p["sp"])
            dlam[d][h] = _sum0(d_la * ((-RG_C) * p["r"])) * dsig_lam[d:d + 1, h * RH:(h + 1) * RH]
            dpre += [d_r * p["r"] * (1.0 - p["r"]), d_i * p["i"] * (1.0 - p["i"])]
        dpre = jnp.concatenate(dpre, axis=1)
        du = du + lax.dot_general(dpre.astype(BF16), wbd[h], _NT, preferred_element_type=F32)
        du_halves.append(du)
        dpre_halves.append(dpre)
    dpre_all = jnp.concatenate(dpre_halves, axis=1)
    dlam_row = jnp.concatenate([dlam[0][0], dlam[0][1], dlam[1][0], dlam[1][1]], axis=1)
    return [dpre_all, jnp.concatenate(du_halves, axis=1)], [_sum0(dpre_all), dlam_row]


def _tile_flags(i, n_tiles, seq_starts):
    starts_here = functools.reduce(jnp.logical_or, [i == s for s in seq_starts])
    ends_here = functools.reduce(jnp.logical_or, [i + 1 == s for s in seq_starts] + [i + 1 == n_tiles])
    return jnp.logical_not(starts_here), jnp.logical_not(ends_here)


def _halo_specs(col0, cw):
    hb = ROW_TILE // HALO
    prev = pl.BlockSpec((HALO, cw), lambda i, c: (jnp.maximum(i * hb - 1, 0), col0 + c))
    cur = pl.BlockSpec((ROW_TILE, cw), lambda i, c: (i, col0 + c))
    return prev, cur, hb


def _window(prev_ref, cur_ref, next_ref, has_prev, has_next):
    prev = jnp.where(has_prev, prev_ref[...], 0.0)
    nxt = jnp.where(has_next, next_ref[...], 0.0)
    return jnp.concatenate([prev, cur_ref[...], nxt], axis=0)


def _tap_reader(win):
    sub = 8
    n = win.shape[0]
    shifted = {0: win}

    def tap(off):
        s = off % sub
        if s not in shifted:
            shifted[s] = pltpu.roll(win, n - s, axis=0)
        return shifted[s][off - s:off - s + ROW_TILE, :]

    return tap


def _dwconv(name, x, col0, w, bias, pad_left, seq_starts, n_ch, cw=256, flip=False, into=None):
    n_rows = x.shape[0]
    n_tiles = n_rows // ROW_TILE
    n_taps = w.shape[0]
    prev_spec, cur_spec, hb = _halo_specs(col0, cw)
    last_hb = n_rows // HALO - 1
    next_spec = pl.BlockSpec((HALO, cw), lambda i, c: (jnp.minimum((i + 1) * hb, last_hb), col0 + c))
    dest, out_col0 = (None, 0) if into is None else into

    def kern(prev_ref, cur_ref, next_ref, w_ref, b_ref, *rest):
        o_ref = rest[-1]
        has_prev, has_next = _tile_flags(pl.program_id(0), n_tiles, seq_starts)
        win = _window(prev_ref, cur_ref, next_ref, has_prev, has_next)
        tap = _tap_reader(win)
        wv = w_ref[...]
        acc = jnp.zeros((ROW_TILE, cw), F32) + b_ref[...]
        for k in range(n_taps):
            kw = n_taps - 1 - k if flip else k
            acc = acc + wv[kw:kw + 1, :] * tap(HALO + k - pad_left)
        o_ref[...] = acc.astype(o_ref.dtype)

    return _pcall(
        kern, name=name, grid=(n_tiles, n_ch // cw),
        in_specs=[prev_spec, cur_spec, next_spec,
                  pl.BlockSpec((n_taps, cw), lambda i, c: (0, c)), pl.BlockSpec((1, cw), lambda i, c: (0, c))]
        + ([] if dest is None else [ANY]),
        out_specs=pl.BlockSpec((ROW_TILE, cw), lambda i, c: (i, out_col0 + c)),
        out_shape=_sds((n_rows, n_ch), F32) if dest is None else _sds(dest.shape, dest.dtype),
        input_output_aliases={} if dest is None else {5: 0}, compiler_params=_cparams(),
    )(x, x, x, w, bias, *([] if dest is None else [dest]))


def _dwconv_wgrad(name, dy, x, col0, n_taps, pad_left, seq_starts, n_ch, cw=256, dep=None):
    deps = [] if dep is None else [dep]
    n_rows = dy.shape[0]
    n_tiles = n_rows // ROW_TILE
    n_out = -(-(n_taps + 1) // 8) * 8
    prev_spec, cur_spec, hb = _halo_specs(col0, cw)
    last_hb = n_rows // HALO - 1
    next_spec = pl.BlockSpec((HALO, cw), lambda c, i: (jnp.minimum((i + 1) * hb, last_hb), col0 + c))
    prev_spec = pl.BlockSpec((HALO, cw), lambda c, i: (jnp.maximum(i * hb - 1, 0), col0 + c))
    cur_spec = pl.BlockSpec((ROW_TILE, cw), lambda c, i: (i, col0 + c))

    def kern(dy_ref, prev_ref, cur_ref, next_ref, *rest):
        o_ref = rest[-1]
        i = pl.program_id(1)
        has_prev, has_next = _tile_flags(i, n_tiles, seq_starts)
        win = _window(prev_ref, cur_ref, next_ref, has_prev, has_next)
        dyv = dy_ref[...]
        tap = _tap_reader(win)
        rid = lax.broadcasted_iota(jnp.int32, (n_out, cw), 0)
        inc = jnp.where(rid == n_taps, _sum0(dyv), 0.0)
        for k in range(n_taps):
            inc = inc + jnp.where(rid == k, _sum0(dyv * tap(HALO + k - pad_left)), 0.0)

        @pl.when(i == 0)
        def _():
            o_ref[...] = jnp.zeros_like(o_ref)

        o_ref[...] += inc

    return _pcall(
        kern, name=name, grid=(n_ch // cw, n_tiles),
        in_specs=[pl.BlockSpec((ROW_TILE, cw), lambda c, i: (i, c)), prev_spec, cur_spec, next_spec]
        + [pl.BlockSpec(d.shape, lambda c, i: (0, 0)) for d in deps],
        out_specs=pl.BlockSpec((n_out, cw), lambda c, i: (0, c)),
        out_shape=_sds((n_out, n_ch), F32), compiler_params=_cparams(),
    )(dy, x, x, x, *deps)


N_SCAN = TA // ROW_TILE


def _rev_block(j):
    return jnp.where(j == 0, 0, N_SCAN - j)


def _scan_fwd(a_f, b_f, a_r, b_r):
    fwd_spec = pl.BlockSpec((ROW_TILE, R), lambda i: (i, 0))
    rev_spec = pl.BlockSpec((ROW_TILE, R), lambda i: (_rev_block(i), 0))
    hin_spec = pl.BlockSpec((None, 1, R), lambda i: (i, 0, 0))

    def kern(af, bf, ar, br, yf, yr, hin_f, hin_r, hf_s, hr_s):
        @pl.when(pl.program_id(0) == 0)
        def _():
            hf_s[...] = jnp.zeros_like(hf_s)
            hr_s[...] = jnp.zeros_like(hr_s)

        hin_f[...] = hf_s[...]
        hin_r[...] = hr_s[...]

        def step(s8, carry):
            hf, hr = carry
            t0 = pl.multiple_of(s8 * 8, 8)
            for q in range(8):
                tf = t0 + q
                hf = af[pl.ds(tf, 1), :] * hf + bf[pl.ds(tf, 1), :]
                yf[pl.ds(tf, 1), :] = hf
                tr = ROW_TILE - 1 - tf
                hr = ar[pl.ds(tr, 1), :] * hr + br[pl.ds(tr, 1), :]
                yr[pl.ds(tr, 1), :] = hr
            return hf, hr

        hf, hr = lax.fori_loop(0, ROW_TILE // 8, step, (hf_s[...], hr_s[...]))
        hf_s[...] = hf
        hr_s[...] = hr

    return _pcall(
        kern, name="scan_fwd", grid=(N_SCAN,),
        in_specs=[fwd_spec, fwd_spec, rev_spec, rev_spec],
        out_specs=[fwd_spec, rev_spec, hin_spec, hin_spec],
        out_shape=[_sds((TA, R), F32), _sds((TA, R), F32), _sds((N_SCAN, 1, R), F32), _sds((N_SCAN, 1, R), F32)],
        scratch_shapes=[pltpu.VMEM((1, R), F32), pltpu.VMEM((1, R), F32)], compiler_params=_cparams(),
    )(a_f, b_f, a_r, b_r)


def _scan_bwd(dy, a_f, y_f, hin_f, a_r, y_r, hin_r):
    fwd_spec = pl.BlockSpec((ROW_TILE, R), lambda i: (N_SCAN - 1 - i, 0))
    rev_spec = pl.BlockSpec((ROW_TILE, R), lambda i: (_rev_block(N_SCAN - 1 - i), 0))
    hin_spec = pl.BlockSpec((None, 1, R), lambda i: (N_SCAN - 1 - i, 0, 0))
    last = ROW_TILE - 1

    def kern(dyf, af, yf, hf0, dyr, ar, yr, hr0, daf, dbf, dar, dbr, gf_s, anf_s, gr_s, anr_s):
        @pl.when(pl.program_id(0) == 0)
        def _():
            for r in (gf_s, anf_s, gr_s, anr_s):
                r[...] = jnp.zeros_like(r)

        def one(dy_ref, a_ref, y_ref, da_ref, db_ref, g, an, p, pprev):
            gnew = dy_ref[pl.ds(p, 1), :] + an * g
            db_ref[pl.ds(p, 1), :] = gnew
            da_ref[pl.ds(p, 1), :] = gnew * y_ref[pl.ds(pprev, 1), :]
            return gnew, a_ref[pl.ds(p, 1), :]

        def step(s8, carry):
            gf, anf, gr, anr = carry
            base = s8 * 8
            for q in range(8):
                s = last - (base + q)
                gf, anf = one(dyf, af, yf, daf, dbf, gf, anf, s, s - 1)
                gr, anr = one(dyr, ar, yr, dar, dbr, gr, anr, last - s, last - s + 1)
            return gf, anf, gr, anr

        carry = (gf_s[...], anf_s[...], gr_s[...], anr_s[...])
        carry = lax.fori_loop(0, ROW_TILE // 8 - 1, step, carry)
        gf, anf, gr, anr = carry
        for s in range(7, 0, -1):
            gf, anf = one(dyf, af, yf, daf, dbf, gf, anf, s, s - 1)
            gr, anr = one(dyr, ar, yr, dar, dbr, gr, anr, last - s, last - s + 1)
        gf0 = dyf[0:1, :] + anf * gf
        dbf[0:1, :] = gf0
        daf[0:1, :] = gf0 * hf0[...]
        gr0 = dyr[last:last + 1, :] + anr * gr
        dbr[last:last + 1, :] = gr0
        dar[last:last + 1, :] = gr0 * hr0[...]
        gf_s[...] = gf0
        anf_s[...] = af[0:1, :]
        gr_s[...] = gr0
        anr_s[...] = ar[last:last + 1, :]

    return _pcall(
        kern, name="scan_bwd", grid=(N_SCAN,),
        in_specs=[fwd_spec, fwd_spec, fwd_spec, hin_spec, rev_spec, rev_spec, rev_spec, hin_spec],
        out_specs=[fwd_spec, fwd_spec, rev_spec, rev_spec],
        out_shape=[_sds((TA, R), F32)] * 4,
        scratch_shapes=[pltpu.VMEM((1, R), F32)] * 4, compiler_params=_cparams(),
    )(dy, a_f, y_f, hin_f, dy, a_r, y_r, hin_r)


def _me():
    return lax.axis_index("x"), lax.axis_index("y"), lax.axis_index("c")


def _other_chips(mx, my):
    return [(1 - mx, my), (mx, 1 - my), (1 - mx, 1 - my)]


def _rcopy(src, dst, ssem, rsem, dev):
    return pltpu.make_async_remote_copy(src_ref=src, dst_ref=dst, send_sem=ssem, recv_sem=rsem,
                                        device_id=dev, device_id_type=MESH)


def _peers7(mx, my, mc):
    peers = []
    for k in range(1, 8):
        peers.append((1 - mx if (k >> 2) & 1 else mx, 1 - my if (k >> 1) & 1 else my, 1 - mc if k & 1 else mc))
    return peers


def _halves_of(refs, c):
    out = []
    for r in refs:
        out += [r.at[c]] if len(r.shape) == 3 else [r.at[l, c] for l in range(r.shape[0])]
    return out


def _share_start(name, fulls, after):
    n = len(fulls)
    n_cp = sum(1 if f.ndim == 3 else f.shape[0] for f in fulls)

    def kern(*refs):
        o = refs[n + 1:2 * n + 1]
        ssem, rsem, token = refs[2 * n + 1:]
        mx, my, mc = _me()
        for q, half in enumerate(_halves_of(o, mc)):
            _rcopy(half, half, ssem.at[q], rsem.at[q], (mx, my, 1 - mc)).start()
        token[...] = jnp.zeros_like(token)

    dma = pltpu.SemaphoreType.DMA
    res = _pcall(
        kern, name=name, in_specs=[ANY] * (n + 1),
        out_specs=[ANY] * n + [SEM, SEM, pl.BlockSpec(memory_space=pltpu.VMEM)],
        out_shape=[_sds(f.shape, f.dtype) for f in fulls] + [dma((n_cp,)), dma((n_cp,)), _sds((8, LANE), F32)],
        input_output_aliases={t: t for t in range(n)},
        compiler_params=pltpu.CompilerParams(has_side_effects=_DATAFLOW),
    )(*fulls, after)
    return (list(res[:n]), res[n], res[n + 1]), res[n + 2]


def _share_wait(name, fulls, ssem, rsem, after):
    n = len(fulls)

    def kern(*refs):
        o = refs[:n]
        ssem_ref, rsem_ref = refs[n], refs[n + 1]
        mx, my, mc = _me()
        sib = (mx, my, 1 - mc)
        for q, (theirs, mine) in enumerate(zip(_halves_of(o, 1 - mc), _halves_of(o, mc))):
            _rcopy(theirs, theirs, ssem_ref.at[q], rsem_ref.at[q], sib).wait_recv()
            _rcopy(mine, mine, ssem_ref.at[q], rsem_ref.at[q], sib).wait_send()

    return list(_pcall(
        kern, name=name, in_specs=[ANY] * n + [SEM, SEM, ANY], out_specs=[ANY] * n,
        out_shape=[_sds(f.shape, f.dtype) for f in fulls], input_output_aliases={t: t for t in range(n)},
        compiler_params=pltpu.CompilerParams(has_side_effects=_DATAFLOW),
    )(*fulls, ssem, rsem, after))


def _tiled_sp(name, fn, grid, sp, ins, outs, into=None):
    n_in = len(ins)
    dest = [] if into is None else [into]

    def kern(sp_ref, *refs):
        tout = fn([r[...] for r in refs[:n_in]])
        for r, v in zip(refs[n_in + len(dest):], tout):
            r[...] = v.astype(r.dtype)

    gs = pltpu.PrefetchScalarGridSpec(num_scalar_prefetch=1, grid=tuple(grid),
                                      in_specs=[s for _, s in ins] + [ANY] * len(dest), out_specs=[s for _, s in outs])
    res = _pcall(kern, name=name, grid_spec=gs, out_shape=[o for o, _ in outs], compiler_params=_cparams(),
                 input_output_aliases={1 + n_in: 0} if dest else {})(sp, *[a for a, _ in ins], *dest)
    return list(res)


def _row_tile(rows, cols, itemsize=4, budget=2 * 1024 * 1024):
    tr = rows
    while tr * cols * itemsize > budget and tr % 32 == 0:
        tr //= 2
    return tr


def _place_big(shards, place, dep=None):
    slots = []
    for tag, s, layer in shards:
        rr, cc = s.shape[2], s.shape[3]
        tr = _row_tile(rr, cc)
        (slot,) = _tiled_sp(
            f"place_{tag}", lambda tin: [tin[0]], (2, rr // tr), place,
            [(s, pl.BlockSpec((None, None, tr, cc), lambda h, i, sp, layer=layer: (layer, h, i, 0)))]
            + [(d, pl.BlockSpec(d.shape, lambda h, i, sp: (0, 0))) for d in _behind(dep)],
            [(_sds((4, 2, rr, cc), BF16), pl.BlockSpec((None, None, tr, cc), lambda h, i, sp: (sp[0], h, i, 0)))])
        slots.append(slot)
    return slots


def _allreduce_small_begin(vec, place, after, during):
    hr = vec.shape[0] // 2
    tr = _row_tile(hr, LANE)
    blk = (None, None, tr, LANE)
    (pair,) = _tiled_sp(
        "small_place", lambda tin: [tin[0]], (2, hr // tr), place,
        [(vec.reshape(2, hr, LANE), pl.BlockSpec((None, tr, LANE), lambda h, i, sp: (h, i, 0)))],
        [(_sds((2, 2, hr, LANE), F32), pl.BlockSpec(blk, lambda h, i, sp: (sp[1], h, i, 0)))])
    crossing, token = _share_start("small_share_start", [pair.reshape(2, 2 * hr, LANE)], place)
    (pair,) = _share_wait("small_share_wait", *crossing, during(token))
    pair = pair.reshape(2, 2, hr, LANE)
    (slot,) = _tiled_sp(
        "small_pair_add", lambda tin: [tin[0] + tin[1]], (2, hr // tr), place,
        [(pair, pl.BlockSpec(blk, lambda h, i, sp: (0, h, i, 0))),
         (pair, pl.BlockSpec(blk, lambda h, i, sp: (1, h, i, 0)))],
        [(_sds((4, 2, hr, LANE), F32), pl.BlockSpec(blk, lambda h, i, sp: (sp[0], h, i, 0)))])
    fly, sems, token = _gather_start("small_start", [slot], ((0,),), after)
    return (fly, sems), token


def _allreduce_small_end(state, after):
    fly, sems = state
    (chips,) = _swap_halves("small_swap", _gather_wait("small_wait", fly, *sems, after))
    hr = chips.shape[2]
    tr = _row_tile(hr, LANE)
    blk = (None, None, tr, LANE)
    (total,) = _tiled(
        "small_chip_sum", lambda ids, tin, vin: ([((tin[0] + tin[1]) + tin[2]) + tin[3]], []), (2, hr // tr),
        [(chips, pl.BlockSpec(blk, lambda h, i, _j=j: (_j, h, i, 0))) for j in range(4)], [],
        [(_sds((2, hr, LANE), F32), pl.BlockSpec((None, tr, LANE), lambda h, i: (h, i, 0)))])
    return total.reshape(2 * hr, LANE)


SEM =pl.BlockSpec(memory_space=pltpu.SEMAPHORE)
_DATAFLOW = pltpu.SideEffectType.DATAFLOW_SIDE_EFFECTING


def _gather_start(name, slots, groups, after):
    n = len(slots)

    def kern(*refs):
        o = refs[n + 1:2 * n + 1]
        sems, token = refs[2 * n + 1:-1], refs[-1]
        mx, my, mc = _me()
        j0 = 2 * mx + my
        for gi, grp in enumerate(groups):
            for k, t in enumerate(grp):
                for q, (qx, qy) in enumerate(_other_chips(mx, my)):
                    _rcopy(o[t].at[j0, mc], o[t].at[j0, mc], sems[2 * gi].at[3 * k + q],
                           sems[2 * gi + 1].at[3 * k + q], (qx, qy, mc)).start()
        token[...] = jnp.zeros_like(token)

    sem_shapes = []
    for grp in groups:
        sem_shapes += [pltpu.SemaphoreType.DMA((3 * len(grp),))] * 2
    res = _pcall(
        kern, name=name, in_specs=[ANY] * (n + 1),
        out_specs=[ANY] * n + [SEM] * len(sem_shapes) + [pl.BlockSpec(memory_space=pltpu.VMEM)],
        out_shape=[_sds(w.shape, w.dtype) for w in slots] + sem_shapes + [_sds((8, LANE), F32)],
        input_output_aliases={t: t for t in range(n)},
        compiler_params=pltpu.CompilerParams(has_side_effects=_DATAFLOW),
    )(*slots, after)
    return list(res[:n]), list(res[n:-1]), res[-1]


def _gather_wait(name, bufs, ssem, rsem, after):
    n = len(bufs)

    def kern(*refs):
        b = refs[:n]
        ssem_ref, rsem_ref = refs[n], refs[n + 1]
        mx, my, mc = _me()
        j0 = 2 * mx + my
        for k in range(n):
            for q, (qx, qy) in enumerate(_other_chips(mx, my)):
                jq = 2 * qx + qy
                _rcopy(b[k].at[jq, mc], b[k].at[jq, mc], ssem_ref.at[3 * k + q], rsem_ref.at[3 * k + q],
                       (qx, qy, mc)).wait_recv()
                _rcopy(b[k].at[j0, mc], b[k].at[j0, mc], ssem_ref.at[3 * k + q], rsem_ref.at[3 * k + q],
                       (qx, qy, mc)).wait_send()

    return list(_pcall(
        kern, name=name, in_specs=[ANY] * n + [SEM, SEM, ANY], out_specs=[ANY] * n,
        out_shape=[_sds(w.shape, w.dtype) for w in bufs], input_output_aliases={k: k for k in range(n)},
        compiler_params=pltpu.CompilerParams(has_side_effects=_DATAFLOW),
    )(*bufs, ssem, rsem, after))


def _swap_halves(name, bufs):
    n = len(bufs)

    def kern(*refs):
        o = refs[n:2 * n]
        ss, rs = refs[2 * n:]
        mx, my, mc = _me()
        sib = (mx, my, 1 - mc)
        sends = []
        for k in range(n):
            for q, (qx, qy) in enumerate(_other_chips(mx, my)):
                jq = 2 * qx + qy
                cp = _rcopy(o[k].at[jq, mc], o[k].at[jq, mc], ss.at[3 * k + q], rs.at[3 * k + q], sib)
                cp.start()
                sends.append(cp)
        for k in range(n):
            for q, (qx, qy) in enumerate(_other_chips(mx, my)):
                jq = 2 * qx + qy
                _rcopy(o[k].at[jq, 1 - mc], o[k].at[jq, 1 - mc], ss.at[3 * k + q], rs.at[3 * k + q], sib).wait_recv()
        for cp in sends:
            cp.wait_send()

    dma = pltpu.SemaphoreType.DMA
    return list(_pcall(
        kern, name=name, in_specs=[ANY] * n, out_specs=[ANY] * n,
        out_shape=[_sds(w.shape, w.dtype) for w in bufs], input_output_aliases={k: k for k in range(n)},
        scratch_shapes=[dma((3 * n,)), dma((3 * n,))],
    )(*bufs))


def _swap_start(name, bufs, after):
    n = len(bufs)

    def kern(*refs):
        o = refs[n + 1:2 * n + 1]
        ssem, rsem, token = refs[2 * n + 1:]
        mx, my, mc = _me()
        for k in range(n):
            for q, (qx, qy) in enumerate(_other_chips(mx, my)):
                jq = 2 * qx + qy
                _rcopy(o[k].at[jq, mc], o[k].at[jq, mc], ssem.at[3 * k + q], rsem.at[3 * k + q], (mx, my, 1 - mc)).start()
        token[...] = jnp.zeros_like(token)

    dma = pltpu.SemaphoreType.DMA
    res = _pcall(
        kern, name=name, in_specs=[ANY] * (n + 1),
        out_specs=[ANY] * n + [SEM, SEM, pl.BlockSpec(memory_space=pltpu.VMEM)],
        out_shape=[_sds(w.shape, w.dtype) for w in bufs] + [dma((3 * n,)), dma((3 * n,)), _sds((8, LANE), F32)],
        input_output_aliases={k: k for k in range(n)},
        compiler_params=pltpu.CompilerParams(has_side_effects=_DATAFLOW),
    )(*bufs, after)
    return (list(res[:n]), res[n], res[n + 1]), res[n + 2]


def _swap_wait(name, bufs, ssem, rsem, after):
    n = len(bufs)

    def kern(*refs):
        b = refs[:n]
        ssem_ref, rsem_ref = refs[n], refs[n + 1]
        mx, my, mc = _me()
        sib = (mx, my, 1 - mc)
        for k in range(n):
            for q, (qx, qy) in enumerate(_other_chips(mx, my)):
                jq = 2 * qx + qy
                _rcopy(b[k].at[jq, 1 - mc], b[k].at[jq, 1 - mc], ssem_ref.at[3 * k + q], rsem_ref.at[3 * k + q],
                       sib).wait_recv()
                _rcopy(b[k].at[jq, mc], b[k].at[jq, mc], ssem_ref.at[3 * k + q], rsem_ref.at[3 * k + q],
                       sib).wait_send()

    return list(_pcall(
        kern, name=name, in_specs=[ANY] * n + [SEM, SEM, ANY], out_specs=[ANY] * n,
        out_shape=[_sds(w.shape, w.dtype) for w in bufs], input_output_aliases={k: k for k in range(n)},
        compiler_params=pltpu.CompilerParams(has_side_effects=_DATAFLOW),
    )(*bufs, ssem, rsem, after))


def _to_sibling(mx, my, mc):
    return [((j, 1 - mc), j, (mx, my, 1 - mc)) for j in range(4)]


def _to_chips(mx, my, mc):
    return [((2 * qx + qy,), q, (qx, qy, mc)) for q, (qx, qy) in enumerate(_other_chips(mx, my))]


def _to_all7(mx, my, mc):
    return [((0,), k, dev) for k, dev in enumerate(_peers7(mx, my, mc))]


def _send_start(name, srcs, plan, land_shapes, after):
    n = len(srcs)
    per = len(plan(0, 0, 0))

    def kern(*refs):
        s, land = refs[n + 1:2 * n + 1], refs[2 * n + 1:3 * n + 1]
        ssem, rsem, token = refs[3 * n + 1:]
        for k in range(n):
            for q, (idx, slot, dev) in enumerate(plan(*_me())):
                _rcopy(s[k].at[idx], land[k].at[slot], ssem.at[per * k + q], rsem.at[per * k + q], dev).start()
        token[...] = jnp.zeros_like(token)

    dma = pltpu.SemaphoreType.DMA
    res = _pcall(
        kern, name=name, in_specs=[ANY] * (n + 1),
        out_specs=[ANY] * (2 * n) + [SEM, SEM, pl.BlockSpec(memory_space=pltpu.VMEM)],
        out_shape=[_sds(s.shape, s.dtype) for s in srcs] + [_sds(ls, s.dtype) for ls, s in zip(land_shapes, srcs)]
        + [dma((per * n,)), dma((per * n,)), _sds((8, LANE), F32)],
        input_output_aliases={k: k for k in range(n)},
        compiler_params=pltpu.CompilerParams(has_side_effects=_DATAFLOW),
    )(*srcs, after)
    return (list(res[:n]), list(res[n:2 * n]), res[2 * n], res[2 * n + 1]), res[2 * n + 2]


def _send_wait(name, srcs, lands, ssem, rsem, plan, after):
    n = len(srcs)
    per = len(plan(0, 0, 0))

    def kern(*refs):
        s, land = refs[:n], refs[n:2 * n]
        ssem_ref, rsem_ref = refs[2 * n], refs[2 * n + 1]
        for k in range(n):
            for q, (idx, slot, dev) in enumerate(plan(*_me())):
                cp = _rcopy(s[k].at[idx], land[k].at[slot], ssem_ref.at[per * k + q], rsem_ref.at[per * k + q], dev)
                cp.wait_recv()
                cp.wait_send()

    res = _pcall(
        kern, name=name, in_specs=[ANY] * (2 * n) + [SEM, SEM, ANY], out_specs=[ANY] * (2 * n),
        out_shape=[_sds(a.shape, a.dtype) for a in list(srcs) + list(lands)],
        input_output_aliases={k: k for k in range(2 * n)},
        compiler_params=pltpu.CompilerParams(has_side_effects=_DATAFLOW),
    )(*srcs, *lands, ssem, rsem, after)
    return list(res[:n]), list(res[n:])


def _reduce_begin(tag, parts, after):
    return _send_start(f"pair_start_{tag}", parts, _to_sibling, [(4,) + p.shape[2:] for p in parts], after)


def _reduce_mid(tag, pairing, place, after):
    parts, theirs = _send_wait(f"pair_wait_{tag}", *pairing, _to_sibling, after)
    sums = []
    for k, (p, o) in enumerate(zip(parts, theirs)):
        rr, cc = p.shape[2], p.shape[3]
        tr = _row_tile(rr, cc)
        (s_k,) = _tiled_sp(
            f"pair_add_{tag}{k}", lambda tin: [tin[0].astype(F32) + tin[1].astype(F32)], (4, rr // tr), place,
            [(p, pl.BlockSpec((None, None, tr, cc), lambda j, i, sp: (j, sp[1], i, 0))),
             (o, pl.BlockSpec((None, tr, cc), lambda j, i, sp: (j, i, 0)))],
            [(_sds((4, rr, cc), BF16), pl.BlockSpec((None, tr, cc), lambda j, i, sp: (j, i, 0)))])
        sums.append(s_k)
    return _send_start(f"chips_start_{tag}", sums, _to_chips, [(3,) + s.shape[1:] for s in sums], theirs[0])


def _reduce_end(tag, flying, place, after, layer=None, into=None):
    sums, lands = _send_wait(f"chips_wait_{tag}", *flying, _to_chips, after)
    fulls = []
    for k, (s, q) in enumerate(zip(sums, lands)):
        rr, cc = q.shape[1], q.shape[2]
        tr = _row_tile(rr, cc)

        def add4(tin):
            return [((tin[0].astype(F32) + tin[1].astype(F32)) + tin[2].astype(F32)) + tin[3].astype(F32)]

        ins = [(s, pl.BlockSpec((None, tr, cc), lambda i, sp: (sp[0], i, 0)))]
        ins += [(q, pl.BlockSpec((None, tr, cc), lambda i, sp, _k=kk: (_k, i, 0))) for kk in range(3)]
        if layer is None:
            out = (_sds((2, rr, cc), F32), pl.BlockSpec((None, tr, cc), lambda i, sp: (sp[1], i, 0)))
        else:
            out = (_sds((layer[1], 2, rr, cc), F32),
                   pl.BlockSpec((None, None, tr, cc), lambda i, sp, _l=layer[0]: (_l, sp[1], i, 0)))
        (f_k,) = _tiled_sp(f"chip_add_{tag}{k}", add4, (rr // tr,), place, ins, [out],
                           None if into is None else into[k])
        fulls.append(f_k)
    return fulls


def _pack(parts, PACK_ROWS=PACK_ROWS):
    flat, offs, pos = [], [], 0
    for p in parts:
        v = p.reshape(-1).astype(F32)
        n = -(-v.shape[0] // LANE) * LANE
        flat.append(jnp.pad(v, (0, n - v.shape[0])))
        offs.append((pos, v.shape[0], p.shape))
        pos += n
    total = -(-pos // (PACK_ROWS * LANE)) * PACK_ROWS * LANE
    flat.append(jnp.zeros((total - pos,), F32))
    return jnp.concatenate(flat).reshape(-1, LANE), offs


def _unpack(vec, offs):
    v = vec.reshape(-1)
    return [v[p:p + n].reshape(shape) for p, n, shape in offs]


def _adamw_math(wv, gv, mv, vv):
    bc1 = 1.0 - ADAM_B1 ** ADAM_STEP
    bc2 = 1.0 - ADAM_B2 ** ADAM_STEP
    mn = ADAM_B1 * mv + (1.0 - ADAM_B1) * gv
    vn = ADAM_B2 * vv + (1.0 - ADAM_B2) * (gv * gv)
    delta = -ADAM_LR * ((mn / bc1) / (jnp.sqrt(vn / bc2) + ADAM_EPS) + ADAM_WD * wv)
    return delta, mn, vn


def _adamw(name, w, g, m, v, dep=None):
    rows, cols = w.shape
    tr = rows
    for cand in (512, 256, 128, 64, 32, 16, 8):
        if rows % cand == 0 and cand * cols * 4 <= 2 * 1024 * 1024:
            tr = cand
            break

    def fn(ids, tin, vin):
        return list(_adamw_math(*tin)), []

    spec = pl.BlockSpec((tr, cols), lambda i: (i, 0))
    outs = [(_sds((rows, cols), F32), spec)] * 3
    return _tiled(name, fn, (rows // tr,), [(a, spec) for a in (w, g, m, v)], _behind(dep), outs)


def _adamw_many(name, ws, gs, ms, vs):
    n = len(ws)
    views = [(-1, a.shape[-1]) if a.ndim > 1 else (1, -1) for a in ws]
    flat = lambda arrs: [a.reshape(vw) for a, vw in zip(arrs, views)]

    def kern(*refs):
        ins, outs = refs[:4 * n], refs[4 * n:]
        for t in range(n):
            res = _adamw_math(*[ins[q * n + t][...] for q in range(4)])
            for q in range(3):
                outs[q * n + t][...] = res[q]

    shapes = [_sds(a.shape, F32) for a in flat(ws)]
    res = _pcall(kern, name=name, out_shape=shapes * 3, compiler_params=_cparams(),
                 )(*flat(ws), *flat(gs), *flat(ms), *flat(vs))
    back = lambda part: [a.reshape(w.shape) for a, w in zip(part, ws)]
    return back(res[:n]), back(res[n:2 * n]), back(res[2 * n:])


def _pos_embed():
    n_rows = T // GRID_W
    q = D // 4
    omega = 1.0 / (10000.0 ** (jnp.arange(q, dtype=F32) / q))
    er = jnp.arange(n_rows, dtype=jnp.int32).astype(F32)[:, None] * omega[None, :]
    ec = jnp.arange(GRID_W, dtype=jnp.int32).astype(F32)[:, None] * omega[None, :]
    by_row = jnp.concatenate([jnp.sin(er), jnp.cos(er)], axis=-1)
    by_col = jnp.concatenate([jnp.sin(ec), jnp.cos(ec)], axis=-1)
    return jnp.concatenate([jnp.repeat(by_row, GRID_W, axis=0), jnp.tile(by_col, (n_rows, 1))], axis=-1)


def _dense_gates(w_a, w_x):
    rows = jnp.stack([w_a[0], w_x[0], w_a[1], w_x[1]]).reshape(4, 2, RH, BLK)
    mask, spread = _block_mask(), _block_spread().T.astype(BF16)

    def kern(r_ref, m_ref, s_ref, o_ref):
        tiled = jnp.dot(r_ref[...].astype(BF16), s_ref[...], preferred_element_type=F32)
        o_ref[...] = (tiled * m_ref[...]).astype(o_ref.dtype)

    return _pcall(
        kern, name="gates_dense", grid=(2, 4),
        in_specs=[pl.BlockSpec((None, None, RH, BLK), lambda h, q: (q, h, 0, 0)),
                  pl.BlockSpec((RH, RH), lambda h, q: (0, 0)), pl.BlockSpec((BLK, RH), lambda h, q: (0, 0))],
        out_specs=pl.BlockSpec((None, RH, RH), lambda h, q: (h, 0, q)),
        out_shape=_sds((2, RH, NQ), BF16),
    )(rows, mask, spread)


def _block_mask():
    r = lax.broadcasted_iota(jnp.int32, (RH, RH), 0) // BLK
    c = lax.broadcasted_iota(jnp.int32, (RH, RH), 1) // BLK
    return (r == c).astype(F32)


def _block_spread():
    c = lax.broadcasted_iota(jnp.int32, (RH, BLK), 0) % BLK
    j = lax.broadcasted_iota(jnp.int32, (RH, BLK), 1)
    return (c == j).astype(F32)


def _fold_blocks(dense, mask, spread):
    return jnp.dot(dense * mask, spread, preferred_element_type=F32, precision=lax.Precision.HIGHEST)


def _gate_block_grads(folded):
    per = N_BLK // 2
    kinds = [jnp.concatenate([folded[h, q].reshape(per, BLK, BLK) for h in range(2)], axis=0) for q in range(4)]
    return jnp.stack([kinds[0], kinds[2]]), jnp.stack([kinds[1], kinds[3]])


def _gate_bias_dense(b_a, b_x):
    cols = []
    for h in range(2):
        for src in (b_a[0], b_x[0], b_a[1], b_x[1]):
            cols.append(src.reshape(R)[h * RH:(h + 1) * RH])
    return jnp.concatenate(cols).reshape(1, 2 * NQ)


def _gate_bias_grads(dgb):
    v = dgb.reshape(2, 4, RH)
    kinds = [jnp.concatenate([v[0, q], v[1, q]]).reshape(N_BLK, BLK) for q in range(4)]
    return jnp.stack([kinds[0], kinds[2]]), jnp.stack([kinds[1], kinds[3]])


def _residual_epilogue(next_norm):
    def epi(acc, ex):
        x_new = ex[0] + ex[1] * acc
        outs = [acc, x_new]
        if next_norm:
            outs.append(_norm_mod(x_new, ex[-3], ex[-2], ex[-1]))
        return outs
    return epi


def _mlp_fwd(tag, x_in, h, gate, w_in, w_out, next_norm=None, dep=None):
    tm = MM_TILE
    (r,) = _mm(f"{tag}_in", h, w_in, _NN, (T // tm, 4, 1),
               pl.BlockSpec((tm, D), lambda i, j, k: (i, 0)), pl.BlockSpec((None, D, D), lambda i, j, k: (j, 0, 0)),
               [(_sds((T, FF), BF16), pl.BlockSpec((tm, D), lambda i, j, k: (i, j)))], (tm, D),
               extra=[(d_, _full_spec(d_)) for d_ in _behind(dep)], epi=lambda acc, ex: [jnp.maximum(acc, 0.0)])
    row_spec = pl.BlockSpec((tm, D), lambda i, j, k: (i, 0))
    outs = [(_sds((T, D), F32), row_spec)] * 2 + ([(_sds((T, D), BF16), row_spec)] if next_norm else [])
    res = _mm(f"{tag}_out", r, w_out, _NN, (T // tm, 1, FF // D),
              pl.BlockSpec((tm, D), lambda i, j, k: (i, k)), pl.BlockSpec((D, D), lambda i, j, k: (k, 0)),
              outs, (tm, D),
              extra=[(x_in, row_spec), (gate, _full_spec(gate))] + [(v, _full_spec(v)) for v in next_norm or ()],
              a_pre=lambda a: a * a, epi=_residual_epilogue(next_norm))
    return dict(h=h, r=r, o=res[0], x_in=x_in), res[1], (res[2] if next_norm else None)


def _behind(dep):
    return [] if dep is None else [dep]


def _gate_bwd(tag, dx, o, gate, dep=None):
    def fn(ids, t, v):
        d_o = t[0] * v[0]
        return [d_o], [_sum0(t[0] * t[1]), _sum0(d_o)]
    return _tiled(f"{tag}_gate_bwd", fn, (T // ROW_TILE,), [_rows(dx), _rows(o)], [gate] + _behind(dep),
                  [_orow(T, D, BF16)], [(1, D), (1, D)])


def _norm_bwd(tag, dx_res, dh, dh_off, x, g_norm, sc, with_dx=True, dep=None):
    n_t = x.shape[0] // ROW_TILE

    def fn(ids, t, v):
        if with_dx:
            dres, dhv, xv = t
        else:
            dhv, xv = t
        dxv, d_sh, d_sc, d_g = _norm_mod_bwd(dhv, xv, v[0], v[1])
        return ([dres + dxv] if with_dx else []), [d_sh, d_sc, d_g]

    ins = ([_rows(dx_res)] if with_dx else []) + [_rows(dh, off=dh_off), _rows(x)]
    outs = [_orow(x.shape[0], D, F32)] if with_dx else []
    return _tiled(f"{tag}_norm_bwd", fn, (n_t,), ins, [g_norm, sc] + _behind(dep), outs, [(1, D)] * 3)


def _mlp_bwd(tag, dx, saved, g_norm, sc, gate, w_in, w_out, dep=None):
    d_o, d_gate, _ = _gate_bwd(tag, dx, saved["o"], gate, dep)
    tm = MM_TILE
    r = saved["r"]
    (da,) = _mm(f"{tag}_dz", d_o, w_out, _NT, (T // tm, FF // D, 1),
                pl.BlockSpec((tm, D), lambda i, j, k: (i, 0)), pl.BlockSpec((D, D), lambda i, j, k: (j, 0)),
                [(_sds((T, FF), BF16), pl.BlockSpec((tm, D), lambda i, j, k: (i, j)))], (tm, D),
                extra=[(r, pl.BlockSpec((tm, D), lambda i, j, k: (i, j)))],
                epi=lambda acc, ex: [acc * (2.0 * ex[0].astype(F32))])
    tk = MM_TILE
    (dw_out,) = _mm(f"{tag}_dwout", r, d_o, _TN, (FF // tm, 1, T // tk),
                    pl.BlockSpec((tk, tm), lambda i, j, k: (k, i)), pl.BlockSpec((tk, D), lambda i, j, k: (k, 0)),
                    [(_sds((FF, D), BF16), pl.BlockSpec((tm, D), lambda i, j, k: (i, 0)))], (tm, D),
                    a_pre=lambda a: a * a)
    (dh,) = _mm(f"{tag}_dh", da, w_in, _NT, (T // tm, 1, 4),
                pl.BlockSpec((tm, D), lambda i, j, k: (i, k)), pl.BlockSpec((None, D, D), lambda i, j, k: (k, 0, 0)),
                [(_sds((T, D), F32), pl.BlockSpec((tm, D), lambda i, j, k: (i, 0)))], (tm, D))
    (dw_in,) = _mm(f"{tag}_dwin", saved["h"], da, _TN, (D // tm, 4, T // tk),
                   pl.BlockSpec((tk, tm), lambda i, j, k: (k, i)), pl.BlockSpec((tk, D), lambda i, j, k: (k, j)),
                   [(_sds((4, D, D), BF16), pl.BlockSpec((None, tm, D), lambda i, j, k: (j, i, 0)))], (tm, D))
    dx_in, d_sh, d_sc, d_g = _norm_bwd(tag, dx, dh, 0, saved["x_in"], g_norm, sc)
    return dx_in, dw_in, dw_out, dict(sh=d_sh, sc=d_sc, gate=d_gate, g_norm=d_g)


def _local_step(x, ctx, tgt, mods, cmods, norm_g, final_g, rec, conf, wg, on_grads=None, wg_pre=None, on_later=None):
    on_grads = on_grads or (lambda group, dws: None)
    wg_pre = wg_pre or (lambda group, after: None)
    on_later = on_later or (lambda after: None)
    n_t = T // ROW_TILE
    row = lambda v: v.reshape(1, -1)
    m0 = [row(mods[0, q]) for q in range(6)]
    m1 = [row(mods[1, q]) for q in range(6)]
    g00, g01, g10, g11 = (row(norm_g[0, 0]), row(norm_g[0, 1]), row(norm_g[1, 0]), row(norm_g[1, 1]))
    csh, csc = row(cmods[0]), row(cmods[1])
    pos = _pos_embed()

    def prep0(ids, t, v):
        cx, xv, pv = t
        is_ctx = ids[0] == 0
        xin = jnp.where(is_ctx, cx, xv + pv)
        sh = jnp.where(is_ctx, v[3], v[1])
        sc = jnp.where(is_ctx, v[4], v[2])
        return [_norm_mod(xin, v[0], sc, sh), xv + pv], []

    dep = wg_pre("rec_in", csh)
    hcat, x0 = _tiled(
        "prep0", prep0, (N_SCAN,),
        [(ctx, pl.BlockSpec((ROW_TILE, D), lambda i: (0, 0))), _rows(x, off=-1, clamp_lo=True),
         _rows(pos, off=-1, clamp_lo=True)],
        [g00, m0[0], m0[1], csh, csc] + _behind(dep),
        [_orow(TA, D, BF16), _orow(T, D, F32, off=-1, clamp_lo=True)])

    tm_a = REC_TILE
    w_rin = wg("rec_in", hcat)["rec_w_in"]
    (a_in,) = _mm("rec_in", hcat, w_rin, _NN, (TA // tm_a, 4, 1),
                  pl.BlockSpec((tm_a, D), lambda i, j, k: (i, 0)),
                  pl.BlockSpec((None, D, RH), lambda i, j, k: (j, 0, 0)),
                  [(_sds((TA, 2 * R), F32), pl.BlockSpec((tm_a, RH), lambda i, j, k: (i, j)))], (tm_a, RH))
    rec_starts = (0, 1)
    u = _dwconv("rec_conv", a_in, R // CW_REC, rec["conv_w"], row(rec["conv_b"]), 1, rec_starts, R, CW_REC)
    wbd = _dense_gates(rec["w_a"], rec["w_x"])
    gbias = _gate_bias_dense(rec["b_a"], rec["b_x"])
    lam = rec["lam"]
    a_f, b_f, a_r, b_r = _tiled("rg_fwd", _rg_fwd_fn, (TA // RG_TILE,), [_rows(u, tm=RG_TILE)], [wbd, gbias, lam],
                                [_orow(TA, R, F32, tm=RG_TILE)] * 4, vec_refs=True)
    dep = wg_pre("rec_out", a_f)
    dep = wg_pre("mlp0", a_f if dep is None else dep)
    y_f, y_r, hin_f, hin_r = _scan_fwd(a_f, b_f, a_r, b_r)

    def rec_mid(ids, t, v):
        gp, yf, yr = t
        g, _ = _gelu(gp)
        return [g * (yf + yr)], []

    (m_rec,) = _tiled("rec_mid", rec_mid, (n_t,),
                      [_rows(a_in, R, off=1), _rows(y_f, off=1), _rows(y_r, off=1)], _behind(dep),
                      [_orow(T, R, BF16)])
    tm = MM_TILE
    row_spec = pl.BlockSpec((tm, D), lambda i, j, k: (i, 0))
    norm_mlp0 = (g01, m0[4], m0[3])
    w_rout = wg("rec_out", m_rec)["rec_w_out"]
    o_rec, x1, h_mlp0 = _mm(
        "rec_out", m_rec, w_rout, _NN, (T // tm, 1, 1),
        pl.BlockSpec((tm, R), lambda i, j, k: (i, 0)), pl.BlockSpec((R, D), lambda i, j, k: (0, 0)),
        [(_sds((T, D), F32), row_spec)] * 2 + [(_sds((T, D), BF16), row_spec)], (tm, D),
        extra=[(x0, row_spec), (m0[2], _full_spec(m0[2]))] + [(v, _full_spec(v)) for v in norm_mlp0],
        epi=_residual_epilogue(norm_mlp0))
    w_m0 = wg("mlp0", x1)
    dep = wg_pre("conf", x1)
    mlp0, x2, h1 = _mlp_fwd("mlp0", x1, h_mlp0, m0[5], w_m0["w_in"], w_m0["w_out"], (g10, m1[1], m1[0]), dep)

    b_pw1 = row(conf["b_pw1"])
    w_cf = wg("conf", x2)
    dep = wg_pre("mlp1", x2)
    (pre,) = _mm("conf_pw1", h1, w_cf["conf_w_pw1"], _NN, (T // tm, 4, 1),
                 pl.BlockSpec((tm, D), lambda i, j, k: (i, 0)),
                 pl.BlockSpec((None, D, D // 2), lambda i, j, k: (j, 0, 0)),
                 [(_sds((T, 2 * D), F32), pl.BlockSpec((tm, D // 2), lambda i, j, k: (i, j)))], (tm, D // 2),
                 extra=[(b_pw1, pl.BlockSpec((1, D // 2), lambda i, j, k: (0, j)))]
                 + [(d_, _full_spec(d_)) for d_ in _behind(dep)],
                 epi=lambda acc, ex: [acc + ex[0]])
    (zg,) = _tiled("conf_glu", lambda ids, t, v: ([t[0] * _sigmoid(t[1])], []), (n_t,),
                   [_rows(pre, D, col=0), _rows(pre, D, col=1)], [], [_orow(T, D, F32)])
    conf_starts = (0,)
    zc = _dwconv("conf_conv", zg, 0, conf["conv_w"], row(conf["conv_b"]), CONF_KW // 2, conf_starts, D, CW_CONF)
    ln_g, ln_b = row(conf["ln_g"]), row(conf["ln_b"])

    def ln_silu(ids, t, v):
        nh, _ = _layernorm_parts(t[0])
        ln = nh * v[0] + v[1]
        return [ln * _sigmoid(ln)], []

    (s_conf,) = _tiled("conf_ln", ln_silu, (n_t,), [_rows(zc)], [ln_g, ln_b], [_orow(T, D, BF16)])
    b_pw2 = row(conf["b_pw2"])
    norm_mlp1 = (g11, m1[4], m1[3])
    pw2_epi = _residual_epilogue(norm_mlp1)
    y_conf, x3, h_mlp1 = _mm(
        "conf_pw2", s_conf, w_cf["conf_w_pw2"], _NN, (T // tm, 1, 1),
        row_spec, pl.BlockSpec((D, D), lambda i, j, k: (0, 0)),
        [(_sds((T, D), F32), row_spec)] * 2 + [(_sds((T, D), BF16), row_spec)], (tm, D),
        extra=[(x2, row_spec), (m1[2], _full_spec(m1[2])), (b_pw2, _full_spec(b_pw2))]
        + [(v, _full_spec(v)) for v in norm_mlp1],
        epi=lambda acc, ex: pw2_epi(acc + ex[2], ex))
    w_m1 = wg("mlp1", x3)
    mlp1, x4, _ = _mlp_fwd("mlp1", x3, h_mlp1, m1[5], w_m1["w_in"], w_m1["w_out"])

    fg = row(final_g)

    def head(ids, t, v):
        n, r = _rms(t[0])
        err = n * v[0] - t[1]
        d_out = err * (1.0 / D)
        dn = d_out * v[0]
        dxv = r * (dn - n * jnp.mean(dn * n, axis=-1, keepdims=True))
        part = jnp.sum(_sum0(err * err), axis=1, keepdims=True) * (0.5 / D)
        return [dxv], [part, _sum0(d_out * n)]

    dx4, loss, d_fg = _tiled("head", head, (n_t,), [_rows(x4), _rows(tgt)], [fg], [_orow(T, D, F32)],
                             [(1, 1), (1, D)])

    dx3, dw_in1, dw_out1, dm_mlp1 = _mlp_bwd("mlp1", dx4, mlp1, g11, m1[4], m1[5],
                                             w_m1["w_in"], w_m1["w_out"])
    dep = on_grads("mlp1", (dw_in1, dw_out1))
    d_y, d_g1c, d_bpw2 = _gate_bwd("conf", dx3, y_conf, m1[2], dep)
    tk = MM_TILE
    (dw_pw2,) = _mm("conf_dwpw2", s_conf, d_y, _TN, (D // tm, 1, T // tk),
                    pl.BlockSpec((tk, tm), lambda i, j, k: (k, i)), pl.BlockSpec((tk, D), lambda i, j, k: (k, 0)),
                    [(_sds((D, D), BF16), pl.BlockSpec((tm, D), lambda i, j, k: (i, 0)))], (tm, D))
    (ds,) = _mm("conf_ds", d_y, w_cf["conf_w_pw2"], _NT, (T // tm, 1, 1),
                pl.BlockSpec((tm, D), lambda i, j, k: (i, 0)), pl.BlockSpec((D, D), lambda i, j, k: (0, 0)),
                [(_sds((T, D), F32), pl.BlockSpec((tm, D), lambda i, j, k: (i, 0)))], (tm, D))
    dep = on_later(ds)

    def ln_silu_bwd(ids, t, v):
        dsv, zcv = t
        nh, rstd = _layernorm_parts(zcv)
        ln = nh * v[0] + v[1]
        sg = _sigmoid(ln)
        d_ln = dsv * (sg * (1.0 + ln * (1.0 - sg)))
        d_nh = d_ln * v[0]
        d_zc = rstd * (d_nh - jnp.mean(d_nh, axis=-1, keepdims=True)
                       - nh * jnp.mean(d_nh * nh, axis=-1, keepdims=True))
        return [d_zc], [_sum0(d_ln * nh), _sum0(d_ln)]

    d_zc, d_lng, d_lnb = _tiled("conf_ln_bwd", ln_silu_bwd, (n_t,), [_rows(ds), _rows(zc)],
                                [ln_g, ln_b] + _behind(dep), [_orow(T, D, F32)], [(1, D), (1, D)])
    d_zg = _dwconv("conf_conv_dx", d_zc, 0, conf["conv_w"], jnp.zeros((1, D), F32),
                   CONF_KW - 1 - CONF_KW // 2, conf_starts, D, CW_CONF, flip=True)

    def glu_bwd(ids, t, v):
        dz, pa, pb = t
        sg = _sigmoid(pb)
        d_a = dz * sg
        d_b = dz * pa * sg * (1.0 - sg)
        return [jnp.concatenate([d_a, d_b], axis=1)], [_sum0(d_a), _sum0(d_b)]

    d_pre, d_b1a, d_b1b = _tiled(
        "conf_glu_bwd", glu_bwd, (n_t,), [_rows(d_zg), _rows(pre, D, col=0), _rows(pre, D, col=1)], [],
        [_orow(T, 2 * D, BF16)], [(1, D), (1, D)])
    (dw_pw1,) = _mm("conf_dwpw1", h1, d_pre, _TN, (D // tm, 4, T // tk),
                    pl.BlockSpec((tk, tm), lambda i, j, k: (k, i)),
                    pl.BlockSpec((tk, D // 2), lambda i, j, k: (k, j)),
                    [(_sds((4, D, D // 2), BF16), pl.BlockSpec((None, tm, D // 2), lambda i, j, k: (j, i, 0)))],
                    (tm, D // 2))
    dep = on_grads("conf", (dw_pw1, dw_pw2))
    (dh1,) = _mm("conf_dh", d_pre, w_cf["conf_w_pw1"], _NT, (T // tm, 1, 4),
                 pl.BlockSpec((tm, D // 2), lambda i, j, k: (i, k)),
                 pl.BlockSpec((None, D, D // 2), lambda i, j, k: (k, 0, 0)),
                 [(_sds((T, D), F32), pl.BlockSpec((tm, D), lambda i, j, k: (i, 0)))], (tm, D))
    dx2, d_sh1c, d_sc1c, d_g10 = _norm_bwd("conf", dx3, dh1, 0, x2, g10, m1[1], dep=dep)
    dep = on_later(dx2)

    dx1, dw_in0, dw_out0, dm_mlp0 = _mlp_bwd("mlp0", dx2, mlp0, g01, m0[4], m0[5],
                                             w_m0["w_in"], w_m0["w_out"], dep)
    dep = on_grads("mlp0", (dw_in0, dw_out0))
    d_orec, d_g1r, _ = _gate_bwd("rec", dx1, o_rec, m0[2], dep)
    (dw_rout,) = _mm("rec_dwout", m_rec, d_orec, _TN, (R // RH, 1, T // tk),
                     pl.BlockSpec((tk, RH), lambda i, j, k: (k, i)), pl.BlockSpec((tk, D), lambda i, j, k: (k, 0)),
                     [(_sds((R, D), BF16), pl.BlockSpec((RH, D), lambda i, j, k: (i, 0)))], (RH, D))
    (dm_rec,) = _mm("rec_dm", d_orec, w_rout, _NT, (T // tm, 1, 1),
                    pl.BlockSpec((tm, D), lambda i, j, k: (i, 0)), pl.BlockSpec((R, D), lambda i, j, k: (0, 0)),
                    [(_sds((T, R), F32), pl.BlockSpec((tm, R), lambda i, j, k: (i, 0)))], (tm, R))
    dep = on_later(dm_rec)

    def rec_mid_bwd(ids, t, v):
        dmv, gp, yf, yr = t
        g, th = _gelu(gp)
        lat = ids[0] > 0
        d_gp = jnp.where(lat, dmv * (yf + yr) * _gelu_grad(gp, th), 0.0)
        dy = jnp.where(lat, dmv * g, 0.0)
        return [d_gp, dy], []

    d_a, dy = _tiled("rec_mid_bwd", rec_mid_bwd, (N_SCAN,),
                     [_rows(dm_rec, off=-1, clamp_lo=True), _rows(a_in, R), _rows(y_f), _rows(y_r)], _behind(dep),
                     [(_sds((TA, 2 * R), BF16), pl.BlockSpec((ROW_TILE, R), lambda i: (i, 0))), _orow(TA, R, F32)])
    da_f, db_f, da_r, db_r = _scan_bwd(dy, a_f, y_f, hin_f, a_r, y_r, hin_r)
    d_gpre, d_u, d_gbias, d_lam = _tiled(
        "rg_bwd", _rg_bwd_fn, (TA // RG_TILE,), [_rows(a, tm=RG_TILE) for a in (u, da_f, db_f, da_r, db_r)],
        [wbd, gbias, lam], [_orow(TA, 2 * NQ, BF16, tm=RG_TILE), _orow(TA, R, F32, tm=RG_TILE)],
        [(1, 2 * NQ), (1, 2 * R)], vec_refs=True)
    tk_a = REC_TILE
    d_a = _dwconv("rec_conv_dx", d_u, 0, rec["conv_w"], jnp.zeros((1, R), F32), REC_KW - 1 - 1,
                  rec_starts, R, CW_REC, flip=True, into=(d_a, R // CW_REC))
    (dw_rin,) = _mm("rec_dwin", hcat, d_a, _TN, (D // tm, 4, TA // tk_a),
                    pl.BlockSpec((tk_a, tm), lambda i, j, k: (k, i)), pl.BlockSpec((tk_a, RH), lambda i, j, k: (k, j)),
                    [(_sds((4, D, RH), BF16), pl.BlockSpec((None, tm, RH), lambda i, j, k: (j, i, 0)))], (tm, RH))
    dep = on_grads("rec", (dw_rin, dw_rout))
    (dhcat,) = _mm("rec_dh", d_a, w_rin, _NT, (TA // tm_a, 1, 4),
                   pl.BlockSpec((tm_a, RH), lambda i, j, k: (i, k)),
                   pl.BlockSpec((None, D, RH), lambda i, j, k: (k, 0, 0)),
                   [(_sds((TA, D), F32), pl.BlockSpec((tm_a, D), lambda i, j, k: (i, 0)))], (tm_a, D))
    dx0, d_sh1r, d_sc1r, d_g00 = _norm_bwd("rec", dx1, dhcat, 1, x0, g00, m0[1], dep=dep)
    dep = on_later(dx0)

    d_csh, d_csc, d_g00c = _norm_bwd("ctx", None, dhcat, 0, ctx, g00, csc, with_dx=False, dep=dep)
    blk_mask, blk_spread = _block_mask(), _block_spread()
    (d_wbd,) = _mm("rg_dw", u, d_gpre, _TN, (2, 2, TA // tk_a),
                   pl.BlockSpec((tk_a, RH), lambda i, j, k: (k, i)),
                   pl.BlockSpec((tk_a, NQ // 2), lambda i, j, k: (k, 2 * i + j)),
                   [(_sds((2, 4, RH, BLK), F32), pl.BlockSpec((None, 2, RH, BLK), lambda i, j, k: (i, j, 0, 0)))],
                   (RH, NQ // 2),
                   extra=[(blk_mask, _full_spec(blk_mask)), (blk_spread, _full_spec(blk_spread))]
                   + [(d, _full_spec(d)) for d in _behind(dep)],
                   epi=lambda acc, ex: [jnp.stack([_fold_blocks(acc[:, s * RH:(s + 1) * RH], ex[0], ex[1])
                                                   for s in range(2)])])
    d_cw_rec = _dwconv_wgrad("rec_conv_dw", d_u, a_in, R // CW_REC, REC_KW, 1, rec_starts, R, CW_REC, dep)
    d_cw_conf = _dwconv_wgrad("conf_conv_dw", d_zc, zg, 0, CONF_KW, CONF_KW // 2, conf_starts, D, CW_CONF, dep)

    big = dict(rec_w_in=dw_rin, rec_w_out=dw_rout, conf_w_pw1=dw_pw1, conf_w_pw2=dw_pw2,
               mlp_w_in=(dw_in0, dw_in1), mlp_w_out=(dw_out0, dw_out1))
    d_wa, d_wx = _gate_block_grads(d_wbd)
    d_ba, d_bx = _gate_bias_grads(d_gbias)
    d_mod = jnp.concatenate([
        d_sh1r, d_sc1r, d_g1r, dm_mlp0["sh"], dm_mlp0["sc"], dm_mlp0["gate"],
        d_sh1c, d_sc1c, d_g1c, dm_mlp1["sh"], dm_mlp1["sc"], dm_mlp1["gate"]], axis=1).reshape(2, 6 * D)
    small = dict(
        d_mod=d_mod, d_cmod=jnp.concatenate([d_csh, d_csc], axis=1),
        norm_g=jnp.concatenate([d_g00 + d_g00c, dm_mlp0["g_norm"], d_g10, dm_mlp1["g_norm"]], axis=1),
        rec_conv_w=d_cw_rec[:REC_KW], rec_conv_b=d_cw_rec[REC_KW], rec_lambda=d_lam.reshape(2, R),
        rec_w_a=d_wa, rec_b_a=d_ba, rec_w_x=d_wx, rec_b_x=d_bx,
        conf_b_pw1=jnp.concatenate([d_b1a, d_b1b], axis=1), conf_conv_w=d_cw_conf[:CONF_KW],
        conf_conv_b=d_cw_conf[CONF_KW], conf_ln_g=d_lng, conf_ln_b=d_lnb, conf_b_pw2=d_bpw2, final_g=d_fg)
    return loss.reshape(()), dx0, big, small


_BIG = ("rec_w_in", "rec_w_out", "conf_w_pw1", "conf_w_pw2", "mlp_w_in", "mlp_w_out")


def _halves(w):
    return w.reshape(w.shape[0], 2, w.shape[1] // 2, w.shape[2])


def _ada_fwd(c16, w_ada, b_shard):
    ns = w_ada.shape[2]
    tn = 512

    def kern(c_ref, w_ref, b_ref, o_ref):
        cv = c_ref[...]
        s = (cv * _sigmoid(cv)).astype(BF16)
        o_ref[...] = jnp.dot(s, w_ref[...].astype(BF16), preferred_element_type=F32) + b_ref[...]

    return _pcall(
        kern, name="ada_fwd", grid=(2, ns // tn),
        in_specs=[pl.BlockSpec((16, D), lambda l, j: (0, 0)), pl.BlockSpec((None, D, tn), lambda l, j: (l, 0, j)),
                  pl.BlockSpec((None, 1, tn), lambda l, j: (l, 0, j))],
        out_specs=pl.BlockSpec((None, 16, tn), lambda l, j: (l, 0, j)),
        out_shape=_sds((2, 16, ns), F32), compiler_params=_cparams(),
    )(c16, w_ada, b_shard)


def _ada_bwd(c16, dm16, w_ada):
    ns = w_ada.shape[2]
    tn = 512

    def kern(c_ref, dm_ref, w_ref, gw_ref, ds_ref):
        cv = c_ref[...]
        s = (cv * _sigmoid(cv)).astype(BF16)
        dm = dm_ref[...].astype(BF16)
        gw_ref[...] = lax.dot_general(s, dm, _TN, preferred_element_type=F32)

        @pl.when(jnp.logical_and(pl.program_id(0) == 0, pl.program_id(1) == 0))
        def _():
            ds_ref[...] = jnp.zeros_like(ds_ref)

        ds_ref[...] += lax.dot_general(dm, w_ref[...].astype(BF16), _NT, preferred_element_type=F32)

    return _pcall(
        kern, name="ada_bwd", grid=(2, ns // tn),
        in_specs=[pl.BlockSpec((16, D), lambda l, j: (0, 0)), pl.BlockSpec((None, 16, tn), lambda l, j: (l, 0, j)),
                  pl.BlockSpec((None, D, tn), lambda l, j: (l, 0, j))],
        out_specs=[pl.BlockSpec((None, D, tn), lambda l, j: (l, 0, j)), pl.BlockSpec((16, D), lambda l, j: (0, 0))],
        out_shape=[_sds((2, D, ns), F32), _sds((16, D), F32)], compiler_params=_cparams(),
    )(c16, dm16, w_ada)


def _cctx_grad(ds4, c_ctx):
    def kern(d_ref, c_ref, o_ref):
        tot = d_ref[0, 0:1, :] + d_ref[1, 0:1, :] + d_ref[2, 0:1, :] + d_ref[3, 0:1, :]
        cv = c_ref[...]
        sg = _sigmoid(cv)
        o_ref[...] = tot * (sg * (1.0 + cv * (1.0 - sg)))

    return _pcall(kern, name="cctx_grad", out_shape=_sds((1, D), F32))(ds4, c_ctx.reshape(1, D))


def kernel(x, c, ctx, c_ctx, w_ada, b_ada, norm_g, rec_w_in, rec_conv_w, rec_conv_b, rec_lambda, rec_w_a, rec_b_a, rec_w_x, rec_b_x, rec_w_out, conf_w_pw1, conf_b_pw1, conf_conv_w, conf_conv_b, conf_ln_g, conf_ln_b, conf_w_pw2, conf_b_pw2, mlp_w_in, mlp_w_out, final_g, loss_target, m_c_ctx, m_w_ada, m_b_ada, m_norm_g, m_rec_w_in, m_rec_conv_w, m_rec_conv_b, m_rec_lambda, m_rec_w_a, m_rec_b_a, m_rec_w_x, m_rec_b_x, m_rec_w_out, m_conf_w_pw1, m_conf_b_pw1, m_conf_conv_w, m_conf_conv_b, m_conf_ln_g, m_conf_ln_b, m_conf_w_pw2, m_conf_b_pw2, m_mlp_w_in, m_mlp_w_out, m_final_g, v_c_ctx, v_w_ada, v_b_ada, v_norm_g, v_rec_w_in, v_rec_conv_w, v_rec_conv_b, v_rec_lambda, v_rec_w_a, v_rec_b_a, v_rec_w_x, v_rec_b_x, v_rec_w_out, v_conf_w_pw1, v_conf_b_pw1, v_conf_conv_w, v_conf_conv_b, v_conf_ln_g, v_conf_ln_b, v_conf_w_pw2, v_conf_b_pw2, v_mlp_w_in, v_mlp_w_out, v_final_g):
    names = ["c_ctx", "w_ada", "b_ada", "norm_g", "rec_w_in", "rec_conv_w", "rec_conv_b", "rec_lambda", "rec_w_a",
             "rec_b_a", "rec_w_x", "rec_b_x", "rec_w_out", "conf_w_pw1", "conf_b_pw1", "conf_conv_w", "conf_conv_b",
             "conf_ln_g", "conf_ln_b", "conf_w_pw2", "conf_b_pw2", "mlp_w_in", "mlp_w_out", "final_g"]
    w = dict(zip(names, [c_ctx, w_ada, b_ada, norm_g, rec_w_in, rec_conv_w, rec_conv_b, rec_lambda, rec_w_a,
                         rec_b_a, rec_w_x, rec_b_x, rec_w_out, conf_w_pw1, conf_b_pw1, conf_conv_w, conf_conv_b,
                         conf_ln_g, conf_ln_b, conf_w_pw2, conf_b_pw2, mlp_w_in, mlp_w_out, final_g]))
    m = dict(zip(names, [m_c_ctx, m_w_ada, m_b_ada, m_norm_g, m_rec_w_in, m_rec_conv_w, m_rec_conv_b, m_rec_lambda,
                         m_rec_w_a, m_rec_b_a, m_rec_w_x, m_rec_b_x, m_rec_w_out, m_conf_w_pw1, m_conf_b_pw1,
                         m_conf_conv_w, m_conf_conv_b, m_conf_ln_g, m_conf_ln_b, m_conf_w_pw2, m_conf_b_pw2,
                         m_mlp_w_in, m_mlp_w_out, m_final_g]))
    v = dict(zip(names, [v_c_ctx, v_w_ada, v_b_ada, v_norm_g, v_rec_w_in, v_rec_conv_w, v_rec_conv_b, v_rec_lambda,
                         v_rec_w_a, v_rec_b_a, v_rec_w_x, v_rec_b_x, v_rec_w_out, v_conf_w_pw1, v_conf_b_pw1,
                         v_conf_conv_w, v_conf_conv_b, v_conf_ln_g, v_conf_ln_b, v_conf_w_pw2, v_conf_b_pw2,
                         v_mlp_w_in, v_mlp_w_out, v_final_g]))
    mx, my, mc = _me()
    chip = 2 * mx + my
    me = 4 * mx + 2 * my + mc

    sharded_small = ["norm_g", "rec_conv_w", "rec_lambda", "conf_b_pw1", "conf_conv_w", "conf_conv_b", "conf_ln_g",
                     "conf_ln_b", "conf_b_pw2"]
    packed, offs = _pack([c] + [w[k] for k in sharded_small], 8)
    place = jnp.stack([chip, mc]).astype(jnp.int32)
    shards = [("rec_in", _halves(rec_w_in), 0), ("rec_out", _halves(rec_w_out), 0),
              ("pw1", _halves(conf_w_pw1), 0), ("pw2", _halves(conf_w_pw2), 0),
              ("mlp_in0", _halves(mlp_w_in), 0), ("mlp_in1", _halves(mlp_w_in), 1),
              ("mlp_out0", _halves(mlp_w_out), 0), ("mlp_out1", _halves(mlp_w_out), 1)]
    small_state, small_sent = _send_start("gather_small_start", [packed[None]], _to_all7, [(7,) + packed.shape], place)
    (slot_rin,) = _place_big(shards[:1], place, small_sent)
    flying, gsems, swapping = {}, {}, {}
    flying["rec_in"], gsems["rec_in"], rec_started = _gather_start("gather_start_rec", [slot_rin], ((0,),), small_sent)
    slots = [slot_rin] + _place_big(shards[1:], place, rec_started)
    placed = jnp.broadcast_to(lax.dynamic_slice(slots[-1], (chip, 0, 0, 0), (1, 1, 1, 1)).reshape(1, 1), (8, 1))
    (own,), (landed,) = _send_wait("gather_small_wait", *small_state, _to_all7, placed)
    by_flip = jnp.concatenate([own, landed], axis=0)
    got_flat = jnp.take(by_flip, jnp.arange(8) ^ me, axis=0).reshape(8, -1)

    def piece(i):
        p, n, shape = offs[i]
        return got_flat[:, p:p + n].reshape((8,) + tuple(shape))

    c_rows = piece(0).reshape(8, D)
    full = {}
    for i, k in enumerate(sharded_small):
        per_chip = jnp.moveaxis(piece(1 + i)[0::2], 0, -2)
        full[k] = per_chip.reshape(per_chip.shape[:-2] + (4 * per_chip.shape[-1],))
    c16 = jnp.concatenate([c_rows, c_ctx.reshape(1, D), jnp.zeros((7, D), F32)], axis=0)

    ns = w_ada.shape[2]
    b_shard = lax.dynamic_slice_in_dim(b_ada, chip * ns, ns, axis=1).reshape(2, 1, ns)
    prod = _ada_fwd(c16, w_ada, b_shard)

    own_rows = lax.dynamic_index_in_dim(prod[:, :8].reshape(2, 4, 2, ns), mc, axis=2, keepdims=False)
    rows = jnp.concatenate([own_rows.transpose(1, 0, 2), jnp.broadcast_to(prod[0, 8], (4, 1, ns)),
                            jnp.zeros((4, 5, ns), F32)], axis=1)
    mod_state, mod_started = _send_start("mod_start", [rows], _to_chips, [(3, 8, ns)], place)
    use_order = dict(rec=(0, 1), mlp0=(4, 6), conf=(2, 3), mlp1=(5, 7))
    fetch_order = dict(rec_out=(1,), mlp0=(4, 6), conf=(2, 3), mlp1=(5, 7))
    order = [t for g in fetch_order for t in fetch_order[g]]
    groups = [tuple(order.index(t) for t in fetch_order[g]) for g in fetch_order]
    fly, sems, all_started = _gather_start("gather_start_rest", [slots[t] for t in order], tuple(groups), mod_started)
    for gi, g in enumerate(fetch_order):
        flying[g], gsems[g] = [fly[k] for k in groups[gi]], sems[2 * gi:2 * gi + 2]

    def wg_pre(group, after):
        bufs = _gather_wait(f"gather_wait_{group}", flying[group], *gsems[group], after)
        swapping[group], token = _swap_start(f"swap_start_{group}", bufs, after)
        return token

    def wg(group, after):
        bufs = _swap_wait(f"swap_wait_{group}", *swapping[group], after)
        if group == "rec_in":
            return dict(rec_w_in=bufs[0].reshape(4, D, RH))
        if group == "rec_out":
            return dict(rec_w_out=bufs[0].reshape(R, D))
        if group == "conf":
            return dict(conf_w_pw1=bufs[0].reshape(4, D, D // 2), conf_w_pw2=bufs[1].reshape(D, D))
        return dict(w_in=bufs[0].reshape(4, D, D), w_out=bufs[1].reshape(FF, D))

    (rows,), (landed,) = _send_wait("mod_wait", *mod_state, _to_chips, all_started)
    own = lax.dynamic_index_in_dim(rows, chip, axis=0, keepdims=True)
    by_flip = jnp.concatenate([own, landed[1:2], landed[0:1], landed[2:3]], axis=0)
    by_chip = jnp.take(by_flip, jnp.arange(4) ^ chip, axis=0)
    mods = by_chip[:, :2].transpose(1, 0, 2).reshape(2, 6, D)
    cmods = by_chip[:, 2].reshape(6, D)[:2]

    rec = dict(conv_w=full["rec_conv_w"][0], conv_b=rec_conv_b[0], lam=full["rec_lambda"][0],
               w_a=rec_w_a[0], b_a=rec_b_a[0], w_x=rec_w_x[0], b_x=rec_b_x[0])
    conf = dict(b_pw1=full["conf_b_pw1"][0], conv_w=full["conf_conv_w"][0], conv_b=full["conf_conv_b"][0],
                ln_g=full["conf_ln_g"][0], ln_b=full["conf_ln_b"][0], b_pw2=full["conf_b_pw2"][0])
    pairing, sent, sharing = {}, {}, {}

    def on_grads(group, dws):
        parts = [dw.reshape((4,) + shards[t][1].shape[1:]) for dw, t in zip(dws, use_order[group])]
        pairing[group], token = _reduce_begin(group, parts, place)
        return token

    def finish_pair(after):
        (group, state), = pairing.items()
        pairing.clear()
        sent[group], token = _reduce_mid(group, state, place, after)
        if group == "rec":
            mlp = _reduce_end("mlp1", sent["mlp1"], place, token, layer=(1, 2))
            cf = _reduce_end("conf", sent["conf"], place, token)
            mlp = _reduce_end("mlp0", sent["mlp0"], place, token, layer=(0, 2), into=mlp)
            sharing["state"], token = _share_start("share_start", cf + mlp, place)
        sent["token"] = token
        return token

    loss_local, grad_x, _, small = _local_step(x[0], ctx[0], loss_target[0], mods, cmods, full["norm_g"], final_g,
                                               rec, conf, wg, on_grads, wg_pre, finish_pair)
    rec_sent = sent["token"]
    small["loss"] = loss_local.reshape(1)

    small_names = ["loss", "d_mod", "d_cmod", "norm_g", "rec_conv_w", "rec_conv_b", "rec_lambda", "rec_w_a", "rec_b_a",
                   "rec_w_x", "rec_b_x", "conf_b_pw1", "conf_conv_w", "conf_conv_b", "conf_ln_g", "conf_ln_b",
                   "conf_b_pw2", "final_g"]
    mine = lax.broadcasted_iota(jnp.int32, (8, 1), 0) == me
    mod_slots = jnp.where(mine, small["d_mod"].reshape(1, -1), 0.0)
    spacked, soffs = _pack([small[k] for k in small_names] + [mod_slots])
    def sum_rec(after):
        sharing["rec"], token = _share_start("share_rec_start", _reduce_end("rec", sent["rec"], place, after), place)
        return token

    small_state, small_started = _allreduce_small_begin(spacked, place, rec_sent, sum_rec)

    shared = _share_wait("share_wait", *sharing["state"], small_started)
    delta, new_m, new_v = {}, {}, {}

    def adamw_of(k, g, dep=None):
        cols = w[k].shape[-1]
        d_, m_, v_ = _adamw(f"adamw_{k}", w[k].reshape(-1, cols), g.reshape(-1, cols),
                            m[k].reshape(-1, cols), v[k].reshape(-1, cols), dep)
        delta[k], new_m[k], new_v[k] = (a.reshape(w[k].shape) for a in (d_, m_, v_))

    g_big = {}
    for k, g in zip(_BIG[2:], shared):
        g_big[k] = g.reshape(w[k].shape)
        adamw_of(k, g_big[k])
    def behind_adamw(ks):
        return jnp.broadcast_to(sum(new_v[k][:1, :1, :1] for k in ks).reshape(1, 1), (8, 1))

    for k, g in zip(_BIG[:2], _share_wait("share_rec_wait", *sharing["rec"], behind_adamw(_BIG[4:]))):
        g_big[k] = g.reshape(w[k].shape)
        adamw_of(k, g_big[k])
    unpacked = _unpack(_allreduce_small_end(small_state, behind_adamw(_BIG)), soffs)
    ssum = dict(zip(small_names, unpacked[:-1]))
    loss = ssum["loss"].reshape(())
    dmod_rows = unpacked[-1].reshape(8, 2, 6 * D).transpose(1, 0, 2)

    d_cmod_full =jnp.concatenate([ssum["d_cmod"].reshape(1, 2 * D), jnp.zeros((1, 4 * D), F32)], axis=1)
    dm16 = jnp.concatenate([dmod_rows, jnp.stack([d_cmod_full, jnp.zeros((1, 6 * D), F32)]),
                            jnp.zeros((2, 7, 6 * D), F32)], axis=1)
    dm16_shard = lax.dynamic_slice_in_dim(dm16, chip * ns, ns, axis=2)
    g_w_ada, ds_part = _ada_bwd(c16, dm16_shard, w_ada)
    ds_state, ds_sent = _send_start("dsilu_start", [jnp.broadcast_to(ds_part[8:16], (4, 8, D))], _to_chips,
                                    [(3, 8, D)], place)
    adamw_of("w_ada", g_w_ada, ds_sent)
    (ds_own,), (ds_landed,) = _send_wait("dsilu_wait", *ds_state, _to_chips, new_v["w_ada"])
    ds_flip = jnp.concatenate([ds_own[:1], ds_landed[1:2], ds_landed[0:1], ds_landed[2:3]], axis=0)
    g_c_ctx = _cctx_grad(jnp.take(ds_flip, jnp.arange(4) ^ chip, axis=0), c_ctx).reshape(D)
    g_b_ada = ssum["d_mod"] + jnp.stack([d_cmod_full[0], jnp.zeros((6 * D,), F32)])

    def shard_of(a, axis):
        n = a.shape[axis] // 4
        return lax.dynamic_slice_in_dim(a, chip * n, n, axis=axis)

    grads = dict(
        c_ctx=g_c_ctx, w_ada=g_w_ada, b_ada=g_b_ada,
        norm_g=shard_of(ssum["norm_g"].reshape(2, 2, D), 2),
        rec_w_in=g_big["rec_w_in"], rec_conv_w=shard_of(ssum["rec_conv_w"].reshape(1, REC_KW, R), 2),
        rec_conv_b=ssum["rec_conv_b"].reshape(1, R), rec_lambda=shard_of(ssum["rec_lambda"].reshape(1, 2, R), 2),
        rec_w_a=ssum["rec_w_a"].reshape(rec_w_a.shape), rec_b_a=ssum["rec_b_a"].reshape(rec_b_a.shape),
        rec_w_x=ssum["rec_w_x"].reshape(rec_w_x.shape), rec_b_x=ssum["rec_b_x"].reshape(rec_b_x.shape),
        rec_w_out=g_big["rec_w_out"], conf_w_pw1=g_big["conf_w_pw1"],
        conf_b_pw1=shard_of(ssum["conf_b_pw1"].reshape(1, 2 * D), 1),
        conf_conv_w=shard_of(ssum["conf_conv_w"].reshape(1, CONF_KW, D), 2),
        conf_conv_b=shard_of(ssum["conf_conv_b"].reshape(1, D), 1),
        conf_ln_g=shard_of(ssum["conf_ln_g"].reshape(1, D), 1), conf_ln_b=shard_of(ssum["conf_ln_b"].reshape(1, D), 1),
        conf_w_pw2=g_big["conf_w_pw2"], conf_b_pw2=shard_of(ssum["conf_b_pw2"].reshape(1, D), 1),
        mlp_w_in=g_big["mlp_w_in"], mlp_w_out=g_big["mlp_w_out"], final_g=ssum["final_g"].reshape(D))

    rest =[k for k in names if k not in ("w_ada",) + _BIG]
    d_, m_, v_ = _adamw_many("adamw_small", [w[k] for k in rest], [grads[k] for k in rest],
                             [m[k] for k in rest], [v[k] for k in rest])
    for k, dd, mm, vv in zip(rest, d_, m_, v_):
        delta[k], new_m[k], new_v[k] = dd, mm, vv

    return (loss, grad_x[None], *[grads[k] for k in names], *[delta[k] for k in names],
            *[new_m[k] for k in names], *[new_v[k] for k in names])
```

```python
import functools
import math

import jax
import jax.numpy as jnp
from jax import lax
from jax.experimental import pallas as pl
from jax.experimental.pallas import tpu as pltpu

F32 = jnp.float32
BF16 = jnp.bfloat16

D = 1024
T = 2048
TC = 256
TA = T + TC
R = 1280
RH = R // 2
NQ = 4 * RH
FF = 4096
N_BLK = 16
BLK = R // N_BLK
GRID_W = 64
EPS = 1e-6
RG_C = 8.0
CONF_KW = 31
REC_KW = 4
LANE = 128
ROW_TILE = 256
HALO = 16
RG_TILE = 128
PACK_ROWS = 512
MM_TILE = 1024
REC_TILE = TA // 2
CW_REC = 640
CW_CONF = 512
V7X_VMEM_BYTES = 64 * 1024 * 1024
VMEM_LIMIT = V7X_VMEM_BYTES - 8 * 1024 * 1024

ADAM_LR = 0.001
ADAM_B1 = 0.9
ADAM_B2 = 0.999
ADAM_EPS = 1e-08
ADAM_WD = 0.01
ADAM_STEP = 10

MESH = pl.DeviceIdType.MESH
ANY = pl.BlockSpec(memory_space=pl.ANY)


def _sds(shape, dtype):
    return jax.ShapeDtypeStruct(tuple(shape), dtype)


def _pcall(body, **kw):
    return pl.pallas_call(body, **kw)


def _cparams():
    return pltpu.CompilerParams(vmem_limit_bytes=VMEM_LIMIT)


def _full_spec(arr):
    nd = arr.ndim
    return pl.BlockSpec(arr.shape, lambda *ids, _n=nd: (0,) * _n)


def _sum0(v):
    return jnp.sum(v, axis=0, keepdims=True)


def _tiled(name, fn, grid, ins, vecs, outs, vec_outs=(), vec_refs=False):
    n_in, n_vec, n_out = len(ins), len(vecs), len(outs)
    n_grid = len(grid)

    def kern(*refs):
        ids = [pl.program_id(a) for a in range(n_grid)]
        tin = [r[...] for r in refs[:n_in]]
        vin = list(refs[n_in:n_in + n_vec]) if vec_refs else [r[...] for r in refs[n_in:n_in + n_vec]]
        o_refs = refs[n_in + n_vec:n_in + n_vec + n_out]
        a_refs = refs[n_in + n_vec + n_out:]
        tout, incs = fn(ids, tin, vin)
        for r, v in zip(o_refs, tout):
            r[...] = v.astype(r.dtype)
        if a_refs:
            first = functools.reduce(jnp.logical_and, [i == 0 for i in ids])

            @pl.when(first)
            def _():
                for r in a_refs:
                    r[...] = jnp.zeros_like(r)

            for r, v in zip(a_refs, incs):
                r[...] += v

    out_shape = [o for o, _ in outs] + [_sds(s, F32) for s in vec_outs]
    out_specs = [s for _, s in outs] + [
        pl.BlockSpec(tuple(s), lambda *ids, _n=len(s): (0,) * _n) for s in vec_outs]
    res = _pcall(
        kern, name=name, grid=tuple(grid),
        in_specs=[s for _, s in ins] + [_full_spec(v) for v in vecs],
        out_specs=out_specs, out_shape=out_shape, compiler_params=_cparams(),
    )(*[a for a, _ in ins], *vecs)
    return list(res)


def _rows(arr, ncols=None, tm=ROW_TILE, off=0, col=0, clamp_lo=False):
    ncols = arr.shape[1] if ncols is None else ncols
    if clamp_lo:
        return arr, pl.BlockSpec((tm, ncols), lambda i: (jnp.maximum(i + off, 0), col))
    return arr, pl.BlockSpec((tm, ncols), lambda i: (i + off, col))


def _orow(nrows, ncols, dtype, tm=ROW_TILE, off=0, clamp_lo=False):
    if clamp_lo:
        return _sds((nrows, ncols), dtype), pl.BlockSpec((tm, ncols), lambda i: (jnp.maximum(i + off, 0), 0))
    return _sds((nrows, ncols), dtype), pl.BlockSpec((tm, ncols), lambda i: (i + off, 0))


_NN = (((1,), (0,)), ((), ()))
_TN = (((0,), (0,)), ((), ()))
_NT = (((1,), (1,)), ((), ()))


def _mm(name, a, b, dims, grid, a_spec, b_spec, out, acc_shape, extra=(), a_pre=None, epi=None):
    n_k = grid[2]
    n_ex = len(extra)

    def kern(a_ref, b_ref, *rest):
        ex = rest[:n_ex]
        o_refs = rest[n_ex:n_ex + len(out)]
        k = pl.program_id(2)
        av = a_ref[...]
        if a_pre is not None:
            av = a_pre(av)
        part = lax.dot_general(av.astype(BF16), b_ref[...].astype(BF16), dims, preferred_element_type=F32)

        def finish(total):
            vals = [total] if epi is None else epi(total, [e[...] for e in ex])
            for r, v in zip(o_refs, vals):
                r[...] = v.astype(r.dtype)

        if n_k == 1:
            finish(part)
        else:
            acc = rest[-1]

            @pl.when(k == 0)
            def _():
                acc[...] = part

            @pl.when(jnp.logical_and(k > 0, k < n_k - 1))
            def _():
                acc[...] += part

            @pl.when(k == n_k - 1)
            def _():
                finish(acc[...] + part)

    res = _pcall(
        kern, name=name, grid=tuple(grid),
        in_specs=[a_spec, b_spec] + [s for _, s in extra],
        out_specs=[s for _, s in out], out_shape=[o for o, _ in out],
        scratch_shapes=[] if n_k == 1 else [pltpu.VMEM(tuple(acc_shape), F32)], compiler_params=_cparams(),
    )(a, b, *[e for e, _ in extra])
    return list(res)


def _rms(x):
    r = lax.rsqrt(jnp.mean(x * x, axis=-1, keepdims=True) + EPS)
    return x * r, r


def _norm_mod(x, g, sc, sh):
    n, _ = _rms(x)
    return (n * g) * (1.0 + sc) + sh


def _norm_mod_bwd(dh, x, g, sc):
    n, r = _rms(x)
    d_sh = _sum0(dh)
    d_sc = _sum0(dh * (n * g))
    d_g = _sum0(dh * (1.0 + sc) * n)
    dn = dh * (g * (1.0 + sc))
    dx = r * (dn - n * jnp.mean(dn * n, axis=-1, keepdims=True))
    return dx, d_sh, d_sc, d_g


_GELU_K = math.sqrt(2.0 / math.pi)


def _gelu(x):
    t = jnp.tanh(_GELU_K * (x + 0.044715 * x * x * x))
    return 0.5 * x * (1.0 + t), t


def _gelu_grad(x, t):
    return 0.5 * (1.0 + t) + 0.5 * x * (1.0 - t * t) * (_GELU_K * (1.0 + 3.0 * 0.044715 * x * x))


def _sigmoid(x):
    return 0.5 * jnp.tanh(0.5 * x) + 0.5


def _expm1(x):
    p = jnp.full_like(x, 1.0 / 5040.0)
    for c in (1.0 / 720.0, 1.0 / 120.0, 1.0 / 24.0, 1.0 / 6.0, 0.5, 1.0):
        p = p * x + c
    return jnp.where(jnp.abs(x) < 0.3, x * p, jnp.exp(x) - 1.0)


def _softplus_neg(lam):
    return jnp.log1p(jnp.exp(-jnp.abs(lam))) + jnp.maximum(-lam, 0.0)


def _layernorm_parts(x):
    mu = jnp.mean(x, axis=-1, keepdims=True)
    xc = x - mu
    rstd = lax.rsqrt(jnp.mean(xc * xc, axis=-1, keepdims=True) + EPS)
    return xc * rstd, rstd


def _rg_gates(u, wbd, gbias, lam):
    sp = _softplus_neg(lam)
    parts = {}
    for h in range(2):
        uh = u[:, h * RH:(h + 1) * RH]
        g = jnp.dot(uh.astype(BF16), wbd[h], preferred_element_type=F32) + gbias[:, h * NQ:(h + 1) * NQ]
        for d in range(2):
            r = _sigmoid(g[:, (2 * d) * RH:(2 * d + 1) * RH])
            i = _sigmoid(g[:, (2 * d + 1) * RH:(2 * d + 2) * RH])
            sph = sp[d:d + 1, h * RH:(h + 1) * RH]
            la = (-RG_C) * r * sph
            e2 = _expm1(2.0 * la)
            inv_mult = jnp.where(e2 < 0.0, lax.rsqrt(-e2), 0.0)
            parts[(d, h)] = dict(r=r, i=i, la=la, a=jnp.exp(la), e2=e2, mult=-e2 * inv_mult, inv_mult=inv_mult,
                                 uh=uh, sp=sph)
    return parts


def _rg_fwd_fn(ids, tin, vin):
    (u,) = tin
    wbd = vin[0]
    parts = _rg_gates(u, wbd, vin[1][...], vin[2][...])
    outs = []
    for d in range(2):
        a = jnp.concatenate([parts[(d, h)]["a"] for h in range(2)], axis=1)
        b = jnp.concatenate([parts[(d, h)]["mult"] * parts[(d, h)]["i"] * parts[(d, h)]["uh"]
                             for h in range(2)], axis=1)
        outs += [a, b]
    return outs, []


def _rg_bwd_fn(ids, tin, vin):
    u, da_f, db_f, da_r, db_r = tin
    wbd, lam = vin[0], vin[2][...]
    parts = _rg_gates(u, wbd, vin[1][...], lam)
    dab = ((da_f, db_f), (da_r, db_r))
    dsig_lam = -1.0 / (1.0 + jnp.exp(lam))
    du_halves, dpre_halves, dlam = [], [], [[None, None], [None, None]]
    for h in range(2):
        du = jnp.zeros_like(parts[(0, h)]["uh"])
        dpre = []
        for d in range(2):
            p = parts[(d, h)]
            da = dab[d][0][:, h * RH:(h + 1) * RH]
            db = dab[d][1][:, h * RH:(h + 1) * RH]
            d_mult = db * p["i"] * p["uh"]
            d_i = db * p["mult"] * p["uh"]
            du = du + db * p["mult"] * p["i"]
            d_la = da * p["a"] - d_mult * (p["e2"] + 1.0) * p["inv_mult"]
            d_r = d_la * ((-RG_C) * p["sp"])
            dlam[d][h] = _sum0(d_la * ((-RG_C) * p["r"])) * dsig_lam[d:d + 1, h * RH:(h + 1) * RH]
            dpre += [d_r * p["r"] * (1.0 - p["r"]), d_i * p["i"] * (1.0 - p["i"])]
        dpre = jnp.concatenate(dpre, axis=1)
        du = du + lax.dot_general(dpre.astype(BF16), wbd[h], _NT, preferred_element_type=F32)
        du_halves.append(du)
        dpre_halves.append(dpre)
    dpre_all = jnp.concatenate(dpre_halves, axis=1)
    dlam_row = jnp.concatenate([dlam[0][0], dlam[0][1], dlam[1][0], dlam[1][1]], axis=1)
    return [dpre_all, jnp.concatenate(du_halves, axis=1)], [_sum0(dpre_all), dlam_row]


def _tile_flags(i, n_tiles, seq_starts):
    starts_here = functools.reduce(jnp.logical_or, [i == s for s in seq_starts])
    ends_here = functools.reduce(jnp.logical_or, [i + 1 == s for s in seq_starts] + [i + 1 == n_tiles])
    return jnp.logical_not(starts_here), jnp.logical_not(ends_here)


def _halo_specs(col0, cw):
    hb = ROW_TILE // HALO
    prev = pl.BlockSpec((HALO, cw), lambda i, c: (jnp.maximum(i * hb - 1, 0), col0 + c))
    cur = pl.BlockSpec((ROW_TILE, cw), lambda i, c: (i, col0 + c))
    return prev, cur, hb


def _window(prev_ref, cur_ref, next_ref, has_prev, has_next):
    prev = jnp.where(has_prev, prev_ref[...], 0.0)
    nxt = jnp.where(has_next, next_ref[...], 0.0)
    return jnp.concatenate([prev, cur_ref[...], nxt], axis=0)


def _tap_reader(win):
    sub = 8
    n = win.shape[0]
    shifted = {0: win}

    def tap(off):
        s = off % sub
        if s not in shifted:
            shifted[s] = pltpu.roll(win, n - s, axis=0)
        return shifted[s][off - s:off - s + ROW_TILE, :]

    return tap


def _dwconv(name, x, col0, w, bias, pad_left, seq_starts, n_ch, cw=256, flip=False, into=None):
    n_rows = x.shape[0]
    n_tiles = n_rows // ROW_TILE
    n_taps = w.shape[0]
    prev_spec, cur_spec, hb = _halo_specs(col0, cw)
    last_hb = n_rows // HALO - 1
    next_spec = pl.BlockSpec((HALO, cw), lambda i, c: (jnp.minimum((i + 1) * hb, last_hb), col0 + c))
    dest, out_col0 = (None, 0) if into is None else into

    def kern(prev_ref, cur_ref, next_ref, w_ref, b_ref, *rest):
        o_ref = rest[-1]
        has_prev, has_next = _tile_flags(pl.program_id(0), n_tiles, seq_starts)
        win = _window(prev_ref, cur_ref, next_ref, has_prev, has_next)
        tap = _tap_reader(win)
        wv = w_ref[...]
        acc = jnp.zeros((ROW_TILE, cw), F32) + b_ref[...]
        for k in range(n_taps):
            kw = n_taps - 1 - k if flip else k
            acc = acc + wv[kw:kw + 1, :] * tap(HALO + k - pad_left)
        o_ref[...] = acc.astype(o_ref.dtype)

    return _pcall(
        kern, name=name, grid=(n_tiles, n_ch // cw),
        in_specs=[prev_spec, cur_spec, next_spec,
                  pl.BlockSpec((n_taps, cw), lambda i, c: (0, c)), pl.BlockSpec((1, cw), lambda i, c: (0, c))]
        + ([] if dest is None else [ANY]),
        out_specs=pl.BlockSpec((ROW_TILE, cw), lambda i, c: (i, out_col0 + c)),
        out_shape=_sds((n_rows, n_ch), F32) if dest is None else _sds(dest.shape, dest.dtype),
        input_output_aliases={} if dest is None else {5: 0}, compiler_params=_cparams(),
    )(x, x, x, w, bias, *([] if dest is None else [dest]))


def _dwconv_wgrad(name, dy, x, col0, n_taps, pad_left, seq_starts, n_ch, cw=256, dep=None):
    deps = [] if dep is None else [dep]
    n_rows = dy.shape[0]
    n_tiles = n_rows // ROW_TILE
    n_out = -(-(n_taps + 1) // 8) * 8
    prev_spec, cur_spec, hb = _halo_specs(col0, cw)
    last_hb = n_rows // HALO - 1
    next_spec = pl.BlockSpec((HALO, cw), lambda c, i: (jnp.minimum((i + 1) * hb, last_hb), col0 + c))
    prev_spec = pl.BlockSpec((HALO, cw), lambda c, i: (jnp.maximum(i * hb - 1, 0), col0 + c))
    cur_spec = pl.BlockSpec((ROW_TILE, cw), lambda c, i: (i, col0 + c))

    def kern(dy_ref, prev_ref, cur_ref, next_ref, *rest):
        o_ref = rest[-1]
        i = pl.program_id(1)
        has_prev, has_next = _tile_flags(i, n_tiles, seq_starts)
        win = _window(prev_ref, cur_ref, next_ref, has_prev, has_next)
        dyv = dy_ref[...]
        tap = _tap_reader(win)
        rid = lax.broadcasted_iota(jnp.int32, (n_out, cw), 0)
        inc = jnp.where(rid == n_taps, _sum0(dyv), 0.0)
        for k in range(n_taps):
            inc = inc + jnp.where(rid == k, _sum0(dyv * tap(HALO + k - pad_left)), 0.0)

        @pl.when(i == 0)
        def _():
            o_ref[...] = jnp.zeros_like(o_ref)

        o_ref[...] += inc

    return _pcall(
        kern, name=name, grid=(n_ch // cw, n_tiles),
        in_specs=[pl.BlockSpec((ROW_TILE, cw), lambda c, i: (i, c)), prev_spec, cur_spec, next_spec]
        + [pl.BlockSpec(d.shape, lambda c, i: (0, 0)) for d in deps],
        out_specs=pl.BlockSpec((n_out, cw), lambda c, i: (0, c)),
        out_shape=_sds((n_out, n_ch), F32), compiler_params=_cparams(),
    )(dy, x, x, x, *deps)


N_SCAN = TA // ROW_TILE


def _rev_block(j):
    return jnp.where(j == 0, 0, N_SCAN - j)


def _scan_fwd(a_f, b_f, a_r, b_r):
    fwd_spec = pl.BlockSpec((ROW_TILE, R), lambda i: (i, 0))
    rev_spec = pl.BlockSpec((ROW_TILE, R), lambda i: (_rev_block(i), 0))
    hin_spec = pl.BlockSpec((None, 1, R), lambda i: (i, 0, 0))

    def kern(af, bf, ar, br, yf, yr, hin_f, hin_r, hf_s, hr_s):
        @pl.when(pl.program_id(0) == 0)
        def _():
            hf_s[...] = jnp.zeros_like(hf_s)
            hr_s[...] = jnp.zeros_like(hr_s)

        hin_f[...] = hf_s[...]
        hin_r[...] = hr_s[...]

        def step(s8, carry):
            hf, hr = carry
            t0 = pl.multiple_of(s8 * 8, 8)
            for q in range(8):
                tf = t0 + q
                hf = af[pl.ds(tf, 1), :] * hf + bf[pl.ds(tf, 1), :]
                yf[pl.ds(tf, 1), :] = hf
                tr = ROW_TILE - 1 - tf
                hr = ar[pl.ds(tr, 1), :] * hr + br[pl.ds(tr, 1), :]
                yr[pl.ds(tr, 1), :] = hr
            return hf, hr

        hf, hr = lax.fori_loop(0, ROW_TILE // 8, step, (hf_s[...], hr_s[...]))
        hf_s[...] = hf
        hr_s[...] = hr

    return _pcall(
        kern, name="scan_fwd", grid=(N_SCAN,),
        in_specs=[fwd_spec, fwd_spec, rev_spec, rev_spec],
        out_specs=[fwd_spec, rev_spec, hin_spec, hin_spec],
        out_shape=[_sds((TA, R), F32), _sds((TA, R), F32), _sds((N_SCAN, 1, R), F32), _sds((N_SCAN, 1, R), F32)],
        scratch_shapes=[pltpu.VMEM((1, R), F32), pltpu.VMEM((1, R), F32)], compiler_params=_cparams(),
    )(a_f, b_f, a_r, b_r)


def _scan_bwd(dy, a_f, y_f, hin_f, a_r, y_r, hin_r):
    fwd_spec = pl.BlockSpec((ROW_TILE, R), lambda i: (N_SCAN - 1 - i, 0))
    rev_spec = pl.BlockSpec((ROW_TILE, R), lambda i: (_rev_block(N_SCAN - 1 - i), 0))
    hin_spec = pl.BlockSpec((None, 1, R), lambda i: (N_SCAN - 1 - i, 0, 0))
    last = ROW_TILE - 1

    def kern(dyf, af, yf, hf0, dyr, ar, yr, hr0, daf, dbf, dar, dbr, gf_s, anf_s, gr_s, anr_s):
        @pl.when(pl.program_id(0) == 0)
        def _():
            for r in (gf_s, anf_s, gr_s, anr_s):
                r[...] = jnp.zeros_like(r)

        def one(dy_ref, a_ref, y_ref, da_ref, db_ref, g, an, p, pprev):
            gnew = dy_ref[pl.ds(p, 1), :] + an * g
            db_ref[pl.ds(p, 1), :] = gnew
            da_ref[pl.ds(p, 1), :] = gnew * y_ref[pl.ds(pprev, 1), :]
            return gnew, a_ref[pl.ds(p, 1), :]

        def step(s8, carry):
            gf, anf, gr, anr = carry
            base = s8 * 8
            for q in range(8):
                s = last - (base + q)
                gf, anf = one(dyf, af, yf, daf, dbf, gf, anf, s, s - 1)
                gr, anr = one(dyr, ar, yr, dar, dbr, gr, anr, last - s, last - s + 1)
            return gf, anf, gr, anr

        carry = (gf_s[...], anf_s[...], gr_s[...], anr_s[...])
        carry = lax.fori_loop(0, ROW_TILE // 8 - 1, step, carry)
        gf, anf, gr, anr = carry
        for s in range(7, 0, -1):
            gf, anf = one(dyf, af, yf, daf, dbf, gf, anf, s, s - 1)
            gr, anr = one(dyr, ar, yr, dar, dbr, gr, anr, last - s, last - s + 1)
        gf0 = dyf[0:1, :] + anf * gf
        dbf[0:1, :] = gf0
        daf[0:1, :] = gf0 * hf0[...]
        gr0 = dyr[last:last + 1, :] + anr * gr
        dbr[last:last + 1, :] = gr0
        dar[last:last + 1, :] = gr0 * hr0[...]
        gf_s[...] = gf0
        anf_s[...] = af[0:1, :]
        gr_s[...] = gr0
        anr_s[...] = ar[last:last + 1, :]

    return _pcall(
        kern, name="scan_bwd", grid=(N_SCAN,),
        in_specs=[fwd_spec, fwd_spec, fwd_spec, hin_spec, rev_spec, rev_spec, rev_spec, hin_spec],
        out_specs=[fwd_spec, fwd_spec, rev_spec, rev_spec],
        out_shape=[_sds((TA, R), F32)] * 4,
        scratch_shapes=[pltpu.VMEM((1, R), F32)] * 4, compiler_params=_cparams(),
    )(dy, a_f, y_f, hin_f, dy, a_r, y_r, hin_r)


def _me():
    return lax.axis_index("x"), lax.axis_index("y"), lax.axis_index("c")


def _other_chips(mx, my):
    return [(1 - mx, my), (mx, 1 - my), (1 - mx, 1 - my)]


def _rcopy(src, dst, ssem, rsem, dev):
    return pltpu.make_async_remote_copy(src_ref=src, dst_ref=dst, send_sem=ssem, recv_sem=rsem,
                                        device_id=dev, device_id_type=MESH)


def _peers7(mx, my, mc):
    peers = []
    for k in range(1, 8):
        peers.append((1 - mx if (k >> 2) & 1 else mx, 1 - my if (k >> 1) & 1 else my, 1 - mc if k & 1 else mc))
    return peers


def _halves_of(refs, c):
    out = []
    for r in refs:
        out += [r.at[c]] if len(r.shape) == 3 else [r.at[l, c] for l in range(r.shape[0])]
    return out


def _share_start(name, fulls, after):
    n = len(fulls)
    n_cp = sum(1 if f.ndim == 3 else f.shape[0] for f in fulls)

    def kern(*refs):
        o = refs[n + 1:2 * n + 1]
        ssem, rsem, token = refs[2 * n + 1:]
        mx, my, mc = _me()
        for q, half in enumerate(_halves_of(o, mc)):
            _rcopy(half, half, ssem.at[q], rsem.at[q], (mx, my, 1 - mc)).start()
        token[...] = jnp.zeros_like(token)

    dma = pltpu.SemaphoreType.DMA
    res = _pcall(
        kern, name=name, in_specs=[ANY] * (n + 1),
        out_specs=[ANY] * n + [SEM, SEM, pl.BlockSpec(memory_space=pltpu.VMEM)],
        out_shape=[_sds(f.shape, f.dtype) for f in fulls] + [dma((n_cp,)), dma((n_cp,)), _sds((8, LANE), F32)],
        input_output_aliases={t: t for t in range(n)},
        compiler_params=pltpu.CompilerParams(has_side_effects=_DATAFLOW),
    )(*fulls, after)
    return (list(res[:n]), res[n], res[n + 1]), res[n + 2]


def _share_wait(name, fulls, ssem, rsem, after):
    n = len(fulls)

    def kern(*refs):
        o = refs[:n]
        ssem_ref, rsem_ref = refs[n], refs[n + 1]
        mx, my, mc = _me()
        sib = (mx, my, 1 - mc)
        for q, (theirs, mine) in enumerate(zip(_halves_of(o, 1 - mc), _halves_of(o, mc))):
            _rcopy(theirs, theirs, ssem_ref.at[q], rsem_ref.at[q], sib).wait_recv()
            _rcopy(mine, mine, ssem_ref.at[q], rsem_ref.at[q], sib).wait_send()

    return list(_pcall(
        kern, name=name, in_specs=[ANY] * n + [SEM, SEM, ANY], out_specs=[ANY] * n,
        out_shape=[_sds(f.shape, f.dtype) for f in fulls], input_output_aliases={t: t for t in range(n)},
        compiler_params=pltpu.CompilerParams(has_side_effects=_DATAFLOW),
    )(*fulls, ssem, rsem, after))


def _tiled_sp(name, fn, grid, sp, ins, outs, into=None):
    n_in = len(ins)
    dest = [] if into is None else [into]

    def kern(sp_ref, *refs):
        tout = fn([r[...] for r in refs[:n_in]])
        for r, v in zip(refs[n_in + len(dest):], tout):
            r[...] = v.astype(r.dtype)

    gs = pltpu.PrefetchScalarGridSpec(num_scalar_prefetch=1, grid=tuple(grid),
                                      in_specs=[s for _, s in ins] + [ANY] * len(dest), out_specs=[s for _, s in outs])
    res = _pcall(kern, name=name, grid_spec=gs, out_shape=[o for o, _ in outs], compiler_params=_cparams(),
                 input_output_aliases={1 + n_in: 0} if dest else {})(sp, *[a for a, _ in ins], *dest)
    return list(res)


def _row_tile(rows, cols, itemsize=4, budget=2 * 1024 * 1024):
    tr = rows
    while tr * cols * itemsize > budget and tr % 32 == 0:
        tr //= 2
    return tr


def _place_big(shards, place, dep=None):
    slots = []
    for tag, s, layer in shards:
        rr, cc = s.shape[2], s.shape[3]
        tr = _row_tile(rr, cc)
        (slot,) = _tiled_sp(
            f"place_{tag}", lambda tin: [tin[0]], (2, rr // tr), place,
            [(s, pl.BlockSpec((None, None, tr, cc), lambda h, i, sp, layer=layer: (layer, h, i, 0)))]
            + [(d, pl.BlockSpec(d.shape, lambda h, i, sp: (0, 0))) for d in _behind(dep)],
            [(_sds((4, 2, rr, cc), BF16), pl.BlockSpec((None, None, tr, cc), lambda h, i, sp: (sp[0], h, i, 0)))])
        slots.append(slot)
    return slots


def _allreduce_small_begin(vec, place, after, during):
    hr = vec.shape[0] // 2
    tr = _row_tile(hr, LANE)
    blk = (None, None, tr, LANE)
    (pair,) = _tiled_sp(
        "small_place", lambda tin: [tin[0]], (2, hr // tr), place,
        [(vec.reshape(2, hr, LANE), pl.BlockSpec((None, tr, LANE), lambda h, i, sp: (h, i, 0)))],
        [(_sds((2, 2, hr, LANE), F32), pl.BlockSpec(blk, lambda h, i, sp: (sp[1], h, i, 0)))])
    crossing, token = _share_start("small_share_start", [pair.reshape(2, 2 * hr, LANE)], place)
    (pair,) = _share_wait("small_share_wait", *crossing, during(token))
    pair = pair.reshape(2, 2, hr, LANE)
    (slot,) = _tiled_sp(
        "small_pair_add", lambda tin: [tin[0] + tin[1]], (2, hr // tr), place,
        [(pair, pl.BlockSpec(blk, lambda h, i, sp: (0, h, i, 0))),
         (pair, pl.BlockSpec(blk, lambda h, i, sp: (1, h, i, 0)))],
        [(_sds((4, 2, hr, LANE), F32), pl.BlockSpec(blk, lambda h, i, sp: (sp[0], h, i, 0)))])
    fly, sems, token = _gather_start("small_start", [slot], ((0,),), after)
    return (fly, sems), token


def _allreduce_small_end(state, after):
    fly, sems = state
    (chips,) = _swap_halves("small_swap", _gather_wait("small_wait", fly, *sems, after))
    hr = chips.shape[2]
    tr = _row_tile(hr, LANE)
    blk = (None, None, tr, LANE)
    (total,) = _tiled(
        "small_chip_sum", lambda ids, tin, vin: ([((tin[0] + tin[1]) + tin[2]) + tin[3]], []), (2, hr // tr),
        [(chips, pl.BlockSpec(blk, lambda h, i, _j=j: (_j, h, i, 0))) for j in range(4)], [],
        [(_sds((2, hr, LANE), F32), pl.BlockSpec((None, tr, LANE), lambda h, i: (h, i, 0)))])
    return total.reshape(2 * hr, LANE)


SEM =pl.BlockSpec(memory_space=pltpu.SEMAPHORE)
_DATAFLOW = pltpu.SideEffectType.DATAFLOW_SIDE_EFFECTING


def _gather_start(name, slots, groups, after):
    n = len(slots)

    def kern(*refs):
        o = refs[n + 1:2 * n + 1]
        sems, token = refs[2 * n + 1:-1], refs[-1]
        mx, my, mc = _me()
        j0 = 2 * mx + my
        for gi, grp in enumerate(groups):
            for k, t in enumerate(grp):
                for q, (qx, qy) in enumerate(_other_chips(mx, my)):
                    _rcopy(o[t].at[j0, mc], o[t].at[j0, mc], sems[2 * gi].at[3 * k + q],
                           sems[2 * gi + 1].at[3 * k + q], (qx, qy, mc)).start()
        token[...] = jnp.zeros_like(token)

    sem_shapes = []
    for grp in groups:
        sem_shapes += [pltpu.SemaphoreType.DMA((3 * len(grp),))] * 2
    res = _pcall(
        kern, name=name, in_specs=[ANY] * (n + 1),
        out_specs=[ANY] * n + [SEM] * len(sem_shapes) + [pl.BlockSpec(memory_space=pltpu.VMEM)],
        out_shape=[_sds(w.shape, w.dtype) for w in slots] + sem_shapes + [_sds((8, LANE), F32)],
        input_output_aliases={t: t for t in range(n)},
        compiler_params=pltpu.CompilerParams(has_side_effects=_DATAFLOW),
    )(*slots, after)
    return list(res[:n]), list(res[n:-1]), res[-1]


def _gather_wait(name, bufs, ssem, rsem, after):
    n = len(bufs)
    afters = list(after) if isinstance(after, (list, tuple)) else [after]

    def kern(*refs):
        b = refs[:n]
        ssem_ref, rsem_ref = refs[n], refs[n + 1]
        mx, my, mc = _me()
        j0 = 2 * mx + my
        for k in range(n):
            for q, (qx, qy) in enumerate(_other_chips(mx, my)):
                jq = 2 * qx + qy
                _rcopy(b[k].at[jq, mc], b[k].at[jq, mc], ssem_ref.at[3 * k + q], rsem_ref.at[3 * k + q],
                       (qx, qy, mc)).wait_recv()
                _rcopy(b[k].at[j0, mc], b[k].at[j0, mc], ssem_ref.at[3 * k + q], rsem_ref.at[3 * k + q],
                       (qx, qy, mc)).wait_send()

    return list(_pcall(
        kern, name=name, in_specs=[ANY] * n + [SEM, SEM] + [ANY] * len(afters), out_specs=[ANY] * n,
        out_shape=[_sds(w.shape, w.dtype) for w in bufs], input_output_aliases={k: k for k in range(n)},
        compiler_params=pltpu.CompilerParams(has_side_effects=_DATAFLOW),
    )(*bufs, ssem, rsem, *afters))


def _swap_halves(name, bufs):
    n = len(bufs)

    def kern(*refs):
        o = refs[n:2 * n]
        ss, rs = refs[2 * n:]
        mx, my, mc = _me()
        sib = (mx, my, 1 - mc)
        sends = []
        for k in range(n):
            for q, (qx, qy) in enumerate(_other_chips(mx, my)):
                jq = 2 * qx + qy
                cp = _rcopy(o[k].at[jq, mc], o[k].at[jq, mc], ss.at[3 * k + q], rs.at[3 * k + q], sib)
                cp.start()
                sends.append(cp)
        for k in range(n):
            for q, (qx, qy) in enumerate(_other_chips(mx, my)):
                jq = 2 * qx + qy
                _rcopy(o[k].at[jq, 1 - mc], o[k].at[jq, 1 - mc], ss.at[3 * k + q], rs.at[3 * k + q], sib).wait_recv()
        for cp in sends:
            cp.wait_send()

    dma = pltpu.SemaphoreType.DMA
    return list(_pcall(
        kern, name=name, in_specs=[ANY] * n, out_specs=[ANY] * n,
        out_shape=[_sds(w.shape, w.dtype) for w in bufs], input_output_aliases={k: k for k in range(n)},
        scratch_shapes=[dma((3 * n,)), dma((3 * n,))],
    )(*bufs))


def _swap_start(name, bufs, after):
    n = len(bufs)

    def kern(*refs):
        o = refs[n + 1:2 * n + 1]
        ssem, rsem, token = refs[2 * n + 1:]
        mx, my, mc = _me()
        for k in range(n):
            for q, (qx, qy) in enumerate(_other_chips(mx, my)):
                jq = 2 * qx + qy
                _rcopy(o[k].at[jq, mc], o[k].at[jq, mc], ssem.at[3 * k + q], rsem.at[3 * k + q], (mx, my, 1 - mc)).start()
        token[...] = jnp.zeros_like(token)

    dma = pltpu.SemaphoreType.DMA
    res = _pcall(
        kern, name=name, in_specs=[ANY] * (n + 1),
        out_specs=[ANY] * n + [SEM, SEM, pl.BlockSpec(memory_space=pltpu.VMEM)],
        out_shape=[_sds(w.shape, w.dtype) for w in bufs] + [dma((3 * n,)), dma((3 * n,)), _sds((8, LANE), F32)],
        input_output_aliases={k: k for k in range(n)},
        compiler_params=pltpu.CompilerParams(has_side_effects=_DATAFLOW),
    )(*bufs, after)
    return (list(res[:n]), res[n], res[n + 1]), res[n + 2]


def _swap_wait(name, bufs, ssem, rsem, after):
    n = len(bufs)

    def kern(*refs):
        b = refs[:n]
        ssem_ref, rsem_ref = refs[n], refs[n + 1]
        mx, my, mc = _me()
        sib = (mx, my, 1 - mc)
        for k in range(n):
            for q, (qx, qy) in enumerate(_other_chips(mx, my)):
                jq = 2 * qx + qy
                _rcopy(b[k].at[jq, 1 - mc], b[k].at[jq, 1 - mc], ssem_ref.at[3 * k + q], rsem_ref.at[3 * k + q],
                       sib).wait_recv()
                _rcopy(b[k].at[jq, mc], b[k].at[jq, mc], ssem_ref.at[3 * k + q], rsem_ref.at[3 * k + q],
                       sib).wait_send()

    return list(_pcall(
        kern, name=name, in_specs=[ANY] * n + [SEM, SEM, ANY], out_specs=[ANY] * n,
        out_shape=[_sds(w.shape, w.dtype) for w in bufs], input_output_aliases={k: k for k in range(n)},
        compiler_params=pltpu.CompilerParams(has_side_effects=_DATAFLOW),
    )(*bufs, ssem, rsem, after))


def _to_sibling(mx, my, mc):
    return [((j, 1 - mc), j, (mx, my, 1 - mc)) for j in range(4)]


def _to_chips(mx, my, mc):
    return [((2 * qx + qy,), q, (qx, qy, mc)) for q, (qx, qy) in enumerate(_other_chips(mx, my))]


def _to_all7(mx, my, mc):
    return [((0,), k, dev) for k, dev in enumerate(_peers7(mx, my, mc))]


def _send_start(name, srcs, plan, land_shapes, after):
    n = len(srcs)
    per = len(plan(0, 0, 0))

    def kern(*refs):
        s, land = refs[n + 1:2 * n + 1], refs[2 * n + 1:3 * n + 1]
        ssem, rsem, token = refs[3 * n + 1:]
        for k in range(n):
            for q, (idx, slot, dev) in enumerate(plan(*_me())):
                _rcopy(s[k].at[idx], land[k].at[slot], ssem.at[per * k + q], rsem.at[per * k + q], dev).start()
        token[...] = jnp.zeros_like(token)

    dma = pltpu.SemaphoreType.DMA
    res = _pcall(
        kern, name=name, in_specs=[ANY] * (n + 1),
        out_specs=[ANY] * (2 * n) + [SEM, SEM, pl.BlockSpec(memory_space=pltpu.VMEM)],
        out_shape=[_sds(s.shape, s.dtype) for s in srcs] + [_sds(ls, s.dtype) for ls, s in zip(land_shapes, srcs)]
        + [dma((per * n,)), dma((per * n,)), _sds((8, LANE), F32)],
        input_output_aliases={k: k for k in range(n)},
        compiler_params=pltpu.CompilerParams(has_side_effects=_DATAFLOW),
    )(*srcs, after)
    return (list(res[:n]), list(res[n:2 * n]), res[2 * n], res[2 * n + 1]), res[2 * n + 2]


def _send_wait(name, srcs, lands, ssem, rsem, plan, after):
    n = len(srcs)
    per = len(plan(0, 0, 0))

    def kern(*refs):
        s, land = refs[:n], refs[n:2 * n]
        ssem_ref, rsem_ref = refs[2 * n], refs[2 * n + 1]
        for k in range(n):
            for q, (idx, slot, dev) in enumerate(plan(*_me())):
                cp = _rcopy(s[k].at[idx], land[k].at[slot], ssem_ref.at[per * k + q], rsem_ref.at[per * k + q], dev)
                cp.wait_recv()
                cp.wait_send()

    res = _pcall(
        kern, name=name, in_specs=[ANY] * (2 * n) + [SEM, SEM, ANY], out_specs=[ANY] * (2 * n),
        out_shape=[_sds(a.shape, a.dtype) for a in list(srcs) + list(lands)],
        input_output_aliases={k: k for k in range(2 * n)},
        compiler_params=pltpu.CompilerParams(has_side_effects=_DATAFLOW),
    )(*srcs, *lands, ssem, rsem, after)
    return list(res[:n]), list(res[n:])


def _reduce_begin(tag, parts, after):
    return _send_start(f"pair_start_{tag}", parts, _to_sibling, [(4,) + p.shape[2:] for p in parts], after)


def _reduce_mid(tag, pairing, place, after):
    parts, theirs = _send_wait(f"pair_wait_{tag}", *pairing, _to_sibling, after)
    sums = []
    for k, (p, o) in enumerate(zip(parts, theirs)):
        rr, cc = p.shape[2], p.shape[3]
        tr = _row_tile(rr, cc)
        (s_k,) = _tiled_sp(
            f"pair_add_{tag}{k}", lambda tin: [tin[0].astype(F32) + tin[1].astype(F32)], (4, rr // tr), place,
            [(p, pl.BlockSpec((None, None, tr, cc), lambda j, i, sp: (j, sp[1], i, 0))),
             (o, pl.BlockSpec((None, tr, cc), lambda j, i, sp: (j, i, 0)))],
            [(_sds((4, rr, cc), BF16), pl.BlockSpec((None, tr, cc), lambda j, i, sp: (j, i, 0)))])
        sums.append(s_k)
    return _send_start(f"chips_start_{tag}", sums, _to_chips, [(3,) + s.shape[1:] for s in sums], theirs[0])


def _reduce_end(tag, flying, place, after, layer=None, into=None):
    sums, lands = _send_wait(f"chips_wait_{tag}", *flying, _to_chips, after)
    fulls = []
    for k, (s, q) in enumerate(zip(sums, lands)):
        rr, cc = q.shape[1], q.shape[2]
        tr = _row_tile(rr, cc)

        def add4(tin):
            return [((tin[0].astype(F32) + tin[1].astype(F32)) + tin[2].astype(F32)) + tin[3].astype(F32)]

        ins = [(s, pl.BlockSpec((None, tr, cc), lambda i, sp: (sp[0], i, 0)))]
        ins += [(q, pl.BlockSpec((None, tr, cc), lambda i, sp, _k=kk: (_k, i, 0))) for kk in range(3)]
        if layer is None:
            out = (_sds((2, rr, cc), F32), pl.BlockSpec((None, tr, cc), lambda i, sp: (sp[1], i, 0)))
        else:
            out = (_sds((layer[1], 2, rr, cc), F32),
                   pl.BlockSpec((None, None, tr, cc), lambda i, sp, _l=layer[0]: (_l, sp[1], i, 0)))
        (f_k,) = _tiled_sp(f"chip_add_{tag}{k}", add4, (rr // tr,), place, ins, [out],
                           None if into is None else into[k])
        fulls.append(f_k)
    return fulls


def _pack(parts, PACK_ROWS=PACK_ROWS):
    flat, offs, pos = [], [], 0
    for p in parts:
        v = p.reshape(-1).astype(F32)
        n = -(-v.shape[0] // LANE) * LANE
        flat.append(jnp.pad(v, (0, n - v.shape[0])))
        offs.append((pos, v.shape[0], p.shape))
        pos += n
    total = -(-pos // (PACK_ROWS * LANE)) * PACK_ROWS * LANE
    flat.append(jnp.zeros((total - pos,), F32))
    return jnp.concatenate(flat).reshape(-1, LANE), offs


def _unpack(vec, offs):
    v = vec.reshape(-1)
    return [v[p:p + n].reshape(shape) for p, n, shape in offs]


def _adamw_math(wv, gv, mv, vv):
    bc1 = 1.0 - ADAM_B1 ** ADAM_STEP
    bc2 = 1.0 - ADAM_B2 ** ADAM_STEP
    mn = ADAM_B1 * mv + (1.0 - ADAM_B1) * gv
    vn = ADAM_B2 * vv + (1.0 - ADAM_B2) * (gv * gv)
    delta = -ADAM_LR * ((mn / bc1) / (jnp.sqrt(vn / bc2) + ADAM_EPS) + ADAM_WD * wv)
    return delta, mn, vn


def _adamw(name, w, g, m, v, dep=None):
    rows, cols = w.shape
    tr = rows
    for cand in (512, 256, 128, 64, 32, 16, 8):
        if rows % cand == 0 and cand * cols * 4 <= 2 * 1024 * 1024:
            tr = cand
            break

    def fn(ids, tin, vin):
        return list(_adamw_math(*tin)), []

    spec = pl.BlockSpec((tr, cols), lambda i: (i, 0))
    outs = [(_sds((rows, cols), F32), spec)] * 3
    return _tiled(name, fn, (rows // tr,), [(a, spec) for a in (w, g, m, v)], _behind(dep), outs)


def _adamw_many(name, ws, gs, ms, vs):
    n = len(ws)
    views = [(-1, a.shape[-1]) if a.ndim > 1 else (1, -1) for a in ws]
    flat = lambda arrs: [a.reshape(vw) for a, vw in zip(arrs, views)]

    def kern(*refs):
        ins, outs = refs[:4 * n], refs[4 * n:]
        for t in range(n):
            res = _adamw_math(*[ins[q * n + t][...] for q in range(4)])
            for q in range(3):
                outs[q * n + t][...] = res[q]

    shapes = [_sds(a.shape, F32) for a in flat(ws)]
    res = _pcall(kern, name=name, out_shape=shapes * 3, compiler_params=_cparams(),
                 )(*flat(ws), *flat(gs), *flat(ms), *flat(vs))
    back = lambda part: [a.reshape(w.shape) for a, w in zip(part, ws)]
    return back(res[:n]), back(res[n:2 * n]), back(res[2 * n:])


def _pos_embed():
    n_rows = T // GRID_W
    q = D // 4
    omega = 1.0 / (10000.0 ** (jnp.arange(q, dtype=F32) / q))
    er = jnp.arange(n_rows, dtype=jnp.int32).astype(F32)[:, None] * omega[None, :]
    ec = jnp.arange(GRID_W, dtype=jnp.int32).astype(F32)[:, None] * omega[None, :]
    by_row = jnp.concatenate([jnp.sin(er), jnp.cos(er)], axis=-1)
    by_col = jnp.concatenate([jnp.sin(ec), jnp.cos(ec)], axis=-1)
    return jnp.concatenate([jnp.repeat(by_row, GRID_W, axis=0), jnp.tile(by_col, (n_rows, 1))], axis=-1)


def _dense_gates(w_a, w_x):
    rows = jnp.stack([w_a[0], w_x[0], w_a[1], w_x[1]]).reshape(4, 2, RH, BLK)
    mask, spread = _block_mask(), _block_spread().T.astype(BF16)

    def kern(r_ref, m_ref, s_ref, o_ref):
        tiled = jnp.dot(r_ref[...].astype(BF16), s_ref[...], preferred_element_type=F32)
        o_ref[...] = (tiled * m_ref[...]).astype(o_ref.dtype)

    return _pcall(
        kern, name="gates_dense", grid=(2, 4),
        in_specs=[pl.BlockSpec((None, None, RH, BLK), lambda h, q: (q, h, 0, 0)),
                  pl.BlockSpec((RH, RH), lambda h, q: (0, 0)), pl.BlockSpec((BLK, RH), lambda h, q: (0, 0))],
        out_specs=pl.BlockSpec((None, RH, RH), lambda h, q: (h, 0, q)),
        out_shape=_sds((2, RH, NQ), BF16),
    )(rows, mask, spread)


def _block_mask():
    r = lax.broadcasted_iota(jnp.int32, (RH, RH), 0) // BLK
    c = lax.broadcasted_iota(jnp.int32, (RH, RH), 1) // BLK
    return (r == c).astype(F32)


def _block_spread():
    c = lax.broadcasted_iota(jnp.int32, (RH, BLK), 0) % BLK
    j = lax.broadcasted_iota(jnp.int32, (RH, BLK), 1)
    return (c == j).astype(F32)


def _fold_blocks(dense, mask, spread):
    return jnp.dot(dense * mask, spread, preferred_element_type=F32, precision=lax.Precision.HIGHEST)


def _gate_block_grads(folded):
    per = N_BLK // 2
    kinds = [jnp.concatenate([folded[h, q].reshape(per, BLK, BLK) for h in range(2)], axis=0) for q in range(4)]
    return jnp.stack([kinds[0], kinds[2]]), jnp.stack([kinds[1], kinds[3]])


def _gate_bias_dense(b_a, b_x):
    cols = []
    for h in range(2):
        for src in (b_a[0], b_x[0], b_a[1], b_x[1]):
            cols.append(src.reshape(R)[h * RH:(h + 1) * RH])
    return jnp.concatenate(cols).reshape(1, 2 * NQ)


def _gate_bias_grads(dgb):
    v = dgb.reshape(2, 4, RH)
    kinds = [jnp.concatenate([v[0, q], v[1, q]]).reshape(N_BLK, BLK) for q in range(4)]
    return jnp.stack([kinds[0], kinds[2]]), jnp.stack([kinds[1], kinds[3]])


def _residual_epilogue(next_norm):
    def epi(acc, ex):
        x_new = ex[0] + ex[1] * acc
        outs = [acc, x_new]
        if next_norm:
            outs.append(_norm_mod(x_new, ex[-3], ex[-2], ex[-1]))
        return outs
    return epi


def _mlp_fwd(tag, x_in, h, gate, w_in, w_out, next_norm=None, dep=None):
    tm = MM_TILE
    (r,) = _mm(f"{tag}_in", h, w_in, _NN, (T // tm, 4, 1),
               pl.BlockSpec((tm, D), lambda i, j, k: (i, 0)), pl.BlockSpec((None, D, D), lambda i, j, k: (j, 0, 0)),
               [(_sds((T, FF), BF16), pl.BlockSpec((tm, D), lambda i, j, k: (i, j)))], (tm, D),
               extra=[(d_, _full_spec(d_)) for d_ in _behind(dep)], epi=lambda acc, ex: [jnp.maximum(acc, 0.0)])
    row_spec = pl.BlockSpec((tm, D), lambda i, j, k: (i, 0))
    outs = [(_sds((T, D), F32), row_spec)] * 2 + ([(_sds((T, D), BF16), row_spec)] if next_norm else [])
    res = _mm(f"{tag}_out", r, w_out, _NN, (T // tm, 1, FF // D),
              pl.BlockSpec((tm, D), lambda i, j, k: (i, k)), pl.BlockSpec((D, D), lambda i, j, k: (k, 0)),
              outs, (tm, D),
              extra=[(x_in, row_spec), (gate, _full_spec(gate))] + [(v, _full_spec(v)) for v in next_norm or ()],
              a_pre=lambda a: a * a, epi=_residual_epilogue(next_norm))
    return dict(h=h, r=r, o=res[0], x_in=x_in), res[1], (res[2] if next_norm else None)


def _behind(dep):
    return [] if dep is None else [dep]


def _gate_bwd(tag, dx, o, gate, dep=None):
    def fn(ids, t, v):
        d_o = t[0] * v[0]
        return [d_o], [_sum0(t[0] * t[1]), _sum0(d_o)]
    return _tiled(f"{tag}_gate_bwd", fn, (T // ROW_TILE,), [_rows(dx), _rows(o)], [gate] + _behind(dep),
                  [_orow(T, D, BF16)], [(1, D), (1, D)])


def _norm_bwd(tag, dx_res, dh, dh_off, x, g_norm, sc, with_dx=True, dep=None):
    n_t = x.shape[0] // ROW_TILE

    def fn(ids, t, v):
        if with_dx:
            dres, dhv, xv = t
        else:
            dhv, xv = t
        dxv, d_sh, d_sc, d_g = _norm_mod_bwd(dhv, xv, v[0], v[1])
        return ([dres + dxv] if with_dx else []), [d_sh, d_sc, d_g]

    ins = ([_rows(dx_res)] if with_dx else []) + [_rows(dh, off=dh_off), _rows(x)]
    outs = [_orow(x.shape[0], D, F32)] if with_dx else []
    return _tiled(f"{tag}_norm_bwd", fn, (n_t,), ins, [g_norm, sc] + _behind(dep), outs, [(1, D)] * 3)


def _mlp_bwd(tag, dx, saved, g_norm, sc, gate, w_in, w_out, dep=None):
    d_o, d_gate, _ = _gate_bwd(tag, dx, saved["o"], gate, dep)
    tm = MM_TILE
    r = saved["r"]
    (da,) = _mm(f"{tag}_dz", d_o, w_out, _NT, (T // tm, FF // D, 1),
                pl.BlockSpec((tm, D), lambda i, j, k: (i, 0)), pl.BlockSpec((D, D), lambda i, j, k: (j, 0)),
                [(_sds((T, FF), BF16), pl.BlockSpec((tm, D), lambda i, j, k: (i, j)))], (tm, D),
                extra=[(r, pl.BlockSpec((tm, D), lambda i, j, k: (i, j)))],
                epi=lambda acc, ex: [acc * (2.0 * ex[0].astype(F32))])
    tk = MM_TILE
    (dw_out,) = _mm(f"{tag}_dwout", r, d_o, _TN, (FF // tm, 1, T // tk),
                    pl.BlockSpec((tk, tm), lambda i, j, k: (k, i)), pl.BlockSpec((tk, D), lambda i, j, k: (k, 0)),
                    [(_sds((FF, D), BF16), pl.BlockSpec((tm, D), lambda i, j, k: (i, 0)))], (tm, D),
                    a_pre=lambda a: a * a)
    (dh,) = _mm(f"{tag}_dh", da, w_in, _NT, (T // tm, 1, 4),
                pl.BlockSpec((tm, D), lambda i, j, k: (i, k)), pl.BlockSpec((None, D, D), lambda i, j, k: (k, 0, 0)),
                [(_sds((T, D), F32), pl.BlockSpec((tm, D), lambda i, j, k: (i, 0)))], (tm, D))
    (dw_in,) = _mm(f"{tag}_dwin", saved["h"], da, _TN, (D // tm, 4, T // tk),
                   pl.BlockSpec((tk, tm), lambda i, j, k: (k, i)), pl.BlockSpec((tk, D), lambda i, j, k: (k, j)),
                   [(_sds((4, D, D), BF16), pl.BlockSpec((None, tm, D), lambda i, j, k: (j, i, 0)))], (tm, D))
    dx_in, d_sh, d_sc, d_g = _norm_bwd(tag, dx, dh, 0, saved["x_in"], g_norm, sc)
    return dx_in, dw_in, dw_out, dict(sh=d_sh, sc=d_sc, gate=d_gate, g_norm=d_g)


def _local_step(x, ctx, tgt, mods, cmods, norm_g, final_g, rec, conf, wg, on_grads=None, wg_pre=None, on_later=None):
    on_grads = on_grads or (lambda group, dws: None)
    wg_pre = wg_pre or (lambda group, after: None)
    on_later = on_later or (lambda after: None)
    n_t = T // ROW_TILE
    row = lambda v: v.reshape(1, -1)
    m0 = [row(mods[0, q]) for q in range(6)]
    m1 = [row(mods[1, q]) for q in range(6)]
    g00, g01, g10, g11 = (row(norm_g[0, 0]), row(norm_g[0, 1]), row(norm_g[1, 0]), row(norm_g[1, 1]))
    csh, csc = row(cmods[0]), row(cmods[1])
    pos = _pos_embed()

    def prep0(ids, t, v):
        cx, xv, pv = t
        is_ctx = ids[0] == 0
        xin = jnp.where(is_ctx, cx, xv + pv)
        sh = jnp.where(is_ctx, v[3], v[1])
        sc = jnp.where(is_ctx, v[4], v[2])
        return [_norm_mod(xin, v[0], sc, sh), xv + pv], []

    dep = wg_pre("rec_in", csh)
    hcat, x0 = _tiled(
        "prep0", prep0, (N_SCAN,),
        [(ctx, pl.BlockSpec((ROW_TILE, D), lambda i: (0, 0))), _rows(x, off=-1, clamp_lo=True),
         _rows(pos, off=-1, clamp_lo=True)],
        [g00, m0[0], m0[1], csh, csc] + _behind(dep),
        [_orow(TA, D, BF16), _orow(T, D, F32, off=-1, clamp_lo=True)])

    tm_a = REC_TILE
    w_rin = wg("rec_in", hcat)["rec_w_in"]
    (a_in,) = _mm("rec_in", hcat, w_rin, _NN, (TA // tm_a, 4, 1),
                  pl.BlockSpec((tm_a, D), lambda i, j, k: (i, 0)),
                  pl.BlockSpec((None, D, RH), lambda i, j, k: (j, 0, 0)),
                  [(_sds((TA, 2 * R), F32), pl.BlockSpec((tm_a, RH), lambda i, j, k: (i, j)))], (tm_a, RH))
    rec_starts = (0, 1)
    u = _dwconv("rec_conv", a_in, R // CW_REC, rec["conv_w"], row(rec["conv_b"]), 1, rec_starts, R, CW_REC)
    wbd = _dense_gates(rec["w_a"], rec["w_x"])
    gbias = _gate_bias_dense(rec["b_a"], rec["b_x"])
    lam = rec["lam"]
    a_f, b_f, a_r, b_r = _tiled("rg_fwd", _rg_fwd_fn, (TA // RG_TILE,), [_rows(u, tm=RG_TILE)], [wbd, gbias, lam],
                                [_orow(TA, R, F32, tm=RG_TILE)] * 4, vec_refs=True)
    dep = wg_pre("rec_out", a_f)
    dep = wg_pre("mlp0", a_f if dep is None else dep)
    y_f, y_r, hin_f, hin_r = _scan_fwd(a_f, b_f, a_r, b_r)

    def rec_mid(ids, t, v):
        gp, yf, yr = t
        g, _ = _gelu(gp)
        return [g * (yf + yr)], []

    (m_rec,) = _tiled("rec_mid", rec_mid, (n_t,),
                      [_rows(a_in, R, off=1), _rows(y_f, off=1), _rows(y_r, off=1)], _behind(dep),
                      [_orow(T, R, BF16)])
    tm = MM_TILE
    row_spec = pl.BlockSpec((tm, D), lambda i, j, k: (i, 0))
    norm_mlp0 = (g01, m0[4], m0[3])
    w_rout = wg("rec_out", m_rec)["rec_w_out"]
    o_rec, x1, h_mlp0 = _mm(
        "rec_out", m_rec, w_rout, _NN, (T // tm, 1, 1),
        pl.BlockSpec((tm, R), lambda i, j, k: (i, 0)), pl.BlockSpec((R, D), lambda i, j, k: (0, 0)),
        [(_sds((T, D), F32), row_spec)] * 2 + [(_sds((T, D), BF16), row_spec)], (tm, D),
        extra=[(x0, row_spec), (m0[2], _full_spec(m0[2]))] + [(v, _full_spec(v)) for v in norm_mlp0],
        epi=_residual_epilogue(norm_mlp0))
    w_m0 = wg("mlp0", x1)
    dep = wg_pre("conf", x1)
    mlp0, x2, h1 = _mlp_fwd("mlp0", x1, h_mlp0, m0[5], w_m0["w_in"], w_m0["w_out"], (g10, m1[1], m1[0]), dep)

    b_pw1 = row(conf["b_pw1"])
    w_cf = wg("conf", x2)
    dep = wg_pre("mlp1", x2)
    (pre,) = _mm("conf_pw1", h1, w_cf["conf_w_pw1"], _NN, (T // tm, 4, 1),
                 pl.BlockSpec((tm, D), lambda i, j, k: (i, 0)),
                 pl.BlockSpec((None, D, D // 2), lambda i, j, k: (j, 0, 0)),
                 [(_sds((T, 2 * D), F32), pl.BlockSpec((tm, D // 2), lambda i, j, k: (i, j)))], (tm, D // 2),
                 extra=[(b_pw1, pl.BlockSpec((1, D // 2), lambda i, j, k: (0, j)))]
                 + [(d_, _full_spec(d_)) for d_ in _behind(dep)],
                 epi=lambda acc, ex: [acc + ex[0]])
    (zg,) = _tiled("conf_glu", lambda ids, t, v: ([t[0] * _sigmoid(t[1])], []), (n_t,),
                   [_rows(pre, D, col=0), _rows(pre, D, col=1)], [], [_orow(T, D, F32)])
    conf_starts = (0,)
    zc = _dwconv("conf_conv", zg, 0, conf["conv_w"], row(conf["conv_b"]), CONF_KW // 2, conf_starts, D, CW_CONF)
    ln_g, ln_b = row(conf["ln_g"]), row(conf["ln_b"])

    def ln_silu(ids, t, v):
        nh, _ = _layernorm_parts(t[0])
        ln = nh * v[0] + v[1]
        return [ln * _sigmoid(ln)], []

    (s_conf,) = _tiled("conf_ln", ln_silu, (n_t,), [_rows(zc)], [ln_g, ln_b], [_orow(T, D, BF16)])
    b_pw2 = row(conf["b_pw2"])
    norm_mlp1 = (g11, m1[4], m1[3])
    pw2_epi = _residual_epilogue(norm_mlp1)
    y_conf, x3, h_mlp1 = _mm(
        "conf_pw2", s_conf, w_cf["conf_w_pw2"], _NN, (T // tm, 1, 1),
        row_spec, pl.BlockSpec((D, D), lambda i, j, k: (0, 0)),
        [(_sds((T, D), F32), row_spec)] * 2 + [(_sds((T, D), BF16), row_spec)], (tm, D),
        extra=[(x2, row_spec), (m1[2], _full_spec(m1[2])), (b_pw2, _full_spec(b_pw2))]
        + [(v, _full_spec(v)) for v in norm_mlp1],
        epi=lambda acc, ex: pw2_epi(acc + ex[2], ex))
    w_m1 = wg("mlp1", x3)
    mlp1, x4, _ = _mlp_fwd("mlp1", x3, h_mlp1, m1[5], w_m1["w_in"], w_m1["w_out"])

    fg = row(final_g)

    def head(ids, t, v):
        n, r = _rms(t[0])
        err = n * v[0] - t[1]
        d_out = err * (1.0 / D)
        dn = d_out * v[0]
        dxv = r * (dn - n * jnp.mean(dn * n, axis=-1, keepdims=True))
        part = jnp.sum(_sum0(err * err), axis=1, keepdims=True) * (0.5 / D)
        return [dxv], [part, _sum0(d_out * n)]

    dx4, loss, d_fg = _tiled("head", head, (n_t,), [_rows(x4), _rows(tgt)], [fg], [_orow(T, D, F32)],
                             [(1, 1), (1, D)])

    dx3, dw_in1, dw_out1, dm_mlp1 = _mlp_bwd("mlp1", dx4, mlp1, g11, m1[4], m1[5],
                                             w_m1["w_in"], w_m1["w_out"])
    dep = on_grads("mlp1", (dw_in1, dw_out1))
    d_y, d_g1c, d_bpw2 = _gate_bwd("conf", dx3, y_conf, m1[2], dep)
    tk = MM_TILE
    (dw_pw2,) = _mm("conf_dwpw2", s_conf, d_y, _TN, (D // tm, 1, T // tk),
                    pl.BlockSpec((tk, tm), lambda i, j, k: (k, i)), pl.BlockSpec((tk, D), lambda i, j, k: (k, 0)),
                    [(_sds((D, D), BF16), pl.BlockSpec((tm, D), lambda i, j, k: (i, 0)))], (tm, D))
    (ds,) = _mm("conf_ds", d_y, w_cf["conf_w_pw2"], _NT, (T // tm, 1, 1),
                pl.BlockSpec((tm, D), lambda i, j, k: (i, 0)), pl.BlockSpec((D, D), lambda i, j, k: (0, 0)),
                [(_sds((T, D), F32), pl.BlockSpec((tm, D), lambda i, j, k: (i, 0)))], (tm, D))
    dep = on_later(ds)

    def ln_silu_bwd(ids, t, v):
        dsv, zcv = t
        nh, rstd = _layernorm_parts(zcv)
        ln = nh * v[0] + v[1]
        sg = _sigmoid(ln)
        d_ln = dsv * (sg * (1.0 + ln * (1.0 - sg)))
        d_nh = d_ln * v[0]
        d_zc = rstd * (d_nh - jnp.mean(d_nh, axis=-1, keepdims=True)
                       - nh * jnp.mean(d_nh * nh, axis=-1, keepdims=True))
        return [d_zc], [_sum0(d_ln * nh), _sum0(d_ln)]

    d_zc, d_lng, d_lnb = _tiled("conf_ln_bwd", ln_silu_bwd, (n_t,), [_rows(ds), _rows(zc)],
                                [ln_g, ln_b] + _behind(dep), [_orow(T, D, F32)], [(1, D), (1, D)])
    d_zg = _dwconv("conf_conv_dx", d_zc, 0, conf["conv_w"], jnp.zeros((1, D), F32),
                   CONF_KW - 1 - CONF_KW // 2, conf_starts, D, CW_CONF, flip=True)

    def glu_bwd(ids, t, v):
        dz, pa, pb = t
        sg = _sigmoid(pb)
        d_a = dz * sg
        d_b = dz * pa * sg * (1.0 - sg)
        return [jnp.concatenate([d_a, d_b], axis=1)], [_sum0(d_a), _sum0(d_b)]

    d_pre, d_b1a, d_b1b = _tiled(
        "conf_glu_bwd", glu_bwd, (n_t,), [_rows(d_zg), _rows(pre, D, col=0), _rows(pre, D, col=1)], [],
        [_orow(T, 2 * D, BF16)], [(1, D), (1, D)])
    (dw_pw1,) = _mm("conf_dwpw1", h1, d_pre, _TN, (D // tm, 4, T // tk),
                    pl.BlockSpec((tk, tm), lambda i, j, k: (k, i)),
                    pl.BlockSpec((tk, D // 2), lambda i, j, k: (k, j)),
                    [(_sds((4, D, D // 2), BF16), pl.BlockSpec((None, tm, D // 2), lambda i, j, k: (j, i, 0)))],
                    (tm, D // 2))
    dep = on_grads("conf", (dw_pw1, dw_pw2))
    (dh1,) = _mm("conf_dh", d_pre, w_cf["conf_w_pw1"], _NT, (T // tm, 1, 4),
                 pl.BlockSpec((tm, D // 2), lambda i, j, k: (i, k)),
                 pl.BlockSpec((None, D, D // 2), lambda i, j, k: (k, 0, 0)),
                 [(_sds((T, D), F32), pl.BlockSpec((tm, D), lambda i, j, k: (i, 0)))], (tm, D))
    dx2, d_sh1c, d_sc1c, d_g10 = _norm_bwd("conf", dx3, dh1, 0, x2, g10, m1[1], dep=dep)
    dep = on_later(dx2)

    dx1, dw_in0, dw_out0, dm_mlp0 = _mlp_bwd("mlp0", dx2, mlp0, g01, m0[4], m0[5],
                                             w_m0["w_in"], w_m0["w_out"], dep)
    dep = on_grads("mlp0", (dw_in0, dw_out0))
    d_orec, d_g1r, _ = _gate_bwd("rec", dx1, o_rec, m0[2], dep)
    (dw_rout,) = _mm("rec_dwout", m_rec, d_orec, _TN, (R // RH, 1, T // tk),
                     pl.BlockSpec((tk, RH), lambda i, j, k: (k, i)), pl.BlockSpec((tk, D), lambda i, j, k: (k, 0)),
                     [(_sds((R, D), BF16), pl.BlockSpec((RH, D), lambda i, j, k: (i, 0)))], (RH, D))
    (dm_rec,) = _mm("rec_dm", d_orec, w_rout, _NT, (T // tm, 1, 1),
                    pl.BlockSpec((tm, D), lambda i, j, k: (i, 0)), pl.BlockSpec((R, D), lambda i, j, k: (0, 0)),
                    [(_sds((T, R), F32), pl.BlockSpec((tm, R), lambda i, j, k: (i, 0)))], (tm, R))
    dep = on_later(dm_rec)

    def rec_mid_bwd(ids, t, v):
        dmv, gp, yf, yr = t
        g, th = _gelu(gp)
        lat = ids[0] > 0
        d_gp = jnp.where(lat, dmv * (yf + yr) * _gelu_grad(gp, th), 0.0)
        dy = jnp.where(lat, dmv * g, 0.0)
        return [d_gp, dy], []

    d_a, dy = _tiled("rec_mid_bwd", rec_mid_bwd, (N_SCAN,),
                     [_rows(dm_rec, off=-1, clamp_lo=True), _rows(a_in, R), _rows(y_f), _rows(y_r)], _behind(dep),
                     [(_sds((TA, 2 * R), BF16), pl.BlockSpec((ROW_TILE, R), lambda i: (i, 0))), _orow(TA, R, F32)])
    da_f, db_f, da_r, db_r = _scan_bwd(dy, a_f, y_f, hin_f, a_r, y_r, hin_r)
    d_gpre, d_u, d_gbias, d_lam = _tiled(
        "rg_bwd", _rg_bwd_fn, (TA // RG_TILE,), [_rows(a, tm=RG_TILE) for a in (u, da_f, db_f, da_r, db_r)],
        [wbd, gbias, lam], [_orow(TA, 2 * NQ, BF16, tm=RG_TILE), _orow(TA, R, F32, tm=RG_TILE)],
        [(1, 2 * NQ), (1, 2 * R)], vec_refs=True)
    tk_a = REC_TILE
    d_a = _dwconv("rec_conv_dx", d_u, 0, rec["conv_w"], jnp.zeros((1, R), F32), REC_KW - 1 - 1,
                  rec_starts, R, CW_REC, flip=True, into=(d_a, R // CW_REC))
    (dw_rin,) = _mm("rec_dwin", hcat, d_a, _TN, (D // tm, 4, TA // tk_a),
                    pl.BlockSpec((tk_a, tm), lambda i, j, k: (k, i)), pl.BlockSpec((tk_a, RH), lambda i, j, k: (k, j)),
                    [(_sds((4, D, RH), BF16), pl.BlockSpec((None, tm, RH), lambda i, j, k: (j, i, 0)))], (tm, RH))
    dep = on_grads("rec", (dw_rin, dw_rout))
    (dhcat,) = _mm("rec_dh", d_a, w_rin, _NT, (TA // tm_a, 1, 4),
                   pl.BlockSpec((tm_a, RH), lambda i, j, k: (i, k)),
                   pl.BlockSpec((None, D, RH), lambda i, j, k: (k, 0, 0)),
                   [(_sds((TA, D), F32), pl.BlockSpec((tm_a, D), lambda i, j, k: (i, 0)))], (tm_a, D))
    dx0, d_sh1r, d_sc1r, d_g00 = _norm_bwd("rec", dx1, dhcat, 1, x0, g00, m0[1], dep=dep)
    dep = on_later(dx0)

    d_csh, d_csc, d_g00c = _norm_bwd("ctx", None, dhcat, 0, ctx, g00, csc, with_dx=False, dep=dep)
    blk_mask, blk_spread = _block_mask(), _block_spread()
    (d_wbd,) = _mm("rg_dw", u, d_gpre, _TN, (2, 2, TA // tk_a),
                   pl.BlockSpec((tk_a, RH), lambda i, j, k: (k, i)),
                   pl.BlockSpec((tk_a, NQ // 2), lambda i, j, k: (k, 2 * i + j)),
                   [(_sds((2, 4, RH, BLK), F32), pl.BlockSpec((None, 2, RH, BLK), lambda i, j, k: (i, j, 0, 0)))],
                   (RH, NQ // 2),
                   extra=[(blk_mask, _full_spec(blk_mask)), (blk_spread, _full_spec(blk_spread))]
                   + [(d, _full_spec(d)) for d in _behind(dep)],
                   epi=lambda acc, ex: [jnp.stack([_fold_blocks(acc[:, s * RH:(s + 1) * RH], ex[0], ex[1])
                                                   for s in range(2)])])
    d_cw_rec = _dwconv_wgrad("rec_conv_dw", d_u, a_in, R // CW_REC, REC_KW, 1, rec_starts, R, CW_REC, dep)
    d_cw_conf = _dwconv_wgrad("conf_conv_dw", d_zc, zg, 0, CONF_KW, CONF_KW // 2, conf_starts, D, CW_CONF, dep)

    big = dict(rec_w_in=dw_rin, rec_w_out=dw_rout, conf_w_pw1=dw_pw1, conf_w_pw2=dw_pw2,
               mlp_w_in=(dw_in0, dw_in1), mlp_w_out=(dw_out0, dw_out1))
    d_wa, d_wx = _gate_block_grads(d_wbd)
    d_ba, d_bx = _gate_bias_grads(d_gbias)
    d_mod = jnp.concatenate([
        d_sh1r, d_sc1r, d_g1r, dm_mlp0["sh"], dm_mlp0["sc"], dm_mlp0["gate"],
        d_sh1c, d_sc1c, d_g1c, dm_mlp1["sh"], dm_mlp1["sc"], dm_mlp1["gate"]], axis=1).reshape(2, 6 * D)
    small = dict(
        d_mod=d_mod, d_cmod=jnp.concatenate([d_csh, d_csc], axis=1),
        norm_g=jnp.concatenate([d_g00 + d_g00c, dm_mlp0["g_norm"], d_g10, dm_mlp1["g_norm"]], axis=1),
        rec_conv_w=d_cw_rec[:REC_KW], rec_conv_b=d_cw_rec[REC_KW], rec_lambda=d_lam.reshape(2, R),
        rec_w_a=d_wa, rec_b_a=d_ba, rec_w_x=d_wx, rec_b_x=d_bx,
        conf_b_pw1=jnp.concatenate([d_b1a, d_b1b], axis=1), conf_conv_w=d_cw_conf[:CONF_KW],
        conf_conv_b=d_cw_conf[CONF_KW], conf_ln_g=d_lng, conf_ln_b=d_lnb, conf_b_pw2=d_bpw2, final_g=d_fg)
    return loss.reshape(()), dx0, big, small


_BIG = ("rec_w_in", "rec_w_out", "conf_w_pw1", "conf_w_pw2", "mlp_w_in", "mlp_w_out")


def _halves(w):
    return w.reshape(w.shape[0], 2, w.shape[1] // 2, w.shape[2])


def _ada_fwd(c16, w_ada, b_shard):
    ns = w_ada.shape[2]
    tn = 512

    def kern(c_ref, w_ref, b_ref, o_ref):
        cv = c_ref[...]
        s = (cv * _sigmoid(cv)).astype(BF16)
        o_ref[...] = jnp.dot(s, w_ref[...].astype(BF16), preferred_element_type=F32) + b_ref[...]

    return _pcall(
        kern, name="ada_fwd", grid=(2, ns // tn),
        in_specs=[pl.BlockSpec((16, D), lambda l, j: (0, 0)), pl.BlockSpec((None, D, tn), lambda l, j: (l, 0, j)),
                  pl.BlockSpec((None, 1, tn), lambda l, j: (l, 0, j))],
        out_specs=pl.BlockSpec((None, 16, tn), lambda l, j: (l, 0, j)),
        out_shape=_sds((2, 16, ns), F32), compiler_params=_cparams(),
    )(c16, w_ada, b_shard)


def _ada_bwd(c16, dm16, w_ada):
    ns = w_ada.shape[2]
    tn = 512

    def kern(c_ref, dm_ref, w_ref, gw_ref, ds_ref):
        cv = c_ref[...]
        s = (cv * _sigmoid(cv)).astype(BF16)
        dm = dm_ref[...].astype(BF16)
        gw_ref[...] = lax.dot_general(s, dm, _TN, preferred_element_type=F32)

        @pl.when(jnp.logical_and(pl.program_id(0) == 0, pl.program_id(1) == 0))
        def _():
            ds_ref[...] = jnp.zeros_like(ds_ref)

        ds_ref[...] += lax.dot_general(dm, w_ref[...].astype(BF16), _NT, preferred_element_type=F32)

    return _pcall(
        kern, name="ada_bwd", grid=(2, ns // tn),
        in_specs=[pl.BlockSpec((16, D), lambda l, j: (0, 0)), pl.BlockSpec((None, 16, tn), lambda l, j: (l, 0, j)),
                  pl.BlockSpec((None, D, tn), lambda l, j: (l, 0, j))],
        out_specs=[pl.BlockSpec((None, D, tn), lambda l, j: (l, 0, j)), pl.BlockSpec((16, D), lambda l, j: (0, 0))],
        out_shape=[_sds((2, D, ns), F32), _sds((16, D), F32)], compiler_params=_cparams(),
    )(c16, dm16, w_ada)


def _cctx_grad(ds4, c_ctx):
    def kern(d_ref, c_ref, o_ref):
        tot = d_ref[0, 0:1, :] + d_ref[1, 0:1, :] + d_ref[2, 0:1, :] + d_ref[3, 0:1, :]
        cv = c_ref[...]
        sg = _sigmoid(cv)
        o_ref[...] = tot * (sg * (1.0 + cv * (1.0 - sg)))

    return _pcall(kern, name="cctx_grad", out_shape=_sds((1, D), F32))(ds4, c_ctx.reshape(1, D))


def kernel(x, c, ctx, c_ctx, w_ada, b_ada, norm_g, rec_w_in, rec_conv_w, rec_conv_b, rec_lambda, rec_w_a, rec_b_a, rec_w_x, rec_b_x, rec_w_out, conf_w_pw1, conf_b_pw1, conf_conv_w, conf_conv_b, conf_ln_g, conf_ln_b, conf_w_pw2, conf_b_pw2, mlp_w_in, mlp_w_out, final_g, loss_target, m_c_ctx, m_w_ada, m_b_ada, m_norm_g, m_rec_w_in, m_rec_conv_w, m_rec_conv_b, m_rec_lambda, m_rec_w_a, m_rec_b_a, m_rec_w_x, m_rec_b_x, m_rec_w_out, m_conf_w_pw1, m_conf_b_pw1, m_conf_conv_w, m_conf_conv_b, m_conf_ln_g, m_conf_ln_b, m_conf_w_pw2, m_conf_b_pw2, m_mlp_w_in, m_mlp_w_out, m_final_g, v_c_ctx, v_w_ada, v_b_ada, v_norm_g, v_rec_w_in, v_rec_conv_w, v_rec_conv_b, v_rec_lambda, v_rec_w_a, v_rec_b_a, v_rec_w_x, v_rec_b_x, v_rec_w_out, v_conf_w_pw1, v_conf_b_pw1, v_conf_conv_w, v_conf_conv_b, v_conf_ln_g, v_conf_ln_b, v_conf_w_pw2, v_conf_b_pw2, v_mlp_w_in, v_mlp_w_out, v_final_g):
    names = ["c_ctx", "w_ada", "b_ada", "norm_g", "rec_w_in", "rec_conv_w", "rec_conv_b", "rec_lambda", "rec_w_a",
             "rec_b_a", "rec_w_x", "rec_b_x", "rec_w_out", "conf_w_pw1", "conf_b_pw1", "conf_conv_w", "conf_conv_b",
             "conf_ln_g", "conf_ln_b", "conf_w_pw2", "conf_b_pw2", "mlp_w_in", "mlp_w_out", "final_g"]
    w = dict(zip(names, [c_ctx, w_ada, b_ada, norm_g, rec_w_in, rec_conv_w, rec_conv_b, rec_lambda, rec_w_a,
                         rec_b_a, rec_w_x, rec_b_x, rec_w_out, conf_w_pw1, conf_b_pw1, conf_conv_w, conf_conv_b,
                         conf_ln_g, conf_ln_b, conf_w_pw2, conf_b_pw2, mlp_w_in, mlp_w_out, final_g]))
    m = dict(zip(names, [m_c_ctx, m_w_ada, m_b_ada, m_norm_g, m_rec_w_in, m_rec_conv_w, m_rec_conv_b, m_rec_lambda,
                         m_rec_w_a, m_rec_b_a, m_rec_w_x, m_rec_b_x, m_rec_w_out, m_conf_w_pw1, m_conf_b_pw1,
                         m_conf_conv_w, m_conf_conv_b, m_conf_ln_g, m_conf_ln_b, m_conf_w_pw2, m_conf_b_pw2,
                         m_mlp_w_in, m_mlp_w_out, m_final_g]))
    v = dict(zip(names, [v_c_ctx, v_w_ada, v_b_ada, v_norm_g, v_rec_w_in, v_rec_conv_w, v_rec_conv_b, v_rec_lambda,
                         v_rec_w_a, v_rec_b_a, v_rec_w_x, v_rec_b_x, v_rec_w_out, v_conf_w_pw1, v_conf_b_pw1,
                         v_conf_conv_w, v_conf_conv_b, v_conf_ln_g, v_conf_ln_b, v_conf_w_pw2, v_conf_b_pw2,
                         v_mlp_w_in, v_mlp_w_out, v_final_g]))
    mx, my, mc = _me()
    chip = 2 * mx + my
    me = 4 * mx + 2 * my + mc

    sharded_small = ["norm_g", "rec_conv_w", "rec_lambda", "conf_b_pw1", "conf_conv_w", "conf_conv_b", "conf_ln_g",
                     "conf_ln_b", "conf_b_pw2"]
    packed, offs = _pack([c] + [w[k] for k in sharded_small], 8)
    place = jnp.stack([chip, mc]).astype(jnp.int32)
    shards = [("rec_in", _halves(rec_w_in), 0), ("rec_out", _halves(rec_w_out), 0),
              ("pw1", _halves(conf_w_pw1), 0), ("pw2", _halves(conf_w_pw2), 0),
              ("mlp_in0", _halves(mlp_w_in), 0), ("mlp_in1", _halves(mlp_w_in), 1),
              ("mlp_out0", _halves(mlp_w_out), 0), ("mlp_out1", _halves(mlp_w_out), 1)]
    small_state, small_sent = _send_start("gather_small_start", [packed[None]], _to_all7, [(7,) + packed.shape], place)
    (slot_rin,) = _place_big(shards[:1], place, small_sent)
    flying, gsems, swapping = {}, {}, {}
    flying["rec_in"], gsems["rec_in"], rec_started = _gather_start("gather_start_rec", [slot_rin], ((0,),), small_sent)
    slots = [slot_rin] + _place_big(shards[1:], place, rec_started)
    placed = jnp.broadcast_to(lax.dynamic_slice(slots[-1], (chip, 0, 0, 0), (1, 1, 1, 1)).reshape(1, 1), (8, 1))
    (own,), (landed,) = _send_wait("gather_small_wait", *small_state, _to_all7, placed)
    by_flip = jnp.concatenate([own, landed], axis=0)
    got_flat = jnp.take(by_flip, jnp.arange(8) ^ me, axis=0).reshape(8, -1)

    def piece(i):
        p, n, shape = offs[i]
        return got_flat[:, p:p + n].reshape((8,) + tuple(shape))

    c_rows = piece(0).reshape(8, D)
    full = {}
    for i, k in enumerate(sharded_small):
        per_chip = jnp.moveaxis(piece(1 + i)[0::2], 0, -2)
        full[k] = per_chip.reshape(per_chip.shape[:-2] + (4 * per_chip.shape[-1],))
    c16 = jnp.concatenate([c_rows, c_ctx.reshape(1, D), jnp.zeros((7, D), F32)], axis=0)

    ns = w_ada.shape[2]
    b_shard = lax.dynamic_slice_in_dim(b_ada, chip * ns, ns, axis=1).reshape(2, 1, ns)
    prod = _ada_fwd(c16, w_ada, b_shard)

    own_rows = lax.dynamic_index_in_dim(prod[:, :8].reshape(2, 4, 2, ns), mc, axis=2, keepdims=False)
    rows = jnp.concatenate([own_rows.transpose(1, 0, 2), jnp.broadcast_to(prod[0, 8], (4, 1, ns)),
                            jnp.zeros((4, 5, ns), F32)], axis=1)
    mod_state, mod_started = _send_start("mod_start", [rows], _to_chips, [(3, 8, ns)], place)
    use_order = dict(rec=(0, 1), mlp0=(4, 6), conf=(2, 3), mlp1=(5, 7))
    fetch_order = dict(rec_out=(1,), mlp0=(4, 6), conf=(2, 3), mlp1=(5, 7))
    order = [t for g in fetch_order for t in fetch_order[g]]
    groups = [tuple(order.index(t) for t in fetch_order[g]) for g in fetch_order]
    fly, sems, all_started = _gather_start("gather_start_rest", [slots[t] for t in order], tuple(groups), mod_started)
    for gi, g in enumerate(fetch_order):
        flying[g], gsems[g] = [fly[k] for k in groups[gi]], sems[2 * gi:2 * gi + 2]

    def wg_pre(group, after):
        bufs = _gather_wait(f"gather_wait_{group}", flying[group], *gsems[group], after)
        swapping[group], token = _swap_start(f"swap_start_{group}", bufs, after)
        return token

    def wg(group, after):
        bufs = _swap_wait(f"swap_wait_{group}", *swapping[group], after)
        if group == "rec_in":
            return dict(rec_w_in=bufs[0].reshape(4, D, RH))
        if group == "rec_out":
            return dict(rec_w_out=bufs[0].reshape(R, D))
        if group == "conf":
            return dict(conf_w_pw1=bufs[0].reshape(4, D, D // 2), conf_w_pw2=bufs[1].reshape(D, D))
        return dict(w_in=bufs[0].reshape(4, D, D), w_out=bufs[1].reshape(FF, D))

    (rows,), (landed,) = _send_wait("mod_wait", *mod_state, _to_chips, all_started)
    own = lax.dynamic_index_in_dim(rows, chip, axis=0, keepdims=True)
    by_flip = jnp.concatenate([own, landed[1:2], landed[0:1], landed[2:3]], axis=0)
    by_chip = jnp.take(by_flip, jnp.arange(4) ^ chip, axis=0)
    mods = by_chip[:, :2].transpose(1, 0, 2).reshape(2, 6, D)
    cmods = by_chip[:, 2].reshape(6, D)[:2]

    rec = dict(conv_w=full["rec_conv_w"][0], conv_b=rec_conv_b[0], lam=full["rec_lambda"][0],
               w_a=rec_w_a[0], b_a=rec_b_a[0], w_x=rec_w_x[0], b_x=rec_b_x[0])
    conf = dict(b_pw1=full["conf_b_pw1"][0], conv_w=full["conf_conv_w"][0], conv_b=full["conf_conv_b"][0],
                ln_g=full["conf_ln_g"][0], ln_b=full["conf_ln_b"][0], b_pw2=full["conf_b_pw2"][0])
    pairing, sent, sharing = {}, {}, {}

    def on_grads(group, dws):
        parts = [dw.reshape((4,) + shards[t][1].shape[1:]) for dw, t in zip(dws, use_order[group])]
        pairing[group], token = _reduce_begin(group, parts, place)
        return token

    def finish_pair(after):
        (group, state), = pairing.items()
        pairing.clear()
        sent[group], token = _reduce_mid(group, state, place, after)
        if group == "rec":
            mlp = _reduce_end("mlp1", sent["mlp1"], place, token, layer=(1, 2))
            cf = _reduce_end("conf", sent["conf"], place, token)
            mlp = _reduce_end("mlp0", sent["mlp0"], place, token, layer=(0, 2), into=mlp)
            sharing["state"], token = _share_start("share_start", cf + mlp, place)
        sent["token"] = token
        return token

    loss_local, grad_x, _, small = _local_step(x[0], ctx[0], loss_target[0], mods, cmods, full["norm_g"], final_g,
                                               rec, conf, wg, on_grads, wg_pre, finish_pair)
    rec_sent = sent["token"]
    small["loss"] = loss_local.reshape(1)

    small_names = ["loss", "d_mod", "d_cmod", "norm_g", "rec_conv_w", "rec_conv_b", "rec_lambda", "rec_w_a", "rec_b_a",
                   "rec_w_x", "rec_b_x", "conf_b_pw1", "conf_conv_w", "conf_conv_b", "conf_ln_g", "conf_ln_b",
                   "conf_b_pw2", "final_g"]
    mine = lax.broadcasted_iota(jnp.int32, (8, 1), 0) == me
    mod_slots = jnp.where(mine, small["d_mod"].reshape(1, -1), 0.0)
    spacked, soffs = _pack([small[k] for k in small_names] + [mod_slots])
    def sum_rec(after):
        sharing["rec"], token = _share_start("share_rec_start", _reduce_end("rec", sent["rec"], place, after), place)
        return token

    small_state, small_started = _allreduce_small_begin(spacked, place, rec_sent, sum_rec)

    shared = _share_wait("share_wait", *sharing["state"], small_started)
    delta, new_m, new_v = {}, {}, {}

    def adamw_of(k, g, dep=None):
        cols = w[k].shape[-1]
        d_, m_, v_ = _adamw(f"adamw_{k}", w[k].reshape(-1, cols), g.reshape(-1, cols),
                            m[k].reshape(-1, cols), v[k].reshape(-1, cols), dep)
        delta[k], new_m[k], new_v[k] = (a.reshape(w[k].shape) for a in (d_, m_, v_))

    g_big = {}
    for k, g in zip(_BIG[2:], shared):
        g_big[k] = g.reshape(w[k].shape)
        adamw_of(k, g_big[k])
    for k, g in zip(_BIG[:2], _share_wait("share_rec_wait", *sharing["rec"], new_v["mlp_w_out"])):
        g_big[k] = g.reshape(w[k].shape)
        adamw_of(k, g_big[k])
    unpacked = _unpack(_allreduce_small_end(small_state, [new_v[k] for k in _BIG]), soffs)
    ssum = dict(zip(small_names, unpacked[:-1]))
    loss = ssum["loss"].reshape(())
    dmod_rows = unpacked[-1].reshape(8, 2, 6 * D).transpose(1, 0, 2)

    d_cmod_full =jnp.concatenate([ssum["d_cmod"].reshape(1, 2 * D), jnp.zeros((1, 4 * D), F32)], axis=1)
    dm16 = jnp.concatenate([dmod_rows, jnp.stack([d_cmod_full, jnp.zeros((1, 6 * D), F32)]),
                            jnp.zeros((2, 7, 6 * D), F32)], axis=1)
    dm16_shard = lax.dynamic_slice_in_dim(dm16, chip * ns, ns, axis=2)
    g_w_ada, ds_part = _ada_bwd(c16, dm16_shard, w_ada)
    ds_state, ds_sent = _send_start("dsilu_start", [jnp.broadcast_to(ds_part[8:16], (4, 8, D))], _to_chips,
                                    [(3, 8, D)], place)
    adamw_of("w_ada", g_w_ada, ds_sent)
    (ds_own,), (ds_landed,) = _send_wait("dsilu_wait", *ds_state, _to_chips, new_v["w_ada"])
    ds_flip = jnp.concatenate([ds_own[:1], ds_landed[1:2], ds_landed[0:1], ds_landed[2:3]], axis=0)
    g_c_ctx = _cctx_grad(jnp.take(ds_flip, jnp.arange(4) ^ chip, axis=0), c_ctx).reshape(D)
    g_b_ada = ssum["d_mod"] + jnp.stack([d_cmod_full[0], jnp.zeros((6 * D,), F32)])

    def shard_of(a, axis):
        n = a.shape[axis] // 4
        return lax.dynamic_slice_in_dim(a, chip * n, n, axis=axis)

    grads = dict(
        c_ctx=g_c_ctx, w_ada=g_w_ada, b_ada=g_b_ada,
        norm_g=shard_of(ssum["norm_g"].reshape(2, 2, D), 2),
        rec_w_in=g_big["rec_w_in"], rec_conv_w=shard_of(ssum["rec_conv_w"].reshape(1, REC_KW, R), 2),
        rec_conv_b=ssum["rec_conv_b"].reshape(1, R), rec_lambda=shard_of(ssum["rec_lambda"].reshape(1, 2, R), 2),
        rec_w_a=ssum["rec_w_a"].reshape(rec_w_a.shape), rec_b_a=ssum["rec_b_a"].reshape(rec_b_a.shape),
        rec_w_x=ssum["rec_w_x"].reshape(rec_w_x.shape), rec_b_x=ssum["rec_b_x"].reshape(rec_b_x.shape),
        rec_w_out=g_big["rec_w_out"], conf_w_pw1=g_big["conf_w_pw1"],
        conf_b_pw1=shard_of(ssum["conf_b_pw1"].reshape(1, 2 * D), 1),
        conf_conv_w=shard_of(ssum["conf_conv_w"].reshape(1, CONF_KW, D), 2),
        conf_conv_b=shard_of(ssum["conf_conv_b"].reshape(1, D), 1),
        conf_ln_g=shard_of(ssum["conf_ln_g"].reshape(1, D), 1), conf_ln_b=shard_of(ssum["conf_ln_b"].reshape(1, D), 1),
        conf_w_pw2=g_big["conf_w_pw2"], conf_b_pw2=shard_of(ssum["conf_b_pw2"].reshape(1, D), 1),
        mlp_w_in=g_big["mlp_w_in"], mlp_w_out=g_big["mlp_w_out"], final_g=ssum["final_g"].reshape(D))

    rest =[k for k in names if k not in ("w_ada",) + _BIG]
    d_, m_, v_ = _adamw_many("adamw_small", [w[k] for k in rest], [grads[k] for k in rest],
                             [m[k] for k in rest], [v[k] for k in rest])
    for k, dd, mm, vv in zip(rest, d_, m_, v_):
        delta[k], new_m[k], new_v[k] = dd, mm, vv

    return (loss, grad_x[None], *[grads[k] for k in names], *[delta[k] for k in names],
            *[new_m[k] for k in names], *[new_v[k] for k in names])
```

```python
import functools
import math

import jax
import jax.numpy as jnp
from jax import lax
from jax.experimental import pallas as pl
from jax.experimental.pallas import tpu as pltpu

F32 = jnp.float32
BF16 = jnp.bfloat16

D = 1024
T = 2048
TC = 256
TA = T + TC
R = 1280
RH = R // 2
NQ = 4 * RH
FF = 4096
N_BLK = 16
BLK = R // N_BLK
GRID_W = 64
EPS = 1e-6
RG_C = 8.0
CONF_KW = 31
REC_KW = 4
LANE = 128
ROW_TILE = 256
HALO = 16
RG_TILE = 128
PACK_ROWS = 512
MM_TILE = 1024
REC_TILE = TA // 2
CW_REC = 640
CW_CONF = 512
V7X_VMEM_BYTES = 64 * 1024 * 1024
VMEM_LIMIT = V7X_VMEM_BYTES - 8 * 1024 * 1024

ADAM_LR = 0.001
ADAM_B1 = 0.9
ADAM_B2 = 0.999
ADAM_EPS = 1e-08
ADAM_WD = 0.01
ADAM_STEP = 10

MESH = pl.DeviceIdType.MESH
ANY = pl.BlockSpec(memory_space=pl.ANY)


def _sds(shape, dtype):
    return jax.ShapeDtypeStruct(tuple(shape), dtype)


def _pcall(body, **kw):
    return pl.pallas_call(body, **kw)


def _cparams():
    return pltpu.CompilerParams(vmem_limit_bytes=VMEM_LIMIT)


def _full_spec(arr):
    nd = arr.ndim
    return pl.BlockSpec(arr.shape, lambda *ids, _n=nd: (0,) * _n)


def _sum0(v):
    return jnp.sum(v, axis=0, keepdims=True)


def _tiled(name, fn, grid, ins, vecs, outs, vec_outs=(), vec_refs=False):
    n_in, n_vec, n_out = len(ins), len(vecs), len(outs)
    n_grid = len(grid)

    def kern(*refs):
        ids = [pl.program_id(a) for a in range(n_grid)]
        tin = [r[...] for r in refs[:n_in]]
        vin = list(refs[n_in:n_in + n_vec]) if vec_refs else [r[...] for r in refs[n_in:n_in + n_vec]]
        o_refs = refs[n_in + n_vec:n_in + n_vec + n_out]
        a_refs = refs[n_in + n_vec + n_out:]
        tout, incs = fn(ids, tin, vin)
        for r, v in zip(o_refs, tout):
            r[...] = v.astype(r.dtype)
        if a_refs:
            first = functools.reduce(jnp.logical_and, [i == 0 for i in ids])

            @pl.when(first)
            def _():
                for r in a_refs:
                    r[...] = jnp.zeros_like(r)

            for r, v in zip(a_refs, incs):
                r[...] += v

    out_shape = [o for o, _ in outs] + [_sds(s, F32) for s in vec_outs]
    out_specs = [s for _, s in outs] + [
        pl.BlockSpec(tuple(s), lambda *ids, _n=len(s): (0,) * _n) for s in vec_outs]
    res = _pcall(
        kern, name=name, grid=tuple(grid),
        in_specs=[s for _, s in ins] + [_full_spec(v) for v in vecs],
        out_specs=out_specs, out_shape=out_shape, compiler_params=_cparams(),
    )(*[a for a, _ in ins], *vecs)
    return list(res)


def _rows(arr, ncols=None, tm=ROW_TILE, off=0, col=0, clamp_lo=False):
    ncols = arr.shape[1] if ncols is None else ncols
    if clamp_lo:
        return arr, pl.BlockSpec((tm, ncols), lambda i: (jnp.maximum(i + off, 0), col))
    return arr, pl.BlockSpec((tm, ncols), lambda i: (i + off, col))


def _orow(nrows, ncols, dtype, tm=ROW_TILE, off=0, clamp_lo=False):
    if clamp_lo:
        return _sds((nrows, ncols), dtype), pl.BlockSpec((tm, ncols), lambda i: (jnp.maximum(i + off, 0), 0))
    return _sds((nrows, ncols), dtype), pl.BlockSpec((tm, ncols), lambda i: (i + off, 0))


_NN = (((1,), (0,)), ((), ()))
_TN = (((0,), (0,)), ((), ()))
_NT = (((1,), (1,)), ((), ()))


def _mm(name, a, b, dims, grid, a_spec, b_spec, out, acc_shape, extra=(), a_pre=None, epi=None):
    n_k = grid[2]
    n_ex = len(extra)

    def kern(a_ref, b_ref, *rest):
        ex = rest[:n_ex]
        o_refs = rest[n_ex:n_ex + len(out)]
        k = pl.program_id(2)
        av = a_ref[...]
        if a_pre is not None:
            av = a_pre(av)
        part = lax.dot_general(av.astype(BF16), b_ref[...].astype(BF16), dims, preferred_element_type=F32)

        def finish(total):
            vals = [total] if epi is None else epi(total, [e[...] for e in ex])
            for r, v in zip(o_refs, vals):
                r[...] = v.astype(r.dtype)

        if n_k == 1:
            finish(part)
        else:
            acc = rest[-1]

            @pl.when(k == 0)
            def _():
                acc[...] = part

            @pl.when(jnp.logical_and(k > 0, k < n_k - 1))
            def _():
                acc[...] += part

            @pl.when(k == n_k - 1)
            def _():
                finish(acc[...] + part)

    res = _pcall(
        kern, name=name, grid=tuple(grid),
        in_specs=[a_spec, b_spec] + [s for _, s in extra],
        out_specs=[s for _, s in out], out_shape=[o for o, _ in out],
        scratch_shapes=[] if n_k == 1 else [pltpu.VMEM(tuple(acc_shape), F32)], compiler_params=_cparams(),
    )(a, b, *[e for e, _ in extra])
    return list(res)


def _rms(x):
    r = lax.rsqrt(jnp.mean(x * x, axis=-1, keepdims=True) + EPS)
    return x * r, r


def _norm_mod(x, g, sc, sh):
    n, _ = _rms(x)
    return (n * g) * (1.0 + sc) + sh


def _norm_mod_bwd(dh, x, g, sc):
    n, r = _rms(x)
    d_sh = _sum0(dh)
    d_sc = _sum0(dh * (n * g))
    d_g = _sum0(dh * (1.0 + sc) * n)
    dn = dh * (g * (1.0 + sc))
    dx = r * (dn - n * jnp.mean(dn * n, axis=-1, keepdims=True))
    return dx, d_sh, d_sc, d_g


_GELU_K = math.sqrt(2.0 / math.pi)


def _gelu(x):
    t = jnp.tanh(_GELU_K * (x + 0.044715 * x * x * x))
    return 0.5 * x * (1.0 + t), t


def _gelu_grad(x, t):
    return 0.5 * (1.0 + t) + 0.5 * x * (1.0 - t * t) * (_GELU_K * (1.0 + 3.0 * 0.044715 * x * x))


def _sigmoid(x):
    return 0.5 * jnp.tanh(0.5 * x) + 0.5


def _expm1(x):
    p = jnp.full_like(x, 1.0 / 5040.0)
    for c in (1.0 / 720.0, 1.0 / 120.0, 1.0 / 24.0, 1.0 / 6.0, 0.5, 1.0):
        p = p * x + c
    return jnp.where(jnp.abs(x) < 0.3, x * p, jnp.exp(x) - 1.0)


def _softplus_neg(lam):
    return jnp.log1p(jnp.exp(-jnp.abs(lam))) + jnp.maximum(-lam, 0.0)


def _layernorm_parts(x):
    mu = jnp.mean(x, axis=-1, keepdims=True)
    xc = x - mu
    rstd = lax.rsqrt(jnp.mean(xc * xc, axis=-1, keepdims=True) + EPS)
    return xc * rstd, rstd


def _rg_gates(u, wbd, gbias, lam):
    sp = _softplus_neg(lam)
    parts = {}
    for h in range(2):
        uh = u[:, h * RH:(h + 1) * RH]
        g = jnp.dot(uh.astype(BF16), wbd[h], preferred_element_type=F32) + gbias[:, h * NQ:(h + 1) * NQ]
        for d in range(2):
            r = _sigmoid(g[:, (2 * d) * RH:(2 * d + 1) * RH])
            i = _sigmoid(g[:, (2 * d + 1) * RH:(2 * d + 2) * RH])
            sph = sp[d:d + 1, h * RH:(h + 1) * RH]
            la = (-RG_C) * r * sph
            e2 = _expm1(2.0 * la)
            inv_mult = jnp.where(e2 < 0.0, lax.rsqrt(-e2), 0.0)
            parts[(d, h)] = dict(r=r, i=i, la=la, a=jnp.exp(la), e2=e2, mult=-e2 * inv_mult, inv_mult=inv_mult,
                                 uh=uh, sp=sph)
    return parts


def _rg_fwd_fn(ids, tin, vin):
    (u,) = tin
    wbd = vin[0]
    parts = _rg_gates(u, wbd, vin[1][...], vin[2][...])
    outs = []
    for d in range(2):
        a = jnp.concatenate([parts[(d, h)]["a"] for h in range(2)], axis=1)
        b = jnp.concatenate([parts[(d, h)]["mult"] * parts[(d, h)]["i"] * parts[(d, h)]["uh"]
                             for h in range(2)], axis=1)
        outs += [a, b]
    return outs, []


def _rg_bwd_fn(ids, tin, vin):
    u, da_f, db_f, da_r, db_r = tin
    wbd, lam = vin[0], vin[2][...]
    parts = _rg_gates(u, wbd, vin[1][...], lam)
    dab = ((da_f, db_f), (da_r, db_r))
    dsig_lam = -1.0 / (1.0 + jnp.exp(lam))
    du_halves, dpre_halves, dlam = [], [], [[None, None], [None, None]]
    for h in range(2):
        du = jnp.zeros_like(parts[(0, h)]["uh"])
        dpre = []
        for d in range(2):
            p = parts[(d, h)]
            da = dab[d][0][:, h * RH:(h + 1) * RH]
            db = dab[d][1][:, h * RH:(h + 1) * RH]
            d_mult = db * p["i"] * p["uh"]
            d_i = db * p["mult"] * p["uh"]
            du = du + db * p["mult"] * p["i"]
            d_la = da * p["a"] - d_mult * (p["e2"] + 1.0) * p["inv_mult"]
            d_r = d_la * ((-RG_C) * p["sp"])
            dlam[d][h] = _sum0(d_la * ((-RG_C) * p["r"])) * dsig_lam[d:d + 1, h * RH:(h + 1) * RH]
            dpre += [d_r * p["r"] * (1.0 - p["r"]), d_i * p["i"] * (1.0 - p["i"])]
        dpre = jnp.concatenate(dpre, axis=1)
        du = du + lax.dot_general(dpre.astype(BF16), wbd[h], _NT, preferred_element_type=F32)
        du_halves.append(du)
        dpre_halves.append(dpre)
    dpre_all = jnp.concatenate(dpre_halves, axis=1)
    dlam_row = jnp.concatenate([dlam[0][0], dlam[0][1], dlam[1][0], dlam[1][1]], axis=1)
    return [dpre_all, jnp.concatenate(du_halves, axis=1)], [_sum0(dpre_all), dlam_row]


def _tile_flags(i, n_tiles, seq_starts):
    starts_here = functools.reduce(jnp.logical_or, [i == s for s in seq_starts])
    ends_here = functools.reduce(jnp.logical_or, [i + 1 == s for s in seq_starts] + [i + 1 == n_tiles])
    return jnp.logical_not(starts_here), jnp.logical_not(ends_here)


def _halo_specs(col0, cw):
    hb = ROW_TILE // HALO
    prev = pl.BlockSpec((HALO, cw), lambda i, c: (jnp.maximum(i * hb - 1, 0), col0 + c))
    cur = pl.BlockSpec((ROW_TILE, cw), lambda i, c: (i, col0 + c))
    return prev, cur, hb


def _window(prev_ref, cur_ref, next_ref, has_prev, has_next):
    prev = jnp.where(has_prev, prev_ref[...], 0.0)
    nxt = jnp.where(has_next, next_ref[...], 0.0)
    return jnp.concatenate([prev, cur_ref[...], nxt], axis=0)


def _tap_reader(win):
    sub = 8
    n = win.shape[0]
    shifted = {0: win}

    def tap(off):
        s = off % sub
        if s not in shifted:
            shifted[s] = pltpu.roll(win, n - s, axis=0)
        return shifted[s][off - s:off - s + ROW_TILE, :]

    return tap


def _dwconv(name, x, col0, w, bias, pad_left, seq_starts, n_ch, cw=256, flip=False, into=None):
    n_rows = x.shape[0]
    n_tiles = n_rows // ROW_TILE
    n_taps = w.shape[0]
    prev_spec, cur_spec, hb = _halo_specs(col0, cw)
    last_hb = n_rows // HALO - 1
    next_spec = pl.BlockSpec((HALO, cw), lambda i, c: (jnp.minimum((i + 1) * hb, last_hb), col0 + c))
    dest, out_col0 = (None, 0) if into is None else into

    def kern(prev_ref, cur_ref, next_ref, w_ref, b_ref, *rest):
        o_ref = rest[-1]
        has_prev, has_next = _tile_flags(pl.program_id(0), n_tiles, seq_starts)
        win = _window(prev_ref, cur_ref, next_ref, has_prev, has_next)
        tap = _tap_reader(win)
        wv = w_ref[...]
        acc = jnp.zeros((ROW_TILE, cw), F32) + b_ref[...]
        for k in range(n_taps):
            kw = n_taps - 1 - k if flip else k
            acc = acc + wv[kw:kw + 1, :] * tap(HALO + k - pad_left)
        o_ref[...] = acc.astype(o_ref.dtype)

    return _pcall(
        kern, name=name, grid=(n_tiles, n_ch // cw),
        in_specs=[prev_spec, cur_spec, next_spec,
                  pl.BlockSpec((n_taps, cw), lambda i, c: (0, c)), pl.BlockSpec((1, cw), lambda i, c: (0, c))]
        + ([] if dest is None else [ANY]),
        out_specs=pl.BlockSpec((ROW_TILE, cw), lambda i, c: (i, out_col0 + c)),
        out_shape=_sds((n_rows, n_ch), F32) if dest is None else _sds(dest.shape, dest.dtype),
        input_output_aliases={} if dest is None else {5: 0}, compiler_params=_cparams(),
    )(x, x, x, w, bias, *([] if dest is None else [dest]))


def _dwconv_wgrad(name, dy, x, col0, n_taps, pad_left, seq_starts, n_ch, cw=256, dep=None):
    deps = [] if dep is None else [dep]
    n_rows = dy.shape[0]
    n_tiles = n_rows // ROW_TILE
    n_out = -(-(n_taps + 1) // 8) * 8
    prev_spec, cur_spec, hb = _halo_specs(col0, cw)
    last_hb = n_rows // HALO - 1
    next_spec = pl.BlockSpec((HALO, cw), lambda c, i: (jnp.minimum((i + 1) * hb, last_hb), col0 + c))
    prev_spec = pl.BlockSpec((HALO, cw), lambda c, i: (jnp.maximum(i * hb - 1, 0), col0 + c))
    cur_spec = pl.BlockSpec((ROW_TILE, cw), lambda c, i: (i, col0 + c))

    def kern(dy_ref, prev_ref, cur_ref, next_ref, *rest):
        o_ref = rest[-1]
        i = pl.program_id(1)
        has_prev, has_next = _tile_flags(i, n_tiles, seq_starts)
        win = _window(prev_ref, cur_ref, next_ref, has_prev, has_next)
        dyv = dy_ref[...]
        tap = _tap_reader(win)
        rid = lax.broadcasted_iota(jnp.int32, (n_out, cw), 0)
        inc = jnp.where(rid == n_taps, _sum0(dyv), 0.0)
        for k in range(n_taps):
            inc = inc + jnp.where(rid == k, _sum0(dyv * tap(HALO + k - pad_left)), 0.0)

        @pl.when(i == 0)
        def _():
            o_ref[...] = jnp.zeros_like(o_ref)

        o_ref[...] += inc

    return _pcall(
        kern, name=name, grid=(n_ch // cw, n_tiles),
        in_specs=[pl.BlockSpec((ROW_TILE, cw), lambda c, i: (i, c)), prev_spec, cur_spec, next_spec]
        + [pl.BlockSpec(d.shape, lambda c, i: (0, 0)) for d in deps],
        out_specs=pl.BlockSpec((n_out, cw), lambda c, i: (0, c)),
        out_shape=_sds((n_out, n_ch), F32), compiler_params=_cparams(),
    )(dy, x, x, x, *deps)


N_SCAN = TA // ROW_TILE


def _rev_block(j):
    return jnp.where(j == 0, 0, N_SCAN - j)


def _scan_fwd(a_f, b_f, a_r, b_r):
    fwd_spec = pl.BlockSpec((ROW_TILE, R), lambda i: (i, 0))
    rev_spec = pl.BlockSpec((ROW_TILE, R), lambda i: (_rev_block(i), 0))
    hin_spec = pl.BlockSpec((None, 1, R), lambda i: (i, 0, 0))

    def kern(af, bf, ar, br, yf, yr, hin_f, hin_r, hf_s, hr_s):
        @pl.when(pl.program_id(0) == 0)
        def _():
            hf_s[...] = jnp.zeros_like(hf_s)
            hr_s[...] = jnp.zeros_like(hr_s)

        hin_f[...] = hf_s[...]
        hin_r[...] = hr_s[...]

        def step(s8, carry):
            hf, hr = carry
            t0 = pl.multiple_of(s8 * 8, 8)
            for q in range(8):
                tf = t0 + q
                hf = af[pl.ds(tf, 1), :] * hf + bf[pl.ds(tf, 1), :]
                yf[pl.ds(tf, 1), :] = hf
                tr = ROW_TILE - 1 - tf
                hr = ar[pl.ds(tr, 1), :] * hr + br[pl.ds(tr, 1), :]
                yr[pl.ds(tr, 1), :] = hr
            return hf, hr

        hf, hr = lax.fori_loop(0, ROW_TILE // 8, step, (hf_s[...], hr_s[...]))
        hf_s[...] = hf
        hr_s[...] = hr

    return _pcall(
        kern, name="scan_fwd", grid=(N_SCAN,),
        in_specs=[fwd_spec, fwd_spec, rev_spec, rev_spec],
        out_specs=[fwd_spec, rev_spec, hin_spec, hin_spec],
        out_shape=[_sds((TA, R), F32), _sds((TA, R), F32), _sds((N_SCAN, 1, R), F32), _sds((N_SCAN, 1, R), F32)],
        scratch_shapes=[pltpu.VMEM((1, R), F32), pltpu.VMEM((1, R), F32)], compiler_params=_cparams(),
    )(a_f, b_f, a_r, b_r)


def _scan_bwd(dy, a_f, y_f, hin_f, a_r, y_r, hin_r):
    fwd_spec = pl.BlockSpec((ROW_TILE, R), lambda i: (N_SCAN - 1 - i, 0))
    rev_spec = pl.BlockSpec((ROW_TILE, R), lambda i: (_rev_block(N_SCAN - 1 - i), 0))
    hin_spec = pl.BlockSpec((None, 1, R), lambda i: (N_SCAN - 1 - i, 0, 0))
    last = ROW_TILE - 1

    def kern(dyf, af, yf, hf0, dyr, ar, yr, hr0, daf, dbf, dar, dbr, gf_s, anf_s, gr_s, anr_s):
        @pl.when(pl.program_id(0) == 0)
        def _():
            for r in (gf_s, anf_s, gr_s, anr_s):
                r[...] = jnp.zeros_like(r)

        def one(dy_ref, a_ref, y_ref, da_ref, db_ref, g, an, p, pprev):
            gnew = dy_ref[pl.ds(p, 1), :] + an * g
            db_ref[pl.ds(p, 1), :] = gnew
            da_ref[pl.ds(p, 1), :] = gnew * y_ref[pl.ds(pprev, 1), :]
            return gnew, a_ref[pl.ds(p, 1), :]

        def step(s8, carry):
            gf, anf, gr, anr = carry
            base = s8 * 8
            for q in range(8):
                s = last - (base + q)
                gf, anf = one(dyf, af, yf, daf, dbf, gf, anf, s, s - 1)
                gr, anr = one(dyr, ar, yr, dar, dbr, gr, anr, last - s, last - s + 1)
            return gf, anf, gr, anr

        carry = (gf_s[...], anf_s[...], gr_s[...], anr_s[...])
        carry = lax.fori_loop(0, ROW_TILE // 8 - 1, step, carry)
        gf, anf, gr, anr = carry
        for s in range(7, 0, -1):
            gf, anf = one(dyf, af, yf, daf, dbf, gf, anf, s, s - 1)
            gr, anr = one(dyr, ar, yr, dar, dbr, gr, anr, last - s, last - s + 1)
        gf0 = dyf[0:1, :] + anf * gf
        dbf[0:1, :] = gf0
        daf[0:1, :] = gf0 * hf0[...]
        gr0 = dyr[last:last + 1, :] + anr * gr
        dbr[last:last + 1, :] = gr0
        dar[last:last + 1, :] = gr0 * hr0[...]
        gf_s[...] = gf0
        anf_s[...] = af[0:1, :]
        gr_s[...] = gr0
        anr_s[...] = ar[last:last + 1, :]

    return _pcall(
        kern, name="scan_bwd", grid=(N_SCAN,),
        in_specs=[fwd_spec, fwd_spec, fwd_spec, hin_spec, rev_spec, rev_spec, rev_spec, hin_spec],
        out_specs=[fwd_spec, fwd_spec, rev_spec, rev_spec],
        out_shape=[_sds((TA, R), F32)] * 4,
        scratch_shapes=[pltpu.VMEM((1, R), F32)] * 4, compiler_params=_cparams(),
    )(dy, a_f, y_f, hin_f, dy, a_r, y_r, hin_r)


def _me():
    return lax.axis_index("x"), lax.axis_index("y"), lax.axis_index("c")


def _other_chips(mx, my):
    return [(1 - mx, my), (mx, 1 - my), (1 - mx, 1 - my)]


def _rcopy(src, dst, ssem, rsem, dev):
    return pltpu.make_async_remote_copy(src_ref=src, dst_ref=dst, send_sem=ssem, recv_sem=rsem,
                                        device_id=dev, device_id_type=MESH)


def _peers7(mx, my, mc):
    peers = []
    for k in range(1, 8):
        peers.append((1 - mx if (k >> 2) & 1 else mx, 1 - my if (k >> 1) & 1 else my, 1 - mc if k & 1 else mc))
    return peers


def _halves_of(refs, c):
    out = []
    for r in refs:
        out += [r.at[c]] if len(r.shape) == 3 else [r.at[l, c] for l in range(r.shape[0])]
    return out


def _share_start(name, fulls, after):
    n = len(fulls)
    n_cp = sum(1 if f.ndim == 3 else f.shape[0] for f in fulls)

    def kern(*refs):
        o = refs[n + 1:2 * n + 1]
        ssem, rsem, token = refs[2 * n + 1:]
        mx, my, mc = _me()
        for q, half in enumerate(_halves_of(o, mc)):
            _rcopy(half, half, ssem.at[q], rsem.at[q], (mx, my, 1 - mc)).start()
        token[...] = jnp.zeros_like(token)

    dma = pltpu.SemaphoreType.DMA
    res = _pcall(
        kern, name=name, in_specs=[ANY] * (n + 1),
        out_specs=[ANY] * n + [SEM, SEM, pl.BlockSpec(memory_space=pltpu.VMEM)],
        out_shape=[_sds(f.shape, f.dtype) for f in fulls] + [dma((n_cp,)), dma((n_cp,)), _sds((8, LANE), F32)],
        input_output_aliases={t: t for t in range(n)},
        compiler_params=pltpu.CompilerParams(has_side_effects=_DATAFLOW),
    )(*fulls, after)
    return (list(res[:n]), res[n], res[n + 1]), res[n + 2]


def _share_wait(name, fulls, ssem, rsem, after):
    n = len(fulls)

    def kern(*refs):
        o = refs[:n]
        ssem_ref, rsem_ref = refs[n], refs[n + 1]
        mx, my, mc = _me()
        sib = (mx, my, 1 - mc)
        for q, (theirs, mine) in enumerate(zip(_halves_of(o, 1 - mc), _halves_of(o, mc))):
            _rcopy(theirs, theirs, ssem_ref.at[q], rsem_ref.at[q], sib).wait_recv()
            _rcopy(mine, mine, ssem_ref.at[q], rsem_ref.at[q], sib).wait_send()

    return list(_pcall(
        kern, name=name, in_specs=[ANY] * n + [SEM, SEM, ANY], out_specs=[ANY] * n,
        out_shape=[_sds(f.shape, f.dtype) for f in fulls], input_output_aliases={t: t for t in range(n)},
        compiler_params=pltpu.CompilerParams(has_side_effects=_DATAFLOW),
    )(*fulls, ssem, rsem, after))


def _tiled_sp(name, fn, grid, sp, ins, outs, into=None):
    n_in = len(ins)
    dest = [] if into is None else [into]

    def kern(sp_ref, *refs):
        tout = fn([r[...] for r in refs[:n_in]])
        for r, v in zip(refs[n_in + len(dest):], tout):
            r[...] = v.astype(r.dtype)

    gs = pltpu.PrefetchScalarGridSpec(num_scalar_prefetch=1, grid=tuple(grid),
                                      in_specs=[s for _, s in ins] + [ANY] * len(dest), out_specs=[s for _, s in outs])
    res = _pcall(kern, name=name, grid_spec=gs, out_shape=[o for o, _ in outs], compiler_params=_cparams(),
                 input_output_aliases={1 + n_in: 0} if dest else {})(sp, *[a for a, _ in ins], *dest)
    return list(res)


def _row_tile(rows, cols, itemsize=4, budget=2 * 1024 * 1024):
    tr = rows
    while tr * cols * itemsize > budget and tr % 32 == 0:
        tr //= 2
    return tr


def _place_big(shards, place, dep=None):
    slots = []
    for tag, s, layer in shards:
        rr, cc = s.shape[2], s.shape[3]
        tr = _row_tile(rr, cc)
        (slot,) = _tiled_sp(
            f"place_{tag}", lambda tin: [tin[0]], (2, rr // tr), place,
            [(s, pl.BlockSpec((None, None, tr, cc), lambda h, i, sp, layer=layer: (layer, h, i, 0)))]
            + [(d, pl.BlockSpec(d.shape, lambda h, i, sp: (0, 0))) for d in _behind(dep)],
            [(_sds((4, 2, rr, cc), BF16), pl.BlockSpec((None, None, tr, cc), lambda h, i, sp: (sp[0], h, i, 0)))])
        slots.append(slot)
    return slots


def _allreduce_small_begin(vec, place, after, during):
    hr = vec.shape[0] // 2
    tr = _row_tile(hr, LANE)
    blk = (None, None, tr, LANE)
    (pair,) = _tiled_sp(
        "small_place", lambda tin: [tin[0]], (2, hr // tr), place,
        [(vec.reshape(2, hr, LANE), pl.BlockSpec((None, tr, LANE), lambda h, i, sp: (h, i, 0)))],
        [(_sds((2, 2, hr, LANE), F32), pl.BlockSpec(blk, lambda h, i, sp: (sp[1], h, i, 0)))])
    crossing, token = _share_start("small_share_start", [pair.reshape(2, 2 * hr, LANE)], place)
    (pair,) = _share_wait("small_share_wait", *crossing, during(token))
    pair = pair.reshape(2, 2, hr, LANE)
    (slot,) = _tiled_sp(
        "small_pair_add", lambda tin: [tin[0] + tin[1]], (2, hr // tr), place,
        [(pair, pl.BlockSpec(blk, lambda h, i, sp: (0, h, i, 0))),
         (pair, pl.BlockSpec(blk, lambda h, i, sp: (1, h, i, 0)))],
        [(_sds((4, 2, hr, LANE), F32), pl.BlockSpec(blk, lambda h, i, sp: (sp[0], h, i, 0)))])
    fly, sems, token = _gather_start("small_start", [slot], ((0,),), after)
    return (fly, sems), token


def _allreduce_small_end(state, after):
    fly, sems = state
    (chips,) = _swap_halves("small_swap", _gather_wait("small_wait", fly, *sems, after))
    hr = chips.shape[2]
    tr = _row_tile(hr, LANE)
    blk = (None, None, tr, LANE)
    (total,) = _tiled(
        "small_chip_sum", lambda ids, tin, vin: ([((tin[0] + tin[1]) + tin[2]) + tin[3]], []), (2, hr // tr),
        [(chips, pl.BlockSpec(blk, lambda h, i, _j=j: (_j, h, i, 0))) for j in range(4)], [],
        [(_sds((2, hr, LANE), F32), pl.BlockSpec((None, tr, LANE), lambda h, i: (h, i, 0)))])
    return total.reshape(2 * hr, LANE)


SEM =pl.BlockSpec(memory_space=pltpu.SEMAPHORE)
_DATAFLOW = pltpu.SideEffectType.DATAFLOW_SIDE_EFFECTING


def _gather_start(name, slots, groups, after):
    n = len(slots)

    def kern(*refs):
        o = refs[n + 1:2 * n + 1]
        sems, token = refs[2 * n + 1:-1], refs[-1]
        mx, my, mc = _me()
        j0 = 2 * mx + my
        for gi, grp in enumerate(groups):
            for k, t in enumerate(grp):
                for q, (qx, qy) in enumerate(_other_chips(mx, my)):
                    _rcopy(o[t].at[j0, mc], o[t].at[j0, mc], sems[2 * gi].at[3 * k + q],
                           sems[2 * gi + 1].at[3 * k + q], (qx, qy, mc)).start()
        token[...] = jnp.zeros_like(token)

    sem_shapes = []
    for grp in groups:
        sem_shapes += [pltpu.SemaphoreType.DMA((3 * len(grp),))] * 2
    res = _pcall(
        kern, name=name, in_specs=[ANY] * (n + 1),
        out_specs=[ANY] * n + [SEM] * len(sem_shapes) + [pl.BlockSpec(memory_space=pltpu.VMEM)],
        out_shape=[_sds(w.shape, w.dtype) for w in slots] + sem_shapes + [_sds((8, LANE), F32)],
        input_output_aliases={t: t for t in range(n)},
        compiler_params=pltpu.CompilerParams(has_side_effects=_DATAFLOW),
    )(*slots, after)
    return list(res[:n]), list(res[n:-1]), res[-1]


def _gather_wait(name, bufs, ssem, rsem, after):
    n = len(bufs)
    afters = list(after) if isinstance(after, (list, tuple)) else [after]

    def kern(*refs):
        b = refs[:n]
        ssem_ref, rsem_ref = refs[n], refs[n + 1]
        mx, my, mc = _me()
        j0 = 2 * mx + my
        for k in range(n):
            for q, (qx, qy) in enumerate(_other_chips(mx, my)):
                jq = 2 * qx + qy
                _rcopy(b[k].at[jq, mc], b[k].at[jq, mc], ssem_ref.at[3 * k + q], rsem_ref.at[3 * k + q],
                       (qx, qy, mc)).wait_recv()
                _rcopy(b[k].at[j0, mc], b[k].at[j0, mc], ssem_ref.at[3 * k + q], rsem_ref.at[3 * k + q],
                       (qx, qy, mc)).wait_send()

    return list(_pcall(
        kern, name=name, in_specs=[ANY] * n + [SEM, SEM] + [ANY] * len(afters), out_specs=[ANY] * n,
        out_shape=[_sds(w.shape, w.dtype) for w in bufs], input_output_aliases={k: k for k in range(n)},
        compiler_params=pltpu.CompilerParams(has_side_effects=_DATAFLOW),
    )(*bufs, ssem, rsem, *afters))


def _swap_halves(name, bufs):
    n = len(bufs)

    def kern(*refs):
        o = refs[n:2 * n]
        ss, rs = refs[2 * n:]
        mx, my, mc = _me()
        sib = (mx, my, 1 - mc)
        sends = []
        for k in range(n):
            for q, (qx, qy) in enumerate(_other_chips(mx, my)):
                jq = 2 * qx + qy
                cp = _rcopy(o[k].at[jq, mc], o[k].at[jq, mc], ss.at[3 * k + q], rs.at[3 * k + q], sib)
                cp.start()
                sends.append(cp)
        for k in range(n):
            for q, (qx, qy) in enumerate(_other_chips(mx, my)):
                jq = 2 * qx + qy
                _rcopy(o[k].at[jq, 1 - mc], o[k].at[jq, 1 - mc], ss.at[3 * k + q], rs.at[3 * k + q], sib).wait_recv()
        for cp in sends:
            cp.wait_send()

    dma = pltpu.SemaphoreType.DMA
    return list(_pcall(
        kern, name=name, in_specs=[ANY] * n, out_specs=[ANY] * n,
        out_shape=[_sds(w.shape, w.dtype) for w in bufs], input_output_aliases={k: k for k in range(n)},
        scratch_shapes=[dma((3 * n,)), dma((3 * n,))],
    )(*bufs))


def _swap_start(name, bufs, after):
    n = len(bufs)

    def kern(*refs):
        o = refs[n + 1:2 * n + 1]
        ssem, rsem, token = refs[2 * n + 1:]
        mx, my, mc = _me()
        for k in range(n):
            for q, (qx, qy) in enumerate(_other_chips(mx, my)):
                jq = 2 * qx + qy
                _rcopy(o[k].at[jq, mc], o[k].at[jq, mc], ssem.at[3 * k + q], rsem.at[3 * k + q], (mx, my, 1 - mc)).start()
        token[...] = jnp.zeros_like(token)

    dma = pltpu.SemaphoreType.DMA
    res = _pcall(
        kern, name=name, in_specs=[ANY] * (n + 1),
        out_specs=[ANY] * n + [SEM, SEM, pl.BlockSpec(memory_space=pltpu.VMEM)],
        out_shape=[_sds(w.shape, w.dtype) for w in bufs] + [dma((3 * n,)), dma((3 * n,)), _sds((8, LANE), F32)],
        input_output_aliases={k: k for k in range(n)},
        compiler_params=pltpu.CompilerParams(has_side_effects=_DATAFLOW),
    )(*bufs, after)
    return (list(res[:n]), res[n], res[n + 1]), res[n + 2]


def _swap_wait(name, bufs, ssem, rsem, after):
    n = len(bufs)

    def kern(*refs):
        b = refs[:n]
        ssem_ref, rsem_ref = refs[n], refs[n + 1]
        mx, my, mc = _me()
        sib = (mx, my, 1 - mc)
        for k in range(n):
            for q, (qx, qy) in enumerate(_other_chips(mx, my)):
                jq = 2 * qx + qy
                _rcopy(b[k].at[jq, 1 - mc], b[k].at[jq, 1 - mc], ssem_ref.at[3 * k + q], rsem_ref.at[3 * k + q],
                       sib).wait_recv()
                _rcopy(b[k].at[jq, mc], b[k].at[jq, mc], ssem_ref.at[3 * k + q], rsem_ref.at[3 * k + q],
                       sib).wait_send()

    return list(_pcall(
        kern, name=name, in_specs=[ANY] * n + [SEM, SEM, ANY], out_specs=[ANY] * n,
        out_shape=[_sds(w.shape, w.dtype) for w in bufs], input_output_aliases={k: k for k in range(n)},
        compiler_params=pltpu.CompilerParams(has_side_effects=_DATAFLOW),
    )(*bufs, ssem, rsem, after))


def _to_sibling(mx, my, mc):
    return [((j, 1 - mc), j, (mx, my, 1 - mc)) for j in range(4)]


def _to_chips(mx, my, mc):
    return [((2 * qx + qy,), q, (qx, qy, mc)) for q, (qx, qy) in enumerate(_other_chips(mx, my))]


def _to_all7(mx, my, mc):
    return [((0,), k, dev) for k, dev in enumerate(_peers7(mx, my, mc))]


def _send_start(name, srcs, plan, land_shapes, after):
    n = len(srcs)
    per = len(plan(0, 0, 0))

    def kern(*refs):
        s, land = refs[n + 1:2 * n + 1], refs[2 * n + 1:3 * n + 1]
        ssem, rsem, token = refs[3 * n + 1:]
        for k in range(n):
            for q, (idx, slot, dev) in enumerate(plan(*_me())):
                _rcopy(s[k].at[idx], land[k].at[slot], ssem.at[per * k + q], rsem.at[per * k + q], dev).start()
        token[...] = jnp.zeros_like(token)

    dma = pltpu.SemaphoreType.DMA
    res = _pcall(
        kern, name=name, in_specs=[ANY] * (n + 1),
        out_specs=[ANY] * (2 * n) + [SEM, SEM, pl.BlockSpec(memory_space=pltpu.VMEM)],
        out_shape=[_sds(s.shape, s.dtype) for s in srcs] + [_sds(ls, s.dtype) for ls, s in zip(land_shapes, srcs)]
        + [dma((per * n,)), dma((per * n,)), _sds((8, LANE), F32)],
        input_output_aliases={k: k for k in range(n)},
        compiler_params=pltpu.CompilerParams(has_side_effects=_DATAFLOW),
    )(*srcs, after)
    return (list(res[:n]), list(res[n:2 * n]), res[2 * n], res[2 * n + 1]), res[2 * n + 2]


def _send_wait(name, srcs, lands, ssem, rsem, plan, after):
    n = len(srcs)
    per = len(plan(0, 0, 0))

    def kern(*refs):
        s, land = refs[:n], refs[n:2 * n]
        ssem_ref, rsem_ref = refs[2 * n], refs[2 * n + 1]
        for k in range(n):
            for q, (idx, slot, dev) in enumerate(plan(*_me())):
                cp = _rcopy(s[k].at[idx], land[k].at[slot], ssem_ref.at[per * k + q], rsem_ref.at[per * k + q], dev)
                cp.wait_recv()
                cp.wait_send()

    res = _pcall(
        kern, name=name, in_specs=[ANY] * (2 * n) + [SEM, SEM, ANY], out_specs=[ANY] * (2 * n),
        out_shape=[_sds(a.shape, a.dtype) for a in list(srcs) + list(lands)],
        input_output_aliases={k: k for k in range(2 * n)},
        compiler_params=pltpu.CompilerParams(has_side_effects=_DATAFLOW),
    )(*srcs, *lands, ssem, rsem, after)
    return list(res[:n]), list(res[n:])


def _reduce_begin(tag, parts, after):
    return _send_start(f"pair_start_{tag}", parts, _to_sibling, [(4,) + p.shape[2:] for p in parts], after)


def _reduce_mid(tag, pairing, place, after):
    parts, theirs = _send_wait(f"pair_wait_{tag}", *pairing, _to_sibling, after)
    sums = []
    for k, (p, o) in enumerate(zip(parts, theirs)):
        rr, cc = p.shape[2], p.shape[3]
        tr = _row_tile(rr, cc)
        (s_k,) = _tiled_sp(
            f"pair_add_{tag}{k}", lambda tin: [tin[0].astype(F32) + tin[1].astype(F32)], (4, rr // tr), place,
            [(p, pl.BlockSpec((None, None, tr, cc), lambda j, i, sp: (j, sp[1], i, 0))),
             (o, pl.BlockSpec((None, tr, cc), lambda j, i, sp: (j, i, 0)))],
            [(_sds((4, rr, cc), BF16), pl.BlockSpec((None, tr, cc), lambda j, i, sp: (j, i, 0)))])
        sums.append(s_k)
    return _send_start(f"chips_start_{tag}", sums, _to_chips, [(3,) + s.shape[1:] for s in sums], theirs[0])


def _reduce_end(tag, flying, place, after, layer=None, into=None):
    sums, lands = _send_wait(f"chips_wait_{tag}", *flying, _to_chips, after)
    fulls = []
    for k, (s, q) in enumerate(zip(sums, lands)):
        rr, cc = q.shape[1], q.shape[2]
        tr = _row_tile(rr, cc)

        def add4(tin):
            return [((tin[0].astype(F32) + tin[1].astype(F32)) + tin[2].astype(F32)) + tin[3].astype(F32)]

        ins = [(s, pl.BlockSpec((None, tr, cc), lambda i, sp: (sp[0], i, 0)))]
        ins += [(q, pl.BlockSpec((None, tr, cc), lambda i, sp, _k=kk: (_k, i, 0))) for kk in range(3)]
        if layer is None:
            out = (_sds((2, rr, cc), F32), pl.BlockSpec((None, tr, cc), lambda i, sp: (sp[1], i, 0)))
        else:
            out = (_sds((layer[1], 2, rr, cc), F32),
                   pl.BlockSpec((None, None, tr, cc), lambda i, sp, _l=layer[0]: (_l, sp[1], i, 0)))
        (f_k,) = _tiled_sp(f"chip_add_{tag}{k}", add4, (rr // tr,), place, ins, [out],
                           None if into is None else into[k])
        fulls.append(f_k)
    return fulls


def _pack(parts, PACK_ROWS=PACK_ROWS):
    flat, offs, pos = [], [], 0
    for p in parts:
        v = p.reshape(-1).astype(F32)
        n = -(-v.shape[0] // LANE) * LANE
        flat.append(jnp.pad(v, (0, n - v.shape[0])))
        offs.append((pos, v.shape[0], p.shape))
        pos += n
    total = -(-pos // (PACK_ROWS * LANE)) * PACK_ROWS * LANE
    flat.append(jnp.zeros((total - pos,), F32))
    return jnp.concatenate(flat).reshape(-1, LANE), offs


def _unpack(vec, offs):
    v = vec.reshape(-1)
    return [v[p:p + n].reshape(shape) for p, n, shape in offs]


def _adamw_math(wv, gv, mv, vv):
    bc1 = 1.0 - ADAM_B1 ** ADAM_STEP
    bc2 = 1.0 - ADAM_B2 ** ADAM_STEP
    mn = ADAM_B1 * mv + (1.0 - ADAM_B1) * gv
    vn = ADAM_B2 * vv + (1.0 - ADAM_B2) * (gv * gv)
    delta = -ADAM_LR * ((mn / bc1) / (jnp.sqrt(vn / bc2) + ADAM_EPS) + ADAM_WD * wv)
    return delta, mn, vn


def _adamw(name, w, g, m, v, dep=None):
    rows, cols = w.shape
    tr = rows
    for cand in (512, 256, 128, 64, 32, 16, 8):
        if rows % cand == 0 and cand * cols * 4 <= 2 * 1024 * 1024:
            tr = cand
            break

    def fn(ids, tin, vin):
        return list(_adamw_math(*tin)), []

    spec = pl.BlockSpec((tr, cols), lambda i: (i, 0))
    outs = [(_sds((rows, cols), F32), spec)] * 3
    return _tiled(name, fn, (rows // tr,), [(a, spec) for a in (w, g, m, v)], _behind(dep), outs)


def _adamw_many(name, ws, gs, ms, vs):
    n = len(ws)
    views = [(-1, a.shape[-1]) if a.ndim > 1 else (1, -1) for a in ws]
    flat = lambda arrs: [a.reshape(vw) for a, vw in zip(arrs, views)]

    def kern(*refs):
        ins, outs = refs[:4 * n], refs[4 * n:]
        for t in range(n):
            res = _adamw_math(*[ins[q * n + t][...] for q in range(4)])
            for q in range(3):
                outs[q * n + t][...] = res[q]

    shapes = [_sds(a.shape, F32) for a in flat(ws)]
    res = _pcall(kern, name=name, out_shape=shapes * 3, compiler_params=_cparams(),
                 )(*flat(ws), *flat(gs), *flat(ms), *flat(vs))
    back = lambda part: [a.reshape(w.shape) for a, w in zip(part, ws)]
    return back(res[:n]), back(res[n:2 * n]), back(res[2 * n:])


def _pos_embed():
    n_rows = T // GRID_W
    q = D // 4
    omega = 1.0 / (10000.0 ** (jnp.arange(q, dtype=F32) / q))
    er = jnp.arange(n_rows, dtype=jnp.int32).astype(F32)[:, None] * omega[None, :]
    ec = jnp.arange(GRID_W, dtype=jnp.int32).astype(F32)[:, None] * omega[None, :]
    by_row = jnp.concatenate([jnp.sin(er), jnp.cos(er)], axis=-1)
    by_col = jnp.concatenate([jnp.sin(ec), jnp.cos(ec)], axis=-1)
    return jnp.concatenate([jnp.repeat(by_row, GRID_W, axis=0), jnp.tile(by_col, (n_rows, 1))], axis=-1)


def _dense_gates(w_a, w_x):
    rows = jnp.stack([w_a[0], w_x[0], w_a[1], w_x[1]]).reshape(4, 2, RH, BLK)
    mask, spread = _block_mask(), _block_spread().T.astype(BF16)

    def kern(r_ref, m_ref, s_ref, o_ref):
        tiled = jnp.dot(r_ref[...].astype(BF16), s_ref[...], preferred_element_type=F32)
        o_ref[...] = (tiled * m_ref[...]).astype(o_ref.dtype)

    return _pcall(
        kern, name="gates_dense", grid=(2, 4),
        in_specs=[pl.BlockSpec((None, None, RH, BLK), lambda h, q: (q, h, 0, 0)),
                  pl.BlockSpec((RH, RH), lambda h, q: (0, 0)), pl.BlockSpec((BLK, RH), lambda h, q: (0, 0))],
        out_specs=pl.BlockSpec((None, RH, RH), lambda h, q: (h, 0, q)),
        out_shape=_sds((2, RH, NQ), BF16),
    )(rows, mask, spread)


def _block_mask():
    r = lax.broadcasted_iota(jnp.int32, (RH, RH), 0) // BLK
    c = lax.broadcasted_iota(jnp.int32, (RH, RH), 1) // BLK
    return (r == c).astype(F32)


def _block_spread():
    c = lax.broadcasted_iota(jnp.int32, (RH, BLK), 0) % BLK
    j = lax.broadcasted_iota(jnp.int32, (RH, BLK), 1)
    return (c == j).astype(F32)


def _fold_blocks(dense, mask, spread):
    return jnp.dot(dense * mask, spread, preferred_element_type=F32, precision=lax.Precision.HIGHEST)


def _gate_block_grads(folded):
    per = N_BLK // 2
    kinds = [jnp.concatenate([folded[h, q].reshape(per, BLK, BLK) for h in range(2)], axis=0) for q in range(4)]
    return jnp.stack([kinds[0], kinds[2]]), jnp.stack([kinds[1], kinds[3]])


def _gate_bias_dense(b_a, b_x):
    cols = []
    for h in range(2):
        for src in (b_a[0], b_x[0], b_a[1], b_x[1]):
            cols.append(src.reshape(R)[h * RH:(h + 1) * RH])
    return jnp.concatenate(cols).reshape(1, 2 * NQ)


def _gate_bias_grads(dgb):
    v = dgb.reshape(2, 4, RH)
    kinds = [jnp.concatenate([v[0, q], v[1, q]]).reshape(N_BLK, BLK) for q in range(4)]
    return jnp.stack([kinds[0], kinds[2]]), jnp.stack([kinds[1], kinds[3]])


def _residual_epilogue(next_norm):
    def epi(acc, ex):
        x_new = ex[0] + ex[1] * acc
        outs = [acc, x_new]
        if next_norm:
            outs.append(_norm_mod(x_new, ex[-3], ex[-2], ex[-1]))
        return outs
    return epi


def _mlp_fwd(tag, x_in, h, gate, w_in, w_out, next_norm=None, dep=None):
    tm = MM_TILE
    (r,) = _mm(f"{tag}_in", h, w_in, _NN, (T // tm, 4, 1),
               pl.BlockSpec((tm, D), lambda i, j, k: (i, 0)), pl.BlockSpec((None, D, D), lambda i, j, k: (j, 0, 0)),
               [(_sds((T, FF), BF16), pl.BlockSpec((tm, D), lambda i, j, k: (i, j)))], (tm, D),
               extra=[(d_, _full_spec(d_)) for d_ in _behind(dep)], epi=lambda acc, ex: [jnp.maximum(acc, 0.0)])
    row_spec = pl.BlockSpec((tm, D), lambda i, j, k: (i, 0))
    outs = [(_sds((T, D), F32), row_spec)] * 2 + ([(_sds((T, D), BF16), row_spec)] if next_norm else [])
    res = _mm(f"{tag}_out", r, w_out, _NN, (T // tm, 1, FF // D),
              pl.BlockSpec((tm, D), lambda i, j, k: (i, k)), pl.BlockSpec((D, D), lambda i, j, k: (k, 0)),
              outs, (tm, D),
              extra=[(x_in, row_spec), (gate, _full_spec(gate))] + [(v, _full_spec(v)) for v in next_norm or ()],
              a_pre=lambda a: a * a, epi=_residual_epilogue(next_norm))
    return dict(h=h, r=r, o=res[0], x_in=x_in), res[1], (res[2] if next_norm else None)


def _behind(dep):
    return [] if dep is None else [dep]


def _gate_bwd(tag, dx, o, gate, dep=None):
    def fn(ids, t, v):
        d_o = t[0] * v[0]
        return [d_o], [_sum0(t[0] * t[1]), _sum0(d_o)]
    return _tiled(f"{tag}_gate_bwd", fn, (T // ROW_TILE,), [_rows(dx), _rows(o)], [gate] + _behind(dep),
                  [_orow(T, D, BF16)], [(1, D), (1, D)])


def _norm_bwd(tag, dx_res, dh, dh_off, x, g_norm, sc, with_dx=True, dep=None):
    n_t = x.shape[0] // ROW_TILE

    def fn(ids, t, v):
        if with_dx:
            dres, dhv, xv = t
        else:
            dhv, xv = t
        dxv, d_sh, d_sc, d_g = _norm_mod_bwd(dhv, xv, v[0], v[1])
        return ([dres + dxv] if with_dx else []), [d_sh, d_sc, d_g]

    ins = ([_rows(dx_res)] if with_dx else []) + [_rows(dh, off=dh_off), _rows(x)]
    outs = [_orow(x.shape[0], D, F32)] if with_dx else []
    return _tiled(f"{tag}_norm_bwd", fn, (n_t,), ins, [g_norm, sc] + _behind(dep), outs, [(1, D)] * 3)


def _mlp_bwd(tag, dx, saved, g_norm, sc, gate, w_in, w_out, dep=None):
    d_o, d_gate, _ = _gate_bwd(tag, dx, saved["o"], gate, dep)
    tm = MM_TILE
    r = saved["r"]
    (da,) = _mm(f"{tag}_dz", d_o, w_out, _NT, (T // tm, FF // D, 1),
                pl.BlockSpec((tm, D), lambda i, j, k: (i, 0)), pl.BlockSpec((D, D), lambda i, j, k: (j, 0)),
                [(_sds((T, FF), BF16), pl.BlockSpec((tm, D), lambda i, j, k: (i, j)))], (tm, D),
                extra=[(r, pl.BlockSpec((tm, D), lambda i, j, k: (i, j)))],
                epi=lambda acc, ex: [acc * (2.0 * ex[0].astype(F32))])
    tk = MM_TILE
    (dw_out,) = _mm(f"{tag}_dwout", r, d_o, _TN, (FF // tm, 1, T // tk),
                    pl.BlockSpec((tk, tm), lambda i, j, k: (k, i)), pl.BlockSpec((tk, D), lambda i, j, k: (k, 0)),
                    [(_sds((FF, D), BF16), pl.BlockSpec((tm, D), lambda i, j, k: (i, 0)))], (tm, D),
                    a_pre=lambda a: a * a)
    (dh,) = _mm(f"{tag}_dh", da, w_in, _NT, (T // tm, 1, 4),
                pl.BlockSpec((tm, D), lambda i, j, k: (i, k)), pl.BlockSpec((None, D, D), lambda i, j, k: (k, 0, 0)),
                [(_sds((T, D), F32), pl.BlockSpec((tm, D), lambda i, j, k: (i, 0)))], (tm, D))
    (dw_in,) = _mm(f"{tag}_dwin", saved["h"], da, _TN, (D // tm, 4, T // tk),
                   pl.BlockSpec((tk, tm), lambda i, j, k: (k, i)), pl.BlockSpec((tk, D), lambda i, j, k: (k, j)),
                   [(_sds((4, D, D), BF16), pl.BlockSpec((None, tm, D), lambda i, j, k: (j, i, 0)))], (tm, D))
    dx_in, d_sh, d_sc, d_g = _norm_bwd(tag, dx, dh, 0, saved["x_in"], g_norm, sc)
    return dx_in, dw_in, dw_out, dict(sh=d_sh, sc=d_sc, gate=d_gate, g_norm=d_g)


def _local_step(x, ctx, tgt, mods, cmods, norm_g, final_g, rec, conf, wg, on_grads=None, wg_pre=None, on_later=None):
    on_grads = on_grads or (lambda group, dws: None)
    wg_pre = wg_pre or (lambda group, after: None)
    on_later = on_later or (lambda after: None)
    n_t = T // ROW_TILE
    row = lambda v: v.reshape(1, -1)
    m0 = [row(mods[0, q]) for q in range(6)]
    m1 = [row(mods[1, q]) for q in range(6)]
    g00, g01, g10, g11 = (row(norm_g[0, 0]), row(norm_g[0, 1]), row(norm_g[1, 0]), row(norm_g[1, 1]))
    csh, csc = row(cmods[0]), row(cmods[1])
    pos = _pos_embed()

    def prep0(ids, t, v):
        cx, xv, pv = t
        is_ctx = ids[0] == 0
        xin = jnp.where(is_ctx, cx, xv + pv)
        sh = jnp.where(is_ctx, v[3], v[1])
        sc = jnp.where(is_ctx, v[4], v[2])
        return [_norm_mod(xin, v[0], sc, sh), xv + pv], []

    dep = wg_pre("rec_in", csh)
    hcat, x0 = _tiled(
        "prep0", prep0, (N_SCAN,),
        [(ctx, pl.BlockSpec((ROW_TILE, D), lambda i: (0, 0))), _rows(x, off=-1, clamp_lo=True),
         _rows(pos, off=-1, clamp_lo=True)],
        [g00, m0[0], m0[1], csh, csc] + _behind(dep),
        [_orow(TA, D, BF16), _orow(T, D, F32, off=-1, clamp_lo=True)])

    tm_a = REC_TILE
    w_rin = wg("rec_in", hcat)["rec_w_in"]
    (a_in,) = _mm("rec_in", hcat, w_rin, _NN, (TA // tm_a, 4, 1),
                  pl.BlockSpec((tm_a, D), lambda i, j, k: (i, 0)),
                  pl.BlockSpec((None, D, RH), lambda i, j, k: (j, 0, 0)),
                  [(_sds((TA, 2 * R), F32), pl.BlockSpec((tm_a, RH), lambda i, j, k: (i, j)))], (tm_a, RH))
    rec_starts = (0, 1)
    u = _dwconv("rec_conv", a_in, R // CW_REC, rec["conv_w"], row(rec["conv_b"]), 1, rec_starts, R, CW_REC)
    wbd = _dense_gates(rec["w_a"], rec["w_x"])
    gbias = _gate_bias_dense(rec["b_a"], rec["b_x"])
    lam = rec["lam"]
    a_f, b_f, a_r, b_r = _tiled("rg_fwd", _rg_fwd_fn, (TA // RG_TILE,), [_rows(u, tm=RG_TILE)], [wbd, gbias, lam],
                                [_orow(TA, R, F32, tm=RG_TILE)] * 4, vec_refs=True)
    dep = wg_pre("rec_out", a_f)
    dep = wg_pre("mlp0", a_f if dep is None else dep)
    y_f, y_r, hin_f, hin_r = _scan_fwd(a_f, b_f, a_r, b_r)

    def rec_mid(ids, t, v):
        gp, yf, yr = t
        g, _ = _gelu(gp)
        return [g * (yf + yr)], []

    (m_rec,) = _tiled("rec_mid", rec_mid, (n_t,),
                      [_rows(a_in, R, off=1), _rows(y_f, off=1), _rows(y_r, off=1)], _behind(dep),
                      [_orow(T, R, BF16)])
    tm = MM_TILE
    row_spec = pl.BlockSpec((tm, D), lambda i, j, k: (i, 0))
    norm_mlp0 = (g01, m0[4], m0[3])
    w_rout = wg("rec_out", m_rec)["rec_w_out"]
    o_rec, x1, h_mlp0 = _mm(
        "rec_out", m_rec, w_rout, _NN, (T // tm, 1, 1),
        pl.BlockSpec((tm, R), lambda i, j, k: (i, 0)), pl.BlockSpec((R, D), lambda i, j, k: (0, 0)),
        [(_sds((T, D), F32), row_spec)] * 2 + [(_sds((T, D), BF16), row_spec)], (tm, D),
        extra=[(x0, row_spec), (m0[2], _full_spec(m0[2]))] + [(v, _full_spec(v)) for v in norm_mlp0],
        epi=_residual_epilogue(norm_mlp0))
    w_m0 = wg("mlp0", x1)
    dep = wg_pre("conf", x1)
    mlp0, x2, h1 = _mlp_fwd("mlp0", x1, h_mlp0, m0[5], w_m0["w_in"], w_m0["w_out"], (g10, m1[1], m1[0]), dep)

    b_pw1 = row(conf["b_pw1"])
    w_cf = wg("conf", x2)
    dep = wg_pre("mlp1", x2)
    (pre,) = _mm("conf_pw1", h1, w_cf["conf_w_pw1"], _NN, (T // tm, 4, 1),
                 pl.BlockSpec((tm, D), lambda i, j, k: (i, 0)),
                 pl.BlockSpec((None, D, D // 2), lambda i, j, k: (j, 0, 0)),
                 [(_sds((T, 2 * D), F32), pl.BlockSpec((tm, D // 2), lambda i, j, k: (i, j)))], (tm, D // 2),
                 extra=[(b_pw1, pl.BlockSpec((1, D // 2), lambda i, j, k: (0, j)))]
                 + [(d_, _full_spec(d_)) for d_ in _behind(dep)],
                 epi=lambda acc, ex: [acc + ex[0]])
    (zg,) = _tiled("conf_glu", lambda ids, t, v: ([t[0] * _sigmoid(t[1])], []), (n_t,),
                   [_rows(pre, D, col=0), _rows(pre, D, col=1)], [], [_orow(T, D, F32)])
    conf_starts = (0,)
    zc = _dwconv("conf_conv", zg, 0, conf["conv_w"], row(conf["conv_b"]), CONF_KW // 2, conf_starts, D, CW_CONF)
    ln_g, ln_b = row(conf["ln_g"]), row(conf["ln_b"])

    def ln_silu(ids, t, v):
        nh, _ = _layernorm_parts(t[0])
        ln = nh * v[0] + v[1]
        return [ln * _sigmoid(ln)], []

    (s_conf,) = _tiled("conf_ln", ln_silu, (n_t,), [_rows(zc)], [ln_g, ln_b], [_orow(T, D, BF16)])
    b_pw2 = row(conf["b_pw2"])
    norm_mlp1 = (g11, m1[4], m1[3])
    pw2_epi = _residual_epilogue(norm_mlp1)
    y_conf, x3, h_mlp1 = _mm(
        "conf_pw2", s_conf, w_cf["conf_w_pw2"], _NN, (T // tm, 1, 1),
        row_spec, pl.BlockSpec((D, D), lambda i, j, k: (0, 0)),
        [(_sds((T, D), F32), row_spec)] * 2 + [(_sds((T, D), BF16), row_spec)], (tm, D),
        extra=[(x2, row_spec), (m1[2], _full_spec(m1[2])), (b_pw2, _full_spec(b_pw2))]
        + [(v, _full_spec(v)) for v in norm_mlp1],
        epi=lambda acc, ex: pw2_epi(acc + ex[2], ex))
    w_m1 = wg("mlp1", x3)
    mlp1, x4, _ = _mlp_fwd("mlp1", x3, h_mlp1, m1[5], w_m1["w_in"], w_m1["w_out"])

    fg = row(final_g)

    def head(ids, t, v):
        n, r = _rms(t[0])
        err = n * v[0] - t[1]
        d_out = err * (1.0 / D)
        dn = d_out * v[0]
        dxv = r * (dn - n * jnp.mean(dn * n, axis=-1, keepdims=True))
        part = jnp.sum(_sum0(err * err), axis=1, keepdims=True) * (0.5 / D)
        return [dxv], [part, _sum0(d_out * n)]

    dx4, loss, d_fg = _tiled("head", head, (n_t,), [_rows(x4), _rows(tgt)], [fg], [_orow(T, D, F32)],
                             [(1, 1), (1, D)])

    dx3, dw_in1, dw_out1, dm_mlp1 = _mlp_bwd("mlp1", dx4, mlp1, g11, m1[4], m1[5],
                                             w_m1["w_in"], w_m1["w_out"])
    dep = on_grads("mlp1", (dw_in1, dw_out1))
    d_y, d_g1c, d_bpw2 = _gate_bwd("conf", dx3, y_conf, m1[2], dep)
    tk = MM_TILE
    (dw_pw2,) = _mm("conf_dwpw2", s_conf, d_y, _TN, (D // tm, 1, T // tk),
                    pl.BlockSpec((tk, tm), lambda i, j, k: (k, i)), pl.BlockSpec((tk, D), lambda i, j, k: (k, 0)),
                    [(_sds((D, D), BF16), pl.BlockSpec((tm, D), lambda i, j, k: (i, 0)))], (tm, D))
    (ds,) = _mm("conf_ds", d_y, w_cf["conf_w_pw2"], _NT, (T // tm, 1, 1),
                pl.BlockSpec((tm, D), lambda i, j, k: (i, 0)), pl.BlockSpec((D, D), lambda i, j, k: (0, 0)),
                [(_sds((T, D), F32), pl.BlockSpec((tm, D), lambda i, j, k: (i, 0)))], (tm, D))
    dep = on_later(ds)

    def ln_silu_bwd(ids, t, v):
        dsv, zcv = t
        nh, rstd = _layernorm_parts(zcv)
        ln = nh * v[0] + v[1]
        sg = _sigmoid(ln)
        d_ln = dsv * (sg * (1.0 + ln * (1.0 - sg)))
        d_nh = d_ln * v[0]
        d_zc = rstd * (d_nh - jnp.mean(d_nh, axis=-1, keepdims=True)
                       - nh * jnp.mean(d_nh * nh, axis=-1, keepdims=True))
        return [d_zc], [_sum0(d_ln * nh), _sum0(d_ln)]

    d_zc, d_lng, d_lnb = _tiled("conf_ln_bwd", ln_silu_bwd, (n_t,), [_rows(ds), _rows(zc)],
                                [ln_g, ln_b] + _behind(dep), [_orow(T, D, F32)], [(1, D), (1, D)])
    d_zg = _dwconv("conf_conv_dx", d_zc, 0, conf["conv_w"], jnp.zeros((1, D), F32),
                   CONF_KW - 1 - CONF_KW // 2, conf_starts, D, CW_CONF, flip=True)

    def glu_bwd(ids, t, v):
        dz, pa, pb = t
        sg = _sigmoid(pb)
        d_a = dz * sg
        d_b = dz * pa * sg * (1.0 - sg)
        return [jnp.concatenate([d_a, d_b], axis=1)], [_sum0(d_a), _sum0(d_b)]

    d_pre, d_b1a, d_b1b = _tiled(
        "conf_glu_bwd", glu_bwd, (n_t,), [_rows(d_zg), _rows(pre, D, col=0), _rows(pre, D, col=1)], [],
        [_orow(T, 2 * D, BF16)], [(1, D), (1, D)])
    (dw_pw1,) = _mm("conf_dwpw1", h1, d_pre, _TN, (D // tm, 4, T // tk),
                    pl.BlockSpec((tk, tm), lambda i, j, k: (k, i)),
                    pl.BlockSpec((tk, D // 2), lambda i, j, k: (k, j)),
                    [(_sds((4, D, D // 2), BF16), pl.BlockSpec((None, tm, D // 2), lambda i, j, k: (j, i, 0)))],
                    (tm, D // 2))
    dep = on_grads("conf", (dw_pw1, dw_pw2))
    (dh1,) = _mm("conf_dh", d_pre, w_cf["conf_w_pw1"], _NT, (T // tm, 1, 4),
                 pl.BlockSpec((tm, D // 2), lambda i, j, k: (i, k)),
                 pl.BlockSpec((None, D, D // 2), lambda i, j, k: (k, 0, 0)),
                 [(_sds((T, D), F32), pl.BlockSpec((tm, D), lambda i, j, k: (i, 0)))], (tm, D))
    dx2, d_sh1c, d_sc1c, d_g10 = _norm_bwd("conf", dx3, dh1, 0, x2, g10, m1[1], dep=dep)
    dep = on_later(dx2)

    dx1, dw_in0, dw_out0, dm_mlp0 = _mlp_bwd("mlp0", dx2, mlp0, g01, m0[4], m0[5],
                                             w_m0["w_in"], w_m0["w_out"], dep)
    dep = on_grads("mlp0", (dw_in0, dw_out0))
    d_orec, d_g1r, _ = _gate_bwd("rec", dx1, o_rec, m0[2], dep)
    (dw_rout,) = _mm("rec_dwout", m_rec, d_orec, _TN, (R // RH, 1, T // tk),
                     pl.BlockSpec((tk, RH), lambda i, j, k: (k, i)), pl.BlockSpec((tk, D), lambda i, j, k: (k, 0)),
                     [(_sds((R, D), BF16), pl.BlockSpec((RH, D), lambda i, j, k: (i, 0)))], (RH, D))
    (dm_rec,) = _mm("rec_dm", d_orec, w_rout, _NT, (T // tm, 1, 1),
                    pl.BlockSpec((tm, D), lambda i, j, k: (i, 0)), pl.BlockSpec((R, D), lambda i, j, k: (0, 0)),
                    [(_sds((T, R), F32), pl.BlockSpec((tm, R), lambda i, j, k: (i, 0)))], (tm, R))
    dep = on_later(dm_rec)

    def rec_mid_bwd(ids, t, v):
        dmv, gp, yf, yr = t
        g, th = _gelu(gp)
        lat = ids[0] > 0
        d_gp = jnp.where(lat, dmv * (yf + yr) * _gelu_grad(gp, th), 0.0)
        dy = jnp.where(lat, dmv * g, 0.0)
        return [d_gp, dy], []

    d_a, dy = _tiled("rec_mid_bwd", rec_mid_bwd, (N_SCAN,),
                     [_rows(dm_rec, off=-1, clamp_lo=True), _rows(a_in, R), _rows(y_f), _rows(y_r)], _behind(dep),
                     [(_sds((TA, 2 * R), BF16), pl.BlockSpec((ROW_TILE, R), lambda i: (i, 0))), _orow(TA, R, F32)])
    da_f, db_f, da_r, db_r = _scan_bwd(dy, a_f, y_f, hin_f, a_r, y_r, hin_r)
    d_gpre, d_u, d_gbias, d_lam = _tiled(
        "rg_bwd", _rg_bwd_fn, (TA // RG_TILE,), [_rows(a, tm=RG_TILE) for a in (u, da_f, db_f, da_r, db_r)],
        [wbd, gbias, lam], [_orow(TA, 2 * NQ, BF16, tm=RG_TILE), _orow(TA, R, F32, tm=RG_TILE)],
        [(1, 2 * NQ), (1, 2 * R)], vec_refs=True)
    tk_a = REC_TILE
    d_a = _dwconv("rec_conv_dx", d_u, 0, rec["conv_w"], jnp.zeros((1, R), F32), REC_KW - 1 - 1,
                  rec_starts, R, CW_REC, flip=True, into=(d_a, R // CW_REC))
    (dw_rin,) = _mm("rec_dwin", hcat, d_a, _TN, (D // tm, 4, TA // tk_a),
                    pl.BlockSpec((tk_a, tm), lambda i, j, k: (k, i)), pl.BlockSpec((tk_a, RH), lambda i, j, k: (k, j)),
                    [(_sds((4, D, RH), BF16), pl.BlockSpec((None, tm, RH), lambda i, j, k: (j, i, 0)))], (tm, RH))
    dep = on_grads("rec", (dw_rin, dw_rout))
    (dhcat,) = _mm("rec_dh", d_a, w_rin, _NT, (TA // tm_a, 1, 4),
                   pl.BlockSpec((tm_a, RH), lambda i, j, k: (i, k)),
                   pl.BlockSpec((None, D, RH), lambda i, j, k: (k, 0, 0)),
                   [(_sds((TA, D), F32), pl.BlockSpec((tm_a, D), lambda i, j, k: (i, 0)))], (tm_a, D))
    dx0, d_sh1r, d_sc1r, d_g00 = _norm_bwd("rec", dx1, dhcat, 1, x0, g00, m0[1], dep=dep)
    dep = on_later(dx0)

    d_csh, d_csc, d_g00c = _norm_bwd("ctx", None, dhcat, 0, ctx, g00, csc, with_dx=False, dep=dep)
    blk_mask, blk_spread = _block_mask(), _block_spread()
    (d_wbd,) = _mm("rg_dw", u, d_gpre, _TN, (2, 2, TA // tk_a),
                   pl.BlockSpec((tk_a, RH), lambda i, j, k: (k, i)),
                   pl.BlockSpec((tk_a, NQ // 2), lambda i, j, k: (k, 2 * i + j)),
                   [(_sds((2, 4, RH, BLK), F32), pl.BlockSpec((None, 2, RH, BLK), lambda i, j, k: (i, j, 0, 0)))],
                   (RH, NQ // 2),
                   extra=[(blk_mask, _full_spec(blk_mask)), (blk_spread, _full_spec(blk_spread))]
                   + [(d, _full_spec(d)) for d in _behind(dep)],
                   epi=lambda acc, ex: [jnp.stack([_fold_blocks(acc[:, s * RH:(s + 1) * RH], ex[0], ex[1])
                                                   for s in range(2)])])
    d_cw_rec = _dwconv_wgrad("rec_conv_dw", d_u, a_in, R // CW_REC, REC_KW, 1, rec_starts, R, CW_REC, dep)
    d_cw_conf = _dwconv_wgrad("conf_conv_dw", d_zc, zg, 0, CONF_KW, CONF_KW // 2, conf_starts, D, CW_CONF, dep)

    big = dict(rec_w_in=dw_rin, rec_w_out=dw_rout, conf_w_pw1=dw_pw1, conf_w_pw2=dw_pw2,
               mlp_w_in=(dw_in0, dw_in1), mlp_w_out=(dw_out0, dw_out1))
    d_wa, d_wx = _gate_block_grads(d_wbd)
    d_ba, d_bx = _gate_bias_grads(d_gbias)
    d_mod = jnp.concatenate([
        d_sh1r, d_sc1r, d_g1r, dm_mlp0["sh"], dm_mlp0["sc"], dm_mlp0["gate"],
        d_sh1c, d_sc1c, d_g1c, dm_mlp1["sh"], dm_mlp1["sc"], dm_mlp1["gate"]], axis=1).reshape(2, 6 * D)
    small = dict(
        d_mod=d_mod, d_cmod=jnp.concatenate([d_csh, d_csc], axis=1),
        norm_g=jnp.concatenate([d_g00 + d_g00c, dm_mlp0["g_norm"], d_g10, dm_mlp1["g_norm"]], axis=1),
        rec_conv_w=d_cw_rec[:REC_KW], rec_conv_b=d_cw_rec[REC_KW], rec_lambda=d_lam.reshape(2, R),
        rec_w_a=d_wa, rec_b_a=d_ba, rec_w_x=d_wx, rec_b_x=d_bx,
        conf_b_pw1=jnp.concatenate([d_b1a, d_b1b], axis=1), conf_conv_w=d_cw_conf[:CONF_KW],
        conf_conv_b=d_cw_conf[CONF_KW], conf_ln_g=d_lng, conf_ln_b=d_lnb, conf_b_pw2=d_bpw2, final_g=d_fg)
    return loss.reshape(()), dx0, big, small


_BIG = ("rec_w_in", "rec_w_out", "conf_w_pw1", "conf_w_pw2", "mlp_w_in", "mlp_w_out")


def _halves(w):
    return w.reshape(w.shape[0], 2, w.shape[1] // 2, w.shape[2])


def _ada_fwd(c16, w_ada, b_shard):
    ns = w_ada.shape[2]
    tn = 512

    def kern(c_ref, w_ref, b_ref, o_ref):
        cv = c_ref[...]
        s = (cv * _sigmoid(cv)).astype(BF16)
        o_ref[...] = jnp.dot(s, w_ref[...].astype(BF16), preferred_element_type=F32) + b_ref[...]

    return _pcall(
        kern, name="ada_fwd", grid=(2, ns // tn),
        in_specs=[pl.BlockSpec((16, D), lambda l, j: (0, 0)), pl.BlockSpec((None, D, tn), lambda l, j: (l, 0, j)),
                  pl.BlockSpec((None, 1, tn), lambda l, j: (l, 0, j))],
        out_specs=pl.BlockSpec((None, 16, tn), lambda l, j: (l, 0, j)),
        out_shape=_sds((2, 16, ns), F32), compiler_params=_cparams(),
    )(c16, w_ada, b_shard)


def _ada_bwd(c16, dm16, w_ada):
    ns = w_ada.shape[2]
    tn = 512

    def kern(c_ref, dm_ref, w_ref, gw_ref, ds_ref):
        cv = c_ref[...]
        s = (cv * _sigmoid(cv)).astype(BF16)
        dm = dm_ref[...].astype(BF16)
        gw_ref[...] = lax.dot_general(s, dm, _TN, preferred_element_type=F32)

        @pl.when(jnp.logical_and(pl.program_id(0) == 0, pl.program_id(1) == 0))
        def _():
            ds_ref[...] = jnp.zeros_like(ds_ref)

        ds_ref[...] += lax.dot_general(dm, w_ref[...].astype(BF16), _NT, preferred_element_type=F32)

    return _pcall(
        kern, name="ada_bwd", grid=(2, ns // tn),
        in_specs=[pl.BlockSpec((16, D), lambda l, j: (0, 0)), pl.BlockSpec((None, 16, tn), lambda l, j: (l, 0, j)),
                  pl.BlockSpec((None, D, tn), lambda l, j: (l, 0, j))],
        out_specs=[pl.BlockSpec((None, D, tn), lambda l, j: (l, 0, j)), pl.BlockSpec((16, D), lambda l, j: (0, 0))],
        out_shape=[_sds((2, D, ns), F32), _sds((16, D), F32)], compiler_params=_cparams(),
    )(c16, dm16, w_ada)


def _cctx_grad(ds4, c_ctx):
    def kern(d_ref, c_ref, o_ref):
        tot = d_ref[0, 0:1, :] + d_ref[1, 0:1, :] + d_ref[2, 0:1, :] + d_ref[3, 0:1, :]
        cv = c_ref[...]
        sg = _sigmoid(cv)
        o_ref[...] = tot * (sg * (1.0 + cv * (1.0 - sg)))

    return _pcall(kern, name="cctx_grad", out_shape=_sds((1, D), F32))(ds4, c_ctx.reshape(1, D))


def kernel(x, c, ctx, c_ctx, w_ada, b_ada, norm_g, rec_w_in, rec_conv_w, rec_conv_b, rec_lambda, rec_w_a, rec_b_a, rec_w_x, rec_b_x, rec_w_out, conf_w_pw1, conf_b_pw1, conf_conv_w, conf_conv_b, conf_ln_g, conf_ln_b, conf_w_pw2, conf_b_pw2, mlp_w_in, mlp_w_out, final_g, loss_target, m_c_ctx, m_w_ada, m_b_ada, m_norm_g, m_rec_w_in, m_rec_conv_w, m_rec_conv_b, m_rec_lambda, m_rec_w_a, m_rec_b_a, m_rec_w_x, m_rec_b_x, m_rec_w_out, m_conf_w_pw1, m_conf_b_pw1, m_conf_conv_w, m_conf_conv_b, m_conf_ln_g, m_conf_ln_b, m_conf_w_pw2, m_conf_b_pw2, m_mlp_w_in, m_mlp_w_out, m_final_g, v_c_ctx, v_w_ada, v_b_ada, v_norm_g, v_rec_w_in, v_rec_conv_w, v_rec_conv_b, v_rec_lambda, v_rec_w_a, v_rec_b_a, v_rec_w_x, v_rec_b_x, v_rec_w_out, v_conf_w_pw1, v_conf_b_pw1, v_conf_conv_w, v_conf_conv_b, v_conf_ln_g, v_conf_ln_b, v_conf_w_pw2, v_conf_b_pw2, v_mlp_w_in, v_mlp_w_out, v_final_g):
    names = ["c_ctx", "w_ada", "b_ada", "norm_g", "rec_w_in", "rec_conv_w", "rec_conv_b", "rec_lambda", "rec_w_a",
             "rec_b_a", "rec_w_x", "rec_b_x", "rec_w_out", "conf_w_pw1", "conf_b_pw1", "conf_conv_w", "conf_conv_b",
             "conf_ln_g", "conf_ln_b", "conf_w_pw2", "conf_b_pw2", "mlp_w_in", "mlp_w_out", "final_g"]
    w = dict(zip(names, [c_ctx, w_ada, b_ada, norm_g, rec_w_in, rec_conv_w, rec_conv_b, rec_lambda, rec_w_a,
                         rec_b_a, rec_w_x, rec_b_x, rec_w_out, conf_w_pw1, conf_b_pw1, conf_conv_w, conf_conv_b,
                         conf_ln_g, conf_ln_b, conf_w_pw2, conf_b_pw2, mlp_w_in, mlp_w_out, final_g]))
    m = dict(zip(names, [m_c_ctx, m_w_ada, m_b_ada, m_norm_g, m_rec_w_in, m_rec_conv_w, m_rec_conv_b, m_rec_lambda,
                         m_rec_w_a, m_rec_b_a, m_rec_w_x, m_rec_b_x, m_rec_w_out, m_conf_w_pw1, m_conf_b_pw1,
                         m_conf_conv_w, m_conf_conv_b, m_conf_ln_g, m_conf_ln_b, m_conf_w_pw2, m_conf_b_pw2,
                         m_mlp_w_in, m_mlp_w_out, m_final_g]))
    v = dict(zip(names, [v_c_ctx, v_w_ada, v_b_ada, v_norm_g, v_rec_w_in, v_rec_conv_w, v_rec_conv_b, v_rec_lambda,
                         v_rec_w_a, v_rec_b_a, v_rec_w_x, v_rec_b_x, v_rec_w_out, v_conf_w_pw1, v_conf_b_pw1,
                         v_conf_conv_w, v_conf_conv_b, v_conf_ln_g, v_conf_ln_b, v_conf_w_pw2, v_conf_b_pw2,
                         v_mlp_w_in, v_mlp_w_out, v_final_g]))
    mx, my, mc = _me()
    chip = 2 * mx + my
    me = 4 * mx + 2 * my + mc

    sharded_small = ["norm_g", "rec_conv_w", "rec_lambda", "conf_b_pw1", "conf_conv_w", "conf_conv_b", "conf_ln_g",
                     "conf_ln_b", "conf_b_pw2"]
    packed, offs = _pack([c] + [w[k] for k in sharded_small], 8)
    place = jnp.stack([chip, mc]).astype(jnp.int32)
    shards = [("rec_in", _halves(rec_w_in), 0), ("rec_out", _halves(rec_w_out), 0),
              ("pw1", _halves(conf_w_pw1), 0), ("pw2", _halves(conf_w_pw2), 0),
              ("mlp_in0", _halves(mlp_w_in), 0), ("mlp_in1", _halves(mlp_w_in), 1),
              ("mlp_out0", _halves(mlp_w_out), 0), ("mlp_out1", _halves(mlp_w_out), 1)]
    small_state, small_sent = _send_start("gather_small_start", [packed[None]], _to_all7, [(7,) + packed.shape], place)
    (slot_rin,) = _place_big(shards[:1], place, small_sent)
    flying, gsems, swapping = {}, {}, {}
    flying["rec_in"], gsems["rec_in"], rec_started = _gather_start("gather_start_rec", [slot_rin], ((0,),), small_sent)
    slots = [slot_rin] + _place_big(shards[1:], place, rec_started)
    placed = jnp.broadcast_to(lax.dynamic_slice(slots[-1], (chip, 0, 0, 0), (1, 1, 1, 1)).reshape(1, 1), (8, 1))
    (own,), (landed,) = _send_wait("gather_small_wait", *small_state, _to_all7, placed)
    by_flip = jnp.concatenate([own, landed], axis=0)
    got_flat = jnp.take(by_flip, jnp.arange(8) ^ me, axis=0).reshape(8, -1)

    def piece(i):
        p, n, shape = offs[i]
        return got_flat[:, p:p + n].reshape((8,) + tuple(shape))

    c_rows = piece(0).reshape(8, D)
    full = {}
    for i, k in enumerate(sharded_small):
        per_chip = jnp.moveaxis(piece(1 + i)[0::2], 0, -2)
        full[k] = per_chip.reshape(per_chip.shape[:-2] + (4 * per_chip.shape[-1],))
    c16 = jnp.concatenate([c_rows, c_ctx.reshape(1, D), jnp.zeros((7, D), F32)], axis=0)

    ns = w_ada.shape[2]
    b_shard = lax.dynamic_slice_in_dim(b_ada, chip * ns, ns, axis=1).reshape(2, 1, ns)
    prod = _ada_fwd(c16, w_ada, b_shard)

    own_rows = lax.dynamic_index_in_dim(prod[:, :8].reshape(2, 4, 2, ns), mc, axis=2, keepdims=False)
    rows = jnp.concatenate([own_rows.transpose(1, 0, 2), jnp.broadcast_to(prod[0, 8], (4, 1, ns)),
                            jnp.zeros((4, 5, ns), F32)], axis=1)
    mod_state, mod_started = _send_start("mod_start", [rows], _to_chips, [(3, 8, ns)], place)
    use_order = dict(rec=(0, 1), mlp0=(4, 6), conf=(2, 3), mlp1=(5, 7))
    fetch_order = dict(rec_out=(1,), mlp0=(4, 6), conf=(2, 3), mlp1=(5, 7))
    order = [t for g in fetch_order for t in fetch_order[g]]
    groups = [tuple(order.index(t) for t in fetch_order[g]) for g in fetch_order]
    fly, sems, all_started = _gather_start("gather_start_rest", [slots[t] for t in order], tuple(groups), mod_started)
    for gi, g in enumerate(fetch_order):
        flying[g], gsems[g] = [fly[k] for k in groups[gi]], sems[2 * gi:2 * gi + 2]

    def wg_pre(group, after):
        bufs = _gather_wait(f"gather_wait_{group}", flying[group], *gsems[group], after)
        swapping[group], token = _swap_start(f"swap_start_{group}", bufs, after)
        return token

    def wg(group, after):
        bufs = _swap_wait(f"swap_wait_{group}", *swapping[group], after)
        if group == "rec_in":
            return dict(rec_w_in=bufs[0].reshape(4, D, RH))
        if group == "rec_out":
            return dict(rec_w_out=bufs[0].reshape(R, D))
        if group == "conf":
            return dict(conf_w_pw1=bufs[0].reshape(4, D, D // 2), conf_w_pw2=bufs[1].reshape(D, D))
        return dict(w_in=bufs[0].reshape(4, D, D), w_out=bufs[1].reshape(FF, D))

    (rows,), (landed,) = _send_wait("mod_wait", *mod_state, _to_chips, all_started)
    own = lax.dynamic_index_in_dim(rows, chip, axis=0, keepdims=True)
    by_flip = jnp.concatenate([own, landed[1:2], landed[0:1], landed[2:3]], axis=0)
    by_chip = jnp.take(by_flip, jnp.arange(4) ^ chip, axis=0)
    mods = by_chip[:, :2].transpose(1, 0, 2).reshape(2, 6, D)
    cmods = by_chip[:, 2].reshape(6, D)[:2]

    rec = dict(conv_w=full["rec_conv_w"][0], conv_b=rec_conv_b[0], lam=full["rec_lambda"][0],
               w_a=rec_w_a[0], b_a=rec_b_a[0], w_x=rec_w_x[0], b_x=rec_b_x[0])
    conf = dict(b_pw1=full["conf_b_pw1"][0], conv_w=full["conf_conv_w"][0], conv_b=full["conf_conv_b"][0],
                ln_g=full["conf_ln_g"][0], ln_b=full["conf_ln_b"][0], b_pw2=full["conf_b_pw2"][0])
    pairing, sent, sharing = {}, {}, {}

    def on_grads(group, dws):
        parts = [dw.reshape((4,) + shards[t][1].shape[1:]) for dw, t in zip(dws, use_order[group])]
        pairing[group], token = _reduce_begin(group, parts, place)
        return token

    def finish_pair(after):
        (group, state), = pairing.items()
        pairing.clear()
        sent[group], token = _reduce_mid(group, state, place, after)
        if group == "rec":
            mlp = _reduce_end("mlp1", sent["mlp1"], place, token, layer=(1, 2))
            cf = _reduce_end("conf", sent["conf"], place, token)
            mlp = _reduce_end("mlp0", sent["mlp0"], place, token, layer=(0, 2), into=mlp)
            sharing["state"], token = _share_start("share_start", cf + mlp, place)
        sent["token"] = token
        return token

    loss_local, grad_x, _, small = _local_step(x[0], ctx[0], loss_target[0], mods, cmods, full["norm_g"], final_g,
                                               rec, conf, wg, on_grads, wg_pre, finish_pair)
    rec_sent = sent["token"]
    small["loss"] = loss_local.reshape(1)

    small_names = ["loss", "d_mod", "d_cmod", "norm_g", "rec_conv_w", "rec_conv_b", "rec_lambda", "rec_w_a", "rec_b_a",
                   "rec_w_x", "rec_b_x", "conf_b_pw1", "conf_conv_w", "conf_conv_b", "conf_ln_g", "conf_ln_b",
                   "conf_b_pw2", "final_g"]
    mine = lax.broadcasted_iota(jnp.int32, (8, 1), 0) == me
    mod_slots = jnp.where(mine, small["d_mod"].reshape(1, -1), 0.0)
    spacked, soffs = _pack([small[k] for k in small_names] + [mod_slots])
    def sum_rec(after):
        sharing["rec"], token = _share_start("share_rec_start", _reduce_end("rec", sent["rec"], place, after), place)
        return token

    small_state, small_started = _allreduce_small_begin(spacked, place, rec_sent, sum_rec)

    shared = _share_wait("share_wait", *sharing["state"], small_started)
    delta, new_m, new_v, done = {}, {}, {}, {}

    def adamw_of(k, g, dep=None):
        cols = w[k].shape[-1]
        d_, m_, v_ = _adamw(f"adamw_{k}", w[k].reshape(-1, cols), g.reshape(-1, cols),
                            m[k].reshape(-1, cols), v[k].reshape(-1, cols), dep)
        done[k] = v_
        delta[k], new_m[k], new_v[k] = (a.reshape(w[k].shape) for a in (d_, m_, v_))

    g_big = {}
    for k, g in zip(_BIG[2:], shared):
        g_big[k] = g.reshape(w[k].shape)
        adamw_of(k, g_big[k])
    for k, g in zip(_BIG[:2], _share_wait("share_rec_wait", *sharing["rec"], done["mlp_w_out"])):
        g_big[k] = g.reshape(w[k].shape)
        adamw_of(k, g_big[k])
    unpacked = _unpack(_allreduce_small_end(small_state, [done[k] for k in _BIG]), soffs)
    ssum = dict(zip(small_names, unpacked[:-1]))
    loss = ssum["loss"].reshape(())
    dmod_rows = unpacked[-1].reshape(8, 2, 6 * D).transpose(1, 0, 2)

    d_cmod_full =jnp.concatenate([ssum["d_cmod"].reshape(1, 2 * D), jnp.zeros((1, 4 * D), F32)], axis=1)
    dm16 = jnp.concatenate([dmod_rows, jnp.stack([d_cmod_full, jnp.zeros((1, 6 * D), F32)]),
                            jnp.zeros((2, 7, 6 * D), F32)], axis=1)
    dm16_shard = lax.dynamic_slice_in_dim(dm16, chip * ns, ns, axis=2)
    g_w_ada, ds_part = _ada_bwd(c16, dm16_shard, w_ada)
    ds_state, ds_sent = _send_start("dsilu_start", [jnp.broadcast_to(ds_part[8:16], (4, 8, D))], _to_chips,
                                    [(3, 8, D)], place)
    adamw_of("w_ada", g_w_ada, ds_sent)
    (ds_own,), (ds_landed,) = _send_wait("dsilu_wait", *ds_state, _to_chips, done["w_ada"])
    ds_flip = jnp.concatenate([ds_own[:1], ds_landed[1:2], ds_landed[0:1], ds_landed[2:3]], axis=0)
    g_c_ctx = _cctx_grad(jnp.take(ds_flip, jnp.arange(4) ^ chip, axis=0), c_ctx).reshape(D)
    g_b_ada = ssum["d_mod"] + jnp.stack([d_cmod_full[0], jnp.zeros((6 * D,), F32)])

    def shard_of(a, axis):
        n = a.shape[axis] // 4
        return lax.dynamic_slice_in_dim(a, chip * n, n, axis=axis)

    grads = dict(
        c_ctx=g_c_ctx, w_ada=g_w_ada, b_ada=g_b_ada,
        norm_g=shard_of(ssum["norm_g"].reshape(2, 2, D), 2),
        rec_w_in=g_big["rec_w_in"], rec_conv_w=shard_of(ssum["rec_conv_w"].reshape(1, REC_KW, R), 2),
        rec_conv_b=ssum["rec_conv_b"].reshape(1, R), rec_lambda=shard_of(ssum["rec_lambda"].reshape(1, 2, R), 2),
        rec_w_a=ssum["rec_w_a"].reshape(rec_w_a.shape), rec_b_a=ssum["rec_b_a"].reshape(rec_b_a.shape),
        rec_w_x=ssum["rec_w_x"].reshape(rec_w_x.shape), rec_b_x=ssum["rec_b_x"].reshape(rec_b_x.shape),
        rec_w_out=g_big["rec_w_out"], conf_w_pw1=g_big["conf_w_pw1"],
        conf_b_pw1=shard_of(ssum["conf_b_pw1"].reshape(1, 2 * D), 1),
        conf_conv_w=shard_of(ssum["conf_conv_w"].reshape(1, CONF_KW, D), 2),
        conf_conv_b=shard_of(ssum["conf_conv_b"].reshape(1, D), 1),
        conf_ln_g=shard_of(ssum["conf_ln_g"].reshape(1, D), 1), conf_ln_b=shard_of(ssum["conf_ln_b"].reshape(1, D), 1),
        conf_w_pw2=g_big["conf_w_pw2"], conf_b_pw2=shard_of(ssum["conf_b_pw2"].reshape(1, D), 1),
        mlp_w_in=g_big["mlp_w_in"], mlp_w_out=g_big["mlp_w_out"], final_g=ssum["final_g"].reshape(D))

    rest =[k for k in names if k not in ("w_ada",) + _BIG]
    d_, m_, v_ = _adamw_many("adamw_small", [w[k] for k in rest], [grads[k] for k in rest],
                             [m[k] for k in rest], [v[k] for k in rest])
    for k, dd, mm, vv in zip(rest, d_, m_, v_):
        delta[k], new_m[k], new_v[k] = dd, mm, vv

    return (loss, grad_x[None], *[grads[k] for k in names], *[delta[k] for k in names],
            *[new_m[k] for k in names], *[new_v[k] for k in names])
```

```python
import functools
import math

import jax
import jax.numpy as jnp
from jax import lax
from jax.experimental import pallas as pl
from jax.experimental.pallas import tpu as pltpu

F32 = jnp.float32
BF16 = jnp.bfloat16

D = 1024
T = 2048
TC = 256
TA = T + TC
R = 1280
RH = R // 2
NQ = 4 * RH
FF = 4096
N_BLK = 16
BLK = R // N_BLK
GRID_W = 64
EPS = 1e-6
RG_C = 8.0
CONF_KW = 31
REC_KW = 4
LANE = 128
ROW_TILE = 256
HALO = 16
RG_TILE = 128
PACK_ROWS = 512
N_CHUNK = 512
MM_TILE = 1024
REC_TILE = TA // 2
CW_REC = 640
CW_CONF = 512
V7X_VMEM_BYTES = 64 * 1024 * 1024
VMEM_LIMIT = V7X_VMEM_BYTES - 8 * 1024 * 1024

ADAM_LR = 0.001
ADAM_B1 = 0.9
ADAM_B2 = 0.999
ADAM_EPS = 1e-08
ADAM_WD = 0.01
ADAM_STEP = 10

MESH = pl.DeviceIdType.MESH
ANY = pl.BlockSpec(memory_space=pl.ANY)


def _sds(shape, dtype):
    return jax.ShapeDtypeStruct(tuple(shape), dtype)


def _pcall(body, **kw):
    return pl.pallas_call(body, **kw)


def _cparams():
    return pltpu.CompilerParams(vmem_limit_bytes=VMEM_LIMIT)


def _full_spec(arr):
    nd = arr.ndim
    return pl.BlockSpec(arr.shape, lambda *ids, _n=nd: (0,) * _n)


def _sum0(v):
    return jnp.sum(v, axis=0, keepdims=True)


def _tiled(name, fn, grid, ins, vecs, outs, vec_outs=(), vec_refs=False):
    n_in, n_vec, n_out = len(ins), len(vecs), len(outs)
    n_grid = len(grid)

    def kern(*refs):
        ids = [pl.program_id(a) for a in range(n_grid)]
        tin = [r[...] for r in refs[:n_in]]
        vin = list(refs[n_in:n_in + n_vec]) if vec_refs else [r[...] for r in refs[n_in:n_in + n_vec]]
        o_refs = refs[n_in + n_vec:n_in + n_vec + n_out]
        a_refs = refs[n_in + n_vec + n_out:]
        tout, incs = fn(ids, tin, vin)
        for r, v in zip(o_refs, tout):
            r[...] = v.astype(r.dtype)
        if a_refs:
            first = functools.reduce(jnp.logical_and, [i == 0 for i in ids])

            @pl.when(first)
            def _():
                for r in a_refs:
                    r[...] = jnp.zeros_like(r)

            for r, v in zip(a_refs, incs):
                r[...] += v

    out_shape = [o for o, _ in outs] + [_sds(s, F32) for s in vec_outs]
    out_specs = [s for _, s in outs] + [
        pl.BlockSpec(tuple(s), lambda *ids, _n=len(s): (0,) * _n) for s in vec_outs]
    res = _pcall(
        kern, name=name, grid=tuple(grid),
        in_specs=[s for _, s in ins] + [_full_spec(v) for v in vecs],
        out_specs=out_specs, out_shape=out_shape, compiler_params=_cparams(),
    )(*[a for a, _ in ins], *vecs)
    return list(res)


def _rows(arr, ncols=None, tm=ROW_TILE, off=0, col=0, clamp_lo=False):
    ncols = arr.shape[1] if ncols is None else ncols
    if clamp_lo:
        return arr, pl.BlockSpec((tm, ncols), lambda i: (jnp.maximum(i + off, 0), col))
    return arr, pl.BlockSpec((tm, ncols), lambda i: (i + off, col))


def _orow(nrows, ncols, dtype, tm=ROW_TILE, off=0, clamp_lo=False):
    if clamp_lo:
        return _sds((nrows, ncols), dtype), pl.BlockSpec((tm, ncols), lambda i: (jnp.maximum(i + off, 0), 0))
    return _sds((nrows, ncols), dtype), pl.BlockSpec((tm, ncols), lambda i: (i + off, 0))


_NN = (((1,), (0,)), ((), ()))
_TN = (((0,), (0,)), ((), ()))
_NT = (((1,), (1,)), ((), ()))


def _mm(name, a, b, dims, grid, a_spec, b_spec, out, acc_shape, extra=(), a_pre=None, epi=None):
    n_k = grid[2]
    n_ex = len(extra)
    tn = acc_shape[1]
    cw = N_CHUNK if epi is None and tn > N_CHUNK and tn % N_CHUNK == 0 else tn
    b_cols_axis = 0 if dims[0][1] == (1,) else 1

    def kern(a_ref, b_ref, *rest):
        ex = rest[:n_ex]
        o_refs = rest[n_ex:n_ex + len(out)]
        k = pl.program_id(2)
        av = a_ref[...]
        if a_pre is not None:
            av = a_pre(av)
        av = av.astype(BF16)

        def finish(total, cols):
            if epi is None:
                o_refs[0][:, cols] = total.astype(o_refs[0].dtype)
                return
            for r, v in zip(o_refs, epi(total, [e[...] for e in ex])):
                r[...] = v.astype(r.dtype)

        for c0 in range(0, tn, cw):
            cols = slice(c0, c0 + cw)
            bv = b_ref[cols, :] if b_cols_axis == 0 else b_ref[:, cols]
            part = lax.dot_general(av, bv.astype(BF16), dims, preferred_element_type=F32)
            if n_k == 1:
                finish(part, cols)
            else:
                acc = rest[-1]

                @pl.when(k == 0)
                def _(part=part, cols=cols):
                    acc[:, cols] = part

                @pl.when(jnp.logical_and(k > 0, k < n_k - 1))
                def _(part=part, cols=cols):
                    acc[:, cols] += part

                @pl.when(k == n_k - 1)
                def _(part=part, cols=cols):
                    finish(acc[:, cols] + part, cols)

    res = _pcall(
        kern, name=name, grid=tuple(grid),
        in_specs=[a_spec, b_spec] + [s for _, s in extra],
        out_specs=[s for _, s in out], out_shape=[o for o, _ in out],
        scratch_shapes=[] if n_k == 1 else [pltpu.VMEM(tuple(acc_shape), F32)], compiler_params=_cparams(),
    )(a, b, *[e for e, _ in extra])
    return list(res)


def _rms(x):
    r = lax.rsqrt(jnp.mean(x * x, axis=-1, keepdims=True) + EPS)
    return x * r, r


def _norm_mod(x, g, sc, sh):
    n, _ = _rms(x)
    return (n * g) * (1.0 + sc) + sh


def _norm_mod_bwd(dh, x, g, sc):
    n, r = _rms(x)
    d_sh = _sum0(dh)
    d_sc = _sum0(dh * (n * g))
    d_g = _sum0(dh * (1.0 + sc) * n)
    dn = dh * (g * (1.0 + sc))
    dx = r * (dn - n * jnp.mean(dn * n, axis=-1, keepdims=True))
    return dx, d_sh, d_sc, d_g


_GELU_K = math.sqrt(2.0 / math.pi)


def _gelu(x):
    t = jnp.tanh(_GELU_K * (x + 0.044715 * x * x * x))
    return 0.5 * x * (1.0 + t), t


def _gelu_grad(x, t):
    return 0.5 * (1.0 + t) + 0.5 * x * (1.0 - t * t) * (_GELU_K * (1.0 + 3.0 * 0.044715 * x * x))


def _sigmoid(x):
    return 0.5 * jnp.tanh(0.5 * x) + 0.5


def _expm1(x):
    p = jnp.full_like(x, 1.0 / 5040.0)
    for c in (1.0 / 720.0, 1.0 / 120.0, 1.0 / 24.0, 1.0 / 6.0, 0.5, 1.0):
        p = p * x + c
    return jnp.where(jnp.abs(x) < 0.3, x * p, jnp.exp(x) - 1.0)


def _softplus_neg(lam):
    return jnp.log1p(jnp.exp(-jnp.abs(lam))) + jnp.maximum(-lam, 0.0)


def _layernorm_parts(x):
    mu = jnp.mean(x, axis=-1, keepdims=True)
    xc = x - mu
    rstd = lax.rsqrt(jnp.mean(xc * xc, axis=-1, keepdims=True) + EPS)
    return xc * rstd, rstd


def _rg_gates(u, wbd, gbias, lam):
    sp = _softplus_neg(lam)
    parts = {}
    for h in range(2):
        uh = u[:, h * RH:(h + 1) * RH]
        g = jnp.dot(uh.astype(BF16), wbd[h], preferred_element_type=F32) + gbias[:, h * NQ:(h + 1) * NQ]
        for d in range(2):
            r = _sigmoid(g[:, (2 * d) * RH:(2 * d + 1) * RH])
            i = _sigmoid(g[:, (2 * d + 1) * RH:(2 * d + 2) * RH])
            sph = sp[d:d + 1, h * RH:(h + 1) * RH]
            la = (-RG_C) * r * sph
            e2 = _expm1(2.0 * la)
            inv_mult = jnp.where(e2 < 0.0, lax.rsqrt(-e2), 0.0)
            parts[(d, h)] = dict(r=r, i=i, la=la, a=jnp.exp(la), e2=e2, mult=-e2 * inv_mult, inv_mult=inv_mult,
                                 uh=uh, sp=sph)
    return parts


def _rg_fwd_fn(ids, tin, vin):
    (u,) = tin
    wbd = vin[0]
    parts = _rg_gates(u, wbd, vin[1][...], vin[2][...])
    outs = []
    for d in range(2):
        a = jnp.concatenate([parts[(d, h)]["a"] for h in range(2)], axis=1)
        b = jnp.concatenate([parts[(d, h)]["mult"] * parts[(d, h)]["i"] * parts[(d, h)]["uh"]
                             for h in range(2)], axis=1)
        outs += [a, b]
    return outs, []


def _rg_bwd_fn(ids, tin, vin):
    u, da_f, db_f, da_r, db_r = tin
    wbd, lam = vin[0], vin[2][...]
    parts = _rg_gates(u, wbd, vin[1][...], lam)
    dab = ((da_f, db_f), (da_r, db_r))
    dsig_lam = -1.0 / (1.0 + jnp.exp(lam))
    du_halves, dpre_halves, dlam = [], [], [[None, None], [None, None]]
    for h in range(2):
        du = jnp.zeros_like(parts[(0, h)]["uh"])
        dpre = []
        for d in range(2):
            p = parts[(d, h)]
            da = dab[d][0][:, h * RH:(h + 1) * RH]
            db = dab[d][1][:, h * RH:(h + 1) * RH]
            d_mult = db * p["i"] * p["uh"]
            d_i = db * p["mult"] * p["uh"]
            du = du + db * p["mult"] * p["i"]
            d_la = da * p["a"] - d_mult * (p["e2"] + 1.0) * p["inv_mult"]
            d_r = d_la * ((-RG_C) * p["sp"])
            dlam[d][h] = _sum0(d_la * ((-RG_C) * p["r"])) * dsig_lam[d:d + 1, h * RH:(h + 1) * RH]
            dpre += [d_r * p["r"] * (1.0 - p["r"]), d_i * p["i"] * (1.0 - p["i"])]
        dpre = jnp.concatenate(dpre, axis=1)
        du = du + lax.dot_general(dpre.astype(BF16), wbd[h], _NT, preferred_element_type=F32)
        du_halves.append(du)
        dpre_halves.append(dpre)
    dpre_all = jnp.concatenate(dpre_halves, axis=1)
    dlam_row = jnp.concatenate([dlam[0][0], dlam[0][1], dlam[1][0], dlam[1][1]], axis=1)
    return [dpre_all, jnp.concatenate(du_halves, axis=1)], [_sum0(dpre_all), dlam_row]


def _tile_flags(i, n_tiles, seq_starts):
    starts_here = functools.reduce(jnp.logical_or, [i == s for s in seq_starts])
    ends_here = functools.reduce(jnp.logical_or, [i + 1 == s for s in seq_starts] + [i + 1 == n_tiles])
    return jnp.logical_not(starts_here), jnp.logical_not(ends_here)


def _halo_specs(col0, cw):
    hb = ROW_TILE // HALO
    prev = pl.BlockSpec((HALO, cw), lambda i, c: (jnp.maximum(i * hb - 1, 0), col0 + c))
    cur = pl.BlockSpec((ROW_TILE, cw), lambda i, c: (i, col0 + c))
    return prev, cur, hb


def _window(prev_ref, cur_ref, next_ref, has_prev, has_next):
    prev = jnp.where(has_prev, prev_ref[...], 0.0)
    nxt = jnp.where(has_next, next_ref[...], 0.0)
    return jnp.concatenate([prev, cur_ref[...], nxt], axis=0)


def _tap_reader(win):
    sub = 8
    n = win.shape[0]
    shifted = {0: win}

    def tap(off):
        s = off % sub
        if s not in shifted:
            shifted[s] = pltpu.roll(win, n - s, axis=0)
        return shifted[s][off - s:off - s + ROW_TILE, :]

    return tap


def _dwconv(name, x, col0, w, bias, pad_left, seq_starts, n_ch, cw=256, flip=False, into=None):
    n_rows = x.shape[0]
    n_tiles = n_rows // ROW_TILE
    n_taps = w.shape[0]
    prev_spec, cur_spec, hb = _halo_specs(col0, cw)
    last_hb = n_rows // HALO - 1
    next_spec = pl.BlockSpec((HALO, cw), lambda i, c: (jnp.minimum((i + 1) * hb, last_hb), col0 + c))
    dest, out_col0 = (None, 0) if into is None else into

    def kern(prev_ref, cur_ref, next_ref, w_ref, b_ref, *rest):
        o_ref = rest[-1]
        has_prev, has_next = _tile_flags(pl.program_id(0), n_tiles, seq_starts)
        win = _window(prev_ref, cur_ref, next_ref, has_prev, has_next)
        tap = _tap_reader(win)
        wv = w_ref[...]
        acc = jnp.zeros((ROW_TILE, cw), F32) + b_ref[...]
        for k in range(n_taps):
            kw = n_taps - 1 - k if flip else k
            acc = acc + wv[kw:kw + 1, :] * tap(HALO + k - pad_left)
        o_ref[...] = acc.astype(o_ref.dtype)

    return _pcall(
        kern, name=name, grid=(n_tiles, n_ch // cw),
        in_specs=[prev_spec, cur_spec, next_spec,
                  pl.BlockSpec((n_taps, cw), lambda i, c: (0, c)), pl.BlockSpec((1, cw), lambda i, c: (0, c))]
        + ([] if dest is None else [ANY]),
        out_specs=pl.BlockSpec((ROW_TILE, cw), lambda i, c: (i, out_col0 + c)),
        out_shape=_sds((n_rows, n_ch), F32) if dest is None else _sds(dest.shape, dest.dtype),
        input_output_aliases={} if dest is None else {5: 0}, compiler_params=_cparams(),
    )(x, x, x, w, bias, *([] if dest is None else [dest]))


def _dwconv_wgrad(name, dy, x, col0, n_taps, pad_left, seq_starts, n_ch, cw=256, dep=None):
    deps = [] if dep is None else [dep]
    n_rows = dy.shape[0]
    n_tiles = n_rows // ROW_TILE
    n_out = -(-(n_taps + 1) // 8) * 8
    prev_spec, cur_spec, hb = _halo_specs(col0, cw)
    last_hb = n_rows // HALO - 1
    next_spec = pl.BlockSpec((HALO, cw), lambda c, i: (jnp.minimum((i + 1) * hb, last_hb), col0 + c))
    prev_spec = pl.BlockSpec((HALO, cw), lambda c, i: (jnp.maximum(i * hb - 1, 0), col0 + c))
    cur_spec = pl.BlockSpec((ROW_TILE, cw), lambda c, i: (i, col0 + c))

    def kern(dy_ref, prev_ref, cur_ref, next_ref, *rest):
        o_ref = rest[-1]
        i = pl.program_id(1)
        has_prev, has_next = _tile_flags(i, n_tiles, seq_starts)
        win = _window(prev_ref, cur_ref, next_ref, has_prev, has_next)
        dyv = dy_ref[...]
        tap = _tap_reader(win)
        rid = lax.broadcasted_iota(jnp.int32, (n_out, cw), 0)
        inc = jnp.where(rid == n_taps, _sum0(dyv), 0.0)
        for k in range(n_taps):
            inc = inc + jnp.where(rid == k, _sum0(dyv * tap(HALO + k - pad_left)), 0.0)

        @pl.when(i == 0)
        def _():
            o_ref[...] = jnp.zeros_like(o_ref)

        o_ref[...] += inc

    return _pcall(
        kern, name=name, grid=(n_ch // cw, n_tiles),
        in_specs=[pl.BlockSpec((ROW_TILE, cw), lambda c, i: (i, c)), prev_spec, cur_spec, next_spec]
        + [pl.BlockSpec(d.shape, lambda c, i: (0, 0)) for d in deps],
        out_specs=pl.BlockSpec((n_out, cw), lambda c, i: (0, c)),
        out_shape=_sds((n_out, n_ch), F32), compiler_params=_cparams(),
    )(dy, x, x, x, *deps)


N_SCAN = TA // ROW_TILE


def _rev_block(j):
    return jnp.where(j == 0, 0, N_SCAN - j)


def _scan_fwd(a_f, b_f, a_r, b_r):
    fwd_spec = pl.BlockSpec((ROW_TILE, R), lambda i: (i, 0))
    rev_spec = pl.BlockSpec((ROW_TILE, R), lambda i: (_rev_block(i), 0))
    hin_spec = pl.BlockSpec((None, 1, R), lambda i: (i, 0, 0))

    def kern(af, bf, ar, br, yf, yr, hin_f, hin_r, hf_s, hr_s):
        @pl.when(pl.program_id(0) == 0)
        def _():
            hf_s[...] = jnp.zeros_like(hf_s)
            hr_s[...] = jnp.zeros_like(hr_s)

        hin_f[...] = hf_s[...]
        hin_r[...] = hr_s[...]

        def step(s8, carry):
            hf, hr = carry
            t0 = pl.multiple_of(s8 * 8, 8)
            for q in range(8):
                tf = t0 + q
                hf = af[pl.ds(tf, 1), :] * hf + bf[pl.ds(tf, 1), :]
                yf[pl.ds(tf, 1), :] = hf
                tr = ROW_TILE - 1 - tf
                hr = ar[pl.ds(tr, 1), :] * hr + br[pl.ds(tr, 1), :]
                yr[pl.ds(tr, 1), :] = hr
            return hf, hr

        hf, hr = lax.fori_loop(0, ROW_TILE // 8, step, (hf_s[...], hr_s[...]))
        hf_s[...] = hf
        hr_s[...] = hr

    return _pcall(
        kern, name="scan_fwd", grid=(N_SCAN,),
        in_specs=[fwd_spec, fwd_spec, rev_spec, rev_spec],
        out_specs=[fwd_spec, rev_spec, hin_spec, hin_spec],
        out_shape=[_sds((TA, R), F32), _sds((TA, R), F32), _sds((N_SCAN, 1, R), F32), _sds((N_SCAN, 1, R), F32)],
        scratch_shapes=[pltpu.VMEM((1, R), F32), pltpu.VMEM((1, R), F32)], compiler_params=_cparams(),
    )(a_f, b_f, a_r, b_r)


def _scan_bwd(dy, a_f, y_f, hin_f, a_r, y_r, hin_r):
    fwd_spec = pl.BlockSpec((ROW_TILE, R), lambda i: (N_SCAN - 1 - i, 0))
    rev_spec = pl.BlockSpec((ROW_TILE, R), lambda i: (_rev_block(N_SCAN - 1 - i), 0))
    hin_spec = pl.BlockSpec((None, 1, R), lambda i: (N_SCAN - 1 - i, 0, 0))
    last = ROW_TILE - 1

    def kern(dyf, af, yf, hf0, dyr, ar, yr, hr0, daf, dbf, dar, dbr, gf_s, anf_s, gr_s, anr_s):
        @pl.when(pl.program_id(0) == 0)
        def _():
            for r in (gf_s, anf_s, gr_s, anr_s):
                r[...] = jnp.zeros_like(r)

        def one(dy_ref, a_ref, y_ref, da_ref, db_ref, g, an, p, pprev):
            gnew = dy_ref[pl.ds(p, 1), :] + an * g
            db_ref[pl.ds(p, 1), :] = gnew
            da_ref[pl.ds(p, 1), :] = gnew * y_ref[pl.ds(pprev, 1), :]
            return gnew, a_ref[pl.ds(p, 1), :]

        def step(s8, carry):
            gf, anf, gr, anr = carry
            base = s8 * 8
            for q in range(8):
                s = last - (base + q)
                gf, anf = one(dyf, af, yf, daf, dbf, gf, anf, s, s - 1)
                gr, anr = one(dyr, ar, yr, dar, dbr, gr, anr, last - s, last - s + 1)
            return gf, anf, gr, anr

        carry = (gf_s[...], anf_s[...], gr_s[...], anr_s[...])
        carry = lax.fori_loop(0, ROW_TILE // 8 - 1, step, carry)
        gf, anf, gr, anr = carry
        for s in range(7, 0, -1):
            gf, anf = one(dyf, af, yf, daf, dbf, gf, anf, s, s - 1)
            gr, anr = one(dyr, ar, yr, dar, dbr, gr, anr, last - s, last - s + 1)
        gf0 = dyf[0:1, :] + anf * gf
        dbf[0:1, :] = gf0
        daf[0:1, :] = gf0 * hf0[...]
        gr0 = dyr[last:last + 1, :] + anr * gr
        dbr[last:last + 1, :] = gr0
        dar[last:last + 1, :] = gr0 * hr0[...]
        gf_s[...] = gf0
        anf_s[...] = af[0:1, :]
        gr_s[...] = gr0
        anr_s[...] = ar[last:last + 1, :]

    return _pcall(
        kern, name="scan_bwd", grid=(N_SCAN,),
        in_specs=[fwd_spec, fwd_spec, fwd_spec, hin_spec, rev_spec, rev_spec, rev_spec, hin_spec],
        out_specs=[fwd_spec, fwd_spec, rev_spec, rev_spec],
        out_shape=[_sds((TA, R), F32)] * 4,
        scratch_shapes=[pltpu.VMEM((1, R), F32)] * 4, compiler_params=_cparams(),
    )(dy, a_f, y_f, hin_f, dy, a_r, y_r, hin_r)


def _me():
    return lax.axis_index("x"), lax.axis_index("y"), lax.axis_index("c")


def _other_chips(mx, my):
    return [(1 - mx, my), (mx, 1 - my), (1 - mx, 1 - my)]


def _rcopy(src, dst, ssem, rsem, dev):
    return pltpu.make_async_remote_copy(src_ref=src, dst_ref=dst, send_sem=ssem, recv_sem=rsem,
                                        device_id=dev, device_id_type=MESH)


def _peers7(mx, my, mc):
    peers = []
    for k in range(1, 8):
        peers.append((1 - mx if (k >> 2) & 1 else mx, 1 - my if (k >> 1) & 1 else my, 1 - mc if k & 1 else mc))
    return peers


def _halves_of(refs, c):
    out = []
    for r in refs:
        out += [r.at[c]] if len(r.shape) == 3 else [r.at[l, c] for l in range(r.shape[0])]
    return out


def _share_start(name, fulls, after):
    n = len(fulls)
    n_cp = sum(1 if f.ndim == 3 else f.shape[0] for f in fulls)

    def kern(*refs):
        o = refs[n + 1:2 * n + 1]
        ssem, rsem, token = refs[2 * n + 1:]
        mx, my, mc = _me()
        for q, half in enumerate(_halves_of(o, mc)):
            _rcopy(half, half, ssem.at[q], rsem.at[q], (mx, my, 1 - mc)).start()
        token[...] = jnp.zeros_like(token)

    dma = pltpu.SemaphoreType.DMA
    res = _pcall(
        kern, name=name, in_specs=[ANY] * (n + 1),
        out_specs=[ANY] * n + [SEM, SEM, pl.BlockSpec(memory_space=pltpu.VMEM)],
        out_shape=[_sds(f.shape, f.dtype) for f in fulls] + [dma((n_cp,)), dma((n_cp,)), _sds((8, LANE), F32)],
        input_output_aliases={t: t for t in range(n)},
        compiler_params=pltpu.CompilerParams(has_side_effects=_DATAFLOW),
    )(*fulls, after)
    return (list(res[:n]), res[n], res[n + 1]), res[n + 2]


def _share_wait(name, fulls, ssem, rsem, after):
    n = len(fulls)

    def kern(*refs):
        o = refs[:n]
        ssem_ref, rsem_ref = refs[n], refs[n + 1]
        mx, my, mc = _me()
        sib = (mx, my, 1 - mc)
        for q, (theirs, mine) in enumerate(zip(_halves_of(o, 1 - mc), _halves_of(o, mc))):
            _rcopy(theirs, theirs, ssem_ref.at[q], rsem_ref.at[q], sib).wait_recv()
            _rcopy(mine, mine, ssem_ref.at[q], rsem_ref.at[q], sib).wait_send()

    return list(_pcall(
        kern, name=name, in_specs=[ANY] * n + [SEM, SEM, ANY], out_specs=[ANY] * n,
        out_shape=[_sds(f.shape, f.dtype) for f in fulls], input_output_aliases={t: t for t in range(n)},
        compiler_params=pltpu.CompilerParams(has_side_effects=_DATAFLOW),
    )(*fulls, ssem, rsem, after))


def _tiled_sp(name, fn, grid, sp, ins, outs, into=None):
    n_in = len(ins)
    dest = [] if into is None else [into]

    def kern(sp_ref, *refs):
        tout = fn([r[...] for r in refs[:n_in]])
        for r, v in zip(refs[n_in + len(dest):], tout):
            r[...] = v.astype(r.dtype)

    gs = pltpu.PrefetchScalarGridSpec(num_scalar_prefetch=1, grid=tuple(grid),
                                      in_specs=[s for _, s in ins] + [ANY] * len(dest), out_specs=[s for _, s in outs])
    res = _pcall(kern, name=name, grid_spec=gs, out_shape=[o for o, _ in outs], compiler_params=_cparams(),
                 input_output_aliases={1 + n_in: 0} if dest else {})(sp, *[a for a, _ in ins], *dest)
    return list(res)


def _row_tile(rows, cols, itemsize=4, budget=2 * 1024 * 1024):
    tr = rows
    while tr * cols * itemsize > budget and tr % 32 == 0:
        tr //= 2
    return tr


def _place_big(shards, place, dep=None):
    slots = []
    for tag, s, layer in shards:
        rr, cc = s.shape[2], s.shape[3]
        tr = _row_tile(rr, cc)
        (slot,) = _tiled_sp(
            f"place_{tag}", lambda tin: [tin[0]], (2, rr // tr), place,
            [(s, pl.BlockSpec((None, None, tr, cc), lambda h, i, sp, layer=layer: (layer, h, i, 0)))]
            + [(d, pl.BlockSpec(d.shape, lambda h, i, sp: (0, 0))) for d in _behind(dep)],
            [(_sds((4, 2, rr, cc), BF16), pl.BlockSpec((None, None, tr, cc), lambda h, i, sp: (sp[0], h, i, 0)))])
        slots.append(slot)
    return slots


def _allreduce_small_begin(vec, place, after, during):
    hr = vec.shape[0] // 2
    tr = _row_tile(hr, LANE)
    blk = (None, None, tr, LANE)
    (pair,) = _tiled_sp(
        "small_place", lambda tin: [tin[0]], (2, hr // tr), place,
        [(vec.reshape(2, hr, LANE), pl.BlockSpec((None, tr, LANE), lambda h, i, sp: (h, i, 0)))],
        [(_sds((2, 2, hr, LANE), F32), pl.BlockSpec(blk, lambda h, i, sp: (sp[1], h, i, 0)))])
    crossing, token = _share_start("small_share_start", [pair.reshape(2, 2 * hr, LANE)], place)
    (pair,) = _share_wait("small_share_wait", *crossing, during(token))
    pair = pair.reshape(2, 2, hr, LANE)
    (slot,) = _tiled_sp(
        "small_pair_add", lambda tin: [tin[0] + tin[1]], (2, hr // tr), place,
        [(pair, pl.BlockSpec(blk, lambda h, i, sp: (0, h, i, 0))),
         (pair, pl.BlockSpec(blk, lambda h, i, sp: (1, h, i, 0)))],
        [(_sds((4, 2, hr, LANE), F32), pl.BlockSpec(blk, lambda h, i, sp: (sp[0], h, i, 0)))])
    fly, sems, token = _gather_start("small_start", [slot], ((0,),), after)
    return (fly, sems), token


def _allreduce_small_end(state, after):
    fly, sems = state
    (chips,) = _swap_halves("small_swap", _gather_wait("small_wait", fly, *sems, after))
    hr = chips.shape[2]
    tr = _row_tile(hr, LANE)
    blk = (None, None, tr, LANE)
    (total,) = _tiled(
        "small_chip_sum", lambda ids, tin, vin: ([((tin[0] + tin[1]) + tin[2]) + tin[3]], []), (2, hr // tr),
        [(chips, pl.BlockSpec(blk, lambda h, i, _j=j: (_j, h, i, 0))) for j in range(4)], [],
        [(_sds((2, hr, LANE), F32), pl.BlockSpec((None, tr, LANE), lambda h, i: (h, i, 0)))])
    return total.reshape(2 * hr, LANE)


SEM =pl.BlockSpec(memory_space=pltpu.SEMAPHORE)
_DATAFLOW = pltpu.SideEffectType.DATAFLOW_SIDE_EFFECTING


def _gather_start(name, slots, groups, after):
    n = len(slots)

    def kern(*refs):
        o = refs[n + 1:2 * n + 1]
        sems, token = refs[2 * n + 1:-1], refs[-1]
        mx, my, mc = _me()
        j0 = 2 * mx + my
        for gi, grp in enumerate(groups):
            for k, t in enumerate(grp):
                for q, (qx, qy) in enumerate(_other_chips(mx, my)):
                    _rcopy(o[t].at[j0, mc], o[t].at[j0, mc], sems[2 * gi].at[3 * k + q],
                           sems[2 * gi + 1].at[3 * k + q], (qx, qy, mc)).start()
        token[...] = jnp.zeros_like(token)

    sem_shapes = []
    for grp in groups:
        sem_shapes += [pltpu.SemaphoreType.DMA((3 * len(grp),))] * 2
    res = _pcall(
        kern, name=name, in_specs=[ANY] * (n + 1),
        out_specs=[ANY] * n + [SEM] * len(sem_shapes) + [pl.BlockSpec(memory_space=pltpu.VMEM)],
        out_shape=[_sds(w.shape, w.dtype) for w in slots] + sem_shapes + [_sds((8, LANE), F32)],
        input_output_aliases={t: t for t in range(n)},
        compiler_params=pltpu.CompilerParams(has_side_effects=_DATAFLOW),
    )(*slots, after)
    return list(res[:n]), list(res[n:-1]), res[-1]


def _gather_wait(name, bufs, ssem, rsem, after):
    n = len(bufs)
    afters = list(after) if isinstance(after, (list, tuple)) else [after]

    def kern(*refs):
        b = refs[:n]
        ssem_ref, rsem_ref = refs[n], refs[n + 1]
        mx, my, mc = _me()
        j0 = 2 * mx + my
        for k in range(n):
            for q, (qx, qy) in enumerate(_other_chips(mx, my)):
                jq = 2 * qx + qy
                _rcopy(b[k].at[jq, mc], b[k].at[jq, mc], ssem_ref.at[3 * k + q], rsem_ref.at[3 * k + q],
                       (qx, qy, mc)).wait_recv()
                _rcopy(b[k].at[j0, mc], b[k].at[j0, mc], ssem_ref.at[3 * k + q], rsem_ref.at[3 * k + q],
                       (qx, qy, mc)).wait_send()

    return list(_pcall(
        kern, name=name, in_specs=[ANY] * n + [SEM, SEM] + [ANY] * len(afters), out_specs=[ANY] * n,
        out_shape=[_sds(w.shape, w.dtype) for w in bufs], input_output_aliases={k: k for k in range(n)},
        compiler_params=pltpu.CompilerParams(has_side_effects=_DATAFLOW),
    )(*bufs, ssem, rsem, *afters))


def _swap_halves(name, bufs):
    n = len(bufs)

    def kern(*refs):
        o = refs[n:2 * n]
        ss, rs = refs[2 * n:]
        mx, my, mc = _me()
        sib = (mx, my, 1 - mc)
        sends = []
        for k in range(n):
            for q, (qx, qy) in enumerate(_other_chips(mx, my)):
                jq = 2 * qx + qy
                cp = _rcopy(o[k].at[jq, mc], o[k].at[jq, mc], ss.at[3 * k + q], rs.at[3 * k + q], sib)
                cp.start()
                sends.append(cp)
        for k in range(n):
            for q, (qx, qy) in enumerate(_other_chips(mx, my)):
                jq = 2 * qx + qy
                _rcopy(o[k].at[jq, 1 - mc], o[k].at[jq, 1 - mc], ss.at[3 * k + q], rs.at[3 * k + q], sib).wait_recv()
        for cp in sends:
            cp.wait_send()

    dma = pltpu.SemaphoreType.DMA
    return list(_pcall(
        kern, name=name, in_specs=[ANY] * n, out_specs=[ANY] * n,
        out_shape=[_sds(w.shape, w.dtype) for w in bufs], input_output_aliases={k: k for k in range(n)},
        scratch_shapes=[dma((3 * n,)), dma((3 * n,))],
    )(*bufs))


def _swap_start(name, bufs, after):
    n = len(bufs)

    def kern(*refs):
        o = refs[n + 1:2 * n + 1]
        ssem, rsem, token = refs[2 * n + 1:]
        mx, my, mc = _me()
        for k in range(n):
            for q, (qx, qy) in enumerate(_other_chips(mx, my)):
                jq = 2 * qx + qy
                _rcopy(o[k].at[jq, mc], o[k].at[jq, mc], ssem.at[3 * k + q], rsem.at[3 * k + q], (mx, my, 1 - mc)).start()
        token[...] = jnp.zeros_like(token)

    dma = pltpu.SemaphoreType.DMA
    res = _pcall(
        kern, name=name, in_specs=[ANY] * (n + 1),
        out_specs=[ANY] * n + [SEM, SEM, pl.BlockSpec(memory_space=pltpu.VMEM)],
        out_shape=[_sds(w.shape, w.dtype) for w in bufs] + [dma((3 * n,)), dma((3 * n,)), _sds((8, LANE), F32)],
        input_output_aliases={k: k for k in range(n)},
        compiler_params=pltpu.CompilerParams(has_side_effects=_DATAFLOW),
    )(*bufs, after)
    return (list(res[:n]), res[n], res[n + 1]), res[n + 2]


def _swap_wait(name, bufs, ssem, rsem, after):
    n = len(bufs)

    def kern(*refs):
        b = refs[:n]
        ssem_ref, rsem_ref = refs[n], refs[n + 1]
        mx, my, mc = _me()
        sib = (mx, my, 1 - mc)
        for k in range(n):
            for q, (qx, qy) in enumerate(_other_chips(mx, my)):
                jq = 2 * qx + qy
                _rcopy(b[k].at[jq, 1 - mc], b[k].at[jq, 1 - mc], ssem_ref.at[3 * k + q], rsem_ref.at[3 * k + q],
                       sib).wait_recv()
                _rcopy(b[k].at[jq, mc], b[k].at[jq, mc], ssem_ref.at[3 * k + q], rsem_ref.at[3 * k + q],
                       sib).wait_send()

    return list(_pcall(
        kern, name=name, in_specs=[ANY] * n + [SEM, SEM, ANY], out_specs=[ANY] * n,
        out_shape=[_sds(w.shape, w.dtype) for w in bufs], input_output_aliases={k: k for k in range(n)},
        compiler_params=pltpu.CompilerParams(has_side_effects=_DATAFLOW),
    )(*bufs, ssem, rsem, after))


def _to_sibling(mx, my, mc):
    return [((j, 1 - mc), j, (mx, my, 1 - mc)) for j in range(4)]


def _to_chips(mx, my, mc):
    return [((2 * qx + qy,), q, (qx, qy, mc)) for q, (qx, qy) in enumerate(_other_chips(mx, my))]


def _to_all7(mx, my, mc):
    return [((0,), k, dev) for k, dev in enumerate(_peers7(mx, my, mc))]


def _send_start(name, srcs, plan, land_shapes, after):
    n = len(srcs)
    per = len(plan(0, 0, 0))

    def kern(*refs):
        s, land = refs[n + 1:2 * n + 1], refs[2 * n + 1:3 * n + 1]
        ssem, rsem, token = refs[3 * n + 1:]
        for k in range(n):
            for q, (idx, slot, dev) in enumerate(plan(*_me())):
                _rcopy(s[k].at[idx], land[k].at[slot], ssem.at[per * k + q], rsem.at[per * k + q], dev).start()
        token[...] = jnp.zeros_like(token)

    dma = pltpu.SemaphoreType.DMA
    res = _pcall(
        kern, name=name, in_specs=[ANY] * (n + 1),
        out_specs=[ANY] * (2 * n) + [SEM, SEM, pl.BlockSpec(memory_space=pltpu.VMEM)],
        out_shape=[_sds(s.shape, s.dtype) for s in srcs] + [_sds(ls, s.dtype) for ls, s in zip(land_shapes, srcs)]
        + [dma((per * n,)), dma((per * n,)), _sds((8, LANE), F32)],
        input_output_aliases={k: k for k in range(n)},
        compiler_params=pltpu.CompilerParams(has_side_effects=_DATAFLOW),
    )(*srcs, after)
    return (list(res[:n]), list(res[n:2 * n]), res[2 * n], res[2 * n + 1]), res[2 * n + 2]


def _send_wait(name, srcs, lands, ssem, rsem, plan, after):
    n = len(srcs)
    per = len(plan(0, 0, 0))

    def kern(*refs):
        s, land = refs[:n], refs[n:2 * n]
        ssem_ref, rsem_ref = refs[2 * n], refs[2 * n + 1]
        for k in range(n):
            for q, (idx, slot, dev) in enumerate(plan(*_me())):
                cp = _rcopy(s[k].at[idx], land[k].at[slot], ssem_ref.at[per * k + q], rsem_ref.at[per * k + q], dev)
                cp.wait_recv()
                cp.wait_send()

    res = _pcall(
        kern, name=name, in_specs=[ANY] * (2 * n) + [SEM, SEM, ANY], out_specs=[ANY] * (2 * n),
        out_shape=[_sds(a.shape, a.dtype) for a in list(srcs) + list(lands)],
        input_output_aliases={k: k for k in range(2 * n)},
        compiler_params=pltpu.CompilerParams(has_side_effects=_DATAFLOW),
    )(*srcs, *lands, ssem, rsem, after)
    return list(res[:n]), list(res[n:])


def _reduce_begin(tag, parts, after):
    return _send_start(f"pair_start_{tag}", parts, _to_sibling, [(4,) + p.shape[2:] for p in parts], after)


def _reduce_mid(tag, pairing, place, after):
    parts, theirs = _send_wait(f"pair_wait_{tag}", *pairing, _to_sibling, after)
    sums = []
    for k, (p, o) in enumerate(zip(parts, theirs)):
        rr, cc = p.shape[2], p.shape[3]
        tr = _row_tile(rr, cc)
        (s_k,) = _tiled_sp(
            f"pair_add_{tag}{k}", lambda tin: [tin[0].astype(F32) + tin[1].astype(F32)], (4, rr // tr), place,
            [(p, pl.BlockSpec((None, None, tr, cc), lambda j, i, sp: (j, sp[1], i, 0))),
             (o, pl.BlockSpec((None, tr, cc), lambda j, i, sp: (j, i, 0)))],
            [(_sds((4, rr, cc), BF16), pl.BlockSpec((None, tr, cc), lambda j, i, sp: (j, i, 0)))])
        sums.append(s_k)
    return _send_start(f"chips_start_{tag}", sums, _to_chips, [(3,) + s.shape[1:] for s in sums], theirs[0])


def _reduce_end(tag, flying, place, after, layer=None, into=None):
    sums, lands = _send_wait(f"chips_wait_{tag}", *flying, _to_chips, after)
    fulls = []
    for k, (s, q) in enumerate(zip(sums, lands)):
        rr, cc = q.shape[1], q.shape[2]
        tr = _row_tile(rr, cc)

        def add4(tin):
            return [((tin[0].astype(F32) + tin[1].astype(F32)) + tin[2].astype(F32)) + tin[3].astype(F32)]

        ins = [(s, pl.BlockSpec((None, tr, cc), lambda i, sp: (sp[0], i, 0)))]
        ins += [(q, pl.BlockSpec((None, tr, cc), lambda i, sp, _k=kk: (_k, i, 0))) for kk in range(3)]
        if layer is None:
            out = (_sds((2, rr, cc), F32), pl.BlockSpec((None, tr, cc), lambda i, sp: (sp[1], i, 0)))
        else:
            out = (_sds((layer[1], 2, rr, cc), F32),
                   pl.BlockSpec((None, None, tr, cc), lambda i, sp, _l=layer[0]: (_l, sp[1], i, 0)))
        (f_k,) = _tiled_sp(f"chip_add_{tag}{k}", add4, (rr // tr,), place, ins, [out],
                           None if into is None else into[k])
        fulls.append(f_k)
    return fulls


def _pack(parts, PACK_ROWS=PACK_ROWS):
    flat, offs, pos = [], [], 0
    for p in parts:
        v = p.reshape(-1).astype(F32)
        n = -(-v.shape[0] // LANE) * LANE
        flat.append(jnp.pad(v, (0, n - v.shape[0])))
        offs.append((pos, v.shape[0], p.shape))
        pos += n
    total = -(-pos // (PACK_ROWS * LANE)) * PACK_ROWS * LANE
    flat.append(jnp.zeros((total - pos,), F32))
    return jnp.concatenate(flat).reshape(-1, LANE), offs


def _unpack(vec, offs):
    v = vec.reshape(-1)
    return [v[p:p + n].reshape(shape) for p, n, shape in offs]


def _adamw_math(wv, gv, mv, vv):
    bc1 = 1.0 - ADAM_B1 ** ADAM_STEP
    bc2 = 1.0 - ADAM_B2 ** ADAM_STEP
    mn = ADAM_B1 * mv + (1.0 - ADAM_B1) * gv
    vn = ADAM_B2 * vv + (1.0 - ADAM_B2) * (gv * gv)
    delta = -ADAM_LR * ((mn / bc1) / (jnp.sqrt(vn / bc2) + ADAM_EPS) + ADAM_WD * wv)
    return delta, mn, vn


def _adamw(name, w, g, m, v, dep=None):
    rows, cols = w.shape
    tr = rows
    for cand in (512, 256, 128, 64, 32, 16, 8):
        if rows % cand == 0 and cand * cols * 4 <= 2 * 1024 * 1024:
            tr = cand
            break

    def fn(ids, tin, vin):
        return list(_adamw_math(*tin)), []

    spec = pl.BlockSpec((tr, cols), lambda i: (i, 0))
    outs = [(_sds((rows, cols), F32), spec)] * 3
    return _tiled(name, fn, (rows // tr,), [(a, spec) for a in (w, g, m, v)], _behind(dep), outs)


def _adamw_many(name, ws, gs, ms, vs):
    n = len(ws)
    views = [(-1, a.shape[-1]) if a.ndim > 1 else (1, -1) for a in ws]
    flat = lambda arrs: [a.reshape(vw) for a, vw in zip(arrs, views)]

    def kern(*refs):
        ins, outs = refs[:4 * n], refs[4 * n:]
        for t in range(n):
            res = _adamw_math(*[ins[q * n + t][...] for q in range(4)])
            for q in range(3):
                outs[q * n + t][...] = res[q]

    shapes = [_sds(a.shape, F32) for a in flat(ws)]
    res = _pcall(kern, name=name, out_shape=shapes * 3, compiler_params=_cparams(),
                 )(*flat(ws), *flat(gs), *flat(ms), *flat(vs))
    back = lambda part: [a.reshape(w.shape) for a, w in zip(part, ws)]
    return back(res[:n]), back(res[n:2 * n]), back(res[2 * n:])


def _pos_embed():
    n_rows = T // GRID_W
    q = D // 4
    omega = 1.0 / (10000.0 ** (jnp.arange(q, dtype=F32) / q))
    er = jnp.arange(n_rows, dtype=jnp.int32).astype(F32)[:, None] * omega[None, :]
    ec = jnp.arange(GRID_W, dtype=jnp.int32).astype(F32)[:, None] * omega[None, :]
    by_row = jnp.concatenate([jnp.sin(er), jnp.cos(er)], axis=-1)
    by_col = jnp.concatenate([jnp.sin(ec), jnp.cos(ec)], axis=-1)
    return jnp.concatenate([jnp.repeat(by_row, GRID_W, axis=0), jnp.tile(by_col, (n_rows, 1))], axis=-1)


def _dense_gates(w_a, w_x):
    rows = jnp.stack([w_a[0], w_x[0], w_a[1], w_x[1]]).reshape(4, 2, RH, BLK)
    mask, spread = _block_mask(), _block_spread().T.astype(BF16)

    def kern(r_ref, m_ref, s_ref, o_ref):
        tiled = jnp.dot(r_ref[...].astype(BF16), s_ref[...], preferred_element_type=F32)
        o_ref[...] = (tiled * m_ref[...]).astype(o_ref.dtype)

    return _pcall(
        kern, name="gates_dense", grid=(2, 4),
        in_specs=[pl.BlockSpec((None, None, RH, BLK), lambda h, q: (q, h, 0, 0)),
                  pl.BlockSpec((RH, RH), lambda h, q: (0, 0)), pl.BlockSpec((BLK, RH), lambda h, q: (0, 0))],
        out_specs=pl.BlockSpec((None, RH, RH), lambda h, q: (h, 0, q)),
        out_shape=_sds((2, RH, NQ), BF16),
    )(rows, mask, spread)


def _block_mask():
    r = lax.broadcasted_iota(jnp.int32, (RH, RH), 0) // BLK
    c = lax.broadcasted_iota(jnp.int32, (RH, RH), 1) // BLK
    return (r == c).astype(F32)


def _block_spread():
    c = lax.broadcasted_iota(jnp.int32, (RH, BLK), 0) % BLK
    j = lax.broadcasted_iota(jnp.int32, (RH, BLK), 1)
    return (c == j).astype(F32)


def _fold_blocks(dense, mask, spread):
    return jnp.dot(dense * mask, spread, preferred_element_type=F32, precision=lax.Precision.HIGHEST)


def _gate_block_grads(folded):
    per = N_BLK // 2
    kinds = [jnp.concatenate([folded[h, q].reshape(per, BLK, BLK) for h in range(2)], axis=0) for q in range(4)]
    return jnp.stack([kinds[0], kinds[2]]), jnp.stack([kinds[1], kinds[3]])


def _gate_bias_dense(b_a, b_x):
    cols = []
    for h in range(2):
        for src in (b_a[0], b_x[0], b_a[1], b_x[1]):
            cols.append(src.reshape(R)[h * RH:(h + 1) * RH])
    return jnp.concatenate(cols).reshape(1, 2 * NQ)


def _gate_bias_grads(dgb):
    v = dgb.reshape(2, 4, RH)
    kinds = [jnp.concatenate([v[0, q], v[1, q]]).reshape(N_BLK, BLK) for q in range(4)]
    return jnp.stack([kinds[0], kinds[2]]), jnp.stack([kinds[1], kinds[3]])


def _residual_epilogue(next_norm):
    def epi(acc, ex):
        x_new = ex[0] + ex[1] * acc
        outs = [acc, x_new]
        if next_norm:
            outs.append(_norm_mod(x_new, ex[-3], ex[-2], ex[-1]))
        return outs
    return epi


def _mlp_fwd(tag, x_in, h, gate, w_in, w_out, next_norm=None, dep=None):
    tm = MM_TILE
    (r,) = _mm(f"{tag}_in", h, w_in, _NN, (T // tm, 4, 1),
               pl.BlockSpec((tm, D), lambda i, j, k: (i, 0)), pl.BlockSpec((None, D, D), lambda i, j, k: (j, 0, 0)),
               [(_sds((T, FF), BF16), pl.BlockSpec((tm, D), lambda i, j, k: (i, j)))], (tm, D),
               extra=[(d_, _full_spec(d_)) for d_ in _behind(dep)], epi=lambda acc, ex: [jnp.maximum(acc, 0.0)])
    row_spec = pl.BlockSpec((tm, D), lambda i, j, k: (i, 0))
    outs = [(_sds((T, D), F32), row_spec)] * 2 + ([(_sds((T, D), BF16), row_spec)] if next_norm else [])
    res = _mm(f"{tag}_out", r, w_out, _NN, (T // tm, 1, FF // D),
              pl.BlockSpec((tm, D), lambda i, j, k: (i, k)), pl.BlockSpec((D, D), lambda i, j, k: (k, 0)),
              outs, (tm, D),
              extra=[(x_in, row_spec), (gate, _full_spec(gate))] + [(v, _full_spec(v)) for v in next_norm or ()],
              a_pre=lambda a: a * a, epi=_residual_epilogue(next_norm))
    return dict(h=h, r=r, o=res[0], x_in=x_in), res[1], (res[2] if next_norm else None)


def _behind(dep):
    return [] if dep is None else [dep]


def _gate_bwd(tag, dx, o, gate, dep=None):
    def fn(ids, t, v):
        d_o = t[0] * v[0]
        return [d_o], [_sum0(t[0] * t[1]), _sum0(d_o)]
    return _tiled(f"{tag}_gate_bwd", fn, (T // ROW_TILE,), [_rows(dx), _rows(o)], [gate] + _behind(dep),
                  [_orow(T, D, BF16)], [(1, D), (1, D)])


def _norm_bwd(tag, dx_res, dh, dh_off, x, g_norm, sc, with_dx=True, dep=None):
    n_t = x.shape[0] // ROW_TILE

    def fn(ids, t, v):
        if with_dx:
            dres, dhv, xv = t
        else:
            dhv, xv = t
        dxv, d_sh, d_sc, d_g = _norm_mod_bwd(dhv, xv, v[0], v[1])
        return ([dres + dxv] if with_dx else []), [d_sh, d_sc, d_g]

    ins = ([_rows(dx_res)] if with_dx else []) + [_rows(dh, off=dh_off), _rows(x)]
    outs = [_orow(x.shape[0], D, F32)] if with_dx else []
    return _tiled(f"{tag}_norm_bwd", fn, (n_t,), ins, [g_norm, sc] + _behind(dep), outs, [(1, D)] * 3)


def _mlp_bwd(tag, dx, saved, g_norm, sc, gate, w_in, w_out, dep=None):
    d_o, d_gate, _ = _gate_bwd(tag, dx, saved["o"], gate, dep)
    tm = MM_TILE
    r = saved["r"]
    (da,) = _mm(f"{tag}_dz", d_o, w_out, _NT, (T // tm, FF // D, 1),
                pl.BlockSpec((tm, D), lambda i, j, k: (i, 0)), pl.BlockSpec((D, D), lambda i, j, k: (j, 0)),
                [(_sds((T, FF), BF16), pl.BlockSpec((tm, D), lambda i, j, k: (i, j)))], (tm, D),
                extra=[(r, pl.BlockSpec((tm, D), lambda i, j, k: (i, j)))],
                epi=lambda acc, ex: [acc * (2.0 * ex[0].astype(F32))])
    tk = MM_TILE
    (dw_out,) = _mm(f"{tag}_dwout", r, d_o, _TN, (FF // tm, 1, T // tk),
                    pl.BlockSpec((tk, tm), lambda i, j, k: (k, i)), pl.BlockSpec((tk, D), lambda i, j, k: (k, 0)),
                    [(_sds((FF, D), BF16), pl.BlockSpec((tm, D), lambda i, j, k: (i, 0)))], (tm, D),
                    a_pre=lambda a: a * a)
    (dh,) = _mm(f"{tag}_dh", da, w_in, _NT, (T // tm, 1, 4),
                pl.BlockSpec((tm, D), lambda i, j, k: (i, k)), pl.BlockSpec((None, D, D), lambda i, j, k: (k, 0, 0)),
                [(_sds((T, D), F32), pl.BlockSpec((tm, D), lambda i, j, k: (i, 0)))], (tm, D))
    (dw_in,) = _mm(f"{tag}_dwin", saved["h"], da, _TN, (D // tm, 4, T // tk),
                   pl.BlockSpec((tk, tm), lambda i, j, k: (k, i)), pl.BlockSpec((tk, D), lambda i, j, k: (k, j)),
                   [(_sds((4, D, D), BF16), pl.BlockSpec((None, tm, D), lambda i, j, k: (j, i, 0)))], (tm, D))
    dx_in, d_sh, d_sc, d_g = _norm_bwd(tag, dx, dh, 0, saved["x_in"], g_norm, sc)
    return dx_in, dw_in, dw_out, dict(sh=d_sh, sc=d_sc, gate=d_gate, g_norm=d_g)


def _local_step(x, ctx, tgt, mods, cmods, norm_g, final_g, rec, conf, wg, on_grads=None, wg_pre=None, on_later=None):
    on_grads = on_grads or (lambda group, dws: None)
    wg_pre = wg_pre or (lambda group, after: None)
    on_later = on_later or (lambda after: None)
    n_t = T // ROW_TILE
    row = lambda v: v.reshape(1, -1)
    m0 = [row(mods[0, q]) for q in range(6)]
    m1 = [row(mods[1, q]) for q in range(6)]
    g00, g01, g10, g11 = (row(norm_g[0, 0]), row(norm_g[0, 1]), row(norm_g[1, 0]), row(norm_g[1, 1]))
    csh, csc = row(cmods[0]), row(cmods[1])
    pos = _pos_embed()

    def prep0(ids, t, v):
        cx, xv, pv = t
        is_ctx = ids[0] == 0
        xin = jnp.where(is_ctx, cx, xv + pv)
        sh = jnp.where(is_ctx, v[3], v[1])
        sc = jnp.where(is_ctx, v[4], v[2])
        return [_norm_mod(xin, v[0], sc, sh), xv + pv], []

    dep = wg_pre("rec_in", csh)
    hcat, x0 = _tiled(
        "prep0", prep0, (N_SCAN,),
        [(ctx, pl.BlockSpec((ROW_TILE, D), lambda i: (0, 0))), _rows(x, off=-1, clamp_lo=True),
         _rows(pos, off=-1, clamp_lo=True)],
        [g00, m0[0], m0[1], csh, csc] + _behind(dep),
        [_orow(TA, D, BF16), _orow(T, D, F32, off=-1, clamp_lo=True)])

    tm_a = REC_TILE
    w_rin = wg("rec_in", hcat)["rec_w_in"]
    (a_in,) = _mm("rec_in", hcat, w_rin, _NN, (TA // tm_a, 4, 1),
                  pl.BlockSpec((tm_a, D), lambda i, j, k: (i, 0)),
                  pl.BlockSpec((None, D, RH), lambda i, j, k: (j, 0, 0)),
                  [(_sds((TA, 2 * R), F32), pl.BlockSpec((tm_a, RH), lambda i, j, k: (i, j)))], (tm_a, RH))
    rec_starts = (0, 1)
    u = _dwconv("rec_conv", a_in, R // CW_REC, rec["conv_w"], row(rec["conv_b"]), 1, rec_starts, R, CW_REC)
    wbd = _dense_gates(rec["w_a"], rec["w_x"])
    gbias = _gate_bias_dense(rec["b_a"], rec["b_x"])
    lam = rec["lam"]
    a_f, b_f, a_r, b_r = _tiled("rg_fwd", _rg_fwd_fn, (TA // RG_TILE,), [_rows(u, tm=RG_TILE)], [wbd, gbias, lam],
                                [_orow(TA, R, F32, tm=RG_TILE)] * 4, vec_refs=True)
    dep = wg_pre("rec_out", a_f)
    dep = wg_pre("mlp0", a_f if dep is None else dep)
    y_f, y_r, hin_f, hin_r = _scan_fwd(a_f, b_f, a_r, b_r)

    def rec_mid(ids, t, v):
        gp, yf, yr = t
        g, _ = _gelu(gp)
        return [g * (yf + yr)], []

    (m_rec,) = _tiled("rec_mid", rec_mid, (n_t,),
                      [_rows(a_in, R, off=1), _rows(y_f, off=1), _rows(y_r, off=1)], _behind(dep),
                      [_orow(T, R, BF16)])
    tm = MM_TILE
    row_spec = pl.BlockSpec((tm, D), lambda i, j, k: (i, 0))
    norm_mlp0 = (g01, m0[4], m0[3])
    w_rout = wg("rec_out", m_rec)["rec_w_out"]
    o_rec, x1, h_mlp0 = _mm(
        "rec_out", m_rec, w_rout, _NN, (T // tm, 1, 1),
        pl.BlockSpec((tm, R), lambda i, j, k: (i, 0)), pl.BlockSpec((R, D), lambda i, j, k: (0, 0)),
        [(_sds((T, D), F32), row_spec)] * 2 + [(_sds((T, D), BF16), row_spec)], (tm, D),
        extra=[(x0, row_spec), (m0[2], _full_spec(m0[2]))] + [(v, _full_spec(v)) for v in norm_mlp0],
        epi=_residual_epilogue(norm_mlp0))
    w_m0 = wg("mlp0", x1)
    dep = wg_pre("conf", x1)
    mlp0, x2, h1 = _mlp_fwd("mlp0", x1, h_mlp0, m0[5], w_m0["w_in"], w_m0["w_out"], (g10, m1[1], m1[0]), dep)

    b_pw1 = row(conf["b_pw1"])
    w_cf = wg("conf", x2)
    dep = wg_pre("mlp1", x2)
    (pre,) = _mm("conf_pw1", h1, w_cf["conf_w_pw1"], _NN, (T // tm, 4, 1),
                 pl.BlockSpec((tm, D), lambda i, j, k: (i, 0)),
                 pl.BlockSpec((None, D, D // 2), lambda i, j, k: (j, 0, 0)),
                 [(_sds((T, 2 * D), F32), pl.BlockSpec((tm, D // 2), lambda i, j, k: (i, j)))], (tm, D // 2),
                 extra=[(b_pw1, pl.BlockSpec((1, D // 2), lambda i, j, k: (0, j)))]
                 + [(d_, _full_spec(d_)) for d_ in _behind(dep)],
                 epi=lambda acc, ex: [acc + ex[0]])
    (zg,) = _tiled("conf_glu", lambda ids, t, v: ([t[0] * _sigmoid(t[1])], []), (n_t,),
                   [_rows(pre, D, col=0), _rows(pre, D, col=1)], [], [_orow(T, D, F32)])
    conf_starts = (0,)
    zc = _dwconv("conf_conv", zg, 0, conf["conv_w"], row(conf["conv_b"]), CONF_KW // 2, conf_starts, D, CW_CONF)
    ln_g, ln_b = row(conf["ln_g"]), row(conf["ln_b"])

    def ln_silu(ids, t, v):
        nh, _ = _layernorm_parts(t[0])
        ln = nh * v[0] + v[1]
        return [ln * _sigmoid(ln)], []

    (s_conf,) = _tiled("conf_ln", ln_silu, (n_t,), [_rows(zc)], [ln_g, ln_b], [_orow(T, D, BF16)])
    b_pw2 = row(conf["b_pw2"])
    norm_mlp1 = (g11, m1[4], m1[3])
    pw2_epi = _residual_epilogue(norm_mlp1)
    y_conf, x3, h_mlp1 = _mm(
        "conf_pw2", s_conf, w_cf["conf_w_pw2"], _NN, (T // tm, 1, 1),
        row_spec, pl.BlockSpec((D, D), lambda i, j, k: (0, 0)),
        [(_sds((T, D), F32), row_spec)] * 2 + [(_sds((T, D), BF16), row_spec)], (tm, D),
        extra=[(x2, row_spec), (m1[2], _full_spec(m1[2])), (b_pw2, _full_spec(b_pw2))]
        + [(v, _full_spec(v)) for v in norm_mlp1],
        epi=lambda acc, ex: pw2_epi(acc + ex[2], ex))
    w_m1 = wg("mlp1", x3)
    mlp1, x4, _ = _mlp_fwd("mlp1", x3, h_mlp1, m1[5], w_m1["w_in"], w_m1["w_out"])

    fg = row(final_g)

    def head(ids, t, v):
        n, r = _rms(t[0])
        err = n * v[0] - t[1]
        d_out = err * (1.0 / D)
        dn = d_out * v[0]
        dxv = r * (dn - n * jnp.mean(dn * n, axis=-1, keepdims=True))
        part = jnp.sum(_sum0(err * err), axis=1, keepdims=True) * (0.5 / D)
        return [dxv], [part, _sum0(d_out * n)]

    dx4, loss, d_fg = _tiled("head", head, (n_t,), [_rows(x4), _rows(tgt)], [fg], [_orow(T, D, F32)],
                             [(1, 1), (1, D)])

    dx3, dw_in1, dw_out1, dm_mlp1 = _mlp_bwd("mlp1", dx4, mlp1, g11, m1[4], m1[5],
                                             w_m1["w_in"], w_m1["w_out"])
    dep = on_grads("mlp1", (dw_in1, dw_out1))
    d_y, d_g1c, d_bpw2 = _gate_bwd("conf", dx3, y_conf, m1[2], dep)
    tk = MM_TILE
    (dw_pw2,) = _mm("conf_dwpw2", s_conf, d_y, _TN, (D // tm, 1, T // tk),
                    pl.BlockSpec((tk, tm), lambda i, j, k: (k, i)), pl.BlockSpec((tk, D), lambda i, j, k: (k, 0)),
                    [(_sds((D, D), BF16), pl.BlockSpec((tm, D), lambda i, j, k: (i, 0)))], (tm, D))
    (ds,) = _mm("conf_ds", d_y, w_cf["conf_w_pw2"], _NT, (T // tm, 1, 1),
                pl.BlockSpec((tm, D), lambda i, j, k: (i, 0)), pl.BlockSpec((D, D), lambda i, j, k: (0, 0)),
                [(_sds((T, D), F32), pl.BlockSpec((tm, D), lambda i, j, k: (i, 0)))], (tm, D))
    dep = on_later(ds)

    def ln_silu_bwd(ids, t, v):
        dsv, zcv = t
        nh, rstd = _layernorm_parts(zcv)
        ln = nh * v[0] + v[1]
        sg = _sigmoid(ln)
        d_ln = dsv * (sg * (1.0 + ln * (1.0 - sg)))
        d_nh = d_ln * v[0]
        d_zc = rstd * (d_nh - jnp.mean(d_nh, axis=-1, keepdims=True)
                       - nh * jnp.mean(d_nh * nh, axis=-1, keepdims=True))
        return [d_zc], [_sum0(d_ln * nh), _sum0(d_ln)]

    d_zc, d_lng, d_lnb = _tiled("conf_ln_bwd", ln_silu_bwd, (n_t,), [_rows(ds), _rows(zc)],
                                [ln_g, ln_b] + _behind(dep), [_orow(T, D, F32)], [(1, D), (1, D)])
    d_zg = _dwconv("conf_conv_dx", d_zc, 0, conf["conv_w"], jnp.zeros((1, D), F32),
                   CONF_KW - 1 - CONF_KW // 2, conf_starts, D, CW_CONF, flip=True)

    def glu_bwd(ids, t, v):
        dz, pa, pb = t
        sg = _sigmoid(pb)
        d_a = dz * sg
        d_b = dz * pa * sg * (1.0 - sg)
        return [jnp.concatenate([d_a, d_b], axis=1)], [_sum0(d_a), _sum0(d_b)]

    d_pre, d_b1a, d_b1b = _tiled(
        "conf_glu_bwd", glu_bwd, (n_t,), [_rows(d_zg), _rows(pre, D, col=0), _rows(pre, D, col=1)], [],
        [_orow(T, 2 * D, BF16)], [(1, D), (1, D)])
    (dw_pw1,) = _mm("conf_dwpw1", h1, d_pre, _TN, (D // tm, 4, T // tk),
                    pl.BlockSpec((tk, tm), lambda i, j, k: (k, i)),
                    pl.BlockSpec((tk, D // 2), lambda i, j, k: (k, j)),
                    [(_sds((4, D, D // 2), BF16), pl.BlockSpec((None, tm, D // 2), lambda i, j, k: (j, i, 0)))],
                    (tm, D // 2))
    dep = on_grads("conf", (dw_pw1, dw_pw2))
    (dh1,) = _mm("conf_dh", d_pre, w_cf["conf_w_pw1"], _NT, (T // tm, 1, 4),
                 pl.BlockSpec((tm, D // 2), lambda i, j, k: (i, k)),
                 pl.BlockSpec((None, D, D // 2), lambda i, j, k: (k, 0, 0)),
                 [(_sds((T, D), F32), pl.BlockSpec((tm, D), lambda i, j, k: (i, 0)))], (tm, D))
    dx2, d_sh1c, d_sc1c, d_g10 = _norm_bwd("conf", dx3, dh1, 0, x2, g10, m1[1], dep=dep)
    dep = on_later(dx2)

    dx1, dw_in0, dw_out0, dm_mlp0 = _mlp_bwd("mlp0", dx2, mlp0, g01, m0[4], m0[5],
                                             w_m0["w_in"], w_m0["w_out"], dep)
    dep = on_grads("mlp0", (dw_in0, dw_out0))
    d_orec, d_g1r, _ = _gate_bwd("rec", dx1, o_rec, m0[2], dep)
    (dw_rout,) = _mm("rec_dwout", m_rec, d_orec, _TN, (R // RH, 1, T // tk),
                     pl.BlockSpec((tk, RH), lambda i, j, k: (k, i)), pl.BlockSpec((tk, D), lambda i, j, k: (k, 0)),
                     [(_sds((R, D), BF16), pl.BlockSpec((RH, D), lambda i, j, k: (i, 0)))], (RH, D))
    (dm_rec,) = _mm("rec_dm", d_orec, w_rout, _NT, (T // tm, 1, 1),
                    pl.BlockSpec((tm, D), lambda i, j, k: (i, 0)), pl.BlockSpec((R, D), lambda i, j, k: (0, 0)),
                    [(_sds((T, R), F32), pl.BlockSpec((tm, R), lambda i, j, k: (i, 0)))], (tm, R))
    dep = on_later(dm_rec)

    def rec_mid_bwd(ids, t, v):
        dmv, gp, yf, yr = t
        g, th = _gelu(gp)
        lat = ids[0] > 0
        d_gp = jnp.where(lat, dmv * (yf + yr) * _gelu_grad(gp, th), 0.0)
        dy = jnp.where(lat, dmv * g, 0.0)
        return [d_gp, dy], []

    d_a, dy = _tiled("rec_mid_bwd", rec_mid_bwd, (N_SCAN,),
                     [_rows(dm_rec, off=-1, clamp_lo=True), _rows(a_in, R), _rows(y_f), _rows(y_r)], _behind(dep),
                     [(_sds((TA, 2 * R), BF16), pl.BlockSpec((ROW_TILE, R), lambda i: (i, 0))), _orow(TA, R, F32)])
    da_f, db_f, da_r, db_r = _scan_bwd(dy, a_f, y_f, hin_f, a_r, y_r, hin_r)
    d_gpre, d_u, d_gbias, d_lam = _tiled(
        "rg_bwd", _rg_bwd_fn, (TA // RG_TILE,), [_rows(a, tm=RG_TILE) for a in (u, da_f, db_f, da_r, db_r)],
        [wbd, gbias, lam], [_orow(TA, 2 * NQ, BF16, tm=RG_TILE), _orow(TA, R, F32, tm=RG_TILE)],
        [(1, 2 * NQ), (1, 2 * R)], vec_refs=True)
    tk_a = REC_TILE
    d_a = _dwconv("rec_conv_dx", d_u, 0, rec["conv_w"], jnp.zeros((1, R), F32), REC_KW - 1 - 1,
                  rec_starts, R, CW_REC, flip=True, into=(d_a, R // CW_REC))
    (dw_rin,) = _mm("rec_dwin", hcat, d_a, _TN, (D // tm, 4, TA // tk_a),
                    pl.BlockSpec((tk_a, tm), lambda i, j, k: (k, i)), pl.BlockSpec((tk_a, RH), lambda i, j, k: (k, j)),
                    [(_sds((4, D, RH), BF16), pl.BlockSpec((None, tm, RH), lambda i, j, k: (j, i, 0)))], (tm, RH))
    dep = on_grads("rec", (dw_rin, dw_rout))
    (dhcat,) = _mm("rec_dh", d_a, w_rin, _NT, (TA // tm_a, 1, 4),
                   pl.BlockSpec((tm_a, RH), lambda i, j, k: (i, k)),
                   pl.BlockSpec((None, D, RH), lambda i, j, k: (k, 0, 0)),
                   [(_sds((TA, D), F32), pl.BlockSpec((tm_a, D), lambda i, j, k: (i, 0)))], (tm_a, D))
    dx0, d_sh1r, d_sc1r, d_g00 = _norm_bwd("rec", dx1, dhcat, 1, x0, g00, m0[1], dep=dep)
    dep = on_later(dx0)

    d_csh, d_csc, d_g00c = _norm_bwd("ctx", None, dhcat, 0, ctx, g00, csc, with_dx=False, dep=dep)
    blk_mask, blk_spread = _block_mask(), _block_spread()
    (d_wbd,) = _mm("rg_dw", u, d_gpre, _TN, (2, 2, TA // tk_a),
                   pl.BlockSpec((tk_a, RH), lambda i, j, k: (k, i)),
                   pl.BlockSpec((tk_a, NQ // 2), lambda i, j, k: (k, 2 * i + j)),
                   [(_sds((2, 4, RH, BLK), F32), pl.BlockSpec((None, 2, RH, BLK), lambda i, j, k: (i, j, 0, 0)))],
                   (RH, NQ // 2),
                   extra=[(blk_mask, _full_spec(blk_mask)), (blk_spread, _full_spec(blk_spread))]
                   + [(d, _full_spec(d)) for d in _behind(dep)],
                   epi=lambda acc, ex: [jnp.stack([_fold_blocks(acc[:, s * RH:(s + 1) * RH], ex[0], ex[1])
                                                   for s in range(2)])])
    d_cw_rec = _dwconv_wgrad("rec_conv_dw", d_u, a_in, R // CW_REC, REC_KW, 1, rec_starts, R, CW_REC, dep)
    d_cw_conf = _dwconv_wgrad("conf_conv_dw", d_zc, zg, 0, CONF_KW, CONF_KW // 2, conf_starts, D, CW_CONF, dep)

    big = dict(rec_w_in=dw_rin, rec_w_out=dw_rout, conf_w_pw1=dw_pw1, conf_w_pw2=dw_pw2,
               mlp_w_in=(dw_in0, dw_in1), mlp_w_out=(dw_out0, dw_out1))
    d_wa, d_wx = _gate_block_grads(d_wbd)
    d_ba, d_bx = _gate_bias_grads(d_gbias)
    d_mod = jnp.concatenate([
        d_sh1r, d_sc1r, d_g1r, dm_mlp0["sh"], dm_mlp0["sc"], dm_mlp0["gate"],
        d_sh1c, d_sc1c, d_g1c, dm_mlp1["sh"], dm_mlp1["sc"], dm_mlp1["gate"]], axis=1).reshape(2, 6 * D)
    small = dict(
        d_mod=d_mod, d_cmod=jnp.concatenate([d_csh, d_csc], axis=1),
        norm_g=jnp.concatenate([d_g00 + d_g00c, dm_mlp0["g_norm"], d_g10, dm_mlp1["g_norm"]], axis=1),
        rec_conv_w=d_cw_rec[:REC_KW], rec_conv_b=d_cw_rec[REC_KW], rec_lambda=d_lam.reshape(2, R),
        rec_w_a=d_wa, rec_b_a=d_ba, rec_w_x=d_wx, rec_b_x=d_bx,
        conf_b_pw1=jnp.concatenate([d_b1a, d_b1b], axis=1), conf_conv_w=d_cw_conf[:CONF_KW],
        conf_conv_b=d_cw_conf[CONF_KW], conf_ln_g=d_lng, conf_ln_b=d_lnb, conf_b_pw2=d_bpw2, final_g=d_fg)
    return loss.reshape(()), dx0, big, small


_BIG = ("rec_w_in", "rec_w_out", "conf_w_pw1", "conf_w_pw2", "mlp_w_in", "mlp_w_out")


def _halves(w):
    return w.reshape(w.shape[0], 2, w.shape[1] // 2, w.shape[2])


def _ada_fwd(c16, w_ada, b_shard):
    ns = w_ada.shape[2]
    tn = 512

    def kern(c_ref, w_ref, b_ref, o_ref):
        cv = c_ref[...]
        s = (cv * _sigmoid(cv)).astype(BF16)
        o_ref[...] = jnp.dot(s, w_ref[...].astype(BF16), preferred_element_type=F32) + b_ref[...]

    return _pcall(
        kern, name="ada_fwd", grid=(2, ns // tn),
        in_specs=[pl.BlockSpec((16, D), lambda l, j: (0, 0)), pl.BlockSpec((None, D, tn), lambda l, j: (l, 0, j)),
                  pl.BlockSpec((None, 1, tn), lambda l, j: (l, 0, j))],
        out_specs=pl.BlockSpec((None, 16, tn), lambda l, j: (l, 0, j)),
        out_shape=_sds((2, 16, ns), F32), compiler_params=_cparams(),
    )(c16, w_ada, b_shard)


def _ada_bwd(c16, dm16, w_ada):
    ns = w_ada.shape[2]
    tn = 512

    def kern(c_ref, dm_ref, w_ref, gw_ref, ds_ref):
        cv = c_ref[...]
        s = (cv * _sigmoid(cv)).astype(BF16)
        dm = dm_ref[...].astype(BF16)
        gw_ref[...] = lax.dot_general(s, dm, _TN, preferred_element_type=F32)

        @pl.when(jnp.logical_and(pl.program_id(0) == 0, pl.program_id(1) == 0))
        def _():
            ds_ref[...] = jnp.zeros_like(ds_ref)

        ds_ref[...] += lax.dot_general(dm, w_ref[...].astype(BF16), _NT, preferred_element_type=F32)

    return _pcall(
        kern, name="ada_bwd", grid=(2, ns // tn),
        in_specs=[pl.BlockSpec((16, D), lambda l, j: (0, 0)), pl.BlockSpec((None, 16, tn), lambda l, j: (l, 0, j)),
                  pl.BlockSpec((None, D, tn), lambda l, j: (l, 0, j))],
        out_specs=[pl.BlockSpec((None, D, tn), lambda l, j: (l, 0, j)), pl.BlockSpec((16, D), lambda l, j: (0, 0))],
        out_shape=[_sds((2, D, ns), F32), _sds((16, D), F32)], compiler_params=_cparams(),
    )(c16, dm16, w_ada)


def _cctx_grad(ds4, c_ctx):
    def kern(d_ref, c_ref, o_ref):
        tot = d_ref[0, 0:1, :] + d_ref[1, 0:1, :] + d_ref[2, 0:1, :] + d_ref[3, 0:1, :]
        cv = c_ref[...]
        sg = _sigmoid(cv)
        o_ref[...] = tot * (sg * (1.0 + cv * (1.0 - sg)))

    return _pcall(kern, name="cctx_grad", out_shape=_sds((1, D), F32))(ds4, c_ctx.reshape(1, D))


def kernel(x, c, ctx, c_ctx, w_ada, b_ada, norm_g, rec_w_in, rec_conv_w, rec_conv_b, rec_lambda, rec_w_a, rec_b_a, rec_w_x, rec_b_x, rec_w_out, conf_w_pw1, conf_b_pw1, conf_conv_w, conf_conv_b, conf_ln_g, conf_ln_b, conf_w_pw2, conf_b_pw2, mlp_w_in, mlp_w_out, final_g, loss_target, m_c_ctx, m_w_ada, m_b_ada, m_norm_g, m_rec_w_in, m_rec_conv_w, m_rec_conv_b, m_rec_lambda, m_rec_w_a, m_rec_b_a, m_rec_w_x, m_rec_b_x, m_rec_w_out, m_conf_w_pw1, m_conf_b_pw1, m_conf_conv_w, m_conf_conv_b, m_conf_ln_g, m_conf_ln_b, m_conf_w_pw2, m_conf_b_pw2, m_mlp_w_in, m_mlp_w_out, m_final_g, v_c_ctx, v_w_ada, v_b_ada, v_norm_g, v_rec_w_in, v_rec_conv_w, v_rec_conv_b, v_rec_lambda, v_rec_w_a, v_rec_b_a, v_rec_w_x, v_rec_b_x, v_rec_w_out, v_conf_w_pw1, v_conf_b_pw1, v_conf_conv_w, v_conf_conv_b, v_conf_ln_g, v_conf_ln_b, v_conf_w_pw2, v_conf_b_pw2, v_mlp_w_in, v_mlp_w_out, v_final_g):
    names = ["c_ctx", "w_ada", "b_ada", "norm_g", "rec_w_in", "rec_conv_w", "rec_conv_b", "rec_lambda", "rec_w_a",
             "rec_b_a", "rec_w_x", "rec_b_x", "rec_w_out", "conf_w_pw1", "conf_b_pw1", "conf_conv_w", "conf_conv_b",
             "conf_ln_g", "conf_ln_b", "conf_w_pw2", "conf_b_pw2", "mlp_w_in", "mlp_w_out", "final_g"]
    w = dict(zip(names, [c_ctx, w_ada, b_ada, norm_g, rec_w_in, rec_conv_w, rec_conv_b, rec_lambda, rec_w_a,
                         rec_b_a, rec_w_x, rec_b_x, rec_w_out, conf_w_pw1, conf_b_pw1, conf_conv_w, conf_conv_b,
                         conf_ln_g, conf_ln_b, conf_w_pw2, conf_b_pw2, mlp_w_in, mlp_w_out, final_g]))
    m = dict(zip(names, [m_c_ctx, m_w_ada, m_b_ada, m_norm_g, m_rec_w_in, m_rec_conv_w, m_rec_conv_b, m_rec_lambda,
                         m_rec_w_a, m_rec_b_a, m_rec_w_x, m_rec_b_x, m_rec_w_out, m_conf_w_pw1, m_conf_b_pw1,
                         m_conf_conv_w, m_conf_conv_b, m_conf_ln_g, m_conf_ln_b, m_conf_w_pw2, m_conf_b_pw2,
                         m_mlp_w_in, m_mlp_w_out, m_final_g]))
    v = dict(zip(names, [v_c_ctx, v_w_ada, v_b_ada, v_norm_g, v_rec_w_in, v_rec_conv_w, v_rec_conv_b, v_rec_lambda,
                         v_rec_w_a, v_rec_b_a, v_rec_w_x, v_rec_b_x, v_rec_w_out, v_conf_w_pw1, v_conf_b_pw1,
                         v_conf_conv_w, v_conf_conv_b, v_conf_ln_g, v_conf_ln_b, v_conf_w_pw2, v_conf_b_pw2,
                         v_mlp_w_in, v_mlp_w_out, v_final_g]))
    mx, my, mc = _me()
    chip = 2 * mx + my
    me = 4 * mx + 2 * my + mc

    sharded_small = ["norm_g", "rec_conv_w", "rec_lambda", "conf_b_pw1", "conf_conv_w", "conf_conv_b", "conf_ln_g",
                     "conf_ln_b", "conf_b_pw2"]
    packed, offs = _pack([c] + [w[k] for k in sharded_small], 8)
    place = jnp.stack([chip, mc]).astype(jnp.int32)
    shards = [("rec_in", _halves(rec_w_in), 0), ("rec_out", _halves(rec_w_out), 0),
              ("pw1", _halves(conf_w_pw1), 0), ("pw2", _halves(conf_w_pw2), 0),
              ("mlp_in0", _halves(mlp_w_in), 0), ("mlp_in1", _halves(mlp_w_in), 1),
              ("mlp_out0", _halves(mlp_w_out), 0), ("mlp_out1", _halves(mlp_w_out), 1)]
    small_state, small_sent = _send_start("gather_small_start", [packed[None]], _to_all7, [(7,) + packed.shape], place)
    (slot_rin,) = _place_big(shards[:1], place, small_sent)
    flying, gsems, swapping = {}, {}, {}
    flying["rec_in"], gsems["rec_in"], rec_started = _gather_start("gather_start_rec", [slot_rin], ((0,),), small_sent)
    slots = [slot_rin] + _place_big(shards[1:], place, rec_started)
    placed = jnp.broadcast_to(lax.dynamic_slice(slots[-1], (chip, 0, 0, 0), (1, 1, 1, 1)).reshape(1, 1), (8, 1))
    (own,), (landed,) = _send_wait("gather_small_wait", *small_state, _to_all7, placed)
    by_flip = jnp.concatenate([own, landed], axis=0)
    got_flat = jnp.take(by_flip, jnp.arange(8) ^ me, axis=0).reshape(8, -1)

    def piece(i):
        p, n, shape = offs[i]
        return got_flat[:, p:p + n].reshape((8,) + tuple(shape))

    c_rows = piece(0).reshape(8, D)
    full = {}
    for i, k in enumerate(sharded_small):
        per_chip = jnp.moveaxis(piece(1 + i)[0::2], 0, -2)
        full[k] = per_chip.reshape(per_chip.shape[:-2] + (4 * per_chip.shape[-1],))
    c16 = jnp.concatenate([c_rows, c_ctx.reshape(1, D), jnp.zeros((7, D), F32)], axis=0)

    ns = w_ada.shape[2]
    b_shard = lax.dynamic_slice_in_dim(b_ada, chip * ns, ns, axis=1).reshape(2, 1, ns)
    prod = _ada_fwd(c16, w_ada, b_shard)

    own_rows = lax.dynamic_index_in_dim(prod[:, :8].reshape(2, 4, 2, ns), mc, axis=2, keepdims=False)
    rows = jnp.concatenate([own_rows.transpose(1, 0, 2), jnp.broadcast_to(prod[0, 8], (4, 1, ns)),
                            jnp.zeros((4, 5, ns), F32)], axis=1)
    mod_state, mod_started = _send_start("mod_start", [rows], _to_chips, [(3, 8, ns)], place)
    use_order = dict(rec=(0, 1), mlp0=(4, 6), conf=(2, 3), mlp1=(5, 7))
    fetch_order = dict(rec_out=(1,), mlp0=(4, 6), conf=(2, 3), mlp1=(5, 7))
    order = [t for g in fetch_order for t in fetch_order[g]]
    groups = [tuple(order.index(t) for t in fetch_order[g]) for g in fetch_order]
    fly, sems, all_started = _gather_start("gather_start_rest", [slots[t] for t in order], tuple(groups), mod_started)
    for gi, g in enumerate(fetch_order):
        flying[g], gsems[g] = [fly[k] for k in groups[gi]], sems[2 * gi:2 * gi + 2]

    def wg_pre(group, after):
        bufs = _gather_wait(f"gather_wait_{group}", flying[group], *gsems[group], after)
        swapping[group], token = _swap_start(f"swap_start_{group}", bufs, after)
        return token

    def wg(group, after):
        bufs = _swap_wait(f"swap_wait_{group}", *swapping[group], after)
        if group == "rec_in":
            return dict(rec_w_in=bufs[0].reshape(4, D, RH))
        if group == "rec_out":
            return dict(rec_w_out=bufs[0].reshape(R, D))
        if group == "conf":
            return dict(conf_w_pw1=bufs[0].reshape(4, D, D // 2), conf_w_pw2=bufs[1].reshape(D, D))
        return dict(w_in=bufs[0].reshape(4, D, D), w_out=bufs[1].reshape(FF, D))

    (rows,), (landed,) = _send_wait("mod_wait", *mod_state, _to_chips, all_started)
    own = lax.dynamic_index_in_dim(rows, chip, axis=0, keepdims=True)
    by_flip = jnp.concatenate([own, landed[1:2], landed[0:1], landed[2:3]], axis=0)
    by_chip = jnp.take(by_flip, jnp.arange(4) ^ chip, axis=0)
    mods = by_chip[:, :2].transpose(1, 0, 2).reshape(2, 6, D)
    cmods = by_chip[:, 2].reshape(6, D)[:2]

    rec = dict(conv_w=full["rec_conv_w"][0], conv_b=rec_conv_b[0], lam=full["rec_lambda"][0],
               w_a=rec_w_a[0], b_a=rec_b_a[0], w_x=rec_w_x[0], b_x=rec_b_x[0])
    conf = dict(b_pw1=full["conf_b_pw1"][0], conv_w=full["conf_conv_w"][0], conv_b=full["conf_conv_b"][0],
                ln_g=full["conf_ln_g"][0], ln_b=full["conf_ln_b"][0], b_pw2=full["conf_b_pw2"][0])
    pairing, sent, sharing = {}, {}, {}

    def on_grads(group, dws):
        parts = [dw.reshape((4,) + shards[t][1].shape[1:]) for dw, t in zip(dws, use_order[group])]
        pairing[group], token = _reduce_begin(group, parts, place)
        return token

    def finish_pair(after):
        (group, state), = pairing.items()
        pairing.clear()
        sent[group], token = _reduce_mid(group, state, place, after)
        if group == "rec":
            mlp = _reduce_end("mlp1", sent["mlp1"], place, token, layer=(1, 2))
            cf = _reduce_end("conf", sent["conf"], place, token)
            mlp = _reduce_end("mlp0", sent["mlp0"], place, token, layer=(0, 2), into=mlp)
            sharing["state"], token = _share_start("share_start", cf + mlp, place)
        sent["token"] = token
        return token

    loss_local, grad_x, _, small = _local_step(x[0], ctx[0], loss_target[0], mods, cmods, full["norm_g"], final_g,
                                               rec, conf, wg, on_grads, wg_pre, finish_pair)
    rec_sent = sent["token"]
    small["loss"] = loss_local.reshape(1)

    small_names = ["loss", "d_mod", "d_cmod", "norm_g", "rec_conv_w", "rec_conv_b", "rec_lambda", "rec_w_a", "rec_b_a",
                   "rec_w_x", "rec_b_x", "conf_b_pw1", "conf_conv_w", "conf_conv_b", "conf_ln_g", "conf_ln_b",
                   "conf_b_pw2", "final_g"]
    mine = lax.broadcasted_iota(jnp.int32, (8, 1), 0) == me
    mod_slots = jnp.where(mine, small["d_mod"].reshape(1, -1), 0.0)
    spacked, soffs = _pack([small[k] for k in small_names] + [mod_slots])
    def sum_rec(after):
        sharing["rec"], token = _share_start("share_rec_start", _reduce_end("rec", sent["rec"], place, after), place)
        return token

    small_state, small_started = _allreduce_small_begin(spacked, place, rec_sent, sum_rec)

    shared = _share_wait("share_wait", *sharing["state"], small_started)
    delta, new_m, new_v, done = {}, {}, {}, {}

    def adamw_of(k, g, dep=None):
        cols = w[k].shape[-1]
        d_, m_, v_ = _adamw(f"adamw_{k}", w[k].reshape(-1, cols), g.reshape(-1, cols),
                            m[k].reshape(-1, cols), v[k].reshape(-1, cols), dep)
        done[k] = v_
        delta[k], new_m[k], new_v[k] = (a.reshape(w[k].shape) for a in (d_, m_, v_))

    g_big = {}
    for k, g in zip(_BIG[2:], shared):
        g_big[k] = g.reshape(w[k].shape)
        adamw_of(k, g_big[k])
    for k, g in zip(_BIG[:2], _share_wait("share_rec_wait", *sharing["rec"], done["mlp_w_out"])):
        g_big[k] = g.reshape(w[k].shape)
        adamw_of(k, g_big[k])
    unpacked = _unpack(_allreduce_small_end(small_state, [done[k] for k in _BIG]), soffs)
    ssum = dict(zip(small_names, unpacked[:-1]))
    loss = ssum["loss"].reshape(())
    dmod_rows = unpacked[-1].reshape(8, 2, 6 * D).transpose(1, 0, 2)

    d_cmod_full =jnp.concatenate([ssum["d_cmod"].reshape(1, 2 * D), jnp.zeros((1, 4 * D), F32)], axis=1)
    dm16 = jnp.concatenate([dmod_rows, jnp.stack([d_cmod_full, jnp.zeros((1, 6 * D), F32)]),
                            jnp.zeros((2, 7, 6 * D), F32)], axis=1)
    dm16_shard = lax.dynamic_slice_in_dim(dm16, chip * ns, ns, axis=2)
    g_w_ada, ds_part = _ada_bwd(c16, dm16_shard, w_ada)
    ds_state, ds_sent = _send_start("dsilu_start", [jnp.broadcast_to(ds_part[8:16], (4, 8, D))], _to_chips,
                                    [(3, 8, D)], place)
    adamw_of("w_ada", g_w_ada, ds_sent)
    (ds_own,), (ds_landed,) = _send_wait("dsilu_wait", *ds_state, _to_chips, done["w_ada"])
    ds_flip = jnp.concatenate([ds_own[:1], ds_landed[1:2], ds_landed[0:1], ds_landed[2:3]], axis=0)
    g_c_ctx = _cctx_grad(jnp.take(ds_flip, jnp.arange(4) ^ chip, axis=0), c_ctx).reshape(D)
    g_b_ada = ssum["d_mod"] + jnp.stack([d_cmod_full[0], jnp.zeros((6 * D,), F32)])

    def shard_of(a, axis):
        n = a.shape[axis] // 4
        return lax.dynamic_slice_in_dim(a, chip * n, n, axis=axis)

    grads = dict(
        c_ctx=g_c_ctx, w_ada=g_w_ada, b_ada=g_b_ada,
        norm_g=shard_of(ssum["norm_g"].reshape(2, 2, D), 2),
        rec_w_in=g_big["rec_w_in"], rec_conv_w=shard_of(ssum["rec_conv_w"].reshape(1, REC_KW, R), 2),
        rec_conv_b=ssum["rec_conv_b"].reshape(1, R), rec_lambda=shard_of(ssum["rec_lambda"].reshape(1, 2, R), 2),
        rec_w_a=ssum["rec_w_a"].reshape(rec_w_a.shape), rec_b_a=ssum["rec_b_a"].reshape(rec_b_a.shape),
        rec_w_x=ssum["rec_w_x"].reshape(rec_w_x.shape), rec_b_x=ssum["rec_b_x"].reshape(rec_b_x.shape),
        rec_w_out=g_big["rec_w_out"], conf_w_pw1=g_big["conf_w_pw1"],
        conf_b_pw1=shard_of(ssum["conf_b_pw1"].reshape(1, 2 * D), 1),
        conf_conv_w=shard_of(ssum["conf_conv_w"].reshape(1, CONF_KW, D), 2),
        conf_conv_b=shard_of(ssum["conf_conv_b"].reshape(1, D), 1),
        conf_ln_g=shard_of(ssum["conf_ln_g"].reshape(1, D), 1), conf_ln_b=shard_of(ssum["conf_ln_b"].reshape(1, D), 1),
        conf_w_pw2=g_big["conf_w_pw2"], conf_b_pw2=shard_of(ssum["conf_b_pw2"].reshape(1, D), 1),
        mlp_w_in=g_big["mlp_w_in"], mlp_w_out=g_big["mlp_w_out"], final_g=ssum["final_g"].reshape(D))

    rest =[k for k in names if k not in ("w_ada",) + _BIG]
    d_, m_, v_ = _adamw_many("adamw_small", [w[k] for k in rest], [grads[k] for k in rest],
                             [m[k] for k in rest], [v[k] for k in rest])
    for k, dd, mm, vv in zip(rest, d_, m_, v_):
        delta[k], new_m[k], new_v[k] = dd, mm, vv

    return (loss, grad_x[None], *[grads[k] for k in names], *[delta[k] for k in names],
            *[new_m[k] for k in names], *[new_v[k] for k in names])
```

```python
import functools
import math

import jax
import jax.numpy as jnp
from jax import lax
from jax.experimental import pallas as pl
from jax.experimental.pallas import tpu as pltpu

F32 = jnp.float32
BF16 = jnp.bfloat16

D = 1024
T = 2048
TC = 256
TA = T + TC
R = 1280
RH = R // 2
NQ = 4 * RH
FF = 4096
N_BLK = 16
BLK = R // N_BLK
GRID_W = 64
EPS = 1e-6
RG_C = 8.0
CONF_KW = 31
REC_KW = 4
LANE = 128
ROW_TILE = 256
HALO = 16
RG_TILE = 256
PACK_ROWS = 512
MM_TILE = 1024
REC_TILE = TA // 2
CW_REC = 640
CW_CONF = 512
V7X_VMEM_BYTES = 64 * 1024 * 1024
VMEM_LIMIT = V7X_VMEM_BYTES - 8 * 1024 * 1024

ADAM_LR = 0.001
ADAM_B1 = 0.9
ADAM_B2 = 0.999
ADAM_EPS = 1e-08
ADAM_WD = 0.01
ADAM_STEP = 10

MESH = pl.DeviceIdType.MESH
ANY = pl.BlockSpec(memory_space=pl.ANY)


def _sds(shape, dtype):
    return jax.ShapeDtypeStruct(tuple(shape), dtype)


def _pcall(body, **kw):
    return pl.pallas_call(body, **kw)


def _cparams():
    return pltpu.CompilerParams(vmem_limit_bytes=VMEM_LIMIT)


def _full_spec(arr):
    nd = arr.ndim
    return pl.BlockSpec(arr.shape, lambda *ids, _n=nd: (0,) * _n)


def _sum0(v):
    return jnp.sum(v, axis=0, keepdims=True)


def _tiled(name, fn, grid, ins, vecs, outs, vec_outs=(), vec_refs=False):
    n_in, n_vec, n_out = len(ins), len(vecs), len(outs)
    n_grid = len(grid)

    def kern(*refs):
        ids = [pl.program_id(a) for a in range(n_grid)]
        tin = [r[...] for r in refs[:n_in]]
        vin = list(refs[n_in:n_in + n_vec]) if vec_refs else [r[...] for r in refs[n_in:n_in + n_vec]]
        o_refs = refs[n_in + n_vec:n_in + n_vec + n_out]
        a_refs = refs[n_in + n_vec + n_out:]
        tout, incs = fn(ids, tin, vin)
        for r, v in zip(o_refs, tout):
            r[...] = v.astype(r.dtype)
        if a_refs:
            first = functools.reduce(jnp.logical_and, [i == 0 for i in ids])

            @pl.when(first)
            def _():
                for r in a_refs:
                    r[...] = jnp.zeros_like(r)

            for r, v in zip(a_refs, incs):
                r[...] += v

    out_shape = [o for o, _ in outs] + [_sds(s, F32) for s in vec_outs]
    out_specs = [s for _, s in outs] + [
        pl.BlockSpec(tuple(s), lambda *ids, _n=len(s): (0,) * _n) for s in vec_outs]
    res = _pcall(
        kern, name=name, grid=tuple(grid),
        in_specs=[s for _, s in ins] + [_full_spec(v) for v in vecs],
        out_specs=out_specs, out_shape=out_shape, compiler_params=_cparams(),
    )(*[a for a, _ in ins], *vecs)
    return list(res)


def _rows(arr, ncols=None, tm=ROW_TILE, off=0, col=0, clamp_lo=False):
    ncols = arr.shape[1] if ncols is None else ncols
    if clamp_lo:
        return arr, pl.BlockSpec((tm, ncols), lambda i: (jnp.maximum(i + off, 0), col))
    return arr, pl.BlockSpec((tm, ncols), lambda i: (i + off, col))


def _orow(nrows, ncols, dtype, tm=ROW_TILE, off=0, clamp_lo=False):
    if clamp_lo:
        return _sds((nrows, ncols), dtype), pl.BlockSpec((tm, ncols), lambda i: (jnp.maximum(i + off, 0), 0))
    return _sds((nrows, ncols), dtype), pl.BlockSpec((tm, ncols), lambda i: (i + off, 0))


_NN = (((1,), (0,)), ((), ()))
_TN = (((0,), (0,)), ((), ()))
_NT = (((1,), (1,)), ((), ()))


def _mm(name, a, b, dims, grid, a_spec, b_spec, out, acc_shape, extra=(), a_pre=None, epi=None):
    n_k = grid[2]
    n_ex = len(extra)

    def kern(a_ref, b_ref, *rest):
        ex = rest[:n_ex]
        o_refs = rest[n_ex:n_ex + len(out)]
        k = pl.program_id(2)
        av = a_ref[...]
        if a_pre is not None:
            av = a_pre(av)
        part = lax.dot_general(av.astype(BF16), b_ref[...].astype(BF16), dims, preferred_element_type=F32)

        def finish(total):
            vals = [total] if epi is None else epi(total, [e[...] for e in ex])
            for r, v in zip(o_refs, vals):
                r[...] = v.astype(r.dtype)

        if n_k == 1:
            finish(part)
        else:
            acc = rest[-1]

            @pl.when(k == 0)
            def _():
                acc[...] = part

            @pl.when(jnp.logical_and(k > 0, k < n_k - 1))
            def _():
                acc[...] += part

            @pl.when(k == n_k - 1)
            def _():
                finish(acc[...] + part)

    res = _pcall(
        kern, name=name, grid=tuple(grid),
        in_specs=[a_spec, b_spec] + [s for _, s in extra],
        out_specs=[s for _, s in out], out_shape=[o for o, _ in out],
        scratch_shapes=[] if n_k == 1 else [pltpu.VMEM(tuple(acc_shape), F32)], compiler_params=_cparams(),
    )(a, b, *[e for e, _ in extra])
    return list(res)


def _rms(x):
    r = lax.rsqrt(jnp.mean(x * x, axis=-1, keepdims=True) + EPS)
    return x * r, r


def _norm_mod(x, g, sc, sh):
    n, _ = _rms(x)
    return (n * g) * (1.0 + sc) + sh


def _norm_mod_bwd(dh, x, g, sc):
    n, r = _rms(x)
    d_sh = _sum0(dh)
    d_sc = _sum0(dh * (n * g))
    d_g = _sum0(dh * (1.0 + sc) * n)
    dn = dh * (g * (1.0 + sc))
    dx = r * (dn - n * jnp.mean(dn * n, axis=-1, keepdims=True))
    return dx, d_sh, d_sc, d_g


_GELU_K = math.sqrt(2.0 / math.pi)


def _gelu(x):
    t = jnp.tanh(_GELU_K * (x + 0.044715 * x * x * x))
    return 0.5 * x * (1.0 + t), t


def _gelu_grad(x, t):
    return 0.5 * (1.0 + t) + 0.5 * x * (1.0 - t * t) * (_GELU_K * (1.0 + 3.0 * 0.044715 * x * x))


def _sigmoid(x):
    return 0.5 * jnp.tanh(0.5 * x) + 0.5


def _expm1(x):
    p = jnp.full_like(x, 1.0 / 5040.0)
    for c in (1.0 / 720.0, 1.0 / 120.0, 1.0 / 24.0, 1.0 / 6.0, 0.5, 1.0):
        p = p * x + c
    return jnp.where(jnp.abs(x) < 0.3, x * p, jnp.exp(x) - 1.0)


def _softplus_neg(lam):
    return jnp.log1p(jnp.exp(-jnp.abs(lam))) + jnp.maximum(-lam, 0.0)


def _layernorm_parts(x):
    mu = jnp.mean(x, axis=-1, keepdims=True)
    xc = x - mu
    rstd = lax.rsqrt(jnp.mean(xc * xc, axis=-1, keepdims=True) + EPS)
    return xc * rstd, rstd


def _rg_gates(u, wbd, gbias, lam):
    sp = _softplus_neg(lam)
    parts = {}
    for h in range(2):
        uh = u[:, h * RH:(h + 1) * RH]
        g = jnp.dot(uh.astype(BF16), wbd[h], preferred_element_type=F32) + gbias[:, h * NQ:(h + 1) * NQ]
        for d in range(2):
            r = _sigmoid(g[:, (2 * d) * RH:(2 * d + 1) * RH])
            i = _sigmoid(g[:, (2 * d + 1) * RH:(2 * d + 2) * RH])
            sph = sp[d:d + 1, h * RH:(h + 1) * RH]
            la = (-RG_C) * r * sph
            e2 = _expm1(2.0 * la)
            inv_mult = jnp.where(e2 < 0.0, lax.rsqrt(-e2), 0.0)
            parts[(d, h)] = dict(r=r, i=i, la=la, a=jnp.exp(la), e2=e2, mult=-e2 * inv_mult, inv_mult=inv_mult,
                                 uh=uh, sp=sph)
    return parts


def _rg_fwd_fn(ids, tin, vin):
    (u,) = tin
    wbd = vin[0]
    parts = _rg_gates(u, wbd, vin[1][...], vin[2][...])
    outs = []
    for d in range(2):
        a = jnp.concatenate([parts[(d, h)]["a"] for h in range(2)], axis=1)
        b = jnp.concatenate([parts[(d, h)]["mult"] * parts[(d, h)]["i"] * parts[(d, h)]["uh"]
                             for h in range(2)], axis=1)
        outs += [a, b]
    return outs, []


def _rg_bwd_fn(ids, tin, vin):
    u, da_f, db_f, da_r, db_r = tin
    wbd, lam = vin[0], vin[2][...]
    parts = _rg_gates(u, wbd, vin[1][...], lam)
    dab = ((da_f, db_f), (da_r, db_r))
    dsig_lam = -1.0 / (1.0 + jnp.exp(lam))
    du_halves, dpre_halves, dlam = [], [], [[None, None], [None, None]]
    for h in range(2):
        du = jnp.zeros_like(parts[(0, h)]["uh"])
        dpre = []
        for d in range(2):
            p = parts[(d, h)]
            da = dab[d][0][:, h * RH:(h + 1) * RH]
            db = dab[d][1][:, h * RH:(h + 1) * RH]
            d_mult = db * p["i"] * p["uh"]
            d_i = db * p["mult"] * p["uh"]
            du = du + db * p["mult"] * p["i"]
            d_la = da * p["a"] - d_mult * (p["e2"] + 1.0) * p["inv_mult"]
            d_r = d_la * ((-RG_C) * p["sp"])
            dlam[d][h] = _sum0(d_la * ((-RG_C) * p["r"])) * dsig_lam[d:d + 1, h * RH:(h + 1) * RH]
            dpre += [d_r * p["r"] * (1.0 - p["r"]), d_i * p["i"] * (1.0 - p["i"])]
        dpre = jnp.concatenate(dpre, axis=1)
        du = du + lax.dot_general(dpre.astype(BF16), wbd[h], _NT, preferred_element_type=F32)
        du_halves.append(du)
        dpre_halves.append(dpre)
    dpre_all = jnp.concatenate(dpre_halves, axis=1)
    dlam_row = jnp.concatenate([dlam[0][0], dlam[0][1], dlam[1][0], dlam[1][1]], axis=1)
    return [dpre_all, jnp.concatenate(du_halves, axis=1)], [_sum0(dpre_all), dlam_row]


def _tile_flags(i, n_tiles, seq_starts):
    starts_here = functools.reduce(jnp.logical_or, [i == s for s in seq_starts])
    ends_here = functools.reduce(jnp.logical_or, [i + 1 == s for s in seq_starts] + [i + 1 == n_tiles])
    return jnp.logical_not(starts_here), jnp.logical_not(ends_here)


def _halo_specs(col0, cw):
    hb = ROW_TILE // HALO
    prev = pl.BlockSpec((HALO, cw), lambda i, c: (jnp.maximum(i * hb - 1, 0), col0 + c))
    cur = pl.BlockSpec((ROW_TILE, cw), lambda i, c: (i, col0 + c))
    return prev, cur, hb


def _window(prev_ref, cur_ref, next_ref, has_prev, has_next):
    prev = jnp.where(has_prev, prev_ref[...], 0.0)
    nxt = jnp.where(has_next, next_ref[...], 0.0)
    return jnp.concatenate([prev, cur_ref[...], nxt], axis=0)


def _tap_reader(win):
    sub = 8
    n = win.shape[0]
    shifted = {0: win}

    def tap(off):
        s = off % sub
        if s not in shifted:
            shifted[s] = pltpu.roll(win, n - s, axis=0)
        return shifted[s][off - s:off - s + ROW_TILE, :]

    return tap


def _dwconv(name, x, col0, w, bias, pad_left, seq_starts, n_ch, cw=256, flip=False, into=None):
    n_rows = x.shape[0]
    n_tiles = n_rows // ROW_TILE
    n_taps = w.shape[0]
    prev_spec, cur_spec, hb = _halo_specs(col0, cw)
    last_hb = n_rows // HALO - 1
    next_spec = pl.BlockSpec((HALO, cw), lambda i, c: (jnp.minimum((i + 1) * hb, last_hb), col0 + c))
    dest, out_col0 = (None, 0) if into is None else into

    def kern(prev_ref, cur_ref, next_ref, w_ref, b_ref, *rest):
        o_ref = rest[-1]
        has_prev, has_next = _tile_flags(pl.program_id(0), n_tiles, seq_starts)
        win = _window(prev_ref, cur_ref, next_ref, has_prev, has_next)
        tap = _tap_reader(win)
        wv = w_ref[...]
        acc = jnp.zeros((ROW_TILE, cw), F32) + b_ref[...]
        for k in range(n_taps):
            kw = n_taps - 1 - k if flip else k
            acc = acc + wv[kw:kw + 1, :] * tap(HALO + k - pad_left)
        o_ref[...] = acc.astype(o_ref.dtype)

    return _pcall(
        kern, name=name, grid=(n_tiles, n_ch // cw),
        in_specs=[prev_spec, cur_spec, next_spec,
                  pl.BlockSpec((n_taps, cw), lambda i, c: (0, c)), pl.BlockSpec((1, cw), lambda i, c: (0, c))]
        + ([] if dest is None else [ANY]),
        out_specs=pl.BlockSpec((ROW_TILE, cw), lambda i, c: (i, out_col0 + c)),
        out_shape=_sds((n_rows, n_ch), F32) if dest is None else _sds(dest.shape, dest.dtype),
        input_output_aliases={} if dest is None else {5: 0}, compiler_params=_cparams(),
    )(x, x, x, w, bias, *([] if dest is None else [dest]))


def _dwconv_wgrad(name, dy, x, col0, n_taps, pad_left, seq_starts, n_ch, cw=256, dep=None):
    deps = [] if dep is None else [dep]
    n_rows = dy.shape[0]
    n_tiles = n_rows // ROW_TILE
    n_out = -(-(n_taps + 1) // 8) * 8
    prev_spec, cur_spec, hb = _halo_specs(col0, cw)
    last_hb = n_rows // HALO - 1
    next_spec = pl.BlockSpec((HALO, cw), lambda c, i: (jnp.minimum((i + 1) * hb, last_hb), col0 + c))
    prev_spec = pl.BlockSpec((HALO, cw), lambda c, i: (jnp.maximum(i * hb - 1, 0), col0 + c))
    cur_spec = pl.BlockSpec((ROW_TILE, cw), lambda c, i: (i, col0 + c))

    def kern(dy_ref, prev_ref, cur_ref, next_ref, *rest):
        o_ref = rest[-1]
        i = pl.program_id(1)
        has_prev, has_next = _tile_flags(i, n_tiles, seq_starts)
        win = _window(prev_ref, cur_ref, next_ref, has_prev, has_next)
        dyv = dy_ref[...]
        tap = _tap_reader(win)
        rid = lax.broadcasted_iota(jnp.int32, (n_out, cw), 0)
        inc = jnp.where(rid == n_taps, _sum0(dyv), 0.0)
        for k in range(n_taps):
            inc = inc + jnp.where(rid == k, _sum0(dyv * tap(HALO + k - pad_left)), 0.0)

        @pl.when(i == 0)
        def _():
            o_ref[...] = jnp.zeros_like(o_ref)

        o_ref[...] += inc

    return _pcall(
        kern, name=name, grid=(n_ch // cw, n_tiles),
        in_specs=[pl.BlockSpec((ROW_TILE, cw), lambda c, i: (i, c)), prev_spec, cur_spec, next_spec]
        + [pl.BlockSpec(d.shape, lambda c, i: (0, 0)) for d in deps],
        out_specs=pl.BlockSpec((n_out, cw), lambda c, i: (0, c)),
        out_shape=_sds((n_out, n_ch), F32), compiler_params=_cparams(),
    )(dy, x, x, x, *deps)


N_SCAN = TA // ROW_TILE


def _rev_block(j):
    return jnp.where(j == 0, 0, N_SCAN - j)


def _scan_fwd(a_f, b_f, a_r, b_r):
    fwd_spec = pl.BlockSpec((ROW_TILE, R), lambda i: (i, 0))
    rev_spec = pl.BlockSpec((ROW_TILE, R), lambda i: (_rev_block(i), 0))
    hin_spec = pl.BlockSpec((None, 1, R), lambda i: (i, 0, 0))

    def kern(af, bf, ar, br, yf, yr, hin_f, hin_r, hf_s, hr_s):
        @pl.when(pl.program_id(0) == 0)
        def _():
            hf_s[...] = jnp.zeros_like(hf_s)
            hr_s[...] = jnp.zeros_like(hr_s)

        hin_f[...] = hf_s[...]
        hin_r[...] = hr_s[...]

        def step(s8, carry):
            hf, hr = carry
            t0 = pl.multiple_of(s8 * 8, 8)
            for q in range(8):
                tf = t0 + q
                hf = af[pl.ds(tf, 1), :] * hf + bf[pl.ds(tf, 1), :]
                yf[pl.ds(tf, 1), :] = hf
                tr = ROW_TILE - 1 - tf
                hr = ar[pl.ds(tr, 1), :] * hr + br[pl.ds(tr, 1), :]
                yr[pl.ds(tr, 1), :] = hr
            return hf, hr

        hf, hr = lax.fori_loop(0, ROW_TILE // 8, step, (hf_s[...], hr_s[...]))
        hf_s[...] = hf
        hr_s[...] = hr

    return _pcall(
        kern, name="scan_fwd", grid=(N_SCAN,),
        in_specs=[fwd_spec, fwd_spec, rev_spec, rev_spec],
        out_specs=[fwd_spec, rev_spec, hin_spec, hin_spec],
        out_shape=[_sds((TA, R), F32), _sds((TA, R), F32), _sds((N_SCAN, 1, R), F32), _sds((N_SCAN, 1, R), F32)],
        scratch_shapes=[pltpu.VMEM((1, R), F32), pltpu.VMEM((1, R), F32)], compiler_params=_cparams(),
    )(a_f, b_f, a_r, b_r)


def _scan_bwd(dy, a_f, y_f, hin_f, a_r, y_r, hin_r):
    fwd_spec = pl.BlockSpec((ROW_TILE, R), lambda i: (N_SCAN - 1 - i, 0))
    rev_spec = pl.BlockSpec((ROW_TILE, R), lambda i: (_rev_block(N_SCAN - 1 - i), 0))
    hin_spec = pl.BlockSpec((None, 1, R), lambda i: (N_SCAN - 1 - i, 0, 0))
    last = ROW_TILE - 1

    def kern(dyf, af, yf, hf0, dyr, ar, yr, hr0, daf, dbf, dar, dbr, gf_s, anf_s, gr_s, anr_s):
        @pl.when(pl.program_id(0) == 0)
        def _():
            for r in (gf_s, anf_s, gr_s, anr_s):
                r[...] = jnp.zeros_like(r)

        def one(dy_ref, a_ref, y_ref, da_ref, db_ref, g, an, p, pprev):
            gnew = dy_ref[pl.ds(p, 1), :] + an * g
            db_ref[pl.ds(p, 1), :] = gnew
            da_ref[pl.ds(p, 1), :] = gnew * y_ref[pl.ds(pprev, 1), :]
            return gnew, a_ref[pl.ds(p, 1), :]

        def step(s8, carry):
            gf, anf, gr, anr = carry
            base = s8 * 8
            for q in range(8):
                s = last - (base + q)
                gf, anf = one(dyf, af, yf, daf, dbf, gf, anf, s, s - 1)
                gr, anr = one(dyr, ar, yr, dar, dbr, gr, anr, last - s, last - s + 1)
            return gf, anf, gr, anr

        carry = (gf_s[...], anf_s[...], gr_s[...], anr_s[...])
        carry = lax.fori_loop(0, ROW_TILE // 8 - 1, step, carry)
        gf, anf, gr, anr = carry
        for s in range(7, 0, -1):
            gf, anf = one(dyf, af, yf, daf, dbf, gf, anf, s, s - 1)
            gr, anr = one(dyr, ar, yr, dar, dbr, gr, anr, last - s, last - s + 1)
        gf0 = dyf[0:1, :] + anf * gf
        dbf[0:1, :] = gf0
        daf[0:1, :] = gf0 * hf0[...]
        gr0 = dyr[last:last + 1, :] + anr * gr
        dbr[last:last + 1, :] = gr0
        dar[last:last + 1, :] = gr0 * hr0[...]
        gf_s[...] = gf0
        anf_s[...] = af[0:1, :]
        gr_s[...] = gr0
        anr_s[...] = ar[last:last + 1, :]

    return _pcall(
        kern, name="scan_bwd", grid=(N_SCAN,),
        in_specs=[fwd_spec, fwd_spec, fwd_spec, hin_spec, rev_spec, rev_spec, rev_spec, hin_spec],
        out_specs=[fwd_spec, fwd_spec, rev_spec, rev_spec],
        out_shape=[_sds((TA, R), F32)] * 4,
        scratch_shapes=[pltpu.VMEM((1, R), F32)] * 4, compiler_params=_cparams(),
    )(dy, a_f, y_f, hin_f, dy, a_r, y_r, hin_r)


def _me():
    return lax.axis_index("x"), lax.axis_index("y"), lax.axis_index("c")


def _other_chips(mx, my):
    return [(1 - mx, my), (mx, 1 - my), (1 - mx, 1 - my)]


def _rcopy(src, dst, ssem, rsem, dev):
    return pltpu.make_async_remote_copy(src_ref=src, dst_ref=dst, send_sem=ssem, recv_sem=rsem,
                                        device_id=dev, device_id_type=MESH)


def _peers7(mx, my, mc):
    peers = []
    for k in range(1, 8):
        peers.append((1 - mx if (k >> 2) & 1 else mx, 1 - my if (k >> 1) & 1 else my, 1 - mc if k & 1 else mc))
    return peers


def _halves_of(refs, c):
    out = []
    for r in refs:
        out += [r.at[c]] if len(r.shape) == 3 else [r.at[l, c] for l in range(r.shape[0])]
    return out


def _share_start(name, fulls, after):
    n = len(fulls)
    n_cp = sum(1 if f.ndim == 3 else f.shape[0] for f in fulls)

    def kern(*refs):
        o = refs[n + 1:2 * n + 1]
        ssem, rsem, token = refs[2 * n + 1:]
        mx, my, mc = _me()
        for q, half in enumerate(_halves_of(o, mc)):
            _rcopy(half, half, ssem.at[q], rsem.at[q], (mx, my, 1 - mc)).start()
        token[...] = jnp.zeros_like(token)

    dma = pltpu.SemaphoreType.DMA
    res = _pcall(
        kern, name=name, in_specs=[ANY] * (n + 1),
        out_specs=[ANY] * n + [SEM, SEM, pl.BlockSpec(memory_space=pltpu.VMEM)],
        out_shape=[_sds(f.shape, f.dtype) for f in fulls] + [dma((n_cp,)), dma((n_cp,)), _sds((8, LANE), F32)],
        input_output_aliases={t: t for t in range(n)},
        compiler_params=pltpu.CompilerParams(has_side_effects=_DATAFLOW),
    )(*fulls, after)
    return (list(res[:n]), res[n], res[n + 1]), res[n + 2]


def _share_wait(name, fulls, ssem, rsem, after):
    n = len(fulls)

    def kern(*refs):
        o = refs[:n]
        ssem_ref, rsem_ref = refs[n], refs[n + 1]
        mx, my, mc = _me()
        sib = (mx, my, 1 - mc)
        for q, (theirs, mine) in enumerate(zip(_halves_of(o, 1 - mc), _halves_of(o, mc))):
            _rcopy(theirs, theirs, ssem_ref.at[q], rsem_ref.at[q], sib).wait_recv()
            _rcopy(mine, mine, ssem_ref.at[q], rsem_ref.at[q], sib).wait_send()

    return list(_pcall(
        kern, name=name, in_specs=[ANY] * n + [SEM, SEM, ANY], out_specs=[ANY] * n,
        out_shape=[_sds(f.shape, f.dtype) for f in fulls], input_output_aliases={t: t for t in range(n)},
        compiler_params=pltpu.CompilerParams(has_side_effects=_DATAFLOW),
    )(*fulls, ssem, rsem, after))


def _tiled_sp(name, fn, grid, sp, ins, outs, into=None):
    n_in = len(ins)
    dest = [] if into is None else [into]

    def kern(sp_ref, *refs):
        tout = fn([r[...] for r in refs[:n_in]])
        for r, v in zip(refs[n_in + len(dest):], tout):
            r[...] = v.astype(r.dtype)

    gs = pltpu.PrefetchScalarGridSpec(num_scalar_prefetch=1, grid=tuple(grid),
                                      in_specs=[s for _, s in ins] + [ANY] * len(dest), out_specs=[s for _, s in outs])
    res = _pcall(kern, name=name, grid_spec=gs, out_shape=[o for o, _ in outs], compiler_params=_cparams(),
                 input_output_aliases={1 + n_in: 0} if dest else {})(sp, *[a for a, _ in ins], *dest)
    return list(res)


def _row_tile(rows, cols, itemsize=4, budget=2 * 1024 * 1024):
    tr = rows
    while tr * cols * itemsize > budget and tr % 32 == 0:
        tr //= 2
    return tr


def _place_big(shards, place, dep=None):
    slots = []
    for tag, s, layer in shards:
        rr, cc = s.shape[2], s.shape[3]
        tr = _row_tile(rr, cc)
        (slot,) = _tiled_sp(
            f"place_{tag}", lambda tin: [tin[0]], (2, rr // tr), place,
            [(s, pl.BlockSpec((None, None, tr, cc), lambda h, i, sp, layer=layer: (layer, h, i, 0)))]
            + [(d, pl.BlockSpec(d.shape, lambda h, i, sp: (0, 0))) for d in _behind(dep)],
            [(_sds((4, 2, rr, cc), BF16), pl.BlockSpec((None, None, tr, cc), lambda h, i, sp: (sp[0], h, i, 0)))])
        slots.append(slot)
    return slots


def _allreduce_small_begin(vec, place, after, during):
    hr = vec.shape[0] // 2
    tr = _row_tile(hr, LANE)
    blk = (None, None, tr, LANE)
    (pair,) = _tiled_sp(
        "small_place", lambda tin: [tin[0]], (2, hr // tr), place,
        [(vec.reshape(2, hr, LANE), pl.BlockSpec((None, tr, LANE), lambda h, i, sp: (h, i, 0)))],
        [(_sds((2, 2, hr, LANE), F32), pl.BlockSpec(blk, lambda h, i, sp: (sp[1], h, i, 0)))])
    crossing, token = _share_start("small_share_start", [pair.reshape(2, 2 * hr, LANE)], place)
    (pair,) = _share_wait("small_share_wait", *crossing, during(token))
    pair = pair.reshape(2, 2, hr, LANE)
    (slot,) = _tiled_sp(
        "small_pair_add", lambda tin: [tin[0] + tin[1]], (2, hr // tr), place,
        [(pair, pl.BlockSpec(blk, lambda h, i, sp: (0, h, i, 0))),
         (pair, pl.BlockSpec(blk, lambda h, i, sp: (1, h, i, 0)))],
        [(_sds((4, 2, hr, LANE), F32), pl.BlockSpec(blk, lambda h, i, sp: (sp[0], h, i, 0)))])
    fly, sems, token = _gather_start("small_start", [slot], ((0,),), after)
    return (fly, sems), token


def _allreduce_small_end(state, after):
    fly, sems = state
    (chips,) = _swap_halves("small_swap", _gather_wait("small_wait", fly, *sems, after))
    hr = chips.shape[2]
    tr = _row_tile(hr, LANE)
    blk = (None, None, tr, LANE)
    (total,) = _tiled(
        "small_chip_sum", lambda ids, tin, vin: ([((tin[0] + tin[1]) + tin[2]) + tin[3]], []), (2, hr // tr),
        [(chips, pl.BlockSpec(blk, lambda h, i, _j=j: (_j, h, i, 0))) for j in range(4)], [],
        [(_sds((2, hr, LANE), F32), pl.BlockSpec((None, tr, LANE), lambda h, i: (h, i, 0)))])
    return total.reshape(2 * hr, LANE)


SEM =pl.BlockSpec(memory_space=pltpu.SEMAPHORE)
_DATAFLOW = pltpu.SideEffectType.DATAFLOW_SIDE_EFFECTING


def _gather_start(name, slots, groups, after):
    n = len(slots)

    def kern(*refs):
        o = refs[n + 1:2 * n + 1]
        sems, token = refs[2 * n + 1:-1], refs[-1]
        mx, my, mc = _me()
        j0 = 2 * mx + my
        for gi, grp in enumerate(groups):
            for k, t in enumerate(grp):
                for q, (qx, qy) in enumerate(_other_chips(mx, my)):
                    _rcopy(o[t].at[j0, mc], o[t].at[j0, mc], sems[2 * gi].at[3 * k + q],
                           sems[2 * gi + 1].at[3 * k + q], (qx, qy, mc)).start()
        token[...] = jnp.zeros_like(token)

    sem_shapes = []
    for grp in groups:
        sem_shapes += [pltpu.SemaphoreType.DMA((3 * len(grp),))] * 2
    res = _pcall(
        kern, name=name, in_specs=[ANY] * (n + 1),
        out_specs=[ANY] * n + [SEM] * len(sem_shapes) + [pl.BlockSpec(memory_space=pltpu.VMEM)],
        out_shape=[_sds(w.shape, w.dtype) for w in slots] + sem_shapes + [_sds((8, LANE), F32)],
        input_output_aliases={t: t for t in range(n)},
        compiler_params=pltpu.CompilerParams(has_side_effects=_DATAFLOW),
    )(*slots, after)
    return list(res[:n]), list(res[n:-1]), res[-1]


def _gather_wait(name, bufs, ssem, rsem, after):
    n = len(bufs)
    afters = list(after) if isinstance(after, (list, tuple)) else [after]

    def kern(*refs):
        b = refs[:n]
        ssem_ref, rsem_ref = refs[n], refs[n + 1]
        mx, my, mc = _me()
        j0 = 2 * mx + my
        for k in range(n):
            for q, (qx, qy) in enumerate(_other_chips(mx, my)):
                jq = 2 * qx + qy
                _rcopy(b[k].at[jq, mc], b[k].at[jq, mc], ssem_ref.at[3 * k + q], rsem_ref.at[3 * k + q],
                       (qx, qy, mc)).wait_recv()
                _rcopy(b[k].at[j0, mc], b[k].at[j0, mc], ssem_ref.at[3 * k + q], rsem_ref.at[3 * k + q],
                       (qx, qy, mc)).wait_send()

    return list(_pcall(
        kern, name=name, in_specs=[ANY] * n + [SEM, SEM] + [ANY] * len(afters), out_specs=[ANY] * n,
        out_shape=[_sds(w.shape, w.dtype) for w in bufs], input_output_aliases={k: k for k in range(n)},
        compiler_params=pltpu.CompilerParams(has_side_effects=_DATAFLOW),
    )(*bufs, ssem, rsem, *afters))


def _swap_halves(name, bufs):
    n = len(bufs)

    def kern(*refs):
        o = refs[n:2 * n]
        ss, rs = refs[2 * n:]
        mx, my, mc = _me()
        sib = (mx, my, 1 - mc)
        sends = []
        for k in range(n):
            for q, (qx, qy) in enumerate(_other_chips(mx, my)):
                jq = 2 * qx + qy
                cp = _rcopy(o[k].at[jq, mc], o[k].at[jq, mc], ss.at[3 * k + q], rs.at[3 * k + q], sib)
                cp.start()
                sends.append(cp)
        for k in range(n):
            for q, (qx, qy) in enumerate(_other_chips(mx, my)):
                jq = 2 * qx + qy
                _rcopy(o[k].at[jq, 1 - mc], o[k].at[jq, 1 - mc], ss.at[3 * k + q], rs.at[3 * k + q], sib).wait_recv()
        for cp in sends:
            cp.wait_send()

    dma = pltpu.SemaphoreType.DMA
    return list(_pcall(
        kern, name=name, in_specs=[ANY] * n, out_specs=[ANY] * n,
        out_shape=[_sds(w.shape, w.dtype) for w in bufs], input_output_aliases={k: k for k in range(n)},
        scratch_shapes=[dma((3 * n,)), dma((3 * n,))],
    )(*bufs))


def _swap_start(name, bufs, after):
    n = len(bufs)

    def kern(*refs):
        o = refs[n + 1:2 * n + 1]
        ssem, rsem, token = refs[2 * n + 1:]
        mx, my, mc = _me()
        for k in range(n):
            for q, (qx, qy) in enumerate(_other_chips(mx, my)):
                jq = 2 * qx + qy
                _rcopy(o[k].at[jq, mc], o[k].at[jq, mc], ssem.at[3 * k + q], rsem.at[3 * k + q], (mx, my, 1 - mc)).start()
        token[...] = jnp.zeros_like(token)

    dma = pltpu.SemaphoreType.DMA
    res = _pcall(
        kern, name=name, in_specs=[ANY] * (n + 1),
        out_specs=[ANY] * n + [SEM, SEM, pl.BlockSpec(memory_space=pltpu.VMEM)],
        out_shape=[_sds(w.shape, w.dtype) for w in bufs] + [dma((3 * n,)), dma((3 * n,)), _sds((8, LANE), F32)],
        input_output_aliases={k: k for k in range(n)},
        compiler_params=pltpu.CompilerParams(has_side_effects=_DATAFLOW),
    )(*bufs, after)
    return (list(res[:n]), res[n], res[n + 1]), res[n + 2]


def _swap_wait(name, bufs, ssem, rsem, after):
    n = len(bufs)

    def kern(*refs):
        b = refs[:n]
        ssem_ref, rsem_ref = refs[n], refs[n + 1]
        mx, my, mc = _me()
        sib = (mx, my, 1 - mc)
        for k in range(n):
            for q, (qx, qy) in enumerate(_other_chips(mx, my)):
                jq = 2 * qx + qy
                _rcopy(b[k].at[jq, 1 - mc], b[k].at[jq, 1 - mc], ssem_ref.at[3 * k + q], rsem_ref.at[3 * k + q],
                       sib).wait_recv()
                _rcopy(b[k].at[jq, mc], b[k].at[jq, mc], ssem_ref.at[3 * k + q], rsem_ref.at[3 * k + q],
                       sib).wait_send()

    return list(_pcall(
        kern, name=name, in_specs=[ANY] * n + [SEM, SEM, ANY], out_specs=[ANY] * n,
        out_shape=[_sds(w.shape, w.dtype) for w in bufs], input_output_aliases={k: k for k in range(n)},
        compiler_params=pltpu.CompilerParams(has_side_effects=_DATAFLOW),
    )(*bufs, ssem, rsem, after))


def _to_sibling(mx, my, mc):
    return [((j, 1 - mc), j, (mx, my, 1 - mc)) for j in range(4)]


def _to_chips(mx, my, mc):
    return [((2 * qx + qy,), q, (qx, qy, mc)) for q, (qx, qy) in enumerate(_other_chips(mx, my))]


def _to_all7(mx, my, mc):
    return [((0,), k, dev) for k, dev in enumerate(_peers7(mx, my, mc))]


def _send_start(name, srcs, plan, land_shapes, after):
    n = len(srcs)
    per = len(plan(0, 0, 0))

    def kern(*refs):
        s, land = refs[n + 1:2 * n + 1], refs[2 * n + 1:3 * n + 1]
        ssem, rsem, token = refs[3 * n + 1:]
        for k in range(n):
            for q, (idx, slot, dev) in enumerate(plan(*_me())):
                _rcopy(s[k].at[idx], land[k].at[slot], ssem.at[per * k + q], rsem.at[per * k + q], dev).start()
        token[...] = jnp.zeros_like(token)

    dma = pltpu.SemaphoreType.DMA
    res = _pcall(
        kern, name=name, in_specs=[ANY] * (n + 1),
        out_specs=[ANY] * (2 * n) + [SEM, SEM, pl.BlockSpec(memory_space=pltpu.VMEM)],
        out_shape=[_sds(s.shape, s.dtype) for s in srcs] + [_sds(ls, s.dtype) for ls, s in zip(land_shapes, srcs)]
        + [dma((per * n,)), dma((per * n,)), _sds((8, LANE), F32)],
        input_output_aliases={k: k for k in range(n)},
        compiler_params=pltpu.CompilerParams(has_side_effects=_DATAFLOW),
    )(*srcs, after)
    return (list(res[:n]), list(res[n:2 * n]), res[2 * n], res[2 * n + 1]), res[2 * n + 2]


def _send_wait(name, srcs, lands, ssem, rsem, plan, after):
    n = len(srcs)
    per = len(plan(0, 0, 0))

    def kern(*refs):
        s, land = refs[:n], refs[n:2 * n]
        ssem_ref, rsem_ref = refs[2 * n], refs[2 * n + 1]
        for k in range(n):
            for q, (idx, slot, dev) in enumerate(plan(*_me())):
                cp = _rcopy(s[k].at[idx], land[k].at[slot], ssem_ref.at[per * k + q], rsem_ref.at[per * k + q], dev)
                cp.wait_recv()
                cp.wait_send()

    res = _pcall(
        kern, name=name, in_specs=[ANY] * (2 * n) + [SEM, SEM, ANY], out_specs=[ANY] * (2 * n),
        out_shape=[_sds(a.shape, a.dtype) for a in list(srcs) + list(lands)],
        input_output_aliases={k: k for k in range(2 * n)},
        compiler_params=pltpu.CompilerParams(has_side_effects=_DATAFLOW),
    )(*srcs, *lands, ssem, rsem, after)
    return list(res[:n]), list(res[n:])


def _reduce_begin(tag, parts, after):
    return _send_start(f"pair_start_{tag}", parts, _to_sibling, [(4,) + p.shape[2:] for p in parts], after)


def _reduce_mid(tag, pairing, place, after):
    parts, theirs = _send_wait(f"pair_wait_{tag}", *pairing, _to_sibling, after)
    sums = []
    for k, (p, o) in enumerate(zip(parts, theirs)):
        rr, cc = p.shape[2], p.shape[3]
        tr = _row_tile(rr, cc)
        (s_k,) = _tiled_sp(
            f"pair_add_{tag}{k}", lambda tin: [tin[0].astype(F32) + tin[1].astype(F32)], (4, rr // tr), place,
            [(p, pl.BlockSpec((None, None, tr, cc), lambda j, i, sp: (j, sp[1], i, 0))),
             (o, pl.BlockSpec((None, tr, cc), lambda j, i, sp: (j, i, 0)))],
            [(_sds((4, rr, cc), BF16), pl.BlockSpec((None, tr, cc), lambda j, i, sp: (j, i, 0)))])
        sums.append(s_k)
    return _send_start(f"chips_start_{tag}", sums, _to_chips, [(3,) + s.shape[1:] for s in sums], theirs[0])


def _reduce_end(tag, flying, place, after, layer=None, into=None):
    sums, lands = _send_wait(f"chips_wait_{tag}", *flying, _to_chips, after)
    fulls = []
    for k, (s, q) in enumerate(zip(sums, lands)):
        rr, cc = q.shape[1], q.shape[2]
        tr = _row_tile(rr, cc)

        def add4(tin):
            return [((tin[0].astype(F32) + tin[1].astype(F32)) + tin[2].astype(F32)) + tin[3].astype(F32)]

        ins = [(s, pl.BlockSpec((None, tr, cc), lambda i, sp: (sp[0], i, 0)))]
        ins += [(q, pl.BlockSpec((None, tr, cc), lambda i, sp, _k=kk: (_k, i, 0))) for kk in range(3)]
        if layer is None:
            out = (_sds((2, rr, cc), F32), pl.BlockSpec((None, tr, cc), lambda i, sp: (sp[1], i, 0)))
        else:
            out = (_sds((layer[1], 2, rr, cc), F32),
                   pl.BlockSpec((None, None, tr, cc), lambda i, sp, _l=layer[0]: (_l, sp[1], i, 0)))
        (f_k,) = _tiled_sp(f"chip_add_{tag}{k}", add4, (rr // tr,), place, ins, [out],
                           None if into is None else into[k])
        fulls.append(f_k)
    return fulls


def _pack(parts, PACK_ROWS=PACK_ROWS):
    flat, offs, pos = [], [], 0
    for p in parts:
        v = p.reshape(-1).astype(F32)
        n = -(-v.shape[0] // LANE) * LANE
        flat.append(jnp.pad(v, (0, n - v.shape[0])))
        offs.append((pos, v.shape[0], p.shape))
        pos += n
    total = -(-pos // (PACK_ROWS * LANE)) * PACK_ROWS * LANE
    flat.append(jnp.zeros((total - pos,), F32))
    return jnp.concatenate(flat).reshape(-1, LANE), offs


def _unpack(vec, offs):
    v = vec.reshape(-1)
    return [v[p:p + n].reshape(shape) for p, n, shape in offs]


def _adamw_math(wv, gv, mv, vv):
    bc1 = 1.0 - ADAM_B1 ** ADAM_STEP
    bc2 = 1.0 - ADAM_B2 ** ADAM_STEP
    mn = ADAM_B1 * mv + (1.0 - ADAM_B1) * gv
    vn = ADAM_B2 * vv + (1.0 - ADAM_B2) * (gv * gv)
    delta = -ADAM_LR * ((mn / bc1) / (jnp.sqrt(vn / bc2) + ADAM_EPS) + ADAM_WD * wv)
    return delta, mn, vn


def _adamw(name, w, g, m, v, dep=None):
    rows, cols = w.shape
    tr = rows
    for cand in (512, 256, 128, 64, 32, 16, 8):
        if rows % cand == 0 and cand * cols * 4 <= 2 * 1024 * 1024:
            tr = cand
            break

    def fn(ids, tin, vin):
        return list(_adamw_math(*tin)), []

    spec = pl.BlockSpec((tr, cols), lambda i: (i, 0))
    outs = [(_sds((rows, cols), F32), spec)] * 3
    return _tiled(name, fn, (rows // tr,), [(a, spec) for a in (w, g, m, v)], _behind(dep), outs)


def _adamw_many(name, ws, gs, ms, vs):
    n = len(ws)
    views = [(-1, a.shape[-1]) if a.ndim > 1 else (1, -1) for a in ws]
    flat = lambda arrs: [a.reshape(vw) for a, vw in zip(arrs, views)]

    def kern(*refs):
        ins, outs = refs[:4 * n], refs[4 * n:]
        for t in range(n):
            res = _adamw_math(*[ins[q * n + t][...] for q in range(4)])
            for q in range(3):
                outs[q * n + t][...] = res[q]

    shapes = [_sds(a.shape, F32) for a in flat(ws)]
    res = _pcall(kern, name=name, out_shape=shapes * 3, compiler_params=_cparams(),
                 )(*flat(ws), *flat(gs), *flat(ms), *flat(vs))
    back = lambda part: [a.reshape(w.shape) for a, w in zip(part, ws)]
    return back(res[:n]), back(res[n:2 * n]), back(res[2 * n:])


def _pos_embed():
    n_rows = T // GRID_W
    q = D // 4
    omega = 1.0 / (10000.0 ** (jnp.arange(q, dtype=F32) / q))
    er = jnp.arange(n_rows, dtype=jnp.int32).astype(F32)[:, None] * omega[None, :]
    ec = jnp.arange(GRID_W, dtype=jnp.int32).astype(F32)[:, None] * omega[None, :]
    by_row = jnp.concatenate([jnp.sin(er), jnp.cos(er)], axis=-1)
    by_col = jnp.concatenate([jnp.sin(ec), jnp.cos(ec)], axis=-1)
    return jnp.concatenate([jnp.repeat(by_row, GRID_W, axis=0), jnp.tile(by_col, (n_rows, 1))], axis=-1)


def _dense_gates(w_a, w_x):
    rows = jnp.stack([w_a[0], w_x[0], w_a[1], w_x[1]]).reshape(4, 2, RH, BLK)
    mask, spread = _block_mask(), _block_spread().T.astype(BF16)

    def kern(r_ref, m_ref, s_ref, o_ref):
        tiled = jnp.dot(r_ref[...].astype(BF16), s_ref[...], preferred_element_type=F32)
        o_ref[...] = (tiled * m_ref[...]).astype(o_ref.dtype)

    return _pcall(
        kern, name="gates_dense", grid=(2, 4),
        in_specs=[pl.BlockSpec((None, None, RH, BLK), lambda h, q: (q, h, 0, 0)),
                  pl.BlockSpec((RH, RH), lambda h, q: (0, 0)), pl.BlockSpec((BLK, RH), lambda h, q: (0, 0))],
        out_specs=pl.BlockSpec((None, RH, RH), lambda h, q: (h, 0, q)),
        out_shape=_sds((2, RH, NQ), BF16),
    )(rows, mask, spread)


def _block_mask():
    r = lax.broadcasted_iota(jnp.int32, (RH, RH), 0) // BLK
    c = lax.broadcasted_iota(jnp.int32, (RH, RH), 1) // BLK
    return (r == c).astype(F32)


def _block_spread():
    c = lax.broadcasted_iota(jnp.int32, (RH, BLK), 0) % BLK
    j = lax.broadcasted_iota(jnp.int32, (RH, BLK), 1)
    return (c == j).astype(F32)


def _fold_blocks(dense, mask, spread):
    return jnp.dot(dense * mask, spread, preferred_element_type=F32, precision=lax.Precision.HIGHEST)


def _gate_block_grads(folded):
    per = N_BLK // 2
    kinds = [jnp.concatenate([folded[h, q].reshape(per, BLK, BLK) for h in range(2)], axis=0) for q in range(4)]
    return jnp.stack([kinds[0], kinds[2]]), jnp.stack([kinds[1], kinds[3]])


def _gate_bias_dense(b_a, b_x):
    cols = []
    for h in range(2):
        for src in (b_a[0], b_x[0], b_a[1], b_x[1]):
            cols.append(src.reshape(R)[h * RH:(h + 1) * RH])
    return jnp.concatenate(cols).reshape(1, 2 * NQ)


def _gate_bias_grads(dgb):
    v = dgb.reshape(2, 4, RH)
    kinds = [jnp.concatenate([v[0, q], v[1, q]]).reshape(N_BLK, BLK) for q in range(4)]
    return jnp.stack([kinds[0], kinds[2]]), jnp.stack([kinds[1], kinds[3]])


def _residual_epilogue(next_norm):
    def epi(acc, ex):
        x_new = ex[0] + ex[1] * acc
        outs = [acc, x_new]
        if next_norm:
            outs.append(_norm_mod(x_new, ex[-3], ex[-2], ex[-1]))
        return outs
    return epi


def _mlp_fwd(tag, x_in, h, gate, w_in, w_out, next_norm=None, dep=None):
    tm = MM_TILE
    (r,) = _mm(f"{tag}_in", h, w_in, _NN, (T // tm, 4, 1),
               pl.BlockSpec((tm, D), lambda i, j, k: (i, 0)), pl.BlockSpec((None, D, D), lambda i, j, k: (j, 0, 0)),
               [(_sds((T, FF), BF16), pl.BlockSpec((tm, D), lambda i, j, k: (i, j)))], (tm, D),
               extra=[(d_, _full_spec(d_)) for d_ in _behind(dep)], epi=lambda acc, ex: [jnp.maximum(acc, 0.0)])
    row_spec = pl.BlockSpec((tm, D), lambda i, j, k: (i, 0))
    outs = [(_sds((T, D), F32), row_spec)] * 2 + ([(_sds((T, D), BF16), row_spec)] if next_norm else [])
    res = _mm(f"{tag}_out", r, w_out, _NN, (T // tm, 1, FF // D),
              pl.BlockSpec((tm, D), lambda i, j, k: (i, k)), pl.BlockSpec((D, D), lambda i, j, k: (k, 0)),
              outs, (tm, D),
              extra=[(x_in, row_spec), (gate, _full_spec(gate))] + [(v, _full_spec(v)) for v in next_norm or ()],
              a_pre=lambda a: a * a, epi=_residual_epilogue(next_norm))
    return dict(h=h, r=r, o=res[0], x_in=x_in), res[1], (res[2] if next_norm else None)


def _behind(dep):
    return [] if dep is None else [dep]


def _gate_bwd(tag, dx, o, gate, dep=None):
    def fn(ids, t, v):
        d_o = t[0] * v[0]
        return [d_o], [_sum0(t[0] * t[1]), _sum0(d_o)]
    return _tiled(f"{tag}_gate_bwd", fn, (T // ROW_TILE,), [_rows(dx), _rows(o)], [gate] + _behind(dep),
                  [_orow(T, D, BF16)], [(1, D), (1, D)])


def _norm_bwd(tag, dx_res, dh, dh_off, x, g_norm, sc, with_dx=True, dep=None):
    n_t = x.shape[0] // ROW_TILE

    def fn(ids, t, v):
        if with_dx:
            dres, dhv, xv = t
        else:
            dhv, xv = t
        dxv, d_sh, d_sc, d_g = _norm_mod_bwd(dhv, xv, v[0], v[1])
        return ([dres + dxv] if with_dx else []), [d_sh, d_sc, d_g]

    ins = ([_rows(dx_res)] if with_dx else []) + [_rows(dh, off=dh_off), _rows(x)]
    outs = [_orow(x.shape[0], D, F32)] if with_dx else []
    return _tiled(f"{tag}_norm_bwd", fn, (n_t,), ins, [g_norm, sc] + _behind(dep), outs, [(1, D)] * 3)


def _mlp_bwd(tag, dx, saved, g_norm, sc, gate, w_in, w_out, dep=None):
    d_o, d_gate, _ = _gate_bwd(tag, dx, saved["o"], gate, dep)
    tm = MM_TILE
    r = saved["r"]
    (da,) = _mm(f"{tag}_dz", d_o, w_out, _NT, (T // tm, FF // D, 1),
                pl.BlockSpec((tm, D), lambda i, j, k: (i, 0)), pl.BlockSpec((D, D), lambda i, j, k: (j, 0)),
                [(_sds((T, FF), BF16), pl.BlockSpec((tm, D), lambda i, j, k: (i, j)))], (tm, D),
                extra=[(r, pl.BlockSpec((tm, D), lambda i, j, k: (i, j)))],
                epi=lambda acc, ex: [acc * (2.0 * ex[0].astype(F32))])
    tk = MM_TILE
    (dw_out,) = _mm(f"{tag}_dwout", r, d_o, _TN, (FF // tm, 1, T // tk),
                    pl.BlockSpec((tk, tm), lambda i, j, k: (k, i)), pl.BlockSpec((tk, D), lambda i, j, k: (k, 0)),
                    [(_sds((FF, D), BF16), pl.BlockSpec((tm, D), lambda i, j, k: (i, 0)))], (tm, D),
                    a_pre=lambda a: a * a)
    (dh,) = _mm(f"{tag}_dh", da, w_in, _NT, (T // tm, 1, 4),
                pl.BlockSpec((tm, D), lambda i, j, k: (i, k)), pl.BlockSpec((None, D, D), lambda i, j, k: (k, 0, 0)),
                [(_sds((T, D), F32), pl.BlockSpec((tm, D), lambda i, j, k: (i, 0)))], (tm, D))
    (dw_in,) = _mm(f"{tag}_dwin", saved["h"], da, _TN, (D // tm, 4, T // tk),
                   pl.BlockSpec((tk, tm), lambda i, j, k: (k, i)), pl.BlockSpec((tk, D), lambda i, j, k: (k, j)),
                   [(_sds((4, D, D), BF16), pl.BlockSpec((None, tm, D), lambda i, j, k: (j, i, 0)))], (tm, D))
    dx_in, d_sh, d_sc, d_g = _norm_bwd(tag, dx, dh, 0, saved["x_in"], g_norm, sc)
    return dx_in, dw_in, dw_out, dict(sh=d_sh, sc=d_sc, gate=d_gate, g_norm=d_g)


def _local_step(x, ctx, tgt, mods, cmods, norm_g, final_g, rec, conf, wg, on_grads=None, wg_pre=None, on_later=None):
    on_grads = on_grads or (lambda group, dws: None)
    wg_pre = wg_pre or (lambda group, after: None)
    on_later = on_later or (lambda after: None)
    n_t = T // ROW_TILE
    row = lambda v: v.reshape(1, -1)
    m0 = [row(mods[0, q]) for q in range(6)]
    m1 = [row(mods[1, q]) for q in range(6)]
    g00, g01, g10, g11 = (row(norm_g[0, 0]), row(norm_g[0, 1]), row(norm_g[1, 0]), row(norm_g[1, 1]))
    csh, csc = row(cmods[0]), row(cmods[1])
    pos = _pos_embed()

    def prep0(ids, t, v):
        cx, xv, pv = t
        is_ctx = ids[0] == 0
        xin = jnp.where(is_ctx, cx, xv + pv)
        sh = jnp.where(is_ctx, v[3], v[1])
        sc = jnp.where(is_ctx, v[4], v[2])
        return [_norm_mod(xin, v[0], sc, sh), xv + pv], []

    dep = wg_pre("rec_in", csh)
    hcat, x0 = _tiled(
        "prep0", prep0, (N_SCAN,),
        [(ctx, pl.BlockSpec((ROW_TILE, D), lambda i: (0, 0))), _rows(x, off=-1, clamp_lo=True),
         _rows(pos, off=-1, clamp_lo=True)],
        [g00, m0[0], m0[1], csh, csc] + _behind(dep),
        [_orow(TA, D, BF16), _orow(T, D, F32, off=-1, clamp_lo=True)])

    tm_a = REC_TILE
    w_rin = wg("rec_in", hcat)["rec_w_in"]
    (a_in,) = _mm("rec_in", hcat, w_rin, _NN, (TA // tm_a, 4, 1),
                  pl.BlockSpec((tm_a, D), lambda i, j, k: (i, 0)),
                  pl.BlockSpec((None, D, RH), lambda i, j, k: (j, 0, 0)),
                  [(_sds((TA, 2 * R), F32), pl.BlockSpec((tm_a, RH), lambda i, j, k: (i, j)))], (tm_a, RH))
    rec_starts = (0, 1)
    u = _dwconv("rec_conv", a_in, R // CW_REC, rec["conv_w"], row(rec["conv_b"]), 1, rec_starts, R, CW_REC)
    wbd = _dense_gates(rec["w_a"], rec["w_x"])
    gbias = _gate_bias_dense(rec["b_a"], rec["b_x"])
    lam = rec["lam"]
    a_f, b_f, a_r, b_r = _tiled("rg_fwd", _rg_fwd_fn, (TA // RG_TILE,), [_rows(u, tm=RG_TILE)], [wbd, gbias, lam],
                                [_orow(TA, R, F32, tm=RG_TILE)] * 4, vec_refs=True)
    dep = wg_pre("rec_out", a_f)
    dep = wg_pre("mlp0", a_f if dep is None else dep)
    y_f, y_r, hin_f, hin_r = _scan_fwd(a_f, b_f, a_r, b_r)

    def rec_mid(ids, t, v):
        gp, yf, yr = t
        g, _ = _gelu(gp)
        return [g * (yf + yr)], []

    (m_rec,) = _tiled("rec_mid", rec_mid, (n_t,),
                      [_rows(a_in, R, off=1), _rows(y_f, off=1), _rows(y_r, off=1)], _behind(dep),
                      [_orow(T, R, BF16)])
    tm = MM_TILE
    row_spec = pl.BlockSpec((tm, D), lambda i, j, k: (i, 0))
    norm_mlp0 = (g01, m0[4], m0[3])
    w_rout = wg("rec_out", m_rec)["rec_w_out"]
    o_rec, x1, h_mlp0 = _mm(
        "rec_out", m_rec, w_rout, _NN, (T // tm, 1, 1),
        pl.BlockSpec((tm, R), lambda i, j, k: (i, 0)), pl.BlockSpec((R, D), lambda i, j, k: (0, 0)),
        [(_sds((T, D), F32), row_spec)] * 2 + [(_sds((T, D), BF16), row_spec)], (tm, D),
        extra=[(x0, row_spec), (m0[2], _full_spec(m0[2]))] + [(v, _full_spec(v)) for v in norm_mlp0],
        epi=_residual_epilogue(norm_mlp0))
    w_m0 = wg("mlp0", x1)
    dep = wg_pre("conf", x1)
    mlp0, x2, h1 = _mlp_fwd("mlp0", x1, h_mlp0, m0[5], w_m0["w_in"], w_m0["w_out"], (g10, m1[1], m1[0]), dep)

    b_pw1 = row(conf["b_pw1"])
    w_cf = wg("conf", x2)
    dep = wg_pre("mlp1", x2)
    (pre,) = _mm("conf_pw1", h1, w_cf["conf_w_pw1"], _NN, (T // tm, 4, 1),
                 pl.BlockSpec((tm, D), lambda i, j, k: (i, 0)),
                 pl.BlockSpec((None, D, D // 2), lambda i, j, k: (j, 0, 0)),
                 [(_sds((T, 2 * D), F32), pl.BlockSpec((tm, D // 2), lambda i, j, k: (i, j)))], (tm, D // 2),
                 extra=[(b_pw1, pl.BlockSpec((1, D // 2), lambda i, j, k: (0, j)))]
                 + [(d_, _full_spec(d_)) for d_ in _behind(dep)],
                 epi=lambda acc, ex: [acc + ex[0]])
    (zg,) = _tiled("conf_glu", lambda ids, t, v: ([t[0] * _sigmoid(t[1])], []), (n_t,),
                   [_rows(pre, D, col=0), _rows(pre, D, col=1)], [], [_orow(T, D, F32)])
    conf_starts = (0,)
    zc = _dwconv("conf_conv", zg, 0, conf["conv_w"], row(conf["conv_b"]), CONF_KW // 2, conf_starts, D, CW_CONF)
    ln_g, ln_b = row(conf["ln_g"]), row(conf["ln_b"])

    def ln_silu(ids, t, v):
        nh, _ = _layernorm_parts(t[0])
        ln = nh * v[0] + v[1]
        return [ln * _sigmoid(ln)], []

    (s_conf,) = _tiled("conf_ln", ln_silu, (n_t,), [_rows(zc)], [ln_g, ln_b], [_orow(T, D, BF16)])
    b_pw2 = row(conf["b_pw2"])
    norm_mlp1 = (g11, m1[4], m1[3])
    pw2_epi = _residual_epilogue(norm_mlp1)
    y_conf, x3, h_mlp1 = _mm(
        "conf_pw2", s_conf, w_cf["conf_w_pw2"], _NN, (T // tm, 1, 1),
        row_spec, pl.BlockSpec((D, D), lambda i, j, k: (0, 0)),
        [(_sds((T, D), F32), row_spec)] * 2 + [(_sds((T, D), BF16), row_spec)], (tm, D),
        extra=[(x2, row_spec), (m1[2], _full_spec(m1[2])), (b_pw2, _full_spec(b_pw2))]
        + [(v, _full_spec(v)) for v in norm_mlp1],
        epi=lambda acc, ex: pw2_epi(acc + ex[2], ex))
    w_m1 = wg("mlp1", x3)
    mlp1, x4, _ = _mlp_fwd("mlp1", x3, h_mlp1, m1[5], w_m1["w_in"], w_m1["w_out"])

    fg = row(final_g)

    def head(ids, t, v):
        n, r = _rms(t[0])
        err = n * v[0] - t[1]
        d_out = err * (1.0 / D)
        dn = d_out * v[0]
        dxv = r * (dn - n * jnp.mean(dn * n, axis=-1, keepdims=True))
        part = jnp.sum(_sum0(err * err), axis=1, keepdims=True) * (0.5 / D)
        return [dxv], [part, _sum0(d_out * n)]

    dx4, loss, d_fg = _tiled("head", head, (n_t,), [_rows(x4), _rows(tgt)], [fg], [_orow(T, D, F32)],
                             [(1, 1), (1, D)])

    dx3, dw_in1, dw_out1, dm_mlp1 = _mlp_bwd("mlp1", dx4, mlp1, g11, m1[4], m1[5],
                                             w_m1["w_in"], w_m1["w_out"])
    dep = on_grads("mlp1", (dw_in1, dw_out1))
    d_y, d_g1c, d_bpw2 = _gate_bwd("conf", dx3, y_conf, m1[2], dep)
    tk = MM_TILE
    (dw_pw2,) = _mm("conf_dwpw2", s_conf, d_y, _TN, (D // tm, 1, T // tk),
                    pl.BlockSpec((tk, tm), lambda i, j, k: (k, i)), pl.BlockSpec((tk, D), lambda i, j, k: (k, 0)),
                    [(_sds((D, D), BF16), pl.BlockSpec((tm, D), lambda i, j, k: (i, 0)))], (tm, D))
    (ds,) = _mm("conf_ds", d_y, w_cf["conf_w_pw2"], _NT, (T // tm, 1, 1),
                pl.BlockSpec((tm, D), lambda i, j, k: (i, 0)), pl.BlockSpec((D, D), lambda i, j, k: (0, 0)),
                [(_sds((T, D), F32), pl.BlockSpec((tm, D), lambda i, j, k: (i, 0)))], (tm, D))
    dep = on_later(ds)

    def ln_silu_bwd(ids, t, v):
        dsv, zcv = t
        nh, rstd = _layernorm_parts(zcv)
        ln = nh * v[0] + v[1]
        sg = _sigmoid(ln)
        d_ln = dsv * (sg * (1.0 + ln * (1.0 - sg)))
        d_nh = d_ln * v[0]
        d_zc = rstd * (d_nh - jnp.mean(d_nh, axis=-1, keepdims=True)
                       - nh * jnp.mean(d_nh * nh, axis=-1, keepdims=True))
        return [d_zc], [_sum0(d_ln * nh), _sum0(d_ln)]

    d_zc, d_lng, d_lnb = _tiled("conf_ln_bwd", ln_silu_bwd, (n_t,), [_rows(ds), _rows(zc)],
                                [ln_g, ln_b] + _behind(dep), [_orow(T, D, F32)], [(1, D), (1, D)])
    d_zg = _dwconv("conf_conv_dx", d_zc, 0, conf["conv_w"], jnp.zeros((1, D), F32),
                   CONF_KW - 1 - CONF_KW // 2, conf_starts, D, CW_CONF, flip=True)

    def glu_bwd(ids, t, v):
        dz, pa, pb = t
        sg = _sigmoid(pb)
        d_a = dz * sg
        d_b = dz * pa * sg * (1.0 - sg)
        return [jnp.concatenate([d_a, d_b], axis=1)], [_sum0(d_a), _sum0(d_b)]

    d_pre, d_b1a, d_b1b = _tiled(
        "conf_glu_bwd", glu_bwd, (n_t,), [_rows(d_zg), _rows(pre, D, col=0), _rows(pre, D, col=1)], [],
        [_orow(T, 2 * D, BF16)], [(1, D), (1, D)])
    (dw_pw1,) = _mm("conf_dwpw1", h1, d_pre, _TN, (D // tm, 4, T // tk),
                    pl.BlockSpec((tk, tm), lambda i, j, k: (k, i)),
                    pl.BlockSpec((tk, D // 2), lambda i, j, k: (k, j)),
                    [(_sds((4, D, D // 2), BF16), pl.BlockSpec((None, tm, D // 2), lambda i, j, k: (j, i, 0)))],
                    (tm, D // 2))
    dep = on_grads("conf", (dw_pw1, dw_pw2))
    (dh1,) = _mm("conf_dh", d_pre, w_cf["conf_w_pw1"], _NT, (T // tm, 1, 4),
                 pl.BlockSpec((tm, D // 2), lambda i, j, k: (i, k)),
                 pl.BlockSpec((None, D, D // 2), lambda i, j, k: (k, 0, 0)),
                 [(_sds((T, D), F32), pl.BlockSpec((tm, D), lambda i, j, k: (i, 0)))], (tm, D))
    dx2, d_sh1c, d_sc1c, d_g10 = _norm_bwd("conf", dx3, dh1, 0, x2, g10, m1[1], dep=dep)
    dep = on_later(dx2)

    dx1, dw_in0, dw_out0, dm_mlp0 = _mlp_bwd("mlp0", dx2, mlp0, g01, m0[4], m0[5],
                                             w_m0["w_in"], w_m0["w_out"], dep)
    dep = on_grads("mlp0", (dw_in0, dw_out0))
    d_orec, d_g1r, _ = _gate_bwd("rec", dx1, o_rec, m0[2], dep)
    (dw_rout,) = _mm("rec_dwout", m_rec, d_orec, _TN, (R // RH, 1, T // tk),
                     pl.BlockSpec((tk, RH), lambda i, j, k: (k, i)), pl.BlockSpec((tk, D), lambda i, j, k: (k, 0)),
                     [(_sds((R, D), BF16), pl.BlockSpec((RH, D), lambda i, j, k: (i, 0)))], (RH, D))
    (dm_rec,) = _mm("rec_dm", d_orec, w_rout, _NT, (T // tm, 1, 1),
                    pl.BlockSpec((tm, D), lambda i, j, k: (i, 0)), pl.BlockSpec((R, D), lambda i, j, k: (0, 0)),
                    [(_sds((T, R), F32), pl.BlockSpec((tm, R), lambda i, j, k: (i, 0)))], (tm, R))
    dep = on_later(dm_rec)

    def rec_mid_bwd(ids, t, v):
        dmv, gp, yf, yr = t
        g, th = _gelu(gp)
        lat = ids[0] > 0
        d_gp = jnp.where(lat, dmv * (yf + yr) * _gelu_grad(gp, th), 0.0)
        dy = jnp.where(lat, dmv * g, 0.0)
        return [d_gp, dy], []

    d_a, dy = _tiled("rec_mid_bwd", rec_mid_bwd, (N_SCAN,),
                     [_rows(dm_rec, off=-1, clamp_lo=True), _rows(a_in, R), _rows(y_f), _rows(y_r)], _behind(dep),
                     [(_sds((TA, 2 * R), BF16), pl.BlockSpec((ROW_TILE, R), lambda i: (i, 0))), _orow(TA, R, F32)])
    da_f, db_f, da_r, db_r = _scan_bwd(dy, a_f, y_f, hin_f, a_r, y_r, hin_r)
    d_gpre, d_u, d_gbias, d_lam = _tiled(
        "rg_bwd", _rg_bwd_fn, (TA // RG_TILE,), [_rows(a, tm=RG_TILE) for a in (u, da_f, db_f, da_r, db_r)],
        [wbd, gbias, lam], [_orow(TA, 2 * NQ, BF16, tm=RG_TILE), _orow(TA, R, F32, tm=RG_TILE)],
        [(1, 2 * NQ), (1, 2 * R)], vec_refs=True)
    tk_a = REC_TILE
    d_a = _dwconv("rec_conv_dx", d_u, 0, rec["conv_w"], jnp.zeros((1, R), F32), REC_KW - 1 - 1,
                  rec_starts, R, CW_REC, flip=True, into=(d_a, R // CW_REC))
    (dw_rin,) = _mm("rec_dwin", hcat, d_a, _TN, (D // tm, 4, TA // tk_a),
                    pl.BlockSpec((tk_a, tm), lambda i, j, k: (k, i)), pl.BlockSpec((tk_a, RH), lambda i, j, k: (k, j)),
                    [(_sds((4, D, RH), BF16), pl.BlockSpec((None, tm, RH), lambda i, j, k: (j, i, 0)))], (tm, RH))
    dep = on_grads("rec", (dw_rin, dw_rout))
    (dhcat,) = _mm("rec_dh", d_a, w_rin, _NT, (TA // tm_a, 1, 4),
                   pl.BlockSpec((tm_a, RH), lambda i, j, k: (i, k)),
                   pl.BlockSpec((None, D, RH), lambda i, j, k: (k, 0, 0)),
                   [(_sds((TA, D), F32), pl.BlockSpec((tm_a, D), lambda i, j, k: (i, 0)))], (tm_a, D))
    dx0, d_sh1r, d_sc1r, d_g00 = _norm_bwd("rec", dx1, dhcat, 1, x0, g00, m0[1], dep=dep)
    dep = on_later(dx0)

    d_csh, d_csc, d_g00c = _norm_bwd("ctx", None, dhcat, 0, ctx, g00, csc, with_dx=False, dep=dep)
    blk_mask, blk_spread = _block_mask(), _block_spread()
    (d_wbd,) = _mm("rg_dw", u, d_gpre, _TN, (2, 2, TA // tk_a),
                   pl.BlockSpec((tk_a, RH), lambda i, j, k: (k, i)),
                   pl.BlockSpec((tk_a, NQ // 2), lambda i, j, k: (k, 2 * i + j)),
                   [(_sds((2, 4, RH, BLK), F32), pl.BlockSpec((None, 2, RH, BLK), lambda i, j, k: (i, j, 0, 0)))],
                   (RH, NQ // 2),
                   extra=[(blk_mask, _full_spec(blk_mask)), (blk_spread, _full_spec(blk_spread))]
                   + [(d, _full_spec(d)) for d in _behind(dep)],
                   epi=lambda acc, ex: [jnp.stack([_fold_blocks(acc[:, s * RH:(s + 1) * RH], ex[0], ex[1])
                                                   for s in range(2)])])
    d_cw_rec = _dwconv_wgrad("rec_conv_dw", d_u, a_in, R // CW_REC, REC_KW, 1, rec_starts, R, CW_REC, dep)
    d_cw_conf = _dwconv_wgrad("conf_conv_dw", d_zc, zg, 0, CONF_KW, CONF_KW // 2, conf_starts, D, CW_CONF, dep)

    big = dict(rec_w_in=dw_rin, rec_w_out=dw_rout, conf_w_pw1=dw_pw1, conf_w_pw2=dw_pw2,
               mlp_w_in=(dw_in0, dw_in1), mlp_w_out=(dw_out0, dw_out1))
    d_wa, d_wx = _gate_block_grads(d_wbd)
    d_ba, d_bx = _gate_bias_grads(d_gbias)
    d_mod = jnp.concatenate([
        d_sh1r, d_sc1r, d_g1r, dm_mlp0["sh"], dm_mlp0["sc"], dm_mlp0["gate"],
        d_sh1c, d_sc1c, d_g1c, dm_mlp1["sh"], dm_mlp1["sc"], dm_mlp1["gate"]], axis=1).reshape(2, 6 * D)
    small = dict(
        d_mod=d_mod, d_cmod=jnp.concatenate([d_csh, d_csc], axis=1),
        norm_g=jnp.concatenate([d_g00 + d_g00c, dm_mlp0["g_norm"], d_g10, dm_mlp1["g_norm"]], axis=1),
        rec_conv_w=d_cw_rec[:REC_KW], rec_conv_b=d_cw_rec[REC_KW], rec_lambda=d_lam.reshape(2, R),
        rec_w_a=d_wa, rec_b_a=d_ba, rec_w_x=d_wx, rec_b_x=d_bx,
        conf_b_pw1=jnp.concatenate([d_b1a, d_b1b], axis=1), conf_conv_w=d_cw_conf[:CONF_KW],
        conf_conv_b=d_cw_conf[CONF_KW], conf_ln_g=d_lng, conf_ln_b=d_lnb, conf_b_pw2=d_bpw2, final_g=d_fg)
    return loss.reshape(()), dx0, big, small


_BIG = ("rec_w_in", "rec_w_out", "conf_w_pw1", "conf_w_pw2", "mlp_w_in", "mlp_w_out")


def _halves(w):
    return w.reshape(w.shape[0], 2, w.shape[1] // 2, w.shape[2])


def _ada_fwd(c16, w_ada, b_shard):
    ns = w_ada.shape[2]
    tn = 512

    def kern(c_ref, w_ref, b_ref, o_ref):
        cv = c_ref[...]
        s = (cv * _sigmoid(cv)).astype(BF16)
        o_ref[...] = jnp.dot(s, w_ref[...].astype(BF16), preferred_element_type=F32) + b_ref[...]

    return _pcall(
        kern, name="ada_fwd", grid=(2, ns // tn),
        in_specs=[pl.BlockSpec((16, D), lambda l, j: (0, 0)), pl.BlockSpec((None, D, tn), lambda l, j: (l, 0, j)),
                  pl.BlockSpec((None, 1, tn), lambda l, j: (l, 0, j))],
        out_specs=pl.BlockSpec((None, 16, tn), lambda l, j: (l, 0, j)),
        out_shape=_sds((2, 16, ns), F32), compiler_params=_cparams(),
    )(c16, w_ada, b_shard)


def _ada_bwd(c16, dm16, w_ada):
    ns = w_ada.shape[2]
    tn = 512

    def kern(c_ref, dm_ref, w_ref, gw_ref, ds_ref):
        cv = c_ref[...]
        s = (cv * _sigmoid(cv)).astype(BF16)
        dm = dm_ref[...].astype(BF16)
        gw_ref[...] = lax.dot_general(s, dm, _TN, preferred_element_type=F32)

        @pl.when(jnp.logical_and(pl.program_id(0) == 0, pl.program_id(1) == 0))
        def _():
            ds_ref[...] = jnp.zeros_like(ds_ref)

        ds_ref[...] += lax.dot_general(dm, w_ref[...].astype(BF16), _NT, preferred_element_type=F32)

    return _pcall(
        kern, name="ada_bwd", grid=(2, ns // tn),
        in_specs=[pl.BlockSpec((16, D), lambda l, j: (0, 0)), pl.BlockSpec((None, 16, tn), lambda l, j: (l, 0, j)),
                  pl.BlockSpec((None, D, tn), lambda l, j: (l, 0, j))],
        out_specs=[pl.BlockSpec((None, D, tn), lambda l, j: (l, 0, j)), pl.BlockSpec((16, D), lambda l, j: (0, 0))],
        out_shape=[_sds((2, D, ns), F32), _sds((16, D), F32)], compiler_params=_cparams(),
    )(c16, dm16, w_ada)


def _cctx_grad(ds4, c_ctx):
    def kern(d_ref, c_ref, o_ref):
        tot = d_ref[0, 0:1, :] + d_ref[1, 0:1, :] + d_ref[2, 0:1, :] + d_ref[3, 0:1, :]
        cv = c_ref[...]
        sg = _sigmoid(cv)
        o_ref[...] = tot * (sg * (1.0 + cv * (1.0 - sg)))

    return _pcall(kern, name="cctx_grad", out_shape=_sds((1, D), F32))(ds4, c_ctx.reshape(1, D))


def kernel(x, c, ctx, c_ctx, w_ada, b_ada, norm_g, rec_w_in, rec_conv_w, rec_conv_b, rec_lambda, rec_w_a, rec_b_a, rec_w_x, rec_b_x, rec_w_out, conf_w_pw1, conf_b_pw1, conf_conv_w, conf_conv_b, conf_ln_g, conf_ln_b, conf_w_pw2, conf_b_pw2, mlp_w_in, mlp_w_out, final_g, loss_target, m_c_ctx, m_w_ada, m_b_ada, m_norm_g, m_rec_w_in, m_rec_conv_w, m_rec_conv_b, m_rec_lambda, m_rec_w_a, m_rec_b_a, m_rec_w_x, m_rec_b_x, m_rec_w_out, m_conf_w_pw1, m_conf_b_pw1, m_conf_conv_w, m_conf_conv_b, m_conf_ln_g, m_conf_ln_b, m_conf_w_pw2, m_conf_b_pw2, m_mlp_w_in, m_mlp_w_out, m_final_g, v_c_ctx, v_w_ada, v_b_ada, v_norm_g, v_rec_w_in, v_rec_conv_w, v_rec_conv_b, v_rec_lambda, v_rec_w_a, v_rec_b_a, v_rec_w_x, v_rec_b_x, v_rec_w_out, v_conf_w_pw1, v_conf_b_pw1, v_conf_conv_w, v_conf_conv_b, v_conf_ln_g, v_conf_ln_b, v_conf_w_pw2, v_conf_b_pw2, v_mlp_w_in, v_mlp_w_out, v_final_g):
    names = ["c_ctx", "w_ada", "b_ada", "norm_g", "rec_w_in", "rec_conv_w", "rec_conv_b", "rec_lambda", "rec_w_a",
             "rec_b_a", "rec_w_x", "rec_b_x", "rec_w_out", "conf_w_pw1", "conf_b_pw1", "conf_conv_w", "conf_conv_b",
             "conf_ln_g", "conf_ln_b", "conf_w_pw2", "conf_b_pw2", "mlp_w_in", "mlp_w_out", "final_g"]
    w = dict(zip(names, [c_ctx, w_ada, b_ada, norm_g, rec_w_in, rec_conv_w, rec_conv_b, rec_lambda, rec_w_a,
                         rec_b_a, rec_w_x, rec_b_x, rec_w_out, conf_w_pw1, conf_b_pw1, conf_conv_w, conf_conv_b,
                         conf_ln_g, conf_ln_b, conf_w_pw2, conf_b_pw2, mlp_w_in, mlp_w_out, final_g]))
    m = dict(zip(names, [m_c_ctx, m_w_ada, m_b_ada, m_norm_g, m_rec_w_in, m_rec_conv_w, m_rec_conv_b, m_rec_lambda,
                         m_rec_w_a, m_rec_b_a, m_rec_w_x, m_rec_b_x, m_rec_w_out, m_conf_w_pw1, m_conf_b_pw1,
                         m_conf_conv_w, m_conf_conv_b, m_conf_ln_g, m_conf_ln_b, m_conf_w_pw2, m_conf_b_pw2,
                         m_mlp_w_in, m_mlp_w_out, m_final_g]))
    v = dict(zip(names, [v_c_ctx, v_w_ada, v_b_ada, v_norm_g, v_rec_w_in, v_rec_conv_w, v_rec_conv_b, v_rec_lambda,
                         v_rec_w_a, v_rec_b_a, v_rec_w_x, v_rec_b_x, v_rec_w_out, v_conf_w_pw1, v_conf_b_pw1,
                         v_conf_conv_w, v_conf_conv_b, v_conf_ln_g, v_conf_ln_b, v_conf_w_pw2, v_conf_b_pw2,
                         v_mlp_w_in, v_mlp_w_out, v_final_g]))
    mx, my, mc = _me()
    chip = 2 * mx + my
    me = 4 * mx + 2 * my + mc

    sharded_small = ["norm_g", "rec_conv_w", "rec_lambda", "conf_b_pw1", "conf_conv_w", "conf_conv_b", "conf_ln_g",
                     "conf_ln_b", "conf_b_pw2"]
    packed, offs = _pack([c] + [w[k] for k in sharded_small], 8)
    place = jnp.stack([chip, mc]).astype(jnp.int32)
    shards = [("rec_in", _halves(rec_w_in), 0), ("rec_out", _halves(rec_w_out), 0),
              ("pw1", _halves(conf_w_pw1), 0), ("pw2", _halves(conf_w_pw2), 0),
              ("mlp_in0", _halves(mlp_w_in), 0), ("mlp_in1", _halves(mlp_w_in), 1),
              ("mlp_out0", _halves(mlp_w_out), 0), ("mlp_out1", _halves(mlp_w_out), 1)]
    small_state, small_sent = _send_start("gather_small_start", [packed[None]], _to_all7, [(7,) + packed.shape], place)
    (slot_rin,) = _place_big(shards[:1], place, small_sent)
    flying, gsems, swapping = {}, {}, {}
    flying["rec_in"], gsems["rec_in"], rec_started = _gather_start("gather_start_rec", [slot_rin], ((0,),), small_sent)
    slots = [slot_rin] + _place_big(shards[1:], place, rec_started)
    placed = jnp.broadcast_to(lax.dynamic_slice(slots[-1], (chip, 0, 0, 0), (1, 1, 1, 1)).reshape(1, 1), (8, 1))
    (own,), (landed,) = _send_wait("gather_small_wait", *small_state, _to_all7, placed)
    by_flip = jnp.concatenate([own, landed], axis=0)
    got_flat = jnp.take(by_flip, jnp.arange(8) ^ me, axis=0).reshape(8, -1)

    def piece(i):
        p, n, shape = offs[i]
        return got_flat[:, p:p + n].reshape((8,) + tuple(shape))

    c_rows = piece(0).reshape(8, D)
    full = {}
    for i, k in enumerate(sharded_small):
        per_chip = jnp.moveaxis(piece(1 + i)[0::2], 0, -2)
        full[k] = per_chip.reshape(per_chip.shape[:-2] + (4 * per_chip.shape[-1],))
    c16 = jnp.concatenate([c_rows, c_ctx.reshape(1, D), jnp.zeros((7, D), F32)], axis=0)

    ns = w_ada.shape[2]
    b_shard = lax.dynamic_slice_in_dim(b_ada, chip * ns, ns, axis=1).reshape(2, 1, ns)
    prod = _ada_fwd(c16, w_ada, b_shard)

    own_rows = lax.dynamic_index_in_dim(prod[:, :8].reshape(2, 4, 2, ns), mc, axis=2, keepdims=False)
    rows = jnp.concatenate([own_rows.transpose(1, 0, 2), jnp.broadcast_to(prod[0, 8], (4, 1, ns)),
                            jnp.zeros((4, 5, ns), F32)], axis=1)
    mod_state, mod_started = _send_start("mod_start", [rows], _to_chips, [(3, 8, ns)], place)
    use_order = dict(rec=(0, 1), mlp0=(4, 6), conf=(2, 3), mlp1=(5, 7))
    fetch_order = dict(rec_out=(1,), mlp0=(4, 6), conf=(2, 3), mlp1=(5, 7))
    order = [t for g in fetch_order for t in fetch_order[g]]
    groups = [tuple(order.index(t) for t in fetch_order[g]) for g in fetch_order]
    fly, sems, all_started = _gather_start("gather_start_rest", [slots[t] for t in order], tuple(groups), mod_started)
    for gi, g in enumerate(fetch_order):
        flying[g], gsems[g] = [fly[k] for k in groups[gi]], sems[2 * gi:2 * gi + 2]

    def wg_pre(group, after):
        bufs = _gather_wait(f"gather_wait_{group}", flying[group], *gsems[group], after)
        swapping[group], token = _swap_start(f"swap_start_{group}", bufs, after)
        return token

    def wg(group, after):
        bufs = _swap_wait(f"swap_wait_{group}", *swapping[group], after)
        if group == "rec_in":
            return dict(rec_w_in=bufs[0].reshape(4, D, RH))
        if group == "rec_out":
            return dict(rec_w_out=bufs[0].reshape(R, D))
        if group == "conf":
            return dict(conf_w_pw1=bufs[0].reshape(4, D, D // 2), conf_w_pw2=bufs[1].reshape(D, D))
        return dict(w_in=bufs[0].reshape(4, D, D), w_out=bufs[1].reshape(FF, D))

    (rows,), (landed,) = _send_wait("mod_wait", *mod_state, _to_chips, all_started)
    own = lax.dynamic_index_in_dim(rows, chip, axis=0, keepdims=True)
    by_flip = jnp.concatenate([own, landed[1:2], landed[0:1], landed[2:3]], axis=0)
    by_chip = jnp.take(by_flip, jnp.arange(4) ^ chip, axis=0)
    mods = by_chip[:, :2].transpose(1, 0, 2).reshape(2, 6, D)
    cmods = by_chip[:, 2].reshape(6, D)[:2]

    rec = dict(conv_w=full["rec_conv_w"][0], conv_b=rec_conv_b[0], lam=full["rec_lambda"][0],
               w_a=rec_w_a[0], b_a=rec_b_a[0], w_x=rec_w_x[0], b_x=rec_b_x[0])
    conf = dict(b_pw1=full["conf_b_pw1"][0], conv_w=full["conf_conv_w"][0], conv_b=full["conf_conv_b"][0],
                ln_g=full["conf_ln_g"][0], ln_b=full["conf_ln_b"][0], b_pw2=full["conf_b_pw2"][0])
    pairing, sent, sharing = {}, {}, {}

    def on_grads(group, dws):
        parts = [dw.reshape((4,) + shards[t][1].shape[1:]) for dw, t in zip(dws, use_order[group])]
        pairing[group], token = _reduce_begin(group, parts, place)
        return token

    def finish_pair(after):
        (group, state), = pairing.items()
        pairing.clear()
        sent[group], token = _reduce_mid(group, state, place, after)
        if group == "rec":
            mlp = _reduce_end("mlp1", sent["mlp1"], place, token, layer=(1, 2))
            cf = _reduce_end("conf", sent["conf"], place, token)
            mlp = _reduce_end("mlp0", sent["mlp0"], place, token, layer=(0, 2), into=mlp)
            sharing["state"], token = _share_start("share_start", cf + mlp, place)
        sent["token"] = token
        return token

    loss_local, grad_x, _, small = _local_step(x[0], ctx[0], loss_target[0], mods, cmods, full["norm_g"], final_g,
                                               rec, conf, wg, on_grads, wg_pre, finish_pair)
    rec_sent = sent["token"]
    small["loss"] = loss_local.reshape(1)

    small_names = ["loss", "d_mod", "d_cmod", "norm_g", "rec_conv_w", "rec_conv_b", "rec_lambda", "rec_w_a", "rec_b_a",
                   "rec_w_x", "rec_b_x", "conf_b_pw1", "conf_conv_w", "conf_conv_b", "conf_ln_g", "conf_ln_b",
                   "conf_b_pw2", "final_g"]
    mine = lax.broadcasted_iota(jnp.int32, (8, 1), 0) == me
    mod_slots = jnp.where(mine, small["d_mod"].reshape(1, -1), 0.0)
    spacked, soffs = _pack([small[k] for k in small_names] + [mod_slots])
    def sum_rec(after):
        sharing["rec"], token = _share_start("share_rec_start", _reduce_end("rec", sent["rec"], place, after), place)
        return token

    small_state, small_started = _allreduce_small_begin(spacked, place, rec_sent, sum_rec)

    shared = _share_wait("share_wait", *sharing["state"], small_started)
    delta, new_m, new_v, done = {}, {}, {}, {}

    def adamw_of(k, g, dep=None):
        cols = w[k].shape[-1]
        d_, m_, v_ = _adamw(f"adamw_{k}", w[k].reshape(-1, cols), g.reshape(-1, cols),
                            m[k].reshape(-1, cols), v[k].reshape(-1, cols), dep)
        done[k] = v_
        delta[k], new_m[k], new_v[k] = (a.reshape(w[k].shape) for a in (d_, m_, v_))

    g_big = {}
    for k, g in zip(_BIG[2:], shared):
        g_big[k] = g.reshape(w[k].shape)
        adamw_of(k, g_big[k])
    for k, g in zip(_BIG[:2], _share_wait("share_rec_wait", *sharing["rec"], done["mlp_w_out"])):
        g_big[k] = g.reshape(w[k].shape)
        adamw_of(k, g_big[k])
    unpacked = _unpack(_allreduce_small_end(small_state, [done[k] for k in _BIG]), soffs)
    ssum = dict(zip(small_names, unpacked[:-1]))
    loss = ssum["loss"].reshape(())
    dmod_rows = unpacked[-1].reshape(8, 2, 6 * D).transpose(1, 0, 2)

    d_cmod_full =jnp.concatenate([ssum["d_cmod"].reshape(1, 2 * D), jnp.zeros((1, 4 * D), F32)], axis=1)
    dm16 = jnp.concatenate([dmod_rows, jnp.stack([d_cmod_full, jnp.zeros((1, 6 * D), F32)]),
                            jnp.zeros((2, 7, 6 * D), F32)], axis=1)
    dm16_shard = lax.dynamic_slice_in_dim(dm16, chip * ns, ns, axis=2)
    g_w_ada, ds_part = _ada_bwd(c16, dm16_shard, w_ada)
    ds_state, ds_sent = _send_start("dsilu_start", [jnp.broadcast_to(ds_part[8:16], (4, 8, D))], _to_chips,
                                    [(3, 8, D)], place)
    adamw_of("w_ada", g_w_ada, ds_sent)
    (ds_own,), (ds_landed,) = _send_wait("dsilu_wait", *ds_state, _to_chips, done["w_ada"])
    ds_flip = jnp.concatenate([ds_own[:1], ds_landed[1:2], ds_landed[0:1], ds_landed[2:3]], axis=0)
    g_c_ctx = _cctx_grad(jnp.take(ds_flip, jnp.arange(4) ^ chip, axis=0), c_ctx).reshape(D)
    g_b_ada = ssum["d_mod"] + jnp.stack([d_cmod_full[0], jnp.zeros((6 * D,), F32)])

    def shard_of(a, axis):
        n = a.shape[axis] // 4
        return lax.dynamic_slice_in_dim(a, chip * n, n, axis=axis)

    grads = dict(
        c_ctx=g_c_ctx, w_ada=g_w_ada, b_ada=g_b_ada,
        norm_g=shard_of(ssum["norm_g"].reshape(2, 2, D), 2),
        rec_w_in=g_big["rec_w_in"], rec_conv_w=shard_of(ssum["rec_conv_w"].reshape(1, REC_KW, R), 2),
        rec_conv_b=ssum["rec_conv_b"].reshape(1, R), rec_lambda=shard_of(ssum["rec_lambda"].reshape(1, 2, R), 2),
        rec_w_a=ssum["rec_w_a"].reshape(rec_w_a.shape), rec_b_a=ssum["rec_b_a"].reshape(rec_b_a.shape),
        rec_w_x=ssum["rec_w_x"].reshape(rec_w_x.shape), rec_b_x=ssum["rec_b_x"].reshape(rec_b_x.shape),
        rec_w_out=g_big["rec_w_out"], conf_w_pw1=g_big["conf_w_pw1"],
        conf_b_pw1=shard_of(ssum["conf_b_pw1"].reshape(1, 2 * D), 1),
        conf_conv_w=shard_of(ssum["conf_conv_w"].reshape(1, CONF_KW, D), 2),
        conf_conv_b=shard_of(ssum["conf_conv_b"].reshape(1, D), 1),
        conf_ln_g=shard_of(ssum["conf_ln_g"].reshape(1, D), 1), conf_ln_b=shard_of(ssum["conf_ln_b"].reshape(1, D), 1),
        conf_w_pw2=g_big["conf_w_pw2"], conf_b_pw2=shard_of(ssum["conf_b_pw2"].reshape(1, D), 1),
        mlp_w_in=g_big["mlp_w_in"], mlp_w_out=g_big["mlp_w_out"], final_g=ssum["final_g"].reshape(D))

    rest =[k for k in names if k not in ("w_ada",) + _BIG]
    d_, m_, v_ = _adamw_many("adamw_small", [w[k] for k in rest], [grads[k] for k in rest],
                             [m[k] for k in rest], [v[k] for k in rest])
    for k, dd, mm, vv in zip(rest, d_, m_, v_):
        delta[k], new_m[k], new_v[k] = dd, mm, vv

    return (loss, grad_x[None], *[grads[k] for k in names], *[delta[k] for k in names],
            *[new_m[k] for k in names], *[new_v[k] for k in names])
```
